```python
import jax, jax.numpy as jnp
from jax import lax
import numpy as np

D_MODEL = 1024
BATCH = 8
SEQ = 4096
DEPTH = 1

D_MIX = D_MODEL
D_POOL = D_MIX // 2
D_DN = D_MIX - D_POOL
POOL_WINDOWS = (2, 4, 8, 16)
N_POOL_GROUPS = len(POOL_WINDOWS)
POOL_GROUP = D_POOL // N_POOL_GROUPS
DN_HEAD_DIM = 128
DN_HEADS = D_DN // DN_HEAD_DIM
CONV_WIDTH = 4
CHUNK = 64
NORM_EPS = 1e-6
SPLIT_SIZES = (D_POOL, D_POOL, D_DN, D_DN, D_DN, D_DN, DN_HEADS, DN_HEADS)
D_IN = sum(SPLIT_SIZES)

kernel_name = "hymba_pool_gated_deltanet_block"


def rms_norm(x, w):
    xf = x.astype(jnp.float32)
    y = xf * lax.rsqrt(jnp.mean(xf * xf, axis=-1, keepdims=True) + NORM_EPS)
    return (y * w.astype(jnp.float32)).astype(x.dtype)


def l2_normalize(t):
    return t * lax.rsqrt(jnp.sum(t * t, axis=-1, keepdims=True) + NORM_EPS)


def pool_mixer(u, z, pool_w, pool_scale):
    B, S, _ = u.shape
    uf = u.astype(jnp.float32).reshape(B, S, N_POOL_GROUPS, POOL_GROUP)
    csum = jnp.cumsum(uf, axis=1)
    counts = jnp.arange(1, S + 1, dtype=jnp.float32)
    outs = []
    for gi, w in enumerate(POOL_WINDOWS):
        c = csum[:, :, gi]
        prev = jnp.pad(c, ((0, 0), (w, 0), (0, 0)))[:, :S]
        cnt = jnp.minimum(counts, float(w))[None, :, None]
        outs.append((c - prev) / cnt)
    mix = jnp.stack(outs, axis=2) - uf
    mix = jnp.einsum('bsgc,gcd->bsgd', mix, pool_w.astype(jnp.float32)).reshape(B, S, D_POOL)
    out = mix * pool_scale.astype(jnp.float32) * jax.nn.silu(z.astype(jnp.float32))
    return out.astype(u.dtype)


def causal_depthwise_conv(u, w):
    K, C = w.shape
    return lax.conv_general_dilated(
        u, w[:, None, :], window_strides=(1,), padding=[(K - 1, 0)],
        dimension_numbers=('NWC', 'WIO', 'NWC'), feature_group_count=C)


def gated_delta_rule(q, k, v, g, beta):
    B, H, S, dk = q.shape
    dv = v.shape[-1]
    n = S // CHUNK
    q = q * (dk ** -0.5)
    k_beta = k * beta[..., None]
    v_beta = v * beta[..., None]
    chunked = lambda t: t.reshape(B, H, n, CHUNK, t.shape[-1])
    q, k, k_beta, v_beta = chunked(q), chunked(k), chunked(k_beta), chunked(v_beta)
    gc = jnp.cumsum(g.reshape(B, H, n, CHUNK), axis=-1)
    causal = jnp.tril(jnp.ones((CHUNK, CHUNK), dtype=bool))
    strict = jnp.tril(jnp.ones((CHUNK, CHUNK), dtype=bool), k=-1)
    diff = gc[..., :, None] - gc[..., None, :]
    decay = jnp.exp(jnp.where(causal, diff, -jnp.inf))
    A = jnp.where(strict, jnp.einsum('bhncd,bhnmd->bhncm', k_beta, k) * decay, 0.0)
    eye = jnp.eye(CHUNK, dtype=jnp.float32)
    T = lax.linalg.triangular_solve(A + eye, jnp.broadcast_to(eye, A.shape),
                                    left_side=True, lower=True, unit_diagonal=True)
    u = jnp.einsum('bhncm,bhnmd->bhncd', T, v_beta)
    w = jnp.einsum('bhncm,bhnmd->bhncd', T, k_beta * jnp.exp(gc)[..., None])
    qk = jnp.einsum('bhncd,bhnmd->bhncm', q, k) * decay
    q_dec = q * jnp.exp(gc)[..., None]
    k_dec = k * jnp.exp(gc[..., -1:] - gc)[..., None]
    chunk_decay = jnp.exp(gc[..., -1])

    def step(state, xs):
        qk_i, q_dec_i, k_dec_i, u_i, w_i, dec_i = xs
        v_new = u_i - jnp.einsum('bhcd,bhde->bhce', w_i, state)
        o = jnp.einsum('bhcd,bhde->bhce', q_dec_i, state) + jnp.einsum('bhcm,bhme->bhce', qk_i, v_new)
        state = state * dec_i[..., None, None] + jnp.einsum('bhcd,bhce->bhde', k_dec_i, v_new)
        return state, o

    to_scan = lambda t: jnp.moveaxis(t, 2, 0)
    xs = (to_scan(qk), to_scan(q_dec), to_scan(k_dec), to_scan(u), to_scan(w), to_scan(chunk_decay))
    state0 = jnp.zeros((B, H, dk, dv), dtype=jnp.float32)
    _, o = lax.scan(step, state0, xs)
    return jnp.moveaxis(o, 0, 2).reshape(B, H, S, dv)


def deltanet_mixer(q, k, v, z, b, a, conv_w, a_log, dt_bias, norm_w):
    B, S, _ = q.shape
    out_dtype = q.dtype
    qkv = jnp.concatenate([q, k, v], axis=-1).astype(jnp.float32)
    qkv = jax.nn.silu(causal_depthwise_conv(qkv, conv_w.astype(jnp.float32)))
    q, k, v = jnp.split(qkv, 3, axis=-1)
    heads = lambda t: t.reshape(B, S, DN_HEADS, DN_HEAD_DIM).transpose(0, 2, 1, 3)
    q, k, v = l2_normalize(heads(q)), l2_normalize(heads(k)), heads(v)
    beta = jax.nn.sigmoid(b.astype(jnp.float32)).transpose(0, 2, 1)
    g = (-jnp.exp(a_log.astype(jnp.float32))
         * jax.nn.softplus(a.astype(jnp.float32) + dt_bias.astype(jnp.float32))).transpose(0, 2, 1)
    o = gated_delta_rule(q, k, v, g, beta).transpose(0, 2, 1, 3)
    o = o * lax.rsqrt(jnp.mean(o * o, axis=-1, keepdims=True) + NORM_EPS) * norm_w.astype(jnp.float32)
    o = o * jax.nn.silu(z.astype(jnp.float32).reshape(B, S, DN_HEADS, DN_HEAD_DIM))
    return o.reshape(B, S, D_DN).astype(out_dtype)


def _fwd_setup_inputs(seed: int = 0) -> dict:
    key = jax.random.key(seed)
    ks = jax.random.split(key, 12)
    f32 = jnp.float32
    x = jax.random.normal(ks[0], (BATCH, SEQ, D_MODEL), f32)
    norm_w = 1.0 + 0.02 * jax.random.normal(ks[1], (DEPTH, D_MODEL), f32)
    w_in = jax.random.normal(ks[2], (DEPTH, D_MODEL, D_IN), f32) * D_MODEL ** -0.5
    pool_w = jax.random.normal(ks[3], (DEPTH, N_POOL_GROUPS, POOL_GROUP, POOL_GROUP), f32) * POOL_GROUP ** -0.5
    pool_scale = 1.0 + 0.1 * jax.random.normal(ks[4], (DEPTH, D_POOL), f32)
    conv_w = jax.random.normal(ks[5], (DEPTH, CONV_WIDTH, 3 * D_DN), f32) * CONV_WIDTH ** -0.5
    a_log = jnp.log(jax.random.uniform(ks[6], (DEPTH, DN_HEADS), f32, 1.0, 16.0))
    dt = jnp.exp(jax.random.uniform(ks[7], (DEPTH, DN_HEADS), f32, np.log(1e-3), np.log(1e-1)))
    dt_bias = dt + jnp.log(-jnp.expm1(-dt))
    dn_norm_w = 1.0 + 0.02 * jax.random.normal(ks[8], (DEPTH, DN_HEAD_DIM), f32)
    w_out = jax.random.normal(ks[9], (DEPTH, D_MIX, D_MODEL), f32) * D_MIX ** -0.5
    final_norm_w = 1.0 + 0.02 * jax.random.normal(ks[10], (D_MODEL,), f32)
    return {"x": x, "norm_w": norm_w, "w_in": w_in, "pool_w": pool_w, "pool_scale": pool_scale,
            "conv_w": conv_w, "a_log": a_log, "dt_bias": dt_bias, "dn_norm_w": dn_norm_w,
            "w_out": w_out, "final_norm_w": final_norm_w}


def _fwd_reference(x, norm_w, w_in, pool_w, pool_scale, conv_w, a_log, dt_bias, dn_norm_w, w_out, final_norm_w):
    offsets = np.cumsum((0,) + SPLIT_SIZES)
    h = x
    for layer in range(DEPTH):
        n = rms_norm(h, norm_w[layer])
        proj = jnp.einsum('bsd,de->bse', n, w_in[layer])
        pu, pz, q, k, v, dz, b, a = [proj[..., int(offsets[i]):int(offsets[i + 1])]
                                     for i in range(len(SPLIT_SIZES))]
        y_pool = pool_mixer(pu, pz, pool_w[layer], pool_scale[layer])
        y_dn = deltanet_mixer(q, k, v, dz, b, a, conv_w[layer], a_log[layer],
                              dt_bias[layer], dn_norm_w[layer])
        y = jnp.concatenate([y_pool, y_dn], axis=-1)
        h = h + jnp.einsum('bse,ed->bsd', y, w_out[layer])
    return rms_norm(h, final_norm_w)


import jax as _jax
import jax.numpy as _jnp

TWIN_FORMAT = 'train_step'
FWD_PARAMS = ['x', 'norm_w', 'w_in', 'pool_w', 'pool_scale', 'conv_w', 'a_log', 'dt_bias', 'dn_norm_w', 'w_out', 'final_norm_w']
TWIN_WEIGHTS = ['norm_w', 'w_in', 'pool_w', 'pool_scale', 'conv_w', 'a_log', 'dt_bias', 'dn_norm_w', 'w_out', 'final_norm_w']
TWIN_DIFF_INPUT = 'x'
TWIN_INPUTS = ['x', 'norm_w', 'w_in', 'pool_w', 'pool_scale', 'conv_w', 'a_log', 'dt_bias', 'dn_norm_w', 'w_out', 'final_norm_w', 'loss_target', 'm_norm_w', 'm_w_in', 'm_pool_w', 'm_pool_scale', 'm_conv_w', 'm_a_log', 'm_dt_bias', 'm_dn_norm_w', 'm_w_out', 'm_final_norm_w', 'v_norm_w', 'v_w_in', 'v_pool_w', 'v_pool_scale', 'v_conv_w', 'v_a_log', 'v_dt_bias', 'v_dn_norm_w', 'v_w_out', 'v_final_norm_w']
TWIN_OUTPUTS = ['loss', 'grad_x', 'grad_norm_w', 'grad_w_in', 'grad_pool_w', 'grad_pool_scale', 'grad_conv_w', 'grad_a_log', 'grad_dt_bias', 'grad_dn_norm_w', 'grad_w_out', 'grad_final_norm_w', 'delta_norm_w', 'delta_w_in', 'delta_pool_w', 'delta_pool_scale', 'delta_conv_w', 'delta_a_log', 'delta_dt_bias', 'delta_dn_norm_w', 'delta_w_out', 'delta_final_norm_w', 'new_m_norm_w', 'new_m_w_in', 'new_m_pool_w', 'new_m_pool_scale', 'new_m_conv_w', 'new_m_a_log', 'new_m_dt_bias', 'new_m_dn_norm_w', 'new_m_w_out', 'new_m_final_norm_w', 'new_v_norm_w', 'new_v_w_in', 'new_v_pool_w', 'new_v_pool_scale', 'new_v_conv_w', 'new_v_a_log', 'new_v_dt_bias', 'new_v_dn_norm_w', 'new_v_w_out', 'new_v_final_norm_w']
TWIN_LEAF_KINDS = {'loss': 'loss', 'grad_x': 'grad_x', 'grad_norm_w': 'grad_w', 'grad_w_in': 'grad_w', 'grad_pool_w': 'grad_w', 'grad_pool_scale': 'grad_w', 'grad_conv_w': 'grad_w', 'grad_a_log': 'grad_w', 'grad_dt_bias': 'grad_w', 'grad_dn_norm_w': 'grad_w', 'grad_w_out': 'grad_w', 'grad_final_norm_w': 'grad_w', 'delta_norm_w': 'delta_w', 'delta_w_in': 'delta_w', 'delta_pool_w': 'delta_w', 'delta_pool_scale': 'delta_w', 'delta_conv_w': 'delta_w', 'delta_a_log': 'delta_w', 'delta_dt_bias': 'delta_w', 'delta_dn_norm_w': 'delta_w', 'delta_w_out': 'delta_w', 'delta_final_norm_w': 'delta_w', 'new_m_norm_w': 'new_m', 'new_m_w_in': 'new_m', 'new_m_pool_w': 'new_m', 'new_m_pool_scale': 'new_m', 'new_m_conv_w': 'new_m', 'new_m_a_log': 'new_m', 'new_m_dt_bias': 'new_m', 'new_m_dn_norm_w': 'new_m', 'new_m_w_out': 'new_m', 'new_m_final_norm_w': 'new_m', 'new_v_norm_w': 'new_v', 'new_v_w_in': 'new_v', 'new_v_pool_w': 'new_v', 'new_v_pool_scale': 'new_v', 'new_v_conv_w': 'new_v', 'new_v_a_log': 'new_v', 'new_v_dt_bias': 'new_v', 'new_v_dn_norm_w': 'new_v', 'new_v_w_out': 'new_v', 'new_v_final_norm_w': 'new_v'}


def _forward(args):
    return _fwd_reference(*[args[k] for k in FWD_PARAMS])


def _output_shape():
    out = _jax.eval_shape(lambda: _forward(_fwd_setup_inputs(0)))
    return out.shape, out.dtype

N_MICROBATCH = 1
ADAM_LR = 0.001
ADAM_B1 = 0.9
ADAM_B2 = 0.999
ADAM_EPS = 1e-08
ADAM_WD = 0.01
ADAM_STEP = 10
PER_EXAMPLE_BATCH_AXIS = {'x': 0, 'loss_target': 0}
SHARED_INPUTS = []
_WEIGHT_DTYPES = {'norm_w': _jnp.float32, 'w_in': _jnp.float32, 'pool_w': _jnp.float32, 'pool_scale': _jnp.float32, 'conv_w': _jnp.float32, 'a_log': _jnp.float32, 'dt_bias': _jnp.float32, 'dn_norm_w': _jnp.float32, 'w_out': _jnp.float32, 'final_norm_w': _jnp.float32}
MOMENT_SCALE = {'norm_w': 1.407220e-01, 'w_in': 7.817475e-02, 'pool_w': 8.081874e-02, 'pool_scale': 8.257985e-02, 'conv_w': 7.131172e-02, 'a_log': 3.863155e-01, 'dt_bias': 3.757178e-01, 'dn_norm_w': 1.898766e-01, 'w_out': 8.636790e-02, 'final_norm_w': 3.200005e+01}


def _to_microbatches(a, axis):
    t = _jnp.moveaxis(a, axis, 0)
    t = t.reshape((N_MICROBATCH, t.shape[0] // N_MICROBATCH) + t.shape[1:])
    return _jnp.moveaxis(t, 1, axis + 1)


def setup_inputs(seed: int = 0) -> dict:
    inp = _fwd_setup_inputs(seed)
    key = _jax.random.fold_in(_jax.random.key(seed), 7919)
    shape, _ = _output_shape()
    out = dict(inp)
    out["loss_target"] = _jax.random.normal(_jax.random.fold_in(key, 0), shape, _jnp.float32)
    for i, name in enumerate(TWIN_WEIGHTS):
        w = inp[name].astype(_jnp.float32)
        if MOMENT_SCALE is None:
            s = _jnp.sqrt(_jnp.mean(_jnp.square(w)) + 1e-30)
        else:
            s = MOMENT_SCALE[name]
        km, kv = _jax.random.split(_jax.random.fold_in(key, i + 1))
        out[name] = w
        out["m_" + name] = s * _jax.random.normal(km, w.shape, _jnp.float32)
        out["v_" + name] = (s * s) * _jax.random.uniform(kv, w.shape, _jnp.float32, 0.5, 1.5)
    if N_MICROBATCH > 1:
        for name, axis in PER_EXAMPLE_BATCH_AXIS.items():
            out[name] = _to_microbatches(out[name], axis)
    return {'x': out['x'], 'norm_w': out['norm_w'], 'w_in': out['w_in'], 'pool_w': out['pool_w'], 'pool_scale': out['pool_scale'], 'conv_w': out['conv_w'], 'a_log': out['a_log'], 'dt_bias': out['dt_bias'], 'dn_norm_w': out['dn_norm_w'], 'w_out': out['w_out'], 'final_norm_w': out['final_norm_w'], 'loss_target': out['loss_target'], 'm_norm_w': out['m_norm_w'], 'm_w_in': out['m_w_in'], 'm_pool_w': out['m_pool_w'], 'm_pool_scale': out['m_pool_scale'], 'm_conv_w': out['m_conv_w'], 'm_a_log': out['m_a_log'], 'm_dt_bias': out['m_dt_bias'], 'm_dn_norm_w': out['m_dn_norm_w'], 'm_w_out': out['m_w_out'], 'm_final_norm_w': out['m_final_norm_w'], 'v_norm_w': out['v_norm_w'], 'v_w_in': out['v_w_in'], 'v_pool_w': out['v_pool_w'], 'v_pool_scale': out['v_pool_scale'], 'v_conv_w': out['v_conv_w'], 'v_a_log': out['v_a_log'], 'v_dt_bias': out['v_dt_bias'], 'v_dn_norm_w': out['v_dn_norm_w'], 'v_w_out': out['v_w_out'], 'v_final_norm_w': out['v_final_norm_w']}


def _loss(weights, diff, rest, loss_target):
    with _jax.named_scope("forward"):
        args = {**rest, TWIN_DIFF_INPUT: diff, **{k: w.astype(_WEIGHT_DTYPES[k]) for k, w in weights.items()}}
        y = _forward(args)
    with _jax.named_scope("loss_head"):
        err = _jnp.square(y.astype(_jnp.float32) - loss_target)
        return 0.5 * _jnp.sum(_jnp.mean(err, axis=-1)) if err.ndim else 0.5 * err


def _adamw(w, g, m, v):
    m = ADAM_B1 * m + (1.0 - ADAM_B1) * g
    v = ADAM_B2 * v + (1.0 - ADAM_B2) * _jnp.square(g)
    m_hat = m / (1.0 - ADAM_B1 ** ADAM_STEP)
    v_hat = v / (1.0 - ADAM_B2 ** ADAM_STEP)
    delta = -ADAM_LR * (m_hat / (_jnp.sqrt(v_hat) + ADAM_EPS) + ADAM_WD * w)
    return delta, m, v


def reference(x, norm_w, w_in, pool_w, pool_scale, conv_w, a_log, dt_bias, dn_norm_w, w_out, final_norm_w, loss_target, m_norm_w, m_w_in, m_pool_w, m_pool_scale, m_conv_w, m_a_log, m_dt_bias, m_dn_norm_w, m_w_out, m_final_norm_w, v_norm_w, v_w_in, v_pool_w, v_pool_scale, v_conv_w, v_a_log, v_dt_bias, v_dn_norm_w, v_w_out, v_final_norm_w):
    given = dict(x=x, norm_w=norm_w, w_in=w_in, pool_w=pool_w, pool_scale=pool_scale, conv_w=conv_w, a_log=a_log, dt_bias=dt_bias, dn_norm_w=dn_norm_w, w_out=w_out, final_norm_w=final_norm_w, loss_target=loss_target, m_norm_w=m_norm_w, m_w_in=m_w_in, m_pool_w=m_pool_w, m_pool_scale=m_pool_scale, m_conv_w=m_conv_w, m_a_log=m_a_log, m_dt_bias=m_dt_bias, m_dn_norm_w=m_dn_norm_w, m_w_out=m_w_out, m_final_norm_w=m_final_norm_w, v_norm_w=v_norm_w, v_w_in=v_w_in, v_pool_w=v_pool_w, v_pool_scale=v_pool_scale, v_conv_w=v_conv_w, v_a_log=v_a_log, v_dt_bias=v_dt_bias, v_dn_norm_w=v_dn_norm_w, v_w_out=v_w_out, v_final_norm_w=v_final_norm_w)
    weights = {n: given[n] for n in TWIN_WEIGHTS}
    shared = {n: given[n] for n in SHARED_INPUTS}
    per_example = {n: given[n] for n in ['x']}
    grad_fn = _jax.value_and_grad(_loss, argnums=(0, 1))

    def one_microbatch(ex, loss_target):
        ex = dict(ex)
        diff = ex.pop(TWIN_DIFF_INPUT)
        return grad_fn(weights, diff, {**shared, **ex}, loss_target)

    if N_MICROBATCH == 1:
        loss, (grad_w, grad_x) = one_microbatch(per_example, given["loss_target"])
    else:
        def body(carry, xs):
            loss_sum, grad_sum = carry
            l_k, (gw_k, gx_k) = one_microbatch(xs[0], xs[1])
            with _jax.named_scope("update"):
                return (loss_sum + l_k, _jax.tree.map(_jnp.add, grad_sum, gw_k)), gx_k

        init = (_jnp.zeros((), _jnp.float32), _jax.tree.map(_jnp.zeros_like, weights))
        (loss, grad_w), grad_x = _jax.lax.scan(body, init, (per_example, given["loss_target"]))
    with _jax.named_scope("update"):
        delta_w, new_m, new_v = {}, {}, {}
        for n in TWIN_WEIGHTS:
            delta_w[n], new_m[n], new_v[n] = _adamw(weights[n], grad_w[n], given["m_" + n], given["v_" + n])
    return (loss, grad_x, *[grad_w[n] for n in TWIN_WEIGHTS], *[delta_w[n] for n in TWIN_WEIGHTS],
            *[new_m[n] for n in TWIN_WEIGHTS], *[new_v[n] for n in TWIN_WEIGHTS])
```

```python
import functools

import jax
import jax.numpy as jnp
from jax import lax
from jax.experimental import pallas as pl
from jax.experimental.pallas import tpu as pltpu

F32 = jnp.float32
BF16 = jnp.bfloat16
I32 = jnp.int32

D_MODEL = 1024
D_HALF = 512
N_HEADS = 4
HEAD = 128
CHUNK = 64
PAIR = 2 * CHUNK
WINDOWS = (2, 4, 8, 16)
CONV_K = 4
EPS = 1e-6
N_IN = 3080
N_IN_PAD = 3200
BLK_IN = 770
BLK_IN_PAD = 896
BLK_OUT = 256
BLK_CONV = 384
COL_BA = 3072
QK_SCALE = HEAD ** -0.5
SMALL_ROWS = 592
VMEM_LIMIT = 56 * 1024 * 1024

ADAM_LR = 0.001
ADAM_B1 = 0.9
ADAM_B2 = 0.999
ADAM_EPS = 1e-08
ADAM_WD = 0.01
ADAM_STEP = 10

CHIP_MASKS = (2, 1, 3)


def _call(body, **kw):
    return pl.pallas_call(body, **kw)


def _params(*sem):
    return pltpu.CompilerParams(dimension_semantics=sem, vmem_limit_bytes=VMEM_LIMIT)


def _sds(shape, dtype=F32):
    return jax.ShapeDtypeStruct(shape, dtype)


def _bdot(a, b):
    return jnp.dot(a.astype(BF16), b.astype(BF16), preferred_element_type=F32)


def _bdot_nt(a, b):
    return lax.dot_general(a.astype(BF16), b.astype(BF16), (((1,), (1,)), ((), ())), preferred_element_type=F32)


def _bdot_tn(a, b):
    return lax.dot_general(a.astype(BF16), b.astype(BF16), (((0,), (0,)), ((), ())), preferred_element_type=F32)


def _split(a):
    hi = a.astype(BF16)
    lo = (a - hi.astype(F32)).astype(BF16)
    return hi, lo


def _dot3(a, b, dims=(((1,), (0,)), ((), ()))):
    ah, al = _split(a)
    bh, bl = _split(b)
    dg = functools.partial(lax.dot_general, dimension_numbers=dims, preferred_element_type=F32)
    return dg(ah, bh) + (dg(ah, bl) + dg(al, bh))


def _mask_dot(m, b, dims=(((1,), (0,)), ((), ()))):
    bh, bl = _split(b)
    dg = functools.partial(lax.dot_general, dimension_numbers=dims, preferred_element_type=F32)
    return dg(m, bh) + dg(m, bl)


def _sigmoid(x):
    return 1.0 / (1.0 + jnp.exp(-x))


def _softplus(x):
    return jnp.maximum(x, 0.0) + jnp.log(1.0 + jnp.exp(-jnp.abs(x)))


def _rowsum(x):
    return jnp.sum(x, axis=-1, keepdims=True)


def _colsum(x):
    return jnp.sum(x, axis=0, keepdims=True)


def _shift_down(xv, prev8, k):
    r = pltpu.roll(xv, k, 0)
    q = pltpu.roll(prev8, k, 0)
    row = lax.broadcasted_iota(I32, prev8.shape, 0)
    top = jnp.where(row < k, q, r[0:8])
    return jnp.concatenate([top, r[8:]], axis=0)


def _shift_up(xv, next8, k):
    t = xv.shape[0]
    r = pltpu.roll(xv, t - k, 0)
    q = pltpu.roll(next8, 8 - k, 0)
    row = lax.broadcasted_iota(I32, next8.shape, 0)
    bot = jnp.where(row >= 8 - k, q, r[t - 8:])
    return jnp.concatenate([r[:t - 8], bot], axis=0)


def _band(rows, cols, off, w, anti=False):
    r = lax.broadcasted_iota(I32, (rows, cols), 0)
    c = lax.broadcasted_iota(I32, (rows, cols), 1)
    d = (c - r + off) if anti else (r - c + off)
    return ((d >= 0) & (d < w)).astype(BF16)


def _head(ref_or_val, h):
    return ref_or_val[:, h * HEAD:(h + 1) * HEAD]


def _proj_fwd(x, norm_w, w_pad):
    s = x.shape[0]
    tm, tn = 512, 640

    def body(x_ref, nw_ref, w_ref, proj_ref, nbf_ref):
        @pl.when(pl.program_id(1) == 0)
        def _():
            xv = x_ref[...]
            r = lax.rsqrt(jnp.mean(xv * xv, axis=-1, keepdims=True) + EPS)
            nbf_ref[...] = (xv * r * nw_ref[...]).astype(BF16)
        proj_ref[...] = jnp.dot(nbf_ref[...], w_ref[...], preferred_element_type=F32)

    return _call(
        body, name="proj_fwd", grid=(s // tm, N_IN_PAD // tn),
        in_specs=[pl.BlockSpec((tm, D_MODEL), lambda i, j: (i, 0)),
                  pl.BlockSpec((1, D_MODEL), lambda i, j: (0, 0)),
                  pl.BlockSpec((D_MODEL, tn), lambda i, j: (0, j))],
        out_specs=[pl.BlockSpec((tm, tn), lambda i, j: (i, j)),
                   pl.BlockSpec((tm, D_MODEL), lambda i, j: (i, 0))],
        out_shape=[_sds((s, N_IN_PAD)), _sds((s, D_MODEL), BF16)],
        compiler_params=_params("arbitrary", "arbitrary"),
    )(x, norm_w, w_pad)


def _pool_mix(ug, hg, zg, pw_g, ps_g, row0, w):
    t = ug.shape[0]
    win = _mask_dot(_band(t, t, 0, w), ug) + _mask_dot(_band(t, HEAD, HEAD, w), hg)
    cnt = jnp.minimum(row0 + lax.broadcasted_iota(I32, (t, 1), 0) + 1, w).astype(F32)
    mix = win / cnt - ug
    mixed = _bdot(mix, pw_g)
    sg = _sigmoid(zg)
    return mix, mixed, sg, cnt


POOL_T = 256


def _pool_fwd(proj, pool_w, pool_scale):
    s = proj.shape[0]
    t = POOL_T
    hb = t // HEAD

    def body(u_ref, z_ref, halo_ref, pw_ref, ps_ref, y_ref):
        i = pl.program_id(0)
        live = (i > 0).astype(F32)
        for g, w in enumerate(WINDOWS):
            _, mixed, sg, _ = _pool_mix(_head(u_ref, g)[...], _head(halo_ref, g)[...] * live, _head(z_ref, g)[...],
                                        pw_ref[g], _head(ps_ref, g)[...], i * t, w)
            y_ref[:, g * HEAD:(g + 1) * HEAD] = mixed * _head(ps_ref, g)[...] * (_head(z_ref, g)[...] * sg)

    return _call(
        body, name="pool_fwd", grid=(s // t,),
        in_specs=[pl.BlockSpec((t, D_HALF), lambda i: (i, 0)),
                  pl.BlockSpec((t, D_HALF), lambda i: (i, 1)),
                  pl.BlockSpec((HEAD, D_HALF), lambda i: (jnp.maximum(i * hb - 1, 0), 0)),
                  pl.BlockSpec((N_HEADS, HEAD, HEAD), lambda i: (0, 0, 0)),
                  pl.BlockSpec((1, D_HALF), lambda i: (0, 0))],
        out_specs=pl.BlockSpec((t, D_HALF), lambda i: (i, 0)),
        out_shape=_sds((s, D_HALF)),
        compiler_params=_params("arbitrary"),
    )(proj, proj, proj, pool_w, pool_scale)


def _conv_pre(xv, prev8, cw):
    y = xv * cw[3:4]
    for sft in (1, 2, 3):
        y = y + _shift_down(xv, prev8, sft) * cw[3 - sft:4 - sft]
    return y


CONV_T = 256


def _conv_specs(t):
    tiles = [pl.BlockSpec((t, D_HALF), functools.partial(lambda i, p: (i, 2 + p), p=p)) for p in range(3)]
    halos = [pl.BlockSpec((8, D_HALF), functools.partial(lambda i, p: (jnp.maximum(i * (t // 8) - 1, 0), 2 + p), p=p))
             for p in range(3)]
    return tiles + halos


def _conv_fwd(proj, conv_w, a_log, dt_bias):
    s = proj.shape[0]
    t = CONV_T

    def body(q_ref, k_ref, v_ref, hq_ref, hk_ref, hv_ref, ba_ref, cw_ref, al_ref, dtb_ref,
             qn_ref, kn_ref, vs_ref, beta_ref, g_ref):
        live = (pl.program_id(0) > 0).astype(F32)
        for p, (x_ref, h_ref, o_ref) in enumerate(((q_ref, hq_ref, qn_ref), (k_ref, hk_ref, kn_ref), (v_ref, hv_ref, vs_ref))):
            y = _conv_pre(x_ref[...], h_ref[...] * live, cw_ref[:, p * D_HALF:(p + 1) * D_HALF])
            sv = y * _sigmoid(y)
            if p == 2:
                o_ref[...] = sv
            else:
                for h in range(N_HEADS):
                    seg = _head(sv, h)
                    o_ref[:, h * HEAD:(h + 1) * HEAD] = seg * lax.rsqrt(_rowsum(seg * seg) + EPS)
        ba = ba_ref[...]
        for h in range(N_HEADS):
            beta = _sigmoid(ba[:, h:h + 1])
            gl = -jnp.exp(al_ref[0:1, h:h + 1]) * _softplus(ba[:, N_HEADS + h:N_HEADS + h + 1] + dtb_ref[0:1, h:h + 1])
            beta_ref[:, h * HEAD:(h + 1) * HEAD] = jnp.broadcast_to(beta, (t, HEAD))
            g_ref[:, h * HEAD:(h + 1) * HEAD] = jnp.broadcast_to(gl, (t, HEAD))

    row = pl.BlockSpec((t, D_HALF), lambda i: (i, 0))
    return _call(
        body, name="conv_fwd", grid=(s // t,),
        in_specs=_conv_specs(t) + [pl.BlockSpec((t, HEAD), lambda i: (i, COL_BA // HEAD)),
                                   pl.BlockSpec((CONV_K, 3 * D_HALF), lambda i: (0, 0)),
                                   pl.BlockSpec((1, N_HEADS), lambda i: (0, 0)),
                                   pl.BlockSpec((1, N_HEADS), lambda i: (0, 0))],
        out_specs=[row] * 5,
        out_shape=[_sds((s, D_HALF))] * 5,
        compiler_params=_params("arbitrary"),
    )(proj, proj, proj, proj, proj, proj, proj, conv_w, a_log, dt_bias)


def _pair_masks():
    r = lax.broadcasted_iota(I32, (PAIR, PAIR), 0)
    c = lax.broadcasted_iota(I32, (PAIR, PAIR), 1)
    same = jnp.right_shift(r, 6) == jnp.right_shift(c, 6)
    return same, same & (r >= c), same & (r > c), r == c


def _pair_common(qn, kn, vs, beta, g):
    same, incl, strict, eye = _pair_masks()
    gc = _mask_dot(incl.astype(BF16), g)
    gc_row = _colsum(jnp.where(eye, gc, 0.0))
    decay = jnp.where(incl, jnp.exp(jnp.where(incl, gc - gc_row, 0.0)), 0.0)
    first = lax.broadcasted_iota(I32, (PAIR, HEAD), 0) < CHUNK
    gl = jnp.where(first, gc[CHUNK - 1:CHUNK], gc[PAIR - 1:PAIR])
    egc = jnp.exp(gc)
    q = qn * QK_SCALE
    kb = kn * beta
    return dict(same=same, incl=incl, strict=strict, eye=eye, gc=gc, decay=decay, gl=gl, egc=egc,
                ekd=jnp.exp(gl - gc), cd=jnp.exp(gl), q=q, kb=kb, vb=vs * beta, kbg=kb * egc,
                kk=_bdot_nt(kb, kn), qk=_bdot_nt(q, kn))


def _tri_inv(a, eye_f):
    p = eye_f - a
    x = _dot3(a, a)
    for it in range(5):
        p = p + _dot3(p, x)
        if it < 4:
            x = _dot3(x, x)
    return p


def _pair_spec():
    return pl.BlockSpec((PAIR, D_HALF), lambda i: (i, 0))


def _intra_fwd(qn, kn, vs, beta, g):
    s = qn.shape[0]

    def body(qn_ref, kn_ref, vs_ref, beta_ref, g_ref, u_ref, w_ref, att_ref, qd_ref, kd_ref, t_ref, cd_ref):
        for h in range(N_HEADS):
            sl = slice(h * HEAD, (h + 1) * HEAD)
            cm = _pair_common(qn_ref[:, sl], kn_ref[:, sl], vs_ref[:, sl], beta_ref[:, sl], g_ref[:, sl])
            a = jnp.where(cm["strict"], cm["kk"] * cm["decay"], 0.0)
            tm = _tri_inv(a, cm["eye"].astype(F32))
            t_ref[:, sl] = tm
            u_ref[:, sl] = _bdot(tm, cm["vb"])
            w_ref[:, sl] = _bdot(tm, cm["kbg"])
            att_ref[:, sl] = cm["qk"] * cm["decay"]
            qd_ref[:, sl] = cm["q"] * cm["egc"]
            kd_ref[:, sl] = kn_ref[:, sl] * cm["ekd"]
            cd_ref[:, sl] = cm["cd"]

    return _call(
        body, name="intra_fwd", grid=(s // PAIR,),
        in_specs=[_pair_spec()] * 5, out_specs=[_pair_spec()] * 7,
        out_shape=[_sds((s, D_HALF))] * 7,
        compiler_params=_params("arbitrary"),
    )(qn, kn, vs, beta, g)


def _scan_fwd(u, w, att, qd, kd, cd):
    s = u.shape[0]
    n_chunks = s // CHUNK

    def body(u_ref, w_ref, att_ref, qd_ref, kd_ref, cd_ref, o_ref, vn_ref, st_ref, state):
        @pl.when(pl.program_id(0) == 0)
        def _():
            state[...] = jnp.zeros_like(state)
        for h in range(N_HEADS):
            sl = slice(h * HEAD, (h + 1) * HEAD)
            sm = state[h]
            qs = []
            for ci in range(2):
                rs = slice(ci * CHUNK, (ci + 1) * CHUNK)
                st_ref[ci, h] = sm
                both = _bdot(jnp.concatenate([w_ref[rs, sl], qd_ref[rs, sl]], axis=0), sm)
                vn = u_ref[rs, sl] - both[:CHUNK]
                qs.append(both[CHUNK:])
                vn_ref[rs, sl] = vn
                sm = sm * cd_ref[ci * CHUNK:ci * CHUNK + 1, sl] + _bdot_tn(kd_ref[rs, sl], vn)
            state[h] = sm
            o_ref[:, sl] = jnp.concatenate(qs, axis=0) + _bdot(att_ref[:, sl], vn_ref[:, sl])

    return _call(
        body, name="scan_fwd", grid=(s // PAIR,),
        in_specs=[_pair_spec()] * 6,
        out_specs=[_pair_spec(), _pair_spec(), pl.BlockSpec((2, N_HEADS, HEAD, HEAD), lambda i: (i, 0, 0, 0))],
        out_shape=[_sds((s, D_HALF)), _sds((s, D_HALF)), _sds((n_chunks, N_HEADS, HEAD, HEAD))],
        scratch_shapes=[pltpu.VMEM((N_HEADS, HEAD, HEAD), F32)],
        compiler_params=_params("arbitrary"),
    )(u, w, att, qd, kd, cd)


OUT_T = 256


def _out_fwd_bwd(x, y_pool, o, proj, target, w_out, dn_norm_w, final_norm_w):
    s = x.shape[0]
    t = OUT_T

    def body(x_ref, yp_ref, o_ref, z_ref, tg_ref, wo_ref, dnw_ref, fnw_ref,
             y_ref, dh_ref, dyp_ref, do_ref, dz_ref, loss_ref, gfn_ref, gdn_ref):
        @pl.when(pl.program_id(0) == 0)
        def _():
            loss_ref[...] = jnp.zeros_like(loss_ref)
            gfn_ref[...] = jnp.zeros_like(gfn_ref)
            gdn_ref[...] = jnp.zeros_like(gdn_ref)

        y_ref[:, :D_HALF] = yp_ref[...].astype(BF16)
        dnw = dnw_ref[...]
        keep = []
        for h in range(N_HEADS):
            ov = _head(o_ref, h)[...]
            zv = _head(z_ref, h)[...]
            ro = lax.rsqrt(jnp.mean(ov * ov, axis=-1, keepdims=True) + EPS)
            ohat = ov * ro
            sg = _sigmoid(zv)
            keep.append((ro, ohat, zv, sg))
            y_ref[:, D_HALF + h * HEAD:D_HALF + (h + 1) * HEAD] = (ohat * dnw * (zv * sg)).astype(BF16)

        hv = x_ref[...] + jnp.dot(y_ref[...], wo_ref[...], preferred_element_type=F32)
        r2 = lax.rsqrt(jnp.mean(hv * hv, axis=-1, keepdims=True) + EPS)
        hhat = hv * r2
        fnw = fnw_ref[...]
        err = hhat * fnw - tg_ref[...]
        loss_ref[...] += 0.5 * jnp.sum(_rowsum(err * err) * (1.0 / D_MODEL), axis=0, keepdims=True)
        dout = err * (1.0 / D_MODEL)
        gfn_ref[...] += _colsum(dout * hhat)
        dhh = dout * fnw
        dh = r2 * (dhh - hhat * jnp.mean(dhh * hhat, axis=-1, keepdims=True))
        dh_ref[...] = dh
        dy = _bdot_nt(dh, wo_ref[...])
        dyp_ref[...] = dy[:, :D_HALF]
        gdn = jnp.zeros((1, HEAD), F32)
        for h in range(N_HEADS):
            ro, ohat, zv, sg = keep[h]
            dyd = dy[:, D_HALF + h * HEAD:D_HALF + (h + 1) * HEAD]
            sz = zv * sg
            dz_ref[:, h * HEAD:(h + 1) * HEAD] = dyd * ohat * dnw * (sg * (1.0 + zv * (1.0 - sg)))
            gdn = gdn + _colsum(dyd * ohat * sz)
            doh = dyd * dnw * sz
            do_ref[:, h * HEAD:(h + 1) * HEAD] = ro * (doh - ohat * jnp.mean(doh * ohat, axis=-1, keepdims=True))
        gdn_ref[...] += gdn

    wide = pl.BlockSpec((t, D_MODEL), lambda i: (i, 0))
    half = pl.BlockSpec((t, D_HALF), lambda i: (i, 0))
    const = lambda shape: pl.BlockSpec(shape, lambda i: (0,) * len(shape))
    return _call(
        body, name="out_fwd_bwd", grid=(s // t,),
        in_specs=[wide, half, half, pl.BlockSpec((t, D_HALF), lambda i: (i, 5)), wide,
                  const((D_MODEL, D_MODEL)), const((1, HEAD)), const((1, D_MODEL))],
        out_specs=[wide, wide, half, half, half, const((1, HEAD)), const((1, D_MODEL)), const((1, HEAD))],
        out_shape=[_sds((s, D_MODEL), BF16), _sds((s, D_MODEL)), _sds((s, D_HALF)), _sds((s, D_HALF)), _sds((s, D_HALF)),
                   _sds((1, HEAD)), _sds((1, D_MODEL)), _sds((1, HEAD))],
        compiler_params=_params("arbitrary"),
    )(x, y_pool, o, proj, target, w_out, dn_norm_w, final_norm_w)


def _tn_matmul(name, a, b, col_block=None):
    s, m = a.shape
    n = b.shape[1]
    tn = min(n, 512)
    tk = 512
    cb = 0 if col_block is None else col_block

    def body(a_ref, b_ref, o_ref):
        @pl.when(pl.program_id(1) == 0)
        def _():
            o_ref[...] = jnp.zeros_like(o_ref)
        o_ref[...] += _bdot_tn(a_ref[...], b_ref[...])

    return _call(
        body, name=name, grid=(n // tn if col_block is None else 1, s // tk),
        in_specs=[pl.BlockSpec((tk, m), lambda j, k: (k, 0)),
                  pl.BlockSpec((tk, tn), lambda j, k: (k, j + cb))],
        out_specs=pl.BlockSpec((m, tn), lambda j, k: (0, j)),
        out_shape=_sds((m, n if col_block is None else tn)),
        compiler_params=_params("arbitrary", "arbitrary"),
    )(a, b)


def _pool_bwd(proj, dyp, pool_w, pool_scale):
    s = proj.shape[0]
    t = POOL_T
    hb = t // HEAD
    last = s // HEAD - 1

    def body(u_ref, z_ref, halo_ref, dy_ref, zn_ref, dyn_ref, pw_ref, ps_ref, du_ref, dz_ref, gpw_ref, gps_ref):
        i = pl.program_id(0)

        @pl.when(i == 0)
        def _():
            gpw_ref[...] = jnp.zeros_like(gpw_ref)
            gps_ref[...] = jnp.zeros_like(gps_ref)

        live = (i > 0).astype(F32)
        more = (i < pl.num_programs(0) - 1).astype(F32)
        for g, w in enumerate(WINDOWS):
            sl = slice(g * HEAD, (g + 1) * HEAD)
            zg = z_ref[:, sl]
            ps = ps_ref[:, sl]
            pw = pw_ref[g]
            mix, mixed, sg, cnt = _pool_mix(u_ref[:, sl], halo_ref[:, sl] * live, zg, pw, ps, i * t, w)
            dyg = dy_ref[:, sl]
            sz = zg * sg
            dz_ref[:, sl] = dyg * mixed * ps * (sg * (1.0 + zg * (1.0 - sg)))
            gps_ref[:, sl] += _colsum(dyg * mixed * sz)
            dmixed = dyg * ps * sz
            gpw_ref[g] += _bdot_tn(mix, dmixed)
            dmix = _bdot_nt(dmixed, pw)
            zn = zn_ref[:, sl]
            dmix_n = _bdot_nt(dyn_ref[:, sl] * more * ps * (zn * _sigmoid(zn)), pw)
            du_ref[:, sl] = (_mask_dot(_band(t, t, 0, w, anti=True), dmix / cnt)
                             + _mask_dot(_band(t, HEAD, t, w, anti=True), dmix_n * (1.0 / w)) - dmix)

    tile = lambda col: pl.BlockSpec((t, D_HALF), lambda i: (i, col))
    below = lambda col: pl.BlockSpec((HEAD, D_HALF), lambda i: (jnp.minimum((i + 1) * hb, last), col))
    return _call(
        body, name="pool_bwd", grid=(s // t,),
        in_specs=[tile(0), tile(1), pl.BlockSpec((HEAD, D_HALF), lambda i: (jnp.maximum(i * hb - 1, 0), 0)),
                  tile(0), below(1), below(0),
                  pl.BlockSpec((N_HEADS, HEAD, HEAD), lambda i: (0, 0, 0)), pl.BlockSpec((1, D_HALF), lambda i: (0, 0))],
        out_specs=[tile(0), tile(0), pl.BlockSpec((N_HEADS, HEAD, HEAD), lambda i: (0, 0, 0)),
                   pl.BlockSpec((1, D_HALF), lambda i: (0, 0))],
        out_shape=[_sds((s, D_HALF)), _sds((s, D_HALF)), _sds((N_HEADS, HEAD, HEAD)), _sds((1, D_HALF))],
        compiler_params=_params("arbitrary"),
    )(proj, proj, proj, dyp, proj, dyp, pool_w, pool_scale)


def _scan_bwd(do, vn, qd, kd, w, att, cd, st):
    s = do.shape[0]
    n_pairs = s // PAIR

    def body(do_ref, vn_ref, qd_ref, kd_ref, w_ref, att_ref, cd_ref, st_ref,
             du_ref, dw_ref, datt_ref, dqd_ref, dkd_ref, dcd_ref, dstate):
        @pl.when(pl.program_id(0) == 0)
        def _():
            dstate[...] = jnp.zeros_like(dstate)
        _, incl, _, _ = _pair_masks()
        for h in range(N_HEADS):
            sl = slice(h * HEAD, (h + 1) * HEAD)
            dv_intra = _bdot_tn(att_ref[:, sl], do_ref[:, sl])
            datt_ref[:, sl] = jnp.where(incl, _bdot_nt(do_ref[:, sl], vn_ref[:, sl]), 0.0)
            ds = dstate[h]
            for ci in (1, 0):
                rs = slice(ci * CHUNK, (ci + 1) * CHUNK)
                sm = st_ref[ci, h]
                dov = do_ref[rs, sl]
                dvn = dv_intra[rs] + _bdot(kd_ref[rs, sl], ds)
                du_ref[rs, sl] = dvn
                dqd_ref[rs, sl] = _bdot_nt(dov, sm)
                dw_ref[rs, sl] = -_bdot_nt(dvn, sm)
                dkd_ref[rs, sl] = _bdot_nt(vn_ref[rs, sl], ds)
                dcd_ref[rs, sl] = jnp.broadcast_to(_rowsum(_colsum(ds * sm)), (CHUNK, HEAD))
                ds = (ds * cd_ref[ci * CHUNK:ci * CHUNK + 1, sl] + _bdot_tn(qd_ref[rs, sl], dov)
                      - _bdot_tn(w_ref[rs, sl], dvn))
            dstate[h] = ds

    rev = pl.BlockSpec((PAIR, D_HALF), lambda i: (n_pairs - 1 - i, 0))
    return _call(
        body, name="scan_bwd", grid=(n_pairs,),
        in_specs=[rev] * 7 + [pl.BlockSpec((2, N_HEADS, HEAD, HEAD), lambda i: (n_pairs - 1 - i, 0, 0, 0))],
        out_specs=[rev] * 6,
        out_shape=[_sds((s, D_HALF))] * 6,
        scratch_shapes=[pltpu.VMEM((N_HEADS, HEAD, HEAD), F32)],
        compiler_params=_params("arbitrary"),
    )(do, vn, qd, kd, w, att, cd, st)


def _intra_bwd(qn, kn, vs, beta, g, tm, du, dw, datt, dqd, dkd, dcd):
    s = qn.shape[0]

    def body(qn_ref, kn_ref, vs_ref, beta_ref, g_ref, t_ref, du_ref, dw_ref, datt_ref, dqd_ref, dkd_ref, dcd_ref,
             dqn_ref, dkn_ref, dvs_ref, dbeta_ref, dg_ref):
        ones = jnp.ones((PAIR, HEAD), BF16)
        tn = (((0,), (0,)), ((), ()))
        nt = (((1,), (1,)), ((), ()))
        for h in range(N_HEADS):
            sl = slice(h * HEAD, (h + 1) * HEAD)
            kn, vs, beta = kn_ref[:, sl], vs_ref[:, sl], beta_ref[:, sl]
            cm = _pair_common(qn_ref[:, sl], kn, vs, beta, g_ref[:, sl])
            tmv, duv, dwv, dattv, dqdv, dkdv = t_ref[:, sl], du_ref[:, sl], dw_ref[:, sl], datt_ref[:, sl], dqd_ref[:, sl], dkd_ref[:, sl]
            dvb = _bdot_tn(tmv, duv)
            dt = _bdot_nt(duv, cm["vb"]) + _bdot_nt(dwv, cm["kbg"])
            dkbg = _bdot_tn(tmv, dwv)
            da = -jnp.where(cm["strict"], _dot3(_dot3(tmv, dt, tn), tmv, nt), 0.0)
            dkk = da * cm["decay"]
            dqk = dattv * cm["decay"]
            dd = dkk * cm["kk"] + dqk * cm["qk"]
            dkb = _bdot(dkk, kn) + dkbg * cm["egc"]
            dq = _bdot(dqk, kn) + dqdv * cm["egc"]
            dkn = _bdot_tn(dkk, cm["kb"]) + _bdot_tn(dqk, cm["q"]) + dkdv * cm["ekd"] + dkb * beta
            t_kd = _rowsum(dkdv * kn * cm["ekd"])
            ddh, ddl = _split(dd)
            rows_dd = jnp.dot(ddh, ones, preferred_element_type=F32) + jnp.dot(ddl, ones, preferred_element_type=F32)
            cols_dd = (lax.dot_general(ddh, ones, tn, preferred_element_type=F32)
                       + lax.dot_general(ddl, ones, tn, preferred_element_type=F32))
            dgc = rows_dd - cols_dd + _rowsum(dqdv * cm["q"] * cm["egc"]) + _rowsum(dkbg * cm["kbg"]) - t_kd
            dgl = _mask_dot(cm["same"].astype(BF16), jnp.broadcast_to(t_kd, (PAIR, HEAD))) + dcd_ref[:, sl] * cm["cd"]
            rowi = lax.broadcasted_iota(I32, (PAIR, HEAD), 0)
            dgc = dgc + jnp.where(jnp.bitwise_and(rowi, CHUNK - 1) == CHUNK - 1, dgl, 0.0)
            r = lax.broadcasted_iota(I32, (PAIR, PAIR), 0)
            c = lax.broadcasted_iota(I32, (PAIR, PAIR), 1)
            dg_ref[:, sl] = _mask_dot((cm["same"] & (r <= c)).astype(BF16), dgc)
            dbeta_ref[:, sl] = jnp.broadcast_to(_rowsum(dkb * kn) + _rowsum(dvb * vs), (PAIR, HEAD))
            dqn_ref[:, sl] = dq * QK_SCALE
            dkn_ref[:, sl] = dkn
            dvs_ref[:, sl] = dvb * beta

    return _call(
        body, name="intra_bwd", grid=(s // PAIR,),
        in_specs=[_pair_spec()] * 12, out_specs=[_pair_spec()] * 5,
        out_shape=[_sds((s, D_HALF))] * 5,
        compiler_params=_params("arbitrary"),
    )(qn, kn, vs, beta, g, tm, du, dw, datt, dqd, dkd, dcd)


def _conv_bwd_pre(proj, conv_w, a_log, dt_bias, dqn, dkn, dvs, dbeta, dg):
    s = proj.shape[0]
    t = CONV_T

    def body(q_ref, k_ref, v_ref, hq_ref, hk_ref, hv_ref, ba_ref, cw_ref, al_ref, dtb_ref,
             dqn_ref, dkn_ref, dvs_ref, dbeta_ref, dg_ref, dyq_ref, dyk_ref, dyv_ref, dba_ref, gcw_ref, gsm_ref):
        @pl.when(pl.program_id(0) == 0)
        def _():
            gcw_ref[...] = jnp.zeros_like(gcw_ref)
            gsm_ref[...] = jnp.zeros_like(gsm_ref)

        live = (pl.program_id(0) > 0).astype(F32)
        parts = ((q_ref, hq_ref, dqn_ref, dyq_ref), (k_ref, hk_ref, dkn_ref, dyk_ref), (v_ref, hv_ref, dvs_ref, dyv_ref))
        for p, (x_ref, h_ref, d_ref, dy_ref) in enumerate(parts):
            cols = slice(p * D_HALF, (p + 1) * D_HALF)
            xv = x_ref[...]
            prev = h_ref[...] * live
            y = _conv_pre(xv, prev, cw_ref[:, cols])
            sg = _sigmoid(y)
            sv = y * sg
            if p == 2:
                ds = d_ref[...]
            else:
                segs = []
                for h in range(N_HEADS):
                    seg = _head(sv, h)
                    rn = lax.rsqrt(_rowsum(seg * seg) + EPS)
                    nrm = seg * rn
                    dn = _head(d_ref, h)[...]
                    segs.append(rn * (dn - nrm * _rowsum(dn * nrm)))
                ds = jnp.concatenate(segs, axis=1)
            dy = ds * (sg * (1.0 + y * (1.0 - sg)))
            dy_ref[...] = dy
            gcw_ref[3:4, cols] += _colsum(dy * xv)
            for sft in (1, 2, 3):
                gcw_ref[3 - sft:4 - sft, cols] += _colsum(dy * _shift_down(xv, prev, sft))

        ba = ba_ref[...]
        lane = lax.broadcasted_iota(I32, (t, HEAD), 1)
        lane1 = lax.broadcasted_iota(I32, (1, HEAD), 1)
        dba = jnp.zeros((t, HEAD), F32)
        gsm = jnp.zeros((1, HEAD), F32)
        for h in range(N_HEADS):
            beta = _sigmoid(ba[:, h:h + 1])
            dbeta = dbeta_ref[:, h * HEAD:h * HEAD + 1]
            xg = ba[:, N_HEADS + h:N_HEADS + h + 1] + dtb_ref[0:1, h:h + 1]
            nexp = -jnp.exp(al_ref[0:1, h:h + 1])
            dgv = dg_ref[:, h * HEAD:h * HEAD + 1]
            da = dgv * nexp * _sigmoid(xg)
            dba = dba + jnp.where(lane == h, dbeta * beta * (1.0 - beta), 0.0) + jnp.where(lane == N_HEADS + h, da, 0.0)
            gsm = (gsm + jnp.where(lane1 == h, _colsum(dgv * nexp * _softplus(xg)), 0.0)
                   + jnp.where(lane1 == N_HEADS + h, _colsum(da), 0.0))
        dba_ref[...] = dba
        gsm_ref[0:1, :] += gsm

    row = pl.BlockSpec((t, D_HALF), lambda i: (i, 0))
    return _call(
        body, name="conv_bwd_pre", grid=(s // t,),
        in_specs=_conv_specs(t) + [pl.BlockSpec((t, HEAD), lambda i: (i, COL_BA // HEAD)),
                                   pl.BlockSpec((CONV_K, 3 * D_HALF), lambda i: (0, 0)),
                                   pl.BlockSpec((1, N_HEADS), lambda i: (0, 0)),
                                   pl.BlockSpec((1, N_HEADS), lambda i: (0, 0))] + [row] * 5,
        out_specs=[row, row, row, pl.BlockSpec((t, HEAD), lambda i: (i, 0)),
                   pl.BlockSpec((8, 3 * D_HALF), lambda i: (0, 0)), pl.BlockSpec((8, HEAD), lambda i: (0, 0))],
        out_shape=[_sds((s, D_HALF))] * 3 + [_sds((s, HEAD)), _sds((8, 3 * D_HALF)), _sds((8, HEAD))],
        compiler_params=_params("arbitrary"),
    )(proj, proj, proj, proj, proj, proj, proj, conv_w, a_log, dt_bias, dqn, dkn, dvs, dbeta, dg)


def _conv_bwd_in(dyq, dyk, dyv, conv_w):
    s = dyq.shape[0]
    t = CONV_T
    last = s // 8 - 1

    def body(q_ref, k_ref, v_ref, nq_ref, nk_ref, nv_ref, cw_ref, oq_ref, ok_ref, ov_ref):
        more = (pl.program_id(0) < pl.num_programs(0) - 1).astype(F32)
        for p, (d_ref, n_ref, o_ref) in enumerate(((q_ref, nq_ref, oq_ref), (k_ref, nk_ref, ok_ref), (v_ref, nv_ref, ov_ref))):
            cw = cw_ref[:, p * D_HALF:(p + 1) * D_HALF]
            dy = d_ref[...]
            nxt = n_ref[...] * more
            acc = dy * cw[3:4]
            for sft in (1, 2, 3):
                acc = acc + _shift_up(dy, nxt, sft) * cw[3 - sft:4 - sft]
            o_ref[...] = acc

    row = pl.BlockSpec((t, D_HALF), lambda i: (i, 0))
    nxt = pl.BlockSpec((8, D_HALF), lambda i: (jnp.minimum((i + 1) * (t // 8), last), 0))
    return _call(
        body, name="conv_bwd_in", grid=(s // t,),
        in_specs=[row] * 3 + [nxt] * 3 + [pl.BlockSpec((CONV_K, 3 * D_HALF), lambda i: (0, 0))],
        out_specs=[row] * 3, out_shape=[_sds((s, D_HALF))] * 3,
        compiler_params=_params("arbitrary"),
    )(dyq, dyk, dyv, dyq, dyk, dyv, conv_w)


IN_T = 256


def _in_bwd(x, dh, norm_w, w_pad, pieces):
    s = x.shape[0]
    t = IN_T
    widths = [p.shape[1] for p in pieces]

    def body(*refs):
        x_ref, dh_ref, nw_ref, w_ref = refs[:4]
        p_refs = refs[4:4 + len(pieces)]
        gx_ref, gnw_ref = refs[4 + len(pieces):]

        @pl.when(pl.program_id(0) == 0)
        def _():
            gnw_ref[...] = jnp.zeros_like(gnw_ref)

        dn = jnp.zeros((t, D_MODEL), F32)
        col = 0
        for p_ref, wd in zip(p_refs, widths):
            dn = dn + _bdot_nt(p_ref[...], w_ref[:, col:col + wd])
            col += wd
        xv = x_ref[...]
        r = lax.rsqrt(jnp.mean(xv * xv, axis=-1, keepdims=True) + EPS)
        xhat = xv * r
        gnw_ref[...] += _colsum(dn * xhat)
        dxh = dn * nw_ref[...]
        gx_ref[...] = dh_ref[...] + r * (dxh - xhat * jnp.mean(dxh * xhat, axis=-1, keepdims=True))

    wide = pl.BlockSpec((t, D_MODEL), lambda i: (i, 0))
    return _call(
        body, name="in_bwd", grid=(s // t,),
        in_specs=[wide, wide, pl.BlockSpec((1, D_MODEL), lambda i: (0, 0)),
                  pl.BlockSpec((D_MODEL, N_IN_PAD), lambda i: (0, 0))]
                 + [pl.BlockSpec((t, wd), lambda i: (i, 0)) for wd in widths],
        out_specs=[wide, pl.BlockSpec((1, D_MODEL), lambda i: (0, 0))],
        out_shape=[_sds((s, D_MODEL)), _sds((1, D_MODEL))],
        compiler_params=_params("arbitrary"),
    )(x, dh, norm_w, w_pad, *pieces)


def _adamw(name, w, g, m, v, rows):
    r, c = w.shape

    def body(w_ref, g_ref, m_ref, v_ref, d_ref, nm_ref, nv_ref):
        gv = g_ref[...]
        mn = ADAM_B1 * m_ref[...] + (1.0 - ADAM_B1) * gv
        vn = ADAM_B2 * v_ref[...] + (1.0 - ADAM_B2) * (gv * gv)
        m_hat = mn / (1.0 - ADAM_B1 ** ADAM_STEP)
        v_hat = vn / (1.0 - ADAM_B2 ** ADAM_STEP)
        d_ref[...] = -ADAM_LR * (m_hat / (jnp.sqrt(v_hat) + ADAM_EPS) + ADAM_WD * w_ref[...])
        nm_ref[...] = mn
        nv_ref[...] = vn

    blk = pl.BlockSpec((rows, c), lambda i: (i, 0))
    return _call(
        body, name=name, grid=(r // rows,),
        in_specs=[blk] * 4, out_specs=[blk] * 3, out_shape=[_sds((r, c))] * 3,
        compiler_params=_params("arbitrary"),
    )(w, g, m, v)


def _exchange(name, inputs, out_shapes, phases, local_copies=()):
    n_in = len(inputs)
    n_out = len(out_shapes)
    n_cp = sum(len(p) for p in phases)

    def body(*refs):
        ins, outs = refs[:n_in], refs[n_in:n_in + n_out]
        send, recv, lsem = refs[n_in + n_out:]
        pos = (lax.axis_index("x"), lax.axis_index("y"), lax.axis_index("c"))
        locs = [pltpu.make_async_copy(src(ins, outs, pos), dst(ins, outs, pos), lsem.at[i])
                for i, (src, dst) in enumerate(local_copies)]
        for lc in locs:
            lc.start()
        k = 0
        for phase in phases:
            cps = []
            for src, dst, target in phase:
                cps.append(pltpu.make_async_remote_copy(
                    src_ref=src(ins, outs, pos), dst_ref=dst(ins, outs, pos), send_sem=send.at[k], recv_sem=recv.at[k],
                    device_id=target(pos), device_id_type=pl.DeviceIdType.MESH))
                k += 1
            for cp in cps:
                cp.start()
            for cp in cps:
                cp.wait()
        for lc in locs:
            lc.wait()

    anyspec = pl.BlockSpec(memory_space=pl.ANY)
    return _call(
        body, name=name,
        in_specs=[anyspec] * n_in, out_specs=[anyspec] * n_out, out_shape=list(out_shapes),
        scratch_shapes=[pltpu.SemaphoreType.DMA((n_cp,)), pltpu.SemaphoreType.DMA((n_cp,)),
                        pltpu.SemaphoreType.DMA((max(len(local_copies), 1),))],
    )(*inputs)


def _chip(pos):
    return 2 * pos[0] + pos[1]


def _other_chip(pos, mask):
    x, y, c = pos
    return (x ^ (mask >> 1), y ^ (mask & 1), c)


def _sibling(pos):
    return (pos[0], pos[1], 1 - pos[2])


def _gather_weights(wb, ob, cb):
    halves = (wb.shape[0] // 2, ob.shape[0] // 2)

    def half(a, pos):
        return pl.ds(pos[2] * halves[a], halves[a])

    first, second = [], []
    for mask in CHIP_MASKS:
        for a in (0, 1):
            first.append((lambda ins, outs, pos, a=a: ins[a].at[half(a, pos)],
                          lambda ins, outs, pos, a=a: outs[a].at[_chip(pos), half(a, pos)],
                          functools.partial(_other_chip, mask=mask)))
            second.append((lambda ins, outs, pos, a=a, mask=mask: outs[a].at[_chip(pos) ^ mask, half(a, pos)],
                           lambda ins, outs, pos, a=a, mask=mask: outs[a].at[_chip(pos) ^ mask, half(a, pos)],
                           _sibling))
        first.append((lambda ins, outs, pos: ins[2],
                      lambda ins, outs, pos: outs[2].at[_chip(pos)],
                      functools.partial(_other_chip, mask=mask)))
    local = [(lambda ins, outs, pos, a=a: ins[a], lambda ins, outs, pos, a=a: outs[a].at[_chip(pos)]) for a in range(3)]
    return _exchange("gather_weights", [wb, ob, cb],
                     [_sds((4,) + wb.shape, wb.dtype), _sds((4,) + ob.shape, ob.dtype), _sds((4,) + cb.shape, cb.dtype)],
                     [first, second], local)


def _to_sibling_half(name, arrays):
    def src(ins, outs, pos, a):
        h = arrays[a].shape[-2] // 2
        sl = pl.ds((1 - pos[2]) * h, h)
        return ins[a].at[:, sl] if arrays[a].ndim == 3 else ins[a].at[sl]

    outs = [_sds(a.shape[:-2] + (a.shape[-2] // 2, a.shape[-1]), a.dtype) for a in arrays]
    phase = [(functools.partial(src, a=a), lambda ins, outs, pos, a=a: outs[a], _sibling) for a in range(len(arrays))]
    return _exchange(name, arrays, outs, [phase])


def _add_half(name, full, part, cidx):
    shape = part.shape
    lead = shape[0] if len(shape) == 3 else 1
    rows, cols = shape[-2], shape[-1]
    tr = rows // 2 if rows % 16 == 0 else rows
    nr = rows // tr
    f3 = full.reshape((lead,) + full.shape[-2:])
    p3 = part.reshape((lead, rows, cols))

    def body(c_ref, f_ref, p_ref, o_ref):
        o_ref[...] = f_ref[...] + p_ref[...]

    out = _call(
        body, name=name,
        grid_spec=pltpu.PrefetchScalarGridSpec(
            num_scalar_prefetch=1, grid=(lead, nr),
            in_specs=[pl.BlockSpec((1, tr, cols), lambda b, r, c_ref: (b, c_ref[0] * nr + r, 0)),
                      pl.BlockSpec((1, tr, cols), lambda b, r, c_ref: (b, r, 0))],
            out_specs=pl.BlockSpec((1, tr, cols), lambda b, r, c_ref: (b, r, 0))),
        out_shape=_sds((lead, rows, cols)),
        compiler_params=_params("arbitrary", "arbitrary"),
    )(cidx, f3, p3)
    return out.reshape(shape)


def _to_other_chips(name, arrays, blocked):
    def src(ins, outs, pos, a, mask):
        return ins[a].at[_chip(pos) ^ mask] if blocked[a] else ins[a]

    outs = [_sds((3,) + (a.shape[1:] if b else a.shape), a.dtype) for a, b in zip(arrays, blocked)]
    phase = []
    for mi, mask in enumerate(CHIP_MASKS):
        for a in range(len(arrays)):
            phase.append((functools.partial(src, a=a, mask=mask), lambda ins, outs, pos, a=a, mi=mi: outs[a].at[mi],
                          functools.partial(_other_chip, mask=mask)))
    return _exchange(name, arrays, outs, [phase])


def _add_chips(name, own, got, jidx, blocked):
    rows, cols = got.shape[-2:]
    tr = rows // 2 if rows % 16 == 0 else rows
    nr = rows // tr
    o3 = own if blocked else own.reshape((1, rows, cols))

    def body(j_ref, o_ref, g_ref, out_ref):
        out_ref[...] = (o_ref[0] + g_ref[0]) + (g_ref[1] + g_ref[2])

    own_map = (lambda r, j_ref: (j_ref[0], r, 0)) if blocked else (lambda r, j_ref: (0, r, 0))
    return _call(
        body, name=name,
        grid_spec=pltpu.PrefetchScalarGridSpec(
            num_scalar_prefetch=1, grid=(nr,),
            in_specs=[pl.BlockSpec((1, tr, cols), own_map),
                      pl.BlockSpec((3, tr, cols), lambda r, j_ref: (0, r, 0))],
            out_specs=pl.BlockSpec((tr, cols), lambda r, j_ref: (r, 0))),
        out_shape=_sds((rows, cols)),
        compiler_params=_params("arbitrary"),
    )(jidx, o3, got)


def _join_halves(name, arrays):
    def where(outs, pos, a):
        h = arrays[a].shape[0]
        return outs[a].at[pl.ds(pos[2] * h, h)]

    outs = [_sds((2 * a.shape[0], a.shape[1]), a.dtype) for a in arrays]
    phase = [(lambda ins, outs, pos, a=a: ins[a], lambda ins, outs, pos, a=a: where(outs, pos, a), _sibling)
             for a in range(len(arrays))]
    local = [(lambda ins, outs, pos, a=a: ins[a], lambda ins, outs, pos, a=a: where(outs, pos, a))
             for a in range(len(arrays))]
    return _exchange(name, arrays, outs, [phase], local)


def _local_step(x, target, w_pad, w_out, conv_w, norm_w, pool_w, pool_scale, a_log, dt_bias, dn_norm_w, final_norm_w):
    proj, nbf = _proj_fwd(x, norm_w, w_pad)
    y_pool = _pool_fwd(proj, pool_w, pool_scale)
    qn, kn, vs, beta, g = _conv_fwd(proj, conv_w, a_log, dt_bias)
    u, w, att, qd, kd, tm, cd = _intra_fwd(qn, kn, vs, beta, g)
    o, vn, st = _scan_fwd(u, w, att, qd, kd, cd)
    y, dh, dyp, do, ddz, loss, g_fnw, g_dnw = _out_fwd_bwd(x, y_pool, o, proj, target, w_out, dn_norm_w, final_norm_w)
    g_wout = _tn_matmul("grad_w_out", y, dh)
    dpu, dpz, g_pw, g_ps = _pool_bwd(proj, dyp, pool_w, pool_scale)
    du, dw, datt, dqd, dkd, dcd = _scan_bwd(do, vn, qd, kd, w, att, cd, st)
    dqn, dkn, dvs, dbeta, dg = _intra_bwd(qn, kn, vs, beta, g, tm, du, dw, datt, dqd, dkd, dcd)
    dyq, dyk, dyv, dba, g_cw, g_sm = _conv_bwd_pre(proj, conv_w, a_log, dt_bias, dqn, dkn, dvs, dbeta, dg)
    dcq, dck, dcv = _conv_bwd_in(dyq, dyk, dyv, conv_w)
    pieces = [dpu, dpz, dcq, dck, dcv, ddz, dba]
    gx, g_nw = _in_bwd(x, dh, norm_w, w_pad, pieces)
    g_win = jnp.concatenate([_tn_matmul("grad_w_in_%d" % i, nbf, p) for i, p in enumerate(pieces)], axis=1)
    small = dict(norm_w=g_nw, pool_w=g_pw, pool_scale=g_ps, conv_w=g_cw[:CONV_K], a_log=g_sm[0:1, 0:N_HEADS],
                 dt_bias=g_sm[0:1, N_HEADS:2 * N_HEADS], dn_norm_w=g_dnw, final_norm_w=g_fnw)
    return loss[0, 0], gx, g_win, g_wout, small


def _pack_small(t):
    lanes = lambda a: jnp.pad(a.reshape(1, -1), ((0, 0), (0, HEAD - a.size)))
    rows = [t["pool_w"].reshape(-1, HEAD), t["norm_w"].reshape(-1, HEAD), t["final_norm_w"].reshape(-1, HEAD),
            t["pool_scale"].reshape(-1, HEAD), t["conv_w"].reshape(-1, HEAD), t["dn_norm_w"].reshape(1, HEAD),
            lanes(t["a_log"]), lanes(t["dt_bias"])]
    buf = jnp.concatenate(rows, axis=0)
    return jnp.pad(buf, ((0, SMALL_ROWS - buf.shape[0]), (0, 0)))


def _unpack_small(buf, conv_cols):
    out, r = {}, 0
    for name, nrow, shape in (("pool_w", 512, (1, N_HEADS, HEAD, HEAD)), ("norm_w", 8, (1, D_MODEL)),
                              ("final_norm_w", 8, (D_MODEL,)), ("pool_scale", 4, (1, D_HALF)),
                              ("conv_w", CONV_K * conv_cols // HEAD, (1, CONV_K, conv_cols)), ("dn_norm_w", 1, (1, HEAD))):
        out[name] = buf[r:r + nrow].reshape(shape)
        r += nrow
    out["a_log"] = buf[r:r + 1, :N_HEADS]
    out["dt_bias"] = buf[r + 1:r + 2, :N_HEADS]
    return out


def kernel(x, norm_w, w_in, pool_w, pool_scale, conv_w, a_log, dt_bias, dn_norm_w, w_out, final_norm_w, loss_target, m_norm_w, m_w_in, m_pool_w, m_pool_scale, m_conv_w, m_a_log, m_dt_bias, m_dn_norm_w, m_w_out, m_final_norm_w, v_norm_w, v_w_in, v_pool_w, v_pool_scale, v_conv_w, v_a_log, v_dt_bias, v_dn_norm_w, v_w_out, v_final_norm_w):
    cidx = lax.axis_index("c").astype(I32).reshape(1)
    jidx = (2 * lax.axis_index("x") + lax.axis_index("y")).astype(I32)

    wb = jnp.pad(w_in[0].astype(BF16), ((0, 0), (0, BLK_IN_PAD - BLK_IN)))
    gw, go, gc = _gather_weights(wb, w_out[0].astype(BF16), conv_w[0])
    w_full = jnp.transpose(gw[:, :, :BLK_IN], (1, 0, 2)).reshape(D_MODEL, N_IN)
    w_pad = jnp.pad(w_full, ((0, 0), (0, N_IN_PAD - N_IN)))
    wo_full = go.reshape(D_MODEL, D_MODEL)
    cw_full = jnp.transpose(gc, (1, 0, 2)).reshape(CONV_K, 3 * D_HALF)

    loss, gx, g_win, g_wout, small = _local_step(
        x[0], loss_target[0], w_pad, wo_full, cw_full, norm_w, pool_w[0], pool_scale, a_log, dt_bias, dn_norm_w,
        final_norm_w.reshape(1, D_MODEL))

    blocks_in = jnp.pad(jnp.transpose(g_win[:, :N_IN].reshape(D_MODEL, 4, BLK_IN), (1, 0, 2)),
                        ((0, 0), (0, 0), (0, BLK_IN_PAD - BLK_IN)))
    blocks_out = g_wout.reshape(4, BLK_OUT, D_MODEL)
    packed = _pack_small(small)
    full = [blocks_in, blocks_out, packed]
    from_sib = _to_sibling_half("reduce_sibling", full)
    chip_sum = [_add_half("add_sibling_%d" % i, f, p, cidx) for i, (f, p) in enumerate(zip(full, from_sib))]
    blocked = [True, True, False]
    from_chips = _to_other_chips("reduce_chips", chip_sum, blocked)
    halves = [_add_chips("add_chips_%d" % i, o, g, jidx.reshape(1), b)
              for i, (o, g, b) in enumerate(zip(chip_sum, from_chips, blocked))]
    g_in_blk, g_out_blk, g_small = _join_halves("join_halves", halves)

    grads = _unpack_small(g_small, 3 * D_HALF)
    grads["conv_w"] = lax.dynamic_slice_in_dim(grads["conv_w"], jidx * BLK_CONV, BLK_CONV, axis=2)
    grads["w_in"] = g_in_blk[:, :BLK_IN][None]
    grads["w_out"] = g_out_blk[None]

    weights = dict(norm_w=norm_w, w_in=w_in, pool_w=pool_w, pool_scale=pool_scale, conv_w=conv_w, a_log=a_log,
                   dt_bias=dt_bias, dn_norm_w=dn_norm_w, w_out=w_out, final_norm_w=final_norm_w)
    ms = dict(norm_w=m_norm_w, w_in=m_w_in, pool_w=m_pool_w, pool_scale=m_pool_scale, conv_w=m_conv_w, a_log=m_a_log,
              dt_bias=m_dt_bias, dn_norm_w=m_dn_norm_w, w_out=m_w_out, final_norm_w=m_final_norm_w)
    vs = dict(norm_w=v_norm_w, w_in=v_w_in, pool_w=v_pool_w, pool_scale=v_pool_scale, conv_w=v_conv_w, a_log=v_a_log,
              dt_bias=v_dt_bias, dn_norm_w=v_dn_norm_w, w_out=v_w_out, final_norm_w=v_final_norm_w)
    names = ["norm_w", "w_in", "pool_w", "pool_scale", "conv_w", "a_log", "dt_bias", "dn_norm_w", "w_out", "final_norm_w"]

    delta, new_m, new_v = {}, {}, {}
    for name, rows in (("w_in", 256), ("w_out", 256)):
        two_d = lambda a: a.reshape(a.shape[-2:])
        d, nm, nv = _adamw("adamw_" + name, two_d(weights[name]), two_d(grads[name]), two_d(ms[name]), two_d(vs[name]), rows)
        delta[name], new_m[name], new_v[name] = d[None], nm[None], nv[None]
    small_names = [n for n in names if n not in ("w_in", "w_out")]
    pack = lambda t: _pack_small({**{n: t[n] for n in small_names if n != "conv_w"},
                                  "conv_w": jnp.pad(t["conv_w"][0], ((0, 0), (0, 3 * D_HALF - BLK_CONV)))})
    d, nm, nv = _adamw("adamw_small", pack(weights), pack(grads), pack(ms), pack(vs), SMALL_ROWS)
    for res, buf in ((delta, d), (new_m, nm), (new_v, nv)):
        got = _unpack_small(buf, 3 * D_HALF)
        got["conv_w"] = got["conv_w"][:, :, :BLK_CONV]
        res.update(got)

    total = lax.psum(loss, ("x", "y", "c"))
    return (total, gx[None], *[grads[n] for n in names], *[delta[n] for n in names],
            *[new_m[n] for n in names], *[new_v[n] for n in names])
```

```python
import functools

import jax
import jax.numpy as jnp
from jax import lax
from jax.experimental import pallas as pl
from jax.experimental.pallas import tpu as pltpu

F32 = jnp.float32
BF16 = jnp.bfloat16
I32 = jnp.int32

D_MODEL = 1024
D_HALF = 512
N_HEADS = 4
HEAD = 128
CHUNK = 64
PAIR = 2 * CHUNK
WINDOWS = (2, 4, 8, 16)
CONV_K = 4
EPS = 1e-6
N_IN = 3080
N_IN_PAD = 3200
BLK_IN = 770
BLK_IN_PAD = 896
BLK_OUT = 256
BLK_CONV = 384
COL_BA = 3072
QK_SCALE = HEAD ** -0.5
SMALL_ROWS = 592
VMEM_LIMIT = 56 * 1024 * 1024

ADAM_LR = 0.001
ADAM_B1 = 0.9
ADAM_B2 = 0.999
ADAM_EPS = 1e-08
ADAM_WD = 0.01
ADAM_STEP = 10

CHIP_MASKS = (2, 1, 3)


def _call(body, **kw):
    return pl.pallas_call(body, **kw)


def _params(*sem):
    return pltpu.CompilerParams(dimension_semantics=sem, vmem_limit_bytes=VMEM_LIMIT)


def _sds(shape, dtype=F32):
    return jax.ShapeDtypeStruct(shape, dtype)


def _bdot(a, b):
    return jnp.dot(a.astype(BF16), b.astype(BF16), preferred_element_type=F32)


def _bdot_nt(a, b):
    return lax.dot_general(a.astype(BF16), b.astype(BF16), (((1,), (1,)), ((), ())), preferred_element_type=F32)


def _bdot_tn(a, b):
    return lax.dot_general(a.astype(BF16), b.astype(BF16), (((0,), (0,)), ((), ())), preferred_element_type=F32)


def _split(a):
    hi = a.astype(BF16)
    lo = (a - hi.astype(F32)).astype(BF16)
    return hi, lo


def _dot3(a, b, dims=(((1,), (0,)), ((), ()))):
    ah, al = _split(a)
    bh, bl = _split(b)
    dg = functools.partial(lax.dot_general, dimension_numbers=dims, preferred_element_type=F32)
    return dg(ah, bh) + (dg(ah, bl) + dg(al, bh))


def _mask_dot(m, b, dims=(((1,), (0,)), ((), ()))):
    bh, bl = _split(b)
    dg = functools.partial(lax.dot_general, dimension_numbers=dims, preferred_element_type=F32)
    return dg(m, bh) + dg(m, bl)


def _sigmoid(x):
    return 1.0 / (1.0 + jnp.exp(-x))


def _softplus(x):
    return jnp.maximum(x, 0.0) + jnp.log(1.0 + jnp.exp(-jnp.abs(x)))


def _rowsum(x):
    return jnp.sum(x, axis=-1, keepdims=True)


def _colsum(x):
    return jnp.sum(x, axis=0, keepdims=True)


def _shift_down(xv, prev8, k):
    r = pltpu.roll(xv, k, 0)
    q = pltpu.roll(prev8, k, 0)
    row = lax.broadcasted_iota(I32, prev8.shape, 0)
    top = jnp.where(row < k, q, r[0:8])
    return jnp.concatenate([top, r[8:]], axis=0)


def _shift_up(xv, next8, k):
    t = xv.shape[0]
    r = pltpu.roll(xv, t - k, 0)
    q = pltpu.roll(next8, 8 - k, 0)
    row = lax.broadcasted_iota(I32, next8.shape, 0)
    bot = jnp.where(row >= 8 - k, q, r[t - 8:])
    return jnp.concatenate([r[:t - 8], bot], axis=0)


def _band(rows, cols, off, w, anti=False):
    r = lax.broadcasted_iota(I32, (rows, cols), 0)
    c = lax.broadcasted_iota(I32, (rows, cols), 1)
    d = (c - r + off) if anti else (r - c + off)
    return ((d >= 0) & (d < w)).astype(BF16)


def _head(ref_or_val, h):
    return ref_or_val[:, h * HEAD:(h + 1) * HEAD]


def _proj_fwd(x, norm_w, w_pad):
    s = x.shape[0]
    tm, tn = 512, 640

    def body(x_ref, nw_ref, w_ref, proj_ref, nbf_ref):
        @pl.when(pl.program_id(1) == 0)
        def _():
            xv = x_ref[...]
            r = lax.rsqrt(jnp.mean(xv * xv, axis=-1, keepdims=True) + EPS)
            nbf_ref[...] = (xv * r * nw_ref[...]).astype(BF16)
        proj_ref[...] = jnp.dot(nbf_ref[...], w_ref[...], preferred_element_type=F32)

    return _call(
        body, name="proj_fwd", grid=(s // tm, N_IN_PAD // tn),
        in_specs=[pl.BlockSpec((tm, D_MODEL), lambda i, j: (i, 0)),
                  pl.BlockSpec((1, D_MODEL), lambda i, j: (0, 0)),
                  pl.BlockSpec((D_MODEL, tn), lambda i, j: (0, j))],
        out_specs=[pl.BlockSpec((tm, tn), lambda i, j: (i, j)),
                   pl.BlockSpec((tm, D_MODEL), lambda i, j: (i, 0))],
        out_shape=[_sds((s, N_IN_PAD)), _sds((s, D_MODEL), BF16)],
        compiler_params=_params("arbitrary", "arbitrary"),
    )(x, norm_w, w_pad)


def _pool_mix(ug, hg, zg, pw_g, ps_g, row0, w):
    t = ug.shape[0]
    win = _mask_dot(_band(t, t, 0, w), ug) + _mask_dot(_band(t, HEAD, HEAD, w), hg)
    cnt = jnp.minimum(row0 + lax.broadcasted_iota(I32, (t, 1), 0) + 1, w).astype(F32)
    mix = win / cnt - ug
    mixed = _bdot(mix, pw_g)
    sg = _sigmoid(zg)
    return mix, mixed, sg, cnt


POOL_T = 256


def _pool_fwd(proj, pool_w, pool_scale):
    s = proj.shape[0]
    t = POOL_T
    hb = t // HEAD

    def body(u_ref, z_ref, halo_ref, pw_ref, ps_ref, y_ref):
        i = pl.program_id(0)
        live = (i > 0).astype(F32)
        for g, w in enumerate(WINDOWS):
            _, mixed, sg, _ = _pool_mix(_head(u_ref, g)[...], _head(halo_ref, g)[...] * live, _head(z_ref, g)[...],
                                        pw_ref[g], _head(ps_ref, g)[...], i * t, w)
            y_ref[:, g * HEAD:(g + 1) * HEAD] = mixed * _head(ps_ref, g)[...] * (_head(z_ref, g)[...] * sg)

    return _call(
        body, name="pool_fwd", grid=(s // t,),
        in_specs=[pl.BlockSpec((t, D_HALF), lambda i: (i, 0)),
                  pl.BlockSpec((t, D_HALF), lambda i: (i, 1)),
                  pl.BlockSpec((HEAD, D_HALF), lambda i: (jnp.maximum(i * hb - 1, 0), 0)),
                  pl.BlockSpec((N_HEADS, HEAD, HEAD), lambda i: (0, 0, 0)),
                  pl.BlockSpec((1, D_HALF), lambda i: (0, 0))],
        out_specs=pl.BlockSpec((t, D_HALF), lambda i: (i, 0)),
        out_shape=_sds((s, D_HALF)),
        compiler_params=_params("arbitrary"),
    )(proj, proj, proj, pool_w, pool_scale)


def _conv_pre(xv, prev8, cw):
    y = xv * cw[3:4]
    for sft in (1, 2, 3):
        y = y + _shift_down(xv, prev8, sft) * cw[3 - sft:4 - sft]
    return y


CONV_T = 256


def _conv_specs(t):
    tiles = [pl.BlockSpec((t, D_HALF), functools.partial(lambda i, p: (i, 2 + p), p=p)) for p in range(3)]
    halos = [pl.BlockSpec((8, D_HALF), functools.partial(lambda i, p: (jnp.maximum(i * (t // 8) - 1, 0), 2 + p), p=p))
             for p in range(3)]
    return tiles + halos


def _conv_fwd(proj, conv_w, a_log, dt_bias):
    s = proj.shape[0]
    t = CONV_T

    def body(q_ref, k_ref, v_ref, hq_ref, hk_ref, hv_ref, ba_ref, cw_ref, al_ref, dtb_ref,
             qn_ref, kn_ref, vs_ref, beta_ref, g_ref):
        live = (pl.program_id(0) > 0).astype(F32)
        for p, (x_ref, h_ref, o_ref) in enumerate(((q_ref, hq_ref, qn_ref), (k_ref, hk_ref, kn_ref), (v_ref, hv_ref, vs_ref))):
            y = _conv_pre(x_ref[...], h_ref[...] * live, cw_ref[:, p * D_HALF:(p + 1) * D_HALF])
            sv = y * _sigmoid(y)
            if p == 2:
                o_ref[...] = sv
            else:
                for h in range(N_HEADS):
                    seg = _head(sv, h)
                    o_ref[:, h * HEAD:(h + 1) * HEAD] = seg * lax.rsqrt(_rowsum(seg * seg) + EPS)
        ba = ba_ref[...]
        for h in range(N_HEADS):
            beta = _sigmoid(ba[:, h:h + 1])
            gl = -jnp.exp(al_ref[0:1, h:h + 1]) * _softplus(ba[:, N_HEADS + h:N_HEADS + h + 1] + dtb_ref[0:1, h:h + 1])
            beta_ref[:, h * HEAD:(h + 1) * HEAD] = jnp.broadcast_to(beta, (t, HEAD))
            g_ref[:, h * HEAD:(h + 1) * HEAD] = jnp.broadcast_to(gl, (t, HEAD))

    row = pl.BlockSpec((t, D_HALF), lambda i: (i, 0))
    return _call(
        body, name="conv_fwd", grid=(s // t,),
        in_specs=_conv_specs(t) + [pl.BlockSpec((t, HEAD), lambda i: (i, COL_BA // HEAD)),
                                   pl.BlockSpec((CONV_K, 3 * D_HALF), lambda i: (0, 0)),
                                   pl.BlockSpec((1, N_HEADS), lambda i: (0, 0)),
                                   pl.BlockSpec((1, N_HEADS), lambda i: (0, 0))],
        out_specs=[row] * 5,
        out_shape=[_sds((s, D_HALF))] * 5,
        compiler_params=_params("arbitrary"),
    )(proj, proj, proj, proj, proj, proj, proj, conv_w, a_log, dt_bias)


def _pair_masks():
    r = lax.broadcasted_iota(I32, (PAIR, PAIR), 0)
    c = lax.broadcasted_iota(I32, (PAIR, PAIR), 1)
    same = jnp.right_shift(r, 6) == jnp.right_shift(c, 6)
    return same, same & (r >= c), same & (r > c), r == c


def _pair_common(qn, kn, vs, beta, g):
    same, incl, strict, eye = _pair_masks()
    gc = _mask_dot(incl.astype(BF16), g)
    gc_row = _colsum(jnp.where(eye, gc, 0.0))
    decay = jnp.where(incl, jnp.exp(jnp.where(incl, gc - gc_row, 0.0)), 0.0)
    first = lax.broadcasted_iota(I32, (PAIR, HEAD), 0) < CHUNK
    gl = jnp.where(first, gc[CHUNK - 1:CHUNK], gc[PAIR - 1:PAIR])
    egc = jnp.exp(gc)
    q = qn * QK_SCALE
    kb = kn * beta
    return dict(same=same, incl=incl, strict=strict, eye=eye, gc=gc, decay=decay, gl=gl, egc=egc,
                ekd=jnp.exp(gl - gc), cd=jnp.exp(gl), q=q, kb=kb, vb=vs * beta, kbg=kb * egc,
                kk=_bdot_nt(kb, kn), qk=_bdot_nt(q, kn))


def _tri_inv(a, eye_f):
    p = eye_f - a
    x = _dot3(a, a)
    for it in range(5):
        p = p + _dot3(p, x)
        if it < 4:
            x = _dot3(x, x)
    return p


def _pair_spec():
    return pl.BlockSpec((PAIR, D_HALF), lambda i: (i, 0))


def _intra_fwd(qn, kn, vs, beta, g):
    s = qn.shape[0]

    def body(qn_ref, kn_ref, vs_ref, beta_ref, g_ref, u_ref, w_ref, att_ref, qd_ref, kd_ref, t_ref, cd_ref):
        for h in range(N_HEADS):
            sl = slice(h * HEAD, (h + 1) * HEAD)
            cm = _pair_common(qn_ref[:, sl], kn_ref[:, sl], vs_ref[:, sl], beta_ref[:, sl], g_ref[:, sl])
            a = jnp.where(cm["strict"], cm["kk"] * cm["decay"], 0.0)
            tm = _tri_inv(a, cm["eye"].astype(F32))
            t_ref[:, sl] = tm
            u_ref[:, sl] = _bdot(tm, cm["vb"])
            w_ref[:, sl] = _bdot(tm, cm["kbg"])
            att_ref[:, sl] = cm["qk"] * cm["decay"]
            qd_ref[:, sl] = cm["q"] * cm["egc"]
            kd_ref[:, sl] = kn_ref[:, sl] * cm["ekd"]
            cd_ref[:, sl] = cm["cd"]

    return _call(
        body, name="intra_fwd", grid=(s // PAIR,),
        in_specs=[_pair_spec()] * 5, out_specs=[_pair_spec()] * 7,
        out_shape=[_sds((s, D_HALF))] * 7,
        compiler_params=_params("arbitrary"),
    )(qn, kn, vs, beta, g)


def _scan_fwd(u, w, att, qd, kd, cd):
    s = u.shape[0]
    n_chunks = s // CHUNK

    def body(u_ref, w_ref, att_ref, qd_ref, kd_ref, cd_ref, o_ref, vn_ref, st_ref, state):
        @pl.when(pl.program_id(0) == 0)
        def _():
            state[...] = jnp.zeros_like(state)
        for h in range(N_HEADS):
            sl = slice(h * HEAD, (h + 1) * HEAD)
            sm = state[h]
            qs = []
            for ci in range(2):
                rs = slice(ci * CHUNK, (ci + 1) * CHUNK)
                st_ref[ci, h] = sm
                both = _bdot(jnp.concatenate([w_ref[rs, sl], qd_ref[rs, sl]], axis=0), sm)
                vn = u_ref[rs, sl] - both[:CHUNK]
                qs.append(both[CHUNK:])
                vn_ref[rs, sl] = vn
                sm = sm * cd_ref[ci * CHUNK:ci * CHUNK + 1, sl] + _bdot_tn(kd_ref[rs, sl], vn)
            state[h] = sm
            o_ref[:, sl] = jnp.concatenate(qs, axis=0) + _bdot(att_ref[:, sl], vn_ref[:, sl])

    return _call(
        body, name="scan_fwd", grid=(s // PAIR,),
        in_specs=[_pair_spec()] * 6,
        out_specs=[_pair_spec(), _pair_spec(), pl.BlockSpec((2, N_HEADS, HEAD, HEAD), lambda i: (i, 0, 0, 0))],
        out_shape=[_sds((s, D_HALF)), _sds((s, D_HALF)), _sds((n_chunks, N_HEADS, HEAD, HEAD))],
        scratch_shapes=[pltpu.VMEM((N_HEADS, HEAD, HEAD), F32)],
        compiler_params=_params("arbitrary"),
    )(u, w, att, qd, kd, cd)


OUT_T = 256


def _out_fwd_bwd(x, y_pool, o, proj, target, w_out, dn_norm_w, final_norm_w):
    s = x.shape[0]
    t = OUT_T

    def body(x_ref, yp_ref, o_ref, z_ref, tg_ref, wo_ref, dnw_ref, fnw_ref,
             y_ref, dh_ref, dyp_ref, do_ref, dz_ref, loss_ref, gfn_ref, gdn_ref):
        @pl.when(pl.program_id(0) == 0)
        def _():
            loss_ref[...] = jnp.zeros_like(loss_ref)
            gfn_ref[...] = jnp.zeros_like(gfn_ref)
            gdn_ref[...] = jnp.zeros_like(gdn_ref)

        y_ref[:, :D_HALF] = yp_ref[...].astype(BF16)
        dnw = dnw_ref[...]
        keep = []
        for h in range(N_HEADS):
            ov = _head(o_ref, h)[...]
            zv = _head(z_ref, h)[...]
            ro = lax.rsqrt(jnp.mean(ov * ov, axis=-1, keepdims=True) + EPS)
            ohat = ov * ro
            sg = _sigmoid(zv)
            keep.append((ro, ohat, zv, sg))
            y_ref[:, D_HALF + h * HEAD:D_HALF + (h + 1) * HEAD] = (ohat * dnw * (zv * sg)).astype(BF16)

        hv = x_ref[...] + jnp.dot(y_ref[...], wo_ref[...], preferred_element_type=F32)
        r2 = lax.rsqrt(jnp.mean(hv * hv, axis=-1, keepdims=True) + EPS)
        hhat = hv * r2
        fnw = fnw_ref[...]
        err = hhat * fnw - tg_ref[...]
        loss_ref[...] += 0.5 * jnp.sum(_rowsum(err * err) * (1.0 / D_MODEL), axis=0, keepdims=True)
        dout = err * (1.0 / D_MODEL)
        gfn_ref[...] += _colsum(dout * hhat)
        dhh = dout * fnw
        dh = r2 * (dhh - hhat * jnp.mean(dhh * hhat, axis=-1, keepdims=True))
        dh_ref[...] = dh
        dy = _bdot_nt(dh, wo_ref[...])
        dyp_ref[...] = dy[:, :D_HALF]
        gdn = jnp.zeros((1, HEAD), F32)
        for h in range(N_HEADS):
            ro, ohat, zv, sg = keep[h]
            dyd = dy[:, D_HALF + h * HEAD:D_HALF + (h + 1) * HEAD]
            sz = zv * sg
            dz_ref[:, h * HEAD:(h + 1) * HEAD] = dyd * ohat * dnw * (sg * (1.0 + zv * (1.0 - sg)))
            gdn = gdn + _colsum(dyd * ohat * sz)
            doh = dyd * dnw * sz
            do_ref[:, h * HEAD:(h + 1) * HEAD] = ro * (doh - ohat * jnp.mean(doh * ohat, axis=-1, keepdims=True))
        gdn_ref[...] += gdn

    wide = pl.BlockSpec((t, D_MODEL), lambda i: (i, 0))
    half = pl.BlockSpec((t, D_HALF), lambda i: (i, 0))
    const = lambda shape: pl.BlockSpec(shape, lambda i: (0,) * len(shape))
    return _call(
        body, name="out_fwd_bwd", grid=(s // t,),
        in_specs=[wide, half, half, pl.BlockSpec((t, D_HALF), lambda i: (i, 5)), wide,
                  const((D_MODEL, D_MODEL)), const((1, HEAD)), const((1, D_MODEL))],
        out_specs=[wide, wide, half, half, half, const((1, HEAD)), const((1, D_MODEL)), const((1, HEAD))],
        out_shape=[_sds((s, D_MODEL), BF16), _sds((s, D_MODEL)), _sds((s, D_HALF)), _sds((s, D_HALF)), _sds((s, D_HALF)),
                   _sds((1, HEAD)), _sds((1, D_MODEL)), _sds((1, HEAD))],
        compiler_params=_params("arbitrary"),
    )(x, y_pool, o, proj, target, w_out, dn_norm_w, final_norm_w)


def _tn_matmul(name, a, b, col_block=None):
    s, m = a.shape
    n = b.shape[1]
    tn = min(n, 512)
    tk = 512
    cb = 0 if col_block is None else col_block

    def body(a_ref, b_ref, o_ref):
        @pl.when(pl.program_id(1) == 0)
        def _():
            o_ref[...] = jnp.zeros_like(o_ref)
        o_ref[...] += _bdot_tn(a_ref[...], b_ref[...])

    return _call(
        body, name=name, grid=(n // tn if col_block is None else 1, s // tk),
        in_specs=[pl.BlockSpec((tk, m), lambda j, k: (k, 0)),
                  pl.BlockSpec((tk, tn), lambda j, k: (k, j + cb))],
        out_specs=pl.BlockSpec((m, tn), lambda j, k: (0, j)),
        out_shape=_sds((m, n if col_block is None else tn)),
        compiler_params=_params("arbitrary", "arbitrary"),
    )(a, b)


def _pool_bwd(proj, dyp, pool_w, pool_scale):
    s = proj.shape[0]
    t = POOL_T
    hb = t // HEAD
    last = s // HEAD - 1

    def body(u_ref, z_ref, halo_ref, dy_ref, zn_ref, dyn_ref, pw_ref, ps_ref, du_ref, dz_ref, gpw_ref, gps_ref):
        i = pl.program_id(0)

        @pl.when(i == 0)
        def _():
            gpw_ref[...] = jnp.zeros_like(gpw_ref)
            gps_ref[...] = jnp.zeros_like(gps_ref)

        live = (i > 0).astype(F32)
        more = (i < pl.num_programs(0) - 1).astype(F32)
        for g, w in enumerate(WINDOWS):
            sl = slice(g * HEAD, (g + 1) * HEAD)
            zg = z_ref[:, sl]
            ps = ps_ref[:, sl]
            pw = pw_ref[g]
            mix, mixed, sg, cnt = _pool_mix(u_ref[:, sl], halo_ref[:, sl] * live, zg, pw, ps, i * t, w)
            dyg = dy_ref[:, sl]
            sz = zg * sg
            dz_ref[:, sl] = dyg * mixed * ps * (sg * (1.0 + zg * (1.0 - sg)))
            gps_ref[:, sl] += _colsum(dyg * mixed * sz)
            dmixed = dyg * ps * sz
            gpw_ref[g] += _bdot_tn(mix, dmixed)
            dmix = _bdot_nt(dmixed, pw)
            zn = zn_ref[:, sl]
            dmix_n = _bdot_nt(dyn_ref[:, sl] * more * ps * (zn * _sigmoid(zn)), pw)
            du_ref[:, sl] = (_mask_dot(_band(t, t, 0, w, anti=True), dmix / cnt)
                             + _mask_dot(_band(t, HEAD, t, w, anti=True), dmix_n * (1.0 / w)) - dmix)

    tile = lambda col: pl.BlockSpec((t, D_HALF), lambda i: (i, col))
    below = lambda col: pl.BlockSpec((HEAD, D_HALF), lambda i: (jnp.minimum((i + 1) * hb, last), col))
    return _call(
        body, name="pool_bwd", grid=(s // t,),
        in_specs=[tile(0), tile(1), pl.BlockSpec((HEAD, D_HALF), lambda i: (jnp.maximum(i * hb - 1, 0), 0)),
                  tile(0), below(1), below(0),
                  pl.BlockSpec((N_HEADS, HEAD, HEAD), lambda i: (0, 0, 0)), pl.BlockSpec((1, D_HALF), lambda i: (0, 0))],
        out_specs=[tile(0), tile(0), pl.BlockSpec((N_HEADS, HEAD, HEAD), lambda i: (0, 0, 0)),
                   pl.BlockSpec((1, D_HALF), lambda i: (0, 0))],
        out_shape=[_sds((s, D_HALF)), _sds((s, D_HALF)), _sds((N_HEADS, HEAD, HEAD)), _sds((1, D_HALF))],
        compiler_params=_params("arbitrary"),
    )(proj, proj, proj, dyp, proj, dyp, pool_w, pool_scale)


def _scan_bwd(do, vn, qd, kd, w, att, cd, st):
    s = do.shape[0]
    n_pairs = s // PAIR

    def body(do_ref, vn_ref, qd_ref, kd_ref, w_ref, att_ref, cd_ref, st_ref,
             du_ref, dw_ref, datt_ref, dqd_ref, dkd_ref, dcd_ref, dstate):
        @pl.when(pl.program_id(0) == 0)
        def _():
            dstate[...] = jnp.zeros_like(dstate)
        _, incl, _, _ = _pair_masks()
        for h in range(N_HEADS):
            sl = slice(h * HEAD, (h + 1) * HEAD)
            dv_intra = _bdot_tn(att_ref[:, sl], do_ref[:, sl])
            datt_ref[:, sl] = jnp.where(incl, _bdot_nt(do_ref[:, sl], vn_ref[:, sl]), 0.0)
            ds = dstate[h]
            for ci in (1, 0):
                rs = slice(ci * CHUNK, (ci + 1) * CHUNK)
                sm = st_ref[ci, h]
                dov = do_ref[rs, sl]
                dvn = dv_intra[rs] + _bdot(kd_ref[rs, sl], ds)
                du_ref[rs, sl] = dvn
                dqd_ref[rs, sl] = _bdot_nt(dov, sm)
                dw_ref[rs, sl] = -_bdot_nt(dvn, sm)
                dkd_ref[rs, sl] = _bdot_nt(vn_ref[rs, sl], ds)
                dcd_ref[rs, sl] = jnp.broadcast_to(_rowsum(_colsum(ds * sm)), (CHUNK, HEAD))
                ds = (ds * cd_ref[ci * CHUNK:ci * CHUNK + 1, sl] + _bdot_tn(qd_ref[rs, sl], dov)
                      - _bdot_tn(w_ref[rs, sl], dvn))
            dstate[h] = ds

    rev = pl.BlockSpec((PAIR, D_HALF), lambda i: (n_pairs - 1 - i, 0))
    return _call(
        body, name="scan_bwd", grid=(n_pairs,),
        in_specs=[rev] * 7 + [pl.BlockSpec((2, N_HEADS, HEAD, HEAD), lambda i: (n_pairs - 1 - i, 0, 0, 0))],
        out_specs=[rev] * 6,
        out_shape=[_sds((s, D_HALF))] * 6,
        scratch_shapes=[pltpu.VMEM((N_HEADS, HEAD, HEAD), F32)],
        compiler_params=_params("arbitrary"),
    )(do, vn, qd, kd, w, att, cd, st)


def _intra_bwd(qn, kn, vs, beta, g, tm, du, dw, datt, dqd, dkd, dcd):
    s = qn.shape[0]

    def body(qn_ref, kn_ref, vs_ref, beta_ref, g_ref, t_ref, du_ref, dw_ref, datt_ref, dqd_ref, dkd_ref, dcd_ref,
             dqn_ref, dkn_ref, dvs_ref, dbeta_ref, dg_ref):
        ones = jnp.ones((PAIR, HEAD), BF16)
        tn = (((0,), (0,)), ((), ()))
        nt = (((1,), (1,)), ((), ()))
        for h in range(N_HEADS):
            sl = slice(h * HEAD, (h + 1) * HEAD)
            kn, vs, beta = kn_ref[:, sl], vs_ref[:, sl], beta_ref[:, sl]
            cm = _pair_common(qn_ref[:, sl], kn, vs, beta, g_ref[:, sl])
            tmv, duv, dwv, dattv, dqdv, dkdv = t_ref[:, sl], du_ref[:, sl], dw_ref[:, sl], datt_ref[:, sl], dqd_ref[:, sl], dkd_ref[:, sl]
            dvb = _bdot_tn(tmv, duv)
            dt = _bdot_nt(duv, cm["vb"]) + _bdot_nt(dwv, cm["kbg"])
            dkbg = _bdot_tn(tmv, dwv)
            da = -jnp.where(cm["strict"], _dot3(_dot3(tmv, dt, tn), tmv, nt), 0.0)
            dkk = da * cm["decay"]
            dqk = dattv * cm["decay"]
            dd = dkk * cm["kk"] + dqk * cm["qk"]
            dkb = _bdot(dkk, kn) + dkbg * cm["egc"]
            dq = _bdot(dqk, kn) + dqdv * cm["egc"]
            dkn = _bdot_tn(dkk, cm["kb"]) + _bdot_tn(dqk, cm["q"]) + dkdv * cm["ekd"] + dkb * beta
            t_kd = _rowsum(dkdv * kn * cm["ekd"])
            ddh, ddl = _split(dd)
            rows_dd = jnp.dot(ddh, ones, preferred_element_type=F32) + jnp.dot(ddl, ones, preferred_element_type=F32)
            cols_dd = (lax.dot_general(ddh, ones, tn, preferred_element_type=F32)
                       + lax.dot_general(ddl, ones, tn, preferred_element_type=F32))
            dgc = rows_dd - cols_dd + _rowsum(dqdv * cm["q"] * cm["egc"]) + _rowsum(dkbg * cm["kbg"]) - t_kd
            dgl = _mask_dot(cm["same"].astype(BF16), jnp.broadcast_to(t_kd, (PAIR, HEAD))) + dcd_ref[:, sl] * cm["cd"]
            rowi = lax.broadcasted_iota(I32, (PAIR, HEAD), 0)
            dgc = dgc + jnp.where(jnp.bitwise_and(rowi, CHUNK - 1) == CHUNK - 1, dgl, 0.0)
            r = lax.broadcasted_iota(I32, (PAIR, PAIR), 0)
            c = lax.broadcasted_iota(I32, (PAIR, PAIR), 1)
            dg_ref[:, sl] = _mask_dot((cm["same"] & (r <= c)).astype(BF16), dgc)
            dbeta_ref[:, sl] = jnp.broadcast_to(_rowsum(dkb * kn) + _rowsum(dvb * vs), (PAIR, HEAD))
            dqn_ref[:, sl] = dq * QK_SCALE
            dkn_ref[:, sl] = dkn
            dvs_ref[:, sl] = dvb * beta

    return _call(
        body, name="intra_bwd", grid=(s // PAIR,),
        in_specs=[_pair_spec()] * 12, out_specs=[_pair_spec()] * 5,
        out_shape=[_sds((s, D_HALF))] * 5,
        compiler_params=_params("arbitrary"),
    )(qn, kn, vs, beta, g, tm, du, dw, datt, dqd, dkd, dcd)


def _conv_bwd_pre(proj, conv_w, a_log, dt_bias, dqn, dkn, dvs, dbeta, dg):
    s = proj.shape[0]
    t = CONV_T

    def body(q_ref, k_ref, v_ref, hq_ref, hk_ref, hv_ref, ba_ref, cw_ref, al_ref, dtb_ref,
             dqn_ref, dkn_ref, dvs_ref, dbeta_ref, dg_ref, dyq_ref, dyk_ref, dyv_ref, dba_ref, gcw_ref, gsm_ref):
        @pl.when(pl.program_id(0) == 0)
        def _():
            gcw_ref[...] = jnp.zeros_like(gcw_ref)
            gsm_ref[...] = jnp.zeros_like(gsm_ref)

        live = (pl.program_id(0) > 0).astype(F32)
        parts = ((q_ref, hq_ref, dqn_ref, dyq_ref), (k_ref, hk_ref, dkn_ref, dyk_ref), (v_ref, hv_ref, dvs_ref, dyv_ref))
        for p, (x_ref, h_ref, d_ref, dy_ref) in enumerate(parts):
            cols = slice(p * D_HALF, (p + 1) * D_HALF)
            xv = x_ref[...]
            prev = h_ref[...] * live
            y = _conv_pre(xv, prev, cw_ref[:, cols])
            sg = _sigmoid(y)
            sv = y * sg
            if p == 2:
                ds = d_ref[...]
            else:
                segs = []
                for h in range(N_HEADS):
                    seg = _head(sv, h)
                    rn = lax.rsqrt(_rowsum(seg * seg) + EPS)
                    nrm = seg * rn
                    dn = _head(d_ref, h)[...]
                    segs.append(rn * (dn - nrm * _rowsum(dn * nrm)))
                ds = jnp.concatenate(segs, axis=1)
            dy = ds * (sg * (1.0 + y * (1.0 - sg)))
            dy_ref[...] = dy
            gcw_ref[3:4, cols] += _colsum(dy * xv)
            for sft in (1, 2, 3):
                gcw_ref[3 - sft:4 - sft, cols] += _colsum(dy * _shift_down(xv, prev, sft))

        ba = ba_ref[...]
        lane = lax.broadcasted_iota(I32, (t, HEAD), 1)
        lane1 = lax.broadcasted_iota(I32, (1, HEAD), 1)
        dba = jnp.zeros((t, HEAD), F32)
        gsm = jnp.zeros((1, HEAD), F32)
        for h in range(N_HEADS):
            beta = _sigmoid(ba[:, h:h + 1])
            dbeta = dbeta_ref[:, h * HEAD:h * HEAD + 1]
            xg = ba[:, N_HEADS + h:N_HEADS + h + 1] + dtb_ref[0:1, h:h + 1]
            nexp = -jnp.exp(al_ref[0:1, h:h + 1])
            dgv = dg_ref[:, h * HEAD:h * HEAD + 1]
            da = dgv * nexp * _sigmoid(xg)
            dba = dba + jnp.where(lane == h, dbeta * beta * (1.0 - beta), 0.0) + jnp.where(lane == N_HEADS + h, da, 0.0)
            gsm = (gsm + jnp.where(lane1 == h, _colsum(dgv * nexp * _softplus(xg)), 0.0)
                   + jnp.where(lane1 == N_HEADS + h, _colsum(da), 0.0))
        dba_ref[...] = dba
        gsm_ref[0:1, :] += gsm

    row = pl.BlockSpec((t, D_HALF), lambda i: (i, 0))
    return _call(
        body, name="conv_bwd_pre", grid=(s // t,),
        in_specs=_conv_specs(t) + [pl.BlockSpec((t, HEAD), lambda i: (i, COL_BA // HEAD)),
                                   pl.BlockSpec((CONV_K, 3 * D_HALF), lambda i: (0, 0)),
                                   pl.BlockSpec((1, N_HEADS), lambda i: (0, 0)),
                                   pl.BlockSpec((1, N_HEADS), lambda i: (0, 0))] + [row] * 5,
        out_specs=[row, row, row, pl.BlockSpec((t, HEAD), lambda i: (i, 0)),
                   pl.BlockSpec((8, 3 * D_HALF), lambda i: (0, 0)), pl.BlockSpec((8, HEAD), lambda i: (0, 0))],
        out_shape=[_sds((s, D_HALF))] * 3 + [_sds((s, HEAD)), _sds((8, 3 * D_HALF)), _sds((8, HEAD))],
        compiler_params=_params("arbitrary"),
    )(proj, proj, proj, proj, proj, proj, proj, conv_w, a_log, dt_bias, dqn, dkn, dvs, dbeta, dg)


def _conv_bwd_in(dyq, dyk, dyv, conv_w):
    s = dyq.shape[0]
    t = CONV_T
    last = s // 8 - 1

    def body(q_ref, k_ref, v_ref, nq_ref, nk_ref, nv_ref, cw_ref, oq_ref, ok_ref, ov_ref):
        more = (pl.program_id(0) < pl.num_programs(0) - 1).astype(F32)
        for p, (d_ref, n_ref, o_ref) in enumerate(((q_ref, nq_ref, oq_ref), (k_ref, nk_ref, ok_ref), (v_ref, nv_ref, ov_ref))):
            cw = cw_ref[:, p * D_HALF:(p + 1) * D_HALF]
            dy = d_ref[...]
            nxt = n_ref[...] * more
            acc = dy * cw[3:4]
            for sft in (1, 2, 3):
                acc = acc + _shift_up(dy, nxt, sft) * cw[3 - sft:4 - sft]
            o_ref[...] = acc

    row = pl.BlockSpec((t, D_HALF), lambda i: (i, 0))
    nxt = pl.BlockSpec((8, D_HALF), lambda i: (jnp.minimum((i + 1) * (t // 8), last), 0))
    return _call(
        body, name="conv_bwd_in", grid=(s // t,),
        in_specs=[row] * 3 + [nxt] * 3 + [pl.BlockSpec((CONV_K, 3 * D_HALF), lambda i: (0, 0))],
        out_specs=[row] * 3, out_shape=[_sds((s, D_HALF))] * 3,
        compiler_params=_params("arbitrary"),
    )(dyq, dyk, dyv, dyq, dyk, dyv, conv_w)


IN_T = 256


def _in_bwd(x, dh, norm_w, w_pad, pieces):
    s = x.shape[0]
    t = IN_T
    widths = [p.shape[1] for p in pieces]

    def body(*refs):
        x_ref, dh_ref, nw_ref, w_ref = refs[:4]
        p_refs = refs[4:4 + len(pieces)]
        gx_ref, gnw_ref = refs[4 + len(pieces):]

        @pl.when(pl.program_id(0) == 0)
        def _():
            gnw_ref[...] = jnp.zeros_like(gnw_ref)

        dn = jnp.zeros((t, D_MODEL), F32)
        col = 0
        for p_ref, wd in zip(p_refs, widths):
            dn = dn + _bdot_nt(p_ref[...], w_ref[:, col:col + wd])
            col += wd
        xv = x_ref[...]
        r = lax.rsqrt(jnp.mean(xv * xv, axis=-1, keepdims=True) + EPS)
        xhat = xv * r
        gnw_ref[...] += _colsum(dn * xhat)
        dxh = dn * nw_ref[...]
        gx_ref[...] = dh_ref[...] + r * (dxh - xhat * jnp.mean(dxh * xhat, axis=-1, keepdims=True))

    wide = pl.BlockSpec((t, D_MODEL), lambda i: (i, 0))
    return _call(
        body, name="in_bwd", grid=(s // t,),
        in_specs=[wide, wide, pl.BlockSpec((1, D_MODEL), lambda i: (0, 0)),
                  pl.BlockSpec((D_MODEL, N_IN_PAD), lambda i: (0, 0))]
                 + [pl.BlockSpec((t, wd), lambda i: (i, 0)) for wd in widths],
        out_specs=[wide, pl.BlockSpec((1, D_MODEL), lambda i: (0, 0))],
        out_shape=[_sds((s, D_MODEL)), _sds((1, D_MODEL))],
        compiler_params=_params("arbitrary"),
    )(x, dh, norm_w, w_pad, *pieces)


def _adamw(name, w, g, m, v, rows):
    r, c = w.shape

    def body(w_ref, g_ref, m_ref, v_ref, d_ref, nm_ref, nv_ref):
        gv = g_ref[...]
        mn = ADAM_B1 * m_ref[...] + (1.0 - ADAM_B1) * gv
        vn = ADAM_B2 * v_ref[...] + (1.0 - ADAM_B2) * (gv * gv)
        m_hat = mn / (1.0 - ADAM_B1 ** ADAM_STEP)
        v_hat = vn / (1.0 - ADAM_B2 ** ADAM_STEP)
        d_ref[...] = -ADAM_LR * (m_hat / (jnp.sqrt(v_hat) + ADAM_EPS) + ADAM_WD * w_ref[...])
        nm_ref[...] = mn
        nv_ref[...] = vn

    blk = pl.BlockSpec((rows, c), lambda i: (i, 0))
    return _call(
        body, name=name, grid=(r // rows,),
        in_specs=[blk] * 4, out_specs=[blk] * 3, out_shape=[_sds((r, c))] * 3,
        compiler_params=_params("arbitrary"),
    )(w, g, m, v)


def _adamw_shard(name, w, g, m, v, rows):
    _, r, c = w.shape

    def body(w_ref, g_ref, m_ref, v_ref, go_ref, d_ref, nm_ref, nv_ref):
        gv = g_ref[:, :c]
        go_ref[0] = gv
        mn = ADAM_B1 * m_ref[0] + (1.0 - ADAM_B1) * gv
        vn = ADAM_B2 * v_ref[0] + (1.0 - ADAM_B2) * (gv * gv)
        m_hat = mn / (1.0 - ADAM_B1 ** ADAM_STEP)
        v_hat = vn / (1.0 - ADAM_B2 ** ADAM_STEP)
        d_ref[0] = -ADAM_LR * (m_hat / (jnp.sqrt(v_hat) + ADAM_EPS) + ADAM_WD * w_ref[0])
        nm_ref[0] = mn
        nv_ref[0] = vn

    blk = pl.BlockSpec((1, rows, c), lambda i: (0, i, 0))
    gblk = pl.BlockSpec((rows, g.shape[1]), lambda i: (i, 0))
    return _call(
        body, name=name, grid=(r // rows,),
        in_specs=[blk, gblk, blk, blk], out_specs=[blk] * 4, out_shape=[_sds((1, r, c))] * 4,
        compiler_params=_params("arbitrary"),
    )(w, g, m, v)


def _exchange(name, inputs, out_shapes, phases, local_copies=()):
    n_in = len(inputs)
    n_out = len(out_shapes)
    n_cp = sum(len(p) for p in phases)

    def body(*refs):
        ins, outs = refs[:n_in], refs[n_in:n_in + n_out]
        send, recv, lsem = refs[n_in + n_out:]
        pos = (lax.axis_index("x"), lax.axis_index("y"), lax.axis_index("c"))
        locs = [pltpu.make_async_copy(src(ins, outs, pos), dst(ins, outs, pos), lsem.at[i])
                for i, (src, dst) in enumerate(local_copies)]
        for lc in locs:
            lc.start()
        k = 0
        for phase in phases:
            cps = []
            for src, dst, target in phase:
                cps.append(pltpu.make_async_remote_copy(
                    src_ref=src(ins, outs, pos), dst_ref=dst(ins, outs, pos), send_sem=send.at[k], recv_sem=recv.at[k],
                    device_id=target(pos), device_id_type=pl.DeviceIdType.MESH))
                k += 1
            for cp in cps:
                cp.start()
            for cp in cps:
                cp.wait()
        for lc in locs:
            lc.wait()

    anyspec = pl.BlockSpec(memory_space=pl.ANY)
    return _call(
        body, name=name,
        in_specs=[anyspec] * n_in, out_specs=[anyspec] * n_out, out_shape=list(out_shapes),
        scratch_shapes=[pltpu.SemaphoreType.DMA((n_cp,)), pltpu.SemaphoreType.DMA((n_cp,)),
                        pltpu.SemaphoreType.DMA((max(len(local_copies), 1),))],
    )(*inputs)


def _chip(pos):
    return 2 * pos[0] + pos[1]


def _other_chip(pos, mask):
    x, y, c = pos
    return (x ^ (mask >> 1), y ^ (mask & 1), c)


def _sibling(pos):
    return (pos[0], pos[1], 1 - pos[2])


def _gather_weights(wb, ob, cb):
    halves = (wb.shape[0] // 2, ob.shape[0] // 2)

    def half(a, pos):
        return pl.ds(pos[2] * halves[a], halves[a])

    first, second = [], []
    for mask in CHIP_MASKS:
        for a in (0, 1):
            first.append((lambda ins, outs, pos, a=a: ins[a].at[half(a, pos)],
                          lambda ins, outs, pos, a=a: outs[a].at[_chip(pos), half(a, pos)],
                          functools.partial(_other_chip, mask=mask)))
            second.append((lambda ins, outs, pos, a=a, mask=mask: outs[a].at[_chip(pos) ^ mask, half(a, pos)],
                           lambda ins, outs, pos, a=a, mask=mask: outs[a].at[_chip(pos) ^ mask, half(a, pos)],
                           _sibling))
        first.append((lambda ins, outs, pos: ins[2],
                      lambda ins, outs, pos: outs[2].at[_chip(pos)],
                      functools.partial(_other_chip, mask=mask)))
    local = [(lambda ins, outs, pos, a=a: ins[a], lambda ins, outs, pos, a=a: outs[a].at[_chip(pos)]) for a in range(3)]
    return _exchange("gather_weights", [wb, ob, cb],
                     [_sds((4,) + wb.shape, wb.dtype), _sds((4,) + ob.shape, ob.dtype), _sds((4,) + cb.shape, cb.dtype)],
                     [first, second], local)


def _to_sibling_half(name, arrays):
    def src(ins, outs, pos, a):
        h = arrays[a].shape[-2] // 2
        sl = pl.ds((1 - pos[2]) * h, h)
        return ins[a].at[:, sl] if arrays[a].ndim == 3 else ins[a].at[sl]

    outs = [_sds(a.shape[:-2] + (a.shape[-2] // 2, a.shape[-1]), a.dtype) for a in arrays]
    phase = [(functools.partial(src, a=a), lambda ins, outs, pos, a=a: outs[a], _sibling) for a in range(len(arrays))]
    return _exchange(name, arrays, outs, [phase])


def _add_half(name, full, part, cidx):
    shape = part.shape
    lead = shape[0] if len(shape) == 3 else 1
    rows, cols = shape[-2], shape[-1]
    tr = rows // 2 if rows % 16 == 0 else rows
    nr = rows // tr
    f3 = full.reshape((lead,) + full.shape[-2:])
    p3 = part.reshape((lead, rows, cols))

    def body(c_ref, f_ref, p_ref, o_ref):
        o_ref[...] = (f_ref[...].astype(F32) + p_ref[...].astype(F32)).astype(o_ref.dtype)

    out = _call(
        body, name=name,
        grid_spec=pltpu.PrefetchScalarGridSpec(
            num_scalar_prefetch=1, grid=(lead, nr),
            in_specs=[pl.BlockSpec((1, tr, cols), lambda b, r, c_ref: (b, c_ref[0] * nr + r, 0)),
                      pl.BlockSpec((1, tr, cols), lambda b, r, c_ref: (b, r, 0))],
            out_specs=pl.BlockSpec((1, tr, cols), lambda b, r, c_ref: (b, r, 0))),
        out_shape=_sds((lead, rows, cols), part.dtype),
        compiler_params=_params("arbitrary", "arbitrary"),
    )(cidx, f3, p3)
    return out.reshape(shape)


def _to_other_chips(name, arrays, blocked):
    def src(ins, outs, pos, a, mask):
        return ins[a].at[_chip(pos) ^ mask] if blocked[a] else ins[a]

    outs = [_sds((3,) + (a.shape[1:] if b else a.shape), a.dtype) for a, b in zip(arrays, blocked)]
    phase = []
    for mi, mask in enumerate(CHIP_MASKS):
        for a in range(len(arrays)):
            phase.append((functools.partial(src, a=a, mask=mask), lambda ins, outs, pos, a=a, mi=mi: outs[a].at[mi],
                          functools.partial(_other_chip, mask=mask)))
    return _exchange(name, arrays, outs, [phase])


def _add_chips(name, own, got, jidx, blocked):
    rows, cols = got.shape[-2:]
    tr = rows // 2 if rows % 16 == 0 else rows
    nr = rows // tr
    o3 = own if blocked else own.reshape((1, rows, cols))

    def body(j_ref, o_ref, g_ref, out_ref):
        out_ref[...] = ((o_ref[0].astype(F32) + g_ref[0].astype(F32))
                        + (g_ref[1].astype(F32) + g_ref[2].astype(F32)))

    own_map = (lambda r, j_ref: (j_ref[0], r, 0)) if blocked else (lambda r, j_ref: (0, r, 0))
    return _call(
        body, name=name,
        grid_spec=pltpu.PrefetchScalarGridSpec(
            num_scalar_prefetch=1, grid=(nr,),
            in_specs=[pl.BlockSpec((1, tr, cols), own_map),
                      pl.BlockSpec((3, tr, cols), lambda r, j_ref: (0, r, 0))],
            out_specs=pl.BlockSpec((tr, cols), lambda r, j_ref: (r, 0))),
        out_shape=_sds((rows, cols)),
        compiler_params=_params("arbitrary"),
    )(jidx, o3, got)


def _join_halves(name, arrays):
    def where(outs, pos, a):
        h = arrays[a].shape[0]
        return outs[a].at[pl.ds(pos[2] * h, h)]

    outs = [_sds((2 * a.shape[0], a.shape[1]), a.dtype) for a in arrays]
    phase = [(lambda ins, outs, pos, a=a: ins[a], lambda ins, outs, pos, a=a: where(outs, pos, a), _sibling)
             for a in range(len(arrays))]
    local = [(lambda ins, outs, pos, a=a: ins[a], lambda ins, outs, pos, a=a: where(outs, pos, a))
             for a in range(len(arrays))]
    return _exchange(name, arrays, outs, [phase], local)


def _local_step(x, target, w_pad, w_out, conv_w, norm_w, pool_w, pool_scale, a_log, dt_bias, dn_norm_w, final_norm_w):
    proj, nbf = _proj_fwd(x, norm_w, w_pad)
    y_pool = _pool_fwd(proj, pool_w, pool_scale)
    qn, kn, vs, beta, g = _conv_fwd(proj, conv_w, a_log, dt_bias)
    u, w, att, qd, kd, tm, cd = _intra_fwd(qn, kn, vs, beta, g)
    o, vn, st = _scan_fwd(u, w, att, qd, kd, cd)
    y, dh, dyp, do, ddz, loss, g_fnw, g_dnw = _out_fwd_bwd(x, y_pool, o, proj, target, w_out, dn_norm_w, final_norm_w)
    g_wout = _tn_matmul("grad_w_out", y, dh)
    dpu, dpz, g_pw, g_ps = _pool_bwd(proj, dyp, pool_w, pool_scale)
    du, dw, datt, dqd, dkd, dcd = _scan_bwd(do, vn, qd, kd, w, att, cd, st)
    dqn, dkn, dvs, dbeta, dg = _intra_bwd(qn, kn, vs, beta, g, tm, du, dw, datt, dqd, dkd, dcd)
    dyq, dyk, dyv, dba, g_cw, g_sm = _conv_bwd_pre(proj, conv_w, a_log, dt_bias, dqn, dkn, dvs, dbeta, dg)
    dcq, dck, dcv = _conv_bwd_in(dyq, dyk, dyv, conv_w)
    pieces = [dpu, dpz, dcq, dck, dcv, ddz, dba]
    gx, g_nw = _in_bwd(x, dh, norm_w, w_pad, pieces)
    g_win = jnp.concatenate([_tn_matmul("grad_w_in_%d" % i, nbf, p) for i, p in enumerate(pieces)], axis=1)
    small = dict(norm_w=g_nw, pool_w=g_pw, pool_scale=g_ps, conv_w=g_cw[:CONV_K], a_log=g_sm[0:1, 0:N_HEADS],
                 dt_bias=g_sm[0:1, N_HEADS:2 * N_HEADS], dn_norm_w=g_dnw, final_norm_w=g_fnw)
    return loss[0, 0], gx, g_win, g_wout, small


def _pack_small(t):
    lanes = lambda a: jnp.pad(a.reshape(1, -1), ((0, 0), (0, HEAD - a.size)))
    rows = [t["pool_w"].reshape(-1, HEAD), t["norm_w"].reshape(-1, HEAD), t["final_norm_w"].reshape(-1, HEAD),
            t["pool_scale"].reshape(-1, HEAD), t["conv_w"].reshape(-1, HEAD), t["dn_norm_w"].reshape(1, HEAD),
            lanes(t["a_log"]), lanes(t["dt_bias"]), lanes(t.get("loss", jnp.zeros((1,), F32)))]
    buf = jnp.concatenate(rows, axis=0)
    return jnp.pad(buf, ((0, SMALL_ROWS - buf.shape[0]), (0, 0)))


def _unpack_small(buf, conv_cols):
    out, r = {}, 0
    for name, nrow, shape in (("pool_w", 512, (1, N_HEADS, HEAD, HEAD)), ("norm_w", 8, (1, D_MODEL)),
                              ("final_norm_w", 8, (D_MODEL,)), ("pool_scale", 4, (1, D_HALF)),
                              ("conv_w", CONV_K * conv_cols // HEAD, (1, CONV_K, conv_cols)), ("dn_norm_w", 1, (1, HEAD))):
        out[name] = buf[r:r + nrow].reshape(shape)
        r += nrow
    out["a_log"] = buf[r:r + 1, :N_HEADS]
    out["dt_bias"] = buf[r + 1:r + 2, :N_HEADS]
    out["loss"] = buf[r + 2, 0]
    return out


def kernel(x, norm_w, w_in, pool_w, pool_scale, conv_w, a_log, dt_bias, dn_norm_w, w_out, final_norm_w, loss_target, m_norm_w, m_w_in, m_pool_w, m_pool_scale, m_conv_w, m_a_log, m_dt_bias, m_dn_norm_w, m_w_out, m_final_norm_w, v_norm_w, v_w_in, v_pool_w, v_pool_scale, v_conv_w, v_a_log, v_dt_bias, v_dn_norm_w, v_w_out, v_final_norm_w):
    cidx = lax.axis_index("c").astype(I32).reshape(1)
    jidx = (2 * lax.axis_index("x") + lax.axis_index("y")).astype(I32)

    wb = jnp.pad(w_in[0].astype(BF16), ((0, 0), (0, BLK_IN_PAD - BLK_IN)))
    gw, go, gc = _gather_weights(wb, w_out[0].astype(BF16), conv_w[0])
    w_pad = jnp.concatenate([gw[j, :, :BLK_IN] for j in range(4)] + [jnp.zeros((D_MODEL, N_IN_PAD - N_IN), BF16)], axis=1)
    wo_full = go.reshape(D_MODEL, D_MODEL)
    cw_full = jnp.concatenate([gc[j] for j in range(4)], axis=1)

    loss, gx, g_win, g_wout, small = _local_step(
        x[0], loss_target[0], w_pad, wo_full, cw_full, norm_w, pool_w[0], pool_scale, a_log, dt_bias, dn_norm_w,
        final_norm_w.reshape(1, D_MODEL))
    small["loss"] = loss

    blocks_in = jnp.stack([jnp.pad(g_win[:, j * BLK_IN:(j + 1) * BLK_IN].astype(BF16), ((0, 0), (0, BLK_IN_PAD - BLK_IN)))
                           for j in range(4)])
    blocks_out = g_wout.astype(BF16).reshape(4, BLK_OUT, D_MODEL)
    full = [blocks_in, blocks_out, _pack_small(small)]
    from_sib = _to_sibling_half("reduce_sibling", full)
    chip_sum = [_add_half("add_sibling_%d" % i, f, p, cidx) for i, (f, p) in enumerate(zip(full, from_sib))]
    blocked = [True, True, False]
    from_chips = _to_other_chips("reduce_chips", chip_sum, blocked)
    halves = [_add_chips("add_chips_%d" % i, o, g, jidx.reshape(1), b)
              for i, (o, g, b) in enumerate(zip(chip_sum, from_chips, blocked))]
    g_in_blk, g_out_blk, g_small = _join_halves("join_halves", halves)

    grads = _unpack_small(g_small, 3 * D_HALF)
    total = grads.pop("loss")
    grads["conv_w"] = lax.dynamic_slice_in_dim(grads["conv_w"], jidx * BLK_CONV, BLK_CONV, axis=2)

    weights = dict(norm_w=norm_w, w_in=w_in, pool_w=pool_w, pool_scale=pool_scale, conv_w=conv_w, a_log=a_log,
                   dt_bias=dt_bias, dn_norm_w=dn_norm_w, w_out=w_out, final_norm_w=final_norm_w)
    ms = dict(norm_w=m_norm_w, w_in=m_w_in, pool_w=m_pool_w, pool_scale=m_pool_scale, conv_w=m_conv_w, a_log=m_a_log,
              dt_bias=m_dt_bias, dn_norm_w=m_dn_norm_w, w_out=m_w_out, final_norm_w=m_final_norm_w)
    vs = dict(norm_w=v_norm_w, w_in=v_w_in, pool_w=v_pool_w, pool_scale=v_pool_scale, conv_w=v_conv_w, a_log=v_a_log,
              dt_bias=v_dt_bias, dn_norm_w=v_dn_norm_w, w_out=v_w_out, final_norm_w=v_final_norm_w)
    names = ["norm_w", "w_in", "pool_w", "pool_scale", "conv_w", "a_log", "dt_bias", "dn_norm_w", "w_out", "final_norm_w"]

    delta, new_m, new_v = {}, {}, {}
    for name, gblk in (("w_in", g_in_blk), ("w_out", g_out_blk)):
        grads[name], delta[name], new_m[name], new_v[name] = _adamw_shard(
            "adamw_" + name, weights[name], gblk, ms[name], vs[name], 256)
    small_names = [n for n in names if n not in ("w_in", "w_out")]
    pack = lambda t: _pack_small({**{n: t[n] for n in small_names if n != "conv_w"},
                                  "conv_w": jnp.pad(t["conv_w"][0], ((0, 0), (0, 3 * D_HALF - BLK_CONV)))})
    d, nm, nv = _adamw("adamw_small", pack(weights), pack(grads), pack(ms), pack(vs), SMALL_ROWS)
    for res, buf in ((delta, d), (new_m, nm), (new_v, nv)):
        got = _unpack_small(buf, 3 * D_HALF)
        got.pop("loss")
        got["conv_w"] = got["conv_w"][:, :, :BLK_CONV]
        res.update(got)

    return (total, gx[None], *[grads[n] for n in names], *[delta[n] for n in names],
            *[new_m[n] for n in names], *[new_v[n] for n in names])
```

```python
import functools

import jax
import jax.numpy as jnp
from jax import lax
from jax.experimental import pallas as pl
from jax.experimental.pallas import tpu as pltpu

F32 = jnp.float32
BF16 = jnp.bfloat16
I32 = jnp.int32

D_MODEL = 1024
D_HALF = 512
N_HEADS = 4
HEAD = 128
CHUNK = 64
PAIR = 2 * CHUNK
WINDOWS = (2, 4, 8, 16)
CONV_K = 4
EPS = 1e-6
N_IN = 3080
N_IN_PAD = 3200
BLK_IN = 770
BLK_IN_PAD = 896
BLK_OUT = 256
BLK_CONV = 384
COL_BA = 3072
QK_SCALE = HEAD ** -0.5
SMALL_ROWS = 592
VMEM_LIMIT = 56 * 1024 * 1024

ADAM_LR = 0.001
ADAM_B1 = 0.9
ADAM_B2 = 0.999
ADAM_EPS = 1e-08
ADAM_WD = 0.01
ADAM_STEP = 10

CHIP_MASKS = (2, 1, 3)
HEADS = range(N_HEADS)
HEAD_COLS = [slice(h * HEAD, (h + 1) * HEAD) for h in HEADS]


def _call(body, **kw):
    return pl.pallas_call(body, **kw)


def _params(*sem):
    return pltpu.CompilerParams(dimension_semantics=sem, vmem_limit_bytes=VMEM_LIMIT)


def _sds(shape, dtype=F32):
    return jax.ShapeDtypeStruct(shape, dtype)


def _bdot(a, b):
    return jnp.dot(a.astype(BF16), b.astype(BF16), preferred_element_type=F32)


def _bdot_nt(a, b):
    return lax.dot_general(a.astype(BF16), b.astype(BF16), (((1,), (1,)), ((), ())), preferred_element_type=F32)


def _bdot_tn(a, b):
    return lax.dot_general(a.astype(BF16), b.astype(BF16), (((0,), (0,)), ((), ())), preferred_element_type=F32)


def _split(a):
    hi = a.astype(BF16)
    lo = (a - hi.astype(F32)).astype(BF16)
    return hi, lo


def _mask_dot(m, b, dims=(((1,), (0,)), ((), ()))):
    bh, bl = _split(b)
    dg = functools.partial(lax.dot_general, dimension_numbers=dims, preferred_element_type=F32)
    return dg(m, bh) + dg(m, bl)


def _sigmoid(x):
    return 1.0 / (1.0 + jnp.exp(-x))


def _softplus(x):
    return jnp.maximum(x, 0.0) + jnp.log(1.0 + jnp.exp(-jnp.abs(x)))


def _rowsum(x):
    return jnp.sum(x, axis=-1, keepdims=True)


def _colsum(x):
    return jnp.sum(x, axis=0, keepdims=True)


def _shift_down(xv, prev8, k):
    r = pltpu.roll(xv, k, 0)
    q = pltpu.roll(prev8, k, 0)
    row = lax.broadcasted_iota(I32, prev8.shape, 0)
    top = jnp.where(row < k, q, r[0:8])
    return jnp.concatenate([top, r[8:]], axis=0)


def _shift_up(xv, next8, k):
    t = xv.shape[0]
    r = pltpu.roll(xv, t - k, 0)
    q = pltpu.roll(next8, 8 - k, 0)
    row = lax.broadcasted_iota(I32, next8.shape, 0)
    bot = jnp.where(row >= 8 - k, q, r[t - 8:])
    return jnp.concatenate([r[:t - 8], bot], axis=0)


def _band(rows, cols, off, w, anti=False):
    r = lax.broadcasted_iota(I32, (rows, cols), 0)
    c = lax.broadcasted_iota(I32, (rows, cols), 1)
    d = (c - r + off) if anti else (r - c + off)
    return ((d >= 0) & (d < w)).astype(BF16)


def _head(ref_or_val, h):
    return ref_or_val[:, h * HEAD:(h + 1) * HEAD]


def _heads(ref):
    return [ref[:, sl] for sl in HEAD_COLS]


def _put_heads(ref, vals):
    for sl, v in zip(HEAD_COLS, vals):
        ref[:, sl] = v


def _each(fn, *lists):
    return [fn(*args) for args in zip(*lists)]


def _proj_fwd(x, norm_w, w_pad):
    s = x.shape[0]
    tm = 512

    def body(x_ref, nw_ref, w_ref, proj_ref, nbf_ref):
        xv = x_ref[...]
        r = lax.rsqrt(jnp.mean(xv * xv, axis=-1, keepdims=True) + EPS)
        nbf_ref[...] = (xv * r * nw_ref[...]).astype(BF16)
        proj_ref[...] = jnp.dot(nbf_ref[...], w_ref[...], preferred_element_type=F32)

    return _call(
        body, name="proj_fwd", grid=(s // tm,),
        in_specs=[pl.BlockSpec((tm, D_MODEL), lambda i: (i, 0)),
                  pl.BlockSpec((1, D_MODEL), lambda i: (0, 0)),
                  pl.BlockSpec((D_MODEL, N_IN_PAD), lambda i: (0, 0))],
        out_specs=[pl.BlockSpec((tm, N_IN_PAD), lambda i: (i, 0)),
                   pl.BlockSpec((tm, D_MODEL), lambda i: (i, 0))],
        out_shape=[_sds((s, N_IN_PAD)), _sds((s, D_MODEL), BF16)],
        compiler_params=_params("arbitrary"),
    )(x, norm_w, w_pad)


def _pool_mix(ug, hg, zg, pw_g, row0, w):
    t = ug.shape[0]
    win = _mask_dot(_band(t, t, 0, w), ug) + _mask_dot(_band(t, HEAD, HEAD, w), hg)
    cnt = jnp.minimum(row0 + lax.broadcasted_iota(I32, (t, 1), 0) + 1, w).astype(F32)
    mix = win / cnt - ug
    mixed = _bdot(mix, pw_g)
    sg = _sigmoid(zg)
    return mix, mixed, sg, cnt


POOL_T = 256


def _pool_fwd(proj, pool_w, pool_scale):
    s = proj.shape[0]
    t = POOL_T
    hb = t // HEAD

    def body(u_ref, z_ref, halo_ref, pw_ref, ps_ref, y_ref):
        i = pl.program_id(0)
        live = (i > 0).astype(F32)
        for g, w in enumerate(WINDOWS):
            sl = HEAD_COLS[g]
            zg = z_ref[:, sl]
            _, mixed, sg, _ = _pool_mix(u_ref[:, sl], halo_ref[:, sl] * live, zg, pw_ref[g], i * t, w)
            y_ref[:, sl] = mixed * ps_ref[:, sl] * (zg * sg)

    return _call(
        body, name="pool_fwd", grid=(s // t,),
        in_specs=[pl.BlockSpec((t, D_HALF), lambda i: (i, 0)),
                  pl.BlockSpec((t, D_HALF), lambda i: (i, 1)),
                  pl.BlockSpec((HEAD, D_HALF), lambda i: (jnp.maximum(i * hb - 1, 0), 0)),
                  pl.BlockSpec((N_HEADS, HEAD, HEAD), lambda i: (0, 0, 0)),
                  pl.BlockSpec((1, D_HALF), lambda i: (0, 0))],
        out_specs=pl.BlockSpec((t, D_HALF), lambda i: (i, 0)),
        out_shape=_sds((s, D_HALF)),
        compiler_params=_params("arbitrary"),
    )(proj, proj, proj, pool_w, pool_scale)


def _conv_pre(xv, prev8, cw):
    y = xv * cw[3:4]
    for sft in (1, 2, 3):
        y = y + _shift_down(xv, prev8, sft) * cw[3 - sft:4 - sft]
    return y


CONV_T = 256


def _conv_specs(t):
    tiles = [pl.BlockSpec((t, D_HALF), functools.partial(lambda i, p: (i, 2 + p), p=p)) for p in range(3)]
    halos = [pl.BlockSpec((8, D_HALF), functools.partial(lambda i, p: (jnp.maximum(i * (t // 8) - 1, 0), 2 + p), p=p))
             for p in range(3)]
    return tiles + halos


def _conv_fwd(proj, conv_w, a_log, dt_bias):
    s = proj.shape[0]
    t = CONV_T

    def body(q_ref, k_ref, v_ref, hq_ref, hk_ref, hv_ref, ba_ref, cw_ref, al_ref, dtb_ref,
             qn_ref, kn_ref, vs_ref, beta_ref, g_ref):
        live = (pl.program_id(0) > 0).astype(F32)
        for p, (x_ref, h_ref, o_ref) in enumerate(((q_ref, hq_ref, qn_ref), (k_ref, hk_ref, kn_ref), (v_ref, hv_ref, vs_ref))):
            y = _conv_pre(x_ref[...], h_ref[...] * live, cw_ref[:, p * D_HALF:(p + 1) * D_HALF])
            sv = y * _sigmoid(y)
            if p == 2:
                o_ref[...] = sv
            else:
                for h in HEADS:
                    seg = _head(sv, h)
                    o_ref[:, HEAD_COLS[h]] = seg * lax.rsqrt(_rowsum(seg * seg) + EPS)
        ba = ba_ref[...]
        for h in HEADS:
            beta = _sigmoid(ba[:, h:h + 1])
            gl = -jnp.exp(al_ref[0:1, h:h + 1]) * _softplus(ba[:, N_HEADS + h:N_HEADS + h + 1] + dtb_ref[0:1, h:h + 1])
            beta_ref[:, HEAD_COLS[h]] = jnp.broadcast_to(beta, (t, HEAD))
            g_ref[:, HEAD_COLS[h]] = jnp.broadcast_to(gl, (t, HEAD))

    row = pl.BlockSpec((t, D_HALF), lambda i: (i, 0))
    return _call(
        body, name="conv_fwd", grid=(s // t,),
        in_specs=_conv_specs(t) + [pl.BlockSpec((t, HEAD), lambda i: (i, COL_BA // HEAD)),
                                   pl.BlockSpec((CONV_K, 3 * D_HALF), lambda i: (0, 0)),
                                   pl.BlockSpec((1, N_HEADS), lambda i: (0, 0)),
                                   pl.BlockSpec((1, N_HEADS), lambda i: (0, 0))],
        out_specs=[row] * 5,
        out_shape=[_sds((s, D_HALF))] * 5,
        compiler_params=_params("arbitrary"),
    )(proj, proj, proj, proj, proj, proj, proj, conv_w, a_log, dt_bias)


def _pair_masks():
    r = lax.broadcasted_iota(I32, (PAIR, PAIR), 0)
    c = lax.broadcasted_iota(I32, (PAIR, PAIR), 1)
    same = jnp.right_shift(r, 6) == jnp.right_shift(c, 6)
    return same, same & (r >= c), same & (r > c), r == c


def _pair_common(qn, kn, vs, beta, g):
    same, incl, strict, eye = _pair_masks()
    incl_b = incl.astype(BF16)
    first = lax.broadcasted_iota(I32, (PAIR, HEAD), 0) < CHUNK
    gc = _each(lambda gv: _mask_dot(incl_b, gv), g)
    gc_row = _each(lambda v: _colsum(jnp.where(eye, v, 0.0)), gc)
    decay = _each(lambda v, r: jnp.where(incl, jnp.exp(jnp.where(incl, v - r, 0.0)), 0.0), gc, gc_row)
    gl = _each(lambda v: jnp.where(first, v[CHUNK - 1:CHUNK], v[PAIR - 1:PAIR]), gc)
    egc = _each(jnp.exp, gc)
    q = _each(lambda v: v * QK_SCALE, qn)
    kb = _each(lambda k, b: k * b, kn, beta)
    return dict(same=same, incl=incl, strict=strict, eye=eye, gc=gc, decay=decay, gl=gl, egc=egc,
                ekd=_each(lambda a, b: jnp.exp(a - b), gl, gc), cd=_each(jnp.exp, gl), q=q, kb=kb,
                vb=_each(lambda v, b: v * b, vs, beta), kbg=_each(lambda k, e: k * e, kb, egc),
                kk=_each(_bdot_nt, kb, kn), qk=_each(_bdot_nt, q, kn))


def _tri_inv(a, eye_f):
    p = _each(lambda v: eye_f - v, a)
    x = _each(_bdot, a, a)
    for it in range(5):
        p = _each(lambda pv, xv: pv + _bdot(pv, xv), p, x)
        if it < 4:
            x = _each(_bdot, x, x)
    return p


def _pair_spec():
    return pl.BlockSpec((PAIR, D_HALF), lambda i: (i, 0))


def _intra_fwd(qn, kn, vs, beta, g):
    s = qn.shape[0]

    def body(qn_ref, kn_ref, vs_ref, beta_ref, g_ref, u_ref, w_ref, att_ref, qd_ref, kd_ref, t_ref, cd_ref):
        kn = _heads(kn_ref)
        cm = _pair_common(_heads(qn_ref), kn, _heads(vs_ref), _heads(beta_ref), _heads(g_ref))
        a = _each(lambda kk, d: jnp.where(cm["strict"], kk * d, 0.0), cm["kk"], cm["decay"])
        tm = _tri_inv(a, cm["eye"].astype(F32))
        _put_heads(t_ref, tm)
        _put_heads(u_ref, _each(_bdot, tm, cm["vb"]))
        _put_heads(w_ref, _each(_bdot, tm, cm["kbg"]))
        _put_heads(att_ref, _each(lambda a, b: a * b, cm["qk"], cm["decay"]))
        _put_heads(qd_ref, _each(lambda a, b: a * b, cm["q"], cm["egc"]))
        _put_heads(kd_ref, _each(lambda a, b: a * b, kn, cm["ekd"]))
        _put_heads(cd_ref, cm["cd"])

    return _call(
        body, name="intra_fwd", grid=(s // PAIR,),
        in_specs=[_pair_spec()] * 5, out_specs=[_pair_spec()] * 7,
        out_shape=[_sds((s, D_HALF))] * 7,
        compiler_params=_params("arbitrary"),
    )(qn, kn, vs, beta, g)


def _scan_fwd(u, w, att, qd, kd, cd):
    s = u.shape[0]
    n_chunks = s // CHUNK

    def body(u_ref, w_ref, att_ref, qd_ref, kd_ref, cd_ref, o_ref, vn_ref, st_ref, state):
        @pl.when(pl.program_id(0) == 0)
        def _():
            state[...] = jnp.zeros_like(state)
        cols = list(enumerate(HEAD_COLS))
        sm = [state[h] for h in HEADS]
        qs = []
        for ci in range(2):
            rs = slice(ci * CHUNK, (ci + 1) * CHUNK)
            for h in HEADS:
                st_ref[ci, h] = sm[h]
            both = [_bdot(jnp.concatenate([w_ref[rs, sl], qd_ref[rs, sl]], axis=0), sm[h]) for h, sl in cols]
            vn = [u_ref[rs, sl] - both[h][:CHUNK] for h, sl in cols]
            qs.append([b[CHUNK:] for b in both])
            for h, sl in cols:
                vn_ref[rs, sl] = vn[h]
            sm = [sm[h] * cd_ref[ci * CHUNK:ci * CHUNK + 1, sl] + _bdot_tn(kd_ref[rs, sl], vn[h]) for h, sl in cols]
        for h in HEADS:
            state[h] = sm[h]
        intra = [_bdot(att_ref[:, sl], vn_ref[:, sl]) for sl in HEAD_COLS]
        for h, sl in cols:
            o_ref[:, sl] = jnp.concatenate([qs[0][h], qs[1][h]], axis=0) + intra[h]

    return _call(
        body, name="scan_fwd", grid=(s // PAIR,),
        in_specs=[_pair_spec()] * 6,
        out_specs=[_pair_spec(), _pair_spec(), pl.BlockSpec((2, N_HEADS, HEAD, HEAD), lambda i: (i, 0, 0, 0))],
        out_shape=[_sds((s, D_HALF)), _sds((s, D_HALF)), _sds((n_chunks, N_HEADS, HEAD, HEAD))],
        scratch_shapes=[pltpu.VMEM((N_HEADS, HEAD, HEAD), F32)],
        compiler_params=_params("arbitrary"),
    )(u, w, att, qd, kd, cd)


OUT_T = 512


def _out_fwd_bwd(x, y_pool, o, proj, target, w_out, dn_norm_w, final_norm_w):
    s = x.shape[0]
    t = OUT_T

    def body(x_ref, yp_ref, o_ref, z_ref, tg_ref, wo_ref, dnw_ref, fnw_ref,
             y_ref, dh_ref, dyp_ref, do_ref, dz_ref, loss_ref, gfn_ref, gdn_ref):
        @pl.when(pl.program_id(0) == 0)
        def _():
            loss_ref[...] = jnp.zeros_like(loss_ref)
            gfn_ref[...] = jnp.zeros_like(gfn_ref)
            gdn_ref[...] = jnp.zeros_like(gdn_ref)

        y_ref[:, :D_HALF] = yp_ref[...].astype(BF16)
        dnw = dnw_ref[...]
        keep = []
        for h in HEADS:
            ov = o_ref[:, HEAD_COLS[h]]
            zv = z_ref[:, HEAD_COLS[h]]
            ro = lax.rsqrt(jnp.mean(ov * ov, axis=-1, keepdims=True) + EPS)
            ohat = ov * ro
            sg = _sigmoid(zv)
            keep.append((ro, ohat, zv, sg))
            y_ref[:, D_HALF + h * HEAD:D_HALF + (h + 1) * HEAD] = (ohat * dnw * (zv * sg)).astype(BF16)

        hv = x_ref[...] + jnp.dot(y_ref[...], wo_ref[...], preferred_element_type=F32)
        r2 = lax.rsqrt(jnp.mean(hv * hv, axis=-1, keepdims=True) + EPS)
        hhat = hv * r2
        fnw = fnw_ref[...]
        err = hhat * fnw - tg_ref[...]
        loss_ref[...] += 0.5 * jnp.sum(_rowsum(err * err) * (1.0 / D_MODEL), axis=0, keepdims=True)
        dout = err * (1.0 / D_MODEL)
        gfn_ref[...] += _colsum(dout * hhat)
        dhh = dout * fnw
        dh = r2 * (dhh - hhat * jnp.mean(dhh * hhat, axis=-1, keepdims=True))
        dh_ref[...] = dh
        dy = _bdot_nt(dh, wo_ref[...])
        dyp_ref[...] = dy[:, :D_HALF]
        gdn = jnp.zeros((1, HEAD), F32)
        for h in HEADS:
            ro, ohat, zv, sg = keep[h]
            dyd = dy[:, D_HALF + h * HEAD:D_HALF + (h + 1) * HEAD]
            sz = zv * sg
            dz_ref[:, HEAD_COLS[h]] = dyd * ohat * dnw * (sg * (1.0 + zv * (1.0 - sg)))
            gdn = gdn + _colsum(dyd * ohat * sz)
            doh = dyd * dnw * sz
            do_ref[:, HEAD_COLS[h]] = ro * (doh - ohat * jnp.mean(doh * ohat, axis=-1, keepdims=True))
        gdn_ref[...] += gdn

    wide = pl.BlockSpec((t, D_MODEL), lambda i: (i, 0))
    half = pl.BlockSpec((t, D_HALF), lambda i: (i, 0))
    const = lambda shape: pl.BlockSpec(shape, lambda i: (0,) * len(shape))
    return _call(
        body, name="out_fwd_bwd", grid=(s // t,),
        in_specs=[wide, half, half, pl.BlockSpec((t, D_HALF), lambda i: (i, 5)), wide,
                  const((D_MODEL, D_MODEL)), const((1, HEAD)), const((1, D_MODEL))],
        out_specs=[wide, wide, half, half, half, const((1, HEAD)), const((1, D_MODEL)), const((1, HEAD))],
        out_shape=[_sds((s, D_MODEL), BF16), _sds((s, D_MODEL)), _sds((s, D_HALF)), _sds((s, D_HALF)), _sds((s, D_HALF)),
                   _sds((1, HEAD)), _sds((1, D_MODEL)), _sds((1, HEAD))],
        compiler_params=_params("arbitrary"),
    )(x, y_pool, o, proj, target, w_out, dn_norm_w, final_norm_w)


def _tn_matmul(name, a, pieces):
    s, m = a.shape
    n = len(pieces)
    tn, tk = D_HALF, 512

    def body(a_ref, *refs):
        p_refs, o_ref = refs[:n], refs[n]
        j = pl.program_id(0)

        @pl.when(pl.program_id(1) == 0)
        def _():
            o_ref[...] = jnp.zeros_like(o_ref)

        for p in range(n):
            @pl.when(j == p)
            def _(p=p):
                o_ref[...] += _bdot_tn(a_ref[...], p_refs[p][...])

    piece_spec = lambda p: pl.BlockSpec((tk, tn), lambda j, k: (jnp.where(j == p, k, 0), pieces[p][1]))
    return _call(
        body, name=name, grid=(n, s // tk),
        in_specs=[pl.BlockSpec((tk, m), lambda j, k: (k, 0))] + [piece_spec(p) for p in range(n)],
        out_specs=pl.BlockSpec((m, tn), lambda j, k: (0, j)),
        out_shape=_sds((m, n * tn)),
        compiler_params=_params("arbitrary", "arbitrary"),
    )(a, *[p[0] for p in pieces])


def _pool_bwd(proj, dyp, pool_w, pool_scale):
    s = proj.shape[0]
    t = POOL_T
    hb = t // HEAD
    last = s // HEAD - 1

    def body(u_ref, z_ref, halo_ref, dy_ref, zn_ref, dyn_ref, pw_ref, ps_ref, du_ref, dz_ref, gpw_ref, gps_ref):
        i = pl.program_id(0)

        @pl.when(i == 0)
        def _():
            gpw_ref[...] = jnp.zeros_like(gpw_ref)
            gps_ref[...] = jnp.zeros_like(gps_ref)

        live = (i > 0).astype(F32)
        more = (i < pl.num_programs(0) - 1).astype(F32)
        for g, w in enumerate(WINDOWS):
            sl = HEAD_COLS[g]
            zg = z_ref[:, sl]
            ps = ps_ref[:, sl]
            pw = pw_ref[g]
            mix, mixed, sg, cnt = _pool_mix(u_ref[:, sl], halo_ref[:, sl] * live, zg, pw, i * t, w)
            dyg = dy_ref[:, sl]
            sz = zg * sg
            dz_ref[:, sl] = dyg * mixed * ps * (sg * (1.0 + zg * (1.0 - sg)))
            gps_ref[:, sl] += _colsum(dyg * mixed * sz)
            dmixed = dyg * ps * sz
            gpw_ref[g] += _bdot_tn(mix, dmixed)
            dmix = _bdot_nt(dmixed, pw)
            zn = zn_ref[:, sl]
            dmix_n = _bdot_nt(dyn_ref[:, sl] * more * ps * (zn * _sigmoid(zn)), pw)
            du_ref[:, sl] = (_mask_dot(_band(t, t, 0, w, anti=True), dmix / cnt)
                             + _mask_dot(_band(t, HEAD, t, w, anti=True), dmix_n * (1.0 / w)) - dmix)

    tile = lambda col: pl.BlockSpec((t, D_HALF), lambda i: (i, col))
    below = lambda col: pl.BlockSpec((HEAD, D_HALF), lambda i: (jnp.minimum((i + 1) * hb, last), col))
    return _call(
        body, name="pool_bwd", grid=(s // t,),
        in_specs=[tile(0), tile(1), pl.BlockSpec((HEAD, D_HALF), lambda i: (jnp.maximum(i * hb - 1, 0), 0)),
                  tile(0), below(1), below(0),
                  pl.BlockSpec((N_HEADS, HEAD, HEAD), lambda i: (0, 0, 0)), pl.BlockSpec((1, D_HALF), lambda i: (0, 0))],
        out_specs=[tile(0), tile(0), pl.BlockSpec((N_HEADS, HEAD, HEAD), lambda i: (0, 0, 0)),
                   pl.BlockSpec((1, D_HALF), lambda i: (0, 0))],
        out_shape=[_sds((s, D_HALF)), _sds((s, D_HALF)), _sds((N_HEADS, HEAD, HEAD)), _sds((1, D_HALF))],
        compiler_params=_params("arbitrary"),
    )(proj, proj, proj, dyp, proj, dyp, pool_w, pool_scale)


def _scan_bwd(do, vn, qd, kd, w, att, cd, st):
    s = do.shape[0]
    n_pairs = s // PAIR

    def body(do_ref, vn_ref, qd_ref, kd_ref, w_ref, att_ref, cd_ref, st_ref,
             du_ref, dw_ref, datt_ref, dqd_ref, dkd_ref, dcd_ref, dstate):
        @pl.when(pl.program_id(0) == 0)
        def _():
            dstate[...] = jnp.zeros_like(dstate)
        _, incl, _, _ = _pair_masks()
        cols = list(enumerate(HEAD_COLS))
        dv_intra = [_bdot_tn(att_ref[:, sl], do_ref[:, sl]) for _, sl in cols]
        _put_heads(datt_ref, [jnp.where(incl, _bdot_nt(do_ref[:, sl], vn_ref[:, sl]), 0.0) for _, sl in cols])
        ds = [dstate[h] for h in HEADS]
        for ci in (1, 0):
            rs = slice(ci * CHUNK, (ci + 1) * CHUNK)
            sm = [st_ref[ci, h] for h in HEADS]
            dvn = [dv_intra[h][rs] + _bdot(kd_ref[rs, sl], ds[h]) for h, sl in cols]
            for h, sl in cols:
                du_ref[rs, sl] = dvn[h]
            dqd = [_bdot_nt(do_ref[rs, sl], sm[h]) for h, sl in cols]
            dw = [-_bdot_nt(dvn[h], sm[h]) for h, _ in cols]
            dkd = [_bdot_nt(vn_ref[rs, sl], ds[h]) for h, sl in cols]
            dcd = [jnp.broadcast_to(_rowsum(_colsum(ds[h] * sm[h])), (CHUNK, HEAD)) for h in HEADS]
            for h, sl in cols:
                dqd_ref[rs, sl] = dqd[h]
                dw_ref[rs, sl] = dw[h]
                dkd_ref[rs, sl] = dkd[h]
                dcd_ref[rs, sl] = dcd[h]
            ds = [ds[h] * cd_ref[ci * CHUNK:ci * CHUNK + 1, sl] + _bdot_tn(qd_ref[rs, sl], do_ref[rs, sl])
                  - _bdot_tn(w_ref[rs, sl], dvn[h]) for h, sl in cols]
        for h in HEADS:
            dstate[h] = ds[h]

    rev = pl.BlockSpec((PAIR, D_HALF), lambda i: (n_pairs - 1 - i, 0))
    return _call(
        body, name="scan_bwd", grid=(n_pairs,),
        in_specs=[rev] * 7 + [pl.BlockSpec((2, N_HEADS, HEAD, HEAD), lambda i: (n_pairs - 1 - i, 0, 0, 0))],
        out_specs=[rev] * 6,
        out_shape=[_sds((s, D_HALF))] * 6,
        scratch_shapes=[pltpu.VMEM((N_HEADS, HEAD, HEAD), F32)],
        compiler_params=_params("arbitrary"),
    )(do, vn, qd, kd, w, att, cd, st)


def _intra_bwd(qn, kn, vs, beta, g, tm, du, dw, datt, dqd, dkd, dcd):
    s = qn.shape[0]

    def body(qn_ref, kn_ref, vs_ref, beta_ref, g_ref, t_ref, du_ref, dw_ref, datt_ref, dqd_ref, dkd_ref, dcd_ref,
             dqn_ref, dkn_ref, dvs_ref, dbeta_ref, dg_ref):
        ones = jnp.ones((PAIR, HEAD), BF16)
        tn = (((0,), (0,)), ((), ()))
        kn, vs, beta = _heads(kn_ref), _heads(vs_ref), _heads(beta_ref)
        cm = _pair_common(_heads(qn_ref), kn, vs, beta, _heads(g_ref))
        tmv, duv, dwv, dattv, dqdv, dkdv = (_heads(r) for r in (t_ref, du_ref, dw_ref, datt_ref, dqd_ref, dkd_ref))
        dvb = _each(_bdot_tn, tmv, duv)
        dt = _each(lambda a, b, c, d: _bdot_nt(a, b) + _bdot_nt(c, d), duv, cm["vb"], dwv, cm["kbg"])
        dkbg = _each(_bdot_tn, tmv, dwv)
        m1 = _each(_bdot_tn, tmv, dt)
        da = _each(lambda a, b: -jnp.where(cm["strict"], _bdot_nt(a, b), 0.0), m1, tmv)
        dkk = _each(lambda a, b: a * b, da, cm["decay"])
        dqk = _each(lambda a, b: a * b, dattv, cm["decay"])
        dd = _each(lambda a, b, c, d: a * b + c * d, dkk, cm["kk"], dqk, cm["qk"])
        dkb = _each(lambda a, b, c, d: _bdot(a, b) + c * d, dkk, kn, dkbg, cm["egc"])
        dq = _each(lambda a, b, c, d: _bdot(a, b) + c * d, dqk, kn, dqdv, cm["egc"])
        dkn = _each(lambda a, b, c, d: _bdot_tn(a, b) + _bdot_tn(c, d), dkk, cm["kb"], dqk, cm["q"])
        dkn = _each(lambda a, b, c, d, e: a + b * c + d * e, dkn, dkdv, cm["ekd"], dkb, beta)
        t_kd = _each(lambda a, b, c: _rowsum(a * b * c), dkdv, kn, cm["ekd"])
        split = _each(_split, dd)
        rows_dd = [jnp.dot(hi, ones, preferred_element_type=F32) + jnp.dot(lo, ones, preferred_element_type=F32)
                   for hi, lo in split]
        cols_dd = [lax.dot_general(hi, ones, tn, preferred_element_type=F32)
                   + lax.dot_general(lo, ones, tn, preferred_element_type=F32) for hi, lo in split]
        dgc = _each(lambda r, c, a, b, e, f, k, t: r - c + _rowsum(a * b * e) + _rowsum(f * k) - t,
                    rows_dd, cols_dd, dqdv, cm["q"], cm["egc"], dkbg, cm["kbg"], t_kd)
        same_b = cm["same"].astype(BF16)
        dgl = _each(lambda t, d, c: _mask_dot(same_b, jnp.broadcast_to(t, (PAIR, HEAD))) + d * c,
                    t_kd, _heads(dcd_ref), cm["cd"])
        rowi = lax.broadcasted_iota(I32, (PAIR, HEAD), 0)
        is_last = jnp.bitwise_and(rowi, CHUNK - 1) == CHUNK - 1
        dgc = _each(lambda a, b: a + jnp.where(is_last, b, 0.0), dgc, dgl)
        r = lax.broadcasted_iota(I32, (PAIR, PAIR), 0)
        c = lax.broadcasted_iota(I32, (PAIR, PAIR), 1)
        upper_b = (cm["same"] & (r <= c)).astype(BF16)
        _put_heads(dg_ref, _each(lambda v: _mask_dot(upper_b, v), dgc))
        _put_heads(dbeta_ref, _each(lambda a, b, c, d: jnp.broadcast_to(_rowsum(a * b) + _rowsum(c * d), (PAIR, HEAD)),
                                    dkb, kn, dvb, vs))
        _put_heads(dqn_ref, _each(lambda v: v * QK_SCALE, dq))
        _put_heads(dkn_ref, dkn)
        _put_heads(dvs_ref, _each(lambda a, b: a * b, dvb, beta))

    return _call(
        body, name="intra_bwd", grid=(s // PAIR,),
        in_specs=[_pair_spec()] * 12, out_specs=[_pair_spec()] * 5,
        out_shape=[_sds((s, D_HALF))] * 5,
        compiler_params=_params("arbitrary"),
    )(qn, kn, vs, beta, g, tm, du, dw, datt, dqd, dkd, dcd)


def _conv_bwd_pre(proj, conv_w, a_log, dt_bias, dqn, dkn, dvs, dbeta, dg):
    s = proj.shape[0]
    t = CONV_T

    def body(q_ref, k_ref, v_ref, hq_ref, hk_ref, hv_ref, ba_ref, cw_ref, al_ref, dtb_ref,
             dqn_ref, dkn_ref, dvs_ref, dbeta_ref, dg_ref, dyq_ref, dyk_ref, dyv_ref, dba_ref, gcw_ref, gsm_ref):
        @pl.when(pl.program_id(0) == 0)
        def _():
            gcw_ref[...] = jnp.zeros_like(gcw_ref)
            gsm_ref[...] = jnp.zeros_like(gsm_ref)

        live = (pl.program_id(0) > 0).astype(F32)
        parts = ((q_ref, hq_ref, dqn_ref, dyq_ref), (k_ref, hk_ref, dkn_ref, dyk_ref), (v_ref, hv_ref, dvs_ref, dyv_ref))
        for p, (x_ref, h_ref, d_ref, dy_ref) in enumerate(parts):
            cols = slice(p * D_HALF, (p + 1) * D_HALF)
            xv = x_ref[...]
            prev = h_ref[...] * live
            y = _conv_pre(xv, prev, cw_ref[:, cols])
            sg = _sigmoid(y)
            sv = y * sg
            if p == 2:
                ds = d_ref[...]
            else:
                segs = []
                for h in HEADS:
                    seg = _head(sv, h)
                    rn = lax.rsqrt(_rowsum(seg * seg) + EPS)
                    nrm = seg * rn
                    dn = d_ref[:, HEAD_COLS[h]]
                    segs.append(rn * (dn - nrm * _rowsum(dn * nrm)))
                ds = jnp.concatenate(segs, axis=1)
            dy = ds * (sg * (1.0 + y * (1.0 - sg)))
            dy_ref[...] = dy
            gcw_ref[3:4, cols] += _colsum(dy * xv)
            for sft in (1, 2, 3):
                gcw_ref[3 - sft:4 - sft, cols] += _colsum(dy * _shift_down(xv, prev, sft))

        ba = ba_ref[...]
        lane = lax.broadcasted_iota(I32, (t, HEAD), 1)
        lane1 = lax.broadcasted_iota(I32, (1, HEAD), 1)
        dba = jnp.zeros((t, HEAD), F32)
        gsm = jnp.zeros((1, HEAD), F32)
        for h in HEADS:
            beta = _sigmoid(ba[:, h:h + 1])
            dbeta = dbeta_ref[:, h * HEAD:h * HEAD + 1]
            xg = ba[:, N_HEADS + h:N_HEADS + h + 1] + dtb_ref[0:1, h:h + 1]
            nexp = -jnp.exp(al_ref[0:1, h:h + 1])
            dgv = dg_ref[:, h * HEAD:h * HEAD + 1]
            da = dgv * nexp * _sigmoid(xg)
            dba = dba + jnp.where(lane == h, dbeta * beta * (1.0 - beta), 0.0) + jnp.where(lane == N_HEADS + h, da, 0.0)
            gsm = (gsm + jnp.where(lane1 == h, _colsum(dgv * nexp * _softplus(xg)), 0.0)
                   + jnp.where(lane1 == N_HEADS + h, _colsum(da), 0.0))
        dba_ref[...] = jnp.zeros_like(dba_ref)
        dba_ref[:, :HEAD] = dba
        gsm_ref[0:1, :] += gsm

    row = pl.BlockSpec((t, D_HALF), lambda i: (i, 0))
    return _call(
        body, name="conv_bwd_pre", grid=(s // t,),
        in_specs=_conv_specs(t) + [pl.BlockSpec((t, HEAD), lambda i: (i, COL_BA // HEAD)),
                                   pl.BlockSpec((CONV_K, 3 * D_HALF), lambda i: (0, 0)),
                                   pl.BlockSpec((1, N_HEADS), lambda i: (0, 0)),
                                   pl.BlockSpec((1, N_HEADS), lambda i: (0, 0))] + [row] * 5,
        out_specs=[row, row, row, row,
                   pl.BlockSpec((8, 3 * D_HALF), lambda i: (0, 0)), pl.BlockSpec((8, HEAD), lambda i: (0, 0))],
        out_shape=[_sds((s, D_HALF))] * 4 + [_sds((8, 3 * D_HALF)), _sds((8, HEAD))],
        compiler_params=_params("arbitrary"),
    )(proj, proj, proj, proj, proj, proj, proj, conv_w, a_log, dt_bias, dqn, dkn, dvs, dbeta, dg)


def _conv_bwd_in(dyq, dyk, dyv, conv_w):
    s = dyq.shape[0]
    t = CONV_T
    last = s // 8 - 1

    def body(q_ref, k_ref, v_ref, nq_ref, nk_ref, nv_ref, cw_ref, oq_ref, ok_ref, ov_ref):
        more = (pl.program_id(0) < pl.num_programs(0) - 1).astype(F32)
        for p, (d_ref, n_ref, o_ref) in enumerate(((q_ref, nq_ref, oq_ref), (k_ref, nk_ref, ok_ref), (v_ref, nv_ref, ov_ref))):
            cw = cw_ref[:, p * D_HALF:(p + 1) * D_HALF]
            dy = d_ref[...]
            nxt = n_ref[...] * more
            acc = dy * cw[3:4]
            for sft in (1, 2, 3):
                acc = acc + _shift_up(dy, nxt, sft) * cw[3 - sft:4 - sft]
            o_ref[...] = acc

    row = pl.BlockSpec((t, D_HALF), lambda i: (i, 0))
    nxt = pl.BlockSpec((8, D_HALF), lambda i: (jnp.minimum((i + 1) * (t // 8), last), 0))
    return _call(
        body, name="conv_bwd_in", grid=(s // t,),
        in_specs=[row] * 3 + [nxt] * 3 + [pl.BlockSpec((CONV_K, 3 * D_HALF), lambda i: (0, 0))],
        out_specs=[row] * 3, out_shape=[_sds((s, D_HALF))] * 3,
        compiler_params=_params("arbitrary"),
    )(dyq, dyk, dyv, dyq, dyk, dyv, conv_w)


IN_T = 512


def _in_bwd(x, dh, norm_w, w_pad, pieces):
    s = x.shape[0]
    t = IN_T
    widths = [D_HALF] * 6 + [N_IN_PAD - COL_BA]

    def body(*refs):
        x_ref, dh_ref, nw_ref, w_ref = refs[:4]
        p_refs = refs[4:4 + len(pieces)]
        gx_ref, gnw_ref = refs[4 + len(pieces):]

        @pl.when(pl.program_id(0) == 0)
        def _():
            gnw_ref[...] = jnp.zeros_like(gnw_ref)

        dn = jnp.zeros((t, D_MODEL), F32)
        col = 0
        for p_ref, wd in zip(p_refs, widths):
            dn = dn + _bdot_nt(p_ref[...], w_ref[:, col:col + wd])
            col += wd
        xv = x_ref[...]
        r = lax.rsqrt(jnp.mean(xv * xv, axis=-1, keepdims=True) + EPS)
        xhat = xv * r
        gnw_ref[...] += _colsum(dn * xhat)
        dxh = dn * nw_ref[...]
        gx_ref[...] = dh_ref[...] + r * (dxh - xhat * jnp.mean(dxh * xhat, axis=-1, keepdims=True))

    wide = pl.BlockSpec((t, D_MODEL), lambda i: (i, 0))
    return _call(
        body, name="in_bwd", grid=(s // t,),
        in_specs=[wide, wide, pl.BlockSpec((1, D_MODEL), lambda i: (0, 0)),
                  pl.BlockSpec((D_MODEL, N_IN_PAD), lambda i: (0, 0))]
                 + [pl.BlockSpec((t, wd), lambda i: (i, 0)) for wd in widths],
        out_specs=[wide, pl.BlockSpec((1, D_MODEL), lambda i: (0, 0))],
        out_shape=[_sds((s, D_MODEL)), _sds((1, D_MODEL))],
        compiler_params=_params("arbitrary"),
    )(x, dh, norm_w, w_pad, *pieces)


def _adamw_shard(name, w, g_own, g_got, cidx, m, v):
    _, r, c = w.shape
    half = r // 2
    rows = 256 if half % 256 == 0 else half
    per_half = half // rows

    def body(c_ref, w_ref, go_ref, gg_ref, m_ref, v_ref, gout_ref, d_ref, nm_ref, nv_ref):
        mine = (pl.program_id(0) // per_half) == c_ref[0]
        gv = jnp.where(mine, go_ref[:, :c], gg_ref[:, :c])
        gout_ref[0] = gv
        mn = ADAM_B1 * m_ref[0] + (1.0 - ADAM_B1) * gv
        vn = ADAM_B2 * v_ref[0] + (1.0 - ADAM_B2) * (gv * gv)
        m_hat = mn / (1.0 - ADAM_B1 ** ADAM_STEP)
        v_hat = vn / (1.0 - ADAM_B2 ** ADAM_STEP)
        d_ref[0] = -ADAM_LR * (m_hat / (jnp.sqrt(v_hat) + ADAM_EPS) + ADAM_WD * w_ref[0])
        nm_ref[0] = mn
        nv_ref[0] = vn

    blk = pl.BlockSpec((1, rows, c), lambda i, c_ref: (0, i, 0))
    gblk = pl.BlockSpec((rows, g_own.shape[1]), lambda i, c_ref: (i % per_half, 0))
    return _call(
        body, name=name,
        grid_spec=pltpu.PrefetchScalarGridSpec(
            num_scalar_prefetch=1, grid=(2 * per_half,),
            in_specs=[blk, gblk, gblk, blk, blk], out_specs=[blk] * 4),
        out_shape=[_sds((1, r, c))] * 4,
        compiler_params=_params("arbitrary"),
    )(cidx, w, g_own, g_got, m, v)


def _exchange(name, inputs, out_shapes, phases):
    n_in = len(inputs)
    n_out = len(out_shapes)
    n_cp = sum(len(p) for p in phases)

    def body(*refs):
        ins, outs = refs[:n_in], refs[n_in:n_in + n_out]
        send, recv = refs[n_in + n_out:]
        pos = (lax.axis_index("x"), lax.axis_index("y"), lax.axis_index("c"))
        k = 0
        for phase in phases:
            cps = []
            for src, dst, target in phase:
                cps.append(pltpu.make_async_remote_copy(
                    src_ref=src(ins, outs, pos), dst_ref=dst(ins, outs, pos), send_sem=send.at[k], recv_sem=recv.at[k],
                    device_id=target(pos), device_id_type=pl.DeviceIdType.MESH))
                k += 1
            for cp in cps:
                cp.start()
            for cp in cps:
                cp.wait()

    anyspec = pl.BlockSpec(memory_space=pl.ANY)
    return _call(
        body, name=name,
        in_specs=[anyspec] * n_in, out_specs=[anyspec] * n_out, out_shape=list(out_shapes),
        scratch_shapes=[pltpu.SemaphoreType.DMA((n_cp,)), pltpu.SemaphoreType.DMA((n_cp,))],
    )(*inputs)


def _chip(pos):
    return 2 * pos[0] + pos[1]


def _other_chip(pos, mask):
    x, y, c = pos
    return (x ^ (mask >> 1), y ^ (mask & 1), c)


def _sibling(pos):
    return (pos[0], pos[1], 1 - pos[2])


def _gather_weights(wb, ob, cb):
    halves = (wb.shape[0] // 2, ob.shape[0] // 2)

    def half(a, pos):
        return pl.ds(pos[2] * halves[a], halves[a])

    first, second = [], []
    for mask in CHIP_MASKS:
        for a in (0, 1):
            first.append((lambda ins, outs, pos, a=a: ins[a].at[half(a, pos)],
                          lambda ins, outs, pos, a=a: outs[a].at[_chip(pos), half(a, pos)],
                          functools.partial(_other_chip, mask=mask)))
            second.append((lambda ins, outs, pos, a=a, mask=mask: outs[a].at[_chip(pos) ^ mask, half(a, pos)],
                           lambda ins, outs, pos, a=a, mask=mask: outs[a].at[_chip(pos) ^ mask, half(a, pos)],
                           _sibling))
        first.append((lambda ins, outs, pos: ins[2],
                      lambda ins, outs, pos: outs[2].at[_chip(pos)],
                      functools.partial(_other_chip, mask=mask)))
    return _exchange("gather_weights", [wb, ob, cb],
                     [_sds((4,) + wb.shape, wb.dtype), _sds((4,) + ob.shape, ob.dtype), _sds((4,) + cb.shape, cb.dtype)],
                     [first, second])


def _to_sibling_half(name, arrays):
    def src(ins, outs, pos, a):
        h = arrays[a].shape[-2] // 2
        sl = pl.ds((1 - pos[2]) * h, h)
        return ins[a].at[:, sl] if arrays[a].ndim == 3 else ins[a].at[sl]

    outs = [_sds(a.shape[:-2] + (a.shape[-2] // 2, a.shape[-1]), a.dtype) for a in arrays]
    phase = [(functools.partial(src, a=a), lambda ins, outs, pos, a=a: outs[a], _sibling) for a in range(len(arrays))]
    return _exchange(name, arrays, outs, [phase])


def _add_half(name, full, part, cidx):
    shape = part.shape
    lead = shape[0] if len(shape) == 3 else 1
    rows, cols = shape[-2], shape[-1]
    tr = rows // 2 if rows % 16 == 0 else rows
    nr = rows // tr
    f3 = full.reshape((lead,) + full.shape[-2:])
    p3 = part.reshape((lead, rows, cols))

    def body(c_ref, f_ref, p_ref, o_ref):
        o_ref[...] = (f_ref[...].astype(F32) + p_ref[...].astype(F32)).astype(o_ref.dtype)

    out = _call(
        body, name=name,
        grid_spec=pltpu.PrefetchScalarGridSpec(
            num_scalar_prefetch=1, grid=(lead, nr),
            in_specs=[pl.BlockSpec((1, tr, cols), lambda b, r, c_ref: (b, c_ref[0] * nr + r, 0)),
                      pl.BlockSpec((1, tr, cols), lambda b, r, c_ref: (b, r, 0))],
            out_specs=pl.BlockSpec((1, tr, cols), lambda b, r, c_ref: (b, r, 0))),
        out_shape=_sds((lead, rows, cols), part.dtype),
        compiler_params=_params("arbitrary", "arbitrary"),
    )(cidx, f3, p3)
    return out.reshape(shape)


def _to_other_chips(name, arrays, blocked):
    def src(ins, outs, pos, a, mask):
        return ins[a].at[_chip(pos) ^ mask] if blocked[a] else ins[a]

    outs = [_sds((3,) + (a.shape[1:] if b else a.shape), a.dtype) for a, b in zip(arrays, blocked)]
    phase = []
    for mi, mask in enumerate(CHIP_MASKS):
        for a in range(len(arrays)):
            phase.append((functools.partial(src, a=a, mask=mask), lambda ins, outs, pos, a=a, mi=mi: outs[a].at[mi],
                          functools.partial(_other_chip, mask=mask)))
    return _exchange(name, arrays, outs, [phase])


def _add_chips(name, own, got, jidx, blocked):
    rows, cols = got.shape[-2:]
    tr = rows // 2 if rows % 16 == 0 else rows
    nr = rows // tr
    o3 = own if blocked else own.reshape((1, rows, cols))

    def body(j_ref, o_ref, g_ref, out_ref):
        out_ref[...] = ((o_ref[0].astype(F32) + g_ref[0].astype(F32))
                        + (g_ref[1].astype(F32) + g_ref[2].astype(F32)))

    own_map = (lambda r, j_ref: (j_ref[0], r, 0)) if blocked else (lambda r, j_ref: (0, r, 0))
    return _call(
        body, name=name,
        grid_spec=pltpu.PrefetchScalarGridSpec(
            num_scalar_prefetch=1, grid=(nr,),
            in_specs=[pl.BlockSpec((1, tr, cols), own_map),
                      pl.BlockSpec((3, tr, cols), lambda r, j_ref: (0, r, 0))],
            out_specs=pl.BlockSpec((tr, cols), lambda r, j_ref: (r, 0))),
        out_shape=_sds((rows, cols)),
        compiler_params=_params("arbitrary"),
    )(jidx, o3, got)


def _to_sibling(name, arrays):
    phase = [(lambda ins, outs, pos, a=a: ins[a], lambda ins, outs, pos, a=a: outs[a], _sibling)
             for a in range(len(arrays))]
    return _exchange(name, arrays, [_sds(a.shape, a.dtype) for a in arrays], [phase])


def _local_step(x, target, w_pad, w_out, conv_w, norm_w, pool_w, pool_scale, a_log, dt_bias, dn_norm_w, final_norm_w):
    proj, nbf = _proj_fwd(x, norm_w, w_pad)
    y_pool = _pool_fwd(proj, pool_w, pool_scale)
    qn, kn, vs, beta, g = _conv_fwd(proj, conv_w, a_log, dt_bias)
    u, w, att, qd, kd, tm, cd = _intra_fwd(qn, kn, vs, beta, g)
    o, vn, st = _scan_fwd(u, w, att, qd, kd, cd)
    y, dh, dyp, do, ddz, loss, g_fnw, g_dnw = _out_fwd_bwd(x, y_pool, o, proj, target, w_out, dn_norm_w, final_norm_w)
    g_wout = _tn_matmul("grad_w_out", y, [(dh, 0), (dh, 1)])
    dpu, dpz, g_pw, g_ps = _pool_bwd(proj, dyp, pool_w, pool_scale)
    du, dw, datt, dqd, dkd, dcd = _scan_bwd(do, vn, qd, kd, w, att, cd, st)
    dqn, dkn, dvs, dbeta, dg = _intra_bwd(qn, kn, vs, beta, g, tm, du, dw, datt, dqd, dkd, dcd)
    dyq, dyk, dyv, dba, g_cw, g_sm = _conv_bwd_pre(proj, conv_w, a_log, dt_bias, dqn, dkn, dvs, dbeta, dg)
    dcq, dck, dcv = _conv_bwd_in(dyq, dyk, dyv, conv_w)
    pieces = [dpu, dpz, dcq, dck, dcv, ddz, dba]
    gx, g_nw = _in_bwd(x, dh, norm_w, w_pad, pieces)
    g_win = _tn_matmul("grad_w_in", nbf, [(p, 0) for p in pieces])
    small = dict(norm_w=g_nw, pool_w=g_pw, pool_scale=g_ps, conv_w=g_cw[:CONV_K], a_log=g_sm[0:1, 0:N_HEADS],
                 dt_bias=g_sm[0:1, N_HEADS:2 * N_HEADS], dn_norm_w=g_dnw, final_norm_w=g_fnw)
    return loss[0, 0], gx, g_win, g_wout, small


def _pack_small(t):
    lanes = lambda a: jnp.pad(a.reshape(1, -1), ((0, 0), (0, HEAD - a.size)))
    rows = [t["pool_w"].reshape(-1, HEAD), t["norm_w"].reshape(-1, HEAD), t["final_norm_w"].reshape(-1, HEAD),
            t["pool_scale"].reshape(-1, HEAD), t["conv_w"].reshape(-1, HEAD), t["dn_norm_w"].reshape(1, HEAD),
            lanes(t["a_log"]), lanes(t["dt_bias"]), lanes(t.get("loss", jnp.zeros((1,), F32)))]
    buf = jnp.concatenate(rows, axis=0)
    return jnp.pad(buf, ((0, SMALL_ROWS - buf.shape[0]), (0, 0)))


def _unpack_small(buf, conv_cols):
    out, r = {}, 0
    for name, nrow, shape in (("pool_w", 512, (1, N_HEADS, HEAD, HEAD)), ("norm_w", 8, (1, D_MODEL)),
                              ("final_norm_w", 8, (D_MODEL,)), ("pool_scale", 4, (1, D_HALF)),
                              ("conv_w", CONV_K * conv_cols // HEAD, (1, CONV_K, conv_cols)), ("dn_norm_w", 1, (1, HEAD))):
        out[name] = buf[r:r + nrow].reshape(shape)
        r += nrow
    out["a_log"] = buf[r:r + 1, :N_HEADS]
    out["dt_bias"] = buf[r + 1:r + 2, :N_HEADS]
    out["loss"] = buf[r + 2, 0]
    return out


def kernel(x, norm_w, w_in, pool_w, pool_scale, conv_w, a_log, dt_bias, dn_norm_w, w_out, final_norm_w, loss_target, m_norm_w, m_w_in, m_pool_w, m_pool_scale, m_conv_w, m_a_log, m_dt_bias, m_dn_norm_w, m_w_out, m_final_norm_w, v_norm_w, v_w_in, v_pool_w, v_pool_scale, v_conv_w, v_a_log, v_dt_bias, v_dn_norm_w, v_w_out, v_final_norm_w):
    cidx = lax.axis_index("c").astype(I32).reshape(1)
    jidx = (2 * lax.axis_index("x") + lax.axis_index("y")).astype(I32)

    wb = jnp.pad(w_in[0].astype(BF16), ((0, 0), (0, BLK_IN_PAD - BLK_IN)))
    ob = w_out[0].astype(BF16)
    gw, go, gc = _gather_weights(wb, ob, conv_w[0])
    mine = lambda j: jidx == j
    w_pad = jnp.concatenate([jnp.where(mine(j), wb[:, :BLK_IN], gw[j, :, :BLK_IN]) for j in range(4)]
                            + [jnp.zeros((D_MODEL, N_IN_PAD - N_IN), BF16)], axis=1)
    wo_full = jnp.where((jnp.arange(4) == jidx)[:, None, None], ob[None], go).reshape(D_MODEL, D_MODEL)
    cw_full = jnp.concatenate([jnp.where(mine(j), conv_w[0], gc[j]) for j in range(4)], axis=1)

    loss, gx, g_win, g_wout, small = _local_step(
        x[0], loss_target[0], w_pad, wo_full, cw_full, norm_w, pool_w[0], pool_scale, a_log, dt_bias, dn_norm_w,
        final_norm_w.reshape(1, D_MODEL))
    small["loss"] = loss

    blocks_in = jnp.stack([jnp.pad(g_win[:, j * BLK_IN:(j + 1) * BLK_IN].astype(BF16), ((0, 0), (0, BLK_IN_PAD - BLK_IN)))
                           for j in range(4)])
    blocks_out = g_wout.astype(BF16).reshape(4, BLK_OUT, D_MODEL)
    full = [blocks_in, blocks_out, _pack_small(small)]
    from_sib = _to_sibling_half("reduce_sibling", full)
    chip_sum = [_add_half("add_sibling_%d" % i, f, p, cidx) for i, (f, p) in enumerate(zip(full, from_sib))]
    blocked = [True, True, False]
    from_chips = _to_other_chips("reduce_chips", chip_sum, blocked)
    halves = [_add_chips("add_chips_%d" % i, o, g, jidx.reshape(1), b)
              for i, (o, g, b) in enumerate(zip(chip_sum, from_chips, blocked))]
    other_halves = _to_sibling("swap_halves", halves)

    weights = dict(norm_w=norm_w, w_in=w_in, pool_w=pool_w, pool_scale=pool_scale, conv_w=conv_w, a_log=a_log,
                   dt_bias=dt_bias, dn_norm_w=dn_norm_w, w_out=w_out, final_norm_w=final_norm_w)
    ms = dict(norm_w=m_norm_w, w_in=m_w_in, pool_w=m_pool_w, pool_scale=m_pool_scale, conv_w=m_conv_w, a_log=m_a_log,
              dt_bias=m_dt_bias, dn_norm_w=m_dn_norm_w, w_out=m_w_out, final_norm_w=m_final_norm_w)
    vs = dict(norm_w=v_norm_w, w_in=v_w_in, pool_w=v_pool_w, pool_scale=v_pool_scale, conv_w=v_conv_w, a_log=v_a_log,
              dt_bias=v_dt_bias, dn_norm_w=v_dn_norm_w, w_out=v_w_out, final_norm_w=v_final_norm_w)
    names = ["norm_w", "w_in", "pool_w", "pool_scale", "conv_w", "a_log", "dt_bias", "dn_norm_w", "w_out", "final_norm_w"]
    small_names = [n for n in names if n not in ("w_in", "w_out")]

    def pack(t):
        conv = lax.dynamic_update_slice_in_dim(jnp.zeros((CONV_K, 3 * D_HALF), F32), t["conv_w"][0], jidx * BLK_CONV, axis=1)
        return _pack_small({**{n: t[n] for n in small_names if n != "conv_w"}, "conv_w": conv})[None]

    results = [{}, {}, {}, {}]
    for i, name in enumerate(("w_in", "w_out")):
        outs = _adamw_shard("adamw_" + name, weights[name], halves[i], other_halves[i], cidx, ms[name], vs[name])
        for res, o in zip(results, outs):
            res[name] = o
    outs = _adamw_shard("adamw_small", pack(weights), halves[2], other_halves[2], cidx, pack(ms), pack(vs))
    for res, o in zip(results, outs):
        got = _unpack_small(o[0], 3 * D_HALF)
        got["conv_w"] = lax.dynamic_slice_in_dim(got["conv_w"], jidx * BLK_CONV, BLK_CONV, axis=2)
        res.update(got)
    grads, delta, new_m, new_v = results

    return (grads["loss"], gx[None], *[grads[n] for n in names], *[delta[n] for n in names],
            *[new_m[n] for n in names], *[new_v[n] for n in names])
```

```python
import functools

import jax
import jax.numpy as jnp
from jax import lax
from jax.experimental import pallas as pl
from jax.experimental.pallas import tpu as pltpu

F32 = jnp.float32
BF16 = jnp.bfloat16
I32 = jnp.int32

D_MODEL = 1024
D_HALF = 512
N_HEADS = 4
HEAD = 128
CHUNK = 64
PAIR = 2 * CHUNK
WINDOWS = (2, 4, 8, 16)
CONV_K = 4
EPS = 1e-6
N_IN = 3080
N_IN_PAD = 3200
BLK_IN = 770
BLK_IN_PAD = 896
BLK_OUT = 256
BLK_CONV = 384
COL_BA = 3072
QK_SCALE = HEAD ** -0.5
SMALL_ROWS = 592
VMEM_LIMIT = 56 * 1024 * 1024

ADAM_LR = 0.001
ADAM_B1 = 0.9
ADAM_B2 = 0.999
ADAM_EPS = 1e-08
ADAM_WD = 0.01
ADAM_STEP = 10

CHIP_MASKS = (2, 1, 3)
HEADS = range(N_HEADS)
HEAD_COLS = [slice(h * HEAD, (h + 1) * HEAD) for h in HEADS]


def _call(body, **kw):
    return pl.pallas_call(body, **kw)


def _params(*sem):
    return pltpu.CompilerParams(dimension_semantics=sem, vmem_limit_bytes=VMEM_LIMIT)


def _sds(shape, dtype=F32):
    return jax.ShapeDtypeStruct(shape, dtype)


def _bdot(a, b):
    return jnp.dot(a.astype(BF16), b.astype(BF16), preferred_element_type=F32)


def _bdot_nt(a, b):
    return lax.dot_general(a.astype(BF16), b.astype(BF16), (((1,), (1,)), ((), ())), preferred_element_type=F32)


def _bdot_tn(a, b):
    return lax.dot_general(a.astype(BF16), b.astype(BF16), (((0,), (0,)), ((), ())), preferred_element_type=F32)


def _split(a):
    hi = a.astype(BF16)
    lo = (a - hi.astype(F32)).astype(BF16)
    return hi, lo


def _mask_dot(m, b, dims=(((1,), (0,)), ((), ()))):
    bh, bl = _split(b)
    dg = functools.partial(lax.dot_general, dimension_numbers=dims, preferred_element_type=F32)
    return dg(m, bh) + dg(m, bl)


def _sigmoid(x):
    return 1.0 / (1.0 + jnp.exp(-x))


def _softplus(x):
    return jnp.maximum(x, 0.0) + jnp.log(1.0 + jnp.exp(-jnp.abs(x)))


def _rowsum(x):
    return jnp.sum(x, axis=-1, keepdims=True)


def _colsum(x):
    return jnp.sum(x, axis=0, keepdims=True)


def _shift_down(xv, prev8, k):
    r = pltpu.roll(xv, k, 0)
    q = pltpu.roll(prev8, k, 0)
    row = lax.broadcasted_iota(I32, prev8.shape, 0)
    top = jnp.where(row < k, q, r[0:8])
    return jnp.concatenate([top, r[8:]], axis=0)


def _shift_up(xv, next8, k):
    t = xv.shape[0]
    r = pltpu.roll(xv, t - k, 0)
    q = pltpu.roll(next8, 8 - k, 0)
    row = lax.broadcasted_iota(I32, next8.shape, 0)
    bot = jnp.where(row >= 8 - k, q, r[t - 8:])
    return jnp.concatenate([r[:t - 8], bot], axis=0)


def _band(rows, cols, off, w, anti=False):
    r = lax.broadcasted_iota(I32, (rows, cols), 0)
    c = lax.broadcasted_iota(I32, (rows, cols), 1)
    d = (c - r + off) if anti else (r - c + off)
    return ((d >= 0) & (d < w)).astype(BF16)


def _head(ref_or_val, h):
    return ref_or_val[:, h * HEAD:(h + 1) * HEAD]


def _heads(ref):
    return [ref[:, sl] for sl in HEAD_COLS]


def _put_heads(ref, vals):
    for sl, v in zip(HEAD_COLS, vals):
        ref[:, sl] = v.astype(ref.dtype)


def _each(fn, *lists):
    return [fn(*args) for args in zip(*lists)]


def _proj_fwd(x, norm_w, w_pad):
    s = x.shape[0]
    tm = 512

    def body(x_ref, nw_ref, w_ref, proj_ref, nt_ref):
        xv = x_ref[...]
        r = lax.rsqrt(jnp.mean(xv * xv, axis=-1, keepdims=True) + EPS)
        nv = xv * r * nw_ref[...]
        nt_ref[...] = nv.T.astype(BF16)
        proj_ref[...] = jnp.dot(nv.astype(BF16), w_ref[...], preferred_element_type=F32)

    return _call(
        body, name="proj_fwd", grid=(s // tm,),
        in_specs=[pl.BlockSpec((tm, D_MODEL), lambda i: (i, 0)),
                  pl.BlockSpec((1, D_MODEL), lambda i: (0, 0)),
                  pl.BlockSpec((D_MODEL, N_IN_PAD), lambda i: (0, 0))],
        out_specs=[pl.BlockSpec((tm, N_IN_PAD), lambda i: (i, 0)),
                   pl.BlockSpec((D_MODEL, tm), lambda i: (0, i))],
        out_shape=[_sds((s, N_IN_PAD)), _sds((D_MODEL, s), BF16)],
        compiler_params=_params("arbitrary"),
    )(x, norm_w, w_pad)


def _pool_mix(ug, hg, zg, pw_g, row0, w):
    t = ug.shape[0]
    win = _mask_dot(_band(t, t, 0, w), ug) + _mask_dot(_band(t, HEAD, HEAD, w), hg)
    cnt = jnp.minimum(row0 + lax.broadcasted_iota(I32, (t, 1), 0) + 1, w).astype(F32)
    mix = win / cnt - ug
    mixed = _bdot(mix, pw_g)
    sg = _sigmoid(zg)
    return mix, mixed, sg, cnt


POOL_T = 256


def _pool_fwd(proj, pool_w, pool_scale):
    s = proj.shape[0]
    t = POOL_T
    hb = t // HEAD

    def body(u_ref, z_ref, halo_ref, pw_ref, ps_ref, y_ref):
        i = pl.program_id(0)
        live = (i > 0).astype(F32)
        for g, w in enumerate(WINDOWS):
            sl = HEAD_COLS[g]
            zg = z_ref[:, sl]
            _, mixed, sg, _ = _pool_mix(u_ref[:, sl], halo_ref[:, sl] * live, zg, pw_ref[g], i * t, w)
            y_ref[:, sl] = mixed * ps_ref[:, sl] * (zg * sg)

    return _call(
        body, name="pool_fwd", grid=(s // t,),
        in_specs=[pl.BlockSpec((t, D_HALF), lambda i: (i, 0)),
                  pl.BlockSpec((t, D_HALF), lambda i: (i, 1)),
                  pl.BlockSpec((HEAD, D_HALF), lambda i: (jnp.maximum(i * hb - 1, 0), 0)),
                  pl.BlockSpec((N_HEADS, HEAD, HEAD), lambda i: (0, 0, 0)),
                  pl.BlockSpec((1, D_HALF), lambda i: (0, 0))],
        out_specs=pl.BlockSpec((t, D_HALF), lambda i: (i, 0)),
        out_shape=_sds((s, D_HALF)),
        compiler_params=_params("arbitrary"),
    )(proj, proj, proj, pool_w, pool_scale)


def _conv_pre(xv, prev8, cw):
    y = xv * cw[3:4]
    for sft in (1, 2, 3):
        y = y + _shift_down(xv, prev8, sft) * cw[3 - sft:4 - sft]
    return y


CONV_T = 256


def _conv_specs(t):
    tiles = [pl.BlockSpec((t, D_HALF), functools.partial(lambda i, p: (i, 2 + p), p=p)) for p in range(3)]
    halos = [pl.BlockSpec((8, D_HALF), functools.partial(lambda i, p: (jnp.maximum(i * (t // 8) - 1, 0), 2 + p), p=p))
             for p in range(3)]
    return tiles + halos


def _conv_fwd(proj, conv_w, a_log, dt_bias):
    s = proj.shape[0]
    t = CONV_T

    def body(q_ref, k_ref, v_ref, hq_ref, hk_ref, hv_ref, ba_ref, cw_ref, al_ref, dtb_ref,
             qn_ref, kn_ref, vs_ref, beta_ref, g_ref):
        live = (pl.program_id(0) > 0).astype(F32)
        for p, (x_ref, h_ref, o_ref) in enumerate(((q_ref, hq_ref, qn_ref), (k_ref, hk_ref, kn_ref), (v_ref, hv_ref, vs_ref))):
            y = _conv_pre(x_ref[...], h_ref[...] * live, cw_ref[:, p * D_HALF:(p + 1) * D_HALF])
            sv = y * _sigmoid(y)
            if p == 2:
                o_ref[...] = sv
            else:
                for h in HEADS:
                    seg = _head(sv, h)
                    o_ref[:, HEAD_COLS[h]] = seg * lax.rsqrt(_rowsum(seg * seg) + EPS)
        ba = ba_ref[...]
        for h in HEADS:
            beta = _sigmoid(ba[:, h:h + 1])
            gl = -jnp.exp(al_ref[0:1, h:h + 1]) * _softplus(ba[:, N_HEADS + h:N_HEADS + h + 1] + dtb_ref[0:1, h:h + 1])
            beta_ref[:, HEAD_COLS[h]] = jnp.broadcast_to(beta, (t, HEAD))
            g_ref[:, HEAD_COLS[h]] = jnp.broadcast_to(gl, (t, HEAD))

    row = pl.BlockSpec((t, D_HALF), lambda i: (i, 0))
    return _call(
        body, name="conv_fwd", grid=(s // t,),
        in_specs=_conv_specs(t) + [pl.BlockSpec((t, HEAD), lambda i: (i, COL_BA // HEAD)),
                                   pl.BlockSpec((CONV_K, 3 * D_HALF), lambda i: (0, 0)),
                                   pl.BlockSpec((1, N_HEADS), lambda i: (0, 0)),
                                   pl.BlockSpec((1, N_HEADS), lambda i: (0, 0))],
        out_specs=[row] * 5,
        out_shape=[_sds((s, D_HALF))] * 5,
        compiler_params=_params("arbitrary"),
    )(proj, proj, proj, proj, proj, proj, proj, conv_w, a_log, dt_bias)


def _pair_masks():
    r = lax.broadcasted_iota(I32, (PAIR, PAIR), 0)
    c = lax.broadcasted_iota(I32, (PAIR, PAIR), 1)
    same = jnp.right_shift(r, 6) == jnp.right_shift(c, 6)
    return same, same & (r >= c), same & (r > c), r == c


def _pair_common(qn, kn, vs, beta, g):
    same, incl, strict, eye = _pair_masks()
    incl_b = incl.astype(BF16)
    first = lax.broadcasted_iota(I32, (PAIR, HEAD), 0) < CHUNK
    gc = _each(lambda gv: _mask_dot(incl_b, gv), g)
    gc_row = _each(lambda v: _colsum(jnp.where(eye, v, 0.0)), gc)
    decay = _each(lambda v, r: jnp.where(incl, jnp.exp(jnp.where(incl, v - r, 0.0)), 0.0), gc, gc_row)
    gl = _each(lambda v: jnp.where(first, v[CHUNK - 1:CHUNK], v[PAIR - 1:PAIR]), gc)
    egc = _each(jnp.exp, gc)
    q = _each(lambda v: v * QK_SCALE, qn)
    kb = _each(lambda k, b: k * b, kn, beta)
    return dict(same=same, incl=incl, strict=strict, eye=eye, gc=gc, decay=decay, gl=gl, egc=egc,
                ekd=_each(lambda a, b: jnp.exp(a - b), gl, gc), cd=_each(jnp.exp, gl), q=q, kb=kb,
                vb=_each(lambda v, b: v * b, vs, beta), kbg=_each(lambda k, e: k * e, kb, egc),
                kk=_each(_bdot_nt, kb, kn), qk=_each(_bdot_nt, q, kn))


def _tri_inv(a, eye_f):
    p = _each(lambda v: eye_f - v, a)
    x = _each(_bdot, a, a)
    for it in range(5):
        p = _each(lambda pv, xv: pv + _bdot(pv, xv), p, x)
        if it < 4:
            x = _each(_bdot, x, x)
    return p


def _pair_spec():
    return pl.BlockSpec((PAIR, D_HALF), lambda i: (i, 0))


def _chunk_scalar_spec(index=lambda i: (i, 0)):
    return pl.BlockSpec((16, D_HALF), index)


def _intra_fwd(qn, kn, vs, beta, g):
    s = qn.shape[0]

    def body(qn_ref, kn_ref, vs_ref, beta_ref, g_ref, u_ref, w_ref, att_ref, qd_ref, kd_ref, t_ref, cd_ref):
        kn = _heads(kn_ref)
        cm = _pair_common(_heads(qn_ref), kn, _heads(vs_ref), _heads(beta_ref), _heads(g_ref))
        a = _each(lambda kk, d: jnp.where(cm["strict"], kk * d, 0.0), cm["kk"], cm["decay"])
        tm = _tri_inv(a, cm["eye"].astype(F32))
        _put_heads(t_ref, tm)
        _put_heads(u_ref, _each(_bdot, tm, cm["vb"]))
        _put_heads(w_ref, _each(_bdot, tm, cm["kbg"]))
        _put_heads(att_ref, _each(lambda a, b: a * b, cm["qk"], cm["decay"]))
        _put_heads(qd_ref, _each(lambda a, b: a * b, cm["q"], cm["egc"]))
        _put_heads(kd_ref, _each(lambda a, b: a * b, kn, cm["ekd"]))
        for ci in range(2):
            for sl, v in zip(HEAD_COLS, cm["cd"]):
                cd_ref[ci * 8:(ci + 1) * 8, sl] = v[ci * CHUNK:ci * CHUNK + 8]

    return _call(
        body, name="intra_fwd", grid=(s // PAIR,),
        in_specs=[_pair_spec()] * 5, out_specs=[_pair_spec()] * 6 + [_chunk_scalar_spec()],
        out_shape=[_sds((s, D_HALF))] + [_sds((s, D_HALF), BF16)] * 5 + [_sds((s // 8, D_HALF))],
        compiler_params=_params("arbitrary"),
    )(qn, kn, vs, beta, g)


def _scan_fwd(u, w, att, qd, kd, cd):
    s = u.shape[0]
    n_chunks = s // CHUNK

    def body(u_ref, w_ref, att_ref, qd_ref, kd_ref, cd_ref, o_ref, vn_ref, st_ref, state):
        @pl.when(pl.program_id(0) == 0)
        def _():
            state[...] = jnp.zeros_like(state)
        cols = list(enumerate(HEAD_COLS))
        sm = [state[h] for h in HEADS]
        qs = []
        for ci in range(2):
            rs = slice(ci * CHUNK, (ci + 1) * CHUNK)
            for h in HEADS:
                st_ref[ci, h] = sm[h]
            both = [_bdot(jnp.concatenate([w_ref[rs, sl], qd_ref[rs, sl]], axis=0), sm[h]) for h, sl in cols]
            vn = [u_ref[rs, sl] - both[h][:CHUNK] for h, sl in cols]
            qs.append([b[CHUNK:] for b in both])
            for h, sl in cols:
                vn_ref[rs, sl] = vn[h].astype(BF16)
            sm = [sm[h] * cd_ref[ci * 8:ci * 8 + 1, sl] + _bdot_tn(kd_ref[rs, sl], vn[h]) for h, sl in cols]
        for h in HEADS:
            state[h] = sm[h]
        intra = [_bdot(att_ref[:, sl], vn_ref[:, sl]) for sl in HEAD_COLS]
        for h, sl in cols:
            o_ref[:, sl] = jnp.concatenate([qs[0][h], qs[1][h]], axis=0) + intra[h]

    return _call(
        body, name="scan_fwd", grid=(s // PAIR,),
        in_specs=[_pair_spec()] * 5 + [_chunk_scalar_spec()],
        out_specs=[_pair_spec(), _pair_spec(), pl.BlockSpec((2, N_HEADS, HEAD, HEAD), lambda i: (i, 0, 0, 0))],
        out_shape=[_sds((s, D_HALF)), _sds((s, D_HALF), BF16), _sds((n_chunks, N_HEADS, HEAD, HEAD))],
        scratch_shapes=[pltpu.VMEM((N_HEADS, HEAD, HEAD), F32)],
        compiler_params=_params("arbitrary"),
    )(u, w, att, qd, kd, cd)


OUT_T = 512


def _out_fwd_bwd(x, y_pool, o, proj, target, w_out, dn_norm_w, final_norm_w):
    s = x.shape[0]
    t = OUT_T

    def body(x_ref, yp_ref, o_ref, z_ref, tg_ref, wo_ref, dnw_ref, fnw_ref,
             yt_ref, dh_ref, dyp_ref, do_ref, dz_ref, loss_ref, gfn_ref, gdn_ref, y_ref):
        @pl.when(pl.program_id(0) == 0)
        def _():
            loss_ref[...] = jnp.zeros_like(loss_ref)
            gfn_ref[...] = jnp.zeros_like(gfn_ref)
            gdn_ref[...] = jnp.zeros_like(gdn_ref)

        ypv = yp_ref[...]
        y_ref[:, :D_HALF] = ypv.astype(BF16)
        yt_ref[:D_HALF, :] = ypv.T.astype(BF16)
        dnw = dnw_ref[...]
        keep = []
        for h in HEADS:
            ov = o_ref[:, HEAD_COLS[h]]
            zv = z_ref[:, HEAD_COLS[h]]
            ro = lax.rsqrt(jnp.mean(ov * ov, axis=-1, keepdims=True) + EPS)
            ohat = ov * ro
            sg = _sigmoid(zv)
            keep.append((ro, ohat, zv, sg))
            ydn = ohat * dnw * (zv * sg)
            y_ref[:, D_HALF + h * HEAD:D_HALF + (h + 1) * HEAD] = ydn.astype(BF16)
            yt_ref[D_HALF + h * HEAD:D_HALF + (h + 1) * HEAD, :] = ydn.T.astype(BF16)

        hv = x_ref[...] + jnp.dot(y_ref[...], wo_ref[...], preferred_element_type=F32)
        r2 = lax.rsqrt(jnp.mean(hv * hv, axis=-1, keepdims=True) + EPS)
        hhat = hv * r2
        fnw = fnw_ref[...]
        err = hhat * fnw - tg_ref[...]
        loss_ref[...] += 0.5 * jnp.sum(_rowsum(err * err) * (1.0 / D_MODEL), axis=0, keepdims=True)
        dout = err * (1.0 / D_MODEL)
        gfn_ref[...] += _colsum(dout * hhat)
        dhh = dout * fnw
        dh = r2 * (dhh - hhat * jnp.mean(dhh * hhat, axis=-1, keepdims=True))
        dh_ref[...] = dh
        dy = _bdot_nt(dh, wo_ref[...])
        dyp_ref[...] = dy[:, :D_HALF]
        gdn = jnp.zeros((1, HEAD), F32)
        for h in HEADS:
            ro, ohat, zv, sg = keep[h]
            dyd = dy[:, D_HALF + h * HEAD:D_HALF + (h + 1) * HEAD]
            sz = zv * sg
            dz_ref[:, HEAD_COLS[h]] = dyd * ohat * dnw * (sg * (1.0 + zv * (1.0 - sg)))
            gdn = gdn + _colsum(dyd * ohat * sz)
            doh = dyd * dnw * sz
            do_ref[:, HEAD_COLS[h]] = ro * (doh - ohat * jnp.mean(doh * ohat, axis=-1, keepdims=True))
        gdn_ref[...] += gdn

    wide = pl.BlockSpec((t, D_MODEL), lambda i: (i, 0))
    half = pl.BlockSpec((t, D_HALF), lambda i: (i, 0))
    const = lambda shape: pl.BlockSpec(shape, lambda i: (0,) * len(shape))
    return _call(
        body, name="out_fwd_bwd", grid=(s // t,),
        in_specs=[wide, half, half, pl.BlockSpec((t, D_HALF), lambda i: (i, 5)), wide,
                  const((D_MODEL, D_MODEL)), const((1, HEAD)), const((1, D_MODEL))],
        out_specs=[pl.BlockSpec((D_MODEL, t), lambda i: (0, i)), wide, half, half, half,
                   const((1, HEAD)), const((1, D_MODEL)), const((1, HEAD))],
        out_shape=[_sds((D_MODEL, s), BF16), _sds((s, D_MODEL)), _sds((s, D_HALF)), _sds((s, D_HALF)), _sds((s, D_HALF)),
                   _sds((1, HEAD)), _sds((1, D_MODEL)), _sds((1, HEAD))],
        scratch_shapes=[pltpu.VMEM((t, D_MODEL), BF16)],
        compiler_params=_params("arbitrary"),
    )(x, y_pool, o, proj, target, w_out, dn_norm_w, final_norm_w)


def _token_matmul(name, at, pieces):
    m, s = at.shape
    n = len(pieces)
    tn, tk = D_HALF, 512

    def body(a_ref, *refs):
        p_refs, o_ref = refs[:n], refs[n]
        j = pl.program_id(0)

        @pl.when(pl.program_id(1) == 0)
        def _():
            o_ref[...] = jnp.zeros_like(o_ref)

        for p in range(n):
            @pl.when(j == p)
            def _(p=p):
                o_ref[...] += _bdot(a_ref[...], p_refs[p][...])

    piece_spec = lambda p: pl.BlockSpec((tk, tn), lambda j, k: (jnp.where(j == p, k, 0), pieces[p][1]))
    return _call(
        body, name=name, grid=(n, s // tk),
        in_specs=[pl.BlockSpec((m, tk), lambda j, k: (0, k))] + [piece_spec(p) for p in range(n)],
        out_specs=pl.BlockSpec((m, tn), lambda j, k: (0, j)),
        out_shape=_sds((m, n * tn)),
        compiler_params=_params("arbitrary", "arbitrary"),
    )(at, *[p[0] for p in pieces])


def _pool_bwd(proj, dyp, pool_w, pool_scale):
    s = proj.shape[0]
    t = POOL_T
    hb = t // HEAD
    last = s // HEAD - 1

    def body(u_ref, z_ref, halo_ref, dy_ref, zn_ref, dyn_ref, pw_ref, ps_ref, du_ref, dz_ref, gpw_ref, gps_ref):
        i = pl.program_id(0)

        @pl.when(i == 0)
        def _():
            gpw_ref[...] = jnp.zeros_like(gpw_ref)
            gps_ref[...] = jnp.zeros_like(gps_ref)

        live = (i > 0).astype(F32)
        more = (i < pl.num_programs(0) - 1).astype(F32)
        for g, w in enumerate(WINDOWS):
            sl = HEAD_COLS[g]
            zg = z_ref[:, sl]
            ps = ps_ref[:, sl]
            pw = pw_ref[g]
            mix, mixed, sg, cnt = _pool_mix(u_ref[:, sl], halo_ref[:, sl] * live, zg, pw, i * t, w)
            dyg = dy_ref[:, sl]
            sz = zg * sg
            dz_ref[:, sl] = dyg * mixed * ps * (sg * (1.0 + zg * (1.0 - sg)))
            gps_ref[:, sl] += _colsum(dyg * mixed * sz)
            dmixed = dyg * ps * sz
            gpw_ref[g] += _bdot_tn(mix, dmixed)
            dmix = _bdot_nt(dmixed, pw)
            zn = zn_ref[:, sl]
            dmix_n = _bdot_nt(dyn_ref[:, sl] * more * ps * (zn * _sigmoid(zn)), pw)
            du_ref[:, sl] = (_mask_dot(_band(t, t, 0, w, anti=True), dmix / cnt)
                             + _mask_dot(_band(t, HEAD, t, w, anti=True), dmix_n * (1.0 / w)) - dmix)

    tile = lambda col: pl.BlockSpec((t, D_HALF), lambda i: (i, col))
    below = lambda col: pl.BlockSpec((HEAD, D_HALF), lambda i: (jnp.minimum((i + 1) * hb, last), col))
    return _call(
        body, name="pool_bwd", grid=(s // t,),
        in_specs=[tile(0), tile(1), pl.BlockSpec((HEAD, D_HALF), lambda i: (jnp.maximum(i * hb - 1, 0), 0)),
                  tile(0), below(1), below(0),
                  pl.BlockSpec((N_HEADS, HEAD, HEAD), lambda i: (0, 0, 0)), pl.BlockSpec((1, D_HALF), lambda i: (0, 0))],
        out_specs=[tile(0), tile(0), pl.BlockSpec((N_HEADS, HEAD, HEAD), lambda i: (0, 0, 0)),
                   pl.BlockSpec((1, D_HALF), lambda i: (0, 0))],
        out_shape=[_sds((s, D_HALF)), _sds((s, D_HALF)), _sds((N_HEADS, HEAD, HEAD)), _sds((1, D_HALF))],
        compiler_params=_params("arbitrary"),
    )(proj, proj, proj, dyp, proj, dyp, pool_w, pool_scale)


def _scan_bwd(do, vn, qd, kd, w, att, cd, st):
    s = do.shape[0]
    n_pairs = s // PAIR

    def body(do_ref, vn_ref, qd_ref, kd_ref, w_ref, att_ref, cd_ref, st_ref,
             du_ref, dw_ref, datt_ref, dqd_ref, dkd_ref, dcd_ref, dstate):
        @pl.when(pl.program_id(0) == 0)
        def _():
            dstate[...] = jnp.zeros_like(dstate)
        _, incl, _, _ = _pair_masks()
        cols = list(enumerate(HEAD_COLS))
        dv_intra = [_bdot_tn(att_ref[:, sl], do_ref[:, sl]) for _, sl in cols]
        _put_heads(datt_ref, [jnp.where(incl, _bdot_nt(do_ref[:, sl], vn_ref[:, sl]), 0.0) for _, sl in cols])
        ds = [dstate[h] for h in HEADS]
        for ci in (1, 0):
            rs = slice(ci * CHUNK, (ci + 1) * CHUNK)
            sm = [st_ref[ci, h] for h in HEADS]
            dvn = [dv_intra[h][rs] + _bdot(kd_ref[rs, sl], ds[h]) for h, sl in cols]
            for h, sl in cols:
                du_ref[rs, sl] = dvn[h].astype(BF16)
            dqd = [_bdot_nt(do_ref[rs, sl], sm[h]) for h, sl in cols]
            dw = [-_bdot_nt(dvn[h], sm[h]) for h, _ in cols]
            dkd = [_bdot_nt(vn_ref[rs, sl], ds[h]) for h, sl in cols]
            dcd = [jnp.broadcast_to(_rowsum(_colsum(ds[h] * sm[h])), (8, HEAD)) for h in HEADS]
            for h, sl in cols:
                dqd_ref[rs, sl] = dqd[h]
                dw_ref[rs, sl] = dw[h].astype(BF16)
                dkd_ref[rs, sl] = dkd[h]
                dcd_ref[ci * 8:(ci + 1) * 8, sl] = dcd[h]
            ds = [ds[h] * cd_ref[ci * 8:ci * 8 + 1, sl] + _bdot_tn(qd_ref[rs, sl], do_ref[rs, sl])
                  - _bdot_tn(w_ref[rs, sl], dvn[h]) for h, sl in cols]
        for h in HEADS:
            dstate[h] = ds[h]

    rev = pl.BlockSpec((PAIR, D_HALF), lambda i: (n_pairs - 1 - i, 0))
    rev_scalar = _chunk_scalar_spec(lambda i: (n_pairs - 1 - i, 0))
    return _call(
        body, name="scan_bwd", grid=(n_pairs,),
        in_specs=[rev] * 6 + [rev_scalar, pl.BlockSpec((2, N_HEADS, HEAD, HEAD), lambda i: (n_pairs - 1 - i, 0, 0, 0))],
        out_specs=[rev] * 5 + [rev_scalar],
        out_shape=[_sds((s, D_HALF), BF16)] * 2 + [_sds((s, D_HALF))] * 3 + [_sds((s // 8, D_HALF))],
        scratch_shapes=[pltpu.VMEM((N_HEADS, HEAD, HEAD), F32)],
        compiler_params=_params("arbitrary"),
    )(do, vn, qd, kd, w, att, cd, st)


def _intra_bwd(qn, kn, vs, beta, g, tm, du, dw, datt, dqd, dkd, dcd):
    s = qn.shape[0]

    def body(qn_ref, kn_ref, vs_ref, beta_ref, g_ref, t_ref, du_ref, dw_ref, datt_ref, dqd_ref, dkd_ref, dcd_ref,
             dqn_ref, dkn_ref, dvs_ref, dbeta_ref, dg_ref):
        ones = jnp.ones((PAIR, HEAD), BF16)
        tn = (((0,), (0,)), ((), ()))
        kn, vs, beta = _heads(kn_ref), _heads(vs_ref), _heads(beta_ref)
        cm = _pair_common(_heads(qn_ref), kn, vs, beta, _heads(g_ref))
        tmv, duv, dwv, dattv, dqdv, dkdv = (_heads(r) for r in (t_ref, du_ref, dw_ref, datt_ref, dqd_ref, dkd_ref))
        dvb = _each(_bdot_tn, tmv, duv)
        dt = _each(lambda a, b, c, d: _bdot_nt(a, b) + _bdot_nt(c, d), duv, cm["vb"], dwv, cm["kbg"])
        dkbg = _each(_bdot_tn, tmv, dwv)
        m1 = _each(_bdot_tn, tmv, dt)
        da = _each(lambda a, b: -jnp.where(cm["strict"], _bdot_nt(a, b), 0.0), m1, tmv)
        dkk = _each(lambda a, b: a * b, da, cm["decay"])
        dqk = _each(lambda a, b: a * b, dattv, cm["decay"])
        dd = _each(lambda a, b, c, d: a * b + c * d, dkk, cm["kk"], dqk, cm["qk"])
        dkb = _each(lambda a, b, c, d: _bdot(a, b) + c * d, dkk, kn, dkbg, cm["egc"])
        dq = _each(lambda a, b, c, d: _bdot(a, b) + c * d, dqk, kn, dqdv, cm["egc"])
        dkn = _each(lambda a, b, c, d: _bdot_tn(a, b) + _bdot_tn(c, d), dkk, cm["kb"], dqk, cm["q"])
        dkn = _each(lambda a, b, c, d, e: a + b * c + d * e, dkn, dkdv, cm["ekd"], dkb, beta)
        t_kd = _each(lambda a, b, c: _rowsum(a * b * c), dkdv, kn, cm["ekd"])
        split = _each(_split, dd)
        rows_dd = [jnp.dot(hi, ones, preferred_element_type=F32) + jnp.dot(lo, ones, preferred_element_type=F32)
                   for hi, lo in split]
        cols_dd = [lax.dot_general(hi, ones, tn, preferred_element_type=F32)
                   + lax.dot_general(lo, ones, tn, preferred_element_type=F32) for hi, lo in split]
        dgc = _each(lambda r, c, a, b, e, f, k, t: r - c + _rowsum(a * b * e) + _rowsum(f * k) - t,
                    rows_dd, cols_dd, dqdv, cm["q"], cm["egc"], dkbg, cm["kbg"], t_kd)
        same_b = cm["same"].astype(BF16)
        rowi = lax.broadcasted_iota(I32, (PAIR, HEAD), 0)
        dcd = _each(lambda d: jnp.where(rowi < CHUNK, d[0:1], d[8:9]), _heads(dcd_ref))
        dgl = _each(lambda t, d, c: _mask_dot(same_b, jnp.broadcast_to(t, (PAIR, HEAD))) + d * c, t_kd, dcd, cm["cd"])
        is_last = jnp.bitwise_and(rowi, CHUNK - 1) == CHUNK - 1
        dgc = _each(lambda a, b: a + jnp.where(is_last, b, 0.0), dgc, dgl)
        r = lax.broadcasted_iota(I32, (PAIR, PAIR), 0)
        c = lax.broadcasted_iota(I32, (PAIR, PAIR), 1)
        upper_b = (cm["same"] & (r <= c)).astype(BF16)
        _put_heads(dg_ref, _each(lambda v: _mask_dot(upper_b, v), dgc))
        _put_heads(dbeta_ref, _each(lambda a, b, c, d: jnp.broadcast_to(_rowsum(a * b) + _rowsum(c * d), (PAIR, HEAD)),
                                    dkb, kn, dvb, vs))
        _put_heads(dqn_ref, _each(lambda v: v * QK_SCALE, dq))
        _put_heads(dkn_ref, dkn)
        _put_heads(dvs_ref, _each(lambda a, b: a * b, dvb, beta))

    return _call(
        body, name="intra_bwd", grid=(s // PAIR,),
        in_specs=[_pair_spec()] * 11 + [_chunk_scalar_spec()], out_specs=[_pair_spec()] * 5,
        out_shape=[_sds((s, D_HALF))] * 5,
        compiler_params=_params("arbitrary"),
    )(qn, kn, vs, beta, g, tm, du, dw, datt, dqd, dkd, dcd)


def _conv_bwd_pre(proj, conv_w, a_log, dt_bias, dqn, dkn, dvs, dbeta, dg):
    s = proj.shape[0]
    t = CONV_T

    def body(q_ref, k_ref, v_ref, hq_ref, hk_ref, hv_ref, ba_ref, cw_ref, al_ref, dtb_ref,
             dqn_ref, dkn_ref, dvs_ref, dbeta_ref, dg_ref, dyq_ref, dyk_ref, dyv_ref, dba_ref, gcw_ref, gsm_ref):
        @pl.when(pl.program_id(0) == 0)
        def _():
            gcw_ref[...] = jnp.zeros_like(gcw_ref)
            gsm_ref[...] = jnp.zeros_like(gsm_ref)

        live = (pl.program_id(0) > 0).astype(F32)
        parts = ((q_ref, hq_ref, dqn_ref, dyq_ref), (k_ref, hk_ref, dkn_ref, dyk_ref), (v_ref, hv_ref, dvs_ref, dyv_ref))
        for p, (x_ref, h_ref, d_ref, dy_ref) in enumerate(parts):
            cols = slice(p * D_HALF, (p + 1) * D_HALF)
            xv = x_ref[...]
            prev = h_ref[...] * live
            y = _conv_pre(xv, prev, cw_ref[:, cols])
            sg = _sigmoid(y)
            sv = y * sg
            if p == 2:
                ds = d_ref[...]
            else:
                segs = []
                for h in HEADS:
                    seg = _head(sv, h)
                    rn = lax.rsqrt(_rowsum(seg * seg) + EPS)
                    nrm = seg * rn
                    dn = d_ref[:, HEAD_COLS[h]]
                    segs.append(rn * (dn - nrm * _rowsum(dn * nrm)))
                ds = jnp.concatenate(segs, axis=1)
            dy = ds * (sg * (1.0 + y * (1.0 - sg)))
            dy_ref[...] = dy
            gcw_ref[3:4, cols] += _colsum(dy * xv)
            for sft in (1, 2, 3):
                gcw_ref[3 - sft:4 - sft, cols] += _colsum(dy * _shift_down(xv, prev, sft))

        ba = ba_ref[...]
        lane = lax.broadcasted_iota(I32, (t, HEAD), 1)
        lane1 = lax.broadcasted_iota(I32, (1, HEAD), 1)
        dba = jnp.zeros((t, HEAD), F32)
        gsm = jnp.zeros((1, HEAD), F32)
        for h in HEADS:
            beta = _sigmoid(ba[:, h:h + 1])
            dbeta = dbeta_ref[:, h * HEAD:h * HEAD + 1]
            xg = ba[:, N_HEADS + h:N_HEADS + h + 1] + dtb_ref[0:1, h:h + 1]
            nexp = -jnp.exp(al_ref[0:1, h:h + 1])
            dgv = dg_ref[:, h * HEAD:h * HEAD + 1]
            da = dgv * nexp * _sigmoid(xg)
            dba = dba + jnp.where(lane == h, dbeta * beta * (1.0 - beta), 0.0) + jnp.where(lane == N_HEADS + h, da, 0.0)
            gsm = (gsm + jnp.where(lane1 == h, _colsum(dgv * nexp * _softplus(xg)), 0.0)
                   + jnp.where(lane1 == N_HEADS + h, _colsum(da), 0.0))
        dba_ref[...] = jnp.zeros_like(dba_ref)
        dba_ref[:, :HEAD] = dba
        gsm_ref[0:1, :] += gsm

    row = pl.BlockSpec((t, D_HALF), lambda i: (i, 0))
    return _call(
        body, name="conv_bwd_pre", grid=(s // t,),
        in_specs=_conv_specs(t) + [pl.BlockSpec((t, HEAD), lambda i: (i, COL_BA // HEAD)),
                                   pl.BlockSpec((CONV_K, 3 * D_HALF), lambda i: (0, 0)),
                                   pl.BlockSpec((1, N_HEADS), lambda i: (0, 0)),
                                   pl.BlockSpec((1, N_HEADS), lambda i: (0, 0))] + [row] * 5,
        out_specs=[row, row, row, row,
                   pl.BlockSpec((8, 3 * D_HALF), lambda i: (0, 0)), pl.BlockSpec((8, HEAD), lambda i: (0, 0))],
        out_shape=[_sds((s, D_HALF))] * 4 + [_sds((8, 3 * D_HALF)), _sds((8, HEAD))],
        compiler_params=_params("arbitrary"),
    )(proj, proj, proj, proj, proj, proj, proj, conv_w, a_log, dt_bias, dqn, dkn, dvs, dbeta, dg)


def _conv_bwd_in(dyq, dyk, dyv, conv_w):
    s = dyq.shape[0]
    t = CONV_T
    last = s // 8 - 1

    def body(q_ref, k_ref, v_ref, nq_ref, nk_ref, nv_ref, cw_ref, oq_ref, ok_ref, ov_ref):
        more = (pl.program_id(0) < pl.num_programs(0) - 1).astype(F32)
        for p, (d_ref, n_ref, o_ref) in enumerate(((q_ref, nq_ref, oq_ref), (k_ref, nk_ref, ok_ref), (v_ref, nv_ref, ov_ref))):
            cw = cw_ref[:, p * D_HALF:(p + 1) * D_HALF]
            dy = d_ref[...]
            nxt = n_ref[...] * more
            acc = dy * cw[3:4]
            for sft in (1, 2, 3):
                acc = acc + _shift_up(dy, nxt, sft) * cw[3 - sft:4 - sft]
            o_ref[...] = acc

    row = pl.BlockSpec((t, D_HALF), lambda i: (i, 0))
    nxt = pl.BlockSpec((8, D_HALF), lambda i: (jnp.minimum((i + 1) * (t // 8), last), 0))
    return _call(
        body, name="conv_bwd_in", grid=(s // t,),
        in_specs=[row] * 3 + [nxt] * 3 + [pl.BlockSpec((CONV_K, 3 * D_HALF), lambda i: (0, 0))],
        out_specs=[row] * 3, out_shape=[_sds((s, D_HALF))] * 3,
        compiler_params=_params("arbitrary"),
    )(dyq, dyk, dyv, dyq, dyk, dyv, conv_w)


IN_T = 512


def _in_bwd(x, dh, norm_w, w_pad, pieces):
    s = x.shape[0]
    t = IN_T
    widths = [D_HALF] * 6 + [N_IN_PAD - COL_BA]

    def body(*refs):
        x_ref, dh_ref, nw_ref, w_ref = refs[:4]
        p_refs = refs[4:4 + len(pieces)]
        gx_ref, gnw_ref = refs[4 + len(pieces):]

        @pl.when(pl.program_id(0) == 0)
        def _():
            gnw_ref[...] = jnp.zeros_like(gnw_ref)

        dn = jnp.zeros((t, D_MODEL), F32)
        col = 0
        for p_ref, wd in zip(p_refs, widths):
            dn = dn + _bdot_nt(p_ref[...], w_ref[:, col:col + wd])
            col += wd
        xv = x_ref[...]
        r = lax.rsqrt(jnp.mean(xv * xv, axis=-1, keepdims=True) + EPS)
        xhat = xv * r
        gnw_ref[...] += _colsum(dn * xhat)
        dxh = dn * nw_ref[...]
        gx_ref[...] = dh_ref[...] + r * (dxh - xhat * jnp.mean(dxh * xhat, axis=-1, keepdims=True))

    wide = pl.BlockSpec((t, D_MODEL), lambda i: (i, 0))
    return _call(
        body, name="in_bwd", grid=(s // t,),
        in_specs=[wide, wide, pl.BlockSpec((1, D_MODEL), lambda i: (0, 0)),
                  pl.BlockSpec((D_MODEL, N_IN_PAD), lambda i: (0, 0))]
                 + [pl.BlockSpec((t, wd), lambda i: (i, 0)) for wd in widths],
        out_specs=[wide, pl.BlockSpec((1, D_MODEL), lambda i: (0, 0))],
        out_shape=[_sds((s, D_MODEL)), _sds((1, D_MODEL))],
        compiler_params=_params("arbitrary"),
    )(x, dh, norm_w, w_pad, *pieces)


def _adamw_shard(name, w, g_own, g_got, cidx, m, v):
    _, r, c = w.shape
    half = r // 2
    rows = 256 if half % 256 == 0 else half
    per_half = half // rows

    def body(c_ref, w_ref, go_ref, gg_ref, m_ref, v_ref, gout_ref, d_ref, nm_ref, nv_ref):
        mine = (pl.program_id(0) // per_half) == c_ref[0]
        gv = jnp.where(mine, go_ref[:, :c], gg_ref[:, :c])
        gout_ref[0] = gv
        mn = ADAM_B1 * m_ref[0] + (1.0 - ADAM_B1) * gv
        vn = ADAM_B2 * v_ref[0] + (1.0 - ADAM_B2) * (gv * gv)
        m_hat = mn / (1.0 - ADAM_B1 ** ADAM_STEP)
        v_hat = vn / (1.0 - ADAM_B2 ** ADAM_STEP)
        d_ref[0] = -ADAM_LR * (m_hat / (jnp.sqrt(v_hat) + ADAM_EPS) + ADAM_WD * w_ref[0])
        nm_ref[0] = mn
        nv_ref[0] = vn

    blk = pl.BlockSpec((1, rows, c), lambda i, c_ref: (0, i, 0))
    gblk = pl.BlockSpec((rows, g_own.shape[1]), lambda i, c_ref: (i % per_half, 0))
    return _call(
        body, name=name,
        grid_spec=pltpu.PrefetchScalarGridSpec(
            num_scalar_prefetch=1, grid=(2 * per_half,),
            in_specs=[blk, gblk, gblk, blk, blk], out_specs=[blk] * 4),
        out_shape=[_sds((1, r, c))] * 4,
        compiler_params=_params("arbitrary"),
    )(cidx, w, g_own, g_got, m, v)


def _exchange(name, inputs, out_shapes, phases):
    n_in = len(inputs)
    n_out = len(out_shapes)
    n_cp = sum(len(p) for p in phases)

    def body(*refs):
        ins, outs = refs[:n_in], refs[n_in:n_in + n_out]
        send, recv = refs[n_in + n_out:]
        pos = (lax.axis_index("x"), lax.axis_index("y"), lax.axis_index("c"))
        k = 0
        for phase in phases:
            cps = []
            for src, dst, target in phase:
                cps.append(pltpu.make_async_remote_copy(
                    src_ref=src(ins, outs, pos), dst_ref=dst(ins, outs, pos), send_sem=send.at[k], recv_sem=recv.at[k],
                    device_id=target(pos), device_id_type=pl.DeviceIdType.MESH))
                k += 1
            for cp in cps:
                cp.start()
            for cp in cps:
                cp.wait()

    anyspec = pl.BlockSpec(memory_space=pl.ANY)
    return _call(
        body, name=name,
        in_specs=[anyspec] * n_in, out_specs=[anyspec] * n_out, out_shape=list(out_shapes),
        scratch_shapes=[pltpu.SemaphoreType.DMA((n_cp,)), pltpu.SemaphoreType.DMA((n_cp,))],
    )(*inputs)


def _chip(pos):
    return 2 * pos[0] + pos[1]


def _other_chip(pos, mask):
    x, y, c = pos
    return (x ^ (mask >> 1), y ^ (mask & 1), c)


def _sibling(pos):
    return (pos[0], pos[1], 1 - pos[2])


def _gather_weights(wb, ob, cb):
    halves = (wb.shape[0] // 2, ob.shape[0] // 2)

    def half(a, pos):
        return pl.ds(pos[2] * halves[a], halves[a])

    first, second = [], []
    for mask in CHIP_MASKS:
        for a in (0, 1):
            first.append((lambda ins, outs, pos, a=a: ins[a].at[half(a, pos)],
                          lambda ins, outs, pos, a=a: outs[a].at[_chip(pos), half(a, pos)],
                          functools.partial(_other_chip, mask=mask)))
            second.append((lambda ins, outs, pos, a=a, mask=mask: outs[a].at[_chip(pos) ^ mask, half(a, pos)],
                           lambda ins, outs, pos, a=a, mask=mask: outs[a].at[_chip(pos) ^ mask, half(a, pos)],
                           _sibling))
        first.append((lambda ins, outs, pos: ins[2],
                      lambda ins, outs, pos: outs[2].at[_chip(pos)],
                      functools.partial(_other_chip, mask=mask)))
    return _exchange("gather_weights", [wb, ob, cb],
                     [_sds((4,) + wb.shape, wb.dtype), _sds((4,) + ob.shape, ob.dtype), _sds((4,) + cb.shape, cb.dtype)],
                     [first, second])


def _to_sibling_half(name, arrays):
    def src(ins, outs, pos, a):
        h = arrays[a].shape[-2] // 2
        sl = pl.ds((1 - pos[2]) * h, h)
        return ins[a].at[:, sl] if arrays[a].ndim == 3 else ins[a].at[sl]

    outs = [_sds(a.shape[:-2] + (a.shape[-2] // 2, a.shape[-1]), a.dtype) for a in arrays]
    phase = [(functools.partial(src, a=a), lambda ins, outs, pos, a=a: outs[a], _sibling) for a in range(len(arrays))]
    return _exchange(name, arrays, outs, [phase])


def _add_half(name, full, part, cidx):
    shape = part.shape
    lead = shape[0] if len(shape) == 3 else 1
    rows, cols = shape[-2], shape[-1]
    tr = rows // 2 if rows % 16 == 0 else rows
    nr = rows // tr
    f3 = full.reshape((lead,) + full.shape[-2:])
    p3 = part.reshape((lead, rows, cols))

    def body(c_ref, f_ref, p_ref, o_ref):
        o_ref[...] = (f_ref[...].astype(F32) + p_ref[...].astype(F32)).astype(o_ref.dtype)

    out = _call(
        body, name=name,
        grid_spec=pltpu.PrefetchScalarGridSpec(
            num_scalar_prefetch=1, grid=(lead, nr),
            in_specs=[pl.BlockSpec((1, tr, cols), lambda b, r, c_ref: (b, c_ref[0] * nr + r, 0)),
                      pl.BlockSpec((1, tr, cols), lambda b, r, c_ref: (b, r, 0))],
            out_specs=pl.BlockSpec((1, tr, cols), lambda b, r, c_ref: (b, r, 0))),
        out_shape=_sds((lead, rows, cols), part.dtype),
        compiler_params=_params("arbitrary", "arbitrary"),
    )(cidx, f3, p3)
    return out.reshape(shape)


def _to_other_chips(name, arrays, blocked):
    def src(ins, outs, pos, a, mask):
        return ins[a].at[_chip(pos) ^ mask] if blocked[a] else ins[a]

    outs = [_sds((3,) + (a.shape[1:] if b else a.shape), a.dtype) for a, b in zip(arrays, blocked)]
    phase = []
    for mi, mask in enumerate(CHIP_MASKS):
        for a in range(len(arrays)):
            phase.append((functools.partial(src, a=a, mask=mask), lambda ins, outs, pos, a=a, mi=mi: outs[a].at[mi],
                          functools.partial(_other_chip, mask=mask)))
    return _exchange(name, arrays, outs, [phase])


def _add_chips(name, own, got, jidx, blocked):
    rows, cols = got.shape[-2:]
    tr = rows // 2 if rows % 16 == 0 else rows
    nr = rows // tr
    o3 = own if blocked else own.reshape((1, rows, cols))

    def body(j_ref, o_ref, g_ref, out_ref):
        out_ref[...] = ((o_ref[0].astype(F32) + g_ref[0].astype(F32))
                        + (g_ref[1].astype(F32) + g_ref[2].astype(F32)))

    own_map = (lambda r, j_ref: (j_ref[0], r, 0)) if blocked else (lambda r, j_ref: (0, r, 0))
    return _call(
        body, name=name,
        grid_spec=pltpu.PrefetchScalarGridSpec(
            num_scalar_prefetch=1, grid=(nr,),
            in_specs=[pl.BlockSpec((1, tr, cols), own_map),
                      pl.BlockSpec((3, tr, cols), lambda r, j_ref: (0, r, 0))],
            out_specs=pl.BlockSpec((tr, cols), lambda r, j_ref: (r, 0))),
        out_shape=_sds((rows, cols)),
        compiler_params=_params("arbitrary"),
    )(jidx, o3, got)


def _to_sibling(name, arrays):
    phase = [(lambda ins, outs, pos, a=a: ins[a], lambda ins, outs, pos, a=a: outs[a], _sibling)
             for a in range(len(arrays))]
    return _exchange(name, arrays, [_sds(a.shape, a.dtype) for a in arrays], [phase])


def _local_step(x, target, w_pad, w_out, conv_w, norm_w, pool_w, pool_scale, a_log, dt_bias, dn_norm_w, final_norm_w):
    proj, n_t = _proj_fwd(x, norm_w, w_pad)
    y_pool = _pool_fwd(proj, pool_w, pool_scale)
    qn, kn, vs, beta, g = _conv_fwd(proj, conv_w, a_log, dt_bias)
    u, w, att, qd, kd, tm, cd = _intra_fwd(qn, kn, vs, beta, g)
    o, vn, st = _scan_fwd(u, w, att, qd, kd, cd)
    y_t, dh, dyp, do, ddz, loss, g_fnw, g_dnw = _out_fwd_bwd(x, y_pool, o, proj, target, w_out, dn_norm_w, final_norm_w)
    g_wout = _token_matmul("grad_w_out", y_t, [(dh, 0), (dh, 1)])
    dpu, dpz, g_pw, g_ps = _pool_bwd(proj, dyp, pool_w, pool_scale)
    du, dw, datt, dqd, dkd, dcd = _scan_bwd(do, vn, qd, kd, w, att, cd, st)
    dqn, dkn, dvs, dbeta, dg = _intra_bwd(qn, kn, vs, beta, g, tm, du, dw, datt, dqd, dkd, dcd)
    dyq, dyk, dyv, dba, g_cw, g_sm = _conv_bwd_pre(proj, conv_w, a_log, dt_bias, dqn, dkn, dvs, dbeta, dg)
    dcq, dck, dcv = _conv_bwd_in(dyq, dyk, dyv, conv_w)
    pieces = [dpu, dpz, dcq, dck, dcv, ddz, dba]
    gx, g_nw = _in_bwd(x, dh, norm_w, w_pad, pieces)
    g_win = _token_matmul("grad_w_in", n_t, [(p, 0) for p in pieces])
    small = dict(norm_w=g_nw, pool_w=g_pw, pool_scale=g_ps, conv_w=g_cw[:CONV_K], a_log=g_sm[0:1, 0:N_HEADS],
                 dt_bias=g_sm[0:1, N_HEADS:2 * N_HEADS], dn_norm_w=g_dnw, final_norm_w=g_fnw)
    return loss[0, 0], gx, g_win, g_wout, small


def _pack_small(t):
    lanes = lambda a: jnp.pad(a.reshape(1, -1), ((0, 0), (0, HEAD - a.size)))
    rows = [t["pool_w"].reshape(-1, HEAD), t["norm_w"].reshape(-1, HEAD), t["final_norm_w"].reshape(-1, HEAD),
            t["pool_scale"].reshape(-1, HEAD), t["conv_w"].reshape(-1, HEAD), t["dn_norm_w"].reshape(1, HEAD),
            lanes(t["a_log"]), lanes(t["dt_bias"]), lanes(t.get("loss", jnp.zeros((1,), F32)))]
    buf = jnp.concatenate(rows, axis=0)
    return jnp.pad(buf, ((0, SMALL_ROWS - buf.shape[0]), (0, 0)))


def _unpack_small(buf, conv_cols):
    out, r = {}, 0
    for name, nrow, shape in (("pool_w", 512, (1, N_HEADS, HEAD, HEAD)), ("norm_w", 8, (1, D_MODEL)),
                              ("final_norm_w", 8, (D_MODEL,)), ("pool_scale", 4, (1, D_HALF)),
                              ("conv_w", CONV_K * conv_cols // HEAD, (1, CONV_K, conv_cols)), ("dn_norm_w", 1, (1, HEAD))):
        out[name] = buf[r:r + nrow].reshape(shape)
        r += nrow
    out["a_log"] = buf[r:r + 1, :N_HEADS]
    out["dt_bias"] = buf[r + 1:r + 2, :N_HEADS]
    out["loss"] = buf[r + 2, 0]
    return out


def kernel(x, norm_w, w_in, pool_w, pool_scale, conv_w, a_log, dt_bias, dn_norm_w, w_out, final_norm_w, loss_target, m_norm_w, m_w_in, m_pool_w, m_pool_scale, m_conv_w, m_a_log, m_dt_bias, m_dn_norm_w, m_w_out, m_final_norm_w, v_norm_w, v_w_in, v_pool_w, v_pool_scale, v_conv_w, v_a_log, v_dt_bias, v_dn_norm_w, v_w_out, v_final_norm_w):
    cidx = lax.axis_index("c").astype(I32).reshape(1)
    jidx = (2 * lax.axis_index("x") + lax.axis_index("y")).astype(I32)

    wb = jnp.pad(w_in[0].astype(BF16), ((0, 0), (0, BLK_IN_PAD - BLK_IN)))
    ob = w_out[0].astype(BF16)
    gw, go, gc = _gather_weights(wb, ob, conv_w[0])
    mine = lambda j: jidx == j
    w_pad = jnp.concatenate([jnp.where(mine(j), wb[:, :BLK_IN], gw[j, :, :BLK_IN]) for j in range(4)]
                            + [jnp.zeros((D_MODEL, N_IN_PAD - N_IN), BF16)], axis=1)
    wo_full = jnp.where((jnp.arange(4) == jidx)[:, None, None], ob[None], go).reshape(D_MODEL, D_MODEL)
    cw_full = jnp.concatenate([jnp.where(mine(j), conv_w[0], gc[j]) for j in range(4)], axis=1)

    loss, gx, g_win, g_wout, small = _local_step(
        x[0], loss_target[0], w_pad, wo_full, cw_full, norm_w, pool_w[0], pool_scale, a_log, dt_bias, dn_norm_w,
        final_norm_w.reshape(1, D_MODEL))
    small["loss"] = loss

    blocks_in = jnp.stack([jnp.pad(g_win[:, j * BLK_IN:(j + 1) * BLK_IN].astype(BF16), ((0, 0), (0, BLK_IN_PAD - BLK_IN)))
                           for j in range(4)])
    blocks_out = g_wout.astype(BF16).reshape(4, BLK_OUT, D_MODEL)
    full = [blocks_in, blocks_out, _pack_small(small)]
    from_sib = _to_sibling_half("reduce_sibling", full)
    chip_sum = [_add_half("add_sibling_%d" % i, f, p, cidx) for i, (f, p) in enumerate(zip(full, from_sib))]
    blocked = [True, True, False]
    from_chips = _to_other_chips("reduce_chips", chip_sum, blocked)
    halves = [_add_chips("add_chips_%d" % i, o, g, jidx.reshape(1), b)
              for i, (o, g, b) in enumerate(zip(chip_sum, from_chips, blocked))]
    other_halves = _to_sibling("swap_halves", halves)

    weights = dict(norm_w=norm_w, w_in=w_in, pool_w=pool_w, pool_scale=pool_scale, conv_w=conv_w, a_log=a_log,
                   dt_bias=dt_bias, dn_norm_w=dn_norm_w, w_out=w_out, final_norm_w=final_norm_w)
    ms = dict(norm_w=m_norm_w, w_in=m_w_in, pool_w=m_pool_w, pool_scale=m_pool_scale, conv_w=m_conv_w, a_log=m_a_log,
              dt_bias=m_dt_bias, dn_norm_w=m_dn_norm_w, w_out=m_w_out, final_norm_w=m_final_norm_w)
    vs = dict(norm_w=v_norm_w, w_in=v_w_in, pool_w=v_pool_w, pool_scale=v_pool_scale, conv_w=v_conv_w, a_log=v_a_log,
              dt_bias=v_dt_bias, dn_norm_w=v_dn_norm_w, w_out=v_w_out, final_norm_w=v_final_norm_w)
    names = ["norm_w", "w_in", "pool_w", "pool_scale", "conv_w", "a_log", "dt_bias", "dn_norm_w", "w_out", "final_norm_w"]
    small_names = [n for n in names if n not in ("w_in", "w_out")]

    def pack(t):
        conv = lax.dynamic_update_slice_in_dim(jnp.zeros((CONV_K, 3 * D_HALF), F32), t["conv_w"][0], jidx * BLK_CONV, axis=1)
        return _pack_small({**{n: t[n] for n in small_names if n != "conv_w"}, "conv_w": conv})[None]

    results = [{}, {}, {}, {}]
    for i, name in enumerate(("w_in", "w_out")):
        outs = _adamw_shard("adamw_" + name, weights[name], halves[i], other_halves[i], cidx, ms[name], vs[name])
        for res, o in zip(results, outs):
            res[name] = o
    outs = _adamw_shard("adamw_small", pack(weights), halves[2], other_halves[2], cidx, pack(ms), pack(vs))
    for res, o in zip(results, outs):
        got = _unpack_small(o[0], 3 * D_HALF)
        got["conv_w"] = lax.dynamic_slice_in_dim(got["conv_w"], jidx * BLK_CONV, BLK_CONV, axis=2)
        res.update(got)
    grads, delta, new_m, new_v = results

    return (grads["loss"], gx[None], *[grads[n] for n in names], *[delta[n] for n in names],
            *[new_m[n] for n in names], *[new_v[n] for n in names])
```

```python
import functools

import jax
import jax.numpy as jnp
from jax import lax
from jax.experimental import pallas as pl
from jax.experimental.pallas import tpu as pltpu

F32 = jnp.float32
BF16 = jnp.bfloat16
I32 = jnp.int32

D_MODEL = 1024
D_HALF = 512
N_HEADS = 4
HEAD = 128
CHUNK = 64
PAIR = 2 * CHUNK
WINDOWS = (2, 4, 8, 16)
CONV_K = 4
EPS = 1e-6
N_IN = 3080
N_IN_PAD = 3200
BLK_IN = 770
BLK_IN_PAD = 896
BLK_OUT = 256
BLK_CONV = 384
COL_BA = 3072
QK_SCALE = HEAD ** -0.5
SMALL_ROWS = 592
VMEM_LIMIT = 56 * 1024 * 1024

ADAM_LR = 0.001
ADAM_B1 = 0.9
ADAM_B2 = 0.999
ADAM_EPS = 1e-08
ADAM_WD = 0.01
ADAM_STEP = 10

CHIP_MASKS = (2, 1, 3)
HEADS = range(N_HEADS)
HEAD_COLS = [slice(h * HEAD, (h + 1) * HEAD) for h in HEADS]


def _call(body, **kw):
    return pl.pallas_call(body, **kw)


def _params(*sem):
    return pltpu.CompilerParams(dimension_semantics=sem, vmem_limit_bytes=VMEM_LIMIT)


def _sds(shape, dtype=F32):
    return jax.ShapeDtypeStruct(shape, dtype)


def _bdot(a, b):
    return jnp.dot(a.astype(BF16), b.astype(BF16), preferred_element_type=F32)


def _bdot_nt(a, b):
    return lax.dot_general(a.astype(BF16), b.astype(BF16), (((1,), (1,)), ((), ())), preferred_element_type=F32)


def _bdot_tn(a, b):
    return lax.dot_general(a.astype(BF16), b.astype(BF16), (((0,), (0,)), ((), ())), preferred_element_type=F32)


def _split(a):
    hi = a.astype(BF16)
    lo = (a - hi.astype(F32)).astype(BF16)
    return hi, lo


def _mask_dot(m, b, dims=(((1,), (0,)), ((), ()))):
    bh, bl = _split(b)
    dg = functools.partial(lax.dot_general, dimension_numbers=dims, preferred_element_type=F32)
    return dg(m, bh) + dg(m, bl)


def _sigmoid(x):
    return 1.0 / (1.0 + jnp.exp(-x))


def _softplus(x):
    return jnp.maximum(x, 0.0) + jnp.log(1.0 + jnp.exp(-jnp.abs(x)))


def _rowsum(x):
    return jnp.sum(x, axis=-1, keepdims=True)


def _colsum(x):
    return jnp.sum(x, axis=0, keepdims=True)


def _shift_down(xv, prev8, k):
    r = pltpu.roll(xv, k, 0)
    q = pltpu.roll(prev8, k, 0)
    row = lax.broadcasted_iota(I32, prev8.shape, 0)
    top = jnp.where(row < k, q, r[0:8])
    return jnp.concatenate([top, r[8:]], axis=0)


def _shift_up(xv, next8, k):
    t = xv.shape[0]
    r = pltpu.roll(xv, t - k, 0)
    q = pltpu.roll(next8, 8 - k, 0)
    row = lax.broadcasted_iota(I32, next8.shape, 0)
    bot = jnp.where(row >= 8 - k, q, r[t - 8:])
    return jnp.concatenate([r[:t - 8], bot], axis=0)


def _band(rows, cols, off, w, anti=False):
    r = lax.broadcasted_iota(I32, (rows, cols), 0)
    c = lax.broadcasted_iota(I32, (rows, cols), 1)
    d = (c - r + off) if anti else (r - c + off)
    return ((d >= 0) & (d < w)).astype(BF16)


def _head(ref_or_val, h):
    return ref_or_val[:, h * HEAD:(h + 1) * HEAD]


def _heads(ref):
    return [ref[:, sl] for sl in HEAD_COLS]


def _put_heads(ref, vals):
    for sl, v in zip(HEAD_COLS, vals):
        ref[:, sl] = v.astype(ref.dtype)


def _each(fn, *lists):
    return [fn(*args) for args in zip(*lists)]


def _proj_fwd(x, norm_w, w_pad):
    s = x.shape[0]
    tm = 512

    def body(x_ref, nw_ref, w_ref, proj_ref, nt_ref):
        xv = x_ref[...]
        r = lax.rsqrt(jnp.mean(xv * xv, axis=-1, keepdims=True) + EPS)
        nv = xv * r * nw_ref[...]
        nt_ref[...] = nv.T.astype(BF16)
        proj_ref[...] = jnp.dot(nv.astype(BF16), w_ref[...], preferred_element_type=F32)

    return _call(
        body, name="proj_fwd", grid=(s // tm,),
        in_specs=[pl.BlockSpec((tm, D_MODEL), lambda i: (i, 0)),
                  pl.BlockSpec((1, D_MODEL), lambda i: (0, 0)),
                  pl.BlockSpec((D_MODEL, N_IN_PAD), lambda i: (0, 0))],
        out_specs=[pl.BlockSpec((tm, N_IN_PAD), lambda i: (i, 0)),
                   pl.BlockSpec((D_MODEL, tm), lambda i: (0, i))],
        out_shape=[_sds((s, N_IN_PAD)), _sds((D_MODEL, s), BF16)],
        compiler_params=_params("arbitrary"),
    )(x, norm_w, w_pad)


def _pool_mix(ug, hg, zg, pw_g, row0, w):
    t = ug.shape[0]
    win = _mask_dot(_band(t, t, 0, w), ug) + _mask_dot(_band(t, HEAD, HEAD, w), hg)
    cnt = jnp.minimum(row0 + lax.broadcasted_iota(I32, (t, 1), 0) + 1, w).astype(F32)
    mix = win / cnt - ug
    mixed = _bdot(mix, pw_g)
    sg = _sigmoid(zg)
    return mix, mixed, sg, cnt


POOL_T = 256


def _pool_fwd(proj, pool_w, pool_scale):
    s = proj.shape[0]
    t = POOL_T
    hb = t // HEAD

    def body(u_ref, z_ref, halo_ref, pw_ref, ps_ref, y_ref):
        i = pl.program_id(0)
        live = (i > 0).astype(F32)
        for g, w in enumerate(WINDOWS):
            sl = HEAD_COLS[g]
            zg = z_ref[:, sl]
            _, mixed, sg, _ = _pool_mix(u_ref[:, sl], halo_ref[:, sl] * live, zg, pw_ref[g], i * t, w)
            y_ref[:, sl] = mixed * ps_ref[:, sl] * (zg * sg)

    return _call(
        body, name="pool_fwd", grid=(s // t,),
        in_specs=[pl.BlockSpec((t, D_HALF), lambda i: (i, 0)),
                  pl.BlockSpec((t, D_HALF), lambda i: (i, 1)),
                  pl.BlockSpec((HEAD, D_HALF), lambda i: (jnp.maximum(i * hb - 1, 0), 0)),
                  pl.BlockSpec((N_HEADS, HEAD, HEAD), lambda i: (0, 0, 0)),
                  pl.BlockSpec((1, D_HALF), lambda i: (0, 0))],
        out_specs=pl.BlockSpec((t, D_HALF), lambda i: (i, 0)),
        out_shape=_sds((s, D_HALF)),
        compiler_params=_params("arbitrary"),
    )(proj, proj, proj, pool_w, pool_scale)


def _conv_taps(xv, prev8):
    return [_shift_down(xv, prev8, CONV_K - 1 - j) for j in range(CONV_K - 1)] + [xv]


def _conv_pre(taps, cw):
    y = taps[CONV_K - 1] * cw[CONV_K - 1:CONV_K]
    for j in range(CONV_K - 2, -1, -1):
        y = y + taps[j] * cw[j:j + 1]
    return y


CONV_T = 256


def _conv_specs(t):
    tiles = [pl.BlockSpec((t, D_HALF), functools.partial(lambda i, p: (i, 2 + p), p=p)) for p in range(3)]
    halos = [pl.BlockSpec((8, D_HALF), functools.partial(lambda i, p: (jnp.maximum(i * (t // 8) - 1, 0), 2 + p), p=p))
             for p in range(3)]
    return tiles + halos


def _conv_fwd(proj, conv_w, a_log, dt_bias):
    s = proj.shape[0]
    t = CONV_T

    def body(q_ref, k_ref, v_ref, hq_ref, hk_ref, hv_ref, ba_ref, cw_ref, al_ref, dtb_ref,
             qn_ref, kn_ref, vs_ref, beta_ref, g_ref):
        live = (pl.program_id(0) > 0).astype(F32)
        for p, (x_ref, h_ref, o_ref) in enumerate(((q_ref, hq_ref, qn_ref), (k_ref, hk_ref, kn_ref), (v_ref, hv_ref, vs_ref))):
            y = _conv_pre(_conv_taps(x_ref[...], h_ref[...] * live), cw_ref[:, p * D_HALF:(p + 1) * D_HALF])
            sv = y * _sigmoid(y)
            if p == 2:
                o_ref[...] = sv
            else:
                for h in HEADS:
                    seg = _head(sv, h)
                    o_ref[:, HEAD_COLS[h]] = seg * lax.rsqrt(_rowsum(seg * seg) + EPS)
        ba = ba_ref[...]
        for h in HEADS:
            beta = _sigmoid(ba[:, h:h + 1])
            gl = -jnp.exp(al_ref[0:1, h:h + 1]) * _softplus(ba[:, N_HEADS + h:N_HEADS + h + 1] + dtb_ref[0:1, h:h + 1])
            beta_ref[:, HEAD_COLS[h]] = jnp.broadcast_to(beta, (t, HEAD))
            g_ref[:, HEAD_COLS[h]] = jnp.broadcast_to(gl, (t, HEAD))

    row = pl.BlockSpec((t, D_HALF), lambda i: (i, 0))
    return _call(
        body, name="conv_fwd", grid=(s // t,),
        in_specs=_conv_specs(t) + [pl.BlockSpec((t, HEAD), lambda i: (i, COL_BA // HEAD)),
                                   pl.BlockSpec((CONV_K, 3 * D_HALF), lambda i: (0, 0)),
                                   pl.BlockSpec((1, N_HEADS), lambda i: (0, 0)),
                                   pl.BlockSpec((1, N_HEADS), lambda i: (0, 0))],
        out_specs=[row] * 5,
        out_shape=[_sds((s, D_HALF))] * 5,
        compiler_params=_params("arbitrary"),
    )(proj, proj, proj, proj, proj, proj, proj, conv_w, a_log, dt_bias)


def _pair_masks():
    r = lax.broadcasted_iota(I32, (PAIR, PAIR), 0)
    c = lax.broadcasted_iota(I32, (PAIR, PAIR), 1)
    same = jnp.right_shift(r, 6) == jnp.right_shift(c, 6)
    return same, same & (r >= c), same & (r > c), r == c


def _pair_common(qn, kn, vs, beta, g):
    same, incl, strict, eye = _pair_masks()
    incl_b = incl.astype(BF16)
    first = lax.broadcasted_iota(I32, (PAIR, HEAD), 0) < CHUNK
    gc = _each(lambda gv: _mask_dot(incl_b, gv), g)
    gc_row = _each(lambda v: _colsum(jnp.where(eye, v, 0.0)), gc)
    decay = _each(lambda v, r: jnp.where(incl, jnp.exp(jnp.where(incl, v - r, 0.0)), 0.0), gc, gc_row)
    gl = _each(lambda v: jnp.where(first, v[CHUNK - 1:CHUNK], v[PAIR - 1:PAIR]), gc)
    egc = _each(jnp.exp, gc)
    q = _each(lambda v: v * QK_SCALE, qn)
    kb = _each(lambda k, b: k * b, kn, beta)
    return dict(same=same, incl=incl, strict=strict, eye=eye, gc=gc, decay=decay, gl=gl, egc=egc,
                ekd=_each(lambda a, b: jnp.exp(a - b), gl, gc), cd=_each(jnp.exp, gl), q=q, kb=kb,
                vb=_each(lambda v, b: v * b, vs, beta), kbg=_each(lambda k, e: k * e, kb, egc),
                kk=_each(_bdot_nt, kb, kn), qk=_each(_bdot_nt, q, kn))


def _tri_inv(a, eye_f):
    p = _each(lambda v: eye_f - v, a)
    x = _each(_bdot, a, a)
    for it in range(5):
        p = _each(lambda pv, xv: pv + _bdot(pv, xv), p, x)
        if it < 4:
            x = _each(_bdot, x, x)
    return p


def _pair_spec():
    return pl.BlockSpec((PAIR, D_HALF), lambda i: (i, 0))


def _chunk_scalar_spec(pairs=1, index=lambda i: (i, 0)):
    return pl.BlockSpec((16 * pairs, D_HALF), index)


SCAN_PAIRS = 2
SCAN_ROWS = SCAN_PAIRS * PAIR


def _intra_fwd(qn, kn, vs, beta, g):
    s = qn.shape[0]

    def body(qn_ref, kn_ref, vs_ref, beta_ref, g_ref, u_ref, w_ref, att_ref, qd_ref, kd_ref, t_ref, cd_ref):
        kn = _heads(kn_ref)
        cm = _pair_common(_heads(qn_ref), kn, _heads(vs_ref), _heads(beta_ref), _heads(g_ref))
        a = _each(lambda kk, d: jnp.where(cm["strict"], kk * d, 0.0), cm["kk"], cm["decay"])
        tm = _tri_inv(a, cm["eye"].astype(F32))
        _put_heads(t_ref, tm)
        _put_heads(u_ref, _each(_bdot, tm, cm["vb"]))
        _put_heads(w_ref, _each(_bdot, tm, cm["kbg"]))
        _put_heads(att_ref, _each(lambda a, b: a * b, cm["qk"], cm["decay"]))
        _put_heads(qd_ref, _each(lambda a, b: a * b, cm["q"], cm["egc"]))
        _put_heads(kd_ref, _each(lambda a, b: a * b, kn, cm["ekd"]))
        for ci in range(2):
            for sl, v in zip(HEAD_COLS, cm["cd"]):
                cd_ref[ci * 8:(ci + 1) * 8, sl] = v[ci * CHUNK:ci * CHUNK + 8]

    return _call(
        body, name="intra_fwd", grid=(s // PAIR,),
        in_specs=[_pair_spec()] * 5, out_specs=[_pair_spec()] * 6 + [_chunk_scalar_spec()],
        out_shape=[_sds((s, D_HALF))] + [_sds((s, D_HALF), BF16)] * 5 + [_sds((s // 8, D_HALF))],
        compiler_params=_params("arbitrary"),
    )(qn, kn, vs, beta, g)


def _scan_fwd(u, w, att, qd, kd, cd):
    s = u.shape[0]
    n_chunks = s // CHUNK

    def body(u_ref, w_ref, att_ref, qd_ref, kd_ref, cd_ref, o_ref, vn_ref, st_ref, state):
        @pl.when(pl.program_id(0) == 0)
        def _():
            state[...] = jnp.zeros_like(state)
        cols = list(enumerate(HEAD_COLS))
        sm = [state[h] for h in HEADS]
        for ci in range(2 * SCAN_PAIRS):
            rs = slice(ci * CHUNK, (ci + 1) * CHUNK)
            for h in HEADS:
                st_ref[ci, h] = sm[h]
            both = [_bdot(jnp.concatenate([w_ref[rs, sl], qd_ref[rs, sl]], axis=0), sm[h]) for h, sl in cols]
            vn = [u_ref[rs, sl] - both[h][:CHUNK] for h, sl in cols]
            for h, sl in cols:
                vn_ref[rs, sl] = vn[h].astype(BF16)
                o_ref[rs, sl] = both[h][CHUNK:]
            sm = [sm[h] * cd_ref[ci * 8:ci * 8 + 1, sl] + _bdot_tn(kd_ref[rs, sl], vn[h]) for h, sl in cols]
        for h in HEADS:
            state[h] = sm[h]
        for pp in range(SCAN_PAIRS):
            rp = slice(pp * PAIR, (pp + 1) * PAIR)
            intra = [_bdot(att_ref[rp, sl], vn_ref[rp, sl]) for sl in HEAD_COLS]
            for h, sl in cols:
                o_ref[rp, sl] += intra[h]

    rows = pl.BlockSpec((SCAN_ROWS, D_HALF), lambda i: (i, 0))
    return _call(
        body, name="scan_fwd", grid=(s // SCAN_ROWS,),
        in_specs=[rows] * 5 + [_chunk_scalar_spec(SCAN_PAIRS)],
        out_specs=[rows, rows, pl.BlockSpec((2 * SCAN_PAIRS, N_HEADS, HEAD, HEAD), lambda i: (i, 0, 0, 0))],
        out_shape=[_sds((s, D_HALF)), _sds((s, D_HALF), BF16), _sds((n_chunks, N_HEADS, HEAD, HEAD))],
        scratch_shapes=[pltpu.VMEM((N_HEADS, HEAD, HEAD), F32)],
        compiler_params=_params("arbitrary"),
    )(u, w, att, qd, kd, cd)


OUT_T = 512


def _out_fwd_bwd(x, y_pool, o, proj, target, w_out, dn_norm_w, final_norm_w):
    s = x.shape[0]
    t = OUT_T

    def body(x_ref, yp_ref, o_ref, z_ref, tg_ref, wo_ref, dnw_ref, fnw_ref,
             yt_ref, dh_ref, dyp_ref, do_ref, dz_ref, loss_ref, gfn_ref, gdn_ref, y_ref):
        @pl.when(pl.program_id(0) == 0)
        def _():
            loss_ref[...] = jnp.zeros_like(loss_ref)
            gfn_ref[...] = jnp.zeros_like(gfn_ref)
            gdn_ref[...] = jnp.zeros_like(gdn_ref)

        ypv = yp_ref[...]
        y_ref[:, :D_HALF] = ypv.astype(BF16)
        yt_ref[:D_HALF, :] = ypv.T.astype(BF16)
        dnw = dnw_ref[...]
        keep = []
        for h in HEADS:
            ov = o_ref[:, HEAD_COLS[h]]
            zv = z_ref[:, HEAD_COLS[h]]
            ro = lax.rsqrt(jnp.mean(ov * ov, axis=-1, keepdims=True) + EPS)
            ohat = ov * ro
            sg = _sigmoid(zv)
            keep.append((ro, ohat, zv, sg))
            ydn = ohat * dnw * (zv * sg)
            y_ref[:, D_HALF + h * HEAD:D_HALF + (h + 1) * HEAD] = ydn.astype(BF16)
            yt_ref[D_HALF + h * HEAD:D_HALF + (h + 1) * HEAD, :] = ydn.T.astype(BF16)

        hv = x_ref[...] + jnp.dot(y_ref[...], wo_ref[...], preferred_element_type=F32)
        r2 = lax.rsqrt(jnp.mean(hv * hv, axis=-1, keepdims=True) + EPS)
        hhat = hv * r2
        fnw = fnw_ref[...]
        err = hhat * fnw - tg_ref[...]
        loss_ref[...] += 0.5 * jnp.sum(_rowsum(err * err) * (1.0 / D_MODEL), axis=0, keepdims=True)
        dout = err * (1.0 / D_MODEL)
        gfn_ref[...] += _colsum(dout * hhat)
        dhh = dout * fnw
        dh = r2 * (dhh - hhat * jnp.mean(dhh * hhat, axis=-1, keepdims=True))
        dh_ref[...] = dh
        dy = _bdot_nt(dh, wo_ref[...])
        dyp_ref[...] = dy[:, :D_HALF]
        gdn = jnp.zeros((1, HEAD), F32)
        for h in HEADS:
            ro, ohat, zv, sg = keep[h]
            dyd = dy[:, D_HALF + h * HEAD:D_HALF + (h + 1) * HEAD]
            sz = zv * sg
            dz_ref[:, HEAD_COLS[h]] = (dyd * ohat * dnw * (sg * (1.0 + zv * (1.0 - sg)))).astype(BF16)
            gdn = gdn + _colsum(dyd * ohat * sz)
            doh = dyd * dnw * sz
            do_ref[:, HEAD_COLS[h]] = ro * (doh - ohat * jnp.mean(doh * ohat, axis=-1, keepdims=True))
        gdn_ref[...] += gdn

    wide = pl.BlockSpec((t, D_MODEL), lambda i: (i, 0))
    half = pl.BlockSpec((t, D_HALF), lambda i: (i, 0))
    const = lambda shape: pl.BlockSpec(shape, lambda i: (0,) * len(shape))
    return _call(
        body, name="out_fwd_bwd", grid=(s // t,),
        in_specs=[wide, half, half, pl.BlockSpec((t, D_HALF), lambda i: (i, 5)), wide,
                  const((D_MODEL, D_MODEL)), const((1, HEAD)), const((1, D_MODEL))],
        out_specs=[pl.BlockSpec((D_MODEL, t), lambda i: (0, i)), wide, half, half, half,
                   const((1, HEAD)), const((1, D_MODEL)), const((1, HEAD))],
        out_shape=[_sds((D_MODEL, s), BF16), _sds((s, D_MODEL)), _sds((s, D_HALF)), _sds((s, D_HALF)), _sds((s, D_HALF), BF16),
                   _sds((1, HEAD)), _sds((1, D_MODEL)), _sds((1, HEAD))],
        scratch_shapes=[pltpu.VMEM((t, D_MODEL), BF16)],
        compiler_params=_params("arbitrary"),
    )(x, y_pool, o, proj, target, w_out, dn_norm_w, final_norm_w)


def _token_matmul(name, at, pieces):
    m, s = at.shape
    n = len(pieces)
    tn, tk = D_HALF, 512

    def body(a_ref, *refs):
        p_refs, o_ref = refs[:n], refs[n]

        @pl.when(pl.program_id(0) == 0)
        def _():
            o_ref[...] = jnp.zeros_like(o_ref)

        av = a_ref[...]
        for p in range(n):
            o_ref[:, p * tn:(p + 1) * tn] += _bdot(av, p_refs[p][...])

    return _call(
        body, name=name, grid=(s // tk,),
        in_specs=[pl.BlockSpec((m, tk), lambda k: (0, k))]
                 + [pl.BlockSpec((tk, tn), functools.partial(lambda k, cb: (k, cb), cb=cb)) for _, cb in pieces],
        out_specs=pl.BlockSpec((m, n * tn), lambda k: (0, 0)),
        out_shape=_sds((m, n * tn)),
        compiler_params=_params("arbitrary"),
    )(at, *[p[0] for p in pieces])


def _pool_bwd(proj, dyp, pool_w, pool_scale):
    s = proj.shape[0]
    t = POOL_T
    hb = t // HEAD
    last = s // HEAD - 1

    def body(u_ref, z_ref, halo_ref, dy_ref, zn_ref, dyn_ref, pw_ref, ps_ref, du_ref, dz_ref, gpw_ref, gps_ref):
        i = pl.program_id(0)

        @pl.when(i == 0)
        def _():
            gpw_ref[...] = jnp.zeros_like(gpw_ref)
            gps_ref[...] = jnp.zeros_like(gps_ref)

        live = (i > 0).astype(F32)
        more = (i < pl.num_programs(0) - 1).astype(F32)
        for g, w in enumerate(WINDOWS):
            sl = HEAD_COLS[g]
            zg = z_ref[:, sl]
            ps = ps_ref[:, sl]
            pw = pw_ref[g]
            mix, mixed, sg, cnt = _pool_mix(u_ref[:, sl], halo_ref[:, sl] * live, zg, pw, i * t, w)
            dyg = dy_ref[:, sl]
            sz = zg * sg
            dz_ref[:, sl] = (dyg * mixed * ps * (sg * (1.0 + zg * (1.0 - sg)))).astype(BF16)
            gps_ref[:, sl] += _colsum(dyg * mixed * sz)
            dmixed = dyg * ps * sz
            gpw_ref[g] += _bdot_tn(mix, dmixed)
            dmix = _bdot_nt(dmixed, pw)
            zn = zn_ref[:, sl]
            dmix_n = _bdot_nt(dyn_ref[:, sl] * more * ps * (zn * _sigmoid(zn)), pw)
            du_ref[:, sl] = (_mask_dot(_band(t, t, 0, w, anti=True), dmix / cnt)
                             + _mask_dot(_band(t, HEAD, t, w, anti=True), dmix_n * (1.0 / w)) - dmix).astype(BF16)

    tile = lambda col: pl.BlockSpec((t, D_HALF), lambda i: (i, col))
    below = lambda col: pl.BlockSpec((HEAD, D_HALF), lambda i: (jnp.minimum((i + 1) * hb, last), col))
    return _call(
        body, name="pool_bwd", grid=(s // t,),
        in_specs=[tile(0), tile(1), pl.BlockSpec((HEAD, D_HALF), lambda i: (jnp.maximum(i * hb - 1, 0), 0)),
                  tile(0), below(1), below(0),
                  pl.BlockSpec((N_HEADS, HEAD, HEAD), lambda i: (0, 0, 0)), pl.BlockSpec((1, D_HALF), lambda i: (0, 0))],
        out_specs=[tile(0), tile(0), pl.BlockSpec((N_HEADS, HEAD, HEAD), lambda i: (0, 0, 0)),
                   pl.BlockSpec((1, D_HALF), lambda i: (0, 0))],
        out_shape=[_sds((s, D_HALF), BF16), _sds((s, D_HALF), BF16), _sds((N_HEADS, HEAD, HEAD)), _sds((1, D_HALF))],
        compiler_params=_params("arbitrary"),
    )(proj, proj, proj, dyp, proj, dyp, pool_w, pool_scale)


def _scan_bwd(do, vn, qd, kd, w, att, cd, st):
    s = do.shape[0]
    n_steps = s // SCAN_ROWS

    def body(do_ref, vn_ref, qd_ref, kd_ref, w_ref, att_ref, cd_ref, st_ref,
             du_ref, dw_ref, datt_ref, dqd_ref, dkd_ref, dcd_ref, dstate):
        @pl.when(pl.program_id(0) == 0)
        def _():
            dstate[...] = jnp.zeros_like(dstate)
        _, incl, _, _ = _pair_masks()
        cols = list(enumerate(HEAD_COLS))
        dv_intra = []
        for pp in range(SCAN_PAIRS):
            rp = slice(pp * PAIR, (pp + 1) * PAIR)
            dv_intra.append([_bdot_tn(att_ref[rp, sl], do_ref[rp, sl]) for _, sl in cols])
            for _, sl in cols:
                datt_ref[rp, sl] = jnp.where(incl, _bdot_nt(do_ref[rp, sl], vn_ref[rp, sl]), 0.0)
        ds = [dstate[h] for h in HEADS]
        for ci in range(2 * SCAN_PAIRS - 1, -1, -1):
            rs = slice(ci * CHUNK, (ci + 1) * CHUNK)
            in_pair = slice((ci % 2) * CHUNK, (ci % 2 + 1) * CHUNK)
            sm = [st_ref[ci, h] for h in HEADS]
            dvn = [dv_intra[ci // 2][h][in_pair] + _bdot(kd_ref[rs, sl], ds[h]) for h, sl in cols]
            for h, sl in cols:
                du_ref[rs, sl] = dvn[h].astype(BF16)
            dqd = [_bdot_nt(do_ref[rs, sl], sm[h]) for h, sl in cols]
            dw = [-_bdot_nt(dvn[h], sm[h]) for h, _ in cols]
            dkd = [_bdot_nt(vn_ref[rs, sl], ds[h]) for h, sl in cols]
            dcd = [jnp.broadcast_to(_rowsum(_colsum(ds[h] * sm[h])), (8, HEAD)) for h in HEADS]
            for h, sl in cols:
                dqd_ref[rs, sl] = dqd[h]
                dw_ref[rs, sl] = dw[h].astype(BF16)
                dkd_ref[rs, sl] = dkd[h]
                dcd_ref[ci * 8:(ci + 1) * 8, sl] = dcd[h]
            ds = [ds[h] * cd_ref[ci * 8:ci * 8 + 1, sl] + _bdot_tn(qd_ref[rs, sl], do_ref[rs, sl])
                  - _bdot_tn(w_ref[rs, sl], dvn[h]) for h, sl in cols]
        for h in HEADS:
            dstate[h] = ds[h]

    rev = pl.BlockSpec((SCAN_ROWS, D_HALF), lambda i: (n_steps - 1 - i, 0))
    rev_scalar = _chunk_scalar_spec(SCAN_PAIRS, lambda i: (n_steps - 1 - i, 0))
    return _call(
        body, name="scan_bwd", grid=(n_steps,),
        in_specs=[rev] * 6 + [rev_scalar,
                              pl.BlockSpec((2 * SCAN_PAIRS, N_HEADS, HEAD, HEAD), lambda i: (n_steps - 1 - i, 0, 0, 0))],
        out_specs=[rev] * 5 + [rev_scalar],
        out_shape=[_sds((s, D_HALF), BF16)] * 2 + [_sds((s, D_HALF))] * 3 + [_sds((s // 8, D_HALF))],
        scratch_shapes=[pltpu.VMEM((N_HEADS, HEAD, HEAD), F32)],
        compiler_params=_params("arbitrary"),
    )(do, vn, qd, kd, w, att, cd, st)


def _intra_bwd(qn, kn, vs, beta, g, tm, du, dw, datt, dqd, dkd, dcd):
    s = qn.shape[0]

    def body(qn_ref, kn_ref, vs_ref, beta_ref, g_ref, t_ref, du_ref, dw_ref, datt_ref, dqd_ref, dkd_ref, dcd_ref,
             dqn_ref, dkn_ref, dvs_ref, dbeta_ref, dg_ref):
        ones = jnp.ones((PAIR, HEAD), BF16)
        tn = (((0,), (0,)), ((), ()))
        kn, vs, beta = _heads(kn_ref), _heads(vs_ref), _heads(beta_ref)
        cm = _pair_common(_heads(qn_ref), kn, vs, beta, _heads(g_ref))
        tmv, duv, dwv, dattv, dqdv, dkdv = (_heads(r) for r in (t_ref, du_ref, dw_ref, datt_ref, dqd_ref, dkd_ref))
        dvb = _each(_bdot_tn, tmv, duv)
        dt = _each(lambda a, b, c, d: _bdot_nt(a, b) + _bdot_nt(c, d), duv, cm["vb"], dwv, cm["kbg"])
        dkbg = _each(_bdot_tn, tmv, dwv)
        m1 = _each(_bdot_tn, tmv, dt)
        da = _each(lambda a, b: -jnp.where(cm["strict"], _bdot_nt(a, b), 0.0), m1, tmv)
        dkk = _each(lambda a, b: a * b, da, cm["decay"])
        dqk = _each(lambda a, b: a * b, dattv, cm["decay"])
        dd = _each(lambda a, b, c, d: a * b + c * d, dkk, cm["kk"], dqk, cm["qk"])
        dkb = _each(lambda a, b, c, d: _bdot(a, b) + c * d, dkk, kn, dkbg, cm["egc"])
        dq = _each(lambda a, b, c, d: _bdot(a, b) + c * d, dqk, kn, dqdv, cm["egc"])
        dkn = _each(lambda a, b, c, d: _bdot_tn(a, b) + _bdot_tn(c, d), dkk, cm["kb"], dqk, cm["q"])
        dkn = _each(lambda a, b, c, d, e: a + b * c + d * e, dkn, dkdv, cm["ekd"], dkb, beta)
        t_kd = _each(lambda a, b, c: _rowsum(a * b * c), dkdv, kn, cm["ekd"])
        split = _each(_split, dd)
        rows_dd = [jnp.dot(hi, ones, preferred_element_type=F32) + jnp.dot(lo, ones, preferred_element_type=F32)
                   for hi, lo in split]
        cols_dd = [lax.dot_general(hi, ones, tn, preferred_element_type=F32)
                   + lax.dot_general(lo, ones, tn, preferred_element_type=F32) for hi, lo in split]
        dgc = _each(lambda r, c, a, b, e, f, k, t: r - c + _rowsum(a * b * e) + _rowsum(f * k) - t,
                    rows_dd, cols_dd, dqdv, cm["q"], cm["egc"], dkbg, cm["kbg"], t_kd)
        same_b = cm["same"].astype(BF16)
        rowi = lax.broadcasted_iota(I32, (PAIR, HEAD), 0)
        dcd = _each(lambda d: jnp.where(rowi < CHUNK, d[0:1], d[8:9]), _heads(dcd_ref))
        dgl = _each(lambda t, d, c: _mask_dot(same_b, jnp.broadcast_to(t, (PAIR, HEAD))) + d * c, t_kd, dcd, cm["cd"])
        is_last = jnp.bitwise_and(rowi, CHUNK - 1) == CHUNK - 1
        dgc = _each(lambda a, b: a + jnp.where(is_last, b, 0.0), dgc, dgl)
        r = lax.broadcasted_iota(I32, (PAIR, PAIR), 0)
        c = lax.broadcasted_iota(I32, (PAIR, PAIR), 1)
        upper_b = (cm["same"] & (r <= c)).astype(BF16)
        _put_heads(dg_ref, _each(lambda v: _mask_dot(upper_b, v), dgc))
        _put_heads(dbeta_ref, _each(lambda a, b, c, d: jnp.broadcast_to(_rowsum(a * b) + _rowsum(c * d), (PAIR, HEAD)),
                                    dkb, kn, dvb, vs))
        _put_heads(dqn_ref, _each(lambda v: v * QK_SCALE, dq))
        _put_heads(dkn_ref, dkn)
        _put_heads(dvs_ref, _each(lambda a, b: a * b, dvb, beta))

    return _call(
        body, name="intra_bwd", grid=(s // PAIR,),
        in_specs=[_pair_spec()] * 11 + [_chunk_scalar_spec()], out_specs=[_pair_spec()] * 5,
        out_shape=[_sds((s, D_HALF))] * 5,
        compiler_params=_params("arbitrary"),
    )(qn, kn, vs, beta, g, tm, du, dw, datt, dqd, dkd, dcd)


def _conv_bwd_pre(proj, conv_w, a_log, dt_bias, dqn, dkn, dvs, dbeta, dg):
    s = proj.shape[0]
    t = CONV_T

    def body(q_ref, k_ref, v_ref, hq_ref, hk_ref, hv_ref, ba_ref, cw_ref, al_ref, dtb_ref,
             dqn_ref, dkn_ref, dvs_ref, dbeta_ref, dg_ref, dyq_ref, dyk_ref, dyv_ref, dba_ref, gcw_ref, gsm_ref):
        @pl.when(pl.program_id(0) == 0)
        def _():
            gcw_ref[...] = jnp.zeros_like(gcw_ref)
            gsm_ref[...] = jnp.zeros_like(gsm_ref)

        live = (pl.program_id(0) > 0).astype(F32)
        parts = ((q_ref, hq_ref, dqn_ref, dyq_ref), (k_ref, hk_ref, dkn_ref, dyk_ref), (v_ref, hv_ref, dvs_ref, dyv_ref))
        for p, (x_ref, h_ref, d_ref, dy_ref) in enumerate(parts):
            cols = slice(p * D_HALF, (p + 1) * D_HALF)
            taps = _conv_taps(x_ref[...], h_ref[...] * live)
            y = _conv_pre(taps, cw_ref[:, cols])
            sg = _sigmoid(y)
            sv = y * sg
            if p == 2:
                ds = d_ref[...]
            else:
                segs = []
                for h in HEADS:
                    seg = _head(sv, h)
                    rn = lax.rsqrt(_rowsum(seg * seg) + EPS)
                    nrm = seg * rn
                    dn = d_ref[:, HEAD_COLS[h]]
                    segs.append(rn * (dn - nrm * _rowsum(dn * nrm)))
                ds = jnp.concatenate(segs, axis=1)
            dy = ds * (sg * (1.0 + y * (1.0 - sg)))
            dy_ref[...] = dy
            for j in range(CONV_K):
                gcw_ref[j:j + 1, cols] += _colsum(dy * taps[j])

        ba = ba_ref[...]
        lane = lax.broadcasted_iota(I32, (t, HEAD), 1)
        lane1 = lax.broadcasted_iota(I32, (1, HEAD), 1)
        dba = jnp.zeros((t, HEAD), F32)
        gsm = jnp.zeros((1, HEAD), F32)
        for h in HEADS:
            beta = _sigmoid(ba[:, h:h + 1])
            dbeta = dbeta_ref[:, h * HEAD:h * HEAD + 1]
            xg = ba[:, N_HEADS + h:N_HEADS + h + 1] + dtb_ref[0:1, h:h + 1]
            nexp = -jnp.exp(al_ref[0:1, h:h + 1])
            dgv = dg_ref[:, h * HEAD:h * HEAD + 1]
            da = dgv * nexp * _sigmoid(xg)
            dba = dba + jnp.where(lane == h, dbeta * beta * (1.0 - beta), 0.0) + jnp.where(lane == N_HEADS + h, da, 0.0)
            gsm = (gsm + jnp.where(lane1 == h, _colsum(dgv * nexp * _softplus(xg)), 0.0)
                   + jnp.where(lane1 == N_HEADS + h, _colsum(da), 0.0))
        dba_ref[...] = jnp.zeros_like(dba_ref)
        dba_ref[:, :HEAD] = dba.astype(BF16)
        gsm_ref[0:1, :] += gsm

    row = pl.BlockSpec((t, D_HALF), lambda i: (i, 0))
    return _call(
        body, name="conv_bwd_pre", grid=(s // t,),
        in_specs=_conv_specs(t) + [pl.BlockSpec((t, HEAD), lambda i: (i, COL_BA // HEAD)),
                                   pl.BlockSpec((CONV_K, 3 * D_HALF), lambda i: (0, 0)),
                                   pl.BlockSpec((1, N_HEADS), lambda i: (0, 0)),
                                   pl.BlockSpec((1, N_HEADS), lambda i: (0, 0))] + [row] * 5,
        out_specs=[row, row, row, row,
                   pl.BlockSpec((8, 3 * D_HALF), lambda i: (0, 0)), pl.BlockSpec((8, HEAD), lambda i: (0, 0))],
        out_shape=[_sds((s, D_HALF))] * 3 + [_sds((s, D_HALF), BF16), _sds((8, 3 * D_HALF)), _sds((8, HEAD))],
        compiler_params=_params("arbitrary"),
    )(proj, proj, proj, proj, proj, proj, proj, conv_w, a_log, dt_bias, dqn, dkn, dvs, dbeta, dg)


def _conv_bwd_in(dyq, dyk, dyv, conv_w):
    s = dyq.shape[0]
    t = CONV_T
    last = s // 8 - 1

    def body(q_ref, k_ref, v_ref, nq_ref, nk_ref, nv_ref, cw_ref, oq_ref, ok_ref, ov_ref):
        more = (pl.program_id(0) < pl.num_programs(0) - 1).astype(F32)
        for p, (d_ref, n_ref, o_ref) in enumerate(((q_ref, nq_ref, oq_ref), (k_ref, nk_ref, ok_ref), (v_ref, nv_ref, ov_ref))):
            cw = cw_ref[:, p * D_HALF:(p + 1) * D_HALF]
            dy = d_ref[...]
            nxt = n_ref[...] * more
            acc = dy * cw[3:4]
            for sft in (1, 2, 3):
                acc = acc + _shift_up(dy, nxt, sft) * cw[3 - sft:4 - sft]
            o_ref[...] = acc.astype(BF16)

    row = pl.BlockSpec((t, D_HALF), lambda i: (i, 0))
    nxt = pl.BlockSpec((8, D_HALF), lambda i: (jnp.minimum((i + 1) * (t // 8), last), 0))
    return _call(
        body, name="conv_bwd_in", grid=(s // t,),
        in_specs=[row] * 3 + [nxt] * 3 + [pl.BlockSpec((CONV_K, 3 * D_HALF), lambda i: (0, 0))],
        out_specs=[row] * 3, out_shape=[_sds((s, D_HALF), BF16)] * 3,
        compiler_params=_params("arbitrary"),
    )(dyq, dyk, dyv, dyq, dyk, dyv, conv_w)


IN_T = 512


def _in_bwd(x, dh, norm_w, w_pad, pieces):
    s = x.shape[0]
    t = IN_T
    widths = [D_HALF] * 6 + [N_IN_PAD - COL_BA]

    def body(*refs):
        x_ref, dh_ref, nw_ref, w_ref = refs[:4]
        p_refs = refs[4:4 + len(pieces)]
        gx_ref, gnw_ref = refs[4 + len(pieces):]

        @pl.when(pl.program_id(0) == 0)
        def _():
            gnw_ref[...] = jnp.zeros_like(gnw_ref)

        dn = jnp.zeros((t, D_MODEL), F32)
        col = 0
        for p_ref, wd in zip(p_refs, widths):
            dn = dn + _bdot_nt(p_ref[...], w_ref[:, col:col + wd])
            col += wd
        xv = x_ref[...]
        r = lax.rsqrt(jnp.mean(xv * xv, axis=-1, keepdims=True) + EPS)
        xhat = xv * r
        gnw_ref[...] += _colsum(dn * xhat)
        dxh = dn * nw_ref[...]
        gx_ref[...] = dh_ref[...] + r * (dxh - xhat * jnp.mean(dxh * xhat, axis=-1, keepdims=True))

    wide = pl.BlockSpec((t, D_MODEL), lambda i: (i, 0))
    return _call(
        body, name="in_bwd", grid=(s // t,),
        in_specs=[wide, wide, pl.BlockSpec((1, D_MODEL), lambda i: (0, 0)),
                  pl.BlockSpec((D_MODEL, N_IN_PAD), lambda i: (0, 0))]
                 + [pl.BlockSpec((t, wd), lambda i: (i, 0)) for wd in widths],
        out_specs=[wide, pl.BlockSpec((1, D_MODEL), lambda i: (0, 0))],
        out_shape=[_sds((s, D_MODEL)), _sds((1, D_MODEL))],
        compiler_params=_params("arbitrary"),
    )(x, dh, norm_w, w_pad, *pieces)


def _adamw_shard(name, w, g_own, g_got, cidx, m, v):
    _, r, c = w.shape
    half = r // 2
    rows = 256 if half % 256 == 0 else half
    per_half = half // rows

    def body(c_ref, w_ref, go_ref, gg_ref, m_ref, v_ref, gout_ref, d_ref, nm_ref, nv_ref):
        mine = (pl.program_id(0) // per_half) == c_ref[0]
        gv = jnp.where(mine, go_ref[:, :c], gg_ref[:, :c])
        gout_ref[0] = gv
        mn = ADAM_B1 * m_ref[0] + (1.0 - ADAM_B1) * gv
        vn = ADAM_B2 * v_ref[0] + (1.0 - ADAM_B2) * (gv * gv)
        m_hat = mn / (1.0 - ADAM_B1 ** ADAM_STEP)
        v_hat = vn / (1.0 - ADAM_B2 ** ADAM_STEP)
        d_ref[0] = -ADAM_LR * (m_hat / (jnp.sqrt(v_hat) + ADAM_EPS) + ADAM_WD * w_ref[0])
        nm_ref[0] = mn
        nv_ref[0] = vn

    blk = pl.BlockSpec((1, rows, c), lambda i, c_ref: (0, i, 0))
    gblk = pl.BlockSpec((rows, g_own.shape[1]), lambda i, c_ref: (i % per_half, 0))
    return _call(
        body, name=name,
        grid_spec=pltpu.PrefetchScalarGridSpec(
            num_scalar_prefetch=1, grid=(2 * per_half,),
            in_specs=[blk, gblk, gblk, blk, blk], out_specs=[blk] * 4),
        out_shape=[_sds((1, r, c))] * 4,
        compiler_params=_params("arbitrary"),
    )(cidx, w, g_own, g_got, m, v)


def _exchange(name, inputs, out_shapes, phases):
    n_in = len(inputs)
    n_out = len(out_shapes)
    n_cp = sum(len(p) for p in phases)

    def body(*refs):
        ins, outs = refs[:n_in], refs[n_in:n_in + n_out]
        send, recv = refs[n_in + n_out:]
        pos = (lax.axis_index("x"), lax.axis_index("y"), lax.axis_index("c"))
        k = 0
        for phase in phases:
            cps = []
            for src, dst, target in phase:
                cps.append(pltpu.make_async_remote_copy(
                    src_ref=src(ins, outs, pos), dst_ref=dst(ins, outs, pos), send_sem=send.at[k], recv_sem=recv.at[k],
                    device_id=target(pos), device_id_type=pl.DeviceIdType.MESH))
                k += 1
            for cp in cps:
                cp.start()
            for cp in cps:
                cp.wait()

    anyspec = pl.BlockSpec(memory_space=pl.ANY)
    return _call(
        body, name=name,
        in_specs=[anyspec] * n_in, out_specs=[anyspec] * n_out, out_shape=list(out_shapes),
        scratch_shapes=[pltpu.SemaphoreType.DMA((n_cp,)), pltpu.SemaphoreType.DMA((n_cp,))],
    )(*inputs)


def _chip(pos):
    return 2 * pos[0] + pos[1]


def _other_chip(pos, mask):
    x, y, c = pos
    return (x ^ (mask >> 1), y ^ (mask & 1), c)


def _sibling(pos):
    return (pos[0], pos[1], 1 - pos[2])


def _gather_weights(wb, ob, cb):
    halves = (wb.shape[0] // 2, ob.shape[0] // 2)

    def half(a, pos):
        return pl.ds(pos[2] * halves[a], halves[a])

    first, second = [], []
    for mask in CHIP_MASKS:
        for a in (0, 1):
            first.append((lambda ins, outs, pos, a=a: ins[a].at[half(a, pos)],
                          lambda ins, outs, pos, a=a: outs[a].at[_chip(pos), half(a, pos)],
                          functools.partial(_other_chip, mask=mask)))
            second.append((lambda ins, outs, pos, a=a, mask=mask: outs[a].at[_chip(pos) ^ mask, half(a, pos)],
                           lambda ins, outs, pos, a=a, mask=mask: outs[a].at[_chip(pos) ^ mask, half(a, pos)],
                           _sibling))
        first.append((lambda ins, outs, pos: ins[2],
                      lambda ins, outs, pos: outs[2].at[_chip(pos)],
                      functools.partial(_other_chip, mask=mask)))
    return _exchange("gather_weights", [wb, ob, cb],
                     [_sds((4,) + wb.shape, wb.dtype), _sds((4,) + ob.shape, ob.dtype), _sds((4,) + cb.shape, cb.dtype)],
                     [first, second])


def _to_sibling_half(name, arrays):
    def src(ins, outs, pos, a):
        h = arrays[a].shape[-2] // 2
        sl = pl.ds((1 - pos[2]) * h, h)
        return ins[a].at[:, sl] if arrays[a].ndim == 3 else ins[a].at[sl]

    outs = [_sds(a.shape[:-2] + (a.shape[-2] // 2, a.shape[-1]), a.dtype) for a in arrays]
    phase = [(functools.partial(src, a=a), lambda ins, outs, pos, a=a: outs[a], _sibling) for a in range(len(arrays))]
    return _exchange(name, arrays, outs, [phase])


def _add_half(name, full, part, cidx):
    shape = part.shape
    lead = shape[0] if len(shape) == 3 else 1
    rows, cols = shape[-2], shape[-1]
    tr = rows // 2 if rows % 16 == 0 else rows
    nr = rows // tr
    f3 = full.reshape((lead,) + full.shape[-2:])
    p3 = part.reshape((lead, rows, cols))

    def body(c_ref, f_ref, p_ref, o_ref):
        o_ref[...] = (f_ref[...].astype(F32) + p_ref[...].astype(F32)).astype(o_ref.dtype)

    out = _call(
        body, name=name,
        grid_spec=pltpu.PrefetchScalarGridSpec(
            num_scalar_prefetch=1, grid=(lead, nr),
            in_specs=[pl.BlockSpec((1, tr, cols), lambda b, r, c_ref: (b, c_ref[0] * nr + r, 0)),
                      pl.BlockSpec((1, tr, cols), lambda b, r, c_ref: (b, r, 0))],
            out_specs=pl.BlockSpec((1, tr, cols), lambda b, r, c_ref: (b, r, 0))),
        out_shape=_sds((lead, rows, cols), part.dtype),
        compiler_params=_params("arbitrary", "arbitrary"),
    )(cidx, f3, p3)
    return out.reshape(shape)


def _to_other_chips(name, arrays, blocked):
    def src(ins, outs, pos, a, mask):
        return ins[a].at[_chip(pos) ^ mask] if blocked[a] else ins[a]

    outs = [_sds((3,) + (a.shape[1:] if b else a.shape), a.dtype) for a, b in zip(arrays, blocked)]
    phase = []
    for mi, mask in enumerate(CHIP_MASKS):
        for a in range(len(arrays)):
            phase.append((functools.partial(src, a=a, mask=mask), lambda ins, outs, pos, a=a, mi=mi: outs[a].at[mi],
                          functools.partial(_other_chip, mask=mask)))
    return _exchange(name, arrays, outs, [phase])


def _add_chips(name, own, got, jidx, blocked):
    rows, cols = got.shape[-2:]
    tr = rows // 2 if rows % 16 == 0 else rows
    nr = rows // tr
    o3 = own if blocked else own.reshape((1, rows, cols))

    def body(j_ref, o_ref, g_ref, out_ref):
        out_ref[...] = ((o_ref[0].astype(F32) + g_ref[0].astype(F32))
                        + (g_ref[1].astype(F32) + g_ref[2].astype(F32)))

    own_map = (lambda r, j_ref: (j_ref[0], r, 0)) if blocked else (lambda r, j_ref: (0, r, 0))
    return _call(
        body, name=name,
        grid_spec=pltpu.PrefetchScalarGridSpec(
            num_scalar_prefetch=1, grid=(nr,),
            in_specs=[pl.BlockSpec((1, tr, cols), own_map),
                      pl.BlockSpec((3, tr, cols), lambda r, j_ref: (0, r, 0))],
            out_specs=pl.BlockSpec((tr, cols), lambda r, j_ref: (r, 0))),
        out_shape=_sds((rows, cols)),
        compiler_params=_params("arbitrary"),
    )(jidx, o3, got)


def _to_sibling(name, arrays):
    phase = [(lambda ins, outs, pos, a=a: ins[a], lambda ins, outs, pos, a=a: outs[a], _sibling)
             for a in range(len(arrays))]
    return _exchange(name, arrays, [_sds(a.shape, a.dtype) for a in arrays], [phase])


def _local_step(x, target, w_pad, w_out, conv_w, norm_w, pool_w, pool_scale, a_log, dt_bias, dn_norm_w, final_norm_w):
    proj, n_t = _proj_fwd(x, norm_w, w_pad)
    y_pool = _pool_fwd(proj, pool_w, pool_scale)
    qn, kn, vs, beta, g = _conv_fwd(proj, conv_w, a_log, dt_bias)
    u, w, att, qd, kd, tm, cd = _intra_fwd(qn, kn, vs, beta, g)
    o, vn, st = _scan_fwd(u, w, att, qd, kd, cd)
    y_t, dh, dyp, do, ddz, loss, g_fnw, g_dnw = _out_fwd_bwd(x, y_pool, o, proj, target, w_out, dn_norm_w, final_norm_w)
    g_wout = _token_matmul("grad_w_out", y_t, [(dh, 0), (dh, 1)])
    dpu, dpz, g_pw, g_ps = _pool_bwd(proj, dyp, pool_w, pool_scale)
    du, dw, datt, dqd, dkd, dcd = _scan_bwd(do, vn, qd, kd, w, att, cd, st)
    dqn, dkn, dvs, dbeta, dg = _intra_bwd(qn, kn, vs, beta, g, tm, du, dw, datt, dqd, dkd, dcd)
    dyq, dyk, dyv, dba, g_cw, g_sm = _conv_bwd_pre(proj, conv_w, a_log, dt_bias, dqn, dkn, dvs, dbeta, dg)
    dcq, dck, dcv = _conv_bwd_in(dyq, dyk, dyv, conv_w)
    pieces = [dpu, dpz, dcq, dck, dcv, ddz, dba]
    gx, g_nw = _in_bwd(x, dh, norm_w, w_pad, pieces)
    g_win = _token_matmul("grad_w_in", n_t, [(p, 0) for p in pieces])
    small = dict(norm_w=g_nw, pool_w=g_pw, pool_scale=g_ps, conv_w=g_cw[:CONV_K], a_log=g_sm[0:1, 0:N_HEADS],
                 dt_bias=g_sm[0:1, N_HEADS:2 * N_HEADS], dn_norm_w=g_dnw, final_norm_w=g_fnw)
    return loss[0, 0], gx, g_win, g_wout, small


def _pack_small(t):
    lanes = lambda a: jnp.pad(a.reshape(1, -1), ((0, 0), (0, HEAD - a.size)))
    rows = [t["pool_w"].reshape(-1, HEAD), t["norm_w"].reshape(-1, HEAD), t["final_norm_w"].reshape(-1, HEAD),
            t["pool_scale"].reshape(-1, HEAD), t["conv_w"].reshape(-1, HEAD), t["dn_norm_w"].reshape(1, HEAD),
            lanes(t["a_log"]), lanes(t["dt_bias"]), lanes(t.get("loss", jnp.zeros((1,), F32)))]
    buf = jnp.concatenate(rows, axis=0)
    return jnp.pad(buf, ((0, SMALL_ROWS - buf.shape[0]), (0, 0)))


def _unpack_small(buf, conv_cols):
    out, r = {}, 0
    for name, nrow, shape in (("pool_w", 512, (1, N_HEADS, HEAD, HEAD)), ("norm_w", 8, (1, D_MODEL)),
                              ("final_norm_w", 8, (D_MODEL,)), ("pool_scale", 4, (1, D_HALF)),
                              ("conv_w", CONV_K * conv_cols // HEAD, (1, CONV_K, conv_cols)), ("dn_norm_w", 1, (1, HEAD))):
        out[name] = buf[r:r + nrow].reshape(shape)
        r += nrow
    out["a_log"] = buf[r:r + 1, :N_HEADS]
    out["dt_bias"] = buf[r + 1:r + 2, :N_HEADS]
    out["loss"] = buf[r + 2, 0]
    return out


def kernel(x, norm_w, w_in, pool_w, pool_scale, conv_w, a_log, dt_bias, dn_norm_w, w_out, final_norm_w, loss_target, m_norm_w, m_w_in, m_pool_w, m_pool_scale, m_conv_w, m_a_log, m_dt_bias, m_dn_norm_w, m_w_out, m_final_norm_w, v_norm_w, v_w_in, v_pool_w, v_pool_scale, v_conv_w, v_a_log, v_dt_bias, v_dn_norm_w, v_w_out, v_final_norm_w):
    cidx = lax.axis_index("c").astype(I32).reshape(1)
    jidx = (2 * lax.axis_index("x") + lax.axis_index("y")).astype(I32)

    wb = jnp.pad(w_in[0].astype(BF16), ((0, 0), (0, BLK_IN_PAD - BLK_IN)))
    ob = w_out[0].astype(BF16)
    gw, go, gc = _gather_weights(wb, ob, conv_w[0])
    mine = lambda j: jidx == j
    w_pad = jnp.concatenate([jnp.where(mine(j), wb[:, :BLK_IN], gw[j, :, :BLK_IN]) for j in range(4)]
                            + [jnp.zeros((D_MODEL, N_IN_PAD - N_IN), BF16)], axis=1)
    wo_full = jnp.where((jnp.arange(4) == jidx)[:, None, None], ob[None], go).reshape(D_MODEL, D_MODEL)
    cw_full = jnp.concatenate([jnp.where(mine(j), conv_w[0], gc[j]) for j in range(4)], axis=1)

    loss, gx, g_win, g_wout, small = _local_step(
        x[0], loss_target[0], w_pad, wo_full, cw_full, norm_w, pool_w[0], pool_scale, a_log, dt_bias, dn_norm_w,
        final_norm_w.reshape(1, D_MODEL))
    small["loss"] = loss

    blocks_in = jnp.stack([jnp.pad(g_win[:, j * BLK_IN:(j + 1) * BLK_IN].astype(BF16), ((0, 0), (0, BLK_IN_PAD - BLK_IN)))
                           for j in range(4)])
    blocks_out = g_wout.astype(BF16).reshape(4, BLK_OUT, D_MODEL)
    full = [blocks_in, blocks_out, _pack_small(small)]
    from_sib = _to_sibling_half("reduce_sibling", full)
    chip_sum = [_add_half("add_sibling_%d" % i, f, p, cidx) for i, (f, p) in enumerate(zip(full, from_sib))]
    blocked = [True, True, False]
    from_chips = _to_other_chips("reduce_chips", chip_sum, blocked)
    halves = [_add_chips("add_chips_%d" % i, o, g, jidx.reshape(1), b)
              for i, (o, g, b) in enumerate(zip(chip_sum, from_chips, blocked))]
    other_halves = _to_sibling("swap_halves", halves)

    weights = dict(norm_w=norm_w, w_in=w_in, pool_w=pool_w, pool_scale=pool_scale, conv_w=conv_w, a_log=a_log,
                   dt_bias=dt_bias, dn_norm_w=dn_norm_w, w_out=w_out, final_norm_w=final_norm_w)
    ms = dict(norm_w=m_norm_w, w_in=m_w_in, pool_w=m_pool_w, pool_scale=m_pool_scale, conv_w=m_conv_w, a_log=m_a_log,
              dt_bias=m_dt_bias, dn_norm_w=m_dn_norm_w, w_out=m_w_out, final_norm_w=m_final_norm_w)
    vs = dict(norm_w=v_norm_w, w_in=v_w_in, pool_w=v_pool_w, pool_scale=v_pool_scale, conv_w=v_conv_w, a_log=v_a_log,
              dt_bias=v_dt_bias, dn_norm_w=v_dn_norm_w, w_out=v_w_out, final_norm_w=v_final_norm_w)
    names = ["norm_w", "w_in", "pool_w", "pool_scale", "conv_w", "a_log", "dt_bias", "dn_norm_w", "w_out", "final_norm_w"]
    small_names = [n for n in names if n not in ("w_in", "w_out")]

    def pack(t):
        conv = lax.dynamic_update_slice_in_dim(jnp.zeros((CONV_K, 3 * D_HALF), F32), t["conv_w"][0], jidx * BLK_CONV, axis=1)
        return _pack_small({**{n: t[n] for n in small_names if n != "conv_w"}, "conv_w": conv})[None]

    results = [{}, {}, {}, {}]
    for i, name in enumerate(("w_in", "w_out")):
        outs = _adamw_shard("adamw_" + name, weights[name], halves[i], other_halves[i], cidx, ms[name], vs[name])
        for res, o in zip(results, outs):
            res[name] = o
    outs = _adamw_shard("adamw_small", pack(weights), halves[2], other_halves[2], cidx, pack(ms), pack(vs))
    for res, o in zip(results, outs):
        got = _unpack_small(o[0], 3 * D_HALF)
        got["conv_w"] = lax.dynamic_slice_in_dim(got["conv_w"], jidx * BLK_CONV, BLK_CONV, axis=2)
        res.update(got)
    grads, delta, new_m, new_v = results

    return (grads["loss"], gx[None], *[grads[n] for n in names], *[delta[n] for n in names],
            *[new_m[n] for n in names], *[new_v[n] for n in names])
```

```python
import functools

import jax
import jax.numpy as jnp
from jax import lax
from jax.experimental import pallas as pl
from jax.experimental.pallas import tpu as pltpu

F32 = jnp.float32
BF16 = jnp.bfloat16
I32 = jnp.int32

D_MODEL = 1024
D_HALF = 512
N_HEADS = 4
HEAD = 128
CHUNK = 64
PAIR = 2 * CHUNK
WINDOWS = (2, 4, 8, 16)
CONV_K = 4
EPS = 1e-6
N_IN = 3080
N_IN_PAD = 3200
BLK_IN = 770
BLK_IN_PAD = 896
BLK_OUT = 256
BLK_CONV = 384
COL_BA = 3072
QK_SCALE = HEAD ** -0.5
SMALL_ROWS = 592
VMEM_LIMIT = 56 * 1024 * 1024

ADAM_LR = 0.001
ADAM_B1 = 0.9
ADAM_B2 = 0.999
ADAM_EPS = 1e-08
ADAM_WD = 0.01
ADAM_STEP = 10

CHIP_MASKS = (2, 1, 3)
HEADS = range(N_HEADS)
HEAD_COLS = [slice(h * HEAD, (h + 1) * HEAD) for h in HEADS]


def _call(body, **kw):
    return pl.pallas_call(body, **kw)


def _params(*sem):
    return pltpu.CompilerParams(dimension_semantics=sem, vmem_limit_bytes=VMEM_LIMIT)


def _sds(shape, dtype=F32):
    return jax.ShapeDtypeStruct(shape, dtype)


def _bdot(a, b):
    return jnp.dot(a.astype(BF16), b.astype(BF16), preferred_element_type=F32)


def _bdot_nt(a, b):
    return lax.dot_general(a.astype(BF16), b.astype(BF16), (((1,), (1,)), ((), ())), preferred_element_type=F32)


def _bdot_tn(a, b):
    return lax.dot_general(a.astype(BF16), b.astype(BF16), (((0,), (0,)), ((), ())), preferred_element_type=F32)


def _split(a):
    hi = a.astype(BF16)
    lo = (a - hi.astype(F32)).astype(BF16)
    return hi, lo


def _mask_dot(m, b, dims=(((1,), (0,)), ((), ()))):
    bh, bl = _split(b)
    dg = functools.partial(lax.dot_general, dimension_numbers=dims, preferred_element_type=F32)
    return dg(m, bh) + dg(m, bl)


def _sigmoid(x):
    return 1.0 / (1.0 + jnp.exp(-x))


def _softplus(x):
    return jnp.maximum(x, 0.0) + jnp.log(1.0 + jnp.exp(-jnp.abs(x)))


def _rowsum(x):
    return jnp.sum(x, axis=-1, keepdims=True)


def _colsum(x):
    return jnp.sum(x, axis=0, keepdims=True)


def _shift_down(xv, prev8, k):
    r = pltpu.roll(xv, k, 0)
    q = pltpu.roll(prev8, k, 0)
    row = lax.broadcasted_iota(I32, prev8.shape, 0)
    top = jnp.where(row < k, q, r[0:8])
    return jnp.concatenate([top, r[8:]], axis=0)


def _shift_up(xv, next8, k):
    t = xv.shape[0]
    r = pltpu.roll(xv, t - k, 0)
    q = pltpu.roll(next8, 8 - k, 0)
    row = lax.broadcasted_iota(I32, next8.shape, 0)
    bot = jnp.where(row >= 8 - k, q, r[t - 8:])
    return jnp.concatenate([r[:t - 8], bot], axis=0)


def _band(rows, cols, off, w, anti=False):
    r = lax.broadcasted_iota(I32, (rows, cols), 0)
    c = lax.broadcasted_iota(I32, (rows, cols), 1)
    d = (c - r + off) if anti else (r - c + off)
    return ((d >= 0) & (d < w)).astype(BF16)


def _head(ref_or_val, h):
    return ref_or_val[:, h * HEAD:(h + 1) * HEAD]


INTRA_PAIRS = 2
UNITS = [(pp, h) for pp in range(INTRA_PAIRS) for h in HEADS]


def _heads(ref, rows=PAIR):
    return [ref[pp * rows:(pp + 1) * rows, HEAD_COLS[h]] for pp, h in UNITS]


def _put_heads(ref, vals, rows=PAIR):
    for (pp, h), v in zip(UNITS, vals):
        ref[pp * rows:(pp + 1) * rows, HEAD_COLS[h]] = v.astype(ref.dtype)


def _each(fn, *lists):
    return [fn(*args) for args in zip(*lists)]


def _proj_fwd(x, norm_w, w_pad):
    s = x.shape[0]
    tm = 512

    def body(x_ref, nw_ref, w_ref, proj_ref, nt_ref):
        xv = x_ref[...]
        r = lax.rsqrt(jnp.mean(xv * xv, axis=-1, keepdims=True) + EPS)
        nv = xv * r * nw_ref[...]
        nt_ref[...] = nv.T.astype(BF16)
        proj_ref[...] = jnp.dot(nv.astype(BF16), w_ref[...], preferred_element_type=F32)

    return _call(
        body, name="proj_fwd", grid=(s // tm,),
        in_specs=[pl.BlockSpec((tm, D_MODEL), lambda i: (i, 0)),
                  pl.BlockSpec((1, D_MODEL), lambda i: (0, 0)),
                  pl.BlockSpec((D_MODEL, N_IN_PAD), lambda i: (0, 0))],
        out_specs=[pl.BlockSpec((tm, N_IN_PAD), lambda i: (i, 0)),
                   pl.BlockSpec((D_MODEL, tm), lambda i: (0, i))],
        out_shape=[_sds((s, N_IN_PAD)), _sds((D_MODEL, s), BF16)],
        compiler_params=_params("arbitrary"),
    )(x, norm_w, w_pad)


def _pool_mix(ug, hg, zg, pw_g, row0, w):
    t = ug.shape[0]
    win = _mask_dot(_band(t, t, 0, w), ug) + _mask_dot(_band(t, HEAD, HEAD, w), hg)
    cnt = jnp.minimum(row0 + lax.broadcasted_iota(I32, (t, 1), 0) + 1, w).astype(F32)
    mix = win / cnt - ug
    mixed = _bdot(mix, pw_g)
    sg = _sigmoid(zg)
    return mix, mixed, sg, cnt


POOL_T = 256


def _pool_fwd(proj, pool_w, pool_scale):
    s = proj.shape[0]
    t = POOL_T
    hb = t // HEAD

    def body(u_ref, z_ref, halo_ref, pw_ref, ps_ref, y_ref):
        i = pl.program_id(0)
        live = (i > 0).astype(F32)
        for g, w in enumerate(WINDOWS):
            sl = HEAD_COLS[g]
            zg = z_ref[:, sl]
            _, mixed, sg, _ = _pool_mix(u_ref[:, sl], halo_ref[:, sl] * live, zg, pw_ref[g], i * t, w)
            y_ref[:, sl] = mixed * ps_ref[:, sl] * (zg * sg)

    return _call(
        body, name="pool_fwd", grid=(s // t,),
        in_specs=[pl.BlockSpec((t, D_HALF), lambda i: (i, 0)),
                  pl.BlockSpec((t, D_HALF), lambda i: (i, 1)),
                  pl.BlockSpec((HEAD, D_HALF), lambda i: (jnp.maximum(i * hb - 1, 0), 0)),
                  pl.BlockSpec((N_HEADS, HEAD, HEAD), lambda i: (0, 0, 0)),
                  pl.BlockSpec((1, D_HALF), lambda i: (0, 0))],
        out_specs=pl.BlockSpec((t, D_HALF), lambda i: (i, 0)),
        out_shape=_sds((s, D_HALF)),
        compiler_params=_params("arbitrary"),
    )(proj, proj, proj, pool_w, pool_scale)


def _conv_taps(xv, prev8):
    return [_shift_down(xv, prev8, CONV_K - 1 - j) for j in range(CONV_K - 1)] + [xv]


def _conv_pre(taps, cw):
    y = taps[CONV_K - 1] * cw[CONV_K - 1:CONV_K]
    for j in range(CONV_K - 2, -1, -1):
        y = y + taps[j] * cw[j:j + 1]
    return y


CONV_T = 256
CONV_SUB = 32


def _conv_specs(t, tile_of=lambda i: i):
    tiles = [pl.BlockSpec((t, D_HALF), functools.partial(lambda i, p: (tile_of(i), 2 + p), p=p)) for p in range(3)]
    halos = [pl.BlockSpec((8, D_HALF),
                          functools.partial(lambda i, p: (jnp.maximum(tile_of(i) * (t // 8) - 1, 0), 2 + p), p=p))
             for p in range(3)]
    return tiles + halos


def _conv_fwd(proj, conv_w, a_log, dt_bias):
    s = proj.shape[0]
    t = CONV_T

    def body(q_ref, k_ref, v_ref, hq_ref, hk_ref, hv_ref, ba_ref, cw_ref, al_ref, dtb_ref,
             qn_ref, kn_ref, vs_ref, beta_ref, g_ref):
        live = (pl.program_id(0) > 0).astype(F32)
        parts = ((q_ref, hq_ref, qn_ref), (k_ref, hk_ref, kn_ref), (v_ref, hv_ref, vs_ref))

        def sub_tile(r0, first):
            rows = pl.ds(r0, CONV_SUB)
            for p, (x_ref, h_ref, o_ref) in enumerate(parts):
                for h in HEADS:
                    cs = HEAD_COLS[h]
                    prev8 = h_ref[:, cs] * live if first else x_ref[pl.ds(r0 - 8, 8), cs]
                    y = _conv_pre(_conv_taps(x_ref[rows, cs], prev8), cw_ref[:, p * D_HALF + h * HEAD:p * D_HALF + (h + 1) * HEAD])
                    sv = y * _sigmoid(y)
                    o_ref[rows, cs] = sv if p == 2 else sv * lax.rsqrt(_rowsum(sv * sv) + EPS)
            ba = ba_ref[rows, :]
            for h in HEADS:
                beta = _sigmoid(ba[:, h:h + 1])
                gl = -jnp.exp(al_ref[0:1, h:h + 1]) * _softplus(ba[:, N_HEADS + h:N_HEADS + h + 1] + dtb_ref[0:1, h:h + 1])
                beta_ref[rows, HEAD_COLS[h]] = jnp.broadcast_to(beta, (CONV_SUB, HEAD))
                g_ref[rows, HEAD_COLS[h]] = jnp.broadcast_to(gl, (CONV_SUB, HEAD))

        sub_tile(0, True)

        def step(k, carry):
            sub_tile(pl.multiple_of(k * CONV_SUB, CONV_SUB), False)
            return carry

        lax.fori_loop(1, t // CONV_SUB, step, 0)

    row = pl.BlockSpec((t, D_HALF), lambda i: (i, 0))
    return _call(
        body, name="conv_fwd", grid=(s // t,),
        in_specs=_conv_specs(t) + [pl.BlockSpec((t, HEAD), lambda i: (i, COL_BA // HEAD)),
                                   pl.BlockSpec((CONV_K, 3 * D_HALF), lambda i: (0, 0)),
                                   pl.BlockSpec((1, N_HEADS), lambda i: (0, 0)),
                                   pl.BlockSpec((1, N_HEADS), lambda i: (0, 0))],
        out_specs=[row] * 5,
        out_shape=[_sds((s, D_HALF))] * 5,
        compiler_params=_params("arbitrary"),
    )(proj, proj, proj, proj, proj, proj, proj, conv_w, a_log, dt_bias)


def _pair_masks():
    r = lax.broadcasted_iota(I32, (PAIR, PAIR), 0)
    c = lax.broadcasted_iota(I32, (PAIR, PAIR), 1)
    same = jnp.right_shift(r, 6) == jnp.right_shift(c, 6)
    return same, same & (r >= c), same & (r > c), r == c


def _pair_common(qn, kn, vs, beta, g):
    same, incl, strict, eye = _pair_masks()
    incl_b = incl.astype(BF16)
    first = lax.broadcasted_iota(I32, (PAIR, HEAD), 0) < CHUNK
    gc = _each(lambda gv: _mask_dot(incl_b, gv), g)
    gc_row = _each(lambda v: _colsum(jnp.where(eye, v, 0.0)), gc)
    decay = _each(lambda v, r: jnp.where(incl, jnp.exp(jnp.where(incl, v - r, 0.0)), 0.0), gc, gc_row)
    gl = _each(lambda v: jnp.where(first, v[CHUNK - 1:CHUNK], v[PAIR - 1:PAIR]), gc)
    egc = _each(jnp.exp, gc)
    q = _each(lambda v: v * QK_SCALE, qn)
    kb = _each(lambda k, b: k * b, kn, beta)
    return dict(same=same, incl=incl, strict=strict, eye=eye, gc=gc, decay=decay, gl=gl, egc=egc,
                ekd=_each(lambda a, b: jnp.exp(a - b), gl, gc), cd=_each(jnp.exp, gl), q=q, kb=kb,
                vb=_each(lambda v, b: v * b, vs, beta), kbg=_each(lambda k, e: k * e, kb, egc),
                kk=_each(_bdot_nt, kb, kn), qk=_each(_bdot_nt, q, kn))


def _tri_inv(a, eye_f):
    p = _each(lambda v: eye_f - v, a)
    x = _each(_bdot, a, a)
    for it in range(5):
        p = _each(lambda pv, xv: pv + _bdot(pv, xv), p, x)
        if it < 4:
            x = _each(_bdot, x, x)
    return p


def _pair_spec():
    return pl.BlockSpec((INTRA_PAIRS * PAIR, D_HALF), lambda i: (i, 0))


def _chunk_scalar_spec(pairs=1, index=lambda i: (i, 0)):
    return pl.BlockSpec((16 * pairs, D_HALF), index)


SCAN_PAIRS = 2
SCAN_ROWS = SCAN_PAIRS * PAIR


def _intra_fwd(qn, kn, vs, beta, g):
    s = qn.shape[0]

    def body(qn_ref, kn_ref, vs_ref, beta_ref, g_ref, u_ref, w_ref, att_ref, qd_ref, kd_ref, t_ref, cd_ref):
        kn = _heads(kn_ref)
        cm = _pair_common(_heads(qn_ref), kn, _heads(vs_ref), _heads(beta_ref), _heads(g_ref))
        a = _each(lambda kk, d: jnp.where(cm["strict"], kk * d, 0.0), cm["kk"], cm["decay"])
        tm = _tri_inv(a, cm["eye"].astype(F32))
        _put_heads(t_ref, tm)
        _put_heads(u_ref, _each(_bdot, tm, cm["vb"]))
        _put_heads(w_ref, _each(_bdot, tm, cm["kbg"]))
        _put_heads(att_ref, _each(lambda a, b: a * b, cm["qk"], cm["decay"]))
        _put_heads(qd_ref, _each(lambda a, b: a * b, cm["q"], cm["egc"]))
        _put_heads(kd_ref, _each(lambda a, b: a * b, kn, cm["ekd"]))
        for ci in range(2):
            for (pp, h), v in zip(UNITS, cm["cd"]):
                cd_ref[pp * 16 + ci * 8:pp * 16 + (ci + 1) * 8, HEAD_COLS[h]] = v[ci * CHUNK:ci * CHUNK + 8]

    return _call(
        body, name="intra_fwd", grid=(s // (INTRA_PAIRS * PAIR),),
        in_specs=[_pair_spec()] * 5, out_specs=[_pair_spec()] * 6 + [_chunk_scalar_spec(INTRA_PAIRS)],
        out_shape=[_sds((s, D_HALF))] + [_sds((s, D_HALF), BF16)] * 5 + [_sds((s // 8, D_HALF))],
        compiler_params=_params("arbitrary"),
    )(qn, kn, vs, beta, g)


def _scan_fwd(u, w, att, qd, kd, cd):
    s = u.shape[0]
    n_chunks = s // CHUNK

    def body(u_ref, w_ref, att_ref, qd_ref, kd_ref, cd_ref, o_ref, vn_ref, st_ref, state):
        @pl.when(pl.program_id(0) == 0)
        def _():
            state[...] = jnp.zeros_like(state)
        cols = list(enumerate(HEAD_COLS))
        sm = [state[h] for h in HEADS]
        for ci in range(2 * SCAN_PAIRS):
            rs = slice(ci * CHUNK, (ci + 1) * CHUNK)
            for h in HEADS:
                st_ref[ci, h] = sm[h]
            both = [_bdot(jnp.concatenate([w_ref[rs, sl], qd_ref[rs, sl]], axis=0), sm[h]) for h, sl in cols]
            vn = [u_ref[rs, sl] - both[h][:CHUNK] for h, sl in cols]
            for h, sl in cols:
                vn_ref[rs, sl] = vn[h].astype(BF16)
                o_ref[rs, sl] = both[h][CHUNK:]
            sm = [sm[h] * cd_ref[ci * 8:ci * 8 + 1, sl] + _bdot_tn(kd_ref[rs, sl], vn[h]) for h, sl in cols]
        for h in HEADS:
            state[h] = sm[h]
        for pp in range(SCAN_PAIRS):
            rp = slice(pp * PAIR, (pp + 1) * PAIR)
            intra = [_bdot(att_ref[rp, sl], vn_ref[rp, sl]) for sl in HEAD_COLS]
            for h, sl in cols:
                o_ref[rp, sl] += intra[h]

    rows = pl.BlockSpec((SCAN_ROWS, D_HALF), lambda i: (i, 0))
    return _call(
        body, name="scan_fwd", grid=(s // SCAN_ROWS,),
        in_specs=[rows] * 5 + [_chunk_scalar_spec(SCAN_PAIRS)],
        out_specs=[rows, rows, pl.BlockSpec((2 * SCAN_PAIRS, N_HEADS, HEAD, HEAD), lambda i: (i, 0, 0, 0))],
        out_shape=[_sds((s, D_HALF)), _sds((s, D_HALF), BF16), _sds((n_chunks, N_HEADS, HEAD, HEAD))],
        scratch_shapes=[pltpu.VMEM((N_HEADS, HEAD, HEAD), F32)],
        compiler_params=_params("arbitrary"),
    )(u, w, att, qd, kd, cd)


OUT_T = 512


def _out_fwd_bwd(x, y_pool, o, proj, target, w_out, dn_norm_w, final_norm_w):
    s = x.shape[0]
    t = OUT_T

    def body(x_ref, yp_ref, o_ref, z_ref, tg_ref, wo_ref, dnw_ref, fnw_ref,
             yt_ref, dh_ref, dyp_ref, do_ref, dz_ref, loss_ref, gfn_ref, gdn_ref, y_ref):
        @pl.when(pl.program_id(0) == 0)
        def _():
            loss_ref[...] = jnp.zeros_like(loss_ref)
            gfn_ref[...] = jnp.zeros_like(gfn_ref)
            gdn_ref[...] = jnp.zeros_like(gdn_ref)

        ypv = yp_ref[...]
        y_ref[:, :D_HALF] = ypv.astype(BF16)
        yt_ref[:D_HALF, :] = ypv.T.astype(BF16)
        dnw = dnw_ref[...]
        keep = []
        for h in HEADS:
            ov = o_ref[:, HEAD_COLS[h]]
            zv = z_ref[:, HEAD_COLS[h]]
            ro = lax.rsqrt(jnp.mean(ov * ov, axis=-1, keepdims=True) + EPS)
            ohat = ov * ro
            sg = _sigmoid(zv)
            keep.append((ro, ohat, zv, sg))
            ydn = ohat * dnw * (zv * sg)
            y_ref[:, D_HALF + h * HEAD:D_HALF + (h + 1) * HEAD] = ydn.astype(BF16)
            yt_ref[D_HALF + h * HEAD:D_HALF + (h + 1) * HEAD, :] = ydn.T.astype(BF16)

        hv = x_ref[...] + jnp.dot(y_ref[...], wo_ref[...], preferred_element_type=F32)
        r2 = lax.rsqrt(jnp.mean(hv * hv, axis=-1, keepdims=True) + EPS)
        hhat = hv * r2
        fnw = fnw_ref[...]
        err = hhat * fnw - tg_ref[...]
        loss_ref[...] += 0.5 * jnp.sum(_rowsum(err * err) * (1.0 / D_MODEL), axis=0, keepdims=True)
        dout = err * (1.0 / D_MODEL)
        gfn_ref[...] += _colsum(dout * hhat)
        dhh = dout * fnw
        dh = r2 * (dhh - hhat * jnp.mean(dhh * hhat, axis=-1, keepdims=True))
        dh_ref[...] = dh
        dy = _bdot_nt(dh, wo_ref[...])
        dyp_ref[...] = dy[:, :D_HALF]
        gdn = jnp.zeros((1, HEAD), F32)
        for h in HEADS:
            ro, ohat, zv, sg = keep[h]
            dyd = dy[:, D_HALF + h * HEAD:D_HALF + (h + 1) * HEAD]
            sz = zv * sg
            dz_ref[:, HEAD_COLS[h]] = (dyd * ohat * dnw * (sg * (1.0 + zv * (1.0 - sg)))).astype(BF16)
            gdn = gdn + _colsum(dyd * ohat * sz)
            doh = dyd * dnw * sz
            do_ref[:, HEAD_COLS[h]] = ro * (doh - ohat * jnp.mean(doh * ohat, axis=-1, keepdims=True))
        gdn_ref[...] += gdn

    wide = pl.BlockSpec((t, D_MODEL), lambda i: (i, 0))
    half = pl.BlockSpec((t, D_HALF), lambda i: (i, 0))
    const = lambda shape: pl.BlockSpec(shape, lambda i: (0,) * len(shape))
    return _call(
        body, name="out_fwd_bwd", grid=(s // t,),
        in_specs=[wide, half, half, pl.BlockSpec((t, D_HALF), lambda i: (i, 5)), wide,
                  const((D_MODEL, D_MODEL)), const((1, HEAD)), const((1, D_MODEL))],
        out_specs=[pl.BlockSpec((D_MODEL, t), lambda i: (0, i)), wide, half, half, half,
                   const((1, HEAD)), const((1, D_MODEL)), const((1, HEAD))],
        out_shape=[_sds((D_MODEL, s), BF16), _sds((s, D_MODEL)), _sds((s, D_HALF)), _sds((s, D_HALF)), _sds((s, D_HALF), BF16),
                   _sds((1, HEAD)), _sds((1, D_MODEL)), _sds((1, HEAD))],
        scratch_shapes=[pltpu.VMEM((t, D_MODEL), BF16)],
        compiler_params=_params("arbitrary"),
    )(x, y_pool, o, proj, target, w_out, dn_norm_w, final_norm_w)


def _token_matmul(name, at, pieces):
    m, s = at.shape
    n = len(pieces)
    tn, tk = D_HALF, 512

    def body(a_ref, *refs):
        p_refs, o_ref = refs[:n], refs[n]

        @pl.when(pl.program_id(0) == 0)
        def _():
            o_ref[...] = jnp.zeros_like(o_ref)

        av = a_ref[...]
        for p in range(n):
            o_ref[:, p * tn:(p + 1) * tn] += _bdot(av, p_refs[p][...])

    return _call(
        body, name=name, grid=(s // tk,),
        in_specs=[pl.BlockSpec((m, tk), lambda k: (0, k))]
                 + [pl.BlockSpec((tk, tn), functools.partial(lambda k, cb: (k, cb), cb=cb)) for _, cb in pieces],
        out_specs=pl.BlockSpec((m, n * tn), lambda k: (0, 0)),
        out_shape=_sds((m, n * tn)),
        compiler_params=_params("arbitrary"),
    )(at, *[p[0] for p in pieces])


def _pool_bwd(proj, dyp, pool_w, pool_scale):
    s = proj.shape[0]
    t = POOL_T
    hb = t // HEAD
    last = s // HEAD - 1

    def body(u_ref, z_ref, halo_ref, dy_ref, zn_ref, dyn_ref, pw_ref, ps_ref, du_ref, dz_ref, gpw_ref, gps_ref):
        i = pl.program_id(0)

        @pl.when(i == 0)
        def _():
            gpw_ref[...] = jnp.zeros_like(gpw_ref)
            gps_ref[...] = jnp.zeros_like(gps_ref)

        live = (i > 0).astype(F32)
        more = (i < pl.num_programs(0) - 1).astype(F32)
        for g, w in enumerate(WINDOWS):
            sl = HEAD_COLS[g]
            zg = z_ref[:, sl]
            ps = ps_ref[:, sl]
            pw = pw_ref[g]
            mix, mixed, sg, cnt = _pool_mix(u_ref[:, sl], halo_ref[:, sl] * live, zg, pw, i * t, w)
            dyg = dy_ref[:, sl]
            sz = zg * sg
            dz_ref[:, sl] = (dyg * mixed * ps * (sg * (1.0 + zg * (1.0 - sg)))).astype(BF16)
            gps_ref[:, sl] += _colsum(dyg * mixed * sz)
            dmixed = dyg * ps * sz
            gpw_ref[g] += _bdot_tn(mix, dmixed)
            dmix = _bdot_nt(dmixed, pw)
            zn = zn_ref[:, sl]
            dmix_n = _bdot_nt(dyn_ref[:, sl] * more * ps * (zn * _sigmoid(zn)), pw)
            du_ref[:, sl] = (_mask_dot(_band(t, t, 0, w, anti=True), dmix / cnt)
                             + _mask_dot(_band(t, HEAD, t, w, anti=True), dmix_n * (1.0 / w)) - dmix).astype(BF16)

    tile = lambda col: pl.BlockSpec((t, D_HALF), lambda i: (i, col))
    below = lambda col: pl.BlockSpec((HEAD, D_HALF), lambda i: (jnp.minimum((i + 1) * hb, last), col))
    return _call(
        body, name="pool_bwd", grid=(s // t,),
        in_specs=[tile(0), tile(1), pl.BlockSpec((HEAD, D_HALF), lambda i: (jnp.maximum(i * hb - 1, 0), 0)),
                  tile(0), below(1), below(0),
                  pl.BlockSpec((N_HEADS, HEAD, HEAD), lambda i: (0, 0, 0)), pl.BlockSpec((1, D_HALF), lambda i: (0, 0))],
        out_specs=[tile(0), tile(0), pl.BlockSpec((N_HEADS, HEAD, HEAD), lambda i: (0, 0, 0)),
                   pl.BlockSpec((1, D_HALF), lambda i: (0, 0))],
        out_shape=[_sds((s, D_HALF), BF16), _sds((s, D_HALF), BF16), _sds((N_HEADS, HEAD, HEAD)), _sds((1, D_HALF))],
        compiler_params=_params("arbitrary"),
    )(proj, proj, proj, dyp, proj, dyp, pool_w, pool_scale)


def _scan_bwd(do, vn, qd, kd, w, att, cd, st):
    s = do.shape[0]
    n_steps = s // SCAN_ROWS

    def body(do_ref, vn_ref, qd_ref, kd_ref, w_ref, att_ref, cd_ref, st_ref,
             du_ref, dw_ref, datt_ref, dqd_ref, dkd_ref, dcd_ref, dstate):
        @pl.when(pl.program_id(0) == 0)
        def _():
            dstate[...] = jnp.zeros_like(dstate)
        _, incl, _, _ = _pair_masks()
        cols = list(enumerate(HEAD_COLS))
        dv_intra = []
        for pp in range(SCAN_PAIRS):
            rp = slice(pp * PAIR, (pp + 1) * PAIR)
            dv_intra.append([_bdot_tn(att_ref[rp, sl], do_ref[rp, sl]) for _, sl in cols])
            for _, sl in cols:
                datt_ref[rp, sl] = jnp.where(incl, _bdot_nt(do_ref[rp, sl], vn_ref[rp, sl]), 0.0)
        ds = [dstate[h] for h in HEADS]
        for ci in range(2 * SCAN_PAIRS - 1, -1, -1):
            rs = slice(ci * CHUNK, (ci + 1) * CHUNK)
            in_pair = slice((ci % 2) * CHUNK, (ci % 2 + 1) * CHUNK)
            sm = [st_ref[ci, h] for h in HEADS]
            dvn = [dv_intra[ci // 2][h][in_pair] + _bdot(kd_ref[rs, sl], ds[h]) for h, sl in cols]
            for h, sl in cols:
                du_ref[rs, sl] = dvn[h].astype(BF16)
            dqd = [_bdot_nt(do_ref[rs, sl], sm[h]) for h, sl in cols]
            dw = [-_bdot_nt(dvn[h], sm[h]) for h, _ in cols]
            dkd = [_bdot_nt(vn_ref[rs, sl], ds[h]) for h, sl in cols]
            dcd = [jnp.broadcast_to(_rowsum(_colsum(ds[h] * sm[h])), (8, HEAD)) for h in HEADS]
            for h, sl in cols:
                dqd_ref[rs, sl] = dqd[h]
                dw_ref[rs, sl] = dw[h].astype(BF16)
                dkd_ref[rs, sl] = dkd[h]
                dcd_ref[ci * 8:(ci + 1) * 8, sl] = dcd[h]
            ds = [ds[h] * cd_ref[ci * 8:ci * 8 + 1, sl] + _bdot_tn(qd_ref[rs, sl], do_ref[rs, sl])
                  - _bdot_tn(w_ref[rs, sl], dvn[h]) for h, sl in cols]
        for h in HEADS:
            dstate[h] = ds[h]

    rev = pl.BlockSpec((SCAN_ROWS, D_HALF), lambda i: (n_steps - 1 - i, 0))
    rev_scalar = _chunk_scalar_spec(SCAN_PAIRS, lambda i: (n_steps - 1 - i, 0))
    return _call(
        body, name="scan_bwd", grid=(n_steps,),
        in_specs=[rev] * 6 + [rev_scalar,
                              pl.BlockSpec((2 * SCAN_PAIRS, N_HEADS, HEAD, HEAD), lambda i: (n_steps - 1 - i, 0, 0, 0))],
        out_specs=[rev] * 5 + [rev_scalar],
        out_shape=[_sds((s, D_HALF), BF16)] * 2 + [_sds((s, D_HALF))] * 3 + [_sds((s // 8, D_HALF))],
        scratch_shapes=[pltpu.VMEM((N_HEADS, HEAD, HEAD), F32)],
        compiler_params=_params("arbitrary"),
    )(do, vn, qd, kd, w, att, cd, st)


def _intra_bwd(qn, kn, vs, beta, g, tm, du, dw, datt, dqd, dkd, dcd):
    s = qn.shape[0]

    def body(qn_ref, kn_ref, vs_ref, beta_ref, g_ref, t_ref, du_ref, dw_ref, datt_ref, dqd_ref, dkd_ref, dcd_ref,
             dqn_ref, dkn_ref, dvs_ref, dbeta_ref, dg_ref):
        ones = jnp.ones((PAIR, HEAD), BF16)
        tn = (((0,), (0,)), ((), ()))
        kn, vs, beta = _heads(kn_ref), _heads(vs_ref), _heads(beta_ref)
        cm = _pair_common(_heads(qn_ref), kn, vs, beta, _heads(g_ref))
        tmv, duv, dwv, dattv, dqdv, dkdv = (_heads(r) for r in (t_ref, du_ref, dw_ref, datt_ref, dqd_ref, dkd_ref))
        dvb = _each(_bdot_tn, tmv, duv)
        dt = _each(lambda a, b, c, d: _bdot_nt(a, b) + _bdot_nt(c, d), duv, cm["vb"], dwv, cm["kbg"])
        dkbg = _each(_bdot_tn, tmv, dwv)
        m1 = _each(_bdot_tn, tmv, dt)
        da = _each(lambda a, b: -jnp.where(cm["strict"], _bdot_nt(a, b), 0.0), m1, tmv)
        dkk = _each(lambda a, b: a * b, da, cm["decay"])
        dqk = _each(lambda a, b: a * b, dattv, cm["decay"])
        dd = _each(lambda a, b, c, d: a * b + c * d, dkk, cm["kk"], dqk, cm["qk"])
        dkb = _each(lambda a, b, c, d: _bdot(a, b) + c * d, dkk, kn, dkbg, cm["egc"])
        dq = _each(lambda a, b, c, d: _bdot(a, b) + c * d, dqk, kn, dqdv, cm["egc"])
        dkn = _each(lambda a, b, c, d: _bdot_tn(a, b) + _bdot_tn(c, d), dkk, cm["kb"], dqk, cm["q"])
        dkn = _each(lambda a, b, c, d, e: a + b * c + d * e, dkn, dkdv, cm["ekd"], dkb, beta)
        t_kd = _each(lambda a, b, c: _rowsum(a * b * c), dkdv, kn, cm["ekd"])
        split = _each(_split, dd)
        rows_dd = [jnp.dot(hi, ones, preferred_element_type=F32) + jnp.dot(lo, ones, preferred_element_type=F32)
                   for hi, lo in split]
        cols_dd = [lax.dot_general(hi, ones, tn, preferred_element_type=F32)
                   + lax.dot_general(lo, ones, tn, preferred_element_type=F32) for hi, lo in split]
        dgc = _each(lambda r, c, a, b, e, f, k, t: r - c + _rowsum(a * b * e) + _rowsum(f * k) - t,
                    rows_dd, cols_dd, dqdv, cm["q"], cm["egc"], dkbg, cm["kbg"], t_kd)
        same_b = cm["same"].astype(BF16)
        rowi = lax.broadcasted_iota(I32, (PAIR, HEAD), 0)
        dcd = _each(lambda d: jnp.where(rowi < CHUNK, d[0:1], d[8:9]), _heads(dcd_ref, rows=16))
        dgl = _each(lambda t, d, c: _mask_dot(same_b, jnp.broadcast_to(t, (PAIR, HEAD))) + d * c, t_kd, dcd, cm["cd"])
        is_last = jnp.bitwise_and(rowi, CHUNK - 1) == CHUNK - 1
        dgc = _each(lambda a, b: a + jnp.where(is_last, b, 0.0), dgc, dgl)
        r = lax.broadcasted_iota(I32, (PAIR, PAIR), 0)
        c = lax.broadcasted_iota(I32, (PAIR, PAIR), 1)
        upper_b = (cm["same"] & (r <= c)).astype(BF16)
        _put_heads(dg_ref, _each(lambda v: _mask_dot(upper_b, v), dgc))
        _put_heads(dbeta_ref, _each(lambda a, b, c, d: jnp.broadcast_to(_rowsum(a * b) + _rowsum(c * d), (PAIR, HEAD)),
                                    dkb, kn, dvb, vs))
        _put_heads(dqn_ref, _each(lambda v: v * QK_SCALE, dq))
        _put_heads(dkn_ref, dkn)
        _put_heads(dvs_ref, _each(lambda a, b: a * b, dvb, beta))

    return _call(
        body, name="intra_bwd", grid=(s // (INTRA_PAIRS * PAIR),),
        in_specs=[_pair_spec()] * 11 + [_chunk_scalar_spec(INTRA_PAIRS)], out_specs=[_pair_spec()] * 5,
        out_shape=[_sds((s, D_HALF))] * 5,
        compiler_params=_params("arbitrary"),
    )(qn, kn, vs, beta, g, tm, du, dw, datt, dqd, dkd, dcd)


def _rows8(x):
    acc = x[0:8]
    for r in range(8, x.shape[0], 8):
        acc = acc + x[r:r + 8]
    return acc


def _conv_bwd(proj, conv_w, a_log, dt_bias, dqn, dkn, dvs, dbeta, dg):
    s = proj.shape[0]
    t = CONV_T
    n_tiles = s // t
    n_sub = t // CONV_SUB
    tile_of = lambda i: n_tiles - 1 - i

    def body(q_ref, k_ref, v_ref, hq_ref, hk_ref, hv_ref, ba_ref, cw_ref, al_ref, dtb_ref,
             dqn_ref, dkn_ref, dvs_ref, dbeta_ref, dg_ref, oq_ref, ok_ref, ov_ref, dba_ref, gcw_out, gsm_out,
             below, gcw_ref, gsm_ref):
        @pl.when(pl.program_id(0) == 0)
        def _():
            gcw_ref[...] = jnp.zeros_like(gcw_ref)
            gsm_ref[...] = jnp.zeros_like(gsm_ref)
            below[...] = jnp.zeros_like(below)

        live = (pl.program_id(0) < n_tiles - 1).astype(F32)
        parts = ((q_ref, hq_ref, dqn_ref, oq_ref), (k_ref, hk_ref, dkn_ref, ok_ref), (v_ref, hv_ref, dvs_ref, ov_ref))
        lane = lax.broadcasted_iota(I32, (CONV_SUB, HEAD), 1)
        lane8 = lax.broadcasted_iota(I32, (8, HEAD), 1)

        def sub_tile(r0, first):
            rows = pl.ds(r0, CONV_SUB)
            for p, (x_ref, h_ref, d_ref, o_ref) in enumerate(parts):
                for h in HEADS:
                    cs = HEAD_COLS[h]
                    wide = slice(p * D_HALF + h * HEAD, p * D_HALF + (h + 1) * HEAD)
                    cw = cw_ref[:, wide]
                    prev8 = h_ref[:, cs] * live if first else x_ref[pl.ds(r0 - 8, 8), cs]
                    taps = _conv_taps(x_ref[rows, cs], prev8)
                    y = _conv_pre(taps, cw)
                    sg = _sigmoid(y)
                    sv = y * sg
                    ds = d_ref[rows, cs]
                    if p < 2:
                        rn = lax.rsqrt(_rowsum(sv * sv) + EPS)
                        nrm = sv * rn
                        ds = rn * (ds - nrm * _rowsum(ds * nrm))
                    dy = ds * (sg * (1.0 + y * (1.0 - sg)))
                    for j in range(CONV_K):
                        gcw_ref[8 * j:8 * j + 8, wide] += _rows8(dy * taps[j])
                    nxt = below[:, wide]
                    acc = dy * cw[CONV_K - 1:CONV_K]
                    for sft in range(1, CONV_K):
                        acc = acc + _shift_up(dy, nxt, sft) * cw[CONV_K - 1 - sft:CONV_K - sft]
                    o_ref[rows, cs] = acc.astype(BF16)
                    below[:, wide] = dy[0:8]

            ba = ba_ref[rows, :]
            dba = jnp.zeros((CONV_SUB, HEAD), F32)
            gsm = jnp.zeros((8, HEAD), F32)
            for h in HEADS:
                beta = _sigmoid(ba[:, h:h + 1])
                dbeta = dbeta_ref[rows, h * HEAD:h * HEAD + 1]
                xg = ba[:, N_HEADS + h:N_HEADS + h + 1] + dtb_ref[0:1, h:h + 1]
                nexp = -jnp.exp(al_ref[0:1, h:h + 1])
                dgv = dg_ref[rows, h * HEAD:h * HEAD + 1]
                da = dgv * nexp * _sigmoid(xg)
                dba = dba + jnp.where(lane == h, dbeta * beta * (1.0 - beta), 0.0) + jnp.where(lane == N_HEADS + h, da, 0.0)
                gsm = (gsm + jnp.where(lane8 == h, _rows8(dgv * nexp * _softplus(xg)), 0.0)
                       + jnp.where(lane8 == N_HEADS + h, _rows8(da), 0.0))
            dba_ref[rows, :] = jnp.zeros((CONV_SUB, D_HALF), BF16)
            dba_ref[rows, :HEAD] = dba.astype(BF16)
            gsm_ref[...] += gsm

        def step(k, carry):
            sub_tile(pl.multiple_of((n_sub - 1 - k) * CONV_SUB, CONV_SUB), False)
            return carry

        lax.fori_loop(0, n_sub - 1, step, 0)
        sub_tile(0, True)

        @pl.when(pl.program_id(0) == n_tiles - 1)
        def _():
            gcw_out[...] = jnp.zeros_like(gcw_out)
            for j in range(CONV_K):
                gcw_out[j:j + 1, :] = _colsum(gcw_ref[8 * j:8 * j + 8, :])
            gsm_out[...] = jnp.broadcast_to(_colsum(gsm_ref[...]), (8, HEAD))

    row = pl.BlockSpec((t, D_HALF), lambda i: (tile_of(i), 0))
    const = lambda shape: pl.BlockSpec(shape, lambda i: (0, 0))
    return _call(
        body, name="conv_bwd", grid=(n_tiles,),
        in_specs=_conv_specs(t, tile_of) + [pl.BlockSpec((t, HEAD), lambda i: (tile_of(i), COL_BA // HEAD)),
                                            const((CONV_K, 3 * D_HALF)), const((1, N_HEADS)), const((1, N_HEADS))] + [row] * 5,
        out_specs=[row, row, row, row, const((8, 3 * D_HALF)), const((8, HEAD))],
        out_shape=[_sds((s, D_HALF), BF16)] * 4 + [_sds((8, 3 * D_HALF)), _sds((8, HEAD))],
        scratch_shapes=[pltpu.VMEM((8, 3 * D_HALF), F32), pltpu.VMEM((8 * CONV_K, 3 * D_HALF), F32),
                        pltpu.VMEM((8, HEAD), F32)],
        compiler_params=_params("arbitrary"),
    )(proj, proj, proj, proj, proj, proj, proj, conv_w, a_log, dt_bias, dqn, dkn, dvs, dbeta, dg)


def _conv_bwd_pre(proj, conv_w, a_log, dt_bias, dqn, dkn, dvs, dbeta, dg):
    s = proj.shape[0]
    t = CONV_T

    def body(q_ref, k_ref, v_ref, hq_ref, hk_ref, hv_ref, ba_ref, cw_ref, al_ref, dtb_ref,
             dqn_ref, dkn_ref, dvs_ref, dbeta_ref, dg_ref, dyq_ref, dyk_ref, dyv_ref, dba_ref, gcw_ref, gsm_ref):
        @pl.when(pl.program_id(0) == 0)
        def _():
            gcw_ref[...] = jnp.zeros_like(gcw_ref)
            gsm_ref[...] = jnp.zeros_like(gsm_ref)

        live = (pl.program_id(0) > 0).astype(F32)
        parts = ((q_ref, hq_ref, dqn_ref, dyq_ref), (k_ref, hk_ref, dkn_ref, dyk_ref), (v_ref, hv_ref, dvs_ref, dyv_ref))
        for p, (x_ref, h_ref, d_ref, dy_ref) in enumerate(parts):
            cols = slice(p * D_HALF, (p + 1) * D_HALF)
            taps = _conv_taps(x_ref[...], h_ref[...] * live)
            y = _conv_pre(taps, cw_ref[:, cols])
            sg = _sigmoid(y)
            sv = y * sg
            if p == 2:
                ds = d_ref[...]
            else:
                segs = []
                for h in HEADS:
                    seg = _head(sv, h)
                    rn = lax.rsqrt(_rowsum(seg * seg) + EPS)
                    nrm = seg * rn
                    dn = d_ref[:, HEAD_COLS[h]]
                    segs.append(rn * (dn - nrm * _rowsum(dn * nrm)))
                ds = jnp.concatenate(segs, axis=1)
            dy = ds * (sg * (1.0 + y * (1.0 - sg)))
            dy_ref[...] = dy
            for j in range(CONV_K):
                gcw_ref[j:j + 1, cols] += _colsum(dy * taps[j])

        ba = ba_ref[...]
        lane = lax.broadcasted_iota(I32, (t, HEAD), 1)
        lane1 = lax.broadcasted_iota(I32, (1, HEAD), 1)
        dba = jnp.zeros((t, HEAD), F32)
        gsm = jnp.zeros((1, HEAD), F32)
        for h in HEADS:
            beta = _sigmoid(ba[:, h:h + 1])
            dbeta = dbeta_ref[:, h * HEAD:h * HEAD + 1]
            xg = ba[:, N_HEADS + h:N_HEADS + h + 1] + dtb_ref[0:1, h:h + 1]
            nexp = -jnp.exp(al_ref[0:1, h:h + 1])
            dgv = dg_ref[:, h * HEAD:h * HEAD + 1]
            da = dgv * nexp * _sigmoid(xg)
            dba = dba + jnp.where(lane == h, dbeta * beta * (1.0 - beta), 0.0) + jnp.where(lane == N_HEADS + h, da, 0.0)
            gsm = (gsm + jnp.where(lane1 == h, _colsum(dgv * nexp * _softplus(xg)), 0.0)
                   + jnp.where(lane1 == N_HEADS + h, _colsum(da), 0.0))
        dba_ref[...] = jnp.zeros_like(dba_ref)
        dba_ref[:, :HEAD] = dba.astype(BF16)
        gsm_ref[0:1, :] += gsm

    row = pl.BlockSpec((t, D_HALF), lambda i: (i, 0))
    return _call(
        body, name="conv_bwd_pre", grid=(s // t,),
        in_specs=_conv_specs(t) + [pl.BlockSpec((t, HEAD), lambda i: (i, COL_BA // HEAD)),
                                   pl.BlockSpec((CONV_K, 3 * D_HALF), lambda i: (0, 0)),
                                   pl.BlockSpec((1, N_HEADS), lambda i: (0, 0)),
                                   pl.BlockSpec((1, N_HEADS), lambda i: (0, 0))] + [row] * 5,
        out_specs=[row, row, row, row,
                   pl.BlockSpec((8, 3 * D_HALF), lambda i: (0, 0)), pl.BlockSpec((8, HEAD), lambda i: (0, 0))],
        out_shape=[_sds((s, D_HALF))] * 3 + [_sds((s, D_HALF), BF16), _sds((8, 3 * D_HALF)), _sds((8, HEAD))],
        compiler_params=_params("arbitrary"),
    )(proj, proj, proj, proj, proj, proj, proj, conv_w, a_log, dt_bias, dqn, dkn, dvs, dbeta, dg)


def _conv_bwd_in(dyq, dyk, dyv, conv_w):
    s = dyq.shape[0]
    t = CONV_T
    last = s // 8 - 1

    def body(q_ref, k_ref, v_ref, nq_ref, nk_ref, nv_ref, cw_ref, oq_ref, ok_ref, ov_ref):
        more = (pl.program_id(0) < pl.num_programs(0) - 1).astype(F32)
        for p, (d_ref, n_ref, o_ref) in enumerate(((q_ref, nq_ref, oq_ref), (k_ref, nk_ref, ok_ref), (v_ref, nv_ref, ov_ref))):
            cw = cw_ref[:, p * D_HALF:(p + 1) * D_HALF]
            dy = d_ref[...]
            nxt = n_ref[...] * more
            acc = dy * cw[3:4]
            for sft in (1, 2, 3):
                acc = acc + _shift_up(dy, nxt, sft) * cw[3 - sft:4 - sft]
            o_ref[...] = acc.astype(BF16)

    row = pl.BlockSpec((t, D_HALF), lambda i: (i, 0))
    nxt = pl.BlockSpec((8, D_HALF), lambda i: (jnp.minimum((i + 1) * (t // 8), last), 0))
    return _call(
        body, name="conv_bwd_in", grid=(s // t,),
        in_specs=[row] * 3 + [nxt] * 3 + [pl.BlockSpec((CONV_K, 3 * D_HALF), lambda i: (0, 0))],
        out_specs=[row] * 3, out_shape=[_sds((s, D_HALF), BF16)] * 3,
        compiler_params=_params("arbitrary"),
    )(dyq, dyk, dyv, dyq, dyk, dyv, conv_w)


IN_T = 512


def _in_bwd(x, dh, norm_w, w_pad, pieces):
    s = x.shape[0]
    t = IN_T
    widths = [D_HALF] * 6 + [N_IN_PAD - COL_BA]

    def body(*refs):
        x_ref, dh_ref, nw_ref, w_ref = refs[:4]
        p_refs = refs[4:4 + len(pieces)]
        gx_ref, gnw_ref = refs[4 + len(pieces):]

        @pl.when(pl.program_id(0) == 0)
        def _():
            gnw_ref[...] = jnp.zeros_like(gnw_ref)

        dn = jnp.zeros((t, D_MODEL), F32)
        col = 0
        for p_ref, wd in zip(p_refs, widths):
            dn = dn + _bdot_nt(p_ref[...], w_ref[:, col:col + wd])
            col += wd
        xv = x_ref[...]
        r = lax.rsqrt(jnp.mean(xv * xv, axis=-1, keepdims=True) + EPS)
        xhat = xv * r
        gnw_ref[...] += _colsum(dn * xhat)
        dxh = dn * nw_ref[...]
        gx_ref[...] = dh_ref[...] + r * (dxh - xhat * jnp.mean(dxh * xhat, axis=-1, keepdims=True))

    wide = pl.BlockSpec((t, D_MODEL), lambda i: (i, 0))
    return _call(
        body, name="in_bwd", grid=(s // t,),
        in_specs=[wide, wide, pl.BlockSpec((1, D_MODEL), lambda i: (0, 0)),
                  pl.BlockSpec((D_MODEL, N_IN_PAD), lambda i: (0, 0))]
                 + [pl.BlockSpec((t, wd), lambda i: (i, 0)) for wd in widths],
        out_specs=[wide, pl.BlockSpec((1, D_MODEL), lambda i: (0, 0))],
        out_shape=[_sds((s, D_MODEL)), _sds((1, D_MODEL))],
        compiler_params=_params("arbitrary"),
    )(x, dh, norm_w, w_pad, *pieces)


def _adamw_shard(name, w, g_own, g_got, cidx, m, v):
    _, r, c = w.shape
    half = r // 2
    rows = 256 if half % 256 == 0 else half
    per_half = half // rows

    def body(c_ref, w_ref, go_ref, gg_ref, m_ref, v_ref, gout_ref, d_ref, nm_ref, nv_ref):
        mine = (pl.program_id(0) // per_half) == c_ref[0]
        gv = jnp.where(mine, go_ref[:, :c], gg_ref[:, :c])
        gout_ref[0] = gv
        mn = ADAM_B1 * m_ref[0] + (1.0 - ADAM_B1) * gv
        vn = ADAM_B2 * v_ref[0] + (1.0 - ADAM_B2) * (gv * gv)
        m_hat = mn / (1.0 - ADAM_B1 ** ADAM_STEP)
        v_hat = vn / (1.0 - ADAM_B2 ** ADAM_STEP)
        d_ref[0] = -ADAM_LR * (m_hat / (jnp.sqrt(v_hat) + ADAM_EPS) + ADAM_WD * w_ref[0])
        nm_ref[0] = mn
        nv_ref[0] = vn

    blk = pl.BlockSpec((1, rows, c), lambda i, c_ref: (0, i, 0))
    gblk = pl.BlockSpec((rows, g_own.shape[1]), lambda i, c_ref: (i % per_half, 0))
    return _call(
        body, name=name,
        grid_spec=pltpu.PrefetchScalarGridSpec(
            num_scalar_prefetch=1, grid=(2 * per_half,),
            in_specs=[blk, gblk, gblk, blk, blk], out_specs=[blk] * 4),
        out_shape=[_sds((1, r, c))] * 4,
        compiler_params=_params("arbitrary"),
    )(cidx, w, g_own, g_got, m, v)


def _exchange(name, inputs, out_shapes, phases):
    n_in = len(inputs)
    n_out = len(out_shapes)
    n_cp = sum(len(p) for p in phases)

    def body(*refs):
        ins, outs = refs[:n_in], refs[n_in:n_in + n_out]
        send, recv = refs[n_in + n_out:]
        pos = (lax.axis_index("x"), lax.axis_index("y"), lax.axis_index("c"))
        k = 0
        for phase in phases:
            cps = []
            for src, dst, target in phase:
                cps.append(pltpu.make_async_remote_copy(
                    src_ref=src(ins, outs, pos), dst_ref=dst(ins, outs, pos), send_sem=send.at[k], recv_sem=recv.at[k],
                    device_id=target(pos), device_id_type=pl.DeviceIdType.MESH))
                k += 1
            for cp in cps:
                cp.start()
            for cp in cps:
                cp.wait()

    anyspec = pl.BlockSpec(memory_space=pl.ANY)
    return _call(
        body, name=name,
        in_specs=[anyspec] * n_in, out_specs=[anyspec] * n_out, out_shape=list(out_shapes),
        scratch_shapes=[pltpu.SemaphoreType.DMA((n_cp,)), pltpu.SemaphoreType.DMA((n_cp,))],
    )(*inputs)


def _chip(pos):
    return 2 * pos[0] + pos[1]


def _other_chip(pos, mask):
    x, y, c = pos
    return (x ^ (mask >> 1), y ^ (mask & 1), c)


def _sibling(pos):
    return (pos[0], pos[1], 1 - pos[2])


def _gather_weights(wb, ob, cb):
    halves = (wb.shape[0] // 2, ob.shape[0] // 2)

    def half(a, pos):
        return pl.ds(pos[2] * halves[a], halves[a])

    first, second = [], []
    for mask in CHIP_MASKS:
        for a in (0, 1):
            first.append((lambda ins, outs, pos, a=a: ins[a].at[half(a, pos)],
                          lambda ins, outs, pos, a=a: outs[a].at[_chip(pos), half(a, pos)],
                          functools.partial(_other_chip, mask=mask)))
            second.append((lambda ins, outs, pos, a=a, mask=mask: outs[a].at[_chip(pos) ^ mask, half(a, pos)],
                           lambda ins, outs, pos, a=a, mask=mask: outs[a].at[_chip(pos) ^ mask, half(a, pos)],
                           _sibling))
        first.append((lambda ins, outs, pos: ins[2],
                      lambda ins, outs, pos: outs[2].at[_chip(pos)],
                      functools.partial(_other_chip, mask=mask)))
    return _exchange("gather_weights", [wb, ob, cb],
                     [_sds((4,) + wb.shape, wb.dtype), _sds((4,) + ob.shape, ob.dtype), _sds((4,) + cb.shape, cb.dtype)],
                     [first, second])


def _to_sibling_half(name, arrays):
    def src(ins, outs, pos, a):
        h = arrays[a].shape[-2] // 2
        sl = pl.ds((1 - pos[2]) * h, h)
        return ins[a].at[:, sl] if arrays[a].ndim == 3 else ins[a].at[sl]

    outs = [_sds(a.shape[:-2] + (a.shape[-2] // 2, a.shape[-1]), a.dtype) for a in arrays]
    phase = [(functools.partial(src, a=a), lambda ins, outs, pos, a=a: outs[a], _sibling) for a in range(len(arrays))]
    return _exchange(name, arrays, outs, [phase])


def _add_half(name, full, part, cidx):
    shape = part.shape
    lead = shape[0] if len(shape) == 3 else 1
    rows, cols = shape[-2], shape[-1]
    tr = rows // 2 if rows % 16 == 0 else rows
    nr = rows // tr
    f3 = full.reshape((lead,) + full.shape[-2:])
    p3 = part.reshape((lead, rows, cols))

    def body(c_ref, f_ref, p_ref, o_ref):
        o_ref[...] = (f_ref[...].astype(F32) + p_ref[...].astype(F32)).astype(o_ref.dtype)

    out = _call(
        body, name=name,
        grid_spec=pltpu.PrefetchScalarGridSpec(
            num_scalar_prefetch=1, grid=(lead, nr),
            in_specs=[pl.BlockSpec((1, tr, cols), lambda b, r, c_ref: (b, c_ref[0] * nr + r, 0)),
                      pl.BlockSpec((1, tr, cols), lambda b, r, c_ref: (b, r, 0))],
            out_specs=pl.BlockSpec((1, tr, cols), lambda b, r, c_ref: (b, r, 0))),
        out_shape=_sds((lead, rows, cols), part.dtype),
        compiler_params=_params("arbitrary", "arbitrary"),
    )(cidx, f3, p3)
    return out.reshape(shape)


def _to_other_chips(name, arrays, blocked):
    def src(ins, outs, pos, a, mask):
        return ins[a].at[_chip(pos) ^ mask] if blocked[a] else ins[a]

    outs = [_sds((3,) + (a.shape[1:] if b else a.shape), a.dtype) for a, b in zip(arrays, blocked)]
    phase = []
    for mi, mask in enumerate(CHIP_MASKS):
        for a in range(len(arrays)):
            phase.append((functools.partial(src, a=a, mask=mask), lambda ins, outs, pos, a=a, mi=mi: outs[a].at[mi],
                          functools.partial(_other_chip, mask=mask)))
    return _exchange(name, arrays, outs, [phase])


def _add_chips(name, own, got, jidx, blocked):
    rows, cols = got.shape[-2:]
    tr = rows // 2 if rows % 16 == 0 else rows
    nr = rows // tr
    o3 = own if blocked else own.reshape((1, rows, cols))

    def body(j_ref, o_ref, g_ref, out_ref):
        out_ref[...] = ((o_ref[0].astype(F32) + g_ref[0].astype(F32))
                        + (g_ref[1].astype(F32) + g_ref[2].astype(F32)))

    own_map = (lambda r, j_ref: (j_ref[0], r, 0)) if blocked else (lambda r, j_ref: (0, r, 0))
    return _call(
        body, name=name,
        grid_spec=pltpu.PrefetchScalarGridSpec(
            num_scalar_prefetch=1, grid=(nr,),
            in_specs=[pl.BlockSpec((1, tr, cols), own_map),
                      pl.BlockSpec((3, tr, cols), lambda r, j_ref: (0, r, 0))],
            out_specs=pl.BlockSpec((tr, cols), lambda r, j_ref: (r, 0))),
        out_shape=_sds((rows, cols)),
        compiler_params=_params("arbitrary"),
    )(jidx, o3, got)


def _to_sibling(name, arrays):
    phase = [(lambda ins, outs, pos, a=a: ins[a], lambda ins, outs, pos, a=a: outs[a], _sibling)
             for a in range(len(arrays))]
    return _exchange(name, arrays, [_sds(a.shape, a.dtype) for a in arrays], [phase])


def _local_step(x, target, w_pad, w_out, conv_w, norm_w, pool_w, pool_scale, a_log, dt_bias, dn_norm_w, final_norm_w):
    proj, n_t = _proj_fwd(x, norm_w, w_pad)
    y_pool = _pool_fwd(proj, pool_w, pool_scale)
    qn, kn, vs, beta, g = _conv_fwd(proj, conv_w, a_log, dt_bias)
    u, w, att, qd, kd, tm, cd = _intra_fwd(qn, kn, vs, beta, g)
    o, vn, st = _scan_fwd(u, w, att, qd, kd, cd)
    y_t, dh, dyp, do, ddz, loss, g_fnw, g_dnw = _out_fwd_bwd(x, y_pool, o, proj, target, w_out, dn_norm_w, final_norm_w)
    g_wout = _token_matmul("grad_w_out", y_t, [(dh, 0), (dh, 1)])
    dpu, dpz, g_pw, g_ps = _pool_bwd(proj, dyp, pool_w, pool_scale)
    du, dw, datt, dqd, dkd, dcd = _scan_bwd(do, vn, qd, kd, w, att, cd, st)
    dqn, dkn, dvs, dbeta, dg = _intra_bwd(qn, kn, vs, beta, g, tm, du, dw, datt, dqd, dkd, dcd)
    dcq, dck, dcv, dba, g_cw, g_sm = _conv_bwd(proj, conv_w, a_log, dt_bias, dqn, dkn, dvs, dbeta, dg)
    pieces = [dpu, dpz, dcq, dck, dcv, ddz, dba]
    gx, g_nw = _in_bwd(x, dh, norm_w, w_pad, pieces)
    g_win = _token_matmul("grad_w_in", n_t, [(p, 0) for p in pieces])
    small = dict(norm_w=g_nw, pool_w=g_pw, pool_scale=g_ps, conv_w=g_cw[:CONV_K], a_log=g_sm[0:1, 0:N_HEADS],
                 dt_bias=g_sm[0:1, N_HEADS:2 * N_HEADS], dn_norm_w=g_dnw, final_norm_w=g_fnw)
    return loss[0, 0], gx, g_win, g_wout, small


def _pack_small(t):
    lanes = lambda a: jnp.pad(a.reshape(1, -1), ((0, 0), (0, HEAD - a.size)))
    rows = [t["pool_w"].reshape(-1, HEAD), t["norm_w"].reshape(-1, HEAD), t["final_norm_w"].reshape(-1, HEAD),
            t["pool_scale"].reshape(-1, HEAD), t["conv_w"].reshape(-1, HEAD), t["dn_norm_w"].reshape(1, HEAD),
            lanes(t["a_log"]), lanes(t["dt_bias"]), lanes(t.get("loss", jnp.zeros((1,), F32)))]
    buf = jnp.concatenate(rows, axis=0)
    return jnp.pad(buf, ((0, SMALL_ROWS - buf.shape[0]), (0, 0)))


def _unpack_small(buf, conv_cols):
    out, r = {}, 0
    for name, nrow, shape in (("pool_w", 512, (1, N_HEADS, HEAD, HEAD)), ("norm_w", 8, (1, D_MODEL)),
                              ("final_norm_w", 8, (D_MODEL,)), ("pool_scale", 4, (1, D_HALF)),
                              ("conv_w", CONV_K * conv_cols // HEAD, (1, CONV_K, conv_cols)), ("dn_norm_w", 1, (1, HEAD))):
        out[name] = buf[r:r + nrow].reshape(shape)
        r += nrow
    out["a_log"] = buf[r:r + 1, :N_HEADS]
    out["dt_bias"] = buf[r + 1:r + 2, :N_HEADS]
    out["loss"] = buf[r + 2, 0]
    return out


def kernel(x, norm_w, w_in, pool_w, pool_scale, conv_w, a_log, dt_bias, dn_norm_w, w_out, final_norm_w, loss_target, m_norm_w, m_w_in, m_pool_w, m_pool_scale, m_conv_w, m_a_log, m_dt_bias, m_dn_norm_w, m_w_out, m_final_norm_w, v_norm_w, v_w_in, v_pool_w, v_pool_scale, v_conv_w, v_a_log, v_dt_bias, v_dn_norm_w, v_w_out, v_final_norm_w):
    cidx = lax.axis_index("c").astype(I32).reshape(1)
    jidx = (2 * lax.axis_index("x") + lax.axis_index("y")).astype(I32)

    wb = jnp.pad(w_in[0].astype(BF16), ((0, 0), (0, BLK_IN_PAD - BLK_IN)))
    ob = w_out[0].astype(BF16)
    gw, go, gc = _gather_weights(wb, ob, conv_w[0])
    mine = lambda j: jidx == j
    w_pad = jnp.concatenate([jnp.where(mine(j), wb[:, :BLK_IN], gw[j, :, :BLK_IN]) for j in range(4)]
                            + [jnp.zeros((D_MODEL, N_IN_PAD - N_IN), BF16)], axis=1)
    wo_full = jnp.where((jnp.arange(4) == jidx)[:, None, None], ob[None], go).reshape(D_MODEL, D_MODEL)
    cw_full = jnp.concatenate([jnp.where(mine(j), conv_w[0], gc[j]) for j in range(4)], axis=1)

    loss, gx, g_win, g_wout, small = _local_step(
        x[0], loss_target[0], w_pad, wo_full, cw_full, norm_w, pool_w[0], pool_scale, a_log, dt_bias, dn_norm_w,
        final_norm_w.reshape(1, D_MODEL))
    small["loss"] = loss

    blocks_in = jnp.stack([jnp.pad(g_win[:, j * BLK_IN:(j + 1) * BLK_IN].astype(BF16), ((0, 0), (0, BLK_IN_PAD - BLK_IN)))
                           for j in range(4)])
    blocks_out = g_wout.astype(BF16).reshape(4, BLK_OUT, D_MODEL)
    full = [blocks_in, blocks_out, _pack_small(small)]
    from_sib = _to_sibling_half("reduce_sibling", full)
    chip_sum = [_add_half("add_sibling_%d" % i, f, p, cidx) for i, (f, p) in enumerate(zip(full, from_sib))]
    blocked = [True, True, False]
    from_chips = _to_other_chips("reduce_chips", chip_sum, blocked)
    halves = [_add_chips("add_chips_%d" % i, o, g, jidx.reshape(1), b)
              for i, (o, g, b) in enumerate(zip(chip_sum, from_chips, blocked))]
    other_halves = _to_sibling("swap_halves", halves)

    weights = dict(norm_w=norm_w, w_in=w_in, pool_w=pool_w, pool_scale=pool_scale, conv_w=conv_w, a_log=a_log,
                   dt_bias=dt_bias, dn_norm_w=dn_norm_w, w_out=w_out, final_norm_w=final_norm_w)
    ms = dict(norm_w=m_norm_w, w_in=m_w_in, pool_w=m_pool_w, pool_scale=m_pool_scale, conv_w=m_conv_w, a_log=m_a_log,
              dt_bias=m_dt_bias, dn_norm_w=m_dn_norm_w, w_out=m_w_out, final_norm_w=m_final_norm_w)
    vs = dict(norm_w=v_norm_w, w_in=v_w_in, pool_w=v_pool_w, pool_scale=v_pool_scale, conv_w=v_conv_w, a_log=v_a_log,
              dt_bias=v_dt_bias, dn_norm_w=v_dn_norm_w, w_out=v_w_out, final_norm_w=v_final_norm_w)
    names = ["norm_w", "w_in", "pool_w", "pool_scale", "conv_w", "a_log", "dt_bias", "dn_norm_w", "w_out", "final_norm_w"]
    small_names = [n for n in names if n not in ("w_in", "w_out")]

    def pack(t):
        conv = lax.dynamic_update_slice_in_dim(jnp.zeros((CONV_K, 3 * D_HALF), F32), t["conv_w"][0], jidx * BLK_CONV, axis=1)
        return _pack_small({**{n: t[n] for n in small_names if n != "conv_w"}, "conv_w": conv})[None]

    results = [{}, {}, {}, {}]
    for i, name in enumerate(("w_in", "w_out")):
        outs = _adamw_shard("adamw_" + name, weights[name], halves[i], other_halves[i], cidx, ms[name], vs[name])
        for res, o in zip(results, outs):
            res[name] = o
    outs = _adamw_shard("adamw_small", pack(weights), halves[2], other_halves[2], cidx, pack(ms), pack(vs))
    for res, o in zip(results, outs):
        got = _unpack_small(o[0], 3 * D_HALF)
        got["conv_w"] = lax.dynamic_slice_in_dim(got["conv_w"], jidx * BLK_CONV, BLK_CONV, axis=2)
        res.update(got)
    grads, delta, new_m, new_v = results

    return (grads["loss"], gx[None], *[grads[n] for n in names], *[delta[n] for n in names],
            *[new_m[n] for n in names], *[new_v[n] for n in names])
```

```python
import functools

import jax
import jax.numpy as jnp
from jax import lax
from jax.experimental import pallas as pl
from jax.experimental.pallas import tpu as pltpu

F32 = jnp.float32
BF16 = jnp.bfloat16
I32 = jnp.int32

D_MODEL = 1024
D_HALF = 512
N_HEADS = 4
HEAD = 128
CHUNK = 64
PAIR = 2 * CHUNK
WINDOWS = (2, 4, 8, 16)
CONV_K = 4
EPS = 1e-6
N_IN = 3080
N_IN_PAD = 3200
BLK_IN = 770
BLK_IN_PAD = 896
BLK_OUT = 256
BLK_CONV = 384
COL_BA = 3072
QK_SCALE = HEAD ** -0.5
SMALL_ROWS = 592
VMEM_LIMIT = 56 * 1024 * 1024

ADAM_LR = 0.001
ADAM_B1 = 0.9
ADAM_B2 = 0.999
ADAM_EPS = 1e-08
ADAM_WD = 0.01
ADAM_STEP = 10

CHIP_MASKS = (2, 1, 3)
HEADS = range(N_HEADS)
HEAD_COLS = [slice(h * HEAD, (h + 1) * HEAD) for h in HEADS]


def _call(body, **kw):
    return pl.pallas_call(body, **kw)


def _params(*sem):
    return pltpu.CompilerParams(dimension_semantics=sem, vmem_limit_bytes=VMEM_LIMIT)


def _sds(shape, dtype=F32):
    return jax.ShapeDtypeStruct(shape, dtype)


def _bdot(a, b):
    return jnp.dot(a.astype(BF16), b.astype(BF16), preferred_element_type=F32)


def _bdot_nt(a, b):
    return lax.dot_general(a.astype(BF16), b.astype(BF16), (((1,), (1,)), ((), ())), preferred_element_type=F32)


def _bdot_tn(a, b):
    return lax.dot_general(a.astype(BF16), b.astype(BF16), (((0,), (0,)), ((), ())), preferred_element_type=F32)


def _split(a):
    hi = a.astype(BF16)
    lo = (a - hi.astype(F32)).astype(BF16)
    return hi, lo


def _mask_dot(m, b, dims=(((1,), (0,)), ((), ()))):
    bh, bl = _split(b)
    dg = functools.partial(lax.dot_general, dimension_numbers=dims, preferred_element_type=F32)
    return dg(m, bh) + dg(m, bl)


def _sigmoid(x):
    return 0.5 * jnp.tanh(0.5 * x) + 0.5


def _softplus(x):
    return jnp.maximum(x, 0.0) + jnp.log(1.0 + jnp.exp(-jnp.abs(x)))


def _rowsum(x):
    return jnp.sum(x, axis=-1, keepdims=True)


def _colsum(x):
    return jnp.sum(x, axis=0, keepdims=True)


def _shift_down(xv, prev8, k):
    r = pltpu.roll(xv, k, 0)
    q = pltpu.roll(prev8, k, 0)
    row = lax.broadcasted_iota(I32, prev8.shape, 0)
    top = jnp.where(row < k, q, r[0:8])
    return jnp.concatenate([top, r[8:]], axis=0)


def _shift_up(xv, next8, k):
    t = xv.shape[0]
    r = pltpu.roll(xv, t - k, 0)
    q = pltpu.roll(next8, 8 - k, 0)
    row = lax.broadcasted_iota(I32, next8.shape, 0)
    bot = jnp.where(row >= 8 - k, q, r[t - 8:])
    return jnp.concatenate([r[:t - 8], bot], axis=0)


def _band(rows, cols, off, w, anti=False):
    r = lax.broadcasted_iota(I32, (rows, cols), 0)
    c = lax.broadcasted_iota(I32, (rows, cols), 1)
    d = (c - r + off) if anti else (r - c + off)
    return ((d >= 0) & (d < w)).astype(BF16)


def _head(ref_or_val, h):
    return ref_or_val[:, h * HEAD:(h + 1) * HEAD]


INTRA_PAIRS = 2
UNITS = [(pp, h) for pp in range(INTRA_PAIRS) for h in HEADS]


def _heads(ref, rows=PAIR):
    return [ref[pp * rows:(pp + 1) * rows, HEAD_COLS[h]] for pp, h in UNITS]


def _put_heads(ref, vals, rows=PAIR):
    for (pp, h), v in zip(UNITS, vals):
        ref[pp * rows:(pp + 1) * rows, HEAD_COLS[h]] = v.astype(ref.dtype)


def _each(fn, *lists):
    return [fn(*args) for args in zip(*lists)]


def _proj_fwd(x, norm_w, w_pad):
    s = x.shape[0]
    tm = 512

    def body(x_ref, nw_ref, w_ref, proj_ref, nt_ref):
        xv = x_ref[...]
        r = lax.rsqrt(jnp.mean(xv * xv, axis=-1, keepdims=True) + EPS)
        nv = xv * r * nw_ref[...]
        nt_ref[...] = nv.T.astype(BF16)
        proj_ref[...] = jnp.dot(nv.astype(BF16), w_ref[...], preferred_element_type=F32)

    return _call(
        body, name="proj_fwd", grid=(s // tm,),
        in_specs=[pl.BlockSpec((tm, D_MODEL), lambda i: (i, 0)),
                  pl.BlockSpec((1, D_MODEL), lambda i: (0, 0)),
                  pl.BlockSpec((D_MODEL, N_IN_PAD), lambda i: (0, 0))],
        out_specs=[pl.BlockSpec((tm, N_IN_PAD), lambda i: (i, 0)),
                   pl.BlockSpec((D_MODEL, tm), lambda i: (0, i))],
        out_shape=[_sds((s, N_IN_PAD)), _sds((D_MODEL, s), BF16)],
        compiler_params=_params("arbitrary"),
    )(x, norm_w, w_pad)


def _pool_mix(ug, hg, zg, pw_g, row0, w):
    t = ug.shape[0]
    win = _mask_dot(_band(t, t, 0, w), ug) + _mask_dot(_band(t, HEAD, HEAD, w), hg)
    cnt = jnp.minimum(row0 + lax.broadcasted_iota(I32, (t, 1), 0) + 1, w).astype(F32)
    mix = win / cnt - ug
    mixed = _bdot(mix, pw_g)
    sg = _sigmoid(zg)
    return mix, mixed, sg, cnt


POOL_T = 256


def _pool_fwd(proj, pool_w, pool_scale):
    s = proj.shape[0]
    t = POOL_T
    hb = t // HEAD

    def body(u_ref, z_ref, halo_ref, pw_ref, ps_ref, y_ref):
        i = pl.program_id(0)
        live = (i > 0).astype(F32)
        for g, w in enumerate(WINDOWS):
            sl = HEAD_COLS[g]
            zg = z_ref[:, sl]
            _, mixed, sg, _ = _pool_mix(u_ref[:, sl], halo_ref[:, sl] * live, zg, pw_ref[g], i * t, w)
            y_ref[:, sl] = mixed * ps_ref[:, sl] * (zg * sg)

    return _call(
        body, name="pool_fwd", grid=(s // t,),
        in_specs=[pl.BlockSpec((t, D_HALF), lambda i: (i, 0)),
                  pl.BlockSpec((t, D_HALF), lambda i: (i, 1)),
                  pl.BlockSpec((HEAD, D_HALF), lambda i: (jnp.maximum(i * hb - 1, 0), 0)),
                  pl.BlockSpec((N_HEADS, HEAD, HEAD), lambda i: (0, 0, 0)),
                  pl.BlockSpec((1, D_HALF), lambda i: (0, 0))],
        out_specs=pl.BlockSpec((t, D_HALF), lambda i: (i, 0)),
        out_shape=_sds((s, D_HALF)),
        compiler_params=_params("arbitrary"),
    )(proj, proj, proj, pool_w, pool_scale)


def _conv_taps(xv, prev8):
    return [_shift_down(xv, prev8, CONV_K - 1 - j) for j in range(CONV_K - 1)] + [xv]


def _conv_pre(taps, cw):
    y = taps[CONV_K - 1] * cw[CONV_K - 1:CONV_K]
    for j in range(CONV_K - 2, -1, -1):
        y = y + taps[j] * cw[j:j + 1]
    return y


CONV_T = 256
CONV_SUB = 256


def _conv_specs(t, tile_of=lambda i: i):
    tiles = [pl.BlockSpec((t, D_HALF), functools.partial(lambda i, p: (tile_of(i), 2 + p), p=p)) for p in range(3)]
    halos = [pl.BlockSpec((8, D_HALF),
                          functools.partial(lambda i, p: (jnp.maximum(tile_of(i) * (t // 8) - 1, 0), 2 + p), p=p))
             for p in range(3)]
    return tiles + halos


def _conv_fwd(proj, conv_w, a_log, dt_bias):
    s = proj.shape[0]
    t = CONV_T

    def body(q_ref, k_ref, v_ref, hq_ref, hk_ref, hv_ref, ba_ref, cw_ref, al_ref, dtb_ref,
             qn_ref, kn_ref, vs_ref, beta_ref, g_ref):
        live = (pl.program_id(0) > 0).astype(F32)
        parts = ((q_ref, hq_ref, qn_ref), (k_ref, hk_ref, kn_ref), (v_ref, hv_ref, vs_ref))

        def sub_tile(r0, first):
            rows = pl.ds(r0, CONV_SUB)
            for p, (x_ref, h_ref, o_ref) in enumerate(parts):
                for h in HEADS:
                    cs = HEAD_COLS[h]
                    prev8 = h_ref[:, cs] * live if first else x_ref[pl.ds(r0 - 8, 8), cs]
                    y = _conv_pre(_conv_taps(x_ref[rows, cs], prev8), cw_ref[:, p * D_HALF + h * HEAD:p * D_HALF + (h + 1) * HEAD])
                    sv = y * _sigmoid(y)
                    o_ref[rows, cs] = sv if p == 2 else sv * lax.rsqrt(_rowsum(sv * sv) + EPS)
            ba = ba_ref[rows, :]
            for h in HEADS:
                beta = _sigmoid(ba[:, h:h + 1])
                gl = -jnp.exp(al_ref[0:1, h:h + 1]) * _softplus(ba[:, N_HEADS + h:N_HEADS + h + 1] + dtb_ref[0:1, h:h + 1])
                beta_ref[rows, HEAD_COLS[h]] = jnp.broadcast_to(beta, (CONV_SUB, HEAD))
                g_ref[rows, HEAD_COLS[h]] = jnp.broadcast_to(gl, (CONV_SUB, HEAD))

        sub_tile(0, True)

        def step(k, carry):
            sub_tile(pl.multiple_of(k * CONV_SUB, CONV_SUB), False)
            return carry

        lax.fori_loop(1, t // CONV_SUB, step, 0)

    row = pl.BlockSpec((t, D_HALF), lambda i: (i, 0))
    return _call(
        body, name="conv_fwd", grid=(s // t,),
        in_specs=_conv_specs(t) + [pl.BlockSpec((t, HEAD), lambda i: (i, COL_BA // HEAD)),
                                   pl.BlockSpec((CONV_K, 3 * D_HALF), lambda i: (0, 0)),
                                   pl.BlockSpec((1, N_HEADS), lambda i: (0, 0)),
                                   pl.BlockSpec((1, N_HEADS), lambda i: (0, 0))],
        out_specs=[row] * 5,
        out_shape=[_sds((s, D_HALF))] * 5,
        compiler_params=_params("arbitrary"),
    )(proj, proj, proj, proj, proj, proj, proj, conv_w, a_log, dt_bias)


def _pair_masks():
    r = lax.broadcasted_iota(I32, (PAIR, PAIR), 0)
    c = lax.broadcasted_iota(I32, (PAIR, PAIR), 1)
    same = jnp.right_shift(r, 6) == jnp.right_shift(c, 6)
    return same, same & (r >= c), same & (r > c), r == c


def _pair_common(qn, kn, vs, beta, g):
    same, incl, strict, eye = _pair_masks()
    incl_b = incl.astype(BF16)
    first = lax.broadcasted_iota(I32, (PAIR, HEAD), 0) < CHUNK
    gc = _each(lambda gv: _mask_dot(incl_b, gv), g)
    gc_row = _each(lambda v: _colsum(jnp.where(eye, v, 0.0)), gc)
    decay = _each(lambda v, r: jnp.where(incl, jnp.exp(jnp.where(incl, v - r, 0.0)), 0.0), gc, gc_row)
    gl = _each(lambda v: jnp.where(first, v[CHUNK - 1:CHUNK], v[PAIR - 1:PAIR]), gc)
    egc = _each(jnp.exp, gc)
    q = _each(lambda v: v * QK_SCALE, qn)
    kb = _each(lambda k, b: k * b, kn, beta)
    return dict(same=same, incl=incl, strict=strict, eye=eye, gc=gc, decay=decay, gl=gl, egc=egc,
                ekd=_each(lambda a, b: jnp.exp(a - b), gl, gc), cd=_each(jnp.exp, gl), q=q, kb=kb,
                vb=_each(lambda v, b: v * b, vs, beta), kbg=_each(lambda k, e: k * e, kb, egc),
                kk=_each(_bdot_nt, kb, kn), qk=_each(_bdot_nt, q, kn))


def _tri_inv(a, eye_f):
    p = _each(lambda v: eye_f - v, a)
    x = _each(_bdot, a, a)
    for it in range(5):
        p = _each(lambda pv, xv: pv + _bdot(pv, xv), p, x)
        if it < 4:
            x = _each(_bdot, x, x)
    return p


def _pair_spec():
    return pl.BlockSpec((INTRA_PAIRS * PAIR, D_HALF), lambda i: (i, 0))


def _chunk_scalar_spec(pairs=1, index=lambda i: (i, 0)):
    return pl.BlockSpec((16 * pairs, D_HALF), index)


SCAN_PAIRS = 2
SCAN_ROWS = SCAN_PAIRS * PAIR


def _intra_fwd(qn, kn, vs, beta, g):
    s = qn.shape[0]

    def body(qn_ref, kn_ref, vs_ref, beta_ref, g_ref, u_ref, w_ref, att_ref, qd_ref, kd_ref, t_ref, cd_ref):
        kn = _heads(kn_ref)
        cm = _pair_common(_heads(qn_ref), kn, _heads(vs_ref), _heads(beta_ref), _heads(g_ref))
        a = _each(lambda kk, d: jnp.where(cm["strict"], kk * d, 0.0), cm["kk"], cm["decay"])
        tm = _tri_inv(a, cm["eye"].astype(F32))
        _put_heads(t_ref, tm)
        _put_heads(u_ref, _each(_bdot, tm, cm["vb"]))
        _put_heads(w_ref, _each(_bdot, tm, cm["kbg"]))
        _put_heads(att_ref, _each(lambda a, b: a * b, cm["qk"], cm["decay"]))
        _put_heads(qd_ref, _each(lambda a, b: a * b, cm["q"], cm["egc"]))
        _put_heads(kd_ref, _each(lambda a, b: a * b, kn, cm["ekd"]))
        for ci in range(2):
            for (pp, h), v in zip(UNITS, cm["cd"]):
                cd_ref[pp * 16 + ci * 8:pp * 16 + (ci + 1) * 8, HEAD_COLS[h]] = v[ci * CHUNK:ci * CHUNK + 8]

    return _call(
        body, name="intra_fwd", grid=(s // (INTRA_PAIRS * PAIR),),
        in_specs=[_pair_spec()] * 5, out_specs=[_pair_spec()] * 6 + [_chunk_scalar_spec(INTRA_PAIRS)],
        out_shape=[_sds((s, D_HALF))] + [_sds((s, D_HALF), BF16)] * 5 + [_sds((s // 8, D_HALF))],
        compiler_params=_params("arbitrary"),
    )(qn, kn, vs, beta, g)


def _scan_fwd(u, w, att, qd, kd, cd):
    s = u.shape[0]
    n_chunks = s // CHUNK

    def body(u_ref, w_ref, att_ref, qd_ref, kd_ref, cd_ref, o_ref, vn_ref, st_ref, state):
        @pl.when(pl.program_id(0) == 0)
        def _():
            state[...] = jnp.zeros_like(state)
        cols = list(enumerate(HEAD_COLS))
        sm = [state[h] for h in HEADS]
        for ci in range(2 * SCAN_PAIRS):
            rs = slice(ci * CHUNK, (ci + 1) * CHUNK)
            for h in HEADS:
                st_ref[ci, h] = sm[h]
            both = [_bdot(jnp.concatenate([w_ref[rs, sl], qd_ref[rs, sl]], axis=0), sm[h]) for h, sl in cols]
            vn = [u_ref[rs, sl] - both[h][:CHUNK] for h, sl in cols]
            for h, sl in cols:
                vn_ref[rs, sl] = vn[h].astype(BF16)
                o_ref[rs, sl] = both[h][CHUNK:]
            sm = [sm[h] * cd_ref[ci * 8:ci * 8 + 1, sl] + _bdot_tn(kd_ref[rs, sl], vn[h]) for h, sl in cols]
        for h in HEADS:
            state[h] = sm[h]
        for pp in range(SCAN_PAIRS):
            rp = slice(pp * PAIR, (pp + 1) * PAIR)
            intra = [_bdot(att_ref[rp, sl], vn_ref[rp, sl]) for sl in HEAD_COLS]
            for h, sl in cols:
                o_ref[rp, sl] += intra[h]

    rows = pl.BlockSpec((SCAN_ROWS, D_HALF), lambda i: (i, 0))
    return _call(
        body, name="scan_fwd", grid=(s // SCAN_ROWS,),
        in_specs=[rows] * 5 + [_chunk_scalar_spec(SCAN_PAIRS)],
        out_specs=[rows, rows, pl.BlockSpec((2 * SCAN_PAIRS, N_HEADS, HEAD, HEAD), lambda i: (i, 0, 0, 0))],
        out_shape=[_sds((s, D_HALF)), _sds((s, D_HALF), BF16), _sds((n_chunks, N_HEADS, HEAD, HEAD))],
        scratch_shapes=[pltpu.VMEM((N_HEADS, HEAD, HEAD), F32)],
        compiler_params=_params("arbitrary"),
    )(u, w, att, qd, kd, cd)


OUT_T = 512


def _out_fwd_bwd(x, y_pool, o, proj, target, w_out, dn_norm_w, final_norm_w):
    s = x.shape[0]
    t = OUT_T

    def body(x_ref, yp_ref, o_ref, z_ref, tg_ref, wo_ref, dnw_ref, fnw_ref,
             yt_ref, dh_ref, dyp_ref, do_ref, dz_ref, loss_ref, gfn_ref, gdn_ref, y_ref):
        @pl.when(pl.program_id(0) == 0)
        def _():
            loss_ref[...] = jnp.zeros_like(loss_ref)
            gfn_ref[...] = jnp.zeros_like(gfn_ref)
            gdn_ref[...] = jnp.zeros_like(gdn_ref)

        ypv = yp_ref[...]
        y_ref[:, :D_HALF] = ypv.astype(BF16)
        yt_ref[:D_HALF, :] = ypv.T.astype(BF16)
        dnw = dnw_ref[...]
        keep = []
        for h in HEADS:
            ov = o_ref[:, HEAD_COLS[h]]
            zv = z_ref[:, HEAD_COLS[h]]
            ro = lax.rsqrt(jnp.mean(ov * ov, axis=-1, keepdims=True) + EPS)
            ohat = ov * ro
            sg = _sigmoid(zv)
            keep.append((ro, ohat, zv, sg))
            ydn = ohat * dnw * (zv * sg)
            y_ref[:, D_HALF + h * HEAD:D_HALF + (h + 1) * HEAD] = ydn.astype(BF16)
            yt_ref[D_HALF + h * HEAD:D_HALF + (h + 1) * HEAD, :] = ydn.T.astype(BF16)

        hv = x_ref[...] + jnp.dot(y_ref[...], wo_ref[...], preferred_element_type=F32)
        r2 = lax.rsqrt(jnp.mean(hv * hv, axis=-1, keepdims=True) + EPS)
        hhat = hv * r2
        fnw = fnw_ref[...]
        err = hhat * fnw - tg_ref[...]
        loss_ref[...] += 0.5 * jnp.sum(_rowsum(err * err) * (1.0 / D_MODEL), axis=0, keepdims=True)
        dout = err * (1.0 / D_MODEL)
        gfn_ref[...] += _colsum(dout * hhat)
        dhh = dout * fnw
        dh = r2 * (dhh - hhat * jnp.mean(dhh * hhat, axis=-1, keepdims=True))
        dh_ref[...] = dh
        dy = _bdot_nt(dh, wo_ref[...])
        dyp_ref[...] = dy[:, :D_HALF]
        gdn = jnp.zeros((1, HEAD), F32)
        for h in HEADS:
            ro, ohat, zv, sg = keep[h]
            dyd = dy[:, D_HALF + h * HEAD:D_HALF + (h + 1) * HEAD]
            sz = zv * sg
            dz_ref[:, HEAD_COLS[h]] = (dyd * ohat * dnw * (sg * (1.0 + zv * (1.0 - sg)))).astype(BF16)
            gdn = gdn + _colsum(dyd * ohat * sz)
            doh = dyd * dnw * sz
            do_ref[:, HEAD_COLS[h]] = ro * (doh - ohat * jnp.mean(doh * ohat, axis=-1, keepdims=True))
        gdn_ref[...] += gdn

    wide = pl.BlockSpec((t, D_MODEL), lambda i: (i, 0))
    half = pl.BlockSpec((t, D_HALF), lambda i: (i, 0))
    const = lambda shape: pl.BlockSpec(shape, lambda i: (0,) * len(shape))
    return _call(
        body, name="out_fwd_bwd", grid=(s // t,),
        in_specs=[wide, half, half, pl.BlockSpec((t, D_HALF), lambda i: (i, 5)), wide,
                  const((D_MODEL, D_MODEL)), const((1, HEAD)), const((1, D_MODEL))],
        out_specs=[pl.BlockSpec((D_MODEL, t), lambda i: (0, i)), wide, half, half, half,
                   const((1, HEAD)), const((1, D_MODEL)), const((1, HEAD))],
        out_shape=[_sds((D_MODEL, s), BF16), _sds((s, D_MODEL)), _sds((s, D_HALF)), _sds((s, D_HALF)), _sds((s, D_HALF), BF16),
                   _sds((1, HEAD)), _sds((1, D_MODEL)), _sds((1, HEAD))],
        scratch_shapes=[pltpu.VMEM((t, D_MODEL), BF16)],
        compiler_params=_params("arbitrary"),
    )(x, y_pool, o, proj, target, w_out, dn_norm_w, final_norm_w)


def _token_matmul(name, at, pieces):
    m, s = at.shape
    n = len(pieces)
    tn, tk = D_HALF, 512

    def body(a_ref, *refs):
        p_refs, o_ref = refs[:n], refs[n]

        @pl.when(pl.program_id(0) == 0)
        def _():
            o_ref[...] = jnp.zeros_like(o_ref)

        av = a_ref[...]
        for p in range(n):
            o_ref[:, p * tn:(p + 1) * tn] += _bdot(av, p_refs[p][...])

    return _call(
        body, name=name, grid=(s // tk,),
        in_specs=[pl.BlockSpec((m, tk), lambda k: (0, k))]
                 + [pl.BlockSpec((tk, tn), functools.partial(lambda k, cb: (k, cb), cb=cb)) for _, cb in pieces],
        out_specs=pl.BlockSpec((m, n * tn), lambda k: (0, 0)),
        out_shape=_sds((m, n * tn)),
        compiler_params=_params("arbitrary"),
    )(at, *[p[0] for p in pieces])


def _pool_bwd(proj, dyp, pool_w, pool_scale):
    s = proj.shape[0]
    t = POOL_T
    hb = t // HEAD
    last = s // HEAD - 1

    def body(u_ref, z_ref, halo_ref, dy_ref, zn_ref, dyn_ref, pw_ref, ps_ref, du_ref, dz_ref, gpw_ref, gps_ref):
        i = pl.program_id(0)

        @pl.when(i == 0)
        def _():
            gpw_ref[...] = jnp.zeros_like(gpw_ref)
            gps_ref[...] = jnp.zeros_like(gps_ref)

        live = (i > 0).astype(F32)
        more = (i < pl.num_programs(0) - 1).astype(F32)
        for g, w in enumerate(WINDOWS):
            sl = HEAD_COLS[g]
            zg = z_ref[:, sl]
            ps = ps_ref[:, sl]
            pw = pw_ref[g]
            mix, mixed, sg, cnt = _pool_mix(u_ref[:, sl], halo_ref[:, sl] * live, zg, pw, i * t, w)
            dyg = dy_ref[:, sl]
            sz = zg * sg
            dz_ref[:, sl] = (dyg * mixed * ps * (sg * (1.0 + zg * (1.0 - sg)))).astype(BF16)
            gps_ref[:, sl] += _colsum(dyg * mixed * sz)
            dmixed = dyg * ps * sz
            gpw_ref[g] += _bdot_tn(mix, dmixed)
            dmix = _bdot_nt(dmixed, pw)
            zn = zn_ref[:, sl]
            dmix_n = _bdot_nt(dyn_ref[:, sl] * more * ps * (zn * _sigmoid(zn)), pw)
            du_ref[:, sl] = (_mask_dot(_band(t, t, 0, w, anti=True), dmix / cnt)
                             + _mask_dot(_band(t, HEAD, t, w, anti=True), dmix_n * (1.0 / w)) - dmix).astype(BF16)

    tile = lambda col: pl.BlockSpec((t, D_HALF), lambda i: (i, col))
    below = lambda col: pl.BlockSpec((HEAD, D_HALF), lambda i: (jnp.minimum((i + 1) * hb, last), col))
    return _call(
        body, name="pool_bwd", grid=(s // t,),
        in_specs=[tile(0), tile(1), pl.BlockSpec((HEAD, D_HALF), lambda i: (jnp.maximum(i * hb - 1, 0), 0)),
                  tile(0), below(1), below(0),
                  pl.BlockSpec((N_HEADS, HEAD, HEAD), lambda i: (0, 0, 0)), pl.BlockSpec((1, D_HALF), lambda i: (0, 0))],
        out_specs=[tile(0), tile(0), pl.BlockSpec((N_HEADS, HEAD, HEAD), lambda i: (0, 0, 0)),
                   pl.BlockSpec((1, D_HALF), lambda i: (0, 0))],
        out_shape=[_sds((s, D_HALF), BF16), _sds((s, D_HALF), BF16), _sds((N_HEADS, HEAD, HEAD)), _sds((1, D_HALF))],
        compiler_params=_params("arbitrary"),
    )(proj, proj, proj, dyp, proj, dyp, pool_w, pool_scale)


def _scan_bwd(do, vn, qd, kd, w, att, cd, st):
    s = do.shape[0]
    n_steps = s // SCAN_ROWS

    def body(do_ref, vn_ref, qd_ref, kd_ref, w_ref, att_ref, cd_ref, st_ref,
             du_ref, dw_ref, datt_ref, dqd_ref, dkd_ref, dcd_ref, dstate):
        @pl.when(pl.program_id(0) == 0)
        def _():
            dstate[...] = jnp.zeros_like(dstate)
        _, incl, _, _ = _pair_masks()
        cols = list(enumerate(HEAD_COLS))
        dv_intra = []
        for pp in range(SCAN_PAIRS):
            rp = slice(pp * PAIR, (pp + 1) * PAIR)
            dv_intra.append([_bdot_tn(att_ref[rp, sl], do_ref[rp, sl]) for _, sl in cols])
            for _, sl in cols:
                datt_ref[rp, sl] = jnp.where(incl, _bdot_nt(do_ref[rp, sl], vn_ref[rp, sl]), 0.0)
        ds = [dstate[h] for h in HEADS]
        for ci in range(2 * SCAN_PAIRS - 1, -1, -1):
            rs = slice(ci * CHUNK, (ci + 1) * CHUNK)
            in_pair = slice((ci % 2) * CHUNK, (ci % 2 + 1) * CHUNK)
            sm = [st_ref[ci, h] for h in HEADS]
            dvn = [dv_intra[ci // 2][h][in_pair] + _bdot(kd_ref[rs, sl], ds[h]) for h, sl in cols]
            for h, sl in cols:
                du_ref[rs, sl] = dvn[h].astype(BF16)
            dqd = [_bdot_nt(do_ref[rs, sl], sm[h]) for h, sl in cols]
            dw = [-_bdot_nt(dvn[h], sm[h]) for h, _ in cols]
            dkd = [_bdot_nt(vn_ref[rs, sl], ds[h]) for h, sl in cols]
            dcd = [jnp.broadcast_to(_rowsum(_colsum(ds[h] * sm[h])), (8, HEAD)) for h in HEADS]
            for h, sl in cols:
                dqd_ref[rs, sl] = dqd[h]
                dw_ref[rs, sl] = dw[h].astype(BF16)
                dkd_ref[rs, sl] = dkd[h]
                dcd_ref[ci * 8:(ci + 1) * 8, sl] = dcd[h]
            ds = [ds[h] * cd_ref[ci * 8:ci * 8 + 1, sl] + _bdot_tn(qd_ref[rs, sl], do_ref[rs, sl])
                  - _bdot_tn(w_ref[rs, sl], dvn[h]) for h, sl in cols]
        for h in HEADS:
            dstate[h] = ds[h]

    rev = pl.BlockSpec((SCAN_ROWS, D_HALF), lambda i: (n_steps - 1 - i, 0))
    rev_scalar = _chunk_scalar_spec(SCAN_PAIRS, lambda i: (n_steps - 1 - i, 0))
    return _call(
        body, name="scan_bwd", grid=(n_steps,),
        in_specs=[rev] * 6 + [rev_scalar,
                              pl.BlockSpec((2 * SCAN_PAIRS, N_HEADS, HEAD, HEAD), lambda i: (n_steps - 1 - i, 0, 0, 0))],
        out_specs=[rev] * 5 + [rev_scalar],
        out_shape=[_sds((s, D_HALF), BF16)] * 2 + [_sds((s, D_HALF))] * 3 + [_sds((s // 8, D_HALF))],
        scratch_shapes=[pltpu.VMEM((N_HEADS, HEAD, HEAD), F32)],
        compiler_params=_params("arbitrary"),
    )(do, vn, qd, kd, w, att, cd, st)


def _intra_bwd(qn, kn, vs, beta, g, tm, du, dw, datt, dqd, dkd, dcd):
    s = qn.shape[0]

    def body(qn_ref, kn_ref, vs_ref, beta_ref, g_ref, t_ref, du_ref, dw_ref, datt_ref, dqd_ref, dkd_ref, dcd_ref,
             dqn_ref, dkn_ref, dvs_ref, dbeta_ref, dg_ref):
        ones = jnp.ones((PAIR, HEAD), BF16)
        tn = (((0,), (0,)), ((), ()))
        kn, vs, beta = _heads(kn_ref), _heads(vs_ref), _heads(beta_ref)
        cm = _pair_common(_heads(qn_ref), kn, vs, beta, _heads(g_ref))
        tmv, duv, dwv, dattv, dqdv, dkdv = (_heads(r) for r in (t_ref, du_ref, dw_ref, datt_ref, dqd_ref, dkd_ref))
        dvb = _each(_bdot_tn, tmv, duv)
        dt = _each(lambda a, b, c, d: _bdot_nt(a, b) + _bdot_nt(c, d), duv, cm["vb"], dwv, cm["kbg"])
        dkbg = _each(_bdot_tn, tmv, dwv)
        m1 = _each(_bdot_tn, tmv, dt)
        da = _each(lambda a, b: -jnp.where(cm["strict"], _bdot_nt(a, b), 0.0), m1, tmv)
        dkk = _each(lambda a, b: a * b, da, cm["decay"])
        dqk = _each(lambda a, b: a * b, dattv, cm["decay"])
        dd = _each(lambda a, b, c, d: a * b + c * d, dkk, cm["kk"], dqk, cm["qk"])
        dkb = _each(lambda a, b, c, d: _bdot(a, b) + c * d, dkk, kn, dkbg, cm["egc"])
        dq = _each(lambda a, b, c, d: _bdot(a, b) + c * d, dqk, kn, dqdv, cm["egc"])
        dkn = _each(lambda a, b, c, d: _bdot_tn(a, b) + _bdot_tn(c, d), dkk, cm["kb"], dqk, cm["q"])
        dkn = _each(lambda a, b, c, d, e: a + b * c + d * e, dkn, dkdv, cm["ekd"], dkb, beta)
        t_kd = _each(lambda a, b, c: _rowsum(a * b * c), dkdv, kn, cm["ekd"])
        split = _each(_split, dd)
        rows_dd = [jnp.dot(hi, ones, preferred_element_type=F32) + jnp.dot(lo, ones, preferred_element_type=F32)
                   for hi, lo in split]
        cols_dd = [lax.dot_general(hi, ones, tn, preferred_element_type=F32)
                   + lax.dot_general(lo, ones, tn, preferred_element_type=F32) for hi, lo in split]
        dgc = _each(lambda r, c, a, b, e, f, k, t: r - c + _rowsum(a * b * e) + _rowsum(f * k) - t,
                    rows_dd, cols_dd, dqdv, cm["q"], cm["egc"], dkbg, cm["kbg"], t_kd)
        same_b = cm["same"].astype(BF16)
        rowi = lax.broadcasted_iota(I32, (PAIR, HEAD), 0)
        dcd = _each(lambda d: jnp.where(rowi < CHUNK, d[0:1], d[8:9]), _heads(dcd_ref, rows=16))
        dgl = _each(lambda t, d, c: _mask_dot(same_b, jnp.broadcast_to(t, (PAIR, HEAD))) + d * c, t_kd, dcd, cm["cd"])
        is_last = jnp.bitwise_and(rowi, CHUNK - 1) == CHUNK - 1
        dgc = _each(lambda a, b: a + jnp.where(is_last, b, 0.0), dgc, dgl)
        r = lax.broadcasted_iota(I32, (PAIR, PAIR), 0)
        c = lax.broadcasted_iota(I32, (PAIR, PAIR), 1)
        upper_b = (cm["same"] & (r <= c)).astype(BF16)
        _put_heads(dg_ref, _each(lambda v: _mask_dot(upper_b, v), dgc))
        _put_heads(dbeta_ref, _each(lambda a, b, c, d: jnp.broadcast_to(_rowsum(a * b) + _rowsum(c * d), (PAIR, HEAD)),
                                    dkb, kn, dvb, vs))
        _put_heads(dqn_ref, _each(lambda v: v * QK_SCALE, dq))
        _put_heads(dkn_ref, dkn)
        _put_heads(dvs_ref, _each(lambda a, b: a * b, dvb, beta))

    return _call(
        body, name="intra_bwd", grid=(s // (INTRA_PAIRS * PAIR),),
        in_specs=[_pair_spec()] * 11 + [_chunk_scalar_spec(INTRA_PAIRS)], out_specs=[_pair_spec()] * 5,
        out_shape=[_sds((s, D_HALF))] * 5,
        compiler_params=_params("arbitrary"),
    )(qn, kn, vs, beta, g, tm, du, dw, datt, dqd, dkd, dcd)


def _rows8(x):
    acc = x[0:8]
    for r in range(8, x.shape[0], 8):
        acc = acc + x[r:r + 8]
    return acc


def _conv_bwd(proj, conv_w, a_log, dt_bias, dqn, dkn, dvs, dbeta, dg):
    s = proj.shape[0]
    t = CONV_T
    n_tiles = s // t
    n_sub = t // CONV_SUB
    tile_of = lambda i: n_tiles - 1 - i

    def body(q_ref, k_ref, v_ref, hq_ref, hk_ref, hv_ref, ba_ref, cw_ref, al_ref, dtb_ref,
             dqn_ref, dkn_ref, dvs_ref, dbeta_ref, dg_ref, oq_ref, ok_ref, ov_ref, dba_ref, gcw_out, gsm_out,
             below, gcw_ref, gsm_ref):
        @pl.when(pl.program_id(0) == 0)
        def _():
            gcw_ref[...] = jnp.zeros_like(gcw_ref)
            gsm_ref[...] = jnp.zeros_like(gsm_ref)
            below[...] = jnp.zeros_like(below)

        live = (pl.program_id(0) < n_tiles - 1).astype(F32)
        parts = ((q_ref, hq_ref, dqn_ref, oq_ref), (k_ref, hk_ref, dkn_ref, ok_ref), (v_ref, hv_ref, dvs_ref, ov_ref))
        lane = lax.broadcasted_iota(I32, (CONV_SUB, HEAD), 1)
        lane8 = lax.broadcasted_iota(I32, (8, HEAD), 1)

        def sub_tile(r0, first):
            rows = pl.ds(r0, CONV_SUB)
            for p, (x_ref, h_ref, d_ref, o_ref) in enumerate(parts):
                for h in HEADS:
                    cs = HEAD_COLS[h]
                    wide = slice(p * D_HALF + h * HEAD, p * D_HALF + (h + 1) * HEAD)
                    cw = cw_ref[:, wide]
                    prev8 = h_ref[:, cs] * live if first else x_ref[pl.ds(r0 - 8, 8), cs]
                    taps = _conv_taps(x_ref[rows, cs], prev8)
                    y = _conv_pre(taps, cw)
                    sg = _sigmoid(y)
                    sv = y * sg
                    ds = d_ref[rows, cs]
                    if p < 2:
                        rn = lax.rsqrt(_rowsum(sv * sv) + EPS)
                        nrm = sv * rn
                        ds = rn * (ds - nrm * _rowsum(ds * nrm))
                    dy = ds * (sg * (1.0 + y * (1.0 - sg)))
                    for j in range(CONV_K):
                        gcw_ref[8 * j:8 * j + 8, wide] += _rows8(dy * taps[j])
                    nxt = below[:, wide]
                    acc = dy * cw[CONV_K - 1:CONV_K]
                    for sft in range(1, CONV_K):
                        acc = acc + _shift_up(dy, nxt, sft) * cw[CONV_K - 1 - sft:CONV_K - sft]
                    o_ref[rows, cs] = acc.astype(BF16)
                    below[:, wide] = dy[0:8]

            ba = ba_ref[rows, :]
            dba = jnp.zeros((CONV_SUB, HEAD), F32)
            gsm = jnp.zeros((8, HEAD), F32)
            for h in HEADS:
                beta = _sigmoid(ba[:, h:h + 1])
                dbeta = dbeta_ref[rows, h * HEAD:h * HEAD + 1]
                xg = ba[:, N_HEADS + h:N_HEADS + h + 1] + dtb_ref[0:1, h:h + 1]
                nexp = -jnp.exp(al_ref[0:1, h:h + 1])
                dgv = dg_ref[rows, h * HEAD:h * HEAD + 1]
                da = dgv * nexp * _sigmoid(xg)
                dba = dba + jnp.where(lane == h, dbeta * beta * (1.0 - beta), 0.0) + jnp.where(lane == N_HEADS + h, da, 0.0)
                gsm = (gsm + jnp.where(lane8 == h, _rows8(dgv * nexp * _softplus(xg)), 0.0)
                       + jnp.where(lane8 == N_HEADS + h, _rows8(da), 0.0))
            dba_ref[rows, :] = jnp.zeros((CONV_SUB, D_HALF), BF16)
            dba_ref[rows, :HEAD] = dba.astype(BF16)
            gsm_ref[...] += gsm

        def step(k, carry):
            sub_tile(pl.multiple_of((n_sub - 1 - k) * CONV_SUB, CONV_SUB), False)
            return carry

        lax.fori_loop(0, n_sub - 1, step, 0)
        sub_tile(0, True)

        @pl.when(pl.program_id(0) == n_tiles - 1)
        def _():
            gcw_out[...] = jnp.zeros_like(gcw_out)
            for j in range(CONV_K):
                gcw_out[j:j + 1, :] = _colsum(gcw_ref[8 * j:8 * j + 8, :])
            gsm_out[...] = jnp.broadcast_to(_colsum(gsm_ref[...]), (8, HEAD))

    row = pl.BlockSpec((t, D_HALF), lambda i: (tile_of(i), 0))
    const = lambda shape: pl.BlockSpec(shape, lambda i: (0, 0))
    return _call(
        body, name="conv_bwd", grid=(n_tiles,),
        in_specs=_conv_specs(t, tile_of) + [pl.BlockSpec((t, HEAD), lambda i: (tile_of(i), COL_BA // HEAD)),
                                            const((CONV_K, 3 * D_HALF)), const((1, N_HEADS)), const((1, N_HEADS))] + [row] * 5,
        out_specs=[row, row, row, row, const((8, 3 * D_HALF)), const((8, HEAD))],
        out_shape=[_sds((s, D_HALF), BF16)] * 4 + [_sds((8, 3 * D_HALF)), _sds((8, HEAD))],
        scratch_shapes=[pltpu.VMEM((8, 3 * D_HALF), F32), pltpu.VMEM((8 * CONV_K, 3 * D_HALF), F32),
                        pltpu.VMEM((8, HEAD), F32)],
        compiler_params=_params("arbitrary"),
    )(proj, proj, proj, proj, proj, proj, proj, conv_w, a_log, dt_bias, dqn, dkn, dvs, dbeta, dg)


def _conv_bwd_pre(proj, conv_w, a_log, dt_bias, dqn, dkn, dvs, dbeta, dg):
    s = proj.shape[0]
    t = CONV_T

    def body(q_ref, k_ref, v_ref, hq_ref, hk_ref, hv_ref, ba_ref, cw_ref, al_ref, dtb_ref,
             dqn_ref, dkn_ref, dvs_ref, dbeta_ref, dg_ref, dyq_ref, dyk_ref, dyv_ref, dba_ref, gcw_ref, gsm_ref):
        @pl.when(pl.program_id(0) == 0)
        def _():
            gcw_ref[...] = jnp.zeros_like(gcw_ref)
            gsm_ref[...] = jnp.zeros_like(gsm_ref)

        live = (pl.program_id(0) > 0).astype(F32)
        parts = ((q_ref, hq_ref, dqn_ref, dyq_ref), (k_ref, hk_ref, dkn_ref, dyk_ref), (v_ref, hv_ref, dvs_ref, dyv_ref))
        for p, (x_ref, h_ref, d_ref, dy_ref) in enumerate(parts):
            cols = slice(p * D_HALF, (p + 1) * D_HALF)
            taps = _conv_taps(x_ref[...], h_ref[...] * live)
            y = _conv_pre(taps, cw_ref[:, cols])
            sg = _sigmoid(y)
            sv = y * sg
            if p == 2:
                ds = d_ref[...]
            else:
                segs = []
                for h in HEADS:
                    seg = _head(sv, h)
                    rn = lax.rsqrt(_rowsum(seg * seg) + EPS)
                    nrm = seg * rn
                    dn = d_ref[:, HEAD_COLS[h]]
                    segs.append(rn * (dn - nrm * _rowsum(dn * nrm)))
                ds = jnp.concatenate(segs, axis=1)
            dy = ds * (sg * (1.0 + y * (1.0 - sg)))
            dy_ref[...] = dy
            for j in range(CONV_K):
                gcw_ref[j:j + 1, cols] += _colsum(dy * taps[j])

        ba = ba_ref[...]
        lane = lax.broadcasted_iota(I32, (t, HEAD), 1)
        lane1 = lax.broadcasted_iota(I32, (1, HEAD), 1)
        dba = jnp.zeros((t, HEAD), F32)
        gsm = jnp.zeros((1, HEAD), F32)
        for h in HEADS:
            beta = _sigmoid(ba[:, h:h + 1])
            dbeta = dbeta_ref[:, h * HEAD:h * HEAD + 1]
            xg = ba[:, N_HEADS + h:N_HEADS + h + 1] + dtb_ref[0:1, h:h + 1]
            nexp = -jnp.exp(al_ref[0:1, h:h + 1])
            dgv = dg_ref[:, h * HEAD:h * HEAD + 1]
            da = dgv * nexp * _sigmoid(xg)
            dba = dba + jnp.where(lane == h, dbeta * beta * (1.0 - beta), 0.0) + jnp.where(lane == N_HEADS + h, da, 0.0)
            gsm = (gsm + jnp.where(lane1 == h, _colsum(dgv * nexp * _softplus(xg)), 0.0)
                   + jnp.where(lane1 == N_HEADS + h, _colsum(da), 0.0))
        dba_ref[...] = jnp.zeros_like(dba_ref)
        dba_ref[:, :HEAD] = dba.astype(BF16)
        gsm_ref[0:1, :] += gsm

    row = pl.BlockSpec((t, D_HALF), lambda i: (i, 0))
    return _call(
        body, name="conv_bwd_pre", grid=(s // t,),
        in_specs=_conv_specs(t) + [pl.BlockSpec((t, HEAD), lambda i: (i, COL_BA // HEAD)),
                                   pl.BlockSpec((CONV_K, 3 * D_HALF), lambda i: (0, 0)),
                                   pl.BlockSpec((1, N_HEADS), lambda i: (0, 0)),
                                   pl.BlockSpec((1, N_HEADS), lambda i: (0, 0))] + [row] * 5,
        out_specs=[row, row, row, row,
                   pl.BlockSpec((8, 3 * D_HALF), lambda i: (0, 0)), pl.BlockSpec((8, HEAD), lambda i: (0, 0))],
        out_shape=[_sds((s, D_HALF))] * 3 + [_sds((s, D_HALF), BF16), _sds((8, 3 * D_HALF)), _sds((8, HEAD))],
        compiler_params=_params("arbitrary"),
    )(proj, proj, proj, proj, proj, proj, proj, conv_w, a_log, dt_bias, dqn, dkn, dvs, dbeta, dg)


def _conv_bwd_in(dyq, dyk, dyv, conv_w):
    s = dyq.shape[0]
    t = CONV_T
    last = s // 8 - 1

    def body(q_ref, k_ref, v_ref, nq_ref, nk_ref, nv_ref, cw_ref, oq_ref, ok_ref, ov_ref):
        more = (pl.program_id(0) < pl.num_programs(0) - 1).astype(F32)
        for p, (d_ref, n_ref, o_ref) in enumerate(((q_ref, nq_ref, oq_ref), (k_ref, nk_ref, ok_ref), (v_ref, nv_ref, ov_ref))):
            cw = cw_ref[:, p * D_HALF:(p + 1) * D_HALF]
            dy = d_ref[...]
            nxt = n_ref[...] * more
            acc = dy * cw[3:4]
            for sft in (1, 2, 3):
                acc = acc + _shift_up(dy, nxt, sft) * cw[3 - sft:4 - sft]
            o_ref[...] = acc.astype(BF16)

    row = pl.BlockSpec((t, D_HALF), lambda i: (i, 0))
    nxt = pl.BlockSpec((8, D_HALF), lambda i: (jnp.minimum((i + 1) * (t // 8), last), 0))
    return _call(
        body, name="conv_bwd_in", grid=(s // t,),
        in_specs=[row] * 3 + [nxt] * 3 + [pl.BlockSpec((CONV_K, 3 * D_HALF), lambda i: (0, 0))],
        out_specs=[row] * 3, out_shape=[_sds((s, D_HALF), BF16)] * 3,
        compiler_params=_params("arbitrary"),
    )(dyq, dyk, dyv, dyq, dyk, dyv, conv_w)


IN_T = 512


def _in_bwd(x, dh, norm_w, w_pad, pieces):
    s = x.shape[0]
    t = IN_T
    widths = [D_HALF] * 6 + [N_IN_PAD - COL_BA]

    def body(*refs):
        x_ref, dh_ref, nw_ref, w_ref = refs[:4]
        p_refs = refs[4:4 + len(pieces)]
        gx_ref, gnw_ref = refs[4 + len(pieces):]

        @pl.when(pl.program_id(0) == 0)
        def _():
            gnw_ref[...] = jnp.zeros_like(gnw_ref)

        dn = jnp.zeros((t, D_MODEL), F32)
        col = 0
        for p_ref, wd in zip(p_refs, widths):
            dn = dn + _bdot_nt(p_ref[...], w_ref[:, col:col + wd])
            col += wd
        xv = x_ref[...]
        r = lax.rsqrt(jnp.mean(xv * xv, axis=-1, keepdims=True) + EPS)
        xhat = xv * r
        gnw_ref[...] += _colsum(dn * xhat)
        dxh = dn * nw_ref[...]
        gx_ref[...] = dh_ref[...] + r * (dxh - xhat * jnp.mean(dxh * xhat, axis=-1, keepdims=True))

    wide = pl.BlockSpec((t, D_MODEL), lambda i: (i, 0))
    return _call(
        body, name="in_bwd", grid=(s // t,),
        in_specs=[wide, wide, pl.BlockSpec((1, D_MODEL), lambda i: (0, 0)),
                  pl.BlockSpec((D_MODEL, N_IN_PAD), lambda i: (0, 0))]
                 + [pl.BlockSpec((t, wd), lambda i: (i, 0)) for wd in widths],
        out_specs=[wide, pl.BlockSpec((1, D_MODEL), lambda i: (0, 0))],
        out_shape=[_sds((s, D_MODEL)), _sds((1, D_MODEL))],
        compiler_params=_params("arbitrary"),
    )(x, dh, norm_w, w_pad, *pieces)


def _adamw_shard(name, w, g_own, g_got, cidx, m, v):
    _, r, c = w.shape
    half = r // 2
    rows = 256 if half % 256 == 0 else half
    per_half = half // rows

    def body(c_ref, w_ref, go_ref, gg_ref, m_ref, v_ref, gout_ref, d_ref, nm_ref, nv_ref):
        mine = (pl.program_id(0) // per_half) == c_ref[0]
        gv = jnp.where(mine, go_ref[:, :c], gg_ref[:, :c])
        gout_ref[0] = gv
        mn = ADAM_B1 * m_ref[0] + (1.0 - ADAM_B1) * gv
        vn = ADAM_B2 * v_ref[0] + (1.0 - ADAM_B2) * (gv * gv)
        m_hat = mn / (1.0 - ADAM_B1 ** ADAM_STEP)
        v_hat = vn / (1.0 - ADAM_B2 ** ADAM_STEP)
        d_ref[0] = -ADAM_LR * (m_hat / (jnp.sqrt(v_hat) + ADAM_EPS) + ADAM_WD * w_ref[0])
        nm_ref[0] = mn
        nv_ref[0] = vn

    blk = pl.BlockSpec((1, rows, c), lambda i, c_ref: (0, i, 0))
    gblk = pl.BlockSpec((rows, g_own.shape[1]), lambda i, c_ref: (i % per_half, 0))
    return _call(
        body, name=name,
        grid_spec=pltpu.PrefetchScalarGridSpec(
            num_scalar_prefetch=1, grid=(2 * per_half,),
            in_specs=[blk, gblk, gblk, blk, blk], out_specs=[blk] * 4),
        out_shape=[_sds((1, r, c))] * 4,
        compiler_params=_params("arbitrary"),
    )(cidx, w, g_own, g_got, m, v)


def _exchange(name, inputs, out_shapes, phases):
    n_in = len(inputs)
    n_out = len(out_shapes)
    n_cp = sum(len(p) for p in phases)

    def body(*refs):
        ins, outs = refs[:n_in], refs[n_in:n_in + n_out]
        send, recv = refs[n_in + n_out:]
        pos = (lax.axis_index("x"), lax.axis_index("y"), lax.axis_index("c"))
        k = 0
        for phase in phases:
            cps = []
            for src, dst, target in phase:
                cps.append(pltpu.make_async_remote_copy(
                    src_ref=src(ins, outs, pos), dst_ref=dst(ins, outs, pos), send_sem=send.at[k], recv_sem=recv.at[k],
                    device_id=target(pos), device_id_type=pl.DeviceIdType.MESH))
                k += 1
            for cp in cps:
                cp.start()
            for cp in cps:
                cp.wait()

    anyspec = pl.BlockSpec(memory_space=pl.ANY)
    return _call(
        body, name=name,
        in_specs=[anyspec] * n_in, out_specs=[anyspec] * n_out, out_shape=list(out_shapes),
        scratch_shapes=[pltpu.SemaphoreType.DMA((n_cp,)), pltpu.SemaphoreType.DMA((n_cp,))],
    )(*inputs)


def _chip(pos):
    return 2 * pos[0] + pos[1]


def _other_chip(pos, mask):
    x, y, c = pos
    return (x ^ (mask >> 1), y ^ (mask & 1), c)


def _sibling(pos):
    return (pos[0], pos[1], 1 - pos[2])


def _gather_weights(wb, ob, cb):
    halves = (wb.shape[0] // 2, ob.shape[0] // 2)

    def half(a, pos):
        return pl.ds(pos[2] * halves[a], halves[a])

    first, second = [], []
    for mask in CHIP_MASKS:
        for a in (0, 1):
            first.append((lambda ins, outs, pos, a=a: ins[a].at[half(a, pos)],
                          lambda ins, outs, pos, a=a: outs[a].at[_chip(pos), half(a, pos)],
                          functools.partial(_other_chip, mask=mask)))
            second.append((lambda ins, outs, pos, a=a, mask=mask: outs[a].at[_chip(pos) ^ mask, half(a, pos)],
                           lambda ins, outs, pos, a=a, mask=mask: outs[a].at[_chip(pos) ^ mask, half(a, pos)],
                           _sibling))
        first.append((lambda ins, outs, pos: ins[2],
                      lambda ins, outs, pos: outs[2].at[_chip(pos)],
                      functools.partial(_other_chip, mask=mask)))
    return _exchange("gather_weights", [wb, ob, cb],
                     [_sds((4,) + wb.shape, wb.dtype), _sds((4,) + ob.shape, ob.dtype), _sds((4,) + cb.shape, cb.dtype)],
                     [first, second])


def _to_sibling_half(name, arrays):
    def src(ins, outs, pos, a):
        h = arrays[a].shape[-2] // 2
        sl = pl.ds((1 - pos[2]) * h, h)
        return ins[a].at[:, sl] if arrays[a].ndim == 3 else ins[a].at[sl]

    outs = [_sds(a.shape[:-2] + (a.shape[-2] // 2, a.shape[-1]), a.dtype) for a in arrays]
    phase = [(functools.partial(src, a=a), lambda ins, outs, pos, a=a: outs[a], _sibling) for a in range(len(arrays))]
    return _exchange(name, arrays, outs, [phase])


def _add_half(name, full, part, cidx):
    shape = part.shape
    lead = shape[0] if len(shape) == 3 else 1
    rows, cols = shape[-2], shape[-1]
    tr = rows // 2 if rows % 16 == 0 else rows
    nr = rows // tr
    f3 = full.reshape((lead,) + full.shape[-2:])
    p3 = part.reshape((lead, rows, cols))

    def body(c_ref, f_ref, p_ref, o_ref):
        o_ref[...] = (f_ref[...].astype(F32) + p_ref[...].astype(F32)).astype(o_ref.dtype)

    out = _call(
        body, name=name,
        grid_spec=pltpu.PrefetchScalarGridSpec(
            num_scalar_prefetch=1, grid=(lead, nr),
            in_specs=[pl.BlockSpec((1, tr, cols), lambda b, r, c_ref: (b, c_ref[0] * nr + r, 0)),
                      pl.BlockSpec((1, tr, cols), lambda b, r, c_ref: (b, r, 0))],
            out_specs=pl.BlockSpec((1, tr, cols), lambda b, r, c_ref: (b, r, 0))),
        out_shape=_sds((lead, rows, cols), part.dtype),
        compiler_params=_params("arbitrary", "arbitrary"),
    )(cidx, f3, p3)
    return out.reshape(shape)


def _to_other_chips(name, arrays, blocked):
    def src(ins, outs, pos, a, mask):
        return ins[a].at[_chip(pos) ^ mask] if blocked[a] else ins[a]

    outs = [_sds((3,) + (a.shape[1:] if b else a.shape), a.dtype) for a, b in zip(arrays, blocked)]
    phase = []
    for mi, mask in enumerate(CHIP_MASKS):
        for a in range(len(arrays)):
            phase.append((functools.partial(src, a=a, mask=mask), lambda ins, outs, pos, a=a, mi=mi: outs[a].at[mi],
                          functools.partial(_other_chip, mask=mask)))
    return _exchange(name, arrays, outs, [phase])


def _add_chips(name, own, got, jidx, blocked):
    rows, cols = got.shape[-2:]
    tr = rows // 2 if rows % 16 == 0 else rows
    nr = rows // tr
    o3 = own if blocked else own.reshape((1, rows, cols))

    def body(j_ref, o_ref, g_ref, out_ref):
        out_ref[...] = ((o_ref[0].astype(F32) + g_ref[0].astype(F32))
                        + (g_ref[1].astype(F32) + g_ref[2].astype(F32)))

    own_map = (lambda r, j_ref: (j_ref[0], r, 0)) if blocked else (lambda r, j_ref: (0, r, 0))
    return _call(
        body, name=name,
        grid_spec=pltpu.PrefetchScalarGridSpec(
            num_scalar_prefetch=1, grid=(nr,),
            in_specs=[pl.BlockSpec((1, tr, cols), own_map),
                      pl.BlockSpec((3, tr, cols), lambda r, j_ref: (0, r, 0))],
            out_specs=pl.BlockSpec((tr, cols), lambda r, j_ref: (r, 0))),
        out_shape=_sds((rows, cols)),
        compiler_params=_params("arbitrary"),
    )(jidx, o3, got)


def _to_sibling(name, arrays):
    phase = [(lambda ins, outs, pos, a=a: ins[a], lambda ins, outs, pos, a=a: outs[a], _sibling)
             for a in range(len(arrays))]
    return _exchange(name, arrays, [_sds(a.shape, a.dtype) for a in arrays], [phase])


def _local_step(x, target, w_pad, w_out, conv_w, norm_w, pool_w, pool_scale, a_log, dt_bias, dn_norm_w, final_norm_w):
    proj, n_t = _proj_fwd(x, norm_w, w_pad)
    y_pool = _pool_fwd(proj, pool_w, pool_scale)
    qn, kn, vs, beta, g = _conv_fwd(proj, conv_w, a_log, dt_bias)
    u, w, att, qd, kd, tm, cd = _intra_fwd(qn, kn, vs, beta, g)
    o, vn, st = _scan_fwd(u, w, att, qd, kd, cd)
    y_t, dh, dyp, do, ddz, loss, g_fnw, g_dnw = _out_fwd_bwd(x, y_pool, o, proj, target, w_out, dn_norm_w, final_norm_w)
    g_wout = _token_matmul("grad_w_out", y_t, [(dh, 0), (dh, 1)])
    dpu, dpz, g_pw, g_ps = _pool_bwd(proj, dyp, pool_w, pool_scale)
    du, dw, datt, dqd, dkd, dcd = _scan_bwd(do, vn, qd, kd, w, att, cd, st)
    dqn, dkn, dvs, dbeta, dg = _intra_bwd(qn, kn, vs, beta, g, tm, du, dw, datt, dqd, dkd, dcd)
    dcq, dck, dcv, dba, g_cw, g_sm = _conv_bwd(proj, conv_w, a_log, dt_bias, dqn, dkn, dvs, dbeta, dg)
    pieces = [dpu, dpz, dcq, dck, dcv, ddz, dba]
    gx, g_nw = _in_bwd(x, dh, norm_w, w_pad, pieces)
    g_win = _token_matmul("grad_w_in", n_t, [(p, 0) for p in pieces])
    small = dict(norm_w=g_nw, pool_w=g_pw, pool_scale=g_ps, conv_w=g_cw[:CONV_K], a_log=g_sm[0:1, 0:N_HEADS],
                 dt_bias=g_sm[0:1, N_HEADS:2 * N_HEADS], dn_norm_w=g_dnw, final_norm_w=g_fnw)
    return loss[0, 0], gx, g_win, g_wout, small


def _pack_small(t):
    lanes = lambda a: jnp.pad(a.reshape(1, -1), ((0, 0), (0, HEAD - a.size)))
    rows = [t["pool_w"].reshape(-1, HEAD), t["norm_w"].reshape(-1, HEAD), t["final_norm_w"].reshape(-1, HEAD),
            t["pool_scale"].reshape(-1, HEAD), t["conv_w"].reshape(-1, HEAD), t["dn_norm_w"].reshape(1, HEAD),
            lanes(t["a_log"]), lanes(t["dt_bias"]), lanes(t.get("loss", jnp.zeros((1,), F32)))]
    buf = jnp.concatenate(rows, axis=0)
    return jnp.pad(buf, ((0, SMALL_ROWS - buf.shape[0]), (0, 0)))


def _unpack_small(buf, conv_cols):
    out, r = {}, 0
    for name, nrow, shape in (("pool_w", 512, (1, N_HEADS, HEAD, HEAD)), ("norm_w", 8, (1, D_MODEL)),
                              ("final_norm_w", 8, (D_MODEL,)), ("pool_scale", 4, (1, D_HALF)),
                              ("conv_w", CONV_K * conv_cols // HEAD, (1, CONV_K, conv_cols)), ("dn_norm_w", 1, (1, HEAD))):
        out[name] = buf[r:r + nrow].reshape(shape)
        r += nrow
    out["a_log"] = buf[r:r + 1, :N_HEADS]
    out["dt_bias"] = buf[r + 1:r + 2, :N_HEADS]
    out["loss"] = buf[r + 2, 0]
    return out


def kernel(x, norm_w, w_in, pool_w, pool_scale, conv_w, a_log, dt_bias, dn_norm_w, w_out, final_norm_w, loss_target, m_norm_w, m_w_in, m_pool_w, m_pool_scale, m_conv_w, m_a_log, m_dt_bias, m_dn_norm_w, m_w_out, m_final_norm_w, v_norm_w, v_w_in, v_pool_w, v_pool_scale, v_conv_w, v_a_log, v_dt_bias, v_dn_norm_w, v_w_out, v_final_norm_w):
    cidx = lax.axis_index("c").astype(I32).reshape(1)
    jidx = (2 * lax.axis_index("x") + lax.axis_index("y")).astype(I32)

    wb = jnp.pad(w_in[0].astype(BF16), ((0, 0), (0, BLK_IN_PAD - BLK_IN)))
    ob = w_out[0].astype(BF16)
    gw, go, gc = _gather_weights(wb, ob, conv_w[0])
    mine = lambda j: jidx == j
    w_pad = jnp.concatenate([jnp.where(mine(j), wb[:, :BLK_IN], gw[j, :, :BLK_IN]) for j in range(4)]
                            + [jnp.zeros((D_MODEL, N_IN_PAD - N_IN), BF16)], axis=1)
    wo_full = jnp.where((jnp.arange(4) == jidx)[:, None, None], ob[None], go).reshape(D_MODEL, D_MODEL)
    cw_full = jnp.concatenate([jnp.where(mine(j), conv_w[0], gc[j]) for j in range(4)], axis=1)

    loss, gx, g_win, g_wout, small = _local_step(
        x[0], loss_target[0], w_pad, wo_full, cw_full, norm_w, pool_w[0], pool_scale, a_log, dt_bias, dn_norm_w,
        final_norm_w.reshape(1, D_MODEL))
    small["loss"] = loss

    blocks_in = jnp.stack([jnp.pad(g_win[:, j * BLK_IN:(j + 1) * BLK_IN].astype(BF16), ((0, 0), (0, BLK_IN_PAD - BLK_IN)))
                           for j in range(4)])
    blocks_out = g_wout.astype(BF16).reshape(4, BLK_OUT, D_MODEL)
    full = [blocks_in, blocks_out, _pack_small(small)]
    from_sib = _to_sibling_half("reduce_sibling", full)
    chip_sum = [_add_half("add_sibling_%d" % i, f, p, cidx) for i, (f, p) in enumerate(zip(full, from_sib))]
    blocked = [True, True, False]
    from_chips = _to_other_chips("reduce_chips", chip_sum, blocked)
    halves = [_add_chips("add_chips_%d" % i, o, g, jidx.reshape(1), b)
              for i, (o, g, b) in enumerate(zip(chip_sum, from_chips, blocked))]
    other_halves = _to_sibling("swap_halves", halves)

    weights = dict(norm_w=norm_w, w_in=w_in, pool_w=pool_w, pool_scale=pool_scale, conv_w=conv_w, a_log=a_log,
                   dt_bias=dt_bias, dn_norm_w=dn_norm_w, w_out=w_out, final_norm_w=final_norm_w)
    ms = dict(norm_w=m_norm_w, w_in=m_w_in, pool_w=m_pool_w, pool_scale=m_pool_scale, conv_w=m_conv_w, a_log=m_a_log,
              dt_bias=m_dt_bias, dn_norm_w=m_dn_norm_w, w_out=m_w_out, final_norm_w=m_final_norm_w)
    vs = dict(norm_w=v_norm_w, w_in=v_w_in, pool_w=v_pool_w, pool_scale=v_pool_scale, conv_w=v_conv_w, a_log=v_a_log,
              dt_bias=v_dt_bias, dn_norm_w=v_dn_norm_w, w_out=v_w_out, final_norm_w=v_final_norm_w)
    names = ["norm_w", "w_in", "pool_w", "pool_scale", "conv_w", "a_log", "dt_bias", "dn_norm_w", "w_out", "final_norm_w"]
    small_names = [n for n in names if n not in ("w_in", "w_out")]

    def pack(t):
        conv = lax.dynamic_update_slice_in_dim(jnp.zeros((CONV_K, 3 * D_HALF), F32), t["conv_w"][0], jidx * BLK_CONV, axis=1)
        return _pack_small({**{n: t[n] for n in small_names if n != "conv_w"}, "conv_w": conv})[None]

    results = [{}, {}, {}, {}]
    for i, name in enumerate(("w_in", "w_out")):
        outs = _adamw_shard("adamw_" + name, weights[name], halves[i], other_halves[i], cidx, ms[name], vs[name])
        for res, o in zip(results, outs):
            res[name] = o
    outs = _adamw_shard("adamw_small", pack(weights), halves[2], other_halves[2], cidx, pack(ms), pack(vs))
    for res, o in zip(results, outs):
        got = _unpack_small(o[0], 3 * D_HALF)
        got["conv_w"] = lax.dynamic_slice_in_dim(got["conv_w"], jidx * BLK_CONV, BLK_CONV, axis=2)
        res.update(got)
    grads, delta, new_m, new_v = results

    return (grads["loss"], gx[None], *[grads[n] for n in names], *[delta[n] for n in names],
            *[new_m[n] for n in names], *[new_v[n] for n in names])
```

```python
import functools

import jax
import jax.numpy as jnp
from jax import lax
from jax.experimental import pallas as pl
from jax.experimental.pallas import tpu as pltpu

F32 = jnp.float32
BF16 = jnp.bfloat16
I32 = jnp.int32

D_MODEL = 1024
D_HALF = 512
N_HEADS = 4
HEAD = 128
CHUNK = 64
PAIR = 2 * CHUNK
WINDOWS = (2, 4, 8, 16)
CONV_K = 4
EPS = 1e-6
N_IN = 3080
N_IN_PAD = 3200
BLK_IN = 770
BLK_IN_PAD = 896
BLK_OUT = 256
BLK_CONV = 384
COL_BA = 3072
QK_SCALE = HEAD ** -0.5
SMALL_ROWS = 592
VMEM_LIMIT = 56 * 1024 * 1024

ADAM_LR = 0.001
ADAM_B1 = 0.9
ADAM_B2 = 0.999
ADAM_EPS = 1e-08
ADAM_WD = 0.01
ADAM_STEP = 10

CHIP_MASKS = (2, 1, 3)
HEADS = range(N_HEADS)
HEAD_COLS = [slice(h * HEAD, (h + 1) * HEAD) for h in HEADS]


def _call(body, **kw):
    return pl.pallas_call(body, **kw)


def _params(*sem):
    return pltpu.CompilerParams(dimension_semantics=sem, vmem_limit_bytes=VMEM_LIMIT)


def _sds(shape, dtype=F32):
    return jax.ShapeDtypeStruct(shape, dtype)


def _bdot(a, b):
    return jnp.dot(a.astype(BF16), b.astype(BF16), preferred_element_type=F32)


def _bdot_nt(a, b):
    return lax.dot_general(a.astype(BF16), b.astype(BF16), (((1,), (1,)), ((), ())), preferred_element_type=F32)


def _bdot_tn(a, b):
    return lax.dot_general(a.astype(BF16), b.astype(BF16), (((0,), (0,)), ((), ())), preferred_element_type=F32)


def _split(a):
    hi = a.astype(BF16)
    lo = (a - hi.astype(F32)).astype(BF16)
    return hi, lo


def _mask_dot(m, b, dims=(((1,), (0,)), ((), ()))):
    bh, bl = _split(b)
    dg = functools.partial(lax.dot_general, dimension_numbers=dims, preferred_element_type=F32)
    return dg(m, bh) + dg(m, bl)


def _sigmoid(x):
    return 0.5 * jnp.tanh(0.5 * x) + 0.5


def _softplus(x):
    return jnp.maximum(x, 0.0) + jnp.log(1.0 + jnp.exp(-jnp.abs(x)))


def _rowsum(x):
    return jnp.sum(x, axis=-1, keepdims=True)


def _colsum(x):
    return jnp.sum(x, axis=0, keepdims=True)


def _shift_down(xv, prev8, k):
    r = pltpu.roll(xv, k, 0)
    q = pltpu.roll(prev8, k, 0)
    row = lax.broadcasted_iota(I32, prev8.shape, 0)
    top = jnp.where(row < k, q, r[0:8])
    return jnp.concatenate([top, r[8:]], axis=0)


def _shift_up(xv, next8, k):
    t = xv.shape[0]
    r = pltpu.roll(xv, t - k, 0)
    q = pltpu.roll(next8, 8 - k, 0)
    row = lax.broadcasted_iota(I32, next8.shape, 0)
    bot = jnp.where(row >= 8 - k, q, r[t - 8:])
    return jnp.concatenate([r[:t - 8], bot], axis=0)


def _band(rows, cols, off, w, anti=False):
    r = lax.broadcasted_iota(I32, (rows, cols), 0)
    c = lax.broadcasted_iota(I32, (rows, cols), 1)
    d = (c - r + off) if anti else (r - c + off)
    return ((d >= 0) & (d < w)).astype(BF16)


def _head(ref_or_val, h):
    return ref_or_val[:, h * HEAD:(h + 1) * HEAD]


INTRA_PAIRS = 2
UNITS = [(pp, h) for pp in range(INTRA_PAIRS) for h in HEADS]


def _heads(ref, rows=PAIR):
    return [ref[pp * rows:(pp + 1) * rows, HEAD_COLS[h]] for pp, h in UNITS]


def _put_heads(ref, vals, rows=PAIR):
    for (pp, h), v in zip(UNITS, vals):
        ref[pp * rows:(pp + 1) * rows, HEAD_COLS[h]] = v.astype(ref.dtype)


def _each(fn, *lists):
    return [fn(*args) for args in zip(*lists)]


def _proj_fwd(x, norm_w, w_pad):
    s = x.shape[0]
    tm = 512

    def body(x_ref, nw_ref, w_ref, proj_ref, nt_ref):
        xv = x_ref[...]
        r = lax.rsqrt(jnp.mean(xv * xv, axis=-1, keepdims=True) + EPS)
        nv = xv * r * nw_ref[...]
        nt_ref[...] = nv.T.astype(BF16)
        proj_ref[...] = jnp.dot(nv.astype(BF16), w_ref[...], preferred_element_type=F32)

    return _call(
        body, name="proj_fwd", grid=(s // tm,),
        in_specs=[pl.BlockSpec((tm, D_MODEL), lambda i: (i, 0)),
                  pl.BlockSpec((1, D_MODEL), lambda i: (0, 0)),
                  pl.BlockSpec((D_MODEL, N_IN_PAD), lambda i: (0, 0))],
        out_specs=[pl.BlockSpec((tm, N_IN_PAD), lambda i: (i, 0)),
                   pl.BlockSpec((D_MODEL, tm), lambda i: (0, i))],
        out_shape=[_sds((s, N_IN_PAD)), _sds((D_MODEL, s), BF16)],
        compiler_params=_params("arbitrary"),
    )(x, norm_w, w_pad)


def _pool_mix(ug, hg, zg, pw_g, row0, w):
    t = ug.shape[0]
    win = _mask_dot(_band(t, t, 0, w), ug) + _mask_dot(_band(t, HEAD, HEAD, w), hg)
    cnt = jnp.minimum(row0 + lax.broadcasted_iota(I32, (t, 1), 0) + 1, w).astype(F32)
    mix = win / cnt - ug
    mixed = _bdot(mix, pw_g)
    sg = _sigmoid(zg)
    return mix, mixed, sg, cnt


POOL_T = 256


def _pool_fwd(proj, pool_w, pool_scale):
    s = proj.shape[0]
    t = POOL_T
    hb = t // HEAD

    def body(u_ref, z_ref, halo_ref, pw_ref, ps_ref, y_ref):
        i = pl.program_id(0)
        live = (i > 0).astype(F32)
        for g, w in enumerate(WINDOWS):
            sl = HEAD_COLS[g]
            zg = z_ref[:, sl]
            _, mixed, sg, _ = _pool_mix(u_ref[:, sl], halo_ref[:, sl] * live, zg, pw_ref[g], i * t, w)
            y_ref[:, sl] = mixed * ps_ref[:, sl] * (zg * sg)

    return _call(
        body, name="pool_fwd", grid=(s // t,),
        in_specs=[pl.BlockSpec((t, D_HALF), lambda i: (i, 0)),
                  pl.BlockSpec((t, D_HALF), lambda i: (i, 1)),
                  pl.BlockSpec((HEAD, D_HALF), lambda i: (jnp.maximum(i * hb - 1, 0), 0)),
                  pl.BlockSpec((N_HEADS, HEAD, HEAD), lambda i: (0, 0, 0)),
                  pl.BlockSpec((1, D_HALF), lambda i: (0, 0))],
        out_specs=pl.BlockSpec((t, D_HALF), lambda i: (i, 0)),
        out_shape=_sds((s, D_HALF)),
        compiler_params=_params("arbitrary"),
    )(proj, proj, proj, pool_w, pool_scale)


def _conv_taps(xv, prev8):
    return [_shift_down(xv, prev8, CONV_K - 1 - j) for j in range(CONV_K - 1)] + [xv]


def _conv_pre(taps, cw):
    y = taps[CONV_K - 1] * cw[CONV_K - 1:CONV_K]
    for j in range(CONV_K - 2, -1, -1):
        y = y + taps[j] * cw[j:j + 1]
    return y


CONV_T = 256
CONV_SUB = 256


def _conv_specs(t, tile_of=lambda i: i):
    tiles = [pl.BlockSpec((t, D_HALF), functools.partial(lambda i, p: (tile_of(i), 2 + p), p=p)) for p in range(3)]
    halos = [pl.BlockSpec((8, D_HALF),
                          functools.partial(lambda i, p: (jnp.maximum(tile_of(i) * (t // 8) - 1, 0), 2 + p), p=p))
             for p in range(3)]
    return tiles + halos


def _conv_fwd(proj, conv_w, a_log, dt_bias):
    s = proj.shape[0]
    t = CONV_T

    def body(q_ref, k_ref, v_ref, hq_ref, hk_ref, hv_ref, ba_ref, cw_ref, al_ref, dtb_ref,
             qn_ref, kn_ref, vs_ref, beta_ref, g_ref):
        live = (pl.program_id(0) > 0).astype(F32)
        parts = ((q_ref, hq_ref, qn_ref), (k_ref, hk_ref, kn_ref), (v_ref, hv_ref, vs_ref))

        def sub_tile(r0, first):
            rows = pl.ds(r0, CONV_SUB)
            for p, (x_ref, h_ref, o_ref) in enumerate(parts):
                for h in HEADS:
                    cs = HEAD_COLS[h]
                    prev8 = h_ref[:, cs] * live if first else x_ref[pl.ds(r0 - 8, 8), cs]
                    y = _conv_pre(_conv_taps(x_ref[rows, cs], prev8), cw_ref[:, p * D_HALF + h * HEAD:p * D_HALF + (h + 1) * HEAD])
                    sv = y * _sigmoid(y)
                    o_ref[rows, cs] = sv if p == 2 else sv * lax.rsqrt(_rowsum(sv * sv) + EPS)
            ba = ba_ref[rows, :]
            for h in HEADS:
                beta = _sigmoid(ba[:, h:h + 1])
                gl = -jnp.exp(al_ref[0:1, h:h + 1]) * _softplus(ba[:, N_HEADS + h:N_HEADS + h + 1] + dtb_ref[0:1, h:h + 1])
                beta_ref[rows, HEAD_COLS[h]] = jnp.broadcast_to(beta, (CONV_SUB, HEAD))
                g_ref[rows, HEAD_COLS[h]] = jnp.broadcast_to(gl, (CONV_SUB, HEAD))

        sub_tile(0, True)

        def step(k, carry):
            sub_tile(pl.multiple_of(k * CONV_SUB, CONV_SUB), False)
            return carry

        lax.fori_loop(1, t // CONV_SUB, step, 0)

    row = pl.BlockSpec((t, D_HALF), lambda i: (i, 0))
    return _call(
        body, name="conv_fwd", grid=(s // t,),
        in_specs=_conv_specs(t) + [pl.BlockSpec((t, HEAD), lambda i: (i, COL_BA // HEAD)),
                                   pl.BlockSpec((CONV_K, 3 * D_HALF), lambda i: (0, 0)),
                                   pl.BlockSpec((1, N_HEADS), lambda i: (0, 0)),
                                   pl.BlockSpec((1, N_HEADS), lambda i: (0, 0))],
        out_specs=[row] * 5,
        out_shape=[_sds((s, D_HALF))] * 5,
        compiler_params=_params("arbitrary"),
    )(proj, proj, proj, proj, proj, proj, proj, conv_w, a_log, dt_bias)


def _pair_masks():
    r = lax.broadcasted_iota(I32, (PAIR, PAIR), 0)
    c = lax.broadcasted_iota(I32, (PAIR, PAIR), 1)
    same = jnp.right_shift(r, 6) == jnp.right_shift(c, 6)
    return same, same & (r >= c), same & (r > c), r == c


def _pair_common(qn, kn, vs, beta, g):
    same, incl, strict, eye = _pair_masks()
    incl_b = incl.astype(BF16)
    first = lax.broadcasted_iota(I32, (PAIR, HEAD), 0) < CHUNK
    gc = _each(lambda gv: _mask_dot(incl_b, gv), g)
    gc_row = _each(lambda v: _colsum(jnp.where(eye, v, 0.0)), gc)
    decay = _each(lambda v, r: jnp.where(incl, jnp.exp(jnp.where(incl, v - r, 0.0)), 0.0), gc, gc_row)
    gl = _each(lambda v: jnp.where(first, v[CHUNK - 1:CHUNK], v[PAIR - 1:PAIR]), gc)
    egc = _each(jnp.exp, gc)
    q = _each(lambda v: v * QK_SCALE, qn)
    kb = _each(lambda k, b: k * b, kn, beta)
    return dict(same=same, incl=incl, strict=strict, eye=eye, gc=gc, decay=decay, gl=gl, egc=egc,
                ekd=_each(lambda a, b: jnp.exp(a - b), gl, gc), cd=_each(jnp.exp, gl), q=q, kb=kb,
                vb=_each(lambda v, b: v * b, vs, beta), kbg=_each(lambda k, e: k * e, kb, egc),
                kk=_each(_bdot_nt, kb, kn), qk=_each(_bdot_nt, q, kn))


def _tri_inv(a, eye_f):
    p = _each(lambda v: eye_f - v, a)
    x = _each(_bdot, a, a)
    for it in range(5):
        p = _each(lambda pv, xv: pv + _bdot(pv, xv), p, x)
        if it < 4:
            x = _each(_bdot, x, x)
    return p


def _pair_spec():
    return pl.BlockSpec((INTRA_PAIRS * PAIR, D_HALF), lambda i: (i, 0))


def _chunk_scalar_spec(pairs=1, index=lambda i: (i, 0)):
    return pl.BlockSpec((16 * pairs, D_HALF), index)


SCAN_PAIRS = 2
SCAN_ROWS = SCAN_PAIRS * PAIR


def _intra_fwd(qn, kn, vs, beta, g):
    s = qn.shape[0]

    def body(qn_ref, kn_ref, vs_ref, beta_ref, g_ref, u_ref, w_ref, att_ref, qd_ref, kd_ref, t_ref, cd_ref):
        kn = _heads(kn_ref)
        cm = _pair_common(_heads(qn_ref), kn, _heads(vs_ref), _heads(beta_ref), _heads(g_ref))
        a = _each(lambda kk, d: jnp.where(cm["strict"], kk * d, 0.0), cm["kk"], cm["decay"])
        tm = _tri_inv(a, cm["eye"].astype(F32))
        _put_heads(t_ref, tm)
        _put_heads(u_ref, _each(_bdot, tm, cm["vb"]))
        _put_heads(w_ref, _each(_bdot, tm, cm["kbg"]))
        _put_heads(att_ref, _each(lambda a, b: a * b, cm["qk"], cm["decay"]))
        _put_heads(qd_ref, _each(lambda a, b: a * b, cm["q"], cm["egc"]))
        _put_heads(kd_ref, _each(lambda a, b: a * b, kn, cm["ekd"]))
        for ci in range(2):
            for (pp, h), v in zip(UNITS, cm["cd"]):
                cd_ref[pp * 16 + ci * 8:pp * 16 + (ci + 1) * 8, HEAD_COLS[h]] = v[ci * CHUNK:ci * CHUNK + 8]

    return _call(
        body, name="intra_fwd", grid=(s // (INTRA_PAIRS * PAIR),),
        in_specs=[_pair_spec()] * 5, out_specs=[_pair_spec()] * 6 + [_chunk_scalar_spec(INTRA_PAIRS)],
        out_shape=[_sds((s, D_HALF))] + [_sds((s, D_HALF), BF16)] * 5 + [_sds((s // 8, D_HALF))],
        compiler_params=_params("arbitrary"),
    )(qn, kn, vs, beta, g)


def _scan_fwd(u, w, att, qd, kd, cd):
    s = u.shape[0]
    n_chunks = s // CHUNK

    def body(u_ref, w_ref, att_ref, qd_ref, kd_ref, cd_ref, o_ref, vn_ref, st_ref, state):
        @pl.when(pl.program_id(0) == 0)
        def _():
            state[...] = jnp.zeros_like(state)
        cols = list(enumerate(HEAD_COLS))
        sm = [state[h] for h in HEADS]
        for ci in range(2 * SCAN_PAIRS):
            rs = slice(ci * CHUNK, (ci + 1) * CHUNK)
            for h in HEADS:
                st_ref[ci, h] = sm[h]
            both = [_bdot(jnp.concatenate([w_ref[rs, sl], qd_ref[rs, sl]], axis=0), sm[h]) for h, sl in cols]
            vn = [u_ref[rs, sl] - both[h][:CHUNK] for h, sl in cols]
            for h, sl in cols:
                vn_ref[rs, sl] = vn[h].astype(BF16)
                o_ref[rs, sl] = both[h][CHUNK:]
            sm = [sm[h] * cd_ref[ci * 8:ci * 8 + 1, sl] + _bdot_tn(kd_ref[rs, sl], vn[h]) for h, sl in cols]
        for h in HEADS:
            state[h] = sm[h]
        for pp in range(SCAN_PAIRS):
            rp = slice(pp * PAIR, (pp + 1) * PAIR)
            intra = [_bdot(att_ref[rp, sl], vn_ref[rp, sl]) for sl in HEAD_COLS]
            for h, sl in cols:
                o_ref[rp, sl] += intra[h]

    rows = pl.BlockSpec((SCAN_ROWS, D_HALF), lambda i: (i, 0))
    return _call(
        body, name="scan_fwd", grid=(s // SCAN_ROWS,),
        in_specs=[rows] * 5 + [_chunk_scalar_spec(SCAN_PAIRS)],
        out_specs=[rows, rows, pl.BlockSpec((2 * SCAN_PAIRS, N_HEADS, HEAD, HEAD), lambda i: (i, 0, 0, 0))],
        out_shape=[_sds((s, D_HALF)), _sds((s, D_HALF), BF16), _sds((n_chunks, N_HEADS, HEAD, HEAD))],
        scratch_shapes=[pltpu.VMEM((N_HEADS, HEAD, HEAD), F32)],
        compiler_params=_params("arbitrary"),
    )(u, w, att, qd, kd, cd)


OUT_T = 512


def _out_fwd_bwd(x, y_pool, o, proj, target, w_out, dn_norm_w, final_norm_w):
    s = x.shape[0]
    t = OUT_T

    def body(x_ref, yp_ref, o_ref, z_ref, tg_ref, wo_ref, dnw_ref, fnw_ref,
             yt_ref, dh_ref, dyp_ref, do_ref, dz_ref, loss_ref, gfn_ref, gdn_ref, y_ref):
        @pl.when(pl.program_id(0) == 0)
        def _():
            loss_ref[...] = jnp.zeros_like(loss_ref)
            gfn_ref[...] = jnp.zeros_like(gfn_ref)
            gdn_ref[...] = jnp.zeros_like(gdn_ref)

        ypv = yp_ref[...]
        y_ref[:, :D_HALF] = ypv.astype(BF16)
        yt_ref[:D_HALF, :] = ypv.T.astype(BF16)
        dnw = dnw_ref[...]
        keep = []
        for h in HEADS:
            ov = o_ref[:, HEAD_COLS[h]]
            zv = z_ref[:, HEAD_COLS[h]]
            ro = lax.rsqrt(jnp.mean(ov * ov, axis=-1, keepdims=True) + EPS)
            ohat = ov * ro
            sg = _sigmoid(zv)
            keep.append((ro, ohat, zv, sg))
            ydn = ohat * dnw * (zv * sg)
            y_ref[:, D_HALF + h * HEAD:D_HALF + (h + 1) * HEAD] = ydn.astype(BF16)
            yt_ref[D_HALF + h * HEAD:D_HALF + (h + 1) * HEAD, :] = ydn.T.astype(BF16)

        hv = x_ref[...] + jnp.dot(y_ref[...], wo_ref[...], preferred_element_type=F32)
        r2 = lax.rsqrt(jnp.mean(hv * hv, axis=-1, keepdims=True) + EPS)
        hhat = hv * r2
        fnw = fnw_ref[...]
        err = hhat * fnw - tg_ref[...]
        loss_ref[...] += 0.5 * jnp.sum(_rowsum(err * err) * (1.0 / D_MODEL), axis=0, keepdims=True)
        dout = err * (1.0 / D_MODEL)
        gfn_ref[...] += _colsum(dout * hhat)
        dhh = dout * fnw
        dh = r2 * (dhh - hhat * jnp.mean(dhh * hhat, axis=-1, keepdims=True))
        dh_ref[...] = dh
        dy = _bdot_nt(dh, wo_ref[...])
        dyp_ref[...] = dy[:, :D_HALF]
        gdn = jnp.zeros((1, HEAD), F32)
        for h in HEADS:
            ro, ohat, zv, sg = keep[h]
            dyd = dy[:, D_HALF + h * HEAD:D_HALF + (h + 1) * HEAD]
            sz = zv * sg
            dz_ref[:, HEAD_COLS[h]] = (dyd * ohat * dnw * (sg * (1.0 + zv * (1.0 - sg)))).astype(BF16)
            gdn = gdn + _colsum(dyd * ohat * sz)
            doh = dyd * dnw * sz
            do_ref[:, HEAD_COLS[h]] = ro * (doh - ohat * jnp.mean(doh * ohat, axis=-1, keepdims=True))
        gdn_ref[...] += gdn

    wide = pl.BlockSpec((t, D_MODEL), lambda i: (i, 0))
    half = pl.BlockSpec((t, D_HALF), lambda i: (i, 0))
    const = lambda shape: pl.BlockSpec(shape, lambda i: (0,) * len(shape))
    return _call(
        body, name="out_fwd_bwd", grid=(s // t,),
        in_specs=[wide, half, half, pl.BlockSpec((t, D_HALF), lambda i: (i, 5)), wide,
                  const((D_MODEL, D_MODEL)), const((1, HEAD)), const((1, D_MODEL))],
        out_specs=[pl.BlockSpec((D_MODEL, t), lambda i: (0, i)), wide, half, half, half,
                   const((1, HEAD)), const((1, D_MODEL)), const((1, HEAD))],
        out_shape=[_sds((D_MODEL, s), BF16), _sds((s, D_MODEL)), _sds((s, D_HALF)), _sds((s, D_HALF)), _sds((s, D_HALF), BF16),
                   _sds((1, HEAD)), _sds((1, D_MODEL)), _sds((1, HEAD))],
        scratch_shapes=[pltpu.VMEM((t, D_MODEL), BF16)],
        compiler_params=_params("arbitrary"),
    )(x, y_pool, o, proj, target, w_out, dn_norm_w, final_norm_w)


def _token_matmul(name, at, pieces):
    m, s = at.shape
    n = len(pieces)
    tn, tk = D_HALF, 512

    def body(a_ref, *refs):
        p_refs, o_ref = refs[:n], refs[n]

        @pl.when(pl.program_id(0) == 0)
        def _():
            o_ref[...] = jnp.zeros_like(o_ref)

        av = a_ref[...]
        for p in range(n):
            o_ref[:, p * tn:(p + 1) * tn] += _bdot(av, p_refs[p][...])

    return _call(
        body, name=name, grid=(s // tk,),
        in_specs=[pl.BlockSpec((m, tk), lambda k: (0, k))]
                 + [pl.BlockSpec((tk, tn), functools.partial(lambda k, cb: (k, cb), cb=cb)) for _, cb in pieces],
        out_specs=pl.BlockSpec((m, n * tn), lambda k: (0, 0)),
        out_shape=_sds((m, n * tn)),
        compiler_params=_params("arbitrary"),
    )(at, *[p[0] for p in pieces])


def _pool_bwd(proj, dyp, pool_w, pool_scale):
    s = proj.shape[0]
    t = POOL_T
    hb = t // HEAD
    last = s // HEAD - 1

    def body(u_ref, z_ref, halo_ref, dy_ref, zn_ref, dyn_ref, pw_ref, ps_ref, du_ref, dz_ref, gpw_ref, gps_ref):
        i = pl.program_id(0)

        @pl.when(i == 0)
        def _():
            gpw_ref[...] = jnp.zeros_like(gpw_ref)
            gps_ref[...] = jnp.zeros_like(gps_ref)

        live = (i > 0).astype(F32)
        more = (i < pl.num_programs(0) - 1).astype(F32)
        for g, w in enumerate(WINDOWS):
            sl = HEAD_COLS[g]
            zg = z_ref[:, sl]
            ps = ps_ref[:, sl]
            pw = pw_ref[g]
            mix, mixed, sg, cnt = _pool_mix(u_ref[:, sl], halo_ref[:, sl] * live, zg, pw, i * t, w)
            dyg = dy_ref[:, sl]
            sz = zg * sg
            dz_ref[:, sl] = (dyg * mixed * ps * (sg * (1.0 + zg * (1.0 - sg)))).astype(BF16)
            gps_ref[:, sl] += _colsum(dyg * mixed * sz)
            dmixed = dyg * ps * sz
            gpw_ref[g] += _bdot_tn(mix, dmixed)
            dmix = _bdot_nt(dmixed, pw)
            zn = zn_ref[:, sl]
            dmix_n = _bdot_nt(dyn_ref[:, sl] * more * ps * (zn * _sigmoid(zn)), pw)
            du_ref[:, sl] = (_mask_dot(_band(t, t, 0, w, anti=True), dmix / cnt)
                             + _mask_dot(_band(t, HEAD, t, w, anti=True), dmix_n * (1.0 / w)) - dmix).astype(BF16)

    tile = lambda col: pl.BlockSpec((t, D_HALF), lambda i: (i, col))
    below = lambda col: pl.BlockSpec((HEAD, D_HALF), lambda i: (jnp.minimum((i + 1) * hb, last), col))
    return _call(
        body, name="pool_bwd", grid=(s // t,),
        in_specs=[tile(0), tile(1), pl.BlockSpec((HEAD, D_HALF), lambda i: (jnp.maximum(i * hb - 1, 0), 0)),
                  tile(0), below(1), below(0),
                  pl.BlockSpec((N_HEADS, HEAD, HEAD), lambda i: (0, 0, 0)), pl.BlockSpec((1, D_HALF), lambda i: (0, 0))],
        out_specs=[tile(0), tile(0), pl.BlockSpec((N_HEADS, HEAD, HEAD), lambda i: (0, 0, 0)),
                   pl.BlockSpec((1, D_HALF), lambda i: (0, 0))],
        out_shape=[_sds((s, D_HALF), BF16), _sds((s, D_HALF), BF16), _sds((N_HEADS, HEAD, HEAD)), _sds((1, D_HALF))],
        compiler_params=_params("arbitrary"),
    )(proj, proj, proj, dyp, proj, dyp, pool_w, pool_scale)


def _scan_bwd(do, vn, qd, kd, w, att, cd, st):
    s = do.shape[0]
    n_steps = s // SCAN_ROWS

    def body(do_ref, vn_ref, qd_ref, kd_ref, w_ref, att_ref, cd_ref, st_ref,
             du_ref, dw_ref, datt_ref, dqd_ref, dkd_ref, dcd_ref, dstate):
        @pl.when(pl.program_id(0) == 0)
        def _():
            dstate[...] = jnp.zeros_like(dstate)
        _, incl, _, _ = _pair_masks()
        cols = list(enumerate(HEAD_COLS))
        dv_intra = []
        for pp in range(SCAN_PAIRS):
            rp = slice(pp * PAIR, (pp + 1) * PAIR)
            dv_intra.append([_bdot_tn(att_ref[rp, sl], do_ref[rp, sl]) for _, sl in cols])
            for _, sl in cols:
                datt_ref[rp, sl] = jnp.where(incl, _bdot_nt(do_ref[rp, sl], vn_ref[rp, sl]), 0.0)
        ds = [dstate[h] for h in HEADS]
        for ci in range(2 * SCAN_PAIRS - 1, -1, -1):
            rs = slice(ci * CHUNK, (ci + 1) * CHUNK)
            in_pair = slice((ci % 2) * CHUNK, (ci % 2 + 1) * CHUNK)
            sm = [st_ref[ci, h] for h in HEADS]
            dvn = [dv_intra[ci // 2][h][in_pair] + _bdot(kd_ref[rs, sl], ds[h]) for h, sl in cols]
            for h, sl in cols:
                du_ref[rs, sl] = dvn[h].astype(BF16)
            dqd = [_bdot_nt(do_ref[rs, sl], sm[h]) for h, sl in cols]
            dw = [-_bdot_nt(dvn[h], sm[h]) for h, _ in cols]
            dkd = [_bdot_nt(vn_ref[rs, sl], ds[h]) for h, sl in cols]
            dcd = [jnp.broadcast_to(_rowsum(_colsum(ds[h] * sm[h])), (8, HEAD)) for h in HEADS]
            for h, sl in cols:
                dqd_ref[rs, sl] = dqd[h]
                dw_ref[rs, sl] = dw[h].astype(BF16)
                dkd_ref[rs, sl] = dkd[h]
                dcd_ref[ci * 8:(ci + 1) * 8, sl] = dcd[h]
            ds = [ds[h] * cd_ref[ci * 8:ci * 8 + 1, sl] + _bdot_tn(qd_ref[rs, sl], do_ref[rs, sl])
                  - _bdot_tn(w_ref[rs, sl], dvn[h]) for h, sl in cols]
        for h in HEADS:
            dstate[h] = ds[h]

    rev = pl.BlockSpec((SCAN_ROWS, D_HALF), lambda i: (n_steps - 1 - i, 0))
    rev_scalar = _chunk_scalar_spec(SCAN_PAIRS, lambda i: (n_steps - 1 - i, 0))
    return _call(
        body, name="scan_bwd", grid=(n_steps,),
        in_specs=[rev] * 6 + [rev_scalar,
                              pl.BlockSpec((2 * SCAN_PAIRS, N_HEADS, HEAD, HEAD), lambda i: (n_steps - 1 - i, 0, 0, 0))],
        out_specs=[rev] * 5 + [rev_scalar],
        out_shape=[_sds((s, D_HALF), BF16)] * 2 + [_sds((s, D_HALF))] * 3 + [_sds((s // 8, D_HALF))],
        scratch_shapes=[pltpu.VMEM((N_HEADS, HEAD, HEAD), F32)],
        compiler_params=_params("arbitrary"),
    )(do, vn, qd, kd, w, att, cd, st)


def _intra_bwd(qn, kn, vs, beta, g, tm, du, dw, datt, dqd, dkd, dcd):
    s = qn.shape[0]

    def body(qn_ref, kn_ref, vs_ref, beta_ref, g_ref, t_ref, du_ref, dw_ref, datt_ref, dqd_ref, dkd_ref, dcd_ref,
             dqn_ref, dkn_ref, dvs_ref, dbeta_ref, dg_ref):
        ones = jnp.ones((PAIR, HEAD), BF16)
        tn = (((0,), (0,)), ((), ()))
        kn, vs, beta = _heads(kn_ref), _heads(vs_ref), _heads(beta_ref)
        cm = _pair_common(_heads(qn_ref), kn, vs, beta, _heads(g_ref))
        tmv, duv, dwv, dattv, dqdv, dkdv = (_heads(r) for r in (t_ref, du_ref, dw_ref, datt_ref, dqd_ref, dkd_ref))
        dvb = _each(_bdot_tn, tmv, duv)
        dt = _each(lambda a, b, c, d: _bdot_nt(a, b) + _bdot_nt(c, d), duv, cm["vb"], dwv, cm["kbg"])
        dkbg = _each(_bdot_tn, tmv, dwv)
        m1 = _each(_bdot_tn, tmv, dt)
        da = _each(lambda a, b: -jnp.where(cm["strict"], _bdot_nt(a, b), 0.0), m1, tmv)
        dkk = _each(lambda a, b: a * b, da, cm["decay"])
        dqk = _each(lambda a, b: a * b, dattv, cm["decay"])
        dd = _each(lambda a, b, c, d: a * b + c * d, dkk, cm["kk"], dqk, cm["qk"])
        dkb = _each(lambda a, b, c, d: _bdot(a, b) + c * d, dkk, kn, dkbg, cm["egc"])
        dq = _each(lambda a, b, c, d: _bdot(a, b) + c * d, dqk, kn, dqdv, cm["egc"])
        dkn = _each(lambda a, b, c, d: _bdot_tn(a, b) + _bdot_tn(c, d), dkk, cm["kb"], dqk, cm["q"])
        dkn = _each(lambda a, b, c, d, e: a + b * c + d * e, dkn, dkdv, cm["ekd"], dkb, beta)
        t_kd = _each(lambda a, b, c: _rowsum(a * b * c), dkdv, kn, cm["ekd"])
        split = _each(_split, dd)
        rows_dd = [jnp.dot(hi, ones, preferred_element_type=F32) + jnp.dot(lo, ones, preferred_element_type=F32)
                   for hi, lo in split]
        cols_dd = [lax.dot_general(hi, ones, tn, preferred_element_type=F32)
                   + lax.dot_general(lo, ones, tn, preferred_element_type=F32) for hi, lo in split]
        dgc = _each(lambda r, c, a, b, e, f, k, t: r - c + _rowsum(a * b * e) + _rowsum(f * k) - t,
                    rows_dd, cols_dd, dqdv, cm["q"], cm["egc"], dkbg, cm["kbg"], t_kd)
        same_b = cm["same"].astype(BF16)
        rowi = lax.broadcasted_iota(I32, (PAIR, HEAD), 0)
        dcd = _each(lambda d: jnp.where(rowi < CHUNK, d[0:1], d[8:9]), _heads(dcd_ref, rows=16))
        dgl = _each(lambda t, d, c: _mask_dot(same_b, jnp.broadcast_to(t, (PAIR, HEAD))) + d * c, t_kd, dcd, cm["cd"])
        is_last = jnp.bitwise_and(rowi, CHUNK - 1) == CHUNK - 1
        dgc = _each(lambda a, b: a + jnp.where(is_last, b, 0.0), dgc, dgl)
        r = lax.broadcasted_iota(I32, (PAIR, PAIR), 0)
        c = lax.broadcasted_iota(I32, (PAIR, PAIR), 1)
        upper_b = (cm["same"] & (r <= c)).astype(BF16)
        _put_heads(dg_ref, _each(lambda v: _mask_dot(upper_b, v), dgc))
        _put_heads(dbeta_ref, _each(lambda a, b, c, d: jnp.broadcast_to(_rowsum(a * b) + _rowsum(c * d), (PAIR, HEAD)),
                                    dkb, kn, dvb, vs))
        _put_heads(dqn_ref, _each(lambda v: v * QK_SCALE, dq))
        _put_heads(dkn_ref, dkn)
        _put_heads(dvs_ref, _each(lambda a, b: a * b, dvb, beta))

    return _call(
        body, name="intra_bwd", grid=(s // (INTRA_PAIRS * PAIR),),
        in_specs=[_pair_spec()] * 11 + [_chunk_scalar_spec(INTRA_PAIRS)], out_specs=[_pair_spec()] * 5,
        out_shape=[_sds((s, D_HALF))] * 5,
        compiler_params=_params("arbitrary"),
    )(qn, kn, vs, beta, g, tm, du, dw, datt, dqd, dkd, dcd)


def _rows8(x):
    acc = x[0:8]
    for r in range(8, x.shape[0], 8):
        acc = acc + x[r:r + 8]
    return acc


def _conv_bwd(proj, conv_w, a_log, dt_bias, dqn, dkn, dvs, dbeta, dg):
    s = proj.shape[0]
    t = CONV_T
    n_tiles = s // t
    n_sub = t // CONV_SUB
    tile_of = lambda i: n_tiles - 1 - i

    def body(q_ref, k_ref, v_ref, hq_ref, hk_ref, hv_ref, ba_ref, cw_ref, al_ref, dtb_ref,
             dqn_ref, dkn_ref, dvs_ref, dbeta_ref, dg_ref, oq_ref, ok_ref, ov_ref, dba_ref, gcw_out, gsm_out,
             below, gcw_ref, gsm_ref):
        @pl.when(pl.program_id(0) == 0)
        def _():
            gcw_ref[...] = jnp.zeros_like(gcw_ref)
            gsm_ref[...] = jnp.zeros_like(gsm_ref)
            below[...] = jnp.zeros_like(below)

        live = (pl.program_id(0) < n_tiles - 1).astype(F32)
        parts = ((q_ref, hq_ref, dqn_ref, oq_ref), (k_ref, hk_ref, dkn_ref, ok_ref), (v_ref, hv_ref, dvs_ref, ov_ref))
        lane = lax.broadcasted_iota(I32, (CONV_SUB, HEAD), 1)
        lane8 = lax.broadcasted_iota(I32, (8, HEAD), 1)

        def sub_tile(r0, first):
            rows = pl.ds(r0, CONV_SUB)
            for p, (x_ref, h_ref, d_ref, o_ref) in enumerate(parts):
                for h in HEADS:
                    cs = HEAD_COLS[h]
                    wide = slice(p * D_HALF + h * HEAD, p * D_HALF + (h + 1) * HEAD)
                    cw = cw_ref[:, wide]
                    prev8 = h_ref[:, cs] * live if first else x_ref[pl.ds(r0 - 8, 8), cs]
                    taps = _conv_taps(x_ref[rows, cs], prev8)
                    y = _conv_pre(taps, cw)
                    sg = _sigmoid(y)
                    sv = y * sg
                    ds = d_ref[rows, cs]
                    if p < 2:
                        rn = lax.rsqrt(_rowsum(sv * sv) + EPS)
                        nrm = sv * rn
                        ds = rn * (ds - nrm * _rowsum(ds * nrm))
                    dy = ds * (sg * (1.0 + y * (1.0 - sg)))
                    for j in range(CONV_K):
                        gcw_ref[8 * j:8 * j + 8, wide] += _rows8(dy * taps[j])
                    nxt = below[:, wide]
                    acc = dy * cw[CONV_K - 1:CONV_K]
                    for sft in range(1, CONV_K):
                        acc = acc + _shift_up(dy, nxt, sft) * cw[CONV_K - 1 - sft:CONV_K - sft]
                    o_ref[rows, cs] = acc.astype(BF16)
                    below[:, wide] = dy[0:8]

            ba = ba_ref[rows, :]
            dba = jnp.zeros((CONV_SUB, HEAD), F32)
            gsm = jnp.zeros((8, HEAD), F32)
            for h in HEADS:
                beta = _sigmoid(ba[:, h:h + 1])
                dbeta = dbeta_ref[rows, h * HEAD:h * HEAD + 1]
                xg = ba[:, N_HEADS + h:N_HEADS + h + 1] + dtb_ref[0:1, h:h + 1]
                nexp = -jnp.exp(al_ref[0:1, h:h + 1])
                dgv = dg_ref[rows, h * HEAD:h * HEAD + 1]
                da = dgv * nexp * _sigmoid(xg)
                dba = dba + jnp.where(lane == h, dbeta * beta * (1.0 - beta), 0.0) + jnp.where(lane == N_HEADS + h, da, 0.0)
                gsm = (gsm + jnp.where(lane8 == h, _rows8(dgv * nexp * _softplus(xg)), 0.0)
                       + jnp.where(lane8 == N_HEADS + h, _rows8(da), 0.0))
            dba_ref[rows, :] = jnp.zeros((CONV_SUB, D_HALF), BF16)
            dba_ref[rows, :HEAD] = dba.astype(BF16)
            gsm_ref[...] += gsm

        def step(k, carry):
            sub_tile(pl.multiple_of((n_sub - 1 - k) * CONV_SUB, CONV_SUB), False)
            return carry

        lax.fori_loop(0, n_sub - 1, step, 0)
        sub_tile(0, True)

        @pl.when(pl.program_id(0) == n_tiles - 1)
        def _():
            gcw_out[...] = jnp.zeros_like(gcw_out)
            for j in range(CONV_K):
                gcw_out[j:j + 1, :] = _colsum(gcw_ref[8 * j:8 * j + 8, :])
            gsm_out[...] = jnp.broadcast_to(_colsum(gsm_ref[...]), (8, HEAD))

    row = pl.BlockSpec((t, D_HALF), lambda i: (tile_of(i), 0))
    const = lambda shape: pl.BlockSpec(shape, lambda i: (0, 0))
    return _call(
        body, name="conv_bwd", grid=(n_tiles,),
        in_specs=_conv_specs(t, tile_of) + [pl.BlockSpec((t, HEAD), lambda i: (tile_of(i), COL_BA // HEAD)),
                                            const((CONV_K, 3 * D_HALF)), const((1, N_HEADS)), const((1, N_HEADS))] + [row] * 5,
        out_specs=[row, row, row, row, const((8, 3 * D_HALF)), const((8, HEAD))],
        out_shape=[_sds((s, D_HALF), BF16)] * 4 + [_sds((8, 3 * D_HALF)), _sds((8, HEAD))],
        scratch_shapes=[pltpu.VMEM((8, 3 * D_HALF), F32), pltpu.VMEM((8 * CONV_K, 3 * D_HALF), F32),
                        pltpu.VMEM((8, HEAD), F32)],
        compiler_params=_params("arbitrary"),
    )(proj, proj, proj, proj, proj, proj, proj, conv_w, a_log, dt_bias, dqn, dkn, dvs, dbeta, dg)


def _conv_bwd_pre(proj, conv_w, a_log, dt_bias, dqn, dkn, dvs, dbeta, dg):
    s = proj.shape[0]
    t = CONV_T

    def body(q_ref, k_ref, v_ref, hq_ref, hk_ref, hv_ref, ba_ref, cw_ref, al_ref, dtb_ref,
             dqn_ref, dkn_ref, dvs_ref, dbeta_ref, dg_ref, dyq_ref, dyk_ref, dyv_ref, dba_ref, gcw_ref, gsm_ref):
        @pl.when(pl.program_id(0) == 0)
        def _():
            gcw_ref[...] = jnp.zeros_like(gcw_ref)
            gsm_ref[...] = jnp.zeros_like(gsm_ref)

        live = (pl.program_id(0) > 0).astype(F32)
        parts = ((q_ref, hq_ref, dqn_ref, dyq_ref), (k_ref, hk_ref, dkn_ref, dyk_ref), (v_ref, hv_ref, dvs_ref, dyv_ref))
        for p, (x_ref, h_ref, d_ref, dy_ref) in enumerate(parts):
            cols = slice(p * D_HALF, (p + 1) * D_HALF)
            taps = _conv_taps(x_ref[...], h_ref[...] * live)
            y = _conv_pre(taps, cw_ref[:, cols])
            sg = _sigmoid(y)
            sv = y * sg
            if p == 2:
                ds = d_ref[...]
            else:
                segs = []
                for h in HEADS:
                    seg = _head(sv, h)
                    rn = lax.rsqrt(_rowsum(seg * seg) + EPS)
                    nrm = seg * rn
                    dn = d_ref[:, HEAD_COLS[h]]
                    segs.append(rn * (dn - nrm * _rowsum(dn * nrm)))
                ds = jnp.concatenate(segs, axis=1)
            dy = ds * (sg * (1.0 + y * (1.0 - sg)))
            dy_ref[...] = dy
            for j in range(CONV_K):
                gcw_ref[j:j + 1, cols] += _colsum(dy * taps[j])

        ba = ba_ref[...]
        lane = lax.broadcasted_iota(I32, (t, HEAD), 1)
        lane1 = lax.broadcasted_iota(I32, (1, HEAD), 1)
        dba = jnp.zeros((t, HEAD), F32)
        gsm = jnp.zeros((1, HEAD), F32)
        for h in HEADS:
            beta = _sigmoid(ba[:, h:h + 1])
            dbeta = dbeta_ref[:, h * HEAD:h * HEAD + 1]
            xg = ba[:, N_HEADS + h:N_HEADS + h + 1] + dtb_ref[0:1, h:h + 1]
            nexp = -jnp.exp(al_ref[0:1, h:h + 1])
            dgv = dg_ref[:, h * HEAD:h * HEAD + 1]
            da = dgv * nexp * _sigmoid(xg)
            dba = dba + jnp.where(lane == h, dbeta * beta * (1.0 - beta), 0.0) + jnp.where(lane == N_HEADS + h, da, 0.0)
            gsm = (gsm + jnp.where(lane1 == h, _colsum(dgv * nexp * _softplus(xg)), 0.0)
                   + jnp.where(lane1 == N_HEADS + h, _colsum(da), 0.0))
        dba_ref[...] = jnp.zeros_like(dba_ref)
        dba_ref[:, :HEAD] = dba.astype(BF16)
        gsm_ref[0:1, :] += gsm

    row = pl.BlockSpec((t, D_HALF), lambda i: (i, 0))
    return _call(
        body, name="conv_bwd_pre", grid=(s // t,),
        in_specs=_conv_specs(t) + [pl.BlockSpec((t, HEAD), lambda i: (i, COL_BA // HEAD)),
                                   pl.BlockSpec((CONV_K, 3 * D_HALF), lambda i: (0, 0)),
                                   pl.BlockSpec((1, N_HEADS), lambda i: (0, 0)),
                                   pl.BlockSpec((1, N_HEADS), lambda i: (0, 0))] + [row] * 5,
        out_specs=[row, row, row, row,
                   pl.BlockSpec((8, 3 * D_HALF), lambda i: (0, 0)), pl.BlockSpec((8, HEAD), lambda i: (0, 0))],
        out_shape=[_sds((s, D_HALF))] * 3 + [_sds((s, D_HALF), BF16), _sds((8, 3 * D_HALF)), _sds((8, HEAD))],
        compiler_params=_params("arbitrary"),
    )(proj, proj, proj, proj, proj, proj, proj, conv_w, a_log, dt_bias, dqn, dkn, dvs, dbeta, dg)


def _conv_bwd_in(dyq, dyk, dyv, conv_w):
    s = dyq.shape[0]
    t = CONV_T
    last = s // 8 - 1

    def body(q_ref, k_ref, v_ref, nq_ref, nk_ref, nv_ref, cw_ref, oq_ref, ok_ref, ov_ref):
        more = (pl.program_id(0) < pl.num_programs(0) - 1).astype(F32)
        for p, (d_ref, n_ref, o_ref) in enumerate(((q_ref, nq_ref, oq_ref), (k_ref, nk_ref, ok_ref), (v_ref, nv_ref, ov_ref))):
            cw = cw_ref[:, p * D_HALF:(p + 1) * D_HALF]
            dy = d_ref[...]
            nxt = n_ref[...] * more
            acc = dy * cw[3:4]
            for sft in (1, 2, 3):
                acc = acc + _shift_up(dy, nxt, sft) * cw[3 - sft:4 - sft]
            o_ref[...] = acc.astype(BF16)

    row = pl.BlockSpec((t, D_HALF), lambda i: (i, 0))
    nxt = pl.BlockSpec((8, D_HALF), lambda i: (jnp.minimum((i + 1) * (t // 8), last), 0))
    return _call(
        body, name="conv_bwd_in", grid=(s // t,),
        in_specs=[row] * 3 + [nxt] * 3 + [pl.BlockSpec((CONV_K, 3 * D_HALF), lambda i: (0, 0))],
        out_specs=[row] * 3, out_shape=[_sds((s, D_HALF), BF16)] * 3,
        compiler_params=_params("arbitrary"),
    )(dyq, dyk, dyv, dyq, dyk, dyv, conv_w)


IN_T = 512


def _in_bwd(x, dh, norm_w, w_pad, pieces):
    s = x.shape[0]
    t = IN_T
    widths = [D_HALF] * 6 + [N_IN_PAD - COL_BA]

    def body(*refs):
        x_ref, dh_ref, nw_ref, w_ref = refs[:4]
        p_refs = refs[4:4 + len(pieces)]
        gx_ref, gnw_ref = refs[4 + len(pieces):]

        @pl.when(pl.program_id(0) == 0)
        def _():
            gnw_ref[...] = jnp.zeros_like(gnw_ref)

        dn = jnp.zeros((t, D_MODEL), F32)
        col = 0
        for p_ref, wd in zip(p_refs, widths):
            dn = dn + _bdot_nt(p_ref[...], w_ref[:, col:col + wd])
            col += wd
        xv = x_ref[...]
        r = lax.rsqrt(jnp.mean(xv * xv, axis=-1, keepdims=True) + EPS)
        xhat = xv * r
        gnw_ref[...] += _colsum(dn * xhat)
        dxh = dn * nw_ref[...]
        gx_ref[...] = dh_ref[...] + r * (dxh - xhat * jnp.mean(dxh * xhat, axis=-1, keepdims=True))

    wide = pl.BlockSpec((t, D_MODEL), lambda i: (i, 0))
    return _call(
        body, name="in_bwd", grid=(s // t,),
        in_specs=[wide, wide, pl.BlockSpec((1, D_MODEL), lambda i: (0, 0)),
                  pl.BlockSpec((D_MODEL, N_IN_PAD), lambda i: (0, 0))]
                 + [pl.BlockSpec((t, wd), lambda i: (i, 0)) for wd in widths],
        out_specs=[wide, pl.BlockSpec((1, D_MODEL), lambda i: (0, 0))],
        out_shape=[_sds((s, D_MODEL)), _sds((1, D_MODEL))],
        compiler_params=_params("arbitrary"),
    )(x, dh, norm_w, w_pad, *pieces)


def _adamw_shard(name, w, g_own, g_got, cidx, m, v):
    _, r, c = w.shape
    half = r // 2
    rows = 256 if half % 256 == 0 else half
    per_half = half // rows

    def body(c_ref, w_ref, go_ref, gg_ref, m_ref, v_ref, gout_ref, d_ref, nm_ref, nv_ref):
        mine = (pl.program_id(0) // per_half) == c_ref[0]
        gv = jnp.where(mine, go_ref[:, :c], gg_ref[:, :c])
        gout_ref[0] = gv
        mn = ADAM_B1 * m_ref[0] + (1.0 - ADAM_B1) * gv
        vn = ADAM_B2 * v_ref[0] + (1.0 - ADAM_B2) * (gv * gv)
        m_hat = mn / (1.0 - ADAM_B1 ** ADAM_STEP)
        v_hat = vn / (1.0 - ADAM_B2 ** ADAM_STEP)
        d_ref[0] = -ADAM_LR * (m_hat / (jnp.sqrt(v_hat) + ADAM_EPS) + ADAM_WD * w_ref[0])
        nm_ref[0] = mn
        nv_ref[0] = vn

    blk = pl.BlockSpec((1, rows, c), lambda i, c_ref: (0, i, 0))
    gblk = pl.BlockSpec((rows, g_own.shape[1]), lambda i, c_ref: (i % per_half, 0))
    return _call(
        body, name=name,
        grid_spec=pltpu.PrefetchScalarGridSpec(
            num_scalar_prefetch=1, grid=(2 * per_half,),
            in_specs=[blk, gblk, gblk, blk, blk], out_specs=[blk] * 4),
        out_shape=[_sds((1, r, c))] * 4,
        compiler_params=_params("arbitrary"),
    )(cidx, w, g_own, g_got, m, v)


def _adamw_tiles(name, w, g, m, v):
    n = w.shape[0]
    nb = 77 if n % 77 == 0 else n

    def body(w_ref, g_ref, m_ref, v_ref, d_ref, nm_ref, nv_ref):
        gv = g_ref[...]
        mn = ADAM_B1 * m_ref[...] + (1.0 - ADAM_B1) * gv
        vn = ADAM_B2 * v_ref[...] + (1.0 - ADAM_B2) * (gv * gv)
        m_hat = mn / (1.0 - ADAM_B1 ** ADAM_STEP)
        v_hat = vn / (1.0 - ADAM_B2 ** ADAM_STEP)
        d_ref[...] = -ADAM_LR * (m_hat / (jnp.sqrt(v_hat) + ADAM_EPS) + ADAM_WD * w_ref[...])
        nm_ref[...] = mn
        nv_ref[...] = vn

    blk = pl.BlockSpec((nb, 8, HEAD), lambda i: (i, 0, 0))
    return _call(
        body, name=name, grid=(n // nb,),
        in_specs=[blk] * 4, out_specs=[blk] * 3, out_shape=[_sds(w.shape)] * 3,
        compiler_params=_params("arbitrary"),
    )(w, g, m, v)


def _exchange(name, inputs, out_shapes, phases):
    n_in = len(inputs)
    n_out = len(out_shapes)
    n_cp = sum(len(p) for p in phases)

    def body(*refs):
        ins, outs = refs[:n_in], refs[n_in:n_in + n_out]
        send, recv = refs[n_in + n_out:]
        pos = (lax.axis_index("x"), lax.axis_index("y"), lax.axis_index("c"))
        k = 0
        for phase in phases:
            cps = []
            for src, dst, target in phase:
                cps.append(pltpu.make_async_remote_copy(
                    src_ref=src(ins, outs, pos), dst_ref=dst(ins, outs, pos), send_sem=send.at[k], recv_sem=recv.at[k],
                    device_id=target(pos), device_id_type=pl.DeviceIdType.MESH))
                k += 1
            for cp in cps:
                cp.start()
            for cp in cps:
                cp.wait()

    anyspec = pl.BlockSpec(memory_space=pl.ANY)
    return _call(
        body, name=name,
        in_specs=[anyspec] * n_in, out_specs=[anyspec] * n_out, out_shape=list(out_shapes),
        scratch_shapes=[pltpu.SemaphoreType.DMA((n_cp,)), pltpu.SemaphoreType.DMA((n_cp,))],
    )(*inputs)


def _chip(pos):
    return 2 * pos[0] + pos[1]


def _other_chip(pos, mask):
    x, y, c = pos
    return (x ^ (mask >> 1), y ^ (mask & 1), c)


def _sibling(pos):
    return (pos[0], pos[1], 1 - pos[2])


def _gather_weights(wb, ob, cb):
    halves = (wb.shape[0] // 2, ob.shape[0] // 2)

    def half(a, pos):
        return pl.ds(pos[2] * halves[a], halves[a])

    first, second = [], []
    for mask in CHIP_MASKS:
        for a in (0, 1):
            first.append((lambda ins, outs, pos, a=a: ins[a].at[half(a, pos)],
                          lambda ins, outs, pos, a=a: outs[a].at[_chip(pos), half(a, pos)],
                          functools.partial(_other_chip, mask=mask)))
            second.append((lambda ins, outs, pos, a=a, mask=mask: outs[a].at[_chip(pos) ^ mask, half(a, pos)],
                           lambda ins, outs, pos, a=a, mask=mask: outs[a].at[_chip(pos) ^ mask, half(a, pos)],
                           _sibling))
        first.append((lambda ins, outs, pos: ins[2],
                      lambda ins, outs, pos: outs[2].at[_chip(pos)],
                      functools.partial(_other_chip, mask=mask)))
    return _exchange("gather_weights", [wb, ob, cb],
                     [_sds((4,) + wb.shape, wb.dtype), _sds((4,) + ob.shape, ob.dtype), _sds((4,) + cb.shape, cb.dtype)],
                     [first, second])


def _to_sibling_half(name, arrays):
    def src(ins, outs, pos, a):
        h = arrays[a].shape[-2] // 2
        sl = pl.ds((1 - pos[2]) * h, h)
        return ins[a].at[:, sl] if arrays[a].ndim == 3 else ins[a].at[sl]

    outs = [_sds(a.shape[:-2] + (a.shape[-2] // 2, a.shape[-1]), a.dtype) for a in arrays]
    phase = [(functools.partial(src, a=a), lambda ins, outs, pos, a=a: outs[a], _sibling) for a in range(len(arrays))]
    return _exchange(name, arrays, outs, [phase])


def _add_half(name, full, part, cidx):
    shape = part.shape
    lead = shape[0] if len(shape) == 3 else 1
    rows, cols = shape[-2], shape[-1]
    tr = rows // 2 if rows % 16 == 0 else rows
    nr = rows // tr
    f3 = full.reshape((lead,) + full.shape[-2:])
    p3 = part.reshape((lead, rows, cols))

    def body(c_ref, f_ref, p_ref, o_ref):
        o_ref[...] = (f_ref[...].astype(F32) + p_ref[...].astype(F32)).astype(o_ref.dtype)

    out = _call(
        body, name=name,
        grid_spec=pltpu.PrefetchScalarGridSpec(
            num_scalar_prefetch=1, grid=(lead, nr),
            in_specs=[pl.BlockSpec((1, tr, cols), lambda b, r, c_ref: (b, c_ref[0] * nr + r, 0)),
                      pl.BlockSpec((1, tr, cols), lambda b, r, c_ref: (b, r, 0))],
            out_specs=pl.BlockSpec((1, tr, cols), lambda b, r, c_ref: (b, r, 0))),
        out_shape=_sds((lead, rows, cols), part.dtype),
        compiler_params=_params("arbitrary", "arbitrary"),
    )(cidx, f3, p3)
    return out.reshape(shape)


def _to_other_chips(name, arrays, blocked):
    def src(ins, outs, pos, a, mask):
        return ins[a].at[_chip(pos) ^ mask] if blocked[a] else ins[a]

    outs = [_sds((3,) + (a.shape[1:] if b else a.shape), a.dtype) for a, b in zip(arrays, blocked)]
    phase = []
    for mi, mask in enumerate(CHIP_MASKS):
        for a in range(len(arrays)):
            phase.append((functools.partial(src, a=a, mask=mask), lambda ins, outs, pos, a=a, mi=mi: outs[a].at[mi],
                          functools.partial(_other_chip, mask=mask)))
    return _exchange(name, arrays, outs, [phase])


def _add_chips(name, own, got, jidx, blocked):
    rows, cols = got.shape[-2:]
    tr = rows // 2 if rows % 16 == 0 else rows
    nr = rows // tr
    o3 = own if blocked else own.reshape((1, rows, cols))

    def body(j_ref, o_ref, g_ref, out_ref):
        out_ref[...] = ((o_ref[0].astype(F32) + g_ref[0].astype(F32))
                        + (g_ref[1].astype(F32) + g_ref[2].astype(F32)))

    own_map = (lambda r, j_ref: (j_ref[0], r, 0)) if blocked else (lambda r, j_ref: (0, r, 0))
    return _call(
        body, name=name,
        grid_spec=pltpu.PrefetchScalarGridSpec(
            num_scalar_prefetch=1, grid=(nr,),
            in_specs=[pl.BlockSpec((1, tr, cols), own_map),
                      pl.BlockSpec((3, tr, cols), lambda r, j_ref: (0, r, 0))],
            out_specs=pl.BlockSpec((tr, cols), lambda r, j_ref: (r, 0))),
        out_shape=_sds((rows, cols)),
        compiler_params=_params("arbitrary"),
    )(jidx, o3, got)


def _to_sibling(name, arrays):
    phase = [(lambda ins, outs, pos, a=a: ins[a], lambda ins, outs, pos, a=a: outs[a], _sibling)
             for a in range(len(arrays))]
    return _exchange(name, arrays, [_sds(a.shape, a.dtype) for a in arrays], [phase])


def _local_step(x, target, w_pad, w_out, conv_w, norm_w, pool_w, pool_scale, a_log, dt_bias, dn_norm_w, final_norm_w):
    proj, n_t = _proj_fwd(x, norm_w, w_pad)
    y_pool = _pool_fwd(proj, pool_w, pool_scale)
    qn, kn, vs, beta, g = _conv_fwd(proj, conv_w, a_log, dt_bias)
    u, w, att, qd, kd, tm, cd = _intra_fwd(qn, kn, vs, beta, g)
    o, vn, st = _scan_fwd(u, w, att, qd, kd, cd)
    y_t, dh, dyp, do, ddz, loss, g_fnw, g_dnw = _out_fwd_bwd(x, y_pool, o, proj, target, w_out, dn_norm_w, final_norm_w)
    g_wout = _token_matmul("grad_w_out", y_t, [(dh, 0), (dh, 1)])
    dpu, dpz, g_pw, g_ps = _pool_bwd(proj, dyp, pool_w, pool_scale)
    du, dw, datt, dqd, dkd, dcd = _scan_bwd(do, vn, qd, kd, w, att, cd, st)
    dqn, dkn, dvs, dbeta, dg = _intra_bwd(qn, kn, vs, beta, g, tm, du, dw, datt, dqd, dkd, dcd)
    dcq, dck, dcv, dba, g_cw, g_sm = _conv_bwd(proj, conv_w, a_log, dt_bias, dqn, dkn, dvs, dbeta, dg)
    pieces = [dpu, dpz, dcq, dck, dcv, ddz, dba]
    gx, g_nw = _in_bwd(x, dh, norm_w, w_pad, pieces)
    g_win = _token_matmul("grad_w_in", n_t, [(p, 0) for p in pieces])
    small = dict(norm_w=g_nw, pool_w=g_pw, pool_scale=g_ps, conv_w=g_cw[:CONV_K], a_log=g_sm[0:1, 0:N_HEADS],
                 dt_bias=g_sm[0:1, N_HEADS:2 * N_HEADS], dn_norm_w=g_dnw, final_norm_w=g_fnw)
    return loss[0, 0], gx, g_win, g_wout, small


def _pack_small(t):
    lanes = lambda a: jnp.pad(a.reshape(1, -1), ((0, 0), (0, HEAD - a.size)))
    rows = [t["pool_w"].reshape(-1, HEAD), t["norm_w"].reshape(-1, HEAD), t["final_norm_w"].reshape(-1, HEAD),
            t["pool_scale"].reshape(-1, HEAD), t["conv_w"].reshape(-1, HEAD), t["dn_norm_w"].reshape(1, HEAD),
            lanes(t["a_log"]), lanes(t["dt_bias"]), lanes(t.get("loss", jnp.zeros((1,), F32)))]
    buf = jnp.concatenate(rows, axis=0)
    return jnp.pad(buf, ((0, SMALL_ROWS - buf.shape[0]), (0, 0)))


def _unpack_small(buf, conv_cols):
    out, r = {}, 0
    for name, nrow, shape in (("pool_w", 512, (1, N_HEADS, HEAD, HEAD)), ("norm_w", 8, (1, D_MODEL)),
                              ("final_norm_w", 8, (D_MODEL,)), ("pool_scale", 4, (1, D_HALF)),
                              ("conv_w", CONV_K * conv_cols // HEAD, (1, CONV_K, conv_cols)), ("dn_norm_w", 1, (1, HEAD))):
        out[name] = buf[r:r + nrow].reshape(shape)
        r += nrow
    out["a_log"] = buf[r:r + 1, :N_HEADS]
    out["dt_bias"] = buf[r + 1:r + 2, :N_HEADS]
    out["loss"] = buf[r + 2, 0]
    return out


def kernel(x, norm_w, w_in, pool_w, pool_scale, conv_w, a_log, dt_bias, dn_norm_w, w_out, final_norm_w, loss_target, m_norm_w, m_w_in, m_pool_w, m_pool_scale, m_conv_w, m_a_log, m_dt_bias, m_dn_norm_w, m_w_out, m_final_norm_w, v_norm_w, v_w_in, v_pool_w, v_pool_scale, v_conv_w, v_a_log, v_dt_bias, v_dn_norm_w, v_w_out, v_final_norm_w):
    cidx = lax.axis_index("c").astype(I32).reshape(1)
    jidx = (2 * lax.axis_index("x") + lax.axis_index("y")).astype(I32)

    wb = jnp.pad(w_in[0].astype(BF16), ((0, 0), (0, BLK_IN_PAD - BLK_IN)))
    ob = w_out[0].astype(BF16)
    gw, go, gc = _gather_weights(wb, ob, conv_w[0])
    mine = lambda j: jidx == j
    w_pad = jnp.concatenate([jnp.where(mine(j), wb[:, :BLK_IN], gw[j, :, :BLK_IN]) for j in range(4)]
                            + [jnp.zeros((D_MODEL, N_IN_PAD - N_IN), BF16)], axis=1)
    wo_full = jnp.where((jnp.arange(4) == jidx)[:, None, None], ob[None], go).reshape(D_MODEL, D_MODEL)
    cw_full = jnp.concatenate([jnp.where(mine(j), conv_w[0], gc[j]) for j in range(4)], axis=1)

    loss, gx, g_win, g_wout, small = _local_step(
        x[0], loss_target[0], w_pad, wo_full, cw_full, norm_w, pool_w[0], pool_scale, a_log, dt_bias, dn_norm_w,
        final_norm_w.reshape(1, D_MODEL))
    small["loss"] = loss

    blocks_in = jnp.stack([jnp.pad(g_win[:, j * BLK_IN:(j + 1) * BLK_IN].astype(BF16), ((0, 0), (0, BLK_IN_PAD - BLK_IN)))
                           for j in range(4)])
    blocks_out = g_wout.astype(BF16).reshape(4, BLK_OUT, D_MODEL)
    full = [blocks_in, blocks_out, _pack_small(small)]
    from_sib = _to_sibling_half("reduce_sibling", full)
    chip_sum = [_add_half("add_sibling_%d" % i, f, p, cidx) for i, (f, p) in enumerate(zip(full, from_sib))]
    blocked = [True, True, False]
    from_chips = _to_other_chips("reduce_chips", chip_sum, blocked)
    halves = [_add_chips("add_chips_%d" % i, o, g, jidx.reshape(1), b)
              for i, (o, g, b) in enumerate(zip(chip_sum, from_chips, blocked))]
    other_halves = _to_sibling("swap_halves", halves)

    weights = dict(norm_w=norm_w, w_in=w_in, pool_w=pool_w, pool_scale=pool_scale, conv_w=conv_w, a_log=a_log,
                   dt_bias=dt_bias, dn_norm_w=dn_norm_w, w_out=w_out, final_norm_w=final_norm_w)
    ms = dict(norm_w=m_norm_w, w_in=m_w_in, pool_w=m_pool_w, pool_scale=m_pool_scale, conv_w=m_conv_w, a_log=m_a_log,
              dt_bias=m_dt_bias, dn_norm_w=m_dn_norm_w, w_out=m_w_out, final_norm_w=m_final_norm_w)
    vs = dict(norm_w=v_norm_w, w_in=v_w_in, pool_w=v_pool_w, pool_scale=v_pool_scale, conv_w=v_conv_w, a_log=v_a_log,
              dt_bias=v_dt_bias, dn_norm_w=v_dn_norm_w, w_out=v_w_out, final_norm_w=v_final_norm_w)
    names = ["norm_w", "w_in", "pool_w", "pool_scale", "conv_w", "a_log", "dt_bias", "dn_norm_w", "w_out", "final_norm_w"]
    small_names = [n for n in names if n not in ("w_in", "w_out")]

    def pack(t):
        conv = lax.dynamic_update_slice_in_dim(jnp.zeros((CONV_K, 3 * D_HALF), F32), t["conv_w"][0], jidx * BLK_CONV, axis=1)
        return _pack_small({**{n: t[n] for n in small_names if n != "conv_w"}, "conv_w": conv})[None]

    results = [{}, {}, {}, {}]
    to_tiles = lambda a: jnp.transpose(a, (2, 0, 1)).reshape(BLK_IN, 8, HEAD)
    from_tiles = lambda a: jnp.transpose(a, (1, 2, 0)).reshape(1, D_MODEL, BLK_IN)
    lo = jnp.where(cidx[0] == 0, halves[0], other_halves[0])
    hi = jnp.where(cidx[0] == 0, other_halves[0], halves[0])
    g_tiles = jnp.concatenate([lo[:, :BLK_IN].T, hi[:, :BLK_IN].T], axis=1).reshape(BLK_IN, 8, HEAD)
    outs = _adamw_tiles("adamw_w_in", to_tiles(w_in), g_tiles, to_tiles(m_w_in), to_tiles(v_w_in))
    for res, o in zip(results, (g_tiles,) + tuple(outs)):
        res["w_in"] = from_tiles(o)
    outs = _adamw_shard("adamw_w_out", w_out, halves[1], other_halves[1], cidx, m_w_out, v_w_out)
    for res, o in zip(results, outs):
        res["w_out"] = o
    outs = _adamw_shard("adamw_small", pack(weights), halves[2], other_halves[2], cidx, pack(ms), pack(vs))
    for res, o in zip(results, outs):
        got = _unpack_small(o[0], 3 * D_HALF)
        got["conv_w"] = lax.dynamic_slice_in_dim(got["conv_w"], jidx * BLK_CONV, BLK_CONV, axis=2)
        res.update(got)
    grads, delta, new_m, new_v = results

    return (grads["loss"], gx[None], *[grads[n] for n in names], *[delta[n] for n in names],
            *[new_m[n] for n in names], *[new_v[n] for n in names])
```

```python
import functools

import jax
import jax.numpy as jnp
from jax import lax
from jax.experimental import pallas as pl
from jax.experimental.pallas import tpu as pltpu

F32 = jnp.float32
BF16 = jnp.bfloat16
I32 = jnp.int32

D_MODEL = 1024
D_HALF = 512
N_HEADS = 4
HEAD = 128
CHUNK = 64
PAIR = 2 * CHUNK
WINDOWS = (2, 4, 8, 16)
CONV_K = 4
EPS = 1e-6
N_IN = 3080
N_IN_PAD = 3200
BLK_IN = 770
BLK_IN_PAD = 896
BLK_OUT = 256
BLK_CONV = 384
COL_BA = 3072
QK_SCALE = HEAD ** -0.5
SMALL_ROWS = 592
VMEM_LIMIT = 56 * 1024 * 1024

ADAM_LR = 0.001
ADAM_B1 = 0.9
ADAM_B2 = 0.999
ADAM_EPS = 1e-08
ADAM_WD = 0.01
ADAM_STEP = 10

CHIP_MASKS = (2, 1, 3)
HEADS = range(N_HEADS)
HEAD_COLS = [slice(h * HEAD, (h + 1) * HEAD) for h in HEADS]


def _call(body, **kw):
    return pl.pallas_call(body, **kw)


def _params(*sem):
    return pltpu.CompilerParams(dimension_semantics=sem, vmem_limit_bytes=VMEM_LIMIT)


def _sds(shape, dtype=F32):
    return jax.ShapeDtypeStruct(shape, dtype)


def _bdot(a, b):
    return jnp.dot(a.astype(BF16), b.astype(BF16), preferred_element_type=F32)


def _bdot_nt(a, b):
    return lax.dot_general(a.astype(BF16), b.astype(BF16), (((1,), (1,)), ((), ())), preferred_element_type=F32)


def _bdot_tn(a, b):
    return lax.dot_general(a.astype(BF16), b.astype(BF16), (((0,), (0,)), ((), ())), preferred_element_type=F32)


def _split(a):
    hi = a.astype(BF16)
    lo = (a - hi.astype(F32)).astype(BF16)
    return hi, lo


def _mask_dot(m, b, dims=(((1,), (0,)), ((), ()))):
    bh, bl = _split(b)
    dg = functools.partial(lax.dot_general, dimension_numbers=dims, preferred_element_type=F32)
    return dg(m, bh) + dg(m, bl)


def _sigmoid(x):
    return 0.5 * jnp.tanh(0.5 * x) + 0.5


def _softplus(x):
    return jnp.maximum(x, 0.0) + jnp.log(1.0 + jnp.exp(-jnp.abs(x)))


def _rowsum(x):
    return jnp.sum(x, axis=-1, keepdims=True)


def _colsum(x):
    return jnp.sum(x, axis=0, keepdims=True)


def _shift_down(xv, prev8, k):
    r = pltpu.roll(xv, k, 0)
    q = pltpu.roll(prev8, k, 0)
    row = lax.broadcasted_iota(I32, prev8.shape, 0)
    top = jnp.where(row < k, q, r[0:8])
    return jnp.concatenate([top, r[8:]], axis=0)


def _shift_up(xv, next8, k):
    t = xv.shape[0]
    r = pltpu.roll(xv, t - k, 0)
    q = pltpu.roll(next8, 8 - k, 0)
    row = lax.broadcasted_iota(I32, next8.shape, 0)
    bot = jnp.where(row >= 8 - k, q, r[t - 8:])
    return jnp.concatenate([r[:t - 8], bot], axis=0)


def _band(rows, cols, off, w, anti=False):
    r = lax.broadcasted_iota(I32, (rows, cols), 0)
    c = lax.broadcasted_iota(I32, (rows, cols), 1)
    d = (c - r + off) if anti else (r - c + off)
    return ((d >= 0) & (d < w)).astype(BF16)


def _head(ref_or_val, h):
    return ref_or_val[:, h * HEAD:(h + 1) * HEAD]


INTRA_PAIRS = 2
UNITS = [(pp, h) for pp in range(INTRA_PAIRS) for h in HEADS]


def _heads(ref, rows=PAIR):
    return [ref[pp * rows:(pp + 1) * rows, HEAD_COLS[h]] for pp, h in UNITS]


def _put_heads(ref, vals, rows=PAIR):
    for (pp, h), v in zip(UNITS, vals):
        ref[pp * rows:(pp + 1) * rows, HEAD_COLS[h]] = v.astype(ref.dtype)


def _each(fn, *lists):
    return [fn(*args) for args in zip(*lists)]


def _proj_fwd(x, norm_w, w_pad):
    s = x.shape[0]
    tm = 512

    def body(x_ref, nw_ref, w_ref, proj_ref, nt_ref):
        xv = x_ref[...]
        r = lax.rsqrt(jnp.mean(xv * xv, axis=-1, keepdims=True) + EPS)
        nv = xv * r * nw_ref[...]
        nt_ref[...] = nv.T.astype(BF16)
        proj_ref[...] = jnp.dot(nv.astype(BF16), w_ref[...], preferred_element_type=F32)

    return _call(
        body, name="proj_fwd", grid=(s // tm,),
        in_specs=[pl.BlockSpec((tm, D_MODEL), lambda i: (i, 0)),
                  pl.BlockSpec((1, D_MODEL), lambda i: (0, 0)),
                  pl.BlockSpec((D_MODEL, N_IN_PAD), lambda i: (0, 0))],
        out_specs=[pl.BlockSpec((tm, N_IN_PAD), lambda i: (i, 0)),
                   pl.BlockSpec((D_MODEL, tm), lambda i: (0, i))],
        out_shape=[_sds((s, N_IN_PAD)), _sds((D_MODEL, s), BF16)],
        compiler_params=_params("arbitrary"),
    )(x, norm_w, w_pad)


def _pool_mix(ug, hg, zg, pw_g, row0, w):
    t = ug.shape[0]
    win = _mask_dot(_band(t, t, 0, w), ug) + _mask_dot(_band(t, HEAD, HEAD, w), hg)
    cnt = jnp.minimum(row0 + lax.broadcasted_iota(I32, (t, 1), 0) + 1, w).astype(F32)
    mix = win / cnt - ug
    mixed = _bdot(mix, pw_g)
    sg = _sigmoid(zg)
    return mix, mixed, sg, cnt


POOL_T = 256


def _pool_fwd(proj, pool_w, pool_scale):
    s = proj.shape[0]
    t = POOL_T
    hb = t // HEAD

    def body(u_ref, z_ref, halo_ref, pw_ref, ps_ref, y_ref):
        i = pl.program_id(0)
        live = (i > 0).astype(F32)
        for g, w in enumerate(WINDOWS):
            sl = HEAD_COLS[g]
            zg = z_ref[:, sl]
            _, mixed, sg, _ = _pool_mix(u_ref[:, sl], halo_ref[:, sl] * live, zg, pw_ref[g], i * t, w)
            y_ref[:, sl] = mixed * ps_ref[:, sl] * (zg * sg)

    return _call(
        body, name="pool_fwd", grid=(s // t,),
        in_specs=[pl.BlockSpec((t, D_HALF), lambda i: (i, 0)),
                  pl.BlockSpec((t, D_HALF), lambda i: (i, 1)),
                  pl.BlockSpec((HEAD, D_HALF), lambda i: (jnp.maximum(i * hb - 1, 0), 0)),
                  pl.BlockSpec((N_HEADS, HEAD, HEAD), lambda i: (0, 0, 0)),
                  pl.BlockSpec((1, D_HALF), lambda i: (0, 0))],
        out_specs=pl.BlockSpec((t, D_HALF), lambda i: (i, 0)),
        out_shape=_sds((s, D_HALF)),
        compiler_params=_params("arbitrary"),
    )(proj, proj, proj, pool_w, pool_scale)


def _conv_taps(xv, prev8):
    return [_shift_down(xv, prev8, CONV_K - 1 - j) for j in range(CONV_K - 1)] + [xv]


def _conv_pre(taps, cw):
    y = taps[CONV_K - 1] * cw[CONV_K - 1:CONV_K]
    for j in range(CONV_K - 2, -1, -1):
        y = y + taps[j] * cw[j:j + 1]
    return y


CONV_T = 256
CONV_SUB = 256


def _conv_specs(t, tile_of=lambda i: i):
    tiles = [pl.BlockSpec((t, D_HALF), functools.partial(lambda i, p: (tile_of(i), 2 + p), p=p)) for p in range(3)]
    halos = [pl.BlockSpec((8, D_HALF),
                          functools.partial(lambda i, p: (jnp.maximum(tile_of(i) * (t // 8) - 1, 0), 2 + p), p=p))
             for p in range(3)]
    return tiles + halos


def _conv_fwd(proj, conv_w, a_log, dt_bias):
    s = proj.shape[0]
    t = CONV_T

    def body(q_ref, k_ref, v_ref, hq_ref, hk_ref, hv_ref, ba_ref, cw_ref, al_ref, dtb_ref,
             qn_ref, kn_ref, vs_ref, beta_ref, g_ref):
        live = (pl.program_id(0) > 0).astype(F32)
        parts = ((q_ref, hq_ref, qn_ref), (k_ref, hk_ref, kn_ref), (v_ref, hv_ref, vs_ref))

        def sub_tile(r0, first):
            rows = pl.ds(r0, CONV_SUB)
            for p, (x_ref, h_ref, o_ref) in enumerate(parts):
                for h in HEADS:
                    cs = HEAD_COLS[h]
                    prev8 = h_ref[:, cs] * live if first else x_ref[pl.ds(r0 - 8, 8), cs]
                    y = _conv_pre(_conv_taps(x_ref[rows, cs], prev8), cw_ref[:, p * D_HALF + h * HEAD:p * D_HALF + (h + 1) * HEAD])
                    sv = y * _sigmoid(y)
                    o_ref[rows, cs] = sv if p == 2 else sv * lax.rsqrt(_rowsum(sv * sv) + EPS)
            ba = ba_ref[rows, :]
            for h in HEADS:
                beta = _sigmoid(ba[:, h:h + 1])
                gl = -jnp.exp(al_ref[0:1, h:h + 1]) * _softplus(ba[:, N_HEADS + h:N_HEADS + h + 1] + dtb_ref[0:1, h:h + 1])
                beta_ref[rows, HEAD_COLS[h]] = jnp.broadcast_to(beta, (CONV_SUB, HEAD))
                g_ref[rows, HEAD_COLS[h]] = jnp.broadcast_to(gl, (CONV_SUB, HEAD))

        sub_tile(0, True)

        def step(k, carry):
            sub_tile(pl.multiple_of(k * CONV_SUB, CONV_SUB), False)
            return carry

        lax.fori_loop(1, t // CONV_SUB, step, 0)

    row = pl.BlockSpec((t, D_HALF), lambda i: (i, 0))
    return _call(
        body, name="conv_fwd", grid=(s // t,),
        in_specs=_conv_specs(t) + [pl.BlockSpec((t, HEAD), lambda i: (i, COL_BA // HEAD)),
                                   pl.BlockSpec((CONV_K, 3 * D_HALF), lambda i: (0, 0)),
                                   pl.BlockSpec((1, N_HEADS), lambda i: (0, 0)),
                                   pl.BlockSpec((1, N_HEADS), lambda i: (0, 0))],
        out_specs=[row] * 5,
        out_shape=[_sds((s, D_HALF))] * 5,
        compiler_params=_params("arbitrary"),
    )(proj, proj, proj, proj, proj, proj, proj, conv_w, a_log, dt_bias)


def _pair_masks():
    r = lax.broadcasted_iota(I32, (PAIR, PAIR), 0)
    c = lax.broadcasted_iota(I32, (PAIR, PAIR), 1)
    same = jnp.right_shift(r, 6) == jnp.right_shift(c, 6)
    return same, same & (r >= c), same & (r > c), r == c


def _pair_common(qn, kn, vs, beta, g):
    same, incl, strict, eye = _pair_masks()
    incl_b = incl.astype(BF16)
    first = lax.broadcasted_iota(I32, (PAIR, HEAD), 0) < CHUNK
    gc = _each(lambda gv: _mask_dot(incl_b, gv), g)
    gc_row = _each(lambda v: _colsum(jnp.where(eye, v, 0.0)), gc)
    decay = _each(lambda v, r: jnp.where(incl, jnp.exp(jnp.where(incl, v - r, 0.0)), 0.0), gc, gc_row)
    gl = _each(lambda v: jnp.where(first, v[CHUNK - 1:CHUNK], v[PAIR - 1:PAIR]), gc)
    egc = _each(jnp.exp, gc)
    q = _each(lambda v: v * QK_SCALE, qn)
    kb = _each(lambda k, b: k * b, kn, beta)
    return dict(same=same, incl=incl, strict=strict, eye=eye, gc=gc, decay=decay, gl=gl, egc=egc,
                ekd=_each(lambda a, b: jnp.exp(a - b), gl, gc), cd=_each(jnp.exp, gl), q=q, kb=kb,
                vb=_each(lambda v, b: v * b, vs, beta), kbg=_each(lambda k, e: k * e, kb, egc),
                kk=_each(_bdot_nt, kb, kn), qk=_each(_bdot_nt, q, kn))


def _tri_inv(a, eye_f):
    p = _each(lambda v: eye_f - v, a)
    x = _each(_bdot, a, a)
    for it in range(5):
        p = _each(lambda pv, xv: pv + _bdot(pv, xv), p, x)
        if it < 4:
            x = _each(_bdot, x, x)
    return p


def _pair_spec():
    return pl.BlockSpec((INTRA_PAIRS * PAIR, D_HALF), lambda i: (i, 0))


def _chunk_scalar_spec(pairs=1, index=lambda i: (i, 0)):
    return pl.BlockSpec((16 * pairs, D_HALF), index)


SCAN_PAIRS = 2
SCAN_ROWS = SCAN_PAIRS * PAIR


def _intra_fwd(qn, kn, vs, beta, g):
    s = qn.shape[0]

    def body(qn_ref, kn_ref, vs_ref, beta_ref, g_ref, u_ref, w_ref, att_ref, qd_ref, kd_ref, t_ref, cd_ref):
        kn = _heads(kn_ref)
        cm = _pair_common(_heads(qn_ref), kn, _heads(vs_ref), _heads(beta_ref), _heads(g_ref))
        a = _each(lambda kk, d: jnp.where(cm["strict"], kk * d, 0.0), cm["kk"], cm["decay"])
        tm = _tri_inv(a, cm["eye"].astype(F32))
        _put_heads(t_ref, tm)
        _put_heads(u_ref, _each(_bdot, tm, cm["vb"]))
        _put_heads(w_ref, _each(_bdot, tm, cm["kbg"]))
        _put_heads(att_ref, _each(lambda a, b: a * b, cm["qk"], cm["decay"]))
        _put_heads(qd_ref, _each(lambda a, b: a * b, cm["q"], cm["egc"]))
        _put_heads(kd_ref, _each(lambda a, b: a * b, kn, cm["ekd"]))
        for ci in range(2):
            for (pp, h), v in zip(UNITS, cm["cd"]):
                cd_ref[pp * 16 + ci * 8:pp * 16 + (ci + 1) * 8, HEAD_COLS[h]] = v[ci * CHUNK:ci * CHUNK + 8]

    return _call(
        body, name="intra_fwd", grid=(s // (INTRA_PAIRS * PAIR),),
        in_specs=[_pair_spec()] * 5, out_specs=[_pair_spec()] * 6 + [_chunk_scalar_spec(INTRA_PAIRS)],
        out_shape=[_sds((s, D_HALF))] + [_sds((s, D_HALF), BF16)] * 5 + [_sds((s // 8, D_HALF))],
        compiler_params=_params("arbitrary"),
    )(qn, kn, vs, beta, g)


def _scan_fwd(u, w, att, qd, kd, cd):
    s = u.shape[0]
    n_chunks = s // CHUNK

    def body(u_ref, w_ref, att_ref, qd_ref, kd_ref, cd_ref, o_ref, vn_ref, st_ref, state):
        @pl.when(pl.program_id(0) == 0)
        def _():
            state[...] = jnp.zeros_like(state)
        cols = list(enumerate(HEAD_COLS))
        sm = [state[h] for h in HEADS]
        for ci in range(2 * SCAN_PAIRS):
            rs = slice(ci * CHUNK, (ci + 1) * CHUNK)
            for h in HEADS:
                st_ref[ci, h] = sm[h]
            both = [_bdot(jnp.concatenate([w_ref[rs, sl], qd_ref[rs, sl]], axis=0), sm[h]) for h, sl in cols]
            vn = [u_ref[rs, sl] - both[h][:CHUNK] for h, sl in cols]
            for h, sl in cols:
                vn_ref[rs, sl] = vn[h].astype(BF16)
                o_ref[rs, sl] = both[h][CHUNK:]
            sm = [sm[h] * cd_ref[ci * 8:ci * 8 + 1, sl] + _bdot_tn(kd_ref[rs, sl], vn[h]) for h, sl in cols]
        for h in HEADS:
            state[h] = sm[h]
        for pp in range(SCAN_PAIRS):
            rp = slice(pp * PAIR, (pp + 1) * PAIR)
            intra = [_bdot(att_ref[rp, sl], vn_ref[rp, sl]) for sl in HEAD_COLS]
            for h, sl in cols:
                o_ref[rp, sl] += intra[h]

    rows = pl.BlockSpec((SCAN_ROWS, D_HALF), lambda i: (i, 0))
    return _call(
        body, name="scan_fwd", grid=(s // SCAN_ROWS,),
        in_specs=[rows] * 5 + [_chunk_scalar_spec(SCAN_PAIRS)],
        out_specs=[rows, rows, pl.BlockSpec((2 * SCAN_PAIRS, N_HEADS, HEAD, HEAD), lambda i: (i, 0, 0, 0))],
        out_shape=[_sds((s, D_HALF)), _sds((s, D_HALF), BF16), _sds((n_chunks, N_HEADS, HEAD, HEAD))],
        scratch_shapes=[pltpu.VMEM((N_HEADS, HEAD, HEAD), F32)],
        compiler_params=_params("arbitrary"),
    )(u, w, att, qd, kd, cd)


OUT_T = 512


def _out_fwd_bwd(x, y_pool, o, proj, target, w_out, dn_norm_w, final_norm_w):
    s = x.shape[0]
    t = OUT_T

    def body(x_ref, yp_ref, o_ref, z_ref, tg_ref, wo_ref, dnw_ref, fnw_ref,
             yt_ref, dh_ref, dyp_ref, do_ref, dz_ref, loss_ref, gfn_ref, gdn_ref, y_ref):
        @pl.when(pl.program_id(0) == 0)
        def _():
            loss_ref[...] = jnp.zeros_like(loss_ref)
            gfn_ref[...] = jnp.zeros_like(gfn_ref)
            gdn_ref[...] = jnp.zeros_like(gdn_ref)

        ypv = yp_ref[...]
        y_ref[:, :D_HALF] = ypv.astype(BF16)
        yt_ref[:D_HALF, :] = ypv.T.astype(BF16)
        dnw = dnw_ref[...]
        keep = []
        for h in HEADS:
            ov = o_ref[:, HEAD_COLS[h]]
            zv = z_ref[:, HEAD_COLS[h]]
            ro = lax.rsqrt(jnp.mean(ov * ov, axis=-1, keepdims=True) + EPS)
            ohat = ov * ro
            sg = _sigmoid(zv)
            keep.append((ro, ohat, zv, sg))
            ydn = ohat * dnw * (zv * sg)
            y_ref[:, D_HALF + h * HEAD:D_HALF + (h + 1) * HEAD] = ydn.astype(BF16)
            yt_ref[D_HALF + h * HEAD:D_HALF + (h + 1) * HEAD, :] = ydn.T.astype(BF16)

        hv = x_ref[...] + jnp.dot(y_ref[...], wo_ref[...], preferred_element_type=F32)
        r2 = lax.rsqrt(jnp.mean(hv * hv, axis=-1, keepdims=True) + EPS)
        hhat = hv * r2
        fnw = fnw_ref[...]
        err = hhat * fnw - tg_ref[...]
        loss_ref[...] += 0.5 * jnp.sum(_rowsum(err * err) * (1.0 / D_MODEL), axis=0, keepdims=True)
        dout = err * (1.0 / D_MODEL)
        gfn_ref[...] += _colsum(dout * hhat)
        dhh = dout * fnw
        dh = r2 * (dhh - hhat * jnp.mean(dhh * hhat, axis=-1, keepdims=True))
        dh_ref[...] = dh
        dy = _bdot_nt(dh, wo_ref[...])
        dyp_ref[...] = dy[:, :D_HALF]
        gdn = jnp.zeros((1, HEAD), F32)
        for h in HEADS:
            ro, ohat, zv, sg = keep[h]
            dyd = dy[:, D_HALF + h * HEAD:D_HALF + (h + 1) * HEAD]
            sz = zv * sg
            dz_ref[:, HEAD_COLS[h]] = (dyd * ohat * dnw * (sg * (1.0 + zv * (1.0 - sg)))).astype(BF16)
            gdn = gdn + _colsum(dyd * ohat * sz)
            doh = dyd * dnw * sz
            do_ref[:, HEAD_COLS[h]] = ro * (doh - ohat * jnp.mean(doh * ohat, axis=-1, keepdims=True))
        gdn_ref[...] += gdn

    wide = pl.BlockSpec((t, D_MODEL), lambda i: (i, 0))
    half = pl.BlockSpec((t, D_HALF), lambda i: (i, 0))
    const = lambda shape: pl.BlockSpec(shape, lambda i: (0,) * len(shape))
    return _call(
        body, name="out_fwd_bwd", grid=(s // t,),
        in_specs=[wide, half, half, pl.BlockSpec((t, D_HALF), lambda i: (i, 5)), wide,
                  const((D_MODEL, D_MODEL)), const((1, HEAD)), const((1, D_MODEL))],
        out_specs=[pl.BlockSpec((D_MODEL, t), lambda i: (0, i)), wide, half, half, half,
                   const((1, HEAD)), const((1, D_MODEL)), const((1, HEAD))],
        out_shape=[_sds((D_MODEL, s), BF16), _sds((s, D_MODEL)), _sds((s, D_HALF)), _sds((s, D_HALF)), _sds((s, D_HALF), BF16),
                   _sds((1, HEAD)), _sds((1, D_MODEL)), _sds((1, HEAD))],
        scratch_shapes=[pltpu.VMEM((t, D_MODEL), BF16)],
        compiler_params=_params("arbitrary"),
    )(x, y_pool, o, proj, target, w_out, dn_norm_w, final_norm_w)


def _token_matmul(name, at, pieces):
    m, s = at.shape
    n = len(pieces)
    tn, tk = D_HALF, 512

    def body(a_ref, *refs):
        p_refs, o_ref = refs[:n], refs[n]

        @pl.when(pl.program_id(0) == 0)
        def _():
            o_ref[...] = jnp.zeros_like(o_ref)

        av = a_ref[...]
        for p in range(n):
            o_ref[:, p * tn:(p + 1) * tn] += _bdot(av, p_refs[p][...])

    return _call(
        body, name=name, grid=(s // tk,),
        in_specs=[pl.BlockSpec((m, tk), lambda k: (0, k))]
                 + [pl.BlockSpec((tk, tn), functools.partial(lambda k, cb: (k, cb), cb=cb)) for _, cb in pieces],
        out_specs=pl.BlockSpec((m, n * tn), lambda k: (0, 0)),
        out_shape=_sds((m, n * tn)),
        compiler_params=_params("arbitrary"),
    )(at, *[p[0] for p in pieces])


def _pool_bwd(proj, dyp, pool_w, pool_scale):
    s = proj.shape[0]
    t = POOL_T
    hb = t // HEAD
    last = s // HEAD - 1

    def body(u_ref, z_ref, halo_ref, dy_ref, zn_ref, dyn_ref, pw_ref, ps_ref, du_ref, dz_ref, gpw_ref, gps_ref):
        i = pl.program_id(0)

        @pl.when(i == 0)
        def _():
            gpw_ref[...] = jnp.zeros_like(gpw_ref)
            gps_ref[...] = jnp.zeros_like(gps_ref)

        live = (i > 0).astype(F32)
        more = (i < pl.num_programs(0) - 1).astype(F32)
        for g, w in enumerate(WINDOWS):
            sl = HEAD_COLS[g]
            zg = z_ref[:, sl]
            ps = ps_ref[:, sl]
            pw = pw_ref[g]
            mix, mixed, sg, cnt = _pool_mix(u_ref[:, sl], halo_ref[:, sl] * live, zg, pw, i * t, w)
            dyg = dy_ref[:, sl]
            sz = zg * sg
            dz_ref[:, sl] = (dyg * mixed * ps * (sg * (1.0 + zg * (1.0 - sg)))).astype(BF16)
            gps_ref[:, sl] += _colsum(dyg * mixed * sz)
            dmixed = dyg * ps * sz
            gpw_ref[g] += _bdot_tn(mix, dmixed)
            dmix = _bdot_nt(dmixed, pw)
            zn = zn_ref[:, sl]
            dmix_n = _bdot_nt(dyn_ref[:, sl] * more * ps * (zn * _sigmoid(zn)), pw)
            du_ref[:, sl] = (_mask_dot(_band(t, t, 0, w, anti=True), dmix / cnt)
                             + _mask_dot(_band(t, HEAD, t, w, anti=True), dmix_n * (1.0 / w)) - dmix).astype(BF16)

    tile = lambda col: pl.BlockSpec((t, D_HALF), lambda i: (i, col))
    below = lambda col: pl.BlockSpec((HEAD, D_HALF), lambda i: (jnp.minimum((i + 1) * hb, last), col))
    return _call(
        body, name="pool_bwd", grid=(s // t,),
        in_specs=[tile(0), tile(1), pl.BlockSpec((HEAD, D_HALF), lambda i: (jnp.maximum(i * hb - 1, 0), 0)),
                  tile(0), below(1), below(0),
                  pl.BlockSpec((N_HEADS, HEAD, HEAD), lambda i: (0, 0, 0)), pl.BlockSpec((1, D_HALF), lambda i: (0, 0))],
        out_specs=[tile(0), tile(0), pl.BlockSpec((N_HEADS, HEAD, HEAD), lambda i: (0, 0, 0)),
                   pl.BlockSpec((1, D_HALF), lambda i: (0, 0))],
        out_shape=[_sds((s, D_HALF), BF16), _sds((s, D_HALF), BF16), _sds((N_HEADS, HEAD, HEAD)), _sds((1, D_HALF))],
        compiler_params=_params("arbitrary"),
    )(proj, proj, proj, dyp, proj, dyp, pool_w, pool_scale)


def _scan_bwd(do, vn, qd, kd, w, att, cd, st):
    s = do.shape[0]
    n_steps = s // SCAN_ROWS

    def body(do_ref, vn_ref, qd_ref, kd_ref, w_ref, att_ref, cd_ref, st_ref,
             du_ref, dw_ref, datt_ref, dqd_ref, dkd_ref, dcd_ref, dstate):
        @pl.when(pl.program_id(0) == 0)
        def _():
            dstate[...] = jnp.zeros_like(dstate)
        _, incl, _, _ = _pair_masks()
        cols = list(enumerate(HEAD_COLS))
        dv_intra = []
        for pp in range(SCAN_PAIRS):
            rp = slice(pp * PAIR, (pp + 1) * PAIR)
            dv_intra.append([_bdot_tn(att_ref[rp, sl], do_ref[rp, sl]) for _, sl in cols])
            for _, sl in cols:
                datt_ref[rp, sl] = jnp.where(incl, _bdot_nt(do_ref[rp, sl], vn_ref[rp, sl]), 0.0)
        ds = [dstate[h] for h in HEADS]
        for ci in range(2 * SCAN_PAIRS - 1, -1, -1):
            rs = slice(ci * CHUNK, (ci + 1) * CHUNK)
            in_pair = slice((ci % 2) * CHUNK, (ci % 2 + 1) * CHUNK)
            sm = [st_ref[ci, h] for h in HEADS]
            dvn = [dv_intra[ci // 2][h][in_pair] + _bdot(kd_ref[rs, sl], ds[h]) for h, sl in cols]
            for h, sl in cols:
                du_ref[rs, sl] = dvn[h].astype(BF16)
            dqd = [_bdot_nt(do_ref[rs, sl], sm[h]) for h, sl in cols]
            dw = [-_bdot_nt(dvn[h], sm[h]) for h, _ in cols]
            dkd = [_bdot_nt(vn_ref[rs, sl], ds[h]) for h, sl in cols]
            dcd = [jnp.broadcast_to(_rowsum(_colsum(ds[h] * sm[h])), (8, HEAD)) for h in HEADS]
            for h, sl in cols:
                dqd_ref[rs, sl] = dqd[h]
                dw_ref[rs, sl] = dw[h].astype(BF16)
                dkd_ref[rs, sl] = dkd[h]
                dcd_ref[ci * 8:(ci + 1) * 8, sl] = dcd[h]
            ds = [ds[h] * cd_ref[ci * 8:ci * 8 + 1, sl] + _bdot_tn(qd_ref[rs, sl], do_ref[rs, sl])
                  - _bdot_tn(w_ref[rs, sl], dvn[h]) for h, sl in cols]
        for h in HEADS:
            dstate[h] = ds[h]

    rev = pl.BlockSpec((SCAN_ROWS, D_HALF), lambda i: (n_steps - 1 - i, 0))
    rev_scalar = _chunk_scalar_spec(SCAN_PAIRS, lambda i: (n_steps - 1 - i, 0))
    return _call(
        body, name="scan_bwd", grid=(n_steps,),
        in_specs=[rev] * 6 + [rev_scalar,
                              pl.BlockSpec((2 * SCAN_PAIRS, N_HEADS, HEAD, HEAD), lambda i: (n_steps - 1 - i, 0, 0, 0))],
        out_specs=[rev] * 5 + [rev_scalar],
        out_shape=[_sds((s, D_HALF), BF16)] * 2 + [_sds((s, D_HALF))] * 3 + [_sds((s // 8, D_HALF))],
        scratch_shapes=[pltpu.VMEM((N_HEADS, HEAD, HEAD), F32)],
        compiler_params=_params("arbitrary"),
    )(do, vn, qd, kd, w, att, cd, st)


def _intra_bwd(qn, kn, vs, beta, g, tm, du, dw, datt, dqd, dkd, dcd):
    s = qn.shape[0]

    def body(qn_ref, kn_ref, vs_ref, beta_ref, g_ref, t_ref, du_ref, dw_ref, datt_ref, dqd_ref, dkd_ref, dcd_ref,
             dqn_ref, dkn_ref, dvs_ref, dbeta_ref, dg_ref):
        ones = jnp.ones((PAIR, HEAD), BF16)
        tn = (((0,), (0,)), ((), ()))
        kn, vs, beta = _heads(kn_ref), _heads(vs_ref), _heads(beta_ref)
        cm = _pair_common(_heads(qn_ref), kn, vs, beta, _heads(g_ref))
        tmv, duv, dwv, dattv, dqdv, dkdv = (_heads(r) for r in (t_ref, du_ref, dw_ref, datt_ref, dqd_ref, dkd_ref))
        dvb = _each(_bdot_tn, tmv, duv)
        dt = _each(lambda a, b, c, d: _bdot_nt(a, b) + _bdot_nt(c, d), duv, cm["vb"], dwv, cm["kbg"])
        dkbg = _each(_bdot_tn, tmv, dwv)
        m1 = _each(_bdot_tn, tmv, dt)
        da = _each(lambda a, b: -jnp.where(cm["strict"], _bdot_nt(a, b), 0.0), m1, tmv)
        dkk = _each(lambda a, b: a * b, da, cm["decay"])
        dqk = _each(lambda a, b: a * b, dattv, cm["decay"])
        dd = _each(lambda a, b, c, d: a * b + c * d, dkk, cm["kk"], dqk, cm["qk"])
        dkb = _each(lambda a, b, c, d: _bdot(a, b) + c * d, dkk, kn, dkbg, cm["egc"])
        dq = _each(lambda a, b, c, d: _bdot(a, b) + c * d, dqk, kn, dqdv, cm["egc"])
        dkn = _each(lambda a, b, c, d: _bdot_tn(a, b) + _bdot_tn(c, d), dkk, cm["kb"], dqk, cm["q"])
        dkn = _each(lambda a, b, c, d, e: a + b * c + d * e, dkn, dkdv, cm["ekd"], dkb, beta)
        t_kd = _each(lambda a, b, c: _rowsum(a * b * c), dkdv, kn, cm["ekd"])
        split = _each(_split, dd)
        rows_dd = [jnp.dot(hi, ones, preferred_element_type=F32) + jnp.dot(lo, ones, preferred_element_type=F32)
                   for hi, lo in split]
        cols_dd = [lax.dot_general(hi, ones, tn, preferred_element_type=F32)
                   + lax.dot_general(lo, ones, tn, preferred_element_type=F32) for hi, lo in split]
        dgc = _each(lambda r, c, a, b, e, f, k, t: r - c + _rowsum(a * b * e) + _rowsum(f * k) - t,
                    rows_dd, cols_dd, dqdv, cm["q"], cm["egc"], dkbg, cm["kbg"], t_kd)
        same_b = cm["same"].astype(BF16)
        rowi = lax.broadcasted_iota(I32, (PAIR, HEAD), 0)
        dcd = _each(lambda d: jnp.where(rowi < CHUNK, d[0:1], d[8:9]), _heads(dcd_ref, rows=16))
        dgl = _each(lambda t, d, c: _mask_dot(same_b, jnp.broadcast_to(t, (PAIR, HEAD))) + d * c, t_kd, dcd, cm["cd"])
        is_last = jnp.bitwise_and(rowi, CHUNK - 1) == CHUNK - 1
        dgc = _each(lambda a, b: a + jnp.where(is_last, b, 0.0), dgc, dgl)
        r = lax.broadcasted_iota(I32, (PAIR, PAIR), 0)
        c = lax.broadcasted_iota(I32, (PAIR, PAIR), 1)
        upper_b = (cm["same"] & (r <= c)).astype(BF16)
        _put_heads(dg_ref, _each(lambda v: _mask_dot(upper_b, v), dgc))
        _put_heads(dbeta_ref, _each(lambda a, b, c, d: jnp.broadcast_to(_rowsum(a * b) + _rowsum(c * d), (PAIR, HEAD)),
                                    dkb, kn, dvb, vs))
        _put_heads(dqn_ref, _each(lambda v: v * QK_SCALE, dq))
        _put_heads(dkn_ref, dkn)
        _put_heads(dvs_ref, _each(lambda a, b: a * b, dvb, beta))

    return _call(
        body, name="intra_bwd", grid=(s // (INTRA_PAIRS * PAIR),),
        in_specs=[_pair_spec()] * 11 + [_chunk_scalar_spec(INTRA_PAIRS)], out_specs=[_pair_spec()] * 5,
        out_shape=[_sds((s, D_HALF))] * 5,
        compiler_params=_params("arbitrary"),
    )(qn, kn, vs, beta, g, tm, du, dw, datt, dqd, dkd, dcd)


def _rows8(x):
    acc = x[0:8]
    for r in range(8, x.shape[0], 8):
        acc = acc + x[r:r + 8]
    return acc


def _conv_bwd(proj, conv_w, a_log, dt_bias, dqn, dkn, dvs, dbeta, dg):
    s = proj.shape[0]
    t = CONV_T
    n_tiles = s // t
    n_sub = t // CONV_SUB
    tile_of = lambda i: n_tiles - 1 - i

    def body(q_ref, k_ref, v_ref, hq_ref, hk_ref, hv_ref, ba_ref, cw_ref, al_ref, dtb_ref,
             dqn_ref, dkn_ref, dvs_ref, dbeta_ref, dg_ref, oq_ref, ok_ref, ov_ref, dba_ref, gcw_out, gsm_out,
             below, gcw_ref, gsm_ref):
        @pl.when(pl.program_id(0) == 0)
        def _():
            gcw_ref[...] = jnp.zeros_like(gcw_ref)
            gsm_ref[...] = jnp.zeros_like(gsm_ref)
            below[...] = jnp.zeros_like(below)

        live = (pl.program_id(0) < n_tiles - 1).astype(F32)
        parts = ((q_ref, hq_ref, dqn_ref, oq_ref), (k_ref, hk_ref, dkn_ref, ok_ref), (v_ref, hv_ref, dvs_ref, ov_ref))
        lane = lax.broadcasted_iota(I32, (CONV_SUB, HEAD), 1)
        lane8 = lax.broadcasted_iota(I32, (8, HEAD), 1)

        def sub_tile(r0, first):
            rows = pl.ds(r0, CONV_SUB)
            for p, (x_ref, h_ref, d_ref, o_ref) in enumerate(parts):
                for h in HEADS:
                    cs = HEAD_COLS[h]
                    wide = slice(p * D_HALF + h * HEAD, p * D_HALF + (h + 1) * HEAD)
                    cw = cw_ref[:, wide]
                    prev8 = h_ref[:, cs] * live if first else x_ref[pl.ds(r0 - 8, 8), cs]
                    taps = _conv_taps(x_ref[rows, cs], prev8)
                    y = _conv_pre(taps, cw)
                    sg = _sigmoid(y)
                    sv = y * sg
                    ds = d_ref[rows, cs]
                    if p < 2:
                        rn = lax.rsqrt(_rowsum(sv * sv) + EPS)
                        nrm = sv * rn
                        ds = rn * (ds - nrm * _rowsum(ds * nrm))
                    dy = ds * (sg * (1.0 + y * (1.0 - sg)))
                    for j in range(CONV_K):
                        gcw_ref[8 * j:8 * j + 8, wide] += _rows8(dy * taps[j])
                    nxt = below[:, wide]
                    acc = dy * cw[CONV_K - 1:CONV_K]
                    for sft in range(1, CONV_K):
                        acc = acc + _shift_up(dy, nxt, sft) * cw[CONV_K - 1 - sft:CONV_K - sft]
                    o_ref[rows, cs] = acc.astype(BF16)
                    below[:, wide] = dy[0:8]

            ba = ba_ref[rows, :]
            dba = jnp.zeros((CONV_SUB, HEAD), F32)
            gsm = jnp.zeros((8, HEAD), F32)
            for h in HEADS:
                beta = _sigmoid(ba[:, h:h + 1])
                dbeta = dbeta_ref[rows, h * HEAD:h * HEAD + 1]
                xg = ba[:, N_HEADS + h:N_HEADS + h + 1] + dtb_ref[0:1, h:h + 1]
                nexp = -jnp.exp(al_ref[0:1, h:h + 1])
                dgv = dg_ref[rows, h * HEAD:h * HEAD + 1]
                da = dgv * nexp * _sigmoid(xg)
                dba = dba + jnp.where(lane == h, dbeta * beta * (1.0 - beta), 0.0) + jnp.where(lane == N_HEADS + h, da, 0.0)
                gsm = (gsm + jnp.where(lane8 == h, _rows8(dgv * nexp * _softplus(xg)), 0.0)
                       + jnp.where(lane8 == N_HEADS + h, _rows8(da), 0.0))
            dba_ref[rows, :] = jnp.zeros((CONV_SUB, D_HALF), BF16)
            dba_ref[rows, :HEAD] = dba.astype(BF16)
            gsm_ref[...] += gsm

        def step(k, carry):
            sub_tile(pl.multiple_of((n_sub - 1 - k) * CONV_SUB, CONV_SUB), False)
            return carry

        lax.fori_loop(0, n_sub - 1, step, 0)
        sub_tile(0, True)

        @pl.when(pl.program_id(0) == n_tiles - 1)
        def _():
            gcw_out[...] = jnp.zeros_like(gcw_out)
            for j in range(CONV_K):
                gcw_out[j:j + 1, :] = _colsum(gcw_ref[8 * j:8 * j + 8, :])
            gsm_out[...] = jnp.broadcast_to(_colsum(gsm_ref[...]), (8, HEAD))

    row = pl.BlockSpec((t, D_HALF), lambda i: (tile_of(i), 0))
    const = lambda shape: pl.BlockSpec(shape, lambda i: (0, 0))
    return _call(
        body, name="conv_bwd", grid=(n_tiles,),
        in_specs=_conv_specs(t, tile_of) + [pl.BlockSpec((t, HEAD), lambda i: (tile_of(i), COL_BA // HEAD)),
                                            const((CONV_K, 3 * D_HALF)), const((1, N_HEADS)), const((1, N_HEADS))] + [row] * 5,
        out_specs=[row, row, row, row, const((8, 3 * D_HALF)), const((8, HEAD))],
        out_shape=[_sds((s, D_HALF), BF16)] * 4 + [_sds((8, 3 * D_HALF)), _sds((8, HEAD))],
        scratch_shapes=[pltpu.VMEM((8, 3 * D_HALF), F32), pltpu.VMEM((8 * CONV_K, 3 * D_HALF), F32),
                        pltpu.VMEM((8, HEAD), F32)],
        compiler_params=_params("arbitrary"),
    )(proj, proj, proj, proj, proj, proj, proj, conv_w, a_log, dt_bias, dqn, dkn, dvs, dbeta, dg)


def _conv_bwd_pre(proj, conv_w, a_log, dt_bias, dqn, dkn, dvs, dbeta, dg):
    s = proj.shape[0]
    t = CONV_T

    def body(q_ref, k_ref, v_ref, hq_ref, hk_ref, hv_ref, ba_ref, cw_ref, al_ref, dtb_ref,
             dqn_ref, dkn_ref, dvs_ref, dbeta_ref, dg_ref, dyq_ref, dyk_ref, dyv_ref, dba_ref, gcw_ref, gsm_ref):
        @pl.when(pl.program_id(0) == 0)
        def _():
            gcw_ref[...] = jnp.zeros_like(gcw_ref)
            gsm_ref[...] = jnp.zeros_like(gsm_ref)

        live = (pl.program_id(0) > 0).astype(F32)
        parts = ((q_ref, hq_ref, dqn_ref, dyq_ref), (k_ref, hk_ref, dkn_ref, dyk_ref), (v_ref, hv_ref, dvs_ref, dyv_ref))
        for p, (x_ref, h_ref, d_ref, dy_ref) in enumerate(parts):
            cols = slice(p * D_HALF, (p + 1) * D_HALF)
            taps = _conv_taps(x_ref[...], h_ref[...] * live)
            y = _conv_pre(taps, cw_ref[:, cols])
            sg = _sigmoid(y)
            sv = y * sg
            if p == 2:
                ds = d_ref[...]
            else:
                segs = []
                for h in HEADS:
                    seg = _head(sv, h)
                    rn = lax.rsqrt(_rowsum(seg * seg) + EPS)
                    nrm = seg * rn
                    dn = d_ref[:, HEAD_COLS[h]]
                    segs.append(rn * (dn - nrm * _rowsum(dn * nrm)))
                ds = jnp.concatenate(segs, axis=1)
            dy = ds * (sg * (1.0 + y * (1.0 - sg)))
            dy_ref[...] = dy
            for j in range(CONV_K):
                gcw_ref[j:j + 1, cols] += _colsum(dy * taps[j])

        ba = ba_ref[...]
        lane = lax.broadcasted_iota(I32, (t, HEAD), 1)
        lane1 = lax.broadcasted_iota(I32, (1, HEAD), 1)
        dba = jnp.zeros((t, HEAD), F32)
        gsm = jnp.zeros((1, HEAD), F32)
        for h in HEADS:
            beta = _sigmoid(ba[:, h:h + 1])
            dbeta = dbeta_ref[:, h * HEAD:h * HEAD + 1]
            xg = ba[:, N_HEADS + h:N_HEADS + h + 1] + dtb_ref[0:1, h:h + 1]
            nexp = -jnp.exp(al_ref[0:1, h:h + 1])
            dgv = dg_ref[:, h * HEAD:h * HEAD + 1]
            da = dgv * nexp * _sigmoid(xg)
            dba = dba + jnp.where(lane == h, dbeta * beta * (1.0 - beta), 0.0) + jnp.where(lane == N_HEADS + h, da, 0.0)
            gsm = (gsm + jnp.where(lane1 == h, _colsum(dgv * nexp * _softplus(xg)), 0.0)
                   + jnp.where(lane1 == N_HEADS + h, _colsum(da), 0.0))
        dba_ref[...] = jnp.zeros_like(dba_ref)
        dba_ref[:, :HEAD] = dba.astype(BF16)
        gsm_ref[0:1, :] += gsm

    row = pl.BlockSpec((t, D_HALF), lambda i: (i, 0))
    return _call(
        body, name="conv_bwd_pre", grid=(s // t,),
        in_specs=_conv_specs(t) + [pl.BlockSpec((t, HEAD), lambda i: (i, COL_BA // HEAD)),
                                   pl.BlockSpec((CONV_K, 3 * D_HALF), lambda i: (0, 0)),
                                   pl.BlockSpec((1, N_HEADS), lambda i: (0, 0)),
                                   pl.BlockSpec((1, N_HEADS), lambda i: (0, 0))] + [row] * 5,
        out_specs=[row, row, row, row,
                   pl.BlockSpec((8, 3 * D_HALF), lambda i: (0, 0)), pl.BlockSpec((8, HEAD), lambda i: (0, 0))],
        out_shape=[_sds((s, D_HALF))] * 3 + [_sds((s, D_HALF), BF16), _sds((8, 3 * D_HALF)), _sds((8, HEAD))],
        compiler_params=_params("arbitrary"),
    )(proj, proj, proj, proj, proj, proj, proj, conv_w, a_log, dt_bias, dqn, dkn, dvs, dbeta, dg)


def _conv_bwd_in(dyq, dyk, dyv, conv_w):
    s = dyq.shape[0]
    t = CONV_T
    last = s // 8 - 1

    def body(q_ref, k_ref, v_ref, nq_ref, nk_ref, nv_ref, cw_ref, oq_ref, ok_ref, ov_ref):
        more = (pl.program_id(0) < pl.num_programs(0) - 1).astype(F32)
        for p, (d_ref, n_ref, o_ref) in enumerate(((q_ref, nq_ref, oq_ref), (k_ref, nk_ref, ok_ref), (v_ref, nv_ref, ov_ref))):
            cw = cw_ref[:, p * D_HALF:(p + 1) * D_HALF]
            dy = d_ref[...]
            nxt = n_ref[...] * more
            acc = dy * cw[3:4]
            for sft in (1, 2, 3):
                acc = acc + _shift_up(dy, nxt, sft) * cw[3 - sft:4 - sft]
            o_ref[...] = acc.astype(BF16)

    row = pl.BlockSpec((t, D_HALF), lambda i: (i, 0))
    nxt = pl.BlockSpec((8, D_HALF), lambda i: (jnp.minimum((i + 1) * (t // 8), last), 0))
    return _call(
        body, name="conv_bwd_in", grid=(s // t,),
        in_specs=[row] * 3 + [nxt] * 3 + [pl.BlockSpec((CONV_K, 3 * D_HALF), lambda i: (0, 0))],
        out_specs=[row] * 3, out_shape=[_sds((s, D_HALF), BF16)] * 3,
        compiler_params=_params("arbitrary"),
    )(dyq, dyk, dyv, dyq, dyk, dyv, conv_w)


IN_T = 512


def _in_bwd(x, dh, norm_w, w_pad, pieces):
    s = x.shape[0]
    t = IN_T
    widths = [D_HALF] * 6 + [N_IN_PAD - COL_BA]

    def body(*refs):
        x_ref, dh_ref, nw_ref, w_ref = refs[:4]
        p_refs = refs[4:4 + len(pieces)]
        gx_ref, gnw_ref = refs[4 + len(pieces):]

        @pl.when(pl.program_id(0) == 0)
        def _():
            gnw_ref[...] = jnp.zeros_like(gnw_ref)

        dn = jnp.zeros((t, D_MODEL), F32)
        col = 0
        for p_ref, wd in zip(p_refs, widths):
            dn = dn + _bdot_nt(p_ref[...], w_ref[:, col:col + wd])
            col += wd
        xv = x_ref[...]
        r = lax.rsqrt(jnp.mean(xv * xv, axis=-1, keepdims=True) + EPS)
        xhat = xv * r
        gnw_ref[...] += _colsum(dn * xhat)
        dxh = dn * nw_ref[...]
        gx_ref[...] = dh_ref[...] + r * (dxh - xhat * jnp.mean(dxh * xhat, axis=-1, keepdims=True))

    wide = pl.BlockSpec((t, D_MODEL), lambda i: (i, 0))
    return _call(
        body, name="in_bwd", grid=(s // t,),
        in_specs=[wide, wide, pl.BlockSpec((1, D_MODEL), lambda i: (0, 0)),
                  pl.BlockSpec((D_MODEL, N_IN_PAD), lambda i: (0, 0))]
                 + [pl.BlockSpec((t, wd), lambda i: (i, 0)) for wd in widths],
        out_specs=[wide, pl.BlockSpec((1, D_MODEL), lambda i: (0, 0))],
        out_shape=[_sds((s, D_MODEL)), _sds((1, D_MODEL))],
        compiler_params=_params("arbitrary"),
    )(x, dh, norm_w, w_pad, *pieces)


def _adamw_shard(name, w, g_own, g_got, cidx, m, v):
    _, r, c = w.shape
    half = r // 2
    rows = 256 if half % 256 == 0 else half
    per_half = half // rows

    def body(c_ref, w_ref, go_ref, gg_ref, m_ref, v_ref, gout_ref, d_ref, nm_ref, nv_ref):
        mine = (pl.program_id(0) // per_half) == c_ref[0]
        gv = jnp.where(mine, go_ref[:, :c], gg_ref[:, :c])
        gout_ref[0] = gv
        mn = ADAM_B1 * m_ref[0] + (1.0 - ADAM_B1) * gv
        vn = ADAM_B2 * v_ref[0] + (1.0 - ADAM_B2) * (gv * gv)
        m_hat = mn / (1.0 - ADAM_B1 ** ADAM_STEP)
        v_hat = vn / (1.0 - ADAM_B2 ** ADAM_STEP)
        d_ref[0] = -ADAM_LR * (m_hat / (jnp.sqrt(v_hat) + ADAM_EPS) + ADAM_WD * w_ref[0])
        nm_ref[0] = mn
        nv_ref[0] = vn

    blk = pl.BlockSpec((1, rows, c), lambda i, c_ref: (0, i, 0))
    gblk = pl.BlockSpec((rows, g_own.shape[1]), lambda i, c_ref: (i % per_half, 0))
    return _call(
        body, name=name,
        grid_spec=pltpu.PrefetchScalarGridSpec(
            num_scalar_prefetch=1, grid=(2 * per_half,),
            in_specs=[blk, gblk, gblk, blk, blk], out_specs=[blk] * 4),
        out_shape=[_sds((1, r, c))] * 4,
        compiler_params=_params("arbitrary"),
    )(cidx, w, g_own, g_got, m, v)


def _adamw_tiles(name, w, g, m, v):
    n = w.shape[0]
    nb = 77 if n % 77 == 0 else n

    def body(w_ref, g_ref, m_ref, v_ref, d_ref, nm_ref, nv_ref):
        gv = g_ref[...]
        mn = ADAM_B1 * m_ref[...] + (1.0 - ADAM_B1) * gv
        vn = ADAM_B2 * v_ref[...] + (1.0 - ADAM_B2) * (gv * gv)
        m_hat = mn / (1.0 - ADAM_B1 ** ADAM_STEP)
        v_hat = vn / (1.0 - ADAM_B2 ** ADAM_STEP)
        d_ref[...] = -ADAM_LR * (m_hat / (jnp.sqrt(v_hat) + ADAM_EPS) + ADAM_WD * w_ref[...])
        nm_ref[...] = mn
        nv_ref[...] = vn

    blk = pl.BlockSpec((nb, 8, HEAD), lambda i: (i, 0, 0))
    return _call(
        body, name=name, grid=(n // nb,),
        in_specs=[blk] * 4, out_specs=[blk] * 3, out_shape=[_sds(w.shape)] * 3,
        compiler_params=_params("arbitrary"),
    )(w, g, m, v)


def _exchange(name, inputs, out_shapes, phases):
    n_in = len(inputs)
    n_out = len(out_shapes)
    n_cp = sum(len(p) for p in phases)

    def body(*refs):
        ins, outs = refs[:n_in], refs[n_in:n_in + n_out]
        send, recv = refs[n_in + n_out:]
        pos = (lax.axis_index("x"), lax.axis_index("y"), lax.axis_index("c"))
        k = 0
        for phase in phases:
            cps = []
            for src, dst, target in phase:
                cps.append(pltpu.make_async_remote_copy(
                    src_ref=src(ins, outs, pos), dst_ref=dst(ins, outs, pos), send_sem=send.at[k], recv_sem=recv.at[k],
                    device_id=target(pos), device_id_type=pl.DeviceIdType.MESH))
                k += 1
            for cp in cps:
                cp.start()
            for cp in cps:
                cp.wait()

    anyspec = pl.BlockSpec(memory_space=pl.ANY)
    return _call(
        body, name=name,
        in_specs=[anyspec] * n_in, out_specs=[anyspec] * n_out, out_shape=list(out_shapes),
        scratch_shapes=[pltpu.SemaphoreType.DMA((n_cp,)), pltpu.SemaphoreType.DMA((n_cp,))],
    )(*inputs)


def _exchange_start(name, inputs, out_shapes, copies):
    n_in, n_out, n_cp = len(inputs), len(out_shapes), len(copies)

    def body(*refs):
        ins, lands = refs[:n_in], refs[n_in:n_in + n_out]
        sems = refs[n_in + n_out:n_in + n_out + 2 * n_cp]
        token = refs[-1]
        pos = (lax.axis_index("x"), lax.axis_index("y"), lax.axis_index("c"))
        for k, (src, dst, target) in enumerate(copies):
            pltpu.make_async_remote_copy(
                src_ref=src(ins, lands, pos), dst_ref=dst(ins, lands, pos), send_sem=sems[2 * k], recv_sem=sems[2 * k + 1],
                device_id=target(pos), device_id_type=pl.DeviceIdType.MESH).start()
        token[...] = jnp.zeros_like(token)

    hbm = pl.BlockSpec(memory_space=pltpu.HBM)
    sem = pl.BlockSpec(memory_space=pltpu.SEMAPHORE)
    bufs = list(inputs) + [lax.empty(o.shape, o.dtype) for o in out_shapes]
    outs = _call(
        body, name=name,
        out_shape=tuple([pltpu.SemaphoreType.DMA(())] * (2 * n_cp) + [pltpu.HBM(b.shape, b.dtype) for b in bufs]
                        + [_sds((8, HEAD))]),
        in_specs=[hbm] * len(bufs),
        out_specs=tuple([sem] * (2 * n_cp) + [hbm] * len(bufs) + [pl.BlockSpec(memory_space=pltpu.VMEM)]),
        input_output_aliases={i: 2 * n_cp + i for i in range(len(bufs))},
        compiler_params=pltpu.CompilerParams(has_side_effects=pltpu.SideEffectType.DATAFLOW_SIDE_EFFECTING),
    )(*[pltpu.with_memory_space_constraint(b, pltpu.HBM) for b in bufs])
    return outs[:2 * n_cp], outs[2 * n_cp:2 * n_cp + n_in], outs[2 * n_cp + n_in:-1], outs[-1]


def _exchange_wait(name, sems, sources, lands, copies, after):
    n_in, n_out, n_cp = len(sources), len(lands), len(copies)

    def body(*refs):
        ins, zones = refs[:n_in], refs[n_in:n_in + n_out]
        sem_refs = refs[n_in + n_out:n_in + n_out + 2 * n_cp]
        pos = (lax.axis_index("x"), lax.axis_index("y"), lax.axis_index("c"))
        for k, (src, dst, target) in enumerate(copies):
            cp = pltpu.make_async_remote_copy(
                src_ref=src(ins, zones, pos), dst_ref=dst(ins, zones, pos), send_sem=sem_refs[2 * k],
                recv_sem=sem_refs[2 * k + 1], device_id=target(pos), device_id_type=pl.DeviceIdType.MESH)
            cp.wait_send()
            cp.wait_recv()

    hbm = pl.BlockSpec(memory_space=pltpu.HBM)
    sem = pl.BlockSpec(memory_space=pltpu.SEMAPHORE)
    bufs = list(sources) + list(lands)
    outs = _call(
        body, name=name,
        out_shape=tuple(pltpu.HBM(b.shape, b.dtype) for b in bufs),
        in_specs=[hbm] * len(bufs) + [sem] * (2 * n_cp) + [pl.BlockSpec(memory_space=pl.ANY)],
        out_specs=tuple([hbm] * len(bufs)),
        input_output_aliases={i: i for i in range(len(bufs))},
        compiler_params=pltpu.CompilerParams(has_side_effects=pltpu.SideEffectType.DATAFLOW_SIDE_EFFECTING),
    )(*bufs, *sems, after)
    return outs[:n_in], outs[n_in:]


def _allreduce_tile(name, v):
    def body(v_ref, out_ref, slots, send, recv):
        x, y, c = lax.axis_index("x"), lax.axis_index("y"), lax.axis_index("c")
        me = 4 * x + 2 * y + c
        slots[me] = v_ref[...]
        cps = []
        for k in range(1, 8):
            peer = (x ^ (k >> 2), y ^ ((k >> 1) & 1), c ^ (k & 1))
            cps.append(pltpu.make_async_remote_copy(
                src_ref=v_ref, dst_ref=slots.at[me], send_sem=send.at[k - 1], recv_sem=recv.at[k - 1],
                device_id=peer, device_id_type=pl.DeviceIdType.MESH))
        for cp in cps:
            cp.start()
        for cp in cps:
            cp.wait()
        acc = slots[0]
        for i in range(1, 8):
            acc = acc + slots[i]
        out_ref[...] = acc

    vm = pl.BlockSpec(memory_space=pltpu.VMEM)
    return _call(
        body, name=name, in_specs=[vm], out_specs=vm, out_shape=_sds(v.shape),
        scratch_shapes=[pltpu.VMEM((8,) + v.shape, F32), pltpu.SemaphoreType.DMA((7,)), pltpu.SemaphoreType.DMA((7,))],
    )(v)


def _chip(pos):
    return 2 * pos[0] + pos[1]


def _other_chip(pos, mask):
    x, y, c = pos
    return (x ^ (mask >> 1), y ^ (mask & 1), c)


def _sibling(pos):
    return (pos[0], pos[1], 1 - pos[2])


def _gather_weights(wb, ob, cb):
    halves = (wb.shape[0] // 2, ob.shape[0] // 2)

    def half(a, pos):
        return pl.ds(pos[2] * halves[a], halves[a])

    first, second = [], []
    for mask in CHIP_MASKS:
        for a in (0, 1):
            first.append((lambda ins, outs, pos, a=a: ins[a].at[half(a, pos)],
                          lambda ins, outs, pos, a=a: outs[a].at[_chip(pos), half(a, pos)],
                          functools.partial(_other_chip, mask=mask)))
            second.append((lambda ins, outs, pos, a=a, mask=mask: outs[a].at[_chip(pos) ^ mask, half(a, pos)],
                           lambda ins, outs, pos, a=a, mask=mask: outs[a].at[_chip(pos) ^ mask, half(a, pos)],
                           _sibling))
        first.append((lambda ins, outs, pos: ins[2],
                      lambda ins, outs, pos: outs[2].at[_chip(pos)],
                      functools.partial(_other_chip, mask=mask)))
    return _exchange("gather_weights", [wb, ob, cb],
                     [_sds((4,) + wb.shape, wb.dtype), _sds((4,) + ob.shape, ob.dtype), _sds((4,) + cb.shape, cb.dtype)],
                     [first, second])


def _to_sibling_half(name, arrays):
    def src(ins, outs, pos, a):
        h = arrays[a].shape[-2] // 2
        sl = pl.ds((1 - pos[2]) * h, h)
        return ins[a].at[:, sl] if arrays[a].ndim == 3 else ins[a].at[sl]

    outs = [_sds(a.shape[:-2] + (a.shape[-2] // 2, a.shape[-1]), a.dtype) for a in arrays]
    phase = [(functools.partial(src, a=a), lambda ins, outs, pos, a=a: outs[a], _sibling) for a in range(len(arrays))]
    return _exchange(name, arrays, outs, [phase])


def _add_half(name, full, part, cidx):
    shape = part.shape
    lead = shape[0] if len(shape) == 3 else 1
    rows, cols = shape[-2], shape[-1]
    tr = rows // 2 if rows % 16 == 0 else rows
    nr = rows // tr
    f3 = full.reshape((lead,) + full.shape[-2:])
    p3 = part.reshape((lead, rows, cols))

    def body(c_ref, f_ref, p_ref, o_ref):
        o_ref[...] = (f_ref[...].astype(F32) + p_ref[...].astype(F32)).astype(o_ref.dtype)

    out = _call(
        body, name=name,
        grid_spec=pltpu.PrefetchScalarGridSpec(
            num_scalar_prefetch=1, grid=(lead, nr),
            in_specs=[pl.BlockSpec((1, tr, cols), lambda b, r, c_ref: (b, c_ref[0] * nr + r, 0)),
                      pl.BlockSpec((1, tr, cols), lambda b, r, c_ref: (b, r, 0))],
            out_specs=pl.BlockSpec((1, tr, cols), lambda b, r, c_ref: (b, r, 0))),
        out_shape=_sds((lead, rows, cols), part.dtype),
        compiler_params=_params("arbitrary", "arbitrary"),
    )(cidx, f3, p3)
    return out.reshape(shape)


def _to_other_chips(arrays, blocked):
    def src(ins, outs, pos, a, mask):
        return ins[a].at[_chip(pos) ^ mask] if blocked[a] else ins[a]

    outs = [_sds((3,) + (a.shape[1:] if b else a.shape), a.dtype) for a, b in zip(arrays, blocked)]
    copies = []
    for mi, mask in enumerate(CHIP_MASKS):
        for a in range(len(arrays)):
            copies.append((functools.partial(src, a=a, mask=mask), lambda ins, outs, pos, a=a, mi=mi: outs[a].at[mi],
                           functools.partial(_other_chip, mask=mask)))
    return outs, copies


def _add_chips(name, own, got, jidx, blocked):
    rows, cols = got.shape[-2:]
    tr = rows // 2 if rows % 16 == 0 else rows
    nr = rows // tr
    o3 = own if blocked else own.reshape((1, rows, cols))

    def body(j_ref, o_ref, g_ref, out_ref):
        out_ref[...] = ((o_ref[0].astype(F32) + g_ref[0].astype(F32))
                        + (g_ref[1].astype(F32) + g_ref[2].astype(F32)))

    own_map = (lambda r, j_ref: (j_ref[0], r, 0)) if blocked else (lambda r, j_ref: (0, r, 0))
    return _call(
        body, name=name,
        grid_spec=pltpu.PrefetchScalarGridSpec(
            num_scalar_prefetch=1, grid=(nr,),
            in_specs=[pl.BlockSpec((1, tr, cols), own_map),
                      pl.BlockSpec((3, tr, cols), lambda r, j_ref: (0, r, 0))],
            out_specs=pl.BlockSpec((tr, cols), lambda r, j_ref: (r, 0))),
        out_shape=_sds((rows, cols)),
        compiler_params=_params("arbitrary"),
    )(jidx, o3, got)


def _to_sibling(name, arrays):
    phase = [(lambda ins, outs, pos, a=a: ins[a], lambda ins, outs, pos, a=a: outs[a], _sibling)
             for a in range(len(arrays))]
    return _exchange(name, arrays, [_sds(a.shape, a.dtype) for a in arrays], [phase])


def _local_step(x, target, w_pad, w_out, conv_w, norm_w, pool_w, pool_scale, a_log, dt_bias, dn_norm_w, final_norm_w):
    proj, n_t = _proj_fwd(x, norm_w, w_pad)
    y_pool = _pool_fwd(proj, pool_w, pool_scale)
    qn, kn, vs, beta, g = _conv_fwd(proj, conv_w, a_log, dt_bias)
    u, w, att, qd, kd, tm, cd = _intra_fwd(qn, kn, vs, beta, g)
    o, vn, st = _scan_fwd(u, w, att, qd, kd, cd)
    y_t, dh, dyp, do, ddz, loss, g_fnw, g_dnw = _out_fwd_bwd(x, y_pool, o, proj, target, w_out, dn_norm_w, final_norm_w)
    g_wout = _token_matmul("grad_w_out", y_t, [(dh, 0), (dh, 1)])
    dpu, dpz, g_pw, g_ps = _pool_bwd(proj, dyp, pool_w, pool_scale)
    du, dw, datt, dqd, dkd, dcd = _scan_bwd(do, vn, qd, kd, w, att, cd, st)
    dqn, dkn, dvs, dbeta, dg = _intra_bwd(qn, kn, vs, beta, g, tm, du, dw, datt, dqd, dkd, dcd)
    dcq, dck, dcv, dba, g_cw, g_sm = _conv_bwd(proj, conv_w, a_log, dt_bias, dqn, dkn, dvs, dbeta, dg)
    pieces = [dpu, dpz, dcq, dck, dcv, ddz, dba]
    g_win = _token_matmul("grad_w_in", n_t, [(p, 0) for p in pieces])
    small = dict(norm_w=jnp.zeros_like(norm_w), pool_w=g_pw, pool_scale=g_ps, conv_w=g_cw[:CONV_K],
                 a_log=g_sm[0:1, 0:N_HEADS], dt_bias=g_sm[0:1, N_HEADS:2 * N_HEADS], dn_norm_w=g_dnw, final_norm_w=g_fnw)
    return loss[0, 0], g_win, g_wout, small, dh, pieces


def _pack_small(t):
    lanes = lambda a: jnp.pad(a.reshape(1, -1), ((0, 0), (0, HEAD - a.size)))
    rows = [t["pool_w"].reshape(-1, HEAD), t["norm_w"].reshape(-1, HEAD), t["final_norm_w"].reshape(-1, HEAD),
            t["pool_scale"].reshape(-1, HEAD), t["conv_w"].reshape(-1, HEAD), t["dn_norm_w"].reshape(1, HEAD),
            lanes(t["a_log"]), lanes(t["dt_bias"]), lanes(t.get("loss", jnp.zeros((1,), F32)))]
    buf = jnp.concatenate(rows, axis=0)
    return jnp.pad(buf, ((0, SMALL_ROWS - buf.shape[0]), (0, 0)))


def _unpack_small(buf, conv_cols):
    out, r = {}, 0
    for name, nrow, shape in (("pool_w", 512, (1, N_HEADS, HEAD, HEAD)), ("norm_w", 8, (1, D_MODEL)),
                              ("final_norm_w", 8, (D_MODEL,)), ("pool_scale", 4, (1, D_HALF)),
                              ("conv_w", CONV_K * conv_cols // HEAD, (1, CONV_K, conv_cols)), ("dn_norm_w", 1, (1, HEAD))):
        out[name] = buf[r:r + nrow].reshape(shape)
        r += nrow
    out["a_log"] = buf[r:r + 1, :N_HEADS]
    out["dt_bias"] = buf[r + 1:r + 2, :N_HEADS]
    out["loss"] = buf[r + 2, 0]
    return out


def kernel(x, norm_w, w_in, pool_w, pool_scale, conv_w, a_log, dt_bias, dn_norm_w, w_out, final_norm_w, loss_target, m_norm_w, m_w_in, m_pool_w, m_pool_scale, m_conv_w, m_a_log, m_dt_bias, m_dn_norm_w, m_w_out, m_final_norm_w, v_norm_w, v_w_in, v_pool_w, v_pool_scale, v_conv_w, v_a_log, v_dt_bias, v_dn_norm_w, v_w_out, v_final_norm_w):
    cidx = lax.axis_index("c").astype(I32).reshape(1)
    jidx = (2 * lax.axis_index("x") + lax.axis_index("y")).astype(I32)

    wb = jnp.pad(w_in[0].astype(BF16), ((0, 0), (0, BLK_IN_PAD - BLK_IN)))
    ob = w_out[0].astype(BF16)
    gw, go, gc = _gather_weights(wb, ob, conv_w[0])
    mine = lambda j: jidx == j
    w_pad = jnp.concatenate([jnp.where(mine(j), wb[:, :BLK_IN], gw[j, :, :BLK_IN]) for j in range(4)]
                            + [jnp.zeros((D_MODEL, N_IN_PAD - N_IN), BF16)], axis=1)
    wo_full = jnp.where((jnp.arange(4) == jidx)[:, None, None], ob[None], go).reshape(D_MODEL, D_MODEL)
    cw_full = jnp.concatenate([jnp.where(mine(j), conv_w[0], gc[j]) for j in range(4)], axis=1)

    loss, g_win, g_wout, small, dh, pieces = _local_step(
        x[0], loss_target[0], w_pad, wo_full, cw_full, norm_w, pool_w[0], pool_scale, a_log, dt_bias, dn_norm_w,
        final_norm_w.reshape(1, D_MODEL))
    small["loss"] = loss

    blocks_in = jnp.stack([jnp.pad(g_win[:, j * BLK_IN:(j + 1) * BLK_IN].astype(BF16), ((0, 0), (0, BLK_IN_PAD - BLK_IN)))
                           for j in range(4)])
    blocks_out = g_wout.astype(BF16).reshape(4, BLK_OUT, D_MODEL)
    full = [blocks_in, blocks_out, _pack_small(small)]
    from_sib = _to_sibling_half("reduce_sibling", full)
    chip_sum = [_add_half("add_sibling_%d" % i, f, p, cidx) for i, (f, p) in enumerate(zip(full, from_sib))]
    blocked = [True, True, False]
    lands, copies = _to_other_chips(chip_sum, blocked)
    sems, chip_sum, zones, token = _exchange_start("reduce_chips_start", chip_sum, lands, copies)
    gx, g_nw = _in_bwd(x[0], dh, norm_w + token[0, 0], w_pad, pieces)
    g_nw = _allreduce_tile("reduce_norm_w", g_nw.reshape(8, HEAD)).reshape(1, D_MODEL)
    chip_sum, from_chips = _exchange_wait("reduce_chips_wait", sems, chip_sum, zones, copies, gx)
    halves = [_add_chips("add_chips_%d" % i, o, g, jidx.reshape(1), b)
              for i, (o, g, b) in enumerate(zip(chip_sum, from_chips, blocked))]
    other_halves = _to_sibling("swap_halves", halves)

    weights = dict(norm_w=norm_w, w_in=w_in, pool_w=pool_w, pool_scale=pool_scale, conv_w=conv_w, a_log=a_log,
                   dt_bias=dt_bias, dn_norm_w=dn_norm_w, w_out=w_out, final_norm_w=final_norm_w)
    ms = dict(norm_w=m_norm_w, w_in=m_w_in, pool_w=m_pool_w, pool_scale=m_pool_scale, conv_w=m_conv_w, a_log=m_a_log,
              dt_bias=m_dt_bias, dn_norm_w=m_dn_norm_w, w_out=m_w_out, final_norm_w=m_final_norm_w)
    vs = dict(norm_w=v_norm_w, w_in=v_w_in, pool_w=v_pool_w, pool_scale=v_pool_scale, conv_w=v_conv_w, a_log=v_a_log,
              dt_bias=v_dt_bias, dn_norm_w=v_dn_norm_w, w_out=v_w_out, final_norm_w=v_final_norm_w)
    names = ["norm_w", "w_in", "pool_w", "pool_scale", "conv_w", "a_log", "dt_bias", "dn_norm_w", "w_out", "final_norm_w"]
    small_names = [n for n in names if n not in ("w_in", "w_out")]

    def pack(t):
        conv = lax.dynamic_update_slice_in_dim(jnp.zeros((CONV_K, 3 * D_HALF), F32), t["conv_w"][0], jidx * BLK_CONV, axis=1)
        return _pack_small({**{n: t[n] for n in small_names if n != "conv_w"}, "conv_w": conv})[None]

    results = [{}, {}, {}, {}]
    to_tiles = lambda a: jnp.transpose(a, (2, 0, 1)).reshape(BLK_IN, 8, HEAD)
    from_tiles = lambda a: jnp.transpose(a, (1, 2, 0)).reshape(1, D_MODEL, BLK_IN)
    lo = jnp.where(cidx[0] == 0, halves[0], other_halves[0])
    hi = jnp.where(cidx[0] == 0, other_halves[0], halves[0])
    g_tiles = jnp.concatenate([lo[:, :BLK_IN].T, hi[:, :BLK_IN].T], axis=1).reshape(BLK_IN, 8, HEAD)
    outs = _adamw_tiles("adamw_w_in", to_tiles(w_in), g_tiles, to_tiles(m_w_in), to_tiles(v_w_in))
    for res, o in zip(results, (g_tiles,) + tuple(outs)):
        res["w_in"] = from_tiles(o)
    outs = _adamw_shard("adamw_w_out", w_out, halves[1], other_halves[1], cidx, m_w_out, v_w_out)
    for res, o in zip(results, outs):
        res["w_out"] = o
    outs = _adamw_shard("adamw_small", pack(weights), halves[2], other_halves[2], cidx, pack(ms), pack(vs))
    for res, o in zip(results, outs):
        got = _unpack_small(o[0], 3 * D_HALF)
        got["conv_w"] = lax.dynamic_slice_in_dim(got["conv_w"], jidx * BLK_CONV, BLK_CONV, axis=2)
        res.update(got)
    one_tile = lambda a: a.reshape(1, 8, HEAD)
    outs = _adamw_tiles("adamw_norm_w", one_tile(norm_w), one_tile(g_nw), one_tile(m_norm_w), one_tile(v_norm_w))
    for res, o in zip(results, (g_nw,) + tuple(outs)):
        res["norm_w"] = o.reshape(1, D_MODEL)
    grads, delta, new_m, new_v = results

    return (grads["loss"], gx[None], *[grads[n] for n in names], *[delta[n] for n in names],
            *[new_m[n] for n in names], *[new_v[n] for n in names])
```

```python
import functools

import jax
import jax.numpy as jnp
import numpy as np
from jax import lax
from jax.experimental import pallas as pl
from jax.experimental.pallas import tpu as pltpu

F32 = jnp.float32
BF16 = jnp.bfloat16
I32 = jnp.int32

D_MODEL = 1024
D_HALF = 512
N_HEADS = 4
HEAD = 128
CHUNK = 64
PAIR = 2 * CHUNK
WINDOWS = (2, 4, 8, 16)
CONV_K = 4
EPS = 1e-6
N_IN = 3080
N_IN_PAD = 3200
BLK_IN = 770
BLK_IN_PAD = 896
BLK_OUT = 256
BLK_CONV = 384
COL_BA = 3072
QK_SCALE = HEAD ** -0.5
SMALL_ROWS = 592
VMEM_LIMIT = 56 * 1024 * 1024

ADAM_LR = 0.001
ADAM_B1 = 0.9
ADAM_B2 = 0.999
ADAM_EPS = 1e-08
ADAM_WD = 0.01
ADAM_STEP = 10

CHIP_MASKS = (2, 1, 3)
HEADS = range(N_HEADS)
HEAD_COLS = [slice(h * HEAD, (h + 1) * HEAD) for h in HEADS]


def _call(body, **kw):
    return pl.pallas_call(body, **kw)


def _params(*sem):
    return pltpu.CompilerParams(dimension_semantics=sem, vmem_limit_bytes=VMEM_LIMIT)


def _sds(shape, dtype=F32):
    return jax.ShapeDtypeStruct(shape, dtype)


def _bdot(a, b):
    return jnp.dot(a.astype(BF16), b.astype(BF16), preferred_element_type=F32)


def _bdot_nt(a, b):
    return lax.dot_general(a.astype(BF16), b.astype(BF16), (((1,), (1,)), ((), ())), preferred_element_type=F32)


def _bdot_tn(a, b):
    return lax.dot_general(a.astype(BF16), b.astype(BF16), (((0,), (0,)), ((), ())), preferred_element_type=F32)


def _split(a):
    hi = a.astype(BF16)
    lo = (a - hi.astype(F32)).astype(BF16)
    return hi, lo


def _mask_dot(m, b, dims=(((1,), (0,)), ((), ()))):
    bh, bl = _split(b)
    dg = functools.partial(lax.dot_general, dimension_numbers=dims, preferred_element_type=F32)
    return dg(m, bh) + dg(m, bl)


def _sigmoid(x):
    return 0.5 * jnp.tanh(0.5 * x) + 0.5


def _softplus(x):
    return jnp.maximum(x, 0.0) + jnp.log(1.0 + jnp.exp(-jnp.abs(x)))


def _rowsum(x):
    return jnp.sum(x, axis=-1, keepdims=True)


def _colsum(x):
    return jnp.sum(x, axis=0, keepdims=True)


def _shift_down(xv, prev8, k):
    r = pltpu.roll(xv, k, 0)
    q = pltpu.roll(prev8, k, 0)
    row = lax.broadcasted_iota(I32, prev8.shape, 0)
    top = jnp.where(row < k, q, r[0:8])
    return jnp.concatenate([top, r[8:]], axis=0)


def _shift_up(xv, next8, k):
    t = xv.shape[0]
    r = pltpu.roll(xv, t - k, 0)
    q = pltpu.roll(next8, 8 - k, 0)
    row = lax.broadcasted_iota(I32, next8.shape, 0)
    bot = jnp.where(row >= 8 - k, q, r[t - 8:])
    return jnp.concatenate([r[:t - 8], bot], axis=0)


def _band(rows, cols, off, w, anti=False):
    r = lax.broadcasted_iota(I32, (rows, cols), 0)
    c = lax.broadcasted_iota(I32, (rows, cols), 1)
    d = (c - r + off) if anti else (r - c + off)
    return ((d >= 0) & (d < w)).astype(BF16)


def _head(ref_or_val, h):
    return ref_or_val[:, h * HEAD:(h + 1) * HEAD]


INTRA_PAIRS = 2
UNITS = [(pp, h) for pp in range(INTRA_PAIRS) for h in HEADS]


def _heads(ref, rows=PAIR):
    return [ref[pp * rows:(pp + 1) * rows, HEAD_COLS[h]] for pp, h in UNITS]


def _put_heads(ref, vals, rows=PAIR):
    for (pp, h), v in zip(UNITS, vals):
        ref[pp * rows:(pp + 1) * rows, HEAD_COLS[h]] = v.astype(ref.dtype)


def _each(fn, *lists):
    return [fn(*args) for args in zip(*lists)]


def _proj_fwd(x, norm_w, w_pad):
    s = x.shape[0]
    tm = 512

    def body(x_ref, nw_ref, w_ref, proj_ref, nt_ref):
        xv = x_ref[...]
        r = lax.rsqrt(jnp.mean(xv * xv, axis=-1, keepdims=True) + EPS)
        nv = xv * r * nw_ref[...]
        nt_ref[...] = nv.T.astype(BF16)
        proj_ref[...] = jnp.dot(nv.astype(BF16), w_ref[...], preferred_element_type=F32)

    return _call(
        body, name="proj_fwd", grid=(s // tm,),
        in_specs=[pl.BlockSpec((tm, D_MODEL), lambda i: (i, 0)),
                  pl.BlockSpec((1, D_MODEL), lambda i: (0, 0)),
                  pl.BlockSpec((D_MODEL, N_IN_PAD), lambda i: (0, 0))],
        out_specs=[pl.BlockSpec((tm, N_IN_PAD), lambda i: (i, 0)),
                   pl.BlockSpec((D_MODEL, tm), lambda i: (0, i))],
        out_shape=[_sds((s, N_IN_PAD)), _sds((D_MODEL, s), BF16)],
        compiler_params=_params("arbitrary"),
    )(x, norm_w, w_pad)


def _pool_bands(t, anti=False):
    r = np.arange(t)[:, None]
    c = np.arange(t + HEAD)[None, :]
    d = (c - r) if anti else (r - c + HEAD)
    return jnp.asarray(np.stack([(d >= 0) & (d < w) for w in WINDOWS]), BF16)


def _pool_mix(ug, hg, zg, pw_g, band, row0, w):
    t = ug.shape[0]
    win = _mask_dot(band, jnp.concatenate([hg, ug], axis=0))
    cnt = jnp.minimum(row0 + lax.broadcasted_iota(I32, (t, 1), 0) + 1, w).astype(F32)
    mix = win / cnt - ug
    mixed = _bdot(mix, pw_g)
    sg = _sigmoid(zg)
    return mix, mixed, sg, cnt


POOL_T = 256


def _pool_fwd(proj, pool_w, pool_scale):
    s = proj.shape[0]
    t = POOL_T
    hb = t // HEAD

    def body(u_ref, z_ref, halo_ref, pw_ref, ps_ref, band_ref, y_ref):
        i = pl.program_id(0)
        live = (i > 0).astype(F32)
        for g, w in enumerate(WINDOWS):
            sl = HEAD_COLS[g]
            zg = z_ref[:, sl]
            _, mixed, sg, _ = _pool_mix(u_ref[:, sl], halo_ref[:, sl] * live, zg, pw_ref[g], band_ref[g], i * t, w)
            y_ref[:, sl] = mixed * ps_ref[:, sl] * (zg * sg)

    return _call(
        body, name="pool_fwd", grid=(s // t,),
        in_specs=[pl.BlockSpec((t, D_HALF), lambda i: (i, 0)),
                  pl.BlockSpec((t, D_HALF), lambda i: (i, 1)),
                  pl.BlockSpec((HEAD, D_HALF), lambda i: (jnp.maximum(i * hb - 1, 0), 0)),
                  pl.BlockSpec((N_HEADS, HEAD, HEAD), lambda i: (0, 0, 0)),
                  pl.BlockSpec((1, D_HALF), lambda i: (0, 0)),
                  pl.BlockSpec((N_HEADS, t, HEAD + t), lambda i: (0, 0, 0))],
        out_specs=pl.BlockSpec((t, D_HALF), lambda i: (i, 0)),
        out_shape=_sds((s, D_HALF)),
        compiler_params=_params("arbitrary"),
    )(proj, proj, proj, pool_w, pool_scale, _pool_bands(t))


def _conv_taps(xv, prev8):
    return [_shift_down(xv, prev8, CONV_K - 1 - j) for j in range(CONV_K - 1)] + [xv]


def _conv_pre(taps, cw):
    y = taps[CONV_K - 1] * cw[CONV_K - 1:CONV_K]
    for j in range(CONV_K - 2, -1, -1):
        y = y + taps[j] * cw[j:j + 1]
    return y


CONV_T = 256
CONV_SUB = 256


def _conv_specs(t, tile_of=lambda i: i):
    tiles = [pl.BlockSpec((t, D_HALF), functools.partial(lambda i, p: (tile_of(i), 2 + p), p=p)) for p in range(3)]
    halos = [pl.BlockSpec((8, D_HALF),
                          functools.partial(lambda i, p: (jnp.maximum(tile_of(i) * (t // 8) - 1, 0), 2 + p), p=p))
             for p in range(3)]
    return tiles + halos


def _conv_fwd(proj, conv_w, a_log, dt_bias):
    s = proj.shape[0]
    t = CONV_T

    def body(q_ref, k_ref, v_ref, hq_ref, hk_ref, hv_ref, ba_ref, cw_ref, al_ref, dtb_ref,
             qn_ref, kn_ref, vs_ref, beta_ref, g_ref):
        live = (pl.program_id(0) > 0).astype(F32)
        parts = ((q_ref, hq_ref, qn_ref), (k_ref, hk_ref, kn_ref), (v_ref, hv_ref, vs_ref))

        def sub_tile(r0, first):
            rows = pl.ds(r0, CONV_SUB)
            for p, (x_ref, h_ref, o_ref) in enumerate(parts):
                for h in HEADS:
                    cs = HEAD_COLS[h]
                    prev8 = h_ref[:, cs] * live if first else x_ref[pl.ds(r0 - 8, 8), cs]
                    y = _conv_pre(_conv_taps(x_ref[rows, cs], prev8), cw_ref[:, p * D_HALF + h * HEAD:p * D_HALF + (h + 1) * HEAD])
                    sv = y * _sigmoid(y)
                    o_ref[rows, cs] = sv if p == 2 else sv * lax.rsqrt(_rowsum(sv * sv) + EPS)
            ba = ba_ref[rows, :]
            for h in HEADS:
                beta = _sigmoid(ba[:, h:h + 1])
                gl = -jnp.exp(al_ref[0:1, h:h + 1]) * _softplus(ba[:, N_HEADS + h:N_HEADS + h + 1] + dtb_ref[0:1, h:h + 1])
                beta_ref[rows, HEAD_COLS[h]] = jnp.broadcast_to(beta, (CONV_SUB, HEAD))
                g_ref[rows, HEAD_COLS[h]] = jnp.broadcast_to(gl, (CONV_SUB, HEAD))

        sub_tile(0, True)

        def step(k, carry):
            sub_tile(pl.multiple_of(k * CONV_SUB, CONV_SUB), False)
            return carry

        lax.fori_loop(1, t // CONV_SUB, step, 0)

    row = pl.BlockSpec((t, D_HALF), lambda i: (i, 0))
    return _call(
        body, name="conv_fwd", grid=(s // t,),
        in_specs=_conv_specs(t) + [pl.BlockSpec((t, HEAD), lambda i: (i, COL_BA // HEAD)),
                                   pl.BlockSpec((CONV_K, 3 * D_HALF), lambda i: (0, 0)),
                                   pl.BlockSpec((1, N_HEADS), lambda i: (0, 0)),
                                   pl.BlockSpec((1, N_HEADS), lambda i: (0, 0))],
        out_specs=[row] * 5,
        out_shape=[_sds((s, D_HALF))] * 5,
        compiler_params=_params("arbitrary"),
    )(proj, proj, proj, proj, proj, proj, proj, conv_w, a_log, dt_bias)


def _pair_masks():
    r = lax.broadcasted_iota(I32, (PAIR, PAIR), 0)
    c = lax.broadcasted_iota(I32, (PAIR, PAIR), 1)
    same = jnp.right_shift(r, 6) == jnp.right_shift(c, 6)
    return same, same & (r >= c), same & (r > c), r == c


def _pair_common(qn, kn, vs, beta, g):
    same, incl, strict, eye = _pair_masks()
    incl_b = incl.astype(BF16)
    first = lax.broadcasted_iota(I32, (PAIR, HEAD), 0) < CHUNK
    gc = _each(lambda gv: _mask_dot(incl_b, gv), g)
    gc_row = _each(lambda v: _colsum(jnp.where(eye, v, 0.0)), gc)
    decay = _each(lambda v, r: jnp.where(incl, jnp.exp(jnp.where(incl, v - r, 0.0)), 0.0), gc, gc_row)
    gl = _each(lambda v: jnp.where(first, v[CHUNK - 1:CHUNK], v[PAIR - 1:PAIR]), gc)
    egc = _each(jnp.exp, gc)
    q = _each(lambda v: v * QK_SCALE, qn)
    kb = _each(lambda k, b: k * b, kn, beta)
    return dict(same=same, incl=incl, strict=strict, eye=eye, gc=gc, decay=decay, gl=gl, egc=egc,
                ekd=_each(lambda a, b: jnp.exp(a - b), gl, gc), cd=_each(jnp.exp, gl), q=q, kb=kb,
                vb=_each(lambda v, b: v * b, vs, beta), kbg=_each(lambda k, e: k * e, kb, egc),
                kk=_each(_bdot_nt, kb, kn), qk=_each(_bdot_nt, q, kn))


def _tri_inv(a, eye_f):
    p = _each(lambda v: eye_f - v, a)
    x = _each(_bdot, a, a)
    for it in range(5):
        p = _each(lambda pv, xv: pv + _bdot(pv, xv), p, x)
        if it < 4:
            x = _each(_bdot, x, x)
    return p


def _pair_spec():
    return pl.BlockSpec((INTRA_PAIRS * PAIR, D_HALF), lambda i: (i, 0))


def _chunk_scalar_spec(pairs=1, index=lambda i: (i, 0)):
    return pl.BlockSpec((16 * pairs, D_HALF), index)


SCAN_PAIRS = 2
SCAN_ROWS = SCAN_PAIRS * PAIR


def _intra_fwd(qn, kn, vs, beta, g):
    s = qn.shape[0]

    def body(qn_ref, kn_ref, vs_ref, beta_ref, g_ref, u_ref, w_ref, att_ref, qd_ref, kd_ref, t_ref, cd_ref):
        kn = _heads(kn_ref)
        cm = _pair_common(_heads(qn_ref), kn, _heads(vs_ref), _heads(beta_ref), _heads(g_ref))
        a = _each(lambda kk, d: jnp.where(cm["strict"], kk * d, 0.0), cm["kk"], cm["decay"])
        tm = _tri_inv(a, cm["eye"].astype(F32))
        _put_heads(t_ref, tm)
        _put_heads(u_ref, _each(_bdot, tm, cm["vb"]))
        _put_heads(w_ref, _each(_bdot, tm, cm["kbg"]))
        _put_heads(att_ref, _each(lambda a, b: a * b, cm["qk"], cm["decay"]))
        _put_heads(qd_ref, _each(lambda a, b: a * b, cm["q"], cm["egc"]))
        _put_heads(kd_ref, _each(lambda a, b: a * b, kn, cm["ekd"]))
        for ci in range(2):
            for (pp, h), v in zip(UNITS, cm["cd"]):
                cd_ref[pp * 16 + ci * 8:pp * 16 + (ci + 1) * 8, HEAD_COLS[h]] = v[ci * CHUNK:ci * CHUNK + 8]

    return _call(
        body, name="intra_fwd", grid=(s // (INTRA_PAIRS * PAIR),),
        in_specs=[_pair_spec()] * 5, out_specs=[_pair_spec()] * 6 + [_chunk_scalar_spec(INTRA_PAIRS)],
        out_shape=[_sds((s, D_HALF))] + [_sds((s, D_HALF), BF16)] * 5 + [_sds((s // 8, D_HALF))],
        compiler_params=_params("arbitrary"),
    )(qn, kn, vs, beta, g)


def _scan_fwd(u, w, att, qd, kd, cd):
    s = u.shape[0]
    n_chunks = s // CHUNK

    def body(u_ref, w_ref, att_ref, qd_ref, kd_ref, cd_ref, o_ref, vn_ref, st_ref, state):
        @pl.when(pl.program_id(0) == 0)
        def _():
            state[...] = jnp.zeros_like(state)
        cols = list(enumerate(HEAD_COLS))
        sm = [state[h] for h in HEADS]
        for ci in range(2 * SCAN_PAIRS):
            rs = slice(ci * CHUNK, (ci + 1) * CHUNK)
            for h in HEADS:
                st_ref[ci, h] = sm[h]
            both = [_bdot(jnp.concatenate([w_ref[rs, sl], qd_ref[rs, sl]], axis=0), sm[h]) for h, sl in cols]
            vn = [u_ref[rs, sl] - both[h][:CHUNK] for h, sl in cols]
            for h, sl in cols:
                vn_ref[rs, sl] = vn[h].astype(BF16)
                o_ref[rs, sl] = both[h][CHUNK:]
            sm = [sm[h] * cd_ref[ci * 8:ci * 8 + 1, sl] + _bdot_tn(kd_ref[rs, sl], vn[h]) for h, sl in cols]
        for h in HEADS:
            state[h] = sm[h]
        for pp in range(SCAN_PAIRS):
            rp = slice(pp * PAIR, (pp + 1) * PAIR)
            intra = [_bdot(att_ref[rp, sl], vn_ref[rp, sl]) for sl in HEAD_COLS]
            for h, sl in cols:
                o_ref[rp, sl] += intra[h]

    rows = pl.BlockSpec((SCAN_ROWS, D_HALF), lambda i: (i, 0))
    return _call(
        body, name="scan_fwd", grid=(s // SCAN_ROWS,),
        in_specs=[rows] * 5 + [_chunk_scalar_spec(SCAN_PAIRS)],
        out_specs=[rows, rows, pl.BlockSpec((2 * SCAN_PAIRS, N_HEADS, HEAD, HEAD), lambda i: (i, 0, 0, 0))],
        out_shape=[_sds((s, D_HALF)), _sds((s, D_HALF), BF16), _sds((n_chunks, N_HEADS, HEAD, HEAD))],
        scratch_shapes=[pltpu.VMEM((N_HEADS, HEAD, HEAD), F32)],
        compiler_params=_params("arbitrary"),
    )(u, w, att, qd, kd, cd)


OUT_T = 512


def _out_fwd_bwd(x, y_pool, o, proj, target, w_out, dn_norm_w, final_norm_w):
    s = x.shape[0]
    t = OUT_T

    def body(x_ref, yp_ref, o_ref, z_ref, tg_ref, wo_ref, dnw_ref, fnw_ref,
             yt_ref, dh_ref, dyp_ref, do_ref, dz_ref, loss_ref, gfn_ref, gdn_ref, y_ref):
        @pl.when(pl.program_id(0) == 0)
        def _():
            loss_ref[...] = jnp.zeros_like(loss_ref)
            gfn_ref[...] = jnp.zeros_like(gfn_ref)
            gdn_ref[...] = jnp.zeros_like(gdn_ref)

        ypv = yp_ref[...]
        y_ref[:, :D_HALF] = ypv.astype(BF16)
        yt_ref[:D_HALF, :] = ypv.T.astype(BF16)
        dnw = dnw_ref[...]
        keep = []
        for h in HEADS:
            ov = o_ref[:, HEAD_COLS[h]]
            zv = z_ref[:, HEAD_COLS[h]]
            ro = lax.rsqrt(jnp.mean(ov * ov, axis=-1, keepdims=True) + EPS)
            ohat = ov * ro
            sg = _sigmoid(zv)
            keep.append((ro, ohat, zv, sg))
            ydn = ohat * dnw * (zv * sg)
            y_ref[:, D_HALF + h * HEAD:D_HALF + (h + 1) * HEAD] = ydn.astype(BF16)
            yt_ref[D_HALF + h * HEAD:D_HALF + (h + 1) * HEAD, :] = ydn.T.astype(BF16)

        hv = x_ref[...] + jnp.dot(y_ref[...], wo_ref[...], preferred_element_type=F32)
        r2 = lax.rsqrt(jnp.mean(hv * hv, axis=-1, keepdims=True) + EPS)
        hhat = hv * r2
        fnw = fnw_ref[...]
        err = hhat * fnw - tg_ref[...]
        loss_ref[...] += 0.5 * jnp.sum(_rowsum(err * err) * (1.0 / D_MODEL), axis=0, keepdims=True)
        dout = err * (1.0 / D_MODEL)
        gfn_ref[...] += _colsum(dout * hhat)
        dhh = dout * fnw
        dh = r2 * (dhh - hhat * jnp.mean(dhh * hhat, axis=-1, keepdims=True))
        dh_ref[...] = dh
        dy = _bdot_nt(dh, wo_ref[...])
        dyp_ref[...] = dy[:, :D_HALF]
        gdn = jnp.zeros((1, HEAD), F32)
        for h in HEADS:
            ro, ohat, zv, sg = keep[h]
            dyd = dy[:, D_HALF + h * HEAD:D_HALF + (h + 1) * HEAD]
            sz = zv * sg
            dz_ref[:, HEAD_COLS[h]] = (dyd * ohat * dnw * (sg * (1.0 + zv * (1.0 - sg)))).astype(BF16)
            gdn = gdn + _colsum(dyd * ohat * sz)
            doh = dyd * dnw * sz
            do_ref[:, HEAD_COLS[h]] = ro * (doh - ohat * jnp.mean(doh * ohat, axis=-1, keepdims=True))
        gdn_ref[...] += gdn

    wide = pl.BlockSpec((t, D_MODEL), lambda i: (i, 0))
    half = pl.BlockSpec((t, D_HALF), lambda i: (i, 0))
    const = lambda shape: pl.BlockSpec(shape, lambda i: (0,) * len(shape))
    return _call(
        body, name="out_fwd_bwd", grid=(s // t,),
        in_specs=[wide, half, half, pl.BlockSpec((t, D_HALF), lambda i: (i, 5)), wide,
                  const((D_MODEL, D_MODEL)), const((1, HEAD)), const((1, D_MODEL))],
        out_specs=[pl.BlockSpec((D_MODEL, t), lambda i: (0, i)), wide, half, half, half,
                   const((1, HEAD)), const((1, D_MODEL)), const((1, HEAD))],
        out_shape=[_sds((D_MODEL, s), BF16), _sds((s, D_MODEL)), _sds((s, D_HALF)), _sds((s, D_HALF)), _sds((s, D_HALF), BF16),
                   _sds((1, HEAD)), _sds((1, D_MODEL)), _sds((1, HEAD))],
        scratch_shapes=[pltpu.VMEM((t, D_MODEL), BF16)],
        compiler_params=_params("arbitrary"),
    )(x, y_pool, o, proj, target, w_out, dn_norm_w, final_norm_w)


def _token_matmul(name, at, pieces):
    m, s = at.shape
    n = len(pieces)
    tn, tk = D_HALF, 512

    def body(a_ref, *refs):
        p_refs, o_ref = refs[:n], refs[n]

        @pl.when(pl.program_id(0) == 0)
        def _():
            o_ref[...] = jnp.zeros_like(o_ref)

        av = a_ref[...]
        for p in range(n):
            o_ref[:, p * tn:(p + 1) * tn] += _bdot(av, p_refs[p][...])

    return _call(
        body, name=name, grid=(s // tk,),
        in_specs=[pl.BlockSpec((m, tk), lambda k: (0, k))]
                 + [pl.BlockSpec((tk, tn), functools.partial(lambda k, cb: (k, cb), cb=cb)) for _, cb in pieces],
        out_specs=pl.BlockSpec((m, n * tn), lambda k: (0, 0)),
        out_shape=_sds((m, n * tn)),
        compiler_params=_params("arbitrary"),
    )(at, *[p[0] for p in pieces])


def _pool_bwd(proj, dyp, pool_w, pool_scale):
    s = proj.shape[0]
    t = POOL_T
    hb = t // HEAD
    last = s // HEAD - 1

    def body(u_ref, z_ref, halo_ref, dy_ref, zn_ref, dyn_ref, pw_ref, ps_ref, band_ref, aband_ref,
             du_ref, dz_ref, gpw_ref, gps_ref):
        i = pl.program_id(0)

        @pl.when(i == 0)
        def _():
            gpw_ref[...] = jnp.zeros_like(gpw_ref)
            gps_ref[...] = jnp.zeros_like(gps_ref)

        live = (i > 0).astype(F32)
        more = (i < pl.num_programs(0) - 1).astype(F32)
        for g, w in enumerate(WINDOWS):
            sl = HEAD_COLS[g]
            zg = z_ref[:, sl]
            ps = ps_ref[:, sl]
            pw = pw_ref[g]
            mix, mixed, sg, cnt = _pool_mix(u_ref[:, sl], halo_ref[:, sl] * live, zg, pw, band_ref[g], i * t, w)
            dyg = dy_ref[:, sl]
            sz = zg * sg
            dz_ref[:, sl] = (dyg * mixed * ps * (sg * (1.0 + zg * (1.0 - sg)))).astype(BF16)
            gps_ref[:, sl] += _colsum(dyg * mixed * sz)
            dmixed = dyg * ps * sz
            gpw_ref[g] += _bdot_tn(mix, dmixed)
            dmix = _bdot_nt(dmixed, pw)
            zn = zn_ref[:, sl]
            dmix_n = _bdot_nt(dyn_ref[:, sl] * more * ps * (zn * _sigmoid(zn)), pw)
            scaled = jnp.concatenate([dmix / cnt, dmix_n * (1.0 / w)], axis=0)
            du_ref[:, sl] = (_mask_dot(aband_ref[g], scaled) - dmix).astype(BF16)

    tile = lambda col: pl.BlockSpec((t, D_HALF), lambda i: (i, col))
    below = lambda col: pl.BlockSpec((HEAD, D_HALF), lambda i: (jnp.minimum((i + 1) * hb, last), col))
    return _call(
        body, name="pool_bwd", grid=(s // t,),
        in_specs=[tile(0), tile(1), pl.BlockSpec((HEAD, D_HALF), lambda i: (jnp.maximum(i * hb - 1, 0), 0)),
                  tile(0), below(1), below(0),
                  pl.BlockSpec((N_HEADS, HEAD, HEAD), lambda i: (0, 0, 0)), pl.BlockSpec((1, D_HALF), lambda i: (0, 0)),
                  pl.BlockSpec((N_HEADS, t, HEAD + t), lambda i: (0, 0, 0)),
                  pl.BlockSpec((N_HEADS, t, HEAD + t), lambda i: (0, 0, 0))],
        out_specs=[tile(0), tile(0), pl.BlockSpec((N_HEADS, HEAD, HEAD), lambda i: (0, 0, 0)),
                   pl.BlockSpec((1, D_HALF), lambda i: (0, 0))],
        out_shape=[_sds((s, D_HALF), BF16), _sds((s, D_HALF), BF16), _sds((N_HEADS, HEAD, HEAD)), _sds((1, D_HALF))],
        compiler_params=_params("arbitrary"),
    )(proj, proj, proj, dyp, proj, dyp, pool_w, pool_scale, _pool_bands(t), _pool_bands(t, anti=True))


def _scan_bwd(do, vn, qd, kd, w, att, cd, st):
    s = do.shape[0]
    n_steps = s // SCAN_ROWS

    def body(do_ref, vn_ref, qd_ref, kd_ref, w_ref, att_ref, cd_ref, st_ref,
             du_ref, dw_ref, datt_ref, dqd_ref, dkd_ref, dcd_ref, dstate):
        @pl.when(pl.program_id(0) == 0)
        def _():
            dstate[...] = jnp.zeros_like(dstate)
        _, incl, _, _ = _pair_masks()
        cols = list(enumerate(HEAD_COLS))
        dv_intra = []
        for pp in range(SCAN_PAIRS):
            rp = slice(pp * PAIR, (pp + 1) * PAIR)
            dv_intra.append([_bdot_tn(att_ref[rp, sl], do_ref[rp, sl]) for _, sl in cols])
            for _, sl in cols:
                datt_ref[rp, sl] = jnp.where(incl, _bdot_nt(do_ref[rp, sl], vn_ref[rp, sl]), 0.0)
        ds = [dstate[h] for h in HEADS]
        for ci in range(2 * SCAN_PAIRS - 1, -1, -1):
            rs = slice(ci * CHUNK, (ci + 1) * CHUNK)
            in_pair = slice((ci % 2) * CHUNK, (ci % 2 + 1) * CHUNK)
            sm = [st_ref[ci, h] for h in HEADS]
            dvn = [dv_intra[ci // 2][h][in_pair] + _bdot(kd_ref[rs, sl], ds[h]) for h, sl in cols]
            for h, sl in cols:
                du_ref[rs, sl] = dvn[h].astype(BF16)
            dqd = [_bdot_nt(do_ref[rs, sl], sm[h]) for h, sl in cols]
            dw = [-_bdot_nt(dvn[h], sm[h]) for h, _ in cols]
            dkd = [_bdot_nt(vn_ref[rs, sl], ds[h]) for h, sl in cols]
            dcd = [jnp.broadcast_to(_rowsum(_colsum(ds[h] * sm[h])), (8, HEAD)) for h in HEADS]
            for h, sl in cols:
                dqd_ref[rs, sl] = dqd[h]
                dw_ref[rs, sl] = dw[h].astype(BF16)
                dkd_ref[rs, sl] = dkd[h]
                dcd_ref[ci * 8:(ci + 1) * 8, sl] = dcd[h]
            ds = [ds[h] * cd_ref[ci * 8:ci * 8 + 1, sl] + _bdot_tn(qd_ref[rs, sl], do_ref[rs, sl])
                  - _bdot_tn(w_ref[rs, sl], dvn[h]) for h, sl in cols]
        for h in HEADS:
            dstate[h] = ds[h]

    rev = pl.BlockSpec((SCAN_ROWS, D_HALF), lambda i: (n_steps - 1 - i, 0))
    rev_scalar = _chunk_scalar_spec(SCAN_PAIRS, lambda i: (n_steps - 1 - i, 0))
    return _call(
        body, name="scan_bwd", grid=(n_steps,),
        in_specs=[rev] * 6 + [rev_scalar,
                              pl.BlockSpec((2 * SCAN_PAIRS, N_HEADS, HEAD, HEAD), lambda i: (n_steps - 1 - i, 0, 0, 0))],
        out_specs=[rev] * 5 + [rev_scalar],
        out_shape=[_sds((s, D_HALF), BF16)] * 2 + [_sds((s, D_HALF))] * 3 + [_sds((s // 8, D_HALF))],
        scratch_shapes=[pltpu.VMEM((N_HEADS, HEAD, HEAD), F32)],
        compiler_params=_params("arbitrary"),
    )(do, vn, qd, kd, w, att, cd, st)


def _intra_bwd(qn, kn, vs, beta, g, tm, du, dw, datt, dqd, dkd, dcd):
    s = qn.shape[0]

    def body(qn_ref, kn_ref, vs_ref, beta_ref, g_ref, t_ref, du_ref, dw_ref, datt_ref, dqd_ref, dkd_ref, dcd_ref,
             dqn_ref, dkn_ref, dvs_ref, dbeta_ref, dg_ref):
        ones = jnp.ones((PAIR, HEAD), BF16)
        tn = (((0,), (0,)), ((), ()))
        kn, vs, beta = _heads(kn_ref), _heads(vs_ref), _heads(beta_ref)
        cm = _pair_common(_heads(qn_ref), kn, vs, beta, _heads(g_ref))
        tmv, duv, dwv, dattv, dqdv, dkdv = (_heads(r) for r in (t_ref, du_ref, dw_ref, datt_ref, dqd_ref, dkd_ref))
        dvb = _each(_bdot_tn, tmv, duv)
        dt = _each(lambda a, b, c, d: _bdot_nt(a, b) + _bdot_nt(c, d), duv, cm["vb"], dwv, cm["kbg"])
        dkbg = _each(_bdot_tn, tmv, dwv)
        m1 = _each(_bdot_tn, tmv, dt)
        da = _each(lambda a, b: -jnp.where(cm["strict"], _bdot_nt(a, b), 0.0), m1, tmv)
        dkk = _each(lambda a, b: a * b, da, cm["decay"])
        dqk = _each(lambda a, b: a * b, dattv, cm["decay"])
        dd = _each(lambda a, b, c, d: a * b + c * d, dkk, cm["kk"], dqk, cm["qk"])
        dkb = _each(lambda a, b, c, d: _bdot(a, b) + c * d, dkk, kn, dkbg, cm["egc"])
        dq = _each(lambda a, b, c, d: _bdot(a, b) + c * d, dqk, kn, dqdv, cm["egc"])
        dkn = _each(lambda a, b, c, d: _bdot_tn(a, b) + _bdot_tn(c, d), dkk, cm["kb"], dqk, cm["q"])
        dkn = _each(lambda a, b, c, d, e: a + b * c + d * e, dkn, dkdv, cm["ekd"], dkb, beta)
        t_kd = _each(lambda a, b, c: _rowsum(a * b * c), dkdv, kn, cm["ekd"])
        split = _each(_split, dd)
        rows_dd = [jnp.dot(hi, ones, preferred_element_type=F32) + jnp.dot(lo, ones, preferred_element_type=F32)
                   for hi, lo in split]
        cols_dd = [lax.dot_general(hi, ones, tn, preferred_element_type=F32)
                   + lax.dot_general(lo, ones, tn, preferred_element_type=F32) for hi, lo in split]
        dgc = _each(lambda r, c, a, b, e, f, k, t: r - c + _rowsum(a * b * e) + _rowsum(f * k) - t,
                    rows_dd, cols_dd, dqdv, cm["q"], cm["egc"], dkbg, cm["kbg"], t_kd)
        same_b = cm["same"].astype(BF16)
        rowi = lax.broadcasted_iota(I32, (PAIR, HEAD), 0)
        dcd = _each(lambda d: jnp.where(rowi < CHUNK, d[0:1], d[8:9]), _heads(dcd_ref, rows=16))
        dgl = _each(lambda t, d, c: _mask_dot(same_b, jnp.broadcast_to(t, (PAIR, HEAD))) + d * c, t_kd, dcd, cm["cd"])
        is_last = jnp.bitwise_and(rowi, CHUNK - 1) == CHUNK - 1
        dgc = _each(lambda a, b: a + jnp.where(is_last, b, 0.0), dgc, dgl)
        r = lax.broadcasted_iota(I32, (PAIR, PAIR), 0)
        c = lax.broadcasted_iota(I32, (PAIR, PAIR), 1)
        upper_b = (cm["same"] & (r <= c)).astype(BF16)
        _put_heads(dg_ref, _each(lambda v: _mask_dot(upper_b, v), dgc))
        _put_heads(dbeta_ref, _each(lambda a, b, c, d: jnp.broadcast_to(_rowsum(a * b) + _rowsum(c * d), (PAIR, HEAD)),
                                    dkb, kn, dvb, vs))
        _put_heads(dqn_ref, _each(lambda v: v * QK_SCALE, dq))
        _put_heads(dkn_ref, dkn)
        _put_heads(dvs_ref, _each(lambda a, b: a * b, dvb, beta))

    return _call(
        body, name="intra_bwd", grid=(s // (INTRA_PAIRS * PAIR),),
        in_specs=[_pair_spec()] * 11 + [_chunk_scalar_spec(INTRA_PAIRS)], out_specs=[_pair_spec()] * 5,
        out_shape=[_sds((s, D_HALF))] * 5,
        compiler_params=_params("arbitrary"),
    )(qn, kn, vs, beta, g, tm, du, dw, datt, dqd, dkd, dcd)


def _rows8(x):
    acc = x[0:8]
    for r in range(8, x.shape[0], 8):
        acc = acc + x[r:r + 8]
    return acc


def _conv_bwd(proj, conv_w, a_log, dt_bias, dqn, dkn, dvs, dbeta, dg):
    s = proj.shape[0]
    t = CONV_T
    n_tiles = s // t
    n_sub = t // CONV_SUB
    tile_of = lambda i: n_tiles - 1 - i

    def body(q_ref, k_ref, v_ref, hq_ref, hk_ref, hv_ref, ba_ref, cw_ref, al_ref, dtb_ref,
             dqn_ref, dkn_ref, dvs_ref, dbeta_ref, dg_ref, oq_ref, ok_ref, ov_ref, dba_ref, gcw_out, gsm_out,
             below, gcw_ref, gsm_ref):
        @pl.when(pl.program_id(0) == 0)
        def _():
            gcw_ref[...] = jnp.zeros_like(gcw_ref)
            gsm_ref[...] = jnp.zeros_like(gsm_ref)
            below[...] = jnp.zeros_like(below)

        live = (pl.program_id(0) < n_tiles - 1).astype(F32)
        parts = ((q_ref, hq_ref, dqn_ref, oq_ref), (k_ref, hk_ref, dkn_ref, ok_ref), (v_ref, hv_ref, dvs_ref, ov_ref))
        lane = lax.broadcasted_iota(I32, (CONV_SUB, HEAD), 1)
        lane8 = lax.broadcasted_iota(I32, (8, HEAD), 1)

        def sub_tile(r0, first):
            rows = pl.ds(r0, CONV_SUB)
            for p, (x_ref, h_ref, d_ref, o_ref) in enumerate(parts):
                for h in HEADS:
                    cs = HEAD_COLS[h]
                    wide = slice(p * D_HALF + h * HEAD, p * D_HALF + (h + 1) * HEAD)
                    cw = cw_ref[:, wide]
                    prev8 = h_ref[:, cs] * live if first else x_ref[pl.ds(r0 - 8, 8), cs]
                    taps = _conv_taps(x_ref[rows, cs], prev8)
                    y = _conv_pre(taps, cw)
                    sg = _sigmoid(y)
                    sv = y * sg
                    ds = d_ref[rows, cs]
                    if p < 2:
                        rn = lax.rsqrt(_rowsum(sv * sv) + EPS)
                        nrm = sv * rn
                        ds = rn * (ds - nrm * _rowsum(ds * nrm))
                    dy = ds * (sg * (1.0 + y * (1.0 - sg)))
                    for j in range(CONV_K):
                        gcw_ref[8 * j:8 * j + 8, wide] += _rows8(dy * taps[j])
                    nxt = below[:, wide]
                    acc = dy * cw[CONV_K - 1:CONV_K]
                    for sft in range(1, CONV_K):
                        acc = acc + _shift_up(dy, nxt, sft) * cw[CONV_K - 1 - sft:CONV_K - sft]
                    o_ref[rows, cs] = acc.astype(BF16)
                    below[:, wide] = dy[0:8]

            ba = ba_ref[rows, :]
            dba = jnp.zeros((CONV_SUB, HEAD), F32)
            gsm = jnp.zeros((8, HEAD), F32)
            for h in HEADS:
                beta = _sigmoid(ba[:, h:h + 1])
                dbeta = dbeta_ref[rows, h * HEAD:h * HEAD + 1]
                xg = ba[:, N_HEADS + h:N_HEADS + h + 1] + dtb_ref[0:1, h:h + 1]
                nexp = -jnp.exp(al_ref[0:1, h:h + 1])
                dgv = dg_ref[rows, h * HEAD:h * HEAD + 1]
                da = dgv * nexp * _sigmoid(xg)
                dba = dba + jnp.where(lane == h, dbeta * beta * (1.0 - beta), 0.0) + jnp.where(lane == N_HEADS + h, da, 0.0)
                gsm = (gsm + jnp.where(lane8 == h, _rows8(dgv * nexp * _softplus(xg)), 0.0)
                       + jnp.where(lane8 == N_HEADS + h, _rows8(da), 0.0))
            dba_ref[rows, :] = jnp.zeros((CONV_SUB, D_HALF), BF16)
            dba_ref[rows, :HEAD] = dba.astype(BF16)
            gsm_ref[...] += gsm

        def step(k, carry):
            sub_tile(pl.multiple_of((n_sub - 1 - k) * CONV_SUB, CONV_SUB), False)
            return carry

        lax.fori_loop(0, n_sub - 1, step, 0)
        sub_tile(0, True)

        @pl.when(pl.program_id(0) == n_tiles - 1)
        def _():
            gcw_out[...] = jnp.zeros_like(gcw_out)
            for j in range(CONV_K):
                gcw_out[j:j + 1, :] = _colsum(gcw_ref[8 * j:8 * j + 8, :])
            gsm_out[...] = jnp.broadcast_to(_colsum(gsm_ref[...]), (8, HEAD))

    row = pl.BlockSpec((t, D_HALF), lambda i: (tile_of(i), 0))
    const = lambda shape: pl.BlockSpec(shape, lambda i: (0, 0))
    return _call(
        body, name="conv_bwd", grid=(n_tiles,),
        in_specs=_conv_specs(t, tile_of) + [pl.BlockSpec((t, HEAD), lambda i: (tile_of(i), COL_BA // HEAD)),
                                            const((CONV_K, 3 * D_HALF)), const((1, N_HEADS)), const((1, N_HEADS))] + [row] * 5,
        out_specs=[row, row, row, row, const((8, 3 * D_HALF)), const((8, HEAD))],
        out_shape=[_sds((s, D_HALF), BF16)] * 4 + [_sds((8, 3 * D_HALF)), _sds((8, HEAD))],
        scratch_shapes=[pltpu.VMEM((8, 3 * D_HALF), F32), pltpu.VMEM((8 * CONV_K, 3 * D_HALF), F32),
                        pltpu.VMEM((8, HEAD), F32)],
        compiler_params=_params("arbitrary"),
    )(proj, proj, proj, proj, proj, proj, proj, conv_w, a_log, dt_bias, dqn, dkn, dvs, dbeta, dg)


def _conv_bwd_pre(proj, conv_w, a_log, dt_bias, dqn, dkn, dvs, dbeta, dg):
    s = proj.shape[0]
    t = CONV_T

    def body(q_ref, k_ref, v_ref, hq_ref, hk_ref, hv_ref, ba_ref, cw_ref, al_ref, dtb_ref,
             dqn_ref, dkn_ref, dvs_ref, dbeta_ref, dg_ref, dyq_ref, dyk_ref, dyv_ref, dba_ref, gcw_ref, gsm_ref):
        @pl.when(pl.program_id(0) == 0)
        def _():
            gcw_ref[...] = jnp.zeros_like(gcw_ref)
            gsm_ref[...] = jnp.zeros_like(gsm_ref)

        live = (pl.program_id(0) > 0).astype(F32)
        parts = ((q_ref, hq_ref, dqn_ref, dyq_ref), (k_ref, hk_ref, dkn_ref, dyk_ref), (v_ref, hv_ref, dvs_ref, dyv_ref))
        for p, (x_ref, h_ref, d_ref, dy_ref) in enumerate(parts):
            cols = slice(p * D_HALF, (p + 1) * D_HALF)
            taps = _conv_taps(x_ref[...], h_ref[...] * live)
            y = _conv_pre(taps, cw_ref[:, cols])
            sg = _sigmoid(y)
            sv = y * sg
            if p == 2:
                ds = d_ref[...]
            else:
                segs = []
                for h in HEADS:
                    seg = _head(sv, h)
                    rn = lax.rsqrt(_rowsum(seg * seg) + EPS)
                    nrm = seg * rn
                    dn = d_ref[:, HEAD_COLS[h]]
                    segs.append(rn * (dn - nrm * _rowsum(dn * nrm)))
                ds = jnp.concatenate(segs, axis=1)
            dy = ds * (sg * (1.0 + y * (1.0 - sg)))
            dy_ref[...] = dy
            for j in range(CONV_K):
                gcw_ref[j:j + 1, cols] += _colsum(dy * taps[j])

        ba = ba_ref[...]
        lane = lax.broadcasted_iota(I32, (t, HEAD), 1)
        lane1 = lax.broadcasted_iota(I32, (1, HEAD), 1)
        dba = jnp.zeros((t, HEAD), F32)
        gsm = jnp.zeros((1, HEAD), F32)
        for h in HEADS:
            beta = _sigmoid(ba[:, h:h + 1])
            dbeta = dbeta_ref[:, h * HEAD:h * HEAD + 1]
            xg = ba[:, N_HEADS + h:N_HEADS + h + 1] + dtb_ref[0:1, h:h + 1]
            nexp = -jnp.exp(al_ref[0:1, h:h + 1])
            dgv = dg_ref[:, h * HEAD:h * HEAD + 1]
            da = dgv * nexp * _sigmoid(xg)
            dba = dba + jnp.where(lane == h, dbeta * beta * (1.0 - beta), 0.0) + jnp.where(lane == N_HEADS + h, da, 0.0)
            gsm = (gsm + jnp.where(lane1 == h, _colsum(dgv * nexp * _softplus(xg)), 0.0)
                   + jnp.where(lane1 == N_HEADS + h, _colsum(da), 0.0))
        dba_ref[...] = jnp.zeros_like(dba_ref)
        dba_ref[:, :HEAD] = dba.astype(BF16)
        gsm_ref[0:1, :] += gsm

    row = pl.BlockSpec((t, D_HALF), lambda i: (i, 0))
    return _call(
        body, name="conv_bwd_pre", grid=(s // t,),
        in_specs=_conv_specs(t) + [pl.BlockSpec((t, HEAD), lambda i: (i, COL_BA // HEAD)),
                                   pl.BlockSpec((CONV_K, 3 * D_HALF), lambda i: (0, 0)),
                                   pl.BlockSpec((1, N_HEADS), lambda i: (0, 0)),
                                   pl.BlockSpec((1, N_HEADS), lambda i: (0, 0))] + [row] * 5,
        out_specs=[row, row, row, row,
                   pl.BlockSpec((8, 3 * D_HALF), lambda i: (0, 0)), pl.BlockSpec((8, HEAD), lambda i: (0, 0))],
        out_shape=[_sds((s, D_HALF))] * 3 + [_sds((s, D_HALF), BF16), _sds((8, 3 * D_HALF)), _sds((8, HEAD))],
        compiler_params=_params("arbitrary"),
    )(proj, proj, proj, proj, proj, proj, proj, conv_w, a_log, dt_bias, dqn, dkn, dvs, dbeta, dg)


def _conv_bwd_in(dyq, dyk, dyv, conv_w):
    s = dyq.shape[0]
    t = CONV_T
    last = s // 8 - 1

    def body(q_ref, k_ref, v_ref, nq_ref, nk_ref, nv_ref, cw_ref, oq_ref, ok_ref, ov_ref):
        more = (pl.program_id(0) < pl.num_programs(0) - 1).astype(F32)
        for p, (d_ref, n_ref, o_ref) in enumerate(((q_ref, nq_ref, oq_ref), (k_ref, nk_ref, ok_ref), (v_ref, nv_ref, ov_ref))):
            cw = cw_ref[:, p * D_HALF:(p + 1) * D_HALF]
            dy = d_ref[...]
            nxt = n_ref[...] * more
            acc = dy * cw[3:4]
            for sft in (1, 2, 3):
                acc = acc + _shift_up(dy, nxt, sft) * cw[3 - sft:4 - sft]
            o_ref[...] = acc.astype(BF16)

    row = pl.BlockSpec((t, D_HALF), lambda i: (i, 0))
    nxt = pl.BlockSpec((8, D_HALF), lambda i: (jnp.minimum((i + 1) * (t // 8), last), 0))
    return _call(
        body, name="conv_bwd_in", grid=(s // t,),
        in_specs=[row] * 3 + [nxt] * 3 + [pl.BlockSpec((CONV_K, 3 * D_HALF), lambda i: (0, 0))],
        out_specs=[row] * 3, out_shape=[_sds((s, D_HALF), BF16)] * 3,
        compiler_params=_params("arbitrary"),
    )(dyq, dyk, dyv, dyq, dyk, dyv, conv_w)


IN_T = 512


def _in_bwd(x, dh, norm_w, w_pad, pieces):
    s = x.shape[0]
    t = IN_T
    widths = [D_HALF] * 6 + [N_IN_PAD - COL_BA]

    def body(*refs):
        x_ref, dh_ref, nw_ref, w_ref = refs[:4]
        p_refs = refs[4:4 + len(pieces)]
        gx_ref, gnw_ref = refs[4 + len(pieces):]

        @pl.when(pl.program_id(0) == 0)
        def _():
            gnw_ref[...] = jnp.zeros_like(gnw_ref)

        dn = jnp.zeros((t, D_MODEL), F32)
        col = 0
        for p_ref, wd in zip(p_refs, widths):
            dn = dn + _bdot_nt(p_ref[...], w_ref[:, col:col + wd])
            col += wd
        xv = x_ref[...]
        r = lax.rsqrt(jnp.mean(xv * xv, axis=-1, keepdims=True) + EPS)
        xhat = xv * r
        gnw_ref[...] += _colsum(dn * xhat)
        dxh = dn * nw_ref[...]
        gx_ref[...] = dh_ref[...] + r * (dxh - xhat * jnp.mean(dxh * xhat, axis=-1, keepdims=True))

    wide = pl.BlockSpec((t, D_MODEL), lambda i: (i, 0))
    return _call(
        body, name="in_bwd", grid=(s // t,),
        in_specs=[wide, wide, pl.BlockSpec((1, D_MODEL), lambda i: (0, 0)),
                  pl.BlockSpec((D_MODEL, N_IN_PAD), lambda i: (0, 0))]
                 + [pl.BlockSpec((t, wd), lambda i: (i, 0)) for wd in widths],
        out_specs=[wide, pl.BlockSpec((1, D_MODEL), lambda i: (0, 0))],
        out_shape=[_sds((s, D_MODEL)), _sds((1, D_MODEL))],
        compiler_params=_params("arbitrary"),
    )(x, dh, norm_w, w_pad, *pieces)


def _adamw_shard(name, w, g_own, g_got, cidx, m, v):
    _, r, c = w.shape
    half = r // 2
    rows = 256 if half % 256 == 0 else half
    per_half = half // rows

    def body(c_ref, w_ref, go_ref, gg_ref, m_ref, v_ref, gout_ref, d_ref, nm_ref, nv_ref):
        mine = (pl.program_id(0) // per_half) == c_ref[0]
        gv = jnp.where(mine, go_ref[:, :c], gg_ref[:, :c])
        gout_ref[0] = gv
        mn = ADAM_B1 * m_ref[0] + (1.0 - ADAM_B1) * gv
        vn = ADAM_B2 * v_ref[0] + (1.0 - ADAM_B2) * (gv * gv)
        m_hat = mn / (1.0 - ADAM_B1 ** ADAM_STEP)
        v_hat = vn / (1.0 - ADAM_B2 ** ADAM_STEP)
        d_ref[0] = -ADAM_LR * (m_hat / (jnp.sqrt(v_hat) + ADAM_EPS) + ADAM_WD * w_ref[0])
        nm_ref[0] = mn
        nv_ref[0] = vn

    blk = pl.BlockSpec((1, rows, c), lambda i, c_ref: (0, i, 0))
    gblk = pl.BlockSpec((rows, g_own.shape[1]), lambda i, c_ref: (i % per_half, 0))
    return _call(
        body, name=name,
        grid_spec=pltpu.PrefetchScalarGridSpec(
            num_scalar_prefetch=1, grid=(2 * per_half,),
            in_specs=[blk, gblk, gblk, blk, blk], out_specs=[blk] * 4),
        out_shape=[_sds((1, r, c))] * 4,
        compiler_params=_params("arbitrary"),
    )(cidx, w, g_own, g_got, m, v)


def _adamw_tiles(name, w, g, m, v):
    n = w.shape[0]
    nb = 77 if n % 77 == 0 else n

    def body(w_ref, g_ref, m_ref, v_ref, d_ref, nm_ref, nv_ref):
        gv = g_ref[...]
        mn = ADAM_B1 * m_ref[...] + (1.0 - ADAM_B1) * gv
        vn = ADAM_B2 * v_ref[...] + (1.0 - ADAM_B2) * (gv * gv)
        m_hat = mn / (1.0 - ADAM_B1 ** ADAM_STEP)
        v_hat = vn / (1.0 - ADAM_B2 ** ADAM_STEP)
        d_ref[...] = -ADAM_LR * (m_hat / (jnp.sqrt(v_hat) + ADAM_EPS) + ADAM_WD * w_ref[...])
        nm_ref[...] = mn
        nv_ref[...] = vn

    blk = pl.BlockSpec((nb, 8, HEAD), lambda i: (i, 0, 0))
    return _call(
        body, name=name, grid=(n // nb,),
        in_specs=[blk] * 4, out_specs=[blk] * 3, out_shape=[_sds(w.shape)] * 3,
        compiler_params=_params("arbitrary"),
    )(w, g, m, v)


def _exchange(name, inputs, out_shapes, phases):
    n_in = len(inputs)
    n_out = len(out_shapes)
    n_cp = sum(len(p) for p in phases)

    def body(*refs):
        ins, outs = refs[:n_in], refs[n_in:n_in + n_out]
        send, recv = refs[n_in + n_out:]
        pos = (lax.axis_index("x"), lax.axis_index("y"), lax.axis_index("c"))
        k = 0
        for phase in phases:
            cps = []
            for src, dst, target in phase:
                cps.append(pltpu.make_async_remote_copy(
                    src_ref=src(ins, outs, pos), dst_ref=dst(ins, outs, pos), send_sem=send.at[k], recv_sem=recv.at[k],
                    device_id=target(pos), device_id_type=pl.DeviceIdType.MESH))
                k += 1
            for cp in cps:
                cp.start()
            for cp in cps:
                cp.wait()

    anyspec = pl.BlockSpec(memory_space=pl.ANY)
    return _call(
        body, name=name,
        in_specs=[anyspec] * n_in, out_specs=[anyspec] * n_out, out_shape=list(out_shapes),
        scratch_shapes=[pltpu.SemaphoreType.DMA((n_cp,)), pltpu.SemaphoreType.DMA((n_cp,))],
    )(*inputs)


def _exchange_start(name, inputs, out_shapes, copies):
    n_in, n_out, n_cp = len(inputs), len(out_shapes), len(copies)

    def body(*refs):
        ins, lands = refs[:n_in], refs[n_in:n_in + n_out]
        sems = refs[n_in + n_out:n_in + n_out + 2 * n_cp]
        token = refs[-1]
        pos = (lax.axis_index("x"), lax.axis_index("y"), lax.axis_index("c"))
        for k, (src, dst, target) in enumerate(copies):
            pltpu.make_async_remote_copy(
                src_ref=src(ins, lands, pos), dst_ref=dst(ins, lands, pos), send_sem=sems[2 * k], recv_sem=sems[2 * k + 1],
                device_id=target(pos), device_id_type=pl.DeviceIdType.MESH).start()
        token[...] = jnp.zeros_like(token)

    hbm = pl.BlockSpec(memory_space=pltpu.HBM)
    sem = pl.BlockSpec(memory_space=pltpu.SEMAPHORE)
    bufs = list(inputs) + [lax.empty(o.shape, o.dtype) for o in out_shapes]
    outs = _call(
        body, name=name,
        out_shape=tuple([pltpu.SemaphoreType.DMA(())] * (2 * n_cp) + [pltpu.HBM(b.shape, b.dtype) for b in bufs]
                        + [_sds((8, HEAD))]),
        in_specs=[hbm] * len(bufs),
        out_specs=tuple([sem] * (2 * n_cp) + [hbm] * len(bufs) + [pl.BlockSpec(memory_space=pltpu.VMEM)]),
        input_output_aliases={i: 2 * n_cp + i for i in range(len(bufs))},
        compiler_params=pltpu.CompilerParams(has_side_effects=pltpu.SideEffectType.DATAFLOW_SIDE_EFFECTING),
    )(*[pltpu.with_memory_space_constraint(b, pltpu.HBM) for b in bufs])
    return outs[:2 * n_cp], outs[2 * n_cp:2 * n_cp + n_in], outs[2 * n_cp + n_in:-1], outs[-1]


def _exchange_wait(name, sems, sources, lands, copies, after):
    n_in, n_out, n_cp = len(sources), len(lands), len(copies)

    def body(*refs):
        ins, zones = refs[:n_in], refs[n_in:n_in + n_out]
        sem_refs = refs[n_in + n_out:n_in + n_out + 2 * n_cp]
        pos = (lax.axis_index("x"), lax.axis_index("y"), lax.axis_index("c"))
        for k, (src, dst, target) in enumerate(copies):
            cp = pltpu.make_async_remote_copy(
                src_ref=src(ins, zones, pos), dst_ref=dst(ins, zones, pos), send_sem=sem_refs[2 * k],
                recv_sem=sem_refs[2 * k + 1], device_id=target(pos), device_id_type=pl.DeviceIdType.MESH)
            cp.wait_send()
            cp.wait_recv()

    hbm = pl.BlockSpec(memory_space=pltpu.HBM)
    sem = pl.BlockSpec(memory_space=pltpu.SEMAPHORE)
    bufs = list(sources) + list(lands)
    outs = _call(
        body, name=name,
        out_shape=tuple(pltpu.HBM(b.shape, b.dtype) for b in bufs),
        in_specs=[hbm] * len(bufs) + [sem] * (2 * n_cp) + [pl.BlockSpec(memory_space=pl.ANY)],
        out_specs=tuple([hbm] * len(bufs)),
        input_output_aliases={i: i for i in range(len(bufs))},
        compiler_params=pltpu.CompilerParams(has_side_effects=pltpu.SideEffectType.DATAFLOW_SIDE_EFFECTING),
    )(*bufs, *sems, after)
    return outs[:n_in], outs[n_in:]


def _allreduce_tile(name, v):
    def body(v_ref, out_ref, slots, send, recv):
        x, y, c = lax.axis_index("x"), lax.axis_index("y"), lax.axis_index("c")
        me = 4 * x + 2 * y + c
        slots[me] = v_ref[...]
        cps = []
        for k in range(1, 8):
            peer = (x ^ (k >> 2), y ^ ((k >> 1) & 1), c ^ (k & 1))
            cps.append(pltpu.make_async_remote_copy(
                src_ref=v_ref, dst_ref=slots.at[me], send_sem=send.at[k - 1], recv_sem=recv.at[k - 1],
                device_id=peer, device_id_type=pl.DeviceIdType.MESH))
        for cp in cps:
            cp.start()
        for cp in cps:
            cp.wait()
        acc = slots[0]
        for i in range(1, 8):
            acc = acc + slots[i]
        out_ref[...] = acc

    vm = pl.BlockSpec(memory_space=pltpu.VMEM)
    return _call(
        body, name=name, in_specs=[vm], out_specs=vm, out_shape=_sds(v.shape),
        scratch_shapes=[pltpu.VMEM((8,) + v.shape, F32), pltpu.SemaphoreType.DMA((7,)), pltpu.SemaphoreType.DMA((7,))],
    )(v)


def _chip(pos):
    return 2 * pos[0] + pos[1]


def _other_chip(pos, mask):
    x, y, c = pos
    return (x ^ (mask >> 1), y ^ (mask & 1), c)


def _sibling(pos):
    return (pos[0], pos[1], 1 - pos[2])


def _gather_weights(wb, cb):
    rows = wb.shape[0] // 2

    def half(pos):
        return pl.ds(pos[2] * rows, rows)

    first, second = [], []
    for mask in CHIP_MASKS:
        first.append((lambda ins, outs, pos: ins[0].at[half(pos)],
                      lambda ins, outs, pos: outs[0].at[_chip(pos), half(pos)],
                      functools.partial(_other_chip, mask=mask)))
        second.append((lambda ins, outs, pos, mask=mask: outs[0].at[_chip(pos) ^ mask, half(pos)],
                       lambda ins, outs, pos, mask=mask: outs[0].at[_chip(pos) ^ mask, half(pos)],
                       _sibling))
        first.append((lambda ins, outs, pos: ins[1],
                      lambda ins, outs, pos: outs[1].at[_chip(pos)],
                      functools.partial(_other_chip, mask=mask)))
    return _exchange("gather_weights", [wb, cb], [_sds((4,) + wb.shape, wb.dtype), _sds((4,) + cb.shape, cb.dtype)],
                     [first, second])


def _gather_blocks(ob):
    copies = [(lambda ins, outs, pos: ins[0], lambda ins, outs, pos: outs[0].at[_chip(pos)],
               functools.partial(_other_chip, mask=mask)) for mask in CHIP_MASKS]
    return [_sds((4,) + ob.shape, ob.dtype)], copies


def _to_sibling_half(name, arrays):
    def src(ins, outs, pos, a):
        h = arrays[a].shape[-2] // 2
        sl = pl.ds((1 - pos[2]) * h, h)
        return ins[a].at[:, sl] if arrays[a].ndim == 3 else ins[a].at[sl]

    outs = [_sds(a.shape[:-2] + (a.shape[-2] // 2, a.shape[-1]), a.dtype) for a in arrays]
    phase = [(functools.partial(src, a=a), lambda ins, outs, pos, a=a: outs[a], _sibling) for a in range(len(arrays))]
    return _exchange(name, arrays, outs, [phase])


def _add_half(name, full, part, cidx):
    shape = part.shape
    lead = shape[0] if len(shape) == 3 else 1
    rows, cols = shape[-2], shape[-1]
    tr = rows // 2 if rows % 16 == 0 else rows
    nr = rows // tr
    f3 = full.reshape((lead,) + full.shape[-2:])
    p3 = part.reshape((lead, rows, cols))

    def body(c_ref, f_ref, p_ref, o_ref):
        o_ref[...] = (f_ref[...].astype(F32) + p_ref[...].astype(F32)).astype(o_ref.dtype)

    out = _call(
        body, name=name,
        grid_spec=pltpu.PrefetchScalarGridSpec(
            num_scalar_prefetch=1, grid=(lead, nr),
            in_specs=[pl.BlockSpec((1, tr, cols), lambda b, r, c_ref: (b, c_ref[0] * nr + r, 0)),
                      pl.BlockSpec((1, tr, cols), lambda b, r, c_ref: (b, r, 0))],
            out_specs=pl.BlockSpec((1, tr, cols), lambda b, r, c_ref: (b, r, 0))),
        out_shape=_sds((lead, rows, cols), part.dtype),
        compiler_params=_params("arbitrary", "arbitrary"),
    )(cidx, f3, p3)
    return out.reshape(shape)


def _to_other_chips(arrays, blocked):
    def src(ins, outs, pos, a, mask):
        return ins[a].at[_chip(pos) ^ mask] if blocked[a] else ins[a]

    outs = [_sds((3,) + (a.shape[1:] if b else a.shape), a.dtype) for a, b in zip(arrays, blocked)]
    copies = []
    for mi, mask in enumerate(CHIP_MASKS):
        for a in range(len(arrays)):
            copies.append((functools.partial(src, a=a, mask=mask), lambda ins, outs, pos, a=a, mi=mi: outs[a].at[mi],
                           functools.partial(_other_chip, mask=mask)))
    return outs, copies


def _add_chips(name, own, got, jidx, blocked):
    rows, cols = got.shape[-2:]
    tr = rows // 2 if rows % 16 == 0 else rows
    nr = rows // tr
    o3 = own if blocked else own.reshape((1, rows, cols))

    def body(j_ref, o_ref, g_ref, out_ref):
        out_ref[...] = ((o_ref[0].astype(F32) + g_ref[0].astype(F32))
                        + (g_ref[1].astype(F32) + g_ref[2].astype(F32)))

    own_map = (lambda r, j_ref: (j_ref[0], r, 0)) if blocked else (lambda r, j_ref: (0, r, 0))
    return _call(
        body, name=name,
        grid_spec=pltpu.PrefetchScalarGridSpec(
            num_scalar_prefetch=1, grid=(nr,),
            in_specs=[pl.BlockSpec((1, tr, cols), own_map),
                      pl.BlockSpec((3, tr, cols), lambda r, j_ref: (0, r, 0))],
            out_specs=pl.BlockSpec((tr, cols), lambda r, j_ref: (r, 0))),
        out_shape=_sds((rows, cols)),
        compiler_params=_params("arbitrary"),
    )(jidx, o3, got)


def _to_sibling(name, arrays):
    phase = [(lambda ins, outs, pos, a=a: ins[a], lambda ins, outs, pos, a=a: outs[a], _sibling)
             for a in range(len(arrays))]
    return _exchange(name, arrays, [_sds(a.shape, a.dtype) for a in arrays], [phase])


def _local_step(x, target, w_pad, w_out, conv_w, norm_w, pool_w, pool_scale, a_log, dt_bias, dn_norm_w, final_norm_w):
    proj, n_t = _proj_fwd(x, norm_w, w_pad)
    y_pool = _pool_fwd(proj, pool_w, pool_scale)
    qn, kn, vs, beta, g = _conv_fwd(proj, conv_w, a_log, dt_bias)
    u, w, att, qd, kd, tm, cd = _intra_fwd(qn, kn, vs, beta, g)
    o, vn, st = _scan_fwd(u, w, att, qd, kd, cd)
    w_out = w_out(o) if callable(w_out) else w_out
    y_t, dh, dyp, do, ddz, loss, g_fnw, g_dnw = _out_fwd_bwd(x, y_pool, o, proj, target, w_out, dn_norm_w, final_norm_w)
    g_wout = _token_matmul("grad_w_out", y_t, [(dh, 0), (dh, 1)])
    dpu, dpz, g_pw, g_ps = _pool_bwd(proj, dyp, pool_w, pool_scale)
    du, dw, datt, dqd, dkd, dcd = _scan_bwd(do, vn, qd, kd, w, att, cd, st)
    dqn, dkn, dvs, dbeta, dg = _intra_bwd(qn, kn, vs, beta, g, tm, du, dw, datt, dqd, dkd, dcd)
    dcq, dck, dcv, dba, g_cw, g_sm = _conv_bwd(proj, conv_w, a_log, dt_bias, dqn, dkn, dvs, dbeta, dg)
    pieces = [dpu, dpz, dcq, dck, dcv, ddz, dba]
    g_win = _token_matmul("grad_w_in", n_t, [(p, 0) for p in pieces])
    small = dict(norm_w=jnp.zeros_like(norm_w), pool_w=g_pw, pool_scale=g_ps, conv_w=g_cw[:CONV_K],
                 a_log=g_sm[0:1, 0:N_HEADS], dt_bias=g_sm[0:1, N_HEADS:2 * N_HEADS], dn_norm_w=g_dnw, final_norm_w=g_fnw)
    return loss[0, 0], g_win, g_wout, small, dh, pieces


def _pack_small(t):
    lanes = lambda a: jnp.pad(a.reshape(1, -1), ((0, 0), (0, HEAD - a.size)))
    rows = [t["pool_w"].reshape(-1, HEAD), t["norm_w"].reshape(-1, HEAD), t["final_norm_w"].reshape(-1, HEAD),
            t["pool_scale"].reshape(-1, HEAD), t["conv_w"].reshape(-1, HEAD), t["dn_norm_w"].reshape(1, HEAD),
            lanes(t["a_log"]), lanes(t["dt_bias"]), lanes(t.get("loss", jnp.zeros((1,), F32)))]
    buf = jnp.concatenate(rows, axis=0)
    return jnp.pad(buf, ((0, SMALL_ROWS - buf.shape[0]), (0, 0)))


def _unpack_small(buf, conv_cols):
    out, r = {}, 0
    for name, nrow, shape in (("pool_w", 512, (1, N_HEADS, HEAD, HEAD)), ("norm_w", 8, (1, D_MODEL)),
                              ("final_norm_w", 8, (D_MODEL,)), ("pool_scale", 4, (1, D_HALF)),
                              ("conv_w", CONV_K * conv_cols // HEAD, (1, CONV_K, conv_cols)), ("dn_norm_w", 1, (1, HEAD))):
        out[name] = buf[r:r + nrow].reshape(shape)
        r += nrow
    out["a_log"] = buf[r:r + 1, :N_HEADS]
    out["dt_bias"] = buf[r + 1:r + 2, :N_HEADS]
    out["loss"] = buf[r + 2, 0]
    return out


def kernel(x, norm_w, w_in, pool_w, pool_scale, conv_w, a_log, dt_bias, dn_norm_w, w_out, final_norm_w, loss_target, m_norm_w, m_w_in, m_pool_w, m_pool_scale, m_conv_w, m_a_log, m_dt_bias, m_dn_norm_w, m_w_out, m_final_norm_w, v_norm_w, v_w_in, v_pool_w, v_pool_scale, v_conv_w, v_a_log, v_dt_bias, v_dn_norm_w, v_w_out, v_final_norm_w):
    cidx = lax.axis_index("c").astype(I32).reshape(1)
    jidx = (2 * lax.axis_index("x") + lax.axis_index("y")).astype(I32)

    wb = jnp.pad(w_in[0].astype(BF16), ((0, 0), (0, BLK_IN_PAD - BLK_IN)))
    ob = w_out[0].astype(BF16)
    gw, gc = _gather_weights(wb, conv_w[0])
    mine = lambda j: jidx == j
    w_pad = jnp.concatenate([jnp.where(mine(j), wb[:, :BLK_IN], gw[j, :, :BLK_IN]) for j in range(4)]
                            + [jnp.zeros((D_MODEL, N_IN_PAD - N_IN), BF16)], axis=1)
    cw_full = jnp.concatenate([jnp.where(mine(j), conv_w[0], gc[j]) for j in range(4)], axis=1)

    lands_o, copies_o = _gather_blocks(ob)
    sems_o, ob_thru, zones_o, token_o = _exchange_start("gather_w_out_start", [ob], lands_o, copies_o)

    def w_out_full(after):
        (own,), (got,) = _exchange_wait("gather_w_out_wait", sems_o, ob_thru, zones_o, copies_o, after)
        return jnp.where((jnp.arange(4) == jidx)[:, None, None], own[None], got).reshape(D_MODEL, D_MODEL)

    loss, g_win, g_wout, small, dh, pieces = _local_step(
        x[0], loss_target[0], w_pad, w_out_full, cw_full, norm_w + token_o[0, 0], pool_w[0], pool_scale, a_log, dt_bias,
        dn_norm_w, final_norm_w.reshape(1, D_MODEL))
    small["loss"] = loss

    blocks_in = jnp.stack([jnp.pad(g_win[:, j * BLK_IN:(j + 1) * BLK_IN].astype(BF16), ((0, 0), (0, BLK_IN_PAD - BLK_IN)))
                           for j in range(4)])
    blocks_out = g_wout.astype(BF16).reshape(4, BLK_OUT, D_MODEL)
    full = [blocks_in, blocks_out, _pack_small(small)]
    from_sib = _to_sibling_half("reduce_sibling", full)
    chip_sum = [_add_half("add_sibling_%d" % i, f, p, cidx) for i, (f, p) in enumerate(zip(full, from_sib))]
    blocked = [True, True, False]
    lands, copies = _to_other_chips(chip_sum, blocked)
    sems, chip_sum, zones, token = _exchange_start("reduce_chips_start", chip_sum, lands, copies)
    gx, g_nw = _in_bwd(x[0], dh, norm_w + token[0, 0], w_pad, pieces)
    g_nw = _allreduce_tile("reduce_norm_w", g_nw.reshape(8, HEAD)).reshape(1, D_MODEL)
    chip_sum, from_chips = _exchange_wait("reduce_chips_wait", sems, chip_sum, zones, copies, gx)
    halves = [_add_chips("add_chips_%d" % i, o, g, jidx.reshape(1), b)
              for i, (o, g, b) in enumerate(zip(chip_sum, from_chips, blocked))]
    other_halves = _to_sibling("swap_halves", halves)

    weights = dict(norm_w=norm_w, w_in=w_in, pool_w=pool_w, pool_scale=pool_scale, conv_w=conv_w, a_log=a_log,
                   dt_bias=dt_bias, dn_norm_w=dn_norm_w, w_out=w_out, final_norm_w=final_norm_w)
    ms = dict(norm_w=m_norm_w, w_in=m_w_in, pool_w=m_pool_w, pool_scale=m_pool_scale, conv_w=m_conv_w, a_log=m_a_log,
              dt_bias=m_dt_bias, dn_norm_w=m_dn_norm_w, w_out=m_w_out, final_norm_w=m_final_norm_w)
    vs = dict(norm_w=v_norm_w, w_in=v_w_in, pool_w=v_pool_w, pool_scale=v_pool_scale, conv_w=v_conv_w, a_log=v_a_log,
              dt_bias=v_dt_bias, dn_norm_w=v_dn_norm_w, w_out=v_w_out, final_norm_w=v_final_norm_w)
    names = ["norm_w", "w_in", "pool_w", "pool_scale", "conv_w", "a_log", "dt_bias", "dn_norm_w", "w_out", "final_norm_w"]
    small_names = [n for n in names if n not in ("w_in", "w_out")]

    def pack(t):
        conv = lax.dynamic_update_slice_in_dim(jnp.zeros((CONV_K, 3 * D_HALF), F32), t["conv_w"][0], jidx * BLK_CONV, axis=1)
        return _pack_small({**{n: t[n] for n in small_names if n != "conv_w"}, "conv_w": conv})[None]

    results = [{}, {}, {}, {}]
    to_tiles = lambda a: jnp.transpose(a, (2, 0, 1)).reshape(BLK_IN, 8, HEAD)
    from_tiles = lambda a: jnp.transpose(a, (1, 2, 0)).reshape(1, D_MODEL, BLK_IN)
    lo = jnp.where(cidx[0] == 0, halves[0], other_halves[0])
    hi = jnp.where(cidx[0] == 0, other_halves[0], halves[0])
    g_tiles = jnp.concatenate([lo[:, :BLK_IN].T, hi[:, :BLK_IN].T], axis=1).reshape(BLK_IN, 8, HEAD)
    outs = _adamw_tiles("adamw_w_in", to_tiles(w_in), g_tiles, to_tiles(m_w_in), to_tiles(v_w_in))
    for res, o in zip(results, (g_tiles,) + tuple(outs)):
        res["w_in"] = from_tiles(o)
    outs = _adamw_shard("adamw_w_out", w_out, halves[1], other_halves[1], cidx, m_w_out, v_w_out)
    for res, o in zip(results, outs):
        res["w_out"] = o
    outs = _adamw_shard("adamw_small", pack(weights), halves[2], other_halves[2], cidx, pack(ms), pack(vs))
    for res, o in zip(results, outs):
        got = _unpack_small(o[0], 3 * D_HALF)
        got["conv_w"] = lax.dynamic_slice_in_dim(got["conv_w"], jidx * BLK_CONV, BLK_CONV, axis=2)
        res.update(got)
    one_tile = lambda a: a.reshape(1, 8, HEAD)
    outs = _adamw_tiles("adamw_norm_w", one_tile(norm_w), one_tile(g_nw), one_tile(m_norm_w), one_tile(v_norm_w))
    for res, o in zip(results, (g_nw,) + tuple(outs)):
        res["norm_w"] = o.reshape(1, D_MODEL)
    grads, delta, new_m, new_v = results

    return (grads["loss"], gx[None], *[grads[n] for n in names], *[delta[n] for n in names],
            *[new_m[n] for n in names], *[new_v[n] for n in names])
```

```python
import functools

import jax
import jax.numpy as jnp
import numpy as np
from jax import lax
from jax.experimental import pallas as pl
from jax.experimental.pallas import tpu as pltpu

F32 = jnp.float32
BF16 = jnp.bfloat16
I32 = jnp.int32

D_MODEL = 1024
D_HALF = 512
N_HEADS = 4
HEAD = 128
CHUNK = 64
PAIR = 2 * CHUNK
WINDOWS = (2, 4, 8, 16)
CONV_K = 4
EPS = 1e-6
N_IN = 3080
N_IN_PAD = 3200
BLK_IN = 770
BLK_IN_PAD = 896
BLK_OUT = 256
BLK_CONV = 384
COL_BA = 3072
QK_SCALE = HEAD ** -0.5
SMALL_ROWS = 592
VMEM_LIMIT = 56 * 1024 * 1024

ADAM_LR = 0.001
ADAM_B1 = 0.9
ADAM_B2 = 0.999
ADAM_EPS = 1e-08
ADAM_WD = 0.01
ADAM_STEP = 10

CHIP_MASKS = (2, 1, 3)
HEADS = range(N_HEADS)
HEAD_COLS = [slice(h * HEAD, (h + 1) * HEAD) for h in HEADS]


def _call(body, **kw):
    return pl.pallas_call(body, **kw)


def _params(*sem):
    return pltpu.CompilerParams(dimension_semantics=sem, vmem_limit_bytes=VMEM_LIMIT)


def _sds(shape, dtype=F32):
    return jax.ShapeDtypeStruct(shape, dtype)


def _bdot(a, b):
    return jnp.dot(a.astype(BF16), b.astype(BF16), preferred_element_type=F32)


def _bdot_nt(a, b):
    return lax.dot_general(a.astype(BF16), b.astype(BF16), (((1,), (1,)), ((), ())), preferred_element_type=F32)


def _bdot_tn(a, b):
    return lax.dot_general(a.astype(BF16), b.astype(BF16), (((0,), (0,)), ((), ())), preferred_element_type=F32)


def _split(a):
    hi = a.astype(BF16)
    lo = (a - hi.astype(F32)).astype(BF16)
    return hi, lo


def _mask_dot(m, b, dims=(((1,), (0,)), ((), ()))):
    bh, bl = _split(b)
    dg = functools.partial(lax.dot_general, dimension_numbers=dims, preferred_element_type=F32)
    return dg(m, bh) + dg(m, bl)


def _sigmoid(x):
    return 0.5 * jnp.tanh(0.5 * x) + 0.5


def _softplus(x):
    return jnp.maximum(x, 0.0) + jnp.log(1.0 + jnp.exp(-jnp.abs(x)))


def _rowsum(x):
    return jnp.sum(x, axis=-1, keepdims=True)


def _colsum(x):
    return jnp.sum(x, axis=0, keepdims=True)


def _shift_down(xv, prev8, k):
    r = pltpu.roll(xv, k, 0)
    q = pltpu.roll(prev8, k, 0)
    row = lax.broadcasted_iota(I32, prev8.shape, 0)
    top = jnp.where(row < k, q, r[0:8])
    return jnp.concatenate([top, r[8:]], axis=0)


def _shift_up(xv, next8, k):
    t = xv.shape[0]
    r = pltpu.roll(xv, t - k, 0)
    q = pltpu.roll(next8, 8 - k, 0)
    row = lax.broadcasted_iota(I32, next8.shape, 0)
    bot = jnp.where(row >= 8 - k, q, r[t - 8:])
    return jnp.concatenate([r[:t - 8], bot], axis=0)


def _band(rows, cols, off, w, anti=False):
    r = lax.broadcasted_iota(I32, (rows, cols), 0)
    c = lax.broadcasted_iota(I32, (rows, cols), 1)
    d = (c - r + off) if anti else (r - c + off)
    return ((d >= 0) & (d < w)).astype(BF16)


def _head(ref_or_val, h):
    return ref_or_val[:, h * HEAD:(h + 1) * HEAD]


INTRA_PAIRS = 2
UNITS = [(pp, h) for pp in range(INTRA_PAIRS) for h in HEADS]


def _heads(ref, rows=PAIR):
    return [ref[pp * rows:(pp + 1) * rows, HEAD_COLS[h]] for pp, h in UNITS]


def _put_heads(ref, vals, rows=PAIR):
    for (pp, h), v in zip(UNITS, vals):
        ref[pp * rows:(pp + 1) * rows, HEAD_COLS[h]] = v.astype(ref.dtype)


def _each(fn, *lists):
    return [fn(*args) for args in zip(*lists)]


def _proj_fwd(x, norm_w, w_pad):
    s = x.shape[0]
    tm = 512

    def body(x_ref, nw_ref, w_ref, proj_ref, nt_ref):
        xv = x_ref[...]
        r = lax.rsqrt(jnp.mean(xv * xv, axis=-1, keepdims=True) + EPS)
        nv = xv * r * nw_ref[...]
        nt_ref[...] = nv.T.astype(BF16)
        proj_ref[...] = jnp.dot(nv.astype(BF16), w_ref[...], preferred_element_type=F32)

    return _call(
        body, name="proj_fwd", grid=(s // tm,),
        in_specs=[pl.BlockSpec((tm, D_MODEL), lambda i: (i, 0)),
                  pl.BlockSpec((1, D_MODEL), lambda i: (0, 0)),
                  pl.BlockSpec((D_MODEL, N_IN_PAD), lambda i: (0, 0))],
        out_specs=[pl.BlockSpec((tm, N_IN_PAD), lambda i: (i, 0)),
                   pl.BlockSpec((D_MODEL, tm), lambda i: (0, i))],
        out_shape=[_sds((s, N_IN_PAD)), _sds((D_MODEL, s), BF16)],
        compiler_params=_params("arbitrary"),
    )(x, norm_w, w_pad)


def _pool_bands(t, anti=False):
    r = np.arange(t)[:, None]
    c = np.arange(t + HEAD)[None, :]
    d = (c - r) if anti else (r - c + HEAD)
    return jnp.asarray(np.stack([(d >= 0) & (d < w) for w in WINDOWS]), BF16)


def _pool_mix(u, halo, z, pw, bands, row0):
    t = u[0].shape[0]
    rows = row0 + lax.broadcasted_iota(I32, (t, 1), 0) + 1
    cnt = [jnp.minimum(rows, w).astype(F32) for w in WINDOWS]
    win = _each(lambda b, h, v: _mask_dot(b, jnp.concatenate([h, v], axis=0)), bands, halo, u)
    mix = _each(lambda a, c, v: a / c - v, win, cnt, u)
    mixed = _each(_bdot, mix, pw)
    return mix, mixed, _each(_sigmoid, z), cnt


POOL_T = 256


def _pool_fwd(proj, pool_w, pool_scale):
    s = proj.shape[0]
    t = POOL_T
    hb = t // HEAD

    def body(u_ref, z_ref, halo_ref, pw_ref, ps_ref, band_ref, y_ref):
        i = pl.program_id(0)
        live = (i > 0).astype(F32)
        groups = lambda ref: [ref[:, sl] for sl in HEAD_COLS]
        z = groups(z_ref)
        _, mixed, sg, _ = _pool_mix(groups(u_ref), [h * live for h in groups(halo_ref)], z,
                                    [pw_ref[g] for g in HEADS], [band_ref[g] for g in HEADS], i * t)
        for sl, m, zg, s_ in zip(HEAD_COLS, mixed, z, sg):
            y_ref[:, sl] = m * ps_ref[:, sl] * (zg * s_)

    return _call(
        body, name="pool_fwd", grid=(s // t,),
        in_specs=[pl.BlockSpec((t, D_HALF), lambda i: (i, 0)),
                  pl.BlockSpec((t, D_HALF), lambda i: (i, 1)),
                  pl.BlockSpec((HEAD, D_HALF), lambda i: (jnp.maximum(i * hb - 1, 0), 0)),
                  pl.BlockSpec((N_HEADS, HEAD, HEAD), lambda i: (0, 0, 0)),
                  pl.BlockSpec((1, D_HALF), lambda i: (0, 0)),
                  pl.BlockSpec((N_HEADS, t, HEAD + t), lambda i: (0, 0, 0))],
        out_specs=pl.BlockSpec((t, D_HALF), lambda i: (i, 0)),
        out_shape=_sds((s, D_HALF)),
        compiler_params=_params("arbitrary"),
    )(proj, proj, proj, pool_w, pool_scale, _pool_bands(t))


def _conv_taps(xv, prev8):
    return [_shift_down(xv, prev8, CONV_K - 1 - j) for j in range(CONV_K - 1)] + [xv]


def _conv_pre(taps, cw):
    y = taps[CONV_K - 1] * cw[CONV_K - 1:CONV_K]
    for j in range(CONV_K - 2, -1, -1):
        y = y + taps[j] * cw[j:j + 1]
    return y


CONV_T = 256
CONV_SUB = 256


def _conv_specs(t, tile_of=lambda i: i):
    tiles = [pl.BlockSpec((t, D_HALF), functools.partial(lambda i, p: (tile_of(i), 2 + p), p=p)) for p in range(3)]
    halos = [pl.BlockSpec((8, D_HALF),
                          functools.partial(lambda i, p: (jnp.maximum(tile_of(i) * (t // 8) - 1, 0), 2 + p), p=p))
             for p in range(3)]
    return tiles + halos


def _conv_fwd(proj, conv_w, a_log, dt_bias):
    s = proj.shape[0]
    t = CONV_T

    def body(q_ref, k_ref, v_ref, hq_ref, hk_ref, hv_ref, ba_ref, cw_ref, al_ref, dtb_ref,
             qn_ref, kn_ref, vs_ref, beta_ref, g_ref):
        live = (pl.program_id(0) > 0).astype(F32)
        parts = ((q_ref, hq_ref, qn_ref), (k_ref, hk_ref, kn_ref), (v_ref, hv_ref, vs_ref))

        def sub_tile(r0, first):
            rows = pl.ds(r0, CONV_SUB)
            for p, (x_ref, h_ref, o_ref) in enumerate(parts):
                for h in HEADS:
                    cs = HEAD_COLS[h]
                    prev8 = h_ref[:, cs] * live if first else x_ref[pl.ds(r0 - 8, 8), cs]
                    y = _conv_pre(_conv_taps(x_ref[rows, cs], prev8), cw_ref[:, p * D_HALF + h * HEAD:p * D_HALF + (h + 1) * HEAD])
                    sv = y * _sigmoid(y)
                    o_ref[rows, cs] = sv if p == 2 else sv * lax.rsqrt(_rowsum(sv * sv) + EPS)
            ba = ba_ref[rows, :]
            for h in HEADS:
                beta = _sigmoid(ba[:, h:h + 1])
                gl = -jnp.exp(al_ref[0:1, h:h + 1]) * _softplus(ba[:, N_HEADS + h:N_HEADS + h + 1] + dtb_ref[0:1, h:h + 1])
                beta_ref[rows, HEAD_COLS[h]] = jnp.broadcast_to(beta, (CONV_SUB, HEAD))
                g_ref[rows, HEAD_COLS[h]] = jnp.broadcast_to(gl, (CONV_SUB, HEAD))

        sub_tile(0, True)

        def step(k, carry):
            sub_tile(pl.multiple_of(k * CONV_SUB, CONV_SUB), False)
            return carry

        lax.fori_loop(1, t // CONV_SUB, step, 0)

    row = pl.BlockSpec((t, D_HALF), lambda i: (i, 0))
    return _call(
        body, name="conv_fwd", grid=(s // t,),
        in_specs=_conv_specs(t) + [pl.BlockSpec((t, HEAD), lambda i: (i, COL_BA // HEAD)),
                                   pl.BlockSpec((CONV_K, 3 * D_HALF), lambda i: (0, 0)),
                                   pl.BlockSpec((1, N_HEADS), lambda i: (0, 0)),
                                   pl.BlockSpec((1, N_HEADS), lambda i: (0, 0))],
        out_specs=[row] * 5,
        out_shape=[_sds((s, D_HALF))] * 5,
        compiler_params=_params("arbitrary"),
    )(proj, proj, proj, proj, proj, proj, proj, conv_w, a_log, dt_bias)


def _pair_masks():
    r = lax.broadcasted_iota(I32, (PAIR, PAIR), 0)
    c = lax.broadcasted_iota(I32, (PAIR, PAIR), 1)
    same = jnp.right_shift(r, 6) == jnp.right_shift(c, 6)
    return same, same & (r >= c), same & (r > c), r == c


def _pair_common(qn, kn, vs, beta, g):
    same, incl, strict, eye = _pair_masks()
    incl_b = incl.astype(BF16)
    first = lax.broadcasted_iota(I32, (PAIR, HEAD), 0) < CHUNK
    gc = _each(lambda gv: _mask_dot(incl_b, gv), g)
    gc_row = _each(lambda v: _colsum(jnp.where(eye, v, 0.0)), gc)
    decay = _each(lambda v, r: jnp.where(incl, jnp.exp(jnp.where(incl, v - r, 0.0)), 0.0), gc, gc_row)
    gl = _each(lambda v: jnp.where(first, v[CHUNK - 1:CHUNK], v[PAIR - 1:PAIR]), gc)
    egc = _each(jnp.exp, gc)
    q = _each(lambda v: v * QK_SCALE, qn)
    kb = _each(lambda k, b: k * b, kn, beta)
    return dict(same=same, incl=incl, strict=strict, eye=eye, gc=gc, decay=decay, gl=gl, egc=egc,
                ekd=_each(lambda a, b: jnp.exp(a - b), gl, gc), cd=_each(jnp.exp, gl), q=q, kb=kb,
                vb=_each(lambda v, b: v * b, vs, beta), kbg=_each(lambda k, e: k * e, kb, egc),
                kk=_each(_bdot_nt, kb, kn), qk=_each(_bdot_nt, q, kn))


def _tri_inv(a, eye_f):
    p = _each(lambda v: eye_f - v, a)
    x = _each(_bdot, a, a)
    for it in range(5):
        p = _each(lambda pv, xv: pv + _bdot(pv, xv), p, x)
        if it < 4:
            x = _each(_bdot, x, x)
    return p


def _pair_spec():
    return pl.BlockSpec((INTRA_PAIRS * PAIR, D_HALF), lambda i: (i, 0))


def _chunk_scalar_spec(pairs=1, index=lambda i: (i, 0)):
    return pl.BlockSpec((16 * pairs, D_HALF), index)


SCAN_PAIRS = 2
SCAN_ROWS = SCAN_PAIRS * PAIR


def _intra_fwd(qn, kn, vs, beta, g):
    s = qn.shape[0]

    def body(qn_ref, kn_ref, vs_ref, beta_ref, g_ref, u_ref, w_ref, att_ref, qd_ref, kd_ref, t_ref, cd_ref):
        kn = _heads(kn_ref)
        cm = _pair_common(_heads(qn_ref), kn, _heads(vs_ref), _heads(beta_ref), _heads(g_ref))
        a = _each(lambda kk, d: jnp.where(cm["strict"], kk * d, 0.0), cm["kk"], cm["decay"])
        tm = _tri_inv(a, cm["eye"].astype(F32))
        _put_heads(t_ref, tm)
        _put_heads(u_ref, _each(_bdot, tm, cm["vb"]))
        _put_heads(w_ref, _each(_bdot, tm, cm["kbg"]))
        _put_heads(att_ref, _each(lambda a, b: a * b, cm["qk"], cm["decay"]))
        _put_heads(qd_ref, _each(lambda a, b: a * b, cm["q"], cm["egc"]))
        _put_heads(kd_ref, _each(lambda a, b: a * b, kn, cm["ekd"]))
        for ci in range(2):
            for (pp, h), v in zip(UNITS, cm["cd"]):
                cd_ref[pp * 16 + ci * 8:pp * 16 + (ci + 1) * 8, HEAD_COLS[h]] = v[ci * CHUNK:ci * CHUNK + 8]

    return _call(
        body, name="intra_fwd", grid=(s // (INTRA_PAIRS * PAIR),),
        in_specs=[_pair_spec()] * 5, out_specs=[_pair_spec()] * 6 + [_chunk_scalar_spec(INTRA_PAIRS)],
        out_shape=[_sds((s, D_HALF))] + [_sds((s, D_HALF), BF16)] * 5 + [_sds((s // 8, D_HALF))],
        compiler_params=_params("arbitrary"),
    )(qn, kn, vs, beta, g)


def _scan_fwd(u, w, att, qd, kd, cd):
    s = u.shape[0]
    n_chunks = s // CHUNK

    def body(u_ref, w_ref, att_ref, qd_ref, kd_ref, cd_ref, o_ref, vn_ref, st_ref, state):
        @pl.when(pl.program_id(0) == 0)
        def _():
            state[...] = jnp.zeros_like(state)
        cols = list(enumerate(HEAD_COLS))
        sm = [state[h] for h in HEADS]
        for ci in range(2 * SCAN_PAIRS):
            rs = slice(ci * CHUNK, (ci + 1) * CHUNK)
            for h in HEADS:
                st_ref[ci, h] = sm[h]
            both = [_bdot(jnp.concatenate([w_ref[rs, sl], qd_ref[rs, sl]], axis=0), sm[h]) for h, sl in cols]
            vn = [u_ref[rs, sl] - both[h][:CHUNK] for h, sl in cols]
            for h, sl in cols:
                vn_ref[rs, sl] = vn[h].astype(BF16)
                o_ref[rs, sl] = both[h][CHUNK:]
            sm = [sm[h] * cd_ref[ci * 8:ci * 8 + 1, sl] + _bdot_tn(kd_ref[rs, sl], vn[h]) for h, sl in cols]
        for h in HEADS:
            state[h] = sm[h]
        for pp in range(SCAN_PAIRS):
            rp = slice(pp * PAIR, (pp + 1) * PAIR)
            intra = [_bdot(att_ref[rp, sl], vn_ref[rp, sl]) for sl in HEAD_COLS]
            for h, sl in cols:
                o_ref[rp, sl] += intra[h]

    rows = pl.BlockSpec((SCAN_ROWS, D_HALF), lambda i: (i, 0))
    return _call(
        body, name="scan_fwd", grid=(s // SCAN_ROWS,),
        in_specs=[rows] * 5 + [_chunk_scalar_spec(SCAN_PAIRS)],
        out_specs=[rows, rows, pl.BlockSpec((2 * SCAN_PAIRS, N_HEADS, HEAD, HEAD), lambda i: (i, 0, 0, 0))],
        out_shape=[_sds((s, D_HALF)), _sds((s, D_HALF), BF16), _sds((n_chunks, N_HEADS, HEAD, HEAD))],
        scratch_shapes=[pltpu.VMEM((N_HEADS, HEAD, HEAD), F32)],
        compiler_params=_params("arbitrary"),
    )(u, w, att, qd, kd, cd)


OUT_T = 512


def _out_fwd_bwd(x, y_pool, o, proj, target, w_out, dn_norm_w, final_norm_w):
    s = x.shape[0]
    t = OUT_T

    def body(x_ref, yp_ref, o_ref, z_ref, tg_ref, wo_ref, dnw_ref, fnw_ref,
             yt_ref, dh_ref, dyp_ref, do_ref, dz_ref, loss_ref, gfn_ref, gdn_ref, y_ref):
        @pl.when(pl.program_id(0) == 0)
        def _():
            loss_ref[...] = jnp.zeros_like(loss_ref)
            gfn_ref[...] = jnp.zeros_like(gfn_ref)
            gdn_ref[...] = jnp.zeros_like(gdn_ref)

        ypv = yp_ref[...]
        y_ref[:, :D_HALF] = ypv.astype(BF16)
        yt_ref[:D_HALF, :] = ypv.T.astype(BF16)
        dnw = dnw_ref[...]
        keep = []
        for h in HEADS:
            ov = o_ref[:, HEAD_COLS[h]]
            zv = z_ref[:, HEAD_COLS[h]]
            ro = lax.rsqrt(jnp.mean(ov * ov, axis=-1, keepdims=True) + EPS)
            ohat = ov * ro
            sg = _sigmoid(zv)
            keep.append((ro, ohat, zv, sg))
            ydn = ohat * dnw * (zv * sg)
            y_ref[:, D_HALF + h * HEAD:D_HALF + (h + 1) * HEAD] = ydn.astype(BF16)
            yt_ref[D_HALF + h * HEAD:D_HALF + (h + 1) * HEAD, :] = ydn.T.astype(BF16)

        hv = x_ref[...] + jnp.dot(y_ref[...], wo_ref[...], preferred_element_type=F32)
        r2 = lax.rsqrt(jnp.mean(hv * hv, axis=-1, keepdims=True) + EPS)
        hhat = hv * r2
        fnw = fnw_ref[...]
        err = hhat * fnw - tg_ref[...]
        loss_ref[...] += 0.5 * jnp.sum(_rowsum(err * err) * (1.0 / D_MODEL), axis=0, keepdims=True)
        dout = err * (1.0 / D_MODEL)
        gfn_ref[...] += _colsum(dout * hhat)
        dhh = dout * fnw
        dh = r2 * (dhh - hhat * jnp.mean(dhh * hhat, axis=-1, keepdims=True))
        dh_ref[...] = dh
        dy = _bdot_nt(dh, wo_ref[...])
        dyp_ref[...] = dy[:, :D_HALF]
        gdn = jnp.zeros((1, HEAD), F32)
        for h in HEADS:
            ro, ohat, zv, sg = keep[h]
            dyd = dy[:, D_HALF + h * HEAD:D_HALF + (h + 1) * HEAD]
            sz = zv * sg
            dz_ref[:, HEAD_COLS[h]] = (dyd * ohat * dnw * (sg * (1.0 + zv * (1.0 - sg)))).astype(BF16)
            gdn = gdn + _colsum(dyd * ohat * sz)
            doh = dyd * dnw * sz
            do_ref[:, HEAD_COLS[h]] = ro * (doh - ohat * jnp.mean(doh * ohat, axis=-1, keepdims=True))
        gdn_ref[...] += gdn

    wide = pl.BlockSpec((t, D_MODEL), lambda i: (i, 0))
    half = pl.BlockSpec((t, D_HALF), lambda i: (i, 0))
    const = lambda shape: pl.BlockSpec(shape, lambda i: (0,) * len(shape))
    return _call(
        body, name="out_fwd_bwd", grid=(s // t,),
        in_specs=[wide, half, half, pl.BlockSpec((t, D_HALF), lambda i: (i, 5)), wide,
                  const((D_MODEL, D_MODEL)), const((1, HEAD)), const((1, D_MODEL))],
        out_specs=[pl.BlockSpec((D_MODEL, t), lambda i: (0, i)), wide, half, half, half,
                   const((1, HEAD)), const((1, D_MODEL)), const((1, HEAD))],
        out_shape=[_sds((D_MODEL, s), BF16), _sds((s, D_MODEL)), _sds((s, D_HALF)), _sds((s, D_HALF)), _sds((s, D_HALF), BF16),
                   _sds((1, HEAD)), _sds((1, D_MODEL)), _sds((1, HEAD))],
        scratch_shapes=[pltpu.VMEM((t, D_MODEL), BF16)],
        compiler_params=_params("arbitrary"),
    )(x, y_pool, o, proj, target, w_out, dn_norm_w, final_norm_w)


def _token_matmul(name, at, pieces):
    m, s = at.shape
    n = len(pieces)
    tn, tk = D_HALF, 512

    def body(a_ref, *refs):
        p_refs, o_ref = refs[:n], refs[n]

        @pl.when(pl.program_id(0) == 0)
        def _():
            o_ref[...] = jnp.zeros_like(o_ref)

        av = a_ref[...]
        for p in range(n):
            o_ref[:, p * tn:(p + 1) * tn] += _bdot(av, p_refs[p][...])

    return _call(
        body, name=name, grid=(s // tk,),
        in_specs=[pl.BlockSpec((m, tk), lambda k: (0, k))]
                 + [pl.BlockSpec((tk, tn), functools.partial(lambda k, cb: (k, cb), cb=cb)) for _, cb in pieces],
        out_specs=pl.BlockSpec((m, n * tn), lambda k: (0, 0)),
        out_shape=_sds((m, n * tn)),
        compiler_params=_params("arbitrary"),
    )(at, *[p[0] for p in pieces])


def _pool_bwd(proj, dyp, pool_w, pool_scale):
    s = proj.shape[0]
    t = POOL_T
    hb = t // HEAD
    last = s // HEAD - 1

    def body(u_ref, z_ref, halo_ref, dy_ref, zn_ref, dyn_ref, pw_ref, ps_ref, band_ref, aband_ref,
             du_ref, dz_ref, gpw_ref, gps_ref):
        i = pl.program_id(0)

        @pl.when(i == 0)
        def _():
            gpw_ref[...] = jnp.zeros_like(gpw_ref)
            gps_ref[...] = jnp.zeros_like(gps_ref)

        live = (i > 0).astype(F32)
        more = (i < pl.num_programs(0) - 1).astype(F32)
        groups = lambda ref: [ref[:, sl] for sl in HEAD_COLS]
        z, ps, dy = groups(z_ref), groups(ps_ref), groups(dy_ref)
        pw = [pw_ref[g] for g in HEADS]
        mix, mixed, sg, cnt = _pool_mix(groups(u_ref), [h * live for h in groups(halo_ref)], z, pw,
                                        [band_ref[g] for g in HEADS], i * t)
        sz = _each(lambda a, b: a * b, z, sg)
        for sl, d, m, p, s_, zg in zip(HEAD_COLS, dy, mixed, ps, sg, z):
            dz_ref[:, sl] = (d * m * p * (s_ * (1.0 + zg * (1.0 - s_)))).astype(BF16)
        for sl, d, m, a in zip(HEAD_COLS, dy, mixed, sz):
            gps_ref[:, sl] += _colsum(d * m * a)
        dmixed = _each(lambda d, p, a: d * p * a, dy, ps, sz)
        for g, gp in enumerate(_each(_bdot_tn, mix, dmixed)):
            gpw_ref[g] += gp
        dmix = _each(_bdot_nt, dmixed, pw)
        dmix_n = _each(lambda d, p, zn, w_: _bdot_nt(d * more * p * (zn * _sigmoid(zn)), w_),
                       groups(dyn_ref), ps, groups(zn_ref), pw)
        scaled = [jnp.concatenate([a / c, b * (1.0 / w)], axis=0) for a, c, b, w in zip(dmix, cnt, dmix_n, WINDOWS)]
        du = _each(lambda b, s_, d: _mask_dot(b, s_) - d, [aband_ref[g] for g in HEADS], scaled, dmix)
        for sl, v in zip(HEAD_COLS, du):
            du_ref[:, sl] = v.astype(BF16)

    tile = lambda col: pl.BlockSpec((t, D_HALF), lambda i: (i, col))
    below = lambda col: pl.BlockSpec((HEAD, D_HALF), lambda i: (jnp.minimum((i + 1) * hb, last), col))
    return _call(
        body, name="pool_bwd", grid=(s // t,),
        in_specs=[tile(0), tile(1), pl.BlockSpec((HEAD, D_HALF), lambda i: (jnp.maximum(i * hb - 1, 0), 0)),
                  tile(0), below(1), below(0),
                  pl.BlockSpec((N_HEADS, HEAD, HEAD), lambda i: (0, 0, 0)), pl.BlockSpec((1, D_HALF), lambda i: (0, 0)),
                  pl.BlockSpec((N_HEADS, t, HEAD + t), lambda i: (0, 0, 0)),
                  pl.BlockSpec((N_HEADS, t, HEAD + t), lambda i: (0, 0, 0))],
        out_specs=[tile(0), tile(0), pl.BlockSpec((N_HEADS, HEAD, HEAD), lambda i: (0, 0, 0)),
                   pl.BlockSpec((1, D_HALF), lambda i: (0, 0))],
        out_shape=[_sds((s, D_HALF), BF16), _sds((s, D_HALF), BF16), _sds((N_HEADS, HEAD, HEAD)), _sds((1, D_HALF))],
        compiler_params=_params("arbitrary"),
    )(proj, proj, proj, dyp, proj, dyp, pool_w, pool_scale, _pool_bands(t), _pool_bands(t, anti=True))


def _scan_bwd(do, vn, qd, kd, w, att, cd, st):
    s = do.shape[0]
    n_steps = s // SCAN_ROWS

    def body(do_ref, vn_ref, qd_ref, kd_ref, w_ref, att_ref, cd_ref, st_ref,
             du_ref, dw_ref, datt_ref, dqd_ref, dkd_ref, dcd_ref, dstate):
        @pl.when(pl.program_id(0) == 0)
        def _():
            dstate[...] = jnp.zeros_like(dstate)
        _, incl, _, _ = _pair_masks()
        cols = list(enumerate(HEAD_COLS))
        dv_intra = []
        for pp in range(SCAN_PAIRS):
            rp = slice(pp * PAIR, (pp + 1) * PAIR)
            dv_intra.append([_bdot_tn(att_ref[rp, sl], do_ref[rp, sl]) for _, sl in cols])
            for _, sl in cols:
                datt_ref[rp, sl] = jnp.where(incl, _bdot_nt(do_ref[rp, sl], vn_ref[rp, sl]), 0.0)
        ds = [dstate[h] for h in HEADS]
        for ci in range(2 * SCAN_PAIRS - 1, -1, -1):
            rs = slice(ci * CHUNK, (ci + 1) * CHUNK)
            in_pair = slice((ci % 2) * CHUNK, (ci % 2 + 1) * CHUNK)
            sm = [st_ref[ci, h] for h in HEADS]
            dvn = [dv_intra[ci // 2][h][in_pair] + _bdot(kd_ref[rs, sl], ds[h]) for h, sl in cols]
            for h, sl in cols:
                du_ref[rs, sl] = dvn[h].astype(BF16)
            dqd = [_bdot_nt(do_ref[rs, sl], sm[h]) for h, sl in cols]
            dw = [-_bdot_nt(dvn[h], sm[h]) for h, _ in cols]
            dkd = [_bdot_nt(vn_ref[rs, sl], ds[h]) for h, sl in cols]
            dcd = [jnp.broadcast_to(_rowsum(_colsum(ds[h] * sm[h])), (8, HEAD)) for h in HEADS]
            for h, sl in cols:
                dqd_ref[rs, sl] = dqd[h]
                dw_ref[rs, sl] = dw[h].astype(BF16)
                dkd_ref[rs, sl] = dkd[h]
                dcd_ref[ci * 8:(ci + 1) * 8, sl] = dcd[h]
            ds = [ds[h] * cd_ref[ci * 8:ci * 8 + 1, sl] + _bdot_tn(qd_ref[rs, sl], do_ref[rs, sl])
                  - _bdot_tn(w_ref[rs, sl], dvn[h]) for h, sl in cols]
        for h in HEADS:
            dstate[h] = ds[h]

    rev = pl.BlockSpec((SCAN_ROWS, D_HALF), lambda i: (n_steps - 1 - i, 0))
    rev_scalar = _chunk_scalar_spec(SCAN_PAIRS, lambda i: (n_steps - 1 - i, 0))
    return _call(
        body, name="scan_bwd", grid=(n_steps,),
        in_specs=[rev] * 6 + [rev_scalar,
                              pl.BlockSpec((2 * SCAN_PAIRS, N_HEADS, HEAD, HEAD), lambda i: (n_steps - 1 - i, 0, 0, 0))],
        out_specs=[rev] * 5 + [rev_scalar],
        out_shape=[_sds((s, D_HALF), BF16)] * 2 + [_sds((s, D_HALF))] * 3 + [_sds((s // 8, D_HALF))],
        scratch_shapes=[pltpu.VMEM((N_HEADS, HEAD, HEAD), F32)],
        compiler_params=_params("arbitrary"),
    )(do, vn, qd, kd, w, att, cd, st)


def _intra_bwd(qn, kn, vs, beta, g, tm, du, dw, datt, dqd, dkd, dcd):
    s = qn.shape[0]

    def body(qn_ref, kn_ref, vs_ref, beta_ref, g_ref, t_ref, du_ref, dw_ref, datt_ref, dqd_ref, dkd_ref, dcd_ref,
             dqn_ref, dkn_ref, dvs_ref, dbeta_ref, dg_ref):
        ones = jnp.ones((PAIR, HEAD), BF16)
        tn = (((0,), (0,)), ((), ()))
        kn, vs, beta = _heads(kn_ref), _heads(vs_ref), _heads(beta_ref)
        cm = _pair_common(_heads(qn_ref), kn, vs, beta, _heads(g_ref))
        tmv, duv, dwv, dattv, dqdv, dkdv = (_heads(r) for r in (t_ref, du_ref, dw_ref, datt_ref, dqd_ref, dkd_ref))
        dvb = _each(_bdot_tn, tmv, duv)
        dt = _each(lambda a, b, c, d: _bdot_nt(a, b) + _bdot_nt(c, d), duv, cm["vb"], dwv, cm["kbg"])
        dkbg = _each(_bdot_tn, tmv, dwv)
        m1 = _each(_bdot_tn, tmv, dt)
        da = _each(lambda a, b: -jnp.where(cm["strict"], _bdot_nt(a, b), 0.0), m1, tmv)
        dkk = _each(lambda a, b: a * b, da, cm["decay"])
        dqk = _each(lambda a, b: a * b, dattv, cm["decay"])
        dd = _each(lambda a, b, c, d: a * b + c * d, dkk, cm["kk"], dqk, cm["qk"])
        dkb = _each(lambda a, b, c, d: _bdot(a, b) + c * d, dkk, kn, dkbg, cm["egc"])
        dq = _each(lambda a, b, c, d: _bdot(a, b) + c * d, dqk, kn, dqdv, cm["egc"])
        dkn = _each(lambda a, b, c, d: _bdot_tn(a, b) + _bdot_tn(c, d), dkk, cm["kb"], dqk, cm["q"])
        dkn = _each(lambda a, b, c, d, e: a + b * c + d * e, dkn, dkdv, cm["ekd"], dkb, beta)
        t_kd = _each(lambda a, b, c: _rowsum(a * b * c), dkdv, kn, cm["ekd"])
        split = _each(_split, dd)
        rows_dd = [jnp.dot(hi, ones, preferred_element_type=F32) + jnp.dot(lo, ones, preferred_element_type=F32)
                   for hi, lo in split]
        cols_dd = [lax.dot_general(hi, ones, tn, preferred_element_type=F32)
                   + lax.dot_general(lo, ones, tn, preferred_element_type=F32) for hi, lo in split]
        dgc = _each(lambda r, c, a, b, e, f, k, t: r - c + _rowsum(a * b * e) + _rowsum(f * k) - t,
                    rows_dd, cols_dd, dqdv, cm["q"], cm["egc"], dkbg, cm["kbg"], t_kd)
        same_b = cm["same"].astype(BF16)
        rowi = lax.broadcasted_iota(I32, (PAIR, HEAD), 0)
        dcd = _each(lambda d: jnp.where(rowi < CHUNK, d[0:1], d[8:9]), _heads(dcd_ref, rows=16))
        dgl = _each(lambda t, d, c: _mask_dot(same_b, jnp.broadcast_to(t, (PAIR, HEAD))) + d * c, t_kd, dcd, cm["cd"])
        is_last = jnp.bitwise_and(rowi, CHUNK - 1) == CHUNK - 1
        dgc = _each(lambda a, b: a + jnp.where(is_last, b, 0.0), dgc, dgl)
        r = lax.broadcasted_iota(I32, (PAIR, PAIR), 0)
        c = lax.broadcasted_iota(I32, (PAIR, PAIR), 1)
        upper_b = (cm["same"] & (r <= c)).astype(BF16)
        _put_heads(dg_ref, _each(lambda v: _mask_dot(upper_b, v), dgc))
        _put_heads(dbeta_ref, _each(lambda a, b, c, d: jnp.broadcast_to(_rowsum(a * b) + _rowsum(c * d), (PAIR, HEAD)),
                                    dkb, kn, dvb, vs))
        _put_heads(dqn_ref, _each(lambda v: v * QK_SCALE, dq))
        _put_heads(dkn_ref, dkn)
        _put_heads(dvs_ref, _each(lambda a, b: a * b, dvb, beta))

    return _call(
        body, name="intra_bwd", grid=(s // (INTRA_PAIRS * PAIR),),
        in_specs=[_pair_spec()] * 11 + [_chunk_scalar_spec(INTRA_PAIRS)], out_specs=[_pair_spec()] * 5,
        out_shape=[_sds((s, D_HALF))] * 5,
        compiler_params=_params("arbitrary"),
    )(qn, kn, vs, beta, g, tm, du, dw, datt, dqd, dkd, dcd)


def _rows8(x):
    acc = x[0:8]
    for r in range(8, x.shape[0], 8):
        acc = acc + x[r:r + 8]
    return acc


def _conv_bwd(proj, conv_w, a_log, dt_bias, dqn, dkn, dvs, dbeta, dg):
    s = proj.shape[0]
    t = CONV_T
    n_tiles = s // t
    n_sub = t // CONV_SUB
    tile_of = lambda i: n_tiles - 1 - i

    def body(q_ref, k_ref, v_ref, hq_ref, hk_ref, hv_ref, ba_ref, cw_ref, al_ref, dtb_ref,
             dqn_ref, dkn_ref, dvs_ref, dbeta_ref, dg_ref, oq_ref, ok_ref, ov_ref, dba_ref, gcw_out, gsm_out,
             below, gcw_ref, gsm_ref):
        @pl.when(pl.program_id(0) == 0)
        def _():
            gcw_ref[...] = jnp.zeros_like(gcw_ref)
            gsm_ref[...] = jnp.zeros_like(gsm_ref)
            below[...] = jnp.zeros_like(below)

        live = (pl.program_id(0) < n_tiles - 1).astype(F32)
        parts = ((q_ref, hq_ref, dqn_ref, oq_ref), (k_ref, hk_ref, dkn_ref, ok_ref), (v_ref, hv_ref, dvs_ref, ov_ref))
        lane = lax.broadcasted_iota(I32, (CONV_SUB, HEAD), 1)
        lane8 = lax.broadcasted_iota(I32, (8, HEAD), 1)

        def sub_tile(r0, first):
            rows = pl.ds(r0, CONV_SUB)
            for p, (x_ref, h_ref, d_ref, o_ref) in enumerate(parts):
                for h in HEADS:
                    cs = HEAD_COLS[h]
                    wide = slice(p * D_HALF + h * HEAD, p * D_HALF + (h + 1) * HEAD)
                    cw = cw_ref[:, wide]
                    prev8 = h_ref[:, cs] * live if first else x_ref[pl.ds(r0 - 8, 8), cs]
                    taps = _conv_taps(x_ref[rows, cs], prev8)
                    y = _conv_pre(taps, cw)
                    sg = _sigmoid(y)
                    sv = y * sg
                    ds = d_ref[rows, cs]
                    if p < 2:
                        rn = lax.rsqrt(_rowsum(sv * sv) + EPS)
                        nrm = sv * rn
                        ds = rn * (ds - nrm * _rowsum(ds * nrm))
                    dy = ds * (sg * (1.0 + y * (1.0 - sg)))
                    for j in range(CONV_K):
                        gcw_ref[8 * j:8 * j + 8, wide] += _rows8(dy * taps[j])
                    nxt = below[:, wide]
                    acc = dy * cw[CONV_K - 1:CONV_K]
                    for sft in range(1, CONV_K):
                        acc = acc + _shift_up(dy, nxt, sft) * cw[CONV_K - 1 - sft:CONV_K - sft]
                    o_ref[rows, cs] = acc.astype(BF16)
                    below[:, wide] = dy[0:8]

            ba = ba_ref[rows, :]
            dba = jnp.zeros((CONV_SUB, HEAD), F32)
            gsm = jnp.zeros((8, HEAD), F32)
            for h in HEADS:
                beta = _sigmoid(ba[:, h:h + 1])
                dbeta = dbeta_ref[rows, h * HEAD:h * HEAD + 1]
                xg = ba[:, N_HEADS + h:N_HEADS + h + 1] + dtb_ref[0:1, h:h + 1]
                nexp = -jnp.exp(al_ref[0:1, h:h + 1])
                dgv = dg_ref[rows, h * HEAD:h * HEAD + 1]
                da = dgv * nexp * _sigmoid(xg)
                dba = dba + jnp.where(lane == h, dbeta * beta * (1.0 - beta), 0.0) + jnp.where(lane == N_HEADS + h, da, 0.0)
                gsm = (gsm + jnp.where(lane8 == h, _rows8(dgv * nexp * _softplus(xg)), 0.0)
                       + jnp.where(lane8 == N_HEADS + h, _rows8(da), 0.0))
            dba_ref[rows, :] = jnp.zeros((CONV_SUB, D_HALF), BF16)
            dba_ref[rows, :HEAD] = dba.astype(BF16)
            gsm_ref[...] += gsm

        def step(k, carry):
            sub_tile(pl.multiple_of((n_sub - 1 - k) * CONV_SUB, CONV_SUB), False)
            return carry

        lax.fori_loop(0, n_sub - 1, step, 0)
        sub_tile(0, True)

        @pl.when(pl.program_id(0) == n_tiles - 1)
        def _():
            gcw_out[...] = jnp.zeros_like(gcw_out)
            for j in range(CONV_K):
                gcw_out[j:j + 1, :] = _colsum(gcw_ref[8 * j:8 * j + 8, :])
            gsm_out[...] = jnp.broadcast_to(_colsum(gsm_ref[...]), (8, HEAD))

    row = pl.BlockSpec((t, D_HALF), lambda i: (tile_of(i), 0))
    const = lambda shape: pl.BlockSpec(shape, lambda i: (0, 0))
    return _call(
        body, name="conv_bwd", grid=(n_tiles,),
        in_specs=_conv_specs(t, tile_of) + [pl.BlockSpec((t, HEAD), lambda i: (tile_of(i), COL_BA // HEAD)),
                                            const((CONV_K, 3 * D_HALF)), const((1, N_HEADS)), const((1, N_HEADS))] + [row] * 5,
        out_specs=[row, row, row, row, const((8, 3 * D_HALF)), const((8, HEAD))],
        out_shape=[_sds((s, D_HALF), BF16)] * 4 + [_sds((8, 3 * D_HALF)), _sds((8, HEAD))],
        scratch_shapes=[pltpu.VMEM((8, 3 * D_HALF), F32), pltpu.VMEM((8 * CONV_K, 3 * D_HALF), F32),
                        pltpu.VMEM((8, HEAD), F32)],
        compiler_params=_params("arbitrary"),
    )(proj, proj, proj, proj, proj, proj, proj, conv_w, a_log, dt_bias, dqn, dkn, dvs, dbeta, dg)


def _conv_bwd_pre(proj, conv_w, a_log, dt_bias, dqn, dkn, dvs, dbeta, dg):
    s = proj.shape[0]
    t = CONV_T

    def body(q_ref, k_ref, v_ref, hq_ref, hk_ref, hv_ref, ba_ref, cw_ref, al_ref, dtb_ref,
             dqn_ref, dkn_ref, dvs_ref, dbeta_ref, dg_ref, dyq_ref, dyk_ref, dyv_ref, dba_ref, gcw_ref, gsm_ref):
        @pl.when(pl.program_id(0) == 0)
        def _():
            gcw_ref[...] = jnp.zeros_like(gcw_ref)
            gsm_ref[...] = jnp.zeros_like(gsm_ref)

        live = (pl.program_id(0) > 0).astype(F32)
        parts = ((q_ref, hq_ref, dqn_ref, dyq_ref), (k_ref, hk_ref, dkn_ref, dyk_ref), (v_ref, hv_ref, dvs_ref, dyv_ref))
        for p, (x_ref, h_ref, d_ref, dy_ref) in enumerate(parts):
            cols = slice(p * D_HALF, (p + 1) * D_HALF)
            taps = _conv_taps(x_ref[...], h_ref[...] * live)
            y = _conv_pre(taps, cw_ref[:, cols])
            sg = _sigmoid(y)
            sv = y * sg
            if p == 2:
                ds = d_ref[...]
            else:
                segs = []
                for h in HEADS:
                    seg = _head(sv, h)
                    rn = lax.rsqrt(_rowsum(seg * seg) + EPS)
                    nrm = seg * rn
                    dn = d_ref[:, HEAD_COLS[h]]
                    segs.append(rn * (dn - nrm * _rowsum(dn * nrm)))
                ds = jnp.concatenate(segs, axis=1)
            dy = ds * (sg * (1.0 + y * (1.0 - sg)))
            dy_ref[...] = dy
            for j in range(CONV_K):
                gcw_ref[j:j + 1, cols] += _colsum(dy * taps[j])

        ba = ba_ref[...]
        lane = lax.broadcasted_iota(I32, (t, HEAD), 1)
        lane1 = lax.broadcasted_iota(I32, (1, HEAD), 1)
        dba = jnp.zeros((t, HEAD), F32)
        gsm = jnp.zeros((1, HEAD), F32)
        for h in HEADS:
            beta = _sigmoid(ba[:, h:h + 1])
            dbeta = dbeta_ref[:, h * HEAD:h * HEAD + 1]
            xg = ba[:, N_HEADS + h:N_HEADS + h + 1] + dtb_ref[0:1, h:h + 1]
            nexp = -jnp.exp(al_ref[0:1, h:h + 1])
            dgv = dg_ref[:, h * HEAD:h * HEAD + 1]
            da = dgv * nexp * _sigmoid(xg)
            dba = dba + jnp.where(lane == h, dbeta * beta * (1.0 - beta), 0.0) + jnp.where(lane == N_HEADS + h, da, 0.0)
            gsm = (gsm + jnp.where(lane1 == h, _colsum(dgv * nexp * _softplus(xg)), 0.0)
                   + jnp.where(lane1 == N_HEADS + h, _colsum(da), 0.0))
        dba_ref[...] = jnp.zeros_like(dba_ref)
        dba_ref[:, :HEAD] = dba.astype(BF16)
        gsm_ref[0:1, :] += gsm

    row = pl.BlockSpec((t, D_HALF), lambda i: (i, 0))
    return _call(
        body, name="conv_bwd_pre", grid=(s // t,),
        in_specs=_conv_specs(t) + [pl.BlockSpec((t, HEAD), lambda i: (i, COL_BA // HEAD)),
                                   pl.BlockSpec((CONV_K, 3 * D_HALF), lambda i: (0, 0)),
                                   pl.BlockSpec((1, N_HEADS), lambda i: (0, 0)),
                                   pl.BlockSpec((1, N_HEADS), lambda i: (0, 0))] + [row] * 5,
        out_specs=[row, row, row, row,
                   pl.BlockSpec((8, 3 * D_HALF), lambda i: (0, 0)), pl.BlockSpec((8, HEAD), lambda i: (0, 0))],
        out_shape=[_sds((s, D_HALF))] * 3 + [_sds((s, D_HALF), BF16), _sds((8, 3 * D_HALF)), _sds((8, HEAD))],
        compiler_params=_params("arbitrary"),
    )(proj, proj, proj, proj, proj, proj, proj, conv_w, a_log, dt_bias, dqn, dkn, dvs, dbeta, dg)


def _conv_bwd_in(dyq, dyk, dyv, conv_w):
    s = dyq.shape[0]
    t = CONV_T
    last = s // 8 - 1

    def body(q_ref, k_ref, v_ref, nq_ref, nk_ref, nv_ref, cw_ref, oq_ref, ok_ref, ov_ref):
        more = (pl.program_id(0) < pl.num_programs(0) - 1).astype(F32)
        for p, (d_ref, n_ref, o_ref) in enumerate(((q_ref, nq_ref, oq_ref), (k_ref, nk_ref, ok_ref), (v_ref, nv_ref, ov_ref))):
            cw = cw_ref[:, p * D_HALF:(p + 1) * D_HALF]
            dy = d_ref[...]
            nxt = n_ref[...] * more
            acc = dy * cw[3:4]
            for sft in (1, 2, 3):
                acc = acc + _shift_up(dy, nxt, sft) * cw[3 - sft:4 - sft]
            o_ref[...] = acc.astype(BF16)

    row = pl.BlockSpec((t, D_HALF), lambda i: (i, 0))
    nxt = pl.BlockSpec((8, D_HALF), lambda i: (jnp.minimum((i + 1) * (t // 8), last), 0))
    return _call(
        body, name="conv_bwd_in", grid=(s // t,),
        in_specs=[row] * 3 + [nxt] * 3 + [pl.BlockSpec((CONV_K, 3 * D_HALF), lambda i: (0, 0))],
        out_specs=[row] * 3, out_shape=[_sds((s, D_HALF), BF16)] * 3,
        compiler_params=_params("arbitrary"),
    )(dyq, dyk, dyv, dyq, dyk, dyv, conv_w)


IN_T = 512


def _in_bwd(x, dh, norm_w, w_pad, pieces):
    s = x.shape[0]
    t = IN_T
    widths = [D_HALF] * 6 + [N_IN_PAD - COL_BA]

    def body(*refs):
        x_ref, dh_ref, nw_ref, w_ref = refs[:4]
        p_refs = refs[4:4 + len(pieces)]
        gx_ref, gnw_ref = refs[4 + len(pieces):]

        @pl.when(pl.program_id(0) == 0)
        def _():
            gnw_ref[...] = jnp.zeros_like(gnw_ref)

        dn = jnp.zeros((t, D_MODEL), F32)
        col = 0
        for p_ref, wd in zip(p_refs, widths):
            dn = dn + _bdot_nt(p_ref[...], w_ref[:, col:col + wd])
            col += wd
        xv = x_ref[...]
        r = lax.rsqrt(jnp.mean(xv * xv, axis=-1, keepdims=True) + EPS)
        xhat = xv * r
        gnw_ref[...] += _colsum(dn * xhat)
        dxh = dn * nw_ref[...]
        gx_ref[...] = dh_ref[...] + r * (dxh - xhat * jnp.mean(dxh * xhat, axis=-1, keepdims=True))

    wide = pl.BlockSpec((t, D_MODEL), lambda i: (i, 0))
    return _call(
        body, name="in_bwd", grid=(s // t,),
        in_specs=[wide, wide, pl.BlockSpec((1, D_MODEL), lambda i: (0, 0)),
                  pl.BlockSpec((D_MODEL, N_IN_PAD), lambda i: (0, 0))]
                 + [pl.BlockSpec((t, wd), lambda i: (i, 0)) for wd in widths],
        out_specs=[wide, pl.BlockSpec((1, D_MODEL), lambda i: (0, 0))],
        out_shape=[_sds((s, D_MODEL)), _sds((1, D_MODEL))],
        compiler_params=_params("arbitrary"),
    )(x, dh, norm_w, w_pad, *pieces)


def _adamw_shard(name, w, g_own, g_got, cidx, m, v):
    _, r, c = w.shape
    half = r // 2
    rows = 256 if half % 256 == 0 else half
    per_half = half // rows

    def body(c_ref, w_ref, go_ref, gg_ref, m_ref, v_ref, gout_ref, d_ref, nm_ref, nv_ref):
        mine = (pl.program_id(0) // per_half) == c_ref[0]
        gv = jnp.where(mine, go_ref[:, :c], gg_ref[:, :c])
        gout_ref[0] = gv
        mn = ADAM_B1 * m_ref[0] + (1.0 - ADAM_B1) * gv
        vn = ADAM_B2 * v_ref[0] + (1.0 - ADAM_B2) * (gv * gv)
        m_hat = mn / (1.0 - ADAM_B1 ** ADAM_STEP)
        v_hat = vn / (1.0 - ADAM_B2 ** ADAM_STEP)
        d_ref[0] = -ADAM_LR * (m_hat / (jnp.sqrt(v_hat) + ADAM_EPS) + ADAM_WD * w_ref[0])
        nm_ref[0] = mn
        nv_ref[0] = vn

    blk = pl.BlockSpec((1, rows, c), lambda i, c_ref: (0, i, 0))
    gblk = pl.BlockSpec((rows, g_own.shape[1]), lambda i, c_ref: (i % per_half, 0))
    return _call(
        body, name=name,
        grid_spec=pltpu.PrefetchScalarGridSpec(
            num_scalar_prefetch=1, grid=(2 * per_half,),
            in_specs=[blk, gblk, gblk, blk, blk], out_specs=[blk] * 4),
        out_shape=[_sds((1, r, c))] * 4,
        compiler_params=_params("arbitrary"),
    )(cidx, w, g_own, g_got, m, v)


def _adamw_tiles(name, w, g, m, v):
    n = w.shape[0]
    nb = 77 if n % 77 == 0 else n

    def body(w_ref, g_ref, m_ref, v_ref, d_ref, nm_ref, nv_ref):
        gv = g_ref[...]
        mn = ADAM_B1 * m_ref[...] + (1.0 - ADAM_B1) * gv
        vn = ADAM_B2 * v_ref[...] + (1.0 - ADAM_B2) * (gv * gv)
        m_hat = mn / (1.0 - ADAM_B1 ** ADAM_STEP)
        v_hat = vn / (1.0 - ADAM_B2 ** ADAM_STEP)
        d_ref[...] = -ADAM_LR * (m_hat / (jnp.sqrt(v_hat) + ADAM_EPS) + ADAM_WD * w_ref[...])
        nm_ref[...] = mn
        nv_ref[...] = vn

    blk = pl.BlockSpec((nb, 8, HEAD), lambda i: (i, 0, 0))
    return _call(
        body, name=name, grid=(n // nb,),
        in_specs=[blk] * 4, out_specs=[blk] * 3, out_shape=[_sds(w.shape)] * 3,
        compiler_params=_params("arbitrary"),
    )(w, g, m, v)


def _exchange(name, inputs, out_shapes, phases):
    n_in = len(inputs)
    n_out = len(out_shapes)
    n_cp = sum(len(p) for p in phases)

    def body(*refs):
        ins, outs = refs[:n_in], refs[n_in:n_in + n_out]
        send, recv = refs[n_in + n_out:]
        pos = (lax.axis_index("x"), lax.axis_index("y"), lax.axis_index("c"))
        k = 0
        for phase in phases:
            cps = []
            for src, dst, target in phase:
                cps.append(pltpu.make_async_remote_copy(
                    src_ref=src(ins, outs, pos), dst_ref=dst(ins, outs, pos), send_sem=send.at[k], recv_sem=recv.at[k],
                    device_id=target(pos), device_id_type=pl.DeviceIdType.MESH))
                k += 1
            for cp in cps:
                cp.start()
            for cp in cps:
                cp.wait()

    anyspec = pl.BlockSpec(memory_space=pl.ANY)
    return _call(
        body, name=name,
        in_specs=[anyspec] * n_in, out_specs=[anyspec] * n_out, out_shape=list(out_shapes),
        scratch_shapes=[pltpu.SemaphoreType.DMA((n_cp,)), pltpu.SemaphoreType.DMA((n_cp,))],
    )(*inputs)


def _exchange_start(name, inputs, out_shapes, copies):
    n_in, n_out, n_cp = len(inputs), len(out_shapes), len(copies)

    def body(*refs):
        ins, lands = refs[:n_in], refs[n_in:n_in + n_out]
        sems = refs[n_in + n_out:n_in + n_out + 2 * n_cp]
        token = refs[-1]
        pos = (lax.axis_index("x"), lax.axis_index("y"), lax.axis_index("c"))
        for k, (src, dst, target) in enumerate(copies):
            pltpu.make_async_remote_copy(
                src_ref=src(ins, lands, pos), dst_ref=dst(ins, lands, pos), send_sem=sems[2 * k], recv_sem=sems[2 * k + 1],
                device_id=target(pos), device_id_type=pl.DeviceIdType.MESH).start()
        token[...] = jnp.zeros_like(token)

    hbm = pl.BlockSpec(memory_space=pltpu.HBM)
    sem = pl.BlockSpec(memory_space=pltpu.SEMAPHORE)
    bufs = list(inputs) + [lax.empty(o.shape, o.dtype) for o in out_shapes]
    outs = _call(
        body, name=name,
        out_shape=tuple([pltpu.SemaphoreType.DMA(())] * (2 * n_cp) + [pltpu.HBM(b.shape, b.dtype) for b in bufs]
                        + [_sds((8, HEAD))]),
        in_specs=[hbm] * len(bufs),
        out_specs=tuple([sem] * (2 * n_cp) + [hbm] * len(bufs) + [pl.BlockSpec(memory_space=pltpu.VMEM)]),
        input_output_aliases={i: 2 * n_cp + i for i in range(len(bufs))},
        compiler_params=pltpu.CompilerParams(has_side_effects=pltpu.SideEffectType.DATAFLOW_SIDE_EFFECTING),
    )(*[pltpu.with_memory_space_constraint(b, pltpu.HBM) for b in bufs])
    return outs[:2 * n_cp], outs[2 * n_cp:2 * n_cp + n_in], outs[2 * n_cp + n_in:-1], outs[-1]


def _exchange_wait(name, sems, sources, lands, copies, after):
    n_in, n_out, n_cp = len(sources), len(lands), len(copies)

    def body(*refs):
        ins, zones = refs[:n_in], refs[n_in:n_in + n_out]
        sem_refs = refs[n_in + n_out:n_in + n_out + 2 * n_cp]
        pos = (lax.axis_index("x"), lax.axis_index("y"), lax.axis_index("c"))
        for k, (src, dst, target) in enumerate(copies):
            cp = pltpu.make_async_remote_copy(
                src_ref=src(ins, zones, pos), dst_ref=dst(ins, zones, pos), send_sem=sem_refs[2 * k],
                recv_sem=sem_refs[2 * k + 1], device_id=target(pos), device_id_type=pl.DeviceIdType.MESH)
            cp.wait_send()
            cp.wait_recv()

    hbm = pl.BlockSpec(memory_space=pltpu.HBM)
    sem = pl.BlockSpec(memory_space=pltpu.SEMAPHORE)
    bufs = list(sources) + list(lands)
    outs = _call(
        body, name=name,
        out_shape=tuple(pltpu.HBM(b.shape, b.dtype) for b in bufs),
        in_specs=[hbm] * len(bufs) + [sem] * (2 * n_cp) + [pl.BlockSpec(memory_space=pl.ANY)],
        out_specs=tuple([hbm] * len(bufs)),
        input_output_aliases={i: i for i in range(len(bufs))},
        compiler_params=pltpu.CompilerParams(has_side_effects=pltpu.SideEffectType.DATAFLOW_SIDE_EFFECTING),
    )(*bufs, *sems, after)
    return outs[:n_in], outs[n_in:]


def _allreduce_tile(name, v):
    def body(v_ref, out_ref, slots, send, recv):
        x, y, c = lax.axis_index("x"), lax.axis_index("y"), lax.axis_index("c")
        me = 4 * x + 2 * y + c
        slots[me] = v_ref[...]
        cps = []
        for k in range(1, 8):
            peer = (x ^ (k >> 2), y ^ ((k >> 1) & 1), c ^ (k & 1))
            cps.append(pltpu.make_async_remote_copy(
                src_ref=v_ref, dst_ref=slots.at[me], send_sem=send.at[k - 1], recv_sem=recv.at[k - 1],
                device_id=peer, device_id_type=pl.DeviceIdType.MESH))
        for cp in cps:
            cp.start()
        for cp in cps:
            cp.wait()
        acc = slots[0]
        for i in range(1, 8):
            acc = acc + slots[i]
        out_ref[...] = acc

    vm = pl.BlockSpec(memory_space=pltpu.VMEM)
    return _call(
        body, name=name, in_specs=[vm], out_specs=vm, out_shape=_sds(v.shape),
        scratch_shapes=[pltpu.VMEM((8,) + v.shape, F32), pltpu.SemaphoreType.DMA((7,)), pltpu.SemaphoreType.DMA((7,))],
    )(v)


def _chip(pos):
    return 2 * pos[0] + pos[1]


def _other_chip(pos, mask):
    x, y, c = pos
    return (x ^ (mask >> 1), y ^ (mask & 1), c)


def _sibling(pos):
    return (pos[0], pos[1], 1 - pos[2])


def _gather_weights(wb, cb):
    rows = wb.shape[0] // 2

    def half(pos):
        return pl.ds(pos[2] * rows, rows)

    first, second = [], []
    for mask in CHIP_MASKS:
        first.append((lambda ins, outs, pos: ins[0].at[half(pos)],
                      lambda ins, outs, pos: outs[0].at[_chip(pos), half(pos)],
                      functools.partial(_other_chip, mask=mask)))
        second.append((lambda ins, outs, pos, mask=mask: outs[0].at[_chip(pos) ^ mask, half(pos)],
                       lambda ins, outs, pos, mask=mask: outs[0].at[_chip(pos) ^ mask, half(pos)],
                       _sibling))
        first.append((lambda ins, outs, pos: ins[1],
                      lambda ins, outs, pos: outs[1].at[_chip(pos)],
                      functools.partial(_other_chip, mask=mask)))
    return _exchange("gather_weights", [wb, cb], [_sds((4,) + wb.shape, wb.dtype), _sds((4,) + cb.shape, cb.dtype)],
                     [first, second])


def _gather_blocks(ob):
    copies = [(lambda ins, outs, pos: ins[0], lambda ins, outs, pos: outs[0].at[_chip(pos)],
               functools.partial(_other_chip, mask=mask)) for mask in CHIP_MASKS]
    return [_sds((4,) + ob.shape, ob.dtype)], copies


def _to_sibling_half(name, arrays):
    def src(ins, outs, pos, a):
        h = arrays[a].shape[-2] // 2
        sl = pl.ds((1 - pos[2]) * h, h)
        return ins[a].at[:, sl] if arrays[a].ndim == 3 else ins[a].at[sl]

    outs = [_sds(a.shape[:-2] + (a.shape[-2] // 2, a.shape[-1]), a.dtype) for a in arrays]
    phase = [(functools.partial(src, a=a), lambda ins, outs, pos, a=a: outs[a], _sibling) for a in range(len(arrays))]
    return _exchange(name, arrays, outs, [phase])


def _add_half(name, full, part, cidx):
    shape = part.shape
    lead = shape[0] if len(shape) == 3 else 1
    rows, cols = shape[-2], shape[-1]
    tr = rows // 2 if rows % 16 == 0 else rows
    nr = rows // tr
    f3 = full.reshape((lead,) + full.shape[-2:])
    p3 = part.reshape((lead, rows, cols))

    def body(c_ref, f_ref, p_ref, o_ref):
        o_ref[...] = (f_ref[...].astype(F32) + p_ref[...].astype(F32)).astype(o_ref.dtype)

    out = _call(
        body, name=name,
        grid_spec=pltpu.PrefetchScalarGridSpec(
            num_scalar_prefetch=1, grid=(lead, nr),
            in_specs=[pl.BlockSpec((1, tr, cols), lambda b, r, c_ref: (b, c_ref[0] * nr + r, 0)),
                      pl.BlockSpec((1, tr, cols), lambda b, r, c_ref: (b, r, 0))],
            out_specs=pl.BlockSpec((1, tr, cols), lambda b, r, c_ref: (b, r, 0))),
        out_shape=_sds((lead, rows, cols), part.dtype),
        compiler_params=_params("arbitrary", "arbitrary"),
    )(cidx, f3, p3)
    return out.reshape(shape)


def _to_other_chips(arrays, blocked):
    def src(ins, outs, pos, a, mask):
        return ins[a].at[_chip(pos) ^ mask] if blocked[a] else ins[a]

    outs = [_sds((3,) + (a.shape[1:] if b else a.shape), a.dtype) for a, b in zip(arrays, blocked)]
    copies = []
    for mi, mask in enumerate(CHIP_MASKS):
        for a in range(len(arrays)):
            copies.append((functools.partial(src, a=a, mask=mask), lambda ins, outs, pos, a=a, mi=mi: outs[a].at[mi],
                           functools.partial(_other_chip, mask=mask)))
    return outs, copies


def _add_chips(name, own, got, jidx, blocked):
    rows, cols = got.shape[-2:]
    tr = rows // 2 if rows % 16 == 0 else rows
    nr = rows // tr
    o3 = own if blocked else own.reshape((1, rows, cols))

    def body(j_ref, o_ref, g_ref, out_ref):
        out_ref[...] = ((o_ref[0].astype(F32) + g_ref[0].astype(F32))
                        + (g_ref[1].astype(F32) + g_ref[2].astype(F32)))

    own_map = (lambda r, j_ref: (j_ref[0], r, 0)) if blocked else (lambda r, j_ref: (0, r, 0))
    return _call(
        body, name=name,
        grid_spec=pltpu.PrefetchScalarGridSpec(
            num_scalar_prefetch=1, grid=(nr,),
            in_specs=[pl.BlockSpec((1, tr, cols), own_map),
                      pl.BlockSpec((3, tr, cols), lambda r, j_ref: (0, r, 0))],
            out_specs=pl.BlockSpec((tr, cols), lambda r, j_ref: (r, 0))),
        out_shape=_sds((rows, cols)),
        compiler_params=_params("arbitrary"),
    )(jidx, o3, got)


def _to_sibling(name, arrays):
    phase = [(lambda ins, outs, pos, a=a: ins[a], lambda ins, outs, pos, a=a: outs[a], _sibling)
             for a in range(len(arrays))]
    return _exchange(name, arrays, [_sds(a.shape, a.dtype) for a in arrays], [phase])


def _local_step(x, target, w_pad, w_out, conv_w, norm_w, pool_w, pool_scale, a_log, dt_bias, dn_norm_w, final_norm_w):
    proj, n_t = _proj_fwd(x, norm_w, w_pad)
    y_pool = _pool_fwd(proj, pool_w, pool_scale)
    qn, kn, vs, beta, g = _conv_fwd(proj, conv_w, a_log, dt_bias)
    u, w, att, qd, kd, tm, cd = _intra_fwd(qn, kn, vs, beta, g)
    o, vn, st = _scan_fwd(u, w, att, qd, kd, cd)
    w_out = w_out(o) if callable(w_out) else w_out
    y_t, dh, dyp, do, ddz, loss, g_fnw, g_dnw = _out_fwd_bwd(x, y_pool, o, proj, target, w_out, dn_norm_w, final_norm_w)
    g_wout = _token_matmul("grad_w_out", y_t, [(dh, 0), (dh, 1)])
    dpu, dpz, g_pw, g_ps = _pool_bwd(proj, dyp, pool_w, pool_scale)
    du, dw, datt, dqd, dkd, dcd = _scan_bwd(do, vn, qd, kd, w, att, cd, st)
    dqn, dkn, dvs, dbeta, dg = _intra_bwd(qn, kn, vs, beta, g, tm, du, dw, datt, dqd, dkd, dcd)
    dcq, dck, dcv, dba, g_cw, g_sm = _conv_bwd(proj, conv_w, a_log, dt_bias, dqn, dkn, dvs, dbeta, dg)
    pieces = [dpu, dpz, dcq, dck, dcv, ddz, dba]
    g_win = _token_matmul("grad_w_in", n_t, [(p, 0) for p in pieces])
    small = dict(norm_w=jnp.zeros_like(norm_w), pool_w=g_pw, pool_scale=g_ps, conv_w=g_cw[:CONV_K],
                 a_log=g_sm[0:1, 0:N_HEADS], dt_bias=g_sm[0:1, N_HEADS:2 * N_HEADS], dn_norm_w=g_dnw, final_norm_w=g_fnw)
    return loss[0, 0], g_win, g_wout, small, dh, pieces


def _pack_small(t):
    lanes = lambda a: jnp.pad(a.reshape(1, -1), ((0, 0), (0, HEAD - a.size)))
    rows = [t["pool_w"].reshape(-1, HEAD), t["norm_w"].reshape(-1, HEAD), t["final_norm_w"].reshape(-1, HEAD),
            t["pool_scale"].reshape(-1, HEAD), t["conv_w"].reshape(-1, HEAD), t["dn_norm_w"].reshape(1, HEAD),
            lanes(t["a_log"]), lanes(t["dt_bias"]), lanes(t.get("loss", jnp.zeros((1,), F32)))]
    buf = jnp.concatenate(rows, axis=0)
    return jnp.pad(buf, ((0, SMALL_ROWS - buf.shape[0]), (0, 0)))


def _unpack_small(buf, conv_cols):
    out, r = {}, 0
    for name, nrow, shape in (("pool_w", 512, (1, N_HEADS, HEAD, HEAD)), ("norm_w", 8, (1, D_MODEL)),
                              ("final_norm_w", 8, (D_MODEL,)), ("pool_scale", 4, (1, D_HALF)),
                              ("conv_w", CONV_K * conv_cols // HEAD, (1, CONV_K, conv_cols)), ("dn_norm_w", 1, (1, HEAD))):
        out[name] = buf[r:r + nrow].reshape(shape)
        r += nrow
    out["a_log"] = buf[r:r + 1, :N_HEADS]
    out["dt_bias"] = buf[r + 1:r + 2, :N_HEADS]
    out["loss"] = buf[r + 2, 0]
    return out


def kernel(x, norm_w, w_in, pool_w, pool_scale, conv_w, a_log, dt_bias, dn_norm_w, w_out, final_norm_w, loss_target, m_norm_w, m_w_in, m_pool_w, m_pool_scale, m_conv_w, m_a_log, m_dt_bias, m_dn_norm_w, m_w_out, m_final_norm_w, v_norm_w, v_w_in, v_pool_w, v_pool_scale, v_conv_w, v_a_log, v_dt_bias, v_dn_norm_w, v_w_out, v_final_norm_w):
    cidx = lax.axis_index("c").astype(I32).reshape(1)
    jidx = (2 * lax.axis_index("x") + lax.axis_index("y")).astype(I32)

    wb = jnp.pad(w_in[0].astype(BF16), ((0, 0), (0, BLK_IN_PAD - BLK_IN)))
    ob = w_out[0].astype(BF16)
    gw, gc = _gather_weights(wb, conv_w[0])
    mine = lambda j: jidx == j
    w_pad = jnp.concatenate([jnp.where(mine(j), wb[:, :BLK_IN], gw[j, :, :BLK_IN]) for j in range(4)]
                            + [jnp.zeros((D_MODEL, N_IN_PAD - N_IN), BF16)], axis=1)
    cw_full = jnp.concatenate([jnp.where(mine(j), conv_w[0], gc[j]) for j in range(4)], axis=1)

    lands_o, copies_o = _gather_blocks(ob)
    sems_o, ob_thru, zones_o, token_o = _exchange_start("gather_w_out_start", [ob], lands_o, copies_o)

    def w_out_full(after):
        (own,), (got,) = _exchange_wait("gather_w_out_wait", sems_o, ob_thru, zones_o, copies_o, after)
        return jnp.where((jnp.arange(4) == jidx)[:, None, None], own[None], got).reshape(D_MODEL, D_MODEL)

    loss, g_win, g_wout, small, dh, pieces = _local_step(
        x[0], loss_target[0], w_pad, w_out_full, cw_full, norm_w + token_o[0, 0], pool_w[0], pool_scale, a_log, dt_bias,
        dn_norm_w, final_norm_w.reshape(1, D_MODEL))
    small["loss"] = loss

    blocks_in = jnp.stack([jnp.pad(g_win[:, j * BLK_IN:(j + 1) * BLK_IN].astype(BF16), ((0, 0), (0, BLK_IN_PAD - BLK_IN)))
                           for j in range(4)])
    blocks_out = g_wout.astype(BF16).reshape(4, BLK_OUT, D_MODEL)
    full = [blocks_in, blocks_out, _pack_small(small)]
    from_sib = _to_sibling_half("reduce_sibling", full)
    chip_sum = [_add_half("add_sibling_%d" % i, f, p, cidx) for i, (f, p) in enumerate(zip(full, from_sib))]
    blocked = [True, True, False]
    lands, copies = _to_other_chips(chip_sum, blocked)
    sems, chip_sum, zones, token = _exchange_start("reduce_chips_start", chip_sum, lands, copies)
    gx, g_nw = _in_bwd(x[0], dh, norm_w + token[0, 0], w_pad, pieces)
    g_nw = _allreduce_tile("reduce_norm_w", g_nw.reshape(8, HEAD)).reshape(1, D_MODEL)
    chip_sum, from_chips = _exchange_wait("reduce_chips_wait", sems, chip_sum, zones, copies, gx)
    halves = [_add_chips("add_chips_%d" % i, o, g, jidx.reshape(1), b)
              for i, (o, g, b) in enumerate(zip(chip_sum, from_chips, blocked))]
    other_halves = _to_sibling("swap_halves", halves)

    weights = dict(norm_w=norm_w, w_in=w_in, pool_w=pool_w, pool_scale=pool_scale, conv_w=conv_w, a_log=a_log,
                   dt_bias=dt_bias, dn_norm_w=dn_norm_w, w_out=w_out, final_norm_w=final_norm_w)
    ms = dict(norm_w=m_norm_w, w_in=m_w_in, pool_w=m_pool_w, pool_scale=m_pool_scale, conv_w=m_conv_w, a_log=m_a_log,
              dt_bias=m_dt_bias, dn_norm_w=m_dn_norm_w, w_out=m_w_out, final_norm_w=m_final_norm_w)
    vs = dict(norm_w=v_norm_w, w_in=v_w_in, pool_w=v_pool_w, pool_scale=v_pool_scale, conv_w=v_conv_w, a_log=v_a_log,
              dt_bias=v_dt_bias, dn_norm_w=v_dn_norm_w, w_out=v_w_out, final_norm_w=v_final_norm_w)
    names = ["norm_w", "w_in", "pool_w", "pool_scale", "conv_w", "a_log", "dt_bias", "dn_norm_w", "w_out", "final_norm_w"]
    small_names = [n for n in names if n not in ("w_in", "w_out")]

    def pack(t):
        conv = lax.dynamic_update_slice_in_dim(jnp.zeros((CONV_K, 3 * D_HALF), F32), t["conv_w"][0], jidx * BLK_CONV, axis=1)
        return _pack_small({**{n: t[n] for n in small_names if n != "conv_w"}, "conv_w": conv})[None]

    results = [{}, {}, {}, {}]
    to_tiles = lambda a: jnp.transpose(a, (2, 0, 1)).reshape(BLK_IN, 8, HEAD)
    from_tiles = lambda a: jnp.transpose(a, (1, 2, 0)).reshape(1, D_MODEL, BLK_IN)
    lo = jnp.where(cidx[0] == 0, halves[0], other_halves[0])
    hi = jnp.where(cidx[0] == 0, other_halves[0], halves[0])
    g_tiles = jnp.concatenate([lo[:, :BLK_IN].T, hi[:, :BLK_IN].T], axis=1).reshape(BLK_IN, 8, HEAD)
    outs = _adamw_tiles("adamw_w_in", to_tiles(w_in), g_tiles, to_tiles(m_w_in), to_tiles(v_w_in))
    for res, o in zip(results, (g_tiles,) + tuple(outs)):
        res["w_in"] = from_tiles(o)
    outs = _adamw_shard("adamw_w_out", w_out, halves[1], other_halves[1], cidx, m_w_out, v_w_out)
    for res, o in zip(results, outs):
        res["w_out"] = o
    outs = _adamw_shard("adamw_small", pack(weights), halves[2], other_halves[2], cidx, pack(ms), pack(vs))
    for res, o in zip(results, outs):
        got = _unpack_small(o[0], 3 * D_HALF)
        got["conv_w"] = lax.dynamic_slice_in_dim(got["conv_w"], jidx * BLK_CONV, BLK_CONV, axis=2)
        res.update(got)
    one_tile = lambda a: a.reshape(1, 8, HEAD)
    outs = _adamw_tiles("adamw_norm_w", one_tile(norm_w), one_tile(g_nw), one_tile(m_norm_w), one_tile(v_norm_w))
    for res, o in zip(results, (g_nw,) + tuple(outs)):
        res["norm_w"] = o.reshape(1, D_MODEL)
    grads, delta, new_m, new_v = results

    return (grads["loss"], gx[None], *[grads[n] for n in names], *[delta[n] for n in names],
            *[new_m[n] for n in names], *[new_v[n] for n in names])
```

```python
import functools

import jax
import jax.numpy as jnp
import numpy as np
from jax import lax
from jax.experimental import pallas as pl
from jax.experimental.pallas import tpu as pltpu

F32 = jnp.float32
BF16 = jnp.bfloat16
I32 = jnp.int32

D_MODEL = 1024
D_HALF = 512
N_HEADS = 4
HEAD = 128
CHUNK = 64
PAIR = 2 * CHUNK
WINDOWS = (2, 4, 8, 16)
CONV_K = 4
EPS = 1e-6
N_IN = 3080
N_IN_PAD = 3200
BLK_IN = 770
BLK_IN_PAD = 896
BLK_OUT = 256
BLK_CONV = 384
COL_BA = 3072
QK_SCALE = HEAD ** -0.5
SMALL_ROWS = 592
VMEM_LIMIT = 56 * 1024 * 1024

ADAM_LR = 0.001
ADAM_B1 = 0.9
ADAM_B2 = 0.999
ADAM_EPS = 1e-08
ADAM_WD = 0.01
ADAM_STEP = 10

CHIP_MASKS = (2, 1, 3)
HEADS = range(N_HEADS)
HEAD_COLS = [slice(h * HEAD, (h + 1) * HEAD) for h in HEADS]


def _call(body, **kw):
    return pl.pallas_call(body, **kw)


def _params(*sem):
    return pltpu.CompilerParams(dimension_semantics=sem, vmem_limit_bytes=VMEM_LIMIT)


def _sds(shape, dtype=F32):
    return jax.ShapeDtypeStruct(shape, dtype)


def _bdot(a, b):
    return jnp.dot(a.astype(BF16), b.astype(BF16), preferred_element_type=F32)


def _bdot_nt(a, b):
    return lax.dot_general(a.astype(BF16), b.astype(BF16), (((1,), (1,)), ((), ())), preferred_element_type=F32)


def _bdot_tn(a, b):
    return lax.dot_general(a.astype(BF16), b.astype(BF16), (((0,), (0,)), ((), ())), preferred_element_type=F32)


def _split(a):
    hi = a.astype(BF16)
    lo = (a - hi.astype(F32)).astype(BF16)
    return hi, lo


def _mask_dot(m, b, dims=(((1,), (0,)), ((), ()))):
    bh, bl = _split(b)
    dg = functools.partial(lax.dot_general, dimension_numbers=dims, preferred_element_type=F32)
    return dg(m, bh) + dg(m, bl)


def _sigmoid(x):
    return 0.5 * jnp.tanh(0.5 * x) + 0.5


def _softplus(x):
    return jnp.maximum(x, 0.0) + jnp.log(1.0 + jnp.exp(-jnp.abs(x)))


def _rowsum(x):
    return jnp.sum(x, axis=-1, keepdims=True)


def _colsum(x):
    return jnp.sum(x, axis=0, keepdims=True)


def _shift_down(xv, prev8, k):
    r = pltpu.roll(xv, k, 0)
    q = pltpu.roll(prev8, k, 0)
    row = lax.broadcasted_iota(I32, prev8.shape, 0)
    top = jnp.where(row < k, q, r[0:8])
    return jnp.concatenate([top, r[8:]], axis=0)


def _shift_up(xv, next8, k):
    t = xv.shape[0]
    r = pltpu.roll(xv, t - k, 0)
    q = pltpu.roll(next8, 8 - k, 0)
    row = lax.broadcasted_iota(I32, next8.shape, 0)
    bot = jnp.where(row >= 8 - k, q, r[t - 8:])
    return jnp.concatenate([r[:t - 8], bot], axis=0)


def _band(rows, cols, off, w, anti=False):
    r = lax.broadcasted_iota(I32, (rows, cols), 0)
    c = lax.broadcasted_iota(I32, (rows, cols), 1)
    d = (c - r + off) if anti else (r - c + off)
    return ((d >= 0) & (d < w)).astype(BF16)


def _head(ref_or_val, h):
    return ref_or_val[:, h * HEAD:(h + 1) * HEAD]


INTRA_PAIRS = 2
UNITS = [(pp, h) for pp in range(INTRA_PAIRS) for h in HEADS]


def _heads(ref, rows=PAIR):
    return [ref[pp * rows:(pp + 1) * rows, HEAD_COLS[h]] for pp, h in UNITS]


def _put_heads(ref, vals, rows=PAIR):
    for (pp, h), v in zip(UNITS, vals):
        ref[pp * rows:(pp + 1) * rows, HEAD_COLS[h]] = v.astype(ref.dtype)


def _each(fn, *lists):
    return [fn(*args) for args in zip(*lists)]


def _proj_fwd(x, norm_w, w_pad):
    s = x.shape[0]
    tm = 512

    def body(x_ref, nw_ref, w_ref, proj_ref, nt_ref):
        xv = x_ref[...]
        r = lax.rsqrt(jnp.mean(xv * xv, axis=-1, keepdims=True) + EPS)
        nv = xv * r * nw_ref[...]
        nt_ref[...] = nv.T.astype(BF16)
        proj_ref[...] = jnp.dot(nv.astype(BF16), w_ref[...], preferred_element_type=F32)

    return _call(
        body, name="proj_fwd", grid=(s // tm,),
        in_specs=[pl.BlockSpec((tm, D_MODEL), lambda i: (i, 0)),
                  pl.BlockSpec((1, D_MODEL), lambda i: (0, 0)),
                  pl.BlockSpec((D_MODEL, N_IN_PAD), lambda i: (0, 0))],
        out_specs=[pl.BlockSpec((tm, N_IN_PAD), lambda i: (i, 0)),
                   pl.BlockSpec((D_MODEL, tm), lambda i: (0, i))],
        out_shape=[_sds((s, N_IN_PAD)), _sds((D_MODEL, s), BF16)],
        compiler_params=_params("arbitrary"),
    )(x, norm_w, w_pad)


def _pool_bands(t, anti=False):
    r = np.arange(t)[:, None]
    c = np.arange(t + HEAD)[None, :]
    d = (c - r) if anti else (r - c + HEAD)
    return jnp.asarray(np.stack([(d >= 0) & (d < w) for w in WINDOWS]), BF16)


def _pool_mix(u, halo, z, pw, bands, row0):
    t = u[0].shape[0]
    rows = row0 + lax.broadcasted_iota(I32, (t, 1), 0) + 1
    cnt = [jnp.minimum(rows, w).astype(F32) for w in WINDOWS]
    win = _each(lambda b, h, v: _mask_dot(b, jnp.concatenate([h, v], axis=0)), bands, halo, u)
    mix = _each(lambda a, c, v: a / c - v, win, cnt, u)
    mixed = _each(_bdot, mix, pw)
    return mix, mixed, _each(_sigmoid, z), cnt


POOL_T = 256


def _pool_fwd(proj, pool_w, pool_scale):
    s = proj.shape[0]
    t = POOL_T
    hb = t // HEAD

    def body(u_ref, z_ref, halo_ref, pw_ref, ps_ref, band_ref, y_ref):
        i = pl.program_id(0)
        live = (i > 0).astype(F32)
        groups = lambda ref: [ref[:, sl] for sl in HEAD_COLS]
        z = groups(z_ref)
        _, mixed, sg, _ = _pool_mix(groups(u_ref), [h * live for h in groups(halo_ref)], z,
                                    [pw_ref[g] for g in HEADS], [band_ref[g] for g in HEADS], i * t)
        for sl, m, zg, s_ in zip(HEAD_COLS, mixed, z, sg):
            y_ref[:, sl] = m * ps_ref[:, sl] * (zg * s_)

    return _call(
        body, name="pool_fwd", grid=(s // t,),
        in_specs=[pl.BlockSpec((t, D_HALF), lambda i: (i, 0)),
                  pl.BlockSpec((t, D_HALF), lambda i: (i, 1)),
                  pl.BlockSpec((HEAD, D_HALF), lambda i: (jnp.maximum(i * hb - 1, 0), 0)),
                  pl.BlockSpec((N_HEADS, HEAD, HEAD), lambda i: (0, 0, 0)),
                  pl.BlockSpec((1, D_HALF), lambda i: (0, 0)),
                  pl.BlockSpec((N_HEADS, t, HEAD + t), lambda i: (0, 0, 0))],
        out_specs=pl.BlockSpec((t, D_HALF), lambda i: (i, 0)),
        out_shape=_sds((s, D_HALF)),
        compiler_params=_params("arbitrary"),
    )(proj, proj, proj, pool_w, pool_scale, _pool_bands(t))


def _conv_taps(xv, prev8):
    return [_shift_down(xv, prev8, CONV_K - 1 - j) for j in range(CONV_K - 1)] + [xv]


def _conv_pre(taps, cw):
    y = taps[CONV_K - 1] * cw[CONV_K - 1:CONV_K]
    for j in range(CONV_K - 2, -1, -1):
        y = y + taps[j] * cw[j:j + 1]
    return y


CONV_T = 256
CONV_SUB = 256


def _conv_specs(t, tile_of=lambda i: i):
    tiles = [pl.BlockSpec((t, D_HALF), functools.partial(lambda i, p: (tile_of(i), 2 + p), p=p)) for p in range(3)]
    halos = [pl.BlockSpec((8, D_HALF),
                          functools.partial(lambda i, p: (jnp.maximum(tile_of(i) * (t // 8) - 1, 0), 2 + p), p=p))
             for p in range(3)]
    return tiles + halos


def _conv_fwd(proj, conv_w, a_log, dt_bias):
    s = proj.shape[0]
    t = CONV_T

    def body(q_ref, k_ref, v_ref, hq_ref, hk_ref, hv_ref, ba_ref, cw_ref, al_ref, dtb_ref,
             qn_ref, kn_ref, vs_ref, beta_ref, g_ref):
        live = (pl.program_id(0) > 0).astype(F32)
        parts = ((q_ref, hq_ref, qn_ref), (k_ref, hk_ref, kn_ref), (v_ref, hv_ref, vs_ref))

        def sub_tile(r0, first):
            rows = pl.ds(r0, CONV_SUB)
            for p, (x_ref, h_ref, o_ref) in enumerate(parts):
                for h in HEADS:
                    cs = HEAD_COLS[h]
                    prev8 = h_ref[:, cs] * live if first else x_ref[pl.ds(r0 - 8, 8), cs]
                    y = _conv_pre(_conv_taps(x_ref[rows, cs], prev8), cw_ref[:, p * D_HALF + h * HEAD:p * D_HALF + (h + 1) * HEAD])
                    sv = y * _sigmoid(y)
                    o_ref[rows, cs] = sv if p == 2 else sv * lax.rsqrt(_rowsum(sv * sv) + EPS)
            ba = ba_ref[rows, :]
            for h in HEADS:
                beta = _sigmoid(ba[:, h:h + 1])
                gl = -jnp.exp(al_ref[0:1, h:h + 1]) * _softplus(ba[:, N_HEADS + h:N_HEADS + h + 1] + dtb_ref[0:1, h:h + 1])
                beta_ref[rows, HEAD_COLS[h]] = jnp.broadcast_to(beta, (CONV_SUB, HEAD))
                g_ref[rows, HEAD_COLS[h]] = jnp.broadcast_to(gl, (CONV_SUB, HEAD))

        sub_tile(0, True)

        def step(k, carry):
            sub_tile(pl.multiple_of(k * CONV_SUB, CONV_SUB), False)
            return carry

        lax.fori_loop(1, t // CONV_SUB, step, 0)

    row = pl.BlockSpec((t, D_HALF), lambda i: (i, 0))
    return _call(
        body, name="conv_fwd", grid=(s // t,),
        in_specs=_conv_specs(t) + [pl.BlockSpec((t, HEAD), lambda i: (i, COL_BA // HEAD)),
                                   pl.BlockSpec((CONV_K, 3 * D_HALF), lambda i: (0, 0)),
                                   pl.BlockSpec((1, N_HEADS), lambda i: (0, 0)),
                                   pl.BlockSpec((1, N_HEADS), lambda i: (0, 0))],
        out_specs=[row] * 5,
        out_shape=[_sds((s, D_HALF))] * 5,
        compiler_params=_params("arbitrary"),
    )(proj, proj, proj, proj, proj, proj, proj, conv_w, a_log, dt_bias)


def _pair_masks():
    r = lax.broadcasted_iota(I32, (PAIR, PAIR), 0)
    c = lax.broadcasted_iota(I32, (PAIR, PAIR), 1)
    same = jnp.right_shift(r, 6) == jnp.right_shift(c, 6)
    return same, same & (r >= c), same & (r > c), r == c


def _pair_common(qn, kn, vs, beta, g):
    same, incl, strict, eye = _pair_masks()
    incl_b = incl.astype(BF16)
    first = lax.broadcasted_iota(I32, (PAIR, HEAD), 0) < CHUNK
    gc = _each(lambda gv: _mask_dot(incl_b, gv), g)
    gc_row = _each(lambda v: _colsum(jnp.where(eye, v, 0.0)), gc)
    decay = _each(lambda v, r: jnp.where(incl, jnp.exp(jnp.where(incl, v - r, 0.0)), 0.0), gc, gc_row)
    gl = _each(lambda v: jnp.where(first, v[CHUNK - 1:CHUNK], v[PAIR - 1:PAIR]), gc)
    egc = _each(jnp.exp, gc)
    q = _each(lambda v: v * QK_SCALE, qn)
    kb = _each(lambda k, b: k * b, kn, beta)
    return dict(same=same, incl=incl, strict=strict, eye=eye, gc=gc, decay=decay, gl=gl, egc=egc,
                ekd=_each(lambda a, b: jnp.exp(a - b), gl, gc), cd=_each(jnp.exp, gl), q=q, kb=kb,
                vb=_each(lambda v, b: v * b, vs, beta), kbg=_each(lambda k, e: k * e, kb, egc),
                kk=_each(_bdot_nt, kb, kn), qk=_each(_bdot_nt, q, kn))


def _tri_inv(a, eye_f):
    p = _each(lambda v: eye_f - v, a)
    x = _each(_bdot, a, a)
    for it in range(5):
        p = _each(lambda pv, xv: pv + _bdot(pv, xv), p, x)
        if it < 4:
            x = _each(_bdot, x, x)
    return p


def _pair_spec():
    return pl.BlockSpec((INTRA_PAIRS * PAIR, D_HALF), lambda i: (i, 0))


def _chunk_scalar_spec(pairs=1, index=lambda i: (i, 0)):
    return pl.BlockSpec((16 * pairs, D_HALF), index)


SCAN_PAIRS = 2
SCAN_ROWS = SCAN_PAIRS * PAIR


def _intra_fwd(qn, kn, vs, beta, g):
    s = qn.shape[0]

    def body(qn_ref, kn_ref, vs_ref, beta_ref, g_ref, u_ref, w_ref, att_ref, qd_ref, kd_ref, t_ref, cd_ref):
        kn = _heads(kn_ref)
        cm = _pair_common(_heads(qn_ref), kn, _heads(vs_ref), _heads(beta_ref), _heads(g_ref))
        a = _each(lambda kk, d: jnp.where(cm["strict"], kk * d, 0.0), cm["kk"], cm["decay"])
        tm = _tri_inv(a, cm["eye"].astype(F32))
        _put_heads(t_ref, tm)
        _put_heads(u_ref, _each(_bdot, tm, cm["vb"]))
        _put_heads(w_ref, _each(_bdot, tm, cm["kbg"]))
        _put_heads(att_ref, _each(lambda a, b: a * b, cm["qk"], cm["decay"]))
        _put_heads(qd_ref, _each(lambda a, b: a * b, cm["q"], cm["egc"]))
        _put_heads(kd_ref, _each(lambda a, b: a * b, kn, cm["ekd"]))
        for ci in range(2):
            for (pp, h), v in zip(UNITS, cm["cd"]):
                cd_ref[pp * 16 + ci * 8:pp * 16 + (ci + 1) * 8, HEAD_COLS[h]] = v[ci * CHUNK:ci * CHUNK + 8]

    return _call(
        body, name="intra_fwd", grid=(s // (INTRA_PAIRS * PAIR),),
        in_specs=[_pair_spec()] * 5, out_specs=[_pair_spec()] * 6 + [_chunk_scalar_spec(INTRA_PAIRS)],
        out_shape=[_sds((s, D_HALF))] + [_sds((s, D_HALF), BF16)] * 5 + [_sds((s // 8, D_HALF))],
        compiler_params=_params("arbitrary"),
    )(qn, kn, vs, beta, g)


def _scan_fwd(u, w, att, qd, kd, cd):
    s = u.shape[0]
    n_chunks = s // CHUNK

    def body(u_ref, w_ref, att_ref, qd_ref, kd_ref, cd_ref, o_ref, vn_ref, st_ref, state):
        @pl.when(pl.program_id(0) == 0)
        def _():
            state[...] = jnp.zeros_like(state)
        cols = list(enumerate(HEAD_COLS))
        chunk_rows = [slice(ci * CHUNK, (ci + 1) * CHUNK) for ci in range(2 * SCAN_PAIRS)]
        kw = [[_bdot_tn(kd_ref[rs, sl], w_ref[rs, sl]) for _, sl in cols] for rs in chunk_rows]
        ku = [[_bdot_tn(kd_ref[rs, sl], u_ref[rs, sl]) for _, sl in cols] for rs in chunk_rows]
        sm = [state[h] for h in HEADS]
        for ci in range(2 * SCAN_PAIRS):
            for h in HEADS:
                st_ref[ci, h] = sm[h]
            sm = [sm[h] * cd_ref[ci * 8:ci * 8 + 1, sl] - _bdot(kw[ci][h], sm[h]) + ku[ci][h] for h, sl in cols]
        for h in HEADS:
            state[h] = sm[h]
        for ci, rs in enumerate(chunk_rows):
            both = [_bdot(jnp.concatenate([w_ref[rs, sl], qd_ref[rs, sl]], axis=0), st_ref[ci, h]) for h, sl in cols]
            for h, sl in cols:
                vn_ref[rs, sl] = (u_ref[rs, sl] - both[h][:CHUNK]).astype(BF16)
                o_ref[rs, sl] = both[h][CHUNK:]
        for pp in range(SCAN_PAIRS):
            rp = slice(pp * PAIR, (pp + 1) * PAIR)
            intra = [_bdot(att_ref[rp, sl], vn_ref[rp, sl]) for sl in HEAD_COLS]
            for h, sl in cols:
                o_ref[rp, sl] += intra[h]

    rows = pl.BlockSpec((SCAN_ROWS, D_HALF), lambda i: (i, 0))
    return _call(
        body, name="scan_fwd", grid=(s // SCAN_ROWS,),
        in_specs=[rows] * 5 + [_chunk_scalar_spec(SCAN_PAIRS)],
        out_specs=[rows, rows, pl.BlockSpec((2 * SCAN_PAIRS, N_HEADS, HEAD, HEAD), lambda i: (i, 0, 0, 0))],
        out_shape=[_sds((s, D_HALF)), _sds((s, D_HALF), BF16), _sds((n_chunks, N_HEADS, HEAD, HEAD))],
        scratch_shapes=[pltpu.VMEM((N_HEADS, HEAD, HEAD), F32)],
        compiler_params=_params("arbitrary"),
    )(u, w, att, qd, kd, cd)


OUT_T = 512


def _out_fwd_bwd(x, y_pool, o, proj, target, w_out, dn_norm_w, final_norm_w):
    s = x.shape[0]
    t = OUT_T

    def body(x_ref, yp_ref, o_ref, z_ref, tg_ref, wo_ref, dnw_ref, fnw_ref,
             yt_ref, dh_ref, dyp_ref, do_ref, dz_ref, loss_ref, gfn_ref, gdn_ref, y_ref):
        @pl.when(pl.program_id(0) == 0)
        def _():
            loss_ref[...] = jnp.zeros_like(loss_ref)
            gfn_ref[...] = jnp.zeros_like(gfn_ref)
            gdn_ref[...] = jnp.zeros_like(gdn_ref)

        ypv = yp_ref[...]
        y_ref[:, :D_HALF] = ypv.astype(BF16)
        yt_ref[:D_HALF, :] = ypv.T.astype(BF16)
        dnw = dnw_ref[...]
        keep = []
        for h in HEADS:
            ov = o_ref[:, HEAD_COLS[h]]
            zv = z_ref[:, HEAD_COLS[h]]
            ro = lax.rsqrt(jnp.mean(ov * ov, axis=-1, keepdims=True) + EPS)
            ohat = ov * ro
            sg = _sigmoid(zv)
            keep.append((ro, ohat, zv, sg))
            ydn = ohat * dnw * (zv * sg)
            y_ref[:, D_HALF + h * HEAD:D_HALF + (h + 1) * HEAD] = ydn.astype(BF16)
            yt_ref[D_HALF + h * HEAD:D_HALF + (h + 1) * HEAD, :] = ydn.T.astype(BF16)

        hv = x_ref[...] + jnp.dot(y_ref[...], wo_ref[...], preferred_element_type=F32)
        r2 = lax.rsqrt(jnp.mean(hv * hv, axis=-1, keepdims=True) + EPS)
        hhat = hv * r2
        fnw = fnw_ref[...]
        err = hhat * fnw - tg_ref[...]
        loss_ref[...] += 0.5 * jnp.sum(_rowsum(err * err) * (1.0 / D_MODEL), axis=0, keepdims=True)
        dout = err * (1.0 / D_MODEL)
        gfn_ref[...] += _colsum(dout * hhat)
        dhh = dout * fnw
        dh = r2 * (dhh - hhat * jnp.mean(dhh * hhat, axis=-1, keepdims=True))
        dh_ref[...] = dh
        dy = _bdot_nt(dh, wo_ref[...])
        dyp_ref[...] = dy[:, :D_HALF]
        gdn = jnp.zeros((1, HEAD), F32)
        for h in HEADS:
            ro, ohat, zv, sg = keep[h]
            dyd = dy[:, D_HALF + h * HEAD:D_HALF + (h + 1) * HEAD]
            sz = zv * sg
            dz_ref[:, HEAD_COLS[h]] = (dyd * ohat * dnw * (sg * (1.0 + zv * (1.0 - sg)))).astype(BF16)
            gdn = gdn + _colsum(dyd * ohat * sz)
            doh = dyd * dnw * sz
            do_ref[:, HEAD_COLS[h]] = ro * (doh - ohat * jnp.mean(doh * ohat, axis=-1, keepdims=True))
        gdn_ref[...] += gdn

    wide = pl.BlockSpec((t, D_MODEL), lambda i: (i, 0))
    half = pl.BlockSpec((t, D_HALF), lambda i: (i, 0))
    const = lambda shape: pl.BlockSpec(shape, lambda i: (0,) * len(shape))
    return _call(
        body, name="out_fwd_bwd", grid=(s // t,),
        in_specs=[wide, half, half, pl.BlockSpec((t, D_HALF), lambda i: (i, 5)), wide,
                  const((D_MODEL, D_MODEL)), const((1, HEAD)), const((1, D_MODEL))],
        out_specs=[pl.BlockSpec((D_MODEL, t), lambda i: (0, i)), wide, half, half, half,
                   const((1, HEAD)), const((1, D_MODEL)), const((1, HEAD))],
        out_shape=[_sds((D_MODEL, s), BF16), _sds((s, D_MODEL)), _sds((s, D_HALF)), _sds((s, D_HALF)), _sds((s, D_HALF), BF16),
                   _sds((1, HEAD)), _sds((1, D_MODEL)), _sds((1, HEAD))],
        scratch_shapes=[pltpu.VMEM((t, D_MODEL), BF16)],
        compiler_params=_params("arbitrary"),
    )(x, y_pool, o, proj, target, w_out, dn_norm_w, final_norm_w)


def _token_matmul(name, at, pieces):
    m, s = at.shape
    n = len(pieces)
    tn, tk = D_HALF, 512

    def body(a_ref, *refs):
        p_refs, o_ref = refs[:n], refs[n]

        @pl.when(pl.program_id(0) == 0)
        def _():
            o_ref[...] = jnp.zeros_like(o_ref)

        av = a_ref[...]
        for p in range(n):
            o_ref[:, p * tn:(p + 1) * tn] += _bdot(av, p_refs[p][...])

    return _call(
        body, name=name, grid=(s // tk,),
        in_specs=[pl.BlockSpec((m, tk), lambda k: (0, k))]
                 + [pl.BlockSpec((tk, tn), functools.partial(lambda k, cb: (k, cb), cb=cb)) for _, cb in pieces],
        out_specs=pl.BlockSpec((m, n * tn), lambda k: (0, 0)),
        out_shape=_sds((m, n * tn)),
        compiler_params=_params("arbitrary"),
    )(at, *[p[0] for p in pieces])


def _pool_bwd(proj, dyp, pool_w, pool_scale):
    s = proj.shape[0]
    t = POOL_T
    hb = t // HEAD
    last = s // HEAD - 1

    def body(u_ref, z_ref, halo_ref, dy_ref, zn_ref, dyn_ref, pw_ref, ps_ref, band_ref, aband_ref,
             du_ref, dz_ref, gpw_ref, gps_ref):
        i = pl.program_id(0)

        @pl.when(i == 0)
        def _():
            gpw_ref[...] = jnp.zeros_like(gpw_ref)
            gps_ref[...] = jnp.zeros_like(gps_ref)

        live = (i > 0).astype(F32)
        more = (i < pl.num_programs(0) - 1).astype(F32)
        groups = lambda ref: [ref[:, sl] for sl in HEAD_COLS]
        z, ps, dy = groups(z_ref), groups(ps_ref), groups(dy_ref)
        pw = [pw_ref[g] for g in HEADS]
        mix, mixed, sg, cnt = _pool_mix(groups(u_ref), [h * live for h in groups(halo_ref)], z, pw,
                                        [band_ref[g] for g in HEADS], i * t)
        sz = _each(lambda a, b: a * b, z, sg)
        for sl, d, m, p, s_, zg in zip(HEAD_COLS, dy, mixed, ps, sg, z):
            dz_ref[:, sl] = (d * m * p * (s_ * (1.0 + zg * (1.0 - s_)))).astype(BF16)
        for sl, d, m, a in zip(HEAD_COLS, dy, mixed, sz):
            gps_ref[:, sl] += _colsum(d * m * a)
        dmixed = _each(lambda d, p, a: d * p * a, dy, ps, sz)
        for g, gp in enumerate(_each(_bdot_tn, mix, dmixed)):
            gpw_ref[g] += gp
        dmix = _each(_bdot_nt, dmixed, pw)
        dmix_n = _each(lambda d, p, zn, w_: _bdot_nt(d * more * p * (zn * _sigmoid(zn)), w_),
                       groups(dyn_ref), ps, groups(zn_ref), pw)
        scaled = [jnp.concatenate([a / c, b * (1.0 / w)], axis=0) for a, c, b, w in zip(dmix, cnt, dmix_n, WINDOWS)]
        du = _each(lambda b, s_, d: _mask_dot(b, s_) - d, [aband_ref[g] for g in HEADS], scaled, dmix)
        for sl, v in zip(HEAD_COLS, du):
            du_ref[:, sl] = v.astype(BF16)

    tile = lambda col: pl.BlockSpec((t, D_HALF), lambda i: (i, col))
    below = lambda col: pl.BlockSpec((HEAD, D_HALF), lambda i: (jnp.minimum((i + 1) * hb, last), col))
    return _call(
        body, name="pool_bwd", grid=(s // t,),
        in_specs=[tile(0), tile(1), pl.BlockSpec((HEAD, D_HALF), lambda i: (jnp.maximum(i * hb - 1, 0), 0)),
                  tile(0), below(1), below(0),
                  pl.BlockSpec((N_HEADS, HEAD, HEAD), lambda i: (0, 0, 0)), pl.BlockSpec((1, D_HALF), lambda i: (0, 0)),
                  pl.BlockSpec((N_HEADS, t, HEAD + t), lambda i: (0, 0, 0)),
                  pl.BlockSpec((N_HEADS, t, HEAD + t), lambda i: (0, 0, 0))],
        out_specs=[tile(0), tile(0), pl.BlockSpec((N_HEADS, HEAD, HEAD), lambda i: (0, 0, 0)),
                   pl.BlockSpec((1, D_HALF), lambda i: (0, 0))],
        out_shape=[_sds((s, D_HALF), BF16), _sds((s, D_HALF), BF16), _sds((N_HEADS, HEAD, HEAD)), _sds((1, D_HALF))],
        compiler_params=_params("arbitrary"),
    )(proj, proj, proj, dyp, proj, dyp, pool_w, pool_scale, _pool_bands(t), _pool_bands(t, anti=True))


def _scan_bwd(do, vn, qd, kd, w, att, cd, st):
    s = do.shape[0]
    n_steps = s // SCAN_ROWS

    def body(do_ref, vn_ref, qd_ref, kd_ref, w_ref, att_ref, cd_ref, st_ref,
             du_ref, dw_ref, datt_ref, dqd_ref, dkd_ref, dcd_ref, dstate):
        @pl.when(pl.program_id(0) == 0)
        def _():
            dstate[...] = jnp.zeros_like(dstate)
        _, incl, _, _ = _pair_masks()
        cols = list(enumerate(HEAD_COLS))
        dv_intra = []
        for pp in range(SCAN_PAIRS):
            rp = slice(pp * PAIR, (pp + 1) * PAIR)
            dv_intra.append([_bdot_tn(att_ref[rp, sl], do_ref[rp, sl]) for _, sl in cols])
            for _, sl in cols:
                datt_ref[rp, sl] = jnp.where(incl, _bdot_nt(do_ref[rp, sl], vn_ref[rp, sl]), 0.0)
        n_ch = 2 * SCAN_PAIRS
        chunk_rows = [slice(ci * CHUNK, (ci + 1) * CHUNK) for ci in range(n_ch)]
        in_pair = [slice((ci % 2) * CHUNK, (ci % 2 + 1) * CHUNK) for ci in range(n_ch)]
        ahead = [[_bdot_tn(qd_ref[rs, sl], do_ref[rs, sl]) - _bdot_tn(w_ref[rs, sl], dv_intra[ci // 2][h][in_pair[ci]])
                  for h, sl in cols] for ci, rs in enumerate(chunk_rows)]
        wk = [[_bdot_tn(w_ref[rs, sl], kd_ref[rs, sl]) for _, sl in cols] for rs in chunk_rows]
        ds = [dstate[h] for h in HEADS]
        ds_at = [None] * n_ch
        for ci in range(n_ch - 1, -1, -1):
            ds_at[ci] = ds
            ds = [ds[h] * cd_ref[ci * 8:ci * 8 + 1, sl] + ahead[ci][h] - _bdot(wk[ci][h], ds[h]) for h, sl in cols]
        for h in HEADS:
            dstate[h] = ds[h]
        for ci, rs in enumerate(chunk_rows):
            dsc = ds_at[ci]
            sm = [st_ref[ci, h] for h in HEADS]
            dvn = [dv_intra[ci // 2][h][in_pair[ci]] + _bdot(kd_ref[rs, sl], dsc[h]) for h, sl in cols]
            dqd = [_bdot_nt(do_ref[rs, sl], sm[h]) for h, sl in cols]
            dw = [-_bdot_nt(dvn[h], sm[h]) for h, _ in cols]
            dkd = [_bdot_nt(vn_ref[rs, sl], dsc[h]) for h, sl in cols]
            dcd = [jnp.broadcast_to(_rowsum(_colsum(dsc[h] * sm[h])), (8, HEAD)) for h in HEADS]
            for h, sl in cols:
                du_ref[rs, sl] = dvn[h].astype(BF16)
                dqd_ref[rs, sl] = dqd[h]
                dw_ref[rs, sl] = dw[h].astype(BF16)
                dkd_ref[rs, sl] = dkd[h]
                dcd_ref[ci * 8:(ci + 1) * 8, sl] = dcd[h]

    rev = pl.BlockSpec((SCAN_ROWS, D_HALF), lambda i: (n_steps - 1 - i, 0))
    rev_scalar = _chunk_scalar_spec(SCAN_PAIRS, lambda i: (n_steps - 1 - i, 0))
    return _call(
        body, name="scan_bwd", grid=(n_steps,),
        in_specs=[rev] * 6 + [rev_scalar,
                              pl.BlockSpec((2 * SCAN_PAIRS, N_HEADS, HEAD, HEAD), lambda i: (n_steps - 1 - i, 0, 0, 0))],
        out_specs=[rev] * 5 + [rev_scalar],
        out_shape=[_sds((s, D_HALF), BF16)] * 2 + [_sds((s, D_HALF))] * 3 + [_sds((s // 8, D_HALF))],
        scratch_shapes=[pltpu.VMEM((N_HEADS, HEAD, HEAD), F32)],
        compiler_params=_params("arbitrary"),
    )(do, vn, qd, kd, w, att, cd, st)


def _intra_bwd(qn, kn, vs, beta, g, tm, du, dw, datt, dqd, dkd, dcd):
    s = qn.shape[0]

    def body(qn_ref, kn_ref, vs_ref, beta_ref, g_ref, t_ref, du_ref, dw_ref, datt_ref, dqd_ref, dkd_ref, dcd_ref,
             dqn_ref, dkn_ref, dvs_ref, dbeta_ref, dg_ref):
        ones = jnp.ones((PAIR, HEAD), BF16)
        tn = (((0,), (0,)), ((), ()))
        kn, vs, beta = _heads(kn_ref), _heads(vs_ref), _heads(beta_ref)
        cm = _pair_common(_heads(qn_ref), kn, vs, beta, _heads(g_ref))
        tmv, duv, dwv, dattv, dqdv, dkdv = (_heads(r) for r in (t_ref, du_ref, dw_ref, datt_ref, dqd_ref, dkd_ref))
        dvb = _each(_bdot_tn, tmv, duv)
        dt = _each(lambda a, b, c, d: _bdot_nt(a, b) + _bdot_nt(c, d), duv, cm["vb"], dwv, cm["kbg"])
        dkbg = _each(_bdot_tn, tmv, dwv)
        m1 = _each(_bdot_tn, tmv, dt)
        da = _each(lambda a, b: -jnp.where(cm["strict"], _bdot_nt(a, b), 0.0), m1, tmv)
        dkk = _each(lambda a, b: a * b, da, cm["decay"])
        dqk = _each(lambda a, b: a * b, dattv, cm["decay"])
        dd = _each(lambda a, b, c, d: a * b + c * d, dkk, cm["kk"], dqk, cm["qk"])
        dkb = _each(lambda a, b, c, d: _bdot(a, b) + c * d, dkk, kn, dkbg, cm["egc"])
        dq = _each(lambda a, b, c, d: _bdot(a, b) + c * d, dqk, kn, dqdv, cm["egc"])
        dkn = _each(lambda a, b, c, d: _bdot_tn(a, b) + _bdot_tn(c, d), dkk, cm["kb"], dqk, cm["q"])
        dkn = _each(lambda a, b, c, d, e: a + b * c + d * e, dkn, dkdv, cm["ekd"], dkb, beta)
        t_kd = _each(lambda a, b, c: _rowsum(a * b * c), dkdv, kn, cm["ekd"])
        split = _each(_split, dd)
        rows_dd = [jnp.dot(hi, ones, preferred_element_type=F32) + jnp.dot(lo, ones, preferred_element_type=F32)
                   for hi, lo in split]
        cols_dd = [lax.dot_general(hi, ones, tn, preferred_element_type=F32)
                   + lax.dot_general(lo, ones, tn, preferred_element_type=F32) for hi, lo in split]
        dgc = _each(lambda r, c, a, b, e, f, k, t: r - c + _rowsum(a * b * e) + _rowsum(f * k) - t,
                    rows_dd, cols_dd, dqdv, cm["q"], cm["egc"], dkbg, cm["kbg"], t_kd)
        same_b = cm["same"].astype(BF16)
        rowi = lax.broadcasted_iota(I32, (PAIR, HEAD), 0)
        dcd = _each(lambda d: jnp.where(rowi < CHUNK, d[0:1], d[8:9]), _heads(dcd_ref, rows=16))
        dgl = _each(lambda t, d, c: _mask_dot(same_b, jnp.broadcast_to(t, (PAIR, HEAD))) + d * c, t_kd, dcd, cm["cd"])
        is_last = jnp.bitwise_and(rowi, CHUNK - 1) == CHUNK - 1
        dgc = _each(lambda a, b: a + jnp.where(is_last, b, 0.0), dgc, dgl)
        r = lax.broadcasted_iota(I32, (PAIR, PAIR), 0)
        c = lax.broadcasted_iota(I32, (PAIR, PAIR), 1)
        upper_b = (cm["same"] & (r <= c)).astype(BF16)
        _put_heads(dg_ref, _each(lambda v: _mask_dot(upper_b, v), dgc))
        _put_heads(dbeta_ref, _each(lambda a, b, c, d: jnp.broadcast_to(_rowsum(a * b) + _rowsum(c * d), (PAIR, HEAD)),
                                    dkb, kn, dvb, vs))
        _put_heads(dqn_ref, _each(lambda v: v * QK_SCALE, dq))
        _put_heads(dkn_ref, dkn)
        _put_heads(dvs_ref, _each(lambda a, b: a * b, dvb, beta))

    return _call(
        body, name="intra_bwd", grid=(s // (INTRA_PAIRS * PAIR),),
        in_specs=[_pair_spec()] * 11 + [_chunk_scalar_spec(INTRA_PAIRS)], out_specs=[_pair_spec()] * 5,
        out_shape=[_sds((s, D_HALF))] * 5,
        compiler_params=_params("arbitrary"),
    )(qn, kn, vs, beta, g, tm, du, dw, datt, dqd, dkd, dcd)


def _rows8(x):
    acc = x[0:8]
    for r in range(8, x.shape[0], 8):
        acc = acc + x[r:r + 8]
    return acc


def _conv_bwd(proj, conv_w, a_log, dt_bias, dqn, dkn, dvs, dbeta, dg):
    s = proj.shape[0]
    t = CONV_T
    n_tiles = s // t
    n_sub = t // CONV_SUB
    tile_of = lambda i: n_tiles - 1 - i

    def body(q_ref, k_ref, v_ref, hq_ref, hk_ref, hv_ref, ba_ref, cw_ref, al_ref, dtb_ref,
             dqn_ref, dkn_ref, dvs_ref, dbeta_ref, dg_ref, oq_ref, ok_ref, ov_ref, dba_ref, gcw_out, gsm_out,
             below, gcw_ref, gsm_ref):
        @pl.when(pl.program_id(0) == 0)
        def _():
            gcw_ref[...] = jnp.zeros_like(gcw_ref)
            gsm_ref[...] = jnp.zeros_like(gsm_ref)
            below[...] = jnp.zeros_like(below)

        live = (pl.program_id(0) < n_tiles - 1).astype(F32)
        parts = ((q_ref, hq_ref, dqn_ref, oq_ref), (k_ref, hk_ref, dkn_ref, ok_ref), (v_ref, hv_ref, dvs_ref, ov_ref))
        lane = lax.broadcasted_iota(I32, (CONV_SUB, HEAD), 1)
        lane8 = lax.broadcasted_iota(I32, (8, HEAD), 1)

        def sub_tile(r0, first):
            rows = pl.ds(r0, CONV_SUB)
            for p, (x_ref, h_ref, d_ref, o_ref) in enumerate(parts):
                for h in HEADS:
                    cs = HEAD_COLS[h]
                    wide = slice(p * D_HALF + h * HEAD, p * D_HALF + (h + 1) * HEAD)
                    cw = cw_ref[:, wide]
                    prev8 = h_ref[:, cs] * live if first else x_ref[pl.ds(r0 - 8, 8), cs]
                    taps = _conv_taps(x_ref[rows, cs], prev8)
                    y = _conv_pre(taps, cw)
                    sg = _sigmoid(y)
                    sv = y * sg
                    ds = d_ref[rows, cs]
                    if p < 2:
                        rn = lax.rsqrt(_rowsum(sv * sv) + EPS)
                        nrm = sv * rn
                        ds = rn * (ds - nrm * _rowsum(ds * nrm))
                    dy = ds * (sg * (1.0 + y * (1.0 - sg)))
                    for j in range(CONV_K):
                        gcw_ref[8 * j:8 * j + 8, wide] += _rows8(dy * taps[j])
                    nxt = below[:, wide]
                    acc = dy * cw[CONV_K - 1:CONV_K]
                    for sft in range(1, CONV_K):
                        acc = acc + _shift_up(dy, nxt, sft) * cw[CONV_K - 1 - sft:CONV_K - sft]
                    o_ref[rows, cs] = acc.astype(BF16)
                    below[:, wide] = dy[0:8]

            ba = ba_ref[rows, :]
            dba = jnp.zeros((CONV_SUB, HEAD), F32)
            gsm = jnp.zeros((8, HEAD), F32)
            for h in HEADS:
                beta = _sigmoid(ba[:, h:h + 1])
                dbeta = dbeta_ref[rows, h * HEAD:h * HEAD + 1]
                xg = ba[:, N_HEADS + h:N_HEADS + h + 1] + dtb_ref[0:1, h:h + 1]
                nexp = -jnp.exp(al_ref[0:1, h:h + 1])
                dgv = dg_ref[rows, h * HEAD:h * HEAD + 1]
                da = dgv * nexp * _sigmoid(xg)
                dba = dba + jnp.where(lane == h, dbeta * beta * (1.0 - beta), 0.0) + jnp.where(lane == N_HEADS + h, da, 0.0)
                gsm = (gsm + jnp.where(lane8 == h, _rows8(dgv * nexp * _softplus(xg)), 0.0)
                       + jnp.where(lane8 == N_HEADS + h, _rows8(da), 0.0))
            dba_ref[rows, :] = jnp.zeros((CONV_SUB, D_HALF), BF16)
            dba_ref[rows, :HEAD] = dba.astype(BF16)
            gsm_ref[...] += gsm

        def step(k, carry):
            sub_tile(pl.multiple_of((n_sub - 1 - k) * CONV_SUB, CONV_SUB), False)
            return carry

        lax.fori_loop(0, n_sub - 1, step, 0)
        sub_tile(0, True)

        @pl.when(pl.program_id(0) == n_tiles - 1)
        def _():
            gcw_out[...] = jnp.zeros_like(gcw_out)
            for j in range(CONV_K):
                gcw_out[j:j + 1, :] = _colsum(gcw_ref[8 * j:8 * j + 8, :])
            gsm_out[...] = jnp.broadcast_to(_colsum(gsm_ref[...]), (8, HEAD))

    row = pl.BlockSpec((t, D_HALF), lambda i: (tile_of(i), 0))
    const = lambda shape: pl.BlockSpec(shape, lambda i: (0, 0))
    return _call(
        body, name="conv_bwd", grid=(n_tiles,),
        in_specs=_conv_specs(t, tile_of) + [pl.BlockSpec((t, HEAD), lambda i: (tile_of(i), COL_BA // HEAD)),
                                            const((CONV_K, 3 * D_HALF)), const((1, N_HEADS)), const((1, N_HEADS))] + [row] * 5,
        out_specs=[row, row, row, row, const((8, 3 * D_HALF)), const((8, HEAD))],
        out_shape=[_sds((s, D_HALF), BF16)] * 4 + [_sds((8, 3 * D_HALF)), _sds((8, HEAD))],
        scratch_shapes=[pltpu.VMEM((8, 3 * D_HALF), F32), pltpu.VMEM((8 * CONV_K, 3 * D_HALF), F32),
                        pltpu.VMEM((8, HEAD), F32)],
        compiler_params=_params("arbitrary"),
    )(proj, proj, proj, proj, proj, proj, proj, conv_w, a_log, dt_bias, dqn, dkn, dvs, dbeta, dg)


def _conv_bwd_pre(proj, conv_w, a_log, dt_bias, dqn, dkn, dvs, dbeta, dg):
    s = proj.shape[0]
    t = CONV_T

    def body(q_ref, k_ref, v_ref, hq_ref, hk_ref, hv_ref, ba_ref, cw_ref, al_ref, dtb_ref,
             dqn_ref, dkn_ref, dvs_ref, dbeta_ref, dg_ref, dyq_ref, dyk_ref, dyv_ref, dba_ref, gcw_ref, gsm_ref):
        @pl.when(pl.program_id(0) == 0)
        def _():
            gcw_ref[...] = jnp.zeros_like(gcw_ref)
            gsm_ref[...] = jnp.zeros_like(gsm_ref)

        live = (pl.program_id(0) > 0).astype(F32)
        parts = ((q_ref, hq_ref, dqn_ref, dyq_ref), (k_ref, hk_ref, dkn_ref, dyk_ref), (v_ref, hv_ref, dvs_ref, dyv_ref))
        for p, (x_ref, h_ref, d_ref, dy_ref) in enumerate(parts):
            cols = slice(p * D_HALF, (p + 1) * D_HALF)
            taps = _conv_taps(x_ref[...], h_ref[...] * live)
            y = _conv_pre(taps, cw_ref[:, cols])
            sg = _sigmoid(y)
            sv = y * sg
            if p == 2:
                ds = d_ref[...]
            else:
                segs = []
                for h in HEADS:
                    seg = _head(sv, h)
                    rn = lax.rsqrt(_rowsum(seg * seg) + EPS)
                    nrm = seg * rn
                    dn = d_ref[:, HEAD_COLS[h]]
                    segs.append(rn * (dn - nrm * _rowsum(dn * nrm)))
                ds = jnp.concatenate(segs, axis=1)
            dy = ds * (sg * (1.0 + y * (1.0 - sg)))
            dy_ref[...] = dy
            for j in range(CONV_K):
                gcw_ref[j:j + 1, cols] += _colsum(dy * taps[j])

        ba = ba_ref[...]
        lane = lax.broadcasted_iota(I32, (t, HEAD), 1)
        lane1 = lax.broadcasted_iota(I32, (1, HEAD), 1)
        dba = jnp.zeros((t, HEAD), F32)
        gsm = jnp.zeros((1, HEAD), F32)
        for h in HEADS:
            beta = _sigmoid(ba[:, h:h + 1])
            dbeta = dbeta_ref[:, h * HEAD:h * HEAD + 1]
            xg = ba[:, N_HEADS + h:N_HEADS + h + 1] + dtb_ref[0:1, h:h + 1]
            nexp = -jnp.exp(al_ref[0:1, h:h + 1])
            dgv = dg_ref[:, h * HEAD:h * HEAD + 1]
            da = dgv * nexp * _sigmoid(xg)
            dba = dba + jnp.where(lane == h, dbeta * beta * (1.0 - beta), 0.0) + jnp.where(lane == N_HEADS + h, da, 0.0)
            gsm = (gsm + jnp.where(lane1 == h, _colsum(dgv * nexp * _softplus(xg)), 0.0)
                   + jnp.where(lane1 == N_HEADS + h, _colsum(da), 0.0))
        dba_ref[...] = jnp.zeros_like(dba_ref)
        dba_ref[:, :HEAD] = dba.astype(BF16)
        gsm_ref[0:1, :] += gsm

    row = pl.BlockSpec((t, D_HALF), lambda i: (i, 0))
    return _call(
        body, name="conv_bwd_pre", grid=(s // t,),
        in_specs=_conv_specs(t) + [pl.BlockSpec((t, HEAD), lambda i: (i, COL_BA // HEAD)),
                                   pl.BlockSpec((CONV_K, 3 * D_HALF), lambda i: (0, 0)),
                                   pl.BlockSpec((1, N_HEADS), lambda i: (0, 0)),
                                   pl.BlockSpec((1, N_HEADS), lambda i: (0, 0))] + [row] * 5,
        out_specs=[row, row, row, row,
                   pl.BlockSpec((8, 3 * D_HALF), lambda i: (0, 0)), pl.BlockSpec((8, HEAD), lambda i: (0, 0))],
        out_shape=[_sds((s, D_HALF))] * 3 + [_sds((s, D_HALF), BF16), _sds((8, 3 * D_HALF)), _sds((8, HEAD))],
        compiler_params=_params("arbitrary"),
    )(proj, proj, proj, proj, proj, proj, proj, conv_w, a_log, dt_bias, dqn, dkn, dvs, dbeta, dg)


def _conv_bwd_in(dyq, dyk, dyv, conv_w):
    s = dyq.shape[0]
    t = CONV_T
    last = s // 8 - 1

    def body(q_ref, k_ref, v_ref, nq_ref, nk_ref, nv_ref, cw_ref, oq_ref, ok_ref, ov_ref):
        more = (pl.program_id(0) < pl.num_programs(0) - 1).astype(F32)
        for p, (d_ref, n_ref, o_ref) in enumerate(((q_ref, nq_ref, oq_ref), (k_ref, nk_ref, ok_ref), (v_ref, nv_ref, ov_ref))):
            cw = cw_ref[:, p * D_HALF:(p + 1) * D_HALF]
            dy = d_ref[...]
            nxt = n_ref[...] * more
            acc = dy * cw[3:4]
            for sft in (1, 2, 3):
                acc = acc + _shift_up(dy, nxt, sft) * cw[3 - sft:4 - sft]
            o_ref[...] = acc.astype(BF16)

    row = pl.BlockSpec((t, D_HALF), lambda i: (i, 0))
    nxt = pl.BlockSpec((8, D_HALF), lambda i: (jnp.minimum((i + 1) * (t // 8), last), 0))
    return _call(
        body, name="conv_bwd_in", grid=(s // t,),
        in_specs=[row] * 3 + [nxt] * 3 + [pl.BlockSpec((CONV_K, 3 * D_HALF), lambda i: (0, 0))],
        out_specs=[row] * 3, out_shape=[_sds((s, D_HALF), BF16)] * 3,
        compiler_params=_params("arbitrary"),
    )(dyq, dyk, dyv, dyq, dyk, dyv, conv_w)


IN_T = 512


def _in_bwd(x, dh, norm_w, w_pad, pieces):
    s = x.shape[0]
    t = IN_T
    widths = [D_HALF] * 6 + [N_IN_PAD - COL_BA]

    def body(*refs):
        x_ref, dh_ref, nw_ref, w_ref = refs[:4]
        p_refs = refs[4:4 + len(pieces)]
        gx_ref, gnw_ref = refs[4 + len(pieces):]

        @pl.when(pl.program_id(0) == 0)
        def _():
            gnw_ref[...] = jnp.zeros_like(gnw_ref)

        dn = jnp.zeros((t, D_MODEL), F32)
        col = 0
        for p_ref, wd in zip(p_refs, widths):
            dn = dn + _bdot_nt(p_ref[...], w_ref[:, col:col + wd])
            col += wd
        xv = x_ref[...]
        r = lax.rsqrt(jnp.mean(xv * xv, axis=-1, keepdims=True) + EPS)
        xhat = xv * r
        gnw_ref[...] += _colsum(dn * xhat)
        dxh = dn * nw_ref[...]
        gx_ref[...] = dh_ref[...] + r * (dxh - xhat * jnp.mean(dxh * xhat, axis=-1, keepdims=True))

    wide = pl.BlockSpec((t, D_MODEL), lambda i: (i, 0))
    return _call(
        body, name="in_bwd", grid=(s // t,),
        in_specs=[wide, wide, pl.BlockSpec((1, D_MODEL), lambda i: (0, 0)),
                  pl.BlockSpec((D_MODEL, N_IN_PAD), lambda i: (0, 0))]
                 + [pl.BlockSpec((t, wd), lambda i: (i, 0)) for wd in widths],
        out_specs=[wide, pl.BlockSpec((1, D_MODEL), lambda i: (0, 0))],
        out_shape=[_sds((s, D_MODEL)), _sds((1, D_MODEL))],
        compiler_params=_params("arbitrary"),
    )(x, dh, norm_w, w_pad, *pieces)


def _adamw_shard(name, w, g_own, g_got, cidx, m, v):
    _, r, c = w.shape
    half = r // 2
    rows = 256 if half % 256 == 0 else half
    per_half = half // rows

    def body(c_ref, w_ref, go_ref, gg_ref, m_ref, v_ref, gout_ref, d_ref, nm_ref, nv_ref):
        mine = (pl.program_id(0) // per_half) == c_ref[0]
        gv = jnp.where(mine, go_ref[:, :c], gg_ref[:, :c])
        gout_ref[0] = gv
        mn = ADAM_B1 * m_ref[0] + (1.0 - ADAM_B1) * gv
        vn = ADAM_B2 * v_ref[0] + (1.0 - ADAM_B2) * (gv * gv)
        m_hat = mn / (1.0 - ADAM_B1 ** ADAM_STEP)
        v_hat = vn / (1.0 - ADAM_B2 ** ADAM_STEP)
        d_ref[0] = -ADAM_LR * (m_hat / (jnp.sqrt(v_hat) + ADAM_EPS) + ADAM_WD * w_ref[0])
        nm_ref[0] = mn
        nv_ref[0] = vn

    blk = pl.BlockSpec((1, rows, c), lambda i, c_ref: (0, i, 0))
    gblk = pl.BlockSpec((rows, g_own.shape[1]), lambda i, c_ref: (i % per_half, 0))
    return _call(
        body, name=name,
        grid_spec=pltpu.PrefetchScalarGridSpec(
            num_scalar_prefetch=1, grid=(2 * per_half,),
            in_specs=[blk, gblk, gblk, blk, blk], out_specs=[blk] * 4),
        out_shape=[_sds((1, r, c))] * 4,
        compiler_params=_params("arbitrary"),
    )(cidx, w, g_own, g_got, m, v)


def _adamw_tiles(name, w, g, m, v):
    n = w.shape[0]
    nb = 77 if n % 77 == 0 else n

    def body(w_ref, g_ref, m_ref, v_ref, d_ref, nm_ref, nv_ref):
        gv = g_ref[...]
        mn = ADAM_B1 * m_ref[...] + (1.0 - ADAM_B1) * gv
        vn = ADAM_B2 * v_ref[...] + (1.0 - ADAM_B2) * (gv * gv)
        m_hat = mn / (1.0 - ADAM_B1 ** ADAM_STEP)
        v_hat = vn / (1.0 - ADAM_B2 ** ADAM_STEP)
        d_ref[...] = -ADAM_LR * (m_hat / (jnp.sqrt(v_hat) + ADAM_EPS) + ADAM_WD * w_ref[...])
        nm_ref[...] = mn
        nv_ref[...] = vn

    blk = pl.BlockSpec((nb, 8, HEAD), lambda i: (i, 0, 0))
    return _call(
        body, name=name, grid=(n // nb,),
        in_specs=[blk] * 4, out_specs=[blk] * 3, out_shape=[_sds(w.shape)] * 3,
        compiler_params=_params("arbitrary"),
    )(w, g, m, v)


def _exchange(name, inputs, out_shapes, phases):
    n_in = len(inputs)
    n_out = len(out_shapes)
    n_cp = sum(len(p) for p in phases)

    def body(*refs):
        ins, outs = refs[:n_in], refs[n_in:n_in + n_out]
        send, recv = refs[n_in + n_out:]
        pos = (lax.axis_index("x"), lax.axis_index("y"), lax.axis_index("c"))
        k = 0
        for phase in phases:
            cps = []
            for src, dst, target in phase:
                cps.append(pltpu.make_async_remote_copy(
                    src_ref=src(ins, outs, pos), dst_ref=dst(ins, outs, pos), send_sem=send.at[k], recv_sem=recv.at[k],
                    device_id=target(pos), device_id_type=pl.DeviceIdType.MESH))
                k += 1
            for cp in cps:
                cp.start()
            for cp in cps:
                cp.wait()

    anyspec = pl.BlockSpec(memory_space=pl.ANY)
    return _call(
        body, name=name,
        in_specs=[anyspec] * n_in, out_specs=[anyspec] * n_out, out_shape=list(out_shapes),
        scratch_shapes=[pltpu.SemaphoreType.DMA((n_cp,)), pltpu.SemaphoreType.DMA((n_cp,))],
    )(*inputs)


def _exchange_start(name, inputs, out_shapes, copies):
    n_in, n_out, n_cp = len(inputs), len(out_shapes), len(copies)

    def body(*refs):
        ins, lands = refs[:n_in], refs[n_in:n_in + n_out]
        sems = refs[n_in + n_out:n_in + n_out + 2 * n_cp]
        token = refs[-1]
        pos = (lax.axis_index("x"), lax.axis_index("y"), lax.axis_index("c"))
        for k, (src, dst, target) in enumerate(copies):
            pltpu.make_async_remote_copy(
                src_ref=src(ins, lands, pos), dst_ref=dst(ins, lands, pos), send_sem=sems[2 * k], recv_sem=sems[2 * k + 1],
                device_id=target(pos), device_id_type=pl.DeviceIdType.MESH).start()
        token[...] = jnp.zeros_like(token)

    hbm = pl.BlockSpec(memory_space=pltpu.HBM)
    sem = pl.BlockSpec(memory_space=pltpu.SEMAPHORE)
    bufs = list(inputs) + [lax.empty(o.shape, o.dtype) for o in out_shapes]
    outs = _call(
        body, name=name,
        out_shape=tuple([pltpu.SemaphoreType.DMA(())] * (2 * n_cp) + [pltpu.HBM(b.shape, b.dtype) for b in bufs]
                        + [_sds((8, HEAD))]),
        in_specs=[hbm] * len(bufs),
        out_specs=tuple([sem] * (2 * n_cp) + [hbm] * len(bufs) + [pl.BlockSpec(memory_space=pltpu.VMEM)]),
        input_output_aliases={i: 2 * n_cp + i for i in range(len(bufs))},
        compiler_params=pltpu.CompilerParams(has_side_effects=pltpu.SideEffectType.DATAFLOW_SIDE_EFFECTING),
    )(*[pltpu.with_memory_space_constraint(b, pltpu.HBM) for b in bufs])
    return outs[:2 * n_cp], outs[2 * n_cp:2 * n_cp + n_in], outs[2 * n_cp + n_in:-1], outs[-1]


def _exchange_wait(name, sems, sources, lands, copies, after):
    n_in, n_out, n_cp = len(sources), len(lands), len(copies)

    def body(*refs):
        ins, zones = refs[:n_in], refs[n_in:n_in + n_out]
        sem_refs = refs[n_in + n_out:n_in + n_out + 2 * n_cp]
        pos = (lax.axis_index("x"), lax.axis_index("y"), lax.axis_index("c"))
        for k, (src, dst, target) in enumerate(copies):
            cp = pltpu.make_async_remote_copy(
                src_ref=src(ins, zones, pos), dst_ref=dst(ins, zones, pos), send_sem=sem_refs[2 * k],
                recv_sem=sem_refs[2 * k + 1], device_id=target(pos), device_id_type=pl.DeviceIdType.MESH)
            cp.wait_send()
            cp.wait_recv()

    hbm = pl.BlockSpec(memory_space=pltpu.HBM)
    sem = pl.BlockSpec(memory_space=pltpu.SEMAPHORE)
    bufs = list(sources) + list(lands)
    outs = _call(
        body, name=name,
        out_shape=tuple(pltpu.HBM(b.shape, b.dtype) for b in bufs),
        in_specs=[hbm] * len(bufs) + [sem] * (2 * n_cp) + [pl.BlockSpec(memory_space=pl.ANY)],
        out_specs=tuple([hbm] * len(bufs)),
        input_output_aliases={i: i for i in range(len(bufs))},
        compiler_params=pltpu.CompilerParams(has_side_effects=pltpu.SideEffectType.DATAFLOW_SIDE_EFFECTING),
    )(*bufs, *sems, after)
    return outs[:n_in], outs[n_in:]


def _allreduce_tile(name, v):
    def body(v_ref, out_ref, slots, send, recv):
        x, y, c = lax.axis_index("x"), lax.axis_index("y"), lax.axis_index("c")
        me = 4 * x + 2 * y + c
        slots[me] = v_ref[...]
        cps = []
        for k in range(1, 8):
            peer = (x ^ (k >> 2), y ^ ((k >> 1) & 1), c ^ (k & 1))
            cps.append(pltpu.make_async_remote_copy(
                src_ref=v_ref, dst_ref=slots.at[me], send_sem=send.at[k - 1], recv_sem=recv.at[k - 1],
                device_id=peer, device_id_type=pl.DeviceIdType.MESH))
        for cp in cps:
            cp.start()
        for cp in cps:
            cp.wait()
        acc = slots[0]
        for i in range(1, 8):
            acc = acc + slots[i]
        out_ref[...] = acc

    vm = pl.BlockSpec(memory_space=pltpu.VMEM)
    return _call(
        body, name=name, in_specs=[vm], out_specs=vm, out_shape=_sds(v.shape),
        scratch_shapes=[pltpu.VMEM((8,) + v.shape, F32), pltpu.SemaphoreType.DMA((7,)), pltpu.SemaphoreType.DMA((7,))],
    )(v)


def _chip(pos):
    return 2 * pos[0] + pos[1]


def _other_chip(pos, mask):
    x, y, c = pos
    return (x ^ (mask >> 1), y ^ (mask & 1), c)


def _sibling(pos):
    return (pos[0], pos[1], 1 - pos[2])


def _gather_weights(wb, cb):
    rows = wb.shape[0] // 2

    def half(pos):
        return pl.ds(pos[2] * rows, rows)

    first, second = [], []
    for mask in CHIP_MASKS:
        first.append((lambda ins, outs, pos: ins[0].at[half(pos)],
                      lambda ins, outs, pos: outs[0].at[_chip(pos), half(pos)],
                      functools.partial(_other_chip, mask=mask)))
        second.append((lambda ins, outs, pos, mask=mask: outs[0].at[_chip(pos) ^ mask, half(pos)],
                       lambda ins, outs, pos, mask=mask: outs[0].at[_chip(pos) ^ mask, half(pos)],
                       _sibling))
        first.append((lambda ins, outs, pos: ins[1],
                      lambda ins, outs, pos: outs[1].at[_chip(pos)],
                      functools.partial(_other_chip, mask=mask)))
    return _exchange("gather_weights", [wb, cb], [_sds((4,) + wb.shape, wb.dtype), _sds((4,) + cb.shape, cb.dtype)],
                     [first, second])


def _gather_blocks(ob):
    copies = [(lambda ins, outs, pos: ins[0], lambda ins, outs, pos: outs[0].at[_chip(pos)],
               functools.partial(_other_chip, mask=mask)) for mask in CHIP_MASKS]
    return [_sds((4,) + ob.shape, ob.dtype)], copies


def _to_sibling_half(name, arrays):
    def src(ins, outs, pos, a):
        h = arrays[a].shape[-2] // 2
        sl = pl.ds((1 - pos[2]) * h, h)
        return ins[a].at[:, sl] if arrays[a].ndim == 3 else ins[a].at[sl]

    outs = [_sds(a.shape[:-2] + (a.shape[-2] // 2, a.shape[-1]), a.dtype) for a in arrays]
    phase = [(functools.partial(src, a=a), lambda ins, outs, pos, a=a: outs[a], _sibling) for a in range(len(arrays))]
    return _exchange(name, arrays, outs, [phase])


def _add_half(name, full, part, cidx):
    shape = part.shape
    lead = shape[0] if len(shape) == 3 else 1
    rows, cols = shape[-2], shape[-1]
    tr = rows // 2 if rows % 16 == 0 else rows
    nr = rows // tr
    f3 = full.reshape((lead,) + full.shape[-2:])
    p3 = part.reshape((lead, rows, cols))

    def body(c_ref, f_ref, p_ref, o_ref):
        o_ref[...] = (f_ref[...].astype(F32) + p_ref[...].astype(F32)).astype(o_ref.dtype)

    out = _call(
        body, name=name,
        grid_spec=pltpu.PrefetchScalarGridSpec(
            num_scalar_prefetch=1, grid=(lead, nr),
            in_specs=[pl.BlockSpec((1, tr, cols), lambda b, r, c_ref: (b, c_ref[0] * nr + r, 0)),
                      pl.BlockSpec((1, tr, cols), lambda b, r, c_ref: (b, r, 0))],
            out_specs=pl.BlockSpec((1, tr, cols), lambda b, r, c_ref: (b, r, 0))),
        out_shape=_sds((lead, rows, cols), part.dtype),
        compiler_params=_params("arbitrary", "arbitrary"),
    )(cidx, f3, p3)
    return out.reshape(shape)


def _to_other_chips(arrays, blocked):
    def src(ins, outs, pos, a, mask):
        return ins[a].at[_chip(pos) ^ mask] if blocked[a] else ins[a]

    outs = [_sds((3,) + (a.shape[1:] if b else a.shape), a.dtype) for a, b in zip(arrays, blocked)]
    copies = []
    for mi, mask in enumerate(CHIP_MASKS):
        for a in range(len(arrays)):
            copies.append((functools.partial(src, a=a, mask=mask), lambda ins, outs, pos, a=a, mi=mi: outs[a].at[mi],
                           functools.partial(_other_chip, mask=mask)))
    return outs, copies


def _add_chips(name, own, got, jidx, blocked):
    rows, cols = got.shape[-2:]
    tr = rows // 2 if rows % 16 == 0 else rows
    nr = rows // tr
    o3 = own if blocked else own.reshape((1, rows, cols))

    def body(j_ref, o_ref, g_ref, out_ref):
        out_ref[...] = ((o_ref[0].astype(F32) + g_ref[0].astype(F32))
                        + (g_ref[1].astype(F32) + g_ref[2].astype(F32)))

    own_map = (lambda r, j_ref: (j_ref[0], r, 0)) if blocked else (lambda r, j_ref: (0, r, 0))
    return _call(
        body, name=name,
        grid_spec=pltpu.PrefetchScalarGridSpec(
            num_scalar_prefetch=1, grid=(nr,),
            in_specs=[pl.BlockSpec((1, tr, cols), own_map),
                      pl.BlockSpec((3, tr, cols), lambda r, j_ref: (0, r, 0))],
            out_specs=pl.BlockSpec((tr, cols), lambda r, j_ref: (r, 0))),
        out_shape=_sds((rows, cols)),
        compiler_params=_params("arbitrary"),
    )(jidx, o3, got)


def _to_sibling(name, arrays):
    phase = [(lambda ins, outs, pos, a=a: ins[a], lambda ins, outs, pos, a=a: outs[a], _sibling)
             for a in range(len(arrays))]
    return _exchange(name, arrays, [_sds(a.shape, a.dtype) for a in arrays], [phase])


def _local_step(x, target, w_pad, w_out, conv_w, norm_w, pool_w, pool_scale, a_log, dt_bias, dn_norm_w, final_norm_w):
    proj, n_t = _proj_fwd(x, norm_w, w_pad)
    y_pool = _pool_fwd(proj, pool_w, pool_scale)
    qn, kn, vs, beta, g = _conv_fwd(proj, conv_w, a_log, dt_bias)
    u, w, att, qd, kd, tm, cd = _intra_fwd(qn, kn, vs, beta, g)
    o, vn, st = _scan_fwd(u, w, att, qd, kd, cd)
    w_out = w_out(o) if callable(w_out) else w_out
    y_t, dh, dyp, do, ddz, loss, g_fnw, g_dnw = _out_fwd_bwd(x, y_pool, o, proj, target, w_out, dn_norm_w, final_norm_w)
    g_wout = _token_matmul("grad_w_out", y_t, [(dh, 0), (dh, 1)])
    dpu, dpz, g_pw, g_ps = _pool_bwd(proj, dyp, pool_w, pool_scale)
    du, dw, datt, dqd, dkd, dcd = _scan_bwd(do, vn, qd, kd, w, att, cd, st)
    dqn, dkn, dvs, dbeta, dg = _intra_bwd(qn, kn, vs, beta, g, tm, du, dw, datt, dqd, dkd, dcd)
    dcq, dck, dcv, dba, g_cw, g_sm = _conv_bwd(proj, conv_w, a_log, dt_bias, dqn, dkn, dvs, dbeta, dg)
    pieces = [dpu, dpz, dcq, dck, dcv, ddz, dba]
    g_win = _token_matmul("grad_w_in", n_t, [(p, 0) for p in pieces])
    small = dict(norm_w=jnp.zeros_like(norm_w), pool_w=g_pw, pool_scale=g_ps, conv_w=g_cw[:CONV_K],
                 a_log=g_sm[0:1, 0:N_HEADS], dt_bias=g_sm[0:1, N_HEADS:2 * N_HEADS], dn_norm_w=g_dnw, final_norm_w=g_fnw)
    return loss[0, 0], g_win, g_wout, small, dh, pieces


def _pack_small(t):
    lanes = lambda a: jnp.pad(a.reshape(1, -1), ((0, 0), (0, HEAD - a.size)))
    rows = [t["pool_w"].reshape(-1, HEAD), t["norm_w"].reshape(-1, HEAD), t["final_norm_w"].reshape(-1, HEAD),
            t["pool_scale"].reshape(-1, HEAD), t["conv_w"].reshape(-1, HEAD), t["dn_norm_w"].reshape(1, HEAD),
            lanes(t["a_log"]), lanes(t["dt_bias"]), lanes(t.get("loss", jnp.zeros((1,), F32)))]
    buf = jnp.concatenate(rows, axis=0)
    return jnp.pad(buf, ((0, SMALL_ROWS - buf.shape[0]), (0, 0)))


def _unpack_small(buf, conv_cols):
    out, r = {}, 0
    for name, nrow, shape in (("pool_w", 512, (1, N_HEADS, HEAD, HEAD)), ("norm_w", 8, (1, D_MODEL)),
                              ("final_norm_w", 8, (D_MODEL,)), ("pool_scale", 4, (1, D_HALF)),
                              ("conv_w", CONV_K * conv_cols // HEAD, (1, CONV_K, conv_cols)), ("dn_norm_w", 1, (1, HEAD))):
        out[name] = buf[r:r + nrow].reshape(shape)
        r += nrow
    out["a_log"] = buf[r:r + 1, :N_HEADS]
    out["dt_bias"] = buf[r + 1:r + 2, :N_HEADS]
    out["loss"] = buf[r + 2, 0]
    return out


def kernel(x, norm_w, w_in, pool_w, pool_scale, conv_w, a_log, dt_bias, dn_norm_w, w_out, final_norm_w, loss_target, m_norm_w, m_w_in, m_pool_w, m_pool_scale, m_conv_w, m_a_log, m_dt_bias, m_dn_norm_w, m_w_out, m_final_norm_w, v_norm_w, v_w_in, v_pool_w, v_pool_scale, v_conv_w, v_a_log, v_dt_bias, v_dn_norm_w, v_w_out, v_final_norm_w):
    cidx = lax.axis_index("c").astype(I32).reshape(1)
    jidx = (2 * lax.axis_index("x") + lax.axis_index("y")).astype(I32)

    wb = jnp.pad(w_in[0].astype(BF16), ((0, 0), (0, BLK_IN_PAD - BLK_IN)))
    ob = w_out[0].astype(BF16)
    gw, gc = _gather_weights(wb, conv_w[0])
    mine = lambda j: jidx == j
    w_pad = jnp.concatenate([jnp.where(mine(j), wb[:, :BLK_IN], gw[j, :, :BLK_IN]) for j in range(4)]
                            + [jnp.zeros((D_MODEL, N_IN_PAD - N_IN), BF16)], axis=1)
    cw_full = jnp.concatenate([jnp.where(mine(j), conv_w[0], gc[j]) for j in range(4)], axis=1)

    lands_o, copies_o = _gather_blocks(ob)
    sems_o, ob_thru, zones_o, token_o = _exchange_start("gather_w_out_start", [ob], lands_o, copies_o)

    def w_out_full(after):
        (own,), (got,) = _exchange_wait("gather_w_out_wait", sems_o, ob_thru, zones_o, copies_o, after)
        return jnp.where((jnp.arange(4) == jidx)[:, None, None], own[None], got).reshape(D_MODEL, D_MODEL)

    loss, g_win, g_wout, small, dh, pieces = _local_step(
        x[0], loss_target[0], w_pad, w_out_full, cw_full, norm_w + token_o[0, 0], pool_w[0], pool_scale, a_log, dt_bias,
        dn_norm_w, final_norm_w.reshape(1, D_MODEL))
    small["loss"] = loss

    blocks_in = jnp.stack([jnp.pad(g_win[:, j * BLK_IN:(j + 1) * BLK_IN].astype(BF16), ((0, 0), (0, BLK_IN_PAD - BLK_IN)))
                           for j in range(4)])
    blocks_out = g_wout.astype(BF16).reshape(4, BLK_OUT, D_MODEL)
    full = [blocks_in, blocks_out, _pack_small(small)]
    from_sib = _to_sibling_half("reduce_sibling", full)
    chip_sum = [_add_half("add_sibling_%d" % i, f, p, cidx) for i, (f, p) in enumerate(zip(full, from_sib))]
    blocked = [True, True, False]
    lands, copies = _to_other_chips(chip_sum, blocked)
    sems, chip_sum, zones, token = _exchange_start("reduce_chips_start", chip_sum, lands, copies)
    gx, g_nw = _in_bwd(x[0], dh, norm_w + token[0, 0], w_pad, pieces)
    g_nw = _allreduce_tile("reduce_norm_w", g_nw.reshape(8, HEAD)).reshape(1, D_MODEL)
    chip_sum, from_chips = _exchange_wait("reduce_chips_wait", sems, chip_sum, zones, copies, gx)
    halves = [_add_chips("add_chips_%d" % i, o, g, jidx.reshape(1), b)
              for i, (o, g, b) in enumerate(zip(chip_sum, from_chips, blocked))]
    other_halves = _to_sibling("swap_halves", halves)

    weights = dict(norm_w=norm_w, w_in=w_in, pool_w=pool_w, pool_scale=pool_scale, conv_w=conv_w, a_log=a_log,
                   dt_bias=dt_bias, dn_norm_w=dn_norm_w, w_out=w_out, final_norm_w=final_norm_w)
    ms = dict(norm_w=m_norm_w, w_in=m_w_in, pool_w=m_pool_w, pool_scale=m_pool_scale, conv_w=m_conv_w, a_log=m_a_log,
              dt_bias=m_dt_bias, dn_norm_w=m_dn_norm_w, w_out=m_w_out, final_norm_w=m_final_norm_w)
    vs = dict(norm_w=v_norm_w, w_in=v_w_in, pool_w=v_pool_w, pool_scale=v_pool_scale, conv_w=v_conv_w, a_log=v_a_log,
              dt_bias=v_dt_bias, dn_norm_w=v_dn_norm_w, w_out=v_w_out, final_norm_w=v_final_norm_w)
    names = ["norm_w", "w_in", "pool_w", "pool_scale", "conv_w", "a_log", "dt_bias", "dn_norm_w", "w_out", "final_norm_w"]
    small_names = [n for n in names if n not in ("w_in", "w_out")]

    def pack(t):
        conv = lax.dynamic_update_slice_in_dim(jnp.zeros((CONV_K, 3 * D_HALF), F32), t["conv_w"][0], jidx * BLK_CONV, axis=1)
        return _pack_small({**{n: t[n] for n in small_names if n != "conv_w"}, "conv_w": conv})[None]

    results = [{}, {}, {}, {}]
    to_tiles = lambda a: jnp.transpose(a, (2, 0, 1)).reshape(BLK_IN, 8, HEAD)
    from_tiles = lambda a: jnp.transpose(a, (1, 2, 0)).reshape(1, D_MODEL, BLK_IN)
    lo = jnp.where(cidx[0] == 0, halves[0], other_halves[0])
    hi = jnp.where(cidx[0] == 0, other_halves[0], halves[0])
    g_tiles = jnp.concatenate([lo[:, :BLK_IN].T, hi[:, :BLK_IN].T], axis=1).reshape(BLK_IN, 8, HEAD)
    outs = _adamw_tiles("adamw_w_in", to_tiles(w_in), g_tiles, to_tiles(m_w_in), to_tiles(v_w_in))
    for res, o in zip(results, (g_tiles,) + tuple(outs)):
        res["w_in"] = from_tiles(o)
    outs = _adamw_shard("adamw_w_out", w_out, halves[1], other_halves[1], cidx, m_w_out, v_w_out)
    for res, o in zip(results, outs):
        res["w_out"] = o
    outs = _adamw_shard("adamw_small", pack(weights), halves[2], other_halves[2], cidx, pack(ms), pack(vs))
    for res, o in zip(results, outs):
        got = _unpack_small(o[0], 3 * D_HALF)
        got["conv_w"] = lax.dynamic_slice_in_dim(got["conv_w"], jidx * BLK_CONV, BLK_CONV, axis=2)
        res.update(got)
    one_tile = lambda a: a.reshape(1, 8, HEAD)
    outs = _adamw_tiles("adamw_norm_w", one_tile(norm_w), one_tile(g_nw), one_tile(m_norm_w), one_tile(v_norm_w))
    for res, o in zip(results, (g_nw,) + tuple(outs)):
        res["norm_w"] = o.reshape(1, D_MODEL)
    grads, delta, new_m, new_v = results

    return (grads["loss"], gx[None], *[grads[n] for n in names], *[delta[n] for n in names],
            *[new_m[n] for n in names], *[new_v[n] for n in names])
```

```python
import functools

import jax
import jax.numpy as jnp
import numpy as np
from jax import lax
from jax.experimental import pallas as pl
from jax.experimental.pallas import tpu as pltpu

F32 = jnp.float32
BF16 = jnp.bfloat16
I32 = jnp.int32

D_MODEL = 1024
D_HALF = 512
N_HEADS = 4
HEAD = 128
CHUNK = 64
PAIR = 2 * CHUNK
WINDOWS = (2, 4, 8, 16)
CONV_K = 4
EPS = 1e-6
N_IN = 3080
N_IN_PAD = 3200
BLK_IN = 770
BLK_IN_PAD = 896
BLK_OUT = 256
BLK_CONV = 384
COL_BA = 3072
QK_SCALE = HEAD ** -0.5
SMALL_ROWS = 592
VMEM_LIMIT = 56 * 1024 * 1024

ADAM_LR = 0.001
ADAM_B1 = 0.9
ADAM_B2 = 0.999
ADAM_EPS = 1e-08
ADAM_WD = 0.01
ADAM_STEP = 10

CHIP_MASKS = (2, 1, 3)
HEADS = range(N_HEADS)
HEAD_COLS = [slice(h * HEAD, (h + 1) * HEAD) for h in HEADS]


def _call(body, **kw):
    return pl.pallas_call(body, **kw)


def _params(*sem):
    return pltpu.CompilerParams(dimension_semantics=sem, vmem_limit_bytes=VMEM_LIMIT)


def _sds(shape, dtype=F32):
    return jax.ShapeDtypeStruct(shape, dtype)


def _bdot(a, b):
    return jnp.dot(a.astype(BF16), b.astype(BF16), preferred_element_type=F32)


def _bdot_nt(a, b):
    return lax.dot_general(a.astype(BF16), b.astype(BF16), (((1,), (1,)), ((), ())), preferred_element_type=F32)


def _bdot_tn(a, b):
    return lax.dot_general(a.astype(BF16), b.astype(BF16), (((0,), (0,)), ((), ())), preferred_element_type=F32)


def _split(a):
    hi = a.astype(BF16)
    lo = (a - hi.astype(F32)).astype(BF16)
    return hi, lo


def _mask_dot(m, b, dims=(((1,), (0,)), ((), ()))):
    bh, bl = _split(b)
    dg = functools.partial(lax.dot_general, dimension_numbers=dims, preferred_element_type=F32)
    return dg(m, bh) + dg(m, bl)


def _sigmoid(x):
    return 0.5 * jnp.tanh(0.5 * x) + 0.5


def _softplus(x):
    return jnp.maximum(x, 0.0) + jnp.log(1.0 + jnp.exp(-jnp.abs(x)))


def _rowsum(x):
    return jnp.sum(x, axis=-1, keepdims=True)


def _colsum(x):
    return jnp.sum(x, axis=0, keepdims=True)


def _shift_down(xv, prev8, k):
    r = pltpu.roll(xv, k, 0)
    q = pltpu.roll(prev8, k, 0)
    row = lax.broadcasted_iota(I32, prev8.shape, 0)
    top = jnp.where(row < k, q, r[0:8])
    return jnp.concatenate([top, r[8:]], axis=0)


def _shift_up(xv, next8, k):
    t = xv.shape[0]
    r = pltpu.roll(xv, t - k, 0)
    q = pltpu.roll(next8, 8 - k, 0)
    row = lax.broadcasted_iota(I32, next8.shape, 0)
    bot = jnp.where(row >= 8 - k, q, r[t - 8:])
    return jnp.concatenate([r[:t - 8], bot], axis=0)


def _band(rows, cols, off, w, anti=False):
    r = lax.broadcasted_iota(I32, (rows, cols), 0)
    c = lax.broadcasted_iota(I32, (rows, cols), 1)
    d = (c - r + off) if anti else (r - c + off)
    return ((d >= 0) & (d < w)).astype(BF16)


def _head(ref_or_val, h):
    return ref_or_val[:, h * HEAD:(h + 1) * HEAD]


INTRA_PAIRS = 2
UNITS = [(pp, h) for pp in range(INTRA_PAIRS) for h in HEADS]


def _heads(ref, rows=PAIR):
    return [ref[pp * rows:(pp + 1) * rows, HEAD_COLS[h]] for pp, h in UNITS]


def _put_heads(ref, vals, rows=PAIR):
    for (pp, h), v in zip(UNITS, vals):
        ref[pp * rows:(pp + 1) * rows, HEAD_COLS[h]] = v.astype(ref.dtype)


def _each(fn, *lists):
    return [fn(*args) for args in zip(*lists)]


def _proj_fwd(x, norm_w, w_pad):
    s = x.shape[0]
    tm = 512

    def body(x_ref, nw_ref, w_ref, proj_ref, nt_ref):
        xv = x_ref[...]
        r = lax.rsqrt(jnp.mean(xv * xv, axis=-1, keepdims=True) + EPS)
        nv = xv * r * nw_ref[...]
        nt_ref[...] = nv.T.astype(BF16)
        proj_ref[...] = jnp.dot(nv.astype(BF16), w_ref[...], preferred_element_type=F32)

    return _call(
        body, name="proj_fwd", grid=(s // tm,),
        in_specs=[pl.BlockSpec((tm, D_MODEL), lambda i: (i, 0)),
                  pl.BlockSpec((1, D_MODEL), lambda i: (0, 0)),
                  pl.BlockSpec((D_MODEL, N_IN_PAD), lambda i: (0, 0))],
        out_specs=[pl.BlockSpec((tm, N_IN_PAD), lambda i: (i, 0)),
                   pl.BlockSpec((D_MODEL, tm), lambda i: (0, i))],
        out_shape=[_sds((s, N_IN_PAD)), _sds((D_MODEL, s), BF16)],
        compiler_params=_params("arbitrary"),
    )(x, norm_w, w_pad)


def _pool_bands(t, anti=False):
    r = np.arange(t)[:, None]
    c = np.arange(t + HEAD)[None, :]
    d = (c - r) if anti else (r - c + HEAD)
    return jnp.asarray(np.stack([(d >= 0) & (d < w) for w in WINDOWS]), BF16)


def _pool_mix(u, halo, z, pw, bands, row0):
    t = u[0].shape[0]
    rows = row0 + lax.broadcasted_iota(I32, (t, 1), 0) + 1
    cnt = [jnp.minimum(rows, w).astype(F32) for w in WINDOWS]
    win = _each(lambda b, h, v: _mask_dot(b, jnp.concatenate([h, v], axis=0)), bands, halo, u)
    mix = _each(lambda a, c, v: a / c - v, win, cnt, u)
    mixed = _each(_bdot, mix, pw)
    return mix, mixed, _each(_sigmoid, z), cnt


POOL_T = 256


def _pool_fwd(proj, pool_w, pool_scale):
    s = proj.shape[0]
    t = POOL_T
    hb = t // HEAD

    def body(u_ref, z_ref, halo_ref, pw_ref, ps_ref, band_ref, y_ref):
        i = pl.program_id(0)
        live = (i > 0).astype(F32)
        groups = lambda ref: [ref[:, sl] for sl in HEAD_COLS]
        z = groups(z_ref)
        _, mixed, sg, _ = _pool_mix(groups(u_ref), [h * live for h in groups(halo_ref)], z,
                                    [pw_ref[g] for g in HEADS], [band_ref[g] for g in HEADS], i * t)
        for sl, m, zg, s_ in zip(HEAD_COLS, mixed, z, sg):
            y_ref[:, sl] = m * ps_ref[:, sl] * (zg * s_)

    return _call(
        body, name="pool_fwd", grid=(s // t,),
        in_specs=[pl.BlockSpec((t, D_HALF), lambda i: (i, 0)),
                  pl.BlockSpec((t, D_HALF), lambda i: (i, 1)),
                  pl.BlockSpec((HEAD, D_HALF), lambda i: (jnp.maximum(i * hb - 1, 0), 0)),
                  pl.BlockSpec((N_HEADS, HEAD, HEAD), lambda i: (0, 0, 0)),
                  pl.BlockSpec((1, D_HALF), lambda i: (0, 0)),
                  pl.BlockSpec((N_HEADS, t, HEAD + t), lambda i: (0, 0, 0))],
        out_specs=pl.BlockSpec((t, D_HALF), lambda i: (i, 0)),
        out_shape=_sds((s, D_HALF)),
        compiler_params=_params("arbitrary"),
    )(proj, proj, proj, pool_w, pool_scale, _pool_bands(t))


def _conv_taps(xv, prev8):
    return [_shift_down(xv, prev8, CONV_K - 1 - j) for j in range(CONV_K - 1)] + [xv]


def _conv_pre(taps, cw):
    y = taps[CONV_K - 1] * cw[CONV_K - 1:CONV_K]
    for j in range(CONV_K - 2, -1, -1):
        y = y + taps[j] * cw[j:j + 1]
    return y


CONV_T = 256
CONV_SUB = 256


def _conv_specs(t, tile_of=lambda i: i):
    tiles = [pl.BlockSpec((t, D_HALF), functools.partial(lambda i, p: (tile_of(i), 2 + p), p=p)) for p in range(3)]
    halos = [pl.BlockSpec((8, D_HALF),
                          functools.partial(lambda i, p: (jnp.maximum(tile_of(i) * (t // 8) - 1, 0), 2 + p), p=p))
             for p in range(3)]
    return tiles + halos


def _conv_fwd(proj, conv_w, a_log, dt_bias):
    s = proj.shape[0]
    t = CONV_T

    def body(q_ref, k_ref, v_ref, hq_ref, hk_ref, hv_ref, ba_ref, cw_ref, al_ref, dtb_ref,
             qn_ref, kn_ref, vs_ref, beta_ref, g_ref):
        live = (pl.program_id(0) > 0).astype(F32)
        parts = ((q_ref, hq_ref, qn_ref), (k_ref, hk_ref, kn_ref), (v_ref, hv_ref, vs_ref))

        def sub_tile(r0, first):
            rows = pl.ds(r0, CONV_SUB)
            for p, (x_ref, h_ref, o_ref) in enumerate(parts):
                for h in HEADS:
                    cs = HEAD_COLS[h]
                    prev8 = h_ref[:, cs] * live if first else x_ref[pl.ds(r0 - 8, 8), cs]
                    y = _conv_pre(_conv_taps(x_ref[rows, cs], prev8), cw_ref[:, p * D_HALF + h * HEAD:p * D_HALF + (h + 1) * HEAD])
                    sv = y * _sigmoid(y)
                    o_ref[rows, cs] = sv if p == 2 else sv * lax.rsqrt(_rowsum(sv * sv) + EPS)
            ba = ba_ref[rows, :]
            for h in HEADS:
                beta = _sigmoid(ba[:, h:h + 1])
                gl = -jnp.exp(al_ref[0:1, h:h + 1]) * _softplus(ba[:, N_HEADS + h:N_HEADS + h + 1] + dtb_ref[0:1, h:h + 1])
                beta_ref[rows, HEAD_COLS[h]] = jnp.broadcast_to(beta, (CONV_SUB, HEAD))
                g_ref[rows, HEAD_COLS[h]] = jnp.broadcast_to(gl, (CONV_SUB, HEAD))

        sub_tile(0, True)

        def step(k, carry):
            sub_tile(pl.multiple_of(k * CONV_SUB, CONV_SUB), False)
            return carry

        lax.fori_loop(1, t // CONV_SUB, step, 0)

    row = pl.BlockSpec((t, D_HALF), lambda i: (i, 0))
    return _call(
        body, name="conv_fwd", grid=(s // t,),
        in_specs=_conv_specs(t) + [pl.BlockSpec((t, HEAD), lambda i: (i, COL_BA // HEAD)),
                                   pl.BlockSpec((CONV_K, 3 * D_HALF), lambda i: (0, 0)),
                                   pl.BlockSpec((1, N_HEADS), lambda i: (0, 0)),
                                   pl.BlockSpec((1, N_HEADS), lambda i: (0, 0))],
        out_specs=[row] * 5,
        out_shape=[_sds((s, D_HALF))] * 5,
        compiler_params=_params("arbitrary"),
    )(proj, proj, proj, proj, proj, proj, proj, conv_w, a_log, dt_bias)


def _pair_masks():
    r = lax.broadcasted_iota(I32, (PAIR, PAIR), 0)
    c = lax.broadcasted_iota(I32, (PAIR, PAIR), 1)
    same = jnp.right_shift(r, 6) == jnp.right_shift(c, 6)
    return same, same & (r >= c), same & (r > c), r == c


def _pair_common(qn, kn, vs, beta, g):
    same, incl, strict, eye = _pair_masks()
    incl_b = incl.astype(BF16)
    first = lax.broadcasted_iota(I32, (PAIR, HEAD), 0) < CHUNK
    gc = _each(lambda gv: _mask_dot(incl_b, gv), g)
    gc_row = _each(lambda v: _colsum(jnp.where(eye, v, 0.0)), gc)
    decay = _each(lambda v, r: jnp.where(incl, jnp.exp(jnp.where(incl, v - r, 0.0)), 0.0), gc, gc_row)
    gl = _each(lambda v: jnp.where(first, v[CHUNK - 1:CHUNK], v[PAIR - 1:PAIR]), gc)
    egc = _each(jnp.exp, gc)
    q = _each(lambda v: v * QK_SCALE, qn)
    kb = _each(lambda k, b: k * b, kn, beta)
    return dict(same=same, incl=incl, strict=strict, eye=eye, gc=gc, decay=decay, gl=gl, egc=egc,
                ekd=_each(lambda a, b: jnp.exp(a - b), gl, gc), cd=_each(jnp.exp, gl), q=q, kb=kb,
                vb=_each(lambda v, b: v * b, vs, beta), kbg=_each(lambda k, e: k * e, kb, egc),
                kk=_each(_bdot_nt, kb, kn), qk=_each(_bdot_nt, q, kn))


def _tri_inv(a, eye_f):
    p = _each(lambda v: eye_f - v, a)
    x = _each(_bdot, a, a)
    for it in range(5):
        p = _each(lambda pv, xv: pv + _bdot(pv, xv), p, x)
        if it < 4:
            x = _each(_bdot, x, x)
    return p


def _pair_spec():
    return pl.BlockSpec((INTRA_PAIRS * PAIR, D_HALF), lambda i: (i, 0))


def _chunk_scalar_spec(pairs=1, index=lambda i: (i, 0)):
    return pl.BlockSpec((16 * pairs, D_HALF), index)


SCAN_PAIRS = 2
SCAN_ROWS = SCAN_PAIRS * PAIR


def _intra_fwd(qn, kn, vs, beta, g):
    s = qn.shape[0]

    def body(qn_ref, kn_ref, vs_ref, beta_ref, g_ref, u_ref, w_ref, att_ref, qd_ref, kd_ref, t_ref, cd_ref):
        kn = _heads(kn_ref)
        cm = _pair_common(_heads(qn_ref), kn, _heads(vs_ref), _heads(beta_ref), _heads(g_ref))
        a = _each(lambda kk, d: jnp.where(cm["strict"], kk * d, 0.0), cm["kk"], cm["decay"])
        tm = _tri_inv(a, cm["eye"].astype(F32))
        _put_heads(t_ref, tm)
        _put_heads(u_ref, _each(_bdot, tm, cm["vb"]))
        _put_heads(w_ref, _each(_bdot, tm, cm["kbg"]))
        _put_heads(att_ref, _each(lambda a, b: a * b, cm["qk"], cm["decay"]))
        _put_heads(qd_ref, _each(lambda a, b: a * b, cm["q"], cm["egc"]))
        _put_heads(kd_ref, _each(lambda a, b: a * b, kn, cm["ekd"]))
        for ci in range(2):
            for (pp, h), v in zip(UNITS, cm["cd"]):
                cd_ref[pp * 16 + ci * 8:pp * 16 + (ci + 1) * 8, HEAD_COLS[h]] = v[ci * CHUNK:ci * CHUNK + 8]

    return _call(
        body, name="intra_fwd", grid=(s // (INTRA_PAIRS * PAIR),),
        in_specs=[_pair_spec()] * 5, out_specs=[_pair_spec()] * 6 + [_chunk_scalar_spec(INTRA_PAIRS)],
        out_shape=[_sds((s, D_HALF))] + [_sds((s, D_HALF), BF16)] * 5 + [_sds((s // 8, D_HALF))],
        compiler_params=_params("arbitrary"),
    )(qn, kn, vs, beta, g)


def _scan_fwd(u, w, att, qd, kd, cd):
    s = u.shape[0]
    n_chunks = s // CHUNK

    def body(u_ref, w_ref, att_ref, qd_ref, kd_ref, cd_ref, o_ref, vn_ref, st_ref, state):
        @pl.when(pl.program_id(0) == 0)
        def _():
            state[...] = jnp.zeros_like(state)
        cols = list(enumerate(HEAD_COLS))
        sm = [state[h] for h in HEADS]
        for ci in range(2 * SCAN_PAIRS):
            rs = slice(ci * CHUNK, (ci + 1) * CHUNK)
            for h in HEADS:
                st_ref[ci, h] = sm[h]
            both = [_bdot(jnp.concatenate([w_ref[rs, sl], qd_ref[rs, sl]], axis=0), sm[h]) for h, sl in cols]
            vn = [u_ref[rs, sl] - both[h][:CHUNK] for h, sl in cols]
            for h, sl in cols:
                vn_ref[rs, sl] = vn[h].astype(BF16)
                o_ref[rs, sl] = both[h][CHUNK:]
            sm = [sm[h] * cd_ref[ci * 8:ci * 8 + 1, sl] + _bdot_tn(kd_ref[rs, sl], vn[h]) for h, sl in cols]
        for h in HEADS:
            state[h] = sm[h]
        for pp in range(SCAN_PAIRS):
            rp = slice(pp * PAIR, (pp + 1) * PAIR)
            intra = [_bdot(att_ref[rp, sl], vn_ref[rp, sl]) for sl in HEAD_COLS]
            for h, sl in cols:
                o_ref[rp, sl] += intra[h]

    rows = pl.BlockSpec((SCAN_ROWS, D_HALF), lambda i: (i, 0))
    return _call(
        body, name="scan_fwd", grid=(s // SCAN_ROWS,),
        in_specs=[rows] * 5 + [_chunk_scalar_spec(SCAN_PAIRS)],
        out_specs=[rows, rows, pl.BlockSpec((2 * SCAN_PAIRS, N_HEADS, HEAD, HEAD), lambda i: (i, 0, 0, 0))],
        out_shape=[_sds((s, D_HALF)), _sds((s, D_HALF), BF16), _sds((n_chunks, N_HEADS, HEAD, HEAD))],
        scratch_shapes=[pltpu.VMEM((N_HEADS, HEAD, HEAD), F32)],
        compiler_params=_params("arbitrary"),
    )(u, w, att, qd, kd, cd)


OUT_T = 512


def _out_fwd_bwd(x, y_pool, o, proj, target, w_out, dn_norm_w, final_norm_w):
    s = x.shape[0]
    t = OUT_T

    def body(x_ref, yp_ref, o_ref, z_ref, tg_ref, wo_ref, dnw_ref, fnw_ref,
             yt_ref, dh_ref, dyp_ref, do_ref, dz_ref, loss_ref, gfn_ref, gdn_ref, y_ref):
        @pl.when(pl.program_id(0) == 0)
        def _():
            loss_ref[...] = jnp.zeros_like(loss_ref)
            gfn_ref[...] = jnp.zeros_like(gfn_ref)
            gdn_ref[...] = jnp.zeros_like(gdn_ref)

        ypv = yp_ref[...]
        y_ref[:, :D_HALF] = ypv.astype(BF16)
        yt_ref[:D_HALF, :] = ypv.T.astype(BF16)
        dnw = dnw_ref[...]
        keep = []
        for h in HEADS:
            ov = o_ref[:, HEAD_COLS[h]]
            zv = z_ref[:, HEAD_COLS[h]]
            ro = lax.rsqrt(jnp.mean(ov * ov, axis=-1, keepdims=True) + EPS)
            ohat = ov * ro
            sg = _sigmoid(zv)
            keep.append((ro, ohat, zv, sg))
            ydn = ohat * dnw * (zv * sg)
            y_ref[:, D_HALF + h * HEAD:D_HALF + (h + 1) * HEAD] = ydn.astype(BF16)
            yt_ref[D_HALF + h * HEAD:D_HALF + (h + 1) * HEAD, :] = ydn.T.astype(BF16)

        hv = x_ref[...] + jnp.dot(y_ref[...], wo_ref[...], preferred_element_type=F32)
        r2 = lax.rsqrt(jnp.mean(hv * hv, axis=-1, keepdims=True) + EPS)
        hhat = hv * r2
        fnw = fnw_ref[...]
        err = hhat * fnw - tg_ref[...]
        loss_ref[...] += 0.5 * jnp.sum(_rowsum(err * err) * (1.0 / D_MODEL), axis=0, keepdims=True)
        dout = err * (1.0 / D_MODEL)
        gfn_ref[...] += _colsum(dout * hhat)
        dhh = dout * fnw
        dh = r2 * (dhh - hhat * jnp.mean(dhh * hhat, axis=-1, keepdims=True))
        dh_ref[...] = dh
        dy = _bdot_nt(dh, wo_ref[...])
        dyp_ref[...] = dy[:, :D_HALF]
        gdn = jnp.zeros((1, HEAD), F32)
        for h in HEADS:
            ro, ohat, zv, sg = keep[h]
            dyd = dy[:, D_HALF + h * HEAD:D_HALF + (h + 1) * HEAD]
            sz = zv * sg
            dz_ref[:, HEAD_COLS[h]] = (dyd * ohat * dnw * (sg * (1.0 + zv * (1.0 - sg)))).astype(BF16)
            gdn = gdn + _colsum(dyd * ohat * sz)
            doh = dyd * dnw * sz
            do_ref[:, HEAD_COLS[h]] = ro * (doh - ohat * jnp.mean(doh * ohat, axis=-1, keepdims=True))
        gdn_ref[...] += gdn

    wide = pl.BlockSpec((t, D_MODEL), lambda i: (i, 0))
    half = pl.BlockSpec((t, D_HALF), lambda i: (i, 0))
    const = lambda shape: pl.BlockSpec(shape, lambda i: (0,) * len(shape))
    return _call(
        body, name="out_fwd_bwd", grid=(s // t,),
        in_specs=[wide, half, half, pl.BlockSpec((t, D_HALF), lambda i: (i, 5)), wide,
                  const((D_MODEL, D_MODEL)), const((1, HEAD)), const((1, D_MODEL))],
        out_specs=[pl.BlockSpec((D_MODEL, t), lambda i: (0, i)), wide, half, half, half,
                   const((1, HEAD)), const((1, D_MODEL)), const((1, HEAD))],
        out_shape=[_sds((D_MODEL, s), BF16), _sds((s, D_MODEL)), _sds((s, D_HALF)), _sds((s, D_HALF)), _sds((s, D_HALF), BF16),
                   _sds((1, HEAD)), _sds((1, D_MODEL)), _sds((1, HEAD))],
        scratch_shapes=[pltpu.VMEM((t, D_MODEL), BF16)],
        compiler_params=_params("arbitrary"),
    )(x, y_pool, o, proj, target, w_out, dn_norm_w, final_norm_w)


def _token_matmul(name, at, pieces):
    m, s = at.shape
    n = len(pieces)
    tn, tk = D_HALF, 512

    def body(a_ref, *refs):
        p_refs, o_ref, acc = refs[:n], refs[n], refs[n + 1]

        @pl.when(pl.program_id(0) == 0)
        def _():
            acc[...] = jnp.zeros_like(acc)

        av = a_ref[...]
        for p in range(n):
            acc[:, p * tn:(p + 1) * tn] += _bdot(av, p_refs[p][...])

        @pl.when(pl.program_id(0) == pl.num_programs(0) - 1)
        def _():
            o_ref[...] = acc[...].astype(BF16)

    return _call(
        body, name=name, grid=(s // tk,),
        in_specs=[pl.BlockSpec((m, tk), lambda k: (0, k))]
                 + [pl.BlockSpec((tk, tn), functools.partial(lambda k, cb: (k, cb), cb=cb)) for _, cb in pieces],
        out_specs=pl.BlockSpec((m, n * tn), lambda k: (0, 0)),
        out_shape=_sds((m, n * tn), BF16),
        scratch_shapes=[pltpu.VMEM((m, n * tn), F32)],
        compiler_params=_params("arbitrary"),
    )(at, *[p[0] for p in pieces])


def _grad_w_in(at, pieces):
    m, s = at.shape
    n = len(pieces)
    tn, tk = D_HALF, 512

    def body(a_ref, *refs):
        p_refs, o_ref, acc = refs[:n], refs[n], refs[n + 1]

        @pl.when(pl.program_id(0) == 0)
        def _():
            acc[...] = jnp.zeros_like(acc)

        av = a_ref[...]
        for p in range(n):
            acc[:, p * tn:(p + 1) * tn] += _bdot(av, p_refs[p][...])

        @pl.when(pl.program_id(0) == pl.num_programs(0) - 1)
        def _():
            for j in range(4):
                base = j * BLK_IN // HEAD * HEAD
                win = acc[:, base:base + BLK_IN_PAD]
                if j * BLK_IN > base:
                    win = pltpu.roll(win, BLK_IN_PAD - (j * BLK_IN - base), 1)
                o_ref[j] = win.astype(BF16)

    return _call(
        body, name="grad_w_in", grid=(s // tk,),
        in_specs=[pl.BlockSpec((m, tk), lambda k: (0, k))] + [pl.BlockSpec((tk, tn), lambda k: (k, 0))] * n,
        out_specs=pl.BlockSpec((4, m, BLK_IN_PAD), lambda k: (0, 0, 0)),
        out_shape=_sds((4, m, BLK_IN_PAD), BF16),
        scratch_shapes=[pltpu.VMEM((m, n * tn), F32)],
        compiler_params=_params("arbitrary"),
    )(at, *pieces)


def _pool_bwd(proj, dyp, pool_w, pool_scale):
    s = proj.shape[0]
    t = POOL_T
    hb = t // HEAD
    last = s // HEAD - 1

    def body(u_ref, z_ref, halo_ref, dy_ref, zn_ref, dyn_ref, pw_ref, ps_ref, band_ref, aband_ref,
             du_ref, dz_ref, gpw_ref, gps_ref):
        i = pl.program_id(0)

        @pl.when(i == 0)
        def _():
            gpw_ref[...] = jnp.zeros_like(gpw_ref)
            gps_ref[...] = jnp.zeros_like(gps_ref)

        live = (i > 0).astype(F32)
        more = (i < pl.num_programs(0) - 1).astype(F32)
        groups = lambda ref: [ref[:, sl] for sl in HEAD_COLS]
        z, ps, dy = groups(z_ref), groups(ps_ref), groups(dy_ref)
        pw = [pw_ref[g] for g in HEADS]
        mix, mixed, sg, cnt = _pool_mix(groups(u_ref), [h * live for h in groups(halo_ref)], z, pw,
                                        [band_ref[g] for g in HEADS], i * t)
        sz = _each(lambda a, b: a * b, z, sg)
        for sl, d, m, p, s_, zg in zip(HEAD_COLS, dy, mixed, ps, sg, z):
            dz_ref[:, sl] = (d * m * p * (s_ * (1.0 + zg * (1.0 - s_)))).astype(BF16)
        for sl, d, m, a in zip(HEAD_COLS, dy, mixed, sz):
            gps_ref[:, sl] += _colsum(d * m * a)
        dmixed = _each(lambda d, p, a: d * p * a, dy, ps, sz)
        for g, gp in enumerate(_each(_bdot_tn, mix, dmixed)):
            gpw_ref[g] += gp
        dmix = _each(_bdot_nt, dmixed, pw)
        dmix_n = _each(lambda d, p, zn, w_: _bdot_nt(d * more * p * (zn * _sigmoid(zn)), w_),
                       groups(dyn_ref), ps, groups(zn_ref), pw)
        scaled = [jnp.concatenate([a / c, b * (1.0 / w)], axis=0) for a, c, b, w in zip(dmix, cnt, dmix_n, WINDOWS)]
        du = _each(lambda b, s_, d: _mask_dot(b, s_) - d, [aband_ref[g] for g in HEADS], scaled, dmix)
        for sl, v in zip(HEAD_COLS, du):
            du_ref[:, sl] = v.astype(BF16)

    tile = lambda col: pl.BlockSpec((t, D_HALF), lambda i: (i, col))
    below = lambda col: pl.BlockSpec((HEAD, D_HALF), lambda i: (jnp.minimum((i + 1) * hb, last), col))
    return _call(
        body, name="pool_bwd", grid=(s // t,),
        in_specs=[tile(0), tile(1), pl.BlockSpec((HEAD, D_HALF), lambda i: (jnp.maximum(i * hb - 1, 0), 0)),
                  tile(0), below(1), below(0),
                  pl.BlockSpec((N_HEADS, HEAD, HEAD), lambda i: (0, 0, 0)), pl.BlockSpec((1, D_HALF), lambda i: (0, 0)),
                  pl.BlockSpec((N_HEADS, t, HEAD + t), lambda i: (0, 0, 0)),
                  pl.BlockSpec((N_HEADS, t, HEAD + t), lambda i: (0, 0, 0))],
        out_specs=[tile(0), tile(0), pl.BlockSpec((N_HEADS, HEAD, HEAD), lambda i: (0, 0, 0)),
                   pl.BlockSpec((1, D_HALF), lambda i: (0, 0))],
        out_shape=[_sds((s, D_HALF), BF16), _sds((s, D_HALF), BF16), _sds((N_HEADS, HEAD, HEAD)), _sds((1, D_HALF))],
        compiler_params=_params("arbitrary"),
    )(proj, proj, proj, dyp, proj, dyp, pool_w, pool_scale, _pool_bands(t), _pool_bands(t, anti=True))


def _scan_bwd(do, vn, qd, kd, w, att, cd, st):
    s = do.shape[0]
    n_steps = s // SCAN_ROWS

    def body(do_ref, vn_ref, qd_ref, kd_ref, w_ref, att_ref, cd_ref, st_ref,
             du_ref, dw_ref, datt_ref, dqd_ref, dkd_ref, dcd_ref, dstate):
        @pl.when(pl.program_id(0) == 0)
        def _():
            dstate[...] = jnp.zeros_like(dstate)
        _, incl, _, _ = _pair_masks()
        cols = list(enumerate(HEAD_COLS))
        dv_intra = []
        for pp in range(SCAN_PAIRS):
            rp = slice(pp * PAIR, (pp + 1) * PAIR)
            dv_intra.append([_bdot_tn(att_ref[rp, sl], do_ref[rp, sl]) for _, sl in cols])
            for _, sl in cols:
                datt_ref[rp, sl] = jnp.where(incl, _bdot_nt(do_ref[rp, sl], vn_ref[rp, sl]), 0.0)
        ds = [dstate[h] for h in HEADS]
        for ci in range(2 * SCAN_PAIRS - 1, -1, -1):
            rs = slice(ci * CHUNK, (ci + 1) * CHUNK)
            in_pair = slice((ci % 2) * CHUNK, (ci % 2 + 1) * CHUNK)
            sm = [st_ref[ci, h] for h in HEADS]
            dvn = [dv_intra[ci // 2][h][in_pair] + _bdot(kd_ref[rs, sl], ds[h]) for h, sl in cols]
            for h, sl in cols:
                du_ref[rs, sl] = dvn[h].astype(BF16)
            dqd = [_bdot_nt(do_ref[rs, sl], sm[h]) for h, sl in cols]
            dw = [-_bdot_nt(dvn[h], sm[h]) for h, _ in cols]
            dkd = [_bdot_nt(vn_ref[rs, sl], ds[h]) for h, sl in cols]
            dcd = [jnp.broadcast_to(_rowsum(_colsum(ds[h] * sm[h])), (8, HEAD)) for h in HEADS]
            for h, sl in cols:
                dqd_ref[rs, sl] = dqd[h]
                dw_ref[rs, sl] = dw[h].astype(BF16)
                dkd_ref[rs, sl] = dkd[h]
                dcd_ref[ci * 8:(ci + 1) * 8, sl] = dcd[h]
            ds = [ds[h] * cd_ref[ci * 8:ci * 8 + 1, sl] + _bdot_tn(qd_ref[rs, sl], do_ref[rs, sl])
                  - _bdot_tn(w_ref[rs, sl], dvn[h]) for h, sl in cols]
        for h in HEADS:
            dstate[h] = ds[h]

    rev = pl.BlockSpec((SCAN_ROWS, D_HALF), lambda i: (n_steps - 1 - i, 0))
    rev_scalar = _chunk_scalar_spec(SCAN_PAIRS, lambda i: (n_steps - 1 - i, 0))
    return _call(
        body, name="scan_bwd", grid=(n_steps,),
        in_specs=[rev] * 6 + [rev_scalar,
                              pl.BlockSpec((2 * SCAN_PAIRS, N_HEADS, HEAD, HEAD), lambda i: (n_steps - 1 - i, 0, 0, 0))],
        out_specs=[rev] * 5 + [rev_scalar],
        out_shape=[_sds((s, D_HALF), BF16)] * 2 + [_sds((s, D_HALF))] * 3 + [_sds((s // 8, D_HALF))],
        scratch_shapes=[pltpu.VMEM((N_HEADS, HEAD, HEAD), F32)],
        compiler_params=_params("arbitrary"),
    )(do, vn, qd, kd, w, att, cd, st)


def _intra_bwd(qn, kn, vs, beta, g, tm, du, dw, datt, dqd, dkd, dcd):
    s = qn.shape[0]

    def body(qn_ref, kn_ref, vs_ref, beta_ref, g_ref, t_ref, du_ref, dw_ref, datt_ref, dqd_ref, dkd_ref, dcd_ref,
             dqn_ref, dkn_ref, dvs_ref, dbeta_ref, dg_ref):
        ones = jnp.ones((PAIR, HEAD), BF16)
        tn = (((0,), (0,)), ((), ()))
        kn, vs, beta = _heads(kn_ref), _heads(vs_ref), _heads(beta_ref)
        cm = _pair_common(_heads(qn_ref), kn, vs, beta, _heads(g_ref))
        tmv, duv, dwv, dattv, dqdv, dkdv = (_heads(r) for r in (t_ref, du_ref, dw_ref, datt_ref, dqd_ref, dkd_ref))
        dvb = _each(_bdot_tn, tmv, duv)
        dt = _each(lambda a, b, c, d: _bdot_nt(a, b) + _bdot_nt(c, d), duv, cm["vb"], dwv, cm["kbg"])
        dkbg = _each(_bdot_tn, tmv, dwv)
        m1 = _each(_bdot_tn, tmv, dt)
        da = _each(lambda a, b: -jnp.where(cm["strict"], _bdot_nt(a, b), 0.0), m1, tmv)
        dkk = _each(lambda a, b: a * b, da, cm["decay"])
        dqk = _each(lambda a, b: a * b, dattv, cm["decay"])
        dd = _each(lambda a, b, c, d: a * b + c * d, dkk, cm["kk"], dqk, cm["qk"])
        dkb = _each(lambda a, b, c, d: _bdot(a, b) + c * d, dkk, kn, dkbg, cm["egc"])
        dq = _each(lambda a, b, c, d: _bdot(a, b) + c * d, dqk, kn, dqdv, cm["egc"])
        dkn = _each(lambda a, b, c, d: _bdot_tn(a, b) + _bdot_tn(c, d), dkk, cm["kb"], dqk, cm["q"])
        dkn = _each(lambda a, b, c, d, e: a + b * c + d * e, dkn, dkdv, cm["ekd"], dkb, beta)
        t_kd = _each(lambda a, b, c: _rowsum(a * b * c), dkdv, kn, cm["ekd"])
        split = _each(_split, dd)
        rows_dd = [jnp.dot(hi, ones, preferred_element_type=F32) + jnp.dot(lo, ones, preferred_element_type=F32)
                   for hi, lo in split]
        cols_dd = [lax.dot_general(hi, ones, tn, preferred_element_type=F32)
                   + lax.dot_general(lo, ones, tn, preferred_element_type=F32) for hi, lo in split]
        dgc = _each(lambda r, c, a, b, e, f, k, t: r - c + _rowsum(a * b * e) + _rowsum(f * k) - t,
                    rows_dd, cols_dd, dqdv, cm["q"], cm["egc"], dkbg, cm["kbg"], t_kd)
        same_b = cm["same"].astype(BF16)
        rowi = lax.broadcasted_iota(I32, (PAIR, HEAD), 0)
        dcd = _each(lambda d: jnp.where(rowi < CHUNK, d[0:1], d[8:9]), _heads(dcd_ref, rows=16))
        dgl = _each(lambda t, d, c: _mask_dot(same_b, jnp.broadcast_to(t, (PAIR, HEAD))) + d * c, t_kd, dcd, cm["cd"])
        is_last = jnp.bitwise_and(rowi, CHUNK - 1) == CHUNK - 1
        dgc = _each(lambda a, b: a + jnp.where(is_last, b, 0.0), dgc, dgl)
        r = lax.broadcasted_iota(I32, (PAIR, PAIR), 0)
        c = lax.broadcasted_iota(I32, (PAIR, PAIR), 1)
        upper_b = (cm["same"] & (r <= c)).astype(BF16)
        _put_heads(dg_ref, _each(lambda v: _mask_dot(upper_b, v), dgc))
        _put_heads(dbeta_ref, _each(lambda a, b, c, d: jnp.broadcast_to(_rowsum(a * b) + _rowsum(c * d), (PAIR, HEAD)),
                                    dkb, kn, dvb, vs))
        _put_heads(dqn_ref, _each(lambda v: v * QK_SCALE, dq))
        _put_heads(dkn_ref, dkn)
        _put_heads(dvs_ref, _each(lambda a, b: a * b, dvb, beta))

    return _call(
        body, name="intra_bwd", grid=(s // (INTRA_PAIRS * PAIR),),
        in_specs=[_pair_spec()] * 11 + [_chunk_scalar_spec(INTRA_PAIRS)], out_specs=[_pair_spec()] * 5,
        out_shape=[_sds((s, D_HALF))] * 5,
        compiler_params=_params("arbitrary"),
    )(qn, kn, vs, beta, g, tm, du, dw, datt, dqd, dkd, dcd)


def _rows8(x):
    acc = x[0:8]
    for r in range(8, x.shape[0], 8):
        acc = acc + x[r:r + 8]
    return acc


def _conv_bwd(proj, conv_w, a_log, dt_bias, dqn, dkn, dvs, dbeta, dg):
    s = proj.shape[0]
    t = CONV_T
    n_tiles = s // t
    n_sub = t // CONV_SUB
    tile_of = lambda i: n_tiles - 1 - i

    def body(q_ref, k_ref, v_ref, hq_ref, hk_ref, hv_ref, ba_ref, cw_ref, al_ref, dtb_ref,
             dqn_ref, dkn_ref, dvs_ref, dbeta_ref, dg_ref, oq_ref, ok_ref, ov_ref, dba_ref, gcw_out, gsm_out,
             below, gcw_ref, gsm_ref):
        @pl.when(pl.program_id(0) == 0)
        def _():
            gcw_ref[...] = jnp.zeros_like(gcw_ref)
            gsm_ref[...] = jnp.zeros_like(gsm_ref)
            below[...] = jnp.zeros_like(below)

        live = (pl.program_id(0) < n_tiles - 1).astype(F32)
        parts = ((q_ref, hq_ref, dqn_ref, oq_ref), (k_ref, hk_ref, dkn_ref, ok_ref), (v_ref, hv_ref, dvs_ref, ov_ref))
        lane = lax.broadcasted_iota(I32, (CONV_SUB, HEAD), 1)
        lane8 = lax.broadcasted_iota(I32, (8, HEAD), 1)

        def sub_tile(r0, first):
            rows = pl.ds(r0, CONV_SUB)
            for p, (x_ref, h_ref, d_ref, o_ref) in enumerate(parts):
                for h in HEADS:
                    cs = HEAD_COLS[h]
                    wide = slice(p * D_HALF + h * HEAD, p * D_HALF + (h + 1) * HEAD)
                    cw = cw_ref[:, wide]
                    prev8 = h_ref[:, cs] * live if first else x_ref[pl.ds(r0 - 8, 8), cs]
                    taps = _conv_taps(x_ref[rows, cs], prev8)
                    y = _conv_pre(taps, cw)
                    sg = _sigmoid(y)
                    sv = y * sg
                    ds = d_ref[rows, cs]
                    if p < 2:
                        rn = lax.rsqrt(_rowsum(sv * sv) + EPS)
                        nrm = sv * rn
                        ds = rn * (ds - nrm * _rowsum(ds * nrm))
                    dy = ds * (sg * (1.0 + y * (1.0 - sg)))
                    for j in range(CONV_K):
                        gcw_ref[8 * j:8 * j + 8, wide] += _rows8(dy * taps[j])
                    nxt = below[:, wide]
                    acc = dy * cw[CONV_K - 1:CONV_K]
                    for sft in range(1, CONV_K):
                        acc = acc + _shift_up(dy, nxt, sft) * cw[CONV_K - 1 - sft:CONV_K - sft]
                    o_ref[rows, cs] = acc.astype(BF16)
                    below[:, wide] = dy[0:8]

            ba = ba_ref[rows, :]
            dba = jnp.zeros((CONV_SUB, HEAD), F32)
            gsm = jnp.zeros((8, HEAD), F32)
            for h in HEADS:
                beta = _sigmoid(ba[:, h:h + 1])
                dbeta = dbeta_ref[rows, h * HEAD:h * HEAD + 1]
                xg = ba[:, N_HEADS + h:N_HEADS + h + 1] + dtb_ref[0:1, h:h + 1]
                nexp = -jnp.exp(al_ref[0:1, h:h + 1])
                dgv = dg_ref[rows, h * HEAD:h * HEAD + 1]
                da = dgv * nexp * _sigmoid(xg)
                dba = dba + jnp.where(lane == h, dbeta * beta * (1.0 - beta), 0.0) + jnp.where(lane == N_HEADS + h, da, 0.0)
                gsm = (gsm + jnp.where(lane8 == h, _rows8(dgv * nexp * _softplus(xg)), 0.0)
                       + jnp.where(lane8 == N_HEADS + h, _rows8(da), 0.0))
            dba_ref[rows, :] = jnp.zeros((CONV_SUB, D_HALF), BF16)
            dba_ref[rows, :HEAD] = dba.astype(BF16)
            gsm_ref[...] += gsm

        def step(k, carry):
            sub_tile(pl.multiple_of((n_sub - 1 - k) * CONV_SUB, CONV_SUB), False)
            return carry

        lax.fori_loop(0, n_sub - 1, step, 0)
        sub_tile(0, True)

        @pl.when(pl.program_id(0) == n_tiles - 1)
        def _():
            gcw_out[...] = jnp.zeros_like(gcw_out)
            for j in range(CONV_K):
                gcw_out[j:j + 1, :] = _colsum(gcw_ref[8 * j:8 * j + 8, :])
            gsm_out[...] = jnp.broadcast_to(_colsum(gsm_ref[...]), (8, HEAD))

    row = pl.BlockSpec((t, D_HALF), lambda i: (tile_of(i), 0))
    const = lambda shape: pl.BlockSpec(shape, lambda i: (0, 0))
    return _call(
        body, name="conv_bwd", grid=(n_tiles,),
        in_specs=_conv_specs(t, tile_of) + [pl.BlockSpec((t, HEAD), lambda i: (tile_of(i), COL_BA // HEAD)),
                                            const((CONV_K, 3 * D_HALF)), const((1, N_HEADS)), const((1, N_HEADS))] + [row] * 5,
        out_specs=[row, row, row, row, const((8, 3 * D_HALF)), const((8, HEAD))],
        out_shape=[_sds((s, D_HALF), BF16)] * 4 + [_sds((8, 3 * D_HALF)), _sds((8, HEAD))],
        scratch_shapes=[pltpu.VMEM((8, 3 * D_HALF), F32), pltpu.VMEM((8 * CONV_K, 3 * D_HALF), F32),
                        pltpu.VMEM((8, HEAD), F32)],
        compiler_params=_params("arbitrary"),
    )(proj, proj, proj, proj, proj, proj, proj, conv_w, a_log, dt_bias, dqn, dkn, dvs, dbeta, dg)


def _conv_bwd_pre(proj, conv_w, a_log, dt_bias, dqn, dkn, dvs, dbeta, dg):
    s = proj.shape[0]
    t = CONV_T

    def body(q_ref, k_ref, v_ref, hq_ref, hk_ref, hv_ref, ba_ref, cw_ref, al_ref, dtb_ref,
             dqn_ref, dkn_ref, dvs_ref, dbeta_ref, dg_ref, dyq_ref, dyk_ref, dyv_ref, dba_ref, gcw_ref, gsm_ref):
        @pl.when(pl.program_id(0) == 0)
        def _():
            gcw_ref[...] = jnp.zeros_like(gcw_ref)
            gsm_ref[...] = jnp.zeros_like(gsm_ref)

        live = (pl.program_id(0) > 0).astype(F32)
        parts = ((q_ref, hq_ref, dqn_ref, dyq_ref), (k_ref, hk_ref, dkn_ref, dyk_ref), (v_ref, hv_ref, dvs_ref, dyv_ref))
        for p, (x_ref, h_ref, d_ref, dy_ref) in enumerate(parts):
            cols = slice(p * D_HALF, (p + 1) * D_HALF)
            taps = _conv_taps(x_ref[...], h_ref[...] * live)
            y = _conv_pre(taps, cw_ref[:, cols])
            sg = _sigmoid(y)
            sv = y * sg
            if p == 2:
                ds = d_ref[...]
            else:
                segs = []
                for h in HEADS:
                    seg = _head(sv, h)
                    rn = lax.rsqrt(_rowsum(seg * seg) + EPS)
                    nrm = seg * rn
                    dn = d_ref[:, HEAD_COLS[h]]
                    segs.append(rn * (dn - nrm * _rowsum(dn * nrm)))
                ds = jnp.concatenate(segs, axis=1)
            dy = ds * (sg * (1.0 + y * (1.0 - sg)))
            dy_ref[...] = dy
            for j in range(CONV_K):
                gcw_ref[j:j + 1, cols] += _colsum(dy * taps[j])

        ba = ba_ref[...]
        lane = lax.broadcasted_iota(I32, (t, HEAD), 1)
        lane1 = lax.broadcasted_iota(I32, (1, HEAD), 1)
        dba = jnp.zeros((t, HEAD), F32)
        gsm = jnp.zeros((1, HEAD), F32)
        for h in HEADS:
            beta = _sigmoid(ba[:, h:h + 1])
            dbeta = dbeta_ref[:, h * HEAD:h * HEAD + 1]
            xg = ba[:, N_HEADS + h:N_HEADS + h + 1] + dtb_ref[0:1, h:h + 1]
            nexp = -jnp.exp(al_ref[0:1, h:h + 1])
            dgv = dg_ref[:, h * HEAD:h * HEAD + 1]
            da = dgv * nexp * _sigmoid(xg)
            dba = dba + jnp.where(lane == h, dbeta * beta * (1.0 - beta), 0.0) + jnp.where(lane == N_HEADS + h, da, 0.0)
            gsm = (gsm + jnp.where(lane1 == h, _colsum(dgv * nexp * _softplus(xg)), 0.0)
                   + jnp.where(lane1 == N_HEADS + h, _colsum(da), 0.0))
        dba_ref[...] = jnp.zeros_like(dba_ref)
        dba_ref[:, :HEAD] = dba.astype(BF16)
        gsm_ref[0:1, :] += gsm

    row = pl.BlockSpec((t, D_HALF), lambda i: (i, 0))
    return _call(
        body, name="conv_bwd_pre", grid=(s // t,),
        in_specs=_conv_specs(t) + [pl.BlockSpec((t, HEAD), lambda i: (i, COL_BA // HEAD)),
                                   pl.BlockSpec((CONV_K, 3 * D_HALF), lambda i: (0, 0)),
                                   pl.BlockSpec((1, N_HEADS), lambda i: (0, 0)),
                                   pl.BlockSpec((1, N_HEADS), lambda i: (0, 0))] + [row] * 5,
        out_specs=[row, row, row, row,
                   pl.BlockSpec((8, 3 * D_HALF), lambda i: (0, 0)), pl.BlockSpec((8, HEAD), lambda i: (0, 0))],
        out_shape=[_sds((s, D_HALF))] * 3 + [_sds((s, D_HALF), BF16), _sds((8, 3 * D_HALF)), _sds((8, HEAD))],
        compiler_params=_params("arbitrary"),
    )(proj, proj, proj, proj, proj, proj, proj, conv_w, a_log, dt_bias, dqn, dkn, dvs, dbeta, dg)


def _conv_bwd_in(dyq, dyk, dyv, conv_w):
    s = dyq.shape[0]
    t = CONV_T
    last = s // 8 - 1

    def body(q_ref, k_ref, v_ref, nq_ref, nk_ref, nv_ref, cw_ref, oq_ref, ok_ref, ov_ref):
        more = (pl.program_id(0) < pl.num_programs(0) - 1).astype(F32)
        for p, (d_ref, n_ref, o_ref) in enumerate(((q_ref, nq_ref, oq_ref), (k_ref, nk_ref, ok_ref), (v_ref, nv_ref, ov_ref))):
            cw = cw_ref[:, p * D_HALF:(p + 1) * D_HALF]
            dy = d_ref[...]
            nxt = n_ref[...] * more
            acc = dy * cw[3:4]
            for sft in (1, 2, 3):
                acc = acc + _shift_up(dy, nxt, sft) * cw[3 - sft:4 - sft]
            o_ref[...] = acc.astype(BF16)

    row = pl.BlockSpec((t, D_HALF), lambda i: (i, 0))
    nxt = pl.BlockSpec((8, D_HALF), lambda i: (jnp.minimum((i + 1) * (t // 8), last), 0))
    return _call(
        body, name="conv_bwd_in", grid=(s // t,),
        in_specs=[row] * 3 + [nxt] * 3 + [pl.BlockSpec((CONV_K, 3 * D_HALF), lambda i: (0, 0))],
        out_specs=[row] * 3, out_shape=[_sds((s, D_HALF), BF16)] * 3,
        compiler_params=_params("arbitrary"),
    )(dyq, dyk, dyv, dyq, dyk, dyv, conv_w)


IN_T = 512


def _in_bwd(x, dh, norm_w, w_pad, pieces):
    s = x.shape[0]
    t = IN_T
    widths = [D_HALF] * 6 + [N_IN_PAD - COL_BA]

    def body(*refs):
        x_ref, dh_ref, nw_ref, w_ref = refs[:4]
        p_refs = refs[4:4 + len(pieces)]
        gx_ref, gnw_ref = refs[4 + len(pieces):]

        @pl.when(pl.program_id(0) == 0)
        def _():
            gnw_ref[...] = jnp.zeros_like(gnw_ref)

        dn = jnp.zeros((t, D_MODEL), F32)
        col = 0
        for p_ref, wd in zip(p_refs, widths):
            dn = dn + _bdot_nt(p_ref[...], w_ref[:, col:col + wd])
            col += wd
        xv = x_ref[...]
        r = lax.rsqrt(jnp.mean(xv * xv, axis=-1, keepdims=True) + EPS)
        xhat = xv * r
        gnw_ref[...] += _colsum(dn * xhat)
        dxh = dn * nw_ref[...]
        gx_ref[...] = dh_ref[...] + r * (dxh - xhat * jnp.mean(dxh * xhat, axis=-1, keepdims=True))

    wide = pl.BlockSpec((t, D_MODEL), lambda i: (i, 0))
    return _call(
        body, name="in_bwd", grid=(s // t,),
        in_specs=[wide, wide, pl.BlockSpec((1, D_MODEL), lambda i: (0, 0)),
                  pl.BlockSpec((D_MODEL, N_IN_PAD), lambda i: (0, 0))]
                 + [pl.BlockSpec((t, wd), lambda i: (i, 0)) for wd in widths],
        out_specs=[wide, pl.BlockSpec((1, D_MODEL), lambda i: (0, 0))],
        out_shape=[_sds((s, D_MODEL)), _sds((1, D_MODEL))],
        compiler_params=_params("arbitrary"),
    )(x, dh, norm_w, w_pad, *pieces)


def _adamw_shard(name, w, g_own, g_got, cidx, m, v):
    _, r, c = w.shape
    half = r // 2
    rows = 256 if half % 256 == 0 else half
    per_half = half // rows

    def body(c_ref, w_ref, go_ref, gg_ref, m_ref, v_ref, gout_ref, d_ref, nm_ref, nv_ref):
        mine = (pl.program_id(0) // per_half) == c_ref[0]
        gv = jnp.where(mine, go_ref[:, :c], gg_ref[:, :c])
        gout_ref[0] = gv
        mn = ADAM_B1 * m_ref[0] + (1.0 - ADAM_B1) * gv
        vn = ADAM_B2 * v_ref[0] + (1.0 - ADAM_B2) * (gv * gv)
        m_hat = mn / (1.0 - ADAM_B1 ** ADAM_STEP)
        v_hat = vn / (1.0 - ADAM_B2 ** ADAM_STEP)
        d_ref[0] = -ADAM_LR * (m_hat / (jnp.sqrt(v_hat) + ADAM_EPS) + ADAM_WD * w_ref[0])
        nm_ref[0] = mn
        nv_ref[0] = vn

    blk = pl.BlockSpec((1, rows, c), lambda i, c_ref: (0, i, 0))
    gblk = pl.BlockSpec((rows, g_own.shape[1]), lambda i, c_ref: (i % per_half, 0))
    return _call(
        body, name=name,
        grid_spec=pltpu.PrefetchScalarGridSpec(
            num_scalar_prefetch=1, grid=(2 * per_half,),
            in_specs=[blk, gblk, gblk, blk, blk], out_specs=[blk] * 4),
        out_shape=[_sds((1, r, c))] * 4,
        compiler_params=_params("arbitrary"),
    )(cidx, w, g_own, g_got, m, v)


def _adamw_tiles(name, w, g, m, v):
    n = w.shape[0]
    nb = 77 if n % 77 == 0 else n

    def body(w_ref, g_ref, m_ref, v_ref, d_ref, nm_ref, nv_ref):
        gv = g_ref[...]
        mn = ADAM_B1 * m_ref[...] + (1.0 - ADAM_B1) * gv
        vn = ADAM_B2 * v_ref[...] + (1.0 - ADAM_B2) * (gv * gv)
        m_hat = mn / (1.0 - ADAM_B1 ** ADAM_STEP)
        v_hat = vn / (1.0 - ADAM_B2 ** ADAM_STEP)
        d_ref[...] = -ADAM_LR * (m_hat / (jnp.sqrt(v_hat) + ADAM_EPS) + ADAM_WD * w_ref[...])
        nm_ref[...] = mn
        nv_ref[...] = vn

    blk = pl.BlockSpec((nb, 8, HEAD), lambda i: (i, 0, 0))
    return _call(
        body, name=name, grid=(n // nb,),
        in_specs=[blk] * 4, out_specs=[blk] * 3, out_shape=[_sds(w.shape)] * 3,
        compiler_params=_params("arbitrary"),
    )(w, g, m, v)


def _exchange(name, inputs, out_shapes, phases):
    n_in = len(inputs)
    n_out = len(out_shapes)
    n_cp = sum(len(p) for p in phases)

    def body(*refs):
        ins, outs = refs[:n_in], refs[n_in:n_in + n_out]
        send, recv = refs[n_in + n_out:]
        pos = (lax.axis_index("x"), lax.axis_index("y"), lax.axis_index("c"))
        k = 0
        for phase in phases:
            cps = []
            for src, dst, target in phase:
                cps.append(pltpu.make_async_remote_copy(
                    src_ref=src(ins, outs, pos), dst_ref=dst(ins, outs, pos), send_sem=send.at[k], recv_sem=recv.at[k],
                    device_id=target(pos), device_id_type=pl.DeviceIdType.MESH))
                k += 1
            for cp in cps:
                cp.start()
            for cp in cps:
                cp.wait()

    anyspec = pl.BlockSpec(memory_space=pl.ANY)
    return _call(
        body, name=name,
        in_specs=[anyspec] * n_in, out_specs=[anyspec] * n_out, out_shape=list(out_shapes),
        scratch_shapes=[pltpu.SemaphoreType.DMA((n_cp,)), pltpu.SemaphoreType.DMA((n_cp,))],
    )(*inputs)


def _exchange_start(name, inputs, out_shapes, copies):
    n_in, n_out, n_cp = len(inputs), len(out_shapes), len(copies)

    def body(*refs):
        ins, lands = refs[:n_in], refs[n_in:n_in + n_out]
        sems = refs[n_in + n_out:n_in + n_out + 2 * n_cp]
        token = refs[-1]
        pos = (lax.axis_index("x"), lax.axis_index("y"), lax.axis_index("c"))
        for k, (src, dst, target) in enumerate(copies):
            pltpu.make_async_remote_copy(
                src_ref=src(ins, lands, pos), dst_ref=dst(ins, lands, pos), send_sem=sems[2 * k], recv_sem=sems[2 * k + 1],
                device_id=target(pos), device_id_type=pl.DeviceIdType.MESH).start()
        token[...] = jnp.zeros_like(token)

    hbm = pl.BlockSpec(memory_space=pltpu.HBM)
    sem = pl.BlockSpec(memory_space=pltpu.SEMAPHORE)
    bufs = list(inputs) + [lax.empty(o.shape, o.dtype) for o in out_shapes]
    outs = _call(
        body, name=name,
        out_shape=tuple([pltpu.SemaphoreType.DMA(())] * (2 * n_cp) + [pltpu.HBM(b.shape, b.dtype) for b in bufs]
                        + [_sds((8, HEAD))]),
        in_specs=[hbm] * len(bufs),
        out_specs=tuple([sem] * (2 * n_cp) + [hbm] * len(bufs) + [pl.BlockSpec(memory_space=pltpu.VMEM)]),
        input_output_aliases={i: 2 * n_cp + i for i in range(len(bufs))},
        compiler_params=pltpu.CompilerParams(has_side_effects=pltpu.SideEffectType.DATAFLOW_SIDE_EFFECTING),
    )(*[pltpu.with_memory_space_constraint(b, pltpu.HBM) for b in bufs])
    return outs[:2 * n_cp], outs[2 * n_cp:2 * n_cp + n_in], outs[2 * n_cp + n_in:-1], outs[-1]


def _exchange_wait(name, sems, sources, lands, copies, after):
    n_in, n_out, n_cp = len(sources), len(lands), len(copies)

    def body(*refs):
        ins, zones = refs[:n_in], refs[n_in:n_in + n_out]
        sem_refs = refs[n_in + n_out:n_in + n_out + 2 * n_cp]
        pos = (lax.axis_index("x"), lax.axis_index("y"), lax.axis_index("c"))
        for k, (src, dst, target) in enumerate(copies):
            cp = pltpu.make_async_remote_copy(
                src_ref=src(ins, zones, pos), dst_ref=dst(ins, zones, pos), send_sem=sem_refs[2 * k],
                recv_sem=sem_refs[2 * k + 1], device_id=target(pos), device_id_type=pl.DeviceIdType.MESH)
            cp.wait_send()
            cp.wait_recv()

    hbm = pl.BlockSpec(memory_space=pltpu.HBM)
    sem = pl.BlockSpec(memory_space=pltpu.SEMAPHORE)
    bufs = list(sources) + list(lands)
    outs = _call(
        body, name=name,
        out_shape=tuple(pltpu.HBM(b.shape, b.dtype) for b in bufs),
        in_specs=[hbm] * len(bufs) + [sem] * (2 * n_cp) + [pl.BlockSpec(memory_space=pl.ANY)],
        out_specs=tuple([hbm] * len(bufs)),
        input_output_aliases={i: i for i in range(len(bufs))},
        compiler_params=pltpu.CompilerParams(has_side_effects=pltpu.SideEffectType.DATAFLOW_SIDE_EFFECTING),
    )(*bufs, *sems, after)
    return outs[:n_in], outs[n_in:]


def _allreduce_tile(name, v):
    def body(v_ref, out_ref, slots, send, recv):
        x, y, c = lax.axis_index("x"), lax.axis_index("y"), lax.axis_index("c")
        me = 4 * x + 2 * y + c
        slots[me] = v_ref[...]
        cps = []
        for k in range(1, 8):
            peer = (x ^ (k >> 2), y ^ ((k >> 1) & 1), c ^ (k & 1))
            cps.append(pltpu.make_async_remote_copy(
                src_ref=v_ref, dst_ref=slots.at[me], send_sem=send.at[k - 1], recv_sem=recv.at[k - 1],
                device_id=peer, device_id_type=pl.DeviceIdType.MESH))
        for cp in cps:
            cp.start()
        for cp in cps:
            cp.wait()
        acc = slots[0]
        for i in range(1, 8):
            acc = acc + slots[i]
        out_ref[...] = acc

    vm = pl.BlockSpec(memory_space=pltpu.VMEM)
    return _call(
        body, name=name, in_specs=[vm], out_specs=vm, out_shape=_sds(v.shape),
        scratch_shapes=[pltpu.VMEM((8,) + v.shape, F32), pltpu.SemaphoreType.DMA((7,)), pltpu.SemaphoreType.DMA((7,))],
    )(v)


def _chip(pos):
    return 2 * pos[0] + pos[1]


def _other_chip(pos, mask):
    x, y, c = pos
    return (x ^ (mask >> 1), y ^ (mask & 1), c)


def _sibling(pos):
    return (pos[0], pos[1], 1 - pos[2])


def _gather_weights(wb, cb):
    rows = wb.shape[0] // 2
    x_nb, y_nb, diag = CHIP_MASKS

    def part(pos, mask, quarter=None):
        start = pos[2] * rows if quarter is None else pos[2] * rows + quarter * (rows // 2)
        return lambda outs: outs[0].at[_chip(pos) ^ mask, pl.ds(start, rows if quarter is None else rows // 2)]

    def passed_on(mask, to, quarter=None):
        return (lambda ins, outs, pos: part(pos, mask, quarter)(outs), lambda ins, outs, pos: part(pos, mask, quarter)(outs), to)

    first = [(lambda ins, outs, pos: ins[0].at[pl.ds(pos[2] * rows, rows)], lambda ins, outs, pos: part(pos, 0)(outs),
              functools.partial(_other_chip, mask=mask)) for mask in (x_nb, y_nb)]
    first += [(lambda ins, outs, pos: ins[1], lambda ins, outs, pos: outs[1].at[_chip(pos)],
               functools.partial(_other_chip, mask=mask)) for mask in CHIP_MASKS]
    second = [passed_on(x_nb, functools.partial(_other_chip, mask=y_nb), quarter=0),
              passed_on(y_nb, functools.partial(_other_chip, mask=x_nb), quarter=1),
              passed_on(x_nb, _sibling), passed_on(y_nb, _sibling)]
    third = [passed_on(diag, _sibling)]
    return _exchange("gather_weights", [wb, cb], [_sds((4,) + wb.shape, wb.dtype), _sds((4,) + cb.shape, cb.dtype)],
                     [first, second, third])


def _gather_blocks(ob):
    copies = [(lambda ins, outs, pos: ins[0], lambda ins, outs, pos: outs[0].at[_chip(pos)],
               functools.partial(_other_chip, mask=mask)) for mask in CHIP_MASKS]
    return [_sds((4,) + ob.shape, ob.dtype)], copies


def _to_sibling_half(name, arrays):
    def src(ins, outs, pos, a):
        h = arrays[a].shape[-2] // 2
        sl = pl.ds((1 - pos[2]) * h, h)
        return ins[a].at[:, sl] if arrays[a].ndim == 3 else ins[a].at[sl]

    outs = [_sds(a.shape[:-2] + (a.shape[-2] // 2, a.shape[-1]), a.dtype) for a in arrays]
    phase = [(functools.partial(src, a=a), lambda ins, outs, pos, a=a: outs[a], _sibling) for a in range(len(arrays))]
    return _exchange(name, arrays, outs, [phase])


def _add_half(name, full, part, cidx):
    shape = part.shape
    lead = shape[0] if len(shape) == 3 else 1
    rows, cols = shape[-2], shape[-1]
    tr = rows // 2 if rows % 16 == 0 else rows
    nr = rows // tr
    f3 = full.reshape((lead,) + full.shape[-2:])
    p3 = part.reshape((lead, rows, cols))

    def body(c_ref, f_ref, p_ref, o_ref):
        o_ref[...] = (f_ref[...].astype(F32) + p_ref[...].astype(F32)).astype(o_ref.dtype)

    out = _call(
        body, name=name,
        grid_spec=pltpu.PrefetchScalarGridSpec(
            num_scalar_prefetch=1, grid=(lead, nr),
            in_specs=[pl.BlockSpec((1, tr, cols), lambda b, r, c_ref: (b, c_ref[0] * nr + r, 0)),
                      pl.BlockSpec((1, tr, cols), lambda b, r, c_ref: (b, r, 0))],
            out_specs=pl.BlockSpec((1, tr, cols), lambda b, r, c_ref: (b, r, 0))),
        out_shape=_sds((lead, rows, cols), part.dtype),
        compiler_params=_params("arbitrary", "arbitrary"),
    )(cidx, f3, p3)
    return out.reshape(shape)


def _to_other_chips(arrays, blocked):
    def src(ins, outs, pos, a, mask):
        return ins[a].at[_chip(pos) ^ mask] if blocked[a] else ins[a]

    outs = [_sds((3,) + (a.shape[1:] if b else a.shape), a.dtype) for a, b in zip(arrays, blocked)]
    copies = []
    for mi, mask in enumerate(CHIP_MASKS):
        for a in range(len(arrays)):
            copies.append((functools.partial(src, a=a, mask=mask), lambda ins, outs, pos, a=a, mi=mi: outs[a].at[mi],
                           functools.partial(_other_chip, mask=mask)))
    return outs, copies


def _add_chips(name, own, got, jidx, blocked):
    rows, cols = got.shape[-2:]
    tr = rows // 2 if rows % 16 == 0 else rows
    nr = rows // tr
    o3 = own if blocked else own.reshape((1, rows, cols))

    def body(j_ref, o_ref, g_ref, out_ref):
        out_ref[...] = ((o_ref[0].astype(F32) + g_ref[0].astype(F32))
                        + (g_ref[1].astype(F32) + g_ref[2].astype(F32)))

    own_map = (lambda r, j_ref: (j_ref[0], r, 0)) if blocked else (lambda r, j_ref: (0, r, 0))
    return _call(
        body, name=name,
        grid_spec=pltpu.PrefetchScalarGridSpec(
            num_scalar_prefetch=1, grid=(nr,),
            in_specs=[pl.BlockSpec((1, tr, cols), own_map),
                      pl.BlockSpec((3, tr, cols), lambda r, j_ref: (0, r, 0))],
            out_specs=pl.BlockSpec((tr, cols), lambda r, j_ref: (r, 0))),
        out_shape=_sds((rows, cols)),
        compiler_params=_params("arbitrary"),
    )(jidx, o3, got)


def _to_sibling(name, arrays):
    phase = [(lambda ins, outs, pos, a=a: ins[a], lambda ins, outs, pos, a=a: outs[a], _sibling)
             for a in range(len(arrays))]
    return _exchange(name, arrays, [_sds(a.shape, a.dtype) for a in arrays], [phase])


def _local_step(x, target, w_pad, w_out, conv_w, norm_w, pool_w, pool_scale, a_log, dt_bias, dn_norm_w, final_norm_w):
    proj, n_t = _proj_fwd(x, norm_w, w_pad)
    y_pool = _pool_fwd(proj, pool_w, pool_scale)
    qn, kn, vs, beta, g = _conv_fwd(proj, conv_w, a_log, dt_bias)
    u, w, att, qd, kd, tm, cd = _intra_fwd(qn, kn, vs, beta, g)
    o, vn, st = _scan_fwd(u, w, att, qd, kd, cd)
    w_out = w_out(o) if callable(w_out) else w_out
    y_t, dh, dyp, do, ddz, loss, g_fnw, g_dnw = _out_fwd_bwd(x, y_pool, o, proj, target, w_out, dn_norm_w, final_norm_w)
    g_wout = _token_matmul("grad_w_out", y_t, [(dh, 0), (dh, 1)])
    dpu, dpz, g_pw, g_ps = _pool_bwd(proj, dyp, pool_w, pool_scale)
    du, dw, datt, dqd, dkd, dcd = _scan_bwd(do, vn, qd, kd, w, att, cd, st)
    dqn, dkn, dvs, dbeta, dg = _intra_bwd(qn, kn, vs, beta, g, tm, du, dw, datt, dqd, dkd, dcd)
    dcq, dck, dcv, dba, g_cw, g_sm = _conv_bwd(proj, conv_w, a_log, dt_bias, dqn, dkn, dvs, dbeta, dg)
    pieces = [dpu, dpz, dcq, dck, dcv, ddz, dba]
    g_win = _grad_w_in(n_t, pieces)
    small = dict(norm_w=jnp.zeros_like(norm_w), pool_w=g_pw, pool_scale=g_ps, conv_w=g_cw[:CONV_K],
                 a_log=g_sm[0:1, 0:N_HEADS], dt_bias=g_sm[0:1, N_HEADS:2 * N_HEADS], dn_norm_w=g_dnw, final_norm_w=g_fnw)
    return loss[0, 0], g_win, g_wout, small, dh, pieces


def _pack_small(t):
    lanes = lambda a: jnp.pad(a.reshape(1, -1), ((0, 0), (0, HEAD - a.size)))
    rows = [t["pool_w"].reshape(-1, HEAD), t["norm_w"].reshape(-1, HEAD), t["final_norm_w"].reshape(-1, HEAD),
            t["pool_scale"].reshape(-1, HEAD), t["conv_w"].reshape(-1, HEAD), t["dn_norm_w"].reshape(1, HEAD),
            lanes(t["a_log"]), lanes(t["dt_bias"]), lanes(t.get("loss", jnp.zeros((1,), F32)))]
    buf = jnp.concatenate(rows, axis=0)
    return jnp.pad(buf, ((0, SMALL_ROWS - buf.shape[0]), (0, 0)))


def _unpack_small(buf, conv_cols):
    out, r = {}, 0
    for name, nrow, shape in (("pool_w", 512, (1, N_HEADS, HEAD, HEAD)), ("norm_w", 8, (1, D_MODEL)),
                              ("final_norm_w", 8, (D_MODEL,)), ("pool_scale", 4, (1, D_HALF)),
                              ("conv_w", CONV_K * conv_cols // HEAD, (1, CONV_K, conv_cols)), ("dn_norm_w", 1, (1, HEAD))):
        out[name] = buf[r:r + nrow].reshape(shape)
        r += nrow
    out["a_log"] = buf[r:r + 1, :N_HEADS]
    out["dt_bias"] = buf[r + 1:r + 2, :N_HEADS]
    out["loss"] = buf[r + 2, 0]
    return out


def kernel(x, norm_w, w_in, pool_w, pool_scale, conv_w, a_log, dt_bias, dn_norm_w, w_out, final_norm_w, loss_target, m_norm_w, m_w_in, m_pool_w, m_pool_scale, m_conv_w, m_a_log, m_dt_bias, m_dn_norm_w, m_w_out, m_final_norm_w, v_norm_w, v_w_in, v_pool_w, v_pool_scale, v_conv_w, v_a_log, v_dt_bias, v_dn_norm_w, v_w_out, v_final_norm_w):
    cidx = lax.axis_index("c").astype(I32).reshape(1)
    jidx = (2 * lax.axis_index("x") + lax.axis_index("y")).astype(I32)

    wb = jnp.pad(w_in[0].astype(BF16), ((0, 0), (0, BLK_IN_PAD - BLK_IN)))
    ob = w_out[0].astype(BF16)
    gw, gc = _gather_weights(wb, conv_w[0])
    mine = lambda j: jidx == j
    w_pad = jnp.concatenate([jnp.where(mine(j), wb[:, :BLK_IN], gw[j, :, :BLK_IN]) for j in range(4)]
                            + [jnp.zeros((D_MODEL, N_IN_PAD - N_IN), BF16)], axis=1)
    cw_full = jnp.concatenate([jnp.where(mine(j), conv_w[0], gc[j]) for j in range(4)], axis=1)

    lands_o, copies_o = _gather_blocks(ob)
    sems_o, ob_thru, zones_o, token_o = _exchange_start("gather_w_out_start", [ob], lands_o, copies_o)

    def w_out_full(after):
        (own,), (got,) = _exchange_wait("gather_w_out_wait", sems_o, ob_thru, zones_o, copies_o, after)
        return jnp.where((jnp.arange(4) == jidx)[:, None, None], own[None], got).reshape(D_MODEL, D_MODEL)

    loss, g_win, g_wout, small, dh, pieces = _local_step(
        x[0], loss_target[0], w_pad, w_out_full, cw_full, norm_w + token_o[0, 0], pool_w[0], pool_scale, a_log, dt_bias,
        dn_norm_w, final_norm_w.reshape(1, D_MODEL))
    small["loss"] = loss

    blocks_out = g_wout.reshape(4, BLK_OUT, D_MODEL)
    full = [g_win, blocks_out, _pack_small(small)]
    from_sib = _to_sibling_half("reduce_sibling", full)
    chip_sum = [_add_half("add_sibling_%d" % i, f, p, cidx) for i, (f, p) in enumerate(zip(full, from_sib))]
    blocked = [True, True, False]
    lands, copies = _to_other_chips(chip_sum, blocked)
    sems, chip_sum, zones, token = _exchange_start("reduce_chips_start", chip_sum, lands, copies)
    gx, g_nw = _in_bwd(x[0], dh, norm_w + token[0, 0], w_pad, pieces)
    g_nw = _allreduce_tile("reduce_norm_w", g_nw.reshape(8, HEAD)).reshape(1, D_MODEL)
    chip_sum, from_chips = _exchange_wait("reduce_chips_wait", sems, chip_sum, zones, copies, gx)
    halves = [_add_chips("add_chips_%d" % i, o, g, jidx.reshape(1), b)
              for i, (o, g, b) in enumerate(zip(chip_sum, from_chips, blocked))]
    other_halves = _to_sibling("swap_halves", halves)

    weights = dict(norm_w=norm_w, w_in=w_in, pool_w=pool_w, pool_scale=pool_scale, conv_w=conv_w, a_log=a_log,
                   dt_bias=dt_bias, dn_norm_w=dn_norm_w, w_out=w_out, final_norm_w=final_norm_w)
    ms = dict(norm_w=m_norm_w, w_in=m_w_in, pool_w=m_pool_w, pool_scale=m_pool_scale, conv_w=m_conv_w, a_log=m_a_log,
              dt_bias=m_dt_bias, dn_norm_w=m_dn_norm_w, w_out=m_w_out, final_norm_w=m_final_norm_w)
    vs = dict(norm_w=v_norm_w, w_in=v_w_in, pool_w=v_pool_w, pool_scale=v_pool_scale, conv_w=v_conv_w, a_log=v_a_log,
              dt_bias=v_dt_bias, dn_norm_w=v_dn_norm_w, w_out=v_w_out, final_norm_w=v_final_norm_w)
    names = ["norm_w", "w_in", "pool_w", "pool_scale", "conv_w", "a_log", "dt_bias", "dn_norm_w", "w_out", "final_norm_w"]
    small_names = [n for n in names if n not in ("w_in", "w_out")]

    def pack(t):
        conv = lax.dynamic_update_slice_in_dim(jnp.zeros((CONV_K, 3 * D_HALF), F32), t["conv_w"][0], jidx * BLK_CONV, axis=1)
        return _pack_small({**{n: t[n] for n in small_names if n != "conv_w"}, "conv_w": conv})[None]

    results = [{}, {}, {}, {}]
    to_tiles = lambda a: jnp.transpose(a, (2, 0, 1)).reshape(BLK_IN, 8, HEAD)
    from_tiles = lambda a: jnp.transpose(a, (1, 2, 0)).reshape(1, D_MODEL, BLK_IN)
    lo = jnp.where(cidx[0] == 0, halves[0], other_halves[0])
    hi = jnp.where(cidx[0] == 0, other_halves[0], halves[0])
    g_tiles = jnp.concatenate([lo[:, :BLK_IN].T, hi[:, :BLK_IN].T], axis=1).reshape(BLK_IN, 8, HEAD)
    outs = _adamw_tiles("adamw_w_in", to_tiles(w_in), g_tiles, to_tiles(m_w_in), to_tiles(v_w_in))
    for res, o in zip(results, (g_tiles,) + tuple(outs)):
        res["w_in"] = from_tiles(o)
    outs = _adamw_shard("adamw_w_out", w_out, halves[1], other_halves[1], cidx, m_w_out, v_w_out)
    for res, o in zip(results, outs):
        res["w_out"] = o
    outs = _adamw_shard("adamw_small", pack(weights), halves[2], other_halves[2], cidx, pack(ms), pack(vs))
    for res, o in zip(results, outs):
        got = _unpack_small(o[0], 3 * D_HALF)
        got["conv_w"] = lax.dynamic_slice_in_dim(got["conv_w"], jidx * BLK_CONV, BLK_CONV, axis=2)
        res.update(got)
    one_tile = lambda a: a.reshape(1, 8, HEAD)
    outs = _adamw_tiles("adamw_norm_w", one_tile(norm_w), one_tile(g_nw), one_tile(m_norm_w), one_tile(v_norm_w))
    for res, o in zip(results, (g_nw,) + tuple(outs)):
        res["norm_w"] = o.reshape(1, D_MODEL)
    grads, delta, new_m, new_v = results

    return (grads["loss"], gx[None], *[grads[n] for n in names], *[delta[n] for n in names],
            *[new_m[n] for n in names], *[new_v[n] for n in names])
```

```python
import functools

import jax
import jax.numpy as jnp
import numpy as np
from jax import lax
from jax.experimental import pallas as pl
from jax.experimental.pallas import tpu as pltpu

F32 = jnp.float32
BF16 = jnp.bfloat16
I32 = jnp.int32

D_MODEL = 1024
D_HALF = 512
N_HEADS = 4
HEAD = 128
CHUNK = 64
PAIR = 2 * CHUNK
WINDOWS = (2, 4, 8, 16)
CONV_K = 4
EPS = 1e-6
N_IN = 3080
N_IN_PAD = 3200
BLK_IN = 770
BLK_IN_PAD = 896
BLK_OUT = 256
BLK_CONV = 384
COL_BA = 3072
QK_SCALE = HEAD ** -0.5
SMALL_ROWS = 592
VMEM_LIMIT = 56 * 1024 * 1024

ADAM_LR = 0.001
ADAM_B1 = 0.9
ADAM_B2 = 0.999
ADAM_EPS = 1e-08
ADAM_WD = 0.01
ADAM_STEP = 10

CHIP_MASKS = (2, 1, 3)
HEADS = range(N_HEADS)
HEAD_COLS = [slice(h * HEAD, (h + 1) * HEAD) for h in HEADS]


def _call(body, **kw):
    return pl.pallas_call(body, **kw)


def _params(*sem):
    return pltpu.CompilerParams(dimension_semantics=sem, vmem_limit_bytes=VMEM_LIMIT)


def _sds(shape, dtype=F32):
    return jax.ShapeDtypeStruct(shape, dtype)


def _bdot(a, b):
    return jnp.dot(a.astype(BF16), b.astype(BF16), preferred_element_type=F32)


def _bdot_nt(a, b):
    return lax.dot_general(a.astype(BF16), b.astype(BF16), (((1,), (1,)), ((), ())), preferred_element_type=F32)


def _bdot_tn(a, b):
    return lax.dot_general(a.astype(BF16), b.astype(BF16), (((0,), (0,)), ((), ())), preferred_element_type=F32)


def _split(a):
    hi = a.astype(BF16)
    lo = (a - hi.astype(F32)).astype(BF16)
    return hi, lo


def _mask_dot(m, b, dims=(((1,), (0,)), ((), ()))):
    bh, bl = _split(b)
    dg = functools.partial(lax.dot_general, dimension_numbers=dims, preferred_element_type=F32)
    return dg(m, bh) + dg(m, bl)


def _sigmoid(x):
    return 0.5 * jnp.tanh(0.5 * x) + 0.5


def _softplus(x):
    return jnp.maximum(x, 0.0) + jnp.log(1.0 + jnp.exp(-jnp.abs(x)))


def _rowsum(x):
    return jnp.sum(x, axis=-1, keepdims=True)


def _colsum(x):
    return jnp.sum(x, axis=0, keepdims=True)


def _shift_down(xv, prev8, k):
    r = pltpu.roll(xv, k, 0)
    q = pltpu.roll(prev8, k, 0)
    row = lax.broadcasted_iota(I32, prev8.shape, 0)
    top = jnp.where(row < k, q, r[0:8])
    return jnp.concatenate([top, r[8:]], axis=0)


def _shift_up(xv, next8, k):
    t = xv.shape[0]
    r = pltpu.roll(xv, t - k, 0)
    q = pltpu.roll(next8, 8 - k, 0)
    row = lax.broadcasted_iota(I32, next8.shape, 0)
    bot = jnp.where(row >= 8 - k, q, r[t - 8:])
    return jnp.concatenate([r[:t - 8], bot], axis=0)


def _band(rows, cols, off, w, anti=False):
    r = lax.broadcasted_iota(I32, (rows, cols), 0)
    c = lax.broadcasted_iota(I32, (rows, cols), 1)
    d = (c - r + off) if anti else (r - c + off)
    return ((d >= 0) & (d < w)).astype(BF16)


def _head(ref_or_val, h):
    return ref_or_val[:, h * HEAD:(h + 1) * HEAD]


INTRA_PAIRS = 2
UNITS = [(pp, h) for pp in range(INTRA_PAIRS) for h in HEADS]


def _heads(ref, rows=PAIR):
    return [ref[pp * rows:(pp + 1) * rows, HEAD_COLS[h]] for pp, h in UNITS]


def _put_heads(ref, vals, rows=PAIR):
    for (pp, h), v in zip(UNITS, vals):
        ref[pp * rows:(pp + 1) * rows, HEAD_COLS[h]] = v.astype(ref.dtype)


def _each(fn, *lists):
    return [fn(*args) for args in zip(*lists)]


def _proj_fwd(x, norm_w, w_pad):
    s = x.shape[0]
    tm = 512

    def body(x_ref, nw_ref, w_ref, proj_ref, nt_ref):
        xv = x_ref[...]
        r = lax.rsqrt(jnp.mean(xv * xv, axis=-1, keepdims=True) + EPS)
        nv = xv * r * nw_ref[...]
        nt_ref[...] = nv.T.astype(BF16)
        proj_ref[...] = jnp.dot(nv.astype(BF16), w_ref[...], preferred_element_type=F32)

    return _call(
        body, name="proj_fwd", grid=(s // tm,),
        in_specs=[pl.BlockSpec((tm, D_MODEL), lambda i: (i, 0)),
                  pl.BlockSpec((1, D_MODEL), lambda i: (0, 0)),
                  pl.BlockSpec((D_MODEL, N_IN_PAD), lambda i: (0, 0))],
        out_specs=[pl.BlockSpec((tm, N_IN_PAD), lambda i: (i, 0)),
                   pl.BlockSpec((D_MODEL, tm), lambda i: (0, i))],
        out_shape=[_sds((s, N_IN_PAD)), _sds((D_MODEL, s), BF16)],
        compiler_params=_params("arbitrary"),
    )(x, norm_w, w_pad)


def _pool_bands(t, anti=False):
    r = np.arange(t)[:, None]
    c = np.arange(t + HEAD)[None, :]
    d = (c - r) if anti else (r - c + HEAD)
    return jnp.asarray(np.stack([(d >= 0) & (d < w) for w in WINDOWS]), BF16)


def _pool_mix(u, halo, z, pw, bands, row0):
    t = u[0].shape[0]
    rows = row0 + lax.broadcasted_iota(I32, (t, 1), 0) + 1
    cnt = [jnp.minimum(rows, w).astype(F32) for w in WINDOWS]
    win = _each(lambda b, h, v: _mask_dot(b, jnp.concatenate([h, v], axis=0)), bands, halo, u)
    mix = _each(lambda a, c, v: a / c - v, win, cnt, u)
    mixed = _each(_bdot, mix, pw)
    return mix, mixed, _each(_sigmoid, z), cnt


POOL_T = 256


def _pool_fwd(proj, pool_w, pool_scale):
    s = proj.shape[0]
    t = POOL_T
    hb = t // HEAD

    def body(u_ref, z_ref, halo_ref, pw_ref, ps_ref, band_ref, y_ref):
        i = pl.program_id(0)
        live = (i > 0).astype(F32)
        groups = lambda ref: [ref[:, sl] for sl in HEAD_COLS]
        z = groups(z_ref)
        _, mixed, sg, _ = _pool_mix(groups(u_ref), [h * live for h in groups(halo_ref)], z,
                                    [pw_ref[g] for g in HEADS], [band_ref[g] for g in HEADS], i * t)
        for sl, m, zg, s_ in zip(HEAD_COLS, mixed, z, sg):
            y_ref[:, sl] = m * ps_ref[:, sl] * (zg * s_)

    return _call(
        body, name="pool_fwd", grid=(s // t,),
        in_specs=[pl.BlockSpec((t, D_HALF), lambda i: (i, 0)),
                  pl.BlockSpec((t, D_HALF), lambda i: (i, 1)),
                  pl.BlockSpec((HEAD, D_HALF), lambda i: (jnp.maximum(i * hb - 1, 0), 0)),
                  pl.BlockSpec((N_HEADS, HEAD, HEAD), lambda i: (0, 0, 0)),
                  pl.BlockSpec((1, D_HALF), lambda i: (0, 0)),
                  pl.BlockSpec((N_HEADS, t, HEAD + t), lambda i: (0, 0, 0))],
        out_specs=pl.BlockSpec((t, D_HALF), lambda i: (i, 0)),
        out_shape=_sds((s, D_HALF)),
        compiler_params=_params("arbitrary"),
    )(proj, proj, proj, pool_w, pool_scale, _pool_bands(t))


def _conv_taps(xv, prev8):
    return [_shift_down(xv, prev8, CONV_K - 1 - j) for j in range(CONV_K - 1)] + [xv]


def _conv_pre(taps, cw):
    y = taps[CONV_K - 1] * cw[CONV_K - 1:CONV_K]
    for j in range(CONV_K - 2, -1, -1):
        y = y + taps[j] * cw[j:j + 1]
    return y


CONV_T = 256
CONV_SUB = 256


def _conv_specs(t, tile_of=lambda i: i):
    tiles = [pl.BlockSpec((t, D_HALF), functools.partial(lambda i, p: (tile_of(i), 2 + p), p=p)) for p in range(3)]
    halos = [pl.BlockSpec((8, D_HALF),
                          functools.partial(lambda i, p: (jnp.maximum(tile_of(i) * (t // 8) - 1, 0), 2 + p), p=p))
             for p in range(3)]
    return tiles + halos


def _conv_fwd(proj, conv_w, a_log, dt_bias):
    s = proj.shape[0]
    t = CONV_T

    def body(q_ref, k_ref, v_ref, hq_ref, hk_ref, hv_ref, ba_ref, cw_ref, al_ref, dtb_ref,
             qn_ref, kn_ref, vs_ref, beta_ref, g_ref):
        live = (pl.program_id(0) > 0).astype(F32)
        parts = ((q_ref, hq_ref, qn_ref), (k_ref, hk_ref, kn_ref), (v_ref, hv_ref, vs_ref))

        def sub_tile(r0, first):
            rows = pl.ds(r0, CONV_SUB)
            for p, (x_ref, h_ref, o_ref) in enumerate(parts):
                for h in HEADS:
                    cs = HEAD_COLS[h]
                    prev8 = h_ref[:, cs] * live if first else x_ref[pl.ds(r0 - 8, 8), cs]
                    y = _conv_pre(_conv_taps(x_ref[rows, cs], prev8), cw_ref[:, p * D_HALF + h * HEAD:p * D_HALF + (h + 1) * HEAD])
                    sv = y * _sigmoid(y)
                    o_ref[rows, cs] = sv if p == 2 else sv * lax.rsqrt(_rowsum(sv * sv) + EPS)
            ba = ba_ref[rows, :]
            for h in HEADS:
                beta = _sigmoid(ba[:, h:h + 1])
                gl = -jnp.exp(al_ref[0:1, h:h + 1]) * _softplus(ba[:, N_HEADS + h:N_HEADS + h + 1] + dtb_ref[0:1, h:h + 1])
                beta_ref[rows, HEAD_COLS[h]] = jnp.broadcast_to(beta, (CONV_SUB, HEAD))
                g_ref[rows, HEAD_COLS[h]] = jnp.broadcast_to(gl, (CONV_SUB, HEAD))

        sub_tile(0, True)

        def step(k, carry):
            sub_tile(pl.multiple_of(k * CONV_SUB, CONV_SUB), False)
            return carry

        lax.fori_loop(1, t // CONV_SUB, step, 0)

    row = pl.BlockSpec((t, D_HALF), lambda i: (i, 0))
    return _call(
        body, name="conv_fwd", grid=(s // t,),
        in_specs=_conv_specs(t) + [pl.BlockSpec((t, HEAD), lambda i: (i, COL_BA // HEAD)),
                                   pl.BlockSpec((CONV_K, 3 * D_HALF), lambda i: (0, 0)),
                                   pl.BlockSpec((1, N_HEADS), lambda i: (0, 0)),
                                   pl.BlockSpec((1, N_HEADS), lambda i: (0, 0))],
        out_specs=[row] * 5,
        out_shape=[_sds((s, D_HALF))] * 5,
        compiler_params=_params("arbitrary"),
    )(proj, proj, proj, proj, proj, proj, proj, conv_w, a_log, dt_bias)


def _pair_masks():
    r = lax.broadcasted_iota(I32, (PAIR, PAIR), 0)
    c = lax.broadcasted_iota(I32, (PAIR, PAIR), 1)
    same = jnp.right_shift(r, 6) == jnp.right_shift(c, 6)
    return same, same & (r >= c), same & (r > c), r == c


def _run(stages):
    for _ in stages:
        pass


def _interleave(*stage_lists):
    live = list(stage_lists)
    while live:
        for gen in list(live):
            try:
                next(gen)
            except StopIteration:
                live.remove(gen)


def _pair_common_stages(cm, qn, kn, vs, beta, g):
    same, incl, strict, eye = _pair_masks()
    incl_b = incl.astype(BF16)
    first = lax.broadcasted_iota(I32, (PAIR, HEAD), 0) < CHUNK
    cm.update(same=same, incl=incl, strict=strict, eye=eye)
    gc = _each(lambda gv: _mask_dot(incl_b, gv), g)
    q = _each(lambda v: v * QK_SCALE, qn)
    kb = _each(lambda k, b: k * b, kn, beta)
    cm.update(gc=gc, q=q, kb=kb, vb=_each(lambda v, b: v * b, vs, beta))
    yield
    cm.update(kk=_each(_bdot_nt, kb, kn), qk=_each(_bdot_nt, q, kn))
    gc_row = _each(lambda v: _colsum(jnp.where(eye, v, 0.0)), gc)
    gl = _each(lambda v: jnp.where(first, v[CHUNK - 1:CHUNK], v[PAIR - 1:PAIR]), gc)
    egc = _each(jnp.exp, gc)
    cm.update(gl=gl, egc=egc,
              decay=_each(lambda v, r: jnp.where(incl, jnp.exp(jnp.where(incl, v - r, 0.0)), 0.0), gc, gc_row))
    yield
    cm.update(ekd=_each(lambda a, b: jnp.exp(a - b), gl, gc), cd=_each(jnp.exp, gl),
              kbg=_each(lambda k, e: k * e, kb, egc))
    yield


def _pair_common(qn, kn, vs, beta, g):
    cm = {}
    _run(_pair_common_stages(cm, qn, kn, vs, beta, g))
    return cm


def _tri_inv_stages(out, a, eye_f):
    p = _each(lambda v: eye_f - v, a)
    x = _each(_bdot, a, a)
    yield
    for it in range(5):
        p = _each(lambda pv, xv: pv + _bdot(pv, xv), p, x)
        if it < 4:
            x = _each(_bdot, x, x)
        yield
    out["t"] = p


def _tri_inv(a, eye_f):
    out = {}
    _run(_tri_inv_stages(out, a, eye_f))
    return out["t"]


def _pair_spec():
    return pl.BlockSpec((INTRA_PAIRS * PAIR, D_HALF), lambda i: (i, 0))


def _chunk_scalar_spec(pairs=1, index=lambda i: (i, 0)):
    return pl.BlockSpec((16 * pairs, D_HALF), index)


SCAN_PAIRS = 2
SCAN_ROWS = SCAN_PAIRS * PAIR


def _intra_fwd(qn, kn, vs, beta, g):
    s = qn.shape[0]

    def body(qn_ref, kn_ref, vs_ref, beta_ref, g_ref, u_ref, w_ref, att_ref, qd_ref, kd_ref, t_ref, cd_ref):
        kn = _heads(kn_ref)
        cm = _pair_common(_heads(qn_ref), kn, _heads(vs_ref), _heads(beta_ref), _heads(g_ref))
        a = _each(lambda kk, d: jnp.where(cm["strict"], kk * d, 0.0), cm["kk"], cm["decay"])
        tm = _tri_inv(a, cm["eye"].astype(F32))
        _put_heads(t_ref, tm)
        _put_heads(u_ref, _each(_bdot, tm, cm["vb"]))
        _put_heads(w_ref, _each(_bdot, tm, cm["kbg"]))
        _put_heads(att_ref, _each(lambda a, b: a * b, cm["qk"], cm["decay"]))
        _put_heads(qd_ref, _each(lambda a, b: a * b, cm["q"], cm["egc"]))
        _put_heads(kd_ref, _each(lambda a, b: a * b, kn, cm["ekd"]))
        for ci in range(2):
            for (pp, h), v in zip(UNITS, cm["cd"]):
                cd_ref[pp * 16 + ci * 8:pp * 16 + (ci + 1) * 8, HEAD_COLS[h]] = v[ci * CHUNK:ci * CHUNK + 8]

    return _call(
        body, name="intra_fwd", grid=(s // (INTRA_PAIRS * PAIR),),
        in_specs=[_pair_spec()] * 5, out_specs=[_pair_spec()] * 6 + [_chunk_scalar_spec(INTRA_PAIRS)],
        out_shape=[_sds((s, D_HALF))] + [_sds((s, D_HALF), BF16)] * 5 + [_sds((s // 8, D_HALF))],
        compiler_params=_params("arbitrary"),
    )(qn, kn, vs, beta, g)


def _scan_fwd(u, w, att, qd, kd, cd):
    s = u.shape[0]
    n_chunks = s // CHUNK

    def body(u_ref, w_ref, att_ref, qd_ref, kd_ref, cd_ref, o_ref, vn_ref, st_ref, state):
        @pl.when(pl.program_id(0) == 0)
        def _():
            state[...] = jnp.zeros_like(state)
        cols = list(enumerate(HEAD_COLS))
        sm = [state[h] for h in HEADS]
        for ci in range(2 * SCAN_PAIRS):
            rs = slice(ci * CHUNK, (ci + 1) * CHUNK)
            for h in HEADS:
                st_ref[ci, h] = sm[h]
            both = [_bdot(jnp.concatenate([w_ref[rs, sl], qd_ref[rs, sl]], axis=0), sm[h]) for h, sl in cols]
            vn = [u_ref[rs, sl] - both[h][:CHUNK] for h, sl in cols]
            for h, sl in cols:
                vn_ref[rs, sl] = vn[h].astype(BF16)
                o_ref[rs, sl] = both[h][CHUNK:]
            sm = [sm[h] * cd_ref[ci * 8:ci * 8 + 1, sl] + _bdot_tn(kd_ref[rs, sl], vn[h]) for h, sl in cols]
        for h in HEADS:
            state[h] = sm[h]
        for pp in range(SCAN_PAIRS):
            rp = slice(pp * PAIR, (pp + 1) * PAIR)
            intra = [_bdot(att_ref[rp, sl], vn_ref[rp, sl]) for sl in HEAD_COLS]
            for h, sl in cols:
                o_ref[rp, sl] += intra[h]

    rows = pl.BlockSpec((SCAN_ROWS, D_HALF), lambda i: (i, 0))
    return _call(
        body, name="scan_fwd", grid=(s // SCAN_ROWS,),
        in_specs=[rows] * 5 + [_chunk_scalar_spec(SCAN_PAIRS)],
        out_specs=[rows, rows, pl.BlockSpec((2 * SCAN_PAIRS, N_HEADS, HEAD, HEAD), lambda i: (i, 0, 0, 0))],
        out_shape=[_sds((s, D_HALF)), _sds((s, D_HALF), BF16), _sds((n_chunks, N_HEADS, HEAD, HEAD))],
        scratch_shapes=[pltpu.VMEM((N_HEADS, HEAD, HEAD), F32)],
        compiler_params=_params("arbitrary"),
    )(u, w, att, qd, kd, cd)


def _delta_fwd(qn, kn, vs, beta, g):
    s = qn.shape[0]
    n_steps = s // SCAN_ROWS
    n_chunks = s // CHUNK
    assert INTRA_PAIRS == SCAN_PAIRS

    def body(qn_ref, kn_ref, vs_ref, beta_ref, g_ref, w_ref, att_ref, qd_ref, kd_ref, t_ref, cd_ref, o_ref, vn_ref, st_ref,
             state, u_s, w_s, att_s, qd_s, kd_s, cd_s):
        t = pl.program_id(0)

        @pl.when(t <= 1)
        def _():
            state[...] = jnp.zeros_like(state)

        @pl.when(t == 0)
        def _():
            for ref in (u_s, w_s, att_s, qd_s, kd_s, cd_s):
                ref[1] = jnp.zeros(ref.shape[1:], ref.dtype)

        cur = lax.rem(t, 2)
        prev = 1 - cur
        cols = list(enumerate(HEAD_COLS))

        def recurrence():
            sm = [state[h] for h in HEADS]
            for ci in range(2 * SCAN_PAIRS):
                rs = slice(ci * CHUNK, (ci + 1) * CHUNK)
                for h in HEADS:
                    st_ref[ci, h] = sm[h]
                both = [_bdot(jnp.concatenate([w_s[prev, rs, sl], qd_s[prev, rs, sl]], axis=0), sm[h]) for h, sl in cols]
                vn = [u_s[prev, rs, sl] - both[h][:CHUNK] for h, sl in cols]
                for h, sl in cols:
                    vn_ref[rs, sl] = vn[h].astype(BF16)
                    o_ref[rs, sl] = both[h][CHUNK:]
                yield
                sm = [sm[h] * cd_s[prev, ci * 8:ci * 8 + 1, sl] + _bdot_tn(kd_s[prev, rs, sl], vn[h]) for h, sl in cols]
                yield
            for h in HEADS:
                state[h] = sm[h]
            for pp in range(SCAN_PAIRS):
                rp = slice(pp * PAIR, (pp + 1) * PAIR)
                intra = [_bdot(att_s[prev, rp, sl], vn_ref[rp, sl]) for sl in HEAD_COLS]
                for h, sl in cols:
                    o_ref[rp, sl] += intra[h]
                yield

        def factors():
            kn = _heads(kn_ref)
            cm = {}
            yield from _pair_common_stages(cm, _heads(qn_ref), kn, _heads(vs_ref), _heads(beta_ref), _heads(g_ref))
            a = _each(lambda kk, d: jnp.where(cm["strict"], kk * d, 0.0), cm["kk"], cm["decay"])
            inv = {}
            yield from _tri_inv_stages(inv, a, cm["eye"].astype(F32))
            tm = inv["t"]
            res = dict(u=_each(_bdot, tm, cm["vb"]), w=_each(_bdot, tm, cm["kbg"]),
                       att=_each(lambda a, b: a * b, cm["qk"], cm["decay"]),
                       qd=_each(lambda a, b: a * b, cm["q"], cm["egc"]), kd=_each(lambda a, b: a * b, kn, cm["ekd"]))
            yield
            _put_heads(t_ref, tm)
            for key, out, keep in (("w", w_ref, w_s), ("att", att_ref, att_s), ("qd", qd_ref, qd_s), ("kd", kd_ref, kd_s)):
                _put_heads(out, res[key])
                for (pp, h), v in zip(UNITS, res[key]):
                    keep[cur, pp * PAIR:(pp + 1) * PAIR, HEAD_COLS[h]] = v.astype(BF16)
            for (pp, h), v in zip(UNITS, res["u"]):
                u_s[cur, pp * PAIR:(pp + 1) * PAIR, HEAD_COLS[h]] = v
            for ci in range(2):
                for (pp, h), v in zip(UNITS, cm["cd"]):
                    rows8 = slice(pp * 16 + ci * 8, pp * 16 + (ci + 1) * 8)
                    cd_ref[rows8, HEAD_COLS[h]] = v[ci * CHUNK:ci * CHUNK + 8]
                    cd_s[cur, rows8, HEAD_COLS[h]] = v[ci * CHUNK:ci * CHUNK + 8]
            yield

        _interleave(recurrence(), factors())

    last = n_steps - 1
    now = lambda i: (jnp.minimum(i, last), 0)
    before = lambda i: (jnp.maximum(i - 1, 0), 0)
    rows = lambda index: pl.BlockSpec((SCAN_ROWS, D_HALF), index)
    slot = lambda r, dtype: pltpu.VMEM((2, r, D_HALF), dtype)
    return _call(
        body, name="delta_fwd", grid=(n_steps + 1,),
        in_specs=[rows(now)] * 5,
        out_specs=[rows(now)] * 5 + [_chunk_scalar_spec(SCAN_PAIRS, now), rows(before), rows(before),
                                     pl.BlockSpec((2 * SCAN_PAIRS, N_HEADS, HEAD, HEAD), lambda i: (jnp.maximum(i - 1, 0), 0, 0, 0))],
        out_shape=[_sds((s, D_HALF), BF16)] * 5 + [_sds((s // 8, D_HALF)), _sds((s, D_HALF)), _sds((s, D_HALF), BF16),
                                                  _sds((n_chunks, N_HEADS, HEAD, HEAD))],
        scratch_shapes=[pltpu.VMEM((N_HEADS, HEAD, HEAD), F32), slot(SCAN_ROWS, F32), slot(SCAN_ROWS, BF16),
                        slot(SCAN_ROWS, BF16), slot(SCAN_ROWS, BF16), slot(SCAN_ROWS, BF16), slot(16 * SCAN_PAIRS, F32)],
        compiler_params=_params("arbitrary"),
    )(qn, kn, vs, beta, g)


OUT_T = 512


def _out_fwd_bwd(x, y_pool, o, proj, target, w_out, dn_norm_w, final_norm_w):
    s = x.shape[0]
    t = OUT_T

    def body(x_ref, yp_ref, o_ref, z_ref, tg_ref, wo_ref, dnw_ref, fnw_ref,
             yt_ref, dh_ref, dyp_ref, do_ref, dz_ref, loss_ref, gfn_ref, gdn_ref, y_ref):
        @pl.when(pl.program_id(0) == 0)
        def _():
            loss_ref[...] = jnp.zeros_like(loss_ref)
            gfn_ref[...] = jnp.zeros_like(gfn_ref)
            gdn_ref[...] = jnp.zeros_like(gdn_ref)

        ypv = yp_ref[...]
        y_ref[:, :D_HALF] = ypv.astype(BF16)
        yt_ref[:D_HALF, :] = ypv.T.astype(BF16)
        dnw = dnw_ref[...]
        keep = []
        for h in HEADS:
            ov = o_ref[:, HEAD_COLS[h]]
            zv = z_ref[:, HEAD_COLS[h]]
            ro = lax.rsqrt(jnp.mean(ov * ov, axis=-1, keepdims=True) + EPS)
            ohat = ov * ro
            sg = _sigmoid(zv)
            keep.append((ro, ohat, zv, sg))
            ydn = ohat * dnw * (zv * sg)
            y_ref[:, D_HALF + h * HEAD:D_HALF + (h + 1) * HEAD] = ydn.astype(BF16)
            yt_ref[D_HALF + h * HEAD:D_HALF + (h + 1) * HEAD, :] = ydn.T.astype(BF16)

        hv = x_ref[...] + jnp.dot(y_ref[...], wo_ref[...], preferred_element_type=F32)
        r2 = lax.rsqrt(jnp.mean(hv * hv, axis=-1, keepdims=True) + EPS)
        hhat = hv * r2
        fnw = fnw_ref[...]
        err = hhat * fnw - tg_ref[...]
        loss_ref[...] += 0.5 * jnp.sum(_rowsum(err * err) * (1.0 / D_MODEL), axis=0, keepdims=True)
        dout = err * (1.0 / D_MODEL)
        gfn_ref[...] += _colsum(dout * hhat)
        dhh = dout * fnw
        dh = r2 * (dhh - hhat * jnp.mean(dhh * hhat, axis=-1, keepdims=True))
        dh_ref[...] = dh
        dy = _bdot_nt(dh, wo_ref[...])
        dyp_ref[...] = dy[:, :D_HALF]
        gdn = jnp.zeros((1, HEAD), F32)
        for h in HEADS:
            ro, ohat, zv, sg = keep[h]
            dyd = dy[:, D_HALF + h * HEAD:D_HALF + (h + 1) * HEAD]
            sz = zv * sg
            dz_ref[:, HEAD_COLS[h]] = (dyd * ohat * dnw * (sg * (1.0 + zv * (1.0 - sg)))).astype(BF16)
            gdn = gdn + _colsum(dyd * ohat * sz)
            doh = dyd * dnw * sz
            do_ref[:, HEAD_COLS[h]] = ro * (doh - ohat * jnp.mean(doh * ohat, axis=-1, keepdims=True))
        gdn_ref[...] += gdn

    wide = pl.BlockSpec((t, D_MODEL), lambda i: (i, 0))
    half = pl.BlockSpec((t, D_HALF), lambda i: (i, 0))
    const = lambda shape: pl.BlockSpec(shape, lambda i: (0,) * len(shape))
    return _call(
        body, name="out_fwd_bwd", grid=(s // t,),
        in_specs=[wide, half, half, pl.BlockSpec((t, D_HALF), lambda i: (i, 5)), wide,
                  const((D_MODEL, D_MODEL)), const((1, HEAD)), const((1, D_MODEL))],
        out_specs=[pl.BlockSpec((D_MODEL, t), lambda i: (0, i)), wide, half, half, half,
                   const((1, HEAD)), const((1, D_MODEL)), const((1, HEAD))],
        out_shape=[_sds((D_MODEL, s), BF16), _sds((s, D_MODEL)), _sds((s, D_HALF)), _sds((s, D_HALF)), _sds((s, D_HALF), BF16),
                   _sds((1, HEAD)), _sds((1, D_MODEL)), _sds((1, HEAD))],
        scratch_shapes=[pltpu.VMEM((t, D_MODEL), BF16)],
        compiler_params=_params("arbitrary"),
    )(x, y_pool, o, proj, target, w_out, dn_norm_w, final_norm_w)


def _token_matmul(name, at, pieces):
    m, s = at.shape
    n = len(pieces)
    tn, tk = D_HALF, 512

    def body(a_ref, *refs):
        p_refs, o_ref, acc = refs[:n], refs[n], refs[n + 1]

        @pl.when(pl.program_id(0) == 0)
        def _():
            acc[...] = jnp.zeros_like(acc)

        av = a_ref[...]
        for p in range(n):
            acc[:, p * tn:(p + 1) * tn] += _bdot(av, p_refs[p][...])

        @pl.when(pl.program_id(0) == pl.num_programs(0) - 1)
        def _():
            o_ref[...] = acc[...].astype(BF16)

    return _call(
        body, name=name, grid=(s // tk,),
        in_specs=[pl.BlockSpec((m, tk), lambda k: (0, k))]
                 + [pl.BlockSpec((tk, tn), functools.partial(lambda k, cb: (k, cb), cb=cb)) for _, cb in pieces],
        out_specs=pl.BlockSpec((m, n * tn), lambda k: (0, 0)),
        out_shape=_sds((m, n * tn), BF16),
        scratch_shapes=[pltpu.VMEM((m, n * tn), F32)],
        compiler_params=_params("arbitrary"),
    )(at, *[p[0] for p in pieces])


def _grad_w_in(at, pieces):
    m, s = at.shape
    n = len(pieces)
    tn, tk = D_HALF, 512

    def body(a_ref, *refs):
        p_refs, o_ref, acc = refs[:n], refs[n], refs[n + 1]

        @pl.when(pl.program_id(0) == 0)
        def _():
            acc[...] = jnp.zeros_like(acc)

        av = a_ref[...]
        for p in range(n):
            acc[:, p * tn:(p + 1) * tn] += _bdot(av, p_refs[p][...])

        @pl.when(pl.program_id(0) == pl.num_programs(0) - 1)
        def _():
            for j in range(4):
                base = j * BLK_IN // HEAD * HEAD
                win = acc[:, base:base + BLK_IN_PAD]
                if j * BLK_IN > base:
                    win = pltpu.roll(win, BLK_IN_PAD - (j * BLK_IN - base), 1)
                o_ref[j] = win.astype(BF16)

    return _call(
        body, name="grad_w_in", grid=(s // tk,),
        in_specs=[pl.BlockSpec((m, tk), lambda k: (0, k))] + [pl.BlockSpec((tk, tn), lambda k: (k, 0))] * n,
        out_specs=pl.BlockSpec((4, m, BLK_IN_PAD), lambda k: (0, 0, 0)),
        out_shape=_sds((4, m, BLK_IN_PAD), BF16),
        scratch_shapes=[pltpu.VMEM((m, n * tn), F32)],
        compiler_params=_params("arbitrary"),
    )(at, *pieces)


def _pool_bwd(proj, dyp, pool_w, pool_scale):
    s = proj.shape[0]
    t = POOL_T
    hb = t // HEAD
    last = s // HEAD - 1

    def body(u_ref, z_ref, halo_ref, dy_ref, zn_ref, dyn_ref, pw_ref, ps_ref, band_ref, aband_ref,
             du_ref, dz_ref, gpw_ref, gps_ref):
        i = pl.program_id(0)

        @pl.when(i == 0)
        def _():
            gpw_ref[...] = jnp.zeros_like(gpw_ref)
            gps_ref[...] = jnp.zeros_like(gps_ref)

        live = (i > 0).astype(F32)
        more = (i < pl.num_programs(0) - 1).astype(F32)
        groups = lambda ref: [ref[:, sl] for sl in HEAD_COLS]
        z, ps, dy = groups(z_ref), groups(ps_ref), groups(dy_ref)
        pw = [pw_ref[g] for g in HEADS]
        mix, mixed, sg, cnt = _pool_mix(groups(u_ref), [h * live for h in groups(halo_ref)], z, pw,
                                        [band_ref[g] for g in HEADS], i * t)
        sz = _each(lambda a, b: a * b, z, sg)
        for sl, d, m, p, s_, zg in zip(HEAD_COLS, dy, mixed, ps, sg, z):
            dz_ref[:, sl] = (d * m * p * (s_ * (1.0 + zg * (1.0 - s_)))).astype(BF16)
        for sl, d, m, a in zip(HEAD_COLS, dy, mixed, sz):
            gps_ref[:, sl] += _colsum(d * m * a)
        dmixed = _each(lambda d, p, a: d * p * a, dy, ps, sz)
        for g, gp in enumerate(_each(_bdot_tn, mix, dmixed)):
            gpw_ref[g] += gp
        dmix = _each(_bdot_nt, dmixed, pw)
        dmix_n = _each(lambda d, p, zn, w_: _bdot_nt(d * more * p * (zn * _sigmoid(zn)), w_),
                       groups(dyn_ref), ps, groups(zn_ref), pw)
        scaled = [jnp.concatenate([a / c, b * (1.0 / w)], axis=0) for a, c, b, w in zip(dmix, cnt, dmix_n, WINDOWS)]
        du = _each(lambda b, s_, d: _mask_dot(b, s_) - d, [aband_ref[g] for g in HEADS], scaled, dmix)
        for sl, v in zip(HEAD_COLS, du):
            du_ref[:, sl] = v.astype(BF16)

    tile = lambda col: pl.BlockSpec((t, D_HALF), lambda i: (i, col))
    below = lambda col: pl.BlockSpec((HEAD, D_HALF), lambda i: (jnp.minimum((i + 1) * hb, last), col))
    return _call(
        body, name="pool_bwd", grid=(s // t,),
        in_specs=[tile(0), tile(1), pl.BlockSpec((HEAD, D_HALF), lambda i: (jnp.maximum(i * hb - 1, 0), 0)),
                  tile(0), below(1), below(0),
                  pl.BlockSpec((N_HEADS, HEAD, HEAD), lambda i: (0, 0, 0)), pl.BlockSpec((1, D_HALF), lambda i: (0, 0)),
                  pl.BlockSpec((N_HEADS, t, HEAD + t), lambda i: (0, 0, 0)),
                  pl.BlockSpec((N_HEADS, t, HEAD + t), lambda i: (0, 0, 0))],
        out_specs=[tile(0), tile(0), pl.BlockSpec((N_HEADS, HEAD, HEAD), lambda i: (0, 0, 0)),
                   pl.BlockSpec((1, D_HALF), lambda i: (0, 0))],
        out_shape=[_sds((s, D_HALF), BF16), _sds((s, D_HALF), BF16), _sds((N_HEADS, HEAD, HEAD)), _sds((1, D_HALF))],
        compiler_params=_params("arbitrary"),
    )(proj, proj, proj, dyp, proj, dyp, pool_w, pool_scale, _pool_bands(t), _pool_bands(t, anti=True))


def _scan_bwd(do, vn, qd, kd, w, att, cd, st):
    s = do.shape[0]
    n_steps = s // SCAN_ROWS

    def body(do_ref, vn_ref, qd_ref, kd_ref, w_ref, att_ref, cd_ref, st_ref,
             du_ref, dw_ref, datt_ref, dqd_ref, dkd_ref, dcd_ref, dstate):
        @pl.when(pl.program_id(0) == 0)
        def _():
            dstate[...] = jnp.zeros_like(dstate)
        _, incl, _, _ = _pair_masks()
        cols = list(enumerate(HEAD_COLS))
        dv_intra = []
        for pp in range(SCAN_PAIRS):
            rp = slice(pp * PAIR, (pp + 1) * PAIR)
            dv_intra.append([_bdot_tn(att_ref[rp, sl], do_ref[rp, sl]) for _, sl in cols])
            for _, sl in cols:
                datt_ref[rp, sl] = jnp.where(incl, _bdot_nt(do_ref[rp, sl], vn_ref[rp, sl]), 0.0)
        ds = [dstate[h] for h in HEADS]
        for ci in range(2 * SCAN_PAIRS - 1, -1, -1):
            rs = slice(ci * CHUNK, (ci + 1) * CHUNK)
            in_pair = slice((ci % 2) * CHUNK, (ci % 2 + 1) * CHUNK)
            sm = [st_ref[ci, h] for h in HEADS]
            dvn = [dv_intra[ci // 2][h][in_pair] + _bdot(kd_ref[rs, sl], ds[h]) for h, sl in cols]
            for h, sl in cols:
                du_ref[rs, sl] = dvn[h].astype(BF16)
            dqd = [_bdot_nt(do_ref[rs, sl], sm[h]) for h, sl in cols]
            dw = [-_bdot_nt(dvn[h], sm[h]) for h, _ in cols]
            dkd = [_bdot_nt(vn_ref[rs, sl], ds[h]) for h, sl in cols]
            dcd = [jnp.broadcast_to(_rowsum(_colsum(ds[h] * sm[h])), (8, HEAD)) for h in HEADS]
            for h, sl in cols:
                dqd_ref[rs, sl] = dqd[h]
                dw_ref[rs, sl] = dw[h].astype(BF16)
                dkd_ref[rs, sl] = dkd[h]
                dcd_ref[ci * 8:(ci + 1) * 8, sl] = dcd[h]
            ds = [ds[h] * cd_ref[ci * 8:ci * 8 + 1, sl] + _bdot_tn(qd_ref[rs, sl], do_ref[rs, sl])
                  - _bdot_tn(w_ref[rs, sl], dvn[h]) for h, sl in cols]
        for h in HEADS:
            dstate[h] = ds[h]

    rev = pl.BlockSpec((SCAN_ROWS, D_HALF), lambda i: (n_steps - 1 - i, 0))
    rev_scalar = _chunk_scalar_spec(SCAN_PAIRS, lambda i: (n_steps - 1 - i, 0))
    return _call(
        body, name="scan_bwd", grid=(n_steps,),
        in_specs=[rev] * 6 + [rev_scalar,
                              pl.BlockSpec((2 * SCAN_PAIRS, N_HEADS, HEAD, HEAD), lambda i: (n_steps - 1 - i, 0, 0, 0))],
        out_specs=[rev] * 5 + [rev_scalar],
        out_shape=[_sds((s, D_HALF), BF16)] * 2 + [_sds((s, D_HALF))] * 3 + [_sds((s // 8, D_HALF))],
        scratch_shapes=[pltpu.VMEM((N_HEADS, HEAD, HEAD), F32)],
        compiler_params=_params("arbitrary"),
    )(do, vn, qd, kd, w, att, cd, st)


def _intra_bwd(qn, kn, vs, beta, g, tm, du, dw, datt, dqd, dkd, dcd):
    s = qn.shape[0]

    def body(qn_ref, kn_ref, vs_ref, beta_ref, g_ref, t_ref, du_ref, dw_ref, datt_ref, dqd_ref, dkd_ref, dcd_ref,
             dqn_ref, dkn_ref, dvs_ref, dbeta_ref, dg_ref):
        ones = jnp.ones((PAIR, HEAD), BF16)
        tn = (((0,), (0,)), ((), ()))
        kn, vs, beta = _heads(kn_ref), _heads(vs_ref), _heads(beta_ref)
        cm = _pair_common(_heads(qn_ref), kn, vs, beta, _heads(g_ref))
        tmv, duv, dwv, dattv, dqdv, dkdv = (_heads(r) for r in (t_ref, du_ref, dw_ref, datt_ref, dqd_ref, dkd_ref))
        dvb = _each(_bdot_tn, tmv, duv)
        dt = _each(lambda a, b, c, d: _bdot_nt(a, b) + _bdot_nt(c, d), duv, cm["vb"], dwv, cm["kbg"])
        dkbg = _each(_bdot_tn, tmv, dwv)
        m1 = _each(_bdot_tn, tmv, dt)
        da = _each(lambda a, b: -jnp.where(cm["strict"], _bdot_nt(a, b), 0.0), m1, tmv)
        dkk = _each(lambda a, b: a * b, da, cm["decay"])
        dqk = _each(lambda a, b: a * b, dattv, cm["decay"])
        dd = _each(lambda a, b, c, d: a * b + c * d, dkk, cm["kk"], dqk, cm["qk"])
        dkb = _each(lambda a, b, c, d: _bdot(a, b) + c * d, dkk, kn, dkbg, cm["egc"])
        dq = _each(lambda a, b, c, d: _bdot(a, b) + c * d, dqk, kn, dqdv, cm["egc"])
        dkn = _each(lambda a, b, c, d: _bdot_tn(a, b) + _bdot_tn(c, d), dkk, cm["kb"], dqk, cm["q"])
        dkn = _each(lambda a, b, c, d, e: a + b * c + d * e, dkn, dkdv, cm["ekd"], dkb, beta)
        t_kd = _each(lambda a, b, c: _rowsum(a * b * c), dkdv, kn, cm["ekd"])
        split = _each(_split, dd)
        rows_dd = [jnp.dot(hi, ones, preferred_element_type=F32) + jnp.dot(lo, ones, preferred_element_type=F32)
                   for hi, lo in split]
        cols_dd = [lax.dot_general(hi, ones, tn, preferred_element_type=F32)
                   + lax.dot_general(lo, ones, tn, preferred_element_type=F32) for hi, lo in split]
        dgc = _each(lambda r, c, a, b, e, f, k, t: r - c + _rowsum(a * b * e) + _rowsum(f * k) - t,
                    rows_dd, cols_dd, dqdv, cm["q"], cm["egc"], dkbg, cm["kbg"], t_kd)
        same_b = cm["same"].astype(BF16)
        rowi = lax.broadcasted_iota(I32, (PAIR, HEAD), 0)
        dcd = _each(lambda d: jnp.where(rowi < CHUNK, d[0:1], d[8:9]), _heads(dcd_ref, rows=16))
        dgl = _each(lambda t, d, c: _mask_dot(same_b, jnp.broadcast_to(t, (PAIR, HEAD))) + d * c, t_kd, dcd, cm["cd"])
        is_last = jnp.bitwise_and(rowi, CHUNK - 1) == CHUNK - 1
        dgc = _each(lambda a, b: a + jnp.where(is_last, b, 0.0), dgc, dgl)
        r = lax.broadcasted_iota(I32, (PAIR, PAIR), 0)
        c = lax.broadcasted_iota(I32, (PAIR, PAIR), 1)
        upper_b = (cm["same"] & (r <= c)).astype(BF16)
        _put_heads(dg_ref, _each(lambda v: _mask_dot(upper_b, v), dgc))
        _put_heads(dbeta_ref, _each(lambda a, b, c, d: jnp.broadcast_to(_rowsum(a * b) + _rowsum(c * d), (PAIR, HEAD)),
                                    dkb, kn, dvb, vs))
        _put_heads(dqn_ref, _each(lambda v: v * QK_SCALE, dq))
        _put_heads(dkn_ref, dkn)
        _put_heads(dvs_ref, _each(lambda a, b: a * b, dvb, beta))

    return _call(
        body, name="intra_bwd", grid=(s // (INTRA_PAIRS * PAIR),),
        in_specs=[_pair_spec()] * 11 + [_chunk_scalar_spec(INTRA_PAIRS)], out_specs=[_pair_spec()] * 5,
        out_shape=[_sds((s, D_HALF))] * 5,
        compiler_params=_params("arbitrary"),
    )(qn, kn, vs, beta, g, tm, du, dw, datt, dqd, dkd, dcd)


def _rows8(x):
    acc = x[0:8]
    for r in range(8, x.shape[0], 8):
        acc = acc + x[r:r + 8]
    return acc


def _conv_bwd(proj, conv_w, a_log, dt_bias, dqn, dkn, dvs, dbeta, dg):
    s = proj.shape[0]
    t = CONV_T
    n_tiles = s // t
    n_sub = t // CONV_SUB
    tile_of = lambda i: n_tiles - 1 - i

    def body(q_ref, k_ref, v_ref, hq_ref, hk_ref, hv_ref, ba_ref, cw_ref, al_ref, dtb_ref,
             dqn_ref, dkn_ref, dvs_ref, dbeta_ref, dg_ref, oq_ref, ok_ref, ov_ref, dba_ref, gcw_out, gsm_out,
             below, gcw_ref, gsm_ref):
        @pl.when(pl.program_id(0) == 0)
        def _():
            gcw_ref[...] = jnp.zeros_like(gcw_ref)
            gsm_ref[...] = jnp.zeros_like(gsm_ref)
            below[...] = jnp.zeros_like(below)

        live = (pl.program_id(0) < n_tiles - 1).astype(F32)
        parts = ((q_ref, hq_ref, dqn_ref, oq_ref), (k_ref, hk_ref, dkn_ref, ok_ref), (v_ref, hv_ref, dvs_ref, ov_ref))
        lane = lax.broadcasted_iota(I32, (CONV_SUB, HEAD), 1)
        lane8 = lax.broadcasted_iota(I32, (8, HEAD), 1)

        def sub_tile(r0, first):
            rows = pl.ds(r0, CONV_SUB)
            for p, (x_ref, h_ref, d_ref, o_ref) in enumerate(parts):
                for h in HEADS:
                    cs = HEAD_COLS[h]
                    wide = slice(p * D_HALF + h * HEAD, p * D_HALF + (h + 1) * HEAD)
                    cw = cw_ref[:, wide]
                    prev8 = h_ref[:, cs] * live if first else x_ref[pl.ds(r0 - 8, 8), cs]
                    taps = _conv_taps(x_ref[rows, cs], prev8)
                    y = _conv_pre(taps, cw)
                    sg = _sigmoid(y)
                    sv = y * sg
                    ds = d_ref[rows, cs]
                    if p < 2:
                        rn = lax.rsqrt(_rowsum(sv * sv) + EPS)
                        nrm = sv * rn
                        ds = rn * (ds - nrm * _rowsum(ds * nrm))
                    dy = ds * (sg * (1.0 + y * (1.0 - sg)))
                    for j in range(CONV_K):
                        gcw_ref[8 * j:8 * j + 8, wide] += _rows8(dy * taps[j])
                    nxt = below[:, wide]
                    acc = dy * cw[CONV_K - 1:CONV_K]
                    for sft in range(1, CONV_K):
                        acc = acc + _shift_up(dy, nxt, sft) * cw[CONV_K - 1 - sft:CONV_K - sft]
                    o_ref[rows, cs] = acc.astype(BF16)
                    below[:, wide] = dy[0:8]

            ba = ba_ref[rows, :]
            dba = jnp.zeros((CONV_SUB, HEAD), F32)
            gsm = jnp.zeros((8, HEAD), F32)
            for h in HEADS:
                beta = _sigmoid(ba[:, h:h + 1])
                dbeta = dbeta_ref[rows, h * HEAD:h * HEAD + 1]
                xg = ba[:, N_HEADS + h:N_HEADS + h + 1] + dtb_ref[0:1, h:h + 1]
                nexp = -jnp.exp(al_ref[0:1, h:h + 1])
                dgv = dg_ref[rows, h * HEAD:h * HEAD + 1]
                da = dgv * nexp * _sigmoid(xg)
                dba = dba + jnp.where(lane == h, dbeta * beta * (1.0 - beta), 0.0) + jnp.where(lane == N_HEADS + h, da, 0.0)
                gsm = (gsm + jnp.where(lane8 == h, _rows8(dgv * nexp * _softplus(xg)), 0.0)
                       + jnp.where(lane8 == N_HEADS + h, _rows8(da), 0.0))
            dba_ref[rows, :] = jnp.zeros((CONV_SUB, D_HALF), BF16)
            dba_ref[rows, :HEAD] = dba.astype(BF16)
            gsm_ref[...] += gsm

        def step(k, carry):
            sub_tile(pl.multiple_of((n_sub - 1 - k) * CONV_SUB, CONV_SUB), False)
            return carry

        lax.fori_loop(0, n_sub - 1, step, 0)
        sub_tile(0, True)

        @pl.when(pl.program_id(0) == n_tiles - 1)
        def _():
            gcw_out[...] = jnp.zeros_like(gcw_out)
            for j in range(CONV_K):
                gcw_out[j:j + 1, :] = _colsum(gcw_ref[8 * j:8 * j + 8, :])
            gsm_out[...] = jnp.broadcast_to(_colsum(gsm_ref[...]), (8, HEAD))

    row = pl.BlockSpec((t, D_HALF), lambda i: (tile_of(i), 0))
    const = lambda shape: pl.BlockSpec(shape, lambda i: (0, 0))
    return _call(
        body, name="conv_bwd", grid=(n_tiles,),
        in_specs=_conv_specs(t, tile_of) + [pl.BlockSpec((t, HEAD), lambda i: (tile_of(i), COL_BA // HEAD)),
                                            const((CONV_K, 3 * D_HALF)), const((1, N_HEADS)), const((1, N_HEADS))] + [row] * 5,
        out_specs=[row, row, row, row, const((8, 3 * D_HALF)), const((8, HEAD))],
        out_shape=[_sds((s, D_HALF), BF16)] * 4 + [_sds((8, 3 * D_HALF)), _sds((8, HEAD))],
        scratch_shapes=[pltpu.VMEM((8, 3 * D_HALF), F32), pltpu.VMEM((8 * CONV_K, 3 * D_HALF), F32),
                        pltpu.VMEM((8, HEAD), F32)],
        compiler_params=_params("arbitrary"),
    )(proj, proj, proj, proj, proj, proj, proj, conv_w, a_log, dt_bias, dqn, dkn, dvs, dbeta, dg)


def _conv_bwd_pre(proj, conv_w, a_log, dt_bias, dqn, dkn, dvs, dbeta, dg):
    s = proj.shape[0]
    t = CONV_T

    def body(q_ref, k_ref, v_ref, hq_ref, hk_ref, hv_ref, ba_ref, cw_ref, al_ref, dtb_ref,
             dqn_ref, dkn_ref, dvs_ref, dbeta_ref, dg_ref, dyq_ref, dyk_ref, dyv_ref, dba_ref, gcw_ref, gsm_ref):
        @pl.when(pl.program_id(0) == 0)
        def _():
            gcw_ref[...] = jnp.zeros_like(gcw_ref)
            gsm_ref[...] = jnp.zeros_like(gsm_ref)

        live = (pl.program_id(0) > 0).astype(F32)
        parts = ((q_ref, hq_ref, dqn_ref, dyq_ref), (k_ref, hk_ref, dkn_ref, dyk_ref), (v_ref, hv_ref, dvs_ref, dyv_ref))
        for p, (x_ref, h_ref, d_ref, dy_ref) in enumerate(parts):
            cols = slice(p * D_HALF, (p + 1) * D_HALF)
            taps = _conv_taps(x_ref[...], h_ref[...] * live)
            y = _conv_pre(taps, cw_ref[:, cols])
            sg = _sigmoid(y)
            sv = y * sg
            if p == 2:
                ds = d_ref[...]
            else:
                segs = []
                for h in HEADS:
                    seg = _head(sv, h)
                    rn = lax.rsqrt(_rowsum(seg * seg) + EPS)
                    nrm = seg * rn
                    dn = d_ref[:, HEAD_COLS[h]]
                    segs.append(rn * (dn - nrm * _rowsum(dn * nrm)))
                ds = jnp.concatenate(segs, axis=1)
            dy = ds * (sg * (1.0 + y * (1.0 - sg)))
            dy_ref[...] = dy
            for j in range(CONV_K):
                gcw_ref[j:j + 1, cols] += _colsum(dy * taps[j])

        ba = ba_ref[...]
        lane = lax.broadcasted_iota(I32, (t, HEAD), 1)
        lane1 = lax.broadcasted_iota(I32, (1, HEAD), 1)
        dba = jnp.zeros((t, HEAD), F32)
        gsm = jnp.zeros((1, HEAD), F32)
        for h in HEADS:
            beta = _sigmoid(ba[:, h:h + 1])
            dbeta = dbeta_ref[:, h * HEAD:h * HEAD + 1]
            xg = ba[:, N_HEADS + h:N_HEADS + h + 1] + dtb_ref[0:1, h:h + 1]
            nexp = -jnp.exp(al_ref[0:1, h:h + 1])
            dgv = dg_ref[:, h * HEAD:h * HEAD + 1]
            da = dgv * nexp * _sigmoid(xg)
            dba = dba + jnp.where(lane == h, dbeta * beta * (1.0 - beta), 0.0) + jnp.where(lane == N_HEADS + h, da, 0.0)
            gsm = (gsm + jnp.where(lane1 == h, _colsum(dgv * nexp * _softplus(xg)), 0.0)
                   + jnp.where(lane1 == N_HEADS + h, _colsum(da), 0.0))
        dba_ref[...] = jnp.zeros_like(dba_ref)
        dba_ref[:, :HEAD] = dba.astype(BF16)
        gsm_ref[0:1, :] += gsm

    row = pl.BlockSpec((t, D_HALF), lambda i: (i, 0))
    return _call(
        body, name="conv_bwd_pre", grid=(s // t,),
        in_specs=_conv_specs(t) + [pl.BlockSpec((t, HEAD), lambda i: (i, COL_BA // HEAD)),
                                   pl.BlockSpec((CONV_K, 3 * D_HALF), lambda i: (0, 0)),
                                   pl.BlockSpec((1, N_HEADS), lambda i: (0, 0)),
                                   pl.BlockSpec((1, N_HEADS), lambda i: (0, 0))] + [row] * 5,
        out_specs=[row, row, row, row,
                   pl.BlockSpec((8, 3 * D_HALF), lambda i: (0, 0)), pl.BlockSpec((8, HEAD), lambda i: (0, 0))],
        out_shape=[_sds((s, D_HALF))] * 3 + [_sds((s, D_HALF), BF16), _sds((8, 3 * D_HALF)), _sds((8, HEAD))],
        compiler_params=_params("arbitrary"),
    )(proj, proj, proj, proj, proj, proj, proj, conv_w, a_log, dt_bias, dqn, dkn, dvs, dbeta, dg)


def _conv_bwd_in(dyq, dyk, dyv, conv_w):
    s = dyq.shape[0]
    t = CONV_T
    last = s // 8 - 1

    def body(q_ref, k_ref, v_ref, nq_ref, nk_ref, nv_ref, cw_ref, oq_ref, ok_ref, ov_ref):
        more = (pl.program_id(0) < pl.num_programs(0) - 1).astype(F32)
        for p, (d_ref, n_ref, o_ref) in enumerate(((q_ref, nq_ref, oq_ref), (k_ref, nk_ref, ok_ref), (v_ref, nv_ref, ov_ref))):
            cw = cw_ref[:, p * D_HALF:(p + 1) * D_HALF]
            dy = d_ref[...]
            nxt = n_ref[...] * more
            acc = dy * cw[3:4]
            for sft in (1, 2, 3):
                acc = acc + _shift_up(dy, nxt, sft) * cw[3 - sft:4 - sft]
            o_ref[...] = acc.astype(BF16)

    row = pl.BlockSpec((t, D_HALF), lambda i: (i, 0))
    nxt = pl.BlockSpec((8, D_HALF), lambda i: (jnp.minimum((i + 1) * (t // 8), last), 0))
    return _call(
        body, name="conv_bwd_in", grid=(s // t,),
        in_specs=[row] * 3 + [nxt] * 3 + [pl.BlockSpec((CONV_K, 3 * D_HALF), lambda i: (0, 0))],
        out_specs=[row] * 3, out_shape=[_sds((s, D_HALF), BF16)] * 3,
        compiler_params=_params("arbitrary"),
    )(dyq, dyk, dyv, dyq, dyk, dyv, conv_w)


IN_T = 512


def _in_bwd(x, dh, norm_w, w_pad, pieces):
    s = x.shape[0]
    t = IN_T
    widths = [D_HALF] * 6 + [N_IN_PAD - COL_BA]

    def body(*refs):
        x_ref, dh_ref, nw_ref, w_ref = refs[:4]
        p_refs = refs[4:4 + len(pieces)]
        gx_ref, gnw_ref = refs[4 + len(pieces):]

        @pl.when(pl.program_id(0) == 0)
        def _():
            gnw_ref[...] = jnp.zeros_like(gnw_ref)

        dn = jnp.zeros((t, D_MODEL), F32)
        col = 0
        for p_ref, wd in zip(p_refs, widths):
            dn = dn + _bdot_nt(p_ref[...], w_ref[:, col:col + wd])
            col += wd
        xv = x_ref[...]
        r = lax.rsqrt(jnp.mean(xv * xv, axis=-1, keepdims=True) + EPS)
        xhat = xv * r
        gnw_ref[...] += _colsum(dn * xhat)
        dxh = dn * nw_ref[...]
        gx_ref[...] = dh_ref[...] + r * (dxh - xhat * jnp.mean(dxh * xhat, axis=-1, keepdims=True))

    wide = pl.BlockSpec((t, D_MODEL), lambda i: (i, 0))
    return _call(
        body, name="in_bwd", grid=(s // t,),
        in_specs=[wide, wide, pl.BlockSpec((1, D_MODEL), lambda i: (0, 0)),
                  pl.BlockSpec((D_MODEL, N_IN_PAD), lambda i: (0, 0))]
                 + [pl.BlockSpec((t, wd), lambda i: (i, 0)) for wd in widths],
        out_specs=[wide, pl.BlockSpec((1, D_MODEL), lambda i: (0, 0))],
        out_shape=[_sds((s, D_MODEL)), _sds((1, D_MODEL))],
        compiler_params=_params("arbitrary"),
    )(x, dh, norm_w, w_pad, *pieces)


def _adamw_shard(name, w, g_own, g_got, cidx, m, v):
    _, r, c = w.shape
    half = r // 2
    rows = 256 if half % 256 == 0 else half
    per_half = half // rows

    def body(c_ref, w_ref, go_ref, gg_ref, m_ref, v_ref, gout_ref, d_ref, nm_ref, nv_ref):
        mine = (pl.program_id(0) // per_half) == c_ref[0]
        gv = jnp.where(mine, go_ref[:, :c], gg_ref[:, :c])
        gout_ref[0] = gv
        mn = ADAM_B1 * m_ref[0] + (1.0 - ADAM_B1) * gv
        vn = ADAM_B2 * v_ref[0] + (1.0 - ADAM_B2) * (gv * gv)
        m_hat = mn / (1.0 - ADAM_B1 ** ADAM_STEP)
        v_hat = vn / (1.0 - ADAM_B2 ** ADAM_STEP)
        d_ref[0] = -ADAM_LR * (m_hat / (jnp.sqrt(v_hat) + ADAM_EPS) + ADAM_WD * w_ref[0])
        nm_ref[0] = mn
        nv_ref[0] = vn

    blk = pl.BlockSpec((1, rows, c), lambda i, c_ref: (0, i, 0))
    gblk = pl.BlockSpec((rows, g_own.shape[1]), lambda i, c_ref: (i % per_half, 0))
    return _call(
        body, name=name,
        grid_spec=pltpu.PrefetchScalarGridSpec(
            num_scalar_prefetch=1, grid=(2 * per_half,),
            in_specs=[blk, gblk, gblk, blk, blk], out_specs=[blk] * 4),
        out_shape=[_sds((1, r, c))] * 4,
        compiler_params=_params("arbitrary"),
    )(cidx, w, g_own, g_got, m, v)


def _adamw_tiles(name, w, g, m, v):
    n = w.shape[0]
    nb = 77 if n % 77 == 0 else n

    def body(w_ref, g_ref, m_ref, v_ref, d_ref, nm_ref, nv_ref):
        gv = g_ref[...]
        mn = ADAM_B1 * m_ref[...] + (1.0 - ADAM_B1) * gv
        vn = ADAM_B2 * v_ref[...] + (1.0 - ADAM_B2) * (gv * gv)
        m_hat = mn / (1.0 - ADAM_B1 ** ADAM_STEP)
        v_hat = vn / (1.0 - ADAM_B2 ** ADAM_STEP)
        d_ref[...] = -ADAM_LR * (m_hat / (jnp.sqrt(v_hat) + ADAM_EPS) + ADAM_WD * w_ref[...])
        nm_ref[...] = mn
        nv_ref[...] = vn

    blk = pl.BlockSpec((nb, 8, HEAD), lambda i: (i, 0, 0))
    return _call(
        body, name=name, grid=(n // nb,),
        in_specs=[blk] * 4, out_specs=[blk] * 3, out_shape=[_sds(w.shape)] * 3,
        compiler_params=_params("arbitrary"),
    )(w, g, m, v)


def _exchange(name, inputs, out_shapes, phases):
    n_in = len(inputs)
    n_out = len(out_shapes)
    n_cp = sum(len(p) for p in phases)

    def body(*refs):
        ins, outs = refs[:n_in], refs[n_in:n_in + n_out]
        send, recv = refs[n_in + n_out:]
        pos = (lax.axis_index("x"), lax.axis_index("y"), lax.axis_index("c"))
        k = 0
        for phase in phases:
            cps = []
            for src, dst, target in phase:
                cps.append(pltpu.make_async_remote_copy(
                    src_ref=src(ins, outs, pos), dst_ref=dst(ins, outs, pos), send_sem=send.at[k], recv_sem=recv.at[k],
                    device_id=target(pos), device_id_type=pl.DeviceIdType.MESH))
                k += 1
            for cp in cps:
                cp.start()
            for cp in cps:
                cp.wait()

    anyspec = pl.BlockSpec(memory_space=pl.ANY)
    return _call(
        body, name=name,
        in_specs=[anyspec] * n_in, out_specs=[anyspec] * n_out, out_shape=list(out_shapes),
        scratch_shapes=[pltpu.SemaphoreType.DMA((n_cp,)), pltpu.SemaphoreType.DMA((n_cp,))],
    )(*inputs)


def _exchange_start(name, inputs, out_shapes, copies):
    n_in, n_out, n_cp = len(inputs), len(out_shapes), len(copies)

    def body(*refs):
        ins, lands = refs[:n_in], refs[n_in:n_in + n_out]
        sems = refs[n_in + n_out:n_in + n_out + 2 * n_cp]
        token = refs[-1]
        pos = (lax.axis_index("x"), lax.axis_index("y"), lax.axis_index("c"))
        for k, (src, dst, target) in enumerate(copies):
            pltpu.make_async_remote_copy(
                src_ref=src(ins, lands, pos), dst_ref=dst(ins, lands, pos), send_sem=sems[2 * k], recv_sem=sems[2 * k + 1],
                device_id=target(pos), device_id_type=pl.DeviceIdType.MESH).start()
        token[...] = jnp.zeros_like(token)

    hbm = pl.BlockSpec(memory_space=pltpu.HBM)
    sem = pl.BlockSpec(memory_space=pltpu.SEMAPHORE)
    bufs = list(inputs) + [lax.empty(o.shape, o.dtype) for o in out_shapes]
    outs = _call(
        body, name=name,
        out_shape=tuple([pltpu.SemaphoreType.DMA(())] * (2 * n_cp) + [pltpu.HBM(b.shape, b.dtype) for b in bufs]
                        + [_sds((8, HEAD))]),
        in_specs=[hbm] * len(bufs),
        out_specs=tuple([sem] * (2 * n_cp) + [hbm] * len(bufs) + [pl.BlockSpec(memory_space=pltpu.VMEM)]),
        input_output_aliases={i: 2 * n_cp + i for i in range(len(bufs))},
        compiler_params=pltpu.CompilerParams(has_side_effects=pltpu.SideEffectType.DATAFLOW_SIDE_EFFECTING),
    )(*[pltpu.with_memory_space_constraint(b, pltpu.HBM) for b in bufs])
    return outs[:2 * n_cp], outs[2 * n_cp:2 * n_cp + n_in], outs[2 * n_cp + n_in:-1], outs[-1]


def _exchange_wait(name, sems, sources, lands, copies, after):
    n_in, n_out, n_cp = len(sources), len(lands), len(copies)

    def body(*refs):
        ins, zones = refs[:n_in], refs[n_in:n_in + n_out]
        sem_refs = refs[n_in + n_out:n_in + n_out + 2 * n_cp]
        pos = (lax.axis_index("x"), lax.axis_index("y"), lax.axis_index("c"))
        for k, (src, dst, target) in enumerate(copies):
            cp = pltpu.make_async_remote_copy(
                src_ref=src(ins, zones, pos), dst_ref=dst(ins, zones, pos), send_sem=sem_refs[2 * k],
                recv_sem=sem_refs[2 * k + 1], device_id=target(pos), device_id_type=pl.DeviceIdType.MESH)
            cp.wait_send()
            cp.wait_recv()

    hbm = pl.BlockSpec(memory_space=pltpu.HBM)
    sem = pl.BlockSpec(memory_space=pltpu.SEMAPHORE)
    bufs = list(sources) + list(lands)
    outs = _call(
        body, name=name,
        out_shape=tuple(pltpu.HBM(b.shape, b.dtype) for b in bufs),
        in_specs=[hbm] * len(bufs) + [sem] * (2 * n_cp) + [pl.BlockSpec(memory_space=pl.ANY)],
        out_specs=tuple([hbm] * len(bufs)),
        input_output_aliases={i: i for i in range(len(bufs))},
        compiler_params=pltpu.CompilerParams(has_side_effects=pltpu.SideEffectType.DATAFLOW_SIDE_EFFECTING),
    )(*bufs, *sems, after)
    return outs[:n_in], outs[n_in:]


def _allreduce_tile(name, v):
    def body(v_ref, out_ref, slots, send, recv):
        x, y, c = lax.axis_index("x"), lax.axis_index("y"), lax.axis_index("c")
        me = 4 * x + 2 * y + c
        slots[me] = v_ref[...]
        cps = []
        for k in range(1, 8):
            peer = (x ^ (k >> 2), y ^ ((k >> 1) & 1), c ^ (k & 1))
            cps.append(pltpu.make_async_remote_copy(
                src_ref=v_ref, dst_ref=slots.at[me], send_sem=send.at[k - 1], recv_sem=recv.at[k - 1],
                device_id=peer, device_id_type=pl.DeviceIdType.MESH))
        for cp in cps:
            cp.start()
        for cp in cps:
            cp.wait()
        acc = slots[0]
        for i in range(1, 8):
            acc = acc + slots[i]
        out_ref[...] = acc

    vm = pl.BlockSpec(memory_space=pltpu.VMEM)
    return _call(
        body, name=name, in_specs=[vm], out_specs=vm, out_shape=_sds(v.shape),
        scratch_shapes=[pltpu.VMEM((8,) + v.shape, F32), pltpu.SemaphoreType.DMA((7,)), pltpu.SemaphoreType.DMA((7,))],
    )(v)


def _chip(pos):
    return 2 * pos[0] + pos[1]


def _other_chip(pos, mask):
    x, y, c = pos
    return (x ^ (mask >> 1), y ^ (mask & 1), c)


def _sibling(pos):
    return (pos[0], pos[1], 1 - pos[2])


def _gather_weights(wb, cb):
    rows = wb.shape[0] // 2
    x_nb, y_nb, diag = CHIP_MASKS

    def part(pos, mask, quarter=None):
        start = pos[2] * rows if quarter is None else pos[2] * rows + quarter * (rows // 2)
        return lambda outs: outs[0].at[_chip(pos) ^ mask, pl.ds(start, rows if quarter is None else rows // 2)]

    def passed_on(mask, to, quarter=None):
        return (lambda ins, outs, pos: part(pos, mask, quarter)(outs), lambda ins, outs, pos: part(pos, mask, quarter)(outs), to)

    first = [(lambda ins, outs, pos: ins[0].at[pl.ds(pos[2] * rows, rows)], lambda ins, outs, pos: part(pos, 0)(outs),
              functools.partial(_other_chip, mask=mask)) for mask in (x_nb, y_nb)]
    first += [(lambda ins, outs, pos: ins[1], lambda ins, outs, pos: outs[1].at[_chip(pos)],
               functools.partial(_other_chip, mask=mask)) for mask in CHIP_MASKS]
    second = [passed_on(x_nb, functools.partial(_other_chip, mask=y_nb), quarter=0),
              passed_on(y_nb, functools.partial(_other_chip, mask=x_nb), quarter=1),
              passed_on(x_nb, _sibling), passed_on(y_nb, _sibling)]
    third = [passed_on(diag, _sibling)]
    return _exchange("gather_weights", [wb, cb], [_sds((4,) + wb.shape, wb.dtype), _sds((4,) + cb.shape, cb.dtype)],
                     [first, second, third])


def _gather_blocks(ob):
    copies = [(lambda ins, outs, pos: ins[0], lambda ins, outs, pos: outs[0].at[_chip(pos)],
               functools.partial(_other_chip, mask=mask)) for mask in CHIP_MASKS]
    return [_sds((4,) + ob.shape, ob.dtype)], copies


def _to_sibling_half(name, arrays):
    def src(ins, outs, pos, a):
        h = arrays[a].shape[-2] // 2
        sl = pl.ds((1 - pos[2]) * h, h)
        return ins[a].at[:, sl] if arrays[a].ndim == 3 else ins[a].at[sl]

    outs = [_sds(a.shape[:-2] + (a.shape[-2] // 2, a.shape[-1]), a.dtype) for a in arrays]
    phase = [(functools.partial(src, a=a), lambda ins, outs, pos, a=a: outs[a], _sibling) for a in range(len(arrays))]
    return _exchange(name, arrays, outs, [phase])


def _add_half(name, full, part, cidx):
    shape = part.shape
    lead = shape[0] if len(shape) == 3 else 1
    rows, cols = shape[-2], shape[-1]
    tr = rows // 2 if rows % 16 == 0 else rows
    nr = rows // tr
    f3 = full.reshape((lead,) + full.shape[-2:])
    p3 = part.reshape((lead, rows, cols))

    def body(c_ref, f_ref, p_ref, o_ref):
        o_ref[...] = (f_ref[...].astype(F32) + p_ref[...].astype(F32)).astype(o_ref.dtype)

    out = _call(
        body, name=name,
        grid_spec=pltpu.PrefetchScalarGridSpec(
            num_scalar_prefetch=1, grid=(lead, nr),
            in_specs=[pl.BlockSpec((1, tr, cols), lambda b, r, c_ref: (b, c_ref[0] * nr + r, 0)),
                      pl.BlockSpec((1, tr, cols), lambda b, r, c_ref: (b, r, 0))],
            out_specs=pl.BlockSpec((1, tr, cols), lambda b, r, c_ref: (b, r, 0))),
        out_shape=_sds((lead, rows, cols), part.dtype),
        compiler_params=_params("arbitrary", "arbitrary"),
    )(cidx, f3, p3)
    return out.reshape(shape)


def _to_other_chips(arrays, blocked):
    def src(ins, outs, pos, a, mask):
        return ins[a].at[_chip(pos) ^ mask] if blocked[a] else ins[a]

    outs = [_sds((3,) + (a.shape[1:] if b else a.shape), a.dtype) for a, b in zip(arrays, blocked)]
    copies = []
    for mi, mask in enumerate(CHIP_MASKS):
        for a in range(len(arrays)):
            copies.append((functools.partial(src, a=a, mask=mask), lambda ins, outs, pos, a=a, mi=mi: outs[a].at[mi],
                           functools.partial(_other_chip, mask=mask)))
    return outs, copies


def _add_chips(name, own, got, jidx, blocked):
    rows, cols = got.shape[-2:]
    tr = rows // 2 if rows % 16 == 0 else rows
    nr = rows // tr
    o3 = own if blocked else own.reshape((1, rows, cols))

    def body(j_ref, o_ref, g_ref, out_ref):
        out_ref[...] = ((o_ref[0].astype(F32) + g_ref[0].astype(F32))
                        + (g_ref[1].astype(F32) + g_ref[2].astype(F32)))

    own_map = (lambda r, j_ref: (j_ref[0], r, 0)) if blocked else (lambda r, j_ref: (0, r, 0))
    return _call(
        body, name=name,
        grid_spec=pltpu.PrefetchScalarGridSpec(
            num_scalar_prefetch=1, grid=(nr,),
            in_specs=[pl.BlockSpec((1, tr, cols), own_map),
                      pl.BlockSpec((3, tr, cols), lambda r, j_ref: (0, r, 0))],
            out_specs=pl.BlockSpec((tr, cols), lambda r, j_ref: (r, 0))),
        out_shape=_sds((rows, cols)),
        compiler_params=_params("arbitrary"),
    )(jidx, o3, got)


def _to_sibling(name, arrays):
    phase = [(lambda ins, outs, pos, a=a: ins[a], lambda ins, outs, pos, a=a: outs[a], _sibling)
             for a in range(len(arrays))]
    return _exchange(name, arrays, [_sds(a.shape, a.dtype) for a in arrays], [phase])


def _local_step(x, target, w_pad, w_out, conv_w, norm_w, pool_w, pool_scale, a_log, dt_bias, dn_norm_w, final_norm_w):
    proj, n_t = _proj_fwd(x, norm_w, w_pad)
    y_pool = _pool_fwd(proj, pool_w, pool_scale)
    qn, kn, vs, beta, g = _conv_fwd(proj, conv_w, a_log, dt_bias)
    w, att, qd, kd, tm, cd, o, vn, st = _delta_fwd(qn, kn, vs, beta, g)
    w_out = w_out(o) if callable(w_out) else w_out
    y_t, dh, dyp, do, ddz, loss, g_fnw, g_dnw = _out_fwd_bwd(x, y_pool, o, proj, target, w_out, dn_norm_w, final_norm_w)
    g_wout = _token_matmul("grad_w_out", y_t, [(dh, 0), (dh, 1)])
    dpu, dpz, g_pw, g_ps = _pool_bwd(proj, dyp, pool_w, pool_scale)
    du, dw, datt, dqd, dkd, dcd = _scan_bwd(do, vn, qd, kd, w, att, cd, st)
    dqn, dkn, dvs, dbeta, dg = _intra_bwd(qn, kn, vs, beta, g, tm, du, dw, datt, dqd, dkd, dcd)
    dcq, dck, dcv, dba, g_cw, g_sm = _conv_bwd(proj, conv_w, a_log, dt_bias, dqn, dkn, dvs, dbeta, dg)
    pieces = [dpu, dpz, dcq, dck, dcv, ddz, dba]
    g_win = _grad_w_in(n_t, pieces)
    small = dict(norm_w=jnp.zeros_like(norm_w), pool_w=g_pw, pool_scale=g_ps, conv_w=g_cw[:CONV_K],
                 a_log=g_sm[0:1, 0:N_HEADS], dt_bias=g_sm[0:1, N_HEADS:2 * N_HEADS], dn_norm_w=g_dnw, final_norm_w=g_fnw)
    return loss[0, 0], g_win, g_wout, small, dh, pieces


def _pack_small(t):
    lanes = lambda a: jnp.pad(a.reshape(1, -1), ((0, 0), (0, HEAD - a.size)))
    rows = [t["pool_w"].reshape(-1, HEAD), t["norm_w"].reshape(-1, HEAD), t["final_norm_w"].reshape(-1, HEAD),
            t["pool_scale"].reshape(-1, HEAD), t["conv_w"].reshape(-1, HEAD), t["dn_norm_w"].reshape(1, HEAD),
            lanes(t["a_log"]), lanes(t["dt_bias"]), lanes(t.get("loss", jnp.zeros((1,), F32)))]
    buf = jnp.concatenate(rows, axis=0)
    return jnp.pad(buf, ((0, SMALL_ROWS - buf.shape[0]), (0, 0)))


def _unpack_small(buf, conv_cols):
    out, r = {}, 0
    for name, nrow, shape in (("pool_w", 512, (1, N_HEADS, HEAD, HEAD)), ("norm_w", 8, (1, D_MODEL)),
                              ("final_norm_w", 8, (D_MODEL,)), ("pool_scale", 4, (1, D_HALF)),
                              ("conv_w", CONV_K * conv_cols // HEAD, (1, CONV_K, conv_cols)), ("dn_norm_w", 1, (1, HEAD))):
        out[name] = buf[r:r + nrow].reshape(shape)
        r += nrow
    out["a_log"] = buf[r:r + 1, :N_HEADS]
    out["dt_bias"] = buf[r + 1:r + 2, :N_HEADS]
    out["loss"] = buf[r + 2, 0]
    return out


def kernel(x, norm_w, w_in, pool_w, pool_scale, conv_w, a_log, dt_bias, dn_norm_w, w_out, final_norm_w, loss_target, m_norm_w, m_w_in, m_pool_w, m_pool_scale, m_conv_w, m_a_log, m_dt_bias, m_dn_norm_w, m_w_out, m_final_norm_w, v_norm_w, v_w_in, v_pool_w, v_pool_scale, v_conv_w, v_a_log, v_dt_bias, v_dn_norm_w, v_w_out, v_final_norm_w):
    cidx = lax.axis_index("c").astype(I32).reshape(1)
    jidx = (2 * lax.axis_index("x") + lax.axis_index("y")).astype(I32)

    wb = jnp.pad(w_in[0].astype(BF16), ((0, 0), (0, BLK_IN_PAD - BLK_IN)))
    ob = w_out[0].astype(BF16)
    gw, gc = _gather_weights(wb, conv_w[0])
    mine = lambda j: jidx == j
    w_pad = jnp.concatenate([jnp.where(mine(j), wb[:, :BLK_IN], gw[j, :, :BLK_IN]) for j in range(4)]
                            + [jnp.zeros((D_MODEL, N_IN_PAD - N_IN), BF16)], axis=1)
    cw_full = jnp.concatenate([jnp.where(mine(j), conv_w[0], gc[j]) for j in range(4)], axis=1)

    lands_o, copies_o = _gather_blocks(ob)
    sems_o, ob_thru, zones_o, token_o = _exchange_start("gather_w_out_start", [ob], lands_o, copies_o)

    def w_out_full(after):
        (own,), (got,) = _exchange_wait("gather_w_out_wait", sems_o, ob_thru, zones_o, copies_o, after)
        return jnp.where((jnp.arange(4) == jidx)[:, None, None], own[None], got).reshape(D_MODEL, D_MODEL)

    loss, g_win, g_wout, small, dh, pieces = _local_step(
        x[0], loss_target[0], w_pad, w_out_full, cw_full, norm_w + token_o[0, 0], pool_w[0], pool_scale, a_log, dt_bias,
        dn_norm_w, final_norm_w.reshape(1, D_MODEL))
    small["loss"] = loss

    blocks_out = g_wout.reshape(4, BLK_OUT, D_MODEL)
    full = [g_win, blocks_out, _pack_small(small)]
    from_sib = _to_sibling_half("reduce_sibling", full)
    chip_sum = [_add_half("add_sibling_%d" % i, f, p, cidx) for i, (f, p) in enumerate(zip(full, from_sib))]
    blocked = [True, True, False]
    lands, copies = _to_other_chips(chip_sum, blocked)
    sems, chip_sum, zones, token = _exchange_start("reduce_chips_start", chip_sum, lands, copies)
    gx, g_nw = _in_bwd(x[0], dh, norm_w + token[0, 0], w_pad, pieces)
    g_nw = _allreduce_tile("reduce_norm_w", g_nw.reshape(8, HEAD)).reshape(1, D_MODEL)
    chip_sum, from_chips = _exchange_wait("reduce_chips_wait", sems, chip_sum, zones, copies, gx)
    halves = [_add_chips("add_chips_%d" % i, o, g, jidx.reshape(1), b)
              for i, (o, g, b) in enumerate(zip(chip_sum, from_chips, blocked))]
    other_halves = _to_sibling("swap_halves", halves)

    weights = dict(norm_w=norm_w, w_in=w_in, pool_w=pool_w, pool_scale=pool_scale, conv_w=conv_w, a_log=a_log,
                   dt_bias=dt_bias, dn_norm_w=dn_norm_w, w_out=w_out, final_norm_w=final_norm_w)
    ms = dict(norm_w=m_norm_w, w_in=m_w_in, pool_w=m_pool_w, pool_scale=m_pool_scale, conv_w=m_conv_w, a_log=m_a_log,
              dt_bias=m_dt_bias, dn_norm_w=m_dn_norm_w, w_out=m_w_out, final_norm_w=m_final_norm_w)
    vs = dict(norm_w=v_norm_w, w_in=v_w_in, pool_w=v_pool_w, pool_scale=v_pool_scale, conv_w=v_conv_w, a_log=v_a_log,
              dt_bias=v_dt_bias, dn_norm_w=v_dn_norm_w, w_out=v_w_out, final_norm_w=v_final_norm_w)
    names = ["norm_w", "w_in", "pool_w", "pool_scale", "conv_w", "a_log", "dt_bias", "dn_norm_w", "w_out", "final_norm_w"]
    small_names = [n for n in names if n not in ("w_in", "w_out")]

    def pack(t):
        conv = lax.dynamic_update_slice_in_dim(jnp.zeros((CONV_K, 3 * D_HALF), F32), t["conv_w"][0], jidx * BLK_CONV, axis=1)
        return _pack_small({**{n: t[n] for n in small_names if n != "conv_w"}, "conv_w": conv})[None]

    results = [{}, {}, {}, {}]
    to_tiles = lambda a: jnp.transpose(a, (2, 0, 1)).reshape(BLK_IN, 8, HEAD)
    from_tiles = lambda a: jnp.transpose(a, (1, 2, 0)).reshape(1, D_MODEL, BLK_IN)
    lo = jnp.where(cidx[0] == 0, halves[0], other_halves[0])
    hi = jnp.where(cidx[0] == 0, other_halves[0], halves[0])
    g_tiles = jnp.concatenate([lo[:, :BLK_IN].T, hi[:, :BLK_IN].T], axis=1).reshape(BLK_IN, 8, HEAD)
    outs = _adamw_tiles("adamw_w_in", to_tiles(w_in), g_tiles, to_tiles(m_w_in), to_tiles(v_w_in))
    for res, o in zip(results, (g_tiles,) + tuple(outs)):
        res["w_in"] = from_tiles(o)
    outs = _adamw_shard("adamw_w_out", w_out, halves[1], other_halves[1], cidx, m_w_out, v_w_out)
    for res, o in zip(results, outs):
        res["w_out"] = o
    outs = _adamw_shard("adamw_small", pack(weights), halves[2], other_halves[2], cidx, pack(ms), pack(vs))
    for res, o in zip(results, outs):
        got = _unpack_small(o[0], 3 * D_HALF)
        got["conv_w"] = lax.dynamic_slice_in_dim(got["conv_w"], jidx * BLK_CONV, BLK_CONV, axis=2)
        res.update(got)
    one_tile = lambda a: a.reshape(1, 8, HEAD)
    outs = _adamw_tiles("adamw_norm_w", one_tile(norm_w), one_tile(g_nw), one_tile(m_norm_w), one_tile(v_norm_w))
    for res, o in zip(results, (g_nw,) + tuple(outs)):
        res["norm_w"] = o.reshape(1, D_MODEL)
    grads, delta, new_m, new_v = results

    return (grads["loss"], gx[None], *[grads[n] for n in names], *[delta[n] for n in names],
            *[new_m[n] for n in names], *[new_v[n] for n in names])
```

```python
import functools

import jax
import jax.numpy as jnp
import numpy as np
from jax import lax
from jax.experimental import pallas as pl
from jax.experimental.pallas import tpu as pltpu

F32 = jnp.float32
BF16 = jnp.bfloat16
I32 = jnp.int32

D_MODEL = 1024
D_HALF = 512
N_HEADS = 4
HEAD = 128
CHUNK = 64
PAIR = 2 * CHUNK
WINDOWS = (2, 4, 8, 16)
CONV_K = 4
EPS = 1e-6
N_IN = 3080
N_IN_PAD = 3200
BLK_IN = 770
BLK_IN_PAD = 896
BLK_OUT = 256
BLK_CONV = 384
COL_BA = 3072
QK_SCALE = HEAD ** -0.5
SMALL_ROWS = 592
VMEM_LIMIT = 56 * 1024 * 1024

ADAM_LR = 0.001
ADAM_B1 = 0.9
ADAM_B2 = 0.999
ADAM_EPS = 1e-08
ADAM_WD = 0.01
ADAM_STEP = 10

CHIP_MASKS = (2, 1, 3)
HEADS = range(N_HEADS)
HEAD_COLS = [slice(h * HEAD, (h + 1) * HEAD) for h in HEADS]


def _call(body, **kw):
    return pl.pallas_call(body, **kw)


def _params(*sem):
    return pltpu.CompilerParams(dimension_semantics=sem, vmem_limit_bytes=VMEM_LIMIT)


def _sds(shape, dtype=F32):
    return jax.ShapeDtypeStruct(shape, dtype)


def _bdot(a, b):
    return jnp.dot(a.astype(BF16), b.astype(BF16), preferred_element_type=F32)


def _bdot_nt(a, b):
    return lax.dot_general(a.astype(BF16), b.astype(BF16), (((1,), (1,)), ((), ())), preferred_element_type=F32)


def _bdot_tn(a, b):
    return lax.dot_general(a.astype(BF16), b.astype(BF16), (((0,), (0,)), ((), ())), preferred_element_type=F32)


def _split(a):
    hi = a.astype(BF16)
    lo = (a - hi.astype(F32)).astype(BF16)
    return hi, lo


def _mask_dot(m, b, dims=(((1,), (0,)), ((), ()))):
    bh, bl = _split(b)
    dg = functools.partial(lax.dot_general, dimension_numbers=dims, preferred_element_type=F32)
    return dg(m, bh) + dg(m, bl)


def _sigmoid(x):
    return 0.5 * jnp.tanh(0.5 * x) + 0.5


def _softplus(x):
    return jnp.maximum(x, 0.0) + jnp.log(1.0 + jnp.exp(-jnp.abs(x)))


def _rowsum(x):
    return jnp.sum(x, axis=-1, keepdims=True)


def _colsum(x):
    return jnp.sum(x, axis=0, keepdims=True)


def _shift_down(xv, prev8, k):
    r = pltpu.roll(xv, k, 0)
    q = pltpu.roll(prev8, k, 0)
    row = lax.broadcasted_iota(I32, prev8.shape, 0)
    top = jnp.where(row < k, q, r[0:8])
    return jnp.concatenate([top, r[8:]], axis=0)


def _shift_up(xv, next8, k):
    t = xv.shape[0]
    r = pltpu.roll(xv, t - k, 0)
    q = pltpu.roll(next8, 8 - k, 0)
    row = lax.broadcasted_iota(I32, next8.shape, 0)
    bot = jnp.where(row >= 8 - k, q, r[t - 8:])
    return jnp.concatenate([r[:t - 8], bot], axis=0)


def _band(rows, cols, off, w, anti=False):
    r = lax.broadcasted_iota(I32, (rows, cols), 0)
    c = lax.broadcasted_iota(I32, (rows, cols), 1)
    d = (c - r + off) if anti else (r - c + off)
    return ((d >= 0) & (d < w)).astype(BF16)


def _head(ref_or_val, h):
    return ref_or_val[:, h * HEAD:(h + 1) * HEAD]


INTRA_PAIRS = 2
UNITS = [(pp, h) for pp in range(INTRA_PAIRS) for h in HEADS]


def _heads(ref, rows=PAIR):
    return [ref[pp * rows:(pp + 1) * rows, HEAD_COLS[h]] for pp, h in UNITS]


def _put_heads(ref, vals, rows=PAIR):
    for (pp, h), v in zip(UNITS, vals):
        ref[pp * rows:(pp + 1) * rows, HEAD_COLS[h]] = v.astype(ref.dtype)


def _each(fn, *lists):
    return [fn(*args) for args in zip(*lists)]


def _proj_fwd(x, norm_w, w_pad):
    s = x.shape[0]
    tm = 512

    def body(x_ref, nw_ref, w_ref, proj_ref, nt_ref):
        xv = x_ref[...]
        r = lax.rsqrt(jnp.mean(xv * xv, axis=-1, keepdims=True) + EPS)
        nv = xv * r * nw_ref[...]
        nt_ref[...] = nv.T.astype(BF16)
        proj_ref[...] = jnp.dot(nv.astype(BF16), w_ref[...], preferred_element_type=F32)

    return _call(
        body, name="proj_fwd", grid=(s // tm,),
        in_specs=[pl.BlockSpec((tm, D_MODEL), lambda i: (i, 0)),
                  pl.BlockSpec((1, D_MODEL), lambda i: (0, 0)),
                  pl.BlockSpec((D_MODEL, N_IN_PAD), lambda i: (0, 0))],
        out_specs=[pl.BlockSpec((tm, N_IN_PAD), lambda i: (i, 0)),
                   pl.BlockSpec((D_MODEL, tm), lambda i: (0, i))],
        out_shape=[_sds((s, N_IN_PAD)), _sds((D_MODEL, s), BF16)],
        compiler_params=_params("arbitrary"),
    )(x, norm_w, w_pad)


def _pool_bands(t, anti=False):
    r = np.arange(t)[:, None]
    c = np.arange(t + HEAD)[None, :]
    d = (c - r) if anti else (r - c + HEAD)
    return jnp.asarray(np.stack([(d >= 0) & (d < w) for w in WINDOWS]), BF16)


def _pool_mix(u, halo, z, pw, bands, row0):
    t = u[0].shape[0]
    rows = row0 + lax.broadcasted_iota(I32, (t, 1), 0) + 1
    cnt = [jnp.minimum(rows, w).astype(F32) for w in WINDOWS]
    win = _each(lambda b, h, v: _mask_dot(b, jnp.concatenate([h, v], axis=0)), bands, halo, u)
    mix = _each(lambda a, c, v: a / c - v, win, cnt, u)
    mixed = _each(_bdot, mix, pw)
    return mix, mixed, _each(_sigmoid, z), cnt


POOL_T = 256


def _pool_fwd(proj, pool_w, pool_scale):
    s = proj.shape[0]
    t = POOL_T
    hb = t // HEAD

    def body(u_ref, z_ref, halo_ref, pw_ref, ps_ref, band_ref, y_ref):
        i = pl.program_id(0)
        live = (i > 0).astype(F32)
        groups = lambda ref: [ref[:, sl] for sl in HEAD_COLS]
        z = groups(z_ref)
        _, mixed, sg, _ = _pool_mix(groups(u_ref), [h * live for h in groups(halo_ref)], z,
                                    [pw_ref[g] for g in HEADS], [band_ref[g] for g in HEADS], i * t)
        for sl, m, zg, s_ in zip(HEAD_COLS, mixed, z, sg):
            y_ref[:, sl] = m * ps_ref[:, sl] * (zg * s_)

    return _call(
        body, name="pool_fwd", grid=(s // t,),
        in_specs=[pl.BlockSpec((t, D_HALF), lambda i: (i, 0)),
                  pl.BlockSpec((t, D_HALF), lambda i: (i, 1)),
                  pl.BlockSpec((HEAD, D_HALF), lambda i: (jnp.maximum(i * hb - 1, 0), 0)),
                  pl.BlockSpec((N_HEADS, HEAD, HEAD), lambda i: (0, 0, 0)),
                  pl.BlockSpec((1, D_HALF), lambda i: (0, 0)),
                  pl.BlockSpec((N_HEADS, t, HEAD + t), lambda i: (0, 0, 0))],
        out_specs=pl.BlockSpec((t, D_HALF), lambda i: (i, 0)),
        out_shape=_sds((s, D_HALF)),
        compiler_params=_params("arbitrary"),
    )(proj, proj, proj, pool_w, pool_scale, _pool_bands(t))


def _conv_taps(xv, prev8):
    return [_shift_down(xv, prev8, CONV_K - 1 - j) for j in range(CONV_K - 1)] + [xv]


def _conv_pre(taps, cw):
    y = taps[CONV_K - 1] * cw[CONV_K - 1:CONV_K]
    for j in range(CONV_K - 2, -1, -1):
        y = y + taps[j] * cw[j:j + 1]
    return y


CONV_T = 256
CONV_SUB = 256


def _conv_specs(t, tile_of=lambda i: i):
    tiles = [pl.BlockSpec((t, D_HALF), functools.partial(lambda i, p: (tile_of(i), 2 + p), p=p)) for p in range(3)]
    halos = [pl.BlockSpec((8, D_HALF),
                          functools.partial(lambda i, p: (jnp.maximum(tile_of(i) * (t // 8) - 1, 0), 2 + p), p=p))
             for p in range(3)]
    return tiles + halos


def _conv_fwd(proj, conv_w, a_log, dt_bias):
    s = proj.shape[0]
    t = CONV_T

    def body(q_ref, k_ref, v_ref, hq_ref, hk_ref, hv_ref, ba_ref, cw_ref, al_ref, dtb_ref,
             qn_ref, kn_ref, vs_ref, beta_ref, g_ref):
        live = (pl.program_id(0) > 0).astype(F32)
        parts = ((q_ref, hq_ref, qn_ref), (k_ref, hk_ref, kn_ref), (v_ref, hv_ref, vs_ref))

        def sub_tile(r0, first):
            rows = pl.ds(r0, CONV_SUB)
            for p, (x_ref, h_ref, o_ref) in enumerate(parts):
                for h in HEADS:
                    cs = HEAD_COLS[h]
                    prev8 = h_ref[:, cs] * live if first else x_ref[pl.ds(r0 - 8, 8), cs]
                    y = _conv_pre(_conv_taps(x_ref[rows, cs], prev8), cw_ref[:, p * D_HALF + h * HEAD:p * D_HALF + (h + 1) * HEAD])
                    sv = y * _sigmoid(y)
                    o_ref[rows, cs] = sv if p == 2 else sv * lax.rsqrt(_rowsum(sv * sv) + EPS)
            ba = ba_ref[rows, :]
            for h in HEADS:
                beta = _sigmoid(ba[:, h:h + 1])
                gl = -jnp.exp(al_ref[0:1, h:h + 1]) * _softplus(ba[:, N_HEADS + h:N_HEADS + h + 1] + dtb_ref[0:1, h:h + 1])
                beta_ref[rows, HEAD_COLS[h]] = jnp.broadcast_to(beta, (CONV_SUB, HEAD))
                g_ref[rows, HEAD_COLS[h]] = jnp.broadcast_to(gl, (CONV_SUB, HEAD))

        sub_tile(0, True)

        def step(k, carry):
            sub_tile(pl.multiple_of(k * CONV_SUB, CONV_SUB), False)
            return carry

        lax.fori_loop(1, t // CONV_SUB, step, 0)

    row = pl.BlockSpec((t, D_HALF), lambda i: (i, 0))
    return _call(
        body, name="conv_fwd", grid=(s // t,),
        in_specs=_conv_specs(t) + [pl.BlockSpec((t, HEAD), lambda i: (i, COL_BA // HEAD)),
                                   pl.BlockSpec((CONV_K, 3 * D_HALF), lambda i: (0, 0)),
                                   pl.BlockSpec((1, N_HEADS), lambda i: (0, 0)),
                                   pl.BlockSpec((1, N_HEADS), lambda i: (0, 0))],
        out_specs=[row] * 5,
        out_shape=[_sds((s, D_HALF))] * 5,
        compiler_params=_params("arbitrary"),
    )(proj, proj, proj, proj, proj, proj, proj, conv_w, a_log, dt_bias)


def _pair_masks():
    r = lax.broadcasted_iota(I32, (PAIR, PAIR), 0)
    c = lax.broadcasted_iota(I32, (PAIR, PAIR), 1)
    same = jnp.right_shift(r, 6) == jnp.right_shift(c, 6)
    return same, same & (r >= c), same & (r > c), r == c


def _run(stages):
    for _ in stages:
        pass


def _interleave(*stage_lists):
    live = list(stage_lists)
    while live:
        for gen in list(live):
            try:
                next(gen)
            except StopIteration:
                live.remove(gen)


def _pair_common_stages(cm, qn, kn, vs, beta, g):
    same, incl, strict, eye = _pair_masks()
    incl_b = incl.astype(BF16)
    first = lax.broadcasted_iota(I32, (PAIR, HEAD), 0) < CHUNK
    cm.update(same=same, incl=incl, strict=strict, eye=eye)
    gc = _each(lambda gv: _mask_dot(incl_b, gv), g)
    q = _each(lambda v: v * QK_SCALE, qn)
    kb = _each(lambda k, b: k * b, kn, beta)
    cm.update(gc=gc, q=q, kb=kb, vb=_each(lambda v, b: v * b, vs, beta))
    yield
    cm.update(kk=_each(_bdot_nt, kb, kn), qk=_each(_bdot_nt, q, kn))
    gc_row = _each(lambda v: _colsum(jnp.where(eye, v, 0.0)), gc)
    gl = _each(lambda v: jnp.where(first, v[CHUNK - 1:CHUNK], v[PAIR - 1:PAIR]), gc)
    egc = _each(jnp.exp, gc)
    cm.update(gl=gl, egc=egc,
              decay=_each(lambda v, r: jnp.where(incl, jnp.exp(jnp.where(incl, v - r, 0.0)), 0.0), gc, gc_row))
    yield
    cm.update(ekd=_each(lambda a, b: jnp.exp(a - b), gl, gc), cd=_each(jnp.exp, gl),
              kbg=_each(lambda k, e: k * e, kb, egc))
    yield


def _pair_common(qn, kn, vs, beta, g):
    cm = {}
    _run(_pair_common_stages(cm, qn, kn, vs, beta, g))
    return cm


def _tri_inv_stages(out, a, eye_f):
    p = _each(lambda v: eye_f - v, a)
    x = _each(_bdot, a, a)
    yield
    for it in range(5):
        p = _each(lambda pv, xv: pv + _bdot(pv, xv), p, x)
        if it < 4:
            x = _each(_bdot, x, x)
        yield
    out["t"] = p


def _tri_inv(a, eye_f):
    out = {}
    _run(_tri_inv_stages(out, a, eye_f))
    return out["t"]


def _pair_spec():
    return pl.BlockSpec((INTRA_PAIRS * PAIR, D_HALF), lambda i: (i, 0))


def _chunk_scalar_spec(pairs=1, index=lambda i: (i, 0)):
    return pl.BlockSpec((16 * pairs, D_HALF), index)


SCAN_PAIRS = 2
SCAN_ROWS = SCAN_PAIRS * PAIR


def _intra_fwd(qn, kn, vs, beta, g):
    s = qn.shape[0]

    def body(qn_ref, kn_ref, vs_ref, beta_ref, g_ref, u_ref, w_ref, att_ref, qd_ref, kd_ref, t_ref, cd_ref):
        kn = _heads(kn_ref)
        cm = _pair_common(_heads(qn_ref), kn, _heads(vs_ref), _heads(beta_ref), _heads(g_ref))
        a = _each(lambda kk, d: jnp.where(cm["strict"], kk * d, 0.0), cm["kk"], cm["decay"])
        tm = _tri_inv(a, cm["eye"].astype(F32))
        _put_heads(t_ref, tm)
        _put_heads(u_ref, _each(_bdot, tm, cm["vb"]))
        _put_heads(w_ref, _each(_bdot, tm, cm["kbg"]))
        _put_heads(att_ref, _each(lambda a, b: a * b, cm["qk"], cm["decay"]))
        _put_heads(qd_ref, _each(lambda a, b: a * b, cm["q"], cm["egc"]))
        _put_heads(kd_ref, _each(lambda a, b: a * b, kn, cm["ekd"]))
        for ci in range(2):
            for (pp, h), v in zip(UNITS, cm["cd"]):
                cd_ref[pp * 16 + ci * 8:pp * 16 + (ci + 1) * 8, HEAD_COLS[h]] = v[ci * CHUNK:ci * CHUNK + 8]

    return _call(
        body, name="intra_fwd", grid=(s // (INTRA_PAIRS * PAIR),),
        in_specs=[_pair_spec()] * 5, out_specs=[_pair_spec()] * 6 + [_chunk_scalar_spec(INTRA_PAIRS)],
        out_shape=[_sds((s, D_HALF))] + [_sds((s, D_HALF), BF16)] * 5 + [_sds((s // 8, D_HALF))],
        compiler_params=_params("arbitrary"),
    )(qn, kn, vs, beta, g)


def _scan_fwd(u, w, att, qd, kd, cd):
    s = u.shape[0]
    n_chunks = s // CHUNK

    def body(u_ref, w_ref, att_ref, qd_ref, kd_ref, cd_ref, o_ref, vn_ref, st_ref, state):
        @pl.when(pl.program_id(0) == 0)
        def _():
            state[...] = jnp.zeros_like(state)
        cols = list(enumerate(HEAD_COLS))
        sm = [state[h] for h in HEADS]
        for ci in range(2 * SCAN_PAIRS):
            rs = slice(ci * CHUNK, (ci + 1) * CHUNK)
            for h in HEADS:
                st_ref[ci, h] = sm[h]
            both = [_bdot(jnp.concatenate([w_ref[rs, sl], qd_ref[rs, sl]], axis=0), sm[h]) for h, sl in cols]
            vn = [u_ref[rs, sl] - both[h][:CHUNK] for h, sl in cols]
            for h, sl in cols:
                vn_ref[rs, sl] = vn[h].astype(BF16)
                o_ref[rs, sl] = both[h][CHUNK:]
            sm = [sm[h] * cd_ref[ci * 8:ci * 8 + 1, sl] + _bdot_tn(kd_ref[rs, sl], vn[h]) for h, sl in cols]
        for h in HEADS:
            state[h] = sm[h]
        for pp in range(SCAN_PAIRS):
            rp = slice(pp * PAIR, (pp + 1) * PAIR)
            intra = [_bdot(att_ref[rp, sl], vn_ref[rp, sl]) for sl in HEAD_COLS]
            for h, sl in cols:
                o_ref[rp, sl] += intra[h]

    rows = pl.BlockSpec((SCAN_ROWS, D_HALF), lambda i: (i, 0))
    return _call(
        body, name="scan_fwd", grid=(s // SCAN_ROWS,),
        in_specs=[rows] * 5 + [_chunk_scalar_spec(SCAN_PAIRS)],
        out_specs=[rows, rows, pl.BlockSpec((2 * SCAN_PAIRS, N_HEADS, HEAD, HEAD), lambda i: (i, 0, 0, 0))],
        out_shape=[_sds((s, D_HALF)), _sds((s, D_HALF), BF16), _sds((n_chunks, N_HEADS, HEAD, HEAD))],
        scratch_shapes=[pltpu.VMEM((N_HEADS, HEAD, HEAD), F32)],
        compiler_params=_params("arbitrary"),
    )(u, w, att, qd, kd, cd)


def _delta_fwd(qn, kn, vs, beta, g):
    s = qn.shape[0]
    n_steps = s // SCAN_ROWS
    n_chunks = s // CHUNK
    assert INTRA_PAIRS == SCAN_PAIRS

    def body(qn_ref, kn_ref, vs_ref, beta_ref, g_ref, w_ref, att_ref, qd_ref, kd_ref, t_ref, cd_ref, o_ref, vn_ref, st_ref,
             state, u_s, w_s, att_s, qd_s, kd_s, cd_s):
        t = pl.program_id(0)

        @pl.when(t <= 1)
        def _():
            state[...] = jnp.zeros_like(state)

        @pl.when(t == 0)
        def _():
            for ref in (u_s, w_s, att_s, qd_s, kd_s, cd_s):
                ref[1] = jnp.zeros(ref.shape[1:], ref.dtype)

        cur = lax.rem(t, 2)
        prev = 1 - cur
        cols = list(enumerate(HEAD_COLS))

        def recurrence():
            sm = [state[h] for h in HEADS]
            for ci in range(2 * SCAN_PAIRS):
                rs = slice(ci * CHUNK, (ci + 1) * CHUNK)
                for h in HEADS:
                    st_ref[ci, h] = sm[h]
                both = [_bdot(jnp.concatenate([w_s[prev, rs, sl], qd_s[prev, rs, sl]], axis=0), sm[h]) for h, sl in cols]
                vn = [u_s[prev, rs, sl] - both[h][:CHUNK] for h, sl in cols]
                for h, sl in cols:
                    vn_ref[rs, sl] = vn[h].astype(BF16)
                    o_ref[rs, sl] = both[h][CHUNK:]
                yield
                sm = [sm[h] * cd_s[prev, ci * 8:ci * 8 + 1, sl] + _bdot_tn(kd_s[prev, rs, sl], vn[h]) for h, sl in cols]
                yield
            for h in HEADS:
                state[h] = sm[h]
            for pp in range(SCAN_PAIRS):
                rp = slice(pp * PAIR, (pp + 1) * PAIR)
                intra = [_bdot(att_s[prev, rp, sl], vn_ref[rp, sl]) for sl in HEAD_COLS]
                for h, sl in cols:
                    o_ref[rp, sl] += intra[h]
                yield

        def factors():
            kn = _heads(kn_ref)
            cm = {}
            yield from _pair_common_stages(cm, _heads(qn_ref), kn, _heads(vs_ref), _heads(beta_ref), _heads(g_ref))
            a = _each(lambda kk, d: jnp.where(cm["strict"], kk * d, 0.0), cm["kk"], cm["decay"])
            inv = {}
            yield from _tri_inv_stages(inv, a, cm["eye"].astype(F32))
            tm = inv["t"]
            res = dict(u=_each(_bdot, tm, cm["vb"]), w=_each(_bdot, tm, cm["kbg"]),
                       att=_each(lambda a, b: a * b, cm["qk"], cm["decay"]),
                       qd=_each(lambda a, b: a * b, cm["q"], cm["egc"]), kd=_each(lambda a, b: a * b, kn, cm["ekd"]))
            yield
            _put_heads(t_ref, tm)
            for key, out, keep in (("w", w_ref, w_s), ("att", att_ref, att_s), ("qd", qd_ref, qd_s), ("kd", kd_ref, kd_s)):
                _put_heads(out, res[key])
                for (pp, h), v in zip(UNITS, res[key]):
                    keep[cur, pp * PAIR:(pp + 1) * PAIR, HEAD_COLS[h]] = v.astype(BF16)
            for (pp, h), v in zip(UNITS, res["u"]):
                u_s[cur, pp * PAIR:(pp + 1) * PAIR, HEAD_COLS[h]] = v
            for ci in range(2):
                for (pp, h), v in zip(UNITS, cm["cd"]):
                    rows8 = slice(pp * 16 + ci * 8, pp * 16 + (ci + 1) * 8)
                    cd_ref[rows8, HEAD_COLS[h]] = v[ci * CHUNK:ci * CHUNK + 8]
                    cd_s[cur, rows8, HEAD_COLS[h]] = v[ci * CHUNK:ci * CHUNK + 8]
            yield

        _interleave(recurrence(), factors())

    last = n_steps - 1
    now = lambda i: (jnp.minimum(i, last), 0)
    before = lambda i: (jnp.maximum(i - 1, 0), 0)
    rows = lambda index: pl.BlockSpec((SCAN_ROWS, D_HALF), index)
    slot = lambda r, dtype: pltpu.VMEM((2, r, D_HALF), dtype)
    return _call(
        body, name="delta_fwd", grid=(n_steps + 1,),
        in_specs=[rows(now)] * 5,
        out_specs=[rows(now)] * 5 + [_chunk_scalar_spec(SCAN_PAIRS, now), rows(before), rows(before),
                                     pl.BlockSpec((2 * SCAN_PAIRS, N_HEADS, HEAD, HEAD), lambda i: (jnp.maximum(i - 1, 0), 0, 0, 0))],
        out_shape=[_sds((s, D_HALF), BF16)] * 5 + [_sds((s // 8, D_HALF)), _sds((s, D_HALF)), _sds((s, D_HALF), BF16),
                                                  _sds((n_chunks, N_HEADS, HEAD, HEAD))],
        scratch_shapes=[pltpu.VMEM((N_HEADS, HEAD, HEAD), F32), slot(SCAN_ROWS, F32), slot(SCAN_ROWS, BF16),
                        slot(SCAN_ROWS, BF16), slot(SCAN_ROWS, BF16), slot(SCAN_ROWS, BF16), slot(16 * SCAN_PAIRS, F32)],
        compiler_params=_params("arbitrary"),
    )(qn, kn, vs, beta, g)


OUT_T = 512


def _out_fwd_bwd(x, y_pool, o, proj, target, w_out, dn_norm_w, final_norm_w):
    s = x.shape[0]
    t = OUT_T

    def body(x_ref, yp_ref, o_ref, z_ref, tg_ref, wo_ref, dnw_ref, fnw_ref,
             yt_ref, dh_ref, dyp_ref, do_ref, dz_ref, loss_ref, gfn_ref, gdn_ref, y_ref):
        @pl.when(pl.program_id(0) == 0)
        def _():
            loss_ref[...] = jnp.zeros_like(loss_ref)
            gfn_ref[...] = jnp.zeros_like(gfn_ref)
            gdn_ref[...] = jnp.zeros_like(gdn_ref)

        ypv = yp_ref[...]
        y_ref[:, :D_HALF] = ypv.astype(BF16)
        yt_ref[:D_HALF, :] = ypv.T.astype(BF16)
        dnw = dnw_ref[...]
        keep = []
        for h in HEADS:
            ov = o_ref[:, HEAD_COLS[h]]
            zv = z_ref[:, HEAD_COLS[h]]
            ro = lax.rsqrt(jnp.mean(ov * ov, axis=-1, keepdims=True) + EPS)
            ohat = ov * ro
            sg = _sigmoid(zv)
            keep.append((ro, ohat, zv, sg))
            ydn = ohat * dnw * (zv * sg)
            y_ref[:, D_HALF + h * HEAD:D_HALF + (h + 1) * HEAD] = ydn.astype(BF16)
            yt_ref[D_HALF + h * HEAD:D_HALF + (h + 1) * HEAD, :] = ydn.T.astype(BF16)

        hv = x_ref[...] + jnp.dot(y_ref[...], wo_ref[...], preferred_element_type=F32)
        r2 = lax.rsqrt(jnp.mean(hv * hv, axis=-1, keepdims=True) + EPS)
        hhat = hv * r2
        fnw = fnw_ref[...]
        err = hhat * fnw - tg_ref[...]
        loss_ref[...] += 0.5 * jnp.sum(_rowsum(err * err) * (1.0 / D_MODEL), axis=0, keepdims=True)
        dout = err * (1.0 / D_MODEL)
        gfn_ref[...] += _colsum(dout * hhat)
        dhh = dout * fnw
        dh = r2 * (dhh - hhat * jnp.mean(dhh * hhat, axis=-1, keepdims=True))
        dh_ref[...] = dh
        dy = _bdot_nt(dh, wo_ref[...])
        dyp_ref[...] = dy[:, :D_HALF]
        gdn = jnp.zeros((1, HEAD), F32)
        for h in HEADS:
            ro, ohat, zv, sg = keep[h]
            dyd = dy[:, D_HALF + h * HEAD:D_HALF + (h + 1) * HEAD]
            sz = zv * sg
            dz_ref[:, HEAD_COLS[h]] = (dyd * ohat * dnw * (sg * (1.0 + zv * (1.0 - sg)))).astype(BF16)
            gdn = gdn + _colsum(dyd * ohat * sz)
            doh = dyd * dnw * sz
            do_ref[:, HEAD_COLS[h]] = ro * (doh - ohat * jnp.mean(doh * ohat, axis=-1, keepdims=True))
        gdn_ref[...] += gdn

    wide = pl.BlockSpec((t, D_MODEL), lambda i: (i, 0))
    half = pl.BlockSpec((t, D_HALF), lambda i: (i, 0))
    const = lambda shape: pl.BlockSpec(shape, lambda i: (0,) * len(shape))
    return _call(
        body, name="out_fwd_bwd", grid=(s // t,),
        in_specs=[wide, half, half, pl.BlockSpec((t, D_HALF), lambda i: (i, 5)), wide,
                  const((D_MODEL, D_MODEL)), const((1, HEAD)), const((1, D_MODEL))],
        out_specs=[pl.BlockSpec((D_MODEL, t), lambda i: (0, i)), wide, half, half, half,
                   const((1, HEAD)), const((1, D_MODEL)), const((1, HEAD))],
        out_shape=[_sds((D_MODEL, s), BF16), _sds((s, D_MODEL)), _sds((s, D_HALF)), _sds((s, D_HALF)), _sds((s, D_HALF), BF16),
                   _sds((1, HEAD)), _sds((1, D_MODEL)), _sds((1, HEAD))],
        scratch_shapes=[pltpu.VMEM((t, D_MODEL), BF16)],
        compiler_params=_params("arbitrary"),
    )(x, y_pool, o, proj, target, w_out, dn_norm_w, final_norm_w)


def _token_matmul(name, at, pieces):
    m, s = at.shape
    n = len(pieces)
    tn, tk = D_HALF, 512

    def body(a_ref, *refs):
        p_refs, o_ref, acc = refs[:n], refs[n], refs[n + 1]

        @pl.when(pl.program_id(0) == 0)
        def _():
            acc[...] = jnp.zeros_like(acc)

        av = a_ref[...]
        for p in range(n):
            acc[:, p * tn:(p + 1) * tn] += _bdot(av, p_refs[p][...])

        @pl.when(pl.program_id(0) == pl.num_programs(0) - 1)
        def _():
            o_ref[...] = acc[...].astype(BF16)

    return _call(
        body, name=name, grid=(s // tk,),
        in_specs=[pl.BlockSpec((m, tk), lambda k: (0, k))]
                 + [pl.BlockSpec((tk, tn), functools.partial(lambda k, cb: (k, cb), cb=cb)) for _, cb in pieces],
        out_specs=pl.BlockSpec((m, n * tn), lambda k: (0, 0)),
        out_shape=_sds((m, n * tn), BF16),
        scratch_shapes=[pltpu.VMEM((m, n * tn), F32)],
        compiler_params=_params("arbitrary"),
    )(at, *[p[0] for p in pieces])


def _grad_w_in(at, pieces):
    m, s = at.shape
    n = len(pieces)
    tn, tk = D_HALF, 512

    def body(a_ref, *refs):
        p_refs, o_ref, acc = refs[:n], refs[n], refs[n + 1]

        @pl.when(pl.program_id(0) == 0)
        def _():
            acc[...] = jnp.zeros_like(acc)

        av = a_ref[...]
        for p in range(n):
            acc[:, p * tn:(p + 1) * tn] += _bdot(av, p_refs[p][...])

        @pl.when(pl.program_id(0) == pl.num_programs(0) - 1)
        def _():
            for j in range(4):
                base = j * BLK_IN // HEAD * HEAD
                win = acc[:, base:base + BLK_IN_PAD]
                if j * BLK_IN > base:
                    win = pltpu.roll(win, BLK_IN_PAD - (j * BLK_IN - base), 1)
                o_ref[j] = win.astype(BF16)

    return _call(
        body, name="grad_w_in", grid=(s // tk,),
        in_specs=[pl.BlockSpec((m, tk), lambda k: (0, k))] + [pl.BlockSpec((tk, tn), lambda k: (k, 0))] * n,
        out_specs=pl.BlockSpec((4, m, BLK_IN_PAD), lambda k: (0, 0, 0)),
        out_shape=_sds((4, m, BLK_IN_PAD), BF16),
        scratch_shapes=[pltpu.VMEM((m, n * tn), F32)],
        compiler_params=_params("arbitrary"),
    )(at, *pieces)


def _pool_bwd(proj, dyp, pool_w, pool_scale):
    s = proj.shape[0]
    t = POOL_T
    hb = t // HEAD
    last = s // HEAD - 1

    def body(u_ref, z_ref, halo_ref, dy_ref, zn_ref, dyn_ref, pw_ref, ps_ref, band_ref, aband_ref,
             du_ref, dz_ref, gpw_ref, gps_ref):
        i = pl.program_id(0)

        @pl.when(i == 0)
        def _():
            gpw_ref[...] = jnp.zeros_like(gpw_ref)
            gps_ref[...] = jnp.zeros_like(gps_ref)

        live = (i > 0).astype(F32)
        more = (i < pl.num_programs(0) - 1).astype(F32)
        groups = lambda ref: [ref[:, sl] for sl in HEAD_COLS]
        z, ps, dy = groups(z_ref), groups(ps_ref), groups(dy_ref)
        pw = [pw_ref[g] for g in HEADS]
        mix, mixed, sg, cnt = _pool_mix(groups(u_ref), [h * live for h in groups(halo_ref)], z, pw,
                                        [band_ref[g] for g in HEADS], i * t)
        sz = _each(lambda a, b: a * b, z, sg)
        for sl, d, m, p, s_, zg in zip(HEAD_COLS, dy, mixed, ps, sg, z):
            dz_ref[:, sl] = (d * m * p * (s_ * (1.0 + zg * (1.0 - s_)))).astype(BF16)
        for sl, d, m, a in zip(HEAD_COLS, dy, mixed, sz):
            gps_ref[:, sl] += _colsum(d * m * a)
        dmixed = _each(lambda d, p, a: d * p * a, dy, ps, sz)
        for g, gp in enumerate(_each(_bdot_tn, mix, dmixed)):
            gpw_ref[g] += gp
        dmix = _each(_bdot_nt, dmixed, pw)
        dmix_n = _each(lambda d, p, zn, w_: _bdot_nt(d * more * p * (zn * _sigmoid(zn)), w_),
                       groups(dyn_ref), ps, groups(zn_ref), pw)
        scaled = [jnp.concatenate([a / c, b * (1.0 / w)], axis=0) for a, c, b, w in zip(dmix, cnt, dmix_n, WINDOWS)]
        du = _each(lambda b, s_, d: _mask_dot(b, s_) - d, [aband_ref[g] for g in HEADS], scaled, dmix)
        for sl, v in zip(HEAD_COLS, du):
            du_ref[:, sl] = v.astype(BF16)

    tile = lambda col: pl.BlockSpec((t, D_HALF), lambda i: (i, col))
    below = lambda col: pl.BlockSpec((HEAD, D_HALF), lambda i: (jnp.minimum((i + 1) * hb, last), col))
    return _call(
        body, name="pool_bwd", grid=(s // t,),
        in_specs=[tile(0), tile(1), pl.BlockSpec((HEAD, D_HALF), lambda i: (jnp.maximum(i * hb - 1, 0), 0)),
                  tile(0), below(1), below(0),
                  pl.BlockSpec((N_HEADS, HEAD, HEAD), lambda i: (0, 0, 0)), pl.BlockSpec((1, D_HALF), lambda i: (0, 0)),
                  pl.BlockSpec((N_HEADS, t, HEAD + t), lambda i: (0, 0, 0)),
                  pl.BlockSpec((N_HEADS, t, HEAD + t), lambda i: (0, 0, 0))],
        out_specs=[tile(0), tile(0), pl.BlockSpec((N_HEADS, HEAD, HEAD), lambda i: (0, 0, 0)),
                   pl.BlockSpec((1, D_HALF), lambda i: (0, 0))],
        out_shape=[_sds((s, D_HALF), BF16), _sds((s, D_HALF), BF16), _sds((N_HEADS, HEAD, HEAD)), _sds((1, D_HALF))],
        compiler_params=_params("arbitrary"),
    )(proj, proj, proj, dyp, proj, dyp, pool_w, pool_scale, _pool_bands(t), _pool_bands(t, anti=True))


def _scan_bwd(do, vn, qd, kd, w, att, cd, st):
    s = do.shape[0]
    n_steps = s // SCAN_ROWS

    def body(do_ref, vn_ref, qd_ref, kd_ref, w_ref, att_ref, cd_ref, st_ref,
             du_ref, dw_ref, datt_ref, dqd_ref, dkd_ref, dcd_ref, dstate):
        @pl.when(pl.program_id(0) == 0)
        def _():
            dstate[...] = jnp.zeros_like(dstate)
        _, incl, _, _ = _pair_masks()
        cols = list(enumerate(HEAD_COLS))
        dv_intra = []
        for pp in range(SCAN_PAIRS):
            rp = slice(pp * PAIR, (pp + 1) * PAIR)
            dv_intra.append([_bdot_tn(att_ref[rp, sl], do_ref[rp, sl]) for _, sl in cols])
            for _, sl in cols:
                datt_ref[rp, sl] = jnp.where(incl, _bdot_nt(do_ref[rp, sl], vn_ref[rp, sl]), 0.0)
        ds = [dstate[h] for h in HEADS]
        for ci in range(2 * SCAN_PAIRS - 1, -1, -1):
            rs = slice(ci * CHUNK, (ci + 1) * CHUNK)
            in_pair = slice((ci % 2) * CHUNK, (ci % 2 + 1) * CHUNK)
            sm = [st_ref[ci, h] for h in HEADS]
            dvn = [dv_intra[ci // 2][h][in_pair] + _bdot(kd_ref[rs, sl], ds[h]) for h, sl in cols]
            for h, sl in cols:
                du_ref[rs, sl] = dvn[h].astype(BF16)
            dqd = [_bdot_nt(do_ref[rs, sl], sm[h]) for h, sl in cols]
            dw = [-_bdot_nt(dvn[h], sm[h]) for h, _ in cols]
            dkd = [_bdot_nt(vn_ref[rs, sl], ds[h]) for h, sl in cols]
            dcd = [jnp.broadcast_to(_rowsum(_colsum(ds[h] * sm[h])), (8, HEAD)) for h in HEADS]
            for h, sl in cols:
                dqd_ref[rs, sl] = dqd[h]
                dw_ref[rs, sl] = dw[h].astype(BF16)
                dkd_ref[rs, sl] = dkd[h]
                dcd_ref[ci * 8:(ci + 1) * 8, sl] = dcd[h]
            ds = [ds[h] * cd_ref[ci * 8:ci * 8 + 1, sl] + _bdot_tn(qd_ref[rs, sl], do_ref[rs, sl])
                  - _bdot_tn(w_ref[rs, sl], dvn[h]) for h, sl in cols]
        for h in HEADS:
            dstate[h] = ds[h]

    rev = pl.BlockSpec((SCAN_ROWS, D_HALF), lambda i: (n_steps - 1 - i, 0))
    rev_scalar = _chunk_scalar_spec(SCAN_PAIRS, lambda i: (n_steps - 1 - i, 0))
    return _call(
        body, name="scan_bwd", grid=(n_steps,),
        in_specs=[rev] * 6 + [rev_scalar,
                              pl.BlockSpec((2 * SCAN_PAIRS, N_HEADS, HEAD, HEAD), lambda i: (n_steps - 1 - i, 0, 0, 0))],
        out_specs=[rev] * 5 + [rev_scalar],
        out_shape=[_sds((s, D_HALF), BF16)] * 2 + [_sds((s, D_HALF))] * 3 + [_sds((s // 8, D_HALF))],
        scratch_shapes=[pltpu.VMEM((N_HEADS, HEAD, HEAD), F32)],
        compiler_params=_params("arbitrary"),
    )(do, vn, qd, kd, w, att, cd, st)


def _intra_bwd(qn, kn, vs, beta, g, tm, du, dw, datt, dqd, dkd, dcd):
    s = qn.shape[0]

    def body(qn_ref, kn_ref, vs_ref, beta_ref, g_ref, t_ref, du_ref, dw_ref, datt_ref, dqd_ref, dkd_ref, dcd_ref,
             dqn_ref, dkn_ref, dvs_ref, dbeta_ref, dg_ref):
        ones = jnp.ones((PAIR, HEAD), BF16)
        tn = (((0,), (0,)), ((), ()))
        kn, vs, beta = _heads(kn_ref), _heads(vs_ref), _heads(beta_ref)
        cm = _pair_common(_heads(qn_ref), kn, vs, beta, _heads(g_ref))
        tmv, duv, dwv, dattv, dqdv, dkdv = (_heads(r) for r in (t_ref, du_ref, dw_ref, datt_ref, dqd_ref, dkd_ref))
        dvb = _each(_bdot_tn, tmv, duv)
        dt = _each(lambda a, b, c, d: _bdot_nt(a, b) + _bdot_nt(c, d), duv, cm["vb"], dwv, cm["kbg"])
        dkbg = _each(_bdot_tn, tmv, dwv)
        m1 = _each(_bdot_tn, tmv, dt)
        da = _each(lambda a, b: -jnp.where(cm["strict"], _bdot_nt(a, b), 0.0), m1, tmv)
        dkk = _each(lambda a, b: a * b, da, cm["decay"])
        dqk = _each(lambda a, b: a * b, dattv, cm["decay"])
        dd = _each(lambda a, b, c, d: a * b + c * d, dkk, cm["kk"], dqk, cm["qk"])
        dkb = _each(lambda a, b, c, d: _bdot(a, b) + c * d, dkk, kn, dkbg, cm["egc"])
        dq = _each(lambda a, b, c, d: _bdot(a, b) + c * d, dqk, kn, dqdv, cm["egc"])
        dkn = _each(lambda a, b, c, d: _bdot_tn(a, b) + _bdot_tn(c, d), dkk, cm["kb"], dqk, cm["q"])
        dkn = _each(lambda a, b, c, d, e: a + b * c + d * e, dkn, dkdv, cm["ekd"], dkb, beta)
        t_kd = _each(lambda a, b, c: _rowsum(a * b * c), dkdv, kn, cm["ekd"])
        split = _each(_split, dd)
        rows_dd = [jnp.dot(hi, ones, preferred_element_type=F32) + jnp.dot(lo, ones, preferred_element_type=F32)
                   for hi, lo in split]
        cols_dd = [lax.dot_general(hi, ones, tn, preferred_element_type=F32)
                   + lax.dot_general(lo, ones, tn, preferred_element_type=F32) for hi, lo in split]
        dgc = _each(lambda r, c, a, b, e, f, k, t: r - c + _rowsum(a * b * e) + _rowsum(f * k) - t,
                    rows_dd, cols_dd, dqdv, cm["q"], cm["egc"], dkbg, cm["kbg"], t_kd)
        same_b = cm["same"].astype(BF16)
        rowi = lax.broadcasted_iota(I32, (PAIR, HEAD), 0)
        dcd = _each(lambda d: jnp.where(rowi < CHUNK, d[0:1], d[8:9]), _heads(dcd_ref, rows=16))
        dgl = _each(lambda t, d, c: _mask_dot(same_b, jnp.broadcast_to(t, (PAIR, HEAD))) + d * c, t_kd, dcd, cm["cd"])
        is_last = jnp.bitwise_and(rowi, CHUNK - 1) == CHUNK - 1
        dgc = _each(lambda a, b: a + jnp.where(is_last, b, 0.0), dgc, dgl)
        r = lax.broadcasted_iota(I32, (PAIR, PAIR), 0)
        c = lax.broadcasted_iota(I32, (PAIR, PAIR), 1)
        upper_b = (cm["same"] & (r <= c)).astype(BF16)
        _put_heads(dg_ref, _each(lambda v: _mask_dot(upper_b, v), dgc))
        _put_heads(dbeta_ref, _each(lambda a, b, c, d: jnp.broadcast_to(_rowsum(a * b) + _rowsum(c * d), (PAIR, HEAD)),
                                    dkb, kn, dvb, vs))
        _put_heads(dqn_ref, _each(lambda v: v * QK_SCALE, dq))
        _put_heads(dkn_ref, dkn)
        _put_heads(dvs_ref, _each(lambda a, b: a * b, dvb, beta))

    return _call(
        body, name="intra_bwd", grid=(s // (INTRA_PAIRS * PAIR),),
        in_specs=[_pair_spec()] * 11 + [_chunk_scalar_spec(INTRA_PAIRS)], out_specs=[_pair_spec()] * 5,
        out_shape=[_sds((s, D_HALF))] * 5,
        compiler_params=_params("arbitrary"),
    )(qn, kn, vs, beta, g, tm, du, dw, datt, dqd, dkd, dcd)


def _delta_bwd(do, vn, qd, kd, w, att, cd, st, qn, kn, vs, beta, g, tm):
    s = do.shape[0]
    n_steps = s // SCAN_ROWS
    assert INTRA_PAIRS == SCAN_PAIRS

    def body(do_ref, vn_ref, qd_ref, kd_ref, w_ref, att_ref, cd_ref, st_ref, qn_ref, kn_ref, vs_ref, beta_ref, g_ref, t_ref,
             dqn_ref, dkn_ref, dvs_ref, dbeta_ref, dg_ref, dstate, du_s, dw_s, datt_s, dqd_s, dkd_s, dcd_s):
        t = pl.program_id(0)

        @pl.when(t == 0)
        def _():
            dstate[...] = jnp.zeros_like(dstate)
            for ref in (du_s, dw_s, datt_s, dqd_s, dkd_s, dcd_s):
                ref[1] = jnp.zeros(ref.shape[1:], ref.dtype)

        cur = lax.rem(t, 2)
        prev = 1 - cur
        cols = list(enumerate(HEAD_COLS))
        _, incl, _, _ = _pair_masks()

        def recurrence():
            dv_intra = []
            for pp in range(SCAN_PAIRS):
                rp = slice(pp * PAIR, (pp + 1) * PAIR)
                dv_intra.append([_bdot_tn(att_ref[rp, sl], do_ref[rp, sl]) for _, sl in cols])
                for _, sl in cols:
                    datt_s[cur, rp, sl] = jnp.where(incl, _bdot_nt(do_ref[rp, sl], vn_ref[rp, sl]), 0.0)
                yield
            ds = [dstate[h] for h in HEADS]
            for ci in range(2 * SCAN_PAIRS - 1, -1, -1):
                rs = slice(ci * CHUNK, (ci + 1) * CHUNK)
                in_pair = slice((ci % 2) * CHUNK, (ci % 2 + 1) * CHUNK)
                sm = [st_ref[ci, h] for h in HEADS]
                dvn = [dv_intra[ci // 2][h][in_pair] + _bdot(kd_ref[rs, sl], ds[h]) for h, sl in cols]
                dqd = [_bdot_nt(do_ref[rs, sl], sm[h]) for h, sl in cols]
                dkd = [_bdot_nt(vn_ref[rs, sl], ds[h]) for h, sl in cols]
                dcd = [jnp.broadcast_to(_rowsum(_colsum(ds[h] * sm[h])), (8, HEAD)) for h in HEADS]
                yield
                dw = [-_bdot_nt(dvn[h], sm[h]) for h, _ in cols]
                for h, sl in cols:
                    du_s[cur, rs, sl] = dvn[h].astype(BF16)
                    dqd_s[cur, rs, sl] = dqd[h]
                    dw_s[cur, rs, sl] = dw[h].astype(BF16)
                    dkd_s[cur, rs, sl] = dkd[h]
                    dcd_s[cur, ci * 8:(ci + 1) * 8, sl] = dcd[h]
                ds = [ds[h] * cd_ref[ci * 8:ci * 8 + 1, sl] + _bdot_tn(qd_ref[rs, sl], do_ref[rs, sl])
                      - _bdot_tn(w_ref[rs, sl], dvn[h]) for h, sl in cols]
                yield
            for h in HEADS:
                dstate[h] = ds[h]

        def factors():
            ones = jnp.ones((PAIR, HEAD), BF16)
            tn = (((0,), (0,)), ((), ()))
            kept = lambda ref, rows=PAIR: [ref[prev, pp * rows:(pp + 1) * rows, HEAD_COLS[h]] for pp, h in UNITS]
            kn, vs, beta = _heads(kn_ref), _heads(vs_ref), _heads(beta_ref)
            cm = {}
            yield from _pair_common_stages(cm, _heads(qn_ref), kn, vs, beta, _heads(g_ref))
            tmv = _heads(t_ref)
            duv, dwv, dattv, dqdv, dkdv = kept(du_s), kept(dw_s), kept(datt_s), kept(dqd_s), kept(dkd_s)
            dvb = _each(_bdot_tn, tmv, duv)
            dt = _each(lambda a, b, c, d: _bdot_nt(a, b) + _bdot_nt(c, d), duv, cm["vb"], dwv, cm["kbg"])
            dkbg = _each(_bdot_tn, tmv, dwv)
            yield
            m1 = _each(_bdot_tn, tmv, dt)
            yield
            da = _each(lambda a, b: -jnp.where(cm["strict"], _bdot_nt(a, b), 0.0), m1, tmv)
            yield
            dkk = _each(lambda a, b: a * b, da, cm["decay"])
            dqk = _each(lambda a, b: a * b, dattv, cm["decay"])
            dd = _each(lambda a, b, c, d: a * b + c * d, dkk, cm["kk"], dqk, cm["qk"])
            dkb = _each(lambda a, b, c, d: _bdot(a, b) + c * d, dkk, kn, dkbg, cm["egc"])
            dq = _each(lambda a, b, c, d: _bdot(a, b) + c * d, dqk, kn, dqdv, cm["egc"])
            yield
            dkn = _each(lambda a, b, c, d: _bdot_tn(a, b) + _bdot_tn(c, d), dkk, cm["kb"], dqk, cm["q"])
            dkn = _each(lambda a, b, c, d, e: a + b * c + d * e, dkn, dkdv, cm["ekd"], dkb, beta)
            t_kd = _each(lambda a, b, c: _rowsum(a * b * c), dkdv, kn, cm["ekd"])
            yield
            split = _each(_split, dd)
            rows_dd = [jnp.dot(hi, ones, preferred_element_type=F32) + jnp.dot(lo, ones, preferred_element_type=F32)
                       for hi, lo in split]
            cols_dd = [lax.dot_general(hi, ones, tn, preferred_element_type=F32)
                       + lax.dot_general(lo, ones, tn, preferred_element_type=F32) for hi, lo in split]
            yield
            dgc = _each(lambda r, c, a, b, e, f, k, tk: r - c + _rowsum(a * b * e) + _rowsum(f * k) - tk,
                        rows_dd, cols_dd, dqdv, cm["q"], cm["egc"], dkbg, cm["kbg"], t_kd)
            same_b = cm["same"].astype(BF16)
            rowi = lax.broadcasted_iota(I32, (PAIR, HEAD), 0)
            dcd = _each(lambda d: jnp.where(rowi < CHUNK, d[0:1], d[8:9]), kept(dcd_s, rows=16))
            dgl = _each(lambda tk, d, c: _mask_dot(same_b, jnp.broadcast_to(tk, (PAIR, HEAD))) + d * c, t_kd, dcd, cm["cd"])
            yield
            is_last = jnp.bitwise_and(rowi, CHUNK - 1) == CHUNK - 1
            dgc = _each(lambda a, b: a + jnp.where(is_last, b, 0.0), dgc, dgl)
            r = lax.broadcasted_iota(I32, (PAIR, PAIR), 0)
            c = lax.broadcasted_iota(I32, (PAIR, PAIR), 1)
            upper_b = (cm["same"] & (r <= c)).astype(BF16)
            _put_heads(dg_ref, _each(lambda v: _mask_dot(upper_b, v), dgc))
            yield
            _put_heads(dbeta_ref, _each(lambda a, b, c, d: jnp.broadcast_to(_rowsum(a * b) + _rowsum(c * d), (PAIR, HEAD)),
                                        dkb, kn, dvb, vs))
            _put_heads(dqn_ref, _each(lambda v: v * QK_SCALE, dq))
            _put_heads(dkn_ref, dkn)
            _put_heads(dvs_ref, _each(lambda a, b: a * b, dvb, beta))
            yield

        _interleave(recurrence(), factors())

    last = n_steps - 1
    now = lambda i: (jnp.maximum(last - i, 0), 0)
    after = lambda i: (jnp.minimum(n_steps - i, last), 0)
    rows = lambda index: pl.BlockSpec((SCAN_ROWS, D_HALF), index)
    slot = lambda r, dtype: pltpu.VMEM((2, r, D_HALF), dtype)
    return _call(
        body, name="delta_bwd", grid=(n_steps + 1,),
        in_specs=[rows(now)] * 6 + [_chunk_scalar_spec(SCAN_PAIRS, now),
                                    pl.BlockSpec((2 * SCAN_PAIRS, N_HEADS, HEAD, HEAD), lambda i: (jnp.maximum(last - i, 0), 0, 0, 0))]
                 + [rows(after)] * 6,
        out_specs=[rows(after)] * 5,
        out_shape=[_sds((s, D_HALF))] * 5,
        scratch_shapes=[pltpu.VMEM((N_HEADS, HEAD, HEAD), F32), slot(SCAN_ROWS, BF16), slot(SCAN_ROWS, BF16),
                        slot(SCAN_ROWS, F32), slot(SCAN_ROWS, F32), slot(SCAN_ROWS, F32), slot(16 * SCAN_PAIRS, F32)],
        compiler_params=_params("arbitrary"),
    )(do, vn, qd, kd, w, att, cd, st, qn, kn, vs, beta, g, tm)


def _rows8(x):
    acc = x[0:8]
    for r in range(8, x.shape[0], 8):
        acc = acc + x[r:r + 8]
    return acc


def _conv_bwd(proj, conv_w, a_log, dt_bias, dqn, dkn, dvs, dbeta, dg):
    s = proj.shape[0]
    t = CONV_T
    n_tiles = s // t
    n_sub = t // CONV_SUB
    tile_of = lambda i: n_tiles - 1 - i

    def body(q_ref, k_ref, v_ref, hq_ref, hk_ref, hv_ref, ba_ref, cw_ref, al_ref, dtb_ref,
             dqn_ref, dkn_ref, dvs_ref, dbeta_ref, dg_ref, oq_ref, ok_ref, ov_ref, dba_ref, gcw_out, gsm_out,
             below, gcw_ref, gsm_ref):
        @pl.when(pl.program_id(0) == 0)
        def _():
            gcw_ref[...] = jnp.zeros_like(gcw_ref)
            gsm_ref[...] = jnp.zeros_like(gsm_ref)
            below[...] = jnp.zeros_like(below)

        live = (pl.program_id(0) < n_tiles - 1).astype(F32)
        parts = ((q_ref, hq_ref, dqn_ref, oq_ref), (k_ref, hk_ref, dkn_ref, ok_ref), (v_ref, hv_ref, dvs_ref, ov_ref))
        lane = lax.broadcasted_iota(I32, (CONV_SUB, HEAD), 1)
        lane8 = lax.broadcasted_iota(I32, (8, HEAD), 1)

        def sub_tile(r0, first):
            rows = pl.ds(r0, CONV_SUB)
            for p, (x_ref, h_ref, d_ref, o_ref) in enumerate(parts):
                for h in HEADS:
                    cs = HEAD_COLS[h]
                    wide = slice(p * D_HALF + h * HEAD, p * D_HALF + (h + 1) * HEAD)
                    cw = cw_ref[:, wide]
                    prev8 = h_ref[:, cs] * live if first else x_ref[pl.ds(r0 - 8, 8), cs]
                    taps = _conv_taps(x_ref[rows, cs], prev8)
                    y = _conv_pre(taps, cw)
                    sg = _sigmoid(y)
                    sv = y * sg
                    ds = d_ref[rows, cs]
                    if p < 2:
                        rn = lax.rsqrt(_rowsum(sv * sv) + EPS)
                        nrm = sv * rn
                        ds = rn * (ds - nrm * _rowsum(ds * nrm))
                    dy = ds * (sg * (1.0 + y * (1.0 - sg)))
                    for j in range(CONV_K):
                        gcw_ref[8 * j:8 * j + 8, wide] += _rows8(dy * taps[j])
                    nxt = below[:, wide]
                    acc = dy * cw[CONV_K - 1:CONV_K]
                    for sft in range(1, CONV_K):
                        acc = acc + _shift_up(dy, nxt, sft) * cw[CONV_K - 1 - sft:CONV_K - sft]
                    o_ref[rows, cs] = acc.astype(BF16)
                    below[:, wide] = dy[0:8]

            ba = ba_ref[rows, :]
            dba = jnp.zeros((CONV_SUB, HEAD), F32)
            gsm = jnp.zeros((8, HEAD), F32)
            for h in HEADS:
                beta = _sigmoid(ba[:, h:h + 1])
                dbeta = dbeta_ref[rows, h * HEAD:h * HEAD + 1]
                xg = ba[:, N_HEADS + h:N_HEADS + h + 1] + dtb_ref[0:1, h:h + 1]
                nexp = -jnp.exp(al_ref[0:1, h:h + 1])
                dgv = dg_ref[rows, h * HEAD:h * HEAD + 1]
                da = dgv * nexp * _sigmoid(xg)
                dba = dba + jnp.where(lane == h, dbeta * beta * (1.0 - beta), 0.0) + jnp.where(lane == N_HEADS + h, da, 0.0)
                gsm = (gsm + jnp.where(lane8 == h, _rows8(dgv * nexp * _softplus(xg)), 0.0)
                       + jnp.where(lane8 == N_HEADS + h, _rows8(da), 0.0))
            dba_ref[rows, :] = jnp.zeros((CONV_SUB, D_HALF), BF16)
            dba_ref[rows, :HEAD] = dba.astype(BF16)
            gsm_ref[...] += gsm

        def step(k, carry):
            sub_tile(pl.multiple_of((n_sub - 1 - k) * CONV_SUB, CONV_SUB), False)
            return carry

        lax.fori_loop(0, n_sub - 1, step, 0)
        sub_tile(0, True)

        @pl.when(pl.program_id(0) == n_tiles - 1)
        def _():
            gcw_out[...] = jnp.zeros_like(gcw_out)
            for j in range(CONV_K):
                gcw_out[j:j + 1, :] = _colsum(gcw_ref[8 * j:8 * j + 8, :])
            gsm_out[...] = jnp.broadcast_to(_colsum(gsm_ref[...]), (8, HEAD))

    row = pl.BlockSpec((t, D_HALF), lambda i: (tile_of(i), 0))
    const = lambda shape: pl.BlockSpec(shape, lambda i: (0, 0))
    return _call(
        body, name="conv_bwd", grid=(n_tiles,),
        in_specs=_conv_specs(t, tile_of) + [pl.BlockSpec((t, HEAD), lambda i: (tile_of(i), COL_BA // HEAD)),
                                            const((CONV_K, 3 * D_HALF)), const((1, N_HEADS)), const((1, N_HEADS))] + [row] * 5,
        out_specs=[row, row, row, row, const((8, 3 * D_HALF)), const((8, HEAD))],
        out_shape=[_sds((s, D_HALF), BF16)] * 4 + [_sds((8, 3 * D_HALF)), _sds((8, HEAD))],
        scratch_shapes=[pltpu.VMEM((8, 3 * D_HALF), F32), pltpu.VMEM((8 * CONV_K, 3 * D_HALF), F32),
                        pltpu.VMEM((8, HEAD), F32)],
        compiler_params=_params("arbitrary"),
    )(proj, proj, proj, proj, proj, proj, proj, conv_w, a_log, dt_bias, dqn, dkn, dvs, dbeta, dg)


def _conv_bwd_pre(proj, conv_w, a_log, dt_bias, dqn, dkn, dvs, dbeta, dg):
    s = proj.shape[0]
    t = CONV_T

    def body(q_ref, k_ref, v_ref, hq_ref, hk_ref, hv_ref, ba_ref, cw_ref, al_ref, dtb_ref,
             dqn_ref, dkn_ref, dvs_ref, dbeta_ref, dg_ref, dyq_ref, dyk_ref, dyv_ref, dba_ref, gcw_ref, gsm_ref):
        @pl.when(pl.program_id(0) == 0)
        def _():
            gcw_ref[...] = jnp.zeros_like(gcw_ref)
            gsm_ref[...] = jnp.zeros_like(gsm_ref)

        live = (pl.program_id(0) > 0).astype(F32)
        parts = ((q_ref, hq_ref, dqn_ref, dyq_ref), (k_ref, hk_ref, dkn_ref, dyk_ref), (v_ref, hv_ref, dvs_ref, dyv_ref))
        for p, (x_ref, h_ref, d_ref, dy_ref) in enumerate(parts):
            cols = slice(p * D_HALF, (p + 1) * D_HALF)
            taps = _conv_taps(x_ref[...], h_ref[...] * live)
            y = _conv_pre(taps, cw_ref[:, cols])
            sg = _sigmoid(y)
            sv = y * sg
            if p == 2:
                ds = d_ref[...]
            else:
                segs = []
                for h in HEADS:
                    seg = _head(sv, h)
                    rn = lax.rsqrt(_rowsum(seg * seg) + EPS)
                    nrm = seg * rn
                    dn = d_ref[:, HEAD_COLS[h]]
                    segs.append(rn * (dn - nrm * _rowsum(dn * nrm)))
                ds = jnp.concatenate(segs, axis=1)
            dy = ds * (sg * (1.0 + y * (1.0 - sg)))
            dy_ref[...] = dy
            for j in range(CONV_K):
                gcw_ref[j:j + 1, cols] += _colsum(dy * taps[j])

        ba = ba_ref[...]
        lane = lax.broadcasted_iota(I32, (t, HEAD), 1)
        lane1 = lax.broadcasted_iota(I32, (1, HEAD), 1)
        dba = jnp.zeros((t, HEAD), F32)
        gsm = jnp.zeros((1, HEAD), F32)
        for h in HEADS:
            beta = _sigmoid(ba[:, h:h + 1])
            dbeta = dbeta_ref[:, h * HEAD:h * HEAD + 1]
            xg = ba[:, N_HEADS + h:N_HEADS + h + 1] + dtb_ref[0:1, h:h + 1]
            nexp = -jnp.exp(al_ref[0:1, h:h + 1])
            dgv = dg_ref[:, h * HEAD:h * HEAD + 1]
            da = dgv * nexp * _sigmoid(xg)
            dba = dba + jnp.where(lane == h, dbeta * beta * (1.0 - beta), 0.0) + jnp.where(lane == N_HEADS + h, da, 0.0)
            gsm = (gsm + jnp.where(lane1 == h, _colsum(dgv * nexp * _softplus(xg)), 0.0)
                   + jnp.where(lane1 == N_HEADS + h, _colsum(da), 0.0))
        dba_ref[...] = jnp.zeros_like(dba_ref)
        dba_ref[:, :HEAD] = dba.astype(BF16)
        gsm_ref[0:1, :] += gsm

    row = pl.BlockSpec((t, D_HALF), lambda i: (i, 0))
    return _call(
        body, name="conv_bwd_pre", grid=(s // t,),
        in_specs=_conv_specs(t) + [pl.BlockSpec((t, HEAD), lambda i: (i, COL_BA // HEAD)),
                                   pl.BlockSpec((CONV_K, 3 * D_HALF), lambda i: (0, 0)),
                                   pl.BlockSpec((1, N_HEADS), lambda i: (0, 0)),
                                   pl.BlockSpec((1, N_HEADS), lambda i: (0, 0))] + [row] * 5,
        out_specs=[row, row, row, row,
                   pl.BlockSpec((8, 3 * D_HALF), lambda i: (0, 0)), pl.BlockSpec((8, HEAD), lambda i: (0, 0))],
        out_shape=[_sds((s, D_HALF))] * 3 + [_sds((s, D_HALF), BF16), _sds((8, 3 * D_HALF)), _sds((8, HEAD))],
        compiler_params=_params("arbitrary"),
    )(proj, proj, proj, proj, proj, proj, proj, conv_w, a_log, dt_bias, dqn, dkn, dvs, dbeta, dg)


def _conv_bwd_in(dyq, dyk, dyv, conv_w):
    s = dyq.shape[0]
    t = CONV_T
    last = s // 8 - 1

    def body(q_ref, k_ref, v_ref, nq_ref, nk_ref, nv_ref, cw_ref, oq_ref, ok_ref, ov_ref):
        more = (pl.program_id(0) < pl.num_programs(0) - 1).astype(F32)
        for p, (d_ref, n_ref, o_ref) in enumerate(((q_ref, nq_ref, oq_ref), (k_ref, nk_ref, ok_ref), (v_ref, nv_ref, ov_ref))):
            cw = cw_ref[:, p * D_HALF:(p + 1) * D_HALF]
            dy = d_ref[...]
            nxt = n_ref[...] * more
            acc = dy * cw[3:4]
            for sft in (1, 2, 3):
                acc = acc + _shift_up(dy, nxt, sft) * cw[3 - sft:4 - sft]
            o_ref[...] = acc.astype(BF16)

    row = pl.BlockSpec((t, D_HALF), lambda i: (i, 0))
    nxt = pl.BlockSpec((8, D_HALF), lambda i: (jnp.minimum((i + 1) * (t // 8), last), 0))
    return _call(
        body, name="conv_bwd_in", grid=(s // t,),
        in_specs=[row] * 3 + [nxt] * 3 + [pl.BlockSpec((CONV_K, 3 * D_HALF), lambda i: (0, 0))],
        out_specs=[row] * 3, out_shape=[_sds((s, D_HALF), BF16)] * 3,
        compiler_params=_params("arbitrary"),
    )(dyq, dyk, dyv, dyq, dyk, dyv, conv_w)


IN_T = 512


def _in_bwd(x, dh, norm_w, w_pad, pieces):
    s = x.shape[0]
    t = IN_T
    widths = [D_HALF] * 6 + [N_IN_PAD - COL_BA]

    def body(*refs):
        x_ref, dh_ref, nw_ref, w_ref = refs[:4]
        p_refs = refs[4:4 + len(pieces)]
        gx_ref, gnw_ref = refs[4 + len(pieces):]

        @pl.when(pl.program_id(0) == 0)
        def _():
            gnw_ref[...] = jnp.zeros_like(gnw_ref)

        dn = jnp.zeros((t, D_MODEL), F32)
        col = 0
        for p_ref, wd in zip(p_refs, widths):
            dn = dn + _bdot_nt(p_ref[...], w_ref[:, col:col + wd])
            col += wd
        xv = x_ref[...]
        r = lax.rsqrt(jnp.mean(xv * xv, axis=-1, keepdims=True) + EPS)
        xhat = xv * r
        gnw_ref[...] += _colsum(dn * xhat)
        dxh = dn * nw_ref[...]
        gx_ref[...] = dh_ref[...] + r * (dxh - xhat * jnp.mean(dxh * xhat, axis=-1, keepdims=True))

    wide = pl.BlockSpec((t, D_MODEL), lambda i: (i, 0))
    return _call(
        body, name="in_bwd", grid=(s // t,),
        in_specs=[wide, wide, pl.BlockSpec((1, D_MODEL), lambda i: (0, 0)),
                  pl.BlockSpec((D_MODEL, N_IN_PAD), lambda i: (0, 0))]
                 + [pl.BlockSpec((t, wd), lambda i: (i, 0)) for wd in widths],
        out_specs=[wide, pl.BlockSpec((1, D_MODEL), lambda i: (0, 0))],
        out_shape=[_sds((s, D_MODEL)), _sds((1, D_MODEL))],
        compiler_params=_params("arbitrary"),
    )(x, dh, norm_w, w_pad, *pieces)


def _adamw_shard(name, w, g_own, g_got, cidx, m, v):
    _, r, c = w.shape
    half = r // 2
    rows = 256 if half % 256 == 0 else half
    per_half = half // rows

    def body(c_ref, w_ref, go_ref, gg_ref, m_ref, v_ref, gout_ref, d_ref, nm_ref, nv_ref):
        mine = (pl.program_id(0) // per_half) == c_ref[0]
        gv = jnp.where(mine, go_ref[:, :c], gg_ref[:, :c])
        gout_ref[0] = gv
        mn = ADAM_B1 * m_ref[0] + (1.0 - ADAM_B1) * gv
        vn = ADAM_B2 * v_ref[0] + (1.0 - ADAM_B2) * (gv * gv)
        m_hat = mn / (1.0 - ADAM_B1 ** ADAM_STEP)
        v_hat = vn / (1.0 - ADAM_B2 ** ADAM_STEP)
        d_ref[0] = -ADAM_LR * (m_hat / (jnp.sqrt(v_hat) + ADAM_EPS) + ADAM_WD * w_ref[0])
        nm_ref[0] = mn
        nv_ref[0] = vn

    blk = pl.BlockSpec((1, rows, c), lambda i, c_ref: (0, i, 0))
    gblk = pl.BlockSpec((rows, g_own.shape[1]), lambda i, c_ref: (i % per_half, 0))
    return _call(
        body, name=name,
        grid_spec=pltpu.PrefetchScalarGridSpec(
            num_scalar_prefetch=1, grid=(2 * per_half,),
            in_specs=[blk, gblk, gblk, blk, blk], out_specs=[blk] * 4),
        out_shape=[_sds((1, r, c))] * 4,
        compiler_params=_params("arbitrary"),
    )(cidx, w, g_own, g_got, m, v)


def _adamw_tiles(name, w, g, m, v):
    n = w.shape[0]
    nb = 77 if n % 77 == 0 else n

    def body(w_ref, g_ref, m_ref, v_ref, d_ref, nm_ref, nv_ref):
        gv = g_ref[...]
        mn = ADAM_B1 * m_ref[...] + (1.0 - ADAM_B1) * gv
        vn = ADAM_B2 * v_ref[...] + (1.0 - ADAM_B2) * (gv * gv)
        m_hat = mn / (1.0 - ADAM_B1 ** ADAM_STEP)
        v_hat = vn / (1.0 - ADAM_B2 ** ADAM_STEP)
        d_ref[...] = -ADAM_LR * (m_hat / (jnp.sqrt(v_hat) + ADAM_EPS) + ADAM_WD * w_ref[...])
        nm_ref[...] = mn
        nv_ref[...] = vn

    blk = pl.BlockSpec((nb, 8, HEAD), lambda i: (i, 0, 0))
    return _call(
        body, name=name, grid=(n // nb,),
        in_specs=[blk] * 4, out_specs=[blk] * 3, out_shape=[_sds(w.shape)] * 3,
        compiler_params=_params("arbitrary"),
    )(w, g, m, v)


def _exchange(name, inputs, out_shapes, phases):
    n_in = len(inputs)
    n_out = len(out_shapes)
    n_cp = sum(len(p) for p in phases)

    def body(*refs):
        ins, outs = refs[:n_in], refs[n_in:n_in + n_out]
        send, recv = refs[n_in + n_out:]
        pos = (lax.axis_index("x"), lax.axis_index("y"), lax.axis_index("c"))
        k = 0
        for phase in phases:
            cps = []
            for src, dst, target in phase:
                cps.append(pltpu.make_async_remote_copy(
                    src_ref=src(ins, outs, pos), dst_ref=dst(ins, outs, pos), send_sem=send.at[k], recv_sem=recv.at[k],
                    device_id=target(pos), device_id_type=pl.DeviceIdType.MESH))
                k += 1
            for cp in cps:
                cp.start()
            for cp in cps:
                cp.wait()

    anyspec = pl.BlockSpec(memory_space=pl.ANY)
    return _call(
        body, name=name,
        in_specs=[anyspec] * n_in, out_specs=[anyspec] * n_out, out_shape=list(out_shapes),
        scratch_shapes=[pltpu.SemaphoreType.DMA((n_cp,)), pltpu.SemaphoreType.DMA((n_cp,))],
    )(*inputs)


def _exchange_start(name, inputs, out_shapes, copies):
    n_in, n_out, n_cp = len(inputs), len(out_shapes), len(copies)

    def body(*refs):
        ins, lands = refs[:n_in], refs[n_in:n_in + n_out]
        sems = refs[n_in + n_out:n_in + n_out + 2 * n_cp]
        token = refs[-1]
        pos = (lax.axis_index("x"), lax.axis_index("y"), lax.axis_index("c"))
        for k, (src, dst, target) in enumerate(copies):
            pltpu.make_async_remote_copy(
                src_ref=src(ins, lands, pos), dst_ref=dst(ins, lands, pos), send_sem=sems[2 * k], recv_sem=sems[2 * k + 1],
                device_id=target(pos), device_id_type=pl.DeviceIdType.MESH).start()
        token[...] = jnp.zeros_like(token)

    hbm = pl.BlockSpec(memory_space=pltpu.HBM)
    sem = pl.BlockSpec(memory_space=pltpu.SEMAPHORE)
    bufs = list(inputs) + [lax.empty(o.shape, o.dtype) for o in out_shapes]
    outs = _call(
        body, name=name,
        out_shape=tuple([pltpu.SemaphoreType.DMA(())] * (2 * n_cp) + [pltpu.HBM(b.shape, b.dtype) for b in bufs]
                        + [_sds((8, HEAD))]),
        in_specs=[hbm] * len(bufs),
        out_specs=tuple([sem] * (2 * n_cp) + [hbm] * len(bufs) + [pl.BlockSpec(memory_space=pltpu.VMEM)]),
        input_output_aliases={i: 2 * n_cp + i for i in range(len(bufs))},
        compiler_params=pltpu.CompilerParams(has_side_effects=pltpu.SideEffectType.DATAFLOW_SIDE_EFFECTING),
    )(*[pltpu.with_memory_space_constraint(b, pltpu.HBM) for b in bufs])
    return outs[:2 * n_cp], outs[2 * n_cp:2 * n_cp + n_in], outs[2 * n_cp + n_in:-1], outs[-1]


def _exchange_wait(name, sems, sources, lands, copies, after):
    n_in, n_out, n_cp = len(sources), len(lands), len(copies)

    def body(*refs):
        ins, zones = refs[:n_in], refs[n_in:n_in + n_out]
        sem_refs = refs[n_in + n_out:n_in + n_out + 2 * n_cp]
        pos = (lax.axis_index("x"), lax.axis_index("y"), lax.axis_index("c"))
        for k, (src, dst, target) in enumerate(copies):
            cp = pltpu.make_async_remote_copy(
                src_ref=src(ins, zones, pos), dst_ref=dst(ins, zones, pos), send_sem=sem_refs[2 * k],
                recv_sem=sem_refs[2 * k + 1], device_id=target(pos), device_id_type=pl.DeviceIdType.MESH)
            cp.wait_send()
            cp.wait_recv()

    hbm = pl.BlockSpec(memory_space=pltpu.HBM)
    sem = pl.BlockSpec(memory_space=pltpu.SEMAPHORE)
    bufs = list(sources) + list(lands)
    outs = _call(
        body, name=name,
        out_shape=tuple(pltpu.HBM(b.shape, b.dtype) for b in bufs),
        in_specs=[hbm] * len(bufs) + [sem] * (2 * n_cp) + [pl.BlockSpec(memory_space=pl.ANY)],
        out_specs=tuple([hbm] * len(bufs)),
        input_output_aliases={i: i for i in range(len(bufs))},
        compiler_params=pltpu.CompilerParams(has_side_effects=pltpu.SideEffectType.DATAFLOW_SIDE_EFFECTING),
    )(*bufs, *sems, after)
    return outs[:n_in], outs[n_in:]


def _allreduce_tile(name, v):
    def body(v_ref, out_ref, slots, send, recv):
        x, y, c = lax.axis_index("x"), lax.axis_index("y"), lax.axis_index("c")
        me = 4 * x + 2 * y + c
        slots[me] = v_ref[...]
        cps = []
        for k in range(1, 8):
            peer = (x ^ (k >> 2), y ^ ((k >> 1) & 1), c ^ (k & 1))
            cps.append(pltpu.make_async_remote_copy(
                src_ref=v_ref, dst_ref=slots.at[me], send_sem=send.at[k - 1], recv_sem=recv.at[k - 1],
                device_id=peer, device_id_type=pl.DeviceIdType.MESH))
        for cp in cps:
            cp.start()
        for cp in cps:
            cp.wait()
        acc = slots[0]
        for i in range(1, 8):
            acc = acc + slots[i]
        out_ref[...] = acc

    vm = pl.BlockSpec(memory_space=pltpu.VMEM)
    return _call(
        body, name=name, in_specs=[vm], out_specs=vm, out_shape=_sds(v.shape),
        scratch_shapes=[pltpu.VMEM((8,) + v.shape, F32), pltpu.SemaphoreType.DMA((7,)), pltpu.SemaphoreType.DMA((7,))],
    )(v)


def _chip(pos):
    return 2 * pos[0] + pos[1]


def _other_chip(pos, mask):
    x, y, c = pos
    return (x ^ (mask >> 1), y ^ (mask & 1), c)


def _sibling(pos):
    return (pos[0], pos[1], 1 - pos[2])


def _gather_weights(wb, cb):
    rows = wb.shape[0] // 2
    x_nb, y_nb, diag = CHIP_MASKS

    def part(pos, mask, quarter=None):
        start = pos[2] * rows if quarter is None else pos[2] * rows + quarter * (rows // 2)
        return lambda outs: outs[0].at[_chip(pos) ^ mask, pl.ds(start, rows if quarter is None else rows // 2)]

    def passed_on(mask, to, quarter=None):
        return (lambda ins, outs, pos: part(pos, mask, quarter)(outs), lambda ins, outs, pos: part(pos, mask, quarter)(outs), to)

    first = [(lambda ins, outs, pos: ins[0].at[pl.ds(pos[2] * rows, rows)], lambda ins, outs, pos: part(pos, 0)(outs),
              functools.partial(_other_chip, mask=mask)) for mask in (x_nb, y_nb)]
    first += [(lambda ins, outs, pos: ins[1], lambda ins, outs, pos: outs[1].at[_chip(pos)],
               functools.partial(_other_chip, mask=mask)) for mask in CHIP_MASKS]
    second = [passed_on(x_nb, functools.partial(_other_chip, mask=y_nb), quarter=0),
              passed_on(y_nb, functools.partial(_other_chip, mask=x_nb), quarter=1),
              passed_on(x_nb, _sibling), passed_on(y_nb, _sibling)]
    third = [passed_on(diag, _sibling)]
    return _exchange("gather_weights", [wb, cb], [_sds((4,) + wb.shape, wb.dtype), _sds((4,) + cb.shape, cb.dtype)],
                     [first, second, third])


def _gather_blocks(ob):
    copies = [(lambda ins, outs, pos: ins[0], lambda ins, outs, pos: outs[0].at[_chip(pos)],
               functools.partial(_other_chip, mask=mask)) for mask in CHIP_MASKS]
    return [_sds((4,) + ob.shape, ob.dtype)], copies


def _to_sibling_half(name, arrays):
    def src(ins, outs, pos, a):
        h = arrays[a].shape[-2] // 2
        sl = pl.ds((1 - pos[2]) * h, h)
        return ins[a].at[:, sl] if arrays[a].ndim == 3 else ins[a].at[sl]

    outs = [_sds(a.shape[:-2] + (a.shape[-2] // 2, a.shape[-1]), a.dtype) for a in arrays]
    phase = [(functools.partial(src, a=a), lambda ins, outs, pos, a=a: outs[a], _sibling) for a in range(len(arrays))]
    return _exchange(name, arrays, outs, [phase])


def _add_half(name, full, part, cidx):
    shape = part.shape
    lead = shape[0] if len(shape) == 3 else 1
    rows, cols = shape[-2], shape[-1]
    tr = rows // 2 if rows % 16 == 0 else rows
    nr = rows // tr
    f3 = full.reshape((lead,) + full.shape[-2:])
    p3 = part.reshape((lead, rows, cols))

    def body(c_ref, f_ref, p_ref, o_ref):
        o_ref[...] = (f_ref[...].astype(F32) + p_ref[...].astype(F32)).astype(o_ref.dtype)

    out = _call(
        body, name=name,
        grid_spec=pltpu.PrefetchScalarGridSpec(
            num_scalar_prefetch=1, grid=(lead, nr),
            in_specs=[pl.BlockSpec((1, tr, cols), lambda b, r, c_ref: (b, c_ref[0] * nr + r, 0)),
                      pl.BlockSpec((1, tr, cols), lambda b, r, c_ref: (b, r, 0))],
            out_specs=pl.BlockSpec((1, tr, cols), lambda b, r, c_ref: (b, r, 0))),
        out_shape=_sds((lead, rows, cols), part.dtype),
        compiler_params=_params("arbitrary", "arbitrary"),
    )(cidx, f3, p3)
    return out.reshape(shape)


def _to_other_chips(arrays, blocked):
    def src(ins, outs, pos, a, mask):
        return ins[a].at[_chip(pos) ^ mask] if blocked[a] else ins[a]

    outs = [_sds((3,) + (a.shape[1:] if b else a.shape), a.dtype) for a, b in zip(arrays, blocked)]
    copies = []
    for mi, mask in enumerate(CHIP_MASKS):
        for a in range(len(arrays)):
            copies.append((functools.partial(src, a=a, mask=mask), lambda ins, outs, pos, a=a, mi=mi: outs[a].at[mi],
                           functools.partial(_other_chip, mask=mask)))
    return outs, copies


def _add_chips(name, own, got, jidx, blocked):
    rows, cols = got.shape[-2:]
    tr = rows // 2 if rows % 16 == 0 else rows
    nr = rows // tr
    o3 = own if blocked else own.reshape((1, rows, cols))

    def body(j_ref, o_ref, g_ref, out_ref):
        out_ref[...] = ((o_ref[0].astype(F32) + g_ref[0].astype(F32))
                        + (g_ref[1].astype(F32) + g_ref[2].astype(F32)))

    own_map = (lambda r, j_ref: (j_ref[0], r, 0)) if blocked else (lambda r, j_ref: (0, r, 0))
    return _call(
        body, name=name,
        grid_spec=pltpu.PrefetchScalarGridSpec(
            num_scalar_prefetch=1, grid=(nr,),
            in_specs=[pl.BlockSpec((1, tr, cols), own_map),
                      pl.BlockSpec((3, tr, cols), lambda r, j_ref: (0, r, 0))],
            out_specs=pl.BlockSpec((tr, cols), lambda r, j_ref: (r, 0))),
        out_shape=_sds((rows, cols)),
        compiler_params=_params("arbitrary"),
    )(jidx, o3, got)


def _to_sibling(name, arrays):
    phase = [(lambda ins, outs, pos, a=a: ins[a], lambda ins, outs, pos, a=a: outs[a], _sibling)
             for a in range(len(arrays))]
    return _exchange(name, arrays, [_sds(a.shape, a.dtype) for a in arrays], [phase])


def _local_step(x, target, w_pad, w_out, conv_w, norm_w, pool_w, pool_scale, a_log, dt_bias, dn_norm_w, final_norm_w):
    proj, n_t = _proj_fwd(x, norm_w, w_pad)
    y_pool = _pool_fwd(proj, pool_w, pool_scale)
    qn, kn, vs, beta, g = _conv_fwd(proj, conv_w, a_log, dt_bias)
    w, att, qd, kd, tm, cd, o, vn, st = _delta_fwd(qn, kn, vs, beta, g)
    w_out = w_out(o) if callable(w_out) else w_out
    y_t, dh, dyp, do, ddz, loss, g_fnw, g_dnw = _out_fwd_bwd(x, y_pool, o, proj, target, w_out, dn_norm_w, final_norm_w)
    g_wout = _token_matmul("grad_w_out", y_t, [(dh, 0), (dh, 1)])
    dpu, dpz, g_pw, g_ps = _pool_bwd(proj, dyp, pool_w, pool_scale)
    dqn, dkn, dvs, dbeta, dg = _delta_bwd(do, vn, qd, kd, w, att, cd, st, qn, kn, vs, beta, g, tm)
    dcq, dck, dcv, dba, g_cw, g_sm = _conv_bwd(proj, conv_w, a_log, dt_bias, dqn, dkn, dvs, dbeta, dg)
    pieces = [dpu, dpz, dcq, dck, dcv, ddz, dba]
    g_win = _grad_w_in(n_t, pieces)
    small = dict(norm_w=jnp.zeros_like(norm_w), pool_w=g_pw, pool_scale=g_ps, conv_w=g_cw[:CONV_K],
                 a_log=g_sm[0:1, 0:N_HEADS], dt_bias=g_sm[0:1, N_HEADS:2 * N_HEADS], dn_norm_w=g_dnw, final_norm_w=g_fnw)
    return loss[0, 0], g_win, g_wout, small, dh, pieces


def _pack_small(t):
    lanes = lambda a: jnp.pad(a.reshape(1, -1), ((0, 0), (0, HEAD - a.size)))
    rows = [t["pool_w"].reshape(-1, HEAD), t["norm_w"].reshape(-1, HEAD), t["final_norm_w"].reshape(-1, HEAD),
            t["pool_scale"].reshape(-1, HEAD), t["conv_w"].reshape(-1, HEAD), t["dn_norm_w"].reshape(1, HEAD),
            lanes(t["a_log"]), lanes(t["dt_bias"]), lanes(t.get("loss", jnp.zeros((1,), F32)))]
    buf = jnp.concatenate(rows, axis=0)
    return jnp.pad(buf, ((0, SMALL_ROWS - buf.shape[0]), (0, 0)))


def _unpack_small(buf, conv_cols):
    out, r = {}, 0
    for name, nrow, shape in (("pool_w", 512, (1, N_HEADS, HEAD, HEAD)), ("norm_w", 8, (1, D_MODEL)),
                              ("final_norm_w", 8, (D_MODEL,)), ("pool_scale", 4, (1, D_HALF)),
                              ("conv_w", CONV_K * conv_cols // HEAD, (1, CONV_K, conv_cols)), ("dn_norm_w", 1, (1, HEAD))):
        out[name] = buf[r:r + nrow].reshape(shape)
        r += nrow
    out["a_log"] = buf[r:r + 1, :N_HEADS]
    out["dt_bias"] = buf[r + 1:r + 2, :N_HEADS]
    out["loss"] = buf[r + 2, 0]
    return out


def kernel(x, norm_w, w_in, pool_w, pool_scale, conv_w, a_log, dt_bias, dn_norm_w, w_out, final_norm_w, loss_target, m_norm_w, m_w_in, m_pool_w, m_pool_scale, m_conv_w, m_a_log, m_dt_bias, m_dn_norm_w, m_w_out, m_final_norm_w, v_norm_w, v_w_in, v_pool_w, v_pool_scale, v_conv_w, v_a_log, v_dt_bias, v_dn_norm_w, v_w_out, v_final_norm_w):
    cidx = lax.axis_index("c").astype(I32).reshape(1)
    jidx = (2 * lax.axis_index("x") + lax.axis_index("y")).astype(I32)

    wb = jnp.pad(w_in[0].astype(BF16), ((0, 0), (0, BLK_IN_PAD - BLK_IN)))
    ob = w_out[0].astype(BF16)
    gw, gc = _gather_weights(wb, conv_w[0])
    mine = lambda j: jidx == j
    w_pad = jnp.concatenate([jnp.where(mine(j), wb[:, :BLK_IN], gw[j, :, :BLK_IN]) for j in range(4)]
                            + [jnp.zeros((D_MODEL, N_IN_PAD - N_IN), BF16)], axis=1)
    cw_full = jnp.concatenate([jnp.where(mine(j), conv_w[0], gc[j]) for j in range(4)], axis=1)

    lands_o, copies_o = _gather_blocks(ob)
    sems_o, ob_thru, zones_o, token_o = _exchange_start("gather_w_out_start", [ob], lands_o, copies_o)

    def w_out_full(after):
        (own,), (got,) = _exchange_wait("gather_w_out_wait", sems_o, ob_thru, zones_o, copies_o, after)
        return jnp.where((jnp.arange(4) == jidx)[:, None, None], own[None], got).reshape(D_MODEL, D_MODEL)

    loss, g_win, g_wout, small, dh, pieces = _local_step(
        x[0], loss_target[0], w_pad, w_out_full, cw_full, norm_w + token_o[0, 0], pool_w[0], pool_scale, a_log, dt_bias,
        dn_norm_w, final_norm_w.reshape(1, D_MODEL))
    small["loss"] = loss

    blocks_out = g_wout.reshape(4, BLK_OUT, D_MODEL)
    full = [g_win, blocks_out, _pack_small(small)]
    from_sib = _to_sibling_half("reduce_sibling", full)
    chip_sum = [_add_half("add_sibling_%d" % i, f, p, cidx) for i, (f, p) in enumerate(zip(full, from_sib))]
    blocked = [True, True, False]
    lands, copies = _to_other_chips(chip_sum, blocked)
    sems, chip_sum, zones, token = _exchange_start("reduce_chips_start", chip_sum, lands, copies)
    gx, g_nw = _in_bwd(x[0], dh, norm_w + token[0, 0], w_pad, pieces)
    g_nw = _allreduce_tile("reduce_norm_w", g_nw.reshape(8, HEAD)).reshape(1, D_MODEL)
    chip_sum, from_chips = _exchange_wait("reduce_chips_wait", sems, chip_sum, zones, copies, gx)
    halves = [_add_chips("add_chips_%d" % i, o, g, jidx.reshape(1), b)
              for i, (o, g, b) in enumerate(zip(chip_sum, from_chips, blocked))]
    other_halves = _to_sibling("swap_halves", halves)

    weights = dict(norm_w=norm_w, w_in=w_in, pool_w=pool_w, pool_scale=pool_scale, conv_w=conv_w, a_log=a_log,
                   dt_bias=dt_bias, dn_norm_w=dn_norm_w, w_out=w_out, final_norm_w=final_norm_w)
    ms = dict(norm_w=m_norm_w, w_in=m_w_in, pool_w=m_pool_w, pool_scale=m_pool_scale, conv_w=m_conv_w, a_log=m_a_log,
              dt_bias=m_dt_bias, dn_norm_w=m_dn_norm_w, w_out=m_w_out, final_norm_w=m_final_norm_w)
    vs = dict(norm_w=v_norm_w, w_in=v_w_in, pool_w=v_pool_w, pool_scale=v_pool_scale, conv_w=v_conv_w, a_log=v_a_log,
              dt_bias=v_dt_bias, dn_norm_w=v_dn_norm_w, w_out=v_w_out, final_norm_w=v_final_norm_w)
    names = ["norm_w", "w_in", "pool_w", "pool_scale", "conv_w", "a_log", "dt_bias", "dn_norm_w", "w_out", "final_norm_w"]
    small_names = [n for n in names if n not in ("w_in", "w_out")]

    def pack(t):
        conv = lax.dynamic_update_slice_in_dim(jnp.zeros((CONV_K, 3 * D_HALF), F32), t["conv_w"][0], jidx * BLK_CONV, axis=1)
        return _pack_small({**{n: t[n] for n in small_names if n != "conv_w"}, "conv_w": conv})[None]

    results = [{}, {}, {}, {}]
    to_tiles = lambda a: jnp.transpose(a, (2, 0, 1)).reshape(BLK_IN, 8, HEAD)
    from_tiles = lambda a: jnp.transpose(a, (1, 2, 0)).reshape(1, D_MODEL, BLK_IN)
    lo = jnp.where(cidx[0] == 0, halves[0], other_halves[0])
    hi = jnp.where(cidx[0] == 0, other_halves[0], halves[0])
    g_tiles = jnp.concatenate([lo[:, :BLK_IN].T, hi[:, :BLK_IN].T], axis=1).reshape(BLK_IN, 8, HEAD)
    outs = _adamw_tiles("adamw_w_in", to_tiles(w_in), g_tiles, to_tiles(m_w_in), to_tiles(v_w_in))
    for res, o in zip(results, (g_tiles,) + tuple(outs)):
        res["w_in"] = from_tiles(o)
    outs = _adamw_shard("adamw_w_out", w_out, halves[1], other_halves[1], cidx, m_w_out, v_w_out)
    for res, o in zip(results, outs):
        res["w_out"] = o
    outs = _adamw_shard("adamw_small", pack(weights), halves[2], other_halves[2], cidx, pack(ms), pack(vs))
    for res, o in zip(results, outs):
        got = _unpack_small(o[0], 3 * D_HALF)
        got["conv_w"] = lax.dynamic_slice_in_dim(got["conv_w"], jidx * BLK_CONV, BLK_CONV, axis=2)
        res.update(got)
    one_tile = lambda a: a.reshape(1, 8, HEAD)
    outs = _adamw_tiles("adamw_norm_w", one_tile(norm_w), one_tile(g_nw), one_tile(m_norm_w), one_tile(v_norm_w))
    for res, o in zip(results, (g_nw,) + tuple(outs)):
        res["norm_w"] = o.reshape(1, D_MODEL)
    grads, delta, new_m, new_v = results

    return (grads["loss"], gx[None], *[grads[n] for n in names], *[delta[n] for n in names],
            *[new_m[n] for n in names], *[new_v[n] for n in names])
```

```python
import functools

import jax
import jax.numpy as jnp
import numpy as np
from jax import lax
from jax.experimental import pallas as pl
from jax.experimental.pallas import tpu as pltpu

F32 = jnp.float32
BF16 = jnp.bfloat16
I32 = jnp.int32

D_MODEL = 1024
D_HALF = 512
N_HEADS = 4
HEAD = 128
CHUNK = 64
PAIR = 2 * CHUNK
WINDOWS = (2, 4, 8, 16)
CONV_K = 4
EPS = 1e-6
N_IN = 3080
N_IN_PAD = 3200
BLK_IN = 770
BLK_IN_PAD = 896
BLK_OUT = 256
BLK_CONV = 384
COL_BA = 3072
QK_SCALE = HEAD ** -0.5
SMALL_ROWS = 608
VMEM_LIMIT = 56 * 1024 * 1024

ADAM_LR = 0.001
ADAM_B1 = 0.9
ADAM_B2 = 0.999
ADAM_EPS = 1e-08
ADAM_WD = 0.01
ADAM_STEP = 10

CHIP_MASKS = (2, 1, 3)
HEADS = range(N_HEADS)
HEAD_COLS = [slice(h * HEAD, (h + 1) * HEAD) for h in HEADS]


def _call(body, **kw):
    return pl.pallas_call(body, **kw)


def _params(*sem):
    return pltpu.CompilerParams(dimension_semantics=sem, vmem_limit_bytes=VMEM_LIMIT)


def _sds(shape, dtype=F32):
    return jax.ShapeDtypeStruct(shape, dtype)


def _bdot(a, b):
    return jnp.dot(a.astype(BF16), b.astype(BF16), preferred_element_type=F32)


def _bdot_nt(a, b):
    return lax.dot_general(a.astype(BF16), b.astype(BF16), (((1,), (1,)), ((), ())), preferred_element_type=F32)


def _bdot_tn(a, b):
    return lax.dot_general(a.astype(BF16), b.astype(BF16), (((0,), (0,)), ((), ())), preferred_element_type=F32)


def _split(a):
    hi = a.astype(BF16)
    lo = (a - hi.astype(F32)).astype(BF16)
    return hi, lo


def _mask_dot(m, b, dims=(((1,), (0,)), ((), ()))):
    bh, bl = _split(b)
    dg = functools.partial(lax.dot_general, dimension_numbers=dims, preferred_element_type=F32)
    return dg(m, bh) + dg(m, bl)


def _sigmoid(x):
    return 0.5 * jnp.tanh(0.5 * x) + 0.5


def _softplus(x):
    return jnp.maximum(x, 0.0) + jnp.log(1.0 + jnp.exp(-jnp.abs(x)))


def _rowsum(x):
    return jnp.sum(x, axis=-1, keepdims=True)


def _colsum(x):
    return jnp.sum(x, axis=0, keepdims=True)


def _shift_down(xv, prev8, k):
    r = pltpu.roll(xv, k, 0)
    q = pltpu.roll(prev8, k, 0)
    row = lax.broadcasted_iota(I32, prev8.shape, 0)
    top = jnp.where(row < k, q, r[0:8])
    return jnp.concatenate([top, r[8:]], axis=0)


def _shift_up(xv, next8, k):
    t = xv.shape[0]
    r = pltpu.roll(xv, t - k, 0)
    q = pltpu.roll(next8, 8 - k, 0)
    row = lax.broadcasted_iota(I32, next8.shape, 0)
    bot = jnp.where(row >= 8 - k, q, r[t - 8:])
    return jnp.concatenate([r[:t - 8], bot], axis=0)


def _band(rows, cols, off, w, anti=False):
    r = lax.broadcasted_iota(I32, (rows, cols), 0)
    c = lax.broadcasted_iota(I32, (rows, cols), 1)
    d = (c - r + off) if anti else (r - c + off)
    return ((d >= 0) & (d < w)).astype(BF16)


def _head(ref_or_val, h):
    return ref_or_val[:, h * HEAD:(h + 1) * HEAD]


INTRA_PAIRS = 2
UNITS = [(pp, h) for pp in range(INTRA_PAIRS) for h in HEADS]


def _heads(ref, rows=PAIR):
    return [ref[pp * rows:(pp + 1) * rows, HEAD_COLS[h]] for pp, h in UNITS]


def _put_heads(ref, vals, rows=PAIR):
    for (pp, h), v in zip(UNITS, vals):
        ref[pp * rows:(pp + 1) * rows, HEAD_COLS[h]] = v.astype(ref.dtype)


def _each(fn, *lists):
    return [fn(*args) for args in zip(*lists)]


def _proj_fwd(x, norm_w, w_pad):
    s = x.shape[0]
    tm = 512

    def body(x_ref, nw_ref, w_ref, proj_ref, nt_ref):
        xv = x_ref[...]
        r = lax.rsqrt(jnp.mean(xv * xv, axis=-1, keepdims=True) + EPS)
        nv = xv * r * nw_ref[...]
        nt_ref[...] = nv.T.astype(BF16)
        proj_ref[...] = jnp.dot(nv.astype(BF16), w_ref[...], preferred_element_type=F32)

    return _call(
        body, name="proj_fwd", grid=(s // tm,),
        in_specs=[pl.BlockSpec((tm, D_MODEL), lambda i: (i, 0)),
                  pl.BlockSpec((1, D_MODEL), lambda i: (0, 0)),
                  pl.BlockSpec((D_MODEL, N_IN_PAD), lambda i: (0, 0))],
        out_specs=[pl.BlockSpec((tm, N_IN_PAD), lambda i: (i, 0)),
                   pl.BlockSpec((D_MODEL, tm), lambda i: (0, i))],
        out_shape=[_sds((s, N_IN_PAD)), _sds((D_MODEL, s), BF16)],
        compiler_params=_params("arbitrary"),
    )(x, norm_w, w_pad)


def _pool_bands(t, anti=False):
    r = np.arange(t)[:, None]
    c = np.arange(t + HEAD)[None, :]
    d = (c - r) if anti else (r - c + HEAD)
    return jnp.asarray(np.stack([(d >= 0) & (d < w) for w in WINDOWS]), BF16)


def _pool_mix(u, halo, z, pw, bands, row0):
    t = u[0].shape[0]
    rows = row0 + lax.broadcasted_iota(I32, (t, 1), 0) + 1
    cnt = [jnp.minimum(rows, w).astype(F32) for w in WINDOWS]
    win = _each(lambda b, h, v: _mask_dot(b, jnp.concatenate([h, v], axis=0)), bands, halo, u)
    mix = _each(lambda a, c, v: a / c - v, win, cnt, u)
    mixed = _each(_bdot, mix, pw)
    return mix, mixed, _each(_sigmoid, z), cnt


POOL_T = 256


def _pool_fwd(proj, pool_w, pool_scale):
    s = proj.shape[0]
    t = POOL_T
    hb = t // HEAD

    def body(u_ref, z_ref, halo_ref, pw_ref, ps_ref, band_ref, y_ref):
        i = pl.program_id(0)
        live = (i > 0).astype(F32)
        groups = lambda ref: [ref[:, sl] for sl in HEAD_COLS]
        z = groups(z_ref)
        _, mixed, sg, _ = _pool_mix(groups(u_ref), [h * live for h in groups(halo_ref)], z,
                                    [pw_ref[g] for g in HEADS], [band_ref[g] for g in HEADS], i * t)
        for sl, m, zg, s_ in zip(HEAD_COLS, mixed, z, sg):
            y_ref[:, sl] = m * ps_ref[:, sl] * (zg * s_)

    return _call(
        body, name="pool_fwd", grid=(s // t,),
        in_specs=[pl.BlockSpec((t, D_HALF), lambda i: (i, 0)),
                  pl.BlockSpec((t, D_HALF), lambda i: (i, 1)),
                  pl.BlockSpec((HEAD, D_HALF), lambda i: (jnp.maximum(i * hb - 1, 0), 0)),
                  pl.BlockSpec((N_HEADS, HEAD, HEAD), lambda i: (0, 0, 0)),
                  pl.BlockSpec((1, D_HALF), lambda i: (0, 0)),
                  pl.BlockSpec((N_HEADS, t, HEAD + t), lambda i: (0, 0, 0))],
        out_specs=pl.BlockSpec((t, D_HALF), lambda i: (i, 0)),
        out_shape=_sds((s, D_HALF)),
        compiler_params=_params("arbitrary"),
    )(proj, proj, proj, pool_w, pool_scale, _pool_bands(t))


def _conv_taps(xv, prev8):
    return [_shift_down(xv, prev8, CONV_K - 1 - j) for j in range(CONV_K - 1)] + [xv]


def _conv_pre(taps, cw):
    y = taps[CONV_K - 1] * cw[CONV_K - 1:CONV_K]
    for j in range(CONV_K - 2, -1, -1):
        y = y + taps[j] * cw[j:j + 1]
    return y


CONV_T = 256
CONV_SUB = 256


def _conv_specs(t, tile_of=lambda i: i):
    tiles = [pl.BlockSpec((t, D_HALF), functools.partial(lambda i, p: (tile_of(i), 2 + p), p=p)) for p in range(3)]
    halos = [pl.BlockSpec((8, D_HALF),
                          functools.partial(lambda i, p: (jnp.maximum(tile_of(i) * (t // 8) - 1, 0), 2 + p), p=p))
             for p in range(3)]
    return tiles + halos


def _conv_fwd(proj, conv_w, a_log, dt_bias):
    s = proj.shape[0]
    t = CONV_T

    def body(q_ref, k_ref, v_ref, hq_ref, hk_ref, hv_ref, ba_ref, cw_ref, al_ref, dtb_ref,
             qn_ref, kn_ref, vs_ref, beta_ref, g_ref):
        live = (pl.program_id(0) > 0).astype(F32)
        parts = ((q_ref, hq_ref, qn_ref), (k_ref, hk_ref, kn_ref), (v_ref, hv_ref, vs_ref))

        def sub_tile(r0, first):
            rows = pl.ds(r0, CONV_SUB)
            for p, (x_ref, h_ref, o_ref) in enumerate(parts):
                for h in HEADS:
                    cs = HEAD_COLS[h]
                    prev8 = h_ref[:, cs] * live if first else x_ref[pl.ds(r0 - 8, 8), cs]
                    y = _conv_pre(_conv_taps(x_ref[rows, cs], prev8), cw_ref[:, p * D_HALF + h * HEAD:p * D_HALF + (h + 1) * HEAD])
                    sv = y * _sigmoid(y)
                    o_ref[rows, cs] = sv if p == 2 else sv * lax.rsqrt(_rowsum(sv * sv) + EPS)
            ba = ba_ref[rows, :]
            for h in HEADS:
                beta = _sigmoid(ba[:, h:h + 1])
                gl = -jnp.exp(al_ref[0:1, h:h + 1]) * _softplus(ba[:, N_HEADS + h:N_HEADS + h + 1] + dtb_ref[0:1, h:h + 1])
                beta_ref[rows, HEAD_COLS[h]] = jnp.broadcast_to(beta, (CONV_SUB, HEAD))
                g_ref[rows, HEAD_COLS[h]] = jnp.broadcast_to(gl, (CONV_SUB, HEAD))

        sub_tile(0, True)

        def step(k, carry):
            sub_tile(pl.multiple_of(k * CONV_SUB, CONV_SUB), False)
            return carry

        lax.fori_loop(1, t // CONV_SUB, step, 0)

    row = pl.BlockSpec((t, D_HALF), lambda i: (i, 0))
    return _call(
        body, name="conv_fwd", grid=(s // t,),
        in_specs=_conv_specs(t) + [pl.BlockSpec((t, HEAD), lambda i: (i, COL_BA // HEAD)),
                                   pl.BlockSpec((CONV_K, 3 * D_HALF), lambda i: (0, 0)),
                                   pl.BlockSpec((1, N_HEADS), lambda i: (0, 0)),
                                   pl.BlockSpec((1, N_HEADS), lambda i: (0, 0))],
        out_specs=[row] * 5,
        out_shape=[_sds((s, D_HALF))] * 5,
        compiler_params=_params("arbitrary"),
    )(proj, proj, proj, proj, proj, proj, proj, conv_w, a_log, dt_bias)


def _pair_masks():
    r = lax.broadcasted_iota(I32, (PAIR, PAIR), 0)
    c = lax.broadcasted_iota(I32, (PAIR, PAIR), 1)
    same = jnp.right_shift(r, 6) == jnp.right_shift(c, 6)
    return same, same & (r >= c), same & (r > c), r == c


def _run(stages):
    for _ in stages:
        pass


def _interleave(*stage_lists):
    live = list(stage_lists)
    while live:
        for gen in list(live):
            try:
                next(gen)
            except StopIteration:
                live.remove(gen)


def _pair_common_stages(cm, qn, kn, vs, beta, g):
    same, incl, strict, eye = _pair_masks()
    incl_b = incl.astype(BF16)
    first = lax.broadcasted_iota(I32, (PAIR, HEAD), 0) < CHUNK
    cm.update(same=same, incl=incl, strict=strict, eye=eye)
    gc = _each(lambda gv: _mask_dot(incl_b, gv), g)
    q = _each(lambda v: v * QK_SCALE, qn)
    kb = _each(lambda k, b: k * b, kn, beta)
    cm.update(gc=gc, q=q, kb=kb, vb=_each(lambda v, b: v * b, vs, beta))
    yield
    cm.update(kk=_each(_bdot_nt, kb, kn), qk=_each(_bdot_nt, q, kn))
    gc_row = _each(lambda v: _colsum(jnp.where(eye, v, 0.0)), gc)
    gl = _each(lambda v: jnp.where(first, v[CHUNK - 1:CHUNK], v[PAIR - 1:PAIR]), gc)
    egc = _each(jnp.exp, gc)
    cm.update(gl=gl, egc=egc,
              decay=_each(lambda v, r: jnp.where(incl, jnp.exp(jnp.where(incl, v - r, 0.0)), 0.0), gc, gc_row))
    yield
    cm.update(ekd=_each(lambda a, b: jnp.exp(a - b), gl, gc), cd=_each(jnp.exp, gl),
              kbg=_each(lambda k, e: k * e, kb, egc))
    yield


def _pair_common(qn, kn, vs, beta, g):
    cm = {}
    _run(_pair_common_stages(cm, qn, kn, vs, beta, g))
    return cm


def _tri_inv_stages(out, a, eye_f):
    p = _each(lambda v: eye_f - v, a)
    x = _each(_bdot, a, a)
    yield
    for it in range(5):
        p = _each(lambda pv, xv: pv + _bdot(pv, xv), p, x)
        if it < 4:
            x = _each(_bdot, x, x)
        yield
    out["t"] = p


def _tri_inv(a, eye_f):
    out = {}
    _run(_tri_inv_stages(out, a, eye_f))
    return out["t"]


def _pair_spec():
    return pl.BlockSpec((INTRA_PAIRS * PAIR, D_HALF), lambda i: (i, 0))


def _chunk_scalar_spec(pairs=1, index=lambda i: (i, 0)):
    return pl.BlockSpec((16 * pairs, D_HALF), index)


SCAN_PAIRS = 2
SCAN_ROWS = SCAN_PAIRS * PAIR


def _intra_fwd(qn, kn, vs, beta, g):
    s = qn.shape[0]

    def body(qn_ref, kn_ref, vs_ref, beta_ref, g_ref, u_ref, w_ref, att_ref, qd_ref, kd_ref, t_ref, cd_ref):
        kn = _heads(kn_ref)
        cm = _pair_common(_heads(qn_ref), kn, _heads(vs_ref), _heads(beta_ref), _heads(g_ref))
        a = _each(lambda kk, d: jnp.where(cm["strict"], kk * d, 0.0), cm["kk"], cm["decay"])
        tm = _tri_inv(a, cm["eye"].astype(F32))
        _put_heads(t_ref, tm)
        _put_heads(u_ref, _each(_bdot, tm, cm["vb"]))
        _put_heads(w_ref, _each(_bdot, tm, cm["kbg"]))
        _put_heads(att_ref, _each(lambda a, b: a * b, cm["qk"], cm["decay"]))
        _put_heads(qd_ref, _each(lambda a, b: a * b, cm["q"], cm["egc"]))
        _put_heads(kd_ref, _each(lambda a, b: a * b, kn, cm["ekd"]))
        for ci in range(2):
            for (pp, h), v in zip(UNITS, cm["cd"]):
                cd_ref[pp * 16 + ci * 8:pp * 16 + (ci + 1) * 8, HEAD_COLS[h]] = v[ci * CHUNK:ci * CHUNK + 8]

    return _call(
        body, name="intra_fwd", grid=(s // (INTRA_PAIRS * PAIR),),
        in_specs=[_pair_spec()] * 5, out_specs=[_pair_spec()] * 6 + [_chunk_scalar_spec(INTRA_PAIRS)],
        out_shape=[_sds((s, D_HALF))] + [_sds((s, D_HALF), BF16)] * 5 + [_sds((s // 8, D_HALF))],
        compiler_params=_params("arbitrary"),
    )(qn, kn, vs, beta, g)


def _scan_fwd(u, w, att, qd, kd, cd):
    s = u.shape[0]
    n_chunks = s // CHUNK

    def body(u_ref, w_ref, att_ref, qd_ref, kd_ref, cd_ref, o_ref, vn_ref, st_ref, state):
        @pl.when(pl.program_id(0) == 0)
        def _():
            state[...] = jnp.zeros_like(state)
        cols = list(enumerate(HEAD_COLS))
        sm = [state[h] for h in HEADS]
        for ci in range(2 * SCAN_PAIRS):
            rs = slice(ci * CHUNK, (ci + 1) * CHUNK)
            for h in HEADS:
                st_ref[ci, h] = sm[h]
            both = [_bdot(jnp.concatenate([w_ref[rs, sl], qd_ref[rs, sl]], axis=0), sm[h]) for h, sl in cols]
            vn = [u_ref[rs, sl] - both[h][:CHUNK] for h, sl in cols]
            for h, sl in cols:
                vn_ref[rs, sl] = vn[h].astype(BF16)
                o_ref[rs, sl] = both[h][CHUNK:]
            sm = [sm[h] * cd_ref[ci * 8:ci * 8 + 1, sl] + _bdot_tn(kd_ref[rs, sl], vn[h]) for h, sl in cols]
        for h in HEADS:
            state[h] = sm[h]
        for pp in range(SCAN_PAIRS):
            rp = slice(pp * PAIR, (pp + 1) * PAIR)
            intra = [_bdot(att_ref[rp, sl], vn_ref[rp, sl]) for sl in HEAD_COLS]
            for h, sl in cols:
                o_ref[rp, sl] += intra[h]

    rows = pl.BlockSpec((SCAN_ROWS, D_HALF), lambda i: (i, 0))
    return _call(
        body, name="scan_fwd", grid=(s // SCAN_ROWS,),
        in_specs=[rows] * 5 + [_chunk_scalar_spec(SCAN_PAIRS)],
        out_specs=[rows, rows, pl.BlockSpec((2 * SCAN_PAIRS, N_HEADS, HEAD, HEAD), lambda i: (i, 0, 0, 0))],
        out_shape=[_sds((s, D_HALF)), _sds((s, D_HALF), BF16), _sds((n_chunks, N_HEADS, HEAD, HEAD))],
        scratch_shapes=[pltpu.VMEM((N_HEADS, HEAD, HEAD), F32)],
        compiler_params=_params("arbitrary"),
    )(u, w, att, qd, kd, cd)


def _delta_fwd(qn, kn, vs, beta, g):
    s = qn.shape[0]
    n_steps = s // SCAN_ROWS
    n_chunks = s // CHUNK
    assert INTRA_PAIRS == SCAN_PAIRS

    def body(qn_ref, kn_ref, vs_ref, beta_ref, g_ref, w_ref, att_ref, qd_ref, kd_ref, t_ref, cd_ref, o_ref, vn_ref, st_ref,
             state, u_s, w_s, att_s, qd_s, kd_s, cd_s):
        t = pl.program_id(0)

        @pl.when(t <= 1)
        def _():
            state[...] = jnp.zeros_like(state)

        @pl.when(t == 0)
        def _():
            for ref in (u_s, w_s, att_s, qd_s, kd_s, cd_s):
                ref[1] = jnp.zeros(ref.shape[1:], ref.dtype)

        cur = lax.rem(t, 2)
        prev = 1 - cur
        cols = list(enumerate(HEAD_COLS))

        def recurrence():
            sm = [state[h] for h in HEADS]
            for ci in range(2 * SCAN_PAIRS):
                rs = slice(ci * CHUNK, (ci + 1) * CHUNK)
                for h in HEADS:
                    st_ref[ci, h] = sm[h]
                both = [_bdot(jnp.concatenate([w_s[prev, rs, sl], qd_s[prev, rs, sl]], axis=0), sm[h]) for h, sl in cols]
                vn = [u_s[prev, rs, sl] - both[h][:CHUNK] for h, sl in cols]
                for h, sl in cols:
                    vn_ref[rs, sl] = vn[h].astype(BF16)
                    o_ref[rs, sl] = both[h][CHUNK:]
                yield
                sm = [sm[h] * cd_s[prev, ci * 8:ci * 8 + 1, sl] + _bdot_tn(kd_s[prev, rs, sl], vn[h]) for h, sl in cols]
                yield
            for h in HEADS:
                state[h] = sm[h]
            for pp in range(SCAN_PAIRS):
                rp = slice(pp * PAIR, (pp + 1) * PAIR)
                intra = [_bdot(att_s[prev, rp, sl], vn_ref[rp, sl]) for sl in HEAD_COLS]
                for h, sl in cols:
                    o_ref[rp, sl] += intra[h]
                yield

        def factors():
            kn = _heads(kn_ref)
            cm = {}
            yield from _pair_common_stages(cm, _heads(qn_ref), kn, _heads(vs_ref), _heads(beta_ref), _heads(g_ref))
            a = _each(lambda kk, d: jnp.where(cm["strict"], kk * d, 0.0), cm["kk"], cm["decay"])
            inv = {}
            yield from _tri_inv_stages(inv, a, cm["eye"].astype(F32))
            tm = inv["t"]
            res = dict(u=_each(_bdot, tm, cm["vb"]), w=_each(_bdot, tm, cm["kbg"]),
                       att=_each(lambda a, b: a * b, cm["qk"], cm["decay"]),
                       qd=_each(lambda a, b: a * b, cm["q"], cm["egc"]), kd=_each(lambda a, b: a * b, kn, cm["ekd"]))
            yield
            _put_heads(t_ref, tm)
            for key, out, keep in (("w", w_ref, w_s), ("att", att_ref, att_s), ("qd", qd_ref, qd_s), ("kd", kd_ref, kd_s)):
                _put_heads(out, res[key])
                for (pp, h), v in zip(UNITS, res[key]):
                    keep[cur, pp * PAIR:(pp + 1) * PAIR, HEAD_COLS[h]] = v.astype(BF16)
            for (pp, h), v in zip(UNITS, res["u"]):
                u_s[cur, pp * PAIR:(pp + 1) * PAIR, HEAD_COLS[h]] = v
            for ci in range(2):
                for (pp, h), v in zip(UNITS, cm["cd"]):
                    rows8 = slice(pp * 16 + ci * 8, pp * 16 + (ci + 1) * 8)
                    cd_ref[rows8, HEAD_COLS[h]] = v[ci * CHUNK:ci * CHUNK + 8]
                    cd_s[cur, rows8, HEAD_COLS[h]] = v[ci * CHUNK:ci * CHUNK + 8]
            yield

        _interleave(recurrence(), factors())

    last = n_steps - 1
    now = lambda i: (jnp.minimum(i, last), 0)
    before = lambda i: (jnp.maximum(i - 1, 0), 0)
    rows = lambda index: pl.BlockSpec((SCAN_ROWS, D_HALF), index)
    slot = lambda r, dtype: pltpu.VMEM((2, r, D_HALF), dtype)
    return _call(
        body, name="delta_fwd", grid=(n_steps + 1,),
        in_specs=[rows(now)] * 5,
        out_specs=[rows(now)] * 5 + [_chunk_scalar_spec(SCAN_PAIRS, now), rows(before), rows(before),
                                     pl.BlockSpec((2 * SCAN_PAIRS, N_HEADS, HEAD, HEAD), lambda i: (jnp.maximum(i - 1, 0), 0, 0, 0))],
        out_shape=[_sds((s, D_HALF), BF16)] * 5 + [_sds((s // 8, D_HALF)), _sds((s, D_HALF)), _sds((s, D_HALF), BF16),
                                                  _sds((n_chunks, N_HEADS, HEAD, HEAD))],
        scratch_shapes=[pltpu.VMEM((N_HEADS, HEAD, HEAD), F32), slot(SCAN_ROWS, F32), slot(SCAN_ROWS, BF16),
                        slot(SCAN_ROWS, BF16), slot(SCAN_ROWS, BF16), slot(SCAN_ROWS, BF16), slot(16 * SCAN_PAIRS, F32)],
        compiler_params=_params("arbitrary"),
    )(qn, kn, vs, beta, g)


OUT_T = 512


def _out_fwd_bwd(x, y_pool, o, proj, target, w_out, dn_norm_w, final_norm_w):
    s = x.shape[0]
    t = OUT_T

    def body(x_ref, yp_ref, o_ref, z_ref, tg_ref, wo_ref, dnw_ref, fnw_ref,
             gwo_ref, dh_ref, dyp_ref, do_ref, dz_ref, loss_ref, gfn_ref, gdn_ref, y_ref, yt_ref, gwo_acc):
        @pl.when(pl.program_id(0) == 0)
        def _():
            loss_ref[...] = jnp.zeros_like(loss_ref)
            gfn_ref[...] = jnp.zeros_like(gfn_ref)
            gdn_ref[...] = jnp.zeros_like(gdn_ref)
            gwo_acc[...] = jnp.zeros_like(gwo_acc)

        ypv = yp_ref[...]
        y_ref[:, :D_HALF] = ypv.astype(BF16)
        yt_ref[:D_HALF, :] = ypv.T.astype(BF16)
        dnw = dnw_ref[...]
        keep = []
        for h in HEADS:
            ov = o_ref[:, HEAD_COLS[h]]
            zv = z_ref[:, HEAD_COLS[h]]
            ro = lax.rsqrt(jnp.mean(ov * ov, axis=-1, keepdims=True) + EPS)
            ohat = ov * ro
            sg = _sigmoid(zv)
            keep.append((ro, ohat, zv, sg))
            ydn = ohat * dnw * (zv * sg)
            y_ref[:, D_HALF + h * HEAD:D_HALF + (h + 1) * HEAD] = ydn.astype(BF16)
            yt_ref[D_HALF + h * HEAD:D_HALF + (h + 1) * HEAD, :] = ydn.T.astype(BF16)

        hv = x_ref[...] + jnp.dot(y_ref[...], wo_ref[...], preferred_element_type=F32)
        r2 = lax.rsqrt(jnp.mean(hv * hv, axis=-1, keepdims=True) + EPS)
        hhat = hv * r2
        fnw = fnw_ref[...]
        err = hhat * fnw - tg_ref[...]
        loss_ref[...] += 0.5 * jnp.sum(_rowsum(err * err) * (1.0 / D_MODEL), axis=0, keepdims=True)
        dout = err * (1.0 / D_MODEL)
        gfn_ref[...] += _colsum(dout * hhat)
        dhh = dout * fnw
        dh = r2 * (dhh - hhat * jnp.mean(dhh * hhat, axis=-1, keepdims=True))
        dh_ref[...] = dh
        gwo_acc[...] += _bdot(yt_ref[...], dh)

        @pl.when(pl.program_id(0) == pl.num_programs(0) - 1)
        def _():
            gwo_ref[...] = gwo_acc[...].astype(BF16)

        dy = _bdot_nt(dh, wo_ref[...])
        dyp_ref[...] = dy[:, :D_HALF]
        gdn = jnp.zeros((1, HEAD), F32)
        for h in HEADS:
            ro, ohat, zv, sg = keep[h]
            dyd = dy[:, D_HALF + h * HEAD:D_HALF + (h + 1) * HEAD]
            sz = zv * sg
            dz_ref[:, HEAD_COLS[h]] = (dyd * ohat * dnw * (sg * (1.0 + zv * (1.0 - sg)))).astype(BF16)
            gdn = gdn + _colsum(dyd * ohat * sz)
            doh = dyd * dnw * sz
            do_ref[:, HEAD_COLS[h]] = ro * (doh - ohat * jnp.mean(doh * ohat, axis=-1, keepdims=True))
        gdn_ref[...] += gdn

    wide = pl.BlockSpec((t, D_MODEL), lambda i: (i, 0))
    half = pl.BlockSpec((t, D_HALF), lambda i: (i, 0))
    const = lambda shape: pl.BlockSpec(shape, lambda i: (0,) * len(shape))
    return _call(
        body, name="out_fwd_bwd", grid=(s // t,),
        in_specs=[wide, half, half, pl.BlockSpec((t, D_HALF), lambda i: (i, 5)), wide,
                  const((D_MODEL, D_MODEL)), const((1, HEAD)), const((1, D_MODEL))],
        out_specs=[const((D_MODEL, D_MODEL)), wide, half, half, half,
                   const((1, HEAD)), const((1, D_MODEL)), const((1, HEAD))],
        out_shape=[_sds((D_MODEL, D_MODEL), BF16), _sds((s, D_MODEL)), _sds((s, D_HALF)), _sds((s, D_HALF)),
                   _sds((s, D_HALF), BF16), _sds((1, HEAD)), _sds((1, D_MODEL)), _sds((1, HEAD))],
        scratch_shapes=[pltpu.VMEM((t, D_MODEL), BF16), pltpu.VMEM((D_MODEL, t), BF16), pltpu.VMEM((D_MODEL, D_MODEL), F32)],
        compiler_params=_params("arbitrary"),
    )(x, y_pool, o, proj, target, w_out, dn_norm_w, final_norm_w)


def _token_matmul(name, at, pieces):
    m, s = at.shape
    n = len(pieces)
    tn, tk = D_HALF, 512

    def body(a_ref, *refs):
        p_refs, o_ref, acc = refs[:n], refs[n], refs[n + 1]

        @pl.when(pl.program_id(0) == 0)
        def _():
            acc[...] = jnp.zeros_like(acc)

        av = a_ref[...]
        for p in range(n):
            acc[:, p * tn:(p + 1) * tn] += _bdot(av, p_refs[p][...])

        @pl.when(pl.program_id(0) == pl.num_programs(0) - 1)
        def _():
            o_ref[...] = acc[...].astype(BF16)

    return _call(
        body, name=name, grid=(s // tk,),
        in_specs=[pl.BlockSpec((m, tk), lambda k: (0, k))]
                 + [pl.BlockSpec((tk, tn), functools.partial(lambda k, cb: (k, cb), cb=cb)) for _, cb in pieces],
        out_specs=pl.BlockSpec((m, n * tn), lambda k: (0, 0)),
        out_shape=_sds((m, n * tn), BF16),
        scratch_shapes=[pltpu.VMEM((m, n * tn), F32)],
        compiler_params=_params("arbitrary"),
    )(at, *[p[0] for p in pieces])


def _grad_w_in(at, pieces):
    m, s = at.shape
    n = len(pieces)
    tn, tk = D_HALF, 512

    def body(a_ref, *refs):
        p_refs, o_ref, acc = refs[:n], refs[n], refs[n + 1]

        @pl.when(pl.program_id(0) == 0)
        def _():
            acc[...] = jnp.zeros_like(acc)

        av = a_ref[...]
        for p in range(n):
            acc[:, p * tn:(p + 1) * tn] += _bdot(av, p_refs[p][...])

        @pl.when(pl.program_id(0) == pl.num_programs(0) - 1)
        def _():
            for j in range(4):
                base = j * BLK_IN // HEAD * HEAD
                win = acc[:, base:base + BLK_IN_PAD]
                if j * BLK_IN > base:
                    win = pltpu.roll(win, BLK_IN_PAD - (j * BLK_IN - base), 1)
                o_ref[j] = win.astype(BF16)

    return _call(
        body, name="grad_w_in", grid=(s // tk,),
        in_specs=[pl.BlockSpec((m, tk), lambda k: (0, k))] + [pl.BlockSpec((tk, tn), lambda k: (k, 0))] * n,
        out_specs=pl.BlockSpec((4, m, BLK_IN_PAD), lambda k: (0, 0, 0)),
        out_shape=_sds((4, m, BLK_IN_PAD), BF16),
        scratch_shapes=[pltpu.VMEM((m, n * tn), F32)],
        compiler_params=_params("arbitrary"),
    )(at, *pieces)


def _pool_bwd(proj, dyp, pool_w, pool_scale):
    s = proj.shape[0]
    t = POOL_T
    hb = t // HEAD
    last = s // HEAD - 1

    def body(u_ref, z_ref, halo_ref, dy_ref, zn_ref, dyn_ref, pw_ref, ps_ref, band_ref, aband_ref,
             du_ref, dz_ref, gpw_ref, gps_ref):
        i = pl.program_id(0)

        @pl.when(i == 0)
        def _():
            gpw_ref[...] = jnp.zeros_like(gpw_ref)
            gps_ref[...] = jnp.zeros_like(gps_ref)

        live = (i > 0).astype(F32)
        more = (i < pl.num_programs(0) - 1).astype(F32)
        groups = lambda ref: [ref[:, sl] for sl in HEAD_COLS]
        z, ps, dy = groups(z_ref), groups(ps_ref), groups(dy_ref)
        pw = [pw_ref[g] for g in HEADS]
        mix, mixed, sg, cnt = _pool_mix(groups(u_ref), [h * live for h in groups(halo_ref)], z, pw,
                                        [band_ref[g] for g in HEADS], i * t)
        sz = _each(lambda a, b: a * b, z, sg)
        for sl, d, m, p, s_, zg in zip(HEAD_COLS, dy, mixed, ps, sg, z):
            dz_ref[:, sl] = (d * m * p * (s_ * (1.0 + zg * (1.0 - s_)))).astype(BF16)
        for sl, d, m, a in zip(HEAD_COLS, dy, mixed, sz):
            gps_ref[:, sl] += _colsum(d * m * a)
        dmixed = _each(lambda d, p, a: d * p * a, dy, ps, sz)
        for g, gp in enumerate(_each(_bdot_tn, mix, dmixed)):
            gpw_ref[g] += gp
        dmix = _each(_bdot_nt, dmixed, pw)
        dmix_n = _each(lambda d, p, zn, w_: _bdot_nt(d * more * p * (zn * _sigmoid(zn)), w_),
                       groups(dyn_ref), ps, groups(zn_ref), pw)
        scaled = [jnp.concatenate([a / c, b * (1.0 / w)], axis=0) for a, c, b, w in zip(dmix, cnt, dmix_n, WINDOWS)]
        du = _each(lambda b, s_, d: _mask_dot(b, s_) - d, [aband_ref[g] for g in HEADS], scaled, dmix)
        for sl, v in zip(HEAD_COLS, du):
            du_ref[:, sl] = v.astype(BF16)

    tile = lambda col: pl.BlockSpec((t, D_HALF), lambda i: (i, col))
    below = lambda col: pl.BlockSpec((HEAD, D_HALF), lambda i: (jnp.minimum((i + 1) * hb, last), col))
    return _call(
        body, name="pool_bwd", grid=(s // t,),
        in_specs=[tile(0), tile(1), pl.BlockSpec((HEAD, D_HALF), lambda i: (jnp.maximum(i * hb - 1, 0), 0)),
                  tile(0), below(1), below(0),
                  pl.BlockSpec((N_HEADS, HEAD, HEAD), lambda i: (0, 0, 0)), pl.BlockSpec((1, D_HALF), lambda i: (0, 0)),
                  pl.BlockSpec((N_HEADS, t, HEAD + t), lambda i: (0, 0, 0)),
                  pl.BlockSpec((N_HEADS, t, HEAD + t), lambda i: (0, 0, 0))],
        out_specs=[tile(0), tile(0), pl.BlockSpec((N_HEADS, HEAD, HEAD), lambda i: (0, 0, 0)),
                   pl.BlockSpec((1, D_HALF), lambda i: (0, 0))],
        out_shape=[_sds((s, D_HALF), BF16), _sds((s, D_HALF), BF16), _sds((N_HEADS, HEAD, HEAD)), _sds((1, D_HALF))],
        compiler_params=_params("arbitrary"),
    )(proj, proj, proj, dyp, proj, dyp, pool_w, pool_scale, _pool_bands(t), _pool_bands(t, anti=True))


def _scan_bwd(do, vn, qd, kd, w, att, cd, st):
    s = do.shape[0]
    n_steps = s // SCAN_ROWS

    def body(do_ref, vn_ref, qd_ref, kd_ref, w_ref, att_ref, cd_ref, st_ref,
             du_ref, dw_ref, datt_ref, dqd_ref, dkd_ref, dcd_ref, dstate):
        @pl.when(pl.program_id(0) == 0)
        def _():
            dstate[...] = jnp.zeros_like(dstate)
        _, incl, _, _ = _pair_masks()
        cols = list(enumerate(HEAD_COLS))
        dv_intra = []
        for pp in range(SCAN_PAIRS):
            rp = slice(pp * PAIR, (pp + 1) * PAIR)
            dv_intra.append([_bdot_tn(att_ref[rp, sl], do_ref[rp, sl]) for _, sl in cols])
            for _, sl in cols:
                datt_ref[rp, sl] = jnp.where(incl, _bdot_nt(do_ref[rp, sl], vn_ref[rp, sl]), 0.0)
        ds = [dstate[h] for h in HEADS]
        for ci in range(2 * SCAN_PAIRS - 1, -1, -1):
            rs = slice(ci * CHUNK, (ci + 1) * CHUNK)
            in_pair = slice((ci % 2) * CHUNK, (ci % 2 + 1) * CHUNK)
            sm = [st_ref[ci, h] for h in HEADS]
            dvn = [dv_intra[ci // 2][h][in_pair] + _bdot(kd_ref[rs, sl], ds[h]) for h, sl in cols]
            for h, sl in cols:
                du_ref[rs, sl] = dvn[h].astype(BF16)
            dqd = [_bdot_nt(do_ref[rs, sl], sm[h]) for h, sl in cols]
            dw = [-_bdot_nt(dvn[h], sm[h]) for h, _ in cols]
            dkd = [_bdot_nt(vn_ref[rs, sl], ds[h]) for h, sl in cols]
            dcd = [jnp.broadcast_to(_rowsum(_colsum(ds[h] * sm[h])), (8, HEAD)) for h in HEADS]
            for h, sl in cols:
                dqd_ref[rs, sl] = dqd[h]
                dw_ref[rs, sl] = dw[h].astype(BF16)
                dkd_ref[rs, sl] = dkd[h]
                dcd_ref[ci * 8:(ci + 1) * 8, sl] = dcd[h]
            ds = [ds[h] * cd_ref[ci * 8:ci * 8 + 1, sl] + _bdot_tn(qd_ref[rs, sl], do_ref[rs, sl])
                  - _bdot_tn(w_ref[rs, sl], dvn[h]) for h, sl in cols]
        for h in HEADS:
            dstate[h] = ds[h]

    rev = pl.BlockSpec((SCAN_ROWS, D_HALF), lambda i: (n_steps - 1 - i, 0))
    rev_scalar = _chunk_scalar_spec(SCAN_PAIRS, lambda i: (n_steps - 1 - i, 0))
    return _call(
        body, name="scan_bwd", grid=(n_steps,),
        in_specs=[rev] * 6 + [rev_scalar,
                              pl.BlockSpec((2 * SCAN_PAIRS, N_HEADS, HEAD, HEAD), lambda i: (n_steps - 1 - i, 0, 0, 0))],
        out_specs=[rev] * 5 + [rev_scalar],
        out_shape=[_sds((s, D_HALF), BF16)] * 2 + [_sds((s, D_HALF))] * 3 + [_sds((s // 8, D_HALF))],
        scratch_shapes=[pltpu.VMEM((N_HEADS, HEAD, HEAD), F32)],
        compiler_params=_params("arbitrary"),
    )(do, vn, qd, kd, w, att, cd, st)


def _intra_bwd(qn, kn, vs, beta, g, tm, du, dw, datt, dqd, dkd, dcd):
    s = qn.shape[0]

    def body(qn_ref, kn_ref, vs_ref, beta_ref, g_ref, t_ref, du_ref, dw_ref, datt_ref, dqd_ref, dkd_ref, dcd_ref,
             dqn_ref, dkn_ref, dvs_ref, dbeta_ref, dg_ref):
        ones = jnp.ones((PAIR, HEAD), BF16)
        tn = (((0,), (0,)), ((), ()))
        kn, vs, beta = _heads(kn_ref), _heads(vs_ref), _heads(beta_ref)
        cm = _pair_common(_heads(qn_ref), kn, vs, beta, _heads(g_ref))
        tmv, duv, dwv, dattv, dqdv, dkdv = (_heads(r) for r in (t_ref, du_ref, dw_ref, datt_ref, dqd_ref, dkd_ref))
        dvb = _each(_bdot_tn, tmv, duv)
        dt = _each(lambda a, b, c, d: _bdot_nt(a, b) + _bdot_nt(c, d), duv, cm["vb"], dwv, cm["kbg"])
        dkbg = _each(_bdot_tn, tmv, dwv)
        m1 = _each(_bdot_tn, tmv, dt)
        da = _each(lambda a, b: -jnp.where(cm["strict"], _bdot_nt(a, b), 0.0), m1, tmv)
        dkk = _each(lambda a, b: a * b, da, cm["decay"])
        dqk = _each(lambda a, b: a * b, dattv, cm["decay"])
        dd = _each(lambda a, b, c, d: a * b + c * d, dkk, cm["kk"], dqk, cm["qk"])
        dkb = _each(lambda a, b, c, d: _bdot(a, b) + c * d, dkk, kn, dkbg, cm["egc"])
        dq = _each(lambda a, b, c, d: _bdot(a, b) + c * d, dqk, kn, dqdv, cm["egc"])
        dkn = _each(lambda a, b, c, d: _bdot_tn(a, b) + _bdot_tn(c, d), dkk, cm["kb"], dqk, cm["q"])
        dkn = _each(lambda a, b, c, d, e: a + b * c + d * e, dkn, dkdv, cm["ekd"], dkb, beta)
        t_kd = _each(lambda a, b, c: _rowsum(a * b * c), dkdv, kn, cm["ekd"])
        split = _each(_split, dd)
        rows_dd = [jnp.dot(hi, ones, preferred_element_type=F32) + jnp.dot(lo, ones, preferred_element_type=F32)
                   for hi, lo in split]
        cols_dd = [lax.dot_general(hi, ones, tn, preferred_element_type=F32)
                   + lax.dot_general(lo, ones, tn, preferred_element_type=F32) for hi, lo in split]
        dgc = _each(lambda r, c, a, b, e, f, k, t: r - c + _rowsum(a * b * e) + _rowsum(f * k) - t,
                    rows_dd, cols_dd, dqdv, cm["q"], cm["egc"], dkbg, cm["kbg"], t_kd)
        same_b = cm["same"].astype(BF16)
        rowi = lax.broadcasted_iota(I32, (PAIR, HEAD), 0)
        dcd = _each(lambda d: jnp.where(rowi < CHUNK, d[0:1], d[8:9]), _heads(dcd_ref, rows=16))
        dgl = _each(lambda t, d, c: _mask_dot(same_b, jnp.broadcast_to(t, (PAIR, HEAD))) + d * c, t_kd, dcd, cm["cd"])
        is_last = jnp.bitwise_and(rowi, CHUNK - 1) == CHUNK - 1
        dgc = _each(lambda a, b: a + jnp.where(is_last, b, 0.0), dgc, dgl)
        r = lax.broadcasted_iota(I32, (PAIR, PAIR), 0)
        c = lax.broadcasted_iota(I32, (PAIR, PAIR), 1)
        upper_b = (cm["same"] & (r <= c)).astype(BF16)
        _put_heads(dg_ref, _each(lambda v: _mask_dot(upper_b, v), dgc))
        _put_heads(dbeta_ref, _each(lambda a, b, c, d: jnp.broadcast_to(_rowsum(a * b) + _rowsum(c * d), (PAIR, HEAD)),
                                    dkb, kn, dvb, vs))
        _put_heads(dqn_ref, _each(lambda v: v * QK_SCALE, dq))
        _put_heads(dkn_ref, dkn)
        _put_heads(dvs_ref, _each(lambda a, b: a * b, dvb, beta))

    return _call(
        body, name="intra_bwd", grid=(s // (INTRA_PAIRS * PAIR),),
        in_specs=[_pair_spec()] * 11 + [_chunk_scalar_spec(INTRA_PAIRS)], out_specs=[_pair_spec()] * 5,
        out_shape=[_sds((s, D_HALF))] * 5,
        compiler_params=_params("arbitrary"),
    )(qn, kn, vs, beta, g, tm, du, dw, datt, dqd, dkd, dcd)


def _delta_bwd(do, vn, qd, kd, w, att, cd, st, qn, kn, vs, beta, g, tm):
    s = do.shape[0]
    n_steps = s // SCAN_ROWS
    assert INTRA_PAIRS == SCAN_PAIRS

    def body(do_ref, vn_ref, qd_ref, kd_ref, w_ref, att_ref, cd_ref, st_ref, qn_ref, kn_ref, vs_ref, beta_ref, g_ref, t_ref,
             dqn_ref, dkn_ref, dvs_ref, dbeta_ref, dg_ref, dstate, du_s, dw_s, datt_s, dqd_s, dkd_s, dcd_s):
        t = pl.program_id(0)

        @pl.when(t == 0)
        def _():
            dstate[...] = jnp.zeros_like(dstate)
            for ref in (du_s, dw_s, datt_s, dqd_s, dkd_s, dcd_s):
                ref[1] = jnp.zeros(ref.shape[1:], ref.dtype)

        cur = lax.rem(t, 2)
        prev = 1 - cur
        cols = list(enumerate(HEAD_COLS))
        _, incl, _, _ = _pair_masks()

        def recurrence():
            dv_intra = []
            for pp in range(SCAN_PAIRS):
                rp = slice(pp * PAIR, (pp + 1) * PAIR)
                dv_intra.append([_bdot_tn(att_ref[rp, sl], do_ref[rp, sl]) for _, sl in cols])
                for _, sl in cols:
                    datt_s[cur, rp, sl] = jnp.where(incl, _bdot_nt(do_ref[rp, sl], vn_ref[rp, sl]), 0.0)
                yield
            ds = [dstate[h] for h in HEADS]
            for ci in range(2 * SCAN_PAIRS - 1, -1, -1):
                rs = slice(ci * CHUNK, (ci + 1) * CHUNK)
                in_pair = slice((ci % 2) * CHUNK, (ci % 2 + 1) * CHUNK)
                sm = [st_ref[ci, h] for h in HEADS]
                dvn = [dv_intra[ci // 2][h][in_pair] + _bdot(kd_ref[rs, sl], ds[h]) for h, sl in cols]
                dqd = [_bdot_nt(do_ref[rs, sl], sm[h]) for h, sl in cols]
                dkd = [_bdot_nt(vn_ref[rs, sl], ds[h]) for h, sl in cols]
                dcd = [jnp.broadcast_to(_rowsum(_colsum(ds[h] * sm[h])), (8, HEAD)) for h in HEADS]
                yield
                dw = [-_bdot_nt(dvn[h], sm[h]) for h, _ in cols]
                for h, sl in cols:
                    du_s[cur, rs, sl] = dvn[h].astype(BF16)
                    dqd_s[cur, rs, sl] = dqd[h]
                    dw_s[cur, rs, sl] = dw[h].astype(BF16)
                    dkd_s[cur, rs, sl] = dkd[h]
                    dcd_s[cur, ci * 8:(ci + 1) * 8, sl] = dcd[h]
                ds = [ds[h] * cd_ref[ci * 8:ci * 8 + 1, sl] + _bdot_tn(qd_ref[rs, sl], do_ref[rs, sl])
                      - _bdot_tn(w_ref[rs, sl], dvn[h]) for h, sl in cols]
                yield
            for h in HEADS:
                dstate[h] = ds[h]

        def factors():
            ones = jnp.ones((PAIR, HEAD), BF16)
            tn = (((0,), (0,)), ((), ()))
            kept = lambda ref, rows=PAIR: [ref[prev, pp * rows:(pp + 1) * rows, HEAD_COLS[h]] for pp, h in UNITS]
            kn, vs, beta = _heads(kn_ref), _heads(vs_ref), _heads(beta_ref)
            cm = {}
            yield from _pair_common_stages(cm, _heads(qn_ref), kn, vs, beta, _heads(g_ref))
            tmv = _heads(t_ref)
            duv, dwv, dattv, dqdv, dkdv = kept(du_s), kept(dw_s), kept(datt_s), kept(dqd_s), kept(dkd_s)
            dvb = _each(_bdot_tn, tmv, duv)
            dt = _each(lambda a, b, c, d: _bdot_nt(a, b) + _bdot_nt(c, d), duv, cm["vb"], dwv, cm["kbg"])
            dkbg = _each(_bdot_tn, tmv, dwv)
            yield
            m1 = _each(_bdot_tn, tmv, dt)
            yield
            da = _each(lambda a, b: -jnp.where(cm["strict"], _bdot_nt(a, b), 0.0), m1, tmv)
            yield
            dkk = _each(lambda a, b: a * b, da, cm["decay"])
            dqk = _each(lambda a, b: a * b, dattv, cm["decay"])
            dd = _each(lambda a, b, c, d: a * b + c * d, dkk, cm["kk"], dqk, cm["qk"])
            dkb = _each(lambda a, b, c, d: _bdot(a, b) + c * d, dkk, kn, dkbg, cm["egc"])
            dq = _each(lambda a, b, c, d: _bdot(a, b) + c * d, dqk, kn, dqdv, cm["egc"])
            yield
            dkn = _each(lambda a, b, c, d: _bdot_tn(a, b) + _bdot_tn(c, d), dkk, cm["kb"], dqk, cm["q"])
            dkn = _each(lambda a, b, c, d, e: a + b * c + d * e, dkn, dkdv, cm["ekd"], dkb, beta)
            t_kd = _each(lambda a, b, c: _rowsum(a * b * c), dkdv, kn, cm["ekd"])
            yield
            split = _each(_split, dd)
            rows_dd = [jnp.dot(hi, ones, preferred_element_type=F32) + jnp.dot(lo, ones, preferred_element_type=F32)
                       for hi, lo in split]
            cols_dd = [lax.dot_general(hi, ones, tn, preferred_element_type=F32)
                       + lax.dot_general(lo, ones, tn, preferred_element_type=F32) for hi, lo in split]
            yield
            dgc = _each(lambda r, c, a, b, e, f, k, tk: r - c + _rowsum(a * b * e) + _rowsum(f * k) - tk,
                        rows_dd, cols_dd, dqdv, cm["q"], cm["egc"], dkbg, cm["kbg"], t_kd)
            same_b = cm["same"].astype(BF16)
            rowi = lax.broadcasted_iota(I32, (PAIR, HEAD), 0)
            dcd = _each(lambda d: jnp.where(rowi < CHUNK, d[0:1], d[8:9]), kept(dcd_s, rows=16))
            dgl = _each(lambda tk, d, c: _mask_dot(same_b, jnp.broadcast_to(tk, (PAIR, HEAD))) + d * c, t_kd, dcd, cm["cd"])
            yield
            is_last = jnp.bitwise_and(rowi, CHUNK - 1) == CHUNK - 1
            dgc = _each(lambda a, b: a + jnp.where(is_last, b, 0.0), dgc, dgl)
            r = lax.broadcasted_iota(I32, (PAIR, PAIR), 0)
            c = lax.broadcasted_iota(I32, (PAIR, PAIR), 1)
            upper_b = (cm["same"] & (r <= c)).astype(BF16)
            _put_heads(dg_ref, _each(lambda v: _mask_dot(upper_b, v), dgc))
            yield
            _put_heads(dbeta_ref, _each(lambda a, b, c, d: jnp.broadcast_to(_rowsum(a * b) + _rowsum(c * d), (PAIR, HEAD)),
                                        dkb, kn, dvb, vs))
            _put_heads(dqn_ref, _each(lambda v: v * QK_SCALE, dq))
            _put_heads(dkn_ref, dkn)
            _put_heads(dvs_ref, _each(lambda a, b: a * b, dvb, beta))
            yield

        _interleave(recurrence(), factors())

    last = n_steps - 1
    now = lambda i: (jnp.maximum(last - i, 0), 0)
    after = lambda i: (jnp.minimum(n_steps - i, last), 0)
    rows = lambda index: pl.BlockSpec((SCAN_ROWS, D_HALF), index)
    slot = lambda r, dtype: pltpu.VMEM((2, r, D_HALF), dtype)
    return _call(
        body, name="delta_bwd", grid=(n_steps + 1,),
        in_specs=[rows(now)] * 6 + [_chunk_scalar_spec(SCAN_PAIRS, now),
                                    pl.BlockSpec((2 * SCAN_PAIRS, N_HEADS, HEAD, HEAD), lambda i: (jnp.maximum(last - i, 0), 0, 0, 0))]
                 + [rows(after)] * 6,
        out_specs=[rows(after)] * 5,
        out_shape=[_sds((s, D_HALF))] * 5,
        scratch_shapes=[pltpu.VMEM((N_HEADS, HEAD, HEAD), F32), slot(SCAN_ROWS, BF16), slot(SCAN_ROWS, BF16),
                        slot(SCAN_ROWS, F32), slot(SCAN_ROWS, F32), slot(SCAN_ROWS, F32), slot(16 * SCAN_PAIRS, F32)],
        compiler_params=_params("arbitrary"),
    )(do, vn, qd, kd, w, att, cd, st, qn, kn, vs, beta, g, tm)


def _rows8(x):
    acc = x[0:8]
    for r in range(8, x.shape[0], 8):
        acc = acc + x[r:r + 8]
    return acc


def _conv_bwd(proj, conv_w, a_log, dt_bias, dqn, dkn, dvs, dbeta, dg):
    s = proj.shape[0]
    t = CONV_T
    n_tiles = s // t
    n_sub = t // CONV_SUB
    tile_of = lambda i: n_tiles - 1 - i

    def body(q_ref, k_ref, v_ref, hq_ref, hk_ref, hv_ref, ba_ref, cw_ref, al_ref, dtb_ref,
             dqn_ref, dkn_ref, dvs_ref, dbeta_ref, dg_ref, oq_ref, ok_ref, ov_ref, dba_ref, gcw_out, gsm_out,
             below, gcw_ref, gsm_ref):
        @pl.when(pl.program_id(0) == 0)
        def _():
            gcw_ref[...] = jnp.zeros_like(gcw_ref)
            gsm_ref[...] = jnp.zeros_like(gsm_ref)
            below[...] = jnp.zeros_like(below)

        live = (pl.program_id(0) < n_tiles - 1).astype(F32)
        parts = ((q_ref, hq_ref, dqn_ref, oq_ref), (k_ref, hk_ref, dkn_ref, ok_ref), (v_ref, hv_ref, dvs_ref, ov_ref))
        lane = lax.broadcasted_iota(I32, (CONV_SUB, HEAD), 1)
        lane8 = lax.broadcasted_iota(I32, (8, HEAD), 1)

        def sub_tile(r0, first):
            rows = pl.ds(r0, CONV_SUB)
            for p, (x_ref, h_ref, d_ref, o_ref) in enumerate(parts):
                for h in HEADS:
                    cs = HEAD_COLS[h]
                    wide = slice(p * D_HALF + h * HEAD, p * D_HALF + (h + 1) * HEAD)
                    cw = cw_ref[:, wide]
                    prev8 = h_ref[:, cs] * live if first else x_ref[pl.ds(r0 - 8, 8), cs]
                    taps = _conv_taps(x_ref[rows, cs], prev8)
                    y = _conv_pre(taps, cw)
                    sg = _sigmoid(y)
                    sv = y * sg
                    ds = d_ref[rows, cs]
                    if p < 2:
                        rn = lax.rsqrt(_rowsum(sv * sv) + EPS)
                        nrm = sv * rn
                        ds = rn * (ds - nrm * _rowsum(ds * nrm))
                    dy = ds * (sg * (1.0 + y * (1.0 - sg)))
                    for j in range(CONV_K):
                        gcw_ref[8 * j:8 * j + 8, wide] += _rows8(dy * taps[j])
                    nxt = below[:, wide]
                    acc = dy * cw[CONV_K - 1:CONV_K]
                    for sft in range(1, CONV_K):
                        acc = acc + _shift_up(dy, nxt, sft) * cw[CONV_K - 1 - sft:CONV_K - sft]
                    o_ref[rows, cs] = acc.astype(BF16)
                    below[:, wide] = dy[0:8]

            ba = ba_ref[rows, :]
            dba = jnp.zeros((CONV_SUB, HEAD), F32)
            gsm = jnp.zeros((8, HEAD), F32)
            for h in HEADS:
                beta = _sigmoid(ba[:, h:h + 1])
                dbeta = dbeta_ref[rows, h * HEAD:h * HEAD + 1]
                xg = ba[:, N_HEADS + h:N_HEADS + h + 1] + dtb_ref[0:1, h:h + 1]
                nexp = -jnp.exp(al_ref[0:1, h:h + 1])
                dgv = dg_ref[rows, h * HEAD:h * HEAD + 1]
                da = dgv * nexp * _sigmoid(xg)
                dba = dba + jnp.where(lane == h, dbeta * beta * (1.0 - beta), 0.0) + jnp.where(lane == N_HEADS + h, da, 0.0)
                gsm = (gsm + jnp.where(lane8 == h, _rows8(dgv * nexp * _softplus(xg)), 0.0)
                       + jnp.where(lane8 == N_HEADS + h, _rows8(da), 0.0))
            dba_ref[rows, :] = jnp.zeros((CONV_SUB, D_HALF), BF16)
            dba_ref[rows, :HEAD] = dba.astype(BF16)
            gsm_ref[...] += gsm

        def step(k, carry):
            sub_tile(pl.multiple_of((n_sub - 1 - k) * CONV_SUB, CONV_SUB), False)
            return carry

        lax.fori_loop(0, n_sub - 1, step, 0)
        sub_tile(0, True)

        @pl.when(pl.program_id(0) == n_tiles - 1)
        def _():
            gcw_out[...] = jnp.zeros_like(gcw_out)
            for j in range(CONV_K):
                gcw_out[j:j + 1, :] = _colsum(gcw_ref[8 * j:8 * j + 8, :])
            gsm_out[...] = jnp.broadcast_to(_colsum(gsm_ref[...]), (8, HEAD))

    row = pl.BlockSpec((t, D_HALF), lambda i: (tile_of(i), 0))
    const = lambda shape: pl.BlockSpec(shape, lambda i: (0, 0))
    return _call(
        body, name="conv_bwd", grid=(n_tiles,),
        in_specs=_conv_specs(t, tile_of) + [pl.BlockSpec((t, HEAD), lambda i: (tile_of(i), COL_BA // HEAD)),
                                            const((CONV_K, 3 * D_HALF)), const((1, N_HEADS)), const((1, N_HEADS))] + [row] * 5,
        out_specs=[row, row, row, row, const((8, 3 * D_HALF)), const((8, HEAD))],
        out_shape=[_sds((s, D_HALF), BF16)] * 4 + [_sds((8, 3 * D_HALF)), _sds((8, HEAD))],
        scratch_shapes=[pltpu.VMEM((8, 3 * D_HALF), F32), pltpu.VMEM((8 * CONV_K, 3 * D_HALF), F32),
                        pltpu.VMEM((8, HEAD), F32)],
        compiler_params=_params("arbitrary"),
    )(proj, proj, proj, proj, proj, proj, proj, conv_w, a_log, dt_bias, dqn, dkn, dvs, dbeta, dg)


def _conv_bwd_pre(proj, conv_w, a_log, dt_bias, dqn, dkn, dvs, dbeta, dg):
    s = proj.shape[0]
    t = CONV_T

    def body(q_ref, k_ref, v_ref, hq_ref, hk_ref, hv_ref, ba_ref, cw_ref, al_ref, dtb_ref,
             dqn_ref, dkn_ref, dvs_ref, dbeta_ref, dg_ref, dyq_ref, dyk_ref, dyv_ref, dba_ref, gcw_ref, gsm_ref):
        @pl.when(pl.program_id(0) == 0)
        def _():
            gcw_ref[...] = jnp.zeros_like(gcw_ref)
            gsm_ref[...] = jnp.zeros_like(gsm_ref)

        live = (pl.program_id(0) > 0).astype(F32)
        parts = ((q_ref, hq_ref, dqn_ref, dyq_ref), (k_ref, hk_ref, dkn_ref, dyk_ref), (v_ref, hv_ref, dvs_ref, dyv_ref))
        for p, (x_ref, h_ref, d_ref, dy_ref) in enumerate(parts):
            cols = slice(p * D_HALF, (p + 1) * D_HALF)
            taps = _conv_taps(x_ref[...], h_ref[...] * live)
            y = _conv_pre(taps, cw_ref[:, cols])
            sg = _sigmoid(y)
            sv = y * sg
            if p == 2:
                ds = d_ref[...]
            else:
                segs = []
                for h in HEADS:
                    seg = _head(sv, h)
                    rn = lax.rsqrt(_rowsum(seg * seg) + EPS)
                    nrm = seg * rn
                    dn = d_ref[:, HEAD_COLS[h]]
                    segs.append(rn * (dn - nrm * _rowsum(dn * nrm)))
                ds = jnp.concatenate(segs, axis=1)
            dy = ds * (sg * (1.0 + y * (1.0 - sg)))
            dy_ref[...] = dy
            for j in range(CONV_K):
                gcw_ref[j:j + 1, cols] += _colsum(dy * taps[j])

        ba = ba_ref[...]
        lane = lax.broadcasted_iota(I32, (t, HEAD), 1)
        lane1 = lax.broadcasted_iota(I32, (1, HEAD), 1)
        dba = jnp.zeros((t, HEAD), F32)
        gsm = jnp.zeros((1, HEAD), F32)
        for h in HEADS:
            beta = _sigmoid(ba[:, h:h + 1])
            dbeta = dbeta_ref[:, h * HEAD:h * HEAD + 1]
            xg = ba[:, N_HEADS + h:N_HEADS + h + 1] + dtb_ref[0:1, h:h + 1]
            nexp = -jnp.exp(al_ref[0:1, h:h + 1])
            dgv = dg_ref[:, h * HEAD:h * HEAD + 1]
            da = dgv * nexp * _sigmoid(xg)
            dba = dba + jnp.where(lane == h, dbeta * beta * (1.0 - beta), 0.0) + jnp.where(lane == N_HEADS + h, da, 0.0)
            gsm = (gsm + jnp.where(lane1 == h, _colsum(dgv * nexp * _softplus(xg)), 0.0)
                   + jnp.where(lane1 == N_HEADS + h, _colsum(da), 0.0))
        dba_ref[...] = jnp.zeros_like(dba_ref)
        dba_ref[:, :HEAD] = dba.astype(BF16)
        gsm_ref[0:1, :] += gsm

    row = pl.BlockSpec((t, D_HALF), lambda i: (i, 0))
    return _call(
        body, name="conv_bwd_pre", grid=(s // t,),
        in_specs=_conv_specs(t) + [pl.BlockSpec((t, HEAD), lambda i: (i, COL_BA // HEAD)),
                                   pl.BlockSpec((CONV_K, 3 * D_HALF), lambda i: (0, 0)),
                                   pl.BlockSpec((1, N_HEADS), lambda i: (0, 0)),
                                   pl.BlockSpec((1, N_HEADS), lambda i: (0, 0))] + [row] * 5,
        out_specs=[row, row, row, row,
                   pl.BlockSpec((8, 3 * D_HALF), lambda i: (0, 0)), pl.BlockSpec((8, HEAD), lambda i: (0, 0))],
        out_shape=[_sds((s, D_HALF))] * 3 + [_sds((s, D_HALF), BF16), _sds((8, 3 * D_HALF)), _sds((8, HEAD))],
        compiler_params=_params("arbitrary"),
    )(proj, proj, proj, proj, proj, proj, proj, conv_w, a_log, dt_bias, dqn, dkn, dvs, dbeta, dg)


def _conv_bwd_in(dyq, dyk, dyv, conv_w):
    s = dyq.shape[0]
    t = CONV_T
    last = s // 8 - 1

    def body(q_ref, k_ref, v_ref, nq_ref, nk_ref, nv_ref, cw_ref, oq_ref, ok_ref, ov_ref):
        more = (pl.program_id(0) < pl.num_programs(0) - 1).astype(F32)
        for p, (d_ref, n_ref, o_ref) in enumerate(((q_ref, nq_ref, oq_ref), (k_ref, nk_ref, ok_ref), (v_ref, nv_ref, ov_ref))):
            cw = cw_ref[:, p * D_HALF:(p + 1) * D_HALF]
            dy = d_ref[...]
            nxt = n_ref[...] * more
            acc = dy * cw[3:4]
            for sft in (1, 2, 3):
                acc = acc + _shift_up(dy, nxt, sft) * cw[3 - sft:4 - sft]
            o_ref[...] = acc.astype(BF16)

    row = pl.BlockSpec((t, D_HALF), lambda i: (i, 0))
    nxt = pl.BlockSpec((8, D_HALF), lambda i: (jnp.minimum((i + 1) * (t // 8), last), 0))
    return _call(
        body, name="conv_bwd_in", grid=(s // t,),
        in_specs=[row] * 3 + [nxt] * 3 + [pl.BlockSpec((CONV_K, 3 * D_HALF), lambda i: (0, 0))],
        out_specs=[row] * 3, out_shape=[_sds((s, D_HALF), BF16)] * 3,
        compiler_params=_params("arbitrary"),
    )(dyq, dyk, dyv, dyq, dyk, dyv, conv_w)


IN_T = 512


def _in_bwd(x, dh, norm_w, w_pad, pieces):
    s = x.shape[0]
    t = IN_T
    widths = [D_HALF] * 6 + [N_IN_PAD - COL_BA]

    def body(*refs):
        x_ref, dh_ref, nw_ref, w_ref = refs[:4]
        p_refs = refs[4:4 + len(pieces)]
        gx_ref, gnw_ref = refs[4 + len(pieces):]

        @pl.when(pl.program_id(0) == 0)
        def _():
            gnw_ref[...] = jnp.zeros_like(gnw_ref)

        dn = jnp.zeros((t, D_MODEL), F32)
        col = 0
        for p_ref, wd in zip(p_refs, widths):
            dn = dn + _bdot_nt(p_ref[...], w_ref[:, col:col + wd])
            col += wd
        xv = x_ref[...]
        r = lax.rsqrt(jnp.mean(xv * xv, axis=-1, keepdims=True) + EPS)
        xhat = xv * r
        gnw_ref[...] += _colsum(dn * xhat)
        dxh = dn * nw_ref[...]
        gx_ref[...] = dh_ref[...] + r * (dxh - xhat * jnp.mean(dxh * xhat, axis=-1, keepdims=True))

    wide = pl.BlockSpec((t, D_MODEL), lambda i: (i, 0))
    return _call(
        body, name="in_bwd", grid=(s // t,),
        in_specs=[wide, wide, pl.BlockSpec((1, D_MODEL), lambda i: (0, 0)),
                  pl.BlockSpec((D_MODEL, N_IN_PAD), lambda i: (0, 0))]
                 + [pl.BlockSpec((t, wd), lambda i: (i, 0)) for wd in widths],
        out_specs=[wide, pl.BlockSpec((1, D_MODEL), lambda i: (0, 0))],
        out_shape=[_sds((s, D_MODEL)), _sds((1, D_MODEL))],
        compiler_params=_params("arbitrary"),
    )(x, dh, norm_w, w_pad, *pieces)


def _adamw_shard(name, w, g_own, g_got, cidx, m, v):
    _, r, c = w.shape
    half = r // 2
    rows = 256 if half % 256 == 0 else half
    per_half = half // rows

    def body(c_ref, w_ref, go_ref, gg_ref, m_ref, v_ref, gout_ref, d_ref, nm_ref, nv_ref):
        mine = (pl.program_id(0) // per_half) == c_ref[0]
        gv = jnp.where(mine, go_ref[:, :c], gg_ref[:, :c])
        gout_ref[0] = gv
        mn = ADAM_B1 * m_ref[0] + (1.0 - ADAM_B1) * gv
        vn = ADAM_B2 * v_ref[0] + (1.0 - ADAM_B2) * (gv * gv)
        m_hat = mn / (1.0 - ADAM_B1 ** ADAM_STEP)
        v_hat = vn / (1.0 - ADAM_B2 ** ADAM_STEP)
        d_ref[0] = -ADAM_LR * (m_hat / (jnp.sqrt(v_hat) + ADAM_EPS) + ADAM_WD * w_ref[0])
        nm_ref[0] = mn
        nv_ref[0] = vn

    blk = pl.BlockSpec((1, rows, c), lambda i, c_ref: (0, i, 0))
    gblk = pl.BlockSpec((rows, g_own.shape[1]), lambda i, c_ref: (i % per_half, 0))
    return _call(
        body, name=name,
        grid_spec=pltpu.PrefetchScalarGridSpec(
            num_scalar_prefetch=1, grid=(2 * per_half,),
            in_specs=[blk, gblk, gblk, blk, blk], out_specs=[blk] * 4),
        out_shape=[_sds((1, r, c))] * 4,
        compiler_params=_params("arbitrary"),
    )(cidx, w, g_own, g_got, m, v)


def _adamw_tiles(name, w, g, m, v):
    n = w.shape[0]
    nb = 77 if n % 77 == 0 else n

    def body(w_ref, g_ref, m_ref, v_ref, d_ref, nm_ref, nv_ref):
        gv = g_ref[...]
        mn = ADAM_B1 * m_ref[...] + (1.0 - ADAM_B1) * gv
        vn = ADAM_B2 * v_ref[...] + (1.0 - ADAM_B2) * (gv * gv)
        m_hat = mn / (1.0 - ADAM_B1 ** ADAM_STEP)
        v_hat = vn / (1.0 - ADAM_B2 ** ADAM_STEP)
        d_ref[...] = -ADAM_LR * (m_hat / (jnp.sqrt(v_hat) + ADAM_EPS) + ADAM_WD * w_ref[...])
        nm_ref[...] = mn
        nv_ref[...] = vn

    blk = pl.BlockSpec((nb, 8, HEAD), lambda i: (i, 0, 0))
    return _call(
        body, name=name, grid=(n // nb,),
        in_specs=[blk] * 4, out_specs=[blk] * 3, out_shape=[_sds(w.shape)] * 3,
        compiler_params=_params("arbitrary"),
    )(w, g, m, v)


def _exchange(name, inputs, out_shapes, phases):
    n_in = len(inputs)
    n_out = len(out_shapes)
    n_cp = sum(len(p) for p in phases)

    def body(*refs):
        ins, outs = refs[:n_in], refs[n_in:n_in + n_out]
        send, recv = refs[n_in + n_out:]
        pos = (lax.axis_index("x"), lax.axis_index("y"), lax.axis_index("c"))
        k = 0
        for phase in phases:
            cps = []
            for src, dst, target in phase:
                cps.append(pltpu.make_async_remote_copy(
                    src_ref=src(ins, outs, pos), dst_ref=dst(ins, outs, pos), send_sem=send.at[k], recv_sem=recv.at[k],
                    device_id=target(pos), device_id_type=pl.DeviceIdType.MESH))
                k += 1
            for cp in cps:
                cp.start()
            for cp in cps:
                cp.wait()

    anyspec = pl.BlockSpec(memory_space=pl.ANY)
    return _call(
        body, name=name,
        in_specs=[anyspec] * n_in, out_specs=[anyspec] * n_out, out_shape=list(out_shapes),
        scratch_shapes=[pltpu.SemaphoreType.DMA((n_cp,)), pltpu.SemaphoreType.DMA((n_cp,))],
    )(*inputs)


def _exchange_start(name, inputs, out_shapes, copies):
    n_in, n_out, n_cp = len(inputs), len(out_shapes), len(copies)

    def body(*refs):
        ins, lands = refs[:n_in], refs[n_in:n_in + n_out]
        sems = refs[n_in + n_out:n_in + n_out + 2 * n_cp]
        token = refs[-1]
        pos = (lax.axis_index("x"), lax.axis_index("y"), lax.axis_index("c"))
        for k, (src, dst, target) in enumerate(copies):
            pltpu.make_async_remote_copy(
                src_ref=src(ins, lands, pos), dst_ref=dst(ins, lands, pos), send_sem=sems[2 * k], recv_sem=sems[2 * k + 1],
                device_id=target(pos), device_id_type=pl.DeviceIdType.MESH).start()
        token[...] = jnp.zeros_like(token)

    hbm = pl.BlockSpec(memory_space=pltpu.HBM)
    sem = pl.BlockSpec(memory_space=pltpu.SEMAPHORE)
    bufs = list(inputs) + [lax.empty(o.shape, o.dtype) for o in out_shapes]
    outs = _call(
        body, name=name,
        out_shape=tuple([pltpu.SemaphoreType.DMA(())] * (2 * n_cp) + [pltpu.HBM(b.shape, b.dtype) for b in bufs]
                        + [_sds((8, HEAD))]),
        in_specs=[hbm] * len(bufs),
        out_specs=tuple([sem] * (2 * n_cp) + [hbm] * len(bufs) + [pl.BlockSpec(memory_space=pltpu.VMEM)]),
        input_output_aliases={i: 2 * n_cp + i for i in range(len(bufs))},
        compiler_params=pltpu.CompilerParams(has_side_effects=pltpu.SideEffectType.DATAFLOW_SIDE_EFFECTING),
    )(*[pltpu.with_memory_space_constraint(b, pltpu.HBM) for b in bufs])
    return outs[:2 * n_cp], outs[2 * n_cp:2 * n_cp + n_in], outs[2 * n_cp + n_in:-1], outs[-1]


def _exchange_wait(name, sems, sources, lands, copies, after):
    n_in, n_out, n_cp = len(sources), len(lands), len(copies)

    def body(*refs):
        ins, zones = refs[:n_in], refs[n_in:n_in + n_out]
        sem_refs = refs[n_in + n_out:n_in + n_out + 2 * n_cp]
        pos = (lax.axis_index("x"), lax.axis_index("y"), lax.axis_index("c"))
        for k, (src, dst, target) in enumerate(copies):
            cp = pltpu.make_async_remote_copy(
                src_ref=src(ins, zones, pos), dst_ref=dst(ins, zones, pos), send_sem=sem_refs[2 * k],
                recv_sem=sem_refs[2 * k + 1], device_id=target(pos), device_id_type=pl.DeviceIdType.MESH)
            cp.wait_send()
            cp.wait_recv()

    hbm = pl.BlockSpec(memory_space=pltpu.HBM)
    sem = pl.BlockSpec(memory_space=pltpu.SEMAPHORE)
    bufs = list(sources) + list(lands)
    outs = _call(
        body, name=name,
        out_shape=tuple(pltpu.HBM(b.shape, b.dtype) for b in bufs),
        in_specs=[hbm] * len(bufs) + [sem] * (2 * n_cp) + [pl.BlockSpec(memory_space=pl.ANY)],
        out_specs=tuple([hbm] * len(bufs)),
        input_output_aliases={i: i for i in range(len(bufs))},
        compiler_params=pltpu.CompilerParams(has_side_effects=pltpu.SideEffectType.DATAFLOW_SIDE_EFFECTING),
    )(*bufs, *sems, after)
    return outs[:n_in], outs[n_in:]


def _allreduce_tile(name, v):
    def body(v_ref, out_ref, slots, send, recv):
        x, y, c = lax.axis_index("x"), lax.axis_index("y"), lax.axis_index("c")
        me = 4 * x + 2 * y + c
        slots[me] = v_ref[...]
        cps = []
        for k in range(1, 8):
            peer = (x ^ (k >> 2), y ^ ((k >> 1) & 1), c ^ (k & 1))
            cps.append(pltpu.make_async_remote_copy(
                src_ref=v_ref, dst_ref=slots.at[me], send_sem=send.at[k - 1], recv_sem=recv.at[k - 1],
                device_id=peer, device_id_type=pl.DeviceIdType.MESH))
        for cp in cps:
            cp.start()
        for cp in cps:
            cp.wait()
        acc = slots[0]
        for i in range(1, 8):
            acc = acc + slots[i]
        out_ref[...] = acc

    vm = pl.BlockSpec(memory_space=pltpu.VMEM)
    return _call(
        body, name=name, in_specs=[vm], out_specs=vm, out_shape=_sds(v.shape),
        scratch_shapes=[pltpu.VMEM((8,) + v.shape, F32), pltpu.SemaphoreType.DMA((7,)), pltpu.SemaphoreType.DMA((7,))],
    )(v)


def _chip(pos):
    return 2 * pos[0] + pos[1]


def _other_chip(pos, mask):
    x, y, c = pos
    return (x ^ (mask >> 1), y ^ (mask & 1), c)


def _sibling(pos):
    return (pos[0], pos[1], 1 - pos[2])


def _gather_weights(wb, cb):
    rows = wb.shape[0] // 2
    x_nb, y_nb, diag = CHIP_MASKS

    def part(pos, mask, quarter=None):
        start = pos[2] * rows if quarter is None else pos[2] * rows + quarter * (rows // 2)
        return lambda outs: outs[0].at[_chip(pos) ^ mask, pl.ds(start, rows if quarter is None else rows // 2)]

    def passed_on(mask, to, quarter=None):
        return (lambda ins, outs, pos: part(pos, mask, quarter)(outs), lambda ins, outs, pos: part(pos, mask, quarter)(outs), to)

    first = [(lambda ins, outs, pos: ins[0].at[pl.ds(pos[2] * rows, rows)], lambda ins, outs, pos: part(pos, 0)(outs),
              functools.partial(_other_chip, mask=mask)) for mask in (x_nb, y_nb)]
    first += [(lambda ins, outs, pos: ins[1], lambda ins, outs, pos: outs[1].at[_chip(pos)],
               functools.partial(_other_chip, mask=mask)) for mask in CHIP_MASKS]
    second = [passed_on(x_nb, functools.partial(_other_chip, mask=y_nb), quarter=0),
              passed_on(y_nb, functools.partial(_other_chip, mask=x_nb), quarter=1),
              passed_on(x_nb, _sibling), passed_on(y_nb, _sibling)]
    third = [passed_on(diag, _sibling)]
    return _exchange("gather_weights", [wb, cb], [_sds((4,) + wb.shape, wb.dtype), _sds((4,) + cb.shape, cb.dtype)],
                     [first, second, third])


def _gather_blocks(ob):
    copies = [(lambda ins, outs, pos: ins[0], lambda ins, outs, pos: outs[0].at[_chip(pos)],
               functools.partial(_other_chip, mask=mask)) for mask in CHIP_MASKS]
    return [_sds((4,) + ob.shape, ob.dtype)], copies


def _to_sibling_half(name, arrays):
    def src(ins, outs, pos, a):
        h = arrays[a].shape[-2] // 2
        sl = pl.ds((1 - pos[2]) * h, h)
        return ins[a].at[:, sl] if arrays[a].ndim == 3 else ins[a].at[sl]

    outs = [_sds(a.shape[:-2] + (a.shape[-2] // 2, a.shape[-1]), a.dtype) for a in arrays]
    phase = [(functools.partial(src, a=a), lambda ins, outs, pos, a=a: outs[a], _sibling) for a in range(len(arrays))]
    return _exchange(name, arrays, outs, [phase])


def _add_half(name, full, part, cidx):
    shape = part.shape
    lead = shape[0] if len(shape) == 3 else 1
    rows, cols = shape[-2], shape[-1]
    tr = rows // 2 if rows % 16 == 0 else rows
    nr = rows // tr
    f3 = full.reshape((lead,) + full.shape[-2:])
    p3 = part.reshape((lead, rows, cols))

    def body(c_ref, f_ref, p_ref, o_ref):
        o_ref[...] = (f_ref[...].astype(F32) + p_ref[...].astype(F32)).astype(o_ref.dtype)

    out = _call(
        body, name=name,
        grid_spec=pltpu.PrefetchScalarGridSpec(
            num_scalar_prefetch=1, grid=(lead, nr),
            in_specs=[pl.BlockSpec((1, tr, cols), lambda b, r, c_ref: (b, c_ref[0] * nr + r, 0)),
                      pl.BlockSpec((1, tr, cols), lambda b, r, c_ref: (b, r, 0))],
            out_specs=pl.BlockSpec((1, tr, cols), lambda b, r, c_ref: (b, r, 0))),
        out_shape=_sds((lead, rows, cols), part.dtype),
        compiler_params=_params("arbitrary", "arbitrary"),
    )(cidx, f3, p3)
    return out.reshape(shape)


def _to_other_chips(arrays, blocked):
    def src(ins, outs, pos, a, mask):
        return ins[a].at[_chip(pos) ^ mask] if blocked[a] else ins[a]

    outs = [_sds((3,) + (a.shape[1:] if b else a.shape), a.dtype) for a, b in zip(arrays, blocked)]
    copies = []
    for mi, mask in enumerate(CHIP_MASKS):
        for a in range(len(arrays)):
            copies.append((functools.partial(src, a=a, mask=mask), lambda ins, outs, pos, a=a, mi=mi: outs[a].at[mi],
                           functools.partial(_other_chip, mask=mask)))
    return outs, copies


def _add_chips(name, own, got, jidx, blocked):
    rows, cols = got.shape[-2:]
    tr = rows // 2 if rows % 16 == 0 else rows
    nr = rows // tr
    o3 = own if blocked else own.reshape((1, rows, cols))

    def body(j_ref, o_ref, g_ref, out_ref):
        out_ref[...] = ((o_ref[0].astype(F32) + g_ref[0].astype(F32))
                        + (g_ref[1].astype(F32) + g_ref[2].astype(F32)))

    own_map = (lambda r, j_ref: (j_ref[0], r, 0)) if blocked else (lambda r, j_ref: (0, r, 0))
    return _call(
        body, name=name,
        grid_spec=pltpu.PrefetchScalarGridSpec(
            num_scalar_prefetch=1, grid=(nr,),
            in_specs=[pl.BlockSpec((1, tr, cols), own_map),
                      pl.BlockSpec((3, tr, cols), lambda r, j_ref: (0, r, 0))],
            out_specs=pl.BlockSpec((tr, cols), lambda r, j_ref: (r, 0))),
        out_shape=_sds((rows, cols)),
        compiler_params=_params("arbitrary"),
    )(jidx, o3, got)


def _to_sibling(name, arrays):
    phase = [(lambda ins, outs, pos, a=a: ins[a], lambda ins, outs, pos, a=a: outs[a], _sibling)
             for a in range(len(arrays))]
    return _exchange(name, arrays, [_sds(a.shape, a.dtype) for a in arrays], [phase])


def _local_step(x, target, w_pad, w_out, conv_w, norm_w, pool_w, pool_scale, a_log, dt_bias, dn_norm_w, final_norm_w):
    proj, n_t = _proj_fwd(x, norm_w, w_pad)
    y_pool = _pool_fwd(proj, pool_w, pool_scale)
    qn, kn, vs, beta, g = _conv_fwd(proj, conv_w, a_log, dt_bias)
    w, att, qd, kd, tm, cd, o, vn, st = _delta_fwd(qn, kn, vs, beta, g)
    w_out = w_out(o) if callable(w_out) else w_out
    g_wout, dh, dyp, do, ddz, loss, g_fnw, g_dnw = _out_fwd_bwd(x, y_pool, o, proj, target, w_out, dn_norm_w, final_norm_w)
    dpu, dpz, g_pw, g_ps = _pool_bwd(proj, dyp, pool_w, pool_scale)
    dqn, dkn, dvs, dbeta, dg = _delta_bwd(do, vn, qd, kd, w, att, cd, st, qn, kn, vs, beta, g, tm)
    dcq, dck, dcv, dba, g_cw, g_sm = _conv_bwd(proj, conv_w, a_log, dt_bias, dqn, dkn, dvs, dbeta, dg)
    pieces = [dpu, dpz, dcq, dck, dcv, ddz, dba]
    g_win = _grad_w_in(n_t, pieces)
    small = dict(norm_w=jnp.zeros_like(norm_w), pool_w=g_pw, pool_scale=g_ps, conv_w=g_cw[:CONV_K],
                 a_log=g_sm[0:1, 0:N_HEADS], dt_bias=g_sm[0:1, N_HEADS:2 * N_HEADS], dn_norm_w=g_dnw, final_norm_w=g_fnw)
    return loss[0, 0], g_win, g_wout, small, dh, pieces


SMALL_LAYOUT = (("pool_w", 512, HEAD, (1, N_HEADS, HEAD, HEAD)), ("final_norm_w", 8, HEAD, (D_MODEL,)),
                ("pool_scale", 4, HEAD, (1, D_HALF)), ("conv_w", 48, HEAD, (1, CONV_K, 3 * D_HALF)),
                ("dn_norm_w", 1, HEAD, (1, HEAD)), ("a_log", 1, N_HEADS, (1, N_HEADS)), ("dt_bias", 1, N_HEADS, (1, N_HEADS)),
                ("loss", 1, 1, ()))


def _small_offsets():
    offs, r = {}, 0
    for name, rows, _, _ in SMALL_LAYOUT:
        offs[name] = r
        r += -(-rows // 8) * 8
    assert r <= SMALL_ROWS
    return offs


def _pack_small(t):
    parts = []
    for name, rows, lanes, _ in SMALL_LAYOUT:
        a = t.get(name, jnp.zeros((1,), F32)).reshape(rows, lanes)
        parts.append(jnp.pad(a, ((0, -(-rows // 8) * 8 - rows), (0, HEAD - lanes))))
    buf = jnp.concatenate(parts, axis=0)
    return jnp.pad(buf, ((0, SMALL_ROWS - buf.shape[0]), (0, 0)))


def _adamw_small(w, g_own, g_got, cidx, m, v):
    offs = _small_offsets()
    names = [e[0] for e in SMALL_LAYOUT]
    n = len(names)

    def body(c_ref, w_ref, go_ref, gg_ref, m_ref, v_ref, *outs):
        own_low = c_ref[0] == 0
        gv = jnp.concatenate([jnp.where(own_low, go_ref[...], gg_ref[...]), jnp.where(own_low, gg_ref[...], go_ref[...])], axis=0)
        mn = ADAM_B1 * m_ref[...] + (1.0 - ADAM_B1) * gv
        vn = ADAM_B2 * v_ref[...] + (1.0 - ADAM_B2) * (gv * gv)
        m_hat = mn / (1.0 - ADAM_B1 ** ADAM_STEP)
        v_hat = vn / (1.0 - ADAM_B2 ** ADAM_STEP)
        dl = -ADAM_LR * (m_hat / (jnp.sqrt(v_hat) + ADAM_EPS) + ADAM_WD * w_ref[...])
        for kind, arr in enumerate((gv, dl, mn, vn)):
            for i, (name, rows, lanes, _) in enumerate(SMALL_LAYOUT):
                outs[kind * n + i][...] = arr[offs[name]:offs[name] + rows, :lanes]

    whole = lambda shape: pl.BlockSpec(shape, lambda i, c_ref: (0,) * len(shape))
    out_shapes = [_sds((rows, lanes)) for _, rows, lanes, _ in SMALL_LAYOUT] * 4
    res = _call(
        body, name="adamw_small",
        grid_spec=pltpu.PrefetchScalarGridSpec(
            num_scalar_prefetch=1, grid=(1,),
            in_specs=[whole(w.shape), whole(g_own.shape), whole(g_got.shape), whole(m.shape), whole(v.shape)],
            out_specs=[whole(o.shape) for o in out_shapes]),
        out_shape=out_shapes,
        compiler_params=_params("arbitrary"),
    )(cidx, w, g_own, g_got, m, v)
    return [{name: res[kind * n + i].reshape(shape) for i, (name, _, _, shape) in enumerate(SMALL_LAYOUT)}
            for kind in range(4)]


def kernel(x, norm_w, w_in, pool_w, pool_scale, conv_w, a_log, dt_bias, dn_norm_w, w_out, final_norm_w, loss_target, m_norm_w, m_w_in, m_pool_w, m_pool_scale, m_conv_w, m_a_log, m_dt_bias, m_dn_norm_w, m_w_out, m_final_norm_w, v_norm_w, v_w_in, v_pool_w, v_pool_scale, v_conv_w, v_a_log, v_dt_bias, v_dn_norm_w, v_w_out, v_final_norm_w):
    cidx = lax.axis_index("c").astype(I32).reshape(1)
    jidx = (2 * lax.axis_index("x") + lax.axis_index("y")).astype(I32)

    wb = jnp.pad(w_in[0].astype(BF16), ((0, 0), (0, BLK_IN_PAD - BLK_IN)))
    ob = w_out[0].astype(BF16)
    gw, gc = _gather_weights(wb, conv_w[0])
    mine = lambda j: jidx == j
    w_pad = jnp.concatenate([jnp.where(mine(j), wb[:, :BLK_IN], gw[j, :, :BLK_IN]) for j in range(4)]
                            + [jnp.zeros((D_MODEL, N_IN_PAD - N_IN), BF16)], axis=1)
    cw_full = jnp.concatenate([jnp.where(mine(j), conv_w[0], gc[j]) for j in range(4)], axis=1)

    lands_o, copies_o = _gather_blocks(ob)
    sems_o, ob_thru, zones_o, token_o = _exchange_start("gather_w_out_start", [ob], lands_o, copies_o)

    def w_out_full(after):
        (own,), (got,) = _exchange_wait("gather_w_out_wait", sems_o, ob_thru, zones_o, copies_o, after)
        return jnp.where((jnp.arange(4) == jidx)[:, None, None], own[None], got).reshape(D_MODEL, D_MODEL)

    loss, g_win, g_wout, small, dh, pieces = _local_step(
        x[0], loss_target[0], w_pad, w_out_full, cw_full, norm_w + token_o[0, 0], pool_w[0], pool_scale, a_log, dt_bias,
        dn_norm_w, final_norm_w.reshape(1, D_MODEL))
    small["loss"] = loss

    blocks_out = g_wout.reshape(4, BLK_OUT, D_MODEL)
    full = [g_win, blocks_out, _pack_small(small)]
    from_sib = _to_sibling_half("reduce_sibling", full)
    chip_sum = [_add_half("add_sibling_%d" % i, f, p, cidx) for i, (f, p) in enumerate(zip(full, from_sib))]
    blocked = [True, True, False]
    lands, copies = _to_other_chips(chip_sum, blocked)
    sems, chip_sum, zones, token = _exchange_start("reduce_chips_start", chip_sum, lands, copies)
    gx, g_nw = _in_bwd(x[0], dh, norm_w + token[0, 0], w_pad, pieces)
    g_nw = _allreduce_tile("reduce_norm_w", g_nw.reshape(8, HEAD)).reshape(1, D_MODEL)
    chip_sum, from_chips = _exchange_wait("reduce_chips_wait", sems, chip_sum, zones, copies, gx)
    halves = [_add_chips("add_chips_%d" % i, o, g, jidx.reshape(1), b)
              for i, (o, g, b) in enumerate(zip(chip_sum, from_chips, blocked))]
    other_halves = _to_sibling("swap_halves", halves)

    weights = dict(norm_w=norm_w, w_in=w_in, pool_w=pool_w, pool_scale=pool_scale, conv_w=conv_w, a_log=a_log,
                   dt_bias=dt_bias, dn_norm_w=dn_norm_w, w_out=w_out, final_norm_w=final_norm_w)
    ms = dict(norm_w=m_norm_w, w_in=m_w_in, pool_w=m_pool_w, pool_scale=m_pool_scale, conv_w=m_conv_w, a_log=m_a_log,
              dt_bias=m_dt_bias, dn_norm_w=m_dn_norm_w, w_out=m_w_out, final_norm_w=m_final_norm_w)
    vs = dict(norm_w=v_norm_w, w_in=v_w_in, pool_w=v_pool_w, pool_scale=v_pool_scale, conv_w=v_conv_w, a_log=v_a_log,
              dt_bias=v_dt_bias, dn_norm_w=v_dn_norm_w, w_out=v_w_out, final_norm_w=v_final_norm_w)
    names = ["norm_w", "w_in", "pool_w", "pool_scale", "conv_w", "a_log", "dt_bias", "dn_norm_w", "w_out", "final_norm_w"]
    small_names = [n for n in names if n not in ("w_in", "w_out")]

    def pack(t):
        conv = lax.dynamic_update_slice_in_dim(jnp.zeros((CONV_K, 3 * D_HALF), F32), t["conv_w"][0], jidx * BLK_CONV, axis=1)
        return _pack_small({**{n: t[n] for n in small_names if n != "conv_w"}, "conv_w": conv})

    results = [{}, {}, {}, {}]
    to_tiles = lambda a: jnp.transpose(a, (2, 0, 1)).reshape(BLK_IN, 8, HEAD)
    from_tiles = lambda a: jnp.transpose(a, (1, 2, 0)).reshape(1, D_MODEL, BLK_IN)
    lo = jnp.where(cidx[0] == 0, halves[0], other_halves[0])
    hi = jnp.where(cidx[0] == 0, other_halves[0], halves[0])
    g_tiles = jnp.concatenate([lo[:, :BLK_IN].T, hi[:, :BLK_IN].T], axis=1).reshape(BLK_IN, 8, HEAD)
    outs = _adamw_tiles("adamw_w_in", to_tiles(w_in), g_tiles, to_tiles(m_w_in), to_tiles(v_w_in))
    for res, o in zip(results, (g_tiles,) + tuple(outs)):
        res["w_in"] = from_tiles(o)
    outs = _adamw_shard("adamw_w_out", w_out, halves[1], other_halves[1], cidx, m_w_out, v_w_out)
    for res, o in zip(results, outs):
        res["w_out"] = o
    outs = _adamw_small(pack(weights), halves[2], other_halves[2], cidx, pack(ms), pack(vs))
    for res, got in zip(results, outs):
        got["conv_w"] = lax.dynamic_slice_in_dim(got["conv_w"], jidx * BLK_CONV, BLK_CONV, axis=2)
        res.update(got)
    one_tile = lambda a: a.reshape(1, 8, HEAD)
    outs = _adamw_tiles("adamw_norm_w", one_tile(norm_w), one_tile(g_nw), one_tile(m_norm_w), one_tile(v_norm_w))
    for res, o in zip(results, (g_nw,) + tuple(outs)):
        res["norm_w"] = o.reshape(1, D_MODEL)
    grads, delta, new_m, new_v = results

    return (grads["loss"], gx[None], *[grads[n] for n in names], *[delta[n] for n in names],
            *[new_m[n] for n in names], *[new_v[n] for n in names])
```

```python
import functools

import jax
import jax.numpy as jnp
import numpy as np
from jax import lax
from jax.experimental import pallas as pl
from jax.experimental.pallas import tpu as pltpu

F32 = jnp.float32
BF16 = jnp.bfloat16
I32 = jnp.int32

D_MODEL = 1024
D_HALF = 512
N_HEADS = 4
HEAD = 128
CHUNK = 64
PAIR = 2 * CHUNK
WINDOWS = (2, 4, 8, 16)
CONV_K = 4
EPS = 1e-6
N_IN = 3080
N_IN_PAD = 3200
BLK_IN = 770
BLK_IN_PAD = 896
BLK_OUT = 256
BLK_CONV = 384
COL_BA = 3072
QK_SCALE = HEAD ** -0.5
SMALL_ROWS = 608
VMEM_LIMIT = 56 * 1024 * 1024

ADAM_LR = 0.001
ADAM_B1 = 0.9
ADAM_B2 = 0.999
ADAM_EPS = 1e-08
ADAM_WD = 0.01
ADAM_STEP = 10

CHIP_MASKS = (2, 1, 3)
HEADS = range(N_HEADS)
HEAD_COLS = [slice(h * HEAD, (h + 1) * HEAD) for h in HEADS]


def _call(body, **kw):
    return pl.pallas_call(body, **kw)


def _params(*sem):
    return pltpu.CompilerParams(dimension_semantics=sem, vmem_limit_bytes=VMEM_LIMIT)


def _sds(shape, dtype=F32):
    return jax.ShapeDtypeStruct(shape, dtype)


def _bdot(a, b):
    return jnp.dot(a.astype(BF16), b.astype(BF16), preferred_element_type=F32)


def _bdot_nt(a, b):
    return lax.dot_general(a.astype(BF16), b.astype(BF16), (((1,), (1,)), ((), ())), preferred_element_type=F32)


def _bdot_tn(a, b):
    return lax.dot_general(a.astype(BF16), b.astype(BF16), (((0,), (0,)), ((), ())), preferred_element_type=F32)


def _split(a):
    hi = a.astype(BF16)
    lo = (a - hi.astype(F32)).astype(BF16)
    return hi, lo


def _mask_dot(m, b, dims=(((1,), (0,)), ((), ()))):
    bh, bl = _split(b)
    dg = functools.partial(lax.dot_general, dimension_numbers=dims, preferred_element_type=F32)
    return dg(m, bh) + dg(m, bl)


def _sigmoid(x):
    return 0.5 * jnp.tanh(0.5 * x) + 0.5


def _softplus(x):
    return jnp.maximum(x, 0.0) + jnp.log(1.0 + jnp.exp(-jnp.abs(x)))


def _rowsum(x):
    return jnp.sum(x, axis=-1, keepdims=True)


def _colsum(x):
    return jnp.sum(x, axis=0, keepdims=True)


def _shift_down(xv, prev8, k):
    r = pltpu.roll(xv, k, 0)
    q = pltpu.roll(prev8, k, 0)
    row = lax.broadcasted_iota(I32, prev8.shape, 0)
    top = jnp.where(row < k, q, r[0:8])
    return jnp.concatenate([top, r[8:]], axis=0)


def _shift_up(xv, next8, k):
    t = xv.shape[0]
    r = pltpu.roll(xv, t - k, 0)
    q = pltpu.roll(next8, 8 - k, 0)
    row = lax.broadcasted_iota(I32, next8.shape, 0)
    bot = jnp.where(row >= 8 - k, q, r[t - 8:])
    return jnp.concatenate([r[:t - 8], bot], axis=0)


def _band(rows, cols, off, w, anti=False):
    r = lax.broadcasted_iota(I32, (rows, cols), 0)
    c = lax.broadcasted_iota(I32, (rows, cols), 1)
    d = (c - r + off) if anti else (r - c + off)
    return ((d >= 0) & (d < w)).astype(BF16)


def _head(ref_or_val, h):
    return ref_or_val[:, h * HEAD:(h + 1) * HEAD]


INTRA_PAIRS = 2
UNITS = [(pp, h) for pp in range(INTRA_PAIRS) for h in HEADS]


def _heads(ref, rows=PAIR):
    return [ref[pp * rows:(pp + 1) * rows, HEAD_COLS[h]] for pp, h in UNITS]


def _put_heads(ref, vals, rows=PAIR):
    for (pp, h), v in zip(UNITS, vals):
        ref[pp * rows:(pp + 1) * rows, HEAD_COLS[h]] = v.astype(ref.dtype)


def _each(fn, *lists):
    return [fn(*args) for args in zip(*lists)]


def _proj_fwd(x, norm_w, w_pad):
    s = x.shape[0]
    tm = 512

    def body(x_ref, nw_ref, w_ref, proj_ref, nt_ref):
        xv = x_ref[...]
        r = lax.rsqrt(jnp.mean(xv * xv, axis=-1, keepdims=True) + EPS)
        nv = xv * r * nw_ref[...]
        nt_ref[...] = nv.T.astype(BF16)
        proj_ref[...] = jnp.dot(nv.astype(BF16), w_ref[...], preferred_element_type=F32)

    return _call(
        body, name="proj_fwd", grid=(s // tm,),
        in_specs=[pl.BlockSpec((tm, D_MODEL), lambda i: (i, 0)),
                  pl.BlockSpec((1, D_MODEL), lambda i: (0, 0)),
                  pl.BlockSpec((D_MODEL, N_IN_PAD), lambda i: (0, 0))],
        out_specs=[pl.BlockSpec((tm, N_IN_PAD), lambda i: (i, 0)),
                   pl.BlockSpec((D_MODEL, tm), lambda i: (0, i))],
        out_shape=[_sds((s, N_IN_PAD)), _sds((D_MODEL, s), BF16)],
        compiler_params=_params("arbitrary"),
    )(x, norm_w, w_pad)


def _pool_bands(t, anti=False):
    r = np.arange(t)[:, None]
    c = np.arange(t + HEAD)[None, :]
    d = (c - r) if anti else (r - c + HEAD)
    return jnp.asarray(np.stack([(d >= 0) & (d < w) for w in WINDOWS]), BF16)


def _pool_mix(u, halo, z, pw, bands, row0):
    t = u[0].shape[0]
    rows = row0 + lax.broadcasted_iota(I32, (t, 1), 0) + 1
    cnt = [jnp.minimum(rows, w).astype(F32) for w in WINDOWS]
    win = _each(lambda b, h, v: _mask_dot(b, jnp.concatenate([h, v], axis=0)), bands, halo, u)
    mix = _each(lambda a, c, v: a / c - v, win, cnt, u)
    mixed = _each(_bdot, mix, pw)
    return mix, mixed, _each(_sigmoid, z), cnt


POOL_T = 256


def _pool_fwd(proj, pool_w, pool_scale):
    s = proj.shape[0]
    t = POOL_T
    hb = t // HEAD

    def body(u_ref, z_ref, halo_ref, pw_ref, ps_ref, band_ref, y_ref):
        i = pl.program_id(0)
        live = (i > 0).astype(F32)
        groups = lambda ref: [ref[:, sl] for sl in HEAD_COLS]
        z = groups(z_ref)
        _, mixed, sg, _ = _pool_mix(groups(u_ref), [h * live for h in groups(halo_ref)], z,
                                    [pw_ref[g] for g in HEADS], [band_ref[g] for g in HEADS], i * t)
        for sl, m, zg, s_ in zip(HEAD_COLS, mixed, z, sg):
            y_ref[:, sl] = m * ps_ref[:, sl] * (zg * s_)

    return _call(
        body, name="pool_fwd", grid=(s // t,),
        in_specs=[pl.BlockSpec((t, D_HALF), lambda i: (i, 0)),
                  pl.BlockSpec((t, D_HALF), lambda i: (i, 1)),
                  pl.BlockSpec((HEAD, D_HALF), lambda i: (jnp.maximum(i * hb - 1, 0), 0)),
                  pl.BlockSpec((N_HEADS, HEAD, HEAD), lambda i: (0, 0, 0)),
                  pl.BlockSpec((1, D_HALF), lambda i: (0, 0)),
                  pl.BlockSpec((N_HEADS, t, HEAD + t), lambda i: (0, 0, 0))],
        out_specs=pl.BlockSpec((t, D_HALF), lambda i: (i, 0)),
        out_shape=_sds((s, D_HALF)),
        compiler_params=_params("arbitrary"),
    )(proj, proj, proj, pool_w, pool_scale, _pool_bands(t))


def _conv_taps(xv, prev8):
    return [_shift_down(xv, prev8, CONV_K - 1 - j) for j in range(CONV_K - 1)] + [xv]


def _conv_pre(taps, cw):
    y = taps[CONV_K - 1] * cw[CONV_K - 1:CONV_K]
    for j in range(CONV_K - 2, -1, -1):
        y = y + taps[j] * cw[j:j + 1]
    return y


CONV_T = 256
CONV_SUB = 256


def _conv_specs(t, tile_of=lambda i: i):
    tiles = [pl.BlockSpec((t, D_HALF), functools.partial(lambda i, p: (tile_of(i), 2 + p), p=p)) for p in range(3)]
    halos = [pl.BlockSpec((8, D_HALF),
                          functools.partial(lambda i, p: (jnp.maximum(tile_of(i) * (t // 8) - 1, 0), 2 + p), p=p))
             for p in range(3)]
    return tiles + halos


def _conv_fwd(proj, conv_w, a_log, dt_bias):
    s = proj.shape[0]
    t = CONV_T

    def body(q_ref, k_ref, v_ref, hq_ref, hk_ref, hv_ref, ba_ref, cw_ref, al_ref, dtb_ref,
             qn_ref, kn_ref, vs_ref, beta_ref, g_ref):
        live = (pl.program_id(0) > 0).astype(F32)
        parts = ((q_ref, hq_ref, qn_ref), (k_ref, hk_ref, kn_ref), (v_ref, hv_ref, vs_ref))

        def sub_tile(r0, first):
            rows = pl.ds(r0, CONV_SUB)
            for p, (x_ref, h_ref, o_ref) in enumerate(parts):
                for h in HEADS:
                    cs = HEAD_COLS[h]
                    prev8 = h_ref[:, cs] * live if first else x_ref[pl.ds(r0 - 8, 8), cs]
                    y = _conv_pre(_conv_taps(x_ref[rows, cs], prev8), cw_ref[:, p * D_HALF + h * HEAD:p * D_HALF + (h + 1) * HEAD])
                    sv = y * _sigmoid(y)
                    o_ref[rows, cs] = sv if p == 2 else sv * lax.rsqrt(_rowsum(sv * sv) + EPS)
            ba = ba_ref[rows, :]
            for h in HEADS:
                beta = _sigmoid(ba[:, h:h + 1])
                gl = -jnp.exp(al_ref[0:1, h:h + 1]) * _softplus(ba[:, N_HEADS + h:N_HEADS + h + 1] + dtb_ref[0:1, h:h + 1])
                beta_ref[rows, HEAD_COLS[h]] = jnp.broadcast_to(beta, (CONV_SUB, HEAD))
                g_ref[rows, HEAD_COLS[h]] = jnp.broadcast_to(gl, (CONV_SUB, HEAD))

        sub_tile(0, True)

        def step(k, carry):
            sub_tile(pl.multiple_of(k * CONV_SUB, CONV_SUB), False)
            return carry

        lax.fori_loop(1, t // CONV_SUB, step, 0)

    row = pl.BlockSpec((t, D_HALF), lambda i: (i, 0))
    return _call(
        body, name="conv_fwd", grid=(s // t,),
        in_specs=_conv_specs(t) + [pl.BlockSpec((t, HEAD), lambda i: (i, COL_BA // HEAD)),
                                   pl.BlockSpec((CONV_K, 3 * D_HALF), lambda i: (0, 0)),
                                   pl.BlockSpec((1, N_HEADS), lambda i: (0, 0)),
                                   pl.BlockSpec((1, N_HEADS), lambda i: (0, 0))],
        out_specs=[row] * 5,
        out_shape=[_sds((s, D_HALF))] * 5,
        compiler_params=_params("arbitrary"),
    )(proj, proj, proj, proj, proj, proj, proj, conv_w, a_log, dt_bias)


def _pair_masks():
    r = lax.broadcasted_iota(I32, (PAIR, PAIR), 0)
    c = lax.broadcasted_iota(I32, (PAIR, PAIR), 1)
    same = jnp.right_shift(r, 6) == jnp.right_shift(c, 6)
    return same, same & (r >= c), same & (r > c), r == c


def _run(stages):
    for _ in stages:
        pass


def _interleave(*stage_lists):
    live = list(stage_lists)
    while live:
        for gen in list(live):
            try:
                next(gen)
            except StopIteration:
                live.remove(gen)


def _pair_common_stages(cm, qn, kn, vs, beta, g):
    same, incl, strict, eye = _pair_masks()
    incl_b = incl.astype(BF16)
    first = lax.broadcasted_iota(I32, (PAIR, HEAD), 0) < CHUNK
    cm.update(same=same, incl=incl, strict=strict, eye=eye)
    gc = _each(lambda gv: _mask_dot(incl_b, gv), g)
    q = _each(lambda v: v * QK_SCALE, qn)
    kb = _each(lambda k, b: k * b, kn, beta)
    cm.update(gc=gc, q=q, kb=kb, vb=_each(lambda v, b: v * b, vs, beta))
    yield
    cm.update(kk=_each(_bdot_nt, kb, kn), qk=_each(_bdot_nt, q, kn))
    gc_row = _each(lambda v: _colsum(jnp.where(eye, v, 0.0)), gc)
    gl = _each(lambda v: jnp.where(first, v[CHUNK - 1:CHUNK], v[PAIR - 1:PAIR]), gc)
    egc = _each(jnp.exp, gc)
    cm.update(gl=gl, egc=egc,
              decay=_each(lambda v, r: jnp.where(incl, jnp.exp(jnp.where(incl, v - r, 0.0)), 0.0), gc, gc_row))
    yield
    cm.update(ekd=_each(lambda a, b: jnp.exp(a - b), gl, gc), cd=_each(jnp.exp, gl),
              kbg=_each(lambda k, e: k * e, kb, egc))
    yield


def _pair_common(qn, kn, vs, beta, g):
    cm = {}
    _run(_pair_common_stages(cm, qn, kn, vs, beta, g))
    return cm


def _tri_inv_stages(out, a, eye_f):
    p = _each(lambda v: eye_f - v, a)
    x = _each(_bdot, a, a)
    yield
    for it in range(5):
        p = _each(lambda pv, xv: pv + _bdot(pv, xv), p, x)
        if it < 4:
            x = _each(_bdot, x, x)
        yield
    out["t"] = p


def _tri_inv(a, eye_f):
    out = {}
    _run(_tri_inv_stages(out, a, eye_f))
    return out["t"]


def _pair_spec():
    return pl.BlockSpec((INTRA_PAIRS * PAIR, D_HALF), lambda i: (i, 0))


def _chunk_scalar_spec(pairs=1, index=lambda i: (i, 0)):
    return pl.BlockSpec((16 * pairs, D_HALF), index)


SCAN_PAIRS = 2
SCAN_ROWS = SCAN_PAIRS * PAIR


def _intra_fwd(qn, kn, vs, beta, g):
    s = qn.shape[0]

    def body(qn_ref, kn_ref, vs_ref, beta_ref, g_ref, u_ref, w_ref, att_ref, qd_ref, kd_ref, t_ref, cd_ref):
        kn = _heads(kn_ref)
        cm = _pair_common(_heads(qn_ref), kn, _heads(vs_ref), _heads(beta_ref), _heads(g_ref))
        a = _each(lambda kk, d: jnp.where(cm["strict"], kk * d, 0.0), cm["kk"], cm["decay"])
        tm = _tri_inv(a, cm["eye"].astype(F32))
        _put_heads(t_ref, tm)
        _put_heads(u_ref, _each(_bdot, tm, cm["vb"]))
        _put_heads(w_ref, _each(_bdot, tm, cm["kbg"]))
        _put_heads(att_ref, _each(lambda a, b: a * b, cm["qk"], cm["decay"]))
        _put_heads(qd_ref, _each(lambda a, b: a * b, cm["q"], cm["egc"]))
        _put_heads(kd_ref, _each(lambda a, b: a * b, kn, cm["ekd"]))
        for ci in range(2):
            for (pp, h), v in zip(UNITS, cm["cd"]):
                cd_ref[pp * 16 + ci * 8:pp * 16 + (ci + 1) * 8, HEAD_COLS[h]] = v[ci * CHUNK:ci * CHUNK + 8]

    return _call(
        body, name="intra_fwd", grid=(s // (INTRA_PAIRS * PAIR),),
        in_specs=[_pair_spec()] * 5, out_specs=[_pair_spec()] * 6 + [_chunk_scalar_spec(INTRA_PAIRS)],
        out_shape=[_sds((s, D_HALF))] + [_sds((s, D_HALF), BF16)] * 5 + [_sds((s // 8, D_HALF))],
        compiler_params=_params("arbitrary"),
    )(qn, kn, vs, beta, g)


def _scan_fwd(u, w, att, qd, kd, cd):
    s = u.shape[0]
    n_chunks = s // CHUNK

    def body(u_ref, w_ref, att_ref, qd_ref, kd_ref, cd_ref, o_ref, vn_ref, st_ref, state):
        @pl.when(pl.program_id(0) == 0)
        def _():
            state[...] = jnp.zeros_like(state)
        cols = list(enumerate(HEAD_COLS))
        sm = [state[h] for h in HEADS]
        for ci in range(2 * SCAN_PAIRS):
            rs = slice(ci * CHUNK, (ci + 1) * CHUNK)
            for h in HEADS:
                st_ref[ci, h] = sm[h]
            both = [_bdot(jnp.concatenate([w_ref[rs, sl], qd_ref[rs, sl]], axis=0), sm[h]) for h, sl in cols]
            vn = [u_ref[rs, sl] - both[h][:CHUNK] for h, sl in cols]
            for h, sl in cols:
                vn_ref[rs, sl] = vn[h].astype(BF16)
                o_ref[rs, sl] = both[h][CHUNK:]
            sm = [sm[h] * cd_ref[ci * 8:ci * 8 + 1, sl] + _bdot_tn(kd_ref[rs, sl], vn[h]) for h, sl in cols]
        for h in HEADS:
            state[h] = sm[h]
        for pp in range(SCAN_PAIRS):
            rp = slice(pp * PAIR, (pp + 1) * PAIR)
            intra = [_bdot(att_ref[rp, sl], vn_ref[rp, sl]) for sl in HEAD_COLS]
            for h, sl in cols:
                o_ref[rp, sl] += intra[h]

    rows = pl.BlockSpec((SCAN_ROWS, D_HALF), lambda i: (i, 0))
    return _call(
        body, name="scan_fwd", grid=(s // SCAN_ROWS,),
        in_specs=[rows] * 5 + [_chunk_scalar_spec(SCAN_PAIRS)],
        out_specs=[rows, rows, pl.BlockSpec((2 * SCAN_PAIRS, N_HEADS, HEAD, HEAD), lambda i: (i, 0, 0, 0))],
        out_shape=[_sds((s, D_HALF)), _sds((s, D_HALF), BF16), _sds((n_chunks, N_HEADS, HEAD, HEAD))],
        scratch_shapes=[pltpu.VMEM((N_HEADS, HEAD, HEAD), F32)],
        compiler_params=_params("arbitrary"),
    )(u, w, att, qd, kd, cd)


def _delta_fwd(qn, kn, vs, beta, g):
    s = qn.shape[0]
    n_steps = s // SCAN_ROWS
    n_chunks = s // CHUNK
    assert INTRA_PAIRS == SCAN_PAIRS

    def body(qn_ref, kn_ref, vs_ref, beta_ref, g_ref, w_ref, att_ref, qd_ref, kd_ref, t_ref, cd_ref, o_ref, vn_ref, st_ref,
             state, u_s, w_s, att_s, qd_s, kd_s, cd_s):
        t = pl.program_id(0)

        @pl.when(t <= 1)
        def _():
            state[...] = jnp.zeros_like(state)

        @pl.when(t == 0)
        def _():
            for ref in (u_s, w_s, att_s, qd_s, kd_s, cd_s):
                ref[1] = jnp.zeros(ref.shape[1:], ref.dtype)

        cur = lax.rem(t, 2)
        prev = 1 - cur
        cols = list(enumerate(HEAD_COLS))

        def recurrence():
            sm = [state[h] for h in HEADS]
            for ci in range(2 * SCAN_PAIRS):
                rs = slice(ci * CHUNK, (ci + 1) * CHUNK)
                for h in HEADS:
                    st_ref[ci, h] = sm[h]
                both = [_bdot(jnp.concatenate([w_s[prev, rs, sl], qd_s[prev, rs, sl]], axis=0), sm[h]) for h, sl in cols]
                vn = [u_s[prev, rs, sl] - both[h][:CHUNK] for h, sl in cols]
                for h, sl in cols:
                    vn_ref[rs, sl] = vn[h].astype(BF16)
                    o_ref[rs, sl] = both[h][CHUNK:]
                yield
                sm = [sm[h] * cd_s[prev, ci * 8:ci * 8 + 1, sl] + _bdot_tn(kd_s[prev, rs, sl], vn[h]) for h, sl in cols]
                yield
            for h in HEADS:
                state[h] = sm[h]
            for pp in range(SCAN_PAIRS):
                rp = slice(pp * PAIR, (pp + 1) * PAIR)
                intra = [_bdot(att_s[prev, rp, sl], vn_ref[rp, sl]) for sl in HEAD_COLS]
                for h, sl in cols:
                    o_ref[rp, sl] += intra[h]
                yield

        def factors():
            kn = _heads(kn_ref)
            cm = {}
            yield from _pair_common_stages(cm, _heads(qn_ref), kn, _heads(vs_ref), _heads(beta_ref), _heads(g_ref))
            a = _each(lambda kk, d: jnp.where(cm["strict"], kk * d, 0.0), cm["kk"], cm["decay"])
            inv = {}
            yield from _tri_inv_stages(inv, a, cm["eye"].astype(F32))
            tm = inv["t"]
            res = dict(u=_each(_bdot, tm, cm["vb"]), w=_each(_bdot, tm, cm["kbg"]),
                       att=_each(lambda a, b: a * b, cm["qk"], cm["decay"]),
                       qd=_each(lambda a, b: a * b, cm["q"], cm["egc"]), kd=_each(lambda a, b: a * b, kn, cm["ekd"]))
            yield
            _put_heads(t_ref, tm)
            for key, out, keep in (("w", w_ref, w_s), ("att", att_ref, att_s), ("qd", qd_ref, qd_s), ("kd", kd_ref, kd_s)):
                _put_heads(out, res[key])
                for (pp, h), v in zip(UNITS, res[key]):
                    keep[cur, pp * PAIR:(pp + 1) * PAIR, HEAD_COLS[h]] = v.astype(BF16)
            for (pp, h), v in zip(UNITS, res["u"]):
                u_s[cur, pp * PAIR:(pp + 1) * PAIR, HEAD_COLS[h]] = v
            for ci in range(2):
                for (pp, h), v in zip(UNITS, cm["cd"]):
                    rows8 = slice(pp * 16 + ci * 8, pp * 16 + (ci + 1) * 8)
                    cd_ref[rows8, HEAD_COLS[h]] = v[ci * CHUNK:ci * CHUNK + 8]
                    cd_s[cur, rows8, HEAD_COLS[h]] = v[ci * CHUNK:ci * CHUNK + 8]
            yield

        _interleave(recurrence(), factors())

    last = n_steps - 1
    now = lambda i: (jnp.minimum(i, last), 0)
    before = lambda i: (jnp.maximum(i - 1, 0), 0)
    rows = lambda index: pl.BlockSpec((SCAN_ROWS, D_HALF), index)
    slot = lambda r, dtype: pltpu.VMEM((2, r, D_HALF), dtype)
    return _call(
        body, name="delta_fwd", grid=(n_steps + 1,),
        in_specs=[rows(now)] * 5,
        out_specs=[rows(now)] * 5 + [_chunk_scalar_spec(SCAN_PAIRS, now), rows(before), rows(before),
                                     pl.BlockSpec((2 * SCAN_PAIRS, N_HEADS, HEAD, HEAD), lambda i: (jnp.maximum(i - 1, 0), 0, 0, 0))],
        out_shape=[_sds((s, D_HALF), BF16)] * 5 + [_sds((s // 8, D_HALF)), _sds((s, D_HALF)), _sds((s, D_HALF), BF16),
                                                  _sds((n_chunks, N_HEADS, HEAD, HEAD))],
        scratch_shapes=[pltpu.VMEM((N_HEADS, HEAD, HEAD), F32), slot(SCAN_ROWS, F32), slot(SCAN_ROWS, BF16),
                        slot(SCAN_ROWS, BF16), slot(SCAN_ROWS, BF16), slot(SCAN_ROWS, BF16), slot(16 * SCAN_PAIRS, F32)],
        compiler_params=_params("arbitrary"),
    )(qn, kn, vs, beta, g)


OUT_T = 512


def _out_fwd_bwd(x, y_pool, o, proj, target, w_out, dn_norm_w, final_norm_w):
    s = x.shape[0]
    t = OUT_T

    def body(x_ref, yp_ref, o_ref, z_ref, tg_ref, wo_ref, dnw_ref, fnw_ref,
             gwo_ref, dh_ref, dyp_ref, do_ref, dz_ref, loss_ref, gfn_ref, gdn_ref, y_ref, yt_ref, gwo_acc):
        @pl.when(pl.program_id(0) == 0)
        def _():
            loss_ref[...] = jnp.zeros_like(loss_ref)
            gfn_ref[...] = jnp.zeros_like(gfn_ref)
            gdn_ref[...] = jnp.zeros_like(gdn_ref)
            gwo_acc[...] = jnp.zeros_like(gwo_acc)

        ypv = yp_ref[...]
        y_ref[:, :D_HALF] = ypv.astype(BF16)
        yt_ref[:D_HALF, :] = ypv.T.astype(BF16)
        dnw = dnw_ref[...]
        keep = []
        for h in HEADS:
            ov = o_ref[:, HEAD_COLS[h]]
            zv = z_ref[:, HEAD_COLS[h]]
            ro = lax.rsqrt(jnp.mean(ov * ov, axis=-1, keepdims=True) + EPS)
            ohat = ov * ro
            sg = _sigmoid(zv)
            keep.append((ro, ohat, zv, sg))
            ydn = ohat * dnw * (zv * sg)
            y_ref[:, D_HALF + h * HEAD:D_HALF + (h + 1) * HEAD] = ydn.astype(BF16)
            yt_ref[D_HALF + h * HEAD:D_HALF + (h + 1) * HEAD, :] = ydn.T.astype(BF16)

        hv = x_ref[...] + jnp.dot(y_ref[...], wo_ref[...], preferred_element_type=F32)
        r2 = lax.rsqrt(jnp.mean(hv * hv, axis=-1, keepdims=True) + EPS)
        hhat = hv * r2
        fnw = fnw_ref[...]
        err = hhat * fnw - tg_ref[...]
        loss_ref[...] += 0.5 * jnp.sum(_rowsum(err * err) * (1.0 / D_MODEL), axis=0, keepdims=True)
        dout = err * (1.0 / D_MODEL)
        gfn_ref[...] += _colsum(dout * hhat)
        dhh = dout * fnw
        dh = r2 * (dhh - hhat * jnp.mean(dhh * hhat, axis=-1, keepdims=True))
        dh_ref[...] = dh
        gwo_acc[...] += _bdot(yt_ref[...], dh)

        @pl.when(pl.program_id(0) == pl.num_programs(0) - 1)
        def _():
            gwo_ref[...] = gwo_acc[...].astype(BF16)

        dy = _bdot_nt(dh, wo_ref[...])
        dyp_ref[...] = dy[:, :D_HALF]
        gdn = jnp.zeros((1, HEAD), F32)
        for h in HEADS:
            ro, ohat, zv, sg = keep[h]
            dyd = dy[:, D_HALF + h * HEAD:D_HALF + (h + 1) * HEAD]
            sz = zv * sg
            dz_ref[:, HEAD_COLS[h]] = (dyd * ohat * dnw * (sg * (1.0 + zv * (1.0 - sg)))).astype(BF16)
            gdn = gdn + _colsum(dyd * ohat * sz)
            doh = dyd * dnw * sz
            do_ref[:, HEAD_COLS[h]] = ro * (doh - ohat * jnp.mean(doh * ohat, axis=-1, keepdims=True))
        gdn_ref[...] += gdn

    wide = pl.BlockSpec((t, D_MODEL), lambda i: (i, 0))
    half = pl.BlockSpec((t, D_HALF), lambda i: (i, 0))
    const = lambda shape: pl.BlockSpec(shape, lambda i: (0,) * len(shape))
    return _call(
        body, name="out_fwd_bwd", grid=(s // t,),
        in_specs=[wide, half, half, pl.BlockSpec((t, D_HALF), lambda i: (i, 5)), wide,
                  const((D_MODEL, D_MODEL)), const((1, HEAD)), const((1, D_MODEL))],
        out_specs=[const((D_MODEL, D_MODEL)), wide, half, half, half,
                   const((1, HEAD)), const((1, D_MODEL)), const((1, HEAD))],
        out_shape=[_sds((D_MODEL, D_MODEL), BF16), _sds((s, D_MODEL)), _sds((s, D_HALF)), _sds((s, D_HALF)),
                   _sds((s, D_HALF), BF16), _sds((1, HEAD)), _sds((1, D_MODEL)), _sds((1, HEAD))],
        scratch_shapes=[pltpu.VMEM((t, D_MODEL), BF16), pltpu.VMEM((D_MODEL, t), BF16), pltpu.VMEM((D_MODEL, D_MODEL), F32)],
        compiler_params=_params("arbitrary"),
    )(x, y_pool, o, proj, target, w_out, dn_norm_w, final_norm_w)


def _token_matmul(name, at, pieces):
    m, s = at.shape
    n = len(pieces)
    tn, tk = D_HALF, 512

    def body(a_ref, *refs):
        p_refs, o_ref, acc = refs[:n], refs[n], refs[n + 1]

        @pl.when(pl.program_id(0) == 0)
        def _():
            acc[...] = jnp.zeros_like(acc)

        av = a_ref[...]
        for p in range(n):
            acc[:, p * tn:(p + 1) * tn] += _bdot(av, p_refs[p][...])

        @pl.when(pl.program_id(0) == pl.num_programs(0) - 1)
        def _():
            o_ref[...] = acc[...].astype(BF16)

    return _call(
        body, name=name, grid=(s // tk,),
        in_specs=[pl.BlockSpec((m, tk), lambda k: (0, k))]
                 + [pl.BlockSpec((tk, tn), functools.partial(lambda k, cb: (k, cb), cb=cb)) for _, cb in pieces],
        out_specs=pl.BlockSpec((m, n * tn), lambda k: (0, 0)),
        out_shape=_sds((m, n * tn), BF16),
        scratch_shapes=[pltpu.VMEM((m, n * tn), F32)],
        compiler_params=_params("arbitrary"),
    )(at, *[p[0] for p in pieces])


def _grad_w_in(at, pieces):
    m, s = at.shape
    n = len(pieces)
    tn, tk = D_HALF, 512

    def body(a_ref, *refs):
        p_refs, o_ref, acc = refs[:n], refs[n], refs[n + 1]

        @pl.when(pl.program_id(0) == 0)
        def _():
            acc[...] = jnp.zeros_like(acc)

        av = a_ref[...]
        for p in range(n):
            acc[:, p * tn:(p + 1) * tn] += _bdot(av, p_refs[p][...])

        @pl.when(pl.program_id(0) == pl.num_programs(0) - 1)
        def _():
            for j in range(4):
                base = j * BLK_IN // HEAD * HEAD
                win = acc[:, base:base + BLK_IN_PAD]
                if j * BLK_IN > base:
                    win = pltpu.roll(win, BLK_IN_PAD - (j * BLK_IN - base), 1)
                o_ref[j] = win.astype(BF16)

    return _call(
        body, name="grad_w_in", grid=(s // tk,),
        in_specs=[pl.BlockSpec((m, tk), lambda k: (0, k))] + [pl.BlockSpec((tk, tn), lambda k: (k, 0))] * n,
        out_specs=pl.BlockSpec((4, m, BLK_IN_PAD), lambda k: (0, 0, 0)),
        out_shape=_sds((4, m, BLK_IN_PAD), BF16),
        scratch_shapes=[pltpu.VMEM((m, n * tn), F32)],
        compiler_params=_params("arbitrary"),
    )(at, *pieces)


def _pool_bwd(proj, dyp, pool_w, pool_scale):
    s = proj.shape[0]
    t = POOL_T
    hb = t // HEAD
    last = s // HEAD - 1

    def body(u_ref, z_ref, halo_ref, dy_ref, zn_ref, dyn_ref, pw_ref, ps_ref, band_ref, aband_ref,
             du_ref, dz_ref, gpw_ref, gps_ref):
        i = pl.program_id(0)

        @pl.when(i == 0)
        def _():
            gpw_ref[...] = jnp.zeros_like(gpw_ref)
            gps_ref[...] = jnp.zeros_like(gps_ref)

        live = (i > 0).astype(F32)
        more = (i < pl.num_programs(0) - 1).astype(F32)
        groups = lambda ref: [ref[:, sl] for sl in HEAD_COLS]
        z, ps, dy = groups(z_ref), groups(ps_ref), groups(dy_ref)
        pw = [pw_ref[g] for g in HEADS]
        mix, mixed, sg, cnt = _pool_mix(groups(u_ref), [h * live for h in groups(halo_ref)], z, pw,
                                        [band_ref[g] for g in HEADS], i * t)
        sz = _each(lambda a, b: a * b, z, sg)
        for sl, d, m, p, s_, zg in zip(HEAD_COLS, dy, mixed, ps, sg, z):
            dz_ref[:, sl] = (d * m * p * (s_ * (1.0 + zg * (1.0 - s_)))).astype(BF16)
        for sl, d, m, a in zip(HEAD_COLS, dy, mixed, sz):
            gps_ref[:, sl] += _colsum(d * m * a)
        dmixed = _each(lambda d, p, a: d * p * a, dy, ps, sz)
        for g, gp in enumerate(_each(_bdot_tn, mix, dmixed)):
            gpw_ref[g] += gp
        dmix = _each(_bdot_nt, dmixed, pw)
        dmix_n = _each(lambda d, p, zn, w_: _bdot_nt(d * more * p * (zn * _sigmoid(zn)), w_),
                       groups(dyn_ref), ps, groups(zn_ref), pw)
        scaled = [jnp.concatenate([a / c, b * (1.0 / w)], axis=0) for a, c, b, w in zip(dmix, cnt, dmix_n, WINDOWS)]
        du = _each(lambda b, s_, d: _mask_dot(b, s_) - d, [aband_ref[g] for g in HEADS], scaled, dmix)
        for sl, v in zip(HEAD_COLS, du):
            du_ref[:, sl] = v.astype(BF16)

    tile = lambda col: pl.BlockSpec((t, D_HALF), lambda i: (i, col))
    below = lambda col: pl.BlockSpec((HEAD, D_HALF), lambda i: (jnp.minimum((i + 1) * hb, last), col))
    return _call(
        body, name="pool_bwd", grid=(s // t,),
        in_specs=[tile(0), tile(1), pl.BlockSpec((HEAD, D_HALF), lambda i: (jnp.maximum(i * hb - 1, 0), 0)),
                  tile(0), below(1), below(0),
                  pl.BlockSpec((N_HEADS, HEAD, HEAD), lambda i: (0, 0, 0)), pl.BlockSpec((1, D_HALF), lambda i: (0, 0)),
                  pl.BlockSpec((N_HEADS, t, HEAD + t), lambda i: (0, 0, 0)),
                  pl.BlockSpec((N_HEADS, t, HEAD + t), lambda i: (0, 0, 0))],
        out_specs=[tile(0), tile(0), pl.BlockSpec((N_HEADS, HEAD, HEAD), lambda i: (0, 0, 0)),
                   pl.BlockSpec((1, D_HALF), lambda i: (0, 0))],
        out_shape=[_sds((s, D_HALF), BF16), _sds((s, D_HALF), BF16), _sds((N_HEADS, HEAD, HEAD)), _sds((1, D_HALF))],
        compiler_params=_params("arbitrary"),
    )(proj, proj, proj, dyp, proj, dyp, pool_w, pool_scale, _pool_bands(t), _pool_bands(t, anti=True))


def _scan_bwd(do, vn, qd, kd, w, att, cd, st):
    s = do.shape[0]
    n_steps = s // SCAN_ROWS

    def body(do_ref, vn_ref, qd_ref, kd_ref, w_ref, att_ref, cd_ref, st_ref,
             du_ref, dw_ref, datt_ref, dqd_ref, dkd_ref, dcd_ref, dstate):
        @pl.when(pl.program_id(0) == 0)
        def _():
            dstate[...] = jnp.zeros_like(dstate)
        _, incl, _, _ = _pair_masks()
        cols = list(enumerate(HEAD_COLS))
        dv_intra = []
        for pp in range(SCAN_PAIRS):
            rp = slice(pp * PAIR, (pp + 1) * PAIR)
            dv_intra.append([_bdot_tn(att_ref[rp, sl], do_ref[rp, sl]) for _, sl in cols])
            for _, sl in cols:
                datt_ref[rp, sl] = jnp.where(incl, _bdot_nt(do_ref[rp, sl], vn_ref[rp, sl]), 0.0)
        ds = [dstate[h] for h in HEADS]
        for ci in range(2 * SCAN_PAIRS - 1, -1, -1):
            rs = slice(ci * CHUNK, (ci + 1) * CHUNK)
            in_pair = slice((ci % 2) * CHUNK, (ci % 2 + 1) * CHUNK)
            sm = [st_ref[ci, h] for h in HEADS]
            dvn = [dv_intra[ci // 2][h][in_pair] + _bdot(kd_ref[rs, sl], ds[h]) for h, sl in cols]
            for h, sl in cols:
                du_ref[rs, sl] = dvn[h].astype(BF16)
            dqd = [_bdot_nt(do_ref[rs, sl], sm[h]) for h, sl in cols]
            dw = [-_bdot_nt(dvn[h], sm[h]) for h, _ in cols]
            dkd = [_bdot_nt(vn_ref[rs, sl], ds[h]) for h, sl in cols]
            dcd = [jnp.broadcast_to(_rowsum(_colsum(ds[h] * sm[h])), (8, HEAD)) for h in HEADS]
            for h, sl in cols:
                dqd_ref[rs, sl] = dqd[h]
                dw_ref[rs, sl] = dw[h].astype(BF16)
                dkd_ref[rs, sl] = dkd[h]
                dcd_ref[ci * 8:(ci + 1) * 8, sl] = dcd[h]
            ds = [ds[h] * cd_ref[ci * 8:ci * 8 + 1, sl] + _bdot_tn(qd_ref[rs, sl], do_ref[rs, sl])
                  - _bdot_tn(w_ref[rs, sl], dvn[h]) for h, sl in cols]
        for h in HEADS:
            dstate[h] = ds[h]

    rev = pl.BlockSpec((SCAN_ROWS, D_HALF), lambda i: (n_steps - 1 - i, 0))
    rev_scalar = _chunk_scalar_spec(SCAN_PAIRS, lambda i: (n_steps - 1 - i, 0))
    return _call(
        body, name="scan_bwd", grid=(n_steps,),
        in_specs=[rev] * 6 + [rev_scalar,
                              pl.BlockSpec((2 * SCAN_PAIRS, N_HEADS, HEAD, HEAD), lambda i: (n_steps - 1 - i, 0, 0, 0))],
        out_specs=[rev] * 5 + [rev_scalar],
        out_shape=[_sds((s, D_HALF), BF16)] * 2 + [_sds((s, D_HALF))] * 3 + [_sds((s // 8, D_HALF))],
        scratch_shapes=[pltpu.VMEM((N_HEADS, HEAD, HEAD), F32)],
        compiler_params=_params("arbitrary"),
    )(do, vn, qd, kd, w, att, cd, st)


def _intra_bwd(qn, kn, vs, beta, g, tm, du, dw, datt, dqd, dkd, dcd):
    s = qn.shape[0]

    def body(qn_ref, kn_ref, vs_ref, beta_ref, g_ref, t_ref, du_ref, dw_ref, datt_ref, dqd_ref, dkd_ref, dcd_ref,
             dqn_ref, dkn_ref, dvs_ref, dbeta_ref, dg_ref):
        ones = jnp.ones((PAIR, HEAD), BF16)
        tn = (((0,), (0,)), ((), ()))
        kn, vs, beta = _heads(kn_ref), _heads(vs_ref), _heads(beta_ref)
        cm = _pair_common(_heads(qn_ref), kn, vs, beta, _heads(g_ref))
        tmv, duv, dwv, dattv, dqdv, dkdv = (_heads(r) for r in (t_ref, du_ref, dw_ref, datt_ref, dqd_ref, dkd_ref))
        dvb = _each(_bdot_tn, tmv, duv)
        dt = _each(lambda a, b, c, d: _bdot_nt(a, b) + _bdot_nt(c, d), duv, cm["vb"], dwv, cm["kbg"])
        dkbg = _each(_bdot_tn, tmv, dwv)
        m1 = _each(_bdot_tn, tmv, dt)
        da = _each(lambda a, b: -jnp.where(cm["strict"], _bdot_nt(a, b), 0.0), m1, tmv)
        dkk = _each(lambda a, b: a * b, da, cm["decay"])
        dqk = _each(lambda a, b: a * b, dattv, cm["decay"])
        dd = _each(lambda a, b, c, d: a * b + c * d, dkk, cm["kk"], dqk, cm["qk"])
        dkb = _each(lambda a, b, c, d: _bdot(a, b) + c * d, dkk, kn, dkbg, cm["egc"])
        dq = _each(lambda a, b, c, d: _bdot(a, b) + c * d, dqk, kn, dqdv, cm["egc"])
        dkn = _each(lambda a, b, c, d: _bdot_tn(a, b) + _bdot_tn(c, d), dkk, cm["kb"], dqk, cm["q"])
        dkn = _each(lambda a, b, c, d, e: a + b * c + d * e, dkn, dkdv, cm["ekd"], dkb, beta)
        t_kd = _each(lambda a, b, c: _rowsum(a * b * c), dkdv, kn, cm["ekd"])
        split = _each(_split, dd)
        rows_dd = [jnp.dot(hi, ones, preferred_element_type=F32) + jnp.dot(lo, ones, preferred_element_type=F32)
                   for hi, lo in split]
        cols_dd = [lax.dot_general(hi, ones, tn, preferred_element_type=F32)
                   + lax.dot_general(lo, ones, tn, preferred_element_type=F32) for hi, lo in split]
        dgc = _each(lambda r, c, a, b, e, f, k, t: r - c + _rowsum(a * b * e) + _rowsum(f * k) - t,
                    rows_dd, cols_dd, dqdv, cm["q"], cm["egc"], dkbg, cm["kbg"], t_kd)
        same_b = cm["same"].astype(BF16)
        rowi = lax.broadcasted_iota(I32, (PAIR, HEAD), 0)
        dcd = _each(lambda d: jnp.where(rowi < CHUNK, d[0:1], d[8:9]), _heads(dcd_ref, rows=16))
        dgl = _each(lambda t, d, c: _mask_dot(same_b, jnp.broadcast_to(t, (PAIR, HEAD))) + d * c, t_kd, dcd, cm["cd"])
        is_last = jnp.bitwise_and(rowi, CHUNK - 1) == CHUNK - 1
        dgc = _each(lambda a, b: a + jnp.where(is_last, b, 0.0), dgc, dgl)
        r = lax.broadcasted_iota(I32, (PAIR, PAIR), 0)
        c = lax.broadcasted_iota(I32, (PAIR, PAIR), 1)
        upper_b = (cm["same"] & (r <= c)).astype(BF16)
        _put_heads(dg_ref, _each(lambda v: _mask_dot(upper_b, v), dgc))
        _put_heads(dbeta_ref, _each(lambda a, b, c, d: jnp.broadcast_to(_rowsum(a * b) + _rowsum(c * d), (PAIR, HEAD)),
                                    dkb, kn, dvb, vs))
        _put_heads(dqn_ref, _each(lambda v: v * QK_SCALE, dq))
        _put_heads(dkn_ref, dkn)
        _put_heads(dvs_ref, _each(lambda a, b: a * b, dvb, beta))

    return _call(
        body, name="intra_bwd", grid=(s // (INTRA_PAIRS * PAIR),),
        in_specs=[_pair_spec()] * 11 + [_chunk_scalar_spec(INTRA_PAIRS)], out_specs=[_pair_spec()] * 5,
        out_shape=[_sds((s, D_HALF))] * 5,
        compiler_params=_params("arbitrary"),
    )(qn, kn, vs, beta, g, tm, du, dw, datt, dqd, dkd, dcd)


def _delta_bwd(do, vn, qd, kd, w, att, cd, st, qn, kn, vs, beta, g, tm):
    s = do.shape[0]
    n_steps = s // SCAN_ROWS
    assert INTRA_PAIRS == SCAN_PAIRS

    def body(do_ref, vn_ref, qd_ref, kd_ref, w_ref, att_ref, cd_ref, st_ref, qn_ref, kn_ref, vs_ref, beta_ref, g_ref, t_ref,
             dqn_ref, dkn_ref, dvs_ref, dbeta_ref, dg_ref, dstate, du_s, dw_s, datt_s, dqd_s, dkd_s, dcd_s):
        t = pl.program_id(0)

        @pl.when(t == 0)
        def _():
            dstate[...] = jnp.zeros_like(dstate)
            for ref in (du_s, dw_s, datt_s, dqd_s, dkd_s, dcd_s):
                ref[1] = jnp.zeros(ref.shape[1:], ref.dtype)

        cur = lax.rem(t, 2)
        prev = 1 - cur
        cols = list(enumerate(HEAD_COLS))
        _, incl, _, _ = _pair_masks()

        def recurrence():
            dv_intra = []
            for pp in range(SCAN_PAIRS):
                rp = slice(pp * PAIR, (pp + 1) * PAIR)
                dv_intra.append([_bdot_tn(att_ref[rp, sl], do_ref[rp, sl]) for _, sl in cols])
                for _, sl in cols:
                    datt_s[cur, rp, sl] = jnp.where(incl, _bdot_nt(do_ref[rp, sl], vn_ref[rp, sl]), 0.0)
                yield
            ds = [dstate[h] for h in HEADS]
            for ci in range(2 * SCAN_PAIRS - 1, -1, -1):
                rs = slice(ci * CHUNK, (ci + 1) * CHUNK)
                in_pair = slice((ci % 2) * CHUNK, (ci % 2 + 1) * CHUNK)
                sm = [st_ref[ci, h] for h in HEADS]
                dvn = [dv_intra[ci // 2][h][in_pair] + _bdot(kd_ref[rs, sl], ds[h]) for h, sl in cols]
                dqd = [_bdot_nt(do_ref[rs, sl], sm[h]) for h, sl in cols]
                dkd = [_bdot_nt(vn_ref[rs, sl], ds[h]) for h, sl in cols]
                dcd = [jnp.broadcast_to(_rowsum(_colsum(ds[h] * sm[h])), (8, HEAD)) for h in HEADS]
                yield
                dw = [-_bdot_nt(dvn[h], sm[h]) for h, _ in cols]
                for h, sl in cols:
                    du_s[cur, rs, sl] = dvn[h].astype(BF16)
                    dqd_s[cur, rs, sl] = dqd[h]
                    dw_s[cur, rs, sl] = dw[h].astype(BF16)
                    dkd_s[cur, rs, sl] = dkd[h]
                    dcd_s[cur, ci * 8:(ci + 1) * 8, sl] = dcd[h]
                ds = [ds[h] * cd_ref[ci * 8:ci * 8 + 1, sl] + _bdot_tn(qd_ref[rs, sl], do_ref[rs, sl])
                      - _bdot_tn(w_ref[rs, sl], dvn[h]) for h, sl in cols]
                yield
            for h in HEADS:
                dstate[h] = ds[h]

        def factors():
            ones = jnp.ones((PAIR, HEAD), BF16)
            tn = (((0,), (0,)), ((), ()))
            kept = lambda ref, rows=PAIR: [ref[prev, pp * rows:(pp + 1) * rows, HEAD_COLS[h]] for pp, h in UNITS]
            kn, vs, beta = _heads(kn_ref), _heads(vs_ref), _heads(beta_ref)
            cm = {}
            yield from _pair_common_stages(cm, _heads(qn_ref), kn, vs, beta, _heads(g_ref))
            tmv = _heads(t_ref)
            duv, dwv, dattv, dqdv, dkdv = kept(du_s), kept(dw_s), kept(datt_s), kept(dqd_s), kept(dkd_s)
            dvb = _each(_bdot_tn, tmv, duv)
            dt = _each(lambda a, b, c, d: _bdot_nt(a, b) + _bdot_nt(c, d), duv, cm["vb"], dwv, cm["kbg"])
            dkbg = _each(_bdot_tn, tmv, dwv)
            yield
            m1 = _each(_bdot_tn, tmv, dt)
            yield
            da = _each(lambda a, b: -jnp.where(cm["strict"], _bdot_nt(a, b), 0.0), m1, tmv)
            yield
            dkk = _each(lambda a, b: a * b, da, cm["decay"])
            dqk = _each(lambda a, b: a * b, dattv, cm["decay"])
            dd = _each(lambda a, b, c, d: a * b + c * d, dkk, cm["kk"], dqk, cm["qk"])
            dkb = _each(lambda a, b, c, d: _bdot(a, b) + c * d, dkk, kn, dkbg, cm["egc"])
            dq = _each(lambda a, b, c, d: _bdot(a, b) + c * d, dqk, kn, dqdv, cm["egc"])
            yield
            dkn = _each(lambda a, b, c, d: _bdot_tn(a, b) + _bdot_tn(c, d), dkk, cm["kb"], dqk, cm["q"])
            dkn = _each(lambda a, b, c, d, e: a + b * c + d * e, dkn, dkdv, cm["ekd"], dkb, beta)
            t_kd = _each(lambda a, b, c: _rowsum(a * b * c), dkdv, kn, cm["ekd"])
            yield
            split = _each(_split, dd)
            rows_dd = [jnp.dot(hi, ones, preferred_element_type=F32) + jnp.dot(lo, ones, preferred_element_type=F32)
                       for hi, lo in split]
            cols_dd = [lax.dot_general(hi, ones, tn, preferred_element_type=F32)
                       + lax.dot_general(lo, ones, tn, preferred_element_type=F32) for hi, lo in split]
            yield
            dgc = _each(lambda r, c, a, b, e, f, k, tk: r - c + _rowsum(a * b * e) + _rowsum(f * k) - tk,
                        rows_dd, cols_dd, dqdv, cm["q"], cm["egc"], dkbg, cm["kbg"], t_kd)
            same_b = cm["same"].astype(BF16)
            rowi = lax.broadcasted_iota(I32, (PAIR, HEAD), 0)
            dcd = _each(lambda d: jnp.where(rowi < CHUNK, d[0:1], d[8:9]), kept(dcd_s, rows=16))
            dgl = _each(lambda tk, d, c: _mask_dot(same_b, jnp.broadcast_to(tk, (PAIR, HEAD))) + d * c, t_kd, dcd, cm["cd"])
            yield
            is_last = jnp.bitwise_and(rowi, CHUNK - 1) == CHUNK - 1
            dgc = _each(lambda a, b: a + jnp.where(is_last, b, 0.0), dgc, dgl)
            r = lax.broadcasted_iota(I32, (PAIR, PAIR), 0)
            c = lax.broadcasted_iota(I32, (PAIR, PAIR), 1)
            upper_b = (cm["same"] & (r <= c)).astype(BF16)
            _put_heads(dg_ref, _each(lambda v: _mask_dot(upper_b, v), dgc))
            yield
            _put_heads(dbeta_ref, _each(lambda a, b, c, d: jnp.broadcast_to(_rowsum(a * b) + _rowsum(c * d), (PAIR, HEAD)),
                                        dkb, kn, dvb, vs))
            _put_heads(dqn_ref, _each(lambda v: v * QK_SCALE, dq))
            _put_heads(dkn_ref, dkn)
            _put_heads(dvs_ref, _each(lambda a, b: a * b, dvb, beta))
            yield

        _interleave(recurrence(), factors())

    last = n_steps - 1
    now = lambda i: (jnp.maximum(last - i, 0), 0)
    after = lambda i: (jnp.minimum(n_steps - i, last), 0)
    rows = lambda index: pl.BlockSpec((SCAN_ROWS, D_HALF), index)
    slot = lambda r, dtype: pltpu.VMEM((2, r, D_HALF), dtype)
    return _call(
        body, name="delta_bwd", grid=(n_steps + 1,),
        in_specs=[rows(now)] * 6 + [_chunk_scalar_spec(SCAN_PAIRS, now),
                                    pl.BlockSpec((2 * SCAN_PAIRS, N_HEADS, HEAD, HEAD), lambda i: (jnp.maximum(last - i, 0), 0, 0, 0))]
                 + [rows(after)] * 6,
        out_specs=[rows(after)] * 5,
        out_shape=[_sds((s, D_HALF))] * 5,
        scratch_shapes=[pltpu.VMEM((N_HEADS, HEAD, HEAD), F32), slot(SCAN_ROWS, BF16), slot(SCAN_ROWS, BF16),
                        slot(SCAN_ROWS, F32), slot(SCAN_ROWS, F32), slot(SCAN_ROWS, F32), slot(16 * SCAN_PAIRS, F32)],
        compiler_params=_params("arbitrary"),
    )(do, vn, qd, kd, w, att, cd, st, qn, kn, vs, beta, g, tm)


def _rows8(x):
    acc = x[0:8]
    for r in range(8, x.shape[0], 8):
        acc = acc + x[r:r + 8]
    return acc


def _conv_bwd(proj, conv_w, a_log, dt_bias, dqn, dkn, dvs, dbeta, dg):
    s = proj.shape[0]
    t = CONV_T
    n_tiles = s // t
    n_sub = t // CONV_SUB
    tile_of = lambda i: n_tiles - 1 - i

    def body(q_ref, k_ref, v_ref, hq_ref, hk_ref, hv_ref, ba_ref, cw_ref, al_ref, dtb_ref,
             dqn_ref, dkn_ref, dvs_ref, dbeta_ref, dg_ref, oq_ref, ok_ref, ov_ref, dba_ref, gcw_out, gsm_out,
             below, gcw_ref, gsm_ref):
        @pl.when(pl.program_id(0) == 0)
        def _():
            gcw_ref[...] = jnp.zeros_like(gcw_ref)
            gsm_ref[...] = jnp.zeros_like(gsm_ref)
            below[...] = jnp.zeros_like(below)

        live = (pl.program_id(0) < n_tiles - 1).astype(F32)
        parts = ((q_ref, hq_ref, dqn_ref, oq_ref), (k_ref, hk_ref, dkn_ref, ok_ref), (v_ref, hv_ref, dvs_ref, ov_ref))
        lane = lax.broadcasted_iota(I32, (CONV_SUB, HEAD), 1)
        lane8 = lax.broadcasted_iota(I32, (8, HEAD), 1)

        def sub_tile(r0, first):
            rows = pl.ds(r0, CONV_SUB)
            for p, (x_ref, h_ref, d_ref, o_ref) in enumerate(parts):
                for h in HEADS:
                    cs = HEAD_COLS[h]
                    wide = slice(p * D_HALF + h * HEAD, p * D_HALF + (h + 1) * HEAD)
                    cw = cw_ref[:, wide]
                    prev8 = h_ref[:, cs] * live if first else x_ref[pl.ds(r0 - 8, 8), cs]
                    taps = _conv_taps(x_ref[rows, cs], prev8)
                    y = _conv_pre(taps, cw)
                    sg = _sigmoid(y)
                    sv = y * sg
                    ds = d_ref[rows, cs]
                    if p < 2:
                        rn = lax.rsqrt(_rowsum(sv * sv) + EPS)
                        nrm = sv * rn
                        ds = rn * (ds - nrm * _rowsum(ds * nrm))
                    dy = ds * (sg * (1.0 + y * (1.0 - sg)))
                    for j in range(CONV_K):
                        gcw_ref[8 * j:8 * j + 8, wide] += _rows8(dy * taps[j])
                    nxt = below[:, wide]
                    acc = dy * cw[CONV_K - 1:CONV_K]
                    for sft in range(1, CONV_K):
                        acc = acc + _shift_up(dy, nxt, sft) * cw[CONV_K - 1 - sft:CONV_K - sft]
                    o_ref[rows, cs] = acc.astype(BF16)
                    below[:, wide] = dy[0:8]

            ba = ba_ref[rows, :]
            dba = jnp.zeros((CONV_SUB, HEAD), F32)
            gsm = jnp.zeros((8, HEAD), F32)
            for h in HEADS:
                beta = _sigmoid(ba[:, h:h + 1])
                dbeta = dbeta_ref[rows, h * HEAD:h * HEAD + 1]
                xg = ba[:, N_HEADS + h:N_HEADS + h + 1] + dtb_ref[0:1, h:h + 1]
                nexp = -jnp.exp(al_ref[0:1, h:h + 1])
                dgv = dg_ref[rows, h * HEAD:h * HEAD + 1]
                da = dgv * nexp * _sigmoid(xg)
                dba = dba + jnp.where(lane == h, dbeta * beta * (1.0 - beta), 0.0) + jnp.where(lane == N_HEADS + h, da, 0.0)
                gsm = (gsm + jnp.where(lane8 == h, _rows8(dgv * nexp * _softplus(xg)), 0.0)
                       + jnp.where(lane8 == N_HEADS + h, _rows8(da), 0.0))
            dba_ref[rows, :] = jnp.zeros((CONV_SUB, D_HALF), BF16)
            dba_ref[rows, :HEAD] = dba.astype(BF16)
            gsm_ref[...] += gsm

        def step(k, carry):
            sub_tile(pl.multiple_of((n_sub - 1 - k) * CONV_SUB, CONV_SUB), False)
            return carry

        lax.fori_loop(0, n_sub - 1, step, 0)
        sub_tile(0, True)

        @pl.when(pl.program_id(0) == n_tiles - 1)
        def _():
            gcw_out[...] = jnp.zeros_like(gcw_out)
            for j in range(CONV_K):
                gcw_out[j:j + 1, :] = _colsum(gcw_ref[8 * j:8 * j + 8, :])
            gsm_out[...] = jnp.broadcast_to(_colsum(gsm_ref[...]), (8, HEAD))

    row = pl.BlockSpec((t, D_HALF), lambda i: (tile_of(i), 0))
    const = lambda shape: pl.BlockSpec(shape, lambda i: (0, 0))
    return _call(
        body, name="conv_bwd", grid=(n_tiles,),
        in_specs=_conv_specs(t, tile_of) + [pl.BlockSpec((t, HEAD), lambda i: (tile_of(i), COL_BA // HEAD)),
                                            const((CONV_K, 3 * D_HALF)), const((1, N_HEADS)), const((1, N_HEADS))] + [row] * 5,
        out_specs=[row, row, row, row, const((8, 3 * D_HALF)), const((8, HEAD))],
        out_shape=[_sds((s, D_HALF), BF16)] * 4 + [_sds((8, 3 * D_HALF)), _sds((8, HEAD))],
        scratch_shapes=[pltpu.VMEM((8, 3 * D_HALF), F32), pltpu.VMEM((8 * CONV_K, 3 * D_HALF), F32),
                        pltpu.VMEM((8, HEAD), F32)],
        compiler_params=_params("arbitrary"),
    )(proj, proj, proj, proj, proj, proj, proj, conv_w, a_log, dt_bias, dqn, dkn, dvs, dbeta, dg)


def _conv_bwd_pre(proj, conv_w, a_log, dt_bias, dqn, dkn, dvs, dbeta, dg):
    s = proj.shape[0]
    t = CONV_T

    def body(q_ref, k_ref, v_ref, hq_ref, hk_ref, hv_ref, ba_ref, cw_ref, al_ref, dtb_ref,
             dqn_ref, dkn_ref, dvs_ref, dbeta_ref, dg_ref, dyq_ref, dyk_ref, dyv_ref, dba_ref, gcw_ref, gsm_ref):
        @pl.when(pl.program_id(0) == 0)
        def _():
            gcw_ref[...] = jnp.zeros_like(gcw_ref)
            gsm_ref[...] = jnp.zeros_like(gsm_ref)

        live = (pl.program_id(0) > 0).astype(F32)
        parts = ((q_ref, hq_ref, dqn_ref, dyq_ref), (k_ref, hk_ref, dkn_ref, dyk_ref), (v_ref, hv_ref, dvs_ref, dyv_ref))
        for p, (x_ref, h_ref, d_ref, dy_ref) in enumerate(parts):
            cols = slice(p * D_HALF, (p + 1) * D_HALF)
            taps = _conv_taps(x_ref[...], h_ref[...] * live)
            y = _conv_pre(taps, cw_ref[:, cols])
            sg = _sigmoid(y)
            sv = y * sg
            if p == 2:
                ds = d_ref[...]
            else:
                segs = []
                for h in HEADS:
                    seg = _head(sv, h)
                    rn = lax.rsqrt(_rowsum(seg * seg) + EPS)
                    nrm = seg * rn
                    dn = d_ref[:, HEAD_COLS[h]]
                    segs.append(rn * (dn - nrm * _rowsum(dn * nrm)))
                ds = jnp.concatenate(segs, axis=1)
            dy = ds * (sg * (1.0 + y * (1.0 - sg)))
            dy_ref[...] = dy
            for j in range(CONV_K):
                gcw_ref[j:j + 1, cols] += _colsum(dy * taps[j])

        ba = ba_ref[...]
        lane = lax.broadcasted_iota(I32, (t, HEAD), 1)
        lane1 = lax.broadcasted_iota(I32, (1, HEAD), 1)
        dba = jnp.zeros((t, HEAD), F32)
        gsm = jnp.zeros((1, HEAD), F32)
        for h in HEADS:
            beta = _sigmoid(ba[:, h:h + 1])
            dbeta = dbeta_ref[:, h * HEAD:h * HEAD + 1]
            xg = ba[:, N_HEADS + h:N_HEADS + h + 1] + dtb_ref[0:1, h:h + 1]
            nexp = -jnp.exp(al_ref[0:1, h:h + 1])
            dgv = dg_ref[:, h * HEAD:h * HEAD + 1]
            da = dgv * nexp * _sigmoid(xg)
            dba = dba + jnp.where(lane == h, dbeta * beta * (1.0 - beta), 0.0) + jnp.where(lane == N_HEADS + h, da, 0.0)
            gsm = (gsm + jnp.where(lane1 == h, _colsum(dgv * nexp * _softplus(xg)), 0.0)
                   + jnp.where(lane1 == N_HEADS + h, _colsum(da), 0.0))
        dba_ref[...] = jnp.zeros_like(dba_ref)
        dba_ref[:, :HEAD] = dba.astype(BF16)
        gsm_ref[0:1, :] += gsm

    row = pl.BlockSpec((t, D_HALF), lambda i: (i, 0))
    return _call(
        body, name="conv_bwd_pre", grid=(s // t,),
        in_specs=_conv_specs(t) + [pl.BlockSpec((t, HEAD), lambda i: (i, COL_BA // HEAD)),
                                   pl.BlockSpec((CONV_K, 3 * D_HALF), lambda i: (0, 0)),
                                   pl.BlockSpec((1, N_HEADS), lambda i: (0, 0)),
                                   pl.BlockSpec((1, N_HEADS), lambda i: (0, 0))] + [row] * 5,
        out_specs=[row, row, row, row,
                   pl.BlockSpec((8, 3 * D_HALF), lambda i: (0, 0)), pl.BlockSpec((8, HEAD), lambda i: (0, 0))],
        out_shape=[_sds((s, D_HALF))] * 3 + [_sds((s, D_HALF), BF16), _sds((8, 3 * D_HALF)), _sds((8, HEAD))],
        compiler_params=_params("arbitrary"),
    )(proj, proj, proj, proj, proj, proj, proj, conv_w, a_log, dt_bias, dqn, dkn, dvs, dbeta, dg)


def _conv_bwd_in(dyq, dyk, dyv, conv_w):
    s = dyq.shape[0]
    t = CONV_T
    last = s // 8 - 1

    def body(q_ref, k_ref, v_ref, nq_ref, nk_ref, nv_ref, cw_ref, oq_ref, ok_ref, ov_ref):
        more = (pl.program_id(0) < pl.num_programs(0) - 1).astype(F32)
        for p, (d_ref, n_ref, o_ref) in enumerate(((q_ref, nq_ref, oq_ref), (k_ref, nk_ref, ok_ref), (v_ref, nv_ref, ov_ref))):
            cw = cw_ref[:, p * D_HALF:(p + 1) * D_HALF]
            dy = d_ref[...]
            nxt = n_ref[...] * more
            acc = dy * cw[3:4]
            for sft in (1, 2, 3):
                acc = acc + _shift_up(dy, nxt, sft) * cw[3 - sft:4 - sft]
            o_ref[...] = acc.astype(BF16)

    row = pl.BlockSpec((t, D_HALF), lambda i: (i, 0))
    nxt = pl.BlockSpec((8, D_HALF), lambda i: (jnp.minimum((i + 1) * (t // 8), last), 0))
    return _call(
        body, name="conv_bwd_in", grid=(s // t,),
        in_specs=[row] * 3 + [nxt] * 3 + [pl.BlockSpec((CONV_K, 3 * D_HALF), lambda i: (0, 0))],
        out_specs=[row] * 3, out_shape=[_sds((s, D_HALF), BF16)] * 3,
        compiler_params=_params("arbitrary"),
    )(dyq, dyk, dyv, dyq, dyk, dyv, conv_w)


IN_T = 512


def _in_bwd(x, dh, norm_w, w_pad, pieces):
    s = x.shape[0]
    t = IN_T
    widths = [D_HALF] * 6 + [N_IN_PAD - COL_BA]

    def body(*refs):
        x_ref, dh_ref, nw_ref, w_ref = refs[:4]
        p_refs = refs[4:4 + len(pieces)]
        gx_ref, gnw_ref = refs[4 + len(pieces):]

        @pl.when(pl.program_id(0) == 0)
        def _():
            gnw_ref[...] = jnp.zeros_like(gnw_ref)

        dn = jnp.zeros((t, D_MODEL), F32)
        col = 0
        for p_ref, wd in zip(p_refs, widths):
            dn = dn + _bdot_nt(p_ref[...], w_ref[:, col:col + wd])
            col += wd
        xv = x_ref[...]
        r = lax.rsqrt(jnp.mean(xv * xv, axis=-1, keepdims=True) + EPS)
        xhat = xv * r
        gnw_ref[...] += _colsum(dn * xhat)
        dxh = dn * nw_ref[...]
        gx_ref[...] = dh_ref[...] + r * (dxh - xhat * jnp.mean(dxh * xhat, axis=-1, keepdims=True))

    wide = pl.BlockSpec((t, D_MODEL), lambda i: (i, 0))
    return _call(
        body, name="in_bwd", grid=(s // t,),
        in_specs=[wide, wide, pl.BlockSpec((1, D_MODEL), lambda i: (0, 0)),
                  pl.BlockSpec((D_MODEL, N_IN_PAD), lambda i: (0, 0))]
                 + [pl.BlockSpec((t, wd), lambda i: (i, 0)) for wd in widths],
        out_specs=[wide, pl.BlockSpec((1, D_MODEL), lambda i: (0, 0))],
        out_shape=[_sds((s, D_MODEL)), _sds((1, D_MODEL))],
        compiler_params=_params("arbitrary"),
    )(x, dh, norm_w, w_pad, *pieces)


def _adamw_shard(name, w, g_own, g_got, cidx, m, v):
    _, r, c = w.shape
    half = r // 2
    rows = 256 if half % 256 == 0 else half
    per_half = half // rows

    def body(c_ref, w_ref, go_ref, gg_ref, m_ref, v_ref, gout_ref, d_ref, nm_ref, nv_ref):
        mine = (pl.program_id(0) // per_half) == c_ref[0]
        gv = jnp.where(mine, go_ref[:, :c], gg_ref[:, :c])
        gout_ref[0] = gv
        mn = ADAM_B1 * m_ref[0] + (1.0 - ADAM_B1) * gv
        vn = ADAM_B2 * v_ref[0] + (1.0 - ADAM_B2) * (gv * gv)
        m_hat = mn / (1.0 - ADAM_B1 ** ADAM_STEP)
        v_hat = vn / (1.0 - ADAM_B2 ** ADAM_STEP)
        d_ref[0] = -ADAM_LR * (m_hat / (jnp.sqrt(v_hat) + ADAM_EPS) + ADAM_WD * w_ref[0])
        nm_ref[0] = mn
        nv_ref[0] = vn

    blk = pl.BlockSpec((1, rows, c), lambda i, c_ref: (0, i, 0))
    gblk = pl.BlockSpec((rows, g_own.shape[1]), lambda i, c_ref: (i % per_half, 0))
    return _call(
        body, name=name,
        grid_spec=pltpu.PrefetchScalarGridSpec(
            num_scalar_prefetch=1, grid=(2 * per_half,),
            in_specs=[blk, gblk, gblk, blk, blk], out_specs=[blk] * 4),
        out_shape=[_sds((1, r, c))] * 4,
        compiler_params=_params("arbitrary"),
    )(cidx, w, g_own, g_got, m, v)


def _adamw_tiles(name, w, g, m, v):
    n = w.shape[0]
    nb = 77 if n % 77 == 0 else n

    def body(w_ref, g_ref, m_ref, v_ref, d_ref, nm_ref, nv_ref):
        gv = g_ref[...]
        mn = ADAM_B1 * m_ref[...] + (1.0 - ADAM_B1) * gv
        vn = ADAM_B2 * v_ref[...] + (1.0 - ADAM_B2) * (gv * gv)
        m_hat = mn / (1.0 - ADAM_B1 ** ADAM_STEP)
        v_hat = vn / (1.0 - ADAM_B2 ** ADAM_STEP)
        d_ref[...] = -ADAM_LR * (m_hat / (jnp.sqrt(v_hat) + ADAM_EPS) + ADAM_WD * w_ref[...])
        nm_ref[...] = mn
        nv_ref[...] = vn

    blk = pl.BlockSpec((nb, 8, HEAD), lambda i: (i, 0, 0))
    return _call(
        body, name=name, grid=(n // nb,),
        in_specs=[blk] * 4, out_specs=[blk] * 3, out_shape=[_sds(w.shape)] * 3,
        compiler_params=_params("arbitrary"),
    )(w, g, m, v)


def _exchange(name, inputs, out_shapes, phases):
    n_in = len(inputs)
    n_out = len(out_shapes)
    n_cp = sum(len(p) for p in phases)

    def body(*refs):
        ins, outs = refs[:n_in], refs[n_in:n_in + n_out]
        send, recv = refs[n_in + n_out:]
        pos = (lax.axis_index("x"), lax.axis_index("y"), lax.axis_index("c"))
        k = 0
        for phase in phases:
            cps = []
            for src, dst, target in phase:
                cps.append(pltpu.make_async_remote_copy(
                    src_ref=src(ins, outs, pos), dst_ref=dst(ins, outs, pos), send_sem=send.at[k], recv_sem=recv.at[k],
                    device_id=target(pos), device_id_type=pl.DeviceIdType.MESH))
                k += 1
            for cp in cps:
                cp.start()
            for cp in cps:
                cp.wait()

    anyspec = pl.BlockSpec(memory_space=pl.ANY)
    return _call(
        body, name=name,
        in_specs=[anyspec] * n_in, out_specs=[anyspec] * n_out, out_shape=list(out_shapes),
        scratch_shapes=[pltpu.SemaphoreType.DMA((n_cp,)), pltpu.SemaphoreType.DMA((n_cp,))],
    )(*inputs)


def _exchange_start(name, inputs, out_shapes, copies):
    n_in, n_out, n_cp = len(inputs), len(out_shapes), len(copies)

    def body(*refs):
        ins, lands = refs[:n_in], refs[n_in:n_in + n_out]
        sems = refs[n_in + n_out:n_in + n_out + 2 * n_cp]
        token = refs[-1]
        pos = (lax.axis_index("x"), lax.axis_index("y"), lax.axis_index("c"))
        for k, (src, dst, target) in enumerate(copies):
            pltpu.make_async_remote_copy(
                src_ref=src(ins, lands, pos), dst_ref=dst(ins, lands, pos), send_sem=sems[2 * k], recv_sem=sems[2 * k + 1],
                device_id=target(pos), device_id_type=pl.DeviceIdType.MESH).start()
        token[...] = jnp.zeros_like(token)

    hbm = pl.BlockSpec(memory_space=pltpu.HBM)
    sem = pl.BlockSpec(memory_space=pltpu.SEMAPHORE)
    bufs = list(inputs) + [lax.empty(o.shape, o.dtype) for o in out_shapes]
    outs = _call(
        body, name=name,
        out_shape=tuple([pltpu.SemaphoreType.DMA(())] * (2 * n_cp) + [pltpu.HBM(b.shape, b.dtype) for b in bufs]
                        + [_sds((8, HEAD))]),
        in_specs=[hbm] * len(bufs),
        out_specs=tuple([sem] * (2 * n_cp) + [hbm] * len(bufs) + [pl.BlockSpec(memory_space=pltpu.VMEM)]),
        input_output_aliases={i: 2 * n_cp + i for i in range(len(bufs))},
        compiler_params=pltpu.CompilerParams(has_side_effects=pltpu.SideEffectType.DATAFLOW_SIDE_EFFECTING),
    )(*[pltpu.with_memory_space_constraint(b, pltpu.HBM) for b in bufs])
    return outs[:2 * n_cp], outs[2 * n_cp:2 * n_cp + n_in], outs[2 * n_cp + n_in:-1], outs[-1]


def _exchange_wait(name, sems, sources, lands, copies, after):
    n_in, n_out, n_cp = len(sources), len(lands), len(copies)

    def body(*refs):
        ins, zones = refs[:n_in], refs[n_in:n_in + n_out]
        sem_refs = refs[n_in + n_out:n_in + n_out + 2 * n_cp]
        pos = (lax.axis_index("x"), lax.axis_index("y"), lax.axis_index("c"))
        for k, (src, dst, target) in enumerate(copies):
            cp = pltpu.make_async_remote_copy(
                src_ref=src(ins, zones, pos), dst_ref=dst(ins, zones, pos), send_sem=sem_refs[2 * k],
                recv_sem=sem_refs[2 * k + 1], device_id=target(pos), device_id_type=pl.DeviceIdType.MESH)
            cp.wait_send()
            cp.wait_recv()

    hbm = pl.BlockSpec(memory_space=pltpu.HBM)
    sem = pl.BlockSpec(memory_space=pltpu.SEMAPHORE)
    bufs = list(sources) + list(lands)
    outs = _call(
        body, name=name,
        out_shape=tuple(pltpu.HBM(b.shape, b.dtype) for b in bufs),
        in_specs=[hbm] * len(bufs) + [sem] * (2 * n_cp) + [pl.BlockSpec(memory_space=pl.ANY)],
        out_specs=tuple([hbm] * len(bufs)),
        input_output_aliases={i: i for i in range(len(bufs))},
        compiler_params=pltpu.CompilerParams(has_side_effects=pltpu.SideEffectType.DATAFLOW_SIDE_EFFECTING),
    )(*bufs, *sems, after)
    return outs[:n_in], outs[n_in:]


def _allreduce_tile(name, v):
    def body(v_ref, out_ref, slots, send, recv):
        x, y, c = lax.axis_index("x"), lax.axis_index("y"), lax.axis_index("c")
        me = 4 * x + 2 * y + c
        slots[me] = v_ref[...]
        cps = []
        for k in range(1, 8):
            peer = (x ^ (k >> 2), y ^ ((k >> 1) & 1), c ^ (k & 1))
            cps.append(pltpu.make_async_remote_copy(
                src_ref=v_ref, dst_ref=slots.at[me], send_sem=send.at[k - 1], recv_sem=recv.at[k - 1],
                device_id=peer, device_id_type=pl.DeviceIdType.MESH))
        for cp in cps:
            cp.start()
        for cp in cps:
            cp.wait()
        acc = slots[0]
        for i in range(1, 8):
            acc = acc + slots[i]
        out_ref[...] = acc

    vm = pl.BlockSpec(memory_space=pltpu.VMEM)
    return _call(
        body, name=name, in_specs=[vm], out_specs=vm, out_shape=_sds(v.shape),
        scratch_shapes=[pltpu.VMEM((8,) + v.shape, F32), pltpu.SemaphoreType.DMA((7,)), pltpu.SemaphoreType.DMA((7,))],
    )(v)


def _chip(pos):
    return 2 * pos[0] + pos[1]


def _other_chip(pos, mask):
    x, y, c = pos
    return (x ^ (mask >> 1), y ^ (mask & 1), c)


def _sibling(pos):
    return (pos[0], pos[1], 1 - pos[2])


def _gather_weights(wb, cb):
    rows = wb.shape[0] // 2
    x_nb, y_nb, diag = CHIP_MASKS

    def part(pos, mask, quarter=None):
        start = pos[2] * rows if quarter is None else pos[2] * rows + quarter * (rows // 2)
        return lambda outs: outs[0].at[_chip(pos) ^ mask, pl.ds(start, rows if quarter is None else rows // 2)]

    def passed_on(mask, to, quarter=None):
        return (lambda ins, outs, pos: part(pos, mask, quarter)(outs), lambda ins, outs, pos: part(pos, mask, quarter)(outs), to)

    first = [(lambda ins, outs, pos: ins[0].at[pl.ds(pos[2] * rows, rows)], lambda ins, outs, pos: part(pos, 0)(outs),
              functools.partial(_other_chip, mask=mask)) for mask in (x_nb, y_nb)]
    first += [(lambda ins, outs, pos: ins[1], lambda ins, outs, pos: outs[1].at[_chip(pos)],
               functools.partial(_other_chip, mask=mask)) for mask in CHIP_MASKS]
    second = [passed_on(x_nb, functools.partial(_other_chip, mask=y_nb), quarter=0),
              passed_on(y_nb, functools.partial(_other_chip, mask=x_nb), quarter=1),
              passed_on(x_nb, _sibling), passed_on(y_nb, _sibling)]
    third = [passed_on(diag, _sibling)]
    return _exchange("gather_weights", [wb, cb], [_sds((4,) + wb.shape, wb.dtype), _sds((4,) + cb.shape, cb.dtype)],
                     [first, second, third])


def _assemble_w_in(gw, wb, jidx):
    m = gw.shape[1]

    def body(j_ref, g_ref, wb_ref, o_ref):
        step = pl.program_id(0)

        @pl.when(step == 0)
        def _():
            o_ref[...] = jnp.zeros_like(o_ref)

        blk = jnp.where(step == j_ref[0], wb_ref[...], g_ref[0]).astype(F32)
        lane = lax.broadcasted_iota(I32, (m, BLK_IN_PAD), 1)
        for j in range(4):
            @pl.when(step == j)
            def _(j=j):
                base = j * BLK_IN // HEAD * HEAD
                shift = j * BLK_IN - base
                moved = pltpu.roll(blk, shift, 1) if shift else blk
                window = o_ref[:, base:base + BLK_IN_PAD].astype(F32)
                mine = (lane >= shift) & (lane < shift + BLK_IN)
                o_ref[:, base:base + BLK_IN_PAD] = jnp.where(mine, moved, window).astype(BF16)

    return _call(
        body, name="assemble_w_in",
        grid_spec=pltpu.PrefetchScalarGridSpec(
            num_scalar_prefetch=1, grid=(4,),
            in_specs=[pl.BlockSpec((1, m, BLK_IN_PAD), lambda j, j_ref: (j, 0, 0)),
                      pl.BlockSpec((m, BLK_IN_PAD), lambda j, j_ref: (0, 0))],
            out_specs=pl.BlockSpec((m, N_IN_PAD), lambda j, j_ref: (0, 0))),
        out_shape=_sds((m, N_IN_PAD), BF16),
        compiler_params=_params("arbitrary"),
    )(jidx, gw, wb)


def _gather_blocks(ob):
    copies = [(lambda ins, outs, pos: ins[0], lambda ins, outs, pos: outs[0].at[_chip(pos)],
               functools.partial(_other_chip, mask=mask)) for mask in CHIP_MASKS]
    return [_sds((4,) + ob.shape, ob.dtype)], copies


def _to_sibling_half(name, arrays):
    def src(ins, outs, pos, a):
        h = arrays[a].shape[-2] // 2
        sl = pl.ds((1 - pos[2]) * h, h)
        return ins[a].at[:, sl] if arrays[a].ndim == 3 else ins[a].at[sl]

    outs = [_sds(a.shape[:-2] + (a.shape[-2] // 2, a.shape[-1]), a.dtype) for a in arrays]
    phase = [(functools.partial(src, a=a), lambda ins, outs, pos, a=a: outs[a], _sibling) for a in range(len(arrays))]
    return _exchange(name, arrays, outs, [phase])


def _add_half(name, full, part, cidx):
    shape = part.shape
    lead = shape[0] if len(shape) == 3 else 1
    rows, cols = shape[-2], shape[-1]
    tr = rows // 2 if rows % 16 == 0 else rows
    nr = rows // tr
    f3 = full.reshape((lead,) + full.shape[-2:])
    p3 = part.reshape((lead, rows, cols))

    def body(c_ref, f_ref, p_ref, o_ref):
        o_ref[...] = (f_ref[...].astype(F32) + p_ref[...].astype(F32)).astype(o_ref.dtype)

    out = _call(
        body, name=name,
        grid_spec=pltpu.PrefetchScalarGridSpec(
            num_scalar_prefetch=1, grid=(lead, nr),
            in_specs=[pl.BlockSpec((1, tr, cols), lambda b, r, c_ref: (b, c_ref[0] * nr + r, 0)),
                      pl.BlockSpec((1, tr, cols), lambda b, r, c_ref: (b, r, 0))],
            out_specs=pl.BlockSpec((1, tr, cols), lambda b, r, c_ref: (b, r, 0))),
        out_shape=_sds((lead, rows, cols), part.dtype),
        compiler_params=_params("arbitrary", "arbitrary"),
    )(cidx, f3, p3)
    return out.reshape(shape)


def _to_other_chips(arrays, blocked):
    def src(ins, outs, pos, a, mask):
        return ins[a].at[_chip(pos) ^ mask] if blocked[a] else ins[a]

    outs = [_sds((3,) + (a.shape[1:] if b else a.shape), a.dtype) for a, b in zip(arrays, blocked)]
    copies = []
    for mi, mask in enumerate(CHIP_MASKS):
        for a in range(len(arrays)):
            copies.append((functools.partial(src, a=a, mask=mask), lambda ins, outs, pos, a=a, mi=mi: outs[a].at[mi],
                           functools.partial(_other_chip, mask=mask)))
    return outs, copies


def _add_chips(name, own, got, jidx, blocked):
    rows, cols = got.shape[-2:]
    tr = rows // 2 if rows % 16 == 0 else rows
    nr = rows // tr
    o3 = own if blocked else own.reshape((1, rows, cols))

    def body(j_ref, o_ref, g_ref, out_ref):
        out_ref[...] = ((o_ref[0].astype(F32) + g_ref[0].astype(F32))
                        + (g_ref[1].astype(F32) + g_ref[2].astype(F32)))

    own_map = (lambda r, j_ref: (j_ref[0], r, 0)) if blocked else (lambda r, j_ref: (0, r, 0))
    return _call(
        body, name=name,
        grid_spec=pltpu.PrefetchScalarGridSpec(
            num_scalar_prefetch=1, grid=(nr,),
            in_specs=[pl.BlockSpec((1, tr, cols), own_map),
                      pl.BlockSpec((3, tr, cols), lambda r, j_ref: (0, r, 0))],
            out_specs=pl.BlockSpec((tr, cols), lambda r, j_ref: (r, 0))),
        out_shape=_sds((rows, cols)),
        compiler_params=_params("arbitrary"),
    )(jidx, o3, got)


def _to_sibling(name, arrays):
    phase = [(lambda ins, outs, pos, a=a: ins[a], lambda ins, outs, pos, a=a: outs[a], _sibling)
             for a in range(len(arrays))]
    return _exchange(name, arrays, [_sds(a.shape, a.dtype) for a in arrays], [phase])


def _local_step(x, target, w_pad, w_out, conv_w, norm_w, pool_w, pool_scale, a_log, dt_bias, dn_norm_w, final_norm_w):
    proj, n_t = _proj_fwd(x, norm_w, w_pad)
    y_pool = _pool_fwd(proj, pool_w, pool_scale)
    qn, kn, vs, beta, g = _conv_fwd(proj, conv_w, a_log, dt_bias)
    w, att, qd, kd, tm, cd, o, vn, st = _delta_fwd(qn, kn, vs, beta, g)
    w_out = w_out(o) if callable(w_out) else w_out
    g_wout, dh, dyp, do, ddz, loss, g_fnw, g_dnw = _out_fwd_bwd(x, y_pool, o, proj, target, w_out, dn_norm_w, final_norm_w)
    dpu, dpz, g_pw, g_ps = _pool_bwd(proj, dyp, pool_w, pool_scale)
    dqn, dkn, dvs, dbeta, dg = _delta_bwd(do, vn, qd, kd, w, att, cd, st, qn, kn, vs, beta, g, tm)
    dcq, dck, dcv, dba, g_cw, g_sm = _conv_bwd(proj, conv_w, a_log, dt_bias, dqn, dkn, dvs, dbeta, dg)
    pieces = [dpu, dpz, dcq, dck, dcv, ddz, dba]
    g_win = _grad_w_in(n_t, pieces)
    small = dict(norm_w=jnp.zeros_like(norm_w), pool_w=g_pw, pool_scale=g_ps, conv_w=g_cw[:CONV_K],
                 a_log=g_sm[0:1, 0:N_HEADS], dt_bias=g_sm[0:1, N_HEADS:2 * N_HEADS], dn_norm_w=g_dnw, final_norm_w=g_fnw)
    return loss[0, 0], g_win, g_wout, small, dh, pieces


SMALL_LAYOUT = (("pool_w", 512, HEAD, (1, N_HEADS, HEAD, HEAD)), ("final_norm_w", 8, HEAD, (D_MODEL,)),
                ("pool_scale", 4, HEAD, (1, D_HALF)), ("conv_w", 48, HEAD, (1, CONV_K, 3 * D_HALF)),
                ("dn_norm_w", 1, HEAD, (1, HEAD)), ("a_log", 1, N_HEADS, (1, N_HEADS)), ("dt_bias", 1, N_HEADS, (1, N_HEADS)),
                ("loss", 1, 1, ()))


def _small_offsets():
    offs, r = {}, 0
    for name, rows, _, _ in SMALL_LAYOUT:
        offs[name] = r
        r += -(-rows // 8) * 8
    assert r <= SMALL_ROWS
    return offs


def _pack_small(t):
    parts = []
    for name, rows, lanes, _ in SMALL_LAYOUT:
        a = t.get(name, jnp.zeros((1,), F32)).reshape(rows, lanes)
        parts.append(jnp.pad(a, ((0, -(-rows // 8) * 8 - rows), (0, HEAD - lanes))))
    buf = jnp.concatenate(parts, axis=0)
    return jnp.pad(buf, ((0, SMALL_ROWS - buf.shape[0]), (0, 0)))


def _adamw_small(w, g_own, g_got, cidx, m, v):
    offs = _small_offsets()
    names = [e[0] for e in SMALL_LAYOUT]
    n = len(names)

    def body(c_ref, w_ref, go_ref, gg_ref, m_ref, v_ref, *outs):
        own_low = c_ref[0] == 0
        gv = jnp.concatenate([jnp.where(own_low, go_ref[...], gg_ref[...]), jnp.where(own_low, gg_ref[...], go_ref[...])], axis=0)
        mn = ADAM_B1 * m_ref[...] + (1.0 - ADAM_B1) * gv
        vn = ADAM_B2 * v_ref[...] + (1.0 - ADAM_B2) * (gv * gv)
        m_hat = mn / (1.0 - ADAM_B1 ** ADAM_STEP)
        v_hat = vn / (1.0 - ADAM_B2 ** ADAM_STEP)
        dl = -ADAM_LR * (m_hat / (jnp.sqrt(v_hat) + ADAM_EPS) + ADAM_WD * w_ref[...])
        for kind, arr in enumerate((gv, dl, mn, vn)):
            for i, (name, rows, lanes, _) in enumerate(SMALL_LAYOUT):
                outs[kind * n + i][...] = arr[offs[name]:offs[name] + rows, :lanes]

    whole = lambda shape: pl.BlockSpec(shape, lambda i, c_ref: (0,) * len(shape))
    out_shapes = [_sds((rows, lanes)) for _, rows, lanes, _ in SMALL_LAYOUT] * 4
    res = _call(
        body, name="adamw_small",
        grid_spec=pltpu.PrefetchScalarGridSpec(
            num_scalar_prefetch=1, grid=(1,),
            in_specs=[whole(w.shape), whole(g_own.shape), whole(g_got.shape), whole(m.shape), whole(v.shape)],
            out_specs=[whole(o.shape) for o in out_shapes]),
        out_shape=out_shapes,
        compiler_params=_params("arbitrary"),
    )(cidx, w, g_own, g_got, m, v)
    return [{name: res[kind * n + i].reshape(shape) for i, (name, _, _, shape) in enumerate(SMALL_LAYOUT)}
            for kind in range(4)]


def kernel(x, norm_w, w_in, pool_w, pool_scale, conv_w, a_log, dt_bias, dn_norm_w, w_out, final_norm_w, loss_target, m_norm_w, m_w_in, m_pool_w, m_pool_scale, m_conv_w, m_a_log, m_dt_bias, m_dn_norm_w, m_w_out, m_final_norm_w, v_norm_w, v_w_in, v_pool_w, v_pool_scale, v_conv_w, v_a_log, v_dt_bias, v_dn_norm_w, v_w_out, v_final_norm_w):
    cidx = lax.axis_index("c").astype(I32).reshape(1)
    jidx = (2 * lax.axis_index("x") + lax.axis_index("y")).astype(I32)

    wb = jnp.pad(w_in[0].astype(BF16), ((0, 0), (0, BLK_IN_PAD - BLK_IN)))
    ob = w_out[0].astype(BF16)
    gw, gc = _gather_weights(wb, conv_w[0])
    mine = lambda j: jidx == j
    w_pad = _assemble_w_in(gw, wb, jidx.reshape(1))
    cw_full = jnp.concatenate([jnp.where(mine(j), conv_w[0], gc[j]) for j in range(4)], axis=1)

    lands_o, copies_o = _gather_blocks(ob)
    sems_o, ob_thru, zones_o, token_o = _exchange_start("gather_w_out_start", [ob], lands_o, copies_o)

    def w_out_full(after):
        (own,), (got,) = _exchange_wait("gather_w_out_wait", sems_o, ob_thru, zones_o, copies_o, after)
        return jnp.where((jnp.arange(4) == jidx)[:, None, None], own[None], got).reshape(D_MODEL, D_MODEL)

    loss, g_win, g_wout, small, dh, pieces = _local_step(
        x[0], loss_target[0], w_pad, w_out_full, cw_full, norm_w + token_o[0, 0], pool_w[0], pool_scale, a_log, dt_bias,
        dn_norm_w, final_norm_w.reshape(1, D_MODEL))
    small["loss"] = loss

    blocks_out = g_wout.reshape(4, BLK_OUT, D_MODEL)
    full = [g_win, blocks_out, _pack_small(small)]
    from_sib = _to_sibling_half("reduce_sibling", full)
    chip_sum = [_add_half("add_sibling_%d" % i, f, p, cidx) for i, (f, p) in enumerate(zip(full, from_sib))]
    blocked = [True, True, False]
    lands, copies = _to_other_chips(chip_sum, blocked)
    sems, chip_sum, zones, token = _exchange_start("reduce_chips_start", chip_sum, lands, copies)
    gx, g_nw = _in_bwd(x[0], dh, norm_w + token[0, 0], w_pad, pieces)
    g_nw = _allreduce_tile("reduce_norm_w", g_nw.reshape(8, HEAD)).reshape(1, D_MODEL)
    chip_sum, from_chips = _exchange_wait("reduce_chips_wait", sems, chip_sum, zones, copies, gx)
    halves = [_add_chips("add_chips_%d" % i, o, g, jidx.reshape(1), b)
              for i, (o, g, b) in enumerate(zip(chip_sum, from_chips, blocked))]
    other_halves = _to_sibling("swap_halves", halves)

    weights = dict(norm_w=norm_w, w_in=w_in, pool_w=pool_w, pool_scale=pool_scale, conv_w=conv_w, a_log=a_log,
                   dt_bias=dt_bias, dn_norm_w=dn_norm_w, w_out=w_out, final_norm_w=final_norm_w)
    ms = dict(norm_w=m_norm_w, w_in=m_w_in, pool_w=m_pool_w, pool_scale=m_pool_scale, conv_w=m_conv_w, a_log=m_a_log,
              dt_bias=m_dt_bias, dn_norm_w=m_dn_norm_w, w_out=m_w_out, final_norm_w=m_final_norm_w)
    vs = dict(norm_w=v_norm_w, w_in=v_w_in, pool_w=v_pool_w, pool_scale=v_pool_scale, conv_w=v_conv_w, a_log=v_a_log,
              dt_bias=v_dt_bias, dn_norm_w=v_dn_norm_w, w_out=v_w_out, final_norm_w=v_final_norm_w)
    names = ["norm_w", "w_in", "pool_w", "pool_scale", "conv_w", "a_log", "dt_bias", "dn_norm_w", "w_out", "final_norm_w"]
    small_names = [n for n in names if n not in ("w_in", "w_out")]

    def pack(t):
        conv = lax.dynamic_update_slice_in_dim(jnp.zeros((CONV_K, 3 * D_HALF), F32), t["conv_w"][0], jidx * BLK_CONV, axis=1)
        return _pack_small({**{n: t[n] for n in small_names if n != "conv_w"}, "conv_w": conv})

    results = [{}, {}, {}, {}]
    to_tiles = lambda a: jnp.transpose(a, (2, 0, 1)).reshape(BLK_IN, 8, HEAD)
    from_tiles = lambda a: jnp.transpose(a, (1, 2, 0)).reshape(1, D_MODEL, BLK_IN)
    lo = jnp.where(cidx[0] == 0, halves[0], other_halves[0])
    hi = jnp.where(cidx[0] == 0, other_halves[0], halves[0])
    g_tiles = jnp.concatenate([lo[:, :BLK_IN].T, hi[:, :BLK_IN].T], axis=1).reshape(BLK_IN, 8, HEAD)
    outs = _adamw_tiles("adamw_w_in", to_tiles(w_in), g_tiles, to_tiles(m_w_in), to_tiles(v_w_in))
    for res, o in zip(results, (g_tiles,) + tuple(outs)):
        res["w_in"] = from_tiles(o)
    outs = _adamw_shard("adamw_w_out", w_out, halves[1], other_halves[1], cidx, m_w_out, v_w_out)
    for res, o in zip(results, outs):
        res["w_out"] = o
    outs = _adamw_small(pack(weights), halves[2], other_halves[2], cidx, pack(ms), pack(vs))
    for res, got in zip(results, outs):
        got["conv_w"] = lax.dynamic_slice_in_dim(got["conv_w"], jidx * BLK_CONV, BLK_CONV, axis=2)
        res.update(got)
    one_tile = lambda a: a.reshape(1, 8, HEAD)
    outs = _adamw_tiles("adamw_norm_w", one_tile(norm_w), one_tile(g_nw), one_tile(m_norm_w), one_tile(v_norm_w))
    for res, o in zip(results, (g_nw,) + tuple(outs)):
        res["norm_w"] = o.reshape(1, D_MODEL)
    grads, delta, new_m, new_v = results

    return (grads["loss"], gx[None], *[grads[n] for n in names], *[delta[n] for n in names],
            *[new_m[n] for n in names], *[new_v[n] for n in names])
```

```python
import functools

import jax
import jax.numpy as jnp
import numpy as np
from jax import lax
from jax.experimental import pallas as pl
from jax.experimental.pallas import tpu as pltpu

F32 = jnp.float32
BF16 = jnp.bfloat16
I32 = jnp.int32

D_MODEL = 1024
D_HALF = 512
N_HEADS = 4
HEAD = 128
CHUNK = 64
PAIR = 2 * CHUNK
WINDOWS = (2, 4, 8, 16)
CONV_K = 4
EPS = 1e-6
N_IN = 3080
N_IN_PAD = 3200
BLK_IN = 770
BLK_IN_PAD = 896
BLK_OUT = 256
BLK_CONV = 384
COL_BA = 3072
QK_SCALE = HEAD ** -0.5
SMALL_ROWS = 608
VMEM_LIMIT = 56 * 1024 * 1024

ADAM_LR = 0.001
ADAM_B1 = 0.9
ADAM_B2 = 0.999
ADAM_EPS = 1e-08
ADAM_WD = 0.01
ADAM_STEP = 10

CHIP_MASKS = (2, 1, 3)
HEADS = range(N_HEADS)
HEAD_COLS = [slice(h * HEAD, (h + 1) * HEAD) for h in HEADS]


def _call(body, **kw):
    return pl.pallas_call(body, **kw)


def _params(*sem):
    return pltpu.CompilerParams(dimension_semantics=sem, vmem_limit_bytes=VMEM_LIMIT)


def _sds(shape, dtype=F32):
    return jax.ShapeDtypeStruct(shape, dtype)


def _bdot(a, b):
    return jnp.dot(a.astype(BF16), b.astype(BF16), preferred_element_type=F32)


def _bdot_nt(a, b):
    return lax.dot_general(a.astype(BF16), b.astype(BF16), (((1,), (1,)), ((), ())), preferred_element_type=F32)


def _bdot_tn(a, b):
    return lax.dot_general(a.astype(BF16), b.astype(BF16), (((0,), (0,)), ((), ())), preferred_element_type=F32)


def _split(a):
    hi = a.astype(BF16)
    lo = (a - hi.astype(F32)).astype(BF16)
    return hi, lo


def _mask_dot(m, b, dims=(((1,), (0,)), ((), ()))):
    bh, bl = _split(b)
    dg = functools.partial(lax.dot_general, dimension_numbers=dims, preferred_element_type=F32)
    return dg(m, bh) + dg(m, bl)


def _sigmoid(x):
    return 0.5 * jnp.tanh(0.5 * x) + 0.5


def _softplus(x):
    return jnp.maximum(x, 0.0) + jnp.log(1.0 + jnp.exp(-jnp.abs(x)))


def _rowsum(x):
    return jnp.sum(x, axis=-1, keepdims=True)


def _colsum(x):
    return jnp.sum(x, axis=0, keepdims=True)


def _shift_down(xv, prev8, k):
    r = pltpu.roll(xv, k, 0)
    q = pltpu.roll(prev8, k, 0)
    row = lax.broadcasted_iota(I32, prev8.shape, 0)
    top = jnp.where(row < k, q, r[0:8])
    return jnp.concatenate([top, r[8:]], axis=0)


def _shift_up(xv, next8, k):
    t = xv.shape[0]
    r = pltpu.roll(xv, t - k, 0)
    q = pltpu.roll(next8, 8 - k, 0)
    row = lax.broadcasted_iota(I32, next8.shape, 0)
    bot = jnp.where(row >= 8 - k, q, r[t - 8:])
    return jnp.concatenate([r[:t - 8], bot], axis=0)


def _band(rows, cols, off, w, anti=False):
    r = lax.broadcasted_iota(I32, (rows, cols), 0)
    c = lax.broadcasted_iota(I32, (rows, cols), 1)
    d = (c - r + off) if anti else (r - c + off)
    return ((d >= 0) & (d < w)).astype(BF16)


def _head(ref_or_val, h):
    return ref_or_val[:, h * HEAD:(h + 1) * HEAD]


INTRA_PAIRS = 2
UNITS = [(pp, h) for pp in range(INTRA_PAIRS) for h in HEADS]


def _heads(ref, rows=PAIR):
    return [ref[pp * rows:(pp + 1) * rows, HEAD_COLS[h]] for pp, h in UNITS]


def _put_heads(ref, vals, rows=PAIR):
    for (pp, h), v in zip(UNITS, vals):
        ref[pp * rows:(pp + 1) * rows, HEAD_COLS[h]] = v.astype(ref.dtype)


def _each(fn, *lists):
    return [fn(*args) for args in zip(*lists)]


def _proj_fwd(x, norm_w, w_pad):
    s = x.shape[0]
    tm = 512

    def body(x_ref, nw_ref, w_ref, proj_ref, nt_ref):
        xv = x_ref[...]
        r = lax.rsqrt(jnp.mean(xv * xv, axis=-1, keepdims=True) + EPS)
        nv = xv * r * nw_ref[...]
        nt_ref[...] = nv.T.astype(BF16)
        proj_ref[...] = jnp.dot(nv.astype(BF16), w_ref[...], preferred_element_type=F32)

    return _call(
        body, name="proj_fwd", grid=(s // tm,),
        in_specs=[pl.BlockSpec((tm, D_MODEL), lambda i: (i, 0)),
                  pl.BlockSpec((1, D_MODEL), lambda i: (0, 0)),
                  pl.BlockSpec((D_MODEL, N_IN_PAD), lambda i: (0, 0))],
        out_specs=[pl.BlockSpec((tm, N_IN_PAD), lambda i: (i, 0)),
                   pl.BlockSpec((D_MODEL, tm), lambda i: (0, i))],
        out_shape=[_sds((s, N_IN_PAD)), _sds((D_MODEL, s), BF16)],
        compiler_params=_params("arbitrary"),
    )(x, norm_w, w_pad)


def _pool_bands(t, anti=False):
    r = np.arange(t)[:, None]
    c = np.arange(t + HEAD)[None, :]
    d = (c - r) if anti else (r - c + HEAD)
    return jnp.asarray(np.stack([(d >= 0) & (d < w) for w in WINDOWS]), BF16)


def _pool_mix(u, halo, z, pw, bands, row0):
    t = u[0].shape[0]
    rows = row0 + lax.broadcasted_iota(I32, (t, 1), 0) + 1
    cnt = [jnp.minimum(rows, w).astype(F32) for w in WINDOWS]
    win = _each(lambda b, h, v: _mask_dot(b, jnp.concatenate([h, v], axis=0)), bands, halo, u)
    mix = _each(lambda a, c, v: a / c - v, win, cnt, u)
    mixed = _each(_bdot, mix, pw)
    return mix, mixed, _each(_sigmoid, z), cnt


POOL_T = 256


def _pool_fwd(proj, pool_w, pool_scale):
    s = proj.shape[0]
    t = POOL_T
    hb = t // HEAD

    def body(u_ref, z_ref, halo_ref, pw_ref, ps_ref, band_ref, y_ref):
        i = pl.program_id(0)
        live = (i > 0).astype(F32)
        groups = lambda ref: [ref[:, sl] for sl in HEAD_COLS]
        z = groups(z_ref)
        _, mixed, sg, _ = _pool_mix(groups(u_ref), [h * live for h in groups(halo_ref)], z,
                                    [pw_ref[g] for g in HEADS], [band_ref[g] for g in HEADS], i * t)
        for sl, m, zg, s_ in zip(HEAD_COLS, mixed, z, sg):
            y_ref[:, sl] = m * ps_ref[:, sl] * (zg * s_)

    return _call(
        body, name="pool_fwd", grid=(s // t,),
        in_specs=[pl.BlockSpec((t, D_HALF), lambda i: (i, 0)),
                  pl.BlockSpec((t, D_HALF), lambda i: (i, 1)),
                  pl.BlockSpec((HEAD, D_HALF), lambda i: (jnp.maximum(i * hb - 1, 0), 0)),
                  pl.BlockSpec((N_HEADS, HEAD, HEAD), lambda i: (0, 0, 0)),
                  pl.BlockSpec((1, D_HALF), lambda i: (0, 0)),
                  pl.BlockSpec((N_HEADS, t, HEAD + t), lambda i: (0, 0, 0))],
        out_specs=pl.BlockSpec((t, D_HALF), lambda i: (i, 0)),
        out_shape=_sds((s, D_HALF)),
        compiler_params=_params("arbitrary"),
    )(proj, proj, proj, pool_w, pool_scale, _pool_bands(t))


def _conv_taps(xv, prev8):
    return [_shift_down(xv, prev8, CONV_K - 1 - j) for j in range(CONV_K - 1)] + [xv]


def _conv_pre(taps, cw):
    y = taps[CONV_K - 1] * cw[CONV_K - 1:CONV_K]
    for j in range(CONV_K - 2, -1, -1):
        y = y + taps[j] * cw[j:j + 1]
    return y


CONV_T = 256
CONV_SUB = 256


def _conv_specs(t, tile_of=lambda i: i):
    tiles = [pl.BlockSpec((t, D_HALF), functools.partial(lambda i, p: (tile_of(i), 2 + p), p=p)) for p in range(3)]
    halos = [pl.BlockSpec((8, D_HALF),
                          functools.partial(lambda i, p: (jnp.maximum(tile_of(i) * (t // 8) - 1, 0), 2 + p), p=p))
             for p in range(3)]
    return tiles + halos


def _conv_fwd(proj, conv_w, a_log, dt_bias):
    s = proj.shape[0]
    t = CONV_T

    def body(q_ref, k_ref, v_ref, hq_ref, hk_ref, hv_ref, ba_ref, cw_ref, al_ref, dtb_ref,
             qn_ref, kn_ref, vs_ref, beta_ref, g_ref):
        live = (pl.program_id(0) > 0).astype(F32)
        parts = ((q_ref, hq_ref, qn_ref), (k_ref, hk_ref, kn_ref), (v_ref, hv_ref, vs_ref))

        def sub_tile(r0, first):
            rows = pl.ds(r0, CONV_SUB)
            for p, (x_ref, h_ref, o_ref) in enumerate(parts):
                for h in HEADS:
                    cs = HEAD_COLS[h]
                    prev8 = h_ref[:, cs] * live if first else x_ref[pl.ds(r0 - 8, 8), cs]
                    y = _conv_pre(_conv_taps(x_ref[rows, cs], prev8), cw_ref[:, p * D_HALF + h * HEAD:p * D_HALF + (h + 1) * HEAD])
                    sv = y * _sigmoid(y)
                    o_ref[rows, cs] = sv if p == 2 else sv * lax.rsqrt(_rowsum(sv * sv) + EPS)
            ba = ba_ref[rows, :]
            for h in HEADS:
                beta = _sigmoid(ba[:, h:h + 1])
                gl = -jnp.exp(al_ref[0:1, h:h + 1]) * _softplus(ba[:, N_HEADS + h:N_HEADS + h + 1] + dtb_ref[0:1, h:h + 1])
                beta_ref[rows, HEAD_COLS[h]] = jnp.broadcast_to(beta, (CONV_SUB, HEAD))
                g_ref[rows, HEAD_COLS[h]] = jnp.broadcast_to(gl, (CONV_SUB, HEAD))

        sub_tile(0, True)

        def step(k, carry):
            sub_tile(pl.multiple_of(k * CONV_SUB, CONV_SUB), False)
            return carry

        lax.fori_loop(1, t // CONV_SUB, step, 0)

    row = pl.BlockSpec((t, D_HALF), lambda i: (i, 0))
    return _call(
        body, name="conv_fwd", grid=(s // t,),
        in_specs=_conv_specs(t) + [pl.BlockSpec((t, HEAD), lambda i: (i, COL_BA // HEAD)),
                                   pl.BlockSpec((CONV_K, 3 * D_HALF), lambda i: (0, 0)),
                                   pl.BlockSpec((1, N_HEADS), lambda i: (0, 0)),
                                   pl.BlockSpec((1, N_HEADS), lambda i: (0, 0))],
        out_specs=[row] * 5,
        out_shape=[_sds((s, D_HALF))] * 5,
        compiler_params=_params("arbitrary"),
    )(proj, proj, proj, proj, proj, proj, proj, conv_w, a_log, dt_bias)


def _pair_masks():
    r = lax.broadcasted_iota(I32, (PAIR, PAIR), 0)
    c = lax.broadcasted_iota(I32, (PAIR, PAIR), 1)
    same = jnp.right_shift(r, 6) == jnp.right_shift(c, 6)
    return same, same & (r >= c), same & (r > c), r == c


def _run(stages):
    for _ in stages:
        pass


def _interleave(*stage_lists):
    live = list(stage_lists)
    while live:
        for gen in list(live):
            try:
                next(gen)
            except StopIteration:
                live.remove(gen)


def _pair_common_stages(cm, qn, kn, vs, beta, g):
    same, incl, strict, eye = _pair_masks()
    incl_b = incl.astype(BF16)
    first = lax.broadcasted_iota(I32, (PAIR, HEAD), 0) < CHUNK
    cm.update(same=same, incl=incl, strict=strict, eye=eye)
    gc = _each(lambda gv: _mask_dot(incl_b, gv), g)
    q = _each(lambda v: v * QK_SCALE, qn)
    kb = _each(lambda k, b: k * b, kn, beta)
    cm.update(gc=gc, q=q, kb=kb, vb=_each(lambda v, b: v * b, vs, beta))
    yield
    cm.update(kk=_each(_bdot_nt, kb, kn), qk=_each(_bdot_nt, q, kn))
    gc_row = _each(lambda v: _colsum(jnp.where(eye, v, 0.0)), gc)
    gl = _each(lambda v: jnp.where(first, v[CHUNK - 1:CHUNK], v[PAIR - 1:PAIR]), gc)
    egc = _each(jnp.exp, gc)
    cm.update(gl=gl, egc=egc,
              decay=_each(lambda v, r: jnp.where(incl, jnp.exp(jnp.where(incl, v - r, 0.0)), 0.0), gc, gc_row))
    yield
    cm.update(ekd=_each(lambda a, b: jnp.exp(a - b), gl, gc), cd=_each(jnp.exp, gl),
              kbg=_each(lambda k, e: k * e, kb, egc))
    yield


def _pair_common(qn, kn, vs, beta, g):
    cm = {}
    _run(_pair_common_stages(cm, qn, kn, vs, beta, g))
    return cm


def _tri_inv_stages(out, a, eye_f):
    p = _each(lambda v: eye_f - v, a)
    x = _each(_bdot, a, a)
    yield
    for it in range(5):
        p = _each(lambda pv, xv: pv + _bdot(pv, xv), p, x)
        if it < 4:
            x = _each(_bdot, x, x)
        yield
    out["t"] = p


def _tri_inv(a, eye_f):
    out = {}
    _run(_tri_inv_stages(out, a, eye_f))
    return out["t"]


def _pair_spec():
    return pl.BlockSpec((INTRA_PAIRS * PAIR, D_HALF), lambda i: (i, 0))


def _chunk_scalar_spec(pairs=1, index=lambda i: (i, 0)):
    return pl.BlockSpec((16 * pairs, D_HALF), index)


SCAN_PAIRS = 2
SCAN_ROWS = SCAN_PAIRS * PAIR


def _intra_fwd(qn, kn, vs, beta, g):
    s = qn.shape[0]

    def body(qn_ref, kn_ref, vs_ref, beta_ref, g_ref, u_ref, w_ref, att_ref, qd_ref, kd_ref, t_ref, cd_ref):
        kn = _heads(kn_ref)
        cm = _pair_common(_heads(qn_ref), kn, _heads(vs_ref), _heads(beta_ref), _heads(g_ref))
        a = _each(lambda kk, d: jnp.where(cm["strict"], kk * d, 0.0), cm["kk"], cm["decay"])
        tm = _tri_inv(a, cm["eye"].astype(F32))
        _put_heads(t_ref, tm)
        _put_heads(u_ref, _each(_bdot, tm, cm["vb"]))
        _put_heads(w_ref, _each(_bdot, tm, cm["kbg"]))
        _put_heads(att_ref, _each(lambda a, b: a * b, cm["qk"], cm["decay"]))
        _put_heads(qd_ref, _each(lambda a, b: a * b, cm["q"], cm["egc"]))
        _put_heads(kd_ref, _each(lambda a, b: a * b, kn, cm["ekd"]))
        for ci in range(2):
            for (pp, h), v in zip(UNITS, cm["cd"]):
                cd_ref[pp * 16 + ci * 8:pp * 16 + (ci + 1) * 8, HEAD_COLS[h]] = v[ci * CHUNK:ci * CHUNK + 8]

    return _call(
        body, name="intra_fwd", grid=(s // (INTRA_PAIRS * PAIR),),
        in_specs=[_pair_spec()] * 5, out_specs=[_pair_spec()] * 6 + [_chunk_scalar_spec(INTRA_PAIRS)],
        out_shape=[_sds((s, D_HALF))] + [_sds((s, D_HALF), BF16)] * 5 + [_sds((s // 8, D_HALF))],
        compiler_params=_params("arbitrary"),
    )(qn, kn, vs, beta, g)


def _scan_fwd(u, w, att, qd, kd, cd):
    s = u.shape[0]
    n_chunks = s // CHUNK

    def body(u_ref, w_ref, att_ref, qd_ref, kd_ref, cd_ref, o_ref, vn_ref, st_ref, state):
        @pl.when(pl.program_id(0) == 0)
        def _():
            state[...] = jnp.zeros_like(state)
        cols = list(enumerate(HEAD_COLS))
        sm = [state[h] for h in HEADS]
        for ci in range(2 * SCAN_PAIRS):
            rs = slice(ci * CHUNK, (ci + 1) * CHUNK)
            for h in HEADS:
                st_ref[ci, h] = sm[h]
            both = [_bdot(jnp.concatenate([w_ref[rs, sl], qd_ref[rs, sl]], axis=0), sm[h]) for h, sl in cols]
            vn = [u_ref[rs, sl] - both[h][:CHUNK] for h, sl in cols]
            for h, sl in cols:
                vn_ref[rs, sl] = vn[h].astype(BF16)
                o_ref[rs, sl] = both[h][CHUNK:]
            sm = [sm[h] * cd_ref[ci * 8:ci * 8 + 1, sl] + _bdot_tn(kd_ref[rs, sl], vn[h]) for h, sl in cols]
        for h in HEADS:
            state[h] = sm[h]
        for pp in range(SCAN_PAIRS):
            rp = slice(pp * PAIR, (pp + 1) * PAIR)
            intra = [_bdot(att_ref[rp, sl], vn_ref[rp, sl]) for sl in HEAD_COLS]
            for h, sl in cols:
                o_ref[rp, sl] += intra[h]

    rows = pl.BlockSpec((SCAN_ROWS, D_HALF), lambda i: (i, 0))
    return _call(
        body, name="scan_fwd", grid=(s // SCAN_ROWS,),
        in_specs=[rows] * 5 + [_chunk_scalar_spec(SCAN_PAIRS)],
        out_specs=[rows, rows, pl.BlockSpec((2 * SCAN_PAIRS, N_HEADS, HEAD, HEAD), lambda i: (i, 0, 0, 0))],
        out_shape=[_sds((s, D_HALF)), _sds((s, D_HALF), BF16), _sds((n_chunks, N_HEADS, HEAD, HEAD))],
        scratch_shapes=[pltpu.VMEM((N_HEADS, HEAD, HEAD), F32)],
        compiler_params=_params("arbitrary"),
    )(u, w, att, qd, kd, cd)


def _delta_fwd(qn, kn, vs, beta, g):
    s = qn.shape[0]
    n_steps = s // SCAN_ROWS
    n_chunks = s // CHUNK
    assert INTRA_PAIRS == SCAN_PAIRS

    def body(qn_ref, kn_ref, vs_ref, beta_ref, g_ref, w_ref, att_ref, qd_ref, kd_ref, t_ref, cd_ref, o_ref, vn_ref, st_ref,
             state, u_s, w_s, att_s, qd_s, kd_s, cd_s):
        t = pl.program_id(0)

        @pl.when(t <= 1)
        def _():
            state[...] = jnp.zeros_like(state)

        @pl.when(t == 0)
        def _():
            for ref in (u_s, w_s, att_s, qd_s, kd_s, cd_s):
                ref[1] = jnp.zeros(ref.shape[1:], ref.dtype)

        cur = lax.rem(t, 2)
        prev = 1 - cur
        cols = list(enumerate(HEAD_COLS))

        def recurrence():
            sm = [state[h] for h in HEADS]
            for ci in range(2 * SCAN_PAIRS):
                rs = slice(ci * CHUNK, (ci + 1) * CHUNK)
                for h in HEADS:
                    st_ref[ci, h] = sm[h]
                both = [_bdot(jnp.concatenate([w_s[prev, rs, sl], qd_s[prev, rs, sl]], axis=0), sm[h]) for h, sl in cols]
                vn = [u_s[prev, rs, sl] - both[h][:CHUNK] for h, sl in cols]
                for h, sl in cols:
                    vn_ref[rs, sl] = vn[h].astype(BF16)
                    o_ref[rs, sl] = both[h][CHUNK:]
                yield
                sm = [sm[h] * cd_s[prev, ci * 8:ci * 8 + 1, sl] + _bdot_tn(kd_s[prev, rs, sl], vn[h]) for h, sl in cols]
                yield
            for h in HEADS:
                state[h] = sm[h]
            for pp in range(SCAN_PAIRS):
                rp = slice(pp * PAIR, (pp + 1) * PAIR)
                intra = [_bdot(att_s[prev, rp, sl], vn_ref[rp, sl]) for sl in HEAD_COLS]
                for h, sl in cols:
                    o_ref[rp, sl] += intra[h]
                yield

        def factors():
            kn = _heads(kn_ref)
            cm = {}
            yield from _pair_common_stages(cm, _heads(qn_ref), kn, _heads(vs_ref), _heads(beta_ref), _heads(g_ref))
            a = _each(lambda kk, d: jnp.where(cm["strict"], kk * d, 0.0), cm["kk"], cm["decay"])
            inv = {}
            yield from _tri_inv_stages(inv, a, cm["eye"].astype(F32))
            tm = inv["t"]
            res = dict(u=_each(_bdot, tm, cm["vb"]), w=_each(_bdot, tm, cm["kbg"]),
                       att=_each(lambda a, b: a * b, cm["qk"], cm["decay"]),
                       qd=_each(lambda a, b: a * b, cm["q"], cm["egc"]), kd=_each(lambda a, b: a * b, kn, cm["ekd"]))
            yield
            _put_heads(t_ref, tm)
            for key, out, keep in (("w", w_ref, w_s), ("att", att_ref, att_s), ("qd", qd_ref, qd_s), ("kd", kd_ref, kd_s)):
                _put_heads(out, res[key])
                for (pp, h), v in zip(UNITS, res[key]):
                    keep[cur, pp * PAIR:(pp + 1) * PAIR, HEAD_COLS[h]] = v.astype(BF16)
            for (pp, h), v in zip(UNITS, res["u"]):
                u_s[cur, pp * PAIR:(pp + 1) * PAIR, HEAD_COLS[h]] = v
            for ci in range(2):
                for (pp, h), v in zip(UNITS, cm["cd"]):
                    rows8 = slice(pp * 16 + ci * 8, pp * 16 + (ci + 1) * 8)
                    cd_ref[rows8, HEAD_COLS[h]] = v[ci * CHUNK:ci * CHUNK + 8]
                    cd_s[cur, rows8, HEAD_COLS[h]] = v[ci * CHUNK:ci * CHUNK + 8]
            yield

        _interleave(recurrence(), factors())

    last = n_steps - 1
    now = lambda i: (jnp.minimum(i, last), 0)
    before = lambda i: (jnp.maximum(i - 1, 0), 0)
    rows = lambda index: pl.BlockSpec((SCAN_ROWS, D_HALF), index)
    slot = lambda r, dtype: pltpu.VMEM((2, r, D_HALF), dtype)
    return _call(
        body, name="delta_fwd", grid=(n_steps + 1,),
        in_specs=[rows(now)] * 5,
        out_specs=[rows(now)] * 5 + [_chunk_scalar_spec(SCAN_PAIRS, now), rows(before), rows(before),
                                     pl.BlockSpec((2 * SCAN_PAIRS, N_HEADS, HEAD, HEAD), lambda i: (jnp.maximum(i - 1, 0), 0, 0, 0))],
        out_shape=[_sds((s, D_HALF), BF16)] * 5 + [_sds((s // 8, D_HALF)), _sds((s, D_HALF)), _sds((s, D_HALF), BF16),
                                                  _sds((n_chunks, N_HEADS, HEAD, HEAD))],
        scratch_shapes=[pltpu.VMEM((N_HEADS, HEAD, HEAD), F32), slot(SCAN_ROWS, F32), slot(SCAN_ROWS, BF16),
                        slot(SCAN_ROWS, BF16), slot(SCAN_ROWS, BF16), slot(SCAN_ROWS, BF16), slot(16 * SCAN_PAIRS, F32)],
        compiler_params=_params("arbitrary"),
    )(qn, kn, vs, beta, g)


OUT_T = 512


def _out_fwd_bwd(x, y_pool, o, proj, target, w_out, dn_norm_w, final_norm_w):
    s = x.shape[0]
    t = OUT_T

    def body(x_ref, yp_ref, o_ref, z_ref, tg_ref, wo_ref, dnw_ref, fnw_ref,
             gwo_ref, dh_ref, dyp_ref, do_ref, dz_ref, loss_ref, gfn_ref, gdn_ref, y_ref, yt_ref, gwo_acc):
        @pl.when(pl.program_id(0) == 0)
        def _():
            loss_ref[...] = jnp.zeros_like(loss_ref)
            gfn_ref[...] = jnp.zeros_like(gfn_ref)
            gdn_ref[...] = jnp.zeros_like(gdn_ref)
            gwo_acc[...] = jnp.zeros_like(gwo_acc)

        ypv = yp_ref[...]
        y_ref[:, :D_HALF] = ypv.astype(BF16)
        yt_ref[:D_HALF, :] = ypv.T.astype(BF16)
        dnw = dnw_ref[...]
        keep = []
        for h in HEADS:
            ov = o_ref[:, HEAD_COLS[h]]
            zv = z_ref[:, HEAD_COLS[h]]
            ro = lax.rsqrt(jnp.mean(ov * ov, axis=-1, keepdims=True) + EPS)
            ohat = ov * ro
            sg = _sigmoid(zv)
            keep.append((ro, ohat, zv, sg))
            ydn = ohat * dnw * (zv * sg)
            y_ref[:, D_HALF + h * HEAD:D_HALF + (h + 1) * HEAD] = ydn.astype(BF16)
            yt_ref[D_HALF + h * HEAD:D_HALF + (h + 1) * HEAD, :] = ydn.T.astype(BF16)

        hv = x_ref[...] + jnp.dot(y_ref[...], wo_ref[...], preferred_element_type=F32)
        r2 = lax.rsqrt(jnp.mean(hv * hv, axis=-1, keepdims=True) + EPS)
        hhat = hv * r2
        fnw = fnw_ref[...]
        err = hhat * fnw - tg_ref[...]
        loss_ref[...] += 0.5 * jnp.sum(_rowsum(err * err) * (1.0 / D_MODEL), axis=0, keepdims=True)
        dout = err * (1.0 / D_MODEL)
        gfn_ref[...] += _colsum(dout * hhat)
        dhh = dout * fnw
        dh = r2 * (dhh - hhat * jnp.mean(dhh * hhat, axis=-1, keepdims=True))
        dh_ref[...] = dh
        gwo_acc[...] += _bdot(yt_ref[...], dh)

        @pl.when(pl.program_id(0) == pl.num_programs(0) - 1)
        def _():
            gwo_ref[...] = gwo_acc[...].astype(BF16)

        dy = _bdot_nt(dh, wo_ref[...])
        dyp_ref[...] = dy[:, :D_HALF]
        gdn = jnp.zeros((1, HEAD), F32)
        for h in HEADS:
            ro, ohat, zv, sg = keep[h]
            dyd = dy[:, D_HALF + h * HEAD:D_HALF + (h + 1) * HEAD]
            sz = zv * sg
            dz_ref[:, HEAD_COLS[h]] = (dyd * ohat * dnw * (sg * (1.0 + zv * (1.0 - sg)))).astype(BF16)
            gdn = gdn + _colsum(dyd * ohat * sz)
            doh = dyd * dnw * sz
            do_ref[:, HEAD_COLS[h]] = ro * (doh - ohat * jnp.mean(doh * ohat, axis=-1, keepdims=True))
        gdn_ref[...] += gdn

    wide = pl.BlockSpec((t, D_MODEL), lambda i: (i, 0))
    half = pl.BlockSpec((t, D_HALF), lambda i: (i, 0))
    const = lambda shape: pl.BlockSpec(shape, lambda i: (0,) * len(shape))
    return _call(
        body, name="out_fwd_bwd", grid=(s // t,),
        in_specs=[wide, half, half, pl.BlockSpec((t, D_HALF), lambda i: (i, 5)), wide,
                  const((D_MODEL, D_MODEL)), const((1, HEAD)), const((1, D_MODEL))],
        out_specs=[const((D_MODEL, D_MODEL)), wide, half, half, half,
                   const((1, HEAD)), const((1, D_MODEL)), const((1, HEAD))],
        out_shape=[_sds((D_MODEL, D_MODEL), BF16), _sds((s, D_MODEL)), _sds((s, D_HALF)), _sds((s, D_HALF)),
                   _sds((s, D_HALF), BF16), _sds((1, HEAD)), _sds((1, D_MODEL)), _sds((1, HEAD))],
        scratch_shapes=[pltpu.VMEM((t, D_MODEL), BF16), pltpu.VMEM((D_MODEL, t), BF16), pltpu.VMEM((D_MODEL, D_MODEL), F32)],
        compiler_params=_params("arbitrary"),
    )(x, y_pool, o, proj, target, w_out, dn_norm_w, final_norm_w)


def _token_matmul(name, at, pieces):
    m, s = at.shape
    n = len(pieces)
    tn, tk = D_HALF, 512

    def body(a_ref, *refs):
        p_refs, o_ref, acc = refs[:n], refs[n], refs[n + 1]

        @pl.when(pl.program_id(0) == 0)
        def _():
            acc[...] = jnp.zeros_like(acc)

        av = a_ref[...]
        for p in range(n):
            acc[:, p * tn:(p + 1) * tn] += _bdot(av, p_refs[p][...])

        @pl.when(pl.program_id(0) == pl.num_programs(0) - 1)
        def _():
            o_ref[...] = acc[...].astype(BF16)

    return _call(
        body, name=name, grid=(s // tk,),
        in_specs=[pl.BlockSpec((m, tk), lambda k: (0, k))]
                 + [pl.BlockSpec((tk, tn), functools.partial(lambda k, cb: (k, cb), cb=cb)) for _, cb in pieces],
        out_specs=pl.BlockSpec((m, n * tn), lambda k: (0, 0)),
        out_shape=_sds((m, n * tn), BF16),
        scratch_shapes=[pltpu.VMEM((m, n * tn), F32)],
        compiler_params=_params("arbitrary"),
    )(at, *[p[0] for p in pieces])


def _grad_w_in(at, pieces):
    m, s = at.shape
    n = len(pieces)
    tn, tk = D_HALF, min(s, 1024)

    def body(a_ref, *refs):
        p_refs, o_ref, acc = refs[:n], refs[n], refs[n + 1]

        @pl.when(pl.program_id(0) == 0)
        def _():
            acc[...] = jnp.zeros_like(acc)

        av = a_ref[...]
        for p in range(n):
            acc[:, p * tn:(p + 1) * tn] += _bdot(av, p_refs[p][...])

        @pl.when(pl.program_id(0) == pl.num_programs(0) - 1)
        def _():
            for j in range(4):
                base = j * BLK_IN // HEAD * HEAD
                win = acc[:, base:base + BLK_IN_PAD]
                if j * BLK_IN > base:
                    win = pltpu.roll(win, BLK_IN_PAD - (j * BLK_IN - base), 1)
                o_ref[j] = win.astype(BF16)

    return _call(
        body, name="grad_w_in", grid=(s // tk,),
        in_specs=[pl.BlockSpec((m, tk), lambda k: (0, k))] + [pl.BlockSpec((tk, tn), lambda k: (k, 0))] * n,
        out_specs=pl.BlockSpec((4, m, BLK_IN_PAD), lambda k: (0, 0, 0)),
        out_shape=_sds((4, m, BLK_IN_PAD), BF16),
        scratch_shapes=[pltpu.VMEM((m, n * tn), F32)],
        compiler_params=_params("arbitrary"),
    )(at, *pieces)


def _pool_bwd(proj, dyp, pool_w, pool_scale):
    s = proj.shape[0]
    t = POOL_T
    hb = t // HEAD
    last = s // HEAD - 1

    def body(u_ref, z_ref, halo_ref, dy_ref, zn_ref, dyn_ref, pw_ref, ps_ref, band_ref, aband_ref,
             du_ref, dz_ref, gpw_ref, gps_ref):
        i = pl.program_id(0)

        @pl.when(i == 0)
        def _():
            gpw_ref[...] = jnp.zeros_like(gpw_ref)
            gps_ref[...] = jnp.zeros_like(gps_ref)

        live = (i > 0).astype(F32)
        more = (i < pl.num_programs(0) - 1).astype(F32)
        groups = lambda ref: [ref[:, sl] for sl in HEAD_COLS]
        z, ps, dy = groups(z_ref), groups(ps_ref), groups(dy_ref)
        pw = [pw_ref[g] for g in HEADS]
        mix, mixed, sg, cnt = _pool_mix(groups(u_ref), [h * live for h in groups(halo_ref)], z, pw,
                                        [band_ref[g] for g in HEADS], i * t)
        sz = _each(lambda a, b: a * b, z, sg)
        for sl, d, m, p, s_, zg in zip(HEAD_COLS, dy, mixed, ps, sg, z):
            dz_ref[:, sl] = (d * m * p * (s_ * (1.0 + zg * (1.0 - s_)))).astype(BF16)
        for sl, d, m, a in zip(HEAD_COLS, dy, mixed, sz):
            gps_ref[:, sl] += _colsum(d * m * a)
        dmixed = _each(lambda d, p, a: d * p * a, dy, ps, sz)
        for g, gp in enumerate(_each(_bdot_tn, mix, dmixed)):
            gpw_ref[g] += gp
        dmix = _each(_bdot_nt, dmixed, pw)
        dmix_n = _each(lambda d, p, zn, w_: _bdot_nt(d * more * p * (zn * _sigmoid(zn)), w_),
                       groups(dyn_ref), ps, groups(zn_ref), pw)
        scaled = [jnp.concatenate([a / c, b * (1.0 / w)], axis=0) for a, c, b, w in zip(dmix, cnt, dmix_n, WINDOWS)]
        du = _each(lambda b, s_, d: _mask_dot(b, s_) - d, [aband_ref[g] for g in HEADS], scaled, dmix)
        for sl, v in zip(HEAD_COLS, du):
            du_ref[:, sl] = v.astype(BF16)

    tile = lambda col: pl.BlockSpec((t, D_HALF), lambda i: (i, col))
    below = lambda col: pl.BlockSpec((HEAD, D_HALF), lambda i: (jnp.minimum((i + 1) * hb, last), col))
    return _call(
        body, name="pool_bwd", grid=(s // t,),
        in_specs=[tile(0), tile(1), pl.BlockSpec((HEAD, D_HALF), lambda i: (jnp.maximum(i * hb - 1, 0), 0)),
                  tile(0), below(1), below(0),
                  pl.BlockSpec((N_HEADS, HEAD, HEAD), lambda i: (0, 0, 0)), pl.BlockSpec((1, D_HALF), lambda i: (0, 0)),
                  pl.BlockSpec((N_HEADS, t, HEAD + t), lambda i: (0, 0, 0)),
                  pl.BlockSpec((N_HEADS, t, HEAD + t), lambda i: (0, 0, 0))],
        out_specs=[tile(0), tile(0), pl.BlockSpec((N_HEADS, HEAD, HEAD), lambda i: (0, 0, 0)),
                   pl.BlockSpec((1, D_HALF), lambda i: (0, 0))],
        out_shape=[_sds((s, D_HALF), BF16), _sds((s, D_HALF), BF16), _sds((N_HEADS, HEAD, HEAD)), _sds((1, D_HALF))],
        compiler_params=_params("arbitrary"),
    )(proj, proj, proj, dyp, proj, dyp, pool_w, pool_scale, _pool_bands(t), _pool_bands(t, anti=True))


def _scan_bwd(do, vn, qd, kd, w, att, cd, st):
    s = do.shape[0]
    n_steps = s // SCAN_ROWS

    def body(do_ref, vn_ref, qd_ref, kd_ref, w_ref, att_ref, cd_ref, st_ref,
             du_ref, dw_ref, datt_ref, dqd_ref, dkd_ref, dcd_ref, dstate):
        @pl.when(pl.program_id(0) == 0)
        def _():
            dstate[...] = jnp.zeros_like(dstate)
        _, incl, _, _ = _pair_masks()
        cols = list(enumerate(HEAD_COLS))
        dv_intra = []
        for pp in range(SCAN_PAIRS):
            rp = slice(pp * PAIR, (pp + 1) * PAIR)
            dv_intra.append([_bdot_tn(att_ref[rp, sl], do_ref[rp, sl]) for _, sl in cols])
            for _, sl in cols:
                datt_ref[rp, sl] = jnp.where(incl, _bdot_nt(do_ref[rp, sl], vn_ref[rp, sl]), 0.0)
        ds = [dstate[h] for h in HEADS]
        for ci in range(2 * SCAN_PAIRS - 1, -1, -1):
            rs = slice(ci * CHUNK, (ci + 1) * CHUNK)
            in_pair = slice((ci % 2) * CHUNK, (ci % 2 + 1) * CHUNK)
            sm = [st_ref[ci, h] for h in HEADS]
            dvn = [dv_intra[ci // 2][h][in_pair] + _bdot(kd_ref[rs, sl], ds[h]) for h, sl in cols]
            for h, sl in cols:
                du_ref[rs, sl] = dvn[h].astype(BF16)
            dqd = [_bdot_nt(do_ref[rs, sl], sm[h]) for h, sl in cols]
            dw = [-_bdot_nt(dvn[h], sm[h]) for h, _ in cols]
            dkd = [_bdot_nt(vn_ref[rs, sl], ds[h]) for h, sl in cols]
            dcd = [jnp.broadcast_to(_rowsum(_colsum(ds[h] * sm[h])), (8, HEAD)) for h in HEADS]
            for h, sl in cols:
                dqd_ref[rs, sl] = dqd[h]
                dw_ref[rs, sl] = dw[h].astype(BF16)
                dkd_ref[rs, sl] = dkd[h]
                dcd_ref[ci * 8:(ci + 1) * 8, sl] = dcd[h]
            ds = [ds[h] * cd_ref[ci * 8:ci * 8 + 1, sl] + _bdot_tn(qd_ref[rs, sl], do_ref[rs, sl])
                  - _bdot_tn(w_ref[rs, sl], dvn[h]) for h, sl in cols]
        for h in HEADS:
            dstate[h] = ds[h]

    rev = pl.BlockSpec((SCAN_ROWS, D_HALF), lambda i: (n_steps - 1 - i, 0))
    rev_scalar = _chunk_scalar_spec(SCAN_PAIRS, lambda i: (n_steps - 1 - i, 0))
    return _call(
        body, name="scan_bwd", grid=(n_steps,),
        in_specs=[rev] * 6 + [rev_scalar,
                              pl.BlockSpec((2 * SCAN_PAIRS, N_HEADS, HEAD, HEAD), lambda i: (n_steps - 1 - i, 0, 0, 0))],
        out_specs=[rev] * 5 + [rev_scalar],
        out_shape=[_sds((s, D_HALF), BF16)] * 2 + [_sds((s, D_HALF))] * 3 + [_sds((s // 8, D_HALF))],
        scratch_shapes=[pltpu.VMEM((N_HEADS, HEAD, HEAD), F32)],
        compiler_params=_params("arbitrary"),
    )(do, vn, qd, kd, w, att, cd, st)


def _intra_bwd(qn, kn, vs, beta, g, tm, du, dw, datt, dqd, dkd, dcd):
    s = qn.shape[0]

    def body(qn_ref, kn_ref, vs_ref, beta_ref, g_ref, t_ref, du_ref, dw_ref, datt_ref, dqd_ref, dkd_ref, dcd_ref,
             dqn_ref, dkn_ref, dvs_ref, dbeta_ref, dg_ref):
        ones = jnp.ones((PAIR, HEAD), BF16)
        tn = (((0,), (0,)), ((), ()))
        kn, vs, beta = _heads(kn_ref), _heads(vs_ref), _heads(beta_ref)
        cm = _pair_common(_heads(qn_ref), kn, vs, beta, _heads(g_ref))
        tmv, duv, dwv, dattv, dqdv, dkdv = (_heads(r) for r in (t_ref, du_ref, dw_ref, datt_ref, dqd_ref, dkd_ref))
        dvb = _each(_bdot_tn, tmv, duv)
        dt = _each(lambda a, b, c, d: _bdot_nt(a, b) + _bdot_nt(c, d), duv, cm["vb"], dwv, cm["kbg"])
        dkbg = _each(_bdot_tn, tmv, dwv)
        m1 = _each(_bdot_tn, tmv, dt)
        da = _each(lambda a, b: -jnp.where(cm["strict"], _bdot_nt(a, b), 0.0), m1, tmv)
        dkk = _each(lambda a, b: a * b, da, cm["decay"])
        dqk = _each(lambda a, b: a * b, dattv, cm["decay"])
        dd = _each(lambda a, b, c, d: a * b + c * d, dkk, cm["kk"], dqk, cm["qk"])
        dkb = _each(lambda a, b, c, d: _bdot(a, b) + c * d, dkk, kn, dkbg, cm["egc"])
        dq = _each(lambda a, b, c, d: _bdot(a, b) + c * d, dqk, kn, dqdv, cm["egc"])
        dkn = _each(lambda a, b, c, d: _bdot_tn(a, b) + _bdot_tn(c, d), dkk, cm["kb"], dqk, cm["q"])
        dkn = _each(lambda a, b, c, d, e: a + b * c + d * e, dkn, dkdv, cm["ekd"], dkb, beta)
        t_kd = _each(lambda a, b, c: _rowsum(a * b * c), dkdv, kn, cm["ekd"])
        split = _each(_split, dd)
        rows_dd = [jnp.dot(hi, ones, preferred_element_type=F32) + jnp.dot(lo, ones, preferred_element_type=F32)
                   for hi, lo in split]
        cols_dd = [lax.dot_general(hi, ones, tn, preferred_element_type=F32)
                   + lax.dot_general(lo, ones, tn, preferred_element_type=F32) for hi, lo in split]
        dgc = _each(lambda r, c, a, b, e, f, k, t: r - c + _rowsum(a * b * e) + _rowsum(f * k) - t,
                    rows_dd, cols_dd, dqdv, cm["q"], cm["egc"], dkbg, cm["kbg"], t_kd)
        same_b = cm["same"].astype(BF16)
        rowi = lax.broadcasted_iota(I32, (PAIR, HEAD), 0)
        dcd = _each(lambda d: jnp.where(rowi < CHUNK, d[0:1], d[8:9]), _heads(dcd_ref, rows=16))
        dgl = _each(lambda t, d, c: _mask_dot(same_b, jnp.broadcast_to(t, (PAIR, HEAD))) + d * c, t_kd, dcd, cm["cd"])
        is_last = jnp.bitwise_and(rowi, CHUNK - 1) == CHUNK - 1
        dgc = _each(lambda a, b: a + jnp.where(is_last, b, 0.0), dgc, dgl)
        r = lax.broadcasted_iota(I32, (PAIR, PAIR), 0)
        c = lax.broadcasted_iota(I32, (PAIR, PAIR), 1)
        upper_b = (cm["same"] & (r <= c)).astype(BF16)
        _put_heads(dg_ref, _each(lambda v: _mask_dot(upper_b, v), dgc))
        _put_heads(dbeta_ref, _each(lambda a, b, c, d: jnp.broadcast_to(_rowsum(a * b) + _rowsum(c * d), (PAIR, HEAD)),
                                    dkb, kn, dvb, vs))
        _put_heads(dqn_ref, _each(lambda v: v * QK_SCALE, dq))
        _put_heads(dkn_ref, dkn)
        _put_heads(dvs_ref, _each(lambda a, b: a * b, dvb, beta))

    return _call(
        body, name="intra_bwd", grid=(s // (INTRA_PAIRS * PAIR),),
        in_specs=[_pair_spec()] * 11 + [_chunk_scalar_spec(INTRA_PAIRS)], out_specs=[_pair_spec()] * 5,
        out_shape=[_sds((s, D_HALF))] * 5,
        compiler_params=_params("arbitrary"),
    )(qn, kn, vs, beta, g, tm, du, dw, datt, dqd, dkd, dcd)


def _delta_bwd(do, vn, qd, kd, w, att, cd, st, qn, kn, vs, beta, g, tm):
    s = do.shape[0]
    n_steps = s // SCAN_ROWS
    assert INTRA_PAIRS == SCAN_PAIRS

    def body(do_ref, vn_ref, qd_ref, kd_ref, w_ref, att_ref, cd_ref, st_ref, qn_ref, kn_ref, vs_ref, beta_ref, g_ref, t_ref,
             dqn_ref, dkn_ref, dvs_ref, dbeta_ref, dg_ref, dstate, du_s, dw_s, datt_s, dqd_s, dkd_s, dcd_s):
        t = pl.program_id(0)

        @pl.when(t == 0)
        def _():
            dstate[...] = jnp.zeros_like(dstate)
            for ref in (du_s, dw_s, datt_s, dqd_s, dkd_s, dcd_s):
                ref[1] = jnp.zeros(ref.shape[1:], ref.dtype)

        cur = lax.rem(t, 2)
        prev = 1 - cur
        cols = list(enumerate(HEAD_COLS))
        _, incl, _, _ = _pair_masks()

        def recurrence():
            dv_intra = []
            for pp in range(SCAN_PAIRS):
                rp = slice(pp * PAIR, (pp + 1) * PAIR)
                dv_intra.append([_bdot_tn(att_ref[rp, sl], do_ref[rp, sl]) for _, sl in cols])
                for _, sl in cols:
                    datt_s[cur, rp, sl] = jnp.where(incl, _bdot_nt(do_ref[rp, sl], vn_ref[rp, sl]), 0.0)
                yield
            ds = [dstate[h] for h in HEADS]
            for ci in range(2 * SCAN_PAIRS - 1, -1, -1):
                rs = slice(ci * CHUNK, (ci + 1) * CHUNK)
                in_pair = slice((ci % 2) * CHUNK, (ci % 2 + 1) * CHUNK)
                sm = [st_ref[ci, h] for h in HEADS]
                dvn = [dv_intra[ci // 2][h][in_pair] + _bdot(kd_ref[rs, sl], ds[h]) for h, sl in cols]
                dqd = [_bdot_nt(do_ref[rs, sl], sm[h]) for h, sl in cols]
                dkd = [_bdot_nt(vn_ref[rs, sl], ds[h]) for h, sl in cols]
                dcd = [jnp.broadcast_to(_rowsum(_colsum(ds[h] * sm[h])), (8, HEAD)) for h in HEADS]
                yield
                dw = [-_bdot_nt(dvn[h], sm[h]) for h, _ in cols]
                for h, sl in cols:
                    du_s[cur, rs, sl] = dvn[h].astype(BF16)
                    dqd_s[cur, rs, sl] = dqd[h]
                    dw_s[cur, rs, sl] = dw[h].astype(BF16)
                    dkd_s[cur, rs, sl] = dkd[h]
                    dcd_s[cur, ci * 8:(ci + 1) * 8, sl] = dcd[h]
                ds = [ds[h] * cd_ref[ci * 8:ci * 8 + 1, sl] + _bdot_tn(qd_ref[rs, sl], do_ref[rs, sl])
                      - _bdot_tn(w_ref[rs, sl], dvn[h]) for h, sl in cols]
                yield
            for h in HEADS:
                dstate[h] = ds[h]

        def factors():
            ones = jnp.ones((PAIR, HEAD), BF16)
            tn = (((0,), (0,)), ((), ()))
            kept = lambda ref, rows=PAIR: [ref[prev, pp * rows:(pp + 1) * rows, HEAD_COLS[h]] for pp, h in UNITS]
            kn, vs, beta = _heads(kn_ref), _heads(vs_ref), _heads(beta_ref)
            cm = {}
            yield from _pair_common_stages(cm, _heads(qn_ref), kn, vs, beta, _heads(g_ref))
            tmv = _heads(t_ref)
            duv, dwv, dattv, dqdv, dkdv = kept(du_s), kept(dw_s), kept(datt_s), kept(dqd_s), kept(dkd_s)
            dvb = _each(_bdot_tn, tmv, duv)
            dt = _each(lambda a, b, c, d: _bdot_nt(a, b) + _bdot_nt(c, d), duv, cm["vb"], dwv, cm["kbg"])
            dkbg = _each(_bdot_tn, tmv, dwv)
            yield
            m1 = _each(_bdot_tn, tmv, dt)
            yield
            da = _each(lambda a, b: -jnp.where(cm["strict"], _bdot_nt(a, b), 0.0), m1, tmv)
            yield
            dkk = _each(lambda a, b: a * b, da, cm["decay"])
            dqk = _each(lambda a, b: a * b, dattv, cm["decay"])
            dd = _each(lambda a, b, c, d: a * b + c * d, dkk, cm["kk"], dqk, cm["qk"])
            dkb = _each(lambda a, b, c, d: _bdot(a, b) + c * d, dkk, kn, dkbg, cm["egc"])
            dq = _each(lambda a, b, c, d: _bdot(a, b) + c * d, dqk, kn, dqdv, cm["egc"])
            yield
            dkn = _each(lambda a, b, c, d: _bdot_tn(a, b) + _bdot_tn(c, d), dkk, cm["kb"], dqk, cm["q"])
            dkn = _each(lambda a, b, c, d, e: a + b * c + d * e, dkn, dkdv, cm["ekd"], dkb, beta)
            t_kd = _each(lambda a, b, c: _rowsum(a * b * c), dkdv, kn, cm["ekd"])
            yield
            split = _each(_split, dd)
            rows_dd = [jnp.dot(hi, ones, preferred_element_type=F32) + jnp.dot(lo, ones, preferred_element_type=F32)
                       for hi, lo in split]
            cols_dd = [lax.dot_general(hi, ones, tn, preferred_element_type=F32)
                       + lax.dot_general(lo, ones, tn, preferred_element_type=F32) for hi, lo in split]
            yield
            dgc = _each(lambda r, c, a, b, e, f, k, tk: r - c + _rowsum(a * b * e) + _rowsum(f * k) - tk,
                        rows_dd, cols_dd, dqdv, cm["q"], cm["egc"], dkbg, cm["kbg"], t_kd)
            same_b = cm["same"].astype(BF16)
            rowi = lax.broadcasted_iota(I32, (PAIR, HEAD), 0)
            dcd = _each(lambda d: jnp.where(rowi < CHUNK, d[0:1], d[8:9]), kept(dcd_s, rows=16))
            dgl = _each(lambda tk, d, c: _mask_dot(same_b, jnp.broadcast_to(tk, (PAIR, HEAD))) + d * c, t_kd, dcd, cm["cd"])
            yield
            is_last = jnp.bitwise_and(rowi, CHUNK - 1) == CHUNK - 1
            dgc = _each(lambda a, b: a + jnp.where(is_last, b, 0.0), dgc, dgl)
            r = lax.broadcasted_iota(I32, (PAIR, PAIR), 0)
            c = lax.broadcasted_iota(I32, (PAIR, PAIR), 1)
            upper_b = (cm["same"] & (r <= c)).astype(BF16)
            _put_heads(dg_ref, _each(lambda v: _mask_dot(upper_b, v), dgc))
            yield
            _put_heads(dbeta_ref, _each(lambda a, b, c, d: jnp.broadcast_to(_rowsum(a * b) + _rowsum(c * d), (PAIR, HEAD)),
                                        dkb, kn, dvb, vs))
            _put_heads(dqn_ref, _each(lambda v: v * QK_SCALE, dq))
            _put_heads(dkn_ref, dkn)
            _put_heads(dvs_ref, _each(lambda a, b: a * b, dvb, beta))
            yield

        _interleave(recurrence(), factors())

    last = n_steps - 1
    now = lambda i: (jnp.maximum(last - i, 0), 0)
    after = lambda i: (jnp.minimum(n_steps - i, last), 0)
    rows = lambda index: pl.BlockSpec((SCAN_ROWS, D_HALF), index)
    slot = lambda r, dtype: pltpu.VMEM((2, r, D_HALF), dtype)
    return _call(
        body, name="delta_bwd", grid=(n_steps + 1,),
        in_specs=[rows(now)] * 6 + [_chunk_scalar_spec(SCAN_PAIRS, now),
                                    pl.BlockSpec((2 * SCAN_PAIRS, N_HEADS, HEAD, HEAD), lambda i: (jnp.maximum(last - i, 0), 0, 0, 0))]
                 + [rows(after)] * 6,
        out_specs=[rows(after)] * 5,
        out_shape=[_sds((s, D_HALF))] * 5,
        scratch_shapes=[pltpu.VMEM((N_HEADS, HEAD, HEAD), F32), slot(SCAN_ROWS, BF16), slot(SCAN_ROWS, BF16),
                        slot(SCAN_ROWS, F32), slot(SCAN_ROWS, F32), slot(SCAN_ROWS, F32), slot(16 * SCAN_PAIRS, F32)],
        compiler_params=_params("arbitrary"),
    )(do, vn, qd, kd, w, att, cd, st, qn, kn, vs, beta, g, tm)


def _rows8(x):
    acc = x[0:8]
    for r in range(8, x.shape[0], 8):
        acc = acc + x[r:r + 8]
    return acc


def _conv_bwd(proj, conv_w, a_log, dt_bias, dqn, dkn, dvs, dbeta, dg):
    s = proj.shape[0]
    t = CONV_T
    n_tiles = s // t
    n_sub = t // CONV_SUB
    tile_of = lambda i: n_tiles - 1 - i

    def body(q_ref, k_ref, v_ref, hq_ref, hk_ref, hv_ref, ba_ref, cw_ref, al_ref, dtb_ref,
             dqn_ref, dkn_ref, dvs_ref, dbeta_ref, dg_ref, oq_ref, ok_ref, ov_ref, dba_ref, gcw_out, gsm_out,
             below, gcw_ref, gsm_ref):
        @pl.when(pl.program_id(0) == 0)
        def _():
            gcw_ref[...] = jnp.zeros_like(gcw_ref)
            gsm_ref[...] = jnp.zeros_like(gsm_ref)
            below[...] = jnp.zeros_like(below)

        live = (pl.program_id(0) < n_tiles - 1).astype(F32)
        parts = ((q_ref, hq_ref, dqn_ref, oq_ref), (k_ref, hk_ref, dkn_ref, ok_ref), (v_ref, hv_ref, dvs_ref, ov_ref))
        lane = lax.broadcasted_iota(I32, (CONV_SUB, HEAD), 1)
        lane8 = lax.broadcasted_iota(I32, (8, HEAD), 1)

        def sub_tile(r0, first):
            rows = pl.ds(r0, CONV_SUB)
            for p, (x_ref, h_ref, d_ref, o_ref) in enumerate(parts):
                for h in HEADS:
                    cs = HEAD_COLS[h]
                    wide = slice(p * D_HALF + h * HEAD, p * D_HALF + (h + 1) * HEAD)
                    cw = cw_ref[:, wide]
                    prev8 = h_ref[:, cs] * live if first else x_ref[pl.ds(r0 - 8, 8), cs]
                    taps = _conv_taps(x_ref[rows, cs], prev8)
                    y = _conv_pre(taps, cw)
                    sg = _sigmoid(y)
                    sv = y * sg
                    ds = d_ref[rows, cs]
                    if p < 2:
                        rn = lax.rsqrt(_rowsum(sv * sv) + EPS)
                        nrm = sv * rn
                        ds = rn * (ds - nrm * _rowsum(ds * nrm))
                    dy = ds * (sg * (1.0 + y * (1.0 - sg)))
                    for j in range(CONV_K):
                        gcw_ref[8 * j:8 * j + 8, wide] += _rows8(dy * taps[j])
                    nxt = below[:, wide]
                    acc = dy * cw[CONV_K - 1:CONV_K]
                    for sft in range(1, CONV_K):
                        acc = acc + _shift_up(dy, nxt, sft) * cw[CONV_K - 1 - sft:CONV_K - sft]
                    o_ref[rows, cs] = acc.astype(BF16)
                    below[:, wide] = dy[0:8]

            ba = ba_ref[rows, :]
            dba = jnp.zeros((CONV_SUB, HEAD), F32)
            gsm = jnp.zeros((8, HEAD), F32)
            for h in HEADS:
                beta = _sigmoid(ba[:, h:h + 1])
                dbeta = dbeta_ref[rows, h * HEAD:h * HEAD + 1]
                xg = ba[:, N_HEADS + h:N_HEADS + h + 1] + dtb_ref[0:1, h:h + 1]
                nexp = -jnp.exp(al_ref[0:1, h:h + 1])
                dgv = dg_ref[rows, h * HEAD:h * HEAD + 1]
                da = dgv * nexp * _sigmoid(xg)
                dba = dba + jnp.where(lane == h, dbeta * beta * (1.0 - beta), 0.0) + jnp.where(lane == N_HEADS + h, da, 0.0)
                gsm = (gsm + jnp.where(lane8 == h, _rows8(dgv * nexp * _softplus(xg)), 0.0)
                       + jnp.where(lane8 == N_HEADS + h, _rows8(da), 0.0))
            dba_ref[rows, :] = jnp.zeros((CONV_SUB, D_HALF), BF16)
            dba_ref[rows, :HEAD] = dba.astype(BF16)
            gsm_ref[...] += gsm

        def step(k, carry):
            sub_tile(pl.multiple_of((n_sub - 1 - k) * CONV_SUB, CONV_SUB), False)
            return carry

        lax.fori_loop(0, n_sub - 1, step, 0)
        sub_tile(0, True)

        @pl.when(pl.program_id(0) == n_tiles - 1)
        def _():
            gcw_out[...] = jnp.zeros_like(gcw_out)
            for j in range(CONV_K):
                gcw_out[j:j + 1, :] = _colsum(gcw_ref[8 * j:8 * j + 8, :])
            gsm_out[...] = jnp.broadcast_to(_colsum(gsm_ref[...]), (8, HEAD))

    row = pl.BlockSpec((t, D_HALF), lambda i: (tile_of(i), 0))
    const = lambda shape: pl.BlockSpec(shape, lambda i: (0, 0))
    return _call(
        body, name="conv_bwd", grid=(n_tiles,),
        in_specs=_conv_specs(t, tile_of) + [pl.BlockSpec((t, HEAD), lambda i: (tile_of(i), COL_BA // HEAD)),
                                            const((CONV_K, 3 * D_HALF)), const((1, N_HEADS)), const((1, N_HEADS))] + [row] * 5,
        out_specs=[row, row, row, row, const((8, 3 * D_HALF)), const((8, HEAD))],
        out_shape=[_sds((s, D_HALF), BF16)] * 4 + [_sds((8, 3 * D_HALF)), _sds((8, HEAD))],
        scratch_shapes=[pltpu.VMEM((8, 3 * D_HALF), F32), pltpu.VMEM((8 * CONV_K, 3 * D_HALF), F32),
                        pltpu.VMEM((8, HEAD), F32)],
        compiler_params=_params("arbitrary"),
    )(proj, proj, proj, proj, proj, proj, proj, conv_w, a_log, dt_bias, dqn, dkn, dvs, dbeta, dg)


def _conv_bwd_pre(proj, conv_w, a_log, dt_bias, dqn, dkn, dvs, dbeta, dg):
    s = proj.shape[0]
    t = CONV_T

    def body(q_ref, k_ref, v_ref, hq_ref, hk_ref, hv_ref, ba_ref, cw_ref, al_ref, dtb_ref,
             dqn_ref, dkn_ref, dvs_ref, dbeta_ref, dg_ref, dyq_ref, dyk_ref, dyv_ref, dba_ref, gcw_ref, gsm_ref):
        @pl.when(pl.program_id(0) == 0)
        def _():
            gcw_ref[...] = jnp.zeros_like(gcw_ref)
            gsm_ref[...] = jnp.zeros_like(gsm_ref)

        live = (pl.program_id(0) > 0).astype(F32)
        parts = ((q_ref, hq_ref, dqn_ref, dyq_ref), (k_ref, hk_ref, dkn_ref, dyk_ref), (v_ref, hv_ref, dvs_ref, dyv_ref))
        for p, (x_ref, h_ref, d_ref, dy_ref) in enumerate(parts):
            cols = slice(p * D_HALF, (p + 1) * D_HALF)
            taps = _conv_taps(x_ref[...], h_ref[...] * live)
            y = _conv_pre(taps, cw_ref[:, cols])
            sg = _sigmoid(y)
            sv = y * sg
            if p == 2:
                ds = d_ref[...]
            else:
                segs = []
                for h in HEADS:
                    seg = _head(sv, h)
                    rn = lax.rsqrt(_rowsum(seg * seg) + EPS)
                    nrm = seg * rn
                    dn = d_ref[:, HEAD_COLS[h]]
                    segs.append(rn * (dn - nrm * _rowsum(dn * nrm)))
                ds = jnp.concatenate(segs, axis=1)
            dy = ds * (sg * (1.0 + y * (1.0 - sg)))
            dy_ref[...] = dy
            for j in range(CONV_K):
                gcw_ref[j:j + 1, cols] += _colsum(dy * taps[j])

        ba = ba_ref[...]
        lane = lax.broadcasted_iota(I32, (t, HEAD), 1)
        lane1 = lax.broadcasted_iota(I32, (1, HEAD), 1)
        dba = jnp.zeros((t, HEAD), F32)
        gsm = jnp.zeros((1, HEAD), F32)
        for h in HEADS:
            beta = _sigmoid(ba[:, h:h + 1])
            dbeta = dbeta_ref[:, h * HEAD:h * HEAD + 1]
            xg = ba[:, N_HEADS + h:N_HEADS + h + 1] + dtb_ref[0:1, h:h + 1]
            nexp = -jnp.exp(al_ref[0:1, h:h + 1])
            dgv = dg_ref[:, h * HEAD:h * HEAD + 1]
            da = dgv * nexp * _sigmoid(xg)
            dba = dba + jnp.where(lane == h, dbeta * beta * (1.0 - beta), 0.0) + jnp.where(lane == N_HEADS + h, da, 0.0)
            gsm = (gsm + jnp.where(lane1 == h, _colsum(dgv * nexp * _softplus(xg)), 0.0)
                   + jnp.where(lane1 == N_HEADS + h, _colsum(da), 0.0))
        dba_ref[...] = jnp.zeros_like(dba_ref)
        dba_ref[:, :HEAD] = dba.astype(BF16)
        gsm_ref[0:1, :] += gsm

    row = pl.BlockSpec((t, D_HALF), lambda i: (i, 0))
    return _call(
        body, name="conv_bwd_pre", grid=(s // t,),
        in_specs=_conv_specs(t) + [pl.BlockSpec((t, HEAD), lambda i: (i, COL_BA // HEAD)),
                                   pl.BlockSpec((CONV_K, 3 * D_HALF), lambda i: (0, 0)),
                                   pl.BlockSpec((1, N_HEADS), lambda i: (0, 0)),
                                   pl.BlockSpec((1, N_HEADS), lambda i: (0, 0))] + [row] * 5,
        out_specs=[row, row, row, row,
                   pl.BlockSpec((8, 3 * D_HALF), lambda i: (0, 0)), pl.BlockSpec((8, HEAD), lambda i: (0, 0))],
        out_shape=[_sds((s, D_HALF))] * 3 + [_sds((s, D_HALF), BF16), _sds((8, 3 * D_HALF)), _sds((8, HEAD))],
        compiler_params=_params("arbitrary"),
    )(proj, proj, proj, proj, proj, proj, proj, conv_w, a_log, dt_bias, dqn, dkn, dvs, dbeta, dg)


def _conv_bwd_in(dyq, dyk, dyv, conv_w):
    s = dyq.shape[0]
    t = CONV_T
    last = s // 8 - 1

    def body(q_ref, k_ref, v_ref, nq_ref, nk_ref, nv_ref, cw_ref, oq_ref, ok_ref, ov_ref):
        more = (pl.program_id(0) < pl.num_programs(0) - 1).astype(F32)
        for p, (d_ref, n_ref, o_ref) in enumerate(((q_ref, nq_ref, oq_ref), (k_ref, nk_ref, ok_ref), (v_ref, nv_ref, ov_ref))):
            cw = cw_ref[:, p * D_HALF:(p + 1) * D_HALF]
            dy = d_ref[...]
            nxt = n_ref[...] * more
            acc = dy * cw[3:4]
            for sft in (1, 2, 3):
                acc = acc + _shift_up(dy, nxt, sft) * cw[3 - sft:4 - sft]
            o_ref[...] = acc.astype(BF16)

    row = pl.BlockSpec((t, D_HALF), lambda i: (i, 0))
    nxt = pl.BlockSpec((8, D_HALF), lambda i: (jnp.minimum((i + 1) * (t // 8), last), 0))
    return _call(
        body, name="conv_bwd_in", grid=(s // t,),
        in_specs=[row] * 3 + [nxt] * 3 + [pl.BlockSpec((CONV_K, 3 * D_HALF), lambda i: (0, 0))],
        out_specs=[row] * 3, out_shape=[_sds((s, D_HALF), BF16)] * 3,
        compiler_params=_params("arbitrary"),
    )(dyq, dyk, dyv, dyq, dyk, dyv, conv_w)


IN_T = 512


def _in_bwd(x, dh, norm_w, w_pad, pieces):
    s = x.shape[0]
    t = IN_T
    widths = [D_HALF] * 6 + [N_IN_PAD - COL_BA]

    def body(*refs):
        x_ref, dh_ref, nw_ref, w_ref = refs[:4]
        p_refs = refs[4:4 + len(pieces)]
        gx_ref, gnw_ref = refs[4 + len(pieces):]

        @pl.when(pl.program_id(0) == 0)
        def _():
            gnw_ref[...] = jnp.zeros_like(gnw_ref)

        dn = jnp.zeros((t, D_MODEL), F32)
        col = 0
        for p_ref, wd in zip(p_refs, widths):
            dn = dn + _bdot_nt(p_ref[...], w_ref[:, col:col + wd])
            col += wd
        xv = x_ref[...]
        r = lax.rsqrt(jnp.mean(xv * xv, axis=-1, keepdims=True) + EPS)
        xhat = xv * r
        gnw_ref[...] += _colsum(dn * xhat)
        dxh = dn * nw_ref[...]
        gx_ref[...] = dh_ref[...] + r * (dxh - xhat * jnp.mean(dxh * xhat, axis=-1, keepdims=True))

    wide = pl.BlockSpec((t, D_MODEL), lambda i: (i, 0))
    return _call(
        body, name="in_bwd", grid=(s // t,),
        in_specs=[wide, wide, pl.BlockSpec((1, D_MODEL), lambda i: (0, 0)),
                  pl.BlockSpec((D_MODEL, N_IN_PAD), lambda i: (0, 0))]
                 + [pl.BlockSpec((t, wd), lambda i: (i, 0)) for wd in widths],
        out_specs=[wide, pl.BlockSpec((1, D_MODEL), lambda i: (0, 0))],
        out_shape=[_sds((s, D_MODEL)), _sds((1, D_MODEL))],
        compiler_params=_params("arbitrary"),
    )(x, dh, norm_w, w_pad, *pieces)


def _adamw_shard(name, w, g_own, g_got, cidx, m, v):
    _, r, c = w.shape
    half = r // 2
    rows = 256 if half % 256 == 0 else half
    per_half = half // rows

    def body(c_ref, w_ref, go_ref, gg_ref, m_ref, v_ref, gout_ref, d_ref, nm_ref, nv_ref):
        mine = (pl.program_id(0) // per_half) == c_ref[0]
        gv = jnp.where(mine, go_ref[:, :c], gg_ref[:, :c])
        gout_ref[0] = gv
        mn = ADAM_B1 * m_ref[0] + (1.0 - ADAM_B1) * gv
        vn = ADAM_B2 * v_ref[0] + (1.0 - ADAM_B2) * (gv * gv)
        m_hat = mn / (1.0 - ADAM_B1 ** ADAM_STEP)
        v_hat = vn / (1.0 - ADAM_B2 ** ADAM_STEP)
        d_ref[0] = -ADAM_LR * (m_hat / (jnp.sqrt(v_hat) + ADAM_EPS) + ADAM_WD * w_ref[0])
        nm_ref[0] = mn
        nv_ref[0] = vn

    blk = pl.BlockSpec((1, rows, c), lambda i, c_ref: (0, i, 0))
    gblk = pl.BlockSpec((rows, g_own.shape[1]), lambda i, c_ref: (i % per_half, 0))
    return _call(
        body, name=name,
        grid_spec=pltpu.PrefetchScalarGridSpec(
            num_scalar_prefetch=1, grid=(2 * per_half,),
            in_specs=[blk, gblk, gblk, blk, blk], out_specs=[blk] * 4),
        out_shape=[_sds((1, r, c))] * 4,
        compiler_params=_params("arbitrary"),
    )(cidx, w, g_own, g_got, m, v)


def _adamw_tiles(name, w, g, m, v):
    n = w.shape[0]
    nb = 77 if n % 77 == 0 else n

    def body(w_ref, g_ref, m_ref, v_ref, d_ref, nm_ref, nv_ref):
        gv = g_ref[...]
        mn = ADAM_B1 * m_ref[...] + (1.0 - ADAM_B1) * gv
        vn = ADAM_B2 * v_ref[...] + (1.0 - ADAM_B2) * (gv * gv)
        m_hat = mn / (1.0 - ADAM_B1 ** ADAM_STEP)
        v_hat = vn / (1.0 - ADAM_B2 ** ADAM_STEP)
        d_ref[...] = -ADAM_LR * (m_hat / (jnp.sqrt(v_hat) + ADAM_EPS) + ADAM_WD * w_ref[...])
        nm_ref[...] = mn
        nv_ref[...] = vn

    blk = pl.BlockSpec((nb, 8, HEAD), lambda i: (i, 0, 0))
    return _call(
        body, name=name, grid=(n // nb,),
        in_specs=[blk] * 4, out_specs=[blk] * 3, out_shape=[_sds(w.shape)] * 3,
        compiler_params=_params("arbitrary"),
    )(w, g, m, v)


def _exchange(name, inputs, out_shapes, phases):
    n_in = len(inputs)
    n_out = len(out_shapes)
    n_cp = sum(len(p) for p in phases)

    def body(*refs):
        ins, outs = refs[:n_in], refs[n_in:n_in + n_out]
        send, recv = refs[n_in + n_out:]
        pos = (lax.axis_index("x"), lax.axis_index("y"), lax.axis_index("c"))
        k = 0
        for phase in phases:
            cps = []
            for src, dst, target in phase:
                cps.append(pltpu.make_async_remote_copy(
                    src_ref=src(ins, outs, pos), dst_ref=dst(ins, outs, pos), send_sem=send.at[k], recv_sem=recv.at[k],
                    device_id=target(pos), device_id_type=pl.DeviceIdType.MESH))
                k += 1
            for cp in cps:
                cp.start()
            for cp in cps:
                cp.wait()

    anyspec = pl.BlockSpec(memory_space=pl.ANY)
    return _call(
        body, name=name,
        in_specs=[anyspec] * n_in, out_specs=[anyspec] * n_out, out_shape=list(out_shapes),
        scratch_shapes=[pltpu.SemaphoreType.DMA((n_cp,)), pltpu.SemaphoreType.DMA((n_cp,))],
    )(*inputs)


def _exchange_start(name, inputs, out_shapes, copies):
    n_in, n_out, n_cp = len(inputs), len(out_shapes), len(copies)

    def body(*refs):
        ins, lands = refs[:n_in], refs[n_in:n_in + n_out]
        sems = refs[n_in + n_out:n_in + n_out + 2 * n_cp]
        token = refs[-1]
        pos = (lax.axis_index("x"), lax.axis_index("y"), lax.axis_index("c"))
        for k, (src, dst, target) in enumerate(copies):
            pltpu.make_async_remote_copy(
                src_ref=src(ins, lands, pos), dst_ref=dst(ins, lands, pos), send_sem=sems[2 * k], recv_sem=sems[2 * k + 1],
                device_id=target(pos), device_id_type=pl.DeviceIdType.MESH).start()
        token[...] = jnp.zeros_like(token)

    hbm = pl.BlockSpec(memory_space=pltpu.HBM)
    sem = pl.BlockSpec(memory_space=pltpu.SEMAPHORE)
    bufs = list(inputs) + [lax.empty(o.shape, o.dtype) for o in out_shapes]
    outs = _call(
        body, name=name,
        out_shape=tuple([pltpu.SemaphoreType.DMA(())] * (2 * n_cp) + [pltpu.HBM(b.shape, b.dtype) for b in bufs]
                        + [_sds((8, HEAD))]),
        in_specs=[hbm] * len(bufs),
        out_specs=tuple([sem] * (2 * n_cp) + [hbm] * len(bufs) + [pl.BlockSpec(memory_space=pltpu.VMEM)]),
        input_output_aliases={i: 2 * n_cp + i for i in range(len(bufs))},
        compiler_params=pltpu.CompilerParams(has_side_effects=pltpu.SideEffectType.DATAFLOW_SIDE_EFFECTING),
    )(*[pltpu.with_memory_space_constraint(b, pltpu.HBM) for b in bufs])
    return outs[:2 * n_cp], outs[2 * n_cp:2 * n_cp + n_in], outs[2 * n_cp + n_in:-1], outs[-1]


def _exchange_wait(name, sems, sources, lands, copies, after):
    n_in, n_out, n_cp = len(sources), len(lands), len(copies)

    def body(*refs):
        ins, zones = refs[:n_in], refs[n_in:n_in + n_out]
        sem_refs = refs[n_in + n_out:n_in + n_out + 2 * n_cp]
        pos = (lax.axis_index("x"), lax.axis_index("y"), lax.axis_index("c"))
        for k, (src, dst, target) in enumerate(copies):
            cp = pltpu.make_async_remote_copy(
                src_ref=src(ins, zones, pos), dst_ref=dst(ins, zones, pos), send_sem=sem_refs[2 * k],
                recv_sem=sem_refs[2 * k + 1], device_id=target(pos), device_id_type=pl.DeviceIdType.MESH)
            cp.wait_send()
            cp.wait_recv()

    hbm = pl.BlockSpec(memory_space=pltpu.HBM)
    sem = pl.BlockSpec(memory_space=pltpu.SEMAPHORE)
    bufs = list(sources) + list(lands)
    outs = _call(
        body, name=name,
        out_shape=tuple(pltpu.HBM(b.shape, b.dtype) for b in bufs),
        in_specs=[hbm] * len(bufs) + [sem] * (2 * n_cp) + [pl.BlockSpec(memory_space=pl.ANY)],
        out_specs=tuple([hbm] * len(bufs)),
        input_output_aliases={i: i for i in range(len(bufs))},
        compiler_params=pltpu.CompilerParams(has_side_effects=pltpu.SideEffectType.DATAFLOW_SIDE_EFFECTING),
    )(*bufs, *sems, after)
    return outs[:n_in], outs[n_in:]


def _allreduce_tile(name, v):
    def body(v_ref, out_ref, slots, send, recv):
        x, y, c = lax.axis_index("x"), lax.axis_index("y"), lax.axis_index("c")
        me = 4 * x + 2 * y + c
        slots[me] = v_ref[...]
        cps = []
        for k in range(1, 8):
            peer = (x ^ (k >> 2), y ^ ((k >> 1) & 1), c ^ (k & 1))
            cps.append(pltpu.make_async_remote_copy(
                src_ref=v_ref, dst_ref=slots.at[me], send_sem=send.at[k - 1], recv_sem=recv.at[k - 1],
                device_id=peer, device_id_type=pl.DeviceIdType.MESH))
        for cp in cps:
            cp.start()
        for cp in cps:
            cp.wait()
        acc = slots[0]
        for i in range(1, 8):
            acc = acc + slots[i]
        out_ref[...] = acc

    vm = pl.BlockSpec(memory_space=pltpu.VMEM)
    return _call(
        body, name=name, in_specs=[vm], out_specs=vm, out_shape=_sds(v.shape),
        scratch_shapes=[pltpu.VMEM((8,) + v.shape, F32), pltpu.SemaphoreType.DMA((7,)), pltpu.SemaphoreType.DMA((7,))],
    )(v)


def _chip(pos):
    return 2 * pos[0] + pos[1]


def _other_chip(pos, mask):
    x, y, c = pos
    return (x ^ (mask >> 1), y ^ (mask & 1), c)


def _sibling(pos):
    return (pos[0], pos[1], 1 - pos[2])


def _gather_weights(wb, cb):
    rows = wb.shape[0] // 2
    x_nb, y_nb, diag = CHIP_MASKS

    def part(pos, mask, quarter=None):
        start = pos[2] * rows if quarter is None else pos[2] * rows + quarter * (rows // 2)
        return lambda outs: outs[0].at[_chip(pos) ^ mask, pl.ds(start, rows if quarter is None else rows // 2)]

    def passed_on(mask, to, quarter=None):
        return (lambda ins, outs, pos: part(pos, mask, quarter)(outs), lambda ins, outs, pos: part(pos, mask, quarter)(outs), to)

    first = [(lambda ins, outs, pos: ins[0].at[pl.ds(pos[2] * rows, rows)], lambda ins, outs, pos: part(pos, 0)(outs),
              functools.partial(_other_chip, mask=mask)) for mask in (x_nb, y_nb)]
    first += [(lambda ins, outs, pos: ins[1], lambda ins, outs, pos: outs[1].at[_chip(pos)],
               functools.partial(_other_chip, mask=mask)) for mask in CHIP_MASKS]
    second = [passed_on(x_nb, functools.partial(_other_chip, mask=y_nb), quarter=0),
              passed_on(y_nb, functools.partial(_other_chip, mask=x_nb), quarter=1),
              passed_on(x_nb, _sibling), passed_on(y_nb, _sibling)]
    third = [passed_on(diag, _sibling)]
    return _exchange("gather_weights", [wb, cb], [_sds((4,) + wb.shape, wb.dtype), _sds((4,) + cb.shape, cb.dtype)],
                     [first, second, third])


def _assemble_w_in(gw, wb, jidx):
    m = gw.shape[1]

    def body(j_ref, g_ref, wb_ref, o_ref):
        step = pl.program_id(0)

        @pl.when(step == 0)
        def _():
            o_ref[...] = jnp.zeros_like(o_ref)

        blk = jnp.where(step == j_ref[0], wb_ref[...], g_ref[0]).astype(F32)
        lane = lax.broadcasted_iota(I32, (m, BLK_IN_PAD), 1)
        for j in range(4):
            @pl.when(step == j)
            def _(j=j):
                base = j * BLK_IN // HEAD * HEAD
                shift = j * BLK_IN - base
                moved = pltpu.roll(blk, shift, 1) if shift else blk
                window = o_ref[:, base:base + BLK_IN_PAD].astype(F32)
                mine = (lane >= shift) & (lane < shift + BLK_IN)
                o_ref[:, base:base + BLK_IN_PAD] = jnp.where(mine, moved, window).astype(BF16)

    return _call(
        body, name="assemble_w_in",
        grid_spec=pltpu.PrefetchScalarGridSpec(
            num_scalar_prefetch=1, grid=(4,),
            in_specs=[pl.BlockSpec((1, m, BLK_IN_PAD), lambda j, j_ref: (j, 0, 0)),
                      pl.BlockSpec((m, BLK_IN_PAD), lambda j, j_ref: (0, 0))],
            out_specs=pl.BlockSpec((m, N_IN_PAD), lambda j, j_ref: (0, 0))),
        out_shape=_sds((m, N_IN_PAD), BF16),
        compiler_params=_params("arbitrary"),
    )(jidx, gw, wb)


def _gather_blocks(ob):
    copies = [(lambda ins, outs, pos: ins[0], lambda ins, outs, pos: outs[0].at[_chip(pos)],
               functools.partial(_other_chip, mask=mask)) for mask in CHIP_MASKS]
    return [_sds((4,) + ob.shape, ob.dtype)], copies


def _to_sibling_half(name, arrays):
    def src(ins, outs, pos, a):
        h = arrays[a].shape[-2] // 2
        sl = pl.ds((1 - pos[2]) * h, h)
        return ins[a].at[:, sl] if arrays[a].ndim == 3 else ins[a].at[sl]

    outs = [_sds(a.shape[:-2] + (a.shape[-2] // 2, a.shape[-1]), a.dtype) for a in arrays]
    phase = [(functools.partial(src, a=a), lambda ins, outs, pos, a=a: outs[a], _sibling) for a in range(len(arrays))]
    return _exchange(name, arrays, outs, [phase])


def _add_half(name, full, part, cidx):
    shape = part.shape
    lead = shape[0] if len(shape) == 3 else 1
    rows, cols = shape[-2], shape[-1]
    tr = rows
    nr = rows // tr
    f3 = full.reshape((lead,) + full.shape[-2:])
    p3 = part.reshape((lead, rows, cols))

    def body(c_ref, f_ref, p_ref, o_ref):
        o_ref[...] = (f_ref[...].astype(F32) + p_ref[...].astype(F32)).astype(o_ref.dtype)

    out = _call(
        body, name=name,
        grid_spec=pltpu.PrefetchScalarGridSpec(
            num_scalar_prefetch=1, grid=(lead, nr),
            in_specs=[pl.BlockSpec((1, tr, cols), lambda b, r, c_ref: (b, c_ref[0] * nr + r, 0)),
                      pl.BlockSpec((1, tr, cols), lambda b, r, c_ref: (b, r, 0))],
            out_specs=pl.BlockSpec((1, tr, cols), lambda b, r, c_ref: (b, r, 0))),
        out_shape=_sds((lead, rows, cols), part.dtype),
        compiler_params=_params("arbitrary", "arbitrary"),
    )(cidx, f3, p3)
    return out.reshape(shape)


def _to_other_chips(arrays, blocked):
    def src(ins, outs, pos, a, mask):
        return ins[a].at[_chip(pos) ^ mask] if blocked[a] else ins[a]

    outs = [_sds((3,) + (a.shape[1:] if b else a.shape), a.dtype) for a, b in zip(arrays, blocked)]
    copies = []
    for mi, mask in enumerate(CHIP_MASKS):
        for a in range(len(arrays)):
            copies.append((functools.partial(src, a=a, mask=mask), lambda ins, outs, pos, a=a, mi=mi: outs[a].at[mi],
                           functools.partial(_other_chip, mask=mask)))
    return outs, copies


def _add_chips(name, own, got, jidx, blocked):
    rows, cols = got.shape[-2:]
    tr = rows
    nr = rows // tr
    o3 = own if blocked else own.reshape((1, rows, cols))

    def body(j_ref, o_ref, g_ref, out_ref):
        out_ref[...] = ((o_ref[0].astype(F32) + g_ref[0].astype(F32))
                        + (g_ref[1].astype(F32) + g_ref[2].astype(F32)))

    own_map = (lambda r, j_ref: (j_ref[0], r, 0)) if blocked else (lambda r, j_ref: (0, r, 0))
    return _call(
        body, name=name,
        grid_spec=pltpu.PrefetchScalarGridSpec(
            num_scalar_prefetch=1, grid=(nr,),
            in_specs=[pl.BlockSpec((1, tr, cols), own_map),
                      pl.BlockSpec((3, tr, cols), lambda r, j_ref: (0, r, 0))],
            out_specs=pl.BlockSpec((tr, cols), lambda r, j_ref: (r, 0))),
        out_shape=_sds((rows, cols)),
        compiler_params=_params("arbitrary"),
    )(jidx, o3, got)


def _to_sibling(name, arrays):
    phase = [(lambda ins, outs, pos, a=a: ins[a], lambda ins, outs, pos, a=a: outs[a], _sibling)
             for a in range(len(arrays))]
    return _exchange(name, arrays, [_sds(a.shape, a.dtype) for a in arrays], [phase])


def _local_step(x, target, w_pad, w_out, conv_w, norm_w, pool_w, pool_scale, a_log, dt_bias, dn_norm_w, final_norm_w):
    proj, n_t = _proj_fwd(x, norm_w, w_pad)
    y_pool = _pool_fwd(proj, pool_w, pool_scale)
    qn, kn, vs, beta, g = _conv_fwd(proj, conv_w, a_log, dt_bias)
    w, att, qd, kd, tm, cd, o, vn, st = _delta_fwd(qn, kn, vs, beta, g)
    w_out = w_out(o) if callable(w_out) else w_out
    g_wout, dh, dyp, do, ddz, loss, g_fnw, g_dnw = _out_fwd_bwd(x, y_pool, o, proj, target, w_out, dn_norm_w, final_norm_w)
    dpu, dpz, g_pw, g_ps = _pool_bwd(proj, dyp, pool_w, pool_scale)
    dqn, dkn, dvs, dbeta, dg = _delta_bwd(do, vn, qd, kd, w, att, cd, st, qn, kn, vs, beta, g, tm)
    dcq, dck, dcv, dba, g_cw, g_sm = _conv_bwd(proj, conv_w, a_log, dt_bias, dqn, dkn, dvs, dbeta, dg)
    pieces = [dpu, dpz, dcq, dck, dcv, ddz, dba]
    g_win = _grad_w_in(n_t, pieces)
    small = dict(norm_w=jnp.zeros_like(norm_w), pool_w=g_pw, pool_scale=g_ps, conv_w=g_cw[:CONV_K],
                 a_log=g_sm[0:1, 0:N_HEADS], dt_bias=g_sm[0:1, N_HEADS:2 * N_HEADS], dn_norm_w=g_dnw, final_norm_w=g_fnw)
    return loss[0, 0], g_win, g_wout, small, dh, pieces


SMALL_LAYOUT = (("pool_w", 512, HEAD, (1, N_HEADS, HEAD, HEAD)), ("final_norm_w", 8, HEAD, (D_MODEL,)),
                ("pool_scale", 4, HEAD, (1, D_HALF)), ("conv_w", 48, HEAD, (1, CONV_K, 3 * D_HALF)),
                ("dn_norm_w", 1, HEAD, (1, HEAD)), ("a_log", 1, N_HEADS, (1, N_HEADS)), ("dt_bias", 1, N_HEADS, (1, N_HEADS)),
                ("loss", 1, 1, ()))


def _small_offsets():
    offs, r = {}, 0
    for name, rows, _, _ in SMALL_LAYOUT:
        offs[name] = r
        r += -(-rows // 8) * 8
    assert r <= SMALL_ROWS
    return offs


def _pack_small(t):
    parts = []
    for name, rows, lanes, _ in SMALL_LAYOUT:
        a = t.get(name, jnp.zeros((1,), F32)).reshape(rows, lanes)
        parts.append(jnp.pad(a, ((0, -(-rows // 8) * 8 - rows), (0, HEAD - lanes))))
    buf = jnp.concatenate(parts, axis=0)
    return jnp.pad(buf, ((0, SMALL_ROWS - buf.shape[0]), (0, 0)))


def _adamw_small(w, g_own, g_got, cidx, m, v):
    offs = _small_offsets()
    names = [e[0] for e in SMALL_LAYOUT]
    n = len(names)

    def body(c_ref, w_ref, go_ref, gg_ref, m_ref, v_ref, *outs):
        own_low = c_ref[0] == 0
        gv = jnp.concatenate([jnp.where(own_low, go_ref[...], gg_ref[...]), jnp.where(own_low, gg_ref[...], go_ref[...])], axis=0)
        mn = ADAM_B1 * m_ref[...] + (1.0 - ADAM_B1) * gv
        vn = ADAM_B2 * v_ref[...] + (1.0 - ADAM_B2) * (gv * gv)
        m_hat = mn / (1.0 - ADAM_B1 ** ADAM_STEP)
        v_hat = vn / (1.0 - ADAM_B2 ** ADAM_STEP)
        dl = -ADAM_LR * (m_hat / (jnp.sqrt(v_hat) + ADAM_EPS) + ADAM_WD * w_ref[...])
        for kind, arr in enumerate((gv, dl, mn, vn)):
            for i, (name, rows, lanes, _) in enumerate(SMALL_LAYOUT):
                outs[kind * n + i][...] = arr[offs[name]:offs[name] + rows, :lanes]

    whole = lambda shape: pl.BlockSpec(shape, lambda i, c_ref: (0,) * len(shape))
    out_shapes = [_sds((rows, lanes)) for _, rows, lanes, _ in SMALL_LAYOUT] * 4
    res = _call(
        body, name="adamw_small",
        grid_spec=pltpu.PrefetchScalarGridSpec(
            num_scalar_prefetch=1, grid=(1,),
            in_specs=[whole(w.shape), whole(g_own.shape), whole(g_got.shape), whole(m.shape), whole(v.shape)],
            out_specs=[whole(o.shape) for o in out_shapes]),
        out_shape=out_shapes,
        compiler_params=_params("arbitrary"),
    )(cidx, w, g_own, g_got, m, v)
    return [{name: res[kind * n + i].reshape(shape) for i, (name, _, _, shape) in enumerate(SMALL_LAYOUT)}
            for kind in range(4)]


def kernel(x, norm_w, w_in, pool_w, pool_scale, conv_w, a_log, dt_bias, dn_norm_w, w_out, final_norm_w, loss_target, m_norm_w, m_w_in, m_pool_w, m_pool_scale, m_conv_w, m_a_log, m_dt_bias, m_dn_norm_w, m_w_out, m_final_norm_w, v_norm_w, v_w_in, v_pool_w, v_pool_scale, v_conv_w, v_a_log, v_dt_bias, v_dn_norm_w, v_w_out, v_final_norm_w):
    cidx = lax.axis_index("c").astype(I32).reshape(1)
    jidx = (2 * lax.axis_index("x") + lax.axis_index("y")).astype(I32)

    wb = jnp.pad(w_in[0].astype(BF16), ((0, 0), (0, BLK_IN_PAD - BLK_IN)))
    ob = w_out[0].astype(BF16)
    gw, gc = _gather_weights(wb, conv_w[0])
    mine = lambda j: jidx == j
    w_pad = _assemble_w_in(gw, wb, jidx.reshape(1))
    cw_full = jnp.concatenate([jnp.where(mine(j), conv_w[0], gc[j]) for j in range(4)], axis=1)

    lands_o, copies_o = _gather_blocks(ob)
    sems_o, ob_thru, zones_o, token_o = _exchange_start("gather_w_out_start", [ob], lands_o, copies_o)

    def w_out_full(after):
        (own,), (got,) = _exchange_wait("gather_w_out_wait", sems_o, ob_thru, zones_o, copies_o, after)
        return jnp.where((jnp.arange(4) == jidx)[:, None, None], own[None], got).reshape(D_MODEL, D_MODEL)

    loss, g_win, g_wout, small, dh, pieces = _local_step(
        x[0], loss_target[0], w_pad, w_out_full, cw_full, norm_w + token_o[0, 0], pool_w[0], pool_scale, a_log, dt_bias,
        dn_norm_w, final_norm_w.reshape(1, D_MODEL))
    small["loss"] = loss

    blocks_out = g_wout.reshape(4, BLK_OUT, D_MODEL)
    full = [g_win, blocks_out, _pack_small(small)]
    from_sib = _to_sibling_half("reduce_sibling", full)
    chip_sum = [_add_half("add_sibling_%d" % i, f, p, cidx) for i, (f, p) in enumerate(zip(full, from_sib))]
    blocked = [True, True, False]
    lands, copies = _to_other_chips(chip_sum, blocked)
    sems, chip_sum, zones, token = _exchange_start("reduce_chips_start", chip_sum, lands, copies)
    gx, g_nw = _in_bwd(x[0], dh, norm_w + token[0, 0], w_pad, pieces)
    g_nw = _allreduce_tile("reduce_norm_w", g_nw.reshape(8, HEAD)).reshape(1, D_MODEL)
    chip_sum, from_chips = _exchange_wait("reduce_chips_wait", sems, chip_sum, zones, copies, gx)
    halves = [_add_chips("add_chips_%d" % i, o, g, jidx.reshape(1), b)
              for i, (o, g, b) in enumerate(zip(chip_sum, from_chips, blocked))]
    other_halves = _to_sibling("swap_halves", halves)

    weights = dict(norm_w=norm_w, w_in=w_in, pool_w=pool_w, pool_scale=pool_scale, conv_w=conv_w, a_log=a_log,
                   dt_bias=dt_bias, dn_norm_w=dn_norm_w, w_out=w_out, final_norm_w=final_norm_w)
    ms = dict(norm_w=m_norm_w, w_in=m_w_in, pool_w=m_pool_w, pool_scale=m_pool_scale, conv_w=m_conv_w, a_log=m_a_log,
              dt_bias=m_dt_bias, dn_norm_w=m_dn_norm_w, w_out=m_w_out, final_norm_w=m_final_norm_w)
    vs = dict(norm_w=v_norm_w, w_in=v_w_in, pool_w=v_pool_w, pool_scale=v_pool_scale, conv_w=v_conv_w, a_log=v_a_log,
              dt_bias=v_dt_bias, dn_norm_w=v_dn_norm_w, w_out=v_w_out, final_norm_w=v_final_norm_w)
    names = ["norm_w", "w_in", "pool_w", "pool_scale", "conv_w", "a_log", "dt_bias", "dn_norm_w", "w_out", "final_norm_w"]
    small_names = [n for n in names if n not in ("w_in", "w_out")]

    def pack(t):
        conv = lax.dynamic_update_slice_in_dim(jnp.zeros((CONV_K, 3 * D_HALF), F32), t["conv_w"][0], jidx * BLK_CONV, axis=1)
        return _pack_small({**{n: t[n] for n in small_names if n != "conv_w"}, "conv_w": conv})

    results = [{}, {}, {}, {}]
    to_tiles = lambda a: jnp.transpose(a, (2, 0, 1)).reshape(BLK_IN, 8, HEAD)
    from_tiles = lambda a: jnp.transpose(a, (1, 2, 0)).reshape(1, D_MODEL, BLK_IN)
    lo = jnp.where(cidx[0] == 0, halves[0], other_halves[0])
    hi = jnp.where(cidx[0] == 0, other_halves[0], halves[0])
    g_tiles = jnp.concatenate([lo[:, :BLK_IN].T, hi[:, :BLK_IN].T], axis=1).reshape(BLK_IN, 8, HEAD)
    outs = _adamw_tiles("adamw_w_in", to_tiles(w_in), g_tiles, to_tiles(m_w_in), to_tiles(v_w_in))
    for res, o in zip(results, (g_tiles,) + tuple(outs)):
        res["w_in"] = from_tiles(o)
    outs = _adamw_shard("adamw_w_out", w_out, halves[1], other_halves[1], cidx, m_w_out, v_w_out)
    for res, o in zip(results, outs):
        res["w_out"] = o
    outs = _adamw_small(pack(weights), halves[2], other_halves[2], cidx, pack(ms), pack(vs))
    for res, got in zip(results, outs):
        got["conv_w"] = lax.dynamic_slice_in_dim(got["conv_w"], jidx * BLK_CONV, BLK_CONV, axis=2)
        res.update(got)
    one_tile = lambda a: a.reshape(1, 8, HEAD)
    outs = _adamw_tiles("adamw_norm_w", one_tile(norm_w), one_tile(g_nw), one_tile(m_norm_w), one_tile(v_norm_w))
    for res, o in zip(results, (g_nw,) + tuple(outs)):
        res["norm_w"] = o.reshape(1, D_MODEL)
    grads, delta, new_m, new_v = results

    return (grads["loss"], gx[None], *[grads[n] for n in names], *[delta[n] for n in names],
            *[new_m[n] for n in names], *[new_v[n] for n in names])
```

```python
import functools

import jax
import jax.numpy as jnp
import numpy as np
from jax import lax
from jax.experimental import pallas as pl
from jax.experimental.pallas import tpu as pltpu

F32 = jnp.float32
BF16 = jnp.bfloat16
I32 = jnp.int32

D_MODEL = 1024
D_HALF = 512
N_HEADS = 4
HEAD = 128
CHUNK = 64
PAIR = 2 * CHUNK
WINDOWS = (2, 4, 8, 16)
CONV_K = 4
EPS = 1e-6
N_IN = 3080
N_IN_PAD = 3200
BLK_IN = 770
BLK_IN_PAD = 896
BLK_OUT = 256
BLK_CONV = 384
COL_BA = 3072
QK_SCALE = HEAD ** -0.5
SMALL_ROWS = 608
VMEM_LIMIT = 56 * 1024 * 1024

ADAM_LR = 0.001
ADAM_B1 = 0.9
ADAM_B2 = 0.999
ADAM_EPS = 1e-08
ADAM_WD = 0.01
ADAM_STEP = 10

CHIP_MASKS = (2, 1, 3)
HEADS = range(N_HEADS)
HEAD_COLS = [slice(h * HEAD, (h + 1) * HEAD) for h in HEADS]


def _call(body, **kw):
    return pl.pallas_call(body, **kw)


def _params(*sem):
    return pltpu.CompilerParams(dimension_semantics=sem, vmem_limit_bytes=VMEM_LIMIT)


def _sds(shape, dtype=F32):
    return jax.ShapeDtypeStruct(shape, dtype)


def _bdot(a, b):
    return jnp.dot(a.astype(BF16), b.astype(BF16), preferred_element_type=F32)


def _bdot_nt(a, b):
    return lax.dot_general(a.astype(BF16), b.astype(BF16), (((1,), (1,)), ((), ())), preferred_element_type=F32)


def _bdot_tn(a, b):
    return lax.dot_general(a.astype(BF16), b.astype(BF16), (((0,), (0,)), ((), ())), preferred_element_type=F32)


def _split(a):
    hi = a.astype(BF16)
    lo = (a - hi.astype(F32)).astype(BF16)
    return hi, lo


def _mask_dot(m, b, dims=(((1,), (0,)), ((), ()))):
    bh, bl = _split(b)
    dg = functools.partial(lax.dot_general, dimension_numbers=dims, preferred_element_type=F32)
    return dg(m, bh) + dg(m, bl)


def _sigmoid(x):
    return 0.5 * jnp.tanh(0.5 * x) + 0.5


def _softplus(x):
    return jnp.maximum(x, 0.0) + jnp.log(1.0 + jnp.exp(-jnp.abs(x)))


def _rowsum(x):
    return jnp.sum(x, axis=-1, keepdims=True)


def _colsum(x):
    return jnp.sum(x, axis=0, keepdims=True)


def _shift_down(xv, prev8, k):
    r = pltpu.roll(xv, k, 0)
    q = pltpu.roll(prev8, k, 0)
    row = lax.broadcasted_iota(I32, prev8.shape, 0)
    top = jnp.where(row < k, q, r[0:8])
    return jnp.concatenate([top, r[8:]], axis=0)


def _shift_up(xv, next8, k):
    t = xv.shape[0]
    r = pltpu.roll(xv, t - k, 0)
    q = pltpu.roll(next8, 8 - k, 0)
    row = lax.broadcasted_iota(I32, next8.shape, 0)
    bot = jnp.where(row >= 8 - k, q, r[t - 8:])
    return jnp.concatenate([r[:t - 8], bot], axis=0)


def _band(rows, cols, off, w, anti=False):
    r = lax.broadcasted_iota(I32, (rows, cols), 0)
    c = lax.broadcasted_iota(I32, (rows, cols), 1)
    d = (c - r + off) if anti else (r - c + off)
    return ((d >= 0) & (d < w)).astype(BF16)


def _head(ref_or_val, h):
    return ref_or_val[:, h * HEAD:(h + 1) * HEAD]


INTRA_PAIRS = 2
UNITS = [(pp, h) for pp in range(INTRA_PAIRS) for h in HEADS]


def _heads(ref, rows=PAIR):
    return [ref[pp * rows:(pp + 1) * rows, HEAD_COLS[h]] for pp, h in UNITS]


def _put_heads(ref, vals, rows=PAIR):
    for (pp, h), v in zip(UNITS, vals):
        ref[pp * rows:(pp + 1) * rows, HEAD_COLS[h]] = v.astype(ref.dtype)


def _each(fn, *lists):
    return [fn(*args) for args in zip(*lists)]


def _proj_fwd(x, norm_w, w_pad):
    s = x.shape[0]
    tm = 512

    def body(x_ref, nw_ref, w_ref, proj_ref, nt_ref):
        xv = x_ref[...]
        r = lax.rsqrt(jnp.mean(xv * xv, axis=-1, keepdims=True) + EPS)
        nv = xv * r * nw_ref[...]
        nt_ref[...] = nv.T.astype(BF16)
        proj_ref[...] = jnp.dot(nv.astype(BF16), w_ref[...], preferred_element_type=F32)

    return _call(
        body, name="proj_fwd", grid=(s // tm,),
        in_specs=[pl.BlockSpec((tm, D_MODEL), lambda i: (i, 0)),
                  pl.BlockSpec((1, D_MODEL), lambda i: (0, 0)),
                  pl.BlockSpec((D_MODEL, N_IN_PAD), lambda i: (0, 0))],
        out_specs=[pl.BlockSpec((tm, N_IN_PAD), lambda i: (i, 0)),
                   pl.BlockSpec((D_MODEL, tm), lambda i: (0, i))],
        out_shape=[_sds((s, N_IN_PAD)), _sds((D_MODEL, s), BF16)],
        compiler_params=_params("arbitrary"),
    )(x, norm_w, w_pad)


def _pool_bands(t, anti=False):
    r = np.arange(t)[:, None]
    c = np.arange(t + HEAD)[None, :]
    d = (c - r) if anti else (r - c + HEAD)
    return jnp.asarray(np.stack([(d >= 0) & (d < w) for w in WINDOWS]), BF16)


def _pool_mix(u, halo, z, pw, bands, row0):
    t = u[0].shape[0]
    rows = row0 + lax.broadcasted_iota(I32, (t, 1), 0) + 1
    cnt = [jnp.minimum(rows, w).astype(F32) for w in WINDOWS]
    win = _each(lambda b, h, v: _mask_dot(b, jnp.concatenate([h, v], axis=0)), bands, halo, u)
    mix = _each(lambda a, c, v: a / c - v, win, cnt, u)
    mixed = _each(_bdot, mix, pw)
    return mix, mixed, _each(_sigmoid, z), cnt


POOL_T = 256


def _pool_fwd(proj, pool_w, pool_scale):
    s = proj.shape[0]
    t = POOL_T
    hb = t // HEAD

    def body(u_ref, z_ref, halo_ref, pw_ref, ps_ref, band_ref, y_ref):
        i = pl.program_id(0)
        live = (i > 0).astype(F32)
        groups = lambda ref: [ref[:, sl] for sl in HEAD_COLS]
        z = groups(z_ref)
        _, mixed, sg, _ = _pool_mix(groups(u_ref), [h * live for h in groups(halo_ref)], z,
                                    [pw_ref[g] for g in HEADS], [band_ref[g] for g in HEADS], i * t)
        for sl, m, zg, s_ in zip(HEAD_COLS, mixed, z, sg):
            y_ref[:, sl] = m * ps_ref[:, sl] * (zg * s_)

    return _call(
        body, name="pool_fwd", grid=(s // t,),
        in_specs=[pl.BlockSpec((t, D_HALF), lambda i: (i, 0)),
                  pl.BlockSpec((t, D_HALF), lambda i: (i, 1)),
                  pl.BlockSpec((HEAD, D_HALF), lambda i: (jnp.maximum(i * hb - 1, 0), 0)),
                  pl.BlockSpec((N_HEADS, HEAD, HEAD), lambda i: (0, 0, 0)),
                  pl.BlockSpec((1, D_HALF), lambda i: (0, 0)),
                  pl.BlockSpec((N_HEADS, t, HEAD + t), lambda i: (0, 0, 0))],
        out_specs=pl.BlockSpec((t, D_HALF), lambda i: (i, 0)),
        out_shape=_sds((s, D_HALF)),
        compiler_params=_params("arbitrary"),
    )(proj, proj, proj, pool_w, pool_scale, _pool_bands(t))


def _conv_taps(xv, prev8):
    return [_shift_down(xv, prev8, CONV_K - 1 - j) for j in range(CONV_K - 1)] + [xv]


def _conv_pre(taps, cw):
    y = taps[CONV_K - 1] * cw[CONV_K - 1:CONV_K]
    for j in range(CONV_K - 2, -1, -1):
        y = y + taps[j] * cw[j:j + 1]
    return y


CONV_T = 256
CONV_SUB = 256


def _conv_specs(t, tile_of=lambda i: i):
    tiles = [pl.BlockSpec((t, D_HALF), functools.partial(lambda i, p: (tile_of(i), 2 + p), p=p)) for p in range(3)]
    halos = [pl.BlockSpec((8, D_HALF),
                          functools.partial(lambda i, p: (jnp.maximum(tile_of(i) * (t // 8) - 1, 0), 2 + p), p=p))
             for p in range(3)]
    return tiles + halos


def _conv_fwd(proj, conv_w, a_log, dt_bias):
    s = proj.shape[0]
    t = CONV_T

    def body(q_ref, k_ref, v_ref, hq_ref, hk_ref, hv_ref, ba_ref, cw_ref, al_ref, dtb_ref,
             qn_ref, kn_ref, vs_ref, beta_ref, g_ref):
        live = (pl.program_id(0) > 0).astype(F32)
        parts = ((q_ref, hq_ref, qn_ref), (k_ref, hk_ref, kn_ref), (v_ref, hv_ref, vs_ref))

        def sub_tile(r0, first):
            rows = pl.ds(r0, CONV_SUB)
            for p, (x_ref, h_ref, o_ref) in enumerate(parts):
                for h in HEADS:
                    cs = HEAD_COLS[h]
                    prev8 = h_ref[:, cs] * live if first else x_ref[pl.ds(r0 - 8, 8), cs]
                    y = _conv_pre(_conv_taps(x_ref[rows, cs], prev8), cw_ref[:, p * D_HALF + h * HEAD:p * D_HALF + (h + 1) * HEAD])
                    sv = y * _sigmoid(y)
                    o_ref[rows, cs] = sv if p == 2 else sv * lax.rsqrt(_rowsum(sv * sv) + EPS)
            ba = ba_ref[rows, :]
            for h in HEADS:
                beta = _sigmoid(ba[:, h:h + 1])
                gl = -jnp.exp(al_ref[0:1, h:h + 1]) * _softplus(ba[:, N_HEADS + h:N_HEADS + h + 1] + dtb_ref[0:1, h:h + 1])
                beta_ref[rows, HEAD_COLS[h]] = jnp.broadcast_to(beta, (CONV_SUB, HEAD))
                g_ref[rows, HEAD_COLS[h]] = jnp.broadcast_to(gl, (CONV_SUB, HEAD))

        sub_tile(0, True)

        def step(k, carry):
            sub_tile(pl.multiple_of(k * CONV_SUB, CONV_SUB), False)
            return carry

        lax.fori_loop(1, t // CONV_SUB, step, 0)

    row = pl.BlockSpec((t, D_HALF), lambda i: (i, 0))
    return _call(
        body, name="conv_fwd", grid=(s // t,),
        in_specs=_conv_specs(t) + [pl.BlockSpec((t, HEAD), lambda i: (i, COL_BA // HEAD)),
                                   pl.BlockSpec((CONV_K, 3 * D_HALF), lambda i: (0, 0)),
                                   pl.BlockSpec((1, N_HEADS), lambda i: (0, 0)),
                                   pl.BlockSpec((1, N_HEADS), lambda i: (0, 0))],
        out_specs=[row] * 5,
        out_shape=[_sds((s, D_HALF))] * 5,
        compiler_params=_params("arbitrary"),
    )(proj, proj, proj, proj, proj, proj, proj, conv_w, a_log, dt_bias)


def _pair_masks():
    r = lax.broadcasted_iota(I32, (PAIR, PAIR), 0)
    c = lax.broadcasted_iota(I32, (PAIR, PAIR), 1)
    same = jnp.right_shift(r, 6) == jnp.right_shift(c, 6)
    return same, same & (r >= c), same & (r > c), r == c


def _run(stages):
    for _ in stages:
        pass


def _interleave(*stage_lists):
    live = list(stage_lists)
    while live:
        for gen in list(live):
            try:
                next(gen)
            except StopIteration:
                live.remove(gen)


def _pair_common_stages(cm, qn, kn, vs, beta, g):
    same, incl, strict, eye = _pair_masks()
    incl_b = incl.astype(BF16)
    first = lax.broadcasted_iota(I32, (PAIR, HEAD), 0) < CHUNK
    cm.update(same=same, incl=incl, strict=strict, eye=eye)
    gc = _each(lambda gv: _mask_dot(incl_b, gv), g)
    q = _each(lambda v: v * QK_SCALE, qn)
    kb = _each(lambda k, b: k * b, kn, beta)
    cm.update(gc=gc, q=q, kb=kb, vb=_each(lambda v, b: v * b, vs, beta))
    yield
    cm.update(kk=_each(_bdot_nt, kb, kn), qk=_each(_bdot_nt, q, kn))
    gc_row = _each(lambda v: _colsum(jnp.where(eye, v, 0.0)), gc)
    gl = _each(lambda v: jnp.where(first, v[CHUNK - 1:CHUNK], v[PAIR - 1:PAIR]), gc)
    egc = _each(jnp.exp, gc)
    cm.update(gl=gl, egc=egc,
              decay=_each(lambda v, r: jnp.where(incl, jnp.exp(jnp.where(incl, v - r, 0.0)), 0.0), gc, gc_row))
    yield
    cm.update(ekd=_each(lambda a, b: jnp.exp(a - b), gl, gc), cd=_each(jnp.exp, gl),
              kbg=_each(lambda k, e: k * e, kb, egc))
    yield


def _pair_common(qn, kn, vs, beta, g):
    cm = {}
    _run(_pair_common_stages(cm, qn, kn, vs, beta, g))
    return cm


def _tri_inv_stages(out, a, eye_f):
    p = _each(lambda v: eye_f - v, a)
    x = _each(_bdot, a, a)
    yield
    for it in range(5):
        p = _each(lambda pv, xv: pv + _bdot(pv, xv), p, x)
        if it < 4:
            x = _each(_bdot, x, x)
        yield
    out["t"] = p


def _tri_inv(a, eye_f):
    out = {}
    _run(_tri_inv_stages(out, a, eye_f))
    return out["t"]


def _pair_spec():
    return pl.BlockSpec((INTRA_PAIRS * PAIR, D_HALF), lambda i: (i, 0))


def _chunk_scalar_spec(pairs=1, index=lambda i: (i, 0)):
    return pl.BlockSpec((16 * pairs, D_HALF), index)


SCAN_PAIRS = 2
SCAN_ROWS = SCAN_PAIRS * PAIR


def _intra_fwd(qn, kn, vs, beta, g):
    s = qn.shape[0]

    def body(qn_ref, kn_ref, vs_ref, beta_ref, g_ref, u_ref, w_ref, att_ref, qd_ref, kd_ref, t_ref, cd_ref):
        kn = _heads(kn_ref)
        cm = _pair_common(_heads(qn_ref), kn, _heads(vs_ref), _heads(beta_ref), _heads(g_ref))
        a = _each(lambda kk, d: jnp.where(cm["strict"], kk * d, 0.0), cm["kk"], cm["decay"])
        tm = _tri_inv(a, cm["eye"].astype(F32))
        _put_heads(t_ref, tm)
        _put_heads(u_ref, _each(_bdot, tm, cm["vb"]))
        _put_heads(w_ref, _each(_bdot, tm, cm["kbg"]))
        _put_heads(att_ref, _each(lambda a, b: a * b, cm["qk"], cm["decay"]))
        _put_heads(qd_ref, _each(lambda a, b: a * b, cm["q"], cm["egc"]))
        _put_heads(kd_ref, _each(lambda a, b: a * b, kn, cm["ekd"]))
        for ci in range(2):
            for (pp, h), v in zip(UNITS, cm["cd"]):
                cd_ref[pp * 16 + ci * 8:pp * 16 + (ci + 1) * 8, HEAD_COLS[h]] = v[ci * CHUNK:ci * CHUNK + 8]

    return _call(
        body, name="intra_fwd", grid=(s // (INTRA_PAIRS * PAIR),),
        in_specs=[_pair_spec()] * 5, out_specs=[_pair_spec()] * 6 + [_chunk_scalar_spec(INTRA_PAIRS)],
        out_shape=[_sds((s, D_HALF))] + [_sds((s, D_HALF), BF16)] * 5 + [_sds((s // 8, D_HALF))],
        compiler_params=_params("arbitrary"),
    )(qn, kn, vs, beta, g)


def _scan_fwd(u, w, att, qd, kd, cd):
    s = u.shape[0]
    n_chunks = s // CHUNK

    def body(u_ref, w_ref, att_ref, qd_ref, kd_ref, cd_ref, o_ref, vn_ref, st_ref, state):
        @pl.when(pl.program_id(0) == 0)
        def _():
            state[...] = jnp.zeros_like(state)
        cols = list(enumerate(HEAD_COLS))
        sm = [state[h] for h in HEADS]
        for ci in range(2 * SCAN_PAIRS):
            rs = slice(ci * CHUNK, (ci + 1) * CHUNK)
            for h in HEADS:
                st_ref[ci, h] = sm[h]
            both = [_bdot(jnp.concatenate([w_ref[rs, sl], qd_ref[rs, sl]], axis=0), sm[h]) for h, sl in cols]
            vn = [u_ref[rs, sl] - both[h][:CHUNK] for h, sl in cols]
            for h, sl in cols:
                vn_ref[rs, sl] = vn[h].astype(BF16)
                o_ref[rs, sl] = both[h][CHUNK:]
            sm = [sm[h] * cd_ref[ci * 8:ci * 8 + 1, sl] + _bdot_tn(kd_ref[rs, sl], vn[h]) for h, sl in cols]
        for h in HEADS:
            state[h] = sm[h]
        for pp in range(SCAN_PAIRS):
            rp = slice(pp * PAIR, (pp + 1) * PAIR)
            intra = [_bdot(att_ref[rp, sl], vn_ref[rp, sl]) for sl in HEAD_COLS]
            for h, sl in cols:
                o_ref[rp, sl] += intra[h]

    rows = pl.BlockSpec((SCAN_ROWS, D_HALF), lambda i: (i, 0))
    return _call(
        body, name="scan_fwd", grid=(s // SCAN_ROWS,),
        in_specs=[rows] * 5 + [_chunk_scalar_spec(SCAN_PAIRS)],
        out_specs=[rows, rows, pl.BlockSpec((2 * SCAN_PAIRS, N_HEADS, HEAD, HEAD), lambda i: (i, 0, 0, 0))],
        out_shape=[_sds((s, D_HALF)), _sds((s, D_HALF), BF16), _sds((n_chunks, N_HEADS, HEAD, HEAD))],
        scratch_shapes=[pltpu.VMEM((N_HEADS, HEAD, HEAD), F32)],
        compiler_params=_params("arbitrary"),
    )(u, w, att, qd, kd, cd)


def _delta_fwd(qn, kn, vs, beta, g):
    s = qn.shape[0]
    n_steps = s // SCAN_ROWS
    n_chunks = s // CHUNK
    assert INTRA_PAIRS == SCAN_PAIRS

    def body(qn_ref, kn_ref, vs_ref, beta_ref, g_ref, w_ref, att_ref, qd_ref, kd_ref, t_ref, cd_ref, o_ref, vn_ref, st_ref,
             state, u_s, w_s, att_s, qd_s, kd_s, cd_s):
        t = pl.program_id(0)

        @pl.when(t <= 1)
        def _():
            state[...] = jnp.zeros_like(state)

        @pl.when(t == 0)
        def _():
            for ref in (u_s, w_s, att_s, qd_s, kd_s, cd_s):
                ref[1] = jnp.zeros(ref.shape[1:], ref.dtype)

        cur = lax.rem(t, 2)
        prev = 1 - cur
        cols = list(enumerate(HEAD_COLS))

        def recurrence():
            sm = [state[h] for h in HEADS]
            for ci in range(2 * SCAN_PAIRS):
                rs = slice(ci * CHUNK, (ci + 1) * CHUNK)
                for h in HEADS:
                    st_ref[ci, h] = sm[h]
                both = [_bdot(jnp.concatenate([w_s[prev, rs, sl], qd_s[prev, rs, sl]], axis=0), sm[h]) for h, sl in cols]
                vn = [u_s[prev, rs, sl] - both[h][:CHUNK] for h, sl in cols]
                for h, sl in cols:
                    vn_ref[rs, sl] = vn[h].astype(BF16)
                    o_ref[rs, sl] = both[h][CHUNK:]
                yield
                sm = [sm[h] * cd_s[prev, ci * 8:ci * 8 + 1, sl] + _bdot_tn(kd_s[prev, rs, sl], vn[h]) for h, sl in cols]
                yield
            for h in HEADS:
                state[h] = sm[h]
            for pp in range(SCAN_PAIRS):
                rp = slice(pp * PAIR, (pp + 1) * PAIR)
                intra = [_bdot(att_s[prev, rp, sl], vn_ref[rp, sl]) for sl in HEAD_COLS]
                for h, sl in cols:
                    o_ref[rp, sl] += intra[h]
                yield

        def factors():
            kn = _heads(kn_ref)
            cm = {}
            yield from _pair_common_stages(cm, _heads(qn_ref), kn, _heads(vs_ref), _heads(beta_ref), _heads(g_ref))
            a = _each(lambda kk, d: jnp.where(cm["strict"], kk * d, 0.0), cm["kk"], cm["decay"])
            inv = {}
            yield from _tri_inv_stages(inv, a, cm["eye"].astype(F32))
            tm = inv["t"]
            res = dict(u=_each(_bdot, tm, cm["vb"]), w=_each(_bdot, tm, cm["kbg"]),
                       att=_each(lambda a, b: a * b, cm["qk"], cm["decay"]),
                       qd=_each(lambda a, b: a * b, cm["q"], cm["egc"]), kd=_each(lambda a, b: a * b, kn, cm["ekd"]))
            yield
            _put_heads(t_ref, tm)
            for key, out, keep in (("w", w_ref, w_s), ("att", att_ref, att_s), ("qd", qd_ref, qd_s), ("kd", kd_ref, kd_s)):
                _put_heads(out, res[key])
                for (pp, h), v in zip(UNITS, res[key]):
                    keep[cur, pp * PAIR:(pp + 1) * PAIR, HEAD_COLS[h]] = v.astype(BF16)
            for (pp, h), v in zip(UNITS, res["u"]):
                u_s[cur, pp * PAIR:(pp + 1) * PAIR, HEAD_COLS[h]] = v
            for ci in range(2):
                for (pp, h), v in zip(UNITS, cm["cd"]):
                    rows8 = slice(pp * 16 + ci * 8, pp * 16 + (ci + 1) * 8)
                    cd_ref[rows8, HEAD_COLS[h]] = v[ci * CHUNK:ci * CHUNK + 8]
                    cd_s[cur, rows8, HEAD_COLS[h]] = v[ci * CHUNK:ci * CHUNK + 8]
            yield

        _interleave(recurrence(), factors())

    last = n_steps - 1
    now = lambda i: (jnp.minimum(i, last), 0)
    before = lambda i: (jnp.maximum(i - 1, 0), 0)
    rows = lambda index: pl.BlockSpec((SCAN_ROWS, D_HALF), index)
    slot = lambda r, dtype: pltpu.VMEM((2, r, D_HALF), dtype)
    return _call(
        body, name="delta_fwd", grid=(n_steps + 1,),
        in_specs=[rows(now)] * 5,
        out_specs=[rows(now)] * 5 + [_chunk_scalar_spec(SCAN_PAIRS, now), rows(before), rows(before),
                                     pl.BlockSpec((2 * SCAN_PAIRS, N_HEADS, HEAD, HEAD), lambda i: (jnp.maximum(i - 1, 0), 0, 0, 0))],
        out_shape=[_sds((s, D_HALF), BF16)] * 5 + [_sds((s // 8, D_HALF)), _sds((s, D_HALF)), _sds((s, D_HALF), BF16),
                                                  _sds((n_chunks, N_HEADS, HEAD, HEAD))],
        scratch_shapes=[pltpu.VMEM((N_HEADS, HEAD, HEAD), F32), slot(SCAN_ROWS, F32), slot(SCAN_ROWS, BF16),
                        slot(SCAN_ROWS, BF16), slot(SCAN_ROWS, BF16), slot(SCAN_ROWS, BF16), slot(16 * SCAN_PAIRS, F32)],
        compiler_params=_params("arbitrary"),
    )(qn, kn, vs, beta, g)


OUT_T = 512


def _out_fwd_bwd(x, y_pool, o, proj, target, w_out, dn_norm_w, final_norm_w):
    s = x.shape[0]
    t = OUT_T

    def body(x_ref, yp_ref, o_ref, z_ref, tg_ref, wo_ref, dnw_ref, fnw_ref,
             gwo_ref, dh_ref, dyp_ref, do_ref, dz_ref, loss_ref, gfn_ref, gdn_ref, y_ref, yt_ref, gwo_acc):
        @pl.when(pl.program_id(0) == 0)
        def _():
            loss_ref[...] = jnp.zeros_like(loss_ref)
            gfn_ref[...] = jnp.zeros_like(gfn_ref)
            gdn_ref[...] = jnp.zeros_like(gdn_ref)
            gwo_acc[...] = jnp.zeros_like(gwo_acc)

        ypv = yp_ref[...]
        y_ref[:, :D_HALF] = ypv.astype(BF16)
        yt_ref[:D_HALF, :] = ypv.T.astype(BF16)
        dnw = dnw_ref[...]
        keep = []
        for h in HEADS:
            ov = o_ref[:, HEAD_COLS[h]]
            zv = z_ref[:, HEAD_COLS[h]]
            ro = lax.rsqrt(jnp.mean(ov * ov, axis=-1, keepdims=True) + EPS)
            ohat = ov * ro
            sg = _sigmoid(zv)
            keep.append((ro, ohat, zv, sg))
            ydn = ohat * dnw * (zv * sg)
            y_ref[:, D_HALF + h * HEAD:D_HALF + (h + 1) * HEAD] = ydn.astype(BF16)
            yt_ref[D_HALF + h * HEAD:D_HALF + (h + 1) * HEAD, :] = ydn.T.astype(BF16)

        hv = x_ref[...] + jnp.dot(y_ref[...], wo_ref[...], preferred_element_type=F32)
        r2 = lax.rsqrt(jnp.mean(hv * hv, axis=-1, keepdims=True) + EPS)
        hhat = hv * r2
        fnw = fnw_ref[...]
        err = hhat * fnw - tg_ref[...]
        loss_ref[...] += 0.5 * jnp.sum(_rowsum(err * err) * (1.0 / D_MODEL), axis=0, keepdims=True)
        dout = err * (1.0 / D_MODEL)
        gfn_ref[...] += _colsum(dout * hhat)
        dhh = dout * fnw
        dh = r2 * (dhh - hhat * jnp.mean(dhh * hhat, axis=-1, keepdims=True))
        dh_ref[...] = dh
        gwo_acc[...] += _bdot(yt_ref[...], dh)

        @pl.when(pl.program_id(0) == pl.num_programs(0) - 1)
        def _():
            gwo_ref[...] = gwo_acc[...].astype(BF16)

        dy = _bdot_nt(dh, wo_ref[...])
        dyp_ref[...] = dy[:, :D_HALF]
        gdn = jnp.zeros((1, HEAD), F32)
        for h in HEADS:
            ro, ohat, zv, sg = keep[h]
            dyd = dy[:, D_HALF + h * HEAD:D_HALF + (h + 1) * HEAD]
            sz = zv * sg
            dz_ref[:, HEAD_COLS[h]] = (dyd * ohat * dnw * (sg * (1.0 + zv * (1.0 - sg)))).astype(BF16)
            gdn = gdn + _colsum(dyd * ohat * sz)
            doh = dyd * dnw * sz
            do_ref[:, HEAD_COLS[h]] = ro * (doh - ohat * jnp.mean(doh * ohat, axis=-1, keepdims=True))
        gdn_ref[...] += gdn

    wide = pl.BlockSpec((t, D_MODEL), lambda i: (i, 0))
    half = pl.BlockSpec((t, D_HALF), lambda i: (i, 0))
    const = lambda shape: pl.BlockSpec(shape, lambda i: (0,) * len(shape))
    return _call(
        body, name="out_fwd_bwd", grid=(s // t,),
        in_specs=[wide, half, half, pl.BlockSpec((t, D_HALF), lambda i: (i, 5)), wide,
                  const((D_MODEL, D_MODEL)), const((1, HEAD)), const((1, D_MODEL))],
        out_specs=[const((D_MODEL, D_MODEL)), wide, half, half, half,
                   const((1, HEAD)), const((1, D_MODEL)), const((1, HEAD))],
        out_shape=[_sds((D_MODEL, D_MODEL), BF16), _sds((s, D_MODEL)), _sds((s, D_HALF)), _sds((s, D_HALF)),
                   _sds((s, D_HALF), BF16), _sds((1, HEAD)), _sds((1, D_MODEL)), _sds((1, HEAD))],
        scratch_shapes=[pltpu.VMEM((t, D_MODEL), BF16), pltpu.VMEM((D_MODEL, t), BF16), pltpu.VMEM((D_MODEL, D_MODEL), F32)],
        compiler_params=_params("arbitrary"),
    )(x, y_pool, o, proj, target, w_out, dn_norm_w, final_norm_w)


def _token_matmul(name, at, pieces):
    m, s = at.shape
    n = len(pieces)
    tn, tk = D_HALF, 512

    def body(a_ref, *refs):
        p_refs, o_ref, acc = refs[:n], refs[n], refs[n + 1]

        @pl.when(pl.program_id(0) == 0)
        def _():
            acc[...] = jnp.zeros_like(acc)

        av = a_ref[...]
        for p in range(n):
            acc[:, p * tn:(p + 1) * tn] += _bdot(av, p_refs[p][...])

        @pl.when(pl.program_id(0) == pl.num_programs(0) - 1)
        def _():
            o_ref[...] = acc[...].astype(BF16)

    return _call(
        body, name=name, grid=(s // tk,),
        in_specs=[pl.BlockSpec((m, tk), lambda k: (0, k))]
                 + [pl.BlockSpec((tk, tn), functools.partial(lambda k, cb: (k, cb), cb=cb)) for _, cb in pieces],
        out_specs=pl.BlockSpec((m, n * tn), lambda k: (0, 0)),
        out_shape=_sds((m, n * tn), BF16),
        scratch_shapes=[pltpu.VMEM((m, n * tn), F32)],
        compiler_params=_params("arbitrary"),
    )(at, *[p[0] for p in pieces])


def _grad_w_in(at, pieces):
    m, s = at.shape
    n = len(pieces)
    tn, tk = D_HALF, min(s, 1024)

    def body(a_ref, *refs):
        p_refs, o_ref, acc = refs[:n], refs[n], refs[n + 1]

        @pl.when(pl.program_id(0) == 0)
        def _():
            acc[...] = jnp.zeros_like(acc)

        av = a_ref[...]
        for p in range(n):
            acc[:, p * tn:(p + 1) * tn] += _bdot(av, p_refs[p][...])

        @pl.when(pl.program_id(0) == pl.num_programs(0) - 1)
        def _():
            for j in range(4):
                base = j * BLK_IN // HEAD * HEAD
                win = acc[:, base:base + BLK_IN_PAD]
                if j * BLK_IN > base:
                    win = pltpu.roll(win, BLK_IN_PAD - (j * BLK_IN - base), 1)
                o_ref[j] = win.astype(BF16)

    return _call(
        body, name="grad_w_in", grid=(s // tk,),
        in_specs=[pl.BlockSpec((m, tk), lambda k: (0, k))] + [pl.BlockSpec((tk, tn), lambda k: (k, 0))] * n,
        out_specs=pl.BlockSpec((4, m, BLK_IN_PAD), lambda k: (0, 0, 0)),
        out_shape=_sds((4, m, BLK_IN_PAD), BF16),
        scratch_shapes=[pltpu.VMEM((m, n * tn), F32)],
        compiler_params=_params("arbitrary"),
    )(at, *pieces)


def _pool_bwd(proj, dyp, pool_w, pool_scale):
    s = proj.shape[0]
    t = POOL_T
    hb = t // HEAD
    last = s // HEAD - 1

    def body(u_ref, z_ref, halo_ref, dy_ref, zn_ref, dyn_ref, pw_ref, ps_ref, band_ref, aband_ref,
             du_ref, dz_ref, gpw_ref, gps_ref):
        i = pl.program_id(0)

        @pl.when(i == 0)
        def _():
            gpw_ref[...] = jnp.zeros_like(gpw_ref)
            gps_ref[...] = jnp.zeros_like(gps_ref)

        live = (i > 0).astype(F32)
        more = (i < pl.num_programs(0) - 1).astype(F32)
        groups = lambda ref: [ref[:, sl] for sl in HEAD_COLS]
        z, ps, dy = groups(z_ref), groups(ps_ref), groups(dy_ref)
        pw = [pw_ref[g] for g in HEADS]
        mix, mixed, sg, cnt = _pool_mix(groups(u_ref), [h * live for h in groups(halo_ref)], z, pw,
                                        [band_ref[g] for g in HEADS], i * t)
        sz = _each(lambda a, b: a * b, z, sg)
        for sl, d, m, p, s_, zg in zip(HEAD_COLS, dy, mixed, ps, sg, z):
            dz_ref[:, sl] = (d * m * p * (s_ * (1.0 + zg * (1.0 - s_)))).astype(BF16)
        for sl, d, m, a in zip(HEAD_COLS, dy, mixed, sz):
            gps_ref[:, sl] += _colsum(d * m * a)
        dmixed = _each(lambda d, p, a: d * p * a, dy, ps, sz)
        for g, gp in enumerate(_each(_bdot_tn, mix, dmixed)):
            gpw_ref[g] += gp
        dmix = _each(_bdot_nt, dmixed, pw)
        dmix_n = _each(lambda d, p, zn, w_: _bdot_nt(d * more * p * (zn * _sigmoid(zn)), w_),
                       groups(dyn_ref), ps, groups(zn_ref), pw)
        scaled = [jnp.concatenate([a / c, b * (1.0 / w)], axis=0) for a, c, b, w in zip(dmix, cnt, dmix_n, WINDOWS)]
        du = _each(lambda b, s_, d: _mask_dot(b, s_) - d, [aband_ref[g] for g in HEADS], scaled, dmix)
        for sl, v in zip(HEAD_COLS, du):
            du_ref[:, sl] = v.astype(BF16)

    tile = lambda col: pl.BlockSpec((t, D_HALF), lambda i: (i, col))
    below = lambda col: pl.BlockSpec((HEAD, D_HALF), lambda i: (jnp.minimum((i + 1) * hb, last), col))
    return _call(
        body, name="pool_bwd", grid=(s // t,),
        in_specs=[tile(0), tile(1), pl.BlockSpec((HEAD, D_HALF), lambda i: (jnp.maximum(i * hb - 1, 0), 0)),
                  tile(0), below(1), below(0),
                  pl.BlockSpec((N_HEADS, HEAD, HEAD), lambda i: (0, 0, 0)), pl.BlockSpec((1, D_HALF), lambda i: (0, 0)),
                  pl.BlockSpec((N_HEADS, t, HEAD + t), lambda i: (0, 0, 0)),
                  pl.BlockSpec((N_HEADS, t, HEAD + t), lambda i: (0, 0, 0))],
        out_specs=[tile(0), tile(0), pl.BlockSpec((N_HEADS, HEAD, HEAD), lambda i: (0, 0, 0)),
                   pl.BlockSpec((1, D_HALF), lambda i: (0, 0))],
        out_shape=[_sds((s, D_HALF), BF16), _sds((s, D_HALF), BF16), _sds((N_HEADS, HEAD, HEAD)), _sds((1, D_HALF))],
        compiler_params=_params("arbitrary"),
    )(proj, proj, proj, dyp, proj, dyp, pool_w, pool_scale, _pool_bands(t), _pool_bands(t, anti=True))


def _scan_bwd(do, vn, qd, kd, w, att, cd, st):
    s = do.shape[0]
    n_steps = s // SCAN_ROWS

    def body(do_ref, vn_ref, qd_ref, kd_ref, w_ref, att_ref, cd_ref, st_ref,
             du_ref, dw_ref, datt_ref, dqd_ref, dkd_ref, dcd_ref, dstate):
        @pl.when(pl.program_id(0) == 0)
        def _():
            dstate[...] = jnp.zeros_like(dstate)
        _, incl, _, _ = _pair_masks()
        cols = list(enumerate(HEAD_COLS))
        dv_intra = []
        for pp in range(SCAN_PAIRS):
            rp = slice(pp * PAIR, (pp + 1) * PAIR)
            dv_intra.append([_bdot_tn(att_ref[rp, sl], do_ref[rp, sl]) for _, sl in cols])
            for _, sl in cols:
                datt_ref[rp, sl] = jnp.where(incl, _bdot_nt(do_ref[rp, sl], vn_ref[rp, sl]), 0.0)
        ds = [dstate[h] for h in HEADS]
        for ci in range(2 * SCAN_PAIRS - 1, -1, -1):
            rs = slice(ci * CHUNK, (ci + 1) * CHUNK)
            in_pair = slice((ci % 2) * CHUNK, (ci % 2 + 1) * CHUNK)
            sm = [st_ref[ci, h] for h in HEADS]
            dvn = [dv_intra[ci // 2][h][in_pair] + _bdot(kd_ref[rs, sl], ds[h]) for h, sl in cols]
            for h, sl in cols:
                du_ref[rs, sl] = dvn[h].astype(BF16)
            dqd = [_bdot_nt(do_ref[rs, sl], sm[h]) for h, sl in cols]
            dw = [-_bdot_nt(dvn[h], sm[h]) for h, _ in cols]
            dkd = [_bdot_nt(vn_ref[rs, sl], ds[h]) for h, sl in cols]
            dcd = [jnp.broadcast_to(_rowsum(_colsum(ds[h] * sm[h])), (8, HEAD)) for h in HEADS]
            for h, sl in cols:
                dqd_ref[rs, sl] = dqd[h]
                dw_ref[rs, sl] = dw[h].astype(BF16)
                dkd_ref[rs, sl] = dkd[h]
                dcd_ref[ci * 8:(ci + 1) * 8, sl] = dcd[h]
            ds = [ds[h] * cd_ref[ci * 8:ci * 8 + 1, sl] + _bdot_tn(qd_ref[rs, sl], do_ref[rs, sl])
                  - _bdot_tn(w_ref[rs, sl], dvn[h]) for h, sl in cols]
        for h in HEADS:
            dstate[h] = ds[h]

    rev = pl.BlockSpec((SCAN_ROWS, D_HALF), lambda i: (n_steps - 1 - i, 0))
    rev_scalar = _chunk_scalar_spec(SCAN_PAIRS, lambda i: (n_steps - 1 - i, 0))
    return _call(
        body, name="scan_bwd", grid=(n_steps,),
        in_specs=[rev] * 6 + [rev_scalar,
                              pl.BlockSpec((2 * SCAN_PAIRS, N_HEADS, HEAD, HEAD), lambda i: (n_steps - 1 - i, 0, 0, 0))],
        out_specs=[rev] * 5 + [rev_scalar],
        out_shape=[_sds((s, D_HALF), BF16)] * 2 + [_sds((s, D_HALF))] * 3 + [_sds((s // 8, D_HALF))],
        scratch_shapes=[pltpu.VMEM((N_HEADS, HEAD, HEAD), F32)],
        compiler_params=_params("arbitrary"),
    )(do, vn, qd, kd, w, att, cd, st)


def _intra_bwd(qn, kn, vs, beta, g, tm, du, dw, datt, dqd, dkd, dcd):
    s = qn.shape[0]

    def body(qn_ref, kn_ref, vs_ref, beta_ref, g_ref, t_ref, du_ref, dw_ref, datt_ref, dqd_ref, dkd_ref, dcd_ref,
             dqn_ref, dkn_ref, dvs_ref, dbeta_ref, dg_ref):
        ones = jnp.ones((PAIR, HEAD), BF16)
        tn = (((0,), (0,)), ((), ()))
        kn, vs, beta = _heads(kn_ref), _heads(vs_ref), _heads(beta_ref)
        cm = _pair_common(_heads(qn_ref), kn, vs, beta, _heads(g_ref))
        tmv, duv, dwv, dattv, dqdv, dkdv = (_heads(r) for r in (t_ref, du_ref, dw_ref, datt_ref, dqd_ref, dkd_ref))
        dvb = _each(_bdot_tn, tmv, duv)
        dt = _each(lambda a, b, c, d: _bdot_nt(a, b) + _bdot_nt(c, d), duv, cm["vb"], dwv, cm["kbg"])
        dkbg = _each(_bdot_tn, tmv, dwv)
        m1 = _each(_bdot_tn, tmv, dt)
        da = _each(lambda a, b: -jnp.where(cm["strict"], _bdot_nt(a, b), 0.0), m1, tmv)
        dkk = _each(lambda a, b: a * b, da, cm["decay"])
        dqk = _each(lambda a, b: a * b, dattv, cm["decay"])
        dd = _each(lambda a, b, c, d: a * b + c * d, dkk, cm["kk"], dqk, cm["qk"])
        dkb = _each(lambda a, b, c, d: _bdot(a, b) + c * d, dkk, kn, dkbg, cm["egc"])
        dq = _each(lambda a, b, c, d: _bdot(a, b) + c * d, dqk, kn, dqdv, cm["egc"])
        dkn = _each(lambda a, b, c, d: _bdot_tn(a, b) + _bdot_tn(c, d), dkk, cm["kb"], dqk, cm["q"])
        dkn = _each(lambda a, b, c, d, e: a + b * c + d * e, dkn, dkdv, cm["ekd"], dkb, beta)
        t_kd = _each(lambda a, b, c: _rowsum(a * b * c), dkdv, kn, cm["ekd"])
        split = _each(_split, dd)
        rows_dd = [jnp.dot(hi, ones, preferred_element_type=F32) + jnp.dot(lo, ones, preferred_element_type=F32)
                   for hi, lo in split]
        cols_dd = [lax.dot_general(hi, ones, tn, preferred_element_type=F32)
                   + lax.dot_general(lo, ones, tn, preferred_element_type=F32) for hi, lo in split]
        dgc = _each(lambda r, c, a, b, e, f, k, t: r - c + _rowsum(a * b * e) + _rowsum(f * k) - t,
                    rows_dd, cols_dd, dqdv, cm["q"], cm["egc"], dkbg, cm["kbg"], t_kd)
        same_b = cm["same"].astype(BF16)
        rowi = lax.broadcasted_iota(I32, (PAIR, HEAD), 0)
        dcd = _each(lambda d: jnp.where(rowi < CHUNK, d[0:1], d[8:9]), _heads(dcd_ref, rows=16))
        dgl = _each(lambda t, d, c: _mask_dot(same_b, jnp.broadcast_to(t, (PAIR, HEAD))) + d * c, t_kd, dcd, cm["cd"])
        is_last = jnp.bitwise_and(rowi, CHUNK - 1) == CHUNK - 1
        dgc = _each(lambda a, b: a + jnp.where(is_last, b, 0.0), dgc, dgl)
        r = lax.broadcasted_iota(I32, (PAIR, PAIR), 0)
        c = lax.broadcasted_iota(I32, (PAIR, PAIR), 1)
        upper_b = (cm["same"] & (r <= c)).astype(BF16)
        _put_heads(dg_ref, _each(lambda v: _mask_dot(upper_b, v), dgc))
        _put_heads(dbeta_ref, _each(lambda a, b, c, d: jnp.broadcast_to(_rowsum(a * b) + _rowsum(c * d), (PAIR, HEAD)),
                                    dkb, kn, dvb, vs))
        _put_heads(dqn_ref, _each(lambda v: v * QK_SCALE, dq))
        _put_heads(dkn_ref, dkn)
        _put_heads(dvs_ref, _each(lambda a, b: a * b, dvb, beta))

    return _call(
        body, name="intra_bwd", grid=(s // (INTRA_PAIRS * PAIR),),
        in_specs=[_pair_spec()] * 11 + [_chunk_scalar_spec(INTRA_PAIRS)], out_specs=[_pair_spec()] * 5,
        out_shape=[_sds((s, D_HALF))] * 5,
        compiler_params=_params("arbitrary"),
    )(qn, kn, vs, beta, g, tm, du, dw, datt, dqd, dkd, dcd)


def _delta_bwd(do, vn, qd, kd, w, att, cd, st, qn, kn, vs, beta, g, tm):
    s = do.shape[0]
    n_steps = s // SCAN_ROWS
    assert INTRA_PAIRS == SCAN_PAIRS

    def body(do_ref, vn_ref, qd_ref, kd_ref, w_ref, att_ref, cd_ref, st_ref, qn_ref, kn_ref, vs_ref, beta_ref, g_ref, t_ref,
             dqn_ref, dkn_ref, dvs_ref, dbeta_ref, dg_ref, dstate, du_s, dw_s, datt_s, dqd_s, dkd_s, dcd_s):
        t = pl.program_id(0)

        @pl.when(t == 0)
        def _():
            dstate[...] = jnp.zeros_like(dstate)
            for ref in (du_s, dw_s, datt_s, dqd_s, dkd_s, dcd_s):
                ref[1] = jnp.zeros(ref.shape[1:], ref.dtype)

        cur = lax.rem(t, 2)
        prev = 1 - cur
        cols = list(enumerate(HEAD_COLS))
        _, incl, _, _ = _pair_masks()

        def recurrence():
            dv_intra = []
            for pp in range(SCAN_PAIRS):
                rp = slice(pp * PAIR, (pp + 1) * PAIR)
                dv_intra.append([_bdot_tn(att_ref[rp, sl], do_ref[rp, sl]) for _, sl in cols])
                for _, sl in cols:
                    datt_s[cur, rp, sl] = jnp.where(incl, _bdot_nt(do_ref[rp, sl], vn_ref[rp, sl]), 0.0)
                yield
            ds = [dstate[h] for h in HEADS]
            for ci in range(2 * SCAN_PAIRS - 1, -1, -1):
                rs = slice(ci * CHUNK, (ci + 1) * CHUNK)
                in_pair = slice((ci % 2) * CHUNK, (ci % 2 + 1) * CHUNK)
                sm = [st_ref[ci, h] for h in HEADS]
                dvn = [dv_intra[ci // 2][h][in_pair] + _bdot(kd_ref[rs, sl], ds[h]) for h, sl in cols]
                dqd = [_bdot_nt(do_ref[rs, sl], sm[h]) for h, sl in cols]
                dkd = [_bdot_nt(vn_ref[rs, sl], ds[h]) for h, sl in cols]
                dcd = [jnp.broadcast_to(_rowsum(_colsum(ds[h] * sm[h])), (8, HEAD)) for h in HEADS]
                yield
                dw = [-_bdot_nt(dvn[h], sm[h]) for h, _ in cols]
                for h, sl in cols:
                    du_s[cur, rs, sl] = dvn[h].astype(BF16)
                    dqd_s[cur, rs, sl] = dqd[h]
                    dw_s[cur, rs, sl] = dw[h].astype(BF16)
                    dkd_s[cur, rs, sl] = dkd[h]
                    dcd_s[cur, ci * 8:(ci + 1) * 8, sl] = dcd[h]
                ds = [ds[h] * cd_ref[ci * 8:ci * 8 + 1, sl] + _bdot_tn(qd_ref[rs, sl], do_ref[rs, sl])
                      - _bdot_tn(w_ref[rs, sl], dvn[h]) for h, sl in cols]
                yield
            for h in HEADS:
                dstate[h] = ds[h]

        def factors():
            ones = jnp.ones((PAIR, HEAD), BF16)
            tn = (((0,), (0,)), ((), ()))
            kept = lambda ref, rows=PAIR: [ref[prev, pp * rows:(pp + 1) * rows, HEAD_COLS[h]] for pp, h in UNITS]
            kn, vs, beta = _heads(kn_ref), _heads(vs_ref), _heads(beta_ref)
            cm = {}
            yield from _pair_common_stages(cm, _heads(qn_ref), kn, vs, beta, _heads(g_ref))
            tmv = _heads(t_ref)
            duv, dwv, dattv, dqdv, dkdv = kept(du_s), kept(dw_s), kept(datt_s), kept(dqd_s), kept(dkd_s)
            dvb = _each(_bdot_tn, tmv, duv)
            dt = _each(lambda a, b, c, d: _bdot_nt(a, b) + _bdot_nt(c, d), duv, cm["vb"], dwv, cm["kbg"])
            dkbg = _each(_bdot_tn, tmv, dwv)
            yield
            m1 = _each(_bdot_tn, tmv, dt)
            yield
            da = _each(lambda a, b: -jnp.where(cm["strict"], _bdot_nt(a, b), 0.0), m1, tmv)
            yield
            dkk = _each(lambda a, b: a * b, da, cm["decay"])
            dqk = _each(lambda a, b: a * b, dattv, cm["decay"])
            dd = _each(lambda a, b, c, d: a * b + c * d, dkk, cm["kk"], dqk, cm["qk"])
            dkb = _each(lambda a, b, c, d: _bdot(a, b) + c * d, dkk, kn, dkbg, cm["egc"])
            dq = _each(lambda a, b, c, d: _bdot(a, b) + c * d, dqk, kn, dqdv, cm["egc"])
            yield
            dkn = _each(lambda a, b, c, d: _bdot_tn(a, b) + _bdot_tn(c, d), dkk, cm["kb"], dqk, cm["q"])
            dkn = _each(lambda a, b, c, d, e: a + b * c + d * e, dkn, dkdv, cm["ekd"], dkb, beta)
            t_kd = _each(lambda a, b, c: _rowsum(a * b * c), dkdv, kn, cm["ekd"])
            yield
            split = _each(_split, dd)
            rows_dd = [jnp.dot(hi, ones, preferred_element_type=F32) + jnp.dot(lo, ones, preferred_element_type=F32)
                       for hi, lo in split]
            cols_dd = [lax.dot_general(hi, ones, tn, preferred_element_type=F32)
                       + lax.dot_general(lo, ones, tn, preferred_element_type=F32) for hi, lo in split]
            yield
            dgc = _each(lambda r, c, a, b, e, f, k, tk: r - c + _rowsum(a * b * e) + _rowsum(f * k) - tk,
                        rows_dd, cols_dd, dqdv, cm["q"], cm["egc"], dkbg, cm["kbg"], t_kd)
            same_b = cm["same"].astype(BF16)
            rowi = lax.broadcasted_iota(I32, (PAIR, HEAD), 0)
            dcd = _each(lambda d: jnp.where(rowi < CHUNK, d[0:1], d[8:9]), kept(dcd_s, rows=16))
            dgl = _each(lambda tk, d, c: _mask_dot(same_b, jnp.broadcast_to(tk, (PAIR, HEAD))) + d * c, t_kd, dcd, cm["cd"])
            yield
            is_last = jnp.bitwise_and(rowi, CHUNK - 1) == CHUNK - 1
            dgc = _each(lambda a, b: a + jnp.where(is_last, b, 0.0), dgc, dgl)
            r = lax.broadcasted_iota(I32, (PAIR, PAIR), 0)
            c = lax.broadcasted_iota(I32, (PAIR, PAIR), 1)
            upper_b = (cm["same"] & (r <= c)).astype(BF16)
            _put_heads(dg_ref, _each(lambda v: _mask_dot(upper_b, v), dgc))
            yield
            _put_heads(dbeta_ref, _each(lambda a, b, c, d: jnp.broadcast_to(_rowsum(a * b) + _rowsum(c * d), (PAIR, HEAD)),
                                        dkb, kn, dvb, vs))
            _put_heads(dqn_ref, _each(lambda v: v * QK_SCALE, dq))
            _put_heads(dkn_ref, dkn)
            _put_heads(dvs_ref, _each(lambda a, b: a * b, dvb, beta))
            yield

        _interleave(recurrence(), factors())

    last = n_steps - 1
    now = lambda i: (jnp.maximum(last - i, 0), 0)
    after = lambda i: (jnp.minimum(n_steps - i, last), 0)
    rows = lambda index: pl.BlockSpec((SCAN_ROWS, D_HALF), index)
    slot = lambda r, dtype: pltpu.VMEM((2, r, D_HALF), dtype)
    return _call(
        body, name="delta_bwd", grid=(n_steps + 1,),
        in_specs=[rows(now)] * 6 + [_chunk_scalar_spec(SCAN_PAIRS, now),
                                    pl.BlockSpec((2 * SCAN_PAIRS, N_HEADS, HEAD, HEAD), lambda i: (jnp.maximum(last - i, 0), 0, 0, 0))]
                 + [rows(after)] * 6,
        out_specs=[rows(after)] * 5,
        out_shape=[_sds((s, D_HALF))] * 5,
        scratch_shapes=[pltpu.VMEM((N_HEADS, HEAD, HEAD), F32), slot(SCAN_ROWS, BF16), slot(SCAN_ROWS, BF16),
                        slot(SCAN_ROWS, F32), slot(SCAN_ROWS, F32), slot(SCAN_ROWS, F32), slot(16 * SCAN_PAIRS, F32)],
        compiler_params=_params("arbitrary"),
    )(do, vn, qd, kd, w, att, cd, st, qn, kn, vs, beta, g, tm)


def _fused_call(name, n_steps, parts):
    n_in = [len(p["inputs"]) for p in parts]
    n_out = [len(p["out_shape"]) for p in parts]
    n_scr = [len(p["scratch"]) for p in parts]

    def body(*refs):
        ins, outs, scr = refs[:sum(n_in)], refs[sum(n_in):sum(n_in) + sum(n_out)], refs[sum(n_in) + sum(n_out):]
        gens, a, b, c = [], 0, 0, 0
        for p, ni, no, ns in zip(parts, n_in, n_out, n_scr):
            gens.append(p["stages"](ins[a:a + ni], outs[b:b + no], scr[c:c + ns]))
            a, b, c = a + ni, b + no, c + ns
        _interleave(*gens)

    flat = lambda key: [v for p in parts for v in p[key]]
    res = _call(
        body, name=name, grid=(n_steps,),
        in_specs=flat("in_specs"), out_specs=flat("out_specs"), out_shape=flat("out_shape"),
        scratch_shapes=flat("scratch"),
        compiler_params=_params("arbitrary"),
    )(*flat("inputs"))
    out, b = [], 0
    for no in n_out:
        out.append(res[b:b + no])
        b += no
    return out


def _pool_bwd_part(proj, dyp, pool_w, pool_scale, tile_of, n_tiles):
    s = proj.shape[0]
    t = POOL_T
    hb = t // HEAD
    last = s // HEAD - 1

    def stages(ins, outs, scratch):
        u_ref, z_ref, halo_ref, dy_ref, zn_ref, dyn_ref, pw_ref, ps_ref, band_ref, aband_ref = ins
        du_ref, dz_ref, gpw_ref, gps_ref = outs
        tile = tile_of(pl.program_id(0))

        @pl.when(pl.program_id(0) == 0)
        def _():
            gpw_ref[...] = jnp.zeros_like(gpw_ref)
            gps_ref[...] = jnp.zeros_like(gps_ref)

        live = (tile > 0).astype(F32)
        more = (tile < n_tiles - 1).astype(F32)
        groups = lambda ref: [ref[:, sl] for sl in HEAD_COLS]
        z, ps, dy = groups(z_ref), groups(ps_ref), groups(dy_ref)
        pw = [pw_ref[g] for g in HEADS]
        mix, mixed, sg, cnt = _pool_mix(groups(u_ref), [h * live for h in groups(halo_ref)], z, pw,
                                        [band_ref[g] for g in HEADS], tile * t)
        yield
        sz = _each(lambda a, b: a * b, z, sg)
        for sl, d, m, p, s_, zg in zip(HEAD_COLS, dy, mixed, ps, sg, z):
            dz_ref[:, sl] = (d * m * p * (s_ * (1.0 + zg * (1.0 - s_)))).astype(BF16)
        for sl, d, m, a in zip(HEAD_COLS, dy, mixed, sz):
            gps_ref[:, sl] += _colsum(d * m * a)
        dmixed = _each(lambda d, p, a: d * p * a, dy, ps, sz)
        yield
        for g, gp in enumerate(_each(_bdot_tn, mix, dmixed)):
            gpw_ref[g] += gp
        dmix = _each(_bdot_nt, dmixed, pw)
        yield
        dmix_n = _each(lambda d, p, zn, w_: _bdot_nt(d * more * p * (zn * _sigmoid(zn)), w_),
                       groups(dyn_ref), ps, groups(zn_ref), pw)
        yield
        scaled = [jnp.concatenate([a / c, b * (1.0 / w)], axis=0) for a, c, b, w in zip(dmix, cnt, dmix_n, WINDOWS)]
        du = _each(lambda b, s_, d: _mask_dot(b, s_) - d, [aband_ref[g] for g in HEADS], scaled, dmix)
        for sl, v in zip(HEAD_COLS, du):
            du_ref[:, sl] = v.astype(BF16)
        yield

    tile = lambda col: pl.BlockSpec((t, D_HALF), lambda i: (tile_of(i), col))
    below = lambda col: pl.BlockSpec((HEAD, D_HALF), lambda i: (jnp.minimum((tile_of(i) + 1) * hb, last), col))
    const3 = lambda shape: pl.BlockSpec(shape, lambda i: (0, 0, 0))
    return dict(
        inputs=[proj, proj, proj, dyp, proj, dyp, pool_w, pool_scale, _pool_bands(t), _pool_bands(t, anti=True)],
        in_specs=[tile(0), tile(1), pl.BlockSpec((HEAD, D_HALF), lambda i: (jnp.maximum(tile_of(i) * hb - 1, 0), 0)),
                  tile(0), below(1), below(0), const3((N_HEADS, HEAD, HEAD)), pl.BlockSpec((1, D_HALF), lambda i: (0, 0)),
                  const3((N_HEADS, t, HEAD + t)), const3((N_HEADS, t, HEAD + t))],
        out_specs=[tile(0), tile(0), const3((N_HEADS, HEAD, HEAD)), pl.BlockSpec((1, D_HALF), lambda i: (0, 0))],
        out_shape=[_sds((s, D_HALF), BF16), _sds((s, D_HALF), BF16), _sds((N_HEADS, HEAD, HEAD)), _sds((1, D_HALF))],
        scratch=[], stages=stages)


def _conv_bwd_part(proj, conv_w, a_log, dt_bias, dqn, dkn, dvs, dbeta, dg, tile_of, n_tiles):
    s = proj.shape[0]
    t = CONV_T

    def stages(ins, outs, scratch):
        (q_ref, k_ref, v_ref, hq_ref, hk_ref, hv_ref, ba_ref, cw_ref, al_ref, dtb_ref,
         dqn_ref, dkn_ref, dvs_ref, dbeta_ref, dg_ref) = ins
        oq_ref, ok_ref, ov_ref, dba_ref, gcw_out, gsm_out = outs
        below, gcw_ref, gsm_ref = scratch
        step = pl.program_id(0)

        @pl.when(step == 0)
        def _():
            gcw_ref[...] = jnp.zeros_like(gcw_ref)
            gsm_ref[...] = jnp.zeros_like(gsm_ref)
            below[...] = jnp.zeros_like(below)

        live = (tile_of(step) > 0).astype(F32)
        parts = ((q_ref, hq_ref, dqn_ref, oq_ref), (k_ref, hk_ref, dkn_ref, ok_ref), (v_ref, hv_ref, dvs_ref, ov_ref))
        for p, (x_ref, h_ref, d_ref, o_ref) in enumerate(parts):
            for h in HEADS:
                cs = HEAD_COLS[h]
                wide = slice(p * D_HALF + h * HEAD, p * D_HALF + (h + 1) * HEAD)
                cw = cw_ref[:, wide]
                taps = _conv_taps(x_ref[:, cs], h_ref[:, cs] * live)
                y = _conv_pre(taps, cw)
                sg = _sigmoid(y)
                sv = y * sg
                ds = d_ref[:, cs]
                if p < 2:
                    rn = lax.rsqrt(_rowsum(sv * sv) + EPS)
                    nrm = sv * rn
                    ds = rn * (ds - nrm * _rowsum(ds * nrm))
                dy = ds * (sg * (1.0 + y * (1.0 - sg)))
                for j in range(CONV_K):
                    gcw_ref[8 * j:8 * j + 8, wide] += _rows8(dy * taps[j])
                nxt = below[:, wide]
                acc = dy * cw[CONV_K - 1:CONV_K]
                for sft in range(1, CONV_K):
                    acc = acc + _shift_up(dy, nxt, sft) * cw[CONV_K - 1 - sft:CONV_K - sft]
                o_ref[:, cs] = acc.astype(BF16)
                below[:, wide] = dy[0:8]
                yield

        ba = ba_ref[...]
        lane = lax.broadcasted_iota(I32, (t, HEAD), 1)
        lane8 = lax.broadcasted_iota(I32, (8, HEAD), 1)
        dba = jnp.zeros((t, HEAD), F32)
        gsm = jnp.zeros((8, HEAD), F32)
        for h in HEADS:
            beta = _sigmoid(ba[:, h:h + 1])
            dbeta = dbeta_ref[:, h * HEAD:h * HEAD + 1]
            xg = ba[:, N_HEADS + h:N_HEADS + h + 1] + dtb_ref[0:1, h:h + 1]
            nexp = -jnp.exp(al_ref[0:1, h:h + 1])
            dgv = dg_ref[:, h * HEAD:h * HEAD + 1]
            da = dgv * nexp * _sigmoid(xg)
            dba = dba + jnp.where(lane == h, dbeta * beta * (1.0 - beta), 0.0) + jnp.where(lane == N_HEADS + h, da, 0.0)
            gsm = (gsm + jnp.where(lane8 == h, _rows8(dgv * nexp * _softplus(xg)), 0.0)
                   + jnp.where(lane8 == N_HEADS + h, _rows8(da), 0.0))
        dba_ref[...] = jnp.zeros_like(dba_ref)
        dba_ref[:, :HEAD] = dba.astype(BF16)
        gsm_ref[...] += gsm
        yield

        @pl.when(step == n_tiles - 1)
        def _():
            gcw_out[...] = jnp.zeros_like(gcw_out)
            for j in range(CONV_K):
                gcw_out[j:j + 1, :] = _colsum(gcw_ref[8 * j:8 * j + 8, :])
            gsm_out[...] = jnp.broadcast_to(_colsum(gsm_ref[...]), (8, HEAD))

    row = pl.BlockSpec((t, D_HALF), lambda i: (tile_of(i), 0))
    const = lambda shape: pl.BlockSpec(shape, lambda i: (0, 0))
    return dict(
        inputs=[proj] * 7 + [conv_w, a_log, dt_bias, dqn, dkn, dvs, dbeta, dg],
        in_specs=_conv_specs(t, tile_of) + [pl.BlockSpec((t, HEAD), lambda i: (tile_of(i), COL_BA // HEAD)),
                                            const((CONV_K, 3 * D_HALF)), const((1, N_HEADS)), const((1, N_HEADS))] + [row] * 5,
        out_specs=[row, row, row, row, const((8, 3 * D_HALF)), const((8, HEAD))],
        out_shape=[_sds((s, D_HALF), BF16)] * 4 + [_sds((8, 3 * D_HALF)), _sds((8, HEAD))],
        scratch=[pltpu.VMEM((8, 3 * D_HALF), F32), pltpu.VMEM((8 * CONV_K, 3 * D_HALF), F32), pltpu.VMEM((8, HEAD), F32)],
        stages=stages)


def _pool_fwd_part(proj, pool_w, pool_scale):
    s = proj.shape[0]
    t = POOL_T
    hb = t // HEAD

    def stages(ins, outs, scratch):
        u_ref, z_ref, halo_ref, pw_ref, ps_ref, band_ref = ins
        y_ref, = outs
        i = pl.program_id(0)
        live = (i > 0).astype(F32)
        groups = lambda ref: [ref[:, sl] for sl in HEAD_COLS]
        z = groups(z_ref)
        u, halo = groups(u_ref), [h * live for h in groups(halo_ref)]
        yield
        _, mixed, sg, _ = _pool_mix(u, halo, z, [pw_ref[g] for g in HEADS], [band_ref[g] for g in HEADS], i * t)
        yield
        for sl, m, zg, s_ in zip(HEAD_COLS, mixed, z, sg):
            y_ref[:, sl] = m * ps_ref[:, sl] * (zg * s_)
        yield

    const3 = lambda shape: pl.BlockSpec(shape, lambda i: (0, 0, 0))
    return dict(
        inputs=[proj, proj, proj, pool_w, pool_scale, _pool_bands(t)],
        in_specs=[pl.BlockSpec((t, D_HALF), lambda i: (i, 0)), pl.BlockSpec((t, D_HALF), lambda i: (i, 1)),
                  pl.BlockSpec((HEAD, D_HALF), lambda i: (jnp.maximum(i * hb - 1, 0), 0)),
                  const3((N_HEADS, HEAD, HEAD)), pl.BlockSpec((1, D_HALF), lambda i: (0, 0)), const3((N_HEADS, t, HEAD + t))],
        out_specs=[pl.BlockSpec((t, D_HALF), lambda i: (i, 0))], out_shape=[_sds((s, D_HALF))],
        scratch=[], stages=stages)


def _conv_fwd_part(proj, conv_w, a_log, dt_bias):
    s = proj.shape[0]
    t = CONV_T

    def stages(ins, outs, scratch):
        q_ref, k_ref, v_ref, hq_ref, hk_ref, hv_ref, ba_ref, cw_ref, al_ref, dtb_ref = ins
        qn_ref, kn_ref, vs_ref, beta_ref, g_ref = outs
        live = (pl.program_id(0) > 0).astype(F32)
        for p, (x_ref, h_ref, o_ref) in enumerate(((q_ref, hq_ref, qn_ref), (k_ref, hk_ref, kn_ref), (v_ref, hv_ref, vs_ref))):
            for h in HEADS:
                cs = HEAD_COLS[h]
                taps = _conv_taps(x_ref[:, cs], h_ref[:, cs] * live)
                y = _conv_pre(taps, cw_ref[:, p * D_HALF + h * HEAD:p * D_HALF + (h + 1) * HEAD])
                sv = y * _sigmoid(y)
                o_ref[:, cs] = sv if p == 2 else sv * lax.rsqrt(_rowsum(sv * sv) + EPS)
                yield
        ba = ba_ref[...]
        for h in HEADS:
            beta = _sigmoid(ba[:, h:h + 1])
            gl = -jnp.exp(al_ref[0:1, h:h + 1]) * _softplus(ba[:, N_HEADS + h:N_HEADS + h + 1] + dtb_ref[0:1, h:h + 1])
            beta_ref[:, HEAD_COLS[h]] = jnp.broadcast_to(beta, (t, HEAD))
            g_ref[:, HEAD_COLS[h]] = jnp.broadcast_to(gl, (t, HEAD))
        yield

    row = pl.BlockSpec((t, D_HALF), lambda i: (i, 0))
    const = lambda shape: pl.BlockSpec(shape, lambda i: (0, 0))
    return dict(
        inputs=[proj] * 7 + [conv_w, a_log, dt_bias],
        in_specs=_conv_specs(t) + [pl.BlockSpec((t, HEAD), lambda i: (i, COL_BA // HEAD)),
                                   const((CONV_K, 3 * D_HALF)), const((1, N_HEADS)), const((1, N_HEADS))],
        out_specs=[row] * 5, out_shape=[_sds((s, D_HALF))] * 5, scratch=[], stages=stages)


def _conv_pool_fwd(proj, conv_w, a_log, dt_bias, pool_w, pool_scale):
    assert POOL_T == CONV_T
    return _fused_call("conv_pool_fwd", proj.shape[0] // CONV_T, [
        _conv_fwd_part(proj, conv_w, a_log, dt_bias), _pool_fwd_part(proj, pool_w, pool_scale)])


def _conv_pool_bwd(proj, conv_w, a_log, dt_bias, dqn, dkn, dvs, dbeta, dg, dyp, pool_w, pool_scale):
    n_tiles = proj.shape[0] // CONV_T
    assert POOL_T == CONV_T
    tile_of = lambda i: n_tiles - 1 - i
    return _fused_call("conv_pool_bwd", n_tiles, [
        _conv_bwd_part(proj, conv_w, a_log, dt_bias, dqn, dkn, dvs, dbeta, dg, tile_of, n_tiles),
        _pool_bwd_part(proj, dyp, pool_w, pool_scale, tile_of, n_tiles)])


def _rows8(x):
    acc = x[0:8]
    for r in range(8, x.shape[0], 8):
        acc = acc + x[r:r + 8]
    return acc


def _conv_bwd(proj, conv_w, a_log, dt_bias, dqn, dkn, dvs, dbeta, dg):
    s = proj.shape[0]
    t = CONV_T
    n_tiles = s // t
    n_sub = t // CONV_SUB
    tile_of = lambda i: n_tiles - 1 - i

    def body(q_ref, k_ref, v_ref, hq_ref, hk_ref, hv_ref, ba_ref, cw_ref, al_ref, dtb_ref,
             dqn_ref, dkn_ref, dvs_ref, dbeta_ref, dg_ref, oq_ref, ok_ref, ov_ref, dba_ref, gcw_out, gsm_out,
             below, gcw_ref, gsm_ref):
        @pl.when(pl.program_id(0) == 0)
        def _():
            gcw_ref[...] = jnp.zeros_like(gcw_ref)
            gsm_ref[...] = jnp.zeros_like(gsm_ref)
            below[...] = jnp.zeros_like(below)

        live = (pl.program_id(0) < n_tiles - 1).astype(F32)
        parts = ((q_ref, hq_ref, dqn_ref, oq_ref), (k_ref, hk_ref, dkn_ref, ok_ref), (v_ref, hv_ref, dvs_ref, ov_ref))
        lane = lax.broadcasted_iota(I32, (CONV_SUB, HEAD), 1)
        lane8 = lax.broadcasted_iota(I32, (8, HEAD), 1)

        def sub_tile(r0, first):
            rows = pl.ds(r0, CONV_SUB)
            for p, (x_ref, h_ref, d_ref, o_ref) in enumerate(parts):
                for h in HEADS:
                    cs = HEAD_COLS[h]
                    wide = slice(p * D_HALF + h * HEAD, p * D_HALF + (h + 1) * HEAD)
                    cw = cw_ref[:, wide]
                    prev8 = h_ref[:, cs] * live if first else x_ref[pl.ds(r0 - 8, 8), cs]
                    taps = _conv_taps(x_ref[rows, cs], prev8)
                    y = _conv_pre(taps, cw)
                    sg = _sigmoid(y)
                    sv = y * sg
                    ds = d_ref[rows, cs]
                    if p < 2:
                        rn = lax.rsqrt(_rowsum(sv * sv) + EPS)
                        nrm = sv * rn
                        ds = rn * (ds - nrm * _rowsum(ds * nrm))
                    dy = ds * (sg * (1.0 + y * (1.0 - sg)))
                    for j in range(CONV_K):
                        gcw_ref[8 * j:8 * j + 8, wide] += _rows8(dy * taps[j])
                    nxt = below[:, wide]
                    acc = dy * cw[CONV_K - 1:CONV_K]
                    for sft in range(1, CONV_K):
                        acc = acc + _shift_up(dy, nxt, sft) * cw[CONV_K - 1 - sft:CONV_K - sft]
                    o_ref[rows, cs] = acc.astype(BF16)
                    below[:, wide] = dy[0:8]

            ba = ba_ref[rows, :]
            dba = jnp.zeros((CONV_SUB, HEAD), F32)
            gsm = jnp.zeros((8, HEAD), F32)
            for h in HEADS:
                beta = _sigmoid(ba[:, h:h + 1])
                dbeta = dbeta_ref[rows, h * HEAD:h * HEAD + 1]
                xg = ba[:, N_HEADS + h:N_HEADS + h + 1] + dtb_ref[0:1, h:h + 1]
                nexp = -jnp.exp(al_ref[0:1, h:h + 1])
                dgv = dg_ref[rows, h * HEAD:h * HEAD + 1]
                da = dgv * nexp * _sigmoid(xg)
                dba = dba + jnp.where(lane == h, dbeta * beta * (1.0 - beta), 0.0) + jnp.where(lane == N_HEADS + h, da, 0.0)
                gsm = (gsm + jnp.where(lane8 == h, _rows8(dgv * nexp * _softplus(xg)), 0.0)
                       + jnp.where(lane8 == N_HEADS + h, _rows8(da), 0.0))
            dba_ref[rows, :] = jnp.zeros((CONV_SUB, D_HALF), BF16)
            dba_ref[rows, :HEAD] = dba.astype(BF16)
            gsm_ref[...] += gsm

        def step(k, carry):
            sub_tile(pl.multiple_of((n_sub - 1 - k) * CONV_SUB, CONV_SUB), False)
            return carry

        lax.fori_loop(0, n_sub - 1, step, 0)
        sub_tile(0, True)

        @pl.when(pl.program_id(0) == n_tiles - 1)
        def _():
            gcw_out[...] = jnp.zeros_like(gcw_out)
            for j in range(CONV_K):
                gcw_out[j:j + 1, :] = _colsum(gcw_ref[8 * j:8 * j + 8, :])
            gsm_out[...] = jnp.broadcast_to(_colsum(gsm_ref[...]), (8, HEAD))

    row = pl.BlockSpec((t, D_HALF), lambda i: (tile_of(i), 0))
    const = lambda shape: pl.BlockSpec(shape, lambda i: (0, 0))
    return _call(
        body, name="conv_bwd", grid=(n_tiles,),
        in_specs=_conv_specs(t, tile_of) + [pl.BlockSpec((t, HEAD), lambda i: (tile_of(i), COL_BA // HEAD)),
                                            const((CONV_K, 3 * D_HALF)), const((1, N_HEADS)), const((1, N_HEADS))] + [row] * 5,
        out_specs=[row, row, row, row, const((8, 3 * D_HALF)), const((8, HEAD))],
        out_shape=[_sds((s, D_HALF), BF16)] * 4 + [_sds((8, 3 * D_HALF)), _sds((8, HEAD))],
        scratch_shapes=[pltpu.VMEM((8, 3 * D_HALF), F32), pltpu.VMEM((8 * CONV_K, 3 * D_HALF), F32),
                        pltpu.VMEM((8, HEAD), F32)],
        compiler_params=_params("arbitrary"),
    )(proj, proj, proj, proj, proj, proj, proj, conv_w, a_log, dt_bias, dqn, dkn, dvs, dbeta, dg)


def _conv_bwd_pre(proj, conv_w, a_log, dt_bias, dqn, dkn, dvs, dbeta, dg):
    s = proj.shape[0]
    t = CONV_T

    def body(q_ref, k_ref, v_ref, hq_ref, hk_ref, hv_ref, ba_ref, cw_ref, al_ref, dtb_ref,
             dqn_ref, dkn_ref, dvs_ref, dbeta_ref, dg_ref, dyq_ref, dyk_ref, dyv_ref, dba_ref, gcw_ref, gsm_ref):
        @pl.when(pl.program_id(0) == 0)
        def _():
            gcw_ref[...] = jnp.zeros_like(gcw_ref)
            gsm_ref[...] = jnp.zeros_like(gsm_ref)

        live = (pl.program_id(0) > 0).astype(F32)
        parts = ((q_ref, hq_ref, dqn_ref, dyq_ref), (k_ref, hk_ref, dkn_ref, dyk_ref), (v_ref, hv_ref, dvs_ref, dyv_ref))
        for p, (x_ref, h_ref, d_ref, dy_ref) in enumerate(parts):
            cols = slice(p * D_HALF, (p + 1) * D_HALF)
            taps = _conv_taps(x_ref[...], h_ref[...] * live)
            y = _conv_pre(taps, cw_ref[:, cols])
            sg = _sigmoid(y)
            sv = y * sg
            if p == 2:
                ds = d_ref[...]
            else:
                segs = []
                for h in HEADS:
                    seg = _head(sv, h)
                    rn = lax.rsqrt(_rowsum(seg * seg) + EPS)
                    nrm = seg * rn
                    dn = d_ref[:, HEAD_COLS[h]]
                    segs.append(rn * (dn - nrm * _rowsum(dn * nrm)))
                ds = jnp.concatenate(segs, axis=1)
            dy = ds * (sg * (1.0 + y * (1.0 - sg)))
            dy_ref[...] = dy
            for j in range(CONV_K):
                gcw_ref[j:j + 1, cols] += _colsum(dy * taps[j])

        ba = ba_ref[...]
        lane = lax.broadcasted_iota(I32, (t, HEAD), 1)
        lane1 = lax.broadcasted_iota(I32, (1, HEAD), 1)
        dba = jnp.zeros((t, HEAD), F32)
        gsm = jnp.zeros((1, HEAD), F32)
        for h in HEADS:
            beta = _sigmoid(ba[:, h:h + 1])
            dbeta = dbeta_ref[:, h * HEAD:h * HEAD + 1]
            xg = ba[:, N_HEADS + h:N_HEADS + h + 1] + dtb_ref[0:1, h:h + 1]
            nexp = -jnp.exp(al_ref[0:1, h:h + 1])
            dgv = dg_ref[:, h * HEAD:h * HEAD + 1]
            da = dgv * nexp * _sigmoid(xg)
            dba = dba + jnp.where(lane == h, dbeta * beta * (1.0 - beta), 0.0) + jnp.where(lane == N_HEADS + h, da, 0.0)
            gsm = (gsm + jnp.where(lane1 == h, _colsum(dgv * nexp * _softplus(xg)), 0.0)
                   + jnp.where(lane1 == N_HEADS + h, _colsum(da), 0.0))
        dba_ref[...] = jnp.zeros_like(dba_ref)
        dba_ref[:, :HEAD] = dba.astype(BF16)
        gsm_ref[0:1, :] += gsm

    row = pl.BlockSpec((t, D_HALF), lambda i: (i, 0))
    return _call(
        body, name="conv_bwd_pre", grid=(s // t,),
        in_specs=_conv_specs(t) + [pl.BlockSpec((t, HEAD), lambda i: (i, COL_BA // HEAD)),
                                   pl.BlockSpec((CONV_K, 3 * D_HALF), lambda i: (0, 0)),
                                   pl.BlockSpec((1, N_HEADS), lambda i: (0, 0)),
                                   pl.BlockSpec((1, N_HEADS), lambda i: (0, 0))] + [row] * 5,
        out_specs=[row, row, row, row,
                   pl.BlockSpec((8, 3 * D_HALF), lambda i: (0, 0)), pl.BlockSpec((8, HEAD), lambda i: (0, 0))],
        out_shape=[_sds((s, D_HALF))] * 3 + [_sds((s, D_HALF), BF16), _sds((8, 3 * D_HALF)), _sds((8, HEAD))],
        compiler_params=_params("arbitrary"),
    )(proj, proj, proj, proj, proj, proj, proj, conv_w, a_log, dt_bias, dqn, dkn, dvs, dbeta, dg)


def _conv_bwd_in(dyq, dyk, dyv, conv_w):
    s = dyq.shape[0]
    t = CONV_T
    last = s // 8 - 1

    def body(q_ref, k_ref, v_ref, nq_ref, nk_ref, nv_ref, cw_ref, oq_ref, ok_ref, ov_ref):
        more = (pl.program_id(0) < pl.num_programs(0) - 1).astype(F32)
        for p, (d_ref, n_ref, o_ref) in enumerate(((q_ref, nq_ref, oq_ref), (k_ref, nk_ref, ok_ref), (v_ref, nv_ref, ov_ref))):
            cw = cw_ref[:, p * D_HALF:(p + 1) * D_HALF]
            dy = d_ref[...]
            nxt = n_ref[...] * more
            acc = dy * cw[3:4]
            for sft in (1, 2, 3):
                acc = acc + _shift_up(dy, nxt, sft) * cw[3 - sft:4 - sft]
            o_ref[...] = acc.astype(BF16)

    row = pl.BlockSpec((t, D_HALF), lambda i: (i, 0))
    nxt = pl.BlockSpec((8, D_HALF), lambda i: (jnp.minimum((i + 1) * (t // 8), last), 0))
    return _call(
        body, name="conv_bwd_in", grid=(s // t,),
        in_specs=[row] * 3 + [nxt] * 3 + [pl.BlockSpec((CONV_K, 3 * D_HALF), lambda i: (0, 0))],
        out_specs=[row] * 3, out_shape=[_sds((s, D_HALF), BF16)] * 3,
        compiler_params=_params("arbitrary"),
    )(dyq, dyk, dyv, dyq, dyk, dyv, conv_w)


IN_T = 512


def _in_bwd(x, dh, norm_w, w_pad, pieces):
    s = x.shape[0]
    t = IN_T
    widths = [D_HALF] * 6 + [N_IN_PAD - COL_BA]

    def body(*refs):
        x_ref, dh_ref, nw_ref, w_ref = refs[:4]
        p_refs = refs[4:4 + len(pieces)]
        gx_ref, gnw_ref = refs[4 + len(pieces):]

        @pl.when(pl.program_id(0) == 0)
        def _():
            gnw_ref[...] = jnp.zeros_like(gnw_ref)

        dn = jnp.zeros((t, D_MODEL), F32)
        col = 0
        for p_ref, wd in zip(p_refs, widths):
            dn = dn + _bdot_nt(p_ref[...], w_ref[:, col:col + wd])
            col += wd
        xv = x_ref[...]
        r = lax.rsqrt(jnp.mean(xv * xv, axis=-1, keepdims=True) + EPS)
        xhat = xv * r
        gnw_ref[...] += _colsum(dn * xhat)
        dxh = dn * nw_ref[...]
        gx_ref[...] = dh_ref[...] + r * (dxh - xhat * jnp.mean(dxh * xhat, axis=-1, keepdims=True))

    wide = pl.BlockSpec((t, D_MODEL), lambda i: (i, 0))
    return _call(
        body, name="in_bwd", grid=(s // t,),
        in_specs=[wide, wide, pl.BlockSpec((1, D_MODEL), lambda i: (0, 0)),
                  pl.BlockSpec((D_MODEL, N_IN_PAD), lambda i: (0, 0))]
                 + [pl.BlockSpec((t, wd), lambda i: (i, 0)) for wd in widths],
        out_specs=[wide, pl.BlockSpec((1, D_MODEL), lambda i: (0, 0))],
        out_shape=[_sds((s, D_MODEL)), _sds((1, D_MODEL))],
        compiler_params=_params("arbitrary"),
    )(x, dh, norm_w, w_pad, *pieces)


def _adamw_shard(name, w, g_own, g_got, cidx, m, v):
    _, r, c = w.shape
    half = r // 2
    rows = 256 if half % 256 == 0 else half
    per_half = half // rows

    def body(c_ref, w_ref, go_ref, gg_ref, m_ref, v_ref, gout_ref, d_ref, nm_ref, nv_ref):
        mine = (pl.program_id(0) // per_half) == c_ref[0]
        gv = jnp.where(mine, go_ref[:, :c], gg_ref[:, :c])
        gout_ref[0] = gv
        mn = ADAM_B1 * m_ref[0] + (1.0 - ADAM_B1) * gv
        vn = ADAM_B2 * v_ref[0] + (1.0 - ADAM_B2) * (gv * gv)
        m_hat = mn / (1.0 - ADAM_B1 ** ADAM_STEP)
        v_hat = vn / (1.0 - ADAM_B2 ** ADAM_STEP)
        d_ref[0] = -ADAM_LR * (m_hat / (jnp.sqrt(v_hat) + ADAM_EPS) + ADAM_WD * w_ref[0])
        nm_ref[0] = mn
        nv_ref[0] = vn

    blk = pl.BlockSpec((1, rows, c), lambda i, c_ref: (0, i, 0))
    gblk = pl.BlockSpec((rows, g_own.shape[1]), lambda i, c_ref: (i % per_half, 0))
    return _call(
        body, name=name,
        grid_spec=pltpu.PrefetchScalarGridSpec(
            num_scalar_prefetch=1, grid=(2 * per_half,),
            in_specs=[blk, gblk, gblk, blk, blk], out_specs=[blk] * 4),
        out_shape=[_sds((1, r, c))] * 4,
        compiler_params=_params("arbitrary"),
    )(cidx, w, g_own, g_got, m, v)


def _adamw_tiles(name, w, g, m, v):
    n = w.shape[0]
    nb = 77 if n % 77 == 0 else n

    def body(w_ref, g_ref, m_ref, v_ref, d_ref, nm_ref, nv_ref):
        gv = g_ref[...]
        mn = ADAM_B1 * m_ref[...] + (1.0 - ADAM_B1) * gv
        vn = ADAM_B2 * v_ref[...] + (1.0 - ADAM_B2) * (gv * gv)
        m_hat = mn / (1.0 - ADAM_B1 ** ADAM_STEP)
        v_hat = vn / (1.0 - ADAM_B2 ** ADAM_STEP)
        d_ref[...] = -ADAM_LR * (m_hat / (jnp.sqrt(v_hat) + ADAM_EPS) + ADAM_WD * w_ref[...])
        nm_ref[...] = mn
        nv_ref[...] = vn

    blk = pl.BlockSpec((nb, 8, HEAD), lambda i: (i, 0, 0))
    return _call(
        body, name=name, grid=(n // nb,),
        in_specs=[blk] * 4, out_specs=[blk] * 3, out_shape=[_sds(w.shape)] * 3,
        compiler_params=_params("arbitrary"),
    )(w, g, m, v)


def _exchange(name, inputs, out_shapes, phases):
    n_in = len(inputs)
    n_out = len(out_shapes)
    n_cp = sum(len(p) for p in phases)

    def body(*refs):
        ins, outs = refs[:n_in], refs[n_in:n_in + n_out]
        send, recv = refs[n_in + n_out:]
        pos = (lax.axis_index("x"), lax.axis_index("y"), lax.axis_index("c"))
        k = 0
        for phase in phases:
            cps = []
            for src, dst, target in phase:
                cps.append(pltpu.make_async_remote_copy(
                    src_ref=src(ins, outs, pos), dst_ref=dst(ins, outs, pos), send_sem=send.at[k], recv_sem=recv.at[k],
                    device_id=target(pos), device_id_type=pl.DeviceIdType.MESH))
                k += 1
            for cp in cps:
                cp.start()
            for cp in cps:
                cp.wait()

    anyspec = pl.BlockSpec(memory_space=pl.ANY)
    return _call(
        body, name=name,
        in_specs=[anyspec] * n_in, out_specs=[anyspec] * n_out, out_shape=list(out_shapes),
        scratch_shapes=[pltpu.SemaphoreType.DMA((n_cp,)), pltpu.SemaphoreType.DMA((n_cp,))],
    )(*inputs)


def _exchange_start(name, inputs, out_shapes, copies):
    n_in, n_out, n_cp = len(inputs), len(out_shapes), len(copies)

    def body(*refs):
        ins, lands = refs[:n_in], refs[n_in:n_in + n_out]
        sems = refs[n_in + n_out:n_in + n_out + 2 * n_cp]
        token = refs[-1]
        pos = (lax.axis_index("x"), lax.axis_index("y"), lax.axis_index("c"))
        for k, (src, dst, target) in enumerate(copies):
            pltpu.make_async_remote_copy(
                src_ref=src(ins, lands, pos), dst_ref=dst(ins, lands, pos), send_sem=sems[2 * k], recv_sem=sems[2 * k + 1],
                device_id=target(pos), device_id_type=pl.DeviceIdType.MESH).start()
        token[...] = jnp.zeros_like(token)

    hbm = pl.BlockSpec(memory_space=pltpu.HBM)
    sem = pl.BlockSpec(memory_space=pltpu.SEMAPHORE)
    bufs = list(inputs) + [lax.empty(o.shape, o.dtype) for o in out_shapes]
    outs = _call(
        body, name=name,
        out_shape=tuple([pltpu.SemaphoreType.DMA(())] * (2 * n_cp) + [pltpu.HBM(b.shape, b.dtype) for b in bufs]
                        + [_sds((8, HEAD))]),
        in_specs=[hbm] * len(bufs),
        out_specs=tuple([sem] * (2 * n_cp) + [hbm] * len(bufs) + [pl.BlockSpec(memory_space=pltpu.VMEM)]),
        input_output_aliases={i: 2 * n_cp + i for i in range(len(bufs))},
        compiler_params=pltpu.CompilerParams(has_side_effects=pltpu.SideEffectType.DATAFLOW_SIDE_EFFECTING),
    )(*[pltpu.with_memory_space_constraint(b, pltpu.HBM) for b in bufs])
    return outs[:2 * n_cp], outs[2 * n_cp:2 * n_cp + n_in], outs[2 * n_cp + n_in:-1], outs[-1]


def _exchange_wait(name, sems, sources, lands, copies, after):
    n_in, n_out, n_cp = len(sources), len(lands), len(copies)

    def body(*refs):
        ins, zones = refs[:n_in], refs[n_in:n_in + n_out]
        sem_refs = refs[n_in + n_out:n_in + n_out + 2 * n_cp]
        pos = (lax.axis_index("x"), lax.axis_index("y"), lax.axis_index("c"))
        for k, (src, dst, target) in enumerate(copies):
            cp = pltpu.make_async_remote_copy(
                src_ref=src(ins, zones, pos), dst_ref=dst(ins, zones, pos), send_sem=sem_refs[2 * k],
                recv_sem=sem_refs[2 * k + 1], device_id=target(pos), device_id_type=pl.DeviceIdType.MESH)
            cp.wait_send()
            cp.wait_recv()

    hbm = pl.BlockSpec(memory_space=pltpu.HBM)
    sem = pl.BlockSpec(memory_space=pltpu.SEMAPHORE)
    bufs = list(sources) + list(lands)
    outs = _call(
        body, name=name,
        out_shape=tuple(pltpu.HBM(b.shape, b.dtype) for b in bufs),
        in_specs=[hbm] * len(bufs) + [sem] * (2 * n_cp) + [pl.BlockSpec(memory_space=pl.ANY)],
        out_specs=tuple([hbm] * len(bufs)),
        input_output_aliases={i: i for i in range(len(bufs))},
        compiler_params=pltpu.CompilerParams(has_side_effects=pltpu.SideEffectType.DATAFLOW_SIDE_EFFECTING),
    )(*bufs, *sems, after)
    return outs[:n_in], outs[n_in:]


def _allreduce_tile(name, v):
    def body(v_ref, out_ref, slots, send, recv):
        x, y, c = lax.axis_index("x"), lax.axis_index("y"), lax.axis_index("c")
        me = 4 * x + 2 * y + c
        slots[me] = v_ref[...]
        cps = []
        for k in range(1, 8):
            peer = (x ^ (k >> 2), y ^ ((k >> 1) & 1), c ^ (k & 1))
            cps.append(pltpu.make_async_remote_copy(
                src_ref=v_ref, dst_ref=slots.at[me], send_sem=send.at[k - 1], recv_sem=recv.at[k - 1],
                device_id=peer, device_id_type=pl.DeviceIdType.MESH))
        for cp in cps:
            cp.start()
        for cp in cps:
            cp.wait()
        acc = slots[0]
        for i in range(1, 8):
            acc = acc + slots[i]
        out_ref[...] = acc

    vm = pl.BlockSpec(memory_space=pltpu.VMEM)
    return _call(
        body, name=name, in_specs=[vm], out_specs=vm, out_shape=_sds(v.shape),
        scratch_shapes=[pltpu.VMEM((8,) + v.shape, F32), pltpu.SemaphoreType.DMA((7,)), pltpu.SemaphoreType.DMA((7,))],
    )(v)


def _chip(pos):
    return 2 * pos[0] + pos[1]


def _other_chip(pos, mask):
    x, y, c = pos
    return (x ^ (mask >> 1), y ^ (mask & 1), c)


def _sibling(pos):
    return (pos[0], pos[1], 1 - pos[2])


def _gather_weights(wb, cb):
    rows = wb.shape[0] // 2
    x_nb, y_nb, diag = CHIP_MASKS

    def part(pos, mask, quarter=None):
        start = pos[2] * rows if quarter is None else pos[2] * rows + quarter * (rows // 2)
        return lambda outs: outs[0].at[_chip(pos) ^ mask, pl.ds(start, rows if quarter is None else rows // 2)]

    def passed_on(mask, to, quarter=None):
        return (lambda ins, outs, pos: part(pos, mask, quarter)(outs), lambda ins, outs, pos: part(pos, mask, quarter)(outs), to)

    first = [(lambda ins, outs, pos: ins[0].at[pl.ds(pos[2] * rows, rows)], lambda ins, outs, pos: part(pos, 0)(outs),
              functools.partial(_other_chip, mask=mask)) for mask in (x_nb, y_nb)]
    first += [(lambda ins, outs, pos: ins[1], lambda ins, outs, pos: outs[1].at[_chip(pos)],
               functools.partial(_other_chip, mask=mask)) for mask in CHIP_MASKS]
    second = [passed_on(x_nb, functools.partial(_other_chip, mask=y_nb), quarter=0),
              passed_on(y_nb, functools.partial(_other_chip, mask=x_nb), quarter=1),
              passed_on(x_nb, _sibling), passed_on(y_nb, _sibling)]
    third = [passed_on(diag, _sibling)]
    return _exchange("gather_weights", [wb, cb], [_sds((4,) + wb.shape, wb.dtype), _sds((4,) + cb.shape, cb.dtype)],
                     [first, second, third])


def _assemble_w_in(gw, wb, jidx):
    m = gw.shape[1]

    def body(j_ref, g_ref, wb_ref, o_ref):
        step = pl.program_id(0)

        @pl.when(step == 0)
        def _():
            o_ref[...] = jnp.zeros_like(o_ref)

        blk = jnp.where(step == j_ref[0], wb_ref[...], g_ref[0]).astype(F32)
        lane = lax.broadcasted_iota(I32, (m, BLK_IN_PAD), 1)
        for j in range(4):
            @pl.when(step == j)
            def _(j=j):
                base = j * BLK_IN // HEAD * HEAD
                shift = j * BLK_IN - base
                moved = pltpu.roll(blk, shift, 1) if shift else blk
                window = o_ref[:, base:base + BLK_IN_PAD].astype(F32)
                mine = (lane >= shift) & (lane < shift + BLK_IN)
                o_ref[:, base:base + BLK_IN_PAD] = jnp.where(mine, moved, window).astype(BF16)

    return _call(
        body, name="assemble_w_in",
        grid_spec=pltpu.PrefetchScalarGridSpec(
            num_scalar_prefetch=1, grid=(4,),
            in_specs=[pl.BlockSpec((1, m, BLK_IN_PAD), lambda j, j_ref: (j, 0, 0)),
                      pl.BlockSpec((m, BLK_IN_PAD), lambda j, j_ref: (0, 0))],
            out_specs=pl.BlockSpec((m, N_IN_PAD), lambda j, j_ref: (0, 0))),
        out_shape=_sds((m, N_IN_PAD), BF16),
        compiler_params=_params("arbitrary"),
    )(jidx, gw, wb)


def _gather_blocks(ob):
    copies = [(lambda ins, outs, pos: ins[0], lambda ins, outs, pos: outs[0].at[_chip(pos)],
               functools.partial(_other_chip, mask=mask)) for mask in CHIP_MASKS]
    return [_sds((4,) + ob.shape, ob.dtype)], copies


def _to_sibling_half(name, arrays):
    def src(ins, outs, pos, a):
        h = arrays[a].shape[-2] // 2
        sl = pl.ds((1 - pos[2]) * h, h)
        return ins[a].at[:, sl] if arrays[a].ndim == 3 else ins[a].at[sl]

    outs = [_sds(a.shape[:-2] + (a.shape[-2] // 2, a.shape[-1]), a.dtype) for a in arrays]
    phase = [(functools.partial(src, a=a), lambda ins, outs, pos, a=a: outs[a], _sibling) for a in range(len(arrays))]
    return _exchange(name, arrays, outs, [phase])


def _add_half(name, full, part, cidx):
    shape = part.shape
    lead = shape[0] if len(shape) == 3 else 1
    rows, cols = shape[-2], shape[-1]
    tr = rows
    nr = rows // tr
    f3 = full.reshape((lead,) + full.shape[-2:])
    p3 = part.reshape((lead, rows, cols))

    def body(c_ref, f_ref, p_ref, o_ref):
        o_ref[...] = (f_ref[...].astype(F32) + p_ref[...].astype(F32)).astype(o_ref.dtype)

    out = _call(
        body, name=name,
        grid_spec=pltpu.PrefetchScalarGridSpec(
            num_scalar_prefetch=1, grid=(lead, nr),
            in_specs=[pl.BlockSpec((1, tr, cols), lambda b, r, c_ref: (b, c_ref[0] * nr + r, 0)),
                      pl.BlockSpec((1, tr, cols), lambda b, r, c_ref: (b, r, 0))],
            out_specs=pl.BlockSpec((1, tr, cols), lambda b, r, c_ref: (b, r, 0))),
        out_shape=_sds((lead, rows, cols), part.dtype),
        compiler_params=_params("arbitrary", "arbitrary"),
    )(cidx, f3, p3)
    return out.reshape(shape)


def _to_other_chips(arrays, blocked):
    def src(ins, outs, pos, a, mask):
        return ins[a].at[_chip(pos) ^ mask] if blocked[a] else ins[a]

    outs = [_sds((3,) + (a.shape[1:] if b else a.shape), a.dtype) for a, b in zip(arrays, blocked)]
    copies = []
    for mi, mask in enumerate(CHIP_MASKS):
        for a in range(len(arrays)):
            copies.append((functools.partial(src, a=a, mask=mask), lambda ins, outs, pos, a=a, mi=mi: outs[a].at[mi],
                           functools.partial(_other_chip, mask=mask)))
    return outs, copies


def _add_chips(name, own, got, jidx, blocked):
    rows, cols = got.shape[-2:]
    tr = rows
    nr = rows // tr
    o3 = own if blocked else own.reshape((1, rows, cols))

    def body(j_ref, o_ref, g_ref, out_ref):
        out_ref[...] = ((o_ref[0].astype(F32) + g_ref[0].astype(F32))
                        + (g_ref[1].astype(F32) + g_ref[2].astype(F32)))

    own_map = (lambda r, j_ref: (j_ref[0], r, 0)) if blocked else (lambda r, j_ref: (0, r, 0))
    return _call(
        body, name=name,
        grid_spec=pltpu.PrefetchScalarGridSpec(
            num_scalar_prefetch=1, grid=(nr,),
            in_specs=[pl.BlockSpec((1, tr, cols), own_map),
                      pl.BlockSpec((3, tr, cols), lambda r, j_ref: (0, r, 0))],
            out_specs=pl.BlockSpec((tr, cols), lambda r, j_ref: (r, 0))),
        out_shape=_sds((rows, cols)),
        compiler_params=_params("arbitrary"),
    )(jidx, o3, got)


def _to_sibling(name, arrays):
    phase = [(lambda ins, outs, pos, a=a: ins[a], lambda ins, outs, pos, a=a: outs[a], _sibling)
             for a in range(len(arrays))]
    return _exchange(name, arrays, [_sds(a.shape, a.dtype) for a in arrays], [phase])


def _local_step(x, target, w_pad, w_out, conv_w, norm_w, pool_w, pool_scale, a_log, dt_bias, dn_norm_w, final_norm_w):
    proj, n_t = _proj_fwd(x, norm_w, w_pad)
    (qn, kn, vs, beta, g), (y_pool,) = _conv_pool_fwd(proj, conv_w, a_log, dt_bias, pool_w, pool_scale)
    w, att, qd, kd, tm, cd, o, vn, st = _delta_fwd(qn, kn, vs, beta, g)
    w_out = w_out(o) if callable(w_out) else w_out
    g_wout, dh, dyp, do, ddz, loss, g_fnw, g_dnw = _out_fwd_bwd(x, y_pool, o, proj, target, w_out, dn_norm_w, final_norm_w)
    dqn, dkn, dvs, dbeta, dg = _delta_bwd(do, vn, qd, kd, w, att, cd, st, qn, kn, vs, beta, g, tm)
    (dcq, dck, dcv, dba, g_cw, g_sm), (dpu, dpz, g_pw, g_ps) = _conv_pool_bwd(
        proj, conv_w, a_log, dt_bias, dqn, dkn, dvs, dbeta, dg, dyp, pool_w, pool_scale)
    pieces = [dpu, dpz, dcq, dck, dcv, ddz, dba]
    g_win = _grad_w_in(n_t, pieces)
    small = dict(norm_w=jnp.zeros_like(norm_w), pool_w=g_pw, pool_scale=g_ps, conv_w=g_cw[:CONV_K],
                 a_log=g_sm[0:1, 0:N_HEADS], dt_bias=g_sm[0:1, N_HEADS:2 * N_HEADS], dn_norm_w=g_dnw, final_norm_w=g_fnw)
    return loss[0, 0], g_win, g_wout, small, dh, pieces


SMALL_LAYOUT = (("pool_w", 512, HEAD, (1, N_HEADS, HEAD, HEAD)), ("final_norm_w", 8, HEAD, (D_MODEL,)),
                ("pool_scale", 4, HEAD, (1, D_HALF)), ("conv_w", 48, HEAD, (1, CONV_K, 3 * D_HALF)),
                ("dn_norm_w", 1, HEAD, (1, HEAD)), ("a_log", 1, N_HEADS, (1, N_HEADS)), ("dt_bias", 1, N_HEADS, (1, N_HEADS)),
                ("loss", 1, 1, ()))


def _small_offsets():
    offs, r = {}, 0
    for name, rows, _, _ in SMALL_LAYOUT:
        offs[name] = r
        r += -(-rows // 8) * 8
    assert r <= SMALL_ROWS
    return offs


def _pack_small(t):
    parts = []
    for name, rows, lanes, _ in SMALL_LAYOUT:
        a = t.get(name, jnp.zeros((1,), F32)).reshape(rows, lanes)
        parts.append(jnp.pad(a, ((0, -(-rows // 8) * 8 - rows), (0, HEAD - lanes))))
    buf = jnp.concatenate(parts, axis=0)
    return jnp.pad(buf, ((0, SMALL_ROWS - buf.shape[0]), (0, 0)))


def _adamw_small(w, g_own, g_got, cidx, m, v):
    offs = _small_offsets()
    names = [e[0] for e in SMALL_LAYOUT]
    n = len(names)

    def body(c_ref, w_ref, go_ref, gg_ref, m_ref, v_ref, *outs):
        own_low = c_ref[0] == 0
        gv = jnp.concatenate([jnp.where(own_low, go_ref[...], gg_ref[...]), jnp.where(own_low, gg_ref[...], go_ref[...])], axis=0)
        mn = ADAM_B1 * m_ref[...] + (1.0 - ADAM_B1) * gv
        vn = ADAM_B2 * v_ref[...] + (1.0 - ADAM_B2) * (gv * gv)
        m_hat = mn / (1.0 - ADAM_B1 ** ADAM_STEP)
        v_hat = vn / (1.0 - ADAM_B2 ** ADAM_STEP)
        dl = -ADAM_LR * (m_hat / (jnp.sqrt(v_hat) + ADAM_EPS) + ADAM_WD * w_ref[...])
        for kind, arr in enumerate((gv, dl, mn, vn)):
            for i, (name, rows, lanes, _) in enumerate(SMALL_LAYOUT):
                outs[kind * n + i][...] = arr[offs[name]:offs[name] + rows, :lanes]

    whole = lambda shape: pl.BlockSpec(shape, lambda i, c_ref: (0,) * len(shape))
    out_shapes = [_sds((rows, lanes)) for _, rows, lanes, _ in SMALL_LAYOUT] * 4
    res = _call(
        body, name="adamw_small",
        grid_spec=pltpu.PrefetchScalarGridSpec(
            num_scalar_prefetch=1, grid=(1,),
            in_specs=[whole(w.shape), whole(g_own.shape), whole(g_got.shape), whole(m.shape), whole(v.shape)],
            out_specs=[whole(o.shape) for o in out_shapes]),
        out_shape=out_shapes,
        compiler_params=_params("arbitrary"),
    )(cidx, w, g_own, g_got, m, v)
    return [{name: res[kind * n + i].reshape(shape) for i, (name, _, _, shape) in enumerate(SMALL_LAYOUT)}
            for kind in range(4)]


def kernel(x, norm_w, w_in, pool_w, pool_scale, conv_w, a_log, dt_bias, dn_norm_w, w_out, final_norm_w, loss_target, m_norm_w, m_w_in, m_pool_w, m_pool_scale, m_conv_w, m_a_log, m_dt_bias, m_dn_norm_w, m_w_out, m_final_norm_w, v_norm_w, v_w_in, v_pool_w, v_pool_scale, v_conv_w, v_a_log, v_dt_bias, v_dn_norm_w, v_w_out, v_final_norm_w):
    cidx = lax.axis_index("c").astype(I32).reshape(1)
    jidx = (2 * lax.axis_index("x") + lax.axis_index("y")).astype(I32)

    wb = jnp.pad(w_in[0].astype(BF16), ((0, 0), (0, BLK_IN_PAD - BLK_IN)))
    ob = w_out[0].astype(BF16)
    gw, gc = _gather_weights(wb, conv_w[0])
    mine = lambda j: jidx == j
    w_pad = _assemble_w_in(gw, wb, jidx.reshape(1))
    cw_full = jnp.concatenate([jnp.where(mine(j), conv_w[0], gc[j]) for j in range(4)], axis=1)

    lands_o, copies_o = _gather_blocks(ob)
    sems_o, ob_thru, zones_o, token_o = _exchange_start("gather_w_out_start", [ob], lands_o, copies_o)

    def w_out_full(after):
        (own,), (got,) = _exchange_wait("gather_w_out_wait", sems_o, ob_thru, zones_o, copies_o, after)
        return jnp.where((jnp.arange(4) == jidx)[:, None, None], own[None], got).reshape(D_MODEL, D_MODEL)

    loss, g_win, g_wout, small, dh, pieces = _local_step(
        x[0], loss_target[0], w_pad, w_out_full, cw_full, norm_w + token_o[0, 0], pool_w[0], pool_scale, a_log, dt_bias,
        dn_norm_w, final_norm_w.reshape(1, D_MODEL))
    small["loss"] = loss

    blocks_out = g_wout.reshape(4, BLK_OUT, D_MODEL)
    full = [g_win, blocks_out, _pack_small(small)]
    from_sib = _to_sibling_half("reduce_sibling", full)
    chip_sum = [_add_half("add_sibling_%d" % i, f, p, cidx) for i, (f, p) in enumerate(zip(full, from_sib))]
    blocked = [True, True, False]
    lands, copies = _to_other_chips(chip_sum, blocked)
    sems, chip_sum, zones, token = _exchange_start("reduce_chips_start", chip_sum, lands, copies)
    gx, g_nw = _in_bwd(x[0], dh, norm_w + token[0, 0], w_pad, pieces)
    g_nw = _allreduce_tile("reduce_norm_w", g_nw.reshape(8, HEAD)).reshape(1, D_MODEL)
    chip_sum, from_chips = _exchange_wait("reduce_chips_wait", sems, chip_sum, zones, copies, gx)
    halves = [_add_chips("add_chips_%d" % i, o, g, jidx.reshape(1), b)
              for i, (o, g, b) in enumerate(zip(chip_sum, from_chips, blocked))]
    other_halves = _to_sibling("swap_halves", halves)

    weights = dict(norm_w=norm_w, w_in=w_in, pool_w=pool_w, pool_scale=pool_scale, conv_w=conv_w, a_log=a_log,
                   dt_bias=dt_bias, dn_norm_w=dn_norm_w, w_out=w_out, final_norm_w=final_norm_w)
    ms = dict(norm_w=m_norm_w, w_in=m_w_in, pool_w=m_pool_w, pool_scale=m_pool_scale, conv_w=m_conv_w, a_log=m_a_log,
              dt_bias=m_dt_bias, dn_norm_w=m_dn_norm_w, w_out=m_w_out, final_norm_w=m_final_norm_w)
    vs = dict(norm_w=v_norm_w, w_in=v_w_in, pool_w=v_pool_w, pool_scale=v_pool_scale, conv_w=v_conv_w, a_log=v_a_log,
              dt_bias=v_dt_bias, dn_norm_w=v_dn_norm_w, w_out=v_w_out, final_norm_w=v_final_norm_w)
    names = ["norm_w", "w_in", "pool_w", "pool_scale", "conv_w", "a_log", "dt_bias", "dn_norm_w", "w_out", "final_norm_w"]
    small_names = [n for n in names if n not in ("w_in", "w_out")]

    def pack(t):
        conv = lax.dynamic_update_slice_in_dim(jnp.zeros((CONV_K, 3 * D_HALF), F32), t["conv_w"][0], jidx * BLK_CONV, axis=1)
        return _pack_small({**{n: t[n] for n in small_names if n != "conv_w"}, "conv_w": conv})

    results = [{}, {}, {}, {}]
    to_tiles = lambda a: jnp.transpose(a, (2, 0, 1)).reshape(BLK_IN, 8, HEAD)
    from_tiles = lambda a: jnp.transpose(a, (1, 2, 0)).reshape(1, D_MODEL, BLK_IN)
    lo = jnp.where(cidx[0] == 0, halves[0], other_halves[0])
    hi = jnp.where(cidx[0] == 0, other_halves[0], halves[0])
    g_tiles = jnp.concatenate([lo[:, :BLK_IN].T, hi[:, :BLK_IN].T], axis=1).reshape(BLK_IN, 8, HEAD)
    outs = _adamw_tiles("adamw_w_in", to_tiles(w_in), g_tiles, to_tiles(m_w_in), to_tiles(v_w_in))
    for res, o in zip(results, (g_tiles,) + tuple(outs)):
        res["w_in"] = from_tiles(o)
    outs = _adamw_shard("adamw_w_out", w_out, halves[1], other_halves[1], cidx, m_w_out, v_w_out)
    for res, o in zip(results, outs):
        res["w_out"] = o
    outs = _adamw_small(pack(weights), halves[2], other_halves[2], cidx, pack(ms), pack(vs))
    for res, got in zip(results, outs):
        got["conv_w"] = lax.dynamic_slice_in_dim(got["conv_w"], jidx * BLK_CONV, BLK_CONV, axis=2)
        res.update(got)
    one_tile = lambda a: a.reshape(1, 8, HEAD)
    outs = _adamw_tiles("adamw_norm_w", one_tile(norm_w), one_tile(g_nw), one_tile(m_norm_w), one_tile(v_norm_w))
    for res, o in zip(results, (g_nw,) + tuple(outs)):
        res["norm_w"] = o.reshape(1, D_MODEL)
    grads, delta, new_m, new_v = results

    return (grads["loss"], gx[None], *[grads[n] for n in names], *[delta[n] for n in names],
            *[new_m[n] for n in names], *[new_v[n] for n in names])
```

```python
import functools

import jax
import jax.numpy as jnp
import numpy as np
from jax import lax
from jax.experimental import pallas as pl
from jax.experimental.pallas import tpu as pltpu

F32 = jnp.float32
BF16 = jnp.bfloat16
I32 = jnp.int32

D_MODEL = 1024
D_HALF = 512
N_HEADS = 4
HEAD = 128
CHUNK = 64
PAIR = 2 * CHUNK
WINDOWS = (2, 4, 8, 16)
CONV_K = 4
EPS = 1e-6
N_IN = 3080
N_IN_PAD = 3200
BLK_IN = 770
BLK_IN_PAD = 896
BLK_OUT = 256
BLK_CONV = 384
COL_BA = 3072
QK_SCALE = HEAD ** -0.5
SMALL_ROWS = 608
VMEM_LIMIT = 56 * 1024 * 1024

ADAM_LR = 0.001
ADAM_B1 = 0.9
ADAM_B2 = 0.999
ADAM_EPS = 1e-08
ADAM_WD = 0.01
ADAM_STEP = 10

CHIP_MASKS = (2, 1, 3)
HEADS = range(N_HEADS)
HEAD_COLS = [slice(h * HEAD, (h + 1) * HEAD) for h in HEADS]


def _call(body, **kw):
    return pl.pallas_call(body, **kw)


def _params(*sem):
    return pltpu.CompilerParams(dimension_semantics=sem, vmem_limit_bytes=VMEM_LIMIT)


def _sds(shape, dtype=F32):
    return jax.ShapeDtypeStruct(shape, dtype)


def _bdot(a, b):
    return jnp.dot(a.astype(BF16), b.astype(BF16), preferred_element_type=F32)


def _bdot_nt(a, b):
    return lax.dot_general(a.astype(BF16), b.astype(BF16), (((1,), (1,)), ((), ())), preferred_element_type=F32)


def _bdot_tn(a, b):
    return lax.dot_general(a.astype(BF16), b.astype(BF16), (((0,), (0,)), ((), ())), preferred_element_type=F32)


def _split(a):
    hi = a.astype(BF16)
    lo = (a - hi.astype(F32)).astype(BF16)
    return hi, lo


def _mask_dot(m, b, dims=(((1,), (0,)), ((), ()))):
    bh, bl = _split(b)
    dg = functools.partial(lax.dot_general, dimension_numbers=dims, preferred_element_type=F32)
    return dg(m, bh) + dg(m, bl)


def _sigmoid(x):
    return 0.5 * jnp.tanh(0.5 * x) + 0.5


def _softplus(x):
    return jnp.maximum(x, 0.0) + jnp.log(1.0 + jnp.exp(-jnp.abs(x)))


def _rowsum(x):
    return jnp.sum(x, axis=-1, keepdims=True)


def _colsum(x):
    return jnp.sum(x, axis=0, keepdims=True)


def _shift_down(xv, prev8, k):
    r = pltpu.roll(xv, k, 0)
    q = pltpu.roll(prev8, k, 0)
    row = lax.broadcasted_iota(I32, prev8.shape, 0)
    top = jnp.where(row < k, q, r[0:8])
    return jnp.concatenate([top, r[8:]], axis=0)


def _shift_up(xv, next8, k):
    t = xv.shape[0]
    r = pltpu.roll(xv, t - k, 0)
    q = pltpu.roll(next8, 8 - k, 0)
    row = lax.broadcasted_iota(I32, next8.shape, 0)
    bot = jnp.where(row >= 8 - k, q, r[t - 8:])
    return jnp.concatenate([r[:t - 8], bot], axis=0)


def _band(rows, cols, off, w, anti=False):
    r = lax.broadcasted_iota(I32, (rows, cols), 0)
    c = lax.broadcasted_iota(I32, (rows, cols), 1)
    d = (c - r + off) if anti else (r - c + off)
    return ((d >= 0) & (d < w)).astype(BF16)


def _head(ref_or_val, h):
    return ref_or_val[:, h * HEAD:(h + 1) * HEAD]


INTRA_PAIRS = 2
UNITS = [(pp, h) for pp in range(INTRA_PAIRS) for h in HEADS]


def _heads(ref, rows=PAIR):
    return [ref[pp * rows:(pp + 1) * rows, HEAD_COLS[h]] for pp, h in UNITS]


def _put_heads(ref, vals, rows=PAIR):
    for (pp, h), v in zip(UNITS, vals):
        ref[pp * rows:(pp + 1) * rows, HEAD_COLS[h]] = v.astype(ref.dtype)


def _each(fn, *lists):
    return [fn(*args) for args in zip(*lists)]


def _proj_fwd(x, norm_w, w_pad):
    s = x.shape[0]
    tm = 512

    def body(x_ref, nw_ref, w_ref, proj_ref, nt_ref):
        xv = x_ref[...]
        r = lax.rsqrt(jnp.mean(xv * xv, axis=-1, keepdims=True) + EPS)
        nv = xv * r * nw_ref[...]
        nt_ref[...] = nv.T.astype(BF16)
        proj_ref[...] = jnp.dot(nv.astype(BF16), w_ref[...], preferred_element_type=F32)

    return _call(
        body, name="proj_fwd", grid=(s // tm,),
        in_specs=[pl.BlockSpec((tm, D_MODEL), lambda i: (i, 0)),
                  pl.BlockSpec((1, D_MODEL), lambda i: (0, 0)),
                  pl.BlockSpec((D_MODEL, N_IN_PAD), lambda i: (0, 0))],
        out_specs=[pl.BlockSpec((tm, N_IN_PAD), lambda i: (i, 0)),
                   pl.BlockSpec((D_MODEL, tm), lambda i: (0, i))],
        out_shape=[_sds((s, N_IN_PAD)), _sds((D_MODEL, s), BF16)],
        compiler_params=_params("arbitrary"),
    )(x, norm_w, w_pad)


def _pool_bands(t, anti=False):
    r = np.arange(t)[:, None]
    c = np.arange(t + HEAD)[None, :]
    d = (c - r) if anti else (r - c + HEAD)
    return jnp.asarray(np.stack([(d >= 0) & (d < w) for w in WINDOWS]), BF16)


def _pool_mix(u, halo, z, pw, bands, row0):
    t = u[0].shape[0]
    rows = row0 + lax.broadcasted_iota(I32, (t, 1), 0) + 1
    cnt = [jnp.minimum(rows, w).astype(F32) for w in WINDOWS]
    win = _each(lambda b, h, v: _mask_dot(b, jnp.concatenate([h, v], axis=0)), bands, halo, u)
    mix = _each(lambda a, c, v: a / c - v, win, cnt, u)
    mixed = _each(_bdot, mix, pw)
    return mix, mixed, _each(_sigmoid, z), cnt


POOL_T = 256


def _pool_fwd(proj, pool_w, pool_scale):
    s = proj.shape[0]
    t = POOL_T
    hb = t // HEAD

    def body(u_ref, z_ref, halo_ref, pw_ref, ps_ref, band_ref, y_ref):
        i = pl.program_id(0)
        live = (i > 0).astype(F32)
        groups = lambda ref: [ref[:, sl] for sl in HEAD_COLS]
        z = groups(z_ref)
        _, mixed, sg, _ = _pool_mix(groups(u_ref), [h * live for h in groups(halo_ref)], z,
                                    [pw_ref[g] for g in HEADS], [band_ref[g] for g in HEADS], i * t)
        for sl, m, zg, s_ in zip(HEAD_COLS, mixed, z, sg):
            y_ref[:, sl] = m * ps_ref[:, sl] * (zg * s_)

    return _call(
        body, name="pool_fwd", grid=(s // t,),
        in_specs=[pl.BlockSpec((t, D_HALF), lambda i: (i, 0)),
                  pl.BlockSpec((t, D_HALF), lambda i: (i, 1)),
                  pl.BlockSpec((HEAD, D_HALF), lambda i: (jnp.maximum(i * hb - 1, 0), 0)),
                  pl.BlockSpec((N_HEADS, HEAD, HEAD), lambda i: (0, 0, 0)),
                  pl.BlockSpec((1, D_HALF), lambda i: (0, 0)),
                  pl.BlockSpec((N_HEADS, t, HEAD + t), lambda i: (0, 0, 0))],
        out_specs=pl.BlockSpec((t, D_HALF), lambda i: (i, 0)),
        out_shape=_sds((s, D_HALF)),
        compiler_params=_params("arbitrary"),
    )(proj, proj, proj, pool_w, pool_scale, _pool_bands(t))


def _conv_taps(xv, prev8):
    return [_shift_down(xv, prev8, CONV_K - 1 - j) for j in range(CONV_K - 1)] + [xv]


def _conv_pre(taps, cw):
    y = taps[CONV_K - 1] * cw[CONV_K - 1:CONV_K]
    for j in range(CONV_K - 2, -1, -1):
        y = y + taps[j] * cw[j:j + 1]
    return y


CONV_T = 256
CONV_SUB = 256


def _conv_specs(t, tile_of=lambda i: i):
    tiles = [pl.BlockSpec((t, D_HALF), functools.partial(lambda i, p: (tile_of(i), 2 + p), p=p)) for p in range(3)]
    halos = [pl.BlockSpec((8, D_HALF),
                          functools.partial(lambda i, p: (jnp.maximum(tile_of(i) * (t // 8) - 1, 0), 2 + p), p=p))
             for p in range(3)]
    return tiles + halos


def _conv_fwd(proj, conv_w, a_log, dt_bias):
    s = proj.shape[0]
    t = CONV_T

    def body(q_ref, k_ref, v_ref, hq_ref, hk_ref, hv_ref, ba_ref, cw_ref, al_ref, dtb_ref,
             qn_ref, kn_ref, vs_ref, beta_ref, g_ref):
        live = (pl.program_id(0) > 0).astype(F32)
        parts = ((q_ref, hq_ref, qn_ref), (k_ref, hk_ref, kn_ref), (v_ref, hv_ref, vs_ref))

        def sub_tile(r0, first):
            rows = pl.ds(r0, CONV_SUB)
            for p, (x_ref, h_ref, o_ref) in enumerate(parts):
                for h in HEADS:
                    cs = HEAD_COLS[h]
                    prev8 = h_ref[:, cs] * live if first else x_ref[pl.ds(r0 - 8, 8), cs]
                    y = _conv_pre(_conv_taps(x_ref[rows, cs], prev8), cw_ref[:, p * D_HALF + h * HEAD:p * D_HALF + (h + 1) * HEAD])
                    sv = y * _sigmoid(y)
                    o_ref[rows, cs] = sv if p == 2 else sv * lax.rsqrt(_rowsum(sv * sv) + EPS)
            ba = ba_ref[rows, :]
            for h in HEADS:
                beta = _sigmoid(ba[:, h:h + 1])
                gl = -jnp.exp(al_ref[0:1, h:h + 1]) * _softplus(ba[:, N_HEADS + h:N_HEADS + h + 1] + dtb_ref[0:1, h:h + 1])
                beta_ref[rows, HEAD_COLS[h]] = jnp.broadcast_to(beta, (CONV_SUB, HEAD))
                g_ref[rows, HEAD_COLS[h]] = jnp.broadcast_to(gl, (CONV_SUB, HEAD))

        sub_tile(0, True)

        def step(k, carry):
            sub_tile(pl.multiple_of(k * CONV_SUB, CONV_SUB), False)
            return carry

        lax.fori_loop(1, t // CONV_SUB, step, 0)

    row = pl.BlockSpec((t, D_HALF), lambda i: (i, 0))
    return _call(
        body, name="conv_fwd", grid=(s // t,),
        in_specs=_conv_specs(t) + [pl.BlockSpec((t, HEAD), lambda i: (i, COL_BA // HEAD)),
                                   pl.BlockSpec((CONV_K, 3 * D_HALF), lambda i: (0, 0)),
                                   pl.BlockSpec((1, N_HEADS), lambda i: (0, 0)),
                                   pl.BlockSpec((1, N_HEADS), lambda i: (0, 0))],
        out_specs=[row] * 5,
        out_shape=[_sds((s, D_HALF))] * 5,
        compiler_params=_params("arbitrary"),
    )(proj, proj, proj, proj, proj, proj, proj, conv_w, a_log, dt_bias)


def _pair_masks():
    r = lax.broadcasted_iota(I32, (PAIR, PAIR), 0)
    c = lax.broadcasted_iota(I32, (PAIR, PAIR), 1)
    same = jnp.right_shift(r, 6) == jnp.right_shift(c, 6)
    return same, same & (r >= c), same & (r > c), r == c


def _run(stages):
    for _ in stages:
        pass


def _interleave(*stage_lists):
    live = list(stage_lists)
    while live:
        for gen in list(live):
            try:
                next(gen)
            except StopIteration:
                live.remove(gen)


def _pair_common_stages(cm, qn, kn, vs, beta, g):
    same, incl, strict, eye = _pair_masks()
    incl_b = incl.astype(BF16)
    first = lax.broadcasted_iota(I32, (PAIR, HEAD), 0) < CHUNK
    cm.update(same=same, incl=incl, strict=strict, eye=eye)
    gc = _each(lambda gv: _mask_dot(incl_b, gv), g)
    q = _each(lambda v: v * QK_SCALE, qn)
    kb = _each(lambda k, b: k * b, kn, beta)
    cm.update(gc=gc, q=q, kb=kb, vb=_each(lambda v, b: v * b, vs, beta))
    yield
    cm.update(kk=_each(_bdot_nt, kb, kn), qk=_each(_bdot_nt, q, kn))
    gc_row = _each(lambda v: _colsum(jnp.where(eye, v, 0.0)), gc)
    gl = _each(lambda v: jnp.where(first, v[CHUNK - 1:CHUNK], v[PAIR - 1:PAIR]), gc)
    egc = _each(jnp.exp, gc)
    cm.update(gl=gl, egc=egc,
              decay=_each(lambda v, r: jnp.where(incl, jnp.exp(jnp.where(incl, v - r, 0.0)), 0.0), gc, gc_row))
    yield
    cm.update(ekd=_each(lambda a, b: jnp.exp(a - b), gl, gc), cd=_each(jnp.exp, gl),
              kbg=_each(lambda k, e: k * e, kb, egc))
    yield


def _pair_common(qn, kn, vs, beta, g):
    cm = {}
    _run(_pair_common_stages(cm, qn, kn, vs, beta, g))
    return cm


def _tri_inv_stages(out, a, eye_f):
    p = _each(lambda v: eye_f - v, a)
    x = _each(_bdot, a, a)
    yield
    for it in range(5):
        p = _each(lambda pv, xv: pv + _bdot(pv, xv), p, x)
        if it < 4:
            x = _each(_bdot, x, x)
        yield
    out["t"] = p


def _tri_inv(a, eye_f):
    out = {}
    _run(_tri_inv_stages(out, a, eye_f))
    return out["t"]


def _pair_spec():
    return pl.BlockSpec((INTRA_PAIRS * PAIR, D_HALF), lambda i: (i, 0))


def _chunk_scalar_spec(pairs=1, index=lambda i: (i, 0)):
    return pl.BlockSpec((16 * pairs, D_HALF), index)


SCAN_PAIRS = 2
SCAN_ROWS = SCAN_PAIRS * PAIR


def _intra_fwd(qn, kn, vs, beta, g):
    s = qn.shape[0]

    def body(qn_ref, kn_ref, vs_ref, beta_ref, g_ref, u_ref, w_ref, att_ref, qd_ref, kd_ref, t_ref, cd_ref):
        kn = _heads(kn_ref)
        cm = _pair_common(_heads(qn_ref), kn, _heads(vs_ref), _heads(beta_ref), _heads(g_ref))
        a = _each(lambda kk, d: jnp.where(cm["strict"], kk * d, 0.0), cm["kk"], cm["decay"])
        tm = _tri_inv(a, cm["eye"].astype(F32))
        _put_heads(t_ref, tm)
        _put_heads(u_ref, _each(_bdot, tm, cm["vb"]))
        _put_heads(w_ref, _each(_bdot, tm, cm["kbg"]))
        _put_heads(att_ref, _each(lambda a, b: a * b, cm["qk"], cm["decay"]))
        _put_heads(qd_ref, _each(lambda a, b: a * b, cm["q"], cm["egc"]))
        _put_heads(kd_ref, _each(lambda a, b: a * b, kn, cm["ekd"]))
        for ci in range(2):
            for (pp, h), v in zip(UNITS, cm["cd"]):
                cd_ref[pp * 16 + ci * 8:pp * 16 + (ci + 1) * 8, HEAD_COLS[h]] = v[ci * CHUNK:ci * CHUNK + 8]

    return _call(
        body, name="intra_fwd", grid=(s // (INTRA_PAIRS * PAIR),),
        in_specs=[_pair_spec()] * 5, out_specs=[_pair_spec()] * 6 + [_chunk_scalar_spec(INTRA_PAIRS)],
        out_shape=[_sds((s, D_HALF))] + [_sds((s, D_HALF), BF16)] * 5 + [_sds((s // 8, D_HALF))],
        compiler_params=_params("arbitrary"),
    )(qn, kn, vs, beta, g)


def _scan_fwd(u, w, att, qd, kd, cd):
    s = u.shape[0]
    n_chunks = s // CHUNK

    def body(u_ref, w_ref, att_ref, qd_ref, kd_ref, cd_ref, o_ref, vn_ref, st_ref, state):
        @pl.when(pl.program_id(0) == 0)
        def _():
            state[...] = jnp.zeros_like(state)
        cols = list(enumerate(HEAD_COLS))
        sm = [state[h] for h in HEADS]
        for ci in range(2 * SCAN_PAIRS):
            rs = slice(ci * CHUNK, (ci + 1) * CHUNK)
            for h in HEADS:
                st_ref[ci, h] = sm[h]
            both = [_bdot(jnp.concatenate([w_ref[rs, sl], qd_ref[rs, sl]], axis=0), sm[h]) for h, sl in cols]
            vn = [u_ref[rs, sl] - both[h][:CHUNK] for h, sl in cols]
            for h, sl in cols:
                vn_ref[rs, sl] = vn[h].astype(BF16)
                o_ref[rs, sl] = both[h][CHUNK:]
            sm = [sm[h] * cd_ref[ci * 8:ci * 8 + 1, sl] + _bdot_tn(kd_ref[rs, sl], vn[h]) for h, sl in cols]
        for h in HEADS:
            state[h] = sm[h]
        for pp in range(SCAN_PAIRS):
            rp = slice(pp * PAIR, (pp + 1) * PAIR)
            intra = [_bdot(att_ref[rp, sl], vn_ref[rp, sl]) for sl in HEAD_COLS]
            for h, sl in cols:
                o_ref[rp, sl] += intra[h]

    rows = pl.BlockSpec((SCAN_ROWS, D_HALF), lambda i: (i, 0))
    return _call(
        body, name="scan_fwd", grid=(s // SCAN_ROWS,),
        in_specs=[rows] * 5 + [_chunk_scalar_spec(SCAN_PAIRS)],
        out_specs=[rows, rows, pl.BlockSpec((2 * SCAN_PAIRS, N_HEADS, HEAD, HEAD), lambda i: (i, 0, 0, 0))],
        out_shape=[_sds((s, D_HALF)), _sds((s, D_HALF), BF16), _sds((n_chunks, N_HEADS, HEAD, HEAD))],
        scratch_shapes=[pltpu.VMEM((N_HEADS, HEAD, HEAD), F32)],
        compiler_params=_params("arbitrary"),
    )(u, w, att, qd, kd, cd)


def _delta_fwd(qn, kn, vs, beta, g):
    s = qn.shape[0]
    n_steps = s // SCAN_ROWS
    n_chunks = s // CHUNK
    assert INTRA_PAIRS == SCAN_PAIRS

    def body(qn_ref, kn_ref, vs_ref, beta_ref, g_ref, w_ref, att_ref, qd_ref, kd_ref, t_ref, cd_ref, o_ref, vn_ref, st_ref,
             state, u_s, w_s, att_s, qd_s, kd_s, cd_s):
        t = pl.program_id(0)

        @pl.when(t <= 1)
        def _():
            state[...] = jnp.zeros_like(state)

        @pl.when(t == 0)
        def _():
            for ref in (u_s, w_s, att_s, qd_s, kd_s, cd_s):
                ref[1] = jnp.zeros(ref.shape[1:], ref.dtype)

        cur = lax.rem(t, 2)
        prev = 1 - cur
        cols = list(enumerate(HEAD_COLS))

        def recurrence():
            sm = [state[h] for h in HEADS]
            for ci in range(2 * SCAN_PAIRS):
                rs = slice(ci * CHUNK, (ci + 1) * CHUNK)
                for h in HEADS:
                    st_ref[ci, h] = sm[h]
                both = [_bdot(jnp.concatenate([w_s[prev, rs, sl], qd_s[prev, rs, sl]], axis=0), sm[h]) for h, sl in cols]
                vn = [u_s[prev, rs, sl] - both[h][:CHUNK] for h, sl in cols]
                for h, sl in cols:
                    vn_ref[rs, sl] = vn[h].astype(BF16)
                    o_ref[rs, sl] = both[h][CHUNK:]
                yield
                sm = [sm[h] * cd_s[prev, ci * 8:ci * 8 + 1, sl] + _bdot_tn(kd_s[prev, rs, sl], vn[h]) for h, sl in cols]
                yield
            for h in HEADS:
                state[h] = sm[h]
            for pp in range(SCAN_PAIRS):
                rp = slice(pp * PAIR, (pp + 1) * PAIR)
                intra = [_bdot(att_s[prev, rp, sl], vn_ref[rp, sl]) for sl in HEAD_COLS]
                for h, sl in cols:
                    o_ref[rp, sl] += intra[h]
                yield

        def factors():
            kn = _heads(kn_ref)
            cm = {}
            yield from _pair_common_stages(cm, _heads(qn_ref), kn, _heads(vs_ref), _heads(beta_ref), _heads(g_ref))
            a = _each(lambda kk, d: jnp.where(cm["strict"], kk * d, 0.0), cm["kk"], cm["decay"])
            inv = {}
            yield from _tri_inv_stages(inv, a, cm["eye"].astype(F32))
            tm = inv["t"]
            res = dict(u=_each(_bdot, tm, cm["vb"]), w=_each(_bdot, tm, cm["kbg"]),
                       att=_each(lambda a, b: a * b, cm["qk"], cm["decay"]),
                       qd=_each(lambda a, b: a * b, cm["q"], cm["egc"]), kd=_each(lambda a, b: a * b, kn, cm["ekd"]))
            yield
            _put_heads(t_ref, tm)
            for key, out, keep in (("w", w_ref, w_s), ("att", att_ref, att_s), ("qd", qd_ref, qd_s), ("kd", kd_ref, kd_s)):
                _put_heads(out, res[key])
                for (pp, h), v in zip(UNITS, res[key]):
                    keep[cur, pp * PAIR:(pp + 1) * PAIR, HEAD_COLS[h]] = v.astype(BF16)
            for (pp, h), v in zip(UNITS, res["u"]):
                u_s[cur, pp * PAIR:(pp + 1) * PAIR, HEAD_COLS[h]] = v
            for ci in range(2):
                for (pp, h), v in zip(UNITS, cm["cd"]):
                    rows8 = slice(pp * 16 + ci * 8, pp * 16 + (ci + 1) * 8)
                    cd_ref[rows8, HEAD_COLS[h]] = v[ci * CHUNK:ci * CHUNK + 8]
                    cd_s[cur, rows8, HEAD_COLS[h]] = v[ci * CHUNK:ci * CHUNK + 8]
            yield

        _interleave(recurrence(), factors())

    last = n_steps - 1
    now = lambda i: (jnp.minimum(i, last), 0)
    before = lambda i: (jnp.maximum(i - 1, 0), 0)
    rows = lambda index: pl.BlockSpec((SCAN_ROWS, D_HALF), index)
    slot = lambda r, dtype: pltpu.VMEM((2, r, D_HALF), dtype)
    return _call(
        body, name="delta_fwd", grid=(n_steps + 1,),
        in_specs=[rows(now)] * 5,
        out_specs=[rows(now)] * 5 + [_chunk_scalar_spec(SCAN_PAIRS, now), rows(before), rows(before),
                                     pl.BlockSpec((2 * SCAN_PAIRS, N_HEADS, HEAD, HEAD), lambda i: (jnp.maximum(i - 1, 0), 0, 0, 0))],
        out_shape=[_sds((s, D_HALF), BF16)] * 5 + [_sds((s // 8, D_HALF)), _sds((s, D_HALF)), _sds((s, D_HALF), BF16),
                                                  _sds((n_chunks, N_HEADS, HEAD, HEAD))],
        scratch_shapes=[pltpu.VMEM((N_HEADS, HEAD, HEAD), F32), slot(SCAN_ROWS, F32), slot(SCAN_ROWS, BF16),
                        slot(SCAN_ROWS, BF16), slot(SCAN_ROWS, BF16), slot(SCAN_ROWS, BF16), slot(16 * SCAN_PAIRS, F32)],
        compiler_params=_params("arbitrary"),
    )(qn, kn, vs, beta, g)


OUT_T = 512


def _out_fwd_bwd(x, y_pool, o, proj, target, w_out, dn_norm_w, final_norm_w):
    s = x.shape[0]
    t = OUT_T

    def body(x_ref, yp_ref, o_ref, z_ref, tg_ref, wo_ref, dnw_ref, fnw_ref,
             gwo_ref, dh_ref, dyp_ref, do_ref, dz_ref, loss_ref, gfn_ref, gdn_ref, y_ref, yt_ref, gwo_acc):
        @pl.when(pl.program_id(0) == 0)
        def _():
            loss_ref[...] = jnp.zeros_like(loss_ref)
            gfn_ref[...] = jnp.zeros_like(gfn_ref)
            gdn_ref[...] = jnp.zeros_like(gdn_ref)
            gwo_acc[...] = jnp.zeros_like(gwo_acc)

        ypv = yp_ref[...]
        y_ref[:, :D_HALF] = ypv.astype(BF16)
        yt_ref[:D_HALF, :] = ypv.T.astype(BF16)
        dnw = dnw_ref[...]
        keep = []
        for h in HEADS:
            ov = o_ref[:, HEAD_COLS[h]]
            zv = z_ref[:, HEAD_COLS[h]]
            ro = lax.rsqrt(jnp.mean(ov * ov, axis=-1, keepdims=True) + EPS)
            ohat = ov * ro
            sg = _sigmoid(zv)
            keep.append((ro, ohat, zv, sg))
            ydn = ohat * dnw * (zv * sg)
            y_ref[:, D_HALF + h * HEAD:D_HALF + (h + 1) * HEAD] = ydn.astype(BF16)
            yt_ref[D_HALF + h * HEAD:D_HALF + (h + 1) * HEAD, :] = ydn.T.astype(BF16)

        hv = x_ref[...] + jnp.dot(y_ref[...], wo_ref[...], preferred_element_type=F32)
        r2 = lax.rsqrt(jnp.mean(hv * hv, axis=-1, keepdims=True) + EPS)
        hhat = hv * r2
        fnw = fnw_ref[...]
        err = hhat * fnw - tg_ref[...]
        loss_ref[...] += 0.5 * jnp.sum(_rowsum(err * err) * (1.0 / D_MODEL), axis=0, keepdims=True)
        dout = err * (1.0 / D_MODEL)
        gfn_ref[...] += _colsum(dout * hhat)
        dhh = dout * fnw
        dh = r2 * (dhh - hhat * jnp.mean(dhh * hhat, axis=-1, keepdims=True))
        dh_ref[...] = dh
        gwo_acc[...] += _bdot(yt_ref[...], dh)

        @pl.when(pl.program_id(0) == pl.num_programs(0) - 1)
        def _():
            gwo_ref[...] = gwo_acc[...].astype(BF16)

        dy = _bdot_nt(dh, wo_ref[...])
        dyp_ref[...] = dy[:, :D_HALF]
        gdn = jnp.zeros((1, HEAD), F32)
        for h in HEADS:
            ro, ohat, zv, sg = keep[h]
            dyd = dy[:, D_HALF + h * HEAD:D_HALF + (h + 1) * HEAD]
            sz = zv * sg
            dz_ref[:, HEAD_COLS[h]] = (dyd * ohat * dnw * (sg * (1.0 + zv * (1.0 - sg)))).astype(BF16)
            gdn = gdn + _colsum(dyd * ohat * sz)
            doh = dyd * dnw * sz
            do_ref[:, HEAD_COLS[h]] = ro * (doh - ohat * jnp.mean(doh * ohat, axis=-1, keepdims=True))
        gdn_ref[...] += gdn

    wide = pl.BlockSpec((t, D_MODEL), lambda i: (i, 0))
    half = pl.BlockSpec((t, D_HALF), lambda i: (i, 0))
    const = lambda shape: pl.BlockSpec(shape, lambda i: (0,) * len(shape))
    return _call(
        body, name="out_fwd_bwd", grid=(s // t,),
        in_specs=[wide, half, half, pl.BlockSpec((t, D_HALF), lambda i: (i, 5)), wide,
                  const((D_MODEL, D_MODEL)), const((1, HEAD)), const((1, D_MODEL))],
        out_specs=[const((D_MODEL, D_MODEL)), wide, half, half, half,
                   const((1, HEAD)), const((1, D_MODEL)), const((1, HEAD))],
        out_shape=[_sds((D_MODEL, D_MODEL), BF16), _sds((s, D_MODEL)), _sds((s, D_HALF)), _sds((s, D_HALF)),
                   _sds((s, D_HALF), BF16), _sds((1, HEAD)), _sds((1, D_MODEL)), _sds((1, HEAD))],
        scratch_shapes=[pltpu.VMEM((t, D_MODEL), BF16), pltpu.VMEM((D_MODEL, t), BF16), pltpu.VMEM((D_MODEL, D_MODEL), F32)],
        compiler_params=_params("arbitrary"),
    )(x, y_pool, o, proj, target, w_out, dn_norm_w, final_norm_w)


def _token_matmul(name, at, pieces):
    m, s = at.shape
    n = len(pieces)
    tn, tk = D_HALF, 512

    def body(a_ref, *refs):
        p_refs, o_ref, acc = refs[:n], refs[n], refs[n + 1]

        @pl.when(pl.program_id(0) == 0)
        def _():
            acc[...] = jnp.zeros_like(acc)

        av = a_ref[...]
        for p in range(n):
            acc[:, p * tn:(p + 1) * tn] += _bdot(av, p_refs[p][...])

        @pl.when(pl.program_id(0) == pl.num_programs(0) - 1)
        def _():
            o_ref[...] = acc[...].astype(BF16)

    return _call(
        body, name=name, grid=(s // tk,),
        in_specs=[pl.BlockSpec((m, tk), lambda k: (0, k))]
                 + [pl.BlockSpec((tk, tn), functools.partial(lambda k, cb: (k, cb), cb=cb)) for _, cb in pieces],
        out_specs=pl.BlockSpec((m, n * tn), lambda k: (0, 0)),
        out_shape=_sds((m, n * tn), BF16),
        scratch_shapes=[pltpu.VMEM((m, n * tn), F32)],
        compiler_params=_params("arbitrary"),
    )(at, *[p[0] for p in pieces])


def _grad_w_in(at, pieces):
    m, s = at.shape
    n = len(pieces)
    tn, tk = D_HALF, min(s, 1024)

    def body(a_ref, *refs):
        p_refs, o_ref, acc = refs[:n], refs[n], refs[n + 1]

        @pl.when(pl.program_id(0) == 0)
        def _():
            acc[...] = jnp.zeros_like(acc)

        av = a_ref[...]
        for p in range(n):
            acc[:, p * tn:(p + 1) * tn] += _bdot(av, p_refs[p][...])

        @pl.when(pl.program_id(0) == pl.num_programs(0) - 1)
        def _():
            for j in range(4):
                base = j * BLK_IN // HEAD * HEAD
                win = acc[:, base:base + BLK_IN_PAD]
                if j * BLK_IN > base:
                    win = pltpu.roll(win, BLK_IN_PAD - (j * BLK_IN - base), 1)
                o_ref[j] = win.astype(BF16)

    return _call(
        body, name="grad_w_in", grid=(s // tk,),
        in_specs=[pl.BlockSpec((m, tk), lambda k: (0, k))] + [pl.BlockSpec((tk, tn), lambda k: (k, 0))] * n,
        out_specs=pl.BlockSpec((4, m, BLK_IN_PAD), lambda k: (0, 0, 0)),
        out_shape=_sds((4, m, BLK_IN_PAD), BF16),
        scratch_shapes=[pltpu.VMEM((m, n * tn), F32)],
        compiler_params=_params("arbitrary"),
    )(at, *pieces)


def _pool_bwd(proj, dyp, pool_w, pool_scale):
    s = proj.shape[0]
    t = POOL_T
    hb = t // HEAD
    last = s // HEAD - 1

    def body(u_ref, z_ref, halo_ref, dy_ref, zn_ref, dyn_ref, pw_ref, ps_ref, band_ref, aband_ref,
             du_ref, dz_ref, gpw_ref, gps_ref):
        i = pl.program_id(0)

        @pl.when(i == 0)
        def _():
            gpw_ref[...] = jnp.zeros_like(gpw_ref)
            gps_ref[...] = jnp.zeros_like(gps_ref)

        live = (i > 0).astype(F32)
        more = (i < pl.num_programs(0) - 1).astype(F32)
        groups = lambda ref: [ref[:, sl] for sl in HEAD_COLS]
        z, ps, dy = groups(z_ref), groups(ps_ref), groups(dy_ref)
        pw = [pw_ref[g] for g in HEADS]
        mix, mixed, sg, cnt = _pool_mix(groups(u_ref), [h * live for h in groups(halo_ref)], z, pw,
                                        [band_ref[g] for g in HEADS], i * t)
        sz = _each(lambda a, b: a * b, z, sg)
        for sl, d, m, p, s_, zg in zip(HEAD_COLS, dy, mixed, ps, sg, z):
            dz_ref[:, sl] = (d * m * p * (s_ * (1.0 + zg * (1.0 - s_)))).astype(BF16)
        for sl, d, m, a in zip(HEAD_COLS, dy, mixed, sz):
            gps_ref[:, sl] += _colsum(d * m * a)
        dmixed = _each(lambda d, p, a: d * p * a, dy, ps, sz)
        for g, gp in enumerate(_each(_bdot_tn, mix, dmixed)):
            gpw_ref[g] += gp
        dmix = _each(_bdot_nt, dmixed, pw)
        dmix_n = _each(lambda d, p, zn, w_: _bdot_nt(d * more * p * (zn * _sigmoid(zn)), w_),
                       groups(dyn_ref), ps, groups(zn_ref), pw)
        scaled = [jnp.concatenate([a / c, b * (1.0 / w)], axis=0) for a, c, b, w in zip(dmix, cnt, dmix_n, WINDOWS)]
        du = _each(lambda b, s_, d: _mask_dot(b, s_) - d, [aband_ref[g] for g in HEADS], scaled, dmix)
        for sl, v in zip(HEAD_COLS, du):
            du_ref[:, sl] = v.astype(BF16)

    tile = lambda col: pl.BlockSpec((t, D_HALF), lambda i: (i, col))
    below = lambda col: pl.BlockSpec((HEAD, D_HALF), lambda i: (jnp.minimum((i + 1) * hb, last), col))
    return _call(
        body, name="pool_bwd", grid=(s // t,),
        in_specs=[tile(0), tile(1), pl.BlockSpec((HEAD, D_HALF), lambda i: (jnp.maximum(i * hb - 1, 0), 0)),
                  tile(0), below(1), below(0),
                  pl.BlockSpec((N_HEADS, HEAD, HEAD), lambda i: (0, 0, 0)), pl.BlockSpec((1, D_HALF), lambda i: (0, 0)),
                  pl.BlockSpec((N_HEADS, t, HEAD + t), lambda i: (0, 0, 0)),
                  pl.BlockSpec((N_HEADS, t, HEAD + t), lambda i: (0, 0, 0))],
        out_specs=[tile(0), tile(0), pl.BlockSpec((N_HEADS, HEAD, HEAD), lambda i: (0, 0, 0)),
                   pl.BlockSpec((1, D_HALF), lambda i: (0, 0))],
        out_shape=[_sds((s, D_HALF), BF16), _sds((s, D_HALF), BF16), _sds((N_HEADS, HEAD, HEAD)), _sds((1, D_HALF))],
        compiler_params=_params("arbitrary"),
    )(proj, proj, proj, dyp, proj, dyp, pool_w, pool_scale, _pool_bands(t), _pool_bands(t, anti=True))


def _scan_bwd(do, vn, qd, kd, w, att, cd, st):
    s = do.shape[0]
    n_steps = s // SCAN_ROWS

    def body(do_ref, vn_ref, qd_ref, kd_ref, w_ref, att_ref, cd_ref, st_ref,
             du_ref, dw_ref, datt_ref, dqd_ref, dkd_ref, dcd_ref, dstate):
        @pl.when(pl.program_id(0) == 0)
        def _():
            dstate[...] = jnp.zeros_like(dstate)
        _, incl, _, _ = _pair_masks()
        cols = list(enumerate(HEAD_COLS))
        dv_intra = []
        for pp in range(SCAN_PAIRS):
            rp = slice(pp * PAIR, (pp + 1) * PAIR)
            dv_intra.append([_bdot_tn(att_ref[rp, sl], do_ref[rp, sl]) for _, sl in cols])
            for _, sl in cols:
                datt_ref[rp, sl] = jnp.where(incl, _bdot_nt(do_ref[rp, sl], vn_ref[rp, sl]), 0.0)
        ds = [dstate[h] for h in HEADS]
        for ci in range(2 * SCAN_PAIRS - 1, -1, -1):
            rs = slice(ci * CHUNK, (ci + 1) * CHUNK)
            in_pair = slice((ci % 2) * CHUNK, (ci % 2 + 1) * CHUNK)
            sm = [st_ref[ci, h] for h in HEADS]
            dvn = [dv_intra[ci // 2][h][in_pair] + _bdot(kd_ref[rs, sl], ds[h]) for h, sl in cols]
            for h, sl in cols:
                du_ref[rs, sl] = dvn[h].astype(BF16)
            dqd = [_bdot_nt(do_ref[rs, sl], sm[h]) for h, sl in cols]
            dw = [-_bdot_nt(dvn[h], sm[h]) for h, _ in cols]
            dkd = [_bdot_nt(vn_ref[rs, sl], ds[h]) for h, sl in cols]
            dcd = [jnp.broadcast_to(_rowsum(_colsum(ds[h] * sm[h])), (8, HEAD)) for h in HEADS]
            for h, sl in cols:
                dqd_ref[rs, sl] = dqd[h]
                dw_ref[rs, sl] = dw[h].astype(BF16)
                dkd_ref[rs, sl] = dkd[h]
                dcd_ref[ci * 8:(ci + 1) * 8, sl] = dcd[h]
            ds = [ds[h] * cd_ref[ci * 8:ci * 8 + 1, sl] + _bdot_tn(qd_ref[rs, sl], do_ref[rs, sl])
                  - _bdot_tn(w_ref[rs, sl], dvn[h]) for h, sl in cols]
        for h in HEADS:
            dstate[h] = ds[h]

    rev = pl.BlockSpec((SCAN_ROWS, D_HALF), lambda i: (n_steps - 1 - i, 0))
    rev_scalar = _chunk_scalar_spec(SCAN_PAIRS, lambda i: (n_steps - 1 - i, 0))
    return _call(
        body, name="scan_bwd", grid=(n_steps,),
        in_specs=[rev] * 6 + [rev_scalar,
                              pl.BlockSpec((2 * SCAN_PAIRS, N_HEADS, HEAD, HEAD), lambda i: (n_steps - 1 - i, 0, 0, 0))],
        out_specs=[rev] * 5 + [rev_scalar],
        out_shape=[_sds((s, D_HALF), BF16)] * 2 + [_sds((s, D_HALF))] * 3 + [_sds((s // 8, D_HALF))],
        scratch_shapes=[pltpu.VMEM((N_HEADS, HEAD, HEAD), F32)],
        compiler_params=_params("arbitrary"),
    )(do, vn, qd, kd, w, att, cd, st)


def _intra_bwd(qn, kn, vs, beta, g, tm, du, dw, datt, dqd, dkd, dcd):
    s = qn.shape[0]

    def body(qn_ref, kn_ref, vs_ref, beta_ref, g_ref, t_ref, du_ref, dw_ref, datt_ref, dqd_ref, dkd_ref, dcd_ref,
             dqn_ref, dkn_ref, dvs_ref, dbeta_ref, dg_ref):
        ones = jnp.ones((PAIR, HEAD), BF16)
        tn = (((0,), (0,)), ((), ()))
        kn, vs, beta = _heads(kn_ref), _heads(vs_ref), _heads(beta_ref)
        cm = _pair_common(_heads(qn_ref), kn, vs, beta, _heads(g_ref))
        tmv, duv, dwv, dattv, dqdv, dkdv = (_heads(r) for r in (t_ref, du_ref, dw_ref, datt_ref, dqd_ref, dkd_ref))
        dvb = _each(_bdot_tn, tmv, duv)
        dt = _each(lambda a, b, c, d: _bdot_nt(a, b) + _bdot_nt(c, d), duv, cm["vb"], dwv, cm["kbg"])
        dkbg = _each(_bdot_tn, tmv, dwv)
        m1 = _each(_bdot_tn, tmv, dt)
        da = _each(lambda a, b: -jnp.where(cm["strict"], _bdot_nt(a, b), 0.0), m1, tmv)
        dkk = _each(lambda a, b: a * b, da, cm["decay"])
        dqk = _each(lambda a, b: a * b, dattv, cm["decay"])
        dd = _each(lambda a, b, c, d: a * b + c * d, dkk, cm["kk"], dqk, cm["qk"])
        dkb = _each(lambda a, b, c, d: _bdot(a, b) + c * d, dkk, kn, dkbg, cm["egc"])
        dq = _each(lambda a, b, c, d: _bdot(a, b) + c * d, dqk, kn, dqdv, cm["egc"])
        dkn = _each(lambda a, b, c, d: _bdot_tn(a, b) + _bdot_tn(c, d), dkk, cm["kb"], dqk, cm["q"])
        dkn = _each(lambda a, b, c, d, e: a + b * c + d * e, dkn, dkdv, cm["ekd"], dkb, beta)
        t_kd = _each(lambda a, b, c: _rowsum(a * b * c), dkdv, kn, cm["ekd"])
        split = _each(_split, dd)
        rows_dd = [jnp.dot(hi, ones, preferred_element_type=F32) + jnp.dot(lo, ones, preferred_element_type=F32)
                   for hi, lo in split]
        cols_dd = [lax.dot_general(hi, ones, tn, preferred_element_type=F32)
                   + lax.dot_general(lo, ones, tn, preferred_element_type=F32) for hi, lo in split]
        dgc = _each(lambda r, c, a, b, e, f, k, t: r - c + _rowsum(a * b * e) + _rowsum(f * k) - t,
                    rows_dd, cols_dd, dqdv, cm["q"], cm["egc"], dkbg, cm["kbg"], t_kd)
        same_b = cm["same"].astype(BF16)
        rowi = lax.broadcasted_iota(I32, (PAIR, HEAD), 0)
        dcd = _each(lambda d: jnp.where(rowi < CHUNK, d[0:1], d[8:9]), _heads(dcd_ref, rows=16))
        dgl = _each(lambda t, d, c: _mask_dot(same_b, jnp.broadcast_to(t, (PAIR, HEAD))) + d * c, t_kd, dcd, cm["cd"])
        is_last = jnp.bitwise_and(rowi, CHUNK - 1) == CHUNK - 1
        dgc = _each(lambda a, b: a + jnp.where(is_last, b, 0.0), dgc, dgl)
        r = lax.broadcasted_iota(I32, (PAIR, PAIR), 0)
        c = lax.broadcasted_iota(I32, (PAIR, PAIR), 1)
        upper_b = (cm["same"] & (r <= c)).astype(BF16)
        _put_heads(dg_ref, _each(lambda v: _mask_dot(upper_b, v), dgc))
        _put_heads(dbeta_ref, _each(lambda a, b, c, d: jnp.broadcast_to(_rowsum(a * b) + _rowsum(c * d), (PAIR, HEAD)),
                                    dkb, kn, dvb, vs))
        _put_heads(dqn_ref, _each(lambda v: v * QK_SCALE, dq))
        _put_heads(dkn_ref, dkn)
        _put_heads(dvs_ref, _each(lambda a, b: a * b, dvb, beta))

    return _call(
        body, name="intra_bwd", grid=(s // (INTRA_PAIRS * PAIR),),
        in_specs=[_pair_spec()] * 11 + [_chunk_scalar_spec(INTRA_PAIRS)], out_specs=[_pair_spec()] * 5,
        out_shape=[_sds((s, D_HALF))] * 5,
        compiler_params=_params("arbitrary"),
    )(qn, kn, vs, beta, g, tm, du, dw, datt, dqd, dkd, dcd)


def _delta_bwd(do, vn, qd, kd, w, att, cd, st, qn, kn, vs, beta, g, tm):
    s = do.shape[0]
    n_steps = s // SCAN_ROWS
    assert INTRA_PAIRS == SCAN_PAIRS

    def body(do_ref, vn_ref, qd_ref, kd_ref, w_ref, att_ref, cd_ref, st_ref, qn_ref, kn_ref, vs_ref, beta_ref, g_ref, t_ref,
             dqn_ref, dkn_ref, dvs_ref, dbeta_ref, dg_ref, dstate, du_s, dw_s, datt_s, dqd_s, dkd_s, dcd_s):
        t = pl.program_id(0)

        @pl.when(t == 0)
        def _():
            dstate[...] = jnp.zeros_like(dstate)
            for ref in (du_s, dw_s, datt_s, dqd_s, dkd_s, dcd_s):
                ref[1] = jnp.zeros(ref.shape[1:], ref.dtype)

        cur = lax.rem(t, 2)
        prev = 1 - cur
        cols = list(enumerate(HEAD_COLS))
        _, incl, _, _ = _pair_masks()

        def recurrence():
            dv_intra = []
            for pp in range(SCAN_PAIRS):
                rp = slice(pp * PAIR, (pp + 1) * PAIR)
                dv_intra.append([_bdot_tn(att_ref[rp, sl], do_ref[rp, sl]) for _, sl in cols])
                for _, sl in cols:
                    datt_s[cur, rp, sl] = jnp.where(incl, _bdot_nt(do_ref[rp, sl], vn_ref[rp, sl]), 0.0)
                yield
            ds = [dstate[h] for h in HEADS]
            for ci in range(2 * SCAN_PAIRS - 1, -1, -1):
                rs = slice(ci * CHUNK, (ci + 1) * CHUNK)
                in_pair = slice((ci % 2) * CHUNK, (ci % 2 + 1) * CHUNK)
                sm = [st_ref[ci, h] for h in HEADS]
                dvn = [dv_intra[ci // 2][h][in_pair] + _bdot(kd_ref[rs, sl], ds[h]) for h, sl in cols]
                dqd = [_bdot_nt(do_ref[rs, sl], sm[h]) for h, sl in cols]
                dkd = [_bdot_nt(vn_ref[rs, sl], ds[h]) for h, sl in cols]
                dcd = [jnp.broadcast_to(_rowsum(_colsum(ds[h] * sm[h])), (8, HEAD)) for h in HEADS]
                yield
                dw = [-_bdot_nt(dvn[h], sm[h]) for h, _ in cols]
                for h, sl in cols:
                    du_s[cur, rs, sl] = dvn[h].astype(BF16)
                    dqd_s[cur, rs, sl] = dqd[h]
                    dw_s[cur, rs, sl] = dw[h].astype(BF16)
                    dkd_s[cur, rs, sl] = dkd[h]
                    dcd_s[cur, ci * 8:(ci + 1) * 8, sl] = dcd[h]
                ds = [ds[h] * cd_ref[ci * 8:ci * 8 + 1, sl] + _bdot_tn(qd_ref[rs, sl], do_ref[rs, sl])
                      - _bdot_tn(w_ref[rs, sl], dvn[h]) for h, sl in cols]
                yield
            for h in HEADS:
                dstate[h] = ds[h]

        def factors():
            ones = jnp.ones((PAIR, HEAD), BF16)
            tn = (((0,), (0,)), ((), ()))
            kept = lambda ref, rows=PAIR: [ref[prev, pp * rows:(pp + 1) * rows, HEAD_COLS[h]] for pp, h in UNITS]
            kn, vs, beta = _heads(kn_ref), _heads(vs_ref), _heads(beta_ref)
            cm = {}
            yield from _pair_common_stages(cm, _heads(qn_ref), kn, vs, beta, _heads(g_ref))
            tmv = _heads(t_ref)
            duv, dwv, dattv, dqdv, dkdv = kept(du_s), kept(dw_s), kept(datt_s), kept(dqd_s), kept(dkd_s)
            dvb = _each(_bdot_tn, tmv, duv)
            dt = _each(lambda a, b, c, d: _bdot_nt(a, b) + _bdot_nt(c, d), duv, cm["vb"], dwv, cm["kbg"])
            dkbg = _each(_bdot_tn, tmv, dwv)
            yield
            m1 = _each(_bdot_tn, tmv, dt)
            yield
            da = _each(lambda a, b: -jnp.where(cm["strict"], _bdot_nt(a, b), 0.0), m1, tmv)
            yield
            dkk = _each(lambda a, b: a * b, da, cm["decay"])
            dqk = _each(lambda a, b: a * b, dattv, cm["decay"])
            dd = _each(lambda a, b, c, d: a * b + c * d, dkk, cm["kk"], dqk, cm["qk"])
            dkb = _each(lambda a, b, c, d: _bdot(a, b) + c * d, dkk, kn, dkbg, cm["egc"])
            dq = _each(lambda a, b, c, d: _bdot(a, b) + c * d, dqk, kn, dqdv, cm["egc"])
            yield
            dkn = _each(lambda a, b, c, d: _bdot_tn(a, b) + _bdot_tn(c, d), dkk, cm["kb"], dqk, cm["q"])
            dkn = _each(lambda a, b, c, d, e: a + b * c + d * e, dkn, dkdv, cm["ekd"], dkb, beta)
            t_kd = _each(lambda a, b, c: _rowsum(a * b * c), dkdv, kn, cm["ekd"])
            yield
            split = _each(_split, dd)
            rows_dd = [jnp.dot(hi, ones, preferred_element_type=F32) + jnp.dot(lo, ones, preferred_element_type=F32)
                       for hi, lo in split]
            cols_dd = [lax.dot_general(hi, ones, tn, preferred_element_type=F32)
                       + lax.dot_general(lo, ones, tn, preferred_element_type=F32) for hi, lo in split]
            yield
            dgc = _each(lambda r, c, a, b, e, f, k, tk: r - c + _rowsum(a * b * e) + _rowsum(f * k) - tk,
                        rows_dd, cols_dd, dqdv, cm["q"], cm["egc"], dkbg, cm["kbg"], t_kd)
            same_b = cm["same"].astype(BF16)
            rowi = lax.broadcasted_iota(I32, (PAIR, HEAD), 0)
            dcd = _each(lambda d: jnp.where(rowi < CHUNK, d[0:1], d[8:9]), kept(dcd_s, rows=16))
            dgl = _each(lambda tk, d, c: _mask_dot(same_b, jnp.broadcast_to(tk, (PAIR, HEAD))) + d * c, t_kd, dcd, cm["cd"])
            yield
            is_last = jnp.bitwise_and(rowi, CHUNK - 1) == CHUNK - 1
            dgc = _each(lambda a, b: a + jnp.where(is_last, b, 0.0), dgc, dgl)
            r = lax.broadcasted_iota(I32, (PAIR, PAIR), 0)
            c = lax.broadcasted_iota(I32, (PAIR, PAIR), 1)
            upper_b = (cm["same"] & (r <= c)).astype(BF16)
            _put_heads(dg_ref, _each(lambda v: _mask_dot(upper_b, v), dgc))
            yield
            _put_heads(dbeta_ref, _each(lambda a, b, c, d: jnp.broadcast_to(_rowsum(a * b) + _rowsum(c * d), (PAIR, HEAD)),
                                        dkb, kn, dvb, vs))
            _put_heads(dqn_ref, _each(lambda v: v * QK_SCALE, dq))
            _put_heads(dkn_ref, dkn)
            _put_heads(dvs_ref, _each(lambda a, b: a * b, dvb, beta))
            yield

        _interleave(recurrence(), factors())

    last = n_steps - 1
    now = lambda i: (jnp.maximum(last - i, 0), 0)
    after = lambda i: (jnp.minimum(n_steps - i, last), 0)
    rows = lambda index: pl.BlockSpec((SCAN_ROWS, D_HALF), index)
    slot = lambda r, dtype: pltpu.VMEM((2, r, D_HALF), dtype)
    return _call(
        body, name="delta_bwd", grid=(n_steps + 1,),
        in_specs=[rows(now)] * 6 + [_chunk_scalar_spec(SCAN_PAIRS, now),
                                    pl.BlockSpec((2 * SCAN_PAIRS, N_HEADS, HEAD, HEAD), lambda i: (jnp.maximum(last - i, 0), 0, 0, 0))]
                 + [rows(after)] * 6,
        out_specs=[rows(after)] * 5,
        out_shape=[_sds((s, D_HALF))] * 5,
        scratch_shapes=[pltpu.VMEM((N_HEADS, HEAD, HEAD), F32), slot(SCAN_ROWS, BF16), slot(SCAN_ROWS, BF16),
                        slot(SCAN_ROWS, F32), slot(SCAN_ROWS, F32), slot(SCAN_ROWS, F32), slot(16 * SCAN_PAIRS, F32)],
        compiler_params=_params("arbitrary"),
    )(do, vn, qd, kd, w, att, cd, st, qn, kn, vs, beta, g, tm)


def _fused_call(name, n_steps, parts):
    n_in = [len(p["inputs"]) for p in parts]
    n_out = [len(p["out_shape"]) for p in parts]
    n_scr = [len(p["scratch"]) for p in parts]

    def body(*refs):
        ins, outs, scr = refs[:sum(n_in)], refs[sum(n_in):sum(n_in) + sum(n_out)], refs[sum(n_in) + sum(n_out):]
        gens, a, b, c = [], 0, 0, 0
        for p, ni, no, ns in zip(parts, n_in, n_out, n_scr):
            gens.append(p["stages"](ins[a:a + ni], outs[b:b + no], scr[c:c + ns]))
            a, b, c = a + ni, b + no, c + ns
        _interleave(*gens)

    flat = lambda key: [v for p in parts for v in p[key]]
    res = _call(
        body, name=name, grid=(n_steps,),
        in_specs=flat("in_specs"), out_specs=flat("out_specs"), out_shape=flat("out_shape"),
        scratch_shapes=flat("scratch"),
        compiler_params=_params("arbitrary"),
    )(*flat("inputs"))
    out, b = [], 0
    for no in n_out:
        out.append(res[b:b + no])
        b += no
    return out


def _pool_bwd_part(proj, dyp, pool_w, pool_scale, tile_of, n_tiles):
    s = proj.shape[0]
    t = POOL_T
    hb = t // HEAD
    last = s // HEAD - 1

    def stages(ins, outs, scratch):
        u_ref, z_ref, halo_ref, dy_ref, zn_ref, dyn_ref, pw_ref, ps_ref, band_ref, aband_ref = ins
        du_ref, dz_ref, gpw_ref, gps_ref = outs
        tile = tile_of(pl.program_id(0))

        @pl.when(pl.program_id(0) == 0)
        def _():
            gpw_ref[...] = jnp.zeros_like(gpw_ref)
            gps_ref[...] = jnp.zeros_like(gps_ref)

        live = (tile > 0).astype(F32)
        more = (tile < n_tiles - 1).astype(F32)
        groups = lambda ref: [ref[:, sl] for sl in HEAD_COLS]
        z, ps, dy = groups(z_ref), groups(ps_ref), groups(dy_ref)
        pw = [pw_ref[g] for g in HEADS]
        mix, mixed, sg, cnt = _pool_mix(groups(u_ref), [h * live for h in groups(halo_ref)], z, pw,
                                        [band_ref[g] for g in HEADS], tile * t)
        yield
        sz = _each(lambda a, b: a * b, z, sg)
        for sl, d, m, p, s_, zg in zip(HEAD_COLS, dy, mixed, ps, sg, z):
            dz_ref[:, sl] = (d * m * p * (s_ * (1.0 + zg * (1.0 - s_)))).astype(BF16)
        for sl, d, m, a in zip(HEAD_COLS, dy, mixed, sz):
            gps_ref[:, sl] += _colsum(d * m * a)
        dmixed = _each(lambda d, p, a: d * p * a, dy, ps, sz)
        yield
        for g, gp in enumerate(_each(_bdot_tn, mix, dmixed)):
            gpw_ref[g] += gp
        dmix = _each(_bdot_nt, dmixed, pw)
        yield
        dmix_n = _each(lambda d, p, zn, w_: _bdot_nt(d * more * p * (zn * _sigmoid(zn)), w_),
                       groups(dyn_ref), ps, groups(zn_ref), pw)
        yield
        scaled = [jnp.concatenate([a / c, b * (1.0 / w)], axis=0) for a, c, b, w in zip(dmix, cnt, dmix_n, WINDOWS)]
        du = _each(lambda b, s_, d: _mask_dot(b, s_) - d, [aband_ref[g] for g in HEADS], scaled, dmix)
        for sl, v in zip(HEAD_COLS, du):
            du_ref[:, sl] = v.astype(BF16)
        yield

    tile = lambda col: pl.BlockSpec((t, D_HALF), lambda i: (tile_of(i), col))
    below = lambda col: pl.BlockSpec((HEAD, D_HALF), lambda i: (jnp.minimum((tile_of(i) + 1) * hb, last), col))
    const3 = lambda shape: pl.BlockSpec(shape, lambda i: (0, 0, 0))
    return dict(
        inputs=[proj, proj, proj, dyp, proj, dyp, pool_w, pool_scale, _pool_bands(t), _pool_bands(t, anti=True)],
        in_specs=[tile(0), tile(1), pl.BlockSpec((HEAD, D_HALF), lambda i: (jnp.maximum(tile_of(i) * hb - 1, 0), 0)),
                  tile(0), below(1), below(0), const3((N_HEADS, HEAD, HEAD)), pl.BlockSpec((1, D_HALF), lambda i: (0, 0)),
                  const3((N_HEADS, t, HEAD + t)), const3((N_HEADS, t, HEAD + t))],
        out_specs=[tile(0), tile(0), const3((N_HEADS, HEAD, HEAD)), pl.BlockSpec((1, D_HALF), lambda i: (0, 0))],
        out_shape=[_sds((s, D_HALF), BF16), _sds((s, D_HALF), BF16), _sds((N_HEADS, HEAD, HEAD)), _sds((1, D_HALF))],
        scratch=[], stages=stages)


def _conv_bwd_part(proj, conv_w, a_log, dt_bias, dqn, dkn, dvs, dbeta, dg, tile_of, n_tiles):
    s = proj.shape[0]
    t = CONV_T

    def stages(ins, outs, scratch):
        (q_ref, k_ref, v_ref, hq_ref, hk_ref, hv_ref, ba_ref, cw_ref, al_ref, dtb_ref,
         dqn_ref, dkn_ref, dvs_ref, dbeta_ref, dg_ref) = ins
        oq_ref, ok_ref, ov_ref, dba_ref, gcw_out, gsm_out = outs
        below, gcw_ref, gsm_ref = scratch
        step = pl.program_id(0)

        @pl.when(step == 0)
        def _():
            gcw_ref[...] = jnp.zeros_like(gcw_ref)
            gsm_ref[...] = jnp.zeros_like(gsm_ref)
            below[...] = jnp.zeros_like(below)

        live = (tile_of(step) > 0).astype(F32)
        parts = ((q_ref, hq_ref, dqn_ref, oq_ref), (k_ref, hk_ref, dkn_ref, ok_ref), (v_ref, hv_ref, dvs_ref, ov_ref))
        for p, (x_ref, h_ref, d_ref, o_ref) in enumerate(parts):
            for h in HEADS:
                cs = HEAD_COLS[h]
                wide = slice(p * D_HALF + h * HEAD, p * D_HALF + (h + 1) * HEAD)
                cw = cw_ref[:, wide]
                taps = _conv_taps(x_ref[:, cs], h_ref[:, cs] * live)
                y = _conv_pre(taps, cw)
                sg = _sigmoid(y)
                sv = y * sg
                ds = d_ref[:, cs]
                if p < 2:
                    rn = lax.rsqrt(_rowsum(sv * sv) + EPS)
                    nrm = sv * rn
                    ds = rn * (ds - nrm * _rowsum(ds * nrm))
                dy = ds * (sg * (1.0 + y * (1.0 - sg)))
                for j in range(CONV_K):
                    gcw_ref[8 * j:8 * j + 8, wide] += _rows8(dy * taps[j])
                nxt = below[:, wide]
                acc = dy * cw[CONV_K - 1:CONV_K]
                for sft in range(1, CONV_K):
                    acc = acc + _shift_up(dy, nxt, sft) * cw[CONV_K - 1 - sft:CONV_K - sft]
                o_ref[:, cs] = acc.astype(BF16)
                below[:, wide] = dy[0:8]
                yield

        ba = ba_ref[...]
        lane = lax.broadcasted_iota(I32, (t, HEAD), 1)
        lane8 = lax.broadcasted_iota(I32, (8, HEAD), 1)
        dba = jnp.zeros((t, HEAD), F32)
        gsm = jnp.zeros((8, HEAD), F32)
        for h in HEADS:
            beta = _sigmoid(ba[:, h:h + 1])
            dbeta = dbeta_ref[:, h * HEAD:h * HEAD + 1]
            xg = ba[:, N_HEADS + h:N_HEADS + h + 1] + dtb_ref[0:1, h:h + 1]
            nexp = -jnp.exp(al_ref[0:1, h:h + 1])
            dgv = dg_ref[:, h * HEAD:h * HEAD + 1]
            da = dgv * nexp * _sigmoid(xg)
            dba = dba + jnp.where(lane == h, dbeta * beta * (1.0 - beta), 0.0) + jnp.where(lane == N_HEADS + h, da, 0.0)
            gsm = (gsm + jnp.where(lane8 == h, _rows8(dgv * nexp * _softplus(xg)), 0.0)
                   + jnp.where(lane8 == N_HEADS + h, _rows8(da), 0.0))
        dba_ref[...] = jnp.zeros_like(dba_ref)
        dba_ref[:, :HEAD] = dba.astype(BF16)
        gsm_ref[...] += gsm
        yield

        @pl.when(step == n_tiles - 1)
        def _():
            gcw_out[...] = jnp.zeros_like(gcw_out)
            for j in range(CONV_K):
                gcw_out[j:j + 1, :] = _colsum(gcw_ref[8 * j:8 * j + 8, :])
            gsm_out[...] = jnp.broadcast_to(_colsum(gsm_ref[...]), (8, HEAD))

    row = pl.BlockSpec((t, D_HALF), lambda i: (tile_of(i), 0))
    const = lambda shape: pl.BlockSpec(shape, lambda i: (0, 0))
    return dict(
        inputs=[proj] * 7 + [conv_w, a_log, dt_bias, dqn, dkn, dvs, dbeta, dg],
        in_specs=_conv_specs(t, tile_of) + [pl.BlockSpec((t, HEAD), lambda i: (tile_of(i), COL_BA // HEAD)),
                                            const((CONV_K, 3 * D_HALF)), const((1, N_HEADS)), const((1, N_HEADS))] + [row] * 5,
        out_specs=[row, row, row, row, const((8, 3 * D_HALF)), const((8, HEAD))],
        out_shape=[_sds((s, D_HALF), BF16)] * 4 + [_sds((8, 3 * D_HALF)), _sds((8, HEAD))],
        scratch=[pltpu.VMEM((8, 3 * D_HALF), F32), pltpu.VMEM((8 * CONV_K, 3 * D_HALF), F32), pltpu.VMEM((8, HEAD), F32)],
        stages=stages)


def _pool_fwd_part(proj, pool_w, pool_scale):
    s = proj.shape[0]
    t = POOL_T
    hb = t // HEAD

    def stages(ins, outs, scratch, tile=None):
        u_ref, z_ref, halo_ref, pw_ref, ps_ref, band_ref = ins
        y_ref, = outs
        i = pl.program_id(0) if tile is None else tile
        live = (i > 0).astype(F32)
        groups = lambda ref: [ref[:, sl] for sl in HEAD_COLS]
        z = groups(z_ref)
        u, halo = groups(u_ref), [h * live for h in groups(halo_ref)]
        yield
        _, mixed, sg, _ = _pool_mix(u, halo, z, [pw_ref[g] for g in HEADS], [band_ref[g] for g in HEADS], i * t)
        yield
        for sl, m, zg, s_ in zip(HEAD_COLS, mixed, z, sg):
            y_ref[:, sl] = m * ps_ref[:, sl] * (zg * s_)
        yield

    const3 = lambda shape: pl.BlockSpec(shape, lambda i: (0, 0, 0))
    return dict(
        inputs=[proj, proj, proj, pool_w, pool_scale, _pool_bands(t)],
        in_specs=[pl.BlockSpec((t, D_HALF), lambda i: (i, 0)), pl.BlockSpec((t, D_HALF), lambda i: (i, 1)),
                  pl.BlockSpec((HEAD, D_HALF), lambda i: (jnp.maximum(i * hb - 1, 0), 0)),
                  const3((N_HEADS, HEAD, HEAD)), pl.BlockSpec((1, D_HALF), lambda i: (0, 0)), const3((N_HEADS, t, HEAD + t))],
        out_specs=[pl.BlockSpec((t, D_HALF), lambda i: (i, 0))], out_shape=[_sds((s, D_HALF))],
        scratch=[], stages=stages)


def _conv_fwd_part(proj, conv_w, a_log, dt_bias):
    s = proj.shape[0]
    t = CONV_T

    def stages(ins, outs, scratch, tile=None):
        q_ref, k_ref, v_ref, hq_ref, hk_ref, hv_ref, ba_ref, cw_ref, al_ref, dtb_ref = ins
        qn_ref, kn_ref, vs_ref, beta_ref, g_ref = outs
        live = ((pl.program_id(0) if tile is None else tile) > 0).astype(F32)
        for p, (x_ref, h_ref, o_ref) in enumerate(((q_ref, hq_ref, qn_ref), (k_ref, hk_ref, kn_ref), (v_ref, hv_ref, vs_ref))):
            for h in HEADS:
                cs = HEAD_COLS[h]
                taps = _conv_taps(x_ref[:, cs], h_ref[:, cs] * live)
                y = _conv_pre(taps, cw_ref[:, p * D_HALF + h * HEAD:p * D_HALF + (h + 1) * HEAD])
                sv = y * _sigmoid(y)
                o_ref[:, cs] = sv if p == 2 else sv * lax.rsqrt(_rowsum(sv * sv) + EPS)
                yield
        ba = ba_ref[...]
        for h in HEADS:
            beta = _sigmoid(ba[:, h:h + 1])
            gl = -jnp.exp(al_ref[0:1, h:h + 1]) * _softplus(ba[:, N_HEADS + h:N_HEADS + h + 1] + dtb_ref[0:1, h:h + 1])
            beta_ref[:, HEAD_COLS[h]] = jnp.broadcast_to(beta, (t, HEAD))
            g_ref[:, HEAD_COLS[h]] = jnp.broadcast_to(gl, (t, HEAD))
        yield

    row = pl.BlockSpec((t, D_HALF), lambda i: (i, 0))
    const = lambda shape: pl.BlockSpec(shape, lambda i: (0, 0))
    return dict(
        inputs=[proj] * 7 + [conv_w, a_log, dt_bias],
        in_specs=_conv_specs(t) + [pl.BlockSpec((t, HEAD), lambda i: (i, COL_BA // HEAD)),
                                   const((CONV_K, 3 * D_HALF)), const((1, N_HEADS)), const((1, N_HEADS))],
        out_specs=[row] * 5, out_shape=[_sds((s, D_HALF))] * 5, scratch=[], stages=stages)


def _front_fwd(x, norm_w, w_pad, conv_w, a_log, dt_bias, pool_w, pool_scale):
    s = x.shape[0]
    t = CONV_T
    n_tiles = s // t
    assert POOL_T == CONV_T
    like_proj = _sds((s, N_IN_PAD))
    conv = _conv_fwd_part(like_proj, conv_w, a_log, dt_bias)
    pool = _pool_fwd_part(like_proj, pool_w, pool_scale)
    bands = pool["inputs"][-1]
    n_col = 5
    cw = N_IN_PAD // n_col

    def body(x_ref, nw_ref, w_ref, cw_ref, al_ref, dtb_ref, pw_ref, ps_ref, band_ref,
             proj_ref, nt_ref, qn_ref, kn_ref, vs_ref, beta_ref, g_ref, y_ref, prev):
        i = pl.program_id(0)

        @pl.when(i == 0)
        def _():
            prev[...] = jnp.zeros_like(prev)

        tile = jnp.maximum(i - 1, 0)
        main, above8, above = pl.ds(HEAD, t), pl.ds(HEAD - 8, 8), pl.ds(0, HEAD)
        cols = lambda rows, c0, width=D_HALF: prev.at[rows, pl.ds(c0, width)]
        conv_ins = (cols(main, 2 * D_HALF), cols(main, 3 * D_HALF), cols(main, 4 * D_HALF),
                    cols(above8, 2 * D_HALF), cols(above8, 3 * D_HALF), cols(above8, 4 * D_HALF),
                    cols(main, COL_BA, HEAD), cw_ref, al_ref, dtb_ref)
        pool_ins = (cols(main, 0), cols(main, D_HALF), cols(above, 0), pw_ref, ps_ref, band_ref)

        def projection():
            xv = x_ref[...]
            r = lax.rsqrt(jnp.mean(xv * xv, axis=-1, keepdims=True) + EPS)
            nv = xv * r * nw_ref[...]
            nt_ref[...] = nv.T.astype(BF16)
            nb = nv.astype(BF16)
            yield
            for c in range(n_col):
                proj_ref[:, c * cw:(c + 1) * cw] = jnp.dot(nb, w_ref[:, c * cw:(c + 1) * cw], preferred_element_type=F32)
                yield

        _interleave(projection(),
                    conv["stages"](conv_ins, (qn_ref, kn_ref, vs_ref, beta_ref, g_ref), (), tile),
                    pool["stages"](pool_ins, (y_ref,), (), tile))
        prev[0:HEAD] = prev[t:t + HEAD]
        prev[HEAD:HEAD + t] = proj_ref[...]

    last = n_tiles - 1
    now = lambda i: (jnp.minimum(i, last), 0)
    before = lambda i: (jnp.maximum(i - 1, 0), 0)
    const = lambda a: pl.BlockSpec(a.shape, lambda i: (0,) * a.ndim)
    half = pl.BlockSpec((t, D_HALF), before)
    return _call(
        body, name="front_fwd", grid=(n_tiles + 1,),
        in_specs=[pl.BlockSpec((t, D_MODEL), now), const(norm_w), const(w_pad), const(conv_w), const(a_log), const(dt_bias),
                  const(pool_w), const(pool_scale), const(bands)],
        out_specs=[pl.BlockSpec((t, N_IN_PAD), now), pl.BlockSpec((D_MODEL, t), lambda i: (0, jnp.minimum(i, last)))]
                  + [half] * 6,
        out_shape=[_sds((s, N_IN_PAD)), _sds((D_MODEL, s), BF16)] + [_sds((s, D_HALF))] * 6,
        scratch_shapes=[pltpu.VMEM((HEAD + t, N_IN_PAD), F32)],
        compiler_params=_params("arbitrary"),
    )(x, norm_w, w_pad, conv_w, a_log, dt_bias, pool_w, pool_scale, bands)


def _conv_pool_fwd(proj, conv_w, a_log, dt_bias, pool_w, pool_scale):
    assert POOL_T == CONV_T
    return _fused_call("conv_pool_fwd", proj.shape[0] // CONV_T, [
        _conv_fwd_part(proj, conv_w, a_log, dt_bias), _pool_fwd_part(proj, pool_w, pool_scale)])


def _conv_pool_bwd(proj, conv_w, a_log, dt_bias, dqn, dkn, dvs, dbeta, dg, dyp, pool_w, pool_scale):
    n_tiles = proj.shape[0] // CONV_T
    assert POOL_T == CONV_T
    tile_of = lambda i: n_tiles - 1 - i
    return _fused_call("conv_pool_bwd", n_tiles, [
        _conv_bwd_part(proj, conv_w, a_log, dt_bias, dqn, dkn, dvs, dbeta, dg, tile_of, n_tiles),
        _pool_bwd_part(proj, dyp, pool_w, pool_scale, tile_of, n_tiles)])


def _rows8(x):
    acc = x[0:8]
    for r in range(8, x.shape[0], 8):
        acc = acc + x[r:r + 8]
    return acc


def _conv_bwd(proj, conv_w, a_log, dt_bias, dqn, dkn, dvs, dbeta, dg):
    s = proj.shape[0]
    t = CONV_T
    n_tiles = s // t
    n_sub = t // CONV_SUB
    tile_of = lambda i: n_tiles - 1 - i

    def body(q_ref, k_ref, v_ref, hq_ref, hk_ref, hv_ref, ba_ref, cw_ref, al_ref, dtb_ref,
             dqn_ref, dkn_ref, dvs_ref, dbeta_ref, dg_ref, oq_ref, ok_ref, ov_ref, dba_ref, gcw_out, gsm_out,
             below, gcw_ref, gsm_ref):
        @pl.when(pl.program_id(0) == 0)
        def _():
            gcw_ref[...] = jnp.zeros_like(gcw_ref)
            gsm_ref[...] = jnp.zeros_like(gsm_ref)
            below[...] = jnp.zeros_like(below)

        live = (pl.program_id(0) < n_tiles - 1).astype(F32)
        parts = ((q_ref, hq_ref, dqn_ref, oq_ref), (k_ref, hk_ref, dkn_ref, ok_ref), (v_ref, hv_ref, dvs_ref, ov_ref))
        lane = lax.broadcasted_iota(I32, (CONV_SUB, HEAD), 1)
        lane8 = lax.broadcasted_iota(I32, (8, HEAD), 1)

        def sub_tile(r0, first):
            rows = pl.ds(r0, CONV_SUB)
            for p, (x_ref, h_ref, d_ref, o_ref) in enumerate(parts):
                for h in HEADS:
                    cs = HEAD_COLS[h]
                    wide = slice(p * D_HALF + h * HEAD, p * D_HALF + (h + 1) * HEAD)
                    cw = cw_ref[:, wide]
                    prev8 = h_ref[:, cs] * live if first else x_ref[pl.ds(r0 - 8, 8), cs]
                    taps = _conv_taps(x_ref[rows, cs], prev8)
                    y = _conv_pre(taps, cw)
                    sg = _sigmoid(y)
                    sv = y * sg
                    ds = d_ref[rows, cs]
                    if p < 2:
                        rn = lax.rsqrt(_rowsum(sv * sv) + EPS)
                        nrm = sv * rn
                        ds = rn * (ds - nrm * _rowsum(ds * nrm))
                    dy = ds * (sg * (1.0 + y * (1.0 - sg)))
                    for j in range(CONV_K):
                        gcw_ref[8 * j:8 * j + 8, wide] += _rows8(dy * taps[j])
                    nxt = below[:, wide]
                    acc = dy * cw[CONV_K - 1:CONV_K]
                    for sft in range(1, CONV_K):
                        acc = acc + _shift_up(dy, nxt, sft) * cw[CONV_K - 1 - sft:CONV_K - sft]
                    o_ref[rows, cs] = acc.astype(BF16)
                    below[:, wide] = dy[0:8]

            ba = ba_ref[rows, :]
            dba = jnp.zeros((CONV_SUB, HEAD), F32)
            gsm = jnp.zeros((8, HEAD), F32)
            for h in HEADS:
                beta = _sigmoid(ba[:, h:h + 1])
                dbeta = dbeta_ref[rows, h * HEAD:h * HEAD + 1]
                xg = ba[:, N_HEADS + h:N_HEADS + h + 1] + dtb_ref[0:1, h:h + 1]
                nexp = -jnp.exp(al_ref[0:1, h:h + 1])
                dgv = dg_ref[rows, h * HEAD:h * HEAD + 1]
                da = dgv * nexp * _sigmoid(xg)
                dba = dba + jnp.where(lane == h, dbeta * beta * (1.0 - beta), 0.0) + jnp.where(lane == N_HEADS + h, da, 0.0)
                gsm = (gsm + jnp.where(lane8 == h, _rows8(dgv * nexp * _softplus(xg)), 0.0)
                       + jnp.where(lane8 == N_HEADS + h, _rows8(da), 0.0))
            dba_ref[rows, :] = jnp.zeros((CONV_SUB, D_HALF), BF16)
            dba_ref[rows, :HEAD] = dba.astype(BF16)
            gsm_ref[...] += gsm

        def step(k, carry):
            sub_tile(pl.multiple_of((n_sub - 1 - k) * CONV_SUB, CONV_SUB), False)
            return carry

        lax.fori_loop(0, n_sub - 1, step, 0)
        sub_tile(0, True)

        @pl.when(pl.program_id(0) == n_tiles - 1)
        def _():
            gcw_out[...] = jnp.zeros_like(gcw_out)
            for j in range(CONV_K):
                gcw_out[j:j + 1, :] = _colsum(gcw_ref[8 * j:8 * j + 8, :])
            gsm_out[...] = jnp.broadcast_to(_colsum(gsm_ref[...]), (8, HEAD))

    row = pl.BlockSpec((t, D_HALF), lambda i: (tile_of(i), 0))
    const = lambda shape: pl.BlockSpec(shape, lambda i: (0, 0))
    return _call(
        body, name="conv_bwd", grid=(n_tiles,),
        in_specs=_conv_specs(t, tile_of) + [pl.BlockSpec((t, HEAD), lambda i: (tile_of(i), COL_BA // HEAD)),
                                            const((CONV_K, 3 * D_HALF)), const((1, N_HEADS)), const((1, N_HEADS))] + [row] * 5,
        out_specs=[row, row, row, row, const((8, 3 * D_HALF)), const((8, HEAD))],
        out_shape=[_sds((s, D_HALF), BF16)] * 4 + [_sds((8, 3 * D_HALF)), _sds((8, HEAD))],
        scratch_shapes=[pltpu.VMEM((8, 3 * D_HALF), F32), pltpu.VMEM((8 * CONV_K, 3 * D_HALF), F32),
                        pltpu.VMEM((8, HEAD), F32)],
        compiler_params=_params("arbitrary"),
    )(proj, proj, proj, proj, proj, proj, proj, conv_w, a_log, dt_bias, dqn, dkn, dvs, dbeta, dg)


def _conv_bwd_pre(proj, conv_w, a_log, dt_bias, dqn, dkn, dvs, dbeta, dg):
    s = proj.shape[0]
    t = CONV_T

    def body(q_ref, k_ref, v_ref, hq_ref, hk_ref, hv_ref, ba_ref, cw_ref, al_ref, dtb_ref,
             dqn_ref, dkn_ref, dvs_ref, dbeta_ref, dg_ref, dyq_ref, dyk_ref, dyv_ref, dba_ref, gcw_ref, gsm_ref):
        @pl.when(pl.program_id(0) == 0)
        def _():
            gcw_ref[...] = jnp.zeros_like(gcw_ref)
            gsm_ref[...] = jnp.zeros_like(gsm_ref)

        live = (pl.program_id(0) > 0).astype(F32)
        parts = ((q_ref, hq_ref, dqn_ref, dyq_ref), (k_ref, hk_ref, dkn_ref, dyk_ref), (v_ref, hv_ref, dvs_ref, dyv_ref))
        for p, (x_ref, h_ref, d_ref, dy_ref) in enumerate(parts):
            cols = slice(p * D_HALF, (p + 1) * D_HALF)
            taps = _conv_taps(x_ref[...], h_ref[...] * live)
            y = _conv_pre(taps, cw_ref[:, cols])
            sg = _sigmoid(y)
            sv = y * sg
            if p == 2:
                ds = d_ref[...]
            else:
                segs = []
                for h in HEADS:
                    seg = _head(sv, h)
                    rn = lax.rsqrt(_rowsum(seg * seg) + EPS)
                    nrm = seg * rn
                    dn = d_ref[:, HEAD_COLS[h]]
                    segs.append(rn * (dn - nrm * _rowsum(dn * nrm)))
                ds = jnp.concatenate(segs, axis=1)
            dy = ds * (sg * (1.0 + y * (1.0 - sg)))
            dy_ref[...] = dy
            for j in range(CONV_K):
                gcw_ref[j:j + 1, cols] += _colsum(dy * taps[j])

        ba = ba_ref[...]
        lane = lax.broadcasted_iota(I32, (t, HEAD), 1)
        lane1 = lax.broadcasted_iota(I32, (1, HEAD), 1)
        dba = jnp.zeros((t, HEAD), F32)
        gsm = jnp.zeros((1, HEAD), F32)
        for h in HEADS:
            beta = _sigmoid(ba[:, h:h + 1])
            dbeta = dbeta_ref[:, h * HEAD:h * HEAD + 1]
            xg = ba[:, N_HEADS + h:N_HEADS + h + 1] + dtb_ref[0:1, h:h + 1]
            nexp = -jnp.exp(al_ref[0:1, h:h + 1])
            dgv = dg_ref[:, h * HEAD:h * HEAD + 1]
            da = dgv * nexp * _sigmoid(xg)
            dba = dba + jnp.where(lane == h, dbeta * beta * (1.0 - beta), 0.0) + jnp.where(lane == N_HEADS + h, da, 0.0)
            gsm = (gsm + jnp.where(lane1 == h, _colsum(dgv * nexp * _softplus(xg)), 0.0)
                   + jnp.where(lane1 == N_HEADS + h, _colsum(da), 0.0))
        dba_ref[...] = jnp.zeros_like(dba_ref)
        dba_ref[:, :HEAD] = dba.astype(BF16)
        gsm_ref[0:1, :] += gsm

    row = pl.BlockSpec((t, D_HALF), lambda i: (i, 0))
    return _call(
        body, name="conv_bwd_pre", grid=(s // t,),
        in_specs=_conv_specs(t) + [pl.BlockSpec((t, HEAD), lambda i: (i, COL_BA // HEAD)),
                                   pl.BlockSpec((CONV_K, 3 * D_HALF), lambda i: (0, 0)),
                                   pl.BlockSpec((1, N_HEADS), lambda i: (0, 0)),
                                   pl.BlockSpec((1, N_HEADS), lambda i: (0, 0))] + [row] * 5,
        out_specs=[row, row, row, row,
                   pl.BlockSpec((8, 3 * D_HALF), lambda i: (0, 0)), pl.BlockSpec((8, HEAD), lambda i: (0, 0))],
        out_shape=[_sds((s, D_HALF))] * 3 + [_sds((s, D_HALF), BF16), _sds((8, 3 * D_HALF)), _sds((8, HEAD))],
        compiler_params=_params("arbitrary"),
    )(proj, proj, proj, proj, proj, proj, proj, conv_w, a_log, dt_bias, dqn, dkn, dvs, dbeta, dg)


def _conv_bwd_in(dyq, dyk, dyv, conv_w):
    s = dyq.shape[0]
    t = CONV_T
    last = s // 8 - 1

    def body(q_ref, k_ref, v_ref, nq_ref, nk_ref, nv_ref, cw_ref, oq_ref, ok_ref, ov_ref):
        more = (pl.program_id(0) < pl.num_programs(0) - 1).astype(F32)
        for p, (d_ref, n_ref, o_ref) in enumerate(((q_ref, nq_ref, oq_ref), (k_ref, nk_ref, ok_ref), (v_ref, nv_ref, ov_ref))):
            cw = cw_ref[:, p * D_HALF:(p + 1) * D_HALF]
            dy = d_ref[...]
            nxt = n_ref[...] * more
            acc = dy * cw[3:4]
            for sft in (1, 2, 3):
                acc = acc + _shift_up(dy, nxt, sft) * cw[3 - sft:4 - sft]
            o_ref[...] = acc.astype(BF16)

    row = pl.BlockSpec((t, D_HALF), lambda i: (i, 0))
    nxt = pl.BlockSpec((8, D_HALF), lambda i: (jnp.minimum((i + 1) * (t // 8), last), 0))
    return _call(
        body, name="conv_bwd_in", grid=(s // t,),
        in_specs=[row] * 3 + [nxt] * 3 + [pl.BlockSpec((CONV_K, 3 * D_HALF), lambda i: (0, 0))],
        out_specs=[row] * 3, out_shape=[_sds((s, D_HALF), BF16)] * 3,
        compiler_params=_params("arbitrary"),
    )(dyq, dyk, dyv, dyq, dyk, dyv, conv_w)


IN_T = 512


def _in_bwd(x, dh, norm_w, w_pad, pieces):
    s = x.shape[0]
    t = IN_T
    widths = [D_HALF] * 6 + [N_IN_PAD - COL_BA]

    def body(*refs):
        x_ref, dh_ref, nw_ref, w_ref = refs[:4]
        p_refs = refs[4:4 + len(pieces)]
        gx_ref, gnw_ref = refs[4 + len(pieces):]

        @pl.when(pl.program_id(0) == 0)
        def _():
            gnw_ref[...] = jnp.zeros_like(gnw_ref)

        dn = jnp.zeros((t, D_MODEL), F32)
        col = 0
        for p_ref, wd in zip(p_refs, widths):
            dn = dn + _bdot_nt(p_ref[...], w_ref[:, col:col + wd])
            col += wd
        xv = x_ref[...]
        r = lax.rsqrt(jnp.mean(xv * xv, axis=-1, keepdims=True) + EPS)
        xhat = xv * r
        gnw_ref[...] += _colsum(dn * xhat)
        dxh = dn * nw_ref[...]
        gx_ref[...] = dh_ref[...] + r * (dxh - xhat * jnp.mean(dxh * xhat, axis=-1, keepdims=True))

    wide = pl.BlockSpec((t, D_MODEL), lambda i: (i, 0))
    return _call(
        body, name="in_bwd", grid=(s // t,),
        in_specs=[wide, wide, pl.BlockSpec((1, D_MODEL), lambda i: (0, 0)),
                  pl.BlockSpec((D_MODEL, N_IN_PAD), lambda i: (0, 0))]
                 + [pl.BlockSpec((t, wd), lambda i: (i, 0)) for wd in widths],
        out_specs=[wide, pl.BlockSpec((1, D_MODEL), lambda i: (0, 0))],
        out_shape=[_sds((s, D_MODEL)), _sds((1, D_MODEL))],
        compiler_params=_params("arbitrary"),
    )(x, dh, norm_w, w_pad, *pieces)


def _adamw_shard(name, w, g_own, g_got, cidx, m, v):
    _, r, c = w.shape
    half = r // 2
    rows = 256 if half % 256 == 0 else half
    per_half = half // rows

    def body(c_ref, w_ref, go_ref, gg_ref, m_ref, v_ref, gout_ref, d_ref, nm_ref, nv_ref):
        mine = (pl.program_id(0) // per_half) == c_ref[0]
        gv = jnp.where(mine, go_ref[:, :c], gg_ref[:, :c])
        gout_ref[0] = gv
        mn = ADAM_B1 * m_ref[0] + (1.0 - ADAM_B1) * gv
        vn = ADAM_B2 * v_ref[0] + (1.0 - ADAM_B2) * (gv * gv)
        m_hat = mn / (1.0 - ADAM_B1 ** ADAM_STEP)
        v_hat = vn / (1.0 - ADAM_B2 ** ADAM_STEP)
        d_ref[0] = -ADAM_LR * (m_hat / (jnp.sqrt(v_hat) + ADAM_EPS) + ADAM_WD * w_ref[0])
        nm_ref[0] = mn
        nv_ref[0] = vn

    blk = pl.BlockSpec((1, rows, c), lambda i, c_ref: (0, i, 0))
    gblk = pl.BlockSpec((rows, g_own.shape[1]), lambda i, c_ref: (i % per_half, 0))
    return _call(
        body, name=name,
        grid_spec=pltpu.PrefetchScalarGridSpec(
            num_scalar_prefetch=1, grid=(2 * per_half,),
            in_specs=[blk, gblk, gblk, blk, blk], out_specs=[blk] * 4),
        out_shape=[_sds((1, r, c))] * 4,
        compiler_params=_params("arbitrary"),
    )(cidx, w, g_own, g_got, m, v)


def _adamw_tiles(name, w, g, m, v):
    n = w.shape[0]
    nb = 77 if n % 77 == 0 else n

    def body(w_ref, g_ref, m_ref, v_ref, d_ref, nm_ref, nv_ref):
        gv = g_ref[...]
        mn = ADAM_B1 * m_ref[...] + (1.0 - ADAM_B1) * gv
        vn = ADAM_B2 * v_ref[...] + (1.0 - ADAM_B2) * (gv * gv)
        m_hat = mn / (1.0 - ADAM_B1 ** ADAM_STEP)
        v_hat = vn / (1.0 - ADAM_B2 ** ADAM_STEP)
        d_ref[...] = -ADAM_LR * (m_hat / (jnp.sqrt(v_hat) + ADAM_EPS) + ADAM_WD * w_ref[...])
        nm_ref[...] = mn
        nv_ref[...] = vn

    blk = pl.BlockSpec((nb, 8, HEAD), lambda i: (i, 0, 0))
    return _call(
        body, name=name, grid=(n // nb,),
        in_specs=[blk] * 4, out_specs=[blk] * 3, out_shape=[_sds(w.shape)] * 3,
        compiler_params=_params("arbitrary"),
    )(w, g, m, v)


def _exchange(name, inputs, out_shapes, phases):
    n_in = len(inputs)
    n_out = len(out_shapes)
    n_cp = sum(len(p) for p in phases)

    def body(*refs):
        ins, outs = refs[:n_in], refs[n_in:n_in + n_out]
        send, recv = refs[n_in + n_out:]
        pos = (lax.axis_index("x"), lax.axis_index("y"), lax.axis_index("c"))
        k = 0
        for phase in phases:
            cps = []
            for src, dst, target in phase:
                cps.append(pltpu.make_async_remote_copy(
                    src_ref=src(ins, outs, pos), dst_ref=dst(ins, outs, pos), send_sem=send.at[k], recv_sem=recv.at[k],
                    device_id=target(pos), device_id_type=pl.DeviceIdType.MESH))
                k += 1
            for cp in cps:
                cp.start()
            for cp in cps:
                cp.wait()

    anyspec = pl.BlockSpec(memory_space=pl.ANY)
    return _call(
        body, name=name,
        in_specs=[anyspec] * n_in, out_specs=[anyspec] * n_out, out_shape=list(out_shapes),
        scratch_shapes=[pltpu.SemaphoreType.DMA((n_cp,)), pltpu.SemaphoreType.DMA((n_cp,))],
    )(*inputs)


def _exchange_start(name, inputs, out_shapes, copies):
    n_in, n_out, n_cp = len(inputs), len(out_shapes), len(copies)

    def body(*refs):
        ins, lands = refs[:n_in], refs[n_in:n_in + n_out]
        sems = refs[n_in + n_out:n_in + n_out + 2 * n_cp]
        token = refs[-1]
        pos = (lax.axis_index("x"), lax.axis_index("y"), lax.axis_index("c"))
        for k, (src, dst, target) in enumerate(copies):
            pltpu.make_async_remote_copy(
                src_ref=src(ins, lands, pos), dst_ref=dst(ins, lands, pos), send_sem=sems[2 * k], recv_sem=sems[2 * k + 1],
                device_id=target(pos), device_id_type=pl.DeviceIdType.MESH).start()
        token[...] = jnp.zeros_like(token)

    hbm = pl.BlockSpec(memory_space=pltpu.HBM)
    sem = pl.BlockSpec(memory_space=pltpu.SEMAPHORE)
    bufs = list(inputs) + [lax.empty(o.shape, o.dtype) for o in out_shapes]
    outs = _call(
        body, name=name,
        out_shape=tuple([pltpu.SemaphoreType.DMA(())] * (2 * n_cp) + [pltpu.HBM(b.shape, b.dtype) for b in bufs]
                        + [_sds((8, HEAD))]),
        in_specs=[hbm] * len(bufs),
        out_specs=tuple([sem] * (2 * n_cp) + [hbm] * len(bufs) + [pl.BlockSpec(memory_space=pltpu.VMEM)]),
        input_output_aliases={i: 2 * n_cp + i for i in range(len(bufs))},
        compiler_params=pltpu.CompilerParams(has_side_effects=pltpu.SideEffectType.DATAFLOW_SIDE_EFFECTING),
    )(*[pltpu.with_memory_space_constraint(b, pltpu.HBM) for b in bufs])
    return outs[:2 * n_cp], outs[2 * n_cp:2 * n_cp + n_in], outs[2 * n_cp + n_in:-1], outs[-1]


def _exchange_wait(name, sems, sources, lands, copies, after):
    n_in, n_out, n_cp = len(sources), len(lands), len(copies)

    def body(*refs):
        ins, zones = refs[:n_in], refs[n_in:n_in + n_out]
        sem_refs = refs[n_in + n_out:n_in + n_out + 2 * n_cp]
        pos = (lax.axis_index("x"), lax.axis_index("y"), lax.axis_index("c"))
        for k, (src, dst, target) in enumerate(copies):
            cp = pltpu.make_async_remote_copy(
                src_ref=src(ins, zones, pos), dst_ref=dst(ins, zones, pos), send_sem=sem_refs[2 * k],
                recv_sem=sem_refs[2 * k + 1], device_id=target(pos), device_id_type=pl.DeviceIdType.MESH)
            cp.wait_send()
            cp.wait_recv()

    hbm = pl.BlockSpec(memory_space=pltpu.HBM)
    sem = pl.BlockSpec(memory_space=pltpu.SEMAPHORE)
    bufs = list(sources) + list(lands)
    outs = _call(
        body, name=name,
        out_shape=tuple(pltpu.HBM(b.shape, b.dtype) for b in bufs),
        in_specs=[hbm] * len(bufs) + [sem] * (2 * n_cp) + [pl.BlockSpec(memory_space=pl.ANY)],
        out_specs=tuple([hbm] * len(bufs)),
        input_output_aliases={i: i for i in range(len(bufs))},
        compiler_params=pltpu.CompilerParams(has_side_effects=pltpu.SideEffectType.DATAFLOW_SIDE_EFFECTING),
    )(*bufs, *sems, after)
    return outs[:n_in], outs[n_in:]


def _allreduce_tile(name, v):
    def body(v_ref, out_ref, slots, send, recv):
        x, y, c = lax.axis_index("x"), lax.axis_index("y"), lax.axis_index("c")
        me = 4 * x + 2 * y + c
        slots[me] = v_ref[...]
        cps = []
        for k in range(1, 8):
            peer = (x ^ (k >> 2), y ^ ((k >> 1) & 1), c ^ (k & 1))
            cps.append(pltpu.make_async_remote_copy(
                src_ref=v_ref, dst_ref=slots.at[me], send_sem=send.at[k - 1], recv_sem=recv.at[k - 1],
                device_id=peer, device_id_type=pl.DeviceIdType.MESH))
        for cp in cps:
            cp.start()
        for cp in cps:
            cp.wait()
        acc = slots[0]
        for i in range(1, 8):
            acc = acc + slots[i]
        out_ref[...] = acc

    vm = pl.BlockSpec(memory_space=pltpu.VMEM)
    return _call(
        body, name=name, in_specs=[vm], out_specs=vm, out_shape=_sds(v.shape),
        scratch_shapes=[pltpu.VMEM((8,) + v.shape, F32), pltpu.SemaphoreType.DMA((7,)), pltpu.SemaphoreType.DMA((7,))],
    )(v)


def _chip(pos):
    return 2 * pos[0] + pos[1]


def _other_chip(pos, mask):
    x, y, c = pos
    return (x ^ (mask >> 1), y ^ (mask & 1), c)


def _sibling(pos):
    return (pos[0], pos[1], 1 - pos[2])


def _gather_weights(wb, cb):
    rows = wb.shape[0] // 2
    x_nb, y_nb, diag = CHIP_MASKS

    def part(pos, mask, quarter=None):
        start = pos[2] * rows if quarter is None else pos[2] * rows + quarter * (rows // 2)
        return lambda outs: outs[0].at[_chip(pos) ^ mask, pl.ds(start, rows if quarter is None else rows // 2)]

    def passed_on(mask, to, quarter=None):
        return (lambda ins, outs, pos: part(pos, mask, quarter)(outs), lambda ins, outs, pos: part(pos, mask, quarter)(outs), to)

    first = [(lambda ins, outs, pos: ins[0].at[pl.ds(pos[2] * rows, rows)], lambda ins, outs, pos: part(pos, 0)(outs),
              functools.partial(_other_chip, mask=mask)) for mask in (x_nb, y_nb)]
    first += [(lambda ins, outs, pos: ins[1], lambda ins, outs, pos: outs[1].at[_chip(pos)],
               functools.partial(_other_chip, mask=mask)) for mask in CHIP_MASKS]
    second = [passed_on(x_nb, functools.partial(_other_chip, mask=y_nb), quarter=0),
              passed_on(y_nb, functools.partial(_other_chip, mask=x_nb), quarter=1),
              passed_on(x_nb, _sibling), passed_on(y_nb, _sibling)]
    third = [passed_on(diag, _sibling)]
    return _exchange("gather_weights", [wb, cb], [_sds((4,) + wb.shape, wb.dtype), _sds((4,) + cb.shape, cb.dtype)],
                     [first, second, third])


def _assemble_w_in(gw, wb, jidx):
    m = gw.shape[1]

    def body(j_ref, g_ref, wb_ref, o_ref):
        step = pl.program_id(0)

        @pl.when(step == 0)
        def _():
            o_ref[...] = jnp.zeros_like(o_ref)

        blk = jnp.where(step == j_ref[0], wb_ref[...], g_ref[0]).astype(F32)
        lane = lax.broadcasted_iota(I32, (m, BLK_IN_PAD), 1)
        for j in range(4):
            @pl.when(step == j)
            def _(j=j):
                base = j * BLK_IN // HEAD * HEAD
                shift = j * BLK_IN - base
                moved = pltpu.roll(blk, shift, 1) if shift else blk
                window = o_ref[:, base:base + BLK_IN_PAD].astype(F32)
                mine = (lane >= shift) & (lane < shift + BLK_IN)
                o_ref[:, base:base + BLK_IN_PAD] = jnp.where(mine, moved, window).astype(BF16)

    return _call(
        body, name="assemble_w_in",
        grid_spec=pltpu.PrefetchScalarGridSpec(
            num_scalar_prefetch=1, grid=(4,),
            in_specs=[pl.BlockSpec((1, m, BLK_IN_PAD), lambda j, j_ref: (j, 0, 0)),
                      pl.BlockSpec((m, BLK_IN_PAD), lambda j, j_ref: (0, 0))],
            out_specs=pl.BlockSpec((m, N_IN_PAD), lambda j, j_ref: (0, 0))),
        out_shape=_sds((m, N_IN_PAD), BF16),
        compiler_params=_params("arbitrary"),
    )(jidx, gw, wb)


def _gather_blocks(ob):
    copies = [(lambda ins, outs, pos: ins[0], lambda ins, outs, pos: outs[0].at[_chip(pos)],
               functools.partial(_other_chip, mask=mask)) for mask in CHIP_MASKS]
    return [_sds((4,) + ob.shape, ob.dtype)], copies


def _to_sibling_half(name, arrays):
    def src(ins, outs, pos, a):
        h = arrays[a].shape[-2] // 2
        sl = pl.ds((1 - pos[2]) * h, h)
        return ins[a].at[:, sl] if arrays[a].ndim == 3 else ins[a].at[sl]

    outs = [_sds(a.shape[:-2] + (a.shape[-2] // 2, a.shape[-1]), a.dtype) for a in arrays]
    phase = [(functools.partial(src, a=a), lambda ins, outs, pos, a=a: outs[a], _sibling) for a in range(len(arrays))]
    return _exchange(name, arrays, outs, [phase])


def _add_half(name, full, part, cidx):
    shape = part.shape
    lead = shape[0] if len(shape) == 3 else 1
    rows, cols = shape[-2], shape[-1]
    tr = rows
    nr = rows // tr
    f3 = full.reshape((lead,) + full.shape[-2:])
    p3 = part.reshape((lead, rows, cols))

    def body(c_ref, f_ref, p_ref, o_ref):
        o_ref[...] = (f_ref[...].astype(F32) + p_ref[...].astype(F32)).astype(o_ref.dtype)

    out = _call(
        body, name=name,
        grid_spec=pltpu.PrefetchScalarGridSpec(
            num_scalar_prefetch=1, grid=(lead, nr),
            in_specs=[pl.BlockSpec((1, tr, cols), lambda b, r, c_ref: (b, c_ref[0] * nr + r, 0)),
                      pl.BlockSpec((1, tr, cols), lambda b, r, c_ref: (b, r, 0))],
            out_specs=pl.BlockSpec((1, tr, cols), lambda b, r, c_ref: (b, r, 0))),
        out_shape=_sds((lead, rows, cols), part.dtype),
        compiler_params=_params("arbitrary", "arbitrary"),
    )(cidx, f3, p3)
    return out.reshape(shape)


def _to_other_chips(arrays, blocked):
    def src(ins, outs, pos, a, mask):
        return ins[a].at[_chip(pos) ^ mask] if blocked[a] else ins[a]

    outs = [_sds((3,) + (a.shape[1:] if b else a.shape), a.dtype) for a, b in zip(arrays, blocked)]
    copies = []
    for mi, mask in enumerate(CHIP_MASKS):
        for a in range(len(arrays)):
            copies.append((functools.partial(src, a=a, mask=mask), lambda ins, outs, pos, a=a, mi=mi: outs[a].at[mi],
                           functools.partial(_other_chip, mask=mask)))
    return outs, copies


def _add_chips(name, own, got, jidx, blocked):
    rows, cols = got.shape[-2:]
    tr = rows
    nr = rows // tr
    o3 = own if blocked else own.reshape((1, rows, cols))

    def body(j_ref, o_ref, g_ref, out_ref):
        out_ref[...] = ((o_ref[0].astype(F32) + g_ref[0].astype(F32))
                        + (g_ref[1].astype(F32) + g_ref[2].astype(F32)))

    own_map = (lambda r, j_ref: (j_ref[0], r, 0)) if blocked else (lambda r, j_ref: (0, r, 0))
    return _call(
        body, name=name,
        grid_spec=pltpu.PrefetchScalarGridSpec(
            num_scalar_prefetch=1, grid=(nr,),
            in_specs=[pl.BlockSpec((1, tr, cols), own_map),
                      pl.BlockSpec((3, tr, cols), lambda r, j_ref: (0, r, 0))],
            out_specs=pl.BlockSpec((tr, cols), lambda r, j_ref: (r, 0))),
        out_shape=_sds((rows, cols)),
        compiler_params=_params("arbitrary"),
    )(jidx, o3, got)


def _to_sibling(name, arrays):
    phase = [(lambda ins, outs, pos, a=a: ins[a], lambda ins, outs, pos, a=a: outs[a], _sibling)
             for a in range(len(arrays))]
    return _exchange(name, arrays, [_sds(a.shape, a.dtype) for a in arrays], [phase])


def _local_step(x, target, w_pad, w_out, conv_w, norm_w, pool_w, pool_scale, a_log, dt_bias, dn_norm_w, final_norm_w):
    proj, n_t, qn, kn, vs, beta, g, y_pool = _front_fwd(x, norm_w, w_pad, conv_w, a_log, dt_bias, pool_w, pool_scale)
    w, att, qd, kd, tm, cd, o, vn, st = _delta_fwd(qn, kn, vs, beta, g)
    w_out = w_out(o) if callable(w_out) else w_out
    g_wout, dh, dyp, do, ddz, loss, g_fnw, g_dnw = _out_fwd_bwd(x, y_pool, o, proj, target, w_out, dn_norm_w, final_norm_w)
    dqn, dkn, dvs, dbeta, dg = _delta_bwd(do, vn, qd, kd, w, att, cd, st, qn, kn, vs, beta, g, tm)
    (dcq, dck, dcv, dba, g_cw, g_sm), (dpu, dpz, g_pw, g_ps) = _conv_pool_bwd(
        proj, conv_w, a_log, dt_bias, dqn, dkn, dvs, dbeta, dg, dyp, pool_w, pool_scale)
    pieces = [dpu, dpz, dcq, dck, dcv, ddz, dba]
    g_win = _grad_w_in(n_t, pieces)
    small = dict(norm_w=jnp.zeros_like(norm_w), pool_w=g_pw, pool_scale=g_ps, conv_w=g_cw[:CONV_K],
                 a_log=g_sm[0:1, 0:N_HEADS], dt_bias=g_sm[0:1, N_HEADS:2 * N_HEADS], dn_norm_w=g_dnw, final_norm_w=g_fnw)
    return loss[0, 0], g_win, g_wout, small, dh, pieces


SMALL_LAYOUT = (("pool_w", 512, HEAD, (1, N_HEADS, HEAD, HEAD)), ("final_norm_w", 8, HEAD, (D_MODEL,)),
                ("pool_scale", 4, HEAD, (1, D_HALF)), ("conv_w", 48, HEAD, (1, CONV_K, 3 * D_HALF)),
                ("dn_norm_w", 1, HEAD, (1, HEAD)), ("a_log", 1, N_HEADS, (1, N_HEADS)), ("dt_bias", 1, N_HEADS, (1, N_HEADS)),
                ("loss", 1, 1, ()))


def _small_offsets():
    offs, r = {}, 0
    for name, rows, _, _ in SMALL_LAYOUT:
        offs[name] = r
        r += -(-rows // 8) * 8
    assert r <= SMALL_ROWS
    return offs


def _pack_small(t):
    parts = []
    for name, rows, lanes, _ in SMALL_LAYOUT:
        a = t.get(name, jnp.zeros((1,), F32)).reshape(rows, lanes)
        parts.append(jnp.pad(a, ((0, -(-rows // 8) * 8 - rows), (0, HEAD - lanes))))
    buf = jnp.concatenate(parts, axis=0)
    return jnp.pad(buf, ((0, SMALL_ROWS - buf.shape[0]), (0, 0)))


def _adamw_small(w, g_own, g_got, cidx, m, v):
    offs = _small_offsets()
    names = [e[0] for e in SMALL_LAYOUT]
    n = len(names)

    def body(c_ref, w_ref, go_ref, gg_ref, m_ref, v_ref, *outs):
        own_low = c_ref[0] == 0
        gv = jnp.concatenate([jnp.where(own_low, go_ref[...], gg_ref[...]), jnp.where(own_low, gg_ref[...], go_ref[...])], axis=0)
        mn = ADAM_B1 * m_ref[...] + (1.0 - ADAM_B1) * gv
        vn = ADAM_B2 * v_ref[...] + (1.0 - ADAM_B2) * (gv * gv)
        m_hat = mn / (1.0 - ADAM_B1 ** ADAM_STEP)
        v_hat = vn / (1.0 - ADAM_B2 ** ADAM_STEP)
        dl = -ADAM_LR * (m_hat / (jnp.sqrt(v_hat) + ADAM_EPS) + ADAM_WD * w_ref[...])
        for kind, arr in enumerate((gv, dl, mn, vn)):
            for i, (name, rows, lanes, _) in enumerate(SMALL_LAYOUT):
                outs[kind * n + i][...] = arr[offs[name]:offs[name] + rows, :lanes]

    whole = lambda shape: pl.BlockSpec(shape, lambda i, c_ref: (0,) * len(shape))
    out_shapes = [_sds((rows, lanes)) for _, rows, lanes, _ in SMALL_LAYOUT] * 4
    res = _call(
        body, name="adamw_small",
        grid_spec=pltpu.PrefetchScalarGridSpec(
            num_scalar_prefetch=1, grid=(1,),
            in_specs=[whole(w.shape), whole(g_own.shape), whole(g_got.shape), whole(m.shape), whole(v.shape)],
            out_specs=[whole(o.shape) for o in out_shapes]),
        out_shape=out_shapes,
        compiler_params=_params("arbitrary"),
    )(cidx, w, g_own, g_got, m, v)
    return [{name: res[kind * n + i].reshape(shape) for i, (name, _, _, shape) in enumerate(SMALL_LAYOUT)}
            for kind in range(4)]


def kernel(x, norm_w, w_in, pool_w, pool_scale, conv_w, a_log, dt_bias, dn_norm_w, w_out, final_norm_w, loss_target, m_norm_w, m_w_in, m_pool_w, m_pool_scale, m_conv_w, m_a_log, m_dt_bias, m_dn_norm_w, m_w_out, m_final_norm_w, v_norm_w, v_w_in, v_pool_w, v_pool_scale, v_conv_w, v_a_log, v_dt_bias, v_dn_norm_w, v_w_out, v_final_norm_w):
    cidx = lax.axis_index("c").astype(I32).reshape(1)
    jidx = (2 * lax.axis_index("x") + lax.axis_index("y")).astype(I32)

    wb = jnp.pad(w_in[0].astype(BF16), ((0, 0), (0, BLK_IN_PAD - BLK_IN)))
    ob = w_out[0].astype(BF16)
    gw, gc = _gather_weights(wb, conv_w[0])
    mine = lambda j: jidx == j
    w_pad = _assemble_w_in(gw, wb, jidx.reshape(1))
    cw_full = jnp.concatenate([jnp.where(mine(j), conv_w[0], gc[j]) for j in range(4)], axis=1)

    lands_o, copies_o = _gather_blocks(ob)
    sems_o, ob_thru, zones_o, token_o = _exchange_start("gather_w_out_start", [ob], lands_o, copies_o)

    def w_out_full(after):
        (own,), (got,) = _exchange_wait("gather_w_out_wait", sems_o, ob_thru, zones_o, copies_o, after)
        return jnp.where((jnp.arange(4) == jidx)[:, None, None], own[None], got).reshape(D_MODEL, D_MODEL)

    loss, g_win, g_wout, small, dh, pieces = _local_step(
        x[0], loss_target[0], w_pad, w_out_full, cw_full, norm_w + token_o[0, 0], pool_w[0], pool_scale, a_log, dt_bias,
        dn_norm_w, final_norm_w.reshape(1, D_MODEL))
    small["loss"] = loss

    blocks_out = g_wout.reshape(4, BLK_OUT, D_MODEL)
    full = [g_win, blocks_out, _pack_small(small)]
    from_sib = _to_sibling_half("reduce_sibling", full)
    chip_sum = [_add_half("add_sibling_%d" % i, f, p, cidx) for i, (f, p) in enumerate(zip(full, from_sib))]
    blocked = [True, True, False]
    lands, copies = _to_other_chips(chip_sum, blocked)
    sems, chip_sum, zones, token = _exchange_start("reduce_chips_start", chip_sum, lands, copies)
    gx, g_nw = _in_bwd(x[0], dh, norm_w + token[0, 0], w_pad, pieces)
    g_nw = _allreduce_tile("reduce_norm_w", g_nw.reshape(8, HEAD)).reshape(1, D_MODEL)
    chip_sum, from_chips = _exchange_wait("reduce_chips_wait", sems, chip_sum, zones, copies, gx)
    halves = [_add_chips("add_chips_%d" % i, o, g, jidx.reshape(1), b)
              for i, (o, g, b) in enumerate(zip(chip_sum, from_chips, blocked))]
    other_halves = _to_sibling("swap_halves", halves)

    weights = dict(norm_w=norm_w, w_in=w_in, pool_w=pool_w, pool_scale=pool_scale, conv_w=conv_w, a_log=a_log,
                   dt_bias=dt_bias, dn_norm_w=dn_norm_w, w_out=w_out, final_norm_w=final_norm_w)
    ms = dict(norm_w=m_norm_w, w_in=m_w_in, pool_w=m_pool_w, pool_scale=m_pool_scale, conv_w=m_conv_w, a_log=m_a_log,
              dt_bias=m_dt_bias, dn_norm_w=m_dn_norm_w, w_out=m_w_out, final_norm_w=m_final_norm_w)
    vs = dict(norm_w=v_norm_w, w_in=v_w_in, pool_w=v_pool_w, pool_scale=v_pool_scale, conv_w=v_conv_w, a_log=v_a_log,
              dt_bias=v_dt_bias, dn_norm_w=v_dn_norm_w, w_out=v_w_out, final_norm_w=v_final_norm_w)
    names = ["norm_w", "w_in", "pool_w", "pool_scale", "conv_w", "a_log", "dt_bias", "dn_norm_w", "w_out", "final_norm_w"]
    small_names = [n for n in names if n not in ("w_in", "w_out")]

    def pack(t):
        conv = lax.dynamic_update_slice_in_dim(jnp.zeros((CONV_K, 3 * D_HALF), F32), t["conv_w"][0], jidx * BLK_CONV, axis=1)
        return _pack_small({**{n: t[n] for n in small_names if n != "conv_w"}, "conv_w": conv})

    results = [{}, {}, {}, {}]
    to_tiles = lambda a: jnp.transpose(a, (2, 0, 1)).reshape(BLK_IN, 8, HEAD)
    from_tiles = lambda a: jnp.transpose(a, (1, 2, 0)).reshape(1, D_MODEL, BLK_IN)
    lo = jnp.where(cidx[0] == 0, halves[0], other_halves[0])
    hi = jnp.where(cidx[0] == 0, other_halves[0], halves[0])
    g_tiles = jnp.concatenate([lo[:, :BLK_IN].T, hi[:, :BLK_IN].T], axis=1).reshape(BLK_IN, 8, HEAD)
    outs = _adamw_tiles("adamw_w_in", to_tiles(w_in), g_tiles, to_tiles(m_w_in), to_tiles(v_w_in))
    for res, o in zip(results, (g_tiles,) + tuple(outs)):
        res["w_in"] = from_tiles(o)
    outs = _adamw_shard("adamw_w_out", w_out, halves[1], other_halves[1], cidx, m_w_out, v_w_out)
    for res, o in zip(results, outs):
        res["w_out"] = o
    outs = _adamw_small(pack(weights), halves[2], other_halves[2], cidx, pack(ms), pack(vs))
    for res, got in zip(results, outs):
        got["conv_w"] = lax.dynamic_slice_in_dim(got["conv_w"], jidx * BLK_CONV, BLK_CONV, axis=2)
        res.update(got)
    one_tile = lambda a: a.reshape(1, 8, HEAD)
    outs = _adamw_tiles("adamw_norm_w", one_tile(norm_w), one_tile(g_nw), one_tile(m_norm_w), one_tile(v_norm_w))
    for res, o in zip(results, (g_nw,) + tuple(outs)):
        res["norm_w"] = o.reshape(1, D_MODEL)
    grads, delta, new_m, new_v = results

    return (grads["loss"], gx[None], *[grads[n] for n in names], *[delta[n] for n in names],
            *[new_m[n] for n in names], *[new_v[n] for n in names])
```

```python
import functools

import jax
import jax.numpy as jnp
import numpy as np
from jax import lax
from jax.experimental import pallas as pl
from jax.experimental.pallas import tpu as pltpu

F32 = jnp.float32
BF16 = jnp.bfloat16
I32 = jnp.int32

D_MODEL = 1024
D_HALF = 512
N_HEADS = 4
HEAD = 128
CHUNK = 64
PAIR = 2 * CHUNK
WINDOWS = (2, 4, 8, 16)
CONV_K = 4
EPS = 1e-6
N_IN = 3080
N_IN_PAD = 3200
BLK_IN = 770
BLK_IN_PAD = 896
BLK_OUT = 256
BLK_CONV = 384
COL_BA = 3072
QK_SCALE = HEAD ** -0.5
SMALL_ROWS = 608
VMEM_LIMIT = 56 * 1024 * 1024

ADAM_LR = 0.001
ADAM_B1 = 0.9
ADAM_B2 = 0.999
ADAM_EPS = 1e-08
ADAM_WD = 0.01
ADAM_STEP = 10

CHIP_MASKS = (2, 1, 3)
HEADS = range(N_HEADS)
HEAD_COLS = [slice(h * HEAD, (h + 1) * HEAD) for h in HEADS]


def _call(body, **kw):
    return pl.pallas_call(body, **kw)


def _params(*sem):
    return pltpu.CompilerParams(dimension_semantics=sem, vmem_limit_bytes=VMEM_LIMIT)


def _sds(shape, dtype=F32):
    return jax.ShapeDtypeStruct(shape, dtype)


def _bdot(a, b):
    return jnp.dot(a.astype(BF16), b.astype(BF16), preferred_element_type=F32)


def _bdot_nt(a, b):
    return lax.dot_general(a.astype(BF16), b.astype(BF16), (((1,), (1,)), ((), ())), preferred_element_type=F32)


def _bdot_tn(a, b):
    return lax.dot_general(a.astype(BF16), b.astype(BF16), (((0,), (0,)), ((), ())), preferred_element_type=F32)


def _split(a):
    hi = a.astype(BF16)
    lo = (a - hi.astype(F32)).astype(BF16)
    return hi, lo


def _mask_dot(m, b, dims=(((1,), (0,)), ((), ()))):
    bh, bl = _split(b)
    dg = functools.partial(lax.dot_general, dimension_numbers=dims, preferred_element_type=F32)
    return dg(m, bh) + dg(m, bl)


def _sigmoid(x):
    return 0.5 * jnp.tanh(0.5 * x) + 0.5


def _softplus(x):
    return jnp.maximum(x, 0.0) + jnp.log(1.0 + jnp.exp(-jnp.abs(x)))


def _rowsum(x):
    return jnp.sum(x, axis=-1, keepdims=True)


def _colsum(x):
    return jnp.sum(x, axis=0, keepdims=True)


def _shift_down(xv, prev8, k):
    r = pltpu.roll(xv, k, 0)
    q = pltpu.roll(prev8, k, 0)
    row = lax.broadcasted_iota(I32, prev8.shape, 0)
    top = jnp.where(row < k, q, r[0:8])
    return jnp.concatenate([top, r[8:]], axis=0)


def _shift_up(xv, next8, k):
    t = xv.shape[0]
    r = pltpu.roll(xv, t - k, 0)
    q = pltpu.roll(next8, 8 - k, 0)
    row = lax.broadcasted_iota(I32, next8.shape, 0)
    bot = jnp.where(row >= 8 - k, q, r[t - 8:])
    return jnp.concatenate([r[:t - 8], bot], axis=0)


def _band(rows, cols, off, w, anti=False):
    r = lax.broadcasted_iota(I32, (rows, cols), 0)
    c = lax.broadcasted_iota(I32, (rows, cols), 1)
    d = (c - r + off) if anti else (r - c + off)
    return ((d >= 0) & (d < w)).astype(BF16)


def _head(ref_or_val, h):
    return ref_or_val[:, h * HEAD:(h + 1) * HEAD]


INTRA_PAIRS = 2
UNITS = [(pp, h) for pp in range(INTRA_PAIRS) for h in HEADS]


def _heads(ref, rows=PAIR):
    return [ref[pp * rows:(pp + 1) * rows, HEAD_COLS[h]] for pp, h in UNITS]


def _put_heads(ref, vals, rows=PAIR):
    for (pp, h), v in zip(UNITS, vals):
        ref[pp * rows:(pp + 1) * rows, HEAD_COLS[h]] = v.astype(ref.dtype)


def _each(fn, *lists):
    return [fn(*args) for args in zip(*lists)]


def _proj_fwd(x, norm_w, w_pad):
    s = x.shape[0]
    tm = 512

    def body(x_ref, nw_ref, w_ref, proj_ref, nt_ref):
        xv = x_ref[...]
        r = lax.rsqrt(jnp.mean(xv * xv, axis=-1, keepdims=True) + EPS)
        nv = xv * r * nw_ref[...]
        nt_ref[...] = nv.T.astype(BF16)
        proj_ref[...] = jnp.dot(nv.astype(BF16), w_ref[...], preferred_element_type=F32)

    return _call(
        body, name="proj_fwd", grid=(s // tm,),
        in_specs=[pl.BlockSpec((tm, D_MODEL), lambda i: (i, 0)),
                  pl.BlockSpec((1, D_MODEL), lambda i: (0, 0)),
                  pl.BlockSpec((D_MODEL, N_IN_PAD), lambda i: (0, 0))],
        out_specs=[pl.BlockSpec((tm, N_IN_PAD), lambda i: (i, 0)),
                   pl.BlockSpec((D_MODEL, tm), lambda i: (0, i))],
        out_shape=[_sds((s, N_IN_PAD)), _sds((D_MODEL, s), BF16)],
        compiler_params=_params("arbitrary"),
    )(x, norm_w, w_pad)


def _pool_bands(t, anti=False):
    r = np.arange(t)[:, None]
    c = np.arange(t + HEAD)[None, :]
    d = (c - r) if anti else (r - c + HEAD)
    return jnp.asarray(np.stack([(d >= 0) & (d < w) for w in WINDOWS]), BF16)


def _pool_mix(u, halo, z, pw, bands, row0):
    t = u[0].shape[0]
    rows = row0 + lax.broadcasted_iota(I32, (t, 1), 0) + 1
    cnt = [jnp.minimum(rows, w).astype(F32) for w in WINDOWS]
    win = _each(lambda b, h, v: _mask_dot(b, jnp.concatenate([h, v], axis=0)), bands, halo, u)
    mix = _each(lambda a, c, v: a / c - v, win, cnt, u)
    mixed = _each(_bdot, mix, pw)
    return mix, mixed, _each(_sigmoid, z), cnt


POOL_T = 256


def _pool_fwd(proj, pool_w, pool_scale):
    s = proj.shape[0]
    t = POOL_T
    hb = t // HEAD

    def body(u_ref, z_ref, halo_ref, pw_ref, ps_ref, band_ref, y_ref):
        i = pl.program_id(0)
        live = (i > 0).astype(F32)
        groups = lambda ref: [ref[:, sl] for sl in HEAD_COLS]
        z = groups(z_ref)
        _, mixed, sg, _ = _pool_mix(groups(u_ref), [h * live for h in groups(halo_ref)], z,
                                    [pw_ref[g] for g in HEADS], [band_ref[g] for g in HEADS], i * t)
        for sl, m, zg, s_ in zip(HEAD_COLS, mixed, z, sg):
            y_ref[:, sl] = m * ps_ref[:, sl] * (zg * s_)

    return _call(
        body, name="pool_fwd", grid=(s // t,),
        in_specs=[pl.BlockSpec((t, D_HALF), lambda i: (i, 0)),
                  pl.BlockSpec((t, D_HALF), lambda i: (i, 1)),
                  pl.BlockSpec((HEAD, D_HALF), lambda i: (jnp.maximum(i * hb - 1, 0), 0)),
                  pl.BlockSpec((N_HEADS, HEAD, HEAD), lambda i: (0, 0, 0)),
                  pl.BlockSpec((1, D_HALF), lambda i: (0, 0)),
                  pl.BlockSpec((N_HEADS, t, HEAD + t), lambda i: (0, 0, 0))],
        out_specs=pl.BlockSpec((t, D_HALF), lambda i: (i, 0)),
        out_shape=_sds((s, D_HALF)),
        compiler_params=_params("arbitrary"),
    )(proj, proj, proj, pool_w, pool_scale, _pool_bands(t))


def _conv_taps(xv, prev8):
    return [_shift_down(xv, prev8, CONV_K - 1 - j) for j in range(CONV_K - 1)] + [xv]


def _conv_pre(taps, cw):
    y = taps[CONV_K - 1] * cw[CONV_K - 1:CONV_K]
    for j in range(CONV_K - 2, -1, -1):
        y = y + taps[j] * cw[j:j + 1]
    return y


CONV_T = 256
CONV_SUB = 256


def _conv_specs(t, tile_of=lambda i: i):
    tiles = [pl.BlockSpec((t, D_HALF), functools.partial(lambda i, p: (tile_of(i), 2 + p), p=p)) for p in range(3)]
    halos = [pl.BlockSpec((8, D_HALF),
                          functools.partial(lambda i, p: (jnp.maximum(tile_of(i) * (t // 8) - 1, 0), 2 + p), p=p))
             for p in range(3)]
    return tiles + halos


def _conv_fwd(proj, conv_w, a_log, dt_bias):
    s = proj.shape[0]
    t = CONV_T

    def body(q_ref, k_ref, v_ref, hq_ref, hk_ref, hv_ref, ba_ref, cw_ref, al_ref, dtb_ref,
             qn_ref, kn_ref, vs_ref, beta_ref, g_ref):
        live = (pl.program_id(0) > 0).astype(F32)
        parts = ((q_ref, hq_ref, qn_ref), (k_ref, hk_ref, kn_ref), (v_ref, hv_ref, vs_ref))

        def sub_tile(r0, first):
            rows = pl.ds(r0, CONV_SUB)
            for p, (x_ref, h_ref, o_ref) in enumerate(parts):
                for h in HEADS:
                    cs = HEAD_COLS[h]
                    prev8 = h_ref[:, cs] * live if first else x_ref[pl.ds(r0 - 8, 8), cs]
                    y = _conv_pre(_conv_taps(x_ref[rows, cs], prev8), cw_ref[:, p * D_HALF + h * HEAD:p * D_HALF + (h + 1) * HEAD])
                    sv = y * _sigmoid(y)
                    o_ref[rows, cs] = sv if p == 2 else sv * lax.rsqrt(_rowsum(sv * sv) + EPS)
            ba = ba_ref[rows, :]
            for h in HEADS:
                beta = _sigmoid(ba[:, h:h + 1])
                gl = -jnp.exp(al_ref[0:1, h:h + 1]) * _softplus(ba[:, N_HEADS + h:N_HEADS + h + 1] + dtb_ref[0:1, h:h + 1])
                beta_ref[rows, HEAD_COLS[h]] = jnp.broadcast_to(beta, (CONV_SUB, HEAD))
                g_ref[rows, HEAD_COLS[h]] = jnp.broadcast_to(gl, (CONV_SUB, HEAD))

        sub_tile(0, True)

        def step(k, carry):
            sub_tile(pl.multiple_of(k * CONV_SUB, CONV_SUB), False)
            return carry

        lax.fori_loop(1, t // CONV_SUB, step, 0)

    row = pl.BlockSpec((t, D_HALF), lambda i: (i, 0))
    return _call(
        body, name="conv_fwd", grid=(s // t,),
        in_specs=_conv_specs(t) + [pl.BlockSpec((t, HEAD), lambda i: (i, COL_BA // HEAD)),
                                   pl.BlockSpec((CONV_K, 3 * D_HALF), lambda i: (0, 0)),
                                   pl.BlockSpec((1, N_HEADS), lambda i: (0, 0)),
                                   pl.BlockSpec((1, N_HEADS), lambda i: (0, 0))],
        out_specs=[row] * 5,
        out_shape=[_sds((s, D_HALF))] * 5,
        compiler_params=_params("arbitrary"),
    )(proj, proj, proj, proj, proj, proj, proj, conv_w, a_log, dt_bias)


def _pair_masks():
    r = lax.broadcasted_iota(I32, (PAIR, PAIR), 0)
    c = lax.broadcasted_iota(I32, (PAIR, PAIR), 1)
    same = jnp.right_shift(r, 6) == jnp.right_shift(c, 6)
    return same, same & (r >= c), same & (r > c), r == c


def _run(stages):
    for _ in stages:
        pass


def _interleave(*stage_lists):
    live = list(stage_lists)
    while live:
        for gen in list(live):
            try:
                next(gen)
            except StopIteration:
                live.remove(gen)


def _pair_common_stages(cm, qn, kn, vs, beta, g):
    same, incl, strict, eye = _pair_masks()
    incl_b = incl.astype(BF16)
    first = lax.broadcasted_iota(I32, (PAIR, HEAD), 0) < CHUNK
    cm.update(same=same, incl=incl, strict=strict, eye=eye)
    gc = _each(lambda gv: _mask_dot(incl_b, gv), g)
    q = _each(lambda v: v * QK_SCALE, qn)
    kb = _each(lambda k, b: k * b, kn, beta)
    cm.update(gc=gc, q=q, kb=kb, vb=_each(lambda v, b: v * b, vs, beta))
    yield
    cm.update(kk=_each(_bdot_nt, kb, kn), qk=_each(_bdot_nt, q, kn))
    gc_row = _each(lambda v: _colsum(jnp.where(eye, v, 0.0)), gc)
    gl = _each(lambda v: jnp.where(first, v[CHUNK - 1:CHUNK], v[PAIR - 1:PAIR]), gc)
    egc = _each(jnp.exp, gc)
    cm.update(gl=gl, egc=egc,
              decay=_each(lambda v, r: jnp.where(incl, jnp.exp(jnp.where(incl, v - r, 0.0)), 0.0), gc, gc_row))
    yield
    cm.update(ekd=_each(lambda a, b: jnp.exp(a - b), gl, gc), cd=_each(jnp.exp, gl),
              kbg=_each(lambda k, e: k * e, kb, egc))
    yield


def _pair_common(qn, kn, vs, beta, g):
    cm = {}
    _run(_pair_common_stages(cm, qn, kn, vs, beta, g))
    return cm


def _tri_inv_stages(out, a, eye_f):
    p = _each(lambda v: eye_f - v, a)
    x = _each(_bdot, a, a)
    yield
    for it in range(5):
        p = _each(lambda pv, xv: pv + _bdot(pv, xv), p, x)
        if it < 4:
            x = _each(_bdot, x, x)
        yield
    out["t"] = p


def _tri_inv(a, eye_f):
    out = {}
    _run(_tri_inv_stages(out, a, eye_f))
    return out["t"]


def _pair_spec():
    return pl.BlockSpec((INTRA_PAIRS * PAIR, D_HALF), lambda i: (i, 0))


def _chunk_scalar_spec(pairs=1, index=lambda i: (i, 0)):
    return pl.BlockSpec((16 * pairs, D_HALF), index)


SCAN_PAIRS = 2
SCAN_ROWS = SCAN_PAIRS * PAIR


def _intra_fwd(qn, kn, vs, beta, g):
    s = qn.shape[0]

    def body(qn_ref, kn_ref, vs_ref, beta_ref, g_ref, u_ref, w_ref, att_ref, qd_ref, kd_ref, t_ref, cd_ref):
        kn = _heads(kn_ref)
        cm = _pair_common(_heads(qn_ref), kn, _heads(vs_ref), _heads(beta_ref), _heads(g_ref))
        a = _each(lambda kk, d: jnp.where(cm["strict"], kk * d, 0.0), cm["kk"], cm["decay"])
        tm = _tri_inv(a, cm["eye"].astype(F32))
        _put_heads(t_ref, tm)
        _put_heads(u_ref, _each(_bdot, tm, cm["vb"]))
        _put_heads(w_ref, _each(_bdot, tm, cm["kbg"]))
        _put_heads(att_ref, _each(lambda a, b: a * b, cm["qk"], cm["decay"]))
        _put_heads(qd_ref, _each(lambda a, b: a * b, cm["q"], cm["egc"]))
        _put_heads(kd_ref, _each(lambda a, b: a * b, kn, cm["ekd"]))
        for ci in range(2):
            for (pp, h), v in zip(UNITS, cm["cd"]):
                cd_ref[pp * 16 + ci * 8:pp * 16 + (ci + 1) * 8, HEAD_COLS[h]] = v[ci * CHUNK:ci * CHUNK + 8]

    return _call(
        body, name="intra_fwd", grid=(s // (INTRA_PAIRS * PAIR),),
        in_specs=[_pair_spec()] * 5, out_specs=[_pair_spec()] * 6 + [_chunk_scalar_spec(INTRA_PAIRS)],
        out_shape=[_sds((s, D_HALF))] + [_sds((s, D_HALF), BF16)] * 5 + [_sds((s // 8, D_HALF))],
        compiler_params=_params("arbitrary"),
    )(qn, kn, vs, beta, g)


def _scan_fwd(u, w, att, qd, kd, cd):
    s = u.shape[0]
    n_chunks = s // CHUNK

    def body(u_ref, w_ref, att_ref, qd_ref, kd_ref, cd_ref, o_ref, vn_ref, st_ref, state):
        @pl.when(pl.program_id(0) == 0)
        def _():
            state[...] = jnp.zeros_like(state)
        cols = list(enumerate(HEAD_COLS))
        sm = [state[h] for h in HEADS]
        for ci in range(2 * SCAN_PAIRS):
            rs = slice(ci * CHUNK, (ci + 1) * CHUNK)
            for h in HEADS:
                st_ref[ci, h] = sm[h]
            both = [_bdot(jnp.concatenate([w_ref[rs, sl], qd_ref[rs, sl]], axis=0), sm[h]) for h, sl in cols]
            vn = [u_ref[rs, sl] - both[h][:CHUNK] for h, sl in cols]
            for h, sl in cols:
                vn_ref[rs, sl] = vn[h].astype(BF16)
                o_ref[rs, sl] = both[h][CHUNK:]
            sm = [sm[h] * cd_ref[ci * 8:ci * 8 + 1, sl] + _bdot_tn(kd_ref[rs, sl], vn[h]) for h, sl in cols]
        for h in HEADS:
            state[h] = sm[h]
        for pp in range(SCAN_PAIRS):
            rp = slice(pp * PAIR, (pp + 1) * PAIR)
            intra = [_bdot(att_ref[rp, sl], vn_ref[rp, sl]) for sl in HEAD_COLS]
            for h, sl in cols:
                o_ref[rp, sl] += intra[h]

    rows = pl.BlockSpec((SCAN_ROWS, D_HALF), lambda i: (i, 0))
    return _call(
        body, name="scan_fwd", grid=(s // SCAN_ROWS,),
        in_specs=[rows] * 5 + [_chunk_scalar_spec(SCAN_PAIRS)],
        out_specs=[rows, rows, pl.BlockSpec((2 * SCAN_PAIRS, N_HEADS, HEAD, HEAD), lambda i: (i, 0, 0, 0))],
        out_shape=[_sds((s, D_HALF)), _sds((s, D_HALF), BF16), _sds((n_chunks, N_HEADS, HEAD, HEAD))],
        scratch_shapes=[pltpu.VMEM((N_HEADS, HEAD, HEAD), F32)],
        compiler_params=_params("arbitrary"),
    )(u, w, att, qd, kd, cd)


def _delta_fwd(qn, kn, vs, beta, g):
    s = qn.shape[0]
    n_steps = s // SCAN_ROWS
    n_chunks = s // CHUNK
    assert INTRA_PAIRS == SCAN_PAIRS

    def body(qn_ref, kn_ref, vs_ref, beta_ref, g_ref, w_ref, att_ref, qd_ref, kd_ref, t_ref, cd_ref, o_ref, vn_ref, st_ref,
             state, u_s, w_s, att_s, qd_s, kd_s, cd_s):
        t = pl.program_id(0)

        @pl.when(t <= 1)
        def _():
            state[...] = jnp.zeros_like(state)

        @pl.when(t == 0)
        def _():
            for ref in (u_s, w_s, att_s, qd_s, kd_s, cd_s):
                ref[1] = jnp.zeros(ref.shape[1:], ref.dtype)

        cur = lax.rem(t, 2)
        prev = 1 - cur
        cols = list(enumerate(HEAD_COLS))

        def recurrence():
            sm = [state[h] for h in HEADS]
            for ci in range(2 * SCAN_PAIRS):
                rs = slice(ci * CHUNK, (ci + 1) * CHUNK)
                for h in HEADS:
                    st_ref[ci, h] = sm[h]
                both = [_bdot(jnp.concatenate([w_s[prev, rs, sl], qd_s[prev, rs, sl]], axis=0), sm[h]) for h, sl in cols]
                vn = [u_s[prev, rs, sl] - both[h][:CHUNK] for h, sl in cols]
                for h, sl in cols:
                    vn_ref[rs, sl] = vn[h].astype(BF16)
                    o_ref[rs, sl] = both[h][CHUNK:]
                yield
                sm = [sm[h] * cd_s[prev, ci * 8:ci * 8 + 1, sl] + _bdot_tn(kd_s[prev, rs, sl], vn[h]) for h, sl in cols]
                yield
            for h in HEADS:
                state[h] = sm[h]
            for pp in range(SCAN_PAIRS):
                rp = slice(pp * PAIR, (pp + 1) * PAIR)
                intra = [_bdot(att_s[prev, rp, sl], vn_ref[rp, sl]) for sl in HEAD_COLS]
                for h, sl in cols:
                    o_ref[rp, sl] += intra[h]
                yield

        def factors():
            kn = _heads(kn_ref)
            cm = {}
            yield from _pair_common_stages(cm, _heads(qn_ref), kn, _heads(vs_ref), _heads(beta_ref), _heads(g_ref))
            a = _each(lambda kk, d: jnp.where(cm["strict"], kk * d, 0.0), cm["kk"], cm["decay"])
            inv = {}
            yield from _tri_inv_stages(inv, a, cm["eye"].astype(F32))
            tm = inv["t"]
            res = dict(u=_each(_bdot, tm, cm["vb"]), w=_each(_bdot, tm, cm["kbg"]),
                       att=_each(lambda a, b: a * b, cm["qk"], cm["decay"]),
                       qd=_each(lambda a, b: a * b, cm["q"], cm["egc"]), kd=_each(lambda a, b: a * b, kn, cm["ekd"]))
            yield
            _put_heads(t_ref, tm)
            for key, out, keep in (("w", w_ref, w_s), ("att", att_ref, att_s), ("qd", qd_ref, qd_s), ("kd", kd_ref, kd_s)):
                _put_heads(out, res[key])
                for (pp, h), v in zip(UNITS, res[key]):
                    keep[cur, pp * PAIR:(pp + 1) * PAIR, HEAD_COLS[h]] = v.astype(BF16)
            for (pp, h), v in zip(UNITS, res["u"]):
                u_s[cur, pp * PAIR:(pp + 1) * PAIR, HEAD_COLS[h]] = v
            for ci in range(2):
                for (pp, h), v in zip(UNITS, cm["cd"]):
                    rows8 = slice(pp * 16 + ci * 8, pp * 16 + (ci + 1) * 8)
                    cd_ref[rows8, HEAD_COLS[h]] = v[ci * CHUNK:ci * CHUNK + 8]
                    cd_s[cur, rows8, HEAD_COLS[h]] = v[ci * CHUNK:ci * CHUNK + 8]
            yield

        _interleave(recurrence(), factors())

    last = n_steps - 1
    now = lambda i: (jnp.minimum(i, last), 0)
    before = lambda i: (jnp.maximum(i - 1, 0), 0)
    rows = lambda index: pl.BlockSpec((SCAN_ROWS, D_HALF), index)
    slot = lambda r, dtype: pltpu.VMEM((2, r, D_HALF), dtype)
    return _call(
        body, name="delta_fwd", grid=(n_steps + 1,),
        in_specs=[rows(now)] * 5,
        out_specs=[rows(now)] * 5 + [_chunk_scalar_spec(SCAN_PAIRS, now), rows(before), rows(before),
                                     pl.BlockSpec((2 * SCAN_PAIRS, N_HEADS, HEAD, HEAD), lambda i: (jnp.maximum(i - 1, 0), 0, 0, 0))],
        out_shape=[_sds((s, D_HALF), BF16)] * 5 + [_sds((s // 8, D_HALF)), _sds((s, D_HALF)), _sds((s, D_HALF), BF16),
                                                  _sds((n_chunks, N_HEADS, HEAD, HEAD))],
        scratch_shapes=[pltpu.VMEM((N_HEADS, HEAD, HEAD), F32), slot(SCAN_ROWS, F32), slot(SCAN_ROWS, BF16),
                        slot(SCAN_ROWS, BF16), slot(SCAN_ROWS, BF16), slot(SCAN_ROWS, BF16), slot(16 * SCAN_PAIRS, F32)],
        compiler_params=_params("arbitrary"),
    )(qn, kn, vs, beta, g)


OUT_T = 512


def _out_fwd_bwd(x, y_pool, o, proj, target, w_out, dn_norm_w, final_norm_w):
    s = x.shape[0]
    t = OUT_T

    def body(x_ref, yp_ref, o_ref, z_ref, tg_ref, wo_ref, dnw_ref, fnw_ref,
             gwo_ref, dh_ref, dyp_ref, do_ref, dz_ref, loss_ref, gfn_ref, gdn_ref, y_ref, yt_ref, gwo_acc):
        @pl.when(pl.program_id(0) == 0)
        def _():
            loss_ref[...] = jnp.zeros_like(loss_ref)
            gfn_ref[...] = jnp.zeros_like(gfn_ref)
            gdn_ref[...] = jnp.zeros_like(gdn_ref)
            gwo_acc[...] = jnp.zeros_like(gwo_acc)

        ypv = yp_ref[...]
        y_ref[:, :D_HALF] = ypv.astype(BF16)
        yt_ref[:D_HALF, :] = ypv.T.astype(BF16)
        dnw = dnw_ref[...]
        keep = []
        for h in HEADS:
            ov = o_ref[:, HEAD_COLS[h]]
            zv = z_ref[:, HEAD_COLS[h]]
            ro = lax.rsqrt(jnp.mean(ov * ov, axis=-1, keepdims=True) + EPS)
            ohat = ov * ro
            sg = _sigmoid(zv)
            keep.append((ro, ohat, zv, sg))
            ydn = ohat * dnw * (zv * sg)
            y_ref[:, D_HALF + h * HEAD:D_HALF + (h + 1) * HEAD] = ydn.astype(BF16)
            yt_ref[D_HALF + h * HEAD:D_HALF + (h + 1) * HEAD, :] = ydn.T.astype(BF16)

        hv = x_ref[...] + jnp.dot(y_ref[...], wo_ref[...], preferred_element_type=F32)
        r2 = lax.rsqrt(jnp.mean(hv * hv, axis=-1, keepdims=True) + EPS)
        hhat = hv * r2
        fnw = fnw_ref[...]
        err = hhat * fnw - tg_ref[...]
        loss_ref[...] += 0.5 * jnp.sum(_rowsum(err * err) * (1.0 / D_MODEL), axis=0, keepdims=True)
        dout = err * (1.0 / D_MODEL)
        gfn_ref[...] += _colsum(dout * hhat)
        dhh = dout * fnw
        dh = r2 * (dhh - hhat * jnp.mean(dhh * hhat, axis=-1, keepdims=True))
        dh_ref[...] = dh
        gwo_acc[...] += _bdot(yt_ref[...], dh)

        @pl.when(pl.program_id(0) == pl.num_programs(0) - 1)
        def _():
            gwo_ref[...] = gwo_acc[...].astype(BF16)

        dy = _bdot_nt(dh, wo_ref[...])
        dyp_ref[...] = dy[:, :D_HALF]
        gdn = jnp.zeros((1, HEAD), F32)
        for h in HEADS:
            ro, ohat, zv, sg = keep[h]
            dyd = dy[:, D_HALF + h * HEAD:D_HALF + (h + 1) * HEAD]
            sz = zv * sg
            dz_ref[:, HEAD_COLS[h]] = (dyd * ohat * dnw * (sg * (1.0 + zv * (1.0 - sg)))).astype(BF16)
            gdn = gdn + _colsum(dyd * ohat * sz)
            doh = dyd * dnw * sz
            do_ref[:, HEAD_COLS[h]] = ro * (doh - ohat * jnp.mean(doh * ohat, axis=-1, keepdims=True))
        gdn_ref[...] += gdn

    wide = pl.BlockSpec((t, D_MODEL), lambda i: (i, 0))
    half = pl.BlockSpec((t, D_HALF), lambda i: (i, 0))
    const = lambda shape: pl.BlockSpec(shape, lambda i: (0,) * len(shape))
    return _call(
        body, name="out_fwd_bwd", grid=(s // t,),
        in_specs=[wide, half, half, pl.BlockSpec((t, D_HALF), lambda i: (i, 5)), wide,
                  const((D_MODEL, D_MODEL)), const((1, HEAD)), const((1, D_MODEL))],
        out_specs=[const((D_MODEL, D_MODEL)), wide, half, half, half,
                   const((1, HEAD)), const((1, D_MODEL)), const((1, HEAD))],
        out_shape=[_sds((D_MODEL, D_MODEL), BF16), _sds((s, D_MODEL)), _sds((s, D_HALF)), _sds((s, D_HALF)),
                   _sds((s, D_HALF), BF16), _sds((1, HEAD)), _sds((1, D_MODEL)), _sds((1, HEAD))],
        scratch_shapes=[pltpu.VMEM((t, D_MODEL), BF16), pltpu.VMEM((D_MODEL, t), BF16), pltpu.VMEM((D_MODEL, D_MODEL), F32)],
        compiler_params=_params("arbitrary"),
    )(x, y_pool, o, proj, target, w_out, dn_norm_w, final_norm_w)


def _token_matmul(name, at, pieces):
    m, s = at.shape
    n = len(pieces)
    tn, tk = D_HALF, 512

    def body(a_ref, *refs):
        p_refs, o_ref, acc = refs[:n], refs[n], refs[n + 1]

        @pl.when(pl.program_id(0) == 0)
        def _():
            acc[...] = jnp.zeros_like(acc)

        av = a_ref[...]
        for p in range(n):
            acc[:, p * tn:(p + 1) * tn] += _bdot(av, p_refs[p][...])

        @pl.when(pl.program_id(0) == pl.num_programs(0) - 1)
        def _():
            o_ref[...] = acc[...].astype(BF16)

    return _call(
        body, name=name, grid=(s // tk,),
        in_specs=[pl.BlockSpec((m, tk), lambda k: (0, k))]
                 + [pl.BlockSpec((tk, tn), functools.partial(lambda k, cb: (k, cb), cb=cb)) for _, cb in pieces],
        out_specs=pl.BlockSpec((m, n * tn), lambda k: (0, 0)),
        out_shape=_sds((m, n * tn), BF16),
        scratch_shapes=[pltpu.VMEM((m, n * tn), F32)],
        compiler_params=_params("arbitrary"),
    )(at, *[p[0] for p in pieces])


def _grad_w_in(at, pieces):
    m, s = at.shape
    n = len(pieces)
    tn, tk = D_HALF, min(s, 1024)

    def body(a_ref, *refs):
        p_refs, o_ref, acc = refs[:n], refs[n], refs[n + 1]

        @pl.when(pl.program_id(0) == 0)
        def _():
            acc[...] = jnp.zeros_like(acc)

        av = a_ref[...]
        for p in range(n):
            acc[:, p * tn:(p + 1) * tn] += _bdot(av, p_refs[p][...])

        @pl.when(pl.program_id(0) == pl.num_programs(0) - 1)
        def _():
            for j in range(4):
                base = j * BLK_IN // HEAD * HEAD
                win = acc[:, base:base + BLK_IN_PAD]
                if j * BLK_IN > base:
                    win = pltpu.roll(win, BLK_IN_PAD - (j * BLK_IN - base), 1)
                o_ref[j] = win.astype(BF16)

    return _call(
        body, name="grad_w_in", grid=(s // tk,),
        in_specs=[pl.BlockSpec((m, tk), lambda k: (0, k))] + [pl.BlockSpec((tk, tn), lambda k: (k, 0))] * n,
        out_specs=pl.BlockSpec((4, m, BLK_IN_PAD), lambda k: (0, 0, 0)),
        out_shape=_sds((4, m, BLK_IN_PAD), BF16),
        scratch_shapes=[pltpu.VMEM((m, n * tn), F32)],
        compiler_params=_params("arbitrary"),
    )(at, *pieces)


def _pool_bwd(proj, dyp, pool_w, pool_scale):
    s = proj.shape[0]
    t = POOL_T
    hb = t // HEAD
    last = s // HEAD - 1

    def body(u_ref, z_ref, halo_ref, dy_ref, zn_ref, dyn_ref, pw_ref, ps_ref, band_ref, aband_ref,
             du_ref, dz_ref, gpw_ref, gps_ref):
        i = pl.program_id(0)

        @pl.when(i == 0)
        def _():
            gpw_ref[...] = jnp.zeros_like(gpw_ref)
            gps_ref[...] = jnp.zeros_like(gps_ref)

        live = (i > 0).astype(F32)
        more = (i < pl.num_programs(0) - 1).astype(F32)
        groups = lambda ref: [ref[:, sl] for sl in HEAD_COLS]
        z, ps, dy = groups(z_ref), groups(ps_ref), groups(dy_ref)
        pw = [pw_ref[g] for g in HEADS]
        mix, mixed, sg, cnt = _pool_mix(groups(u_ref), [h * live for h in groups(halo_ref)], z, pw,
                                        [band_ref[g] for g in HEADS], i * t)
        sz = _each(lambda a, b: a * b, z, sg)
        for sl, d, m, p, s_, zg in zip(HEAD_COLS, dy, mixed, ps, sg, z):
            dz_ref[:, sl] = (d * m * p * (s_ * (1.0 + zg * (1.0 - s_)))).astype(BF16)
        for sl, d, m, a in zip(HEAD_COLS, dy, mixed, sz):
            gps_ref[:, sl] += _colsum(d * m * a)
        dmixed = _each(lambda d, p, a: d * p * a, dy, ps, sz)
        for g, gp in enumerate(_each(_bdot_tn, mix, dmixed)):
            gpw_ref[g] += gp
        dmix = _each(_bdot_nt, dmixed, pw)
        dmix_n = _each(lambda d, p, zn, w_: _bdot_nt(d * more * p * (zn * _sigmoid(zn)), w_),
                       groups(dyn_ref), ps, groups(zn_ref), pw)
        scaled = [jnp.concatenate([a / c, b * (1.0 / w)], axis=0) for a, c, b, w in zip(dmix, cnt, dmix_n, WINDOWS)]
        du = _each(lambda b, s_, d: _mask_dot(b, s_) - d, [aband_ref[g] for g in HEADS], scaled, dmix)
        for sl, v in zip(HEAD_COLS, du):
            du_ref[:, sl] = v.astype(BF16)

    tile = lambda col: pl.BlockSpec((t, D_HALF), lambda i: (i, col))
    below = lambda col: pl.BlockSpec((HEAD, D_HALF), lambda i: (jnp.minimum((i + 1) * hb, last), col))
    return _call(
        body, name="pool_bwd", grid=(s // t,),
        in_specs=[tile(0), tile(1), pl.BlockSpec((HEAD, D_HALF), lambda i: (jnp.maximum(i * hb - 1, 0), 0)),
                  tile(0), below(1), below(0),
                  pl.BlockSpec((N_HEADS, HEAD, HEAD), lambda i: (0, 0, 0)), pl.BlockSpec((1, D_HALF), lambda i: (0, 0)),
                  pl.BlockSpec((N_HEADS, t, HEAD + t), lambda i: (0, 0, 0)),
                  pl.BlockSpec((N_HEADS, t, HEAD + t), lambda i: (0, 0, 0))],
        out_specs=[tile(0), tile(0), pl.BlockSpec((N_HEADS, HEAD, HEAD), lambda i: (0, 0, 0)),
                   pl.BlockSpec((1, D_HALF), lambda i: (0, 0))],
        out_shape=[_sds((s, D_HALF), BF16), _sds((s, D_HALF), BF16), _sds((N_HEADS, HEAD, HEAD)), _sds((1, D_HALF))],
        compiler_params=_params("arbitrary"),
    )(proj, proj, proj, dyp, proj, dyp, pool_w, pool_scale, _pool_bands(t), _pool_bands(t, anti=True))


def _scan_bwd(do, vn, qd, kd, w, att, cd, st):
    s = do.shape[0]
    n_steps = s // SCAN_ROWS

    def body(do_ref, vn_ref, qd_ref, kd_ref, w_ref, att_ref, cd_ref, st_ref,
             du_ref, dw_ref, datt_ref, dqd_ref, dkd_ref, dcd_ref, dstate):
        @pl.when(pl.program_id(0) == 0)
        def _():
            dstate[...] = jnp.zeros_like(dstate)
        _, incl, _, _ = _pair_masks()
        cols = list(enumerate(HEAD_COLS))
        dv_intra = []
        for pp in range(SCAN_PAIRS):
            rp = slice(pp * PAIR, (pp + 1) * PAIR)
            dv_intra.append([_bdot_tn(att_ref[rp, sl], do_ref[rp, sl]) for _, sl in cols])
            for _, sl in cols:
                datt_ref[rp, sl] = jnp.where(incl, _bdot_nt(do_ref[rp, sl], vn_ref[rp, sl]), 0.0)
        ds = [dstate[h] for h in HEADS]
        for ci in range(2 * SCAN_PAIRS - 1, -1, -1):
            rs = slice(ci * CHUNK, (ci + 1) * CHUNK)
            in_pair = slice((ci % 2) * CHUNK, (ci % 2 + 1) * CHUNK)
            sm = [st_ref[ci, h] for h in HEADS]
            dvn = [dv_intra[ci // 2][h][in_pair] + _bdot(kd_ref[rs, sl], ds[h]) for h, sl in cols]
            for h, sl in cols:
                du_ref[rs, sl] = dvn[h].astype(BF16)
            dqd = [_bdot_nt(do_ref[rs, sl], sm[h]) for h, sl in cols]
            dw = [-_bdot_nt(dvn[h], sm[h]) for h, _ in cols]
            dkd = [_bdot_nt(vn_ref[rs, sl], ds[h]) for h, sl in cols]
            dcd = [jnp.broadcast_to(_rowsum(_colsum(ds[h] * sm[h])), (8, HEAD)) for h in HEADS]
            for h, sl in cols:
                dqd_ref[rs, sl] = dqd[h]
                dw_ref[rs, sl] = dw[h].astype(BF16)
                dkd_ref[rs, sl] = dkd[h]
                dcd_ref[ci * 8:(ci + 1) * 8, sl] = dcd[h]
            ds = [ds[h] * cd_ref[ci * 8:ci * 8 + 1, sl] + _bdot_tn(qd_ref[rs, sl], do_ref[rs, sl])
                  - _bdot_tn(w_ref[rs, sl], dvn[h]) for h, sl in cols]
        for h in HEADS:
            dstate[h] = ds[h]

    rev = pl.BlockSpec((SCAN_ROWS, D_HALF), lambda i: (n_steps - 1 - i, 0))
    rev_scalar = _chunk_scalar_spec(SCAN_PAIRS, lambda i: (n_steps - 1 - i, 0))
    return _call(
        body, name="scan_bwd", grid=(n_steps,),
        in_specs=[rev] * 6 + [rev_scalar,
                              pl.BlockSpec((2 * SCAN_PAIRS, N_HEADS, HEAD, HEAD), lambda i: (n_steps - 1 - i, 0, 0, 0))],
        out_specs=[rev] * 5 + [rev_scalar],
        out_shape=[_sds((s, D_HALF), BF16)] * 2 + [_sds((s, D_HALF))] * 3 + [_sds((s // 8, D_HALF))],
        scratch_shapes=[pltpu.VMEM((N_HEADS, HEAD, HEAD), F32)],
        compiler_params=_params("arbitrary"),
    )(do, vn, qd, kd, w, att, cd, st)


def _intra_bwd(qn, kn, vs, beta, g, tm, du, dw, datt, dqd, dkd, dcd):
    s = qn.shape[0]

    def body(qn_ref, kn_ref, vs_ref, beta_ref, g_ref, t_ref, du_ref, dw_ref, datt_ref, dqd_ref, dkd_ref, dcd_ref,
             dqn_ref, dkn_ref, dvs_ref, dbeta_ref, dg_ref):
        ones = jnp.ones((PAIR, HEAD), BF16)
        tn = (((0,), (0,)), ((), ()))
        kn, vs, beta = _heads(kn_ref), _heads(vs_ref), _heads(beta_ref)
        cm = _pair_common(_heads(qn_ref), kn, vs, beta, _heads(g_ref))
        tmv, duv, dwv, dattv, dqdv, dkdv = (_heads(r) for r in (t_ref, du_ref, dw_ref, datt_ref, dqd_ref, dkd_ref))
        dvb = _each(_bdot_tn, tmv, duv)
        dt = _each(lambda a, b, c, d: _bdot_nt(a, b) + _bdot_nt(c, d), duv, cm["vb"], dwv, cm["kbg"])
        dkbg = _each(_bdot_tn, tmv, dwv)
        m1 = _each(_bdot_tn, tmv, dt)
        da = _each(lambda a, b: -jnp.where(cm["strict"], _bdot_nt(a, b), 0.0), m1, tmv)
        dkk = _each(lambda a, b: a * b, da, cm["decay"])
        dqk = _each(lambda a, b: a * b, dattv, cm["decay"])
        dd = _each(lambda a, b, c, d: a * b + c * d, dkk, cm["kk"], dqk, cm["qk"])
        dkb = _each(lambda a, b, c, d: _bdot(a, b) + c * d, dkk, kn, dkbg, cm["egc"])
        dq = _each(lambda a, b, c, d: _bdot(a, b) + c * d, dqk, kn, dqdv, cm["egc"])
        dkn = _each(lambda a, b, c, d: _bdot_tn(a, b) + _bdot_tn(c, d), dkk, cm["kb"], dqk, cm["q"])
        dkn = _each(lambda a, b, c, d, e: a + b * c + d * e, dkn, dkdv, cm["ekd"], dkb, beta)
        t_kd = _each(lambda a, b, c: _rowsum(a * b * c), dkdv, kn, cm["ekd"])
        split = _each(_split, dd)
        rows_dd = [jnp.dot(hi, ones, preferred_element_type=F32) + jnp.dot(lo, ones, preferred_element_type=F32)
                   for hi, lo in split]
        cols_dd = [lax.dot_general(hi, ones, tn, preferred_element_type=F32)
                   + lax.dot_general(lo, ones, tn, preferred_element_type=F32) for hi, lo in split]
        dgc = _each(lambda r, c, a, b, e, f, k, t: r - c + _rowsum(a * b * e) + _rowsum(f * k) - t,
                    rows_dd, cols_dd, dqdv, cm["q"], cm["egc"], dkbg, cm["kbg"], t_kd)
        same_b = cm["same"].astype(BF16)
        rowi = lax.broadcasted_iota(I32, (PAIR, HEAD), 0)
        dcd = _each(lambda d: jnp.where(rowi < CHUNK, d[0:1], d[8:9]), _heads(dcd_ref, rows=16))
        dgl = _each(lambda t, d, c: _mask_dot(same_b, jnp.broadcast_to(t, (PAIR, HEAD))) + d * c, t_kd, dcd, cm["cd"])
        is_last = jnp.bitwise_and(rowi, CHUNK - 1) == CHUNK - 1
        dgc = _each(lambda a, b: a + jnp.where(is_last, b, 0.0), dgc, dgl)
        r = lax.broadcasted_iota(I32, (PAIR, PAIR), 0)
        c = lax.broadcasted_iota(I32, (PAIR, PAIR), 1)
        upper_b = (cm["same"] & (r <= c)).astype(BF16)
        _put_heads(dg_ref, _each(lambda v: _mask_dot(upper_b, v), dgc))
        _put_heads(dbeta_ref, _each(lambda a, b, c, d: jnp.broadcast_to(_rowsum(a * b) + _rowsum(c * d), (PAIR, HEAD)),
                                    dkb, kn, dvb, vs))
        _put_heads(dqn_ref, _each(lambda v: v * QK_SCALE, dq))
        _put_heads(dkn_ref, dkn)
        _put_heads(dvs_ref, _each(lambda a, b: a * b, dvb, beta))

    return _call(
        body, name="intra_bwd", grid=(s // (INTRA_PAIRS * PAIR),),
        in_specs=[_pair_spec()] * 11 + [_chunk_scalar_spec(INTRA_PAIRS)], out_specs=[_pair_spec()] * 5,
        out_shape=[_sds((s, D_HALF))] * 5,
        compiler_params=_params("arbitrary"),
    )(qn, kn, vs, beta, g, tm, du, dw, datt, dqd, dkd, dcd)


def _delta_bwd(do, vn, qd, kd, w, att, cd, st, qn, kn, vs, beta, g, tm):
    s = do.shape[0]
    n_steps = s // SCAN_ROWS
    assert INTRA_PAIRS == SCAN_PAIRS

    def body(do_ref, vn_ref, qd_ref, kd_ref, w_ref, att_ref, cd_ref, st_ref, qn_ref, kn_ref, vs_ref, beta_ref, g_ref, t_ref,
             dqn_ref, dkn_ref, dvs_ref, dbeta_ref, dg_ref, dstate, du_s, dw_s, datt_s, dqd_s, dkd_s, dcd_s):
        t = pl.program_id(0)

        @pl.when(t == 0)
        def _():
            dstate[...] = jnp.zeros_like(dstate)
            for ref in (du_s, dw_s, datt_s, dqd_s, dkd_s, dcd_s):
                ref[1] = jnp.zeros(ref.shape[1:], ref.dtype)

        cur = lax.rem(t, 2)
        prev = 1 - cur
        cols = list(enumerate(HEAD_COLS))
        _, incl, _, _ = _pair_masks()

        def recurrence():
            dv_intra = []
            for pp in range(SCAN_PAIRS):
                rp = slice(pp * PAIR, (pp + 1) * PAIR)
                dv_intra.append([_bdot_tn(att_ref[rp, sl], do_ref[rp, sl]) for _, sl in cols])
                for _, sl in cols:
                    datt_s[cur, rp, sl] = jnp.where(incl, _bdot_nt(do_ref[rp, sl], vn_ref[rp, sl]), 0.0)
                yield
            ds = [dstate[h] for h in HEADS]
            for ci in range(2 * SCAN_PAIRS - 1, -1, -1):
                rs = slice(ci * CHUNK, (ci + 1) * CHUNK)
                in_pair = slice((ci % 2) * CHUNK, (ci % 2 + 1) * CHUNK)
                sm = [st_ref[ci, h] for h in HEADS]
                dvn = [dv_intra[ci // 2][h][in_pair] + _bdot(kd_ref[rs, sl], ds[h]) for h, sl in cols]
                dqd = [_bdot_nt(do_ref[rs, sl], sm[h]) for h, sl in cols]
                dkd = [_bdot_nt(vn_ref[rs, sl], ds[h]) for h, sl in cols]
                dcd = [jnp.broadcast_to(_rowsum(_colsum(ds[h] * sm[h])), (8, HEAD)) for h in HEADS]
                yield
                dw = [-_bdot_nt(dvn[h], sm[h]) for h, _ in cols]
                for h, sl in cols:
                    du_s[cur, rs, sl] = dvn[h].astype(BF16)
                    dqd_s[cur, rs, sl] = dqd[h]
                    dw_s[cur, rs, sl] = dw[h].astype(BF16)
                    dkd_s[cur, rs, sl] = dkd[h]
                    dcd_s[cur, ci * 8:(ci + 1) * 8, sl] = dcd[h]
                ds = [ds[h] * cd_ref[ci * 8:ci * 8 + 1, sl] + _bdot_tn(qd_ref[rs, sl], do_ref[rs, sl])
                      - _bdot_tn(w_ref[rs, sl], dvn[h]) for h, sl in cols]
                yield
            for h in HEADS:
                dstate[h] = ds[h]

        def factors():
            ones = jnp.ones((PAIR, HEAD), BF16)
            tn = (((0,), (0,)), ((), ()))
            kept = lambda ref, rows=PAIR: [ref[prev, pp * rows:(pp + 1) * rows, HEAD_COLS[h]] for pp, h in UNITS]
            kn, vs, beta = _heads(kn_ref), _heads(vs_ref), _heads(beta_ref)
            cm = {}
            yield from _pair_common_stages(cm, _heads(qn_ref), kn, vs, beta, _heads(g_ref))
            tmv = _heads(t_ref)
            duv, dwv, dattv, dqdv, dkdv = kept(du_s), kept(dw_s), kept(datt_s), kept(dqd_s), kept(dkd_s)
            dvb = _each(_bdot_tn, tmv, duv)
            dt = _each(lambda a, b, c, d: _bdot_nt(a, b) + _bdot_nt(c, d), duv, cm["vb"], dwv, cm["kbg"])
            dkbg = _each(_bdot_tn, tmv, dwv)
            yield
            m1 = _each(_bdot_tn, tmv, dt)
            yield
            da = _each(lambda a, b: -jnp.where(cm["strict"], _bdot_nt(a, b), 0.0), m1, tmv)
            yield
            dkk = _each(lambda a, b: a * b, da, cm["decay"])
            dqk = _each(lambda a, b: a * b, dattv, cm["decay"])
            dd = _each(lambda a, b, c, d: a * b + c * d, dkk, cm["kk"], dqk, cm["qk"])
            dkb = _each(lambda a, b, c, d: _bdot(a, b) + c * d, dkk, kn, dkbg, cm["egc"])
            dq = _each(lambda a, b, c, d: _bdot(a, b) + c * d, dqk, kn, dqdv, cm["egc"])
            yield
            dkn = _each(lambda a, b, c, d: _bdot_tn(a, b) + _bdot_tn(c, d), dkk, cm["kb"], dqk, cm["q"])
            dkn = _each(lambda a, b, c, d, e: a + b * c + d * e, dkn, dkdv, cm["ekd"], dkb, beta)
            t_kd = _each(lambda a, b, c: _rowsum(a * b * c), dkdv, kn, cm["ekd"])
            yield
            split = _each(_split, dd)
            rows_dd = [jnp.dot(hi, ones, preferred_element_type=F32) + jnp.dot(lo, ones, preferred_element_type=F32)
                       for hi, lo in split]
            cols_dd = [lax.dot_general(hi, ones, tn, preferred_element_type=F32)
                       + lax.dot_general(lo, ones, tn, preferred_element_type=F32) for hi, lo in split]
            yield
            dgc = _each(lambda r, c, a, b, e, f, k, tk: r - c + _rowsum(a * b * e) + _rowsum(f * k) - tk,
                        rows_dd, cols_dd, dqdv, cm["q"], cm["egc"], dkbg, cm["kbg"], t_kd)
            same_b = cm["same"].astype(BF16)
            rowi = lax.broadcasted_iota(I32, (PAIR, HEAD), 0)
            dcd = _each(lambda d: jnp.where(rowi < CHUNK, d[0:1], d[8:9]), kept(dcd_s, rows=16))
            dgl = _each(lambda tk, d, c: _mask_dot(same_b, jnp.broadcast_to(tk, (PAIR, HEAD))) + d * c, t_kd, dcd, cm["cd"])
            yield
            is_last = jnp.bitwise_and(rowi, CHUNK - 1) == CHUNK - 1
            dgc = _each(lambda a, b: a + jnp.where(is_last, b, 0.0), dgc, dgl)
            r = lax.broadcasted_iota(I32, (PAIR, PAIR), 0)
            c = lax.broadcasted_iota(I32, (PAIR, PAIR), 1)
            upper_b = (cm["same"] & (r <= c)).astype(BF16)
            _put_heads(dg_ref, _each(lambda v: _mask_dot(upper_b, v), dgc))
            yield
            _put_heads(dbeta_ref, _each(lambda a, b, c, d: jnp.broadcast_to(_rowsum(a * b) + _rowsum(c * d), (PAIR, HEAD)),
                                        dkb, kn, dvb, vs))
            _put_heads(dqn_ref, _each(lambda v: v * QK_SCALE, dq))
            _put_heads(dkn_ref, dkn)
            _put_heads(dvs_ref, _each(lambda a, b: a * b, dvb, beta))
            yield

        _interleave(recurrence(), factors())

    last = n_steps - 1
    now = lambda i: (jnp.maximum(last - i, 0), 0)
    after = lambda i: (jnp.minimum(n_steps - i, last), 0)
    rows = lambda index: pl.BlockSpec((SCAN_ROWS, D_HALF), index)
    slot = lambda r, dtype: pltpu.VMEM((2, r, D_HALF), dtype)
    return _call(
        body, name="delta_bwd", grid=(n_steps + 1,),
        in_specs=[rows(now)] * 6 + [_chunk_scalar_spec(SCAN_PAIRS, now),
                                    pl.BlockSpec((2 * SCAN_PAIRS, N_HEADS, HEAD, HEAD), lambda i: (jnp.maximum(last - i, 0), 0, 0, 0))]
                 + [rows(after)] * 6,
        out_specs=[rows(after)] * 5,
        out_shape=[_sds((s, D_HALF))] * 5,
        scratch_shapes=[pltpu.VMEM((N_HEADS, HEAD, HEAD), F32), slot(SCAN_ROWS, BF16), slot(SCAN_ROWS, BF16),
                        slot(SCAN_ROWS, F32), slot(SCAN_ROWS, F32), slot(SCAN_ROWS, F32), slot(16 * SCAN_PAIRS, F32)],
        compiler_params=_params("arbitrary"),
    )(do, vn, qd, kd, w, att, cd, st, qn, kn, vs, beta, g, tm)


def _fused_call(name, n_steps, parts):
    n_in = [len(p["inputs"]) for p in parts]
    n_out = [len(p["out_shape"]) for p in parts]
    n_scr = [len(p["scratch"]) for p in parts]

    def body(*refs):
        ins, outs, scr = refs[:sum(n_in)], refs[sum(n_in):sum(n_in) + sum(n_out)], refs[sum(n_in) + sum(n_out):]
        gens, a, b, c = [], 0, 0, 0
        for p, ni, no, ns in zip(parts, n_in, n_out, n_scr):
            gens.append(p["stages"](ins[a:a + ni], outs[b:b + no], scr[c:c + ns]))
            a, b, c = a + ni, b + no, c + ns
        _interleave(*gens)

    flat = lambda key: [v for p in parts for v in p[key]]
    res = _call(
        body, name=name, grid=(n_steps,),
        in_specs=flat("in_specs"), out_specs=flat("out_specs"), out_shape=flat("out_shape"),
        scratch_shapes=flat("scratch"),
        compiler_params=_params("arbitrary"),
    )(*flat("inputs"))
    out, b = [], 0
    for no in n_out:
        out.append(res[b:b + no])
        b += no
    return out


def _pool_bwd_part(proj, dyp, pool_w, pool_scale, tile_of, n_tiles):
    s = proj.shape[0]
    t = POOL_T
    hb = t // HEAD
    last = s // HEAD - 1

    def stages(ins, outs, scratch):
        u_ref, z_ref, halo_ref, dy_ref, zn_ref, dyn_ref, pw_ref, ps_ref, band_ref, aband_ref = ins
        du_ref, dz_ref, gpw_ref, gps_ref = outs
        tile = tile_of(pl.program_id(0))

        @pl.when(pl.program_id(0) == 0)
        def _():
            gpw_ref[...] = jnp.zeros_like(gpw_ref)
            gps_ref[...] = jnp.zeros_like(gps_ref)

        live = (tile > 0).astype(F32)
        more = (tile < n_tiles - 1).astype(F32)
        groups = lambda ref: [ref[:, sl] for sl in HEAD_COLS]
        z, ps, dy = groups(z_ref), groups(ps_ref), groups(dy_ref)
        pw = [pw_ref[g] for g in HEADS]
        mix, mixed, sg, cnt = _pool_mix(groups(u_ref), [h * live for h in groups(halo_ref)], z, pw,
                                        [band_ref[g] for g in HEADS], tile * t)
        yield
        sz = _each(lambda a, b: a * b, z, sg)
        for sl, d, m, p, s_, zg in zip(HEAD_COLS, dy, mixed, ps, sg, z):
            dz_ref[:, sl] = (d * m * p * (s_ * (1.0 + zg * (1.0 - s_)))).astype(BF16)
        for sl, d, m, a in zip(HEAD_COLS, dy, mixed, sz):
            gps_ref[:, sl] += _colsum(d * m * a)
        dmixed = _each(lambda d, p, a: d * p * a, dy, ps, sz)
        yield
        for g, gp in enumerate(_each(_bdot_tn, mix, dmixed)):
            gpw_ref[g] += gp
        dmix = _each(_bdot_nt, dmixed, pw)
        yield
        dmix_n = _each(lambda d, p, zn, w_: _bdot_nt(d * more * p * (zn * _sigmoid(zn)), w_),
                       groups(dyn_ref), ps, groups(zn_ref), pw)
        yield
        scaled = [jnp.concatenate([a / c, b * (1.0 / w)], axis=0) for a, c, b, w in zip(dmix, cnt, dmix_n, WINDOWS)]
        du = _each(lambda b, s_, d: _mask_dot(b, s_) - d, [aband_ref[g] for g in HEADS], scaled, dmix)
        for sl, v in zip(HEAD_COLS, du):
            du_ref[:, sl] = v.astype(BF16)
        yield

    tile = lambda col: pl.BlockSpec((t, D_HALF), lambda i: (tile_of(i), col))
    below = lambda col: pl.BlockSpec((HEAD, D_HALF), lambda i: (jnp.minimum((tile_of(i) + 1) * hb, last), col))
    const3 = lambda shape: pl.BlockSpec(shape, lambda i: (0, 0, 0))
    return dict(
        inputs=[proj, proj, proj, dyp, proj, dyp, pool_w, pool_scale, _pool_bands(t), _pool_bands(t, anti=True)],
        in_specs=[tile(0), tile(1), pl.BlockSpec((HEAD, D_HALF), lambda i: (jnp.maximum(tile_of(i) * hb - 1, 0), 0)),
                  tile(0), below(1), below(0), const3((N_HEADS, HEAD, HEAD)), pl.BlockSpec((1, D_HALF), lambda i: (0, 0)),
                  const3((N_HEADS, t, HEAD + t)), const3((N_HEADS, t, HEAD + t))],
        out_specs=[tile(0), tile(0), const3((N_HEADS, HEAD, HEAD)), pl.BlockSpec((1, D_HALF), lambda i: (0, 0))],
        out_shape=[_sds((s, D_HALF), BF16), _sds((s, D_HALF), BF16), _sds((N_HEADS, HEAD, HEAD)), _sds((1, D_HALF))],
        scratch=[], stages=stages)


def _conv_bwd_part(proj, pre, conv_w, a_log, dt_bias, dqn, dkn, dvs, dbeta, dg, tile_of, n_tiles):
    s = proj.shape[0]
    t = CONV_T

    def stages(ins, outs, scratch):
        (q_ref, k_ref, v_ref, yq_ref, yk_ref, yv_ref, ba_ref, cw_ref, al_ref, dtb_ref,
         dqn_ref, dkn_ref, dvs_ref, dbeta_ref, dg_ref) = ins
        oq_ref, ok_ref, ov_ref, dba_ref, gcw_out, gsm_out = outs
        below, gcw_ref, gsm_ref = scratch
        step = pl.program_id(0)

        @pl.when(step == 0)
        def _():
            gcw_ref[...] = jnp.zeros_like(gcw_ref)
            gsm_ref[...] = jnp.zeros_like(gsm_ref)
            below[...] = jnp.zeros_like(below)

        parts = ((q_ref, yq_ref, dqn_ref, oq_ref), (k_ref, yk_ref, dkn_ref, ok_ref), (v_ref, yv_ref, dvs_ref, ov_ref))
        for p, (x_ref, y_ref, d_ref, o_ref) in enumerate(parts):
            for h in HEADS:
                cs = HEAD_COLS[h]
                wide = slice(p * D_HALF + h * HEAD, p * D_HALF + (h + 1) * HEAD)
                cw = cw_ref[:, wide]
                y = y_ref[:, cs]
                sg = _sigmoid(y)
                sv = y * sg
                ds = d_ref[:, cs]
                if p < 2:
                    rn = lax.rsqrt(_rowsum(sv * sv) + EPS)
                    nrm = sv * rn
                    ds = rn * (ds - nrm * _rowsum(ds * nrm))
                dy = ds * (sg * (1.0 + y * (1.0 - sg)))
                nxt = below[:, wide]
                ahead = [dy] + [_shift_up(dy, nxt, sft) for sft in range(1, CONV_K)]
                xv = x_ref[:, cs]
                acc = dy * cw[CONV_K - 1:CONV_K]
                for sft in range(1, CONV_K):
                    acc = acc + ahead[sft] * cw[CONV_K - 1 - sft:CONV_K - sft]
                for j in range(CONV_K):
                    gcw_ref[8 * j:8 * j + 8, wide] += _rows8(xv * ahead[CONV_K - 1 - j])
                o_ref[:, cs] = acc.astype(BF16)
                below[:, wide] = dy[0:8]
                yield

        ba = ba_ref[...]
        lane = lax.broadcasted_iota(I32, (t, HEAD), 1)
        lane8 = lax.broadcasted_iota(I32, (8, HEAD), 1)
        dba = jnp.zeros((t, HEAD), F32)
        gsm = jnp.zeros((8, HEAD), F32)
        for h in HEADS:
            beta = _sigmoid(ba[:, h:h + 1])
            dbeta = dbeta_ref[:, h * HEAD:h * HEAD + 1]
            xg = ba[:, N_HEADS + h:N_HEADS + h + 1] + dtb_ref[0:1, h:h + 1]
            nexp = -jnp.exp(al_ref[0:1, h:h + 1])
            dgv = dg_ref[:, h * HEAD:h * HEAD + 1]
            da = dgv * nexp * _sigmoid(xg)
            dba = dba + jnp.where(lane == h, dbeta * beta * (1.0 - beta), 0.0) + jnp.where(lane == N_HEADS + h, da, 0.0)
            gsm = (gsm + jnp.where(lane8 == h, _rows8(dgv * nexp * _softplus(xg)), 0.0)
                   + jnp.where(lane8 == N_HEADS + h, _rows8(da), 0.0))
        dba_ref[...] = jnp.zeros_like(dba_ref)
        dba_ref[:, :HEAD] = dba.astype(BF16)
        gsm_ref[...] += gsm
        yield

        @pl.when(step == n_tiles - 1)
        def _():
            gcw_out[...] = jnp.zeros_like(gcw_out)
            for j in range(CONV_K):
                gcw_out[j:j + 1, :] = _colsum(gcw_ref[8 * j:8 * j + 8, :])
            gsm_out[...] = jnp.broadcast_to(_colsum(gsm_ref[...]), (8, HEAD))

    row = pl.BlockSpec((t, D_HALF), lambda i: (tile_of(i), 0))
    const = lambda shape: pl.BlockSpec(shape, lambda i: (0, 0))
    return dict(
        inputs=[proj] * 3 + list(pre) + [proj, conv_w, a_log, dt_bias, dqn, dkn, dvs, dbeta, dg],
        in_specs=_conv_specs(t, tile_of)[:3] + [row] * 3
                 + [pl.BlockSpec((t, HEAD), lambda i: (tile_of(i), COL_BA // HEAD)),
                    const((CONV_K, 3 * D_HALF)), const((1, N_HEADS)), const((1, N_HEADS))] + [row] * 5,
        out_specs=[row, row, row, row, const((8, 3 * D_HALF)), const((8, HEAD))],
        out_shape=[_sds((s, D_HALF), BF16)] * 4 + [_sds((8, 3 * D_HALF)), _sds((8, HEAD))],
        scratch=[pltpu.VMEM((8, 3 * D_HALF), F32), pltpu.VMEM((8 * CONV_K, 3 * D_HALF), F32), pltpu.VMEM((8, HEAD), F32)],
        stages=stages)


def _pool_fwd_part(proj, pool_w, pool_scale):
    s = proj.shape[0]
    t = POOL_T
    hb = t // HEAD

    def stages(ins, outs, scratch, tile=None):
        u_ref, z_ref, halo_ref, pw_ref, ps_ref, band_ref = ins
        y_ref, = outs
        i = pl.program_id(0) if tile is None else tile
        live = (i > 0).astype(F32)
        groups = lambda ref: [ref[:, sl] for sl in HEAD_COLS]
        z = groups(z_ref)
        u, halo = groups(u_ref), [h * live for h in groups(halo_ref)]
        yield
        _, mixed, sg, _ = _pool_mix(u, halo, z, [pw_ref[g] for g in HEADS], [band_ref[g] for g in HEADS], i * t)
        yield
        for sl, m, zg, s_ in zip(HEAD_COLS, mixed, z, sg):
            y_ref[:, sl] = m * ps_ref[:, sl] * (zg * s_)
        yield

    const3 = lambda shape: pl.BlockSpec(shape, lambda i: (0, 0, 0))
    return dict(
        inputs=[proj, proj, proj, pool_w, pool_scale, _pool_bands(t)],
        in_specs=[pl.BlockSpec((t, D_HALF), lambda i: (i, 0)), pl.BlockSpec((t, D_HALF), lambda i: (i, 1)),
                  pl.BlockSpec((HEAD, D_HALF), lambda i: (jnp.maximum(i * hb - 1, 0), 0)),
                  const3((N_HEADS, HEAD, HEAD)), pl.BlockSpec((1, D_HALF), lambda i: (0, 0)), const3((N_HEADS, t, HEAD + t))],
        out_specs=[pl.BlockSpec((t, D_HALF), lambda i: (i, 0))], out_shape=[_sds((s, D_HALF))],
        scratch=[], stages=stages)


def _conv_fwd_part(proj, conv_w, a_log, dt_bias):
    s = proj.shape[0]
    t = CONV_T

    def stages(ins, outs, scratch, tile=None):
        q_ref, k_ref, v_ref, hq_ref, hk_ref, hv_ref, ba_ref, cw_ref, al_ref, dtb_ref = ins
        qn_ref, kn_ref, vs_ref, beta_ref, g_ref, yq_ref, yk_ref, yv_ref = outs
        live = ((pl.program_id(0) if tile is None else tile) > 0).astype(F32)
        parts = ((q_ref, hq_ref, qn_ref, yq_ref), (k_ref, hk_ref, kn_ref, yk_ref), (v_ref, hv_ref, vs_ref, yv_ref))
        for p, (x_ref, h_ref, o_ref, y_ref) in enumerate(parts):
            for h in HEADS:
                cs = HEAD_COLS[h]
                taps = _conv_taps(x_ref[:, cs], h_ref[:, cs] * live)
                y = _conv_pre(taps, cw_ref[:, p * D_HALF + h * HEAD:p * D_HALF + (h + 1) * HEAD])
                y_ref[:, cs] = y
                sv = y * _sigmoid(y)
                o_ref[:, cs] = sv if p == 2 else sv * lax.rsqrt(_rowsum(sv * sv) + EPS)
                yield
        ba = ba_ref[...]
        for h in HEADS:
            beta = _sigmoid(ba[:, h:h + 1])
            gl = -jnp.exp(al_ref[0:1, h:h + 1]) * _softplus(ba[:, N_HEADS + h:N_HEADS + h + 1] + dtb_ref[0:1, h:h + 1])
            beta_ref[:, HEAD_COLS[h]] = jnp.broadcast_to(beta, (t, HEAD))
            g_ref[:, HEAD_COLS[h]] = jnp.broadcast_to(gl, (t, HEAD))
        yield

    row = pl.BlockSpec((t, D_HALF), lambda i: (i, 0))
    const = lambda shape: pl.BlockSpec(shape, lambda i: (0, 0))
    return dict(
        inputs=[proj] * 7 + [conv_w, a_log, dt_bias],
        in_specs=_conv_specs(t) + [pl.BlockSpec((t, HEAD), lambda i: (i, COL_BA // HEAD)),
                                   const((CONV_K, 3 * D_HALF)), const((1, N_HEADS)), const((1, N_HEADS))],
        out_specs=[row] * 8, out_shape=[_sds((s, D_HALF))] * 8, scratch=[], stages=stages)


def _front_fwd(x, norm_w, w_pad, conv_w, a_log, dt_bias, pool_w, pool_scale):
    s = x.shape[0]
    t = CONV_T
    n_tiles = s // t
    assert POOL_T == CONV_T
    like_proj = _sds((s, N_IN_PAD))
    conv = _conv_fwd_part(like_proj, conv_w, a_log, dt_bias)
    pool = _pool_fwd_part(like_proj, pool_w, pool_scale)
    bands = pool["inputs"][-1]
    n_col = 5
    cw = N_IN_PAD // n_col

    def body(x_ref, nw_ref, w_ref, cw_ref, al_ref, dtb_ref, pw_ref, ps_ref, band_ref,
             proj_ref, nt_ref, qn_ref, kn_ref, vs_ref, beta_ref, g_ref, yq_ref, yk_ref, yv_ref, y_ref, prev):
        i = pl.program_id(0)

        @pl.when(i == 0)
        def _():
            prev[...] = jnp.zeros_like(prev)

        tile = jnp.maximum(i - 1, 0)
        main, above8, above = pl.ds(HEAD, t), pl.ds(HEAD - 8, 8), pl.ds(0, HEAD)
        cols = lambda rows, c0, width=D_HALF: prev.at[rows, pl.ds(c0, width)]
        conv_ins = (cols(main, 2 * D_HALF), cols(main, 3 * D_HALF), cols(main, 4 * D_HALF),
                    cols(above8, 2 * D_HALF), cols(above8, 3 * D_HALF), cols(above8, 4 * D_HALF),
                    cols(main, COL_BA, HEAD), cw_ref, al_ref, dtb_ref)
        pool_ins = (cols(main, 0), cols(main, D_HALF), cols(above, 0), pw_ref, ps_ref, band_ref)

        def projection():
            xv = x_ref[...]
            r = lax.rsqrt(jnp.mean(xv * xv, axis=-1, keepdims=True) + EPS)
            nv = xv * r * nw_ref[...]
            nt_ref[...] = nv.T.astype(BF16)
            nb = nv.astype(BF16)
            yield
            for c in range(n_col):
                proj_ref[:, c * cw:(c + 1) * cw] = jnp.dot(nb, w_ref[:, c * cw:(c + 1) * cw], preferred_element_type=F32)
                yield

        _interleave(projection(),
                    conv["stages"](conv_ins, (qn_ref, kn_ref, vs_ref, beta_ref, g_ref, yq_ref, yk_ref, yv_ref), (), tile),
                    pool["stages"](pool_ins, (y_ref,), (), tile))
        prev[0:HEAD] = prev[t:t + HEAD]
        prev[HEAD:HEAD + t] = proj_ref[...]

    last = n_tiles - 1
    now = lambda i: (jnp.minimum(i, last), 0)
    before = lambda i: (jnp.maximum(i - 1, 0), 0)
    const = lambda a: pl.BlockSpec(a.shape, lambda i: (0,) * a.ndim)
    half = pl.BlockSpec((t, D_HALF), before)
    return _call(
        body, name="front_fwd", grid=(n_tiles + 1,),
        in_specs=[pl.BlockSpec((t, D_MODEL), now), const(norm_w), const(w_pad), const(conv_w), const(a_log), const(dt_bias),
                  const(pool_w), const(pool_scale), const(bands)],
        out_specs=[pl.BlockSpec((t, N_IN_PAD), now), pl.BlockSpec((D_MODEL, t), lambda i: (0, jnp.minimum(i, last)))]
                  + [half] * 9,
        out_shape=[_sds((s, N_IN_PAD)), _sds((D_MODEL, s), BF16)] + [_sds((s, D_HALF))] * 9,
        scratch_shapes=[pltpu.VMEM((HEAD + t, N_IN_PAD), F32)],
        compiler_params=_params("arbitrary"),
    )(x, norm_w, w_pad, conv_w, a_log, dt_bias, pool_w, pool_scale, bands)


def _conv_pool_fwd(proj, conv_w, a_log, dt_bias, pool_w, pool_scale):
    assert POOL_T == CONV_T
    return _fused_call("conv_pool_fwd", proj.shape[0] // CONV_T, [
        _conv_fwd_part(proj, conv_w, a_log, dt_bias), _pool_fwd_part(proj, pool_w, pool_scale)])


def _conv_pool_bwd(proj, pre, conv_w, a_log, dt_bias, dqn, dkn, dvs, dbeta, dg, dyp, pool_w, pool_scale):
    n_tiles = proj.shape[0] // CONV_T
    assert POOL_T == CONV_T
    tile_of = lambda i: n_tiles - 1 - i
    return _fused_call("conv_pool_bwd", n_tiles, [
        _conv_bwd_part(proj, pre, conv_w, a_log, dt_bias, dqn, dkn, dvs, dbeta, dg, tile_of, n_tiles),
        _pool_bwd_part(proj, dyp, pool_w, pool_scale, tile_of, n_tiles)])


def _rows8(x):
    acc = x[0:8]
    for r in range(8, x.shape[0], 8):
        acc = acc + x[r:r + 8]
    return acc


def _conv_bwd(proj, conv_w, a_log, dt_bias, dqn, dkn, dvs, dbeta, dg):
    s = proj.shape[0]
    t = CONV_T
    n_tiles = s // t
    n_sub = t // CONV_SUB
    tile_of = lambda i: n_tiles - 1 - i

    def body(q_ref, k_ref, v_ref, hq_ref, hk_ref, hv_ref, ba_ref, cw_ref, al_ref, dtb_ref,
             dqn_ref, dkn_ref, dvs_ref, dbeta_ref, dg_ref, oq_ref, ok_ref, ov_ref, dba_ref, gcw_out, gsm_out,
             below, gcw_ref, gsm_ref):
        @pl.when(pl.program_id(0) == 0)
        def _():
            gcw_ref[...] = jnp.zeros_like(gcw_ref)
            gsm_ref[...] = jnp.zeros_like(gsm_ref)
            below[...] = jnp.zeros_like(below)

        live = (pl.program_id(0) < n_tiles - 1).astype(F32)
        parts = ((q_ref, hq_ref, dqn_ref, oq_ref), (k_ref, hk_ref, dkn_ref, ok_ref), (v_ref, hv_ref, dvs_ref, ov_ref))
        lane = lax.broadcasted_iota(I32, (CONV_SUB, HEAD), 1)
        lane8 = lax.broadcasted_iota(I32, (8, HEAD), 1)

        def sub_tile(r0, first):
            rows = pl.ds(r0, CONV_SUB)
            for p, (x_ref, h_ref, d_ref, o_ref) in enumerate(parts):
                for h in HEADS:
                    cs = HEAD_COLS[h]
                    wide = slice(p * D_HALF + h * HEAD, p * D_HALF + (h + 1) * HEAD)
                    cw = cw_ref[:, wide]
                    prev8 = h_ref[:, cs] * live if first else x_ref[pl.ds(r0 - 8, 8), cs]
                    taps = _conv_taps(x_ref[rows, cs], prev8)
                    y = _conv_pre(taps, cw)
                    sg = _sigmoid(y)
                    sv = y * sg
                    ds = d_ref[rows, cs]
                    if p < 2:
                        rn = lax.rsqrt(_rowsum(sv * sv) + EPS)
                        nrm = sv * rn
                        ds = rn * (ds - nrm * _rowsum(ds * nrm))
                    dy = ds * (sg * (1.0 + y * (1.0 - sg)))
                    for j in range(CONV_K):
                        gcw_ref[8 * j:8 * j + 8, wide] += _rows8(dy * taps[j])
                    nxt = below[:, wide]
                    acc = dy * cw[CONV_K - 1:CONV_K]
                    for sft in range(1, CONV_K):
                        acc = acc + _shift_up(dy, nxt, sft) * cw[CONV_K - 1 - sft:CONV_K - sft]
                    o_ref[rows, cs] = acc.astype(BF16)
                    below[:, wide] = dy[0:8]

            ba = ba_ref[rows, :]
            dba = jnp.zeros((CONV_SUB, HEAD), F32)
            gsm = jnp.zeros((8, HEAD), F32)
            for h in HEADS:
                beta = _sigmoid(ba[:, h:h + 1])
                dbeta = dbeta_ref[rows, h * HEAD:h * HEAD + 1]
                xg = ba[:, N_HEADS + h:N_HEADS + h + 1] + dtb_ref[0:1, h:h + 1]
                nexp = -jnp.exp(al_ref[0:1, h:h + 1])
                dgv = dg_ref[rows, h * HEAD:h * HEAD + 1]
                da = dgv * nexp * _sigmoid(xg)
                dba = dba + jnp.where(lane == h, dbeta * beta * (1.0 - beta), 0.0) + jnp.where(lane == N_HEADS + h, da, 0.0)
                gsm = (gsm + jnp.where(lane8 == h, _rows8(dgv * nexp * _softplus(xg)), 0.0)
                       + jnp.where(lane8 == N_HEADS + h, _rows8(da), 0.0))
            dba_ref[rows, :] = jnp.zeros((CONV_SUB, D_HALF), BF16)
            dba_ref[rows, :HEAD] = dba.astype(BF16)
            gsm_ref[...] += gsm

        def step(k, carry):
            sub_tile(pl.multiple_of((n_sub - 1 - k) * CONV_SUB, CONV_SUB), False)
            return carry

        lax.fori_loop(0, n_sub - 1, step, 0)
        sub_tile(0, True)

        @pl.when(pl.program_id(0) == n_tiles - 1)
        def _():
            gcw_out[...] = jnp.zeros_like(gcw_out)
            for j in range(CONV_K):
                gcw_out[j:j + 1, :] = _colsum(gcw_ref[8 * j:8 * j + 8, :])
            gsm_out[...] = jnp.broadcast_to(_colsum(gsm_ref[...]), (8, HEAD))

    row = pl.BlockSpec((t, D_HALF), lambda i: (tile_of(i), 0))
    const = lambda shape: pl.BlockSpec(shape, lambda i: (0, 0))
    return _call(
        body, name="conv_bwd", grid=(n_tiles,),
        in_specs=_conv_specs(t, tile_of) + [pl.BlockSpec((t, HEAD), lambda i: (tile_of(i), COL_BA // HEAD)),
                                            const((CONV_K, 3 * D_HALF)), const((1, N_HEADS)), const((1, N_HEADS))] + [row] * 5,
        out_specs=[row, row, row, row, const((8, 3 * D_HALF)), const((8, HEAD))],
        out_shape=[_sds((s, D_HALF), BF16)] * 4 + [_sds((8, 3 * D_HALF)), _sds((8, HEAD))],
        scratch_shapes=[pltpu.VMEM((8, 3 * D_HALF), F32), pltpu.VMEM((8 * CONV_K, 3 * D_HALF), F32),
                        pltpu.VMEM((8, HEAD), F32)],
        compiler_params=_params("arbitrary"),
    )(proj, proj, proj, proj, proj, proj, proj, conv_w, a_log, dt_bias, dqn, dkn, dvs, dbeta, dg)


def _conv_bwd_pre(proj, conv_w, a_log, dt_bias, dqn, dkn, dvs, dbeta, dg):
    s = proj.shape[0]
    t = CONV_T

    def body(q_ref, k_ref, v_ref, hq_ref, hk_ref, hv_ref, ba_ref, cw_ref, al_ref, dtb_ref,
             dqn_ref, dkn_ref, dvs_ref, dbeta_ref, dg_ref, dyq_ref, dyk_ref, dyv_ref, dba_ref, gcw_ref, gsm_ref):
        @pl.when(pl.program_id(0) == 0)
        def _():
            gcw_ref[...] = jnp.zeros_like(gcw_ref)
            gsm_ref[...] = jnp.zeros_like(gsm_ref)

        live = (pl.program_id(0) > 0).astype(F32)
        parts = ((q_ref, hq_ref, dqn_ref, dyq_ref), (k_ref, hk_ref, dkn_ref, dyk_ref), (v_ref, hv_ref, dvs_ref, dyv_ref))
        for p, (x_ref, h_ref, d_ref, dy_ref) in enumerate(parts):
            cols = slice(p * D_HALF, (p + 1) * D_HALF)
            taps = _conv_taps(x_ref[...], h_ref[...] * live)
            y = _conv_pre(taps, cw_ref[:, cols])
            sg = _sigmoid(y)
            sv = y * sg
            if p == 2:
                ds = d_ref[...]
            else:
                segs = []
                for h in HEADS:
                    seg = _head(sv, h)
                    rn = lax.rsqrt(_rowsum(seg * seg) + EPS)
                    nrm = seg * rn
                    dn = d_ref[:, HEAD_COLS[h]]
                    segs.append(rn * (dn - nrm * _rowsum(dn * nrm)))
                ds = jnp.concatenate(segs, axis=1)
            dy = ds * (sg * (1.0 + y * (1.0 - sg)))
            dy_ref[...] = dy
            for j in range(CONV_K):
                gcw_ref[j:j + 1, cols] += _colsum(dy * taps[j])

        ba = ba_ref[...]
        lane = lax.broadcasted_iota(I32, (t, HEAD), 1)
        lane1 = lax.broadcasted_iota(I32, (1, HEAD), 1)
        dba = jnp.zeros((t, HEAD), F32)
        gsm = jnp.zeros((1, HEAD), F32)
        for h in HEADS:
            beta = _sigmoid(ba[:, h:h + 1])
            dbeta = dbeta_ref[:, h * HEAD:h * HEAD + 1]
            xg = ba[:, N_HEADS + h:N_HEADS + h + 1] + dtb_ref[0:1, h:h + 1]
            nexp = -jnp.exp(al_ref[0:1, h:h + 1])
            dgv = dg_ref[:, h * HEAD:h * HEAD + 1]
            da = dgv * nexp * _sigmoid(xg)
            dba = dba + jnp.where(lane == h, dbeta * beta * (1.0 - beta), 0.0) + jnp.where(lane == N_HEADS + h, da, 0.0)
            gsm = (gsm + jnp.where(lane1 == h, _colsum(dgv * nexp * _softplus(xg)), 0.0)
                   + jnp.where(lane1 == N_HEADS + h, _colsum(da), 0.0))
        dba_ref[...] = jnp.zeros_like(dba_ref)
        dba_ref[:, :HEAD] = dba.astype(BF16)
        gsm_ref[0:1, :] += gsm

    row = pl.BlockSpec((t, D_HALF), lambda i: (i, 0))
    return _call(
        body, name="conv_bwd_pre", grid=(s // t,),
        in_specs=_conv_specs(t) + [pl.BlockSpec((t, HEAD), lambda i: (i, COL_BA // HEAD)),
                                   pl.BlockSpec((CONV_K, 3 * D_HALF), lambda i: (0, 0)),
                                   pl.BlockSpec((1, N_HEADS), lambda i: (0, 0)),
                                   pl.BlockSpec((1, N_HEADS), lambda i: (0, 0))] + [row] * 5,
        out_specs=[row, row, row, row,
                   pl.BlockSpec((8, 3 * D_HALF), lambda i: (0, 0)), pl.BlockSpec((8, HEAD), lambda i: (0, 0))],
        out_shape=[_sds((s, D_HALF))] * 3 + [_sds((s, D_HALF), BF16), _sds((8, 3 * D_HALF)), _sds((8, HEAD))],
        compiler_params=_params("arbitrary"),
    )(proj, proj, proj, proj, proj, proj, proj, conv_w, a_log, dt_bias, dqn, dkn, dvs, dbeta, dg)


def _conv_bwd_in(dyq, dyk, dyv, conv_w):
    s = dyq.shape[0]
    t = CONV_T
    last = s // 8 - 1

    def body(q_ref, k_ref, v_ref, nq_ref, nk_ref, nv_ref, cw_ref, oq_ref, ok_ref, ov_ref):
        more = (pl.program_id(0) < pl.num_programs(0) - 1).astype(F32)
        for p, (d_ref, n_ref, o_ref) in enumerate(((q_ref, nq_ref, oq_ref), (k_ref, nk_ref, ok_ref), (v_ref, nv_ref, ov_ref))):
            cw = cw_ref[:, p * D_HALF:(p + 1) * D_HALF]
            dy = d_ref[...]
            nxt = n_ref[...] * more
            acc = dy * cw[3:4]
            for sft in (1, 2, 3):
                acc = acc + _shift_up(dy, nxt, sft) * cw[3 - sft:4 - sft]
            o_ref[...] = acc.astype(BF16)

    row = pl.BlockSpec((t, D_HALF), lambda i: (i, 0))
    nxt = pl.BlockSpec((8, D_HALF), lambda i: (jnp.minimum((i + 1) * (t // 8), last), 0))
    return _call(
        body, name="conv_bwd_in", grid=(s // t,),
        in_specs=[row] * 3 + [nxt] * 3 + [pl.BlockSpec((CONV_K, 3 * D_HALF), lambda i: (0, 0))],
        out_specs=[row] * 3, out_shape=[_sds((s, D_HALF), BF16)] * 3,
        compiler_params=_params("arbitrary"),
    )(dyq, dyk, dyv, dyq, dyk, dyv, conv_w)


IN_T = 512


def _in_bwd(x, dh, norm_w, w_pad, pieces):
    s = x.shape[0]
    t = IN_T
    widths = [D_HALF] * 6 + [N_IN_PAD - COL_BA]

    def body(*refs):
        x_ref, dh_ref, nw_ref, w_ref = refs[:4]
        p_refs = refs[4:4 + len(pieces)]
        gx_ref, gnw_ref = refs[4 + len(pieces):]

        @pl.when(pl.program_id(0) == 0)
        def _():
            gnw_ref[...] = jnp.zeros_like(gnw_ref)

        dn = jnp.zeros((t, D_MODEL), F32)
        col = 0
        for p_ref, wd in zip(p_refs, widths):
            dn = dn + _bdot_nt(p_ref[...], w_ref[:, col:col + wd])
            col += wd
        xv = x_ref[...]
        r = lax.rsqrt(jnp.mean(xv * xv, axis=-1, keepdims=True) + EPS)
        xhat = xv * r
        gnw_ref[...] += _colsum(dn * xhat)
        dxh = dn * nw_ref[...]
        gx_ref[...] = dh_ref[...] + r * (dxh - xhat * jnp.mean(dxh * xhat, axis=-1, keepdims=True))

    wide = pl.BlockSpec((t, D_MODEL), lambda i: (i, 0))
    return _call(
        body, name="in_bwd", grid=(s // t,),
        in_specs=[wide, wide, pl.BlockSpec((1, D_MODEL), lambda i: (0, 0)),
                  pl.BlockSpec((D_MODEL, N_IN_PAD), lambda i: (0, 0))]
                 + [pl.BlockSpec((t, wd), lambda i: (i, 0)) for wd in widths],
        out_specs=[wide, pl.BlockSpec((1, D_MODEL), lambda i: (0, 0))],
        out_shape=[_sds((s, D_MODEL)), _sds((1, D_MODEL))],
        compiler_params=_params("arbitrary"),
    )(x, dh, norm_w, w_pad, *pieces)


def _adamw_shard(name, w, g_own, g_got, cidx, m, v):
    _, r, c = w.shape
    half = r // 2
    rows = 256 if half % 256 == 0 else half
    per_half = half // rows

    def body(c_ref, w_ref, go_ref, gg_ref, m_ref, v_ref, gout_ref, d_ref, nm_ref, nv_ref):
        mine = (pl.program_id(0) // per_half) == c_ref[0]
        gv = jnp.where(mine, go_ref[:, :c], gg_ref[:, :c])
        gout_ref[0] = gv
        mn = ADAM_B1 * m_ref[0] + (1.0 - ADAM_B1) * gv
        vn = ADAM_B2 * v_ref[0] + (1.0 - ADAM_B2) * (gv * gv)
        m_hat = mn / (1.0 - ADAM_B1 ** ADAM_STEP)
        v_hat = vn / (1.0 - ADAM_B2 ** ADAM_STEP)
        d_ref[0] = -ADAM_LR * (m_hat / (jnp.sqrt(v_hat) + ADAM_EPS) + ADAM_WD * w_ref[0])
        nm_ref[0] = mn
        nv_ref[0] = vn

    blk = pl.BlockSpec((1, rows, c), lambda i, c_ref: (0, i, 0))
    gblk = pl.BlockSpec((rows, g_own.shape[1]), lambda i, c_ref: (i % per_half, 0))
    return _call(
        body, name=name,
        grid_spec=pltpu.PrefetchScalarGridSpec(
            num_scalar_prefetch=1, grid=(2 * per_half,),
            in_specs=[blk, gblk, gblk, blk, blk], out_specs=[blk] * 4),
        out_shape=[_sds((1, r, c))] * 4,
        compiler_params=_params("arbitrary"),
    )(cidx, w, g_own, g_got, m, v)


def _adamw_tiles(name, w, g, m, v):
    n = w.shape[0]
    nb = 77 if n % 77 == 0 else n

    def body(w_ref, g_ref, m_ref, v_ref, d_ref, nm_ref, nv_ref):
        gv = g_ref[...]
        mn = ADAM_B1 * m_ref[...] + (1.0 - ADAM_B1) * gv
        vn = ADAM_B2 * v_ref[...] + (1.0 - ADAM_B2) * (gv * gv)
        m_hat = mn / (1.0 - ADAM_B1 ** ADAM_STEP)
        v_hat = vn / (1.0 - ADAM_B2 ** ADAM_STEP)
        d_ref[...] = -ADAM_LR * (m_hat / (jnp.sqrt(v_hat) + ADAM_EPS) + ADAM_WD * w_ref[...])
        nm_ref[...] = mn
        nv_ref[...] = vn

    blk = pl.BlockSpec((nb, 8, HEAD), lambda i: (i, 0, 0))
    return _call(
        body, name=name, grid=(n // nb,),
        in_specs=[blk] * 4, out_specs=[blk] * 3, out_shape=[_sds(w.shape)] * 3,
        compiler_params=_params("arbitrary"),
    )(w, g, m, v)


def _exchange(name, inputs, out_shapes, phases):
    n_in = len(inputs)
    n_out = len(out_shapes)
    n_cp = sum(len(p) for p in phases)

    def body(*refs):
        ins, outs = refs[:n_in], refs[n_in:n_in + n_out]
        send, recv = refs[n_in + n_out:]
        pos = (lax.axis_index("x"), lax.axis_index("y"), lax.axis_index("c"))
        k = 0
        for phase in phases:
            cps = []
            for src, dst, target in phase:
                cps.append(pltpu.make_async_remote_copy(
                    src_ref=src(ins, outs, pos), dst_ref=dst(ins, outs, pos), send_sem=send.at[k], recv_sem=recv.at[k],
                    device_id=target(pos), device_id_type=pl.DeviceIdType.MESH))
                k += 1
            for cp in cps:
                cp.start()
            for cp in cps:
                cp.wait()

    anyspec = pl.BlockSpec(memory_space=pl.ANY)
    return _call(
        body, name=name,
        in_specs=[anyspec] * n_in, out_specs=[anyspec] * n_out, out_shape=list(out_shapes),
        scratch_shapes=[pltpu.SemaphoreType.DMA((n_cp,)), pltpu.SemaphoreType.DMA((n_cp,))],
    )(*inputs)


def _exchange_start(name, inputs, out_shapes, copies):
    n_in, n_out, n_cp = len(inputs), len(out_shapes), len(copies)

    def body(*refs):
        ins, lands = refs[:n_in], refs[n_in:n_in + n_out]
        sems = refs[n_in + n_out:n_in + n_out + 2 * n_cp]
        token = refs[-1]
        pos = (lax.axis_index("x"), lax.axis_index("y"), lax.axis_index("c"))
        for k, (src, dst, target) in enumerate(copies):
            pltpu.make_async_remote_copy(
                src_ref=src(ins, lands, pos), dst_ref=dst(ins, lands, pos), send_sem=sems[2 * k], recv_sem=sems[2 * k + 1],
                device_id=target(pos), device_id_type=pl.DeviceIdType.MESH).start()
        token[...] = jnp.zeros_like(token)

    hbm = pl.BlockSpec(memory_space=pltpu.HBM)
    sem = pl.BlockSpec(memory_space=pltpu.SEMAPHORE)
    bufs = list(inputs) + [lax.empty(o.shape, o.dtype) for o in out_shapes]
    outs = _call(
        body, name=name,
        out_shape=tuple([pltpu.SemaphoreType.DMA(())] * (2 * n_cp) + [pltpu.HBM(b.shape, b.dtype) for b in bufs]
                        + [_sds((8, HEAD))]),
        in_specs=[hbm] * len(bufs),
        out_specs=tuple([sem] * (2 * n_cp) + [hbm] * len(bufs) + [pl.BlockSpec(memory_space=pltpu.VMEM)]),
        input_output_aliases={i: 2 * n_cp + i for i in range(len(bufs))},
        compiler_params=pltpu.CompilerParams(has_side_effects=pltpu.SideEffectType.DATAFLOW_SIDE_EFFECTING),
    )(*[pltpu.with_memory_space_constraint(b, pltpu.HBM) for b in bufs])
    return outs[:2 * n_cp], outs[2 * n_cp:2 * n_cp + n_in], outs[2 * n_cp + n_in:-1], outs[-1]


def _exchange_wait(name, sems, sources, lands, copies, after):
    n_in, n_out, n_cp = len(sources), len(lands), len(copies)

    def body(*refs):
        ins, zones = refs[:n_in], refs[n_in:n_in + n_out]
        sem_refs = refs[n_in + n_out:n_in + n_out + 2 * n_cp]
        pos = (lax.axis_index("x"), lax.axis_index("y"), lax.axis_index("c"))
        for k, (src, dst, target) in enumerate(copies):
            cp = pltpu.make_async_remote_copy(
                src_ref=src(ins, zones, pos), dst_ref=dst(ins, zones, pos), send_sem=sem_refs[2 * k],
                recv_sem=sem_refs[2 * k + 1], device_id=target(pos), device_id_type=pl.DeviceIdType.MESH)
            cp.wait_send()
            cp.wait_recv()

    hbm = pl.BlockSpec(memory_space=pltpu.HBM)
    sem = pl.BlockSpec(memory_space=pltpu.SEMAPHORE)
    bufs = list(sources) + list(lands)
    outs = _call(
        body, name=name,
        out_shape=tuple(pltpu.HBM(b.shape, b.dtype) for b in bufs),
        in_specs=[hbm] * len(bufs) + [sem] * (2 * n_cp) + [pl.BlockSpec(memory_space=pl.ANY)],
        out_specs=tuple([hbm] * len(bufs)),
        input_output_aliases={i: i for i in range(len(bufs))},
        compiler_params=pltpu.CompilerParams(has_side_effects=pltpu.SideEffectType.DATAFLOW_SIDE_EFFECTING),
    )(*bufs, *sems, after)
    return outs[:n_in], outs[n_in:]


def _allreduce_tile(name, v):
    def body(v_ref, out_ref, slots, send, recv):
        x, y, c = lax.axis_index("x"), lax.axis_index("y"), lax.axis_index("c")
        me = 4 * x + 2 * y + c
        slots[me] = v_ref[...]
        cps = []
        for k in range(1, 8):
            peer = (x ^ (k >> 2), y ^ ((k >> 1) & 1), c ^ (k & 1))
            cps.append(pltpu.make_async_remote_copy(
                src_ref=v_ref, dst_ref=slots.at[me], send_sem=send.at[k - 1], recv_sem=recv.at[k - 1],
                device_id=peer, device_id_type=pl.DeviceIdType.MESH))
        for cp in cps:
            cp.start()
        for cp in cps:
            cp.wait()
        acc = slots[0]
        for i in range(1, 8):
            acc = acc + slots[i]
        out_ref[...] = acc

    vm = pl.BlockSpec(memory_space=pltpu.VMEM)
    return _call(
        body, name=name, in_specs=[vm], out_specs=vm, out_shape=_sds(v.shape),
        scratch_shapes=[pltpu.VMEM((8,) + v.shape, F32), pltpu.SemaphoreType.DMA((7,)), pltpu.SemaphoreType.DMA((7,))],
    )(v)


def _chip(pos):
    return 2 * pos[0] + pos[1]


def _other_chip(pos, mask):
    x, y, c = pos
    return (x ^ (mask >> 1), y ^ (mask & 1), c)


def _sibling(pos):
    return (pos[0], pos[1], 1 - pos[2])


def _gather_weights(wb, cb):
    rows = wb.shape[0] // 2
    x_nb, y_nb, diag = CHIP_MASKS

    def part(pos, mask, quarter=None):
        start = pos[2] * rows if quarter is None else pos[2] * rows + quarter * (rows // 2)
        return lambda outs: outs[0].at[_chip(pos) ^ mask, pl.ds(start, rows if quarter is None else rows // 2)]

    def passed_on(mask, to, quarter=None):
        return (lambda ins, outs, pos: part(pos, mask, quarter)(outs), lambda ins, outs, pos: part(pos, mask, quarter)(outs), to)

    first = [(lambda ins, outs, pos: ins[0].at[pl.ds(pos[2] * rows, rows)], lambda ins, outs, pos: part(pos, 0)(outs),
              functools.partial(_other_chip, mask=mask)) for mask in (x_nb, y_nb)]
    first += [(lambda ins, outs, pos: ins[1], lambda ins, outs, pos: outs[1].at[_chip(pos)],
               functools.partial(_other_chip, mask=mask)) for mask in CHIP_MASKS]
    second = [passed_on(x_nb, functools.partial(_other_chip, mask=y_nb), quarter=0),
              passed_on(y_nb, functools.partial(_other_chip, mask=x_nb), quarter=1),
              passed_on(x_nb, _sibling), passed_on(y_nb, _sibling)]
    third = [passed_on(diag, _sibling)]
    return _exchange("gather_weights", [wb, cb], [_sds((4,) + wb.shape, wb.dtype), _sds((4,) + cb.shape, cb.dtype)],
                     [first, second, third])


def _assemble_w_in(gw, wb, jidx):
    m = gw.shape[1]

    def body(j_ref, g_ref, wb_ref, o_ref):
        step = pl.program_id(0)

        @pl.when(step == 0)
        def _():
            o_ref[...] = jnp.zeros_like(o_ref)

        blk = jnp.where(step == j_ref[0], wb_ref[...], g_ref[0]).astype(F32)
        lane = lax.broadcasted_iota(I32, (m, BLK_IN_PAD), 1)
        for j in range(4):
            @pl.when(step == j)
            def _(j=j):
                base = j * BLK_IN // HEAD * HEAD
                shift = j * BLK_IN - base
                moved = pltpu.roll(blk, shift, 1) if shift else blk
                window = o_ref[:, base:base + BLK_IN_PAD].astype(F32)
                mine = (lane >= shift) & (lane < shift + BLK_IN)
                o_ref[:, base:base + BLK_IN_PAD] = jnp.where(mine, moved, window).astype(BF16)

    return _call(
        body, name="assemble_w_in",
        grid_spec=pltpu.PrefetchScalarGridSpec(
            num_scalar_prefetch=1, grid=(4,),
            in_specs=[pl.BlockSpec((1, m, BLK_IN_PAD), lambda j, j_ref: (j, 0, 0)),
                      pl.BlockSpec((m, BLK_IN_PAD), lambda j, j_ref: (0, 0))],
            out_specs=pl.BlockSpec((m, N_IN_PAD), lambda j, j_ref: (0, 0))),
        out_shape=_sds((m, N_IN_PAD), BF16),
        compiler_params=_params("arbitrary"),
    )(jidx, gw, wb)


def _gather_blocks(ob):
    copies = [(lambda ins, outs, pos: ins[0], lambda ins, outs, pos: outs[0].at[_chip(pos)],
               functools.partial(_other_chip, mask=mask)) for mask in CHIP_MASKS]
    return [_sds((4,) + ob.shape, ob.dtype)], copies


def _to_sibling_half(name, arrays):
    def src(ins, outs, pos, a):
        h = arrays[a].shape[-2] // 2
        sl = pl.ds((1 - pos[2]) * h, h)
        return ins[a].at[:, sl] if arrays[a].ndim == 3 else ins[a].at[sl]

    outs = [_sds(a.shape[:-2] + (a.shape[-2] // 2, a.shape[-1]), a.dtype) for a in arrays]
    phase = [(functools.partial(src, a=a), lambda ins, outs, pos, a=a: outs[a], _sibling) for a in range(len(arrays))]
    return _exchange(name, arrays, outs, [phase])


def _add_half(name, full, part, cidx):
    shape = part.shape
    lead = shape[0] if len(shape) == 3 else 1
    rows, cols = shape[-2], shape[-1]
    tr = rows
    nr = rows // tr
    f3 = full.reshape((lead,) + full.shape[-2:])
    p3 = part.reshape((lead, rows, cols))

    def body(c_ref, f_ref, p_ref, o_ref):
        o_ref[...] = (f_ref[...].astype(F32) + p_ref[...].astype(F32)).astype(o_ref.dtype)

    out = _call(
        body, name=name,
        grid_spec=pltpu.PrefetchScalarGridSpec(
            num_scalar_prefetch=1, grid=(lead, nr),
            in_specs=[pl.BlockSpec((1, tr, cols), lambda b, r, c_ref: (b, c_ref[0] * nr + r, 0)),
                      pl.BlockSpec((1, tr, cols), lambda b, r, c_ref: (b, r, 0))],
            out_specs=pl.BlockSpec((1, tr, cols), lambda b, r, c_ref: (b, r, 0))),
        out_shape=_sds((lead, rows, cols), part.dtype),
        compiler_params=_params("arbitrary", "arbitrary"),
    )(cidx, f3, p3)
    return out.reshape(shape)


def _to_other_chips(arrays, blocked):
    def src(ins, outs, pos, a, mask):
        return ins[a].at[_chip(pos) ^ mask] if blocked[a] else ins[a]

    outs = [_sds((3,) + (a.shape[1:] if b else a.shape), a.dtype) for a, b in zip(arrays, blocked)]
    copies = []
    for mi, mask in enumerate(CHIP_MASKS):
        for a in range(len(arrays)):
            copies.append((functools.partial(src, a=a, mask=mask), lambda ins, outs, pos, a=a, mi=mi: outs[a].at[mi],
                           functools.partial(_other_chip, mask=mask)))
    return outs, copies


def _add_chips(name, own, got, jidx, blocked):
    rows, cols = got.shape[-2:]
    tr = rows
    nr = rows // tr
    o3 = own if blocked else own.reshape((1, rows, cols))

    def body(j_ref, o_ref, g_ref, out_ref):
        out_ref[...] = ((o_ref[0].astype(F32) + g_ref[0].astype(F32))
                        + (g_ref[1].astype(F32) + g_ref[2].astype(F32)))

    own_map = (lambda r, j_ref: (j_ref[0], r, 0)) if blocked else (lambda r, j_ref: (0, r, 0))
    return _call(
        body, name=name,
        grid_spec=pltpu.PrefetchScalarGridSpec(
            num_scalar_prefetch=1, grid=(nr,),
            in_specs=[pl.BlockSpec((1, tr, cols), own_map),
                      pl.BlockSpec((3, tr, cols), lambda r, j_ref: (0, r, 0))],
            out_specs=pl.BlockSpec((tr, cols), lambda r, j_ref: (r, 0))),
        out_shape=_sds((rows, cols)),
        compiler_params=_params("arbitrary"),
    )(jidx, o3, got)


def _to_sibling(name, arrays):
    phase = [(lambda ins, outs, pos, a=a: ins[a], lambda ins, outs, pos, a=a: outs[a], _sibling)
             for a in range(len(arrays))]
    return _exchange(name, arrays, [_sds(a.shape, a.dtype) for a in arrays], [phase])


def _local_step(x, target, w_pad, w_out, conv_w, norm_w, pool_w, pool_scale, a_log, dt_bias, dn_norm_w, final_norm_w):
    proj, n_t, qn, kn, vs, beta, g, yq, yk, yv, y_pool = _front_fwd(x, norm_w, w_pad, conv_w, a_log, dt_bias, pool_w, pool_scale)
    w, att, qd, kd, tm, cd, o, vn, st = _delta_fwd(qn, kn, vs, beta, g)
    w_out = w_out(o) if callable(w_out) else w_out
    g_wout, dh, dyp, do, ddz, loss, g_fnw, g_dnw = _out_fwd_bwd(x, y_pool, o, proj, target, w_out, dn_norm_w, final_norm_w)
    dqn, dkn, dvs, dbeta, dg = _delta_bwd(do, vn, qd, kd, w, att, cd, st, qn, kn, vs, beta, g, tm)
    (dcq, dck, dcv, dba, g_cw, g_sm), (dpu, dpz, g_pw, g_ps) = _conv_pool_bwd(
        proj, (yq, yk, yv), conv_w, a_log, dt_bias, dqn, dkn, dvs, dbeta, dg, dyp, pool_w, pool_scale)
    pieces = [dpu, dpz, dcq, dck, dcv, ddz, dba]
    g_win = _grad_w_in(n_t, pieces)
    small = dict(norm_w=jnp.zeros_like(norm_w), pool_w=g_pw, pool_scale=g_ps, conv_w=g_cw[:CONV_K],
                 a_log=g_sm[0:1, 0:N_HEADS], dt_bias=g_sm[0:1, N_HEADS:2 * N_HEADS], dn_norm_w=g_dnw, final_norm_w=g_fnw)
    return loss[0, 0], g_win, g_wout, small, dh, pieces


SMALL_LAYOUT = (("pool_w", 512, HEAD, (1, N_HEADS, HEAD, HEAD)), ("final_norm_w", 8, HEAD, (D_MODEL,)),
                ("pool_scale", 4, HEAD, (1, D_HALF)), ("conv_w", 48, HEAD, (1, CONV_K, 3 * D_HALF)),
                ("dn_norm_w", 1, HEAD, (1, HEAD)), ("a_log", 1, N_HEADS, (1, N_HEADS)), ("dt_bias", 1, N_HEADS, (1, N_HEADS)),
                ("loss", 1, 1, ()))


def _small_offsets():
    offs, r = {}, 0
    for name, rows, _, _ in SMALL_LAYOUT:
        offs[name] = r
        r += -(-rows // 8) * 8
    assert r <= SMALL_ROWS
    return offs


def _pack_small(t):
    parts = []
    for name, rows, lanes, _ in SMALL_LAYOUT:
        a = t.get(name, jnp.zeros((1,), F32)).reshape(rows, lanes)
        parts.append(jnp.pad(a, ((0, -(-rows // 8) * 8 - rows), (0, HEAD - lanes))))
    buf = jnp.concatenate(parts, axis=0)
    return jnp.pad(buf, ((0, SMALL_ROWS - buf.shape[0]), (0, 0)))


def _adamw_small(w, g_own, g_got, cidx, m, v):
    offs = _small_offsets()
    names = [e[0] for e in SMALL_LAYOUT]
    n = len(names)

    def body(c_ref, w_ref, go_ref, gg_ref, m_ref, v_ref, *outs):
        own_low = c_ref[0] == 0
        gv = jnp.concatenate([jnp.where(own_low, go_ref[...], gg_ref[...]), jnp.where(own_low, gg_ref[...], go_ref[...])], axis=0)
        mn = ADAM_B1 * m_ref[...] + (1.0 - ADAM_B1) * gv
        vn = ADAM_B2 * v_ref[...] + (1.0 - ADAM_B2) * (gv * gv)
        m_hat = mn / (1.0 - ADAM_B1 ** ADAM_STEP)
        v_hat = vn / (1.0 - ADAM_B2 ** ADAM_STEP)
        dl = -ADAM_LR * (m_hat / (jnp.sqrt(v_hat) + ADAM_EPS) + ADAM_WD * w_ref[...])
        for kind, arr in enumerate((gv, dl, mn, vn)):
            for i, (name, rows, lanes, _) in enumerate(SMALL_LAYOUT):
                outs[kind * n + i][...] = arr[offs[name]:offs[name] + rows, :lanes]

    whole = lambda shape: pl.BlockSpec(shape, lambda i, c_ref: (0,) * len(shape))
    out_shapes = [_sds((rows, lanes)) for _, rows, lanes, _ in SMALL_LAYOUT] * 4
    res = _call(
        body, name="adamw_small",
        grid_spec=pltpu.PrefetchScalarGridSpec(
            num_scalar_prefetch=1, grid=(1,),
            in_specs=[whole(w.shape), whole(g_own.shape), whole(g_got.shape), whole(m.shape), whole(v.shape)],
            out_specs=[whole(o.shape) for o in out_shapes]),
        out_shape=out_shapes,
        compiler_params=_params("arbitrary"),
    )(cidx, w, g_own, g_got, m, v)
    return [{name: res[kind * n + i].reshape(shape) for i, (name, _, _, shape) in enumerate(SMALL_LAYOUT)}
            for kind in range(4)]


def kernel(x, norm_w, w_in, pool_w, pool_scale, conv_w, a_log, dt_bias, dn_norm_w, w_out, final_norm_w, loss_target, m_norm_w, m_w_in, m_pool_w, m_pool_scale, m_conv_w, m_a_log, m_dt_bias, m_dn_norm_w, m_w_out, m_final_norm_w, v_norm_w, v_w_in, v_pool_w, v_pool_scale, v_conv_w, v_a_log, v_dt_bias, v_dn_norm_w, v_w_out, v_final_norm_w):
    cidx = lax.axis_index("c").astype(I32).reshape(1)
    jidx = (2 * lax.axis_index("x") + lax.axis_index("y")).astype(I32)

    wb = jnp.pad(w_in[0].astype(BF16), ((0, 0), (0, BLK_IN_PAD - BLK_IN)))
    ob = w_out[0].astype(BF16)
    gw, gc = _gather_weights(wb, conv_w[0])
    mine = lambda j: jidx == j
    w_pad = _assemble_w_in(gw, wb, jidx.reshape(1))
    cw_full = jnp.concatenate([jnp.where(mine(j), conv_w[0], gc[j]) for j in range(4)], axis=1)

    lands_o, copies_o = _gather_blocks(ob)
    sems_o, ob_thru, zones_o, token_o = _exchange_start("gather_w_out_start", [ob], lands_o, copies_o)

    def w_out_full(after):
        (own,), (got,) = _exchange_wait("gather_w_out_wait", sems_o, ob_thru, zones_o, copies_o, after)
        return jnp.where((jnp.arange(4) == jidx)[:, None, None], own[None], got).reshape(D_MODEL, D_MODEL)

    loss, g_win, g_wout, small, dh, pieces = _local_step(
        x[0], loss_target[0], w_pad, w_out_full, cw_full, norm_w + token_o[0, 0], pool_w[0], pool_scale, a_log, dt_bias,
        dn_norm_w, final_norm_w.reshape(1, D_MODEL))
    small["loss"] = loss

    blocks_out = g_wout.reshape(4, BLK_OUT, D_MODEL)
    full = [g_win, blocks_out, _pack_small(small)]
    from_sib = _to_sibling_half("reduce_sibling", full)
    chip_sum = [_add_half("add_sibling_%d" % i, f, p, cidx) for i, (f, p) in enumerate(zip(full, from_sib))]
    blocked = [True, True, False]
    lands, copies = _to_other_chips(chip_sum, blocked)
    sems, chip_sum, zones, token = _exchange_start("reduce_chips_start", chip_sum, lands, copies)
    gx, g_nw = _in_bwd(x[0], dh, norm_w + token[0, 0], w_pad, pieces)
    g_nw = _allreduce_tile("reduce_norm_w", g_nw.reshape(8, HEAD)).reshape(1, D_MODEL)
    chip_sum, from_chips = _exchange_wait("reduce_chips_wait", sems, chip_sum, zones, copies, gx)
    halves = [_add_chips("add_chips_%d" % i, o, g, jidx.reshape(1), b)
              for i, (o, g, b) in enumerate(zip(chip_sum, from_chips, blocked))]
    other_halves = _to_sibling("swap_halves", halves)

    weights = dict(norm_w=norm_w, w_in=w_in, pool_w=pool_w, pool_scale=pool_scale, conv_w=conv_w, a_log=a_log,
                   dt_bias=dt_bias, dn_norm_w=dn_norm_w, w_out=w_out, final_norm_w=final_norm_w)
    ms = dict(norm_w=m_norm_w, w_in=m_w_in, pool_w=m_pool_w, pool_scale=m_pool_scale, conv_w=m_conv_w, a_log=m_a_log,
              dt_bias=m_dt_bias, dn_norm_w=m_dn_norm_w, w_out=m_w_out, final_norm_w=m_final_norm_w)
    vs = dict(norm_w=v_norm_w, w_in=v_w_in, pool_w=v_pool_w, pool_scale=v_pool_scale, conv_w=v_conv_w, a_log=v_a_log,
              dt_bias=v_dt_bias, dn_norm_w=v_dn_norm_w, w_out=v_w_out, final_norm_w=v_final_norm_w)
    names = ["norm_w", "w_in", "pool_w", "pool_scale", "conv_w", "a_log", "dt_bias", "dn_norm_w", "w_out", "final_norm_w"]
    small_names = [n for n in names if n not in ("w_in", "w_out")]

    def pack(t):
        conv = lax.dynamic_update_slice_in_dim(jnp.zeros((CONV_K, 3 * D_HALF), F32), t["conv_w"][0], jidx * BLK_CONV, axis=1)
        return _pack_small({**{n: t[n] for n in small_names if n != "conv_w"}, "conv_w": conv})

    results = [{}, {}, {}, {}]
    to_tiles = lambda a: jnp.transpose(a, (2, 0, 1)).reshape(BLK_IN, 8, HEAD)
    from_tiles = lambda a: jnp.transpose(a, (1, 2, 0)).reshape(1, D_MODEL, BLK_IN)
    lo = jnp.where(cidx[0] == 0, halves[0], other_halves[0])
    hi = jnp.where(cidx[0] == 0, other_halves[0], halves[0])
    g_tiles = jnp.concatenate([lo[:, :BLK_IN].T, hi[:, :BLK_IN].T], axis=1).reshape(BLK_IN, 8, HEAD)
    outs = _adamw_tiles("adamw_w_in", to_tiles(w_in), g_tiles, to_tiles(m_w_in), to_tiles(v_w_in))
    for res, o in zip(results, (g_tiles,) + tuple(outs)):
        res["w_in"] = from_tiles(o)
    outs = _adamw_shard("adamw_w_out", w_out, halves[1], other_halves[1], cidx, m_w_out, v_w_out)
    for res, o in zip(results, outs):
        res["w_out"] = o
    outs = _adamw_small(pack(weights), halves[2], other_halves[2], cidx, pack(ms), pack(vs))
    for res, got in zip(results, outs):
        got["conv_w"] = lax.dynamic_slice_in_dim(got["conv_w"], jidx * BLK_CONV, BLK_CONV, axis=2)
        res.update(got)
    one_tile = lambda a: a.reshape(1, 8, HEAD)
    outs = _adamw_tiles("adamw_norm_w", one_tile(norm_w), one_tile(g_nw), one_tile(m_norm_w), one_tile(v_norm_w))
    for res, o in zip(results, (g_nw,) + tuple(outs)):
        res["norm_w"] = o.reshape(1, D_MODEL)
    grads, delta, new_m, new_v = results

    return (grads["loss"], gx[None], *[grads[n] for n in names], *[delta[n] for n in names],
            *[new_m[n] for n in names], *[new_v[n] for n in names])
```

```python
import functools

import jax
import jax.numpy as jnp
import numpy as np
from jax import lax
from jax.experimental import pallas as pl
from jax.experimental.pallas import tpu as pltpu

F32 = jnp.float32
BF16 = jnp.bfloat16
I32 = jnp.int32

D_MODEL = 1024
D_HALF = 512
N_HEADS = 4
HEAD = 128
CHUNK = 64
PAIR = 2 * CHUNK
WINDOWS = (2, 4, 8, 16)
CONV_K = 4
EPS = 1e-6
N_IN = 3080
N_IN_PAD = 3200
BLK_IN = 770
BLK_IN_PAD = 896
BLK_OUT = 256
BLK_CONV = 384
COL_BA = 3072
QK_SCALE = HEAD ** -0.5
SMALL_ROWS = 608
VMEM_LIMIT = 56 * 1024 * 1024

ADAM_LR = 0.001
ADAM_B1 = 0.9
ADAM_B2 = 0.999
ADAM_EPS = 1e-08
ADAM_WD = 0.01
ADAM_STEP = 10

CHIP_MASKS = (2, 1, 3)
HEADS = range(N_HEADS)
HEAD_COLS = [slice(h * HEAD, (h + 1) * HEAD) for h in HEADS]


def _call(body, **kw):
    return pl.pallas_call(body, **kw)


def _params(*sem):
    return pltpu.CompilerParams(dimension_semantics=sem, vmem_limit_bytes=VMEM_LIMIT)


def _sds(shape, dtype=F32):
    return jax.ShapeDtypeStruct(shape, dtype)


def _bdot(a, b):
    return jnp.dot(a.astype(BF16), b.astype(BF16), preferred_element_type=F32)


def _bdot_nt(a, b):
    return lax.dot_general(a.astype(BF16), b.astype(BF16), (((1,), (1,)), ((), ())), preferred_element_type=F32)


def _bdot_tn(a, b):
    return lax.dot_general(a.astype(BF16), b.astype(BF16), (((0,), (0,)), ((), ())), preferred_element_type=F32)


def _side(a, b):
    return jnp.concatenate([a.astype(BF16), b.astype(BF16)], axis=1)


def _stack(a, b):
    return jnp.concatenate([a.astype(BF16), b.astype(BF16)], axis=0)


def _split(a):
    hi = a.astype(BF16)
    lo = (a - hi.astype(F32)).astype(BF16)
    return hi, lo


def _mask_dot(m, b):
    n = b.shape[1]
    both = jnp.dot(m, jnp.concatenate(_split(b), axis=1), preferred_element_type=F32)
    return both[:, :n] + both[:, n:]


def _sigmoid(x):
    return 0.5 * jnp.tanh(0.5 * x) + 0.5


def _softplus(x):
    return jnp.maximum(x, 0.0) + jnp.log(1.0 + jnp.exp(-jnp.abs(x)))


def _rowsum(x):
    return jnp.sum(x, axis=-1, keepdims=True)


def _colsum(x):
    return jnp.sum(x, axis=0, keepdims=True)


def _shift_down(xv, prev8, k):
    r = pltpu.roll(xv, k, 0)
    q = pltpu.roll(prev8, k, 0)
    row = lax.broadcasted_iota(I32, prev8.shape, 0)
    top = jnp.where(row < k, q, r[0:8])
    return jnp.concatenate([top, r[8:]], axis=0)


def _shift_up(xv, next8, k):
    t = xv.shape[0]
    r = pltpu.roll(xv, t - k, 0)
    q = pltpu.roll(next8, 8 - k, 0)
    row = lax.broadcasted_iota(I32, next8.shape, 0)
    bot = jnp.where(row >= 8 - k, q, r[t - 8:])
    return jnp.concatenate([r[:t - 8], bot], axis=0)


INTRA_PAIRS = 2
UNITS = [(pp, h) for pp in range(INTRA_PAIRS) for h in HEADS]


def _heads_of(ref, rows=PAIR, units=UNITS):
    return [ref[pp * rows:(pp + 1) * rows, HEAD_COLS[h]] for pp, h in units]


def _put_heads_of(ref, vals, rows=PAIR, units=UNITS):
    for (pp, h), v in zip(units, vals):
        ref[pp * rows:(pp + 1) * rows, HEAD_COLS[h]] = v.astype(ref.dtype)


_heads = _heads_of
_put_heads = _put_heads_of


def _each(fn, *lists):
    return [fn(*args) for args in zip(*lists)]


def _pool_bands(t, anti=False):
    r = np.arange(t)[:, None]
    c = np.arange(t + HEAD)[None, :]
    d = (c - r) if anti else (r - c + HEAD)
    return jnp.asarray(np.stack([(d >= 0) & (d < w) for w in WINDOWS]), BF16)


def _pool_mix(u, halo, z, pw, bands, row0):
    t = u[0].shape[0]
    rows = row0 + lax.broadcasted_iota(I32, (t, 1), 0) + 1
    cnt = [jnp.minimum(rows, w).astype(F32) for w in WINDOWS]
    win = _each(lambda b, h, v: _mask_dot(b, jnp.concatenate([h, v], axis=0)), bands, halo, u)
    mix = _each(lambda a, c, v: a / c - v, win, cnt, u)
    mixed = _each(_bdot, mix, pw)
    return mix, mixed, _each(_sigmoid, z), cnt


POOL_T = 256


def _conv_taps(xv, prev8):
    return [_shift_down(xv, prev8, CONV_K - 1 - j) for j in range(CONV_K - 1)] + [xv]


def _conv_pre(taps, cw):
    y = taps[CONV_K - 1] * cw[CONV_K - 1:CONV_K]
    for j in range(CONV_K - 2, -1, -1):
        y = y + taps[j] * cw[j:j + 1]
    return y


CONV_T = 256


def _conv_specs(t, tile_of=lambda i: i):
    tiles = [pl.BlockSpec((t, D_HALF), functools.partial(lambda i, p: (tile_of(i), 2 + p), p=p)) for p in range(3)]
    halos = [pl.BlockSpec((8, D_HALF),
                          functools.partial(lambda i, p: (jnp.maximum(tile_of(i) * (t // 8) - 1, 0), 2 + p), p=p))
             for p in range(3)]
    return tiles + halos


def _pair_masks():
    r = lax.broadcasted_iota(I32, (PAIR, PAIR), 0)
    c = lax.broadcasted_iota(I32, (PAIR, PAIR), 1)
    same = jnp.right_shift(r, 6) == jnp.right_shift(c, 6)
    return same, same & (r >= c), same & (r > c), r == c


def _interleave(*stage_lists):
    live = list(stage_lists)
    while live:
        for gen in list(live):
            try:
                next(gen)
            except StopIteration:
                live.remove(gen)


def _pair_common_stages(cm, qn, kn, vs, beta, g):
    same, incl, strict, eye = _pair_masks()
    incl_b = incl.astype(BF16)
    first = lax.broadcasted_iota(I32, (PAIR, HEAD), 0) < CHUNK
    cm.update(same=same, incl=incl, strict=strict, eye=eye)
    gc = _each(lambda gv: _mask_dot(incl_b, gv), g)
    q = _each(lambda v: v * QK_SCALE, qn)
    kb = _each(lambda k, b: k * b, kn, beta)
    cm.update(gc=gc, q=q, kb=kb, vb=_each(lambda v, b: v * b, vs, beta))
    yield
    both = _each(lambda a, b, c: _bdot_nt(_stack(a, b), c), kb, q, kn)
    cm.update(kk=[v[:PAIR] for v in both], qk=[v[PAIR:] for v in both])
    gc_row = _each(lambda v: _colsum(jnp.where(eye, v, 0.0)), gc)
    gl = _each(lambda v: jnp.where(first, v[CHUNK - 1:CHUNK], v[PAIR - 1:PAIR]), gc)
    egc = _each(jnp.exp, gc)
    cm.update(gl=gl, egc=egc,
              decay=_each(lambda v, r: jnp.where(incl, jnp.exp(jnp.where(incl, v - r, 0.0)), 0.0), gc, gc_row))
    yield
    cm.update(ekd=_each(lambda a, b: jnp.exp(a - b), gl, gc), cd=_each(jnp.exp, gl),
              kbg=_each(lambda k, e: k * e, kb, egc))
    yield


def _tri_inv_stages(out, a, eye_f):
    p = _each(lambda v: eye_f - v, a)
    x = _each(_bdot, a, a)
    yield
    for it in range(4):
        both = _each(lambda xv, pv: _bdot(xv, _side(pv, xv)), x, p)
        p = _each(lambda pv, b: pv + b[:, :PAIR], p, both)
        x = [b[:, PAIR:] for b in both]
        yield
    out["t"] = _each(lambda pv, xv: pv + _bdot(pv, xv), p, x)
    yield


def _chunk_scalar_spec(pairs=1, index=lambda i: (i, 0)):
    return pl.BlockSpec((16 * pairs, D_HALF), index)


SCAN_PAIRS = 2
SCAN_ROWS = SCAN_PAIRS * PAIR


def _delta_fwd(qn, kn, vs, beta, g):
    s = qn.shape[0]
    n_steps = s // SCAN_ROWS
    n_chunks = s // CHUNK
    assert INTRA_PAIRS == SCAN_PAIRS

    def body(qn_ref, kn_ref, vs_ref, beta_ref, g_ref, w_ref, att_ref, qd_ref, kd_ref, t_ref, cd_ref, o_ref, vn_ref, st_ref,
             state, u_s, w_s, att_s, qd_s, kd_s, cd_s):
        t = pl.program_id(0)

        @pl.when(t <= 1)
        def _():
            state[...] = jnp.zeros_like(state)

        @pl.when(t == 0)
        def _():
            for ref in (u_s, w_s, att_s, qd_s, kd_s, cd_s):
                ref[1] = jnp.zeros(ref.shape[1:], ref.dtype)

        cur = lax.rem(t, 2)
        prev = 1 - cur
        cols = list(enumerate(HEAD_COLS))

        def recurrence():
            sm = [state[h] for h in HEADS]
            for ci in range(2 * SCAN_PAIRS):
                rs = slice(ci * CHUNK, (ci + 1) * CHUNK)
                for h in HEADS:
                    st_ref[ci, h] = sm[h]
                both = [_bdot(jnp.concatenate([w_s[prev, rs, sl], qd_s[prev, rs, sl]], axis=0), sm[h]) for h, sl in cols]
                vn = [u_s[prev, rs, sl] - both[h][:CHUNK] for h, sl in cols]
                for h, sl in cols:
                    vn_ref[rs, sl] = vn[h].astype(BF16)
                    o_ref[rs, sl] = both[h][CHUNK:]
                yield
                sm = [sm[h] * cd_s[prev, ci * 8:ci * 8 + 1, sl] + _bdot_tn(kd_s[prev, rs, sl], vn[h]) for h, sl in cols]
                yield
            for h in HEADS:
                state[h] = sm[h]
            for pp in range(SCAN_PAIRS):
                rp = slice(pp * PAIR, (pp + 1) * PAIR)
                intra = [_bdot(att_s[prev, rp, sl], vn_ref[rp, sl]) for sl in HEAD_COLS]
                for h, sl in cols:
                    o_ref[rp, sl] += intra[h]
                yield

        def factors():
            kn = _heads(kn_ref)
            cm = {}
            yield from _pair_common_stages(cm, _heads(qn_ref), kn, _heads(vs_ref), _heads(beta_ref), _heads(g_ref))
            a = _each(lambda kk, d: jnp.where(cm["strict"], kk * d, 0.0), cm["kk"], cm["decay"])
            inv = {}
            yield from _tri_inv_stages(inv, a, cm["eye"].astype(F32))
            tm = inv["t"]
            uw = _each(lambda tv, a, b: _bdot(tv, _side(a, b)), tm, cm["vb"], cm["kbg"])
            res = dict(u=[v[:, :HEAD] for v in uw], w=[v[:, HEAD:] for v in uw],
                       att=_each(lambda a, b: a * b, cm["qk"], cm["decay"]),
                       qd=_each(lambda a, b: a * b, cm["q"], cm["egc"]), kd=_each(lambda a, b: a * b, kn, cm["ekd"]))
            yield
            _put_heads(t_ref, tm)
            for key, out, keep in (("w", w_ref, w_s), ("att", att_ref, att_s), ("qd", qd_ref, qd_s), ("kd", kd_ref, kd_s)):
                _put_heads(out, res[key])
                for (pp, h), v in zip(UNITS, res[key]):
                    keep[cur, pp * PAIR:(pp + 1) * PAIR, HEAD_COLS[h]] = v.astype(BF16)
            for (pp, h), v in zip(UNITS, res["u"]):
                u_s[cur, pp * PAIR:(pp + 1) * PAIR, HEAD_COLS[h]] = v
            for ci in range(2):
                for (pp, h), v in zip(UNITS, cm["cd"]):
                    rows8 = slice(pp * 16 + ci * 8, pp * 16 + (ci + 1) * 8)
                    cd_ref[rows8, HEAD_COLS[h]] = v[ci * CHUNK:ci * CHUNK + 8]
                    cd_s[cur, rows8, HEAD_COLS[h]] = v[ci * CHUNK:ci * CHUNK + 8]
            yield

        _interleave(recurrence(), factors())

    last = n_steps - 1
    now = lambda i: (jnp.minimum(i, last), 0)
    before = lambda i: (jnp.maximum(i - 1, 0), 0)
    rows = lambda index: pl.BlockSpec((SCAN_ROWS, D_HALF), index)
    slot = lambda r, dtype: pltpu.VMEM((2, r, D_HALF), dtype)
    return _call(
        body, name="delta_fwd", grid=(n_steps + 1,),
        in_specs=[rows(now)] * 5,
        out_specs=[rows(now)] * 5 + [_chunk_scalar_spec(SCAN_PAIRS, now), rows(before), rows(before),
                                     pl.BlockSpec((2 * SCAN_PAIRS, N_HEADS, HEAD, HEAD), lambda i: (jnp.maximum(i - 1, 0), 0, 0, 0))],
        out_shape=[_sds((s, D_HALF), BF16)] * 5 + [_sds((s // 8, D_HALF)), _sds((s, D_HALF)), _sds((s, D_HALF), BF16),
                                                  _sds((n_chunks, N_HEADS, HEAD, HEAD))],
        scratch_shapes=[pltpu.VMEM((N_HEADS, HEAD, HEAD), F32), slot(SCAN_ROWS, F32), slot(SCAN_ROWS, BF16),
                        slot(SCAN_ROWS, BF16), slot(SCAN_ROWS, BF16), slot(SCAN_ROWS, BF16), slot(16 * SCAN_PAIRS, F32)],
        compiler_params=_params("arbitrary"),
    )(qn, kn, vs, beta, g)


OUT_T = 512
OUT_ROWS = 256


def _out_fwd_bwd(x, y_pool, o, proj, target, w_out, dn_norm_w, final_norm_w):
    s = x.shape[0]
    t = OUT_T

    def body(x_ref, yp_ref, o_ref, z_ref, tg_ref, wo_ref, dnw_ref, fnw_ref,
             gwo_ref, dh_ref, dyp_ref, do_ref, dz_ref, loss_ref, gfn_ref, gdn_ref, y_ref, yt_ref, gwo_acc):
        @pl.when(pl.program_id(0) == 0)
        def _():
            loss_ref[...] = jnp.zeros_like(loss_ref)
            gfn_ref[...] = jnp.zeros_like(gfn_ref)
            gdn_ref[...] = jnp.zeros_like(gdn_ref)
            gwo_acc[...] = jnp.zeros_like(gwo_acc)

        dnw = dnw_ref[...]
        fnw = fnw_ref[...]

        def stages(rows, lead):
            for _ in range(lead):
                yield
            ypv = yp_ref[rows]
            y_ref[rows, :D_HALF] = ypv.astype(BF16)
            yt_ref[:D_HALF, rows] = ypv.T.astype(BF16)
            keep = []
            for h in HEADS:
                ov = o_ref[rows, HEAD_COLS[h]]
                zv = z_ref[rows, HEAD_COLS[h]]
                ro = lax.rsqrt(jnp.mean(ov * ov, axis=-1, keepdims=True) + EPS)
                ohat = ov * ro
                sg = _sigmoid(zv)
                keep.append((ro, ohat, zv, sg))
                ydn = ohat * dnw * (zv * sg)
                y_ref[rows, D_HALF + h * HEAD:D_HALF + (h + 1) * HEAD] = ydn.astype(BF16)
                yt_ref[D_HALF + h * HEAD:D_HALF + (h + 1) * HEAD, rows] = ydn.T.astype(BF16)
            yield
            hv = x_ref[rows] + jnp.dot(y_ref[rows], wo_ref[...], preferred_element_type=F32)
            yield
            r2 = lax.rsqrt(jnp.mean(hv * hv, axis=-1, keepdims=True) + EPS)
            hhat = hv * r2
            err = hhat * fnw - tg_ref[rows]
            loss_ref[...] += 0.5 * jnp.sum(_rowsum(err * err) * (1.0 / D_MODEL), axis=0, keepdims=True)
            dout = err * (1.0 / D_MODEL)
            gfn_ref[...] += _colsum(dout * hhat)
            dhh = dout * fnw
            dh = r2 * (dhh - hhat * jnp.mean(dhh * hhat, axis=-1, keepdims=True))
            dh_ref[rows] = dh
            yield
            if lead == t // OUT_ROWS - 1:
                gwo_acc[...] += _bdot(yt_ref[...], dh_ref[...])
            dy = _bdot_nt(dh, wo_ref[...])
            yield
            dyp_ref[rows] = dy[:, :D_HALF]
            gdn = jnp.zeros((1, HEAD), F32)
            for h in HEADS:
                ro, ohat, zv, sg = keep[h]
                dyd = dy[:, D_HALF + h * HEAD:D_HALF + (h + 1) * HEAD]
                sz = zv * sg
                dz_ref[rows, HEAD_COLS[h]] = (dyd * ohat * dnw * (sg * (1.0 + zv * (1.0 - sg)))).astype(BF16)
                gdn = gdn + _colsum(dyd * ohat * sz)
                doh = dyd * dnw * sz
                do_ref[rows, HEAD_COLS[h]] = ro * (doh - ohat * jnp.mean(doh * ohat, axis=-1, keepdims=True))
            gdn_ref[...] += gdn
            yield

        _interleave(*[stages(slice(k * OUT_ROWS, (k + 1) * OUT_ROWS), k) for k in range(t // OUT_ROWS)])

        @pl.when(pl.program_id(0) == pl.num_programs(0) - 1)
        def _():
            gwo_ref[...] = gwo_acc[...].astype(BF16)

    wide = pl.BlockSpec((t, D_MODEL), lambda i: (i, 0))
    half = pl.BlockSpec((t, D_HALF), lambda i: (i, 0))
    const = lambda shape: pl.BlockSpec(shape, lambda i: (0,) * len(shape))
    return _call(
        body, name="out_fwd_bwd", grid=(s // t,),
        in_specs=[wide, half, half, pl.BlockSpec((t, D_HALF), lambda i: (i, 5)), wide,
                  const((D_MODEL, D_MODEL)), const((1, HEAD)), const((1, D_MODEL))],
        out_specs=[const((D_MODEL, D_MODEL)), wide, half, half, half,
                   const((1, HEAD)), const((1, D_MODEL)), const((1, HEAD))],
        out_shape=[_sds((D_MODEL, D_MODEL), BF16), _sds((s, D_MODEL)), _sds((s, D_HALF)), _sds((s, D_HALF)),
                   _sds((s, D_HALF), BF16), _sds((1, HEAD)), _sds((1, D_MODEL)), _sds((1, HEAD))],
        scratch_shapes=[pltpu.VMEM((t, D_MODEL), BF16), pltpu.VMEM((D_MODEL, t), BF16), pltpu.VMEM((D_MODEL, D_MODEL), F32)],
        compiler_params=_params("arbitrary"),
    )(x, y_pool, o, proj, target, w_out, dn_norm_w, final_norm_w)


def _grad_w_in(at, pieces):
    m, s = at.shape
    n = len(pieces)
    tn, tk = D_HALF, min(s, 1024)

    def body(a_ref, *refs):
        p_refs, o_ref, acc = refs[:n], refs[n], refs[n + 1]

        @pl.when(pl.program_id(0) == 0)
        def _():
            acc[...] = jnp.zeros_like(acc)

        av = a_ref[...]
        for p in range(n):
            acc[:, p * tn:(p + 1) * tn] += _bdot(av, p_refs[p][...])

        @pl.when(pl.program_id(0) == pl.num_programs(0) - 1)
        def _():
            for j in range(4):
                base = j * BLK_IN // HEAD * HEAD
                win = acc[:, base:base + BLK_IN_PAD]
                if j * BLK_IN > base:
                    win = pltpu.roll(win, BLK_IN_PAD - (j * BLK_IN - base), 1)
                o_ref[j] = win.astype(BF16)

    return _call(
        body, name="grad_w_in", grid=(s // tk,),
        in_specs=[pl.BlockSpec((m, tk), lambda k: (0, k))] + [pl.BlockSpec((tk, tn), lambda k: (k, 0))] * n,
        out_specs=pl.BlockSpec((4, m, BLK_IN_PAD), lambda k: (0, 0, 0)),
        out_shape=_sds((4, m, BLK_IN_PAD), BF16),
        scratch_shapes=[pltpu.VMEM((m, n * tn), F32)],
        compiler_params=_params("arbitrary"),
    )(at, *pieces)


def _delta_bwd(do, vn, qd, kd, w, att, cd, st, qn, kn, vs, beta, g, tm):
    s = do.shape[0]
    n_steps = s // SCAN_ROWS
    assert INTRA_PAIRS == SCAN_PAIRS

    def body(do_ref, vn_ref, qd_ref, kd_ref, w_ref, att_ref, cd_ref, st_ref, qn_ref, kn_ref, vs_ref, beta_ref, g_ref, t_ref,
             dqn_ref, dkn_ref, dvs_ref, dbeta_ref, dg_ref, dstate, du_s, dw_s, datt_s, dqd_s, dkd_s, dcd_s):
        t = pl.program_id(0)

        @pl.when(t == 0)
        def _():
            dstate[...] = jnp.zeros_like(dstate)
            for ref in (du_s, dw_s, datt_s, dqd_s, dkd_s, dcd_s):
                ref[1] = jnp.zeros(ref.shape[1:], ref.dtype)

        cur = lax.rem(t, 2)
        prev = 1 - cur
        cols = list(enumerate(HEAD_COLS))
        _, incl, _, _ = _pair_masks()

        def recurrence():
            dv_intra = []
            for pp in range(SCAN_PAIRS):
                rp = slice(pp * PAIR, (pp + 1) * PAIR)
                dv_intra.append([_bdot_tn(att_ref[rp, sl], do_ref[rp, sl]) for _, sl in cols])
                for _, sl in cols:
                    datt_s[cur, rp, sl] = jnp.where(incl, _bdot_nt(do_ref[rp, sl], vn_ref[rp, sl]), 0.0)
                yield
            ds = [dstate[h] for h in HEADS]
            for ci in range(2 * SCAN_PAIRS - 1, -1, -1):
                rs = slice(ci * CHUNK, (ci + 1) * CHUNK)
                in_pair = slice((ci % 2) * CHUNK, (ci % 2 + 1) * CHUNK)
                sm = [st_ref[ci, h] for h in HEADS]
                dvn = [dv_intra[ci // 2][h][in_pair] + _bdot(kd_ref[rs, sl], ds[h]) for h, sl in cols]
                dkd = [_bdot_nt(vn_ref[rs, sl], ds[h]) for h, sl in cols]
                dcd = [jnp.broadcast_to(_rowsum(_colsum(ds[h] * sm[h])), (8, HEAD)) for h in HEADS]
                yield
                both = [_bdot_nt(_stack(do_ref[rs, sl], dvn[h]), sm[h]) for h, sl in cols]
                for h, sl in cols:
                    du_s[cur, rs, sl] = dvn[h].astype(BF16)
                    dqd_s[cur, rs, sl] = both[h][:CHUNK]
                    dw_s[cur, rs, sl] = (-both[h][CHUNK:]).astype(BF16)
                    dkd_s[cur, rs, sl] = dkd[h]
                    dcd_s[cur, ci * 8:(ci + 1) * 8, sl] = dcd[h]
                ds = [ds[h] * cd_ref[ci * 8:ci * 8 + 1, sl]
                      + _bdot_tn(_stack(qd_ref[rs, sl], w_ref[rs, sl]), _stack(do_ref[rs, sl], -dvn[h])) for h, sl in cols]
                yield
            for h in HEADS:
                dstate[h] = ds[h]

        def factors(units):
            _heads = functools.partial(_heads_of, units=units)
            _put_heads = functools.partial(_put_heads_of, units=units)
            ones = jnp.ones((2 * PAIR, HEAD), BF16)
            tn = (((0,), (0,)), ((), ()))
            kept = lambda ref, rows=PAIR: [ref[prev, pp * rows:(pp + 1) * rows, HEAD_COLS[h]] for pp, h in units]
            kn, vs, beta = _heads(kn_ref), _heads(vs_ref), _heads(beta_ref)
            cm = {}
            yield from _pair_common_stages(cm, _heads(qn_ref), kn, vs, beta, _heads(g_ref))
            tmv = _heads(t_ref)
            duv, dwv, dattv, dqdv, dkdv = kept(du_s), kept(dw_s), kept(datt_s), kept(dqd_s), kept(dkd_s)
            duw = _each(_side, duv, dwv)
            both = _each(_bdot_tn, tmv, duw)
            dvb, dkbg = [v[:, :HEAD] for v in both], [v[:, HEAD:] for v in both]
            dt = _each(lambda a, b, c: _bdot_nt(a, _side(b, c)), duw, cm["vb"], cm["kbg"])
            yield
            m1 = _each(_bdot_tn, tmv, dt)
            yield
            da = _each(lambda a, b: -jnp.where(cm["strict"], _bdot_nt(a, b), 0.0), m1, tmv)
            yield
            dkk = _each(lambda a, b: a * b, da, cm["decay"])
            dqk = _each(lambda a, b: a * b, dattv, cm["decay"])
            dd = _each(lambda a, b, c, d: a * b + c * d, dkk, cm["kk"], dqk, cm["qk"])
            dkq = _each(_stack, dkk, dqk)
            both = _each(_bdot, dkq, kn)
            dkb = _each(lambda a, c, d: a[:PAIR] + c * d, both, dkbg, cm["egc"])
            dq = _each(lambda a, c, d: a[PAIR:] + c * d, both, dqdv, cm["egc"])
            yield
            dkn = _each(lambda a, b, c: _bdot_tn(a, _stack(b, c)), dkq, cm["kb"], cm["q"])
            dkn = _each(lambda a, b, c, d, e: a + b * c + d * e, dkn, dkdv, cm["ekd"], dkb, beta)
            t_kd = _each(lambda a, b, c: _rowsum(a * b * c), dkdv, kn, cm["ekd"])
            yield
            split = _each(_split, dd)
            rows_dd = [jnp.dot(_side(hi, lo), ones, preferred_element_type=F32) for hi, lo in split]
            cols_dd = [lax.dot_general(_stack(hi, lo), ones, tn, preferred_element_type=F32) for hi, lo in split]
            yield
            dgc = _each(lambda r, c, a, b, e, f, k, tk: r - c + _rowsum(a * b * e) + _rowsum(f * k) - tk,
                        rows_dd, cols_dd, dqdv, cm["q"], cm["egc"], dkbg, cm["kbg"], t_kd)
            same_b = cm["same"].astype(BF16)
            rowi = lax.broadcasted_iota(I32, (PAIR, HEAD), 0)
            dcd = _each(lambda d: jnp.where(rowi < CHUNK, d[0:1], d[8:9]), kept(dcd_s, rows=16))
            dgl = _each(lambda tk, d, c: _mask_dot(same_b, jnp.broadcast_to(tk, (PAIR, HEAD))) + d * c, t_kd, dcd, cm["cd"])
            yield
            is_last = jnp.bitwise_and(rowi, CHUNK - 1) == CHUNK - 1
            dgc = _each(lambda a, b: a + jnp.where(is_last, b, 0.0), dgc, dgl)
            r = lax.broadcasted_iota(I32, (PAIR, PAIR), 0)
            c = lax.broadcasted_iota(I32, (PAIR, PAIR), 1)
            upper_b = (cm["same"] & (r <= c)).astype(BF16)
            _put_heads(dg_ref, _each(lambda v: _mask_dot(upper_b, v), dgc))
            yield
            _put_heads(dbeta_ref, _each(lambda a, b, c, d: jnp.broadcast_to(_rowsum(a * b) + _rowsum(c * d), (PAIR, HEAD)),
                                        dkb, kn, dvb, vs))
            _put_heads(dqn_ref, _each(lambda v: v * QK_SCALE, dq))
            _put_heads(dkn_ref, dkn)
            _put_heads(dvs_ref, _each(lambda a, b: a * b, dvb, beta))
            yield

        _interleave(recurrence(), *[factors(UNITS[pp * N_HEADS:(pp + 1) * N_HEADS]) for pp in range(INTRA_PAIRS)])

    last = n_steps - 1
    now = lambda i: (jnp.maximum(last - i, 0), 0)
    after = lambda i: (jnp.minimum(n_steps - i, last), 0)
    rows = lambda index: pl.BlockSpec((SCAN_ROWS, D_HALF), index)
    slot = lambda r, dtype: pltpu.VMEM((2, r, D_HALF), dtype)
    return _call(
        body, name="delta_bwd", grid=(n_steps + 1,),
        in_specs=[rows(now)] * 6 + [_chunk_scalar_spec(SCAN_PAIRS, now),
                                    pl.BlockSpec((2 * SCAN_PAIRS, N_HEADS, HEAD, HEAD), lambda i: (jnp.maximum(last - i, 0), 0, 0, 0))]
                 + [rows(after)] * 6,
        out_specs=[rows(after)] * 5,
        out_shape=[_sds((s, D_HALF))] * 5,
        scratch_shapes=[pltpu.VMEM((N_HEADS, HEAD, HEAD), F32), slot(SCAN_ROWS, BF16), slot(SCAN_ROWS, BF16),
                        slot(SCAN_ROWS, F32), slot(SCAN_ROWS, F32), slot(SCAN_ROWS, F32), slot(16 * SCAN_PAIRS, F32)],
        compiler_params=_params("arbitrary"),
    )(do, vn, qd, kd, w, att, cd, st, qn, kn, vs, beta, g, tm)


def _fused_call(name, n_steps, parts):
    n_in = [len(p["inputs"]) for p in parts]
    n_out = [len(p["out_shape"]) for p in parts]
    n_scr = [len(p["scratch"]) for p in parts]

    def body(*refs):
        ins, outs, scr = refs[:sum(n_in)], refs[sum(n_in):sum(n_in) + sum(n_out)], refs[sum(n_in) + sum(n_out):]
        gens, a, b, c = [], 0, 0, 0
        for p, ni, no, ns in zip(parts, n_in, n_out, n_scr):
            gens.append(p["stages"](ins[a:a + ni], outs[b:b + no], scr[c:c + ns]))
            a, b, c = a + ni, b + no, c + ns
        _interleave(*gens)

    flat = lambda key: [v for p in parts for v in p[key]]
    res = _call(
        body, name=name, grid=(n_steps,),
        in_specs=flat("in_specs"), out_specs=flat("out_specs"), out_shape=flat("out_shape"),
        scratch_shapes=flat("scratch"),
        compiler_params=_params("arbitrary"),
    )(*flat("inputs"))
    out, b = [], 0
    for no in n_out:
        out.append(res[b:b + no])
        b += no
    return out


def _pool_bwd_part(proj, dyp, pool_w, pool_scale, tile_of, n_tiles):
    s = proj.shape[0]
    t = POOL_T
    hb = t // HEAD
    last = s // HEAD - 1

    def stages(ins, outs, scratch):
        u_ref, z_ref, halo_ref, dy_ref, zn_ref, dyn_ref, pw_ref, ps_ref, band_ref, aband_ref = ins
        du_ref, dz_ref, gpw_ref, gps_ref = outs
        tile = tile_of(pl.program_id(0))

        @pl.when(pl.program_id(0) == 0)
        def _():
            gpw_ref[...] = jnp.zeros_like(gpw_ref)
            gps_ref[...] = jnp.zeros_like(gps_ref)

        live = (tile > 0).astype(F32)
        more = (tile < n_tiles - 1).astype(F32)
        groups = lambda ref: [ref[:, sl] for sl in HEAD_COLS]
        z, ps, dy = groups(z_ref), groups(ps_ref), groups(dy_ref)
        pw = [pw_ref[g] for g in HEADS]
        mix, mixed, sg, cnt = _pool_mix(groups(u_ref), [h * live for h in groups(halo_ref)], z, pw,
                                        [band_ref[g] for g in HEADS], tile * t)
        yield
        sz = _each(lambda a, b: a * b, z, sg)
        for sl, d, m, p, s_, zg in zip(HEAD_COLS, dy, mixed, ps, sg, z):
            dz_ref[:, sl] = (d * m * p * (s_ * (1.0 + zg * (1.0 - s_)))).astype(BF16)
        for sl, d, m, a in zip(HEAD_COLS, dy, mixed, sz):
            gps_ref[:, sl] += _colsum(d * m * a)
        dmixed = _each(lambda d, p, a: d * p * a, dy, ps, sz)
        yield
        for g, gp in enumerate(_each(_bdot_tn, mix, dmixed)):
            gpw_ref[g] += gp
        dmix = _each(_bdot_nt, dmixed, pw)
        yield
        dmix_n = _each(lambda d, p, zn, w_: _bdot_nt(d * more * p * (zn * _sigmoid(zn)), w_),
                       groups(dyn_ref), ps, groups(zn_ref), pw)
        yield
        scaled = [jnp.concatenate([a / c, b * (1.0 / w)], axis=0) for a, c, b, w in zip(dmix, cnt, dmix_n, WINDOWS)]
        du = _each(lambda b, s_, d: _mask_dot(b, s_) - d, [aband_ref[g] for g in HEADS], scaled, dmix)
        for sl, v in zip(HEAD_COLS, du):
            du_ref[:, sl] = v.astype(BF16)
        yield

    tile = lambda col: pl.BlockSpec((t, D_HALF), lambda i: (tile_of(i), col))
    below = lambda col: pl.BlockSpec((HEAD, D_HALF), lambda i: (jnp.minimum((tile_of(i) + 1) * hb, last), col))
    const3 = lambda shape: pl.BlockSpec(shape, lambda i: (0, 0, 0))
    return dict(
        inputs=[proj, proj, proj, dyp, proj, dyp, pool_w, pool_scale, _pool_bands(t), _pool_bands(t, anti=True)],
        in_specs=[tile(0), tile(1), pl.BlockSpec((HEAD, D_HALF), lambda i: (jnp.maximum(tile_of(i) * hb - 1, 0), 0)),
                  tile(0), below(1), below(0), const3((N_HEADS, HEAD, HEAD)), pl.BlockSpec((1, D_HALF), lambda i: (0, 0)),
                  const3((N_HEADS, t, HEAD + t)), const3((N_HEADS, t, HEAD + t))],
        out_specs=[tile(0), tile(0), const3((N_HEADS, HEAD, HEAD)), pl.BlockSpec((1, D_HALF), lambda i: (0, 0))],
        out_shape=[_sds((s, D_HALF), BF16), _sds((s, D_HALF), BF16), _sds((N_HEADS, HEAD, HEAD)), _sds((1, D_HALF))],
        scratch=[], stages=stages)


def _conv_bwd_part(proj, pre, conv_w, a_log, dt_bias, dqn, dkn, dvs, dbeta, dg, tile_of, n_tiles):
    s = proj.shape[0]
    t = CONV_T

    def stages(ins, outs, scratch):
        (q_ref, k_ref, v_ref, yq_ref, yk_ref, yv_ref, ba_ref, cw_ref, al_ref, dtb_ref,
         dqn_ref, dkn_ref, dvs_ref, dbeta_ref, dg_ref) = ins
        oq_ref, ok_ref, ov_ref, dba_ref, gcw_out, gsm_out = outs
        below, gcw_ref, gsm_ref = scratch
        step = pl.program_id(0)

        @pl.when(step == 0)
        def _():
            gcw_ref[...] = jnp.zeros_like(gcw_ref)
            gsm_ref[...] = jnp.zeros_like(gsm_ref)
            below[...] = jnp.zeros_like(below)

        parts = ((q_ref, yq_ref, dqn_ref, oq_ref), (k_ref, yk_ref, dkn_ref, ok_ref), (v_ref, yv_ref, dvs_ref, ov_ref))
        for p, (x_ref, y_ref, d_ref, o_ref) in enumerate(parts):
            for h in HEADS:
                cs = HEAD_COLS[h]
                wide = slice(p * D_HALF + h * HEAD, p * D_HALF + (h + 1) * HEAD)
                cw = cw_ref[:, wide]
                y = y_ref[:, cs]
                sg = _sigmoid(y)
                sv = y * sg
                ds = d_ref[:, cs]
                if p < 2:
                    rn = lax.rsqrt(_rowsum(sv * sv) + EPS)
                    nrm = sv * rn
                    ds = rn * (ds - nrm * _rowsum(ds * nrm))
                dy = ds * (sg * (1.0 + y * (1.0 - sg)))
                nxt = below[:, wide]
                ahead = [dy] + [_shift_up(dy, nxt, sft) for sft in range(1, CONV_K)]
                xv = x_ref[:, cs]
                acc = dy * cw[CONV_K - 1:CONV_K]
                for sft in range(1, CONV_K):
                    acc = acc + ahead[sft] * cw[CONV_K - 1 - sft:CONV_K - sft]
                for j in range(CONV_K):
                    gcw_ref[8 * j:8 * j + 8, wide] += _rows8(xv * ahead[CONV_K - 1 - j])
                o_ref[:, cs] = acc.astype(BF16)
                below[:, wide] = dy[0:8]
                yield

        ba = ba_ref[...]
        lane = lax.broadcasted_iota(I32, (t, HEAD), 1)
        lane8 = lax.broadcasted_iota(I32, (8, HEAD), 1)
        dba = jnp.zeros((t, HEAD), F32)
        gsm = jnp.zeros((8, HEAD), F32)
        for h in HEADS:
            beta = _sigmoid(ba[:, h:h + 1])
            dbeta = dbeta_ref[:, h * HEAD:h * HEAD + 1]
            xg = ba[:, N_HEADS + h:N_HEADS + h + 1] + dtb_ref[0:1, h:h + 1]
            nexp = -jnp.exp(al_ref[0:1, h:h + 1])
            dgv = dg_ref[:, h * HEAD:h * HEAD + 1]
            da = dgv * nexp * _sigmoid(xg)
            dba = dba + jnp.where(lane == h, dbeta * beta * (1.0 - beta), 0.0) + jnp.where(lane == N_HEADS + h, da, 0.0)
            gsm = (gsm + jnp.where(lane8 == h, _rows8(dgv * nexp * _softplus(xg)), 0.0)
                   + jnp.where(lane8 == N_HEADS + h, _rows8(da), 0.0))
        dba_ref[...] = jnp.zeros_like(dba_ref)
        dba_ref[:, :HEAD] = dba.astype(BF16)
        gsm_ref[...] += gsm
        yield

        @pl.when(step == n_tiles - 1)
        def _():
            gcw_out[...] = jnp.zeros_like(gcw_out)
            for j in range(CONV_K):
                gcw_out[j:j + 1, :] = _colsum(gcw_ref[8 * j:8 * j + 8, :])
            gsm_out[...] = jnp.broadcast_to(_colsum(gsm_ref[...]), (8, HEAD))

    row = pl.BlockSpec((t, D_HALF), lambda i: (tile_of(i), 0))
    const = lambda shape: pl.BlockSpec(shape, lambda i: (0, 0))
    return dict(
        inputs=[proj] * 3 + list(pre) + [proj, conv_w, a_log, dt_bias, dqn, dkn, dvs, dbeta, dg],
        in_specs=_conv_specs(t, tile_of)[:3] + [row] * 3
                 + [pl.BlockSpec((t, HEAD), lambda i: (tile_of(i), COL_BA // HEAD)),
                    const((CONV_K, 3 * D_HALF)), const((1, N_HEADS)), const((1, N_HEADS))] + [row] * 5,
        out_specs=[row, row, row, row, const((8, 3 * D_HALF)), const((8, HEAD))],
        out_shape=[_sds((s, D_HALF), BF16)] * 4 + [_sds((8, 3 * D_HALF)), _sds((8, HEAD))],
        scratch=[pltpu.VMEM((8, 3 * D_HALF), F32), pltpu.VMEM((8 * CONV_K, 3 * D_HALF), F32), pltpu.VMEM((8, HEAD), F32)],
        stages=stages)


def _pool_fwd_part(proj, pool_w, pool_scale):
    s = proj.shape[0]
    t = POOL_T
    hb = t // HEAD

    def stages(ins, outs, scratch, tile=None):
        u_ref, z_ref, halo_ref, pw_ref, ps_ref, band_ref = ins
        y_ref, = outs
        i = pl.program_id(0) if tile is None else tile
        live = (i > 0).astype(F32)
        groups = lambda ref: [ref[:, sl] for sl in HEAD_COLS]
        z = groups(z_ref)
        u, halo = groups(u_ref), [h * live for h in groups(halo_ref)]
        yield
        _, mixed, sg, _ = _pool_mix(u, halo, z, [pw_ref[g] for g in HEADS], [band_ref[g] for g in HEADS], i * t)
        yield
        for sl, m, zg, s_ in zip(HEAD_COLS, mixed, z, sg):
            y_ref[:, sl] = m * ps_ref[:, sl] * (zg * s_)
        yield

    const3 = lambda shape: pl.BlockSpec(shape, lambda i: (0, 0, 0))
    return dict(
        inputs=[proj, proj, proj, pool_w, pool_scale, _pool_bands(t)],
        in_specs=[pl.BlockSpec((t, D_HALF), lambda i: (i, 0)), pl.BlockSpec((t, D_HALF), lambda i: (i, 1)),
                  pl.BlockSpec((HEAD, D_HALF), lambda i: (jnp.maximum(i * hb - 1, 0), 0)),
                  const3((N_HEADS, HEAD, HEAD)), pl.BlockSpec((1, D_HALF), lambda i: (0, 0)), const3((N_HEADS, t, HEAD + t))],
        out_specs=[pl.BlockSpec((t, D_HALF), lambda i: (i, 0))], out_shape=[_sds((s, D_HALF))],
        scratch=[], stages=stages)


def _conv_fwd_part(proj, conv_w, a_log, dt_bias):
    s = proj.shape[0]
    t = CONV_T

    def stages(ins, outs, scratch, tile=None):
        q_ref, k_ref, v_ref, hq_ref, hk_ref, hv_ref, ba_ref, cw_ref, al_ref, dtb_ref = ins
        qn_ref, kn_ref, vs_ref, beta_ref, g_ref, yq_ref, yk_ref, yv_ref = outs
        live = ((pl.program_id(0) if tile is None else tile) > 0).astype(F32)
        parts = ((q_ref, hq_ref, qn_ref, yq_ref), (k_ref, hk_ref, kn_ref, yk_ref), (v_ref, hv_ref, vs_ref, yv_ref))
        for p, (x_ref, h_ref, o_ref, y_ref) in enumerate(parts):
            for h in HEADS:
                cs = HEAD_COLS[h]
                taps = _conv_taps(x_ref[:, cs], h_ref[:, cs] * live)
                y = _conv_pre(taps, cw_ref[:, p * D_HALF + h * HEAD:p * D_HALF + (h + 1) * HEAD])
                y_ref[:, cs] = y
                sv = y * _sigmoid(y)
                o_ref[:, cs] = sv if p == 2 else sv * lax.rsqrt(_rowsum(sv * sv) + EPS)
                yield
        ba = ba_ref[...]
        for h in HEADS:
            beta = _sigmoid(ba[:, h:h + 1])
            gl = -jnp.exp(al_ref[0:1, h:h + 1]) * _softplus(ba[:, N_HEADS + h:N_HEADS + h + 1] + dtb_ref[0:1, h:h + 1])
            beta_ref[:, HEAD_COLS[h]] = jnp.broadcast_to(beta, (t, HEAD))
            g_ref[:, HEAD_COLS[h]] = jnp.broadcast_to(gl, (t, HEAD))
        yield

    row = pl.BlockSpec((t, D_HALF), lambda i: (i, 0))
    const = lambda shape: pl.BlockSpec(shape, lambda i: (0, 0))
    return dict(
        inputs=[proj] * 7 + [conv_w, a_log, dt_bias],
        in_specs=_conv_specs(t) + [pl.BlockSpec((t, HEAD), lambda i: (i, COL_BA // HEAD)),
                                   const((CONV_K, 3 * D_HALF)), const((1, N_HEADS)), const((1, N_HEADS))],
        out_specs=[row] * 8, out_shape=[_sds((s, D_HALF))] * 8, scratch=[], stages=stages)


def _front_fwd(x, norm_w, w_pad, conv_w, a_log, dt_bias, pool_w, pool_scale):
    s = x.shape[0]
    t = CONV_T
    n_tiles = s // t
    assert POOL_T == CONV_T
    like_proj = _sds((s, N_IN_PAD))
    conv = _conv_fwd_part(like_proj, conv_w, a_log, dt_bias)
    pool = _pool_fwd_part(like_proj, pool_w, pool_scale)
    bands = pool["inputs"][-1]
    mxu_n = 256
    col_bounds = list(range(0, N_IN_PAD, 3 * mxu_n)) + [N_IN_PAD]

    def body(x_ref, nw_ref, w_ref, cw_ref, al_ref, dtb_ref, pw_ref, ps_ref, band_ref,
             proj_ref, nt_ref, qn_ref, kn_ref, vs_ref, beta_ref, g_ref, yq_ref, yk_ref, yv_ref, y_ref, prev):
        i = pl.program_id(0)

        @pl.when(i == 0)
        def _():
            prev[...] = jnp.zeros_like(prev)

        tile = jnp.maximum(i - 1, 0)
        main, above8, above = pl.ds(HEAD, t), pl.ds(HEAD - 8, 8), pl.ds(0, HEAD)
        cols = lambda rows, c0, width=D_HALF: prev.at[rows, pl.ds(c0, width)]
        conv_ins = (cols(main, 2 * D_HALF), cols(main, 3 * D_HALF), cols(main, 4 * D_HALF),
                    cols(above8, 2 * D_HALF), cols(above8, 3 * D_HALF), cols(above8, 4 * D_HALF),
                    cols(main, COL_BA, HEAD), cw_ref, al_ref, dtb_ref)
        pool_ins = (cols(main, 0), cols(main, D_HALF), cols(above, 0), pw_ref, ps_ref, band_ref)

        def projection():
            xv = x_ref[...]
            r = lax.rsqrt(jnp.mean(xv * xv, axis=-1, keepdims=True) + EPS)
            nv = xv * r * nw_ref[...]
            nt_ref[...] = nv.T.astype(BF16)
            nb = nv.astype(BF16)
            yield
            for lo, hi in zip(col_bounds[:-1], col_bounds[1:]):
                proj_ref[:, lo:hi] = jnp.dot(nb, w_ref[:, lo:hi], preferred_element_type=F32)
                yield

        _interleave(projection(),
                    conv["stages"](conv_ins, (qn_ref, kn_ref, vs_ref, beta_ref, g_ref, yq_ref, yk_ref, yv_ref), (), tile),
                    pool["stages"](pool_ins, (y_ref,), (), tile))
        prev[0:HEAD] = prev[t:t + HEAD]
        prev[HEAD:HEAD + t] = proj_ref[...]

    last = n_tiles - 1
    now = lambda i: (jnp.minimum(i, last), 0)
    before = lambda i: (jnp.maximum(i - 1, 0), 0)
    const = lambda a: pl.BlockSpec(a.shape, lambda i: (0,) * a.ndim)
    half = pl.BlockSpec((t, D_HALF), before)
    return _call(
        body, name="front_fwd", grid=(n_tiles + 1,),
        in_specs=[pl.BlockSpec((t, D_MODEL), now), const(norm_w), const(w_pad), const(conv_w), const(a_log), const(dt_bias),
                  const(pool_w), const(pool_scale), const(bands)],
        out_specs=[pl.BlockSpec((t, N_IN_PAD), now), pl.BlockSpec((D_MODEL, t), lambda i: (0, jnp.minimum(i, last)))]
                  + [half] * 9,
        out_shape=[_sds((s, N_IN_PAD)), _sds((D_MODEL, s), BF16)] + [_sds((s, D_HALF))] * 9,
        scratch_shapes=[pltpu.VMEM((HEAD + t, N_IN_PAD), F32)],
        compiler_params=_params("arbitrary"),
    )(x, norm_w, w_pad, conv_w, a_log, dt_bias, pool_w, pool_scale, bands)


def _conv_pool_bwd(proj, pre, conv_w, a_log, dt_bias, dqn, dkn, dvs, dbeta, dg, dyp, pool_w, pool_scale):
    n_tiles = proj.shape[0] // CONV_T
    assert POOL_T == CONV_T
    tile_of = lambda i: n_tiles - 1 - i
    return _fused_call("conv_pool_bwd", n_tiles, [
        _conv_bwd_part(proj, pre, conv_w, a_log, dt_bias, dqn, dkn, dvs, dbeta, dg, tile_of, n_tiles),
        _pool_bwd_part(proj, dyp, pool_w, pool_scale, tile_of, n_tiles)])


def _rows8(x):
    acc = x[0:8]
    for r in range(8, x.shape[0], 8):
        acc = acc + x[r:r + 8]
    return acc


IN_T = 512


def _in_bwd(x, dh, norm_w, w_pad, pieces):
    s = x.shape[0]
    t = IN_T
    widths = [D_HALF] * 6 + [N_IN_PAD - COL_BA]

    def body(*refs):
        x_ref, dh_ref, nw_ref, w_ref = refs[:4]
        p_refs = refs[4:4 + len(pieces)]
        gx_ref, gnw_ref = refs[4 + len(pieces):]

        @pl.when(pl.program_id(0) == 0)
        def _():
            gnw_ref[...] = jnp.zeros_like(gnw_ref)

        dn = jnp.zeros((t, D_MODEL), F32)
        col = 0
        for p_ref, wd in zip(p_refs, widths):
            dn = dn + _bdot_nt(p_ref[...], w_ref[:, col:col + wd])
            col += wd
        xv = x_ref[...]
        r = lax.rsqrt(jnp.mean(xv * xv, axis=-1, keepdims=True) + EPS)
        xhat = xv * r
        gnw_ref[...] += _colsum(dn * xhat)
        dxh = dn * nw_ref[...]
        gx_ref[...] = dh_ref[...] + r * (dxh - xhat * jnp.mean(dxh * xhat, axis=-1, keepdims=True))

    wide = pl.BlockSpec((t, D_MODEL), lambda i: (i, 0))
    return _call(
        body, name="in_bwd", grid=(s // t,),
        in_specs=[wide, wide, pl.BlockSpec((1, D_MODEL), lambda i: (0, 0)),
                  pl.BlockSpec((D_MODEL, N_IN_PAD), lambda i: (0, 0))]
                 + [pl.BlockSpec((t, wd), lambda i: (i, 0)) for wd in widths],
        out_specs=[wide, pl.BlockSpec((1, D_MODEL), lambda i: (0, 0))],
        out_shape=[_sds((s, D_MODEL)), _sds((1, D_MODEL))],
        compiler_params=_params("arbitrary"),
    )(x, dh, norm_w, w_pad, *pieces)


def _adamw_shard(name, w, g_own, g_got, cidx, m, v):
    _, r, c = w.shape
    half = r // 2
    rows = 256 if half % 256 == 0 else half
    per_half = half // rows

    def body(c_ref, w_ref, go_ref, gg_ref, m_ref, v_ref, gout_ref, d_ref, nm_ref, nv_ref):
        mine = (pl.program_id(0) // per_half) == c_ref[0]
        gv = jnp.where(mine, go_ref[:, :c], gg_ref[:, :c])
        gout_ref[0] = gv
        mn = ADAM_B1 * m_ref[0] + (1.0 - ADAM_B1) * gv
        vn = ADAM_B2 * v_ref[0] + (1.0 - ADAM_B2) * (gv * gv)
        m_hat = mn / (1.0 - ADAM_B1 ** ADAM_STEP)
        v_hat = vn / (1.0 - ADAM_B2 ** ADAM_STEP)
        d_ref[0] = -ADAM_LR * (m_hat / (jnp.sqrt(v_hat) + ADAM_EPS) + ADAM_WD * w_ref[0])
        nm_ref[0] = mn
        nv_ref[0] = vn

    blk = pl.BlockSpec((1, rows, c), lambda i, c_ref: (0, i, 0))
    gblk = pl.BlockSpec((rows, g_own.shape[1]), lambda i, c_ref: (i % per_half, 0))
    return _call(
        body, name=name,
        grid_spec=pltpu.PrefetchScalarGridSpec(
            num_scalar_prefetch=1, grid=(2 * per_half,),
            in_specs=[blk, gblk, gblk, blk, blk], out_specs=[blk] * 4),
        out_shape=[_sds((1, r, c))] * 4,
        compiler_params=_params("arbitrary"),
    )(cidx, w, g_own, g_got, m, v)


def _adamw_tiles(name, w, g, m, v):
    n = w.shape[0]
    nb = 77 if n % 77 == 0 else n

    def body(w_ref, g_ref, m_ref, v_ref, d_ref, nm_ref, nv_ref):
        gv = g_ref[...]
        mn = ADAM_B1 * m_ref[...] + (1.0 - ADAM_B1) * gv
        vn = ADAM_B2 * v_ref[...] + (1.0 - ADAM_B2) * (gv * gv)
        m_hat = mn / (1.0 - ADAM_B1 ** ADAM_STEP)
        v_hat = vn / (1.0 - ADAM_B2 ** ADAM_STEP)
        d_ref[...] = -ADAM_LR * (m_hat / (jnp.sqrt(v_hat) + ADAM_EPS) + ADAM_WD * w_ref[...])
        nm_ref[...] = mn
        nv_ref[...] = vn

    blk = pl.BlockSpec((nb, 8, HEAD), lambda i: (i, 0, 0))
    return _call(
        body, name=name, grid=(n // nb,),
        in_specs=[blk] * 4, out_specs=[blk] * 3, out_shape=[_sds(w.shape)] * 3,
        compiler_params=_params("arbitrary"),
    )(w, g, m, v)


def _exchange(name, inputs, out_shapes, phases):
    n_in = len(inputs)
    n_out = len(out_shapes)
    n_cp = sum(len(p) for p in phases)

    def body(*refs):
        ins, outs = refs[:n_in], refs[n_in:n_in + n_out]
        send, recv = refs[n_in + n_out:]
        pos = (lax.axis_index("x"), lax.axis_index("y"), lax.axis_index("c"))
        k = 0
        for phase in phases:
            cps = []
            for src, dst, target in phase:
                cps.append(pltpu.make_async_remote_copy(
                    src_ref=src(ins, outs, pos), dst_ref=dst(ins, outs, pos), send_sem=send.at[k], recv_sem=recv.at[k],
                    device_id=target(pos), device_id_type=pl.DeviceIdType.MESH))
                k += 1
            for cp in cps:
                cp.start()
            for cp in cps:
                cp.wait()

    anyspec = pl.BlockSpec(memory_space=pl.ANY)
    return _call(
        body, name=name,
        in_specs=[anyspec] * n_in, out_specs=[anyspec] * n_out, out_shape=list(out_shapes),
        scratch_shapes=[pltpu.SemaphoreType.DMA((n_cp,)), pltpu.SemaphoreType.DMA((n_cp,))],
    )(*inputs)


def _exchange_start(name, inputs, out_shapes, copies):
    n_in, n_out, n_cp = len(inputs), len(out_shapes), len(copies)

    def body(*refs):
        ins, lands = refs[:n_in], refs[n_in:n_in + n_out]
        sems = refs[n_in + n_out:n_in + n_out + 2 * n_cp]
        token = refs[-1]
        pos = (lax.axis_index("x"), lax.axis_index("y"), lax.axis_index("c"))
        for k, (src, dst, target) in enumerate(copies):
            pltpu.make_async_remote_copy(
                src_ref=src(ins, lands, pos), dst_ref=dst(ins, lands, pos), send_sem=sems[2 * k], recv_sem=sems[2 * k + 1],
                device_id=target(pos), device_id_type=pl.DeviceIdType.MESH).start()
        token[...] = jnp.zeros_like(token)

    hbm = pl.BlockSpec(memory_space=pltpu.HBM)
    sem = pl.BlockSpec(memory_space=pltpu.SEMAPHORE)
    bufs = list(inputs) + [lax.empty(o.shape, o.dtype) for o in out_shapes]
    outs = _call(
        body, name=name,
        out_shape=tuple([pltpu.SemaphoreType.DMA(())] * (2 * n_cp) + [pltpu.HBM(b.shape, b.dtype) for b in bufs]
                        + [_sds((8, HEAD))]),
        in_specs=[hbm] * len(bufs),
        out_specs=tuple([sem] * (2 * n_cp) + [hbm] * len(bufs) + [pl.BlockSpec(memory_space=pltpu.VMEM)]),
        input_output_aliases={i: 2 * n_cp + i for i in range(len(bufs))},
        compiler_params=pltpu.CompilerParams(has_side_effects=pltpu.SideEffectType.DATAFLOW_SIDE_EFFECTING),
    )(*[pltpu.with_memory_space_constraint(b, pltpu.HBM) for b in bufs])
    return outs[:2 * n_cp], outs[2 * n_cp:2 * n_cp + n_in], outs[2 * n_cp + n_in:-1], outs[-1]


def _exchange_wait(name, sems, sources, lands, copies, after):
    n_in, n_out, n_cp = len(sources), len(lands), len(copies)

    def body(*refs):
        ins, zones = refs[:n_in], refs[n_in:n_in + n_out]
        sem_refs = refs[n_in + n_out:n_in + n_out + 2 * n_cp]
        pos = (lax.axis_index("x"), lax.axis_index("y"), lax.axis_index("c"))
        for k, (src, dst, target) in enumerate(copies):
            cp = pltpu.make_async_remote_copy(
                src_ref=src(ins, zones, pos), dst_ref=dst(ins, zones, pos), send_sem=sem_refs[2 * k],
                recv_sem=sem_refs[2 * k + 1], device_id=target(pos), device_id_type=pl.DeviceIdType.MESH)
            cp.wait_send()
            cp.wait_recv()

    hbm = pl.BlockSpec(memory_space=pltpu.HBM)
    sem = pl.BlockSpec(memory_space=pltpu.SEMAPHORE)
    bufs = list(sources) + list(lands)
    outs = _call(
        body, name=name,
        out_shape=tuple(pltpu.HBM(b.shape, b.dtype) for b in bufs),
        in_specs=[hbm] * len(bufs) + [sem] * (2 * n_cp) + [pl.BlockSpec(memory_space=pl.ANY)],
        out_specs=tuple([hbm] * len(bufs)),
        input_output_aliases={i: i for i in range(len(bufs))},
        compiler_params=pltpu.CompilerParams(has_side_effects=pltpu.SideEffectType.DATAFLOW_SIDE_EFFECTING),
    )(*bufs, *sems, after)
    return outs[:n_in], outs[n_in:]


def _allreduce_tile(name, v):
    def body(v_ref, out_ref, slots, send, recv):
        x, y, c = lax.axis_index("x"), lax.axis_index("y"), lax.axis_index("c")
        me = 4 * x + 2 * y + c
        slots[me] = v_ref[...]
        cps = []
        for k in range(1, 8):
            peer = (x ^ (k >> 2), y ^ ((k >> 1) & 1), c ^ (k & 1))
            cps.append(pltpu.make_async_remote_copy(
                src_ref=v_ref, dst_ref=slots.at[me], send_sem=send.at[k - 1], recv_sem=recv.at[k - 1],
                device_id=peer, device_id_type=pl.DeviceIdType.MESH))
        for cp in cps:
            cp.start()
        for cp in cps:
            cp.wait()
        acc = slots[0]
        for i in range(1, 8):
            acc = acc + slots[i]
        out_ref[...] = acc

    vm = pl.BlockSpec(memory_space=pltpu.VMEM)
    return _call(
        body, name=name, in_specs=[vm], out_specs=vm, out_shape=_sds(v.shape),
        scratch_shapes=[pltpu.VMEM((8,) + v.shape, F32), pltpu.SemaphoreType.DMA((7,)), pltpu.SemaphoreType.DMA((7,))],
    )(v)


def _chip(pos):
    return 2 * pos[0] + pos[1]


def _other_chip(pos, mask):
    x, y, c = pos
    return (x ^ (mask >> 1), y ^ (mask & 1), c)


def _sibling(pos):
    return (pos[0], pos[1], 1 - pos[2])


def _gather_weights(wb, cb):
    rows = wb.shape[0] // 2
    x_nb, y_nb, diag = CHIP_MASKS

    def part(pos, mask, quarter=None):
        start = pos[2] * rows if quarter is None else pos[2] * rows + quarter * (rows // 2)
        return lambda outs: outs[0].at[_chip(pos) ^ mask, pl.ds(start, rows if quarter is None else rows // 2)]

    def passed_on(mask, to, quarter=None):
        return (lambda ins, outs, pos: part(pos, mask, quarter)(outs), lambda ins, outs, pos: part(pos, mask, quarter)(outs), to)

    first = [(lambda ins, outs, pos: ins[0].at[pl.ds(pos[2] * rows, rows)], lambda ins, outs, pos: part(pos, 0)(outs),
              functools.partial(_other_chip, mask=mask)) for mask in (x_nb, y_nb)]
    first += [(lambda ins, outs, pos: ins[1], lambda ins, outs, pos: outs[1].at[_chip(pos)],
               functools.partial(_other_chip, mask=mask)) for mask in CHIP_MASKS]
    second = [passed_on(x_nb, functools.partial(_other_chip, mask=y_nb), quarter=0),
              passed_on(y_nb, functools.partial(_other_chip, mask=x_nb), quarter=1),
              passed_on(x_nb, _sibling), passed_on(y_nb, _sibling)]
    third = [passed_on(diag, _sibling)]
    return _exchange("gather_weights", [wb, cb], [_sds((4,) + wb.shape, wb.dtype), _sds((4,) + cb.shape, cb.dtype)],
                     [first, second, third])


def _assemble_w_in(gw, wb, jidx):
    m = gw.shape[1]

    def body(j_ref, g_ref, wb_ref, o_ref):
        step = pl.program_id(0)

        @pl.when(step == 0)
        def _():
            o_ref[...] = jnp.zeros_like(o_ref)

        blk = jnp.where(step == j_ref[0], wb_ref[...], g_ref[0]).astype(F32)
        lane = lax.broadcasted_iota(I32, (m, BLK_IN_PAD), 1)
        for j in range(4):
            @pl.when(step == j)
            def _(j=j):
                base = j * BLK_IN // HEAD * HEAD
                shift = j * BLK_IN - base
                moved = pltpu.roll(blk, shift, 1) if shift else blk
                window = o_ref[:, base:base + BLK_IN_PAD].astype(F32)
                mine = (lane >= shift) & (lane < shift + BLK_IN)
                o_ref[:, base:base + BLK_IN_PAD] = jnp.where(mine, moved, window).astype(BF16)

    return _call(
        body, name="assemble_w_in",
        grid_spec=pltpu.PrefetchScalarGridSpec(
            num_scalar_prefetch=1, grid=(4,),
            in_specs=[pl.BlockSpec((1, m, BLK_IN_PAD), lambda j, j_ref: (j, 0, 0)),
                      pl.BlockSpec((m, BLK_IN_PAD), lambda j, j_ref: (0, 0))],
            out_specs=pl.BlockSpec((m, N_IN_PAD), lambda j, j_ref: (0, 0))),
        out_shape=_sds((m, N_IN_PAD), BF16),
        compiler_params=_params("arbitrary"),
    )(jidx, gw, wb)


def _gather_blocks(ob):
    copies = [(lambda ins, outs, pos: ins[0], lambda ins, outs, pos: outs[0].at[_chip(pos)],
               functools.partial(_other_chip, mask=mask)) for mask in CHIP_MASKS]
    return [_sds((4,) + ob.shape, ob.dtype)], copies


def _to_sibling_half(name, arrays):
    def src(ins, outs, pos, a):
        h = arrays[a].shape[-2] // 2
        sl = pl.ds((1 - pos[2]) * h, h)
        return ins[a].at[:, sl] if arrays[a].ndim == 3 else ins[a].at[sl]

    outs = [_sds(a.shape[:-2] + (a.shape[-2] // 2, a.shape[-1]), a.dtype) for a in arrays]
    phase = [(functools.partial(src, a=a), lambda ins, outs, pos, a=a: outs[a], _sibling) for a in range(len(arrays))]
    return _exchange(name, arrays, outs, [phase])


def _add_half(name, full, part, cidx):
    shape = part.shape
    lead = shape[0] if len(shape) == 3 else 1
    rows, cols = shape[-2], shape[-1]
    tr = rows
    nr = rows // tr
    f3 = full.reshape((lead,) + full.shape[-2:])
    p3 = part.reshape((lead, rows, cols))

    def body(c_ref, f_ref, p_ref, o_ref):
        o_ref[...] = (f_ref[...].astype(F32) + p_ref[...].astype(F32)).astype(o_ref.dtype)

    out = _call(
        body, name=name,
        grid_spec=pltpu.PrefetchScalarGridSpec(
            num_scalar_prefetch=1, grid=(lead, nr),
            in_specs=[pl.BlockSpec((1, tr, cols), lambda b, r, c_ref: (b, c_ref[0] * nr + r, 0)),
                      pl.BlockSpec((1, tr, cols), lambda b, r, c_ref: (b, r, 0))],
            out_specs=pl.BlockSpec((1, tr, cols), lambda b, r, c_ref: (b, r, 0))),
        out_shape=_sds((lead, rows, cols), part.dtype),
        compiler_params=_params("arbitrary", "arbitrary"),
    )(cidx, f3, p3)
    return out.reshape(shape)


def _to_other_chips(arrays, blocked):
    def src(ins, outs, pos, a, mask):
        return ins[a].at[_chip(pos) ^ mask] if blocked[a] else ins[a]

    outs = [_sds((3,) + (a.shape[1:] if b else a.shape), a.dtype) for a, b in zip(arrays, blocked)]
    copies = []
    for mi, mask in enumerate(CHIP_MASKS):
        for a in range(len(arrays)):
            copies.append((functools.partial(src, a=a, mask=mask), lambda ins, outs, pos, a=a, mi=mi: outs[a].at[mi],
                           functools.partial(_other_chip, mask=mask)))
    return outs, copies


def _add_chips(name, own, got, jidx, blocked):
    rows, cols = got.shape[-2:]
    tr = rows
    nr = rows // tr
    o3 = own if blocked else own.reshape((1, rows, cols))

    def body(j_ref, o_ref, g_ref, out_ref):
        out_ref[...] = ((o_ref[0].astype(F32) + g_ref[0].astype(F32))
                        + (g_ref[1].astype(F32) + g_ref[2].astype(F32)))

    own_map = (lambda r, j_ref: (j_ref[0], r, 0)) if blocked else (lambda r, j_ref: (0, r, 0))
    return _call(
        body, name=name,
        grid_spec=pltpu.PrefetchScalarGridSpec(
            num_scalar_prefetch=1, grid=(nr,),
            in_specs=[pl.BlockSpec((1, tr, cols), own_map),
                      pl.BlockSpec((3, tr, cols), lambda r, j_ref: (0, r, 0))],
            out_specs=pl.BlockSpec((tr, cols), lambda r, j_ref: (r, 0))),
        out_shape=_sds((rows, cols)),
        compiler_params=_params("arbitrary"),
    )(jidx, o3, got)


def _to_sibling(name, arrays):
    phase = [(lambda ins, outs, pos, a=a: ins[a], lambda ins, outs, pos, a=a: outs[a], _sibling)
             for a in range(len(arrays))]
    return _exchange(name, arrays, [_sds(a.shape, a.dtype) for a in arrays], [phase])


def _local_step(x, target, w_pad, w_out, conv_w, norm_w, pool_w, pool_scale, a_log, dt_bias, dn_norm_w, final_norm_w):
    proj, n_t, qn, kn, vs, beta, g, yq, yk, yv, y_pool = _front_fwd(x, norm_w, w_pad, conv_w, a_log, dt_bias, pool_w, pool_scale)
    w, att, qd, kd, tm, cd, o, vn, st = _delta_fwd(qn, kn, vs, beta, g)
    w_out = w_out(o) if callable(w_out) else w_out
    g_wout, dh, dyp, do, ddz, loss, g_fnw, g_dnw = _out_fwd_bwd(x, y_pool, o, proj, target, w_out, dn_norm_w, final_norm_w)
    dqn, dkn, dvs, dbeta, dg = _delta_bwd(do, vn, qd, kd, w, att, cd, st, qn, kn, vs, beta, g, tm)
    (dcq, dck, dcv, dba, g_cw, g_sm), (dpu, dpz, g_pw, g_ps) = _conv_pool_bwd(
        proj, (yq, yk, yv), conv_w, a_log, dt_bias, dqn, dkn, dvs, dbeta, dg, dyp, pool_w, pool_scale)
    pieces = [dpu, dpz, dcq, dck, dcv, ddz, dba]
    g_win = _grad_w_in(n_t, pieces)
    small = dict(norm_w=jnp.zeros_like(norm_w), pool_w=g_pw, pool_scale=g_ps, conv_w=g_cw[:CONV_K],
                 a_log=g_sm[0:1, 0:N_HEADS], dt_bias=g_sm[0:1, N_HEADS:2 * N_HEADS], dn_norm_w=g_dnw, final_norm_w=g_fnw)
    return loss[0, 0], g_win, g_wout, small, dh, pieces


SMALL_LAYOUT = (("pool_w", 512, HEAD, (1, N_HEADS, HEAD, HEAD)), ("final_norm_w", 8, HEAD, (D_MODEL,)),
                ("pool_scale", 4, HEAD, (1, D_HALF)), ("conv_w", 48, HEAD, (1, CONV_K, 3 * D_HALF)),
                ("dn_norm_w", 1, HEAD, (1, HEAD)), ("a_log", 1, N_HEADS, (1, N_HEADS)), ("dt_bias", 1, N_HEADS, (1, N_HEADS)),
                ("loss", 1, 1, ()))


def _small_offsets():
    offs, r = {}, 0
    for name, rows, _, _ in SMALL_LAYOUT:
        offs[name] = r
        r += -(-rows // 8) * 8
    assert r <= SMALL_ROWS
    return offs


def _pack_small(t):
    parts = []
    for name, rows, lanes, _ in SMALL_LAYOUT:
        a = t.get(name, jnp.zeros((1,), F32)).reshape(rows, lanes)
        parts.append(jnp.pad(a, ((0, -(-rows // 8) * 8 - rows), (0, HEAD - lanes))))
    buf = jnp.concatenate(parts, axis=0)
    return jnp.pad(buf, ((0, SMALL_ROWS - buf.shape[0]), (0, 0)))


def _adamw_small(w, g_own, g_got, cidx, m, v):
    offs = _small_offsets()
    names = [e[0] for e in SMALL_LAYOUT]
    n = len(names)

    def body(c_ref, w_ref, go_ref, gg_ref, m_ref, v_ref, *outs):
        own_low = c_ref[0] == 0
        gv = jnp.concatenate([jnp.where(own_low, go_ref[...], gg_ref[...]), jnp.where(own_low, gg_ref[...], go_ref[...])], axis=0)
        mn = ADAM_B1 * m_ref[...] + (1.0 - ADAM_B1) * gv
        vn = ADAM_B2 * v_ref[...] + (1.0 - ADAM_B2) * (gv * gv)
        m_hat = mn / (1.0 - ADAM_B1 ** ADAM_STEP)
        v_hat = vn / (1.0 - ADAM_B2 ** ADAM_STEP)
        dl = -ADAM_LR * (m_hat / (jnp.sqrt(v_hat) + ADAM_EPS) + ADAM_WD * w_ref[...])
        for kind, arr in enumerate((gv, dl, mn, vn)):
            for i, (name, rows, lanes, _) in enumerate(SMALL_LAYOUT):
                outs[kind * n + i][...] = arr[offs[name]:offs[name] + rows, :lanes]

    whole = lambda shape: pl.BlockSpec(shape, lambda i, c_ref: (0,) * len(shape))
    out_shapes = [_sds((rows, lanes)) for _, rows, lanes, _ in SMALL_LAYOUT] * 4
    res = _call(
        body, name="adamw_small",
        grid_spec=pltpu.PrefetchScalarGridSpec(
            num_scalar_prefetch=1, grid=(1,),
            in_specs=[whole(w.shape), whole(g_own.shape), whole(g_got.shape), whole(m.shape), whole(v.shape)],
            out_specs=[whole(o.shape) for o in out_shapes]),
        out_shape=out_shapes,
        compiler_params=_params("arbitrary"),
    )(cidx, w, g_own, g_got, m, v)
    return [{name: res[kind * n + i].reshape(shape) for i, (name, _, _, shape) in enumerate(SMALL_LAYOUT)}
            for kind in range(4)]


def kernel(x, norm_w, w_in, pool_w, pool_scale, conv_w, a_log, dt_bias, dn_norm_w, w_out, final_norm_w, loss_target, m_norm_w, m_w_in, m_pool_w, m_pool_scale, m_conv_w, m_a_log, m_dt_bias, m_dn_norm_w, m_w_out, m_final_norm_w, v_norm_w, v_w_in, v_pool_w, v_pool_scale, v_conv_w, v_a_log, v_dt_bias, v_dn_norm_w, v_w_out, v_final_norm_w):
    cidx = lax.axis_index("c").astype(I32).reshape(1)
    jidx = (2 * lax.axis_index("x") + lax.axis_index("y")).astype(I32)

    wb = jnp.pad(w_in[0].astype(BF16), ((0, 0), (0, BLK_IN_PAD - BLK_IN)))
    ob = w_out[0].astype(BF16)
    gw, gc = _gather_weights(wb, conv_w[0])
    mine = lambda j: jidx == j
    w_pad = _assemble_w_in(gw, wb, jidx.reshape(1))
    cw_full = jnp.concatenate([jnp.where(mine(j), conv_w[0], gc[j]) for j in range(4)], axis=1)

    lands_o, copies_o = _gather_blocks(ob)
    sems_o, ob_thru, zones_o, token_o = _exchange_start("gather_w_out_start", [ob], lands_o, copies_o)

    def w_out_full(after):
        (own,), (got,) = _exchange_wait("gather_w_out_wait", sems_o, ob_thru, zones_o, copies_o, after)
        return jnp.where((jnp.arange(4) == jidx)[:, None, None], own[None], got).reshape(D_MODEL, D_MODEL)

    loss, g_win, g_wout, small, dh, pieces = _local_step(
        x[0], loss_target[0], w_pad, w_out_full, cw_full, norm_w + token_o[0, 0], pool_w[0], pool_scale, a_log, dt_bias,
        dn_norm_w, final_norm_w.reshape(1, D_MODEL))
    small["loss"] = loss

    blocks_out = g_wout.reshape(4, BLK_OUT, D_MODEL)
    full = [g_win, blocks_out, _pack_small(small)]
    from_sib = _to_sibling_half("reduce_sibling", full)
    chip_sum = [_add_half("add_sibling_%d" % i, f, p, cidx) for i, (f, p) in enumerate(zip(full, from_sib))]
    blocked = [True, True, False]
    lands, copies = _to_other_chips(chip_sum, blocked)
    sems, chip_sum, zones, token = _exchange_start("reduce_chips_start", chip_sum, lands, copies)
    gx, g_nw = _in_bwd(x[0], dh, norm_w + token[0, 0], w_pad, pieces)
    g_nw = _allreduce_tile("reduce_norm_w", g_nw.reshape(8, HEAD)).reshape(1, D_MODEL)
    chip_sum, from_chips = _exchange_wait("reduce_chips_wait", sems, chip_sum, zones, copies, gx)
    halves = [_add_chips("add_chips_%d" % i, o, g, jidx.reshape(1), b)
              for i, (o, g, b) in enumerate(zip(chip_sum, from_chips, blocked))]
    other_halves = _to_sibling("swap_halves", halves)

    weights = dict(norm_w=norm_w, w_in=w_in, pool_w=pool_w, pool_scale=pool_scale, conv_w=conv_w, a_log=a_log,
                   dt_bias=dt_bias, dn_norm_w=dn_norm_w, w_out=w_out, final_norm_w=final_norm_w)
    ms = dict(norm_w=m_norm_w, w_in=m_w_in, pool_w=m_pool_w, pool_scale=m_pool_scale, conv_w=m_conv_w, a_log=m_a_log,
              dt_bias=m_dt_bias, dn_norm_w=m_dn_norm_w, w_out=m_w_out, final_norm_w=m_final_norm_w)
    vs = dict(norm_w=v_norm_w, w_in=v_w_in, pool_w=v_pool_w, pool_scale=v_pool_scale, conv_w=v_conv_w, a_log=v_a_log,
              dt_bias=v_dt_bias, dn_norm_w=v_dn_norm_w, w_out=v_w_out, final_norm_w=v_final_norm_w)
    names = ["norm_w", "w_in", "pool_w", "pool_scale", "conv_w", "a_log", "dt_bias", "dn_norm_w", "w_out", "final_norm_w"]
    small_names = [n for n in names if n not in ("w_in", "w_out")]

    def pack(t):
        conv = lax.dynamic_update_slice_in_dim(jnp.zeros((CONV_K, 3 * D_HALF), F32), t["conv_w"][0], jidx * BLK_CONV, axis=1)
        return _pack_small({**{n: t[n] for n in small_names if n != "conv_w"}, "conv_w": conv})

    results = [{}, {}, {}, {}]
    to_tiles = lambda a: jnp.transpose(a, (2, 0, 1)).reshape(BLK_IN, 8, HEAD)
    from_tiles = lambda a: jnp.transpose(a, (1, 2, 0)).reshape(1, D_MODEL, BLK_IN)
    lo = jnp.where(cidx[0] == 0, halves[0], other_halves[0])
    hi = jnp.where(cidx[0] == 0, other_halves[0], halves[0])
    g_tiles = jnp.concatenate([lo[:, :BLK_IN].T, hi[:, :BLK_IN].T], axis=1).reshape(BLK_IN, 8, HEAD)
    outs = _adamw_tiles("adamw_w_in", to_tiles(w_in), g_tiles, to_tiles(m_w_in), to_tiles(v_w_in))
    for res, o in zip(results, (g_tiles,) + tuple(outs)):
        res["w_in"] = from_tiles(o)
    outs = _adamw_shard("adamw_w_out", w_out, halves[1], other_halves[1], cidx, m_w_out, v_w_out)
    for res, o in zip(results, outs):
        res["w_out"] = o
    outs = _adamw_small(pack(weights), halves[2], other_halves[2], cidx, pack(ms), pack(vs))
    for res, got in zip(results, outs):
        got["conv_w"] = lax.dynamic_slice_in_dim(got["conv_w"], jidx * BLK_CONV, BLK_CONV, axis=2)
        res.update(got)
    one_tile = lambda a: a.reshape(1, 8, HEAD)
    outs = _adamw_tiles("adamw_norm_w", one_tile(norm_w), one_tile(g_nw), one_tile(m_norm_w), one_tile(v_norm_w))
    for res, o in zip(results, (g_nw,) + tuple(outs)):
        res["norm_w"] = o.reshape(1, D_MODEL)
    grads, delta, new_m, new_v = results

    return (grads["loss"], gx[None], *[grads[n] for n in names], *[delta[n] for n in names],
            *[new_m[n] for n in names], *[new_v[n] for n in names])
```

```python
import functools

import jax
import jax.numpy as jnp
import numpy as np
from jax import lax
from jax.experimental import pallas as pl
from jax.experimental.pallas import tpu as pltpu

F32 = jnp.float32
BF16 = jnp.bfloat16
I32 = jnp.int32

D_MODEL = 1024
D_HALF = 512
N_HEADS = 4
HEAD = 128
CHUNK = 64
PAIR = 2 * CHUNK
WINDOWS = (2, 4, 8, 16)
CONV_K = 4
EPS = 1e-6
N_IN = 3080
N_IN_PAD = 3200
BLK_IN = 770
BLK_IN_PAD = 896
BLK_OUT = 256
BLK_CONV = 384
COL_BA = 3072
QK_SCALE = HEAD ** -0.5
SMALL_ROWS = 608
VMEM_LIMIT = 56 * 1024 * 1024

ADAM_LR = 0.001
ADAM_B1 = 0.9
ADAM_B2 = 0.999
ADAM_EPS = 1e-08
ADAM_WD = 0.01
ADAM_STEP = 10

CHIP_MASKS = (2, 1, 3)
HEADS = range(N_HEADS)
HEAD_COLS = [slice(h * HEAD, (h + 1) * HEAD) for h in HEADS]


def _call(body, **kw):
    return pl.pallas_call(body, **kw)


def _params(*sem):
    return pltpu.CompilerParams(dimension_semantics=sem, vmem_limit_bytes=VMEM_LIMIT)


def _sds(shape, dtype=F32):
    return jax.ShapeDtypeStruct(shape, dtype)


def _bdot(a, b):
    return jnp.dot(a.astype(BF16), b.astype(BF16), preferred_element_type=F32)


def _bdot_nt(a, b):
    return lax.dot_general(a.astype(BF16), b.astype(BF16), (((1,), (1,)), ((), ())), preferred_element_type=F32)


def _bdot_tn(a, b):
    return lax.dot_general(a.astype(BF16), b.astype(BF16), (((0,), (0,)), ((), ())), preferred_element_type=F32)


def _side(a, b):
    return jnp.concatenate([a.astype(BF16), b.astype(BF16)], axis=1)


def _stack(a, b):
    return jnp.concatenate([a.astype(BF16), b.astype(BF16)], axis=0)


def _split(a):
    hi = a.astype(BF16)
    lo = (a - hi.astype(F32)).astype(BF16)
    return hi, lo


def _mask_dot(m, b):
    n = b.shape[1]
    both = jnp.dot(m, jnp.concatenate(_split(b), axis=1), preferred_element_type=F32)
    return both[:, :n] + both[:, n:]


def _sigmoid(x):
    return 0.5 * jnp.tanh(0.5 * x) + 0.5


def _softplus(x):
    return jnp.maximum(x, 0.0) + jnp.log(1.0 + jnp.exp(-jnp.abs(x)))


def _rowsum(x):
    return jnp.sum(x, axis=-1, keepdims=True)


def _colsum(x):
    return jnp.sum(x, axis=0, keepdims=True)


def _shift_down(xv, prev8, k):
    r = pltpu.roll(xv, k, 0)
    q = pltpu.roll(prev8, k, 0)
    row = lax.broadcasted_iota(I32, prev8.shape, 0)
    top = jnp.where(row < k, q, r[0:8])
    return jnp.concatenate([top, r[8:]], axis=0)


def _shift_up(xv, next8, k):
    t = xv.shape[0]
    r = pltpu.roll(xv, t - k, 0)
    q = pltpu.roll(next8, 8 - k, 0)
    row = lax.broadcasted_iota(I32, next8.shape, 0)
    bot = jnp.where(row >= 8 - k, q, r[t - 8:])
    return jnp.concatenate([r[:t - 8], bot], axis=0)


INTRA_PAIRS = 2
UNITS = [(pp, h) for pp in range(INTRA_PAIRS) for h in HEADS]


def _heads_of(ref, rows=PAIR, units=UNITS):
    return [ref[pp * rows:(pp + 1) * rows, HEAD_COLS[h]] for pp, h in units]


def _put_heads_of(ref, vals, rows=PAIR, units=UNITS):
    for (pp, h), v in zip(units, vals):
        ref[pp * rows:(pp + 1) * rows, HEAD_COLS[h]] = v.astype(ref.dtype)


_heads = _heads_of
_put_heads = _put_heads_of


def _each(fn, *lists):
    return [fn(*args) for args in zip(*lists)]


def _pool_bands(t, anti=False):
    r = np.arange(t)[:, None]
    c = np.arange(t + HEAD)[None, :]
    d = (c - r) if anti else (r - c + HEAD)
    return jnp.asarray(np.stack([(d >= 0) & (d < w) for w in WINDOWS]), BF16)


def _pool_mix(u, halo, z, pw, bands, row0):
    t = u[0].shape[0]
    rows = row0 + lax.broadcasted_iota(I32, (t, 1), 0) + 1
    cnt = [jnp.minimum(rows, w).astype(F32) for w in WINDOWS]
    win = _each(lambda b, h, v: _mask_dot(b, jnp.concatenate([h, v], axis=0)), bands, halo, u)
    mix = _each(lambda a, c, v: a / c - v, win, cnt, u)
    mixed = _each(_bdot, mix, pw)
    return mix, mixed, _each(_sigmoid, z), cnt


POOL_T = 256


def _conv_taps(xv, prev8):
    return [_shift_down(xv, prev8, CONV_K - 1 - j) for j in range(CONV_K - 1)] + [xv]


def _conv_pre(taps, cw):
    y = taps[CONV_K - 1] * cw[CONV_K - 1:CONV_K]
    for j in range(CONV_K - 2, -1, -1):
        y = y + taps[j] * cw[j:j + 1]
    return y


CONV_T = 256


def _conv_specs(t, tile_of=lambda i: i):
    tiles = [pl.BlockSpec((t, D_HALF), functools.partial(lambda i, p: (tile_of(i), 2 + p), p=p)) for p in range(3)]
    halos = [pl.BlockSpec((8, D_HALF),
                          functools.partial(lambda i, p: (jnp.maximum(tile_of(i) * (t // 8) - 1, 0), 2 + p), p=p))
             for p in range(3)]
    return tiles + halos


def _pair_masks():
    r = lax.broadcasted_iota(I32, (PAIR, PAIR), 0)
    c = lax.broadcasted_iota(I32, (PAIR, PAIR), 1)
    same = jnp.right_shift(r, 6) == jnp.right_shift(c, 6)
    return same, same & (r >= c), same & (r > c), r == c


def _interleave(*stage_lists):
    live = list(stage_lists)
    while live:
        for gen in list(live):
            try:
                next(gen)
            except StopIteration:
                live.remove(gen)


def _pair_common_stages(cm, qn, kn, vs, beta, g):
    same, incl, strict, eye = _pair_masks()
    incl_b = incl.astype(BF16)
    first = lax.broadcasted_iota(I32, (PAIR, HEAD), 0) < CHUNK
    cm.update(same=same, incl=incl, strict=strict, eye=eye)
    gc = _each(lambda gv: _mask_dot(incl_b, gv), g)
    q = _each(lambda v: v * QK_SCALE, qn)
    kb = _each(lambda k, b: k * b, kn, beta)
    cm.update(gc=gc, q=q, kb=kb, vb=_each(lambda v, b: v * b, vs, beta))
    yield
    both = _each(lambda a, b, c: _bdot_nt(_stack(a, b), c), kb, q, kn)
    cm.update(kk=[v[:PAIR] for v in both], qk=[v[PAIR:] for v in both])
    gc_row = _each(lambda v: _colsum(jnp.where(eye, v, 0.0)), gc)
    gl = _each(lambda v: jnp.where(first, v[CHUNK - 1:CHUNK], v[PAIR - 1:PAIR]), gc)
    egc = _each(jnp.exp, gc)
    cm.update(gl=gl, egc=egc,
              decay=_each(lambda v, r: jnp.where(incl, jnp.exp(jnp.where(incl, v - r, 0.0)), 0.0), gc, gc_row))
    yield
    cm.update(ekd=_each(lambda a, b: jnp.exp(a - b), gl, gc), cd=_each(jnp.exp, gl),
              kbg=_each(lambda k, e: k * e, kb, egc))
    yield


def _tri_inv_stages(out, a, eye_f):
    p = _each(lambda v: eye_f - v, a)
    x = _each(_bdot, a, a)
    yield
    for it in range(4):
        both = _each(lambda xv, pv: _bdot(xv, _side(pv, xv)), x, p)
        p = _each(lambda pv, b: pv + b[:, :PAIR], p, both)
        x = [b[:, PAIR:] for b in both]
        yield
    out["t"] = _each(lambda pv, xv: pv + _bdot(pv, xv), p, x)
    yield


def _chunk_scalar_spec(pairs=1, index=lambda i: (i, 0)):
    return pl.BlockSpec((16 * pairs, D_HALF), index)


SCAN_PAIRS = 2
SCAN_ROWS = SCAN_PAIRS * PAIR


def _delta_fwd(qn, kn, vs, beta, g):
    s = qn.shape[0]
    n_steps = s // SCAN_ROWS
    n_chunks = s // CHUNK
    assert INTRA_PAIRS == SCAN_PAIRS

    def body(qn_ref, kn_ref, vs_ref, beta_ref, g_ref, w_ref, att_ref, qd_ref, kd_ref, t_ref, cd_ref, o_ref, vn_ref, st_ref,
             state, u_s, w_s, att_s, qd_s, kd_s, cd_s):
        t = pl.program_id(0)

        @pl.when(t <= 1)
        def _():
            state[...] = jnp.zeros_like(state)

        @pl.when(t == 0)
        def _():
            for ref in (u_s, w_s, att_s, qd_s, kd_s, cd_s):
                ref[1] = jnp.zeros(ref.shape[1:], ref.dtype)

        cur = lax.rem(t, 2)
        prev = 1 - cur
        cols = list(enumerate(HEAD_COLS))

        def recurrence():
            sm = [state[h] for h in HEADS]
            for ci in range(2 * SCAN_PAIRS):
                rs = slice(ci * CHUNK, (ci + 1) * CHUNK)
                for h in HEADS:
                    st_ref[ci, h] = sm[h]
                both = [_bdot(jnp.concatenate([w_s[prev, rs, sl], qd_s[prev, rs, sl]], axis=0), sm[h]) for h, sl in cols]
                vn = [u_s[prev, rs, sl] - both[h][:CHUNK] for h, sl in cols]
                for h, sl in cols:
                    vn_ref[rs, sl] = vn[h].astype(BF16)
                    o_ref[rs, sl] = both[h][CHUNK:]
                yield
                sm = [sm[h] * cd_s[prev, ci * 8:ci * 8 + 1, sl] + _bdot_tn(kd_s[prev, rs, sl], vn[h]) for h, sl in cols]
                yield
            for h in HEADS:
                state[h] = sm[h]
            for pp in range(SCAN_PAIRS):
                rp = slice(pp * PAIR, (pp + 1) * PAIR)
                intra = [_bdot(att_s[prev, rp, sl], vn_ref[rp, sl]) for sl in HEAD_COLS]
                for h, sl in cols:
                    o_ref[rp, sl] += intra[h]
                yield

        def factors():
            kn = _heads(kn_ref)
            cm = {}
            yield from _pair_common_stages(cm, _heads(qn_ref), kn, _heads(vs_ref), _heads(beta_ref), _heads(g_ref))
            a = _each(lambda kk, d: jnp.where(cm["strict"], kk * d, 0.0), cm["kk"], cm["decay"])
            inv = {}
            yield from _tri_inv_stages(inv, a, cm["eye"].astype(F32))
            tm = inv["t"]
            uw = _each(lambda tv, a, b: _bdot(tv, _side(a, b)), tm, cm["vb"], cm["kbg"])
            res = dict(u=[v[:, :HEAD] for v in uw], w=[v[:, HEAD:] for v in uw],
                       att=_each(lambda a, b: a * b, cm["qk"], cm["decay"]),
                       qd=_each(lambda a, b: a * b, cm["q"], cm["egc"]), kd=_each(lambda a, b: a * b, kn, cm["ekd"]))
            yield
            _put_heads(t_ref, tm)
            for key, out, keep in (("w", w_ref, w_s), ("att", att_ref, att_s), ("qd", qd_ref, qd_s), ("kd", kd_ref, kd_s)):
                _put_heads(out, res[key])
                for (pp, h), v in zip(UNITS, res[key]):
                    keep[cur, pp * PAIR:(pp + 1) * PAIR, HEAD_COLS[h]] = v.astype(BF16)
            for (pp, h), v in zip(UNITS, res["u"]):
                u_s[cur, pp * PAIR:(pp + 1) * PAIR, HEAD_COLS[h]] = v
            for ci in range(2):
                for (pp, h), v in zip(UNITS, cm["cd"]):
                    rows8 = slice(pp * 16 + ci * 8, pp * 16 + (ci + 1) * 8)
                    cd_ref[rows8, HEAD_COLS[h]] = v[ci * CHUNK:ci * CHUNK + 8]
                    cd_s[cur, rows8, HEAD_COLS[h]] = v[ci * CHUNK:ci * CHUNK + 8]
            yield

        _interleave(recurrence(), factors())

    last = n_steps - 1
    now = lambda i: (jnp.minimum(i, last), 0)
    before = lambda i: (jnp.maximum(i - 1, 0), 0)
    rows = lambda index: pl.BlockSpec((SCAN_ROWS, D_HALF), index)
    slot = lambda r, dtype: pltpu.VMEM((2, r, D_HALF), dtype)
    return _call(
        body, name="delta_fwd", grid=(n_steps + 1,),
        in_specs=[rows(now)] * 5,
        out_specs=[rows(now)] * 5 + [_chunk_scalar_spec(SCAN_PAIRS, now), rows(before), rows(before),
                                     pl.BlockSpec((2 * SCAN_PAIRS, N_HEADS, HEAD, HEAD), lambda i: (jnp.maximum(i - 1, 0), 0, 0, 0))],
        out_shape=[_sds((s, D_HALF), BF16)] * 5 + [_sds((s // 8, D_HALF)), _sds((s, D_HALF)), _sds((s, D_HALF), BF16),
                                                  _sds((n_chunks, N_HEADS, HEAD, HEAD))],
        scratch_shapes=[pltpu.VMEM((N_HEADS, HEAD, HEAD), F32), slot(SCAN_ROWS, F32), slot(SCAN_ROWS, BF16),
                        slot(SCAN_ROWS, BF16), slot(SCAN_ROWS, BF16), slot(SCAN_ROWS, BF16), slot(16 * SCAN_PAIRS, F32)],
        compiler_params=_params("arbitrary"),
    )(qn, kn, vs, beta, g)


OUT_T = 512
OUT_ROWS = 256


def _out_fwd_bwd(x, y_pool, o, proj, target, w_out, dn_norm_w, final_norm_w):
    s = x.shape[0]
    t = OUT_T

    def body(x_ref, yp_ref, o_ref, z_ref, tg_ref, wo_ref, dnw_ref, fnw_ref,
             gwo_ref, dh_ref, dyp_ref, do_ref, dz_ref, loss_ref, gfn_ref, gdn_ref, y_ref, yt_ref, gwo_acc):
        @pl.when(pl.program_id(0) == 0)
        def _():
            loss_ref[...] = jnp.zeros_like(loss_ref)
            gfn_ref[...] = jnp.zeros_like(gfn_ref)
            gdn_ref[...] = jnp.zeros_like(gdn_ref)
            gwo_acc[...] = jnp.zeros_like(gwo_acc)

        dnw = dnw_ref[...]
        fnw = fnw_ref[...]

        def stages(rows, lead):
            for _ in range(lead):
                yield
            ypv = yp_ref[rows]
            y_ref[rows, :D_HALF] = ypv.astype(BF16)
            yt_ref[:D_HALF, rows] = ypv.T.astype(BF16)
            keep = []
            for h in HEADS:
                ov = o_ref[rows, HEAD_COLS[h]]
                zv = z_ref[rows, HEAD_COLS[h]]
                ro = lax.rsqrt(jnp.mean(ov * ov, axis=-1, keepdims=True) + EPS)
                ohat = ov * ro
                sg = _sigmoid(zv)
                keep.append((ro, ohat, zv, sg))
                ydn = ohat * dnw * (zv * sg)
                y_ref[rows, D_HALF + h * HEAD:D_HALF + (h + 1) * HEAD] = ydn.astype(BF16)
                yt_ref[D_HALF + h * HEAD:D_HALF + (h + 1) * HEAD, rows] = ydn.T.astype(BF16)
            yield
            hv = x_ref[rows] + jnp.dot(y_ref[rows], wo_ref[...], preferred_element_type=F32)
            yield
            r2 = lax.rsqrt(jnp.mean(hv * hv, axis=-1, keepdims=True) + EPS)
            hhat = hv * r2
            err = hhat * fnw - tg_ref[rows]
            loss_ref[...] += 0.5 * jnp.sum(_rowsum(err * err) * (1.0 / D_MODEL), axis=0, keepdims=True)
            dout = err * (1.0 / D_MODEL)
            gfn_ref[...] += _colsum(dout * hhat)
            dhh = dout * fnw
            dh = r2 * (dhh - hhat * jnp.mean(dhh * hhat, axis=-1, keepdims=True))
            dh_ref[rows] = dh
            yield
            if lead == t // OUT_ROWS - 1:
                gwo_acc[...] += _bdot(yt_ref[...], dh_ref[...])
            dy = _bdot_nt(dh, wo_ref[...])
            yield
            dyp_ref[rows] = dy[:, :D_HALF]
            gdn = jnp.zeros((1, HEAD), F32)
            for h in HEADS:
                ro, ohat, zv, sg = keep[h]
                dyd = dy[:, D_HALF + h * HEAD:D_HALF + (h + 1) * HEAD]
                sz = zv * sg
                dz_ref[rows, HEAD_COLS[h]] = (dyd * ohat * dnw * (sg * (1.0 + zv * (1.0 - sg)))).astype(BF16)
                gdn = gdn + _colsum(dyd * ohat * sz)
                doh = dyd * dnw * sz
                do_ref[rows, HEAD_COLS[h]] = ro * (doh - ohat * jnp.mean(doh * ohat, axis=-1, keepdims=True))
            gdn_ref[...] += gdn
            yield

        _interleave(*[stages(slice(k * OUT_ROWS, (k + 1) * OUT_ROWS), k) for k in range(t // OUT_ROWS)])

        @pl.when(pl.program_id(0) == pl.num_programs(0) - 1)
        def _():
            gwo_ref[...] = gwo_acc[...].astype(BF16)

    wide = pl.BlockSpec((t, D_MODEL), lambda i: (i, 0))
    half = pl.BlockSpec((t, D_HALF), lambda i: (i, 0))
    const = lambda shape: pl.BlockSpec(shape, lambda i: (0,) * len(shape))
    return _call(
        body, name="out_fwd_bwd", grid=(s // t,),
        in_specs=[wide, half, half, pl.BlockSpec((t, D_HALF), lambda i: (i, 5)), wide,
                  const((D_MODEL, D_MODEL)), const((1, HEAD)), const((1, D_MODEL))],
        out_specs=[const((D_MODEL, D_MODEL)), wide, half, half, half,
                   const((1, HEAD)), const((1, D_MODEL)), const((1, HEAD))],
        out_shape=[_sds((D_MODEL, D_MODEL), BF16), _sds((s, D_MODEL)), _sds((s, D_HALF)), _sds((s, D_HALF)),
                   _sds((s, D_HALF), BF16), _sds((1, HEAD)), _sds((1, D_MODEL)), _sds((1, HEAD))],
        scratch_shapes=[pltpu.VMEM((t, D_MODEL), BF16), pltpu.VMEM((D_MODEL, t), BF16), pltpu.VMEM((D_MODEL, D_MODEL), F32)],
        compiler_params=_params("arbitrary"),
    )(x, y_pool, o, proj, target, w_out, dn_norm_w, final_norm_w)


def _grad_w_in(at, pieces):
    m, s = at.shape
    n = len(pieces)
    tn, tk = D_HALF, min(s, 1024)

    def body(a_ref, *refs):
        p_refs, o_ref, acc = refs[:n], refs[n], refs[n + 1]

        @pl.when(pl.program_id(0) == 0)
        def _():
            acc[...] = jnp.zeros_like(acc)

        av = a_ref[...]
        for p in range(n):
            acc[:, p * tn:(p + 1) * tn] += _bdot(av, p_refs[p][...])

        @pl.when(pl.program_id(0) == pl.num_programs(0) - 1)
        def _():
            for j in range(4):
                base = j * BLK_IN // HEAD * HEAD
                win = acc[:, base:base + BLK_IN_PAD]
                if j * BLK_IN > base:
                    win = pltpu.roll(win, BLK_IN_PAD - (j * BLK_IN - base), 1)
                o_ref[j] = win.astype(BF16)

    return _call(
        body, name="grad_w_in", grid=(s // tk,),
        in_specs=[pl.BlockSpec((m, tk), lambda k: (0, k))] + [pl.BlockSpec((tk, tn), lambda k: (k, 0))] * n,
        out_specs=pl.BlockSpec((4, m, BLK_IN_PAD), lambda k: (0, 0, 0)),
        out_shape=_sds((4, m, BLK_IN_PAD), BF16),
        scratch_shapes=[pltpu.VMEM((m, n * tn), F32)],
        compiler_params=_params("arbitrary"),
    )(at, *pieces)


def _delta_bwd(do, vn, qd, kd, w, att, cd, st, qn, kn, vs, beta, g, tm):
    s = do.shape[0]
    n_steps = s // SCAN_ROWS
    assert INTRA_PAIRS == SCAN_PAIRS

    def body(do_ref, vn_ref, qd_ref, kd_ref, w_ref, att_ref, cd_ref, st_ref, qn_ref, kn_ref, vs_ref, beta_ref, g_ref, t_ref,
             dqn_ref, dkn_ref, dvs_ref, dbeta_ref, dg_ref, dstate, du_s, dw_s, datt_s, dqd_s, dkd_s, dcd_s):
        t = pl.program_id(0)

        @pl.when(t == 0)
        def _():
            dstate[...] = jnp.zeros_like(dstate)
            for ref in (du_s, dw_s, datt_s, dqd_s, dkd_s, dcd_s):
                ref[1] = jnp.zeros(ref.shape[1:], ref.dtype)

        cur = lax.rem(t, 2)
        prev = 1 - cur
        cols = list(enumerate(HEAD_COLS))
        _, incl, _, _ = _pair_masks()

        def recurrence():
            dv_intra = []
            for pp in range(SCAN_PAIRS):
                rp = slice(pp * PAIR, (pp + 1) * PAIR)
                dv_intra.append([_bdot_tn(att_ref[rp, sl], do_ref[rp, sl]) for _, sl in cols])
                for _, sl in cols:
                    datt_s[cur, rp, sl] = jnp.where(incl, _bdot_nt(do_ref[rp, sl], vn_ref[rp, sl]), 0.0)
                yield
            ds = [dstate[h] for h in HEADS]
            for ci in range(2 * SCAN_PAIRS - 1, -1, -1):
                rs = slice(ci * CHUNK, (ci + 1) * CHUNK)
                in_pair = slice((ci % 2) * CHUNK, (ci % 2 + 1) * CHUNK)
                sm = [st_ref[ci, h] for h in HEADS]
                dvn = [dv_intra[ci // 2][h][in_pair] + _bdot(kd_ref[rs, sl], ds[h]) for h, sl in cols]
                dkd = [_bdot_nt(vn_ref[rs, sl], ds[h]) for h, sl in cols]
                dcd = [jnp.broadcast_to(_rowsum(_colsum(ds[h] * sm[h])), (8, HEAD)) for h in HEADS]
                yield
                both = [_bdot_nt(_stack(do_ref[rs, sl], dvn[h]), sm[h]) for h, sl in cols]
                for h, sl in cols:
                    du_s[cur, rs, sl] = dvn[h].astype(BF16)
                    dqd_s[cur, rs, sl] = both[h][:CHUNK]
                    dw_s[cur, rs, sl] = (-both[h][CHUNK:]).astype(BF16)
                    dkd_s[cur, rs, sl] = dkd[h]
                    dcd_s[cur, ci * 8:(ci + 1) * 8, sl] = dcd[h]
                ds = [ds[h] * cd_ref[ci * 8:ci * 8 + 1, sl]
                      + _bdot_tn(_stack(qd_ref[rs, sl], w_ref[rs, sl]), _stack(do_ref[rs, sl], -dvn[h])) for h, sl in cols]
                yield
            for h in HEADS:
                dstate[h] = ds[h]

        def factors(units):
            _heads = functools.partial(_heads_of, units=units)
            _put_heads = functools.partial(_put_heads_of, units=units)
            ones = jnp.ones((2 * PAIR, HEAD), BF16)
            tn = (((0,), (0,)), ((), ()))
            kept = lambda ref, rows=PAIR: [ref[prev, pp * rows:(pp + 1) * rows, HEAD_COLS[h]] for pp, h in units]
            kn, vs, beta = _heads(kn_ref), _heads(vs_ref), _heads(beta_ref)
            cm = {}
            yield from _pair_common_stages(cm, _heads(qn_ref), kn, vs, beta, _heads(g_ref))
            tmv = _heads(t_ref)
            duv, dwv, dattv, dqdv, dkdv = kept(du_s), kept(dw_s), kept(datt_s), kept(dqd_s), kept(dkd_s)
            duw = _each(_side, duv, dwv)
            both = _each(_bdot_tn, tmv, duw)
            dvb, dkbg = [v[:, :HEAD] for v in both], [v[:, HEAD:] for v in both]
            dt = _each(lambda a, b, c: _bdot_nt(a, _side(b, c)), duw, cm["vb"], cm["kbg"])
            yield
            m1 = _each(_bdot_tn, tmv, dt)
            yield
            da = _each(lambda a, b: -jnp.where(cm["strict"], _bdot_nt(a, b), 0.0), m1, tmv)
            yield
            dkk = _each(lambda a, b: a * b, da, cm["decay"])
            dqk = _each(lambda a, b: a * b, dattv, cm["decay"])
            dd = _each(lambda a, b, c, d: a * b + c * d, dkk, cm["kk"], dqk, cm["qk"])
            dkq = _each(_stack, dkk, dqk)
            both = _each(_bdot, dkq, kn)
            dkb = _each(lambda a, c, d: a[:PAIR] + c * d, both, dkbg, cm["egc"])
            dq = _each(lambda a, c, d: a[PAIR:] + c * d, both, dqdv, cm["egc"])
            yield
            dkn = _each(lambda a, b, c: _bdot_tn(a, _stack(b, c)), dkq, cm["kb"], cm["q"])
            dkn = _each(lambda a, b, c, d, e: a + b * c + d * e, dkn, dkdv, cm["ekd"], dkb, beta)
            t_kd = _each(lambda a, b, c: _rowsum(a * b * c), dkdv, kn, cm["ekd"])
            yield
            split = _each(_split, dd)
            rows_dd = [jnp.dot(_side(hi, lo), ones, preferred_element_type=F32) for hi, lo in split]
            cols_dd = [lax.dot_general(_stack(hi, lo), ones, tn, preferred_element_type=F32) for hi, lo in split]
            yield
            dgc = _each(lambda r, c, a, b, e, f, k, tk: r - c + _rowsum(a * b * e) + _rowsum(f * k) - tk,
                        rows_dd, cols_dd, dqdv, cm["q"], cm["egc"], dkbg, cm["kbg"], t_kd)
            same_b = cm["same"].astype(BF16)
            rowi = lax.broadcasted_iota(I32, (PAIR, HEAD), 0)
            dcd = _each(lambda d: jnp.where(rowi < CHUNK, d[0:1], d[8:9]), kept(dcd_s, rows=16))
            dgl = _each(lambda tk, d, c: _mask_dot(same_b, jnp.broadcast_to(tk, (PAIR, HEAD))) + d * c, t_kd, dcd, cm["cd"])
            yield
            is_last = jnp.bitwise_and(rowi, CHUNK - 1) == CHUNK - 1
            dgc = _each(lambda a, b: a + jnp.where(is_last, b, 0.0), dgc, dgl)
            r = lax.broadcasted_iota(I32, (PAIR, PAIR), 0)
            c = lax.broadcasted_iota(I32, (PAIR, PAIR), 1)
            upper_b = (cm["same"] & (r <= c)).astype(BF16)
            _put_heads(dg_ref, _each(lambda v: _mask_dot(upper_b, v), dgc))
            yield
            _put_heads(dbeta_ref, _each(lambda a, b, c, d: jnp.broadcast_to(_rowsum(a * b) + _rowsum(c * d), (PAIR, HEAD)),
                                        dkb, kn, dvb, vs))
            _put_heads(dqn_ref, _each(lambda v: v * QK_SCALE, dq))
            _put_heads(dkn_ref, dkn)
            _put_heads(dvs_ref, _each(lambda a, b: a * b, dvb, beta))
            yield

        _interleave(recurrence(), *[factors(UNITS[pp * N_HEADS:(pp + 1) * N_HEADS]) for pp in range(INTRA_PAIRS)])

    last = n_steps - 1
    now = lambda i: (jnp.maximum(last - i, 0), 0)
    after = lambda i: (jnp.minimum(n_steps - i, last), 0)
    rows = lambda index: pl.BlockSpec((SCAN_ROWS, D_HALF), index)
    slot = lambda r, dtype: pltpu.VMEM((2, r, D_HALF), dtype)
    return _call(
        body, name="delta_bwd", grid=(n_steps + 1,),
        in_specs=[rows(now)] * 6 + [_chunk_scalar_spec(SCAN_PAIRS, now),
                                    pl.BlockSpec((2 * SCAN_PAIRS, N_HEADS, HEAD, HEAD), lambda i: (jnp.maximum(last - i, 0), 0, 0, 0))]
                 + [rows(after)] * 6,
        out_specs=[rows(after)] * 5,
        out_shape=[_sds((s, D_HALF))] * 5,
        scratch_shapes=[pltpu.VMEM((N_HEADS, HEAD, HEAD), F32), slot(SCAN_ROWS, BF16), slot(SCAN_ROWS, BF16),
                        slot(SCAN_ROWS, F32), slot(SCAN_ROWS, F32), slot(SCAN_ROWS, F32), slot(16 * SCAN_PAIRS, F32)],
        compiler_params=_params("arbitrary"),
    )(do, vn, qd, kd, w, att, cd, st, qn, kn, vs, beta, g, tm)


def _fused_call(name, n_steps, parts):
    n_in = [len(p["inputs"]) for p in parts]
    n_out = [len(p["out_shape"]) for p in parts]
    n_scr = [len(p["scratch"]) for p in parts]

    def body(*refs):
        ins, outs, scr = refs[:sum(n_in)], refs[sum(n_in):sum(n_in) + sum(n_out)], refs[sum(n_in) + sum(n_out):]
        gens, a, b, c = [], 0, 0, 0
        for p, ni, no, ns in zip(parts, n_in, n_out, n_scr):
            gens.append(p["stages"](ins[a:a + ni], outs[b:b + no], scr[c:c + ns]))
            a, b, c = a + ni, b + no, c + ns
        _interleave(*gens)

    flat = lambda key: [v for p in parts for v in p[key]]
    res = _call(
        body, name=name, grid=(n_steps,),
        in_specs=flat("in_specs"), out_specs=flat("out_specs"), out_shape=flat("out_shape"),
        scratch_shapes=flat("scratch"),
        compiler_params=_params("arbitrary"),
    )(*flat("inputs"))
    out, b = [], 0
    for no in n_out:
        out.append(res[b:b + no])
        b += no
    return out


def _pool_bwd_part(proj, dyp, pool_w, pool_scale, tile_of, n_tiles):
    s = proj.shape[0]
    t = POOL_T
    hb = t // HEAD
    last = s // HEAD - 1

    def stages(ins, outs, scratch):
        u_ref, z_ref, halo_ref, dy_ref, zn_ref, dyn_ref, pw_ref, ps_ref, band_ref, aband_ref = ins
        du_ref, dz_ref, gpw_ref, gps_ref = outs
        tile = tile_of(pl.program_id(0))

        @pl.when(pl.program_id(0) == 0)
        def _():
            gpw_ref[...] = jnp.zeros_like(gpw_ref)
            gps_ref[...] = jnp.zeros_like(gps_ref)

        live = (tile > 0).astype(F32)
        more = (tile < n_tiles - 1).astype(F32)
        groups = lambda ref: [ref[:, sl] for sl in HEAD_COLS]
        z, ps, dy = groups(z_ref), groups(ps_ref), groups(dy_ref)
        pw = [pw_ref[g] for g in HEADS]
        mix, mixed, sg, cnt = _pool_mix(groups(u_ref), [h * live for h in groups(halo_ref)], z, pw,
                                        [band_ref[g] for g in HEADS], tile * t)
        yield
        sz = _each(lambda a, b: a * b, z, sg)
        for sl, d, m, p, s_, zg in zip(HEAD_COLS, dy, mixed, ps, sg, z):
            dz_ref[:, sl] = (d * m * p * (s_ * (1.0 + zg * (1.0 - s_)))).astype(BF16)
        for sl, d, m, a in zip(HEAD_COLS, dy, mixed, sz):
            gps_ref[:, sl] += _colsum(d * m * a)
        dmixed = _each(lambda d, p, a: d * p * a, dy, ps, sz)
        yield
        for g, gp in enumerate(_each(_bdot_tn, mix, dmixed)):
            gpw_ref[g] += gp
        dmix = _each(_bdot_nt, dmixed, pw)
        yield
        dmix_n = _each(lambda d, p, zn, w_: _bdot_nt(d * more * p * (zn * _sigmoid(zn)), w_),
                       groups(dyn_ref), ps, groups(zn_ref), pw)
        yield
        scaled = [jnp.concatenate([a / c, b * (1.0 / w)], axis=0) for a, c, b, w in zip(dmix, cnt, dmix_n, WINDOWS)]
        du = _each(lambda b, s_, d: _mask_dot(b, s_) - d, [aband_ref[g] for g in HEADS], scaled, dmix)
        for sl, v in zip(HEAD_COLS, du):
            du_ref[:, sl] = v.astype(BF16)
        yield

    tile = lambda col: pl.BlockSpec((t, D_HALF), lambda i: (tile_of(i), col))
    below = lambda col: pl.BlockSpec((HEAD, D_HALF), lambda i: (jnp.minimum((tile_of(i) + 1) * hb, last), col))
    const3 = lambda shape: pl.BlockSpec(shape, lambda i: (0, 0, 0))
    return dict(
        inputs=[proj, proj, proj, dyp, proj, dyp, pool_w, pool_scale, _pool_bands(t), _pool_bands(t, anti=True)],
        in_specs=[tile(0), tile(1), pl.BlockSpec((HEAD, D_HALF), lambda i: (jnp.maximum(tile_of(i) * hb - 1, 0), 0)),
                  tile(0), below(1), below(0), const3((N_HEADS, HEAD, HEAD)), pl.BlockSpec((1, D_HALF), lambda i: (0, 0)),
                  const3((N_HEADS, t, HEAD + t)), const3((N_HEADS, t, HEAD + t))],
        out_specs=[tile(0), tile(0), const3((N_HEADS, HEAD, HEAD)), pl.BlockSpec((1, D_HALF), lambda i: (0, 0))],
        out_shape=[_sds((s, D_HALF), BF16), _sds((s, D_HALF), BF16), _sds((N_HEADS, HEAD, HEAD)), _sds((1, D_HALF))],
        scratch=[], stages=stages)


def _conv_bwd_part(proj, pre, conv_w, a_log, dt_bias, dqn, dkn, dvs, dbeta, dg, tile_of, n_tiles):
    s = proj.shape[0]
    t = CONV_T

    def stages(ins, outs, scratch):
        (q_ref, k_ref, v_ref, yq_ref, yk_ref, yv_ref, ba_ref, cw_ref, al_ref, dtb_ref,
         dqn_ref, dkn_ref, dvs_ref, dbeta_ref, dg_ref) = ins
        oq_ref, ok_ref, ov_ref, dba_ref, gcw_out, gsm_out = outs
        below, gcw_ref, gsm_ref = scratch
        step = pl.program_id(0)

        @pl.when(step == 0)
        def _():
            gcw_ref[...] = jnp.zeros_like(gcw_ref)
            gsm_ref[...] = jnp.zeros_like(gsm_ref)
            below[...] = jnp.zeros_like(below)

        parts = ((q_ref, yq_ref, dqn_ref, oq_ref), (k_ref, yk_ref, dkn_ref, ok_ref), (v_ref, yv_ref, dvs_ref, ov_ref))
        for p, (x_ref, y_ref, d_ref, o_ref) in enumerate(parts):
            for h in HEADS:
                cs = HEAD_COLS[h]
                wide = slice(p * D_HALF + h * HEAD, p * D_HALF + (h + 1) * HEAD)
                cw = cw_ref[:, wide]
                y = y_ref[:, cs]
                sg = _sigmoid(y)
                sv = y * sg
                ds = d_ref[:, cs]
                if p < 2:
                    rn = lax.rsqrt(_rowsum(sv * sv) + EPS)
                    nrm = sv * rn
                    ds = rn * (ds - nrm * _rowsum(ds * nrm))
                dy = ds * (sg * (1.0 + y * (1.0 - sg)))
                nxt = below[:, wide]
                ahead = [dy] + [_shift_up(dy, nxt, sft) for sft in range(1, CONV_K)]
                xv = x_ref[:, cs]
                acc = dy * cw[CONV_K - 1:CONV_K]
                for sft in range(1, CONV_K):
                    acc = acc + ahead[sft] * cw[CONV_K - 1 - sft:CONV_K - sft]
                for j in range(CONV_K):
                    gcw_ref[8 * j:8 * j + 8, wide] += _rows8(xv * ahead[CONV_K - 1 - j])
                o_ref[:, cs] = acc.astype(BF16)
                below[:, wide] = dy[0:8]
                yield

        ba = ba_ref[...]
        lane = lax.broadcasted_iota(I32, (t, HEAD), 1)
        lane8 = lax.broadcasted_iota(I32, (8, HEAD), 1)
        dba = jnp.zeros((t, HEAD), F32)
        gsm = jnp.zeros((8, HEAD), F32)
        for h in HEADS:
            beta = _sigmoid(ba[:, h:h + 1])
            dbeta = dbeta_ref[:, h * HEAD:h * HEAD + 1]
            xg = ba[:, N_HEADS + h:N_HEADS + h + 1] + dtb_ref[0:1, h:h + 1]
            nexp = -jnp.exp(al_ref[0:1, h:h + 1])
            dgv = dg_ref[:, h * HEAD:h * HEAD + 1]
            da = dgv * nexp * _sigmoid(xg)
            dba = dba + jnp.where(lane == h, dbeta * beta * (1.0 - beta), 0.0) + jnp.where(lane == N_HEADS + h, da, 0.0)
            gsm = (gsm + jnp.where(lane8 == h, _rows8(dgv * nexp * _softplus(xg)), 0.0)
                   + jnp.where(lane8 == N_HEADS + h, _rows8(da), 0.0))
        dba_ref[...] = jnp.zeros_like(dba_ref)
        dba_ref[:, :HEAD] = dba.astype(BF16)
        gsm_ref[...] += gsm
        yield

        @pl.when(step == n_tiles - 1)
        def _():
            gcw_out[...] = jnp.zeros_like(gcw_out)
            for j in range(CONV_K):
                gcw_out[j:j + 1, :] = _colsum(gcw_ref[8 * j:8 * j + 8, :])
            gsm_out[...] = jnp.broadcast_to(_colsum(gsm_ref[...]), (8, HEAD))

    row = pl.BlockSpec((t, D_HALF), lambda i: (tile_of(i), 0))
    const = lambda shape: pl.BlockSpec(shape, lambda i: (0, 0))
    return dict(
        inputs=[proj] * 3 + list(pre) + [proj, conv_w, a_log, dt_bias, dqn, dkn, dvs, dbeta, dg],
        in_specs=_conv_specs(t, tile_of)[:3] + [row] * 3
                 + [pl.BlockSpec((t, HEAD), lambda i: (tile_of(i), COL_BA // HEAD)),
                    const((CONV_K, 3 * D_HALF)), const((1, N_HEADS)), const((1, N_HEADS))] + [row] * 5,
        out_specs=[row, row, row, row, const((8, 3 * D_HALF)), const((8, HEAD))],
        out_shape=[_sds((s, D_HALF), BF16)] * 4 + [_sds((8, 3 * D_HALF)), _sds((8, HEAD))],
        scratch=[pltpu.VMEM((8, 3 * D_HALF), F32), pltpu.VMEM((8 * CONV_K, 3 * D_HALF), F32), pltpu.VMEM((8, HEAD), F32)],
        stages=stages)


def _pool_fwd_part(proj, pool_w, pool_scale):
    s = proj.shape[0]
    t = POOL_T
    hb = t // HEAD

    def stages(ins, outs, scratch, tile=None):
        u_ref, z_ref, halo_ref, pw_ref, ps_ref, band_ref = ins
        y_ref, = outs
        i = pl.program_id(0) if tile is None else tile
        live = (i > 0).astype(F32)
        groups = lambda ref: [ref[:, sl] for sl in HEAD_COLS]
        z = groups(z_ref)
        u, halo = groups(u_ref), [h * live for h in groups(halo_ref)]
        yield
        _, mixed, sg, _ = _pool_mix(u, halo, z, [pw_ref[g] for g in HEADS], [band_ref[g] for g in HEADS], i * t)
        yield
        for sl, m, zg, s_ in zip(HEAD_COLS, mixed, z, sg):
            y_ref[:, sl] = m * ps_ref[:, sl] * (zg * s_)
        yield

    const3 = lambda shape: pl.BlockSpec(shape, lambda i: (0, 0, 0))
    return dict(
        inputs=[proj, proj, proj, pool_w, pool_scale, _pool_bands(t)],
        in_specs=[pl.BlockSpec((t, D_HALF), lambda i: (i, 0)), pl.BlockSpec((t, D_HALF), lambda i: (i, 1)),
                  pl.BlockSpec((HEAD, D_HALF), lambda i: (jnp.maximum(i * hb - 1, 0), 0)),
                  const3((N_HEADS, HEAD, HEAD)), pl.BlockSpec((1, D_HALF), lambda i: (0, 0)), const3((N_HEADS, t, HEAD + t))],
        out_specs=[pl.BlockSpec((t, D_HALF), lambda i: (i, 0))], out_shape=[_sds((s, D_HALF))],
        scratch=[], stages=stages)


def _conv_fwd_part(proj, conv_w, a_log, dt_bias):
    s = proj.shape[0]
    t = CONV_T

    def stages(ins, outs, scratch, tile=None):
        q_ref, k_ref, v_ref, hq_ref, hk_ref, hv_ref, ba_ref, cw_ref, al_ref, dtb_ref = ins
        qn_ref, kn_ref, vs_ref, beta_ref, g_ref, yq_ref, yk_ref, yv_ref = outs
        live = ((pl.program_id(0) if tile is None else tile) > 0).astype(F32)
        parts = ((q_ref, hq_ref, qn_ref, yq_ref), (k_ref, hk_ref, kn_ref, yk_ref), (v_ref, hv_ref, vs_ref, yv_ref))
        for p, (x_ref, h_ref, o_ref, y_ref) in enumerate(parts):
            for h in HEADS:
                cs = HEAD_COLS[h]
                taps = _conv_taps(x_ref[:, cs], h_ref[:, cs] * live)
                y = _conv_pre(taps, cw_ref[:, p * D_HALF + h * HEAD:p * D_HALF + (h + 1) * HEAD])
                y_ref[:, cs] = y
                sv = y * _sigmoid(y)
                o_ref[:, cs] = sv if p == 2 else sv * lax.rsqrt(_rowsum(sv * sv) + EPS)
                yield
        ba = ba_ref[...]
        for h in HEADS:
            beta = _sigmoid(ba[:, h:h + 1])
            gl = -jnp.exp(al_ref[0:1, h:h + 1]) * _softplus(ba[:, N_HEADS + h:N_HEADS + h + 1] + dtb_ref[0:1, h:h + 1])
            beta_ref[:, HEAD_COLS[h]] = jnp.broadcast_to(beta, (t, HEAD))
            g_ref[:, HEAD_COLS[h]] = jnp.broadcast_to(gl, (t, HEAD))
        yield

    row = pl.BlockSpec((t, D_HALF), lambda i: (i, 0))
    const = lambda shape: pl.BlockSpec(shape, lambda i: (0, 0))
    return dict(
        inputs=[proj] * 7 + [conv_w, a_log, dt_bias],
        in_specs=_conv_specs(t) + [pl.BlockSpec((t, HEAD), lambda i: (i, COL_BA // HEAD)),
                                   const((CONV_K, 3 * D_HALF)), const((1, N_HEADS)), const((1, N_HEADS))],
        out_specs=[row] * 8, out_shape=[_sds((s, D_HALF))] * 8, scratch=[], stages=stages)


def _front_fwd(x, norm_w, w_pad, conv_w, a_log, dt_bias, pool_w, pool_scale):
    s = x.shape[0]
    t = CONV_T
    n_tiles = s // t
    assert POOL_T == CONV_T
    like_proj = _sds((s, N_IN_PAD))
    conv = _conv_fwd_part(like_proj, conv_w, a_log, dt_bias)
    pool = _pool_fwd_part(like_proj, pool_w, pool_scale)
    bands = pool["inputs"][-1]
    mxu_n = 256
    col_bounds = list(range(0, N_IN_PAD, 3 * mxu_n)) + [N_IN_PAD]

    def body(x_ref, nw_ref, w_ref, cw_ref, al_ref, dtb_ref, pw_ref, ps_ref, band_ref,
             proj_ref, nt_ref, qn_ref, kn_ref, vs_ref, beta_ref, g_ref, yq_ref, yk_ref, yv_ref, y_ref, prev):
        i = pl.program_id(0)

        @pl.when(i == 0)
        def _():
            prev[...] = jnp.zeros_like(prev)

        tile = jnp.maximum(i - 1, 0)
        main, above8, above = pl.ds(HEAD, t), pl.ds(HEAD - 8, 8), pl.ds(0, HEAD)
        cols = lambda rows, c0, width=D_HALF: prev.at[rows, pl.ds(c0, width)]
        conv_ins = (cols(main, 2 * D_HALF), cols(main, 3 * D_HALF), cols(main, 4 * D_HALF),
                    cols(above8, 2 * D_HALF), cols(above8, 3 * D_HALF), cols(above8, 4 * D_HALF),
                    cols(main, COL_BA, HEAD), cw_ref, al_ref, dtb_ref)
        pool_ins = (cols(main, 0), cols(main, D_HALF), cols(above, 0), pw_ref, ps_ref, band_ref)

        def projection():
            xv = x_ref[...]
            r = lax.rsqrt(jnp.mean(xv * xv, axis=-1, keepdims=True) + EPS)
            nv = xv * r * nw_ref[...]
            nt_ref[...] = nv.T.astype(BF16)
            nb = nv.astype(BF16)
            yield
            for lo, hi in zip(col_bounds[:-1], col_bounds[1:]):
                proj_ref[:, lo:hi] = jnp.dot(nb, w_ref[:, lo:hi], preferred_element_type=F32)
                yield

        _interleave(projection(),
                    conv["stages"](conv_ins, (qn_ref, kn_ref, vs_ref, beta_ref, g_ref, yq_ref, yk_ref, yv_ref), (), tile),
                    pool["stages"](pool_ins, (y_ref,), (), tile))
        prev[0:HEAD] = prev[t:t + HEAD]
        prev[HEAD:HEAD + t] = proj_ref[...]

    last = n_tiles - 1
    now = lambda i: (jnp.minimum(i, last), 0)
    before = lambda i: (jnp.maximum(i - 1, 0), 0)
    const = lambda a: pl.BlockSpec(a.shape, lambda i: (0,) * a.ndim)
    half = pl.BlockSpec((t, D_HALF), before)
    return _call(
        body, name="front_fwd", grid=(n_tiles + 1,),
        in_specs=[pl.BlockSpec((t, D_MODEL), now), const(norm_w), const(w_pad), const(conv_w), const(a_log), const(dt_bias),
                  const(pool_w), const(pool_scale), const(bands)],
        out_specs=[pl.BlockSpec((t, N_IN_PAD), now), pl.BlockSpec((D_MODEL, t), lambda i: (0, jnp.minimum(i, last)))]
                  + [half] * 9,
        out_shape=[_sds((s, N_IN_PAD)), _sds((D_MODEL, s), BF16)] + [_sds((s, D_HALF))] * 9,
        scratch_shapes=[pltpu.VMEM((HEAD + t, N_IN_PAD), F32)],
        compiler_params=_params("arbitrary"),
    )(x, norm_w, w_pad, conv_w, a_log, dt_bias, pool_w, pool_scale, bands)


def _conv_pool_bwd(proj, pre, conv_w, a_log, dt_bias, dqn, dkn, dvs, dbeta, dg, dyp, pool_w, pool_scale):
    n_tiles = proj.shape[0] // CONV_T
    assert POOL_T == CONV_T
    tile_of = lambda i: n_tiles - 1 - i
    return _fused_call("conv_pool_bwd", n_tiles, [
        _conv_bwd_part(proj, pre, conv_w, a_log, dt_bias, dqn, dkn, dvs, dbeta, dg, tile_of, n_tiles),
        _pool_bwd_part(proj, dyp, pool_w, pool_scale, tile_of, n_tiles)])


def _rows8(x):
    acc = x[0:8]
    for r in range(8, x.shape[0], 8):
        acc = acc + x[r:r + 8]
    return acc


IN_T = 512


def _in_bwd(x, dh, norm_w, w_pad, pieces):
    s = x.shape[0]
    t = IN_T
    widths = [D_HALF] * 6 + [N_IN_PAD - COL_BA]

    def body(*refs):
        x_ref, dh_ref, nw_ref, w_ref = refs[:4]
        p_refs = refs[4:4 + len(pieces)]
        gx_ref, gnw_ref = refs[4 + len(pieces):]

        @pl.when(pl.program_id(0) == 0)
        def _():
            gnw_ref[...] = jnp.zeros_like(gnw_ref)

        dn = jnp.zeros((t, D_MODEL), F32)
        col = 0
        for p_ref, wd in zip(p_refs, widths):
            dn = dn + _bdot_nt(p_ref[...], w_ref[:, col:col + wd])
            col += wd
        xv = x_ref[...]
        r = lax.rsqrt(jnp.mean(xv * xv, axis=-1, keepdims=True) + EPS)
        xhat = xv * r
        gnw_ref[...] += _colsum(dn * xhat)
        dxh = dn * nw_ref[...]
        gx_ref[...] = dh_ref[...] + r * (dxh - xhat * jnp.mean(dxh * xhat, axis=-1, keepdims=True))

    wide = pl.BlockSpec((t, D_MODEL), lambda i: (i, 0))
    return _call(
        body, name="in_bwd", grid=(s // t,),
        in_specs=[wide, wide, pl.BlockSpec((1, D_MODEL), lambda i: (0, 0)),
                  pl.BlockSpec((D_MODEL, N_IN_PAD), lambda i: (0, 0))]
                 + [pl.BlockSpec((t, wd), lambda i: (i, 0)) for wd in widths],
        out_specs=[wide, pl.BlockSpec((1, D_MODEL), lambda i: (0, 0))],
        out_shape=[_sds((s, D_MODEL)), _sds((1, D_MODEL))],
        compiler_params=_params("arbitrary"),
    )(x, dh, norm_w, w_pad, *pieces)


def _adamw_shard(name, w, g_own, g_got, cidx, m, v):
    _, r, c = w.shape
    half = r // 2
    rows = 256 if half % 256 == 0 else half
    per_half = half // rows

    def body(c_ref, w_ref, go_ref, gg_ref, m_ref, v_ref, gout_ref, d_ref, nm_ref, nv_ref):
        mine = (pl.program_id(0) // per_half) == c_ref[0]
        gv = jnp.where(mine, go_ref[:, :c], gg_ref[:, :c])
        gout_ref[0] = gv
        mn = ADAM_B1 * m_ref[0] + (1.0 - ADAM_B1) * gv
        vn = ADAM_B2 * v_ref[0] + (1.0 - ADAM_B2) * (gv * gv)
        m_hat = mn / (1.0 - ADAM_B1 ** ADAM_STEP)
        v_hat = vn / (1.0 - ADAM_B2 ** ADAM_STEP)
        d_ref[0] = -ADAM_LR * (m_hat / (jnp.sqrt(v_hat) + ADAM_EPS) + ADAM_WD * w_ref[0])
        nm_ref[0] = mn
        nv_ref[0] = vn

    blk = pl.BlockSpec((1, rows, c), lambda i, c_ref: (0, i, 0))
    gblk = pl.BlockSpec((rows, g_own.shape[1]), lambda i, c_ref: (i % per_half, 0))
    return _call(
        body, name=name,
        grid_spec=pltpu.PrefetchScalarGridSpec(
            num_scalar_prefetch=1, grid=(2 * per_half,),
            in_specs=[blk, gblk, gblk, blk, blk], out_specs=[blk] * 4),
        out_shape=[_sds((1, r, c))] * 4,
        compiler_params=_params("arbitrary"),
    )(cidx, w, g_own, g_got, m, v)


def _adamw_tiles(name, w, g, m, v):
    n = w.shape[0]
    nb = 77 if n % 77 == 0 else n

    def body(w_ref, g_ref, m_ref, v_ref, d_ref, nm_ref, nv_ref):
        gv = g_ref[...]
        mn = ADAM_B1 * m_ref[...] + (1.0 - ADAM_B1) * gv
        vn = ADAM_B2 * v_ref[...] + (1.0 - ADAM_B2) * (gv * gv)
        m_hat = mn / (1.0 - ADAM_B1 ** ADAM_STEP)
        v_hat = vn / (1.0 - ADAM_B2 ** ADAM_STEP)
        d_ref[...] = -ADAM_LR * (m_hat / (jnp.sqrt(v_hat) + ADAM_EPS) + ADAM_WD * w_ref[...])
        nm_ref[...] = mn
        nv_ref[...] = vn

    blk = pl.BlockSpec((nb, 8, HEAD), lambda i: (i, 0, 0))
    return _call(
        body, name=name, grid=(n // nb,),
        in_specs=[blk] * 4, out_specs=[blk] * 3, out_shape=[_sds(w.shape)] * 3,
        compiler_params=_params("arbitrary"),
    )(w, g, m, v)


def _exchange(name, inputs, out_shapes, phases):
    n_in = len(inputs)
    n_out = len(out_shapes)
    n_cp = sum(len(p) for p in phases)

    def body(*refs):
        ins, outs = refs[:n_in], refs[n_in:n_in + n_out]
        send, recv = refs[n_in + n_out:]
        pos = (lax.axis_index("x"), lax.axis_index("y"), lax.axis_index("c"))
        k = 0
        for phase in phases:
            cps = []
            for src, dst, target in phase:
                cps.append(pltpu.make_async_remote_copy(
                    src_ref=src(ins, outs, pos), dst_ref=dst(ins, outs, pos), send_sem=send.at[k], recv_sem=recv.at[k],
                    device_id=target(pos), device_id_type=pl.DeviceIdType.MESH))
                k += 1
            for cp in cps:
                cp.start()
            for cp in cps:
                cp.wait()

    anyspec = pl.BlockSpec(memory_space=pl.ANY)
    return _call(
        body, name=name,
        in_specs=[anyspec] * n_in, out_specs=[anyspec] * n_out, out_shape=list(out_shapes),
        scratch_shapes=[pltpu.SemaphoreType.DMA((n_cp,)), pltpu.SemaphoreType.DMA((n_cp,))],
    )(*inputs)


def _exchange_start(name, inputs, out_shapes, copies):
    n_in, n_out, n_cp = len(inputs), len(out_shapes), len(copies)

    def body(*refs):
        ins, lands = refs[:n_in], refs[n_in:n_in + n_out]
        sems = refs[n_in + n_out:n_in + n_out + 2 * n_cp]
        token = refs[-1]
        pos = (lax.axis_index("x"), lax.axis_index("y"), lax.axis_index("c"))
        for k, (src, dst, target) in enumerate(copies):
            pltpu.make_async_remote_copy(
                src_ref=src(ins, lands, pos), dst_ref=dst(ins, lands, pos), send_sem=sems[2 * k], recv_sem=sems[2 * k + 1],
                device_id=target(pos), device_id_type=pl.DeviceIdType.MESH).start()
        token[...] = jnp.zeros_like(token)

    hbm = pl.BlockSpec(memory_space=pltpu.HBM)
    sem = pl.BlockSpec(memory_space=pltpu.SEMAPHORE)
    bufs = list(inputs) + [lax.empty(o.shape, o.dtype) for o in out_shapes]
    outs = _call(
        body, name=name,
        out_shape=tuple([pltpu.SemaphoreType.DMA(())] * (2 * n_cp) + [pltpu.HBM(b.shape, b.dtype) for b in bufs]
                        + [_sds((8, HEAD))]),
        in_specs=[hbm] * len(bufs),
        out_specs=tuple([sem] * (2 * n_cp) + [hbm] * len(bufs) + [pl.BlockSpec(memory_space=pltpu.VMEM)]),
        input_output_aliases={i: 2 * n_cp + i for i in range(len(bufs))},
        compiler_params=pltpu.CompilerParams(has_side_effects=pltpu.SideEffectType.DATAFLOW_SIDE_EFFECTING),
    )(*[pltpu.with_memory_space_constraint(b, pltpu.HBM) for b in bufs])
    return outs[:2 * n_cp], outs[2 * n_cp:2 * n_cp + n_in], outs[2 * n_cp + n_in:-1], outs[-1]


def _exchange_wait(name, sems, sources, lands, copies, after):
    n_in, n_out, n_cp = len(sources), len(lands), len(copies)

    def body(*refs):
        ins, zones = refs[:n_in], refs[n_in:n_in + n_out]
        sem_refs = refs[n_in + n_out:n_in + n_out + 2 * n_cp]
        pos = (lax.axis_index("x"), lax.axis_index("y"), lax.axis_index("c"))
        for k, (src, dst, target) in enumerate(copies):
            cp = pltpu.make_async_remote_copy(
                src_ref=src(ins, zones, pos), dst_ref=dst(ins, zones, pos), send_sem=sem_refs[2 * k],
                recv_sem=sem_refs[2 * k + 1], device_id=target(pos), device_id_type=pl.DeviceIdType.MESH)
            cp.wait_send()
            cp.wait_recv()

    hbm = pl.BlockSpec(memory_space=pltpu.HBM)
    sem = pl.BlockSpec(memory_space=pltpu.SEMAPHORE)
    bufs = list(sources) + list(lands)
    outs = _call(
        body, name=name,
        out_shape=tuple(pltpu.HBM(b.shape, b.dtype) for b in bufs),
        in_specs=[hbm] * len(bufs) + [sem] * (2 * n_cp) + [pl.BlockSpec(memory_space=pl.ANY)],
        out_specs=tuple([hbm] * len(bufs)),
        input_output_aliases={i: i for i in range(len(bufs))},
        compiler_params=pltpu.CompilerParams(has_side_effects=pltpu.SideEffectType.DATAFLOW_SIDE_EFFECTING),
    )(*bufs, *sems, after)
    return outs[:n_in], outs[n_in:]


def _allreduce_tile(name, v):
    def body(v_ref, out_ref, slots, send, recv):
        x, y, c = lax.axis_index("x"), lax.axis_index("y"), lax.axis_index("c")
        me = 4 * x + 2 * y + c
        slots[me] = v_ref[...]
        cps = []
        for k in range(1, 8):
            peer = (x ^ (k >> 2), y ^ ((k >> 1) & 1), c ^ (k & 1))
            cps.append(pltpu.make_async_remote_copy(
                src_ref=v_ref, dst_ref=slots.at[me], send_sem=send.at[k - 1], recv_sem=recv.at[k - 1],
                device_id=peer, device_id_type=pl.DeviceIdType.MESH))
        for cp in cps:
            cp.start()
        for cp in cps:
            cp.wait()
        acc = slots[0]
        for i in range(1, 8):
            acc = acc + slots[i]
        out_ref[...] = acc

    vm = pl.BlockSpec(memory_space=pltpu.VMEM)
    return _call(
        body, name=name, in_specs=[vm], out_specs=vm, out_shape=_sds(v.shape),
        scratch_shapes=[pltpu.VMEM((8,) + v.shape, F32), pltpu.SemaphoreType.DMA((7,)), pltpu.SemaphoreType.DMA((7,))],
    )(v)


def _chip(pos):
    return 2 * pos[0] + pos[1]


def _other_chip(pos, mask):
    x, y, c = pos
    return (x ^ (mask >> 1), y ^ (mask & 1), c)


def _sibling(pos):
    return (pos[0], pos[1], 1 - pos[2])


def _gather_weights(wb, cb):
    rows = wb.shape[0] // 2
    x_nb, y_nb, diag = CHIP_MASKS

    def part(pos, mask, quarter=None):
        start = pos[2] * rows if quarter is None else pos[2] * rows + quarter * (rows // 2)
        return lambda outs: outs[0].at[_chip(pos) ^ mask, pl.ds(start, rows if quarter is None else rows // 2)]

    def passed_on(mask, to, quarter=None):
        return (lambda ins, outs, pos: part(pos, mask, quarter)(outs), lambda ins, outs, pos: part(pos, mask, quarter)(outs), to)

    first = [(lambda ins, outs, pos: ins[0].at[pl.ds(pos[2] * rows, rows)], lambda ins, outs, pos: part(pos, 0)(outs),
              functools.partial(_other_chip, mask=mask)) for mask in (x_nb, y_nb)]
    first += [(lambda ins, outs, pos: ins[1], lambda ins, outs, pos: outs[1].at[_chip(pos)],
               functools.partial(_other_chip, mask=mask)) for mask in CHIP_MASKS]
    second = [passed_on(x_nb, functools.partial(_other_chip, mask=y_nb), quarter=0),
              passed_on(y_nb, functools.partial(_other_chip, mask=x_nb), quarter=1),
              passed_on(x_nb, _sibling), passed_on(y_nb, _sibling)]
    third = [passed_on(diag, _sibling)]
    return _exchange("gather_weights", [wb, cb], [_sds((4,) + wb.shape, wb.dtype), _sds((4,) + cb.shape, cb.dtype)],
                     [first, second, third])


def _assemble_w_in(gw, wb, jidx):
    m = gw.shape[1]

    def body(j_ref, g_ref, wb_ref, o_ref):
        step = pl.program_id(0)

        @pl.when(step == 0)
        def _():
            o_ref[...] = jnp.zeros_like(o_ref)

        blk = jnp.where(step == j_ref[0], wb_ref[...], g_ref[0]).astype(F32)
        lane = lax.broadcasted_iota(I32, (m, BLK_IN_PAD), 1)
        for j in range(4):
            @pl.when(step == j)
            def _(j=j):
                base = j * BLK_IN // HEAD * HEAD
                shift = j * BLK_IN - base
                moved = pltpu.roll(blk, shift, 1) if shift else blk
                window = o_ref[:, base:base + BLK_IN_PAD].astype(F32)
                mine = (lane >= shift) & (lane < shift + BLK_IN)
                o_ref[:, base:base + BLK_IN_PAD] = jnp.where(mine, moved, window).astype(BF16)

    return _call(
        body, name="assemble_w_in",
        grid_spec=pltpu.PrefetchScalarGridSpec(
            num_scalar_prefetch=1, grid=(4,),
            in_specs=[pl.BlockSpec((1, m, BLK_IN_PAD), lambda j, j_ref: (j, 0, 0)),
                      pl.BlockSpec((m, BLK_IN_PAD), lambda j, j_ref: (0, 0))],
            out_specs=pl.BlockSpec((m, N_IN_PAD), lambda j, j_ref: (0, 0))),
        out_shape=_sds((m, N_IN_PAD), BF16),
        compiler_params=_params("arbitrary"),
    )(jidx, gw, wb)


def _gather_blocks(ob):
    copies = [(lambda ins, outs, pos: ins[0], lambda ins, outs, pos: outs[0].at[_chip(pos)],
               functools.partial(_other_chip, mask=mask)) for mask in CHIP_MASKS]
    return [_sds((4,) + ob.shape, ob.dtype)], copies


def _reduce_sibling(name, arrays, cidx):
    n = len(arrays)
    halves = [a.shape[:-2] + (a.shape[-2] // 2, a.shape[-1]) for a in arrays]
    pieces = [(a, j) for a in range(n) for j in (range(arrays[a].shape[0]) if arrays[a].ndim == 3 else [None])]

    def body(c_ref, *refs):
        del c_ref
        whole, own, outs, land = refs[:n], refs[n:2 * n], refs[2 * n:3 * n], refs[3 * n:4 * n]
        send, recv = refs[4 * n:]
        pos = (lax.axis_index("x"), lax.axis_index("y"), lax.axis_index("c"))
        cps = []
        for k, (a, j) in enumerate(pieces):
            rows = pl.ds((1 - pos[2]) * halves[a][-2], halves[a][-2])
            cps.append(pltpu.make_async_remote_copy(
                src_ref=whole[a].at[rows] if j is None else whole[a].at[j, rows],
                dst_ref=land[a] if j is None else land[a].at[j],
                send_sem=send.at[k], recv_sem=recv.at[k],
                device_id=_sibling(pos), device_id_type=pl.DeviceIdType.MESH))
        for cp in cps:
            cp.start()
        for cp, (a, j) in zip(cps, pieces):
            cp.wait()
            at = Ellipsis if j is None else j
            outs[a][at] = (own[a][at].astype(F32) + land[a][at].astype(F32)).astype(outs[a].dtype)

    def mine(shape):
        if len(shape) == 3:
            return pl.BlockSpec(shape, lambda i, c_ref: (0, c_ref[0], 0))
        return pl.BlockSpec(shape, lambda i, c_ref: (c_ref[0], 0))

    return _call(
        body, name=name,
        grid_spec=pltpu.PrefetchScalarGridSpec(
            num_scalar_prefetch=1, grid=(1,),
            in_specs=[pl.BlockSpec(memory_space=pl.ANY)] * n + [mine(h) for h in halves],
            out_specs=[pl.BlockSpec(h, lambda i, c_ref, nd=len(h): (0,) * nd) for h in halves],
            scratch_shapes=[pltpu.VMEM(h, a.dtype) for h, a in zip(halves, arrays)]
                           + [pltpu.SemaphoreType.DMA((len(pieces),)), pltpu.SemaphoreType.DMA((len(pieces),))]),
        out_shape=[_sds(h, a.dtype) for h, a in zip(halves, arrays)],
        compiler_params=_params("arbitrary"),
    )(cidx, *arrays, *arrays)


def _to_other_chips(arrays, blocked):
    def src(ins, outs, pos, a, mask):
        return ins[a].at[_chip(pos) ^ mask] if blocked[a] else ins[a]

    outs = [_sds((3,) + (a.shape[1:] if b else a.shape), a.dtype) for a, b in zip(arrays, blocked)]
    copies = []
    for mi, mask in enumerate(CHIP_MASKS):
        for a in range(len(arrays)):
            copies.append((functools.partial(src, a=a, mask=mask), lambda ins, outs, pos, a=a, mi=mi: outs[a].at[mi],
                           functools.partial(_other_chip, mask=mask)))
    return outs, copies


def _sum_chips_swap(name, owns, gots, jidx, blocked):
    n = len(owns)
    shapes = [g.shape[-2:] for g in gots]
    own3 = [o if b else o.reshape((1,) + o.shape) for o, b in zip(owns, blocked)]

    def body(j_ref, *refs):
        del j_ref
        own, got, mine, theirs = refs[:n], refs[n:2 * n], refs[2 * n:3 * n], refs[3 * n:4 * n]
        send, recv = refs[4 * n:]
        pos = (lax.axis_index("x"), lax.axis_index("y"), lax.axis_index("c"))
        cps = []
        for a in range(n):
            mine[a][...] = ((own[a][0].astype(F32) + got[a][0].astype(F32))
                            + (got[a][1].astype(F32) + got[a][2].astype(F32)))
            cps.append(pltpu.make_async_remote_copy(
                src_ref=mine[a], dst_ref=theirs[a], send_sem=send.at[a], recv_sem=recv.at[a],
                device_id=_sibling(pos), device_id_type=pl.DeviceIdType.MESH))
            cps[-1].start()
        for cp in cps:
            cp.wait()

    own_spec = lambda s, b: pl.BlockSpec((1,) + s, (lambda i, j_ref: (j_ref[0], 0, 0)) if b else (lambda i, j_ref: (0, 0, 0)))
    whole = lambda s: pl.BlockSpec(s, lambda i, j_ref: (0, 0))
    outs = _call(
        body, name=name,
        grid_spec=pltpu.PrefetchScalarGridSpec(
            num_scalar_prefetch=1, grid=(1,),
            in_specs=[own_spec(s, b) for s, b in zip(shapes, blocked)]
                     + [pl.BlockSpec((3,) + s, lambda i, j_ref: (0, 0, 0)) for s in shapes],
            out_specs=[whole(s) for s in shapes] * 2,
            scratch_shapes=[pltpu.SemaphoreType.DMA((n,)), pltpu.SemaphoreType.DMA((n,))]),
        out_shape=[_sds(s) for s in shapes] * 2,
        compiler_params=_params("arbitrary"),
    )(jidx, *own3, *gots)
    return outs[:n], outs[n:]


def _local_step(x, target, w_pad, w_out, conv_w, norm_w, pool_w, pool_scale, a_log, dt_bias, dn_norm_w, final_norm_w):
    proj, n_t, qn, kn, vs, beta, g, yq, yk, yv, y_pool = _front_fwd(x, norm_w, w_pad, conv_w, a_log, dt_bias, pool_w, pool_scale)
    w, att, qd, kd, tm, cd, o, vn, st = _delta_fwd(qn, kn, vs, beta, g)
    w_out = w_out(o) if callable(w_out) else w_out
    g_wout, dh, dyp, do, ddz, loss, g_fnw, g_dnw = _out_fwd_bwd(x, y_pool, o, proj, target, w_out, dn_norm_w, final_norm_w)
    dqn, dkn, dvs, dbeta, dg = _delta_bwd(do, vn, qd, kd, w, att, cd, st, qn, kn, vs, beta, g, tm)
    (dcq, dck, dcv, dba, g_cw, g_sm), (dpu, dpz, g_pw, g_ps) = _conv_pool_bwd(
        proj, (yq, yk, yv), conv_w, a_log, dt_bias, dqn, dkn, dvs, dbeta, dg, dyp, pool_w, pool_scale)
    pieces = [dpu, dpz, dcq, dck, dcv, ddz, dba]
    g_win = _grad_w_in(n_t, pieces)
    small = dict(norm_w=jnp.zeros_like(norm_w), pool_w=g_pw, pool_scale=g_ps, conv_w=g_cw[:CONV_K],
                 a_log=g_sm[0:1, 0:N_HEADS], dt_bias=g_sm[0:1, N_HEADS:2 * N_HEADS], dn_norm_w=g_dnw, final_norm_w=g_fnw)
    return loss[0, 0], g_win, g_wout, small, dh, pieces


SMALL_LAYOUT = (("pool_w", 512, HEAD, (1, N_HEADS, HEAD, HEAD)), ("final_norm_w", 8, HEAD, (D_MODEL,)),
                ("pool_scale", 4, HEAD, (1, D_HALF)), ("conv_w", 48, HEAD, (1, CONV_K, 3 * D_HALF)),
                ("dn_norm_w", 1, HEAD, (1, HEAD)), ("a_log", 1, N_HEADS, (1, N_HEADS)), ("dt_bias", 1, N_HEADS, (1, N_HEADS)),
                ("loss", 1, 1, ()))


def _small_offsets():
    offs, r = {}, 0
    for name, rows, _, _ in SMALL_LAYOUT:
        offs[name] = r
        r += -(-rows // 8) * 8
    assert r <= SMALL_ROWS
    return offs


def _pack_small(t):
    parts = []
    for name, rows, lanes, _ in SMALL_LAYOUT:
        a = t.get(name, jnp.zeros((1,), F32)).reshape(rows, lanes)
        parts.append(jnp.pad(a, ((0, -(-rows // 8) * 8 - rows), (0, HEAD - lanes))))
    buf = jnp.concatenate(parts, axis=0)
    return jnp.pad(buf, ((0, SMALL_ROWS - buf.shape[0]), (0, 0)))


def _adamw_small(w, g_own, g_got, cidx, m, v):
    offs = _small_offsets()
    names = [e[0] for e in SMALL_LAYOUT]
    n = len(names)

    def body(c_ref, w_ref, go_ref, gg_ref, m_ref, v_ref, *outs):
        own_low = c_ref[0] == 0
        gv = jnp.concatenate([jnp.where(own_low, go_ref[...], gg_ref[...]), jnp.where(own_low, gg_ref[...], go_ref[...])], axis=0)
        mn = ADAM_B1 * m_ref[...] + (1.0 - ADAM_B1) * gv
        vn = ADAM_B2 * v_ref[...] + (1.0 - ADAM_B2) * (gv * gv)
        m_hat = mn / (1.0 - ADAM_B1 ** ADAM_STEP)
        v_hat = vn / (1.0 - ADAM_B2 ** ADAM_STEP)
        dl = -ADAM_LR * (m_hat / (jnp.sqrt(v_hat) + ADAM_EPS) + ADAM_WD * w_ref[...])
        for kind, arr in enumerate((gv, dl, mn, vn)):
            for i, (name, rows, lanes, _) in enumerate(SMALL_LAYOUT):
                outs[kind * n + i][...] = arr[offs[name]:offs[name] + rows, :lanes]

    whole = lambda shape: pl.BlockSpec(shape, lambda i, c_ref: (0,) * len(shape))
    out_shapes = [_sds((rows, lanes)) for _, rows, lanes, _ in SMALL_LAYOUT] * 4
    res = _call(
        body, name="adamw_small",
        grid_spec=pltpu.PrefetchScalarGridSpec(
            num_scalar_prefetch=1, grid=(1,),
            in_specs=[whole(w.shape), whole(g_own.shape), whole(g_got.shape), whole(m.shape), whole(v.shape)],
            out_specs=[whole(o.shape) for o in out_shapes]),
        out_shape=out_shapes,
        compiler_params=_params("arbitrary"),
    )(cidx, w, g_own, g_got, m, v)
    return [{name: res[kind * n + i].reshape(shape) for i, (name, _, _, shape) in enumerate(SMALL_LAYOUT)}
            for kind in range(4)]


def kernel(x, norm_w, w_in, pool_w, pool_scale, conv_w, a_log, dt_bias, dn_norm_w, w_out, final_norm_w, loss_target, m_norm_w, m_w_in, m_pool_w, m_pool_scale, m_conv_w, m_a_log, m_dt_bias, m_dn_norm_w, m_w_out, m_final_norm_w, v_norm_w, v_w_in, v_pool_w, v_pool_scale, v_conv_w, v_a_log, v_dt_bias, v_dn_norm_w, v_w_out, v_final_norm_w):
    cidx = lax.axis_index("c").astype(I32).reshape(1)
    jidx = (2 * lax.axis_index("x") + lax.axis_index("y")).astype(I32)

    wb = jnp.pad(w_in[0].astype(BF16), ((0, 0), (0, BLK_IN_PAD - BLK_IN)))
    ob = w_out[0].astype(BF16)
    gw, gc = _gather_weights(wb, conv_w[0])
    mine = lambda j: jidx == j
    w_pad = _assemble_w_in(gw, wb, jidx.reshape(1))
    cw_full = jnp.concatenate([jnp.where(mine(j), conv_w[0], gc[j]) for j in range(4)], axis=1)

    lands_o, copies_o = _gather_blocks(ob)
    sems_o, ob_thru, zones_o, token_o = _exchange_start("gather_w_out_start", [ob], lands_o, copies_o)

    def w_out_full(after):
        (own,), (got,) = _exchange_wait("gather_w_out_wait", sems_o, ob_thru, zones_o, copies_o, after)
        return jnp.where((jnp.arange(4) == jidx)[:, None, None], own[None], got).reshape(D_MODEL, D_MODEL)

    loss, g_win, g_wout, small, dh, pieces = _local_step(
        x[0], loss_target[0], w_pad, w_out_full, cw_full, norm_w + token_o[0, 0], pool_w[0], pool_scale, a_log, dt_bias,
        dn_norm_w, final_norm_w.reshape(1, D_MODEL))
    small["loss"] = loss

    blocks_out = g_wout.reshape(4, BLK_OUT, D_MODEL)
    full = [g_win, blocks_out, _pack_small(small)]
    chip_sum = _reduce_sibling("reduce_sibling", full, cidx)
    blocked = [True, True, False]
    lands, copies = _to_other_chips(chip_sum, blocked)
    sems, chip_sum, zones, token = _exchange_start("reduce_chips_start", chip_sum, lands, copies)
    gx, g_nw = _in_bwd(x[0], dh, norm_w + token[0, 0], w_pad, pieces)
    g_nw = _allreduce_tile("reduce_norm_w", g_nw.reshape(8, HEAD)).reshape(1, D_MODEL)
    chip_sum, from_chips = _exchange_wait("reduce_chips_wait", sems, chip_sum, zones, copies, gx)
    halves, other_halves = _sum_chips_swap("sum_chips_swap", chip_sum, from_chips, jidx.reshape(1), blocked)

    weights = dict(norm_w=norm_w, w_in=w_in, pool_w=pool_w, pool_scale=pool_scale, conv_w=conv_w, a_log=a_log,
                   dt_bias=dt_bias, dn_norm_w=dn_norm_w, w_out=w_out, final_norm_w=final_norm_w)
    ms = dict(norm_w=m_norm_w, w_in=m_w_in, pool_w=m_pool_w, pool_scale=m_pool_scale, conv_w=m_conv_w, a_log=m_a_log,
              dt_bias=m_dt_bias, dn_norm_w=m_dn_norm_w, w_out=m_w_out, final_norm_w=m_final_norm_w)
    vs = dict(norm_w=v_norm_w, w_in=v_w_in, pool_w=v_pool_w, pool_scale=v_pool_scale, conv_w=v_conv_w, a_log=v_a_log,
              dt_bias=v_dt_bias, dn_norm_w=v_dn_norm_w, w_out=v_w_out, final_norm_w=v_final_norm_w)
    names = ["norm_w", "w_in", "pool_w", "pool_scale", "conv_w", "a_log", "dt_bias", "dn_norm_w", "w_out", "final_norm_w"]
    small_names = [n for n in names if n not in ("w_in", "w_out")]

    def pack(t):
        conv = lax.dynamic_update_slice_in_dim(jnp.zeros((CONV_K, 3 * D_HALF), F32), t["conv_w"][0], jidx * BLK_CONV, axis=1)
        return _pack_small({**{n: t[n] for n in small_names if n != "conv_w"}, "conv_w": conv})

    results = [{}, {}, {}, {}]
    to_tiles = lambda a: jnp.transpose(a, (2, 0, 1)).reshape(BLK_IN, 8, HEAD)
    from_tiles = lambda a: jnp.transpose(a, (1, 2, 0)).reshape(1, D_MODEL, BLK_IN)
    lo = jnp.where(cidx[0] == 0, halves[0], other_halves[0])
    hi = jnp.where(cidx[0] == 0, other_halves[0], halves[0])
    g_tiles = jnp.concatenate([lo[:, :BLK_IN].T, hi[:, :BLK_IN].T], axis=1).reshape(BLK_IN, 8, HEAD)
    outs = _adamw_tiles("adamw_w_in", to_tiles(w_in), g_tiles, to_tiles(m_w_in), to_tiles(v_w_in))
    for res, o in zip(results, (g_tiles,) + tuple(outs)):
        res["w_in"] = from_tiles(o)
    outs = _adamw_shard("adamw_w_out", w_out, halves[1], other_halves[1], cidx, m_w_out, v_w_out)
    for res, o in zip(results, outs):
        res["w_out"] = o
    outs = _adamw_small(pack(weights), halves[2], other_halves[2], cidx, pack(ms), pack(vs))
    for res, got in zip(results, outs):
        got["conv_w"] = lax.dynamic_slice_in_dim(got["conv_w"], jidx * BLK_CONV, BLK_CONV, axis=2)
        res.update(got)
    one_tile = lambda a: a.reshape(1, 8, HEAD)
    outs = _adamw_tiles("adamw_norm_w", one_tile(norm_w), one_tile(g_nw), one_tile(m_norm_w), one_tile(v_norm_w))
    for res, o in zip(results, (g_nw,) + tuple(outs)):
        res["norm_w"] = o.reshape(1, D_MODEL)
    grads, delta, new_m, new_v = results

    return (grads["loss"], gx[None], *[grads[n] for n in names], *[delta[n] for n in names],
            *[new_m[n] for n in names], *[new_v[n] for n in names])
```

```python
import functools

import jax
import jax.numpy as jnp
import numpy as np
from jax import lax
from jax.experimental import pallas as pl
from jax.experimental.pallas import tpu as pltpu

F32 = jnp.float32
BF16 = jnp.bfloat16
I32 = jnp.int32

D_MODEL = 1024
D_HALF = 512
N_HEADS = 4
HEAD = 128
CHUNK = 64
PAIR = 2 * CHUNK
WINDOWS = (2, 4, 8, 16)
CONV_K = 4
EPS = 1e-6
N_IN = 3080
N_IN_PAD = 3200
BLK_IN = 770
BLK_IN_PAD = 896
BLK_OUT = 256
BLK_CONV = 384
COL_BA = 3072
QK_SCALE = HEAD ** -0.5
SMALL_ROWS = 608
VMEM_LIMIT = 56 * 1024 * 1024

ADAM_LR = 0.001
ADAM_B1 = 0.9
ADAM_B2 = 0.999
ADAM_EPS = 1e-08
ADAM_WD = 0.01
ADAM_STEP = 10

CHIP_MASKS = (2, 1, 3)
HEADS = range(N_HEADS)
HEAD_COLS = [slice(h * HEAD, (h + 1) * HEAD) for h in HEADS]


def _call(body, **kw):
    return pl.pallas_call(body, **kw)


def _params(*sem):
    return pltpu.CompilerParams(dimension_semantics=sem, vmem_limit_bytes=VMEM_LIMIT)


def _sds(shape, dtype=F32):
    return jax.ShapeDtypeStruct(shape, dtype)


def _bdot(a, b):
    return jnp.dot(a.astype(BF16), b.astype(BF16), preferred_element_type=F32)


def _bdot_nt(a, b):
    return lax.dot_general(a.astype(BF16), b.astype(BF16), (((1,), (1,)), ((), ())), preferred_element_type=F32)


def _bdot_tn(a, b):
    return lax.dot_general(a.astype(BF16), b.astype(BF16), (((0,), (0,)), ((), ())), preferred_element_type=F32)


def _side(a, b):
    return jnp.concatenate([a.astype(BF16), b.astype(BF16)], axis=1)


def _stack(a, b):
    return jnp.concatenate([a.astype(BF16), b.astype(BF16)], axis=0)


def _split(a):
    hi = a.astype(BF16)
    lo = (a - hi.astype(F32)).astype(BF16)
    return hi, lo


def _mask_dot(m, b):
    n = b.shape[1]
    both = jnp.dot(m, jnp.concatenate(_split(b), axis=1), preferred_element_type=F32)
    return both[:, :n] + both[:, n:]


def _sigmoid(x):
    return 0.5 * jnp.tanh(0.5 * x) + 0.5


def _softplus(x):
    return jnp.maximum(x, 0.0) + jnp.log(1.0 + jnp.exp(-jnp.abs(x)))


def _rowsum(x):
    return jnp.sum(x, axis=-1, keepdims=True)


def _colsum(x):
    return jnp.sum(x, axis=0, keepdims=True)


def _shift_down(xv, prev8, k):
    r = pltpu.roll(xv, k, 0)
    q = pltpu.roll(prev8, k, 0)
    row = lax.broadcasted_iota(I32, prev8.shape, 0)
    top = jnp.where(row < k, q, r[0:8])
    return jnp.concatenate([top, r[8:]], axis=0)


def _shift_up(xv, next8, k):
    t = xv.shape[0]
    r = pltpu.roll(xv, t - k, 0)
    q = pltpu.roll(next8, 8 - k, 0)
    row = lax.broadcasted_iota(I32, next8.shape, 0)
    bot = jnp.where(row >= 8 - k, q, r[t - 8:])
    return jnp.concatenate([r[:t - 8], bot], axis=0)


INTRA_PAIRS = 2
UNITS = [(pp, h) for pp in range(INTRA_PAIRS) for h in HEADS]


def _heads_of(ref, rows=PAIR, units=UNITS):
    return [ref[pp * rows:(pp + 1) * rows, HEAD_COLS[h]] for pp, h in units]


def _put_heads_of(ref, vals, rows=PAIR, units=UNITS):
    for (pp, h), v in zip(units, vals):
        ref[pp * rows:(pp + 1) * rows, HEAD_COLS[h]] = v.astype(ref.dtype)


_heads = _heads_of
_put_heads = _put_heads_of


def _each(fn, *lists):
    return [fn(*args) for args in zip(*lists)]


def _pool_bands(t, anti=False):
    r = np.arange(t)[:, None]
    c = np.arange(t + HEAD)[None, :]
    d = (c - r) if anti else (r - c + HEAD)
    return jnp.asarray(np.stack([(d >= 0) & (d < w) for w in WINDOWS]), BF16)


def _pool_mix(u, halo, z, pw, bands, row0):
    t = u[0].shape[0]
    rows = row0 + lax.broadcasted_iota(I32, (t, 1), 0) + 1
    cnt = [jnp.minimum(rows, w).astype(F32) for w in WINDOWS]
    win = _each(lambda b, h, v: _mask_dot(b, jnp.concatenate([h, v], axis=0)), bands, halo, u)
    mix = _each(lambda a, c, v: a / c - v, win, cnt, u)
    mixed = _each(_bdot, mix, pw)
    return mix, mixed, _each(_sigmoid, z), cnt


POOL_T = 256


def _conv_taps(xv, prev8):
    return [_shift_down(xv, prev8, CONV_K - 1 - j) for j in range(CONV_K - 1)] + [xv]


def _conv_pre(taps, cw):
    y = taps[CONV_K - 1] * cw[CONV_K - 1:CONV_K]
    for j in range(CONV_K - 2, -1, -1):
        y = y + taps[j] * cw[j:j + 1]
    return y


CONV_T = 256


def _conv_specs(t, tile_of=lambda i: i):
    tiles = [pl.BlockSpec((t, D_HALF), functools.partial(lambda i, p: (tile_of(i), 2 + p), p=p)) for p in range(3)]
    halos = [pl.BlockSpec((8, D_HALF),
                          functools.partial(lambda i, p: (jnp.maximum(tile_of(i) * (t // 8) - 1, 0), 2 + p), p=p))
             for p in range(3)]
    return tiles + halos


def _pair_masks():
    r = lax.broadcasted_iota(I32, (PAIR, PAIR), 0)
    c = lax.broadcasted_iota(I32, (PAIR, PAIR), 1)
    same = jnp.right_shift(r, 6) == jnp.right_shift(c, 6)
    return same, same & (r >= c), same & (r > c), r == c


def _interleave(*stage_lists):
    live = list(stage_lists)
    while live:
        for gen in list(live):
            try:
                next(gen)
            except StopIteration:
                live.remove(gen)


def _pair_common_stages(cm, qn, kn, vs, beta, g):
    same, incl, strict, eye = _pair_masks()
    incl_b = incl.astype(BF16)
    first = lax.broadcasted_iota(I32, (PAIR, HEAD), 0) < CHUNK
    cm.update(same=same, incl=incl, strict=strict, eye=eye)
    gc = _each(lambda gv: _mask_dot(incl_b, gv), g)
    q = _each(lambda v: v * QK_SCALE, qn)
    kb = _each(lambda k, b: k * b, kn, beta)
    cm.update(gc=gc, q=q, kb=kb, vb=_each(lambda v, b: v * b, vs, beta))
    yield
    both = _each(lambda a, b, c: _bdot_nt(_stack(a, b), c), kb, q, kn)
    cm.update(kk=[v[:PAIR] for v in both], qk=[v[PAIR:] for v in both])
    gc_row = _each(lambda v: _colsum(jnp.where(eye, v, 0.0)), gc)
    gl = _each(lambda v: jnp.where(first, v[CHUNK - 1:CHUNK], v[PAIR - 1:PAIR]), gc)
    egc = _each(jnp.exp, gc)
    cm.update(gl=gl, egc=egc,
              decay=_each(lambda v, r: jnp.where(incl, jnp.exp(jnp.where(incl, v - r, 0.0)), 0.0), gc, gc_row))
    yield
    cm.update(ekd=_each(lambda a, b: jnp.exp(a - b), gl, gc), cd=_each(jnp.exp, gl),
              kbg=_each(lambda k, e: k * e, kb, egc))
    yield


def _tri_inv_stages(out, a, eye_f):
    p = _each(lambda v: eye_f - v, a)
    x = _each(_bdot, a, a)
    yield
    for it in range(4):
        both = _each(lambda xv, pv: _bdot(xv, _side(pv, xv)), x, p)
        p = _each(lambda pv, b: pv + b[:, :PAIR], p, both)
        x = [b[:, PAIR:] for b in both]
        yield
    out["t"] = _each(lambda pv, xv: pv + _bdot(pv, xv), p, x)
    yield


def _chunk_scalar_spec(pairs=1, index=lambda i: (i, 0)):
    return pl.BlockSpec((16 * pairs, D_HALF), index)


SCAN_PAIRS = 2
SCAN_ROWS = SCAN_PAIRS * PAIR


def _delta_fwd(qn, kn, vs, beta, g):
    s = qn.shape[0]
    n_steps = s // SCAN_ROWS
    n_chunks = s // CHUNK
    assert INTRA_PAIRS == SCAN_PAIRS

    def body(qn_ref, kn_ref, vs_ref, beta_ref, g_ref, w_ref, att_ref, qd_ref, kd_ref, t_ref, cd_ref, o_ref, vn_ref, st_ref,
             state, u_s, w_s, att_s, qd_s, kd_s, cd_s):
        t = pl.program_id(0)

        @pl.when(t <= 1)
        def _():
            state[...] = jnp.zeros_like(state)

        @pl.when(t == 0)
        def _():
            for ref in (u_s, w_s, att_s, qd_s, kd_s, cd_s):
                ref[1] = jnp.zeros(ref.shape[1:], ref.dtype)

        cur = lax.rem(t, 2)
        prev = 1 - cur
        cols = list(enumerate(HEAD_COLS))

        def recurrence():
            sm = [state[h] for h in HEADS]
            for ci in range(2 * SCAN_PAIRS):
                rs = slice(ci * CHUNK, (ci + 1) * CHUNK)
                for h in HEADS:
                    st_ref[ci, h] = sm[h]
                both = [_bdot(jnp.concatenate([w_s[prev, rs, sl], qd_s[prev, rs, sl]], axis=0), sm[h]) for h, sl in cols]
                vn = [u_s[prev, rs, sl] - both[h][:CHUNK] for h, sl in cols]
                for h, sl in cols:
                    vn_ref[rs, sl] = vn[h].astype(BF16)
                    o_ref[rs, sl] = both[h][CHUNK:]
                yield
                sm = [sm[h] * cd_s[prev, ci * 8:ci * 8 + 1, sl] + _bdot_tn(kd_s[prev, rs, sl], vn[h]) for h, sl in cols]
                yield
            for h in HEADS:
                state[h] = sm[h]
            for pp in range(SCAN_PAIRS):
                rp = slice(pp * PAIR, (pp + 1) * PAIR)
                intra = [_bdot(att_s[prev, rp, sl], vn_ref[rp, sl]) for sl in HEAD_COLS]
                for h, sl in cols:
                    o_ref[rp, sl] += intra[h]
                yield

        def factors():
            kn = _heads(kn_ref)
            cm = {}
            yield from _pair_common_stages(cm, _heads(qn_ref), kn, _heads(vs_ref), _heads(beta_ref), _heads(g_ref))
            a = _each(lambda kk, d: jnp.where(cm["strict"], kk * d, 0.0), cm["kk"], cm["decay"])
            inv = {}
            yield from _tri_inv_stages(inv, a, cm["eye"].astype(F32))
            tm = inv["t"]
            uw = _each(lambda tv, a, b: _bdot(tv, _side(a, b)), tm, cm["vb"], cm["kbg"])
            res = dict(u=[v[:, :HEAD] for v in uw], w=[v[:, HEAD:] for v in uw],
                       att=_each(lambda a, b: a * b, cm["qk"], cm["decay"]),
                       qd=_each(lambda a, b: a * b, cm["q"], cm["egc"]), kd=_each(lambda a, b: a * b, kn, cm["ekd"]))
            yield
            _put_heads(t_ref, tm)
            for key, out, keep in (("w", w_ref, w_s), ("att", att_ref, att_s), ("qd", qd_ref, qd_s), ("kd", kd_ref, kd_s)):
                _put_heads(out, res[key])
                for (pp, h), v in zip(UNITS, res[key]):
                    keep[cur, pp * PAIR:(pp + 1) * PAIR, HEAD_COLS[h]] = v.astype(BF16)
            for (pp, h), v in zip(UNITS, res["u"]):
                u_s[cur, pp * PAIR:(pp + 1) * PAIR, HEAD_COLS[h]] = v
            for ci in range(2):
                for (pp, h), v in zip(UNITS, cm["cd"]):
                    rows8 = slice(pp * 16 + ci * 8, pp * 16 + (ci + 1) * 8)
                    cd_ref[rows8, HEAD_COLS[h]] = v[ci * CHUNK:ci * CHUNK + 8]
                    cd_s[cur, rows8, HEAD_COLS[h]] = v[ci * CHUNK:ci * CHUNK + 8]
            yield

        _interleave(recurrence(), factors())

    last = n_steps - 1
    now = lambda i: (jnp.minimum(i, last), 0)
    before = lambda i: (jnp.maximum(i - 1, 0), 0)
    rows = lambda index: pl.BlockSpec((SCAN_ROWS, D_HALF), index)
    slot = lambda r, dtype: pltpu.VMEM((2, r, D_HALF), dtype)
    return _call(
        body, name="delta_fwd", grid=(n_steps + 1,),
        in_specs=[rows(now)] * 5,
        out_specs=[rows(now)] * 5 + [_chunk_scalar_spec(SCAN_PAIRS, now), rows(before), rows(before),
                                     pl.BlockSpec((2 * SCAN_PAIRS, N_HEADS, HEAD, HEAD), lambda i: (jnp.maximum(i - 1, 0), 0, 0, 0))],
        out_shape=[_sds((s, D_HALF), BF16)] * 5 + [_sds((s // 8, D_HALF)), _sds((s, D_HALF)), _sds((s, D_HALF), BF16),
                                                  _sds((n_chunks, N_HEADS, HEAD, HEAD))],
        scratch_shapes=[pltpu.VMEM((N_HEADS, HEAD, HEAD), F32), slot(SCAN_ROWS, F32), slot(SCAN_ROWS, BF16),
                        slot(SCAN_ROWS, BF16), slot(SCAN_ROWS, BF16), slot(SCAN_ROWS, BF16), slot(16 * SCAN_PAIRS, F32)],
        compiler_params=_params("arbitrary"),
    )(qn, kn, vs, beta, g)


OUT_T = 512
OUT_ROWS = 256


def _out_fwd_bwd(x, y_pool, o, proj, target, w_out, dn_norm_w, final_norm_w):
    s = x.shape[0]
    t = OUT_T

    def body(x_ref, yp_ref, o_ref, z_ref, tg_ref, wo_ref, dnw_ref, fnw_ref,
             gwo_ref, dh_ref, dyp_ref, do_ref, dz_ref, loss_ref, gfn_ref, gdn_ref, y_ref, yt_ref, gwo_acc):
        @pl.when(pl.program_id(0) == 0)
        def _():
            loss_ref[...] = jnp.zeros_like(loss_ref)
            gfn_ref[...] = jnp.zeros_like(gfn_ref)
            gdn_ref[...] = jnp.zeros_like(gdn_ref)
            gwo_acc[...] = jnp.zeros_like(gwo_acc)

        dnw = dnw_ref[...]
        fnw = fnw_ref[...]

        def stages(rows, lead):
            for _ in range(lead):
                yield
            ypv = yp_ref[rows]
            y_ref[rows, :D_HALF] = ypv.astype(BF16)
            yt_ref[:D_HALF, rows] = ypv.T.astype(BF16)
            keep = []
            for h in HEADS:
                ov = o_ref[rows, HEAD_COLS[h]]
                zv = z_ref[rows, HEAD_COLS[h]]
                ro = lax.rsqrt(jnp.mean(ov * ov, axis=-1, keepdims=True) + EPS)
                ohat = ov * ro
                sg = _sigmoid(zv)
                keep.append((ro, ohat, zv, sg))
                ydn = ohat * dnw * (zv * sg)
                y_ref[rows, D_HALF + h * HEAD:D_HALF + (h + 1) * HEAD] = ydn.astype(BF16)
                yt_ref[D_HALF + h * HEAD:D_HALF + (h + 1) * HEAD, rows] = ydn.T.astype(BF16)
            yield
            hv = x_ref[rows] + jnp.dot(y_ref[rows], wo_ref[...], preferred_element_type=F32)
            yield
            r2 = lax.rsqrt(jnp.mean(hv * hv, axis=-1, keepdims=True) + EPS)
            hhat = hv * r2
            err = hhat * fnw - tg_ref[rows]
            loss_ref[...] += 0.5 * jnp.sum(_rowsum(err * err) * (1.0 / D_MODEL), axis=0, keepdims=True)
            dout = err * (1.0 / D_MODEL)
            gfn_ref[...] += _colsum(dout * hhat)
            dhh = dout * fnw
            dh = r2 * (dhh - hhat * jnp.mean(dhh * hhat, axis=-1, keepdims=True))
            dh_ref[rows] = dh
            yield
            if lead == t // OUT_ROWS - 1:
                gwo_acc[...] += _bdot(yt_ref[...], dh_ref[...])
            dy = _bdot_nt(dh, wo_ref[...])
            yield
            dyp_ref[rows] = dy[:, :D_HALF]
            gdn = jnp.zeros((1, HEAD), F32)
            for h in HEADS:
                ro, ohat, zv, sg = keep[h]
                dyd = dy[:, D_HALF + h * HEAD:D_HALF + (h + 1) * HEAD]
                sz = zv * sg
                dz_ref[rows, HEAD_COLS[h]] = (dyd * ohat * dnw * (sg * (1.0 + zv * (1.0 - sg)))).astype(BF16)
                gdn = gdn + _colsum(dyd * ohat * sz)
                doh = dyd * dnw * sz
                do_ref[rows, HEAD_COLS[h]] = ro * (doh - ohat * jnp.mean(doh * ohat, axis=-1, keepdims=True))
            gdn_ref[...] += gdn
            yield

        _interleave(*[stages(slice(k * OUT_ROWS, (k + 1) * OUT_ROWS), k) for k in range(t // OUT_ROWS)])

        @pl.when(pl.program_id(0) == pl.num_programs(0) - 1)
        def _():
            gwo_ref[...] = gwo_acc[...].astype(BF16)

    wide = pl.BlockSpec((t, D_MODEL), lambda i: (i, 0))
    half = pl.BlockSpec((t, D_HALF), lambda i: (i, 0))
    const = lambda shape: pl.BlockSpec(shape, lambda i: (0,) * len(shape))
    return _call(
        body, name="out_fwd_bwd", grid=(s // t,),
        in_specs=[wide, half, half, pl.BlockSpec((t, D_HALF), lambda i: (i, 5)), wide,
                  const((D_MODEL, D_MODEL)), const((1, HEAD)), const((1, D_MODEL))],
        out_specs=[const((D_MODEL, D_MODEL)), wide, half, half, half,
                   const((1, HEAD)), const((1, D_MODEL)), const((1, HEAD))],
        out_shape=[_sds((D_MODEL, D_MODEL), BF16), _sds((s, D_MODEL)), _sds((s, D_HALF)), _sds((s, D_HALF)),
                   _sds((s, D_HALF), BF16), _sds((1, HEAD)), _sds((1, D_MODEL)), _sds((1, HEAD))],
        scratch_shapes=[pltpu.VMEM((t, D_MODEL), BF16), pltpu.VMEM((D_MODEL, t), BF16), pltpu.VMEM((D_MODEL, D_MODEL), F32)],
        compiler_params=_params("arbitrary"),
    )(x, y_pool, o, proj, target, w_out, dn_norm_w, final_norm_w)


def _grad_w_in(at, pieces):
    m, s = at.shape
    n = len(pieces)
    tn, tk = D_HALF, min(s, 1024)

    def body(a_ref, *refs):
        p_refs, o_ref, acc = refs[:n], refs[n], refs[n + 1]

        @pl.when(pl.program_id(0) == 0)
        def _():
            acc[...] = jnp.zeros_like(acc)

        av = a_ref[...]
        for p in range(n):
            acc[:, p * tn:(p + 1) * tn] += _bdot(av, p_refs[p][...])

        @pl.when(pl.program_id(0) == pl.num_programs(0) - 1)
        def _():
            for j in range(4):
                base = j * BLK_IN // HEAD * HEAD
                win = acc[:, base:base + BLK_IN_PAD]
                if j * BLK_IN > base:
                    win = pltpu.roll(win, BLK_IN_PAD - (j * BLK_IN - base), 1)
                o_ref[j] = win.astype(BF16)

    return _call(
        body, name="grad_w_in", grid=(s // tk,),
        in_specs=[pl.BlockSpec((m, tk), lambda k: (0, k))] + [pl.BlockSpec((tk, tn), lambda k: (k, 0))] * n,
        out_specs=pl.BlockSpec((4, m, BLK_IN_PAD), lambda k: (0, 0, 0)),
        out_shape=_sds((4, m, BLK_IN_PAD), BF16),
        scratch_shapes=[pltpu.VMEM((m, n * tn), F32)],
        compiler_params=_params("arbitrary"),
    )(at, *pieces)


def _delta_bwd(do, vn, qd, kd, w, att, cd, st, qn, kn, vs, beta, g, tm):
    s = do.shape[0]
    n_steps = s // SCAN_ROWS
    assert INTRA_PAIRS == SCAN_PAIRS

    def body(do_ref, vn_ref, qd_ref, kd_ref, w_ref, att_ref, cd_ref, st_ref, qn_ref, kn_ref, vs_ref, beta_ref, g_ref, t_ref,
             dqn_ref, dkn_ref, dvs_ref, dbeta_ref, dg_ref, dstate, du_s, dw_s, datt_s, dqd_s, dkd_s, dcd_s):
        t = pl.program_id(0)

        @pl.when(t == 0)
        def _():
            dstate[...] = jnp.zeros_like(dstate)
            for ref in (du_s, dw_s, datt_s, dqd_s, dkd_s, dcd_s):
                ref[1] = jnp.zeros(ref.shape[1:], ref.dtype)

        cur = lax.rem(t, 2)
        prev = 1 - cur
        cols = list(enumerate(HEAD_COLS))
        _, incl, _, _ = _pair_masks()

        def recurrence():
            dv_intra = []
            for pp in range(SCAN_PAIRS):
                rp = slice(pp * PAIR, (pp + 1) * PAIR)
                dv_intra.append([_bdot_tn(att_ref[rp, sl], do_ref[rp, sl]) for _, sl in cols])
                for _, sl in cols:
                    datt_s[cur, rp, sl] = jnp.where(incl, _bdot_nt(do_ref[rp, sl], vn_ref[rp, sl]), 0.0)
                yield
            ds = [dstate[h] for h in HEADS]
            for ci in range(2 * SCAN_PAIRS - 1, -1, -1):
                rs = slice(ci * CHUNK, (ci + 1) * CHUNK)
                in_pair = slice((ci % 2) * CHUNK, (ci % 2 + 1) * CHUNK)
                sm = [st_ref[ci, h] for h in HEADS]
                dvn = [dv_intra[ci // 2][h][in_pair] + _bdot(kd_ref[rs, sl], ds[h]) for h, sl in cols]
                dkd = [_bdot_nt(vn_ref[rs, sl], ds[h]) for h, sl in cols]
                dcd = [jnp.broadcast_to(_rowsum(_colsum(ds[h] * sm[h])), (8, HEAD)) for h in HEADS]
                yield
                both = [_bdot_nt(_stack(do_ref[rs, sl], dvn[h]), sm[h]) for h, sl in cols]
                for h, sl in cols:
                    du_s[cur, rs, sl] = dvn[h].astype(BF16)
                    dqd_s[cur, rs, sl] = both[h][:CHUNK]
                    dw_s[cur, rs, sl] = (-both[h][CHUNK:]).astype(BF16)
                    dkd_s[cur, rs, sl] = dkd[h]
                    dcd_s[cur, ci * 8:(ci + 1) * 8, sl] = dcd[h]
                ds = [ds[h] * cd_ref[ci * 8:ci * 8 + 1, sl]
                      + _bdot_tn(_stack(qd_ref[rs, sl], w_ref[rs, sl]), _stack(do_ref[rs, sl], -dvn[h])) for h, sl in cols]
                yield
            for h in HEADS:
                dstate[h] = ds[h]

        def factors(units):
            _heads = functools.partial(_heads_of, units=units)
            _put_heads = functools.partial(_put_heads_of, units=units)
            ones = jnp.ones((2 * PAIR, HEAD), BF16)
            tn = (((0,), (0,)), ((), ()))
            kept = lambda ref, rows=PAIR: [ref[prev, pp * rows:(pp + 1) * rows, HEAD_COLS[h]] for pp, h in units]
            kn, vs, beta = _heads(kn_ref), _heads(vs_ref), _heads(beta_ref)
            cm = {}
            yield from _pair_common_stages(cm, _heads(qn_ref), kn, vs, beta, _heads(g_ref))
            tmv = _heads(t_ref)
            duv, dwv, dattv, dqdv, dkdv = kept(du_s), kept(dw_s), kept(datt_s), kept(dqd_s), kept(dkd_s)
            duw = _each(_side, duv, dwv)
            both = _each(_bdot_tn, tmv, duw)
            dvb, dkbg = [v[:, :HEAD] for v in both], [v[:, HEAD:] for v in both]
            dt = _each(lambda a, b, c: _bdot_nt(a, _side(b, c)), duw, cm["vb"], cm["kbg"])
            yield
            m1 = _each(_bdot_tn, tmv, dt)
            yield
            da = _each(lambda a, b: -jnp.where(cm["strict"], _bdot_nt(a, b), 0.0), m1, tmv)
            yield
            dkk = _each(lambda a, b: a * b, da, cm["decay"])
            dqk = _each(lambda a, b: a * b, dattv, cm["decay"])
            dd = _each(lambda a, b, c, d: a * b + c * d, dkk, cm["kk"], dqk, cm["qk"])
            dkq = _each(_stack, dkk, dqk)
            both = _each(_bdot, dkq, kn)
            dkb = _each(lambda a, c, d: a[:PAIR] + c * d, both, dkbg, cm["egc"])
            dq = _each(lambda a, c, d: a[PAIR:] + c * d, both, dqdv, cm["egc"])
            yield
            dkn = _each(lambda a, b, c: _bdot_tn(a, _stack(b, c)), dkq, cm["kb"], cm["q"])
            dkn = _each(lambda a, b, c, d, e: a + b * c + d * e, dkn, dkdv, cm["ekd"], dkb, beta)
            t_kd = _each(lambda a, b, c: _rowsum(a * b * c), dkdv, kn, cm["ekd"])
            yield
            split = _each(_split, dd)
            rows_dd = [jnp.dot(_side(hi, lo), ones, preferred_element_type=F32) for hi, lo in split]
            cols_dd = [lax.dot_general(_stack(hi, lo), ones, tn, preferred_element_type=F32) for hi, lo in split]
            yield
            dgc = _each(lambda r, c, a, b, e, f, k, tk: r - c + _rowsum(a * b * e) + _rowsum(f * k) - tk,
                        rows_dd, cols_dd, dqdv, cm["q"], cm["egc"], dkbg, cm["kbg"], t_kd)
            same_b = cm["same"].astype(BF16)
            rowi = lax.broadcasted_iota(I32, (PAIR, HEAD), 0)
            dcd = _each(lambda d: jnp.where(rowi < CHUNK, d[0:1], d[8:9]), kept(dcd_s, rows=16))
            dgl = _each(lambda tk, d, c: _mask_dot(same_b, jnp.broadcast_to(tk, (PAIR, HEAD))) + d * c, t_kd, dcd, cm["cd"])
            yield
            is_last = jnp.bitwise_and(rowi, CHUNK - 1) == CHUNK - 1
            dgc = _each(lambda a, b: a + jnp.where(is_last, b, 0.0), dgc, dgl)
            r = lax.broadcasted_iota(I32, (PAIR, PAIR), 0)
            c = lax.broadcasted_iota(I32, (PAIR, PAIR), 1)
            upper_b = (cm["same"] & (r <= c)).astype(BF16)
            _put_heads(dg_ref, _each(lambda v: _mask_dot(upper_b, v), dgc))
            yield
            _put_heads(dbeta_ref, _each(lambda a, b, c, d: jnp.broadcast_to(_rowsum(a * b) + _rowsum(c * d), (PAIR, HEAD)),
                                        dkb, kn, dvb, vs))
            _put_heads(dqn_ref, _each(lambda v: v * QK_SCALE, dq))
            _put_heads(dkn_ref, dkn)
            _put_heads(dvs_ref, _each(lambda a, b: a * b, dvb, beta))
            yield

        _interleave(recurrence(), *[factors(UNITS[pp * N_HEADS:(pp + 1) * N_HEADS]) for pp in range(INTRA_PAIRS)])

    last = n_steps - 1
    now = lambda i: (jnp.maximum(last - i, 0), 0)
    after = lambda i: (jnp.minimum(n_steps - i, last), 0)
    rows = lambda index: pl.BlockSpec((SCAN_ROWS, D_HALF), index)
    slot = lambda r, dtype: pltpu.VMEM((2, r, D_HALF), dtype)
    return _call(
        body, name="delta_bwd", grid=(n_steps + 1,),
        in_specs=[rows(now)] * 6 + [_chunk_scalar_spec(SCAN_PAIRS, now),
                                    pl.BlockSpec((2 * SCAN_PAIRS, N_HEADS, HEAD, HEAD), lambda i: (jnp.maximum(last - i, 0), 0, 0, 0))]
                 + [rows(after)] * 6,
        out_specs=[rows(after)] * 5,
        out_shape=[_sds((s, D_HALF))] * 5,
        scratch_shapes=[pltpu.VMEM((N_HEADS, HEAD, HEAD), F32), slot(SCAN_ROWS, BF16), slot(SCAN_ROWS, BF16),
                        slot(SCAN_ROWS, F32), slot(SCAN_ROWS, F32), slot(SCAN_ROWS, F32), slot(16 * SCAN_PAIRS, F32)],
        compiler_params=_params("arbitrary"),
    )(do, vn, qd, kd, w, att, cd, st, qn, kn, vs, beta, g, tm)


def _fused_call(name, n_steps, parts):
    n_in = [len(p["inputs"]) for p in parts]
    n_out = [len(p["out_shape"]) for p in parts]
    n_scr = [len(p["scratch"]) for p in parts]

    def body(*refs):
        ins, outs, scr = refs[:sum(n_in)], refs[sum(n_in):sum(n_in) + sum(n_out)], refs[sum(n_in) + sum(n_out):]
        gens, a, b, c = [], 0, 0, 0
        for p, ni, no, ns in zip(parts, n_in, n_out, n_scr):
            gens.append(p["stages"](ins[a:a + ni], outs[b:b + no], scr[c:c + ns]))
            a, b, c = a + ni, b + no, c + ns
        _interleave(*gens)

    flat = lambda key: [v for p in parts for v in p[key]]
    res = _call(
        body, name=name, grid=(n_steps,),
        in_specs=flat("in_specs"), out_specs=flat("out_specs"), out_shape=flat("out_shape"),
        scratch_shapes=flat("scratch"),
        compiler_params=_params("arbitrary"),
    )(*flat("inputs"))
    out, b = [], 0
    for no in n_out:
        out.append(res[b:b + no])
        b += no
    return out


def _pool_bwd_part(proj, dyp, pool_w, pool_scale, tile_of, n_tiles):
    s = proj.shape[0]
    t = POOL_T
    hb = t // HEAD
    last = s // HEAD - 1

    def stages(ins, outs, scratch):
        u_ref, z_ref, halo_ref, dy_ref, zn_ref, dyn_ref, pw_ref, ps_ref, band_ref, aband_ref = ins
        du_ref, dz_ref, gpw_ref, gps_ref = outs
        tile = tile_of(pl.program_id(0))

        @pl.when(pl.program_id(0) == 0)
        def _():
            gpw_ref[...] = jnp.zeros_like(gpw_ref)
            gps_ref[...] = jnp.zeros_like(gps_ref)

        live = (tile > 0).astype(F32)
        more = (tile < n_tiles - 1).astype(F32)
        groups = lambda ref: [ref[:, sl] for sl in HEAD_COLS]
        z, ps, dy = groups(z_ref), groups(ps_ref), groups(dy_ref)
        pw = [pw_ref[g] for g in HEADS]
        mix, mixed, sg, cnt = _pool_mix(groups(u_ref), [h * live for h in groups(halo_ref)], z, pw,
                                        [band_ref[g] for g in HEADS], tile * t)
        yield
        sz = _each(lambda a, b: a * b, z, sg)
        for sl, d, m, p, s_, zg in zip(HEAD_COLS, dy, mixed, ps, sg, z):
            dz_ref[:, sl] = (d * m * p * (s_ * (1.0 + zg * (1.0 - s_)))).astype(BF16)
        for sl, d, m, a in zip(HEAD_COLS, dy, mixed, sz):
            gps_ref[:, sl] += _colsum(d * m * a)
        dmixed = _each(lambda d, p, a: d * p * a, dy, ps, sz)
        yield
        for g, gp in enumerate(_each(_bdot_tn, mix, dmixed)):
            gpw_ref[g] += gp
        dmix = _each(_bdot_nt, dmixed, pw)
        yield
        dmix_n = _each(lambda d, p, zn, w_: _bdot_nt(d * more * p * (zn * _sigmoid(zn)), w_),
                       groups(dyn_ref), ps, groups(zn_ref), pw)
        yield
        scaled = [jnp.concatenate([a / c, b * (1.0 / w)], axis=0) for a, c, b, w in zip(dmix, cnt, dmix_n, WINDOWS)]
        du = _each(lambda b, s_, d: _mask_dot(b, s_) - d, [aband_ref[g] for g in HEADS], scaled, dmix)
        for sl, v in zip(HEAD_COLS, du):
            du_ref[:, sl] = v.astype(BF16)
        yield

    tile = lambda col: pl.BlockSpec((t, D_HALF), lambda i: (tile_of(i), col))
    below = lambda col: pl.BlockSpec((HEAD, D_HALF), lambda i: (jnp.minimum((tile_of(i) + 1) * hb, last), col))
    const3 = lambda shape: pl.BlockSpec(shape, lambda i: (0, 0, 0))
    return dict(
        inputs=[proj, proj, proj, dyp, proj, dyp, pool_w, pool_scale, _pool_bands(t), _pool_bands(t, anti=True)],
        in_specs=[tile(0), tile(1), pl.BlockSpec((HEAD, D_HALF), lambda i: (jnp.maximum(tile_of(i) * hb - 1, 0), 0)),
                  tile(0), below(1), below(0), const3((N_HEADS, HEAD, HEAD)), pl.BlockSpec((1, D_HALF), lambda i: (0, 0)),
                  const3((N_HEADS, t, HEAD + t)), const3((N_HEADS, t, HEAD + t))],
        out_specs=[tile(0), tile(0), const3((N_HEADS, HEAD, HEAD)), pl.BlockSpec((1, D_HALF), lambda i: (0, 0))],
        out_shape=[_sds((s, D_HALF), BF16), _sds((s, D_HALF), BF16), _sds((N_HEADS, HEAD, HEAD)), _sds((1, D_HALF))],
        scratch=[], stages=stages)


def _conv_bwd_part(proj, pre, conv_w, a_log, dt_bias, dqn, dkn, dvs, dbeta, dg, tile_of, n_tiles):
    s = proj.shape[0]
    t = CONV_T

    def stages(ins, outs, scratch):
        (q_ref, k_ref, v_ref, yq_ref, yk_ref, yv_ref, ba_ref, cw_ref, al_ref, dtb_ref,
         dqn_ref, dkn_ref, dvs_ref, dbeta_ref, dg_ref) = ins
        oq_ref, ok_ref, ov_ref, dba_ref, gcw_out, gsm_out = outs
        below, gcw_ref, gsm_ref = scratch
        step = pl.program_id(0)

        @pl.when(step == 0)
        def _():
            gcw_ref[...] = jnp.zeros_like(gcw_ref)
            gsm_ref[...] = jnp.zeros_like(gsm_ref)
            below[...] = jnp.zeros_like(below)

        parts = ((q_ref, yq_ref, dqn_ref, oq_ref), (k_ref, yk_ref, dkn_ref, ok_ref), (v_ref, yv_ref, dvs_ref, ov_ref))
        for p, (x_ref, y_ref, d_ref, o_ref) in enumerate(parts):
            for h in HEADS:
                cs = HEAD_COLS[h]
                wide = slice(p * D_HALF + h * HEAD, p * D_HALF + (h + 1) * HEAD)
                cw = cw_ref[:, wide]
                y = y_ref[:, cs]
                sg = _sigmoid(y)
                sv = y * sg
                ds = d_ref[:, cs]
                if p < 2:
                    rn = lax.rsqrt(_rowsum(sv * sv) + EPS)
                    nrm = sv * rn
                    ds = rn * (ds - nrm * _rowsum(ds * nrm))
                dy = ds * (sg * (1.0 + y * (1.0 - sg)))
                nxt = below[:, wide]
                ahead = [dy] + [_shift_up(dy, nxt, sft) for sft in range(1, CONV_K)]
                xv = x_ref[:, cs]
                acc = dy * cw[CONV_K - 1:CONV_K]
                for sft in range(1, CONV_K):
                    acc = acc + ahead[sft] * cw[CONV_K - 1 - sft:CONV_K - sft]
                for j in range(CONV_K):
                    gcw_ref[8 * j:8 * j + 8, wide] += _rows8(xv * ahead[CONV_K - 1 - j])
                o_ref[:, cs] = acc.astype(BF16)
                below[:, wide] = dy[0:8]
                yield

        ba = ba_ref[...]
        lane = lax.broadcasted_iota(I32, (t, HEAD), 1)
        lane8 = lax.broadcasted_iota(I32, (8, HEAD), 1)
        dba = jnp.zeros((t, HEAD), F32)
        gsm = jnp.zeros((8, HEAD), F32)
        for h in HEADS:
            beta = _sigmoid(ba[:, h:h + 1])
            dbeta = dbeta_ref[:, h * HEAD:h * HEAD + 1]
            xg = ba[:, N_HEADS + h:N_HEADS + h + 1] + dtb_ref[0:1, h:h + 1]
            nexp = -jnp.exp(al_ref[0:1, h:h + 1])
            dgv = dg_ref[:, h * HEAD:h * HEAD + 1]
            da = dgv * nexp * _sigmoid(xg)
            dba = dba + jnp.where(lane == h, dbeta * beta * (1.0 - beta), 0.0) + jnp.where(lane == N_HEADS + h, da, 0.0)
            gsm = (gsm + jnp.where(lane8 == h, _rows8(dgv * nexp * _softplus(xg)), 0.0)
                   + jnp.where(lane8 == N_HEADS + h, _rows8(da), 0.0))
        dba_ref[...] = jnp.zeros_like(dba_ref)
        dba_ref[:, :HEAD] = dba.astype(BF16)
        gsm_ref[...] += gsm
        yield

        @pl.when(step == n_tiles - 1)
        def _():
            gcw_out[...] = jnp.zeros_like(gcw_out)
            for j in range(CONV_K):
                gcw_out[j:j + 1, :] = _colsum(gcw_ref[8 * j:8 * j + 8, :])
            gsm_out[...] = jnp.broadcast_to(_colsum(gsm_ref[...]), (8, HEAD))

    row = pl.BlockSpec((t, D_HALF), lambda i: (tile_of(i), 0))
    const = lambda shape: pl.BlockSpec(shape, lambda i: (0, 0))
    return dict(
        inputs=[proj] * 3 + list(pre) + [proj, conv_w, a_log, dt_bias, dqn, dkn, dvs, dbeta, dg],
        in_specs=_conv_specs(t, tile_of)[:3] + [row] * 3
                 + [pl.BlockSpec((t, HEAD), lambda i: (tile_of(i), COL_BA // HEAD)),
                    const((CONV_K, 3 * D_HALF)), const((1, N_HEADS)), const((1, N_HEADS))] + [row] * 5,
        out_specs=[row, row, row, row, const((8, 3 * D_HALF)), const((8, HEAD))],
        out_shape=[_sds((s, D_HALF), BF16)] * 4 + [_sds((8, 3 * D_HALF)), _sds((8, HEAD))],
        scratch=[pltpu.VMEM((8, 3 * D_HALF), F32), pltpu.VMEM((8 * CONV_K, 3 * D_HALF), F32), pltpu.VMEM((8, HEAD), F32)],
        stages=stages)


def _pool_fwd_part(proj, pool_w, pool_scale):
    s = proj.shape[0]
    t = POOL_T
    hb = t // HEAD

    def stages(ins, outs, scratch, tile=None):
        u_ref, z_ref, halo_ref, pw_ref, ps_ref, band_ref = ins
        y_ref, = outs
        i = pl.program_id(0) if tile is None else tile
        live = (i > 0).astype(F32)
        groups = lambda ref: [ref[:, sl] for sl in HEAD_COLS]
        z = groups(z_ref)
        u, halo = groups(u_ref), [h * live for h in groups(halo_ref)]
        yield
        _, mixed, sg, _ = _pool_mix(u, halo, z, [pw_ref[g] for g in HEADS], [band_ref[g] for g in HEADS], i * t)
        yield
        for sl, m, zg, s_ in zip(HEAD_COLS, mixed, z, sg):
            y_ref[:, sl] = m * ps_ref[:, sl] * (zg * s_)
        yield

    const3 = lambda shape: pl.BlockSpec(shape, lambda i: (0, 0, 0))
    return dict(
        inputs=[proj, proj, proj, pool_w, pool_scale, _pool_bands(t)],
        in_specs=[pl.BlockSpec((t, D_HALF), lambda i: (i, 0)), pl.BlockSpec((t, D_HALF), lambda i: (i, 1)),
                  pl.BlockSpec((HEAD, D_HALF), lambda i: (jnp.maximum(i * hb - 1, 0), 0)),
                  const3((N_HEADS, HEAD, HEAD)), pl.BlockSpec((1, D_HALF), lambda i: (0, 0)), const3((N_HEADS, t, HEAD + t))],
        out_specs=[pl.BlockSpec((t, D_HALF), lambda i: (i, 0))], out_shape=[_sds((s, D_HALF))],
        scratch=[], stages=stages)


def _conv_fwd_part(proj, conv_w, a_log, dt_bias):
    s = proj.shape[0]
    t = CONV_T

    def stages(ins, outs, scratch, tile=None):
        q_ref, k_ref, v_ref, hq_ref, hk_ref, hv_ref, ba_ref, cw_ref, al_ref, dtb_ref = ins
        qn_ref, kn_ref, vs_ref, beta_ref, g_ref, yq_ref, yk_ref, yv_ref = outs
        live = ((pl.program_id(0) if tile is None else tile) > 0).astype(F32)
        parts = ((q_ref, hq_ref, qn_ref, yq_ref), (k_ref, hk_ref, kn_ref, yk_ref), (v_ref, hv_ref, vs_ref, yv_ref))
        for p, (x_ref, h_ref, o_ref, y_ref) in enumerate(parts):
            for h in HEADS:
                cs = HEAD_COLS[h]
                taps = _conv_taps(x_ref[:, cs], h_ref[:, cs] * live)
                y = _conv_pre(taps, cw_ref[:, p * D_HALF + h * HEAD:p * D_HALF + (h + 1) * HEAD])
                y_ref[:, cs] = y
                sv = y * _sigmoid(y)
                o_ref[:, cs] = sv if p == 2 else sv * lax.rsqrt(_rowsum(sv * sv) + EPS)
                yield
        ba = ba_ref[...]
        for h in HEADS:
            beta = _sigmoid(ba[:, h:h + 1])
            gl = -jnp.exp(al_ref[0:1, h:h + 1]) * _softplus(ba[:, N_HEADS + h:N_HEADS + h + 1] + dtb_ref[0:1, h:h + 1])
            beta_ref[:, HEAD_COLS[h]] = jnp.broadcast_to(beta, (t, HEAD))
            g_ref[:, HEAD_COLS[h]] = jnp.broadcast_to(gl, (t, HEAD))
        yield

    row = pl.BlockSpec((t, D_HALF), lambda i: (i, 0))
    const = lambda shape: pl.BlockSpec(shape, lambda i: (0, 0))
    return dict(
        inputs=[proj] * 7 + [conv_w, a_log, dt_bias],
        in_specs=_conv_specs(t) + [pl.BlockSpec((t, HEAD), lambda i: (i, COL_BA // HEAD)),
                                   const((CONV_K, 3 * D_HALF)), const((1, N_HEADS)), const((1, N_HEADS))],
        out_specs=[row] * 8, out_shape=[_sds((s, D_HALF))] * 8, scratch=[], stages=stages)


def _front_fwd(x, norm_w, w_pad, conv_w, a_log, dt_bias, pool_w, pool_scale, after):
    s = x.shape[0]
    t = CONV_T
    n_tiles = s // t
    assert POOL_T == CONV_T
    like_proj = _sds((s, N_IN_PAD))
    conv = _conv_fwd_part(like_proj, conv_w, a_log, dt_bias)
    pool = _pool_fwd_part(like_proj, pool_w, pool_scale)
    bands = pool["inputs"][-1]
    mxu_n = 256
    col_bounds = list(range(0, N_IN_PAD, 3 * mxu_n)) + [N_IN_PAD]

    def body(x_ref, nw_ref, w_ref, cw_ref, al_ref, dtb_ref, pw_ref, ps_ref, band_ref, after_ref,
             proj_ref, nt_ref, qn_ref, kn_ref, vs_ref, beta_ref, g_ref, yq_ref, yk_ref, yv_ref, y_ref, prev):
        del after_ref
        i = pl.program_id(0)

        @pl.when(i == 0)
        def _():
            prev[...] = jnp.zeros_like(prev)

        tile = jnp.maximum(i - 1, 0)
        main, above8, above = pl.ds(HEAD, t), pl.ds(HEAD - 8, 8), pl.ds(0, HEAD)
        cols = lambda rows, c0, width=D_HALF: prev.at[rows, pl.ds(c0, width)]
        conv_ins = (cols(main, 2 * D_HALF), cols(main, 3 * D_HALF), cols(main, 4 * D_HALF),
                    cols(above8, 2 * D_HALF), cols(above8, 3 * D_HALF), cols(above8, 4 * D_HALF),
                    cols(main, COL_BA, HEAD), cw_ref, al_ref, dtb_ref)
        pool_ins = (cols(main, 0), cols(main, D_HALF), cols(above, 0), pw_ref, ps_ref, band_ref)

        def projection():
            xv = x_ref[...]
            r = lax.rsqrt(jnp.mean(xv * xv, axis=-1, keepdims=True) + EPS)
            nv = xv * r * nw_ref[...]
            nt_ref[...] = nv.T.astype(BF16)
            nb = nv.astype(BF16)
            yield
            for lo, hi in zip(col_bounds[:-1], col_bounds[1:]):
                proj_ref[:, lo:hi] = jnp.dot(nb, w_ref[:, lo:hi], preferred_element_type=F32)
                yield

        _interleave(projection(),
                    conv["stages"](conv_ins, (qn_ref, kn_ref, vs_ref, beta_ref, g_ref, yq_ref, yk_ref, yv_ref), (), tile),
                    pool["stages"](pool_ins, (y_ref,), (), tile))
        prev[0:HEAD] = prev[t:t + HEAD]
        prev[HEAD:HEAD + t] = proj_ref[...]

    last = n_tiles - 1
    now = lambda i: (jnp.minimum(i, last), 0)
    before = lambda i: (jnp.maximum(i - 1, 0), 0)
    const = lambda a: pl.BlockSpec(a.shape, lambda i: (0,) * a.ndim)
    half = pl.BlockSpec((t, D_HALF), before)
    return _call(
        body, name="front_fwd", grid=(n_tiles + 1,),
        in_specs=[pl.BlockSpec((t, D_MODEL), now), const(norm_w), const(w_pad), const(conv_w), const(a_log), const(dt_bias),
                  const(pool_w), const(pool_scale), const(bands), pl.BlockSpec(memory_space=pl.ANY)],
        out_specs=[pl.BlockSpec((t, N_IN_PAD), now), pl.BlockSpec((D_MODEL, t), lambda i: (0, jnp.minimum(i, last)))]
                  + [half] * 9,
        out_shape=[_sds((s, N_IN_PAD)), _sds((D_MODEL, s), BF16)] + [_sds((s, D_HALF))] * 9,
        scratch_shapes=[pltpu.VMEM((HEAD + t, N_IN_PAD), F32)],
        compiler_params=_params("arbitrary"),
    )(x, norm_w, w_pad, conv_w, a_log, dt_bias, pool_w, pool_scale, bands, after)


def _conv_pool_bwd(proj, pre, conv_w, a_log, dt_bias, dqn, dkn, dvs, dbeta, dg, dyp, pool_w, pool_scale):
    n_tiles = proj.shape[0] // CONV_T
    assert POOL_T == CONV_T
    tile_of = lambda i: n_tiles - 1 - i
    return _fused_call("conv_pool_bwd", n_tiles, [
        _conv_bwd_part(proj, pre, conv_w, a_log, dt_bias, dqn, dkn, dvs, dbeta, dg, tile_of, n_tiles),
        _pool_bwd_part(proj, dyp, pool_w, pool_scale, tile_of, n_tiles)])


def _rows8(x):
    acc = x[0:8]
    for r in range(8, x.shape[0], 8):
        acc = acc + x[r:r + 8]
    return acc


IN_T = 512


def _in_bwd(x, dh, norm_w, w_pad, pieces, after):
    s = x.shape[0]
    t = IN_T
    widths = [D_HALF] * 6 + [N_IN_PAD - COL_BA]

    def body(*refs):
        x_ref, dh_ref, nw_ref, w_ref = refs[:4]
        p_refs = refs[4:4 + len(pieces)]
        gx_ref, gnw_ref = refs[5 + len(pieces):]

        @pl.when(pl.program_id(0) == 0)
        def _():
            gnw_ref[...] = jnp.zeros_like(gnw_ref)

        dn = jnp.zeros((t, D_MODEL), F32)
        col = 0
        for p_ref, wd in zip(p_refs, widths):
            dn = dn + _bdot_nt(p_ref[...], w_ref[:, col:col + wd])
            col += wd
        xv = x_ref[...]
        r = lax.rsqrt(jnp.mean(xv * xv, axis=-1, keepdims=True) + EPS)
        xhat = xv * r
        gnw_ref[...] += _colsum(dn * xhat)
        dxh = dn * nw_ref[...]
        gx_ref[...] = dh_ref[...] + r * (dxh - xhat * jnp.mean(dxh * xhat, axis=-1, keepdims=True))

    wide = pl.BlockSpec((t, D_MODEL), lambda i: (i, 0))
    return _call(
        body, name="in_bwd", grid=(s // t,),
        in_specs=[wide, wide, pl.BlockSpec((1, D_MODEL), lambda i: (0, 0)),
                  pl.BlockSpec((D_MODEL, N_IN_PAD), lambda i: (0, 0))]
                 + [pl.BlockSpec((t, wd), lambda i: (i, 0)) for wd in widths] + [pl.BlockSpec(memory_space=pl.ANY)],
        out_specs=[wide, pl.BlockSpec((1, D_MODEL), lambda i: (0, 0))],
        out_shape=[_sds((s, D_MODEL)), _sds((1, D_MODEL))],
        compiler_params=_params("arbitrary"),
    )(x, dh, norm_w, w_pad, *pieces, after)


def _adamw_shard(name, w, g_own, g_got, cidx, m, v):
    _, r, c = w.shape
    half = r // 2
    rows = 256 if half % 256 == 0 else half
    per_half = half // rows

    def body(c_ref, w_ref, go_ref, gg_ref, m_ref, v_ref, gout_ref, d_ref, nm_ref, nv_ref):
        mine = (pl.program_id(0) // per_half) == c_ref[0]
        gv = jnp.where(mine, go_ref[:, :c], gg_ref[:, :c])
        gout_ref[0] = gv
        mn = ADAM_B1 * m_ref[0] + (1.0 - ADAM_B1) * gv
        vn = ADAM_B2 * v_ref[0] + (1.0 - ADAM_B2) * (gv * gv)
        m_hat = mn / (1.0 - ADAM_B1 ** ADAM_STEP)
        v_hat = vn / (1.0 - ADAM_B2 ** ADAM_STEP)
        d_ref[0] = -ADAM_LR * (m_hat / (jnp.sqrt(v_hat) + ADAM_EPS) + ADAM_WD * w_ref[0])
        nm_ref[0] = mn
        nv_ref[0] = vn

    blk = pl.BlockSpec((1, rows, c), lambda i, c_ref: (0, i, 0))
    gblk = pl.BlockSpec((rows, g_own.shape[1]), lambda i, c_ref: (i % per_half, 0))
    return _call(
        body, name=name,
        grid_spec=pltpu.PrefetchScalarGridSpec(
            num_scalar_prefetch=1, grid=(2 * per_half,),
            in_specs=[blk, gblk, gblk, blk, blk], out_specs=[blk] * 4),
        out_shape=[_sds((1, r, c))] * 4,
        compiler_params=_params("arbitrary"),
    )(cidx, w, g_own, g_got, m, v)


def _adamw_tiles(name, w, g, m, v):
    n = w.shape[0]
    nb = 77 if n % 77 == 0 else n

    def body(w_ref, g_ref, m_ref, v_ref, d_ref, nm_ref, nv_ref):
        gv = g_ref[...]
        mn = ADAM_B1 * m_ref[...] + (1.0 - ADAM_B1) * gv
        vn = ADAM_B2 * v_ref[...] + (1.0 - ADAM_B2) * (gv * gv)
        m_hat = mn / (1.0 - ADAM_B1 ** ADAM_STEP)
        v_hat = vn / (1.0 - ADAM_B2 ** ADAM_STEP)
        d_ref[...] = -ADAM_LR * (m_hat / (jnp.sqrt(v_hat) + ADAM_EPS) + ADAM_WD * w_ref[...])
        nm_ref[...] = mn
        nv_ref[...] = vn

    blk = pl.BlockSpec((nb, 8, HEAD), lambda i: (i, 0, 0))
    return _call(
        body, name=name, grid=(n // nb,),
        in_specs=[blk] * 4, out_specs=[blk] * 3, out_shape=[_sds(w.shape)] * 3,
        compiler_params=_params("arbitrary"),
    )(w, g, m, v)


def _exchange(name, inputs, out_shapes, phases):
    n_in = len(inputs)
    n_out = len(out_shapes)
    n_cp = sum(len(p) for p in phases)

    def body(*refs):
        ins, outs = refs[:n_in], refs[n_in:n_in + n_out]
        send, recv = refs[n_in + n_out:]
        pos = (lax.axis_index("x"), lax.axis_index("y"), lax.axis_index("c"))
        k = 0
        for phase in phases:
            cps = []
            for src, dst, target in phase:
                cps.append(pltpu.make_async_remote_copy(
                    src_ref=src(ins, outs, pos), dst_ref=dst(ins, outs, pos), send_sem=send.at[k], recv_sem=recv.at[k],
                    device_id=target(pos), device_id_type=pl.DeviceIdType.MESH))
                k += 1
            for cp in cps:
                cp.start()
            for cp in cps:
                cp.wait()

    anyspec = pl.BlockSpec(memory_space=pl.ANY)
    return _call(
        body, name=name,
        in_specs=[anyspec] * n_in, out_specs=[anyspec] * n_out, out_shape=list(out_shapes),
        scratch_shapes=[pltpu.SemaphoreType.DMA((n_cp,)), pltpu.SemaphoreType.DMA((n_cp,))],
    )(*inputs)


def _exchange_start(name, inputs, out_shapes, copies):
    n_in, n_out, n_cp = len(inputs), len(out_shapes), len(copies)

    def body(*refs):
        ins, lands = refs[:n_in], refs[n_in:n_in + n_out]
        sems = refs[n_in + n_out:n_in + n_out + 2 * n_cp]
        token = refs[-1]
        pos = (lax.axis_index("x"), lax.axis_index("y"), lax.axis_index("c"))
        for k, (src, dst, target) in enumerate(copies):
            pltpu.make_async_remote_copy(
                src_ref=src(ins, lands, pos), dst_ref=dst(ins, lands, pos), send_sem=sems[2 * k], recv_sem=sems[2 * k + 1],
                device_id=target(pos), device_id_type=pl.DeviceIdType.MESH).start()
        token[...] = jnp.zeros_like(token)

    hbm = pl.BlockSpec(memory_space=pltpu.HBM)
    sem = pl.BlockSpec(memory_space=pltpu.SEMAPHORE)
    bufs = list(inputs) + [lax.empty(o.shape, o.dtype) for o in out_shapes]
    outs = _call(
        body, name=name,
        out_shape=tuple([pltpu.SemaphoreType.DMA(())] * (2 * n_cp) + [pltpu.HBM(b.shape, b.dtype) for b in bufs]
                        + [_sds((8, HEAD))]),
        in_specs=[hbm] * len(bufs),
        out_specs=tuple([sem] * (2 * n_cp) + [hbm] * len(bufs) + [pl.BlockSpec(memory_space=pltpu.VMEM)]),
        input_output_aliases={i: 2 * n_cp + i for i in range(len(bufs))},
        compiler_params=pltpu.CompilerParams(has_side_effects=pltpu.SideEffectType.DATAFLOW_SIDE_EFFECTING),
    )(*[pltpu.with_memory_space_constraint(b, pltpu.HBM) for b in bufs])
    return outs[:2 * n_cp], outs[2 * n_cp:2 * n_cp + n_in], outs[2 * n_cp + n_in:-1], outs[-1]


def _exchange_wait(name, sems, sources, lands, copies, after):
    n_in, n_out, n_cp = len(sources), len(lands), len(copies)

    def body(*refs):
        ins, zones = refs[:n_in], refs[n_in:n_in + n_out]
        sem_refs = refs[n_in + n_out:n_in + n_out + 2 * n_cp]
        pos = (lax.axis_index("x"), lax.axis_index("y"), lax.axis_index("c"))
        for k, (src, dst, target) in enumerate(copies):
            cp = pltpu.make_async_remote_copy(
                src_ref=src(ins, zones, pos), dst_ref=dst(ins, zones, pos), send_sem=sem_refs[2 * k],
                recv_sem=sem_refs[2 * k + 1], device_id=target(pos), device_id_type=pl.DeviceIdType.MESH)
            cp.wait_send()
            cp.wait_recv()

    hbm = pl.BlockSpec(memory_space=pltpu.HBM)
    sem = pl.BlockSpec(memory_space=pltpu.SEMAPHORE)
    bufs = list(sources) + list(lands)
    outs = _call(
        body, name=name,
        out_shape=tuple(pltpu.HBM(b.shape, b.dtype) for b in bufs),
        in_specs=[hbm] * len(bufs) + [sem] * (2 * n_cp) + [pl.BlockSpec(memory_space=pl.ANY)],
        out_specs=tuple([hbm] * len(bufs)),
        input_output_aliases={i: i for i in range(len(bufs))},
        compiler_params=pltpu.CompilerParams(has_side_effects=pltpu.SideEffectType.DATAFLOW_SIDE_EFFECTING),
    )(*bufs, *sems, after)
    return outs[:n_in], outs[n_in:]


def _allreduce_tile(name, v):
    def body(v_ref, out_ref, slots, send, recv):
        x, y, c = lax.axis_index("x"), lax.axis_index("y"), lax.axis_index("c")
        me = 4 * x + 2 * y + c
        slots[me] = v_ref[...]
        cps = []
        for k in range(1, 8):
            peer = (x ^ (k >> 2), y ^ ((k >> 1) & 1), c ^ (k & 1))
            cps.append(pltpu.make_async_remote_copy(
                src_ref=v_ref, dst_ref=slots.at[me], send_sem=send.at[k - 1], recv_sem=recv.at[k - 1],
                device_id=peer, device_id_type=pl.DeviceIdType.MESH))
        for cp in cps:
            cp.start()
        for cp in cps:
            cp.wait()
        acc = slots[0]
        for i in range(1, 8):
            acc = acc + slots[i]
        out_ref[...] = acc

    vm = pl.BlockSpec(memory_space=pltpu.VMEM)
    return _call(
        body, name=name, in_specs=[vm], out_specs=vm, out_shape=_sds(v.shape),
        scratch_shapes=[pltpu.VMEM((8,) + v.shape, F32), pltpu.SemaphoreType.DMA((7,)), pltpu.SemaphoreType.DMA((7,))],
    )(v)


def _chip(pos):
    return 2 * pos[0] + pos[1]


def _other_chip(pos, mask):
    x, y, c = pos
    return (x ^ (mask >> 1), y ^ (mask & 1), c)


def _sibling(pos):
    return (pos[0], pos[1], 1 - pos[2])


def _gather_weights(wb, cb):
    rows = wb.shape[0] // 2
    x_nb, y_nb, diag = CHIP_MASKS

    def part(pos, mask, quarter=None):
        start = pos[2] * rows if quarter is None else pos[2] * rows + quarter * (rows // 2)
        return lambda outs: outs[0].at[_chip(pos) ^ mask, pl.ds(start, rows if quarter is None else rows // 2)]

    def passed_on(mask, to, quarter=None):
        return (lambda ins, outs, pos: part(pos, mask, quarter)(outs), lambda ins, outs, pos: part(pos, mask, quarter)(outs), to)

    first = [(lambda ins, outs, pos: ins[0].at[pl.ds(pos[2] * rows, rows)], lambda ins, outs, pos: part(pos, 0)(outs),
              functools.partial(_other_chip, mask=mask)) for mask in (x_nb, y_nb)]
    first += [(lambda ins, outs, pos: ins[1], lambda ins, outs, pos: outs[1].at[_chip(pos)],
               functools.partial(_other_chip, mask=mask)) for mask in CHIP_MASKS]
    second = [passed_on(x_nb, functools.partial(_other_chip, mask=y_nb), quarter=0),
              passed_on(y_nb, functools.partial(_other_chip, mask=x_nb), quarter=1),
              passed_on(x_nb, _sibling), passed_on(y_nb, _sibling)]
    third = [passed_on(diag, _sibling)]
    return _exchange("gather_weights", [wb, cb], [_sds((4,) + wb.shape, wb.dtype), _sds((4,) + cb.shape, cb.dtype)],
                     [first, second, third])


def _assemble_w_in(gw, wb, jidx):
    m = gw.shape[1]

    def body(j_ref, g_ref, wb_ref, o_ref):
        step = pl.program_id(0)

        @pl.when(step == 0)
        def _():
            o_ref[...] = jnp.zeros_like(o_ref)

        blk = jnp.where(step == j_ref[0], wb_ref[...], g_ref[0]).astype(F32)
        lane = lax.broadcasted_iota(I32, (m, BLK_IN_PAD), 1)
        for j in range(4):
            @pl.when(step == j)
            def _(j=j):
                base = j * BLK_IN // HEAD * HEAD
                shift = j * BLK_IN - base
                moved = pltpu.roll(blk, shift, 1) if shift else blk
                window = o_ref[:, base:base + BLK_IN_PAD].astype(F32)
                mine = (lane >= shift) & (lane < shift + BLK_IN)
                o_ref[:, base:base + BLK_IN_PAD] = jnp.where(mine, moved, window).astype(BF16)

    return _call(
        body, name="assemble_w_in",
        grid_spec=pltpu.PrefetchScalarGridSpec(
            num_scalar_prefetch=1, grid=(4,),
            in_specs=[pl.BlockSpec((1, m, BLK_IN_PAD), lambda j, j_ref: (j, 0, 0)),
                      pl.BlockSpec((m, BLK_IN_PAD), lambda j, j_ref: (0, 0))],
            out_specs=pl.BlockSpec((m, N_IN_PAD), lambda j, j_ref: (0, 0))),
        out_shape=_sds((m, N_IN_PAD), BF16),
        compiler_params=_params("arbitrary"),
    )(jidx, gw, wb)


def _gather_blocks(ob):
    copies = [(lambda ins, outs, pos: ins[0], lambda ins, outs, pos: outs[0].at[_chip(pos)],
               functools.partial(_other_chip, mask=mask)) for mask in CHIP_MASKS]
    return [_sds((4,) + ob.shape, ob.dtype)], copies


def _reduce_sibling(name, arrays, cidx):
    n = len(arrays)
    halves = [a.shape[:-2] + (a.shape[-2] // 2, a.shape[-1]) for a in arrays]
    pieces = [(a, j) for a in range(n) for j in (range(arrays[a].shape[0]) if arrays[a].ndim == 3 else [None])]

    def body(c_ref, *refs):
        del c_ref
        whole, own, outs, land = refs[:n], refs[n:2 * n], refs[2 * n:3 * n], refs[3 * n:4 * n]
        send, recv = refs[4 * n:]
        pos = (lax.axis_index("x"), lax.axis_index("y"), lax.axis_index("c"))
        cps = []
        for k, (a, j) in enumerate(pieces):
            rows = pl.ds((1 - pos[2]) * halves[a][-2], halves[a][-2])
            cps.append(pltpu.make_async_remote_copy(
                src_ref=whole[a].at[rows] if j is None else whole[a].at[j, rows],
                dst_ref=land[a] if j is None else land[a].at[j],
                send_sem=send.at[k], recv_sem=recv.at[k],
                device_id=_sibling(pos), device_id_type=pl.DeviceIdType.MESH))
        for cp in cps:
            cp.start()
        for cp, (a, j) in zip(cps, pieces):
            cp.wait()
            at = Ellipsis if j is None else j
            outs[a][at] = (own[a][at].astype(F32) + land[a][at].astype(F32)).astype(outs[a].dtype)

    def mine(shape):
        if len(shape) == 3:
            return pl.BlockSpec(shape, lambda i, c_ref: (0, c_ref[0], 0))
        return pl.BlockSpec(shape, lambda i, c_ref: (c_ref[0], 0))

    return _call(
        body, name=name,
        grid_spec=pltpu.PrefetchScalarGridSpec(
            num_scalar_prefetch=1, grid=(1,),
            in_specs=[pl.BlockSpec(memory_space=pl.ANY)] * n + [mine(h) for h in halves],
            out_specs=[pl.BlockSpec(h, lambda i, c_ref, nd=len(h): (0,) * nd) for h in halves],
            scratch_shapes=[pltpu.VMEM(h, a.dtype) for h, a in zip(halves, arrays)]
                           + [pltpu.SemaphoreType.DMA((len(pieces),)), pltpu.SemaphoreType.DMA((len(pieces),))]),
        out_shape=[_sds(h, a.dtype) for h, a in zip(halves, arrays)],
        compiler_params=_params("arbitrary"),
    )(cidx, *arrays, *arrays)


def _to_other_chips(arrays, blocked):
    def src(ins, outs, pos, a, mask):
        return ins[a].at[_chip(pos) ^ mask] if blocked[a] else ins[a]

    outs = [_sds((3,) + (a.shape[1:] if b else a.shape), a.dtype) for a, b in zip(arrays, blocked)]
    copies = []
    for mi, mask in enumerate(CHIP_MASKS):
        for a in range(len(arrays)):
            copies.append((functools.partial(src, a=a, mask=mask), lambda ins, outs, pos, a=a, mi=mi: outs[a].at[mi],
                           functools.partial(_other_chip, mask=mask)))
    return outs, copies


def _sum_chips_swap(name, owns, gots, jidx, blocked):
    n = len(owns)
    shapes = [g.shape[-2:] for g in gots]
    own3 = [o if b else o.reshape((1,) + o.shape) for o, b in zip(owns, blocked)]

    def body(j_ref, *refs):
        del j_ref
        own, got, mine, theirs = refs[:n], refs[n:2 * n], refs[2 * n:3 * n], refs[3 * n:4 * n]
        send, recv = refs[4 * n:]
        pos = (lax.axis_index("x"), lax.axis_index("y"), lax.axis_index("c"))
        cps = []
        for a in range(n):
            mine[a][...] = ((own[a][0].astype(F32) + got[a][0].astype(F32))
                            + (got[a][1].astype(F32) + got[a][2].astype(F32)))
            cps.append(pltpu.make_async_remote_copy(
                src_ref=mine[a], dst_ref=theirs[a], send_sem=send.at[a], recv_sem=recv.at[a],
                device_id=_sibling(pos), device_id_type=pl.DeviceIdType.MESH))
            cps[-1].start()
        for cp in cps:
            cp.wait()

    own_spec = lambda s, b: pl.BlockSpec((1,) + s, (lambda i, j_ref: (j_ref[0], 0, 0)) if b else (lambda i, j_ref: (0, 0, 0)))
    whole = lambda s: pl.BlockSpec(s, lambda i, j_ref: (0, 0))
    outs = _call(
        body, name=name,
        grid_spec=pltpu.PrefetchScalarGridSpec(
            num_scalar_prefetch=1, grid=(1,),
            in_specs=[own_spec(s, b) for s, b in zip(shapes, blocked)]
                     + [pl.BlockSpec((3,) + s, lambda i, j_ref: (0, 0, 0)) for s in shapes],
            out_specs=[whole(s) for s in shapes] * 2,
            scratch_shapes=[pltpu.SemaphoreType.DMA((n,)), pltpu.SemaphoreType.DMA((n,))]),
        out_shape=[_sds(s) for s in shapes] * 2,
        compiler_params=_params("arbitrary"),
    )(jidx, *own3, *gots)
    return outs[:n], outs[n:]


def _local_step(x, target, w_pad, w_out, conv_w, norm_w, pool_w, pool_scale, a_log, dt_bias, dn_norm_w, final_norm_w,
                after):
    proj, n_t, qn, kn, vs, beta, g, yq, yk, yv, y_pool = _front_fwd(
        x, norm_w, w_pad, conv_w, a_log, dt_bias, pool_w, pool_scale, after)
    w, att, qd, kd, tm, cd, o, vn, st = _delta_fwd(qn, kn, vs, beta, g)
    w_out = w_out(o) if callable(w_out) else w_out
    g_wout, dh, dyp, do, ddz, loss, g_fnw, g_dnw = _out_fwd_bwd(x, y_pool, o, proj, target, w_out, dn_norm_w, final_norm_w)
    dqn, dkn, dvs, dbeta, dg = _delta_bwd(do, vn, qd, kd, w, att, cd, st, qn, kn, vs, beta, g, tm)
    (dcq, dck, dcv, dba, g_cw, g_sm), (dpu, dpz, g_pw, g_ps) = _conv_pool_bwd(
        proj, (yq, yk, yv), conv_w, a_log, dt_bias, dqn, dkn, dvs, dbeta, dg, dyp, pool_w, pool_scale)
    pieces = [dpu, dpz, dcq, dck, dcv, ddz, dba]
    g_win = _grad_w_in(n_t, pieces)
    small = dict(norm_w=jnp.zeros_like(norm_w), pool_w=g_pw, pool_scale=g_ps, conv_w=g_cw[:CONV_K],
                 a_log=g_sm[0:1, 0:N_HEADS], dt_bias=g_sm[0:1, N_HEADS:2 * N_HEADS], dn_norm_w=g_dnw, final_norm_w=g_fnw)
    return loss[0, 0], g_win, g_wout, small, dh, pieces


SMALL_LAYOUT = (("pool_w", 512, HEAD, (1, N_HEADS, HEAD, HEAD)), ("final_norm_w", 8, HEAD, (D_MODEL,)),
                ("pool_scale", 4, HEAD, (1, D_HALF)), ("conv_w", 48, HEAD, (1, CONV_K, 3 * D_HALF)),
                ("dn_norm_w", 1, HEAD, (1, HEAD)), ("a_log", 1, N_HEADS, (1, N_HEADS)), ("dt_bias", 1, N_HEADS, (1, N_HEADS)),
                ("loss", 1, 1, ()))


def _small_offsets():
    offs, r = {}, 0
    for name, rows, _, _ in SMALL_LAYOUT:
        offs[name] = r
        r += -(-rows // 8) * 8
    assert r <= SMALL_ROWS
    return offs


def _pack_small(t):
    parts = []
    for name, rows, lanes, _ in SMALL_LAYOUT:
        a = t.get(name, jnp.zeros((1,), F32)).reshape(rows, lanes)
        parts.append(jnp.pad(a, ((0, -(-rows // 8) * 8 - rows), (0, HEAD - lanes))))
    buf = jnp.concatenate(parts, axis=0)
    return jnp.pad(buf, ((0, SMALL_ROWS - buf.shape[0]), (0, 0)))


def _adamw_small(w, g_own, g_got, cidx, m, v):
    offs = _small_offsets()
    names = [e[0] for e in SMALL_LAYOUT]
    n = len(names)

    def body(c_ref, w_ref, go_ref, gg_ref, m_ref, v_ref, *outs):
        own_low = c_ref[0] == 0
        gv = jnp.concatenate([jnp.where(own_low, go_ref[...], gg_ref[...]), jnp.where(own_low, gg_ref[...], go_ref[...])], axis=0)
        mn = ADAM_B1 * m_ref[...] + (1.0 - ADAM_B1) * gv
        vn = ADAM_B2 * v_ref[...] + (1.0 - ADAM_B2) * (gv * gv)
        m_hat = mn / (1.0 - ADAM_B1 ** ADAM_STEP)
        v_hat = vn / (1.0 - ADAM_B2 ** ADAM_STEP)
        dl = -ADAM_LR * (m_hat / (jnp.sqrt(v_hat) + ADAM_EPS) + ADAM_WD * w_ref[...])
        for kind, arr in enumerate((gv, dl, mn, vn)):
            for i, (name, rows, lanes, _) in enumerate(SMALL_LAYOUT):
                outs[kind * n + i][...] = arr[offs[name]:offs[name] + rows, :lanes]

    whole = lambda shape: pl.BlockSpec(shape, lambda i, c_ref: (0,) * len(shape))
    out_shapes = [_sds((rows, lanes)) for _, rows, lanes, _ in SMALL_LAYOUT] * 4
    res = _call(
        body, name="adamw_small",
        grid_spec=pltpu.PrefetchScalarGridSpec(
            num_scalar_prefetch=1, grid=(1,),
            in_specs=[whole(w.shape), whole(g_own.shape), whole(g_got.shape), whole(m.shape), whole(v.shape)],
            out_specs=[whole(o.shape) for o in out_shapes]),
        out_shape=out_shapes,
        compiler_params=_params("arbitrary"),
    )(cidx, w, g_own, g_got, m, v)
    return [{name: res[kind * n + i].reshape(shape) for i, (name, _, _, shape) in enumerate(SMALL_LAYOUT)}
            for kind in range(4)]


def kernel(x, norm_w, w_in, pool_w, pool_scale, conv_w, a_log, dt_bias, dn_norm_w, w_out, final_norm_w, loss_target, m_norm_w, m_w_in, m_pool_w, m_pool_scale, m_conv_w, m_a_log, m_dt_bias, m_dn_norm_w, m_w_out, m_final_norm_w, v_norm_w, v_w_in, v_pool_w, v_pool_scale, v_conv_w, v_a_log, v_dt_bias, v_dn_norm_w, v_w_out, v_final_norm_w):
    cidx = lax.axis_index("c").astype(I32).reshape(1)
    jidx = (2 * lax.axis_index("x") + lax.axis_index("y")).astype(I32)

    wb = jnp.pad(w_in[0].astype(BF16), ((0, 0), (0, BLK_IN_PAD - BLK_IN)))
    ob = w_out[0].astype(BF16)
    gw, gc = _gather_weights(wb, conv_w[0])
    mine = lambda j: jidx == j
    w_pad = _assemble_w_in(gw, wb, jidx.reshape(1))
    cw_full = jnp.concatenate([jnp.where(mine(j), conv_w[0], gc[j]) for j in range(4)], axis=1)

    lands_o, copies_o = _gather_blocks(ob)
    sems_o, ob_thru, zones_o, token_o = _exchange_start("gather_w_out_start", [ob], lands_o, copies_o)

    def w_out_full(after):
        (own,), (got,) = _exchange_wait("gather_w_out_wait", sems_o, ob_thru, zones_o, copies_o, after)
        return jnp.where((jnp.arange(4) == jidx)[:, None, None], own[None], got).reshape(D_MODEL, D_MODEL)

    loss, g_win, g_wout, small, dh, pieces = _local_step(
        x[0], loss_target[0], w_pad, w_out_full, cw_full, norm_w, pool_w[0], pool_scale, a_log, dt_bias,
        dn_norm_w, final_norm_w.reshape(1, D_MODEL), token_o)
    small["loss"] = loss

    blocks_out = g_wout.reshape(4, BLK_OUT, D_MODEL)
    full = [g_win, blocks_out, _pack_small(small)]
    chip_sum = _reduce_sibling("reduce_sibling", full, cidx)
    blocked = [True, True, False]
    lands, copies = _to_other_chips(chip_sum, blocked)
    sems, chip_sum, zones, token = _exchange_start("reduce_chips_start", chip_sum, lands, copies)
    gx, g_nw = _in_bwd(x[0], dh, norm_w, w_pad, pieces, token)
    g_nw = _allreduce_tile("reduce_norm_w", g_nw.reshape(8, HEAD)).reshape(1, D_MODEL)
    chip_sum, from_chips = _exchange_wait("reduce_chips_wait", sems, chip_sum, zones, copies, gx)
    halves, other_halves = _sum_chips_swap("sum_chips_swap", chip_sum, from_chips, jidx.reshape(1), blocked)

    weights = dict(norm_w=norm_w, w_in=w_in, pool_w=pool_w, pool_scale=pool_scale, conv_w=conv_w, a_log=a_log,
                   dt_bias=dt_bias, dn_norm_w=dn_norm_w, w_out=w_out, final_norm_w=final_norm_w)
    ms = dict(norm_w=m_norm_w, w_in=m_w_in, pool_w=m_pool_w, pool_scale=m_pool_scale, conv_w=m_conv_w, a_log=m_a_log,
              dt_bias=m_dt_bias, dn_norm_w=m_dn_norm_w, w_out=m_w_out, final_norm_w=m_final_norm_w)
    vs = dict(norm_w=v_norm_w, w_in=v_w_in, pool_w=v_pool_w, pool_scale=v_pool_scale, conv_w=v_conv_w, a_log=v_a_log,
              dt_bias=v_dt_bias, dn_norm_w=v_dn_norm_w, w_out=v_w_out, final_norm_w=v_final_norm_w)
    names = ["norm_w", "w_in", "pool_w", "pool_scale", "conv_w", "a_log", "dt_bias", "dn_norm_w", "w_out", "final_norm_w"]
    small_names = [n for n in names if n not in ("w_in", "w_out")]

    def pack(t):
        conv = lax.dynamic_update_slice_in_dim(jnp.zeros((CONV_K, 3 * D_HALF), F32), t["conv_w"][0], jidx * BLK_CONV, axis=1)
        return _pack_small({**{n: t[n] for n in small_names if n != "conv_w"}, "conv_w": conv})

    results = [{}, {}, {}, {}]
    to_tiles = lambda a: jnp.transpose(a, (2, 0, 1)).reshape(BLK_IN, 8, HEAD)
    from_tiles = lambda a: jnp.transpose(a, (1, 2, 0)).reshape(1, D_MODEL, BLK_IN)
    lo = jnp.where(cidx[0] == 0, halves[0], other_halves[0])
    hi = jnp.where(cidx[0] == 0, other_halves[0], halves[0])
    g_tiles = jnp.concatenate([lo[:, :BLK_IN].T, hi[:, :BLK_IN].T], axis=1).reshape(BLK_IN, 8, HEAD)
    outs = _adamw_tiles("adamw_w_in", to_tiles(w_in), g_tiles, to_tiles(m_w_in), to_tiles(v_w_in))
    for res, o in zip(results, (g_tiles,) + tuple(outs)):
        res["w_in"] = from_tiles(o)
    outs = _adamw_shard("adamw_w_out", w_out, halves[1], other_halves[1], cidx, m_w_out, v_w_out)
    for res, o in zip(results, outs):
        res["w_out"] = o
    outs = _adamw_small(pack(weights), halves[2], other_halves[2], cidx, pack(ms), pack(vs))
    for res, got in zip(results, outs):
        got["conv_w"] = lax.dynamic_slice_in_dim(got["conv_w"], jidx * BLK_CONV, BLK_CONV, axis=2)
        res.update(got)
    one_tile = lambda a: a.reshape(1, 8, HEAD)
    outs = _adamw_tiles("adamw_norm_w", one_tile(norm_w), one_tile(g_nw), one_tile(m_norm_w), one_tile(v_norm_w))
    for res, o in zip(results, (g_nw,) + tuple(outs)):
        res["norm_w"] = o.reshape(1, D_MODEL)
    grads, delta, new_m, new_v = results

    return (grads["loss"], gx[None], *[grads[n] for n in names], *[delta[n] for n in names],
            *[new_m[n] for n in names], *[new_v[n] for n in names])
```

```python
import functools

import jax
import jax.numpy as jnp
import numpy as np
from jax import lax
from jax.experimental import pallas as pl
from jax.experimental.pallas import tpu as pltpu

F32 = jnp.float32
BF16 = jnp.bfloat16
I32 = jnp.int32

D_MODEL = 1024
D_HALF = 512
N_HEADS = 4
HEAD = 128
CHUNK = 64
PAIR = 2 * CHUNK
WINDOWS = (2, 4, 8, 16)
CONV_K = 4
EPS = 1e-6
N_IN = 3080
N_IN_PAD = 3200
BLK_IN = 770
BLK_IN_PAD = 896
BLK_OUT = 256
BLK_CONV = 384
COL_BA = 3072
QK_SCALE = HEAD ** -0.5
SMALL_ROWS = 608
VMEM_LIMIT = 56 * 1024 * 1024

ADAM_LR = 0.001
ADAM_B1 = 0.9
ADAM_B2 = 0.999
ADAM_EPS = 1e-08
ADAM_WD = 0.01
ADAM_STEP = 10

CHIP_MASKS = (2, 1, 3)
HEADS = range(N_HEADS)
HEAD_COLS = [slice(h * HEAD, (h + 1) * HEAD) for h in HEADS]


def _call(body, **kw):
    return pl.pallas_call(body, **kw)


def _params(*sem):
    return pltpu.CompilerParams(dimension_semantics=sem, vmem_limit_bytes=VMEM_LIMIT)


def _sds(shape, dtype=F32):
    return jax.ShapeDtypeStruct(shape, dtype)


def _bdot(a, b):
    return jnp.dot(a.astype(BF16), b.astype(BF16), preferred_element_type=F32)


def _bdot_nt(a, b):
    return lax.dot_general(a.astype(BF16), b.astype(BF16), (((1,), (1,)), ((), ())), preferred_element_type=F32)


def _bdot_tn(a, b):
    return lax.dot_general(a.astype(BF16), b.astype(BF16), (((0,), (0,)), ((), ())), preferred_element_type=F32)


def _side(a, b):
    return jnp.concatenate([a.astype(BF16), b.astype(BF16)], axis=1)


def _stack(a, b):
    return jnp.concatenate([a.astype(BF16), b.astype(BF16)], axis=0)


def _split(a):
    hi = a.astype(BF16)
    lo = (a - hi.astype(F32)).astype(BF16)
    return hi, lo


def _mask_dot(m, b):
    n = b.shape[1]
    both = jnp.dot(m, jnp.concatenate(_split(b), axis=1), preferred_element_type=F32)
    return both[:, :n] + both[:, n:]


def _sigmoid(x):
    return 0.5 * jnp.tanh(0.5 * x) + 0.5


def _softplus(x):
    return jnp.maximum(x, 0.0) + jnp.log(1.0 + jnp.exp(-jnp.abs(x)))


def _rowsum(x):
    return jnp.sum(x, axis=-1, keepdims=True)


def _colsum(x):
    return jnp.sum(x, axis=0, keepdims=True)


def _shift_down(xv, prev8, k):
    r = pltpu.roll(xv, k, 0)
    q = pltpu.roll(prev8, k, 0)
    row = lax.broadcasted_iota(I32, prev8.shape, 0)
    top = jnp.where(row < k, q, r[0:8])
    return jnp.concatenate([top, r[8:]], axis=0)


def _shift_up(xv, next8, k):
    t = xv.shape[0]
    r = pltpu.roll(xv, t - k, 0)
    q = pltpu.roll(next8, 8 - k, 0)
    row = lax.broadcasted_iota(I32, next8.shape, 0)
    bot = jnp.where(row >= 8 - k, q, r[t - 8:])
    return jnp.concatenate([r[:t - 8], bot], axis=0)


INTRA_PAIRS = 2
UNITS = [(pp, h) for pp in range(INTRA_PAIRS) for h in HEADS]


def _heads_of(ref, rows=PAIR, units=UNITS):
    return [ref[pp * rows:(pp + 1) * rows, HEAD_COLS[h]] for pp, h in units]


def _put_heads_of(ref, vals, rows=PAIR, units=UNITS):
    for (pp, h), v in zip(units, vals):
        ref[pp * rows:(pp + 1) * rows, HEAD_COLS[h]] = v.astype(ref.dtype)


_heads = _heads_of
_put_heads = _put_heads_of


def _each(fn, *lists):
    return [fn(*args) for args in zip(*lists)]


def _pool_bands(t, anti=False):
    r = np.arange(t)[:, None]
    c = np.arange(t + HEAD)[None, :]
    d = (c - r) if anti else (r - c + HEAD)
    return jnp.asarray(np.stack([(d >= 0) & (d < w) for w in WINDOWS]), BF16)


def _pool_mix(u, halo, z, pw, bands, row0):
    t = u[0].shape[0]
    rows = row0 + lax.broadcasted_iota(I32, (t, 1), 0) + 1
    cnt = [jnp.minimum(rows, w).astype(F32) for w in WINDOWS]
    win = _each(lambda b, h, v: _mask_dot(b, jnp.concatenate([h, v], axis=0)), bands, halo, u)
    mix = _each(lambda a, c, v: a / c - v, win, cnt, u)
    mixed = _each(_bdot, mix, pw)
    return mix, mixed, _each(_sigmoid, z), cnt


POOL_T = 256


def _conv_taps(xv, prev8):
    return [_shift_down(xv, prev8, CONV_K - 1 - j) for j in range(CONV_K - 1)] + [xv]


def _conv_pre(taps, cw):
    y = taps[CONV_K - 1] * cw[CONV_K - 1:CONV_K]
    for j in range(CONV_K - 2, -1, -1):
        y = y + taps[j] * cw[j:j + 1]
    return y


CONV_T = 256


def _conv_specs(t, tile_of=lambda i: i):
    tiles = [pl.BlockSpec((t, D_HALF), functools.partial(lambda i, p: (tile_of(i), 2 + p), p=p)) for p in range(3)]
    halos = [pl.BlockSpec((8, D_HALF),
                          functools.partial(lambda i, p: (jnp.maximum(tile_of(i) * (t // 8) - 1, 0), 2 + p), p=p))
             for p in range(3)]
    return tiles + halos


def _pair_masks():
    r = lax.broadcasted_iota(I32, (PAIR, PAIR), 0)
    c = lax.broadcasted_iota(I32, (PAIR, PAIR), 1)
    same = jnp.right_shift(r, 6) == jnp.right_shift(c, 6)
    return same, same & (r >= c), same & (r > c), r == c


def _interleave(*stage_lists):
    live = list(stage_lists)
    while live:
        for gen in list(live):
            try:
                next(gen)
            except StopIteration:
                live.remove(gen)


def _pair_common_stages(cm, qn, kn, vs, beta, g):
    same, incl, strict, eye = _pair_masks()
    incl_b = incl.astype(BF16)
    first = lax.broadcasted_iota(I32, (PAIR, HEAD), 0) < CHUNK
    cm.update(same=same, incl=incl, strict=strict, eye=eye)
    gc = _each(lambda gv: _mask_dot(incl_b, gv), g)
    q = _each(lambda v: v * QK_SCALE, qn)
    kb = _each(lambda k, b: k * b, kn, beta)
    cm.update(gc=gc, q=q, kb=kb, vb=_each(lambda v, b: v * b, vs, beta))
    yield
    both = _each(lambda a, b, c: _bdot_nt(_stack(a, b), c), kb, q, kn)
    cm.update(kk=[v[:PAIR] for v in both], qk=[v[PAIR:] for v in both])
    gc_row = _each(lambda v: _colsum(jnp.where(eye, v, 0.0)), gc)
    gl = _each(lambda v: jnp.where(first, v[CHUNK - 1:CHUNK], v[PAIR - 1:PAIR]), gc)
    egc = _each(jnp.exp, gc)
    cm.update(gl=gl, egc=egc,
              decay=_each(lambda v, r: jnp.where(incl, jnp.exp(jnp.where(incl, v - r, 0.0)), 0.0), gc, gc_row))
    yield
    cm.update(ekd=_each(lambda a, b: jnp.exp(a - b), gl, gc), cd=_each(jnp.exp, gl),
              kbg=_each(lambda k, e: k * e, kb, egc))
    yield


def _tri_inv_stages(out, a, eye_f):
    p = _each(lambda v: eye_f - v, a)
    x = _each(_bdot, a, a)
    yield
    for it in range(4):
        both = _each(lambda xv, pv: _bdot(xv, _side(pv, xv)), x, p)
        p = _each(lambda pv, b: pv + b[:, :PAIR], p, both)
        x = [b[:, PAIR:] for b in both]
        yield
    out["t"] = _each(lambda pv, xv: pv + _bdot(pv, xv), p, x)
    yield


def _chunk_scalar_spec(pairs=1, index=lambda i: (i, 0)):
    return pl.BlockSpec((16 * pairs, D_HALF), index)


SCAN_PAIRS = 2
SCAN_ROWS = SCAN_PAIRS * PAIR


def _delta_fwd(qn, kn, vs, beta, g):
    s = qn.shape[0]
    n_steps = s // SCAN_ROWS
    n_chunks = s // CHUNK
    assert INTRA_PAIRS == SCAN_PAIRS

    def body(qn_ref, kn_ref, vs_ref, beta_ref, g_ref, w_ref, att_ref, qd_ref, kd_ref, t_ref, cd_ref, o_ref, vn_ref, st_ref,
             state, u_s, w_s, att_s, qd_s, kd_s, cd_s):
        t = pl.program_id(0)

        @pl.when(t <= 1)
        def _():
            state[...] = jnp.zeros_like(state)

        @pl.when(t == 0)
        def _():
            for ref in (u_s, w_s, att_s, qd_s, kd_s, cd_s):
                ref[1] = jnp.zeros(ref.shape[1:], ref.dtype)

        cur = lax.rem(t, 2)
        prev = 1 - cur
        cols = list(enumerate(HEAD_COLS))

        def recurrence():
            sm = [state[h] for h in HEADS]
            for ci in range(2 * SCAN_PAIRS):
                rs = slice(ci * CHUNK, (ci + 1) * CHUNK)
                for h in HEADS:
                    st_ref[ci, h] = sm[h]
                both = [_bdot(jnp.concatenate([w_s[prev, rs, sl], qd_s[prev, rs, sl]], axis=0), sm[h]) for h, sl in cols]
                vn = [u_s[prev, rs, sl] - both[h][:CHUNK] for h, sl in cols]
                for h, sl in cols:
                    vn_ref[rs, sl] = vn[h].astype(BF16)
                    o_ref[rs, sl] = both[h][CHUNK:]
                yield
                sm = [sm[h] * cd_s[prev, ci * 8:ci * 8 + 1, sl] + _bdot_tn(kd_s[prev, rs, sl], vn[h]) for h, sl in cols]
                yield
            for h in HEADS:
                state[h] = sm[h]
            for pp in range(SCAN_PAIRS):
                rp = slice(pp * PAIR, (pp + 1) * PAIR)
                intra = [_bdot(att_s[prev, rp, sl], vn_ref[rp, sl]) for sl in HEAD_COLS]
                for h, sl in cols:
                    o_ref[rp, sl] += intra[h]
                yield

        def factors():
            kn = _heads(kn_ref)
            cm = {}
            yield from _pair_common_stages(cm, _heads(qn_ref), kn, _heads(vs_ref), _heads(beta_ref), _heads(g_ref))
            a = _each(lambda kk, d: jnp.where(cm["strict"], kk * d, 0.0), cm["kk"], cm["decay"])
            inv = {}
            yield from _tri_inv_stages(inv, a, cm["eye"].astype(F32))
            tm = inv["t"]
            uw = _each(lambda tv, a, b: _bdot(tv, _side(a, b)), tm, cm["vb"], cm["kbg"])
            res = dict(u=[v[:, :HEAD] for v in uw], w=[v[:, HEAD:] for v in uw],
                       att=_each(lambda a, b: a * b, cm["qk"], cm["decay"]),
                       qd=_each(lambda a, b: a * b, cm["q"], cm["egc"]), kd=_each(lambda a, b: a * b, kn, cm["ekd"]))
            yield
            _put_heads(t_ref, tm)
            for key, out, keep in (("w", w_ref, w_s), ("att", att_ref, att_s), ("qd", qd_ref, qd_s), ("kd", kd_ref, kd_s)):
                _put_heads(out, res[key])
                for (pp, h), v in zip(UNITS, res[key]):
                    keep[cur, pp * PAIR:(pp + 1) * PAIR, HEAD_COLS[h]] = v.astype(BF16)
            for (pp, h), v in zip(UNITS, res["u"]):
                u_s[cur, pp * PAIR:(pp + 1) * PAIR, HEAD_COLS[h]] = v
            for ci in range(2):
                for (pp, h), v in zip(UNITS, cm["cd"]):
                    rows8 = slice(pp * 16 + ci * 8, pp * 16 + (ci + 1) * 8)
                    cd_ref[rows8, HEAD_COLS[h]] = v[ci * CHUNK:ci * CHUNK + 8]
                    cd_s[cur, rows8, HEAD_COLS[h]] = v[ci * CHUNK:ci * CHUNK + 8]
            yield

        _interleave(recurrence(), factors())

    last = n_steps - 1
    now = lambda i: (jnp.minimum(i, last), 0)
    before = lambda i: (jnp.maximum(i - 1, 0), 0)
    rows = lambda index: pl.BlockSpec((SCAN_ROWS, D_HALF), index)
    slot = lambda r, dtype: pltpu.VMEM((2, r, D_HALF), dtype)
    return _call(
        body, name="delta_fwd", grid=(n_steps + 1,),
        in_specs=[rows(now)] * 5,
        out_specs=[rows(now)] * 5 + [_chunk_scalar_spec(SCAN_PAIRS, now), rows(before), rows(before),
                                     pl.BlockSpec((2 * SCAN_PAIRS, N_HEADS, HEAD, HEAD), lambda i: (jnp.maximum(i - 1, 0), 0, 0, 0))],
        out_shape=[_sds((s, D_HALF), BF16)] * 5 + [_sds((s // 8, D_HALF)), _sds((s, D_HALF)), _sds((s, D_HALF), BF16),
                                                  _sds((n_chunks, N_HEADS, HEAD, HEAD))],
        scratch_shapes=[pltpu.VMEM((N_HEADS, HEAD, HEAD), F32), slot(SCAN_ROWS, F32), slot(SCAN_ROWS, BF16),
                        slot(SCAN_ROWS, BF16), slot(SCAN_ROWS, BF16), slot(SCAN_ROWS, BF16), slot(16 * SCAN_PAIRS, F32)],
        compiler_params=_params("arbitrary"),
    )(qn, kn, vs, beta, g)


OUT_T = 512
OUT_ROWS = 256


def _out_fwd_bwd(x, y_pool, o, proj, target, w_out, dn_norm_w, final_norm_w):
    s = x.shape[0]
    t = OUT_T

    def body(x_ref, yp_ref, o_ref, z_ref, tg_ref, wo_ref, dnw_ref, fnw_ref,
             gwo_ref, dh_ref, dyp_ref, do_ref, dz_ref, loss_ref, gfn_ref, gdn_ref, y_ref, yt_ref, gwo_acc):
        @pl.when(pl.program_id(0) == 0)
        def _():
            loss_ref[...] = jnp.zeros_like(loss_ref)
            gfn_ref[...] = jnp.zeros_like(gfn_ref)
            gdn_ref[...] = jnp.zeros_like(gdn_ref)
            gwo_acc[...] = jnp.zeros_like(gwo_acc)

        dnw = dnw_ref[...]
        fnw = fnw_ref[...]

        def stages(rows, lead):
            for _ in range(lead):
                yield
            ypv = yp_ref[rows]
            y_ref[rows, :D_HALF] = ypv.astype(BF16)
            yt_ref[:D_HALF, rows] = ypv.T.astype(BF16)
            keep = []
            for h in HEADS:
                ov = o_ref[rows, HEAD_COLS[h]]
                zv = z_ref[rows, HEAD_COLS[h]]
                ro = lax.rsqrt(jnp.mean(ov * ov, axis=-1, keepdims=True) + EPS)
                ohat = ov * ro
                sg = _sigmoid(zv)
                keep.append((ro, ohat, zv, sg))
                ydn = ohat * dnw * (zv * sg)
                y_ref[rows, D_HALF + h * HEAD:D_HALF + (h + 1) * HEAD] = ydn.astype(BF16)
                yt_ref[D_HALF + h * HEAD:D_HALF + (h + 1) * HEAD, rows] = ydn.T.astype(BF16)
            yield
            hv = x_ref[rows] + jnp.dot(y_ref[rows], wo_ref[...], preferred_element_type=F32)
            yield
            r2 = lax.rsqrt(jnp.mean(hv * hv, axis=-1, keepdims=True) + EPS)
            hhat = hv * r2
            err = hhat * fnw - tg_ref[rows]
            loss_ref[...] += 0.5 * jnp.sum(_rowsum(err * err) * (1.0 / D_MODEL), axis=0, keepdims=True)
            dout = err * (1.0 / D_MODEL)
            gfn_ref[...] += _colsum(dout * hhat)
            dhh = dout * fnw
            dh = r2 * (dhh - hhat * jnp.mean(dhh * hhat, axis=-1, keepdims=True))
            dh_ref[rows] = dh
            yield
            if lead == t // OUT_ROWS - 1:
                gwo_acc[...] += _bdot(yt_ref[...], dh_ref[...])
            dy = _bdot_nt(dh, wo_ref[...])
            yield
            dyp_ref[rows] = dy[:, :D_HALF]
            gdn = jnp.zeros((1, HEAD), F32)
            for h in HEADS:
                ro, ohat, zv, sg = keep[h]
                dyd = dy[:, D_HALF + h * HEAD:D_HALF + (h + 1) * HEAD]
                sz = zv * sg
                dz_ref[rows, HEAD_COLS[h]] = (dyd * ohat * dnw * (sg * (1.0 + zv * (1.0 - sg)))).astype(BF16)
                gdn = gdn + _colsum(dyd * ohat * sz)
                doh = dyd * dnw * sz
                do_ref[rows, HEAD_COLS[h]] = ro * (doh - ohat * jnp.mean(doh * ohat, axis=-1, keepdims=True))
            gdn_ref[...] += gdn
            yield

        _interleave(*[stages(slice(k * OUT_ROWS, (k + 1) * OUT_ROWS), k) for k in range(t // OUT_ROWS)])

        @pl.when(pl.program_id(0) == pl.num_programs(0) - 1)
        def _():
            gwo_ref[...] = gwo_acc[...].astype(BF16)

    wide = pl.BlockSpec((t, D_MODEL), lambda i: (i, 0))
    half = pl.BlockSpec((t, D_HALF), lambda i: (i, 0))
    const = lambda shape: pl.BlockSpec(shape, lambda i: (0,) * len(shape))
    return _call(
        body, name="out_fwd_bwd", grid=(s // t,),
        in_specs=[wide, half, half, pl.BlockSpec((t, D_HALF), lambda i: (i, 5)), wide,
                  const((D_MODEL, D_MODEL)), const((1, HEAD)), const((1, D_MODEL))],
        out_specs=[const((D_MODEL, D_MODEL)), wide, half, half, half,
                   const((1, HEAD)), const((1, D_MODEL)), const((1, HEAD))],
        out_shape=[_sds((D_MODEL, D_MODEL), BF16), _sds((s, D_MODEL)), _sds((s, D_HALF)), _sds((s, D_HALF)),
                   _sds((s, D_HALF), BF16), _sds((1, HEAD)), _sds((1, D_MODEL)), _sds((1, HEAD))],
        scratch_shapes=[pltpu.VMEM((t, D_MODEL), BF16), pltpu.VMEM((D_MODEL, t), BF16), pltpu.VMEM((D_MODEL, D_MODEL), F32)],
        compiler_params=_params("arbitrary"),
    )(x, y_pool, o, proj, target, w_out, dn_norm_w, final_norm_w)


def _grad_w_in(at, pieces):
    m, s = at.shape
    n = len(pieces)
    tn, tk = D_HALF, min(s, 1024)

    def body(a_ref, *refs):
        p_refs, o_ref, acc = refs[:n], refs[n], refs[n + 1]

        @pl.when(pl.program_id(0) == 0)
        def _():
            acc[...] = jnp.zeros_like(acc)

        av = a_ref[...]
        for p in range(n):
            acc[:, p * tn:(p + 1) * tn] += _bdot(av, p_refs[p][...])

        @pl.when(pl.program_id(0) == pl.num_programs(0) - 1)
        def _():
            for j in range(4):
                base = j * BLK_IN // HEAD * HEAD
                win = acc[:, base:base + BLK_IN_PAD]
                if j * BLK_IN > base:
                    win = pltpu.roll(win, BLK_IN_PAD - (j * BLK_IN - base), 1)
                o_ref[j] = win.astype(BF16)

    return _call(
        body, name="grad_w_in", grid=(s // tk,),
        in_specs=[pl.BlockSpec((m, tk), lambda k: (0, k))] + [pl.BlockSpec((tk, tn), lambda k: (k, 0))] * n,
        out_specs=pl.BlockSpec((4, m, BLK_IN_PAD), lambda k: (0, 0, 0)),
        out_shape=_sds((4, m, BLK_IN_PAD), BF16),
        scratch_shapes=[pltpu.VMEM((m, n * tn), F32)],
        compiler_params=_params("arbitrary"),
    )(at, *pieces)


def _delta_bwd(do, vn, qd, kd, w, att, cd, st, qn, kn, vs, beta, g, tm):
    s = do.shape[0]
    n_steps = s // SCAN_ROWS
    assert INTRA_PAIRS == SCAN_PAIRS

    def body(do_ref, vn_ref, qd_ref, kd_ref, w_ref, att_ref, cd_ref, st_ref, qn_ref, kn_ref, vs_ref, beta_ref, g_ref, t_ref,
             dqn_ref, dkn_ref, dvs_ref, dbeta_ref, dg_ref, dstate, du_s, dw_s, datt_s, dqd_s, dkd_s, dcd_s):
        t = pl.program_id(0)

        @pl.when(t == 0)
        def _():
            dstate[...] = jnp.zeros_like(dstate)
            for ref in (du_s, dw_s, datt_s, dqd_s, dkd_s, dcd_s):
                ref[1] = jnp.zeros(ref.shape[1:], ref.dtype)

        cur = lax.rem(t, 2)
        prev = 1 - cur
        cols = list(enumerate(HEAD_COLS))
        _, incl, _, _ = _pair_masks()

        def recurrence():
            dv_intra = []
            for pp in range(SCAN_PAIRS):
                rp = slice(pp * PAIR, (pp + 1) * PAIR)
                dv_intra.append([_bdot_tn(att_ref[rp, sl], do_ref[rp, sl]) for _, sl in cols])
                for _, sl in cols:
                    datt_s[cur, rp, sl] = jnp.where(incl, _bdot_nt(do_ref[rp, sl], vn_ref[rp, sl]), 0.0)
                yield
            ds = [dstate[h] for h in HEADS]
            for ci in range(2 * SCAN_PAIRS - 1, -1, -1):
                rs = slice(ci * CHUNK, (ci + 1) * CHUNK)
                in_pair = slice((ci % 2) * CHUNK, (ci % 2 + 1) * CHUNK)
                sm = [st_ref[ci, h] for h in HEADS]
                dvn = [dv_intra[ci // 2][h][in_pair] + _bdot(kd_ref[rs, sl], ds[h]) for h, sl in cols]
                dkd = [_bdot_nt(vn_ref[rs, sl], ds[h]) for h, sl in cols]
                dcd = [jnp.broadcast_to(_rowsum(_colsum(ds[h] * sm[h])), (8, HEAD)) for h in HEADS]
                yield
                both = [_bdot_nt(_stack(do_ref[rs, sl], dvn[h]), sm[h]) for h, sl in cols]
                for h, sl in cols:
                    du_s[cur, rs, sl] = dvn[h].astype(BF16)
                    dqd_s[cur, rs, sl] = both[h][:CHUNK]
                    dw_s[cur, rs, sl] = (-both[h][CHUNK:]).astype(BF16)
                    dkd_s[cur, rs, sl] = dkd[h]
                    dcd_s[cur, ci * 8:(ci + 1) * 8, sl] = dcd[h]
                ds = [ds[h] * cd_ref[ci * 8:ci * 8 + 1, sl]
                      + _bdot_tn(_stack(qd_ref[rs, sl], w_ref[rs, sl]), _stack(do_ref[rs, sl], -dvn[h])) for h, sl in cols]
                yield
            for h in HEADS:
                dstate[h] = ds[h]

        def factors(units):
            _heads = functools.partial(_heads_of, units=units)
            _put_heads = functools.partial(_put_heads_of, units=units)
            ones = jnp.ones((2 * PAIR, HEAD), BF16)
            tn = (((0,), (0,)), ((), ()))
            kept = lambda ref, rows=PAIR: [ref[prev, pp * rows:(pp + 1) * rows, HEAD_COLS[h]] for pp, h in units]
            kn, vs, beta = _heads(kn_ref), _heads(vs_ref), _heads(beta_ref)
            cm = {}
            yield from _pair_common_stages(cm, _heads(qn_ref), kn, vs, beta, _heads(g_ref))
            tmv = _heads(t_ref)
            duv, dwv, dattv, dqdv, dkdv = kept(du_s), kept(dw_s), kept(datt_s), kept(dqd_s), kept(dkd_s)
            duw = _each(_side, duv, dwv)
            both = _each(_bdot_tn, tmv, duw)
            dvb, dkbg = [v[:, :HEAD] for v in both], [v[:, HEAD:] for v in both]
            dt = _each(lambda a, b, c: _bdot_nt(a, _side(b, c)), duw, cm["vb"], cm["kbg"])
            yield
            m1 = _each(_bdot_tn, tmv, dt)
            yield
            da = _each(lambda a, b: -jnp.where(cm["strict"], _bdot_nt(a, b), 0.0), m1, tmv)
            yield
            dkk = _each(lambda a, b: a * b, da, cm["decay"])
            dqk = _each(lambda a, b: a * b, dattv, cm["decay"])
            dd = _each(lambda a, b, c, d: a * b + c * d, dkk, cm["kk"], dqk, cm["qk"])
            dkq = _each(_stack, dkk, dqk)
            both = _each(_bdot, dkq, kn)
            dkb = _each(lambda a, c, d: a[:PAIR] + c * d, both, dkbg, cm["egc"])
            dq = _each(lambda a, c, d: a[PAIR:] + c * d, both, dqdv, cm["egc"])
            yield
            dkn = _each(lambda a, b, c: _bdot_tn(a, _stack(b, c)), dkq, cm["kb"], cm["q"])
            dkn = _each(lambda a, b, c, d, e: a + b * c + d * e, dkn, dkdv, cm["ekd"], dkb, beta)
            t_kd = _each(lambda a, b, c: _rowsum(a * b * c), dkdv, kn, cm["ekd"])
            yield
            split = _each(_split, dd)
            rows_dd = [jnp.dot(_side(hi, lo), ones, preferred_element_type=F32) for hi, lo in split]
            cols_dd = [lax.dot_general(_stack(hi, lo), ones, tn, preferred_element_type=F32) for hi, lo in split]
            yield
            dgc = _each(lambda r, c, a, b, e, f, k, tk: r - c + _rowsum(a * b * e) + _rowsum(f * k) - tk,
                        rows_dd, cols_dd, dqdv, cm["q"], cm["egc"], dkbg, cm["kbg"], t_kd)
            same_b = cm["same"].astype(BF16)
            rowi = lax.broadcasted_iota(I32, (PAIR, HEAD), 0)
            dcd = _each(lambda d: jnp.where(rowi < CHUNK, d[0:1], d[8:9]), kept(dcd_s, rows=16))
            dgl = _each(lambda tk, d, c: _mask_dot(same_b, jnp.broadcast_to(tk, (PAIR, HEAD))) + d * c, t_kd, dcd, cm["cd"])
            yield
            is_last = jnp.bitwise_and(rowi, CHUNK - 1) == CHUNK - 1
            dgc = _each(lambda a, b: a + jnp.where(is_last, b, 0.0), dgc, dgl)
            r = lax.broadcasted_iota(I32, (PAIR, PAIR), 0)
            c = lax.broadcasted_iota(I32, (PAIR, PAIR), 1)
            upper_b = (cm["same"] & (r <= c)).astype(BF16)
            _put_heads(dg_ref, _each(lambda v: _mask_dot(upper_b, v), dgc))
            yield
            _put_heads(dbeta_ref, _each(lambda a, b, c, d: jnp.broadcast_to(_rowsum(a * b) + _rowsum(c * d), (PAIR, HEAD)),
                                        dkb, kn, dvb, vs))
            _put_heads(dqn_ref, _each(lambda v: v * QK_SCALE, dq))
            _put_heads(dkn_ref, dkn)
            _put_heads(dvs_ref, _each(lambda a, b: a * b, dvb, beta))
            yield

        _interleave(recurrence(), *[factors(UNITS[pp * N_HEADS:(pp + 1) * N_HEADS]) for pp in range(INTRA_PAIRS)])

    last = n_steps - 1
    now = lambda i: (jnp.maximum(last - i, 0), 0)
    after = lambda i: (jnp.minimum(n_steps - i, last), 0)
    rows = lambda index: pl.BlockSpec((SCAN_ROWS, D_HALF), index)
    slot = lambda r, dtype: pltpu.VMEM((2, r, D_HALF), dtype)
    return _call(
        body, name="delta_bwd", grid=(n_steps + 1,),
        in_specs=[rows(now)] * 6 + [_chunk_scalar_spec(SCAN_PAIRS, now),
                                    pl.BlockSpec((2 * SCAN_PAIRS, N_HEADS, HEAD, HEAD), lambda i: (jnp.maximum(last - i, 0), 0, 0, 0))]
                 + [rows(after)] * 6,
        out_specs=[rows(after)] * 5,
        out_shape=[_sds((s, D_HALF))] * 5,
        scratch_shapes=[pltpu.VMEM((N_HEADS, HEAD, HEAD), F32), slot(SCAN_ROWS, BF16), slot(SCAN_ROWS, BF16),
                        slot(SCAN_ROWS, F32), slot(SCAN_ROWS, F32), slot(SCAN_ROWS, F32), slot(16 * SCAN_PAIRS, F32)],
        compiler_params=_params("arbitrary"),
    )(do, vn, qd, kd, w, att, cd, st, qn, kn, vs, beta, g, tm)


def _fused_call(name, n_steps, parts):
    n_in = [len(p["inputs"]) for p in parts]
    n_out = [len(p["out_shape"]) for p in parts]
    n_scr = [len(p["scratch"]) for p in parts]

    def body(*refs):
        ins, outs, scr = refs[:sum(n_in)], refs[sum(n_in):sum(n_in) + sum(n_out)], refs[sum(n_in) + sum(n_out):]
        gens, a, b, c = [], 0, 0, 0
        for p, ni, no, ns in zip(parts, n_in, n_out, n_scr):
            gens.append(p["stages"](ins[a:a + ni], outs[b:b + no], scr[c:c + ns]))
            a, b, c = a + ni, b + no, c + ns
        _interleave(*gens)

    flat = lambda key: [v for p in parts for v in p[key]]
    res = _call(
        body, name=name, grid=(n_steps,),
        in_specs=flat("in_specs"), out_specs=flat("out_specs"), out_shape=flat("out_shape"),
        scratch_shapes=flat("scratch"),
        compiler_params=_params("arbitrary"),
    )(*flat("inputs"))
    out, b = [], 0
    for no in n_out:
        out.append(res[b:b + no])
        b += no
    return out


def _pool_bwd_part(proj, dyp, pool_w, pool_scale, tile_of, n_tiles):
    s = proj.shape[0]
    t = POOL_T
    hb = t // HEAD
    last = s // HEAD - 1

    def stages(ins, outs, scratch):
        u_ref, z_ref, halo_ref, dy_ref, zn_ref, dyn_ref, pw_ref, ps_ref, band_ref, aband_ref = ins
        du_ref, dz_ref, gpw_ref, gps_ref = outs
        tile = tile_of(pl.program_id(0))

        @pl.when(pl.program_id(0) == 0)
        def _():
            gpw_ref[...] = jnp.zeros_like(gpw_ref)
            gps_ref[...] = jnp.zeros_like(gps_ref)

        live = (tile > 0).astype(F32)
        more = (tile < n_tiles - 1).astype(F32)
        groups = lambda ref: [ref[:, sl] for sl in HEAD_COLS]
        z, ps, dy = groups(z_ref), groups(ps_ref), groups(dy_ref)
        pw = [pw_ref[g] for g in HEADS]
        mix, mixed, sg, cnt = _pool_mix(groups(u_ref), [h * live for h in groups(halo_ref)], z, pw,
                                        [band_ref[g] for g in HEADS], tile * t)
        yield
        sz = _each(lambda a, b: a * b, z, sg)
        for sl, d, m, p, s_, zg in zip(HEAD_COLS, dy, mixed, ps, sg, z):
            dz_ref[:, sl] = (d * m * p * (s_ * (1.0 + zg * (1.0 - s_)))).astype(BF16)
        for sl, d, m, a in zip(HEAD_COLS, dy, mixed, sz):
            gps_ref[:, sl] += _colsum(d * m * a)
        dmixed = _each(lambda d, p, a: d * p * a, dy, ps, sz)
        yield
        for g, gp in enumerate(_each(_bdot_tn, mix, dmixed)):
            gpw_ref[g] += gp
        dmix = _each(_bdot_nt, dmixed, pw)
        yield
        dmix_n = _each(lambda d, p, zn, w_: _bdot_nt(d * more * p * (zn * _sigmoid(zn)), w_),
                       groups(dyn_ref), ps, groups(zn_ref), pw)
        yield
        scaled = [jnp.concatenate([a / c, b * (1.0 / w)], axis=0) for a, c, b, w in zip(dmix, cnt, dmix_n, WINDOWS)]
        du = _each(lambda b, s_, d: _mask_dot(b, s_) - d, [aband_ref[g] for g in HEADS], scaled, dmix)
        for sl, v in zip(HEAD_COLS, du):
            du_ref[:, sl] = v.astype(BF16)
        yield

    tile = lambda col: pl.BlockSpec((t, D_HALF), lambda i: (tile_of(i), col))
    below = lambda col: pl.BlockSpec((HEAD, D_HALF), lambda i: (jnp.minimum((tile_of(i) + 1) * hb, last), col))
    const3 = lambda shape: pl.BlockSpec(shape, lambda i: (0, 0, 0))
    return dict(
        inputs=[proj, proj, proj, dyp, proj, dyp, pool_w, pool_scale, _pool_bands(t), _pool_bands(t, anti=True)],
        in_specs=[tile(0), tile(1), pl.BlockSpec((HEAD, D_HALF), lambda i: (jnp.maximum(tile_of(i) * hb - 1, 0), 0)),
                  tile(0), below(1), below(0), const3((N_HEADS, HEAD, HEAD)), pl.BlockSpec((1, D_HALF), lambda i: (0, 0)),
                  const3((N_HEADS, t, HEAD + t)), const3((N_HEADS, t, HEAD + t))],
        out_specs=[tile(0), tile(0), const3((N_HEADS, HEAD, HEAD)), pl.BlockSpec((1, D_HALF), lambda i: (0, 0))],
        out_shape=[_sds((s, D_HALF), BF16), _sds((s, D_HALF), BF16), _sds((N_HEADS, HEAD, HEAD)), _sds((1, D_HALF))],
        scratch=[], stages=stages)


def _conv_bwd_part(proj, pre, conv_w, a_log, dt_bias, dqn, dkn, dvs, dbeta, dg, tile_of, n_tiles):
    s = proj.shape[0]
    t = CONV_T

    def stages(ins, outs, scratch):
        (q_ref, k_ref, v_ref, yq_ref, yk_ref, yv_ref, ba_ref, cw_ref, al_ref, dtb_ref,
         dqn_ref, dkn_ref, dvs_ref, dbeta_ref, dg_ref) = ins
        oq_ref, ok_ref, ov_ref, dba_ref, gcw_out, gsm_out = outs
        below, gcw_ref, gsm_ref = scratch
        step = pl.program_id(0)

        @pl.when(step == 0)
        def _():
            gcw_ref[...] = jnp.zeros_like(gcw_ref)
            gsm_ref[...] = jnp.zeros_like(gsm_ref)
            below[...] = jnp.zeros_like(below)

        parts = ((q_ref, yq_ref, dqn_ref, oq_ref), (k_ref, yk_ref, dkn_ref, ok_ref), (v_ref, yv_ref, dvs_ref, ov_ref))
        for p, (x_ref, y_ref, d_ref, o_ref) in enumerate(parts):
            for h in HEADS:
                cs = HEAD_COLS[h]
                wide = slice(p * D_HALF + h * HEAD, p * D_HALF + (h + 1) * HEAD)
                cw = cw_ref[:, wide]
                y = y_ref[:, cs]
                sg = _sigmoid(y)
                sv = y * sg
                ds = d_ref[:, cs]
                if p < 2:
                    rn = lax.rsqrt(_rowsum(sv * sv) + EPS)
                    nrm = sv * rn
                    ds = rn * (ds - nrm * _rowsum(ds * nrm))
                dy = ds * (sg * (1.0 + y * (1.0 - sg)))
                nxt = below[:, wide]
                ahead = [dy] + [_shift_up(dy, nxt, sft) for sft in range(1, CONV_K)]
                xv = x_ref[:, cs]
                acc = dy * cw[CONV_K - 1:CONV_K]
                for sft in range(1, CONV_K):
                    acc = acc + ahead[sft] * cw[CONV_K - 1 - sft:CONV_K - sft]
                for j in range(CONV_K):
                    gcw_ref[8 * j:8 * j + 8, wide] += _rows8(xv * ahead[CONV_K - 1 - j])
                o_ref[:, cs] = acc.astype(BF16)
                below[:, wide] = dy[0:8]
                yield

        ba = ba_ref[...]
        lane = lax.broadcasted_iota(I32, (t, HEAD), 1)
        lane8 = lax.broadcasted_iota(I32, (8, HEAD), 1)
        dba = jnp.zeros((t, HEAD), F32)
        gsm = jnp.zeros((8, HEAD), F32)
        for h in HEADS:
            beta = _sigmoid(ba[:, h:h + 1])
            dbeta = dbeta_ref[:, h * HEAD:h * HEAD + 1]
            xg = ba[:, N_HEADS + h:N_HEADS + h + 1] + dtb_ref[0:1, h:h + 1]
            nexp = -jnp.exp(al_ref[0:1, h:h + 1])
            dgv = dg_ref[:, h * HEAD:h * HEAD + 1]
            da = dgv * nexp * _sigmoid(xg)
            dba = dba + jnp.where(lane == h, dbeta * beta * (1.0 - beta), 0.0) + jnp.where(lane == N_HEADS + h, da, 0.0)
            gsm = (gsm + jnp.where(lane8 == h, _rows8(dgv * nexp * _softplus(xg)), 0.0)
                   + jnp.where(lane8 == N_HEADS + h, _rows8(da), 0.0))
        dba_ref[...] = jnp.zeros_like(dba_ref)
        dba_ref[:, :HEAD] = dba.astype(BF16)
        gsm_ref[...] += gsm
        yield

        @pl.when(step == n_tiles - 1)
        def _():
            gcw_out[...] = jnp.zeros_like(gcw_out)
            for j in range(CONV_K):
                gcw_out[j:j + 1, :] = _colsum(gcw_ref[8 * j:8 * j + 8, :])
            gsm_out[...] = jnp.broadcast_to(_colsum(gsm_ref[...]), (8, HEAD))

    row = pl.BlockSpec((t, D_HALF), lambda i: (tile_of(i), 0))
    const = lambda shape: pl.BlockSpec(shape, lambda i: (0, 0))
    return dict(
        inputs=[proj] * 3 + list(pre) + [proj, conv_w, a_log, dt_bias, dqn, dkn, dvs, dbeta, dg],
        in_specs=_conv_specs(t, tile_of)[:3] + [row] * 3
                 + [pl.BlockSpec((t, HEAD), lambda i: (tile_of(i), COL_BA // HEAD)),
                    const((CONV_K, 3 * D_HALF)), const((1, N_HEADS)), const((1, N_HEADS))] + [row] * 5,
        out_specs=[row, row, row, row, const((8, 3 * D_HALF)), const((8, HEAD))],
        out_shape=[_sds((s, D_HALF), BF16)] * 4 + [_sds((8, 3 * D_HALF)), _sds((8, HEAD))],
        scratch=[pltpu.VMEM((8, 3 * D_HALF), F32), pltpu.VMEM((8 * CONV_K, 3 * D_HALF), F32), pltpu.VMEM((8, HEAD), F32)],
        stages=stages)


def _pool_fwd_part(proj, pool_w, pool_scale):
    s = proj.shape[0]
    t = POOL_T
    hb = t // HEAD

    def stages(ins, outs, scratch, tile=None):
        u_ref, z_ref, halo_ref, pw_ref, ps_ref, band_ref = ins
        y_ref, = outs
        i = pl.program_id(0) if tile is None else tile
        live = (i > 0).astype(F32)
        groups = lambda ref: [ref[:, sl] for sl in HEAD_COLS]
        z = groups(z_ref)
        u, halo = groups(u_ref), [h * live for h in groups(halo_ref)]
        yield
        _, mixed, sg, _ = _pool_mix(u, halo, z, [pw_ref[g] for g in HEADS], [band_ref[g] for g in HEADS], i * t)
        yield
        for sl, m, zg, s_ in zip(HEAD_COLS, mixed, z, sg):
            y_ref[:, sl] = m * ps_ref[:, sl] * (zg * s_)
        yield

    const3 = lambda shape: pl.BlockSpec(shape, lambda i: (0, 0, 0))
    return dict(
        inputs=[proj, proj, proj, pool_w, pool_scale, _pool_bands(t)],
        in_specs=[pl.BlockSpec((t, D_HALF), lambda i: (i, 0)), pl.BlockSpec((t, D_HALF), lambda i: (i, 1)),
                  pl.BlockSpec((HEAD, D_HALF), lambda i: (jnp.maximum(i * hb - 1, 0), 0)),
                  const3((N_HEADS, HEAD, HEAD)), pl.BlockSpec((1, D_HALF), lambda i: (0, 0)), const3((N_HEADS, t, HEAD + t))],
        out_specs=[pl.BlockSpec((t, D_HALF), lambda i: (i, 0))], out_shape=[_sds((s, D_HALF))],
        scratch=[], stages=stages)


def _conv_fwd_part(proj, conv_w, a_log, dt_bias):
    s = proj.shape[0]
    t = CONV_T

    def stages(ins, outs, scratch, tile=None):
        q_ref, k_ref, v_ref, hq_ref, hk_ref, hv_ref, ba_ref, cw_ref, al_ref, dtb_ref = ins
        qn_ref, kn_ref, vs_ref, beta_ref, g_ref, yq_ref, yk_ref, yv_ref = outs
        live = ((pl.program_id(0) if tile is None else tile) > 0).astype(F32)
        parts = ((q_ref, hq_ref, qn_ref, yq_ref), (k_ref, hk_ref, kn_ref, yk_ref), (v_ref, hv_ref, vs_ref, yv_ref))
        for p, (x_ref, h_ref, o_ref, y_ref) in enumerate(parts):
            for h in HEADS:
                cs = HEAD_COLS[h]
                taps = _conv_taps(x_ref[:, cs], h_ref[:, cs] * live)
                y = _conv_pre(taps, cw_ref[:, p * D_HALF + h * HEAD:p * D_HALF + (h + 1) * HEAD])
                y_ref[:, cs] = y
                sv = y * _sigmoid(y)
                o_ref[:, cs] = sv if p == 2 else sv * lax.rsqrt(_rowsum(sv * sv) + EPS)
                yield
        ba = ba_ref[...]
        for h in HEADS:
            beta = _sigmoid(ba[:, h:h + 1])
            gl = -jnp.exp(al_ref[0:1, h:h + 1]) * _softplus(ba[:, N_HEADS + h:N_HEADS + h + 1] + dtb_ref[0:1, h:h + 1])
            beta_ref[:, HEAD_COLS[h]] = jnp.broadcast_to(beta, (t, HEAD))
            g_ref[:, HEAD_COLS[h]] = jnp.broadcast_to(gl, (t, HEAD))
        yield

    row = pl.BlockSpec((t, D_HALF), lambda i: (i, 0))
    const = lambda shape: pl.BlockSpec(shape, lambda i: (0, 0))
    return dict(
        inputs=[proj] * 7 + [conv_w, a_log, dt_bias],
        in_specs=_conv_specs(t) + [pl.BlockSpec((t, HEAD), lambda i: (i, COL_BA // HEAD)),
                                   const((CONV_K, 3 * D_HALF)), const((1, N_HEADS)), const((1, N_HEADS))],
        out_specs=[row] * 8, out_shape=[_sds((s, D_HALF))] * 8, scratch=[], stages=stages)


def _front_fwd(x, norm_w, w_pad, conv_w, a_log, dt_bias, pool_w, pool_scale, after):
    s = x.shape[0]
    t = CONV_T
    n_tiles = s // t
    assert POOL_T == CONV_T
    like_proj = _sds((s, N_IN_PAD))
    conv = _conv_fwd_part(like_proj, conv_w, a_log, dt_bias)
    pool = _pool_fwd_part(like_proj, pool_w, pool_scale)
    bands = pool["inputs"][-1]
    mxu_n = 256
    col_bounds = list(range(0, N_IN_PAD, 3 * mxu_n)) + [N_IN_PAD]

    def body(x_ref, nw_ref, w_ref, cw_ref, al_ref, dtb_ref, pw_ref, ps_ref, band_ref, after_ref,
             proj_ref, nt_ref, qn_ref, kn_ref, vs_ref, beta_ref, g_ref, yq_ref, yk_ref, yv_ref, y_ref, prev):
        del after_ref
        i = pl.program_id(0)

        @pl.when(i == 0)
        def _():
            prev[...] = jnp.zeros_like(prev)

        tile = jnp.maximum(i - 1, 0)
        main, above8, above = pl.ds(HEAD, t), pl.ds(HEAD - 8, 8), pl.ds(0, HEAD)
        cols = lambda rows, c0, width=D_HALF: prev.at[rows, pl.ds(c0, width)]
        conv_ins = (cols(main, 2 * D_HALF), cols(main, 3 * D_HALF), cols(main, 4 * D_HALF),
                    cols(above8, 2 * D_HALF), cols(above8, 3 * D_HALF), cols(above8, 4 * D_HALF),
                    cols(main, COL_BA, HEAD), cw_ref, al_ref, dtb_ref)
        pool_ins = (cols(main, 0), cols(main, D_HALF), cols(above, 0), pw_ref, ps_ref, band_ref)

        def projection():
            xv = x_ref[...]
            r = lax.rsqrt(jnp.mean(xv * xv, axis=-1, keepdims=True) + EPS)
            nv = xv * r * nw_ref[...]
            nt_ref[...] = nv.T.astype(BF16)
            nb = nv.astype(BF16)
            yield
            for lo, hi in zip(col_bounds[:-1], col_bounds[1:]):
                proj_ref[:, lo:hi] = jnp.dot(nb, w_ref[:, lo:hi], preferred_element_type=F32)
                yield

        _interleave(projection(),
                    conv["stages"](conv_ins, (qn_ref, kn_ref, vs_ref, beta_ref, g_ref, yq_ref, yk_ref, yv_ref), (), tile),
                    pool["stages"](pool_ins, (y_ref,), (), tile))
        prev[0:HEAD] = prev[t:t + HEAD]
        prev[HEAD:HEAD + t] = proj_ref[...]

    last = n_tiles - 1
    now = lambda i: (jnp.minimum(i, last), 0)
    before = lambda i: (jnp.maximum(i - 1, 0), 0)
    const = lambda a: pl.BlockSpec(a.shape, lambda i: (0,) * a.ndim)
    half = pl.BlockSpec((t, D_HALF), before)
    return _call(
        body, name="front_fwd", grid=(n_tiles + 1,),
        in_specs=[pl.BlockSpec((t, D_MODEL), now), const(norm_w), const(w_pad), const(conv_w), const(a_log), const(dt_bias),
                  const(pool_w), const(pool_scale), const(bands), pl.BlockSpec(memory_space=pl.ANY)],
        out_specs=[pl.BlockSpec((t, N_IN_PAD), now), pl.BlockSpec((D_MODEL, t), lambda i: (0, jnp.minimum(i, last)))]
                  + [half] * 9,
        out_shape=[_sds((s, N_IN_PAD)), _sds((D_MODEL, s), BF16)] + [_sds((s, D_HALF))] * 9,
        scratch_shapes=[pltpu.VMEM((HEAD + t, N_IN_PAD), F32)],
        compiler_params=_params("arbitrary"),
    )(x, norm_w, w_pad, conv_w, a_log, dt_bias, pool_w, pool_scale, bands, after)


def _conv_pool_bwd(proj, pre, conv_w, a_log, dt_bias, dqn, dkn, dvs, dbeta, dg, dyp, pool_w, pool_scale):
    n_tiles = proj.shape[0] // CONV_T
    assert POOL_T == CONV_T
    tile_of = lambda i: n_tiles - 1 - i
    return _fused_call("conv_pool_bwd", n_tiles, [
        _conv_bwd_part(proj, pre, conv_w, a_log, dt_bias, dqn, dkn, dvs, dbeta, dg, tile_of, n_tiles),
        _pool_bwd_part(proj, dyp, pool_w, pool_scale, tile_of, n_tiles)])


def _rows8(x):
    acc = x[0:8]
    for r in range(8, x.shape[0], 8):
        acc = acc + x[r:r + 8]
    return acc


IN_T = 512


def _in_bwd(x, dh, norm_w, w_pad, pieces, after):
    s = x.shape[0]
    t = IN_T
    widths = [D_HALF] * 6 + [N_IN_PAD - COL_BA]

    def body(*refs):
        x_ref, dh_ref, nw_ref, w_ref = refs[:4]
        p_refs = refs[4:4 + len(pieces)]
        gx_ref, gnw_ref = refs[5 + len(pieces):]

        @pl.when(pl.program_id(0) == 0)
        def _():
            gnw_ref[...] = jnp.zeros_like(gnw_ref)

        dn = jnp.zeros((t, D_MODEL), F32)
        col = 0
        for p_ref, wd in zip(p_refs, widths):
            dn = dn + _bdot_nt(p_ref[...], w_ref[:, col:col + wd])
            col += wd
        xv = x_ref[...]
        r = lax.rsqrt(jnp.mean(xv * xv, axis=-1, keepdims=True) + EPS)
        xhat = xv * r
        gnw_ref[...] += _colsum(dn * xhat)
        dxh = dn * nw_ref[...]
        gx_ref[...] = dh_ref[...] + r * (dxh - xhat * jnp.mean(dxh * xhat, axis=-1, keepdims=True))

    wide = pl.BlockSpec((t, D_MODEL), lambda i: (i, 0))
    return _call(
        body, name="in_bwd", grid=(s // t,),
        in_specs=[wide, wide, pl.BlockSpec((1, D_MODEL), lambda i: (0, 0)),
                  pl.BlockSpec((D_MODEL, N_IN_PAD), lambda i: (0, 0))]
                 + [pl.BlockSpec((t, wd), lambda i: (i, 0)) for wd in widths] + [pl.BlockSpec(memory_space=pl.ANY)],
        out_specs=[wide, pl.BlockSpec((1, D_MODEL), lambda i: (0, 0))],
        out_shape=[_sds((s, D_MODEL)), _sds((1, D_MODEL))],
        compiler_params=_params("arbitrary"),
    )(x, dh, norm_w, w_pad, *pieces, after)


def _adamw_shard(name, w, g_own, g_got, cidx, m, v):
    _, r, c = w.shape
    half = r // 2
    rows = 256 if half % 256 == 0 else half
    per_half = half // rows

    def body(c_ref, w_ref, go_ref, gg_ref, m_ref, v_ref, gout_ref, d_ref, nm_ref, nv_ref):
        mine = (pl.program_id(0) // per_half) == c_ref[0]
        gv = jnp.where(mine, go_ref[:, :c], gg_ref[:, :c])
        gout_ref[0] = gv
        mn = ADAM_B1 * m_ref[0] + (1.0 - ADAM_B1) * gv
        vn = ADAM_B2 * v_ref[0] + (1.0 - ADAM_B2) * (gv * gv)
        m_hat = mn / (1.0 - ADAM_B1 ** ADAM_STEP)
        v_hat = vn / (1.0 - ADAM_B2 ** ADAM_STEP)
        d_ref[0] = -ADAM_LR * (m_hat / (jnp.sqrt(v_hat) + ADAM_EPS) + ADAM_WD * w_ref[0])
        nm_ref[0] = mn
        nv_ref[0] = vn

    blk = pl.BlockSpec((1, rows, c), lambda i, c_ref: (0, i, 0))
    gblk = pl.BlockSpec((rows, g_own.shape[1]), lambda i, c_ref: (i % per_half, 0))
    return _call(
        body, name=name,
        grid_spec=pltpu.PrefetchScalarGridSpec(
            num_scalar_prefetch=1, grid=(2 * per_half,),
            in_specs=[blk, gblk, gblk, blk, blk], out_specs=[blk] * 4),
        out_shape=[_sds((1, r, c))] * 4,
        compiler_params=_params("arbitrary"),
    )(cidx, w, g_own, g_got, m, v)


def _adamw_tiles(name, w, g, m, v):
    n = w.shape[0]
    nb = 77 if n % 77 == 0 else n

    def body(w_ref, g_ref, m_ref, v_ref, d_ref, nm_ref, nv_ref):
        gv = g_ref[...]
        mn = ADAM_B1 * m_ref[...] + (1.0 - ADAM_B1) * gv
        vn = ADAM_B2 * v_ref[...] + (1.0 - ADAM_B2) * (gv * gv)
        m_hat = mn / (1.0 - ADAM_B1 ** ADAM_STEP)
        v_hat = vn / (1.0 - ADAM_B2 ** ADAM_STEP)
        d_ref[...] = -ADAM_LR * (m_hat / (jnp.sqrt(v_hat) + ADAM_EPS) + ADAM_WD * w_ref[...])
        nm_ref[...] = mn
        nv_ref[...] = vn

    blk = pl.BlockSpec((nb, 8, HEAD), lambda i: (i, 0, 0))
    return _call(
        body, name=name, grid=(n // nb,),
        in_specs=[blk] * 4, out_specs=[blk] * 3, out_shape=[_sds(w.shape)] * 3,
        compiler_params=_params("arbitrary"),
    )(w, g, m, v)


def _make_copy(src, dst, send, recv, target):
    if target is None:
        return pltpu.make_async_copy(src, dst, recv)
    return pltpu.make_async_remote_copy(src_ref=src, dst_ref=dst, send_sem=send, recv_sem=recv,
                                        device_id=target, device_id_type=pl.DeviceIdType.MESH)


def _exchange(name, inputs, out_shapes, phases):
    n_in = len(inputs)
    n_out = len(out_shapes)
    n_cp = sum(len(p) for p in phases)

    def body(*refs):
        ins, outs = refs[:n_in], refs[n_in:n_in + n_out]
        send, recv = refs[n_in + n_out:]
        pos = (lax.axis_index("x"), lax.axis_index("y"), lax.axis_index("c"))
        k = 0
        for phase in phases:
            cps = []
            for src, dst, target in phase:
                cps.append(_make_copy(src(ins, outs, pos), dst(ins, outs, pos), send.at[k], recv.at[k],
                                      target and target(pos)))
                k += 1
            for cp in cps:
                cp.start()
            for cp in cps:
                cp.wait()

    anyspec = pl.BlockSpec(memory_space=pl.ANY)
    return _call(
        body, name=name,
        in_specs=[anyspec] * n_in, out_specs=[anyspec] * n_out, out_shape=list(out_shapes),
        scratch_shapes=[pltpu.SemaphoreType.DMA((n_cp,)), pltpu.SemaphoreType.DMA((n_cp,))],
    )(*inputs)


def _exchange_start(name, inputs, out_shapes, copies):
    n_in, n_out, n_cp = len(inputs), len(out_shapes), len(copies)

    def body(*refs):
        ins, lands = refs[:n_in], refs[n_in:n_in + n_out]
        sems = refs[n_in + n_out:n_in + n_out + 2 * n_cp]
        token = refs[-1]
        pos = (lax.axis_index("x"), lax.axis_index("y"), lax.axis_index("c"))
        for k, (src, dst, target) in enumerate(copies):
            _make_copy(src(ins, lands, pos), dst(ins, lands, pos), sems[2 * k], sems[2 * k + 1],
                       target and target(pos)).start()
        token[...] = jnp.zeros_like(token)

    hbm = pl.BlockSpec(memory_space=pltpu.HBM)
    sem = pl.BlockSpec(memory_space=pltpu.SEMAPHORE)
    bufs = list(inputs) + [lax.empty(o.shape, o.dtype) for o in out_shapes]
    outs = _call(
        body, name=name,
        out_shape=tuple([pltpu.SemaphoreType.DMA(())] * (2 * n_cp) + [pltpu.HBM(b.shape, b.dtype) for b in bufs]
                        + [_sds((8, HEAD))]),
        in_specs=[hbm] * len(bufs),
        out_specs=tuple([sem] * (2 * n_cp) + [hbm] * len(bufs) + [pl.BlockSpec(memory_space=pltpu.VMEM)]),
        input_output_aliases={i: 2 * n_cp + i for i in range(len(bufs))},
        compiler_params=pltpu.CompilerParams(has_side_effects=pltpu.SideEffectType.DATAFLOW_SIDE_EFFECTING),
    )(*[pltpu.with_memory_space_constraint(b, pltpu.HBM) for b in bufs])
    return outs[:2 * n_cp], outs[2 * n_cp:2 * n_cp + n_in], outs[2 * n_cp + n_in:-1], outs[-1]


def _exchange_wait(name, sems, sources, lands, copies, after):
    n_in, n_out, n_cp = len(sources), len(lands), len(copies)

    def body(*refs):
        ins, zones = refs[:n_in], refs[n_in:n_in + n_out]
        sem_refs = refs[n_in + n_out:n_in + n_out + 2 * n_cp]
        pos = (lax.axis_index("x"), lax.axis_index("y"), lax.axis_index("c"))
        for k, (src, dst, target) in enumerate(copies):
            cp = _make_copy(src(ins, zones, pos), dst(ins, zones, pos), sem_refs[2 * k], sem_refs[2 * k + 1],
                            target and target(pos))
            if target is None:
                cp.wait()
            else:
                cp.wait_send()
                cp.wait_recv()

    hbm = pl.BlockSpec(memory_space=pltpu.HBM)
    sem = pl.BlockSpec(memory_space=pltpu.SEMAPHORE)
    bufs = list(sources) + list(lands)
    outs = _call(
        body, name=name,
        out_shape=tuple(pltpu.HBM(b.shape, b.dtype) for b in bufs),
        in_specs=[hbm] * len(bufs) + [sem] * (2 * n_cp) + [pl.BlockSpec(memory_space=pl.ANY)],
        out_specs=tuple([hbm] * len(bufs)),
        input_output_aliases={i: i for i in range(len(bufs))},
        compiler_params=pltpu.CompilerParams(has_side_effects=pltpu.SideEffectType.DATAFLOW_SIDE_EFFECTING),
    )(*bufs, *sems, after)
    return outs[:n_in], outs[n_in:]


def _allreduce_tile(name, v):
    def body(v_ref, out_ref, slots, send, recv):
        x, y, c = lax.axis_index("x"), lax.axis_index("y"), lax.axis_index("c")
        me = 4 * x + 2 * y + c
        slots[me] = v_ref[...]
        cps = []
        for k in range(1, 8):
            peer = (x ^ (k >> 2), y ^ ((k >> 1) & 1), c ^ (k & 1))
            cps.append(pltpu.make_async_remote_copy(
                src_ref=v_ref, dst_ref=slots.at[me], send_sem=send.at[k - 1], recv_sem=recv.at[k - 1],
                device_id=peer, device_id_type=pl.DeviceIdType.MESH))
        for cp in cps:
            cp.start()
        for cp in cps:
            cp.wait()
        acc = slots[0]
        for i in range(1, 8):
            acc = acc + slots[i]
        out_ref[...] = acc

    vm = pl.BlockSpec(memory_space=pltpu.VMEM)
    return _call(
        body, name=name, in_specs=[vm], out_specs=vm, out_shape=_sds(v.shape),
        scratch_shapes=[pltpu.VMEM((8,) + v.shape, F32), pltpu.SemaphoreType.DMA((7,)), pltpu.SemaphoreType.DMA((7,))],
    )(v)


def _chip(pos):
    return 2 * pos[0] + pos[1]


def _other_chip(pos, mask):
    x, y, c = pos
    return (x ^ (mask >> 1), y ^ (mask & 1), c)


def _sibling(pos):
    return (pos[0], pos[1], 1 - pos[2])


def _gather_weights(wb, cb):
    rows = wb.shape[0] // 2
    x_nb, y_nb, diag = CHIP_MASKS

    def part(pos, mask, quarter=None):
        start = pos[2] * rows if quarter is None else pos[2] * rows + quarter * (rows // 2)
        return lambda outs: outs[0].at[_chip(pos) ^ mask, pl.ds(start, rows if quarter is None else rows // 2)]

    def passed_on(mask, to, quarter=None):
        return (lambda ins, outs, pos: part(pos, mask, quarter)(outs), lambda ins, outs, pos: part(pos, mask, quarter)(outs), to)

    first = [(lambda ins, outs, pos: ins[0].at[pl.ds(pos[2] * rows, rows)], lambda ins, outs, pos: part(pos, 0)(outs),
              functools.partial(_other_chip, mask=mask)) for mask in (x_nb, y_nb)]
    conv_cols = lambda ins, outs, pos: outs[1].at[:, pl.ds(pl.multiple_of(_chip(pos) * cb.shape[1], HEAD), cb.shape[1])]
    first += [(lambda ins, outs, pos: ins[1], conv_cols, functools.partial(_other_chip, mask=mask)) for mask in CHIP_MASKS]
    first += [(lambda ins, outs, pos: ins[1], conv_cols, None)]
    second = [passed_on(x_nb, functools.partial(_other_chip, mask=y_nb), quarter=0),
              passed_on(y_nb, functools.partial(_other_chip, mask=x_nb), quarter=1),
              passed_on(x_nb, _sibling), passed_on(y_nb, _sibling)]
    third = [passed_on(diag, _sibling)]
    return _exchange("gather_weights", [wb, cb],
                     [_sds((4,) + wb.shape, wb.dtype), _sds((cb.shape[0], 4 * cb.shape[1]), cb.dtype)], [first, second, third])


def _assemble_w_in(gw, wb, jidx):
    m = gw.shape[1]

    def body(j_ref, g_ref, wb_ref, o_ref):
        step = pl.program_id(0)

        @pl.when(step == 0)
        def _():
            o_ref[...] = jnp.zeros_like(o_ref)

        blk = jnp.where(step == j_ref[0], wb_ref[...], g_ref[0]).astype(F32)
        lane = lax.broadcasted_iota(I32, (m, BLK_IN_PAD), 1)
        for j in range(4):
            @pl.when(step == j)
            def _(j=j):
                base = j * BLK_IN // HEAD * HEAD
                shift = j * BLK_IN - base
                moved = pltpu.roll(blk, shift, 1) if shift else blk
                window = o_ref[:, base:base + BLK_IN_PAD].astype(F32)
                mine = (lane >= shift) & (lane < shift + BLK_IN)
                o_ref[:, base:base + BLK_IN_PAD] = jnp.where(mine, moved, window).astype(BF16)

    return _call(
        body, name="assemble_w_in",
        grid_spec=pltpu.PrefetchScalarGridSpec(
            num_scalar_prefetch=1, grid=(4,),
            in_specs=[pl.BlockSpec((1, m, BLK_IN_PAD), lambda j, j_ref: (j, 0, 0)),
                      pl.BlockSpec((m, BLK_IN_PAD), lambda j, j_ref: (0, 0))],
            out_specs=pl.BlockSpec((m, N_IN_PAD), lambda j, j_ref: (0, 0))),
        out_shape=_sds((m, N_IN_PAD), BF16),
        compiler_params=_params("arbitrary"),
    )(jidx, gw, wb)


def _gather_blocks(ob):
    copies = [(lambda ins, outs, pos: ins[0], lambda ins, outs, pos: outs[0].at[_chip(pos)],
               functools.partial(_other_chip, mask=mask)) for mask in CHIP_MASKS]
    copies.append((lambda ins, outs, pos: ins[0], lambda ins, outs, pos: outs[0].at[_chip(pos)], None))
    return [_sds((4,) + ob.shape, ob.dtype)], copies


def _reduce_sibling(name, arrays, cidx):
    n = len(arrays)
    halves = [a.shape[:-2] + (a.shape[-2] // 2, a.shape[-1]) for a in arrays]
    pieces = [(a, j) for a in range(n) for j in (range(arrays[a].shape[0]) if arrays[a].ndim == 3 else [None])]

    def body(c_ref, *refs):
        del c_ref
        whole, own, outs, land = refs[:n], refs[n:2 * n], refs[2 * n:3 * n], refs[3 * n:4 * n]
        send, recv = refs[4 * n:]
        pos = (lax.axis_index("x"), lax.axis_index("y"), lax.axis_index("c"))
        cps = []
        for k, (a, j) in enumerate(pieces):
            rows = pl.ds((1 - pos[2]) * halves[a][-2], halves[a][-2])
            cps.append(pltpu.make_async_remote_copy(
                src_ref=whole[a].at[rows] if j is None else whole[a].at[j, rows],
                dst_ref=land[a] if j is None else land[a].at[j],
                send_sem=send.at[k], recv_sem=recv.at[k],
                device_id=_sibling(pos), device_id_type=pl.DeviceIdType.MESH))
        for cp in cps:
            cp.start()
        for cp, (a, j) in zip(cps, pieces):
            cp.wait()
            at = Ellipsis if j is None else j
            outs[a][at] = (own[a][at].astype(F32) + land[a][at].astype(F32)).astype(outs[a].dtype)

    def mine(shape):
        if len(shape) == 3:
            return pl.BlockSpec(shape, lambda i, c_ref: (0, c_ref[0], 0))
        return pl.BlockSpec(shape, lambda i, c_ref: (c_ref[0], 0))

    return _call(
        body, name=name,
        grid_spec=pltpu.PrefetchScalarGridSpec(
            num_scalar_prefetch=1, grid=(1,),
            in_specs=[pl.BlockSpec(memory_space=pl.ANY)] * n + [mine(h) for h in halves],
            out_specs=[pl.BlockSpec(h, lambda i, c_ref, nd=len(h): (0,) * nd) for h in halves],
            scratch_shapes=[pltpu.VMEM(h, a.dtype) for h, a in zip(halves, arrays)]
                           + [pltpu.SemaphoreType.DMA((len(pieces),)), pltpu.SemaphoreType.DMA((len(pieces),))]),
        out_shape=[_sds(h, a.dtype) for h, a in zip(halves, arrays)],
        compiler_params=_params("arbitrary"),
    )(cidx, *arrays, *arrays)


def _to_other_chips(arrays, blocked):
    def src(ins, outs, pos, a, mask):
        return ins[a].at[_chip(pos) ^ mask] if blocked[a] else ins[a]

    outs = [_sds((3,) + (a.shape[1:] if b else a.shape), a.dtype) for a, b in zip(arrays, blocked)]
    copies = []
    for mi, mask in enumerate(CHIP_MASKS):
        for a in range(len(arrays)):
            copies.append((functools.partial(src, a=a, mask=mask), lambda ins, outs, pos, a=a, mi=mi: outs[a].at[mi],
                           functools.partial(_other_chip, mask=mask)))
    return outs, copies


def _sum_chips_swap(name, owns, gots, jidx, blocked):
    n = len(owns)
    shapes = [g.shape[-2:] for g in gots]
    own3 = [o if b else o.reshape((1,) + o.shape) for o, b in zip(owns, blocked)]

    def body(j_ref, *refs):
        del j_ref
        own, got, mine, theirs = refs[:n], refs[n:2 * n], refs[2 * n:3 * n], refs[3 * n:4 * n]
        send, recv = refs[4 * n:]
        pos = (lax.axis_index("x"), lax.axis_index("y"), lax.axis_index("c"))
        cps = []
        for a in range(n):
            mine[a][...] = ((own[a][0].astype(F32) + got[a][0].astype(F32))
                            + (got[a][1].astype(F32) + got[a][2].astype(F32)))
            cps.append(pltpu.make_async_remote_copy(
                src_ref=mine[a], dst_ref=theirs[a], send_sem=send.at[a], recv_sem=recv.at[a],
                device_id=_sibling(pos), device_id_type=pl.DeviceIdType.MESH))
            cps[-1].start()
        for cp in cps:
            cp.wait()

    own_spec = lambda s, b: pl.BlockSpec((1,) + s, (lambda i, j_ref: (j_ref[0], 0, 0)) if b else (lambda i, j_ref: (0, 0, 0)))
    whole = lambda s: pl.BlockSpec(s, lambda i, j_ref: (0, 0))
    outs = _call(
        body, name=name,
        grid_spec=pltpu.PrefetchScalarGridSpec(
            num_scalar_prefetch=1, grid=(1,),
            in_specs=[own_spec(s, b) for s, b in zip(shapes, blocked)]
                     + [pl.BlockSpec((3,) + s, lambda i, j_ref: (0, 0, 0)) for s in shapes],
            out_specs=[whole(s) for s in shapes] * 2,
            scratch_shapes=[pltpu.SemaphoreType.DMA((n,)), pltpu.SemaphoreType.DMA((n,))]),
        out_shape=[_sds(s) for s in shapes] * 2,
        compiler_params=_params("arbitrary"),
    )(jidx, *own3, *gots)
    return outs[:n], outs[n:]


def _local_step(x, target, w_pad, w_out, conv_w, norm_w, pool_w, pool_scale, a_log, dt_bias, dn_norm_w, final_norm_w,
                after):
    proj, n_t, qn, kn, vs, beta, g, yq, yk, yv, y_pool = _front_fwd(
        x, norm_w, w_pad, conv_w, a_log, dt_bias, pool_w, pool_scale, after)
    w, att, qd, kd, tm, cd, o, vn, st = _delta_fwd(qn, kn, vs, beta, g)
    w_out = w_out(o) if callable(w_out) else w_out
    g_wout, dh, dyp, do, ddz, loss, g_fnw, g_dnw = _out_fwd_bwd(x, y_pool, o, proj, target, w_out, dn_norm_w, final_norm_w)
    dqn, dkn, dvs, dbeta, dg = _delta_bwd(do, vn, qd, kd, w, att, cd, st, qn, kn, vs, beta, g, tm)
    (dcq, dck, dcv, dba, g_cw, g_sm), (dpu, dpz, g_pw, g_ps) = _conv_pool_bwd(
        proj, (yq, yk, yv), conv_w, a_log, dt_bias, dqn, dkn, dvs, dbeta, dg, dyp, pool_w, pool_scale)
    pieces = [dpu, dpz, dcq, dck, dcv, ddz, dba]
    g_win = _grad_w_in(n_t, pieces)
    small = dict(norm_w=jnp.zeros_like(norm_w), pool_w=g_pw, pool_scale=g_ps, conv_w=g_cw[:CONV_K],
                 a_log=g_sm[0:1, 0:N_HEADS], dt_bias=g_sm[0:1, N_HEADS:2 * N_HEADS], dn_norm_w=g_dnw, final_norm_w=g_fnw)
    return loss[0, 0], g_win, g_wout, small, dh, pieces


SMALL_LAYOUT = (("pool_w", 512, HEAD, (1, N_HEADS, HEAD, HEAD)), ("final_norm_w", 8, HEAD, (D_MODEL,)),
                ("pool_scale", 4, HEAD, (1, D_HALF)), ("conv_w", 48, HEAD, (1, CONV_K, 3 * D_HALF)),
                ("dn_norm_w", 1, HEAD, (1, HEAD)), ("a_log", 1, N_HEADS, (1, N_HEADS)), ("dt_bias", 1, N_HEADS, (1, N_HEADS)),
                ("loss", 1, 1, ()))


def _small_offsets():
    offs, r = {}, 0
    for name, rows, _, _ in SMALL_LAYOUT:
        offs[name] = r
        r += -(-rows // 8) * 8
    assert r <= SMALL_ROWS
    return offs


def _pack_small(t):
    parts = []
    for name, rows, lanes, _ in SMALL_LAYOUT:
        a = t.get(name, jnp.zeros((1,), F32)).reshape(rows, lanes)
        parts.append(jnp.pad(a, ((0, -(-rows // 8) * 8 - rows), (0, HEAD - lanes))))
    buf = jnp.concatenate(parts, axis=0)
    return jnp.pad(buf, ((0, SMALL_ROWS - buf.shape[0]), (0, 0)))


def _adamw_small(w, g_own, g_got, cidx, m, v):
    offs = _small_offsets()
    names = [e[0] for e in SMALL_LAYOUT]
    n = len(names)

    def body(c_ref, w_ref, go_ref, gg_ref, m_ref, v_ref, *outs):
        own_low = c_ref[0] == 0
        gv = jnp.concatenate([jnp.where(own_low, go_ref[...], gg_ref[...]), jnp.where(own_low, gg_ref[...], go_ref[...])], axis=0)
        mn = ADAM_B1 * m_ref[...] + (1.0 - ADAM_B1) * gv
        vn = ADAM_B2 * v_ref[...] + (1.0 - ADAM_B2) * (gv * gv)
        m_hat = mn / (1.0 - ADAM_B1 ** ADAM_STEP)
        v_hat = vn / (1.0 - ADAM_B2 ** ADAM_STEP)
        dl = -ADAM_LR * (m_hat / (jnp.sqrt(v_hat) + ADAM_EPS) + ADAM_WD * w_ref[...])
        for kind, arr in enumerate((gv, dl, mn, vn)):
            for i, (name, rows, lanes, _) in enumerate(SMALL_LAYOUT):
                outs[kind * n + i][...] = arr[offs[name]:offs[name] + rows, :lanes]

    whole = lambda shape: pl.BlockSpec(shape, lambda i, c_ref: (0,) * len(shape))
    out_shapes = [_sds((rows, lanes)) for _, rows, lanes, _ in SMALL_LAYOUT] * 4
    res = _call(
        body, name="adamw_small",
        grid_spec=pltpu.PrefetchScalarGridSpec(
            num_scalar_prefetch=1, grid=(1,),
            in_specs=[whole(w.shape), whole(g_own.shape), whole(g_got.shape), whole(m.shape), whole(v.shape)],
            out_specs=[whole(o.shape) for o in out_shapes]),
        out_shape=out_shapes,
        compiler_params=_params("arbitrary"),
    )(cidx, w, g_own, g_got, m, v)
    return [{name: res[kind * n + i].reshape(shape) for i, (name, _, _, shape) in enumerate(SMALL_LAYOUT)}
            for kind in range(4)]


def kernel(x, norm_w, w_in, pool_w, pool_scale, conv_w, a_log, dt_bias, dn_norm_w, w_out, final_norm_w, loss_target, m_norm_w, m_w_in, m_pool_w, m_pool_scale, m_conv_w, m_a_log, m_dt_bias, m_dn_norm_w, m_w_out, m_final_norm_w, v_norm_w, v_w_in, v_pool_w, v_pool_scale, v_conv_w, v_a_log, v_dt_bias, v_dn_norm_w, v_w_out, v_final_norm_w):
    cidx = lax.axis_index("c").astype(I32).reshape(1)
    jidx = (2 * lax.axis_index("x") + lax.axis_index("y")).astype(I32)

    wb = jnp.pad(w_in[0].astype(BF16), ((0, 0), (0, BLK_IN_PAD - BLK_IN)))
    ob = w_out[0].astype(BF16)
    gw, cw_full = _gather_weights(wb, conv_w[0])
    w_pad = _assemble_w_in(gw, wb, jidx.reshape(1))

    lands_o, copies_o = _gather_blocks(ob)
    sems_o, ob_thru, zones_o, token_o = _exchange_start("gather_w_out_start", [ob], lands_o, copies_o)

    def w_out_full(after):
        _, (got,) = _exchange_wait("gather_w_out_wait", sems_o, ob_thru, zones_o, copies_o, after)
        return got.reshape(D_MODEL, D_MODEL)

    loss, g_win, g_wout, small, dh, pieces = _local_step(
        x[0], loss_target[0], w_pad, w_out_full, cw_full, norm_w, pool_w[0], pool_scale, a_log, dt_bias,
        dn_norm_w, final_norm_w.reshape(1, D_MODEL), token_o)
    small["loss"] = loss

    blocks_out = g_wout.reshape(4, BLK_OUT, D_MODEL)
    full = [g_win, blocks_out, _pack_small(small)]
    chip_sum = _reduce_sibling("reduce_sibling", full, cidx)
    blocked = [True, True, False]
    lands, copies = _to_other_chips(chip_sum, blocked)
    sems, chip_sum, zones, token = _exchange_start("reduce_chips_start", chip_sum, lands, copies)
    gx, g_nw = _in_bwd(x[0], dh, norm_w, w_pad, pieces, token)
    g_nw = _allreduce_tile("reduce_norm_w", g_nw.reshape(8, HEAD)).reshape(1, D_MODEL)
    chip_sum, from_chips = _exchange_wait("reduce_chips_wait", sems, chip_sum, zones, copies, gx)
    halves, other_halves = _sum_chips_swap("sum_chips_swap", chip_sum, from_chips, jidx.reshape(1), blocked)

    weights = dict(norm_w=norm_w, w_in=w_in, pool_w=pool_w, pool_scale=pool_scale, conv_w=conv_w, a_log=a_log,
                   dt_bias=dt_bias, dn_norm_w=dn_norm_w, w_out=w_out, final_norm_w=final_norm_w)
    ms = dict(norm_w=m_norm_w, w_in=m_w_in, pool_w=m_pool_w, pool_scale=m_pool_scale, conv_w=m_conv_w, a_log=m_a_log,
              dt_bias=m_dt_bias, dn_norm_w=m_dn_norm_w, w_out=m_w_out, final_norm_w=m_final_norm_w)
    vs = dict(norm_w=v_norm_w, w_in=v_w_in, pool_w=v_pool_w, pool_scale=v_pool_scale, conv_w=v_conv_w, a_log=v_a_log,
              dt_bias=v_dt_bias, dn_norm_w=v_dn_norm_w, w_out=v_w_out, final_norm_w=v_final_norm_w)
    names = ["norm_w", "w_in", "pool_w", "pool_scale", "conv_w", "a_log", "dt_bias", "dn_norm_w", "w_out", "final_norm_w"]
    small_names = [n for n in names if n not in ("w_in", "w_out")]

    def pack(t):
        conv = lax.dynamic_update_slice_in_dim(jnp.zeros((CONV_K, 3 * D_HALF), F32), t["conv_w"][0], jidx * BLK_CONV, axis=1)
        return _pack_small({**{n: t[n] for n in small_names if n != "conv_w"}, "conv_w": conv})

    results = [{}, {}, {}, {}]
    to_tiles = lambda a: jnp.transpose(a, (2, 0, 1)).reshape(BLK_IN, 8, HEAD)
    from_tiles = lambda a: jnp.transpose(a, (1, 2, 0)).reshape(1, D_MODEL, BLK_IN)
    lo = jnp.where(cidx[0] == 0, halves[0], other_halves[0])
    hi = jnp.where(cidx[0] == 0, other_halves[0], halves[0])
    g_tiles = jnp.concatenate([lo[:, :BLK_IN].T, hi[:, :BLK_IN].T], axis=1).reshape(BLK_IN, 8, HEAD)
    outs = _adamw_tiles("adamw_w_in", to_tiles(w_in), g_tiles, to_tiles(m_w_in), to_tiles(v_w_in))
    for res, o in zip(results, (g_tiles,) + tuple(outs)):
        res["w_in"] = from_tiles(o)
    outs = _adamw_shard("adamw_w_out", w_out, halves[1], other_halves[1], cidx, m_w_out, v_w_out)
    for res, o in zip(results, outs):
        res["w_out"] = o
    outs = _adamw_small(pack(weights), halves[2], other_halves[2], cidx, pack(ms), pack(vs))
    for res, got in zip(results, outs):
        got["conv_w"] = lax.dynamic_slice_in_dim(got["conv_w"], jidx * BLK_CONV, BLK_CONV, axis=2)
        res.update(got)
    one_tile = lambda a: a.reshape(1, 8, HEAD)
    outs = _adamw_tiles("adamw_norm_w", one_tile(norm_w), one_tile(g_nw), one_tile(m_norm_w), one_tile(v_norm_w))
    for res, o in zip(results, (g_nw,) + tuple(outs)):
        res["norm_w"] = o.reshape(1, D_MODEL)
    grads, delta, new_m, new_v = results

    return (grads["loss"], gx[None], *[grads[n] for n in names], *[delta[n] for n in names],
            *[new_m[n] for n in names], *[new_v[n] for n in names])
```

```python
import functools

import jax
import jax.numpy as jnp
import numpy as np
from jax import lax
from jax.experimental import pallas as pl
from jax.experimental.pallas import tpu as pltpu

F32 = jnp.float32
BF16 = jnp.bfloat16
I32 = jnp.int32

D_MODEL = 1024
D_HALF = 512
N_HEADS = 4
HEAD = 128
CHUNK = 64
PAIR = 2 * CHUNK
WINDOWS = (2, 4, 8, 16)
CONV_K = 4
EPS = 1e-6
N_IN = 3080
N_IN_PAD = 3200
BLK_IN = 770
BLK_IN_PAD = 896
BLK_OUT = 256
BLK_CONV = 384
COL_BA = 3072
QK_SCALE = HEAD ** -0.5
SMALL_ROWS = 608
VMEM_LIMIT = 56 * 1024 * 1024

ADAM_LR = 0.001
ADAM_B1 = 0.9
ADAM_B2 = 0.999
ADAM_EPS = 1e-08
ADAM_WD = 0.01
ADAM_STEP = 10

CHIP_MASKS = (2, 1, 3)
HEADS = range(N_HEADS)
HEAD_COLS = [slice(h * HEAD, (h + 1) * HEAD) for h in HEADS]


def _call(body, **kw):
    return pl.pallas_call(body, **kw)


def _params(*sem):
    return pltpu.CompilerParams(dimension_semantics=sem, vmem_limit_bytes=VMEM_LIMIT)


def _sds(shape, dtype=F32):
    return jax.ShapeDtypeStruct(shape, dtype)


def _bdot(a, b):
    return jnp.dot(a.astype(BF16), b.astype(BF16), preferred_element_type=F32)


def _bdot_nt(a, b):
    return lax.dot_general(a.astype(BF16), b.astype(BF16), (((1,), (1,)), ((), ())), preferred_element_type=F32)


def _bdot_tn(a, b):
    return lax.dot_general(a.astype(BF16), b.astype(BF16), (((0,), (0,)), ((), ())), preferred_element_type=F32)


def _side(a, b):
    return jnp.concatenate([a.astype(BF16), b.astype(BF16)], axis=1)


def _stack(a, b):
    return jnp.concatenate([a.astype(BF16), b.astype(BF16)], axis=0)


def _split(a):
    hi = a.astype(BF16)
    lo = (a - hi.astype(F32)).astype(BF16)
    return hi, lo


def _mask_dot(m, b):
    n = b.shape[1]
    both = jnp.dot(m, jnp.concatenate(_split(b), axis=1), preferred_element_type=F32)
    return both[:, :n] + both[:, n:]


def _sigmoid(x):
    return 0.5 * jnp.tanh(0.5 * x) + 0.5


def _softplus(x):
    return jnp.maximum(x, 0.0) + jnp.log(1.0 + jnp.exp(-jnp.abs(x)))


def _rowsum(x):
    return jnp.sum(x, axis=-1, keepdims=True)


def _colsum(x):
    return jnp.sum(x, axis=0, keepdims=True)


def _shift_down(xv, prev8, k):
    r = pltpu.roll(xv, k, 0)
    q = pltpu.roll(prev8, k, 0)
    row = lax.broadcasted_iota(I32, prev8.shape, 0)
    top = jnp.where(row < k, q, r[0:8])
    return jnp.concatenate([top, r[8:]], axis=0)


def _shift_up(xv, next8, k):
    t = xv.shape[0]
    r = pltpu.roll(xv, t - k, 0)
    q = pltpu.roll(next8, 8 - k, 0)
    row = lax.broadcasted_iota(I32, next8.shape, 0)
    bot = jnp.where(row >= 8 - k, q, r[t - 8:])
    return jnp.concatenate([r[:t - 8], bot], axis=0)


INTRA_PAIRS = 2
UNITS = [(pp, h) for pp in range(INTRA_PAIRS) for h in HEADS]


def _heads_of(ref, rows=PAIR, units=UNITS):
    return [ref[pp * rows:(pp + 1) * rows, HEAD_COLS[h]] for pp, h in units]


def _put_heads_of(ref, vals, rows=PAIR, units=UNITS):
    for (pp, h), v in zip(units, vals):
        ref[pp * rows:(pp + 1) * rows, HEAD_COLS[h]] = v.astype(ref.dtype)


_heads = _heads_of
_put_heads = _put_heads_of


def _each(fn, *lists):
    return [fn(*args) for args in zip(*lists)]


def _pool_bands(t, anti=False):
    r = np.arange(t)[:, None]
    c = np.arange(t + HEAD)[None, :]
    d = (c - r) if anti else (r - c + HEAD)
    return jnp.asarray(np.stack([(d >= 0) & (d < w) for w in WINDOWS]), BF16)


def _pool_mix(u, halo, z, pw, bands, row0):
    t = u[0].shape[0]
    rows = row0 + lax.broadcasted_iota(I32, (t, 1), 0) + 1
    cnt = [jnp.minimum(rows, w).astype(F32) for w in WINDOWS]
    win = _each(lambda b, h, v: _mask_dot(b, jnp.concatenate([h, v], axis=0)), bands, halo, u)
    mix = _each(lambda a, c, v: a / c - v, win, cnt, u)
    mixed = _each(_bdot, mix, pw)
    return mix, mixed, _each(_sigmoid, z), cnt


POOL_T = 256


def _conv_taps(xv, prev8):
    return [_shift_down(xv, prev8, CONV_K - 1 - j) for j in range(CONV_K - 1)] + [xv]


def _conv_pre(taps, cw):
    y = taps[CONV_K - 1] * cw[CONV_K - 1:CONV_K]
    for j in range(CONV_K - 2, -1, -1):
        y = y + taps[j] * cw[j:j + 1]
    return y


CONV_T = 256


def _conv_specs(t, tile_of=lambda i: i):
    tiles = [pl.BlockSpec((t, D_HALF), functools.partial(lambda i, p: (tile_of(i), 2 + p), p=p)) for p in range(3)]
    halos = [pl.BlockSpec((8, D_HALF),
                          functools.partial(lambda i, p: (jnp.maximum(tile_of(i) * (t // 8) - 1, 0), 2 + p), p=p))
             for p in range(3)]
    return tiles + halos


def _pair_masks():
    r = lax.broadcasted_iota(I32, (PAIR, PAIR), 0)
    c = lax.broadcasted_iota(I32, (PAIR, PAIR), 1)
    same = jnp.right_shift(r, 6) == jnp.right_shift(c, 6)
    return same, same & (r >= c), same & (r > c), r == c


def _interleave(*stage_lists):
    live = list(stage_lists)
    while live:
        for gen in list(live):
            try:
                next(gen)
            except StopIteration:
                live.remove(gen)


def _pipelined_step(t, n, leading, trailing):
    @pl.when(t == 0)
    def _():
        _interleave(*leading())

    @pl.when(jnp.logical_and(t > 0, t < n))
    def _():
        _interleave(*leading(), *trailing())

    @pl.when(t == n)
    def _():
        _interleave(*trailing())


def _pair_common_stages(cm, qn, kn, vs, beta, g):
    same, incl, strict, eye = _pair_masks()
    incl_b = incl.astype(BF16)
    first = lax.broadcasted_iota(I32, (PAIR, HEAD), 0) < CHUNK
    cm.update(same=same, incl=incl, strict=strict, eye=eye)
    gc = _each(lambda gv: _mask_dot(incl_b, gv), g)
    q = _each(lambda v: v * QK_SCALE, qn)
    kb = _each(lambda k, b: k * b, kn, beta)
    cm.update(gc=gc, q=q, kb=kb, vb=_each(lambda v, b: v * b, vs, beta))
    yield
    both = _each(lambda a, b, c: _bdot_nt(_stack(a, b), c), kb, q, kn)
    cm.update(kk=[v[:PAIR] for v in both], qk=[v[PAIR:] for v in both])
    gc_row = _each(lambda v: _colsum(jnp.where(eye, v, 0.0)), gc)
    gl = _each(lambda v: jnp.where(first, v[CHUNK - 1:CHUNK], v[PAIR - 1:PAIR]), gc)
    egc = _each(jnp.exp, gc)
    cm.update(gl=gl, egc=egc,
              decay=_each(lambda v, r: jnp.where(incl, jnp.exp(jnp.where(incl, v - r, 0.0)), 0.0), gc, gc_row))
    yield
    cm.update(ekd=_each(lambda a, b: jnp.exp(a - b), gl, gc), cd=_each(jnp.exp, gl),
              kbg=_each(lambda k, e: k * e, kb, egc))
    yield


def _tri_inv_stages(out, a, eye_f):
    p = _each(lambda v: eye_f - v, a)
    x = _each(_bdot, a, a)
    yield
    for it in range(4):
        both = _each(lambda xv, pv: _bdot(xv, _side(pv, xv)), x, p)
        p = _each(lambda pv, b: pv + b[:, :PAIR], p, both)
        x = [b[:, PAIR:] for b in both]
        yield
    out["t"] = _each(lambda pv, xv: pv + _bdot(pv, xv), p, x)
    yield


def _chunk_scalar_spec(pairs=1, index=lambda i: (i, 0)):
    return pl.BlockSpec((16 * pairs, D_HALF), index)


SCAN_PAIRS = 2
SCAN_ROWS = SCAN_PAIRS * PAIR


def _delta_fwd(qn, kn, vs, beta, g):
    s = qn.shape[0]
    n_steps = s // SCAN_ROWS
    n_chunks = s // CHUNK
    assert INTRA_PAIRS == SCAN_PAIRS

    def body(qn_ref, kn_ref, vs_ref, beta_ref, g_ref, w_ref, att_ref, qd_ref, kd_ref, t_ref, cd_ref, o_ref, vn_ref, st_ref,
             state, u_s, w_s, att_s, qd_s, kd_s, cd_s):
        t = pl.program_id(0)

        @pl.when(t == 1)
        def _():
            state[...] = jnp.zeros_like(state)

        cur = lax.rem(t, 2)
        prev = 1 - cur
        cols = list(enumerate(HEAD_COLS))

        def recurrence():
            sm = [state[h] for h in HEADS]
            for ci in range(2 * SCAN_PAIRS):
                rs = slice(ci * CHUNK, (ci + 1) * CHUNK)
                for h in HEADS:
                    st_ref[ci, h] = sm[h]
                both = [_bdot(jnp.concatenate([w_s[prev, rs, sl], qd_s[prev, rs, sl]], axis=0), sm[h]) for h, sl in cols]
                vn = [u_s[prev, rs, sl] - both[h][:CHUNK] for h, sl in cols]
                for h, sl in cols:
                    vn_ref[rs, sl] = vn[h].astype(BF16)
                    o_ref[rs, sl] = both[h][CHUNK:]
                yield
                sm = [sm[h] * cd_s[prev, ci * 8:ci * 8 + 1, sl] + _bdot_tn(kd_s[prev, rs, sl], vn[h]) for h, sl in cols]
                yield
            for h in HEADS:
                state[h] = sm[h]
            for pp in range(SCAN_PAIRS):
                rp = slice(pp * PAIR, (pp + 1) * PAIR)
                intra = [_bdot(att_s[prev, rp, sl], vn_ref[rp, sl]) for sl in HEAD_COLS]
                for h, sl in cols:
                    o_ref[rp, sl] += intra[h]
                yield

        def factors():
            kn = _heads(kn_ref)
            cm = {}
            yield from _pair_common_stages(cm, _heads(qn_ref), kn, _heads(vs_ref), _heads(beta_ref), _heads(g_ref))
            a = _each(lambda kk, d: jnp.where(cm["strict"], kk * d, 0.0), cm["kk"], cm["decay"])
            inv = {}
            yield from _tri_inv_stages(inv, a, cm["eye"].astype(F32))
            tm = inv["t"]
            uw = _each(lambda tv, a, b: _bdot(tv, _side(a, b)), tm, cm["vb"], cm["kbg"])
            res = dict(u=[v[:, :HEAD] for v in uw], w=[v[:, HEAD:] for v in uw],
                       att=_each(lambda a, b: a * b, cm["qk"], cm["decay"]),
                       qd=_each(lambda a, b: a * b, cm["q"], cm["egc"]), kd=_each(lambda a, b: a * b, kn, cm["ekd"]))
            yield
            _put_heads(t_ref, tm)
            for key, out, keep in (("w", w_ref, w_s), ("att", att_ref, att_s), ("qd", qd_ref, qd_s), ("kd", kd_ref, kd_s)):
                _put_heads(out, res[key])
                for (pp, h), v in zip(UNITS, res[key]):
                    keep[cur, pp * PAIR:(pp + 1) * PAIR, HEAD_COLS[h]] = v.astype(BF16)
            for (pp, h), v in zip(UNITS, res["u"]):
                u_s[cur, pp * PAIR:(pp + 1) * PAIR, HEAD_COLS[h]] = v
            for ci in range(2):
                for (pp, h), v in zip(UNITS, cm["cd"]):
                    rows8 = slice(pp * 16 + ci * 8, pp * 16 + (ci + 1) * 8)
                    cd_ref[rows8, HEAD_COLS[h]] = v[ci * CHUNK:ci * CHUNK + 8]
                    cd_s[cur, rows8, HEAD_COLS[h]] = v[ci * CHUNK:ci * CHUNK + 8]
            yield

        _pipelined_step(t, n_steps, lambda: [factors()], lambda: [recurrence()])

    last = n_steps - 1
    now = lambda i: (jnp.minimum(i, last), 0)
    before = lambda i: (jnp.maximum(i - 1, 0), 0)
    rows = lambda index: pl.BlockSpec((SCAN_ROWS, D_HALF), index)
    slot = lambda r, dtype: pltpu.VMEM((2, r, D_HALF), dtype)
    return _call(
        body, name="delta_fwd", grid=(n_steps + 1,),
        in_specs=[rows(now)] * 5,
        out_specs=[rows(now)] * 5 + [_chunk_scalar_spec(SCAN_PAIRS, now), rows(before), rows(before),
                                     pl.BlockSpec((2 * SCAN_PAIRS, N_HEADS, HEAD, HEAD), lambda i: (jnp.maximum(i - 1, 0), 0, 0, 0))],
        out_shape=[_sds((s, D_HALF), BF16)] * 5 + [_sds((s // 8, D_HALF)), _sds((s, D_HALF)), _sds((s, D_HALF), BF16),
                                                  _sds((n_chunks, N_HEADS, HEAD, HEAD))],
        scratch_shapes=[pltpu.VMEM((N_HEADS, HEAD, HEAD), F32), slot(SCAN_ROWS, F32), slot(SCAN_ROWS, BF16),
                        slot(SCAN_ROWS, BF16), slot(SCAN_ROWS, BF16), slot(SCAN_ROWS, BF16), slot(16 * SCAN_PAIRS, F32)],
        compiler_params=_params("arbitrary"),
    )(qn, kn, vs, beta, g)


OUT_T = 512
OUT_ROWS = 256


def _out_fwd_bwd(x, y_pool, o, proj, target, w_out, dn_norm_w, final_norm_w):
    s = x.shape[0]
    t = OUT_T

    def body(x_ref, yp_ref, o_ref, z_ref, tg_ref, wo_ref, dnw_ref, fnw_ref,
             gwo_ref, dh_ref, dyp_ref, do_ref, dz_ref, loss_ref, gfn_ref, gdn_ref, y_ref, yt_ref, gwo_acc):
        @pl.when(pl.program_id(0) == 0)
        def _():
            loss_ref[...] = jnp.zeros_like(loss_ref)
            gfn_ref[...] = jnp.zeros_like(gfn_ref)
            gdn_ref[...] = jnp.zeros_like(gdn_ref)
            gwo_acc[...] = jnp.zeros_like(gwo_acc)

        dnw = dnw_ref[...]
        fnw = fnw_ref[...]

        def stages(rows, lead):
            for _ in range(lead):
                yield
            ypv = yp_ref[rows]
            y_ref[rows, :D_HALF] = ypv.astype(BF16)
            yt_ref[:D_HALF, rows] = ypv.T.astype(BF16)
            keep = []
            for h in HEADS:
                ov = o_ref[rows, HEAD_COLS[h]]
                zv = z_ref[rows, HEAD_COLS[h]]
                ro = lax.rsqrt(jnp.mean(ov * ov, axis=-1, keepdims=True) + EPS)
                ohat = ov * ro
                sg = _sigmoid(zv)
                keep.append((ro, ohat, zv, sg))
                ydn = ohat * dnw * (zv * sg)
                y_ref[rows, D_HALF + h * HEAD:D_HALF + (h + 1) * HEAD] = ydn.astype(BF16)
                yt_ref[D_HALF + h * HEAD:D_HALF + (h + 1) * HEAD, rows] = ydn.T.astype(BF16)
            yield
            hv = x_ref[rows] + jnp.dot(y_ref[rows], wo_ref[...], preferred_element_type=F32)
            yield
            r2 = lax.rsqrt(jnp.mean(hv * hv, axis=-1, keepdims=True) + EPS)
            hhat = hv * r2
            err = hhat * fnw - tg_ref[rows]
            loss_ref[...] += 0.5 * jnp.sum(_rowsum(err * err) * (1.0 / D_MODEL), axis=0, keepdims=True)
            dout = err * (1.0 / D_MODEL)
            gfn_ref[...] += _colsum(dout * hhat)
            dhh = dout * fnw
            dh = r2 * (dhh - hhat * jnp.mean(dhh * hhat, axis=-1, keepdims=True))
            dh_ref[rows] = dh
            yield
            if lead == t // OUT_ROWS - 1:
                gwo_acc[...] += _bdot(yt_ref[...], dh_ref[...])
            dy = _bdot_nt(dh, wo_ref[...])
            yield
            dyp_ref[rows] = dy[:, :D_HALF]
            gdn = jnp.zeros((1, HEAD), F32)
            for h in HEADS:
                ro, ohat, zv, sg = keep[h]
                dyd = dy[:, D_HALF + h * HEAD:D_HALF + (h + 1) * HEAD]
                sz = zv * sg
                dz_ref[rows, HEAD_COLS[h]] = (dyd * ohat * dnw * (sg * (1.0 + zv * (1.0 - sg)))).astype(BF16)
                gdn = gdn + _colsum(dyd * ohat * sz)
                doh = dyd * dnw * sz
                do_ref[rows, HEAD_COLS[h]] = ro * (doh - ohat * jnp.mean(doh * ohat, axis=-1, keepdims=True))
            gdn_ref[...] += gdn
            yield

        _interleave(*[stages(slice(k * OUT_ROWS, (k + 1) * OUT_ROWS), k) for k in range(t // OUT_ROWS)])

        @pl.when(pl.program_id(0) == pl.num_programs(0) - 1)
        def _():
            gwo_ref[...] = gwo_acc[...].astype(BF16)

    wide = pl.BlockSpec((t, D_MODEL), lambda i: (i, 0))
    half = pl.BlockSpec((t, D_HALF), lambda i: (i, 0))
    const = lambda shape: pl.BlockSpec(shape, lambda i: (0,) * len(shape))
    return _call(
        body, name="out_fwd_bwd", grid=(s // t,),
        in_specs=[wide, half, half, pl.BlockSpec((t, D_HALF), lambda i: (i, 5)), wide,
                  const((D_MODEL, D_MODEL)), const((1, HEAD)), const((1, D_MODEL))],
        out_specs=[const((D_MODEL, D_MODEL)), wide, half, half, half,
                   const((1, HEAD)), const((1, D_MODEL)), const((1, HEAD))],
        out_shape=[_sds((D_MODEL, D_MODEL), BF16), _sds((s, D_MODEL)), _sds((s, D_HALF)), _sds((s, D_HALF)),
                   _sds((s, D_HALF), BF16), _sds((1, HEAD)), _sds((1, D_MODEL)), _sds((1, HEAD))],
        scratch_shapes=[pltpu.VMEM((t, D_MODEL), BF16), pltpu.VMEM((D_MODEL, t), BF16), pltpu.VMEM((D_MODEL, D_MODEL), F32)],
        compiler_params=_params("arbitrary"),
    )(x, y_pool, o, proj, target, w_out, dn_norm_w, final_norm_w)


def _grad_w_in(at, pieces):
    m, s = at.shape
    n = len(pieces)
    tn, tk = D_HALF, min(s, 1024)

    def body(a_ref, *refs):
        p_refs, o_ref, acc = refs[:n], refs[n], refs[n + 1]

        @pl.when(pl.program_id(0) == 0)
        def _():
            acc[...] = jnp.zeros_like(acc)

        av = a_ref[...]
        for p in range(n):
            acc[:, p * tn:(p + 1) * tn] += _bdot(av, p_refs[p][...])

        @pl.when(pl.program_id(0) == pl.num_programs(0) - 1)
        def _():
            for j in range(4):
                base = j * BLK_IN // HEAD * HEAD
                win = acc[:, base:base + BLK_IN_PAD]
                if j * BLK_IN > base:
                    win = pltpu.roll(win, BLK_IN_PAD - (j * BLK_IN - base), 1)
                o_ref[j] = win.astype(BF16)

    return _call(
        body, name="grad_w_in", grid=(s // tk,),
        in_specs=[pl.BlockSpec((m, tk), lambda k: (0, k))] + [pl.BlockSpec((tk, tn), lambda k: (k, 0))] * n,
        out_specs=pl.BlockSpec((4, m, BLK_IN_PAD), lambda k: (0, 0, 0)),
        out_shape=_sds((4, m, BLK_IN_PAD), BF16),
        scratch_shapes=[pltpu.VMEM((m, n * tn), F32)],
        compiler_params=_params("arbitrary"),
    )(at, *pieces)


def _delta_bwd(do, vn, qd, kd, w, att, cd, st, qn, kn, vs, beta, g, tm):
    s = do.shape[0]
    n_steps = s // SCAN_ROWS
    assert INTRA_PAIRS == SCAN_PAIRS

    def body(do_ref, vn_ref, qd_ref, kd_ref, w_ref, att_ref, cd_ref, st_ref, qn_ref, kn_ref, vs_ref, beta_ref, g_ref, t_ref,
             dqn_ref, dkn_ref, dvs_ref, dbeta_ref, dg_ref, dstate, du_s, dw_s, datt_s, dqd_s, dkd_s, dcd_s):
        t = pl.program_id(0)

        @pl.when(t == 0)
        def _():
            dstate[...] = jnp.zeros_like(dstate)

        cur = lax.rem(t, 2)
        prev = 1 - cur
        cols = list(enumerate(HEAD_COLS))
        _, incl, _, _ = _pair_masks()

        def recurrence():
            dv_intra = []
            for pp in range(SCAN_PAIRS):
                rp = slice(pp * PAIR, (pp + 1) * PAIR)
                dv_intra.append([_bdot_tn(att_ref[rp, sl], do_ref[rp, sl]) for _, sl in cols])
                for _, sl in cols:
                    datt_s[cur, rp, sl] = jnp.where(incl, _bdot_nt(do_ref[rp, sl], vn_ref[rp, sl]), 0.0)
                yield
            ds = [dstate[h] for h in HEADS]
            for ci in range(2 * SCAN_PAIRS - 1, -1, -1):
                rs = slice(ci * CHUNK, (ci + 1) * CHUNK)
                in_pair = slice((ci % 2) * CHUNK, (ci % 2 + 1) * CHUNK)
                sm = [st_ref[ci, h] for h in HEADS]
                dvn = [dv_intra[ci // 2][h][in_pair] + _bdot(kd_ref[rs, sl], ds[h]) for h, sl in cols]
                dkd = [_bdot_nt(vn_ref[rs, sl], ds[h]) for h, sl in cols]
                dcd = [jnp.broadcast_to(_rowsum(_colsum(ds[h] * sm[h])), (8, HEAD)) for h in HEADS]
                yield
                both = [_bdot_nt(_stack(do_ref[rs, sl], dvn[h]), sm[h]) for h, sl in cols]
                for h, sl in cols:
                    du_s[cur, rs, sl] = dvn[h].astype(BF16)
                    dqd_s[cur, rs, sl] = both[h][:CHUNK]
                    dw_s[cur, rs, sl] = (-both[h][CHUNK:]).astype(BF16)
                    dkd_s[cur, rs, sl] = dkd[h]
                    dcd_s[cur, ci * 8:(ci + 1) * 8, sl] = dcd[h]
                ds = [ds[h] * cd_ref[ci * 8:ci * 8 + 1, sl]
                      + _bdot_tn(_stack(qd_ref[rs, sl], w_ref[rs, sl]), _stack(do_ref[rs, sl], -dvn[h])) for h, sl in cols]
                yield
            for h in HEADS:
                dstate[h] = ds[h]

        def factors(units):
            _heads = functools.partial(_heads_of, units=units)
            _put_heads = functools.partial(_put_heads_of, units=units)
            ones = jnp.ones((2 * PAIR, HEAD), BF16)
            tn = (((0,), (0,)), ((), ()))
            kept = lambda ref, rows=PAIR: [ref[prev, pp * rows:(pp + 1) * rows, HEAD_COLS[h]] for pp, h in units]
            kn, vs, beta = _heads(kn_ref), _heads(vs_ref), _heads(beta_ref)
            cm = {}
            yield from _pair_common_stages(cm, _heads(qn_ref), kn, vs, beta, _heads(g_ref))
            tmv = _heads(t_ref)
            duv, dwv, dattv, dqdv, dkdv = kept(du_s), kept(dw_s), kept(datt_s), kept(dqd_s), kept(dkd_s)
            duw = _each(_side, duv, dwv)
            both = _each(_bdot_tn, tmv, duw)
            dvb, dkbg = [v[:, :HEAD] for v in both], [v[:, HEAD:] for v in both]
            dt = _each(lambda a, b, c: _bdot_nt(a, _side(b, c)), duw, cm["vb"], cm["kbg"])
            yield
            m1 = _each(_bdot_tn, tmv, dt)
            yield
            da = _each(lambda a, b: -jnp.where(cm["strict"], _bdot_nt(a, b), 0.0), m1, tmv)
            yield
            dkk = _each(lambda a, b: a * b, da, cm["decay"])
            dqk = _each(lambda a, b: a * b, dattv, cm["decay"])
            dd = _each(lambda a, b, c, d: a * b + c * d, dkk, cm["kk"], dqk, cm["qk"])
            dkq = _each(_stack, dkk, dqk)
            both = _each(_bdot, dkq, kn)
            dkb = _each(lambda a, c, d: a[:PAIR] + c * d, both, dkbg, cm["egc"])
            dq = _each(lambda a, c, d: a[PAIR:] + c * d, both, dqdv, cm["egc"])
            yield
            dkn = _each(lambda a, b, c: _bdot_tn(a, _stack(b, c)), dkq, cm["kb"], cm["q"])
            dkn = _each(lambda a, b, c, d, e: a + b * c + d * e, dkn, dkdv, cm["ekd"], dkb, beta)
            t_kd = _each(lambda a, b, c: _rowsum(a * b * c), dkdv, kn, cm["ekd"])
            yield
            split = _each(_split, dd)
            rows_dd = [jnp.dot(_side(hi, lo), ones, preferred_element_type=F32) for hi, lo in split]
            cols_dd = [lax.dot_general(_stack(hi, lo), ones, tn, preferred_element_type=F32) for hi, lo in split]
            yield
            dgc = _each(lambda r, c, a, b, e, f, k, tk: r - c + _rowsum(a * b * e) + _rowsum(f * k) - tk,
                        rows_dd, cols_dd, dqdv, cm["q"], cm["egc"], dkbg, cm["kbg"], t_kd)
            same_b = cm["same"].astype(BF16)
            rowi = lax.broadcasted_iota(I32, (PAIR, HEAD), 0)
            dcd = _each(lambda d: jnp.where(rowi < CHUNK, d[0:1], d[8:9]), kept(dcd_s, rows=16))
            dgl = _each(lambda tk, d, c: _mask_dot(same_b, jnp.broadcast_to(tk, (PAIR, HEAD))) + d * c, t_kd, dcd, cm["cd"])
            yield
            is_last = jnp.bitwise_and(rowi, CHUNK - 1) == CHUNK - 1
            dgc = _each(lambda a, b: a + jnp.where(is_last, b, 0.0), dgc, dgl)
            r = lax.broadcasted_iota(I32, (PAIR, PAIR), 0)
            c = lax.broadcasted_iota(I32, (PAIR, PAIR), 1)
            upper_b = (cm["same"] & (r <= c)).astype(BF16)
            _put_heads(dg_ref, _each(lambda v: _mask_dot(upper_b, v), dgc))
            yield
            _put_heads(dbeta_ref, _each(lambda a, b, c, d: jnp.broadcast_to(_rowsum(a * b) + _rowsum(c * d), (PAIR, HEAD)),
                                        dkb, kn, dvb, vs))
            _put_heads(dqn_ref, _each(lambda v: v * QK_SCALE, dq))
            _put_heads(dkn_ref, dkn)
            _put_heads(dvs_ref, _each(lambda a, b: a * b, dvb, beta))
            yield

        _pipelined_step(t, n_steps, lambda: [recurrence()],
                        lambda: [factors(UNITS[pp * N_HEADS:(pp + 1) * N_HEADS]) for pp in range(INTRA_PAIRS)])

    last = n_steps - 1
    now = lambda i: (jnp.maximum(last - i, 0), 0)
    after = lambda i: (jnp.minimum(n_steps - i, last), 0)
    rows = lambda index: pl.BlockSpec((SCAN_ROWS, D_HALF), index)
    slot = lambda r, dtype: pltpu.VMEM((2, r, D_HALF), dtype)
    return _call(
        body, name="delta_bwd", grid=(n_steps + 1,),
        in_specs=[rows(now)] * 6 + [_chunk_scalar_spec(SCAN_PAIRS, now),
                                    pl.BlockSpec((2 * SCAN_PAIRS, N_HEADS, HEAD, HEAD), lambda i: (jnp.maximum(last - i, 0), 0, 0, 0))]
                 + [rows(after)] * 6,
        out_specs=[rows(after)] * 5,
        out_shape=[_sds((s, D_HALF))] * 5,
        scratch_shapes=[pltpu.VMEM((N_HEADS, HEAD, HEAD), F32), slot(SCAN_ROWS, BF16), slot(SCAN_ROWS, BF16),
                        slot(SCAN_ROWS, F32), slot(SCAN_ROWS, F32), slot(SCAN_ROWS, F32), slot(16 * SCAN_PAIRS, F32)],
        compiler_params=_params("arbitrary"),
    )(do, vn, qd, kd, w, att, cd, st, qn, kn, vs, beta, g, tm)


def _fused_call(name, n_steps, parts):
    n_in = [len(p["inputs"]) for p in parts]
    n_out = [len(p["out_shape"]) for p in parts]
    n_scr = [len(p["scratch"]) for p in parts]

    def body(*refs):
        ins, outs, scr = refs[:sum(n_in)], refs[sum(n_in):sum(n_in) + sum(n_out)], refs[sum(n_in) + sum(n_out):]
        gens, a, b, c = [], 0, 0, 0
        for p, ni, no, ns in zip(parts, n_in, n_out, n_scr):
            gens.append(p["stages"](ins[a:a + ni], outs[b:b + no], scr[c:c + ns]))
            a, b, c = a + ni, b + no, c + ns
        _interleave(*gens)

    flat = lambda key: [v for p in parts for v in p[key]]
    res = _call(
        body, name=name, grid=(n_steps,),
        in_specs=flat("in_specs"), out_specs=flat("out_specs"), out_shape=flat("out_shape"),
        scratch_shapes=flat("scratch"),
        compiler_params=_params("arbitrary"),
    )(*flat("inputs"))
    out, b = [], 0
    for no in n_out:
        out.append(res[b:b + no])
        b += no
    return out


def _pool_bwd_part(proj, dyp, pool_w, pool_scale, tile_of, n_tiles):
    s = proj.shape[0]
    t = POOL_T
    hb = t // HEAD
    last = s // HEAD - 1

    def stages(ins, outs, scratch):
        u_ref, z_ref, halo_ref, dy_ref, zn_ref, dyn_ref, pw_ref, ps_ref, band_ref, aband_ref = ins
        du_ref, dz_ref, gpw_ref, gps_ref = outs
        tile = tile_of(pl.program_id(0))

        @pl.when(pl.program_id(0) == 0)
        def _():
            gpw_ref[...] = jnp.zeros_like(gpw_ref)
            gps_ref[...] = jnp.zeros_like(gps_ref)

        live = (tile > 0).astype(F32)
        more = (tile < n_tiles - 1).astype(F32)
        groups = lambda ref: [ref[:, sl] for sl in HEAD_COLS]
        z, ps, dy = groups(z_ref), groups(ps_ref), groups(dy_ref)
        pw = [pw_ref[g] for g in HEADS]
        mix, mixed, sg, cnt = _pool_mix(groups(u_ref), [h * live for h in groups(halo_ref)], z, pw,
                                        [band_ref[g] for g in HEADS], tile * t)
        yield
        sz = _each(lambda a, b: a * b, z, sg)
        for sl, d, m, p, s_, zg in zip(HEAD_COLS, dy, mixed, ps, sg, z):
            dz_ref[:, sl] = (d * m * p * (s_ * (1.0 + zg * (1.0 - s_)))).astype(BF16)
        for sl, d, m, a in zip(HEAD_COLS, dy, mixed, sz):
            gps_ref[:, sl] += _colsum(d * m * a)
        dmixed = _each(lambda d, p, a: d * p * a, dy, ps, sz)
        yield
        for g, gp in enumerate(_each(_bdot_tn, mix, dmixed)):
            gpw_ref[g] += gp
        dmix = _each(_bdot_nt, dmixed, pw)
        yield
        dmix_n = _each(lambda d, p, zn, w_: _bdot_nt(d * more * p * (zn * _sigmoid(zn)), w_),
                       groups(dyn_ref), ps, groups(zn_ref), pw)
        yield
        scaled = [jnp.concatenate([a / c, b * (1.0 / w)], axis=0) for a, c, b, w in zip(dmix, cnt, dmix_n, WINDOWS)]
        du = _each(lambda b, s_, d: _mask_dot(b, s_) - d, [aband_ref[g] for g in HEADS], scaled, dmix)
        for sl, v in zip(HEAD_COLS, du):
            du_ref[:, sl] = v.astype(BF16)
        yield

    tile = lambda col: pl.BlockSpec((t, D_HALF), lambda i: (tile_of(i), col))
    below = lambda col: pl.BlockSpec((HEAD, D_HALF), lambda i: (jnp.minimum((tile_of(i) + 1) * hb, last), col))
    const3 = lambda shape: pl.BlockSpec(shape, lambda i: (0, 0, 0))
    return dict(
        inputs=[proj, proj, proj, dyp, proj, dyp, pool_w, pool_scale, _pool_bands(t), _pool_bands(t, anti=True)],
        in_specs=[tile(0), tile(1), pl.BlockSpec((HEAD, D_HALF), lambda i: (jnp.maximum(tile_of(i) * hb - 1, 0), 0)),
                  tile(0), below(1), below(0), const3((N_HEADS, HEAD, HEAD)), pl.BlockSpec((1, D_HALF), lambda i: (0, 0)),
                  const3((N_HEADS, t, HEAD + t)), const3((N_HEADS, t, HEAD + t))],
        out_specs=[tile(0), tile(0), const3((N_HEADS, HEAD, HEAD)), pl.BlockSpec((1, D_HALF), lambda i: (0, 0))],
        out_shape=[_sds((s, D_HALF), BF16), _sds((s, D_HALF), BF16), _sds((N_HEADS, HEAD, HEAD)), _sds((1, D_HALF))],
        scratch=[], stages=stages)


def _conv_bwd_part(proj, pre, conv_w, a_log, dt_bias, dqn, dkn, dvs, dbeta, dg, tile_of, n_tiles):
    s = proj.shape[0]
    t = CONV_T

    def stages(ins, outs, scratch):
        (q_ref, k_ref, v_ref, yq_ref, yk_ref, yv_ref, ba_ref, cw_ref, al_ref, dtb_ref,
         dqn_ref, dkn_ref, dvs_ref, dbeta_ref, dg_ref) = ins
        oq_ref, ok_ref, ov_ref, dba_ref, gcw_out, gsm_out = outs
        below, gcw_ref, gsm_ref = scratch
        step = pl.program_id(0)

        @pl.when(step == 0)
        def _():
            gcw_ref[...] = jnp.zeros_like(gcw_ref)
            gsm_ref[...] = jnp.zeros_like(gsm_ref)
            below[...] = jnp.zeros_like(below)

        parts = ((q_ref, yq_ref, dqn_ref, oq_ref), (k_ref, yk_ref, dkn_ref, ok_ref), (v_ref, yv_ref, dvs_ref, ov_ref))
        for p, (x_ref, y_ref, d_ref, o_ref) in enumerate(parts):
            for h in HEADS:
                cs = HEAD_COLS[h]
                wide = slice(p * D_HALF + h * HEAD, p * D_HALF + (h + 1) * HEAD)
                cw = cw_ref[:, wide]
                y = y_ref[:, cs]
                sg = _sigmoid(y)
                sv = y * sg
                ds = d_ref[:, cs]
                if p < 2:
                    rn = lax.rsqrt(_rowsum(sv * sv) + EPS)
                    nrm = sv * rn
                    ds = rn * (ds - nrm * _rowsum(ds * nrm))
                dy = ds * (sg * (1.0 + y * (1.0 - sg)))
                nxt = below[:, wide]
                ahead = [dy] + [_shift_up(dy, nxt, sft) for sft in range(1, CONV_K)]
                xv = x_ref[:, cs]
                acc = dy * cw[CONV_K - 1:CONV_K]
                for sft in range(1, CONV_K):
                    acc = acc + ahead[sft] * cw[CONV_K - 1 - sft:CONV_K - sft]
                for j in range(CONV_K):
                    gcw_ref[8 * j:8 * j + 8, wide] += _rows8(xv * ahead[CONV_K - 1 - j])
                o_ref[:, cs] = acc.astype(BF16)
                below[:, wide] = dy[0:8]
                yield

        ba = ba_ref[...]
        lane = lax.broadcasted_iota(I32, (t, HEAD), 1)
        lane8 = lax.broadcasted_iota(I32, (8, HEAD), 1)
        dba = jnp.zeros((t, HEAD), F32)
        gsm = jnp.zeros((8, HEAD), F32)
        for h in HEADS:
            beta = _sigmoid(ba[:, h:h + 1])
            dbeta = dbeta_ref[:, h * HEAD:h * HEAD + 1]
            xg = ba[:, N_HEADS + h:N_HEADS + h + 1] + dtb_ref[0:1, h:h + 1]
            nexp = -jnp.exp(al_ref[0:1, h:h + 1])
            dgv = dg_ref[:, h * HEAD:h * HEAD + 1]
            da = dgv * nexp * _sigmoid(xg)
            dba = dba + jnp.where(lane == h, dbeta * beta * (1.0 - beta), 0.0) + jnp.where(lane == N_HEADS + h, da, 0.0)
            gsm = (gsm + jnp.where(lane8 == h, _rows8(dgv * nexp * _softplus(xg)), 0.0)
                   + jnp.where(lane8 == N_HEADS + h, _rows8(da), 0.0))
        dba_ref[...] = jnp.zeros_like(dba_ref)
        dba_ref[:, :HEAD] = dba.astype(BF16)
        gsm_ref[...] += gsm
        yield

        @pl.when(step == n_tiles - 1)
        def _():
            gcw_out[...] = jnp.zeros_like(gcw_out)
            for j in range(CONV_K):
                gcw_out[j:j + 1, :] = _colsum(gcw_ref[8 * j:8 * j + 8, :])
            gsm_out[...] = jnp.broadcast_to(_colsum(gsm_ref[...]), (8, HEAD))

    row = pl.BlockSpec((t, D_HALF), lambda i: (tile_of(i), 0))
    const = lambda shape: pl.BlockSpec(shape, lambda i: (0, 0))
    return dict(
        inputs=[proj] * 3 + list(pre) + [proj, conv_w, a_log, dt_bias, dqn, dkn, dvs, dbeta, dg],
        in_specs=_conv_specs(t, tile_of)[:3] + [row] * 3
                 + [pl.BlockSpec((t, HEAD), lambda i: (tile_of(i), COL_BA // HEAD)),
                    const((CONV_K, 3 * D_HALF)), const((1, N_HEADS)), const((1, N_HEADS))] + [row] * 5,
        out_specs=[row, row, row, row, const((8, 3 * D_HALF)), const((8, HEAD))],
        out_shape=[_sds((s, D_HALF), BF16)] * 4 + [_sds((8, 3 * D_HALF)), _sds((8, HEAD))],
        scratch=[pltpu.VMEM((8, 3 * D_HALF), F32), pltpu.VMEM((8 * CONV_K, 3 * D_HALF), F32), pltpu.VMEM((8, HEAD), F32)],
        stages=stages)


def _pool_fwd_part(proj, pool_w, pool_scale):
    s = proj.shape[0]
    t = POOL_T
    hb = t // HEAD

    def stages(ins, outs, scratch, tile=None):
        u_ref, z_ref, halo_ref, pw_ref, ps_ref, band_ref = ins
        y_ref, = outs
        i = pl.program_id(0) if tile is None else tile
        live = (i > 0).astype(F32)
        groups = lambda ref: [ref[:, sl] for sl in HEAD_COLS]
        z = groups(z_ref)
        u, halo = groups(u_ref), [h * live for h in groups(halo_ref)]
        yield
        _, mixed, sg, _ = _pool_mix(u, halo, z, [pw_ref[g] for g in HEADS], [band_ref[g] for g in HEADS], i * t)
        yield
        for sl, m, zg, s_ in zip(HEAD_COLS, mixed, z, sg):
            y_ref[:, sl] = m * ps_ref[:, sl] * (zg * s_)
        yield

    const3 = lambda shape: pl.BlockSpec(shape, lambda i: (0, 0, 0))
    return dict(
        inputs=[proj, proj, proj, pool_w, pool_scale, _pool_bands(t)],
        in_specs=[pl.BlockSpec((t, D_HALF), lambda i: (i, 0)), pl.BlockSpec((t, D_HALF), lambda i: (i, 1)),
                  pl.BlockSpec((HEAD, D_HALF), lambda i: (jnp.maximum(i * hb - 1, 0), 0)),
                  const3((N_HEADS, HEAD, HEAD)), pl.BlockSpec((1, D_HALF), lambda i: (0, 0)), const3((N_HEADS, t, HEAD + t))],
        out_specs=[pl.BlockSpec((t, D_HALF), lambda i: (i, 0))], out_shape=[_sds((s, D_HALF))],
        scratch=[], stages=stages)


def _conv_fwd_part(proj, conv_w, a_log, dt_bias):
    s = proj.shape[0]
    t = CONV_T

    def stages(ins, outs, scratch, tile=None):
        q_ref, k_ref, v_ref, hq_ref, hk_ref, hv_ref, ba_ref, cw_ref, al_ref, dtb_ref = ins
        qn_ref, kn_ref, vs_ref, beta_ref, g_ref, yq_ref, yk_ref, yv_ref = outs
        live = ((pl.program_id(0) if tile is None else tile) > 0).astype(F32)
        parts = ((q_ref, hq_ref, qn_ref, yq_ref), (k_ref, hk_ref, kn_ref, yk_ref), (v_ref, hv_ref, vs_ref, yv_ref))
        for p, (x_ref, h_ref, o_ref, y_ref) in enumerate(parts):
            for h in HEADS:
                cs = HEAD_COLS[h]
                taps = _conv_taps(x_ref[:, cs], h_ref[:, cs] * live)
                y = _conv_pre(taps, cw_ref[:, p * D_HALF + h * HEAD:p * D_HALF + (h + 1) * HEAD])
                y_ref[:, cs] = y
                sv = y * _sigmoid(y)
                o_ref[:, cs] = sv if p == 2 else sv * lax.rsqrt(_rowsum(sv * sv) + EPS)
                yield
        ba = ba_ref[...]
        for h in HEADS:
            beta = _sigmoid(ba[:, h:h + 1])
            gl = -jnp.exp(al_ref[0:1, h:h + 1]) * _softplus(ba[:, N_HEADS + h:N_HEADS + h + 1] + dtb_ref[0:1, h:h + 1])
            beta_ref[:, HEAD_COLS[h]] = jnp.broadcast_to(beta, (t, HEAD))
            g_ref[:, HEAD_COLS[h]] = jnp.broadcast_to(gl, (t, HEAD))
        yield

    row = pl.BlockSpec((t, D_HALF), lambda i: (i, 0))
    const = lambda shape: pl.BlockSpec(shape, lambda i: (0, 0))
    return dict(
        inputs=[proj] * 7 + [conv_w, a_log, dt_bias],
        in_specs=_conv_specs(t) + [pl.BlockSpec((t, HEAD), lambda i: (i, COL_BA // HEAD)),
                                   const((CONV_K, 3 * D_HALF)), const((1, N_HEADS)), const((1, N_HEADS))],
        out_specs=[row] * 8, out_shape=[_sds((s, D_HALF))] * 8, scratch=[], stages=stages)


def _front_fwd(x, norm_w, w_pad, conv_w, a_log, dt_bias, pool_w, pool_scale, after):
    s = x.shape[0]
    t = CONV_T
    n_tiles = s // t
    assert POOL_T == CONV_T
    like_proj = _sds((s, N_IN_PAD))
    conv = _conv_fwd_part(like_proj, conv_w, a_log, dt_bias)
    pool = _pool_fwd_part(like_proj, pool_w, pool_scale)
    bands = pool["inputs"][-1]
    mxu_n = 256
    col_bounds = list(range(0, N_IN_PAD, 3 * mxu_n)) + [N_IN_PAD]

    def body(x_ref, nw_ref, w_ref, cw_ref, al_ref, dtb_ref, pw_ref, ps_ref, band_ref, after_ref,
             proj_ref, nt_ref, qn_ref, kn_ref, vs_ref, beta_ref, g_ref, yq_ref, yk_ref, yv_ref, y_ref, prev):
        del after_ref
        i = pl.program_id(0)

        @pl.when(i == 0)
        def _():
            prev[...] = jnp.zeros_like(prev)

        tile = jnp.maximum(i - 1, 0)
        main, above8, above = pl.ds(HEAD, t), pl.ds(HEAD - 8, 8), pl.ds(0, HEAD)
        cols = lambda rows, c0, width=D_HALF: prev.at[rows, pl.ds(c0, width)]
        conv_ins = (cols(main, 2 * D_HALF), cols(main, 3 * D_HALF), cols(main, 4 * D_HALF),
                    cols(above8, 2 * D_HALF), cols(above8, 3 * D_HALF), cols(above8, 4 * D_HALF),
                    cols(main, COL_BA, HEAD), cw_ref, al_ref, dtb_ref)
        pool_ins = (cols(main, 0), cols(main, D_HALF), cols(above, 0), pw_ref, ps_ref, band_ref)

        def projection():
            xv = x_ref[...]
            r = lax.rsqrt(jnp.mean(xv * xv, axis=-1, keepdims=True) + EPS)
            nv = xv * r * nw_ref[...]
            nt_ref[...] = nv.T.astype(BF16)
            nb = nv.astype(BF16)
            yield
            for lo, hi in zip(col_bounds[:-1], col_bounds[1:]):
                proj_ref[:, lo:hi] = jnp.dot(nb, w_ref[:, lo:hi], preferred_element_type=F32)
                yield

        conv_outs = (qn_ref, kn_ref, vs_ref, beta_ref, g_ref, yq_ref, yk_ref, yv_ref)
        _pipelined_step(i, n_tiles, lambda: [projection()],
                        lambda: [conv["stages"](conv_ins, conv_outs, (), tile), pool["stages"](pool_ins, (y_ref,), (), tile)])
        prev[0:HEAD] = prev[t:t + HEAD]
        prev[HEAD:HEAD + t] = proj_ref[...]

    last = n_tiles - 1
    now = lambda i: (jnp.minimum(i, last), 0)
    before = lambda i: (jnp.maximum(i - 1, 0), 0)
    const = lambda a: pl.BlockSpec(a.shape, lambda i: (0,) * a.ndim)
    half = pl.BlockSpec((t, D_HALF), before)
    return _call(
        body, name="front_fwd", grid=(n_tiles + 1,),
        in_specs=[pl.BlockSpec((t, D_MODEL), now), const(norm_w), const(w_pad), const(conv_w), const(a_log), const(dt_bias),
                  const(pool_w), const(pool_scale), const(bands), pl.BlockSpec(memory_space=pl.ANY)],
        out_specs=[pl.BlockSpec((t, N_IN_PAD), now), pl.BlockSpec((D_MODEL, t), lambda i: (0, jnp.minimum(i, last)))]
                  + [half] * 9,
        out_shape=[_sds((s, N_IN_PAD)), _sds((D_MODEL, s), BF16)] + [_sds((s, D_HALF))] * 9,
        scratch_shapes=[pltpu.VMEM((HEAD + t, N_IN_PAD), F32)],
        compiler_params=_params("arbitrary"),
    )(x, norm_w, w_pad, conv_w, a_log, dt_bias, pool_w, pool_scale, bands, after)


def _conv_pool_bwd(proj, pre, conv_w, a_log, dt_bias, dqn, dkn, dvs, dbeta, dg, dyp, pool_w, pool_scale):
    n_tiles = proj.shape[0] // CONV_T
    assert POOL_T == CONV_T
    tile_of = lambda i: n_tiles - 1 - i
    return _fused_call("conv_pool_bwd", n_tiles, [
        _conv_bwd_part(proj, pre, conv_w, a_log, dt_bias, dqn, dkn, dvs, dbeta, dg, tile_of, n_tiles),
        _pool_bwd_part(proj, dyp, pool_w, pool_scale, tile_of, n_tiles)])


def _rows8(x):
    acc = x[0:8]
    for r in range(8, x.shape[0], 8):
        acc = acc + x[r:r + 8]
    return acc


IN_T = 512


def _in_bwd(x, dh, norm_w, w_pad, pieces, after):
    s = x.shape[0]
    t = IN_T
    widths = [D_HALF] * 6 + [N_IN_PAD - COL_BA]

    def body(*refs):
        x_ref, dh_ref, nw_ref, w_ref = refs[:4]
        p_refs = refs[4:4 + len(pieces)]
        gx_ref, gnw_ref = refs[5 + len(pieces):]

        @pl.when(pl.program_id(0) == 0)
        def _():
            gnw_ref[...] = jnp.zeros_like(gnw_ref)

        dn = jnp.zeros((t, D_MODEL), F32)
        col = 0
        for p_ref, wd in zip(p_refs, widths):
            dn = dn + _bdot_nt(p_ref[...], w_ref[:, col:col + wd])
            col += wd
        xv = x_ref[...]
        r = lax.rsqrt(jnp.mean(xv * xv, axis=-1, keepdims=True) + EPS)
        xhat = xv * r
        gnw_ref[...] += _colsum(dn * xhat)
        dxh = dn * nw_ref[...]
        gx_ref[...] = dh_ref[...] + r * (dxh - xhat * jnp.mean(dxh * xhat, axis=-1, keepdims=True))

    wide = pl.BlockSpec((t, D_MODEL), lambda i: (i, 0))
    return _call(
        body, name="in_bwd", grid=(s // t,),
        in_specs=[wide, wide, pl.BlockSpec((1, D_MODEL), lambda i: (0, 0)),
                  pl.BlockSpec((D_MODEL, N_IN_PAD), lambda i: (0, 0))]
                 + [pl.BlockSpec((t, wd), lambda i: (i, 0)) for wd in widths] + [pl.BlockSpec(memory_space=pl.ANY)],
        out_specs=[wide, pl.BlockSpec((1, D_MODEL), lambda i: (0, 0))],
        out_shape=[_sds((s, D_MODEL)), _sds((1, D_MODEL))],
        compiler_params=_params("arbitrary"),
    )(x, dh, norm_w, w_pad, *pieces, after)


def _adamw_shard(name, w, g_own, g_got, cidx, m, v):
    _, r, c = w.shape
    half = r // 2
    rows = 256 if half % 256 == 0 else half
    per_half = half // rows

    def body(c_ref, w_ref, go_ref, gg_ref, m_ref, v_ref, gout_ref, d_ref, nm_ref, nv_ref):
        mine = (pl.program_id(0) // per_half) == c_ref[0]
        gv = jnp.where(mine, go_ref[:, :c], gg_ref[:, :c])
        gout_ref[0] = gv
        mn = ADAM_B1 * m_ref[0] + (1.0 - ADAM_B1) * gv
        vn = ADAM_B2 * v_ref[0] + (1.0 - ADAM_B2) * (gv * gv)
        m_hat = mn / (1.0 - ADAM_B1 ** ADAM_STEP)
        v_hat = vn / (1.0 - ADAM_B2 ** ADAM_STEP)
        d_ref[0] = -ADAM_LR * (m_hat / (jnp.sqrt(v_hat) + ADAM_EPS) + ADAM_WD * w_ref[0])
        nm_ref[0] = mn
        nv_ref[0] = vn

    blk = pl.BlockSpec((1, rows, c), lambda i, c_ref: (0, i, 0))
    gblk = pl.BlockSpec((rows, g_own.shape[1]), lambda i, c_ref: (i % per_half, 0))
    return _call(
        body, name=name,
        grid_spec=pltpu.PrefetchScalarGridSpec(
            num_scalar_prefetch=1, grid=(2 * per_half,),
            in_specs=[blk, gblk, gblk, blk, blk], out_specs=[blk] * 4),
        out_shape=[_sds((1, r, c))] * 4,
        compiler_params=_params("arbitrary"),
    )(cidx, w, g_own, g_got, m, v)


def _adamw_tiles(name, w, g, m, v):
    n = w.shape[0]
    nb = 77 if n % 77 == 0 else n

    def body(w_ref, g_ref, m_ref, v_ref, d_ref, nm_ref, nv_ref):
        gv = g_ref[...]
        mn = ADAM_B1 * m_ref[...] + (1.0 - ADAM_B1) * gv
        vn = ADAM_B2 * v_ref[...] + (1.0 - ADAM_B2) * (gv * gv)
        m_hat = mn / (1.0 - ADAM_B1 ** ADAM_STEP)
        v_hat = vn / (1.0 - ADAM_B2 ** ADAM_STEP)
        d_ref[...] = -ADAM_LR * (m_hat / (jnp.sqrt(v_hat) + ADAM_EPS) + ADAM_WD * w_ref[...])
        nm_ref[...] = mn
        nv_ref[...] = vn

    blk = pl.BlockSpec((nb, 8, HEAD), lambda i: (i, 0, 0))
    return _call(
        body, name=name, grid=(n // nb,),
        in_specs=[blk] * 4, out_specs=[blk] * 3, out_shape=[_sds(w.shape)] * 3,
        compiler_params=_params("arbitrary"),
    )(w, g, m, v)


def _make_copy(src, dst, send, recv, target):
    if target is None:
        return pltpu.make_async_copy(src, dst, recv)
    return pltpu.make_async_remote_copy(src_ref=src, dst_ref=dst, send_sem=send, recv_sem=recv,
                                        device_id=target, device_id_type=pl.DeviceIdType.MESH)


def _exchange(name, inputs, out_shapes, phases):
    n_in = len(inputs)
    n_out = len(out_shapes)
    n_cp = sum(len(p) for p in phases)

    def body(*refs):
        ins, outs = refs[:n_in], refs[n_in:n_in + n_out]
        send, recv = refs[n_in + n_out:]
        pos = (lax.axis_index("x"), lax.axis_index("y"), lax.axis_index("c"))
        k = 0
        for phase in phases:
            cps = []
            for src, dst, target in phase:
                cps.append(_make_copy(src(ins, outs, pos), dst(ins, outs, pos), send.at[k], recv.at[k],
                                      target and target(pos)))
                k += 1
            for cp in cps:
                cp.start()
            for cp in cps:
                cp.wait()

    anyspec = pl.BlockSpec(memory_space=pl.ANY)
    return _call(
        body, name=name,
        in_specs=[anyspec] * n_in, out_specs=[anyspec] * n_out, out_shape=list(out_shapes),
        scratch_shapes=[pltpu.SemaphoreType.DMA((n_cp,)), pltpu.SemaphoreType.DMA((n_cp,))],
    )(*inputs)


def _exchange_start(name, inputs, out_shapes, copies):
    n_in, n_out, n_cp = len(inputs), len(out_shapes), len(copies)

    def body(*refs):
        ins, lands = refs[:n_in], refs[n_in:n_in + n_out]
        sems = refs[n_in + n_out:n_in + n_out + 2 * n_cp]
        token = refs[-1]
        pos = (lax.axis_index("x"), lax.axis_index("y"), lax.axis_index("c"))
        for k, (src, dst, target) in enumerate(copies):
            _make_copy(src(ins, lands, pos), dst(ins, lands, pos), sems[2 * k], sems[2 * k + 1],
                       target and target(pos)).start()
        token[...] = jnp.zeros_like(token)

    hbm = pl.BlockSpec(memory_space=pltpu.HBM)
    sem = pl.BlockSpec(memory_space=pltpu.SEMAPHORE)
    bufs = list(inputs) + [lax.empty(o.shape, o.dtype) for o in out_shapes]
    outs = _call(
        body, name=name,
        out_shape=tuple([pltpu.SemaphoreType.DMA(())] * (2 * n_cp) + [pltpu.HBM(b.shape, b.dtype) for b in bufs]
                        + [_sds((8, HEAD))]),
        in_specs=[hbm] * len(bufs),
        out_specs=tuple([sem] * (2 * n_cp) + [hbm] * len(bufs) + [pl.BlockSpec(memory_space=pltpu.VMEM)]),
        input_output_aliases={i: 2 * n_cp + i for i in range(len(bufs))},
        compiler_params=pltpu.CompilerParams(has_side_effects=pltpu.SideEffectType.DATAFLOW_SIDE_EFFECTING),
    )(*[pltpu.with_memory_space_constraint(b, pltpu.HBM) for b in bufs])
    return outs[:2 * n_cp], outs[2 * n_cp:2 * n_cp + n_in], outs[2 * n_cp + n_in:-1], outs[-1]


def _exchange_wait(name, sems, sources, lands, copies, after):
    n_in, n_out, n_cp = len(sources), len(lands), len(copies)

    def body(*refs):
        ins, zones = refs[:n_in], refs[n_in:n_in + n_out]
        sem_refs = refs[n_in + n_out:n_in + n_out + 2 * n_cp]
        pos = (lax.axis_index("x"), lax.axis_index("y"), lax.axis_index("c"))
        for k, (src, dst, target) in enumerate(copies):
            cp = _make_copy(src(ins, zones, pos), dst(ins, zones, pos), sem_refs[2 * k], sem_refs[2 * k + 1],
                            target and target(pos))
            if target is None:
                cp.wait()
            else:
                cp.wait_send()
                cp.wait_recv()

    hbm = pl.BlockSpec(memory_space=pltpu.HBM)
    sem = pl.BlockSpec(memory_space=pltpu.SEMAPHORE)
    bufs = list(sources) + list(lands)
    outs = _call(
        body, name=name,
        out_shape=tuple(pltpu.HBM(b.shape, b.dtype) for b in bufs),
        in_specs=[hbm] * len(bufs) + [sem] * (2 * n_cp) + [pl.BlockSpec(memory_space=pl.ANY)],
        out_specs=tuple([hbm] * len(bufs)),
        input_output_aliases={i: i for i in range(len(bufs))},
        compiler_params=pltpu.CompilerParams(has_side_effects=pltpu.SideEffectType.DATAFLOW_SIDE_EFFECTING),
    )(*bufs, *sems, after)
    return outs[:n_in], outs[n_in:]


def _allreduce_tile(name, v):
    def body(v_ref, out_ref, slots, send, recv):
        x, y, c = lax.axis_index("x"), lax.axis_index("y"), lax.axis_index("c")
        me = 4 * x + 2 * y + c
        slots[me] = v_ref[...]
        cps = []
        for k in range(1, 8):
            peer = (x ^ (k >> 2), y ^ ((k >> 1) & 1), c ^ (k & 1))
            cps.append(pltpu.make_async_remote_copy(
                src_ref=v_ref, dst_ref=slots.at[me], send_sem=send.at[k - 1], recv_sem=recv.at[k - 1],
                device_id=peer, device_id_type=pl.DeviceIdType.MESH))
        for cp in cps:
            cp.start()
        for cp in cps:
            cp.wait()
        acc = slots[0]
        for i in range(1, 8):
            acc = acc + slots[i]
        out_ref[...] = acc

    vm = pl.BlockSpec(memory_space=pltpu.VMEM)
    return _call(
        body, name=name, in_specs=[vm], out_specs=vm, out_shape=_sds(v.shape),
        scratch_shapes=[pltpu.VMEM((8,) + v.shape, F32), pltpu.SemaphoreType.DMA((7,)), pltpu.SemaphoreType.DMA((7,))],
    )(v)


def _chip(pos):
    return 2 * pos[0] + pos[1]


def _other_chip(pos, mask):
    x, y, c = pos
    return (x ^ (mask >> 1), y ^ (mask & 1), c)


def _sibling(pos):
    return (pos[0], pos[1], 1 - pos[2])


def _gather_weights(wb, cb):
    rows = wb.shape[0] // 2
    x_nb, y_nb, diag = CHIP_MASKS

    def part(pos, mask, quarter=None):
        start = pos[2] * rows if quarter is None else pos[2] * rows + quarter * (rows // 2)
        return lambda outs: outs[0].at[_chip(pos) ^ mask, pl.ds(start, rows if quarter is None else rows // 2)]

    def passed_on(mask, to, quarter=None):
        return (lambda ins, outs, pos: part(pos, mask, quarter)(outs), lambda ins, outs, pos: part(pos, mask, quarter)(outs), to)

    first = [(lambda ins, outs, pos: ins[0].at[pl.ds(pos[2] * rows, rows)], lambda ins, outs, pos: part(pos, 0)(outs),
              functools.partial(_other_chip, mask=mask)) for mask in (x_nb, y_nb)]
    conv_cols = lambda ins, outs, pos: outs[1].at[:, pl.ds(pl.multiple_of(_chip(pos) * cb.shape[1], HEAD), cb.shape[1])]
    first += [(lambda ins, outs, pos: ins[1], conv_cols, functools.partial(_other_chip, mask=mask)) for mask in CHIP_MASKS]
    first += [(lambda ins, outs, pos: ins[1], conv_cols, None)]
    second = [passed_on(x_nb, functools.partial(_other_chip, mask=y_nb), quarter=0),
              passed_on(y_nb, functools.partial(_other_chip, mask=x_nb), quarter=1),
              passed_on(x_nb, _sibling), passed_on(y_nb, _sibling)]
    third = [passed_on(diag, _sibling)]
    return _exchange("gather_weights", [wb, cb],
                     [_sds((4,) + wb.shape, wb.dtype), _sds((cb.shape[0], 4 * cb.shape[1]), cb.dtype)], [first, second, third])


def _assemble_w_in(gw, wb, jidx):
    m = gw.shape[1]

    def body(j_ref, g_ref, wb_ref, o_ref):
        step = pl.program_id(0)

        @pl.when(step == 0)
        def _():
            o_ref[...] = jnp.zeros_like(o_ref)

        blk = jnp.where(step == j_ref[0], wb_ref[...], g_ref[0]).astype(F32)
        lane = lax.broadcasted_iota(I32, (m, BLK_IN_PAD), 1)
        for j in range(4):
            @pl.when(step == j)
            def _(j=j):
                base = j * BLK_IN // HEAD * HEAD
                shift = j * BLK_IN - base
                moved = pltpu.roll(blk, shift, 1) if shift else blk
                window = o_ref[:, base:base + BLK_IN_PAD].astype(F32)
                mine = (lane >= shift) & (lane < shift + BLK_IN)
                o_ref[:, base:base + BLK_IN_PAD] = jnp.where(mine, moved, window).astype(BF16)

    return _call(
        body, name="assemble_w_in",
        grid_spec=pltpu.PrefetchScalarGridSpec(
            num_scalar_prefetch=1, grid=(4,),
            in_specs=[pl.BlockSpec((1, m, BLK_IN_PAD), lambda j, j_ref: (j, 0, 0)),
                      pl.BlockSpec((m, BLK_IN_PAD), lambda j, j_ref: (0, 0))],
            out_specs=pl.BlockSpec((m, N_IN_PAD), lambda j, j_ref: (0, 0))),
        out_shape=_sds((m, N_IN_PAD), BF16),
        compiler_params=_params("arbitrary"),
    )(jidx, gw, wb)


def _gather_blocks(ob):
    copies = [(lambda ins, outs, pos: ins[0], lambda ins, outs, pos: outs[0].at[_chip(pos)],
               functools.partial(_other_chip, mask=mask)) for mask in CHIP_MASKS]
    copies.append((lambda ins, outs, pos: ins[0], lambda ins, outs, pos: outs[0].at[_chip(pos)], None))
    return [_sds((4,) + ob.shape, ob.dtype)], copies


def _reduce_sibling(name, arrays, cidx):
    n = len(arrays)
    halves = [a.shape[:-2] + (a.shape[-2] // 2, a.shape[-1]) for a in arrays]
    pieces = [(a, j) for a in range(n) for j in (range(arrays[a].shape[0]) if arrays[a].ndim == 3 else [None])]

    def body(c_ref, *refs):
        del c_ref
        whole, own, outs, land = refs[:n], refs[n:2 * n], refs[2 * n:3 * n], refs[3 * n:4 * n]
        send, recv = refs[4 * n:]
        pos = (lax.axis_index("x"), lax.axis_index("y"), lax.axis_index("c"))
        cps = []
        for k, (a, j) in enumerate(pieces):
            rows = pl.ds((1 - pos[2]) * halves[a][-2], halves[a][-2])
            cps.append(pltpu.make_async_remote_copy(
                src_ref=whole[a].at[rows] if j is None else whole[a].at[j, rows],
                dst_ref=land[a] if j is None else land[a].at[j],
                send_sem=send.at[k], recv_sem=recv.at[k],
                device_id=_sibling(pos), device_id_type=pl.DeviceIdType.MESH))
        for cp in cps:
            cp.start()
        for cp, (a, j) in zip(cps, pieces):
            cp.wait()
            at = Ellipsis if j is None else j
            outs[a][at] = (own[a][at].astype(F32) + land[a][at].astype(F32)).astype(outs[a].dtype)

    def mine(shape):
        if len(shape) == 3:
            return pl.BlockSpec(shape, lambda i, c_ref: (0, c_ref[0], 0))
        return pl.BlockSpec(shape, lambda i, c_ref: (c_ref[0], 0))

    return _call(
        body, name=name,
        grid_spec=pltpu.PrefetchScalarGridSpec(
            num_scalar_prefetch=1, grid=(1,),
            in_specs=[pl.BlockSpec(memory_space=pl.ANY)] * n + [mine(h) for h in halves],
            out_specs=[pl.BlockSpec(h, lambda i, c_ref, nd=len(h): (0,) * nd) for h in halves],
            scratch_shapes=[pltpu.VMEM(h, a.dtype) for h, a in zip(halves, arrays)]
                           + [pltpu.SemaphoreType.DMA((len(pieces),)), pltpu.SemaphoreType.DMA((len(pieces),))]),
        out_shape=[_sds(h, a.dtype) for h, a in zip(halves, arrays)],
        compiler_params=_params("arbitrary"),
    )(cidx, *arrays, *arrays)


def _to_other_chips(arrays, blocked):
    def src(ins, outs, pos, a, mask):
        return ins[a].at[_chip(pos) ^ mask] if blocked[a] else ins[a]

    outs = [_sds((3,) + (a.shape[1:] if b else a.shape), a.dtype) for a, b in zip(arrays, blocked)]
    copies = []
    for mi, mask in enumerate(CHIP_MASKS):
        for a in range(len(arrays)):
            copies.append((functools.partial(src, a=a, mask=mask), lambda ins, outs, pos, a=a, mi=mi: outs[a].at[mi],
                           functools.partial(_other_chip, mask=mask)))
    return outs, copies


def _sum_chips_swap(name, owns, gots, jidx, blocked):
    n = len(owns)
    shapes = [g.shape[-2:] for g in gots]
    own3 = [o if b else o.reshape((1,) + o.shape) for o, b in zip(owns, blocked)]

    def body(j_ref, *refs):
        del j_ref
        own, got, mine, theirs = refs[:n], refs[n:2 * n], refs[2 * n:3 * n], refs[3 * n:4 * n]
        send, recv = refs[4 * n:]
        pos = (lax.axis_index("x"), lax.axis_index("y"), lax.axis_index("c"))
        cps = []
        for a in range(n):
            mine[a][...] = ((own[a][0].astype(F32) + got[a][0].astype(F32))
                            + (got[a][1].astype(F32) + got[a][2].astype(F32)))
            cps.append(pltpu.make_async_remote_copy(
                src_ref=mine[a], dst_ref=theirs[a], send_sem=send.at[a], recv_sem=recv.at[a],
                device_id=_sibling(pos), device_id_type=pl.DeviceIdType.MESH))
            cps[-1].start()
        for cp in cps:
            cp.wait()

    own_spec = lambda s, b: pl.BlockSpec((1,) + s, (lambda i, j_ref: (j_ref[0], 0, 0)) if b else (lambda i, j_ref: (0, 0, 0)))
    whole = lambda s: pl.BlockSpec(s, lambda i, j_ref: (0, 0))
    outs = _call(
        body, name=name,
        grid_spec=pltpu.PrefetchScalarGridSpec(
            num_scalar_prefetch=1, grid=(1,),
            in_specs=[own_spec(s, b) for s, b in zip(shapes, blocked)]
                     + [pl.BlockSpec((3,) + s, lambda i, j_ref: (0, 0, 0)) for s in shapes],
            out_specs=[whole(s) for s in shapes] * 2,
            scratch_shapes=[pltpu.SemaphoreType.DMA((n,)), pltpu.SemaphoreType.DMA((n,))]),
        out_shape=[_sds(s) for s in shapes] * 2,
        compiler_params=_params("arbitrary"),
    )(jidx, *own3, *gots)
    return outs[:n], outs[n:]


def _local_step(x, target, w_pad, w_out, conv_w, norm_w, pool_w, pool_scale, a_log, dt_bias, dn_norm_w, final_norm_w,
                after):
    proj, n_t, qn, kn, vs, beta, g, yq, yk, yv, y_pool = _front_fwd(
        x, norm_w, w_pad, conv_w, a_log, dt_bias, pool_w, pool_scale, after)
    w, att, qd, kd, tm, cd, o, vn, st = _delta_fwd(qn, kn, vs, beta, g)
    w_out = w_out(o) if callable(w_out) else w_out
    g_wout, dh, dyp, do, ddz, loss, g_fnw, g_dnw = _out_fwd_bwd(x, y_pool, o, proj, target, w_out, dn_norm_w, final_norm_w)
    dqn, dkn, dvs, dbeta, dg = _delta_bwd(do, vn, qd, kd, w, att, cd, st, qn, kn, vs, beta, g, tm)
    (dcq, dck, dcv, dba, g_cw, g_sm), (dpu, dpz, g_pw, g_ps) = _conv_pool_bwd(
        proj, (yq, yk, yv), conv_w, a_log, dt_bias, dqn, dkn, dvs, dbeta, dg, dyp, pool_w, pool_scale)
    pieces = [dpu, dpz, dcq, dck, dcv, ddz, dba]
    g_win = _grad_w_in(n_t, pieces)
    small = dict(norm_w=jnp.zeros_like(norm_w), pool_w=g_pw, pool_scale=g_ps, conv_w=g_cw[:CONV_K],
                 a_log=g_sm[0:1, 0:N_HEADS], dt_bias=g_sm[0:1, N_HEADS:2 * N_HEADS], dn_norm_w=g_dnw, final_norm_w=g_fnw)
    return loss[0, 0], g_win, g_wout, small, dh, pieces


SMALL_LAYOUT = (("pool_w", 512, HEAD, (1, N_HEADS, HEAD, HEAD)), ("final_norm_w", 8, HEAD, (D_MODEL,)),
                ("pool_scale", 4, HEAD, (1, D_HALF)), ("conv_w", 48, HEAD, (1, CONV_K, 3 * D_HALF)),
                ("dn_norm_w", 1, HEAD, (1, HEAD)), ("a_log", 1, N_HEADS, (1, N_HEADS)), ("dt_bias", 1, N_HEADS, (1, N_HEADS)),
                ("loss", 1, 1, ()))


def _small_offsets():
    offs, r = {}, 0
    for name, rows, _, _ in SMALL_LAYOUT:
        offs[name] = r
        r += -(-rows // 8) * 8
    assert r <= SMALL_ROWS
    return offs


def _pack_small(t):
    parts = []
    for name, rows, lanes, _ in SMALL_LAYOUT:
        a = t.get(name, jnp.zeros((1,), F32)).reshape(rows, lanes)
        parts.append(jnp.pad(a, ((0, -(-rows // 8) * 8 - rows), (0, HEAD - lanes))))
    buf = jnp.concatenate(parts, axis=0)
    return jnp.pad(buf, ((0, SMALL_ROWS - buf.shape[0]), (0, 0)))


def _adamw_small(w, g_own, g_got, cidx, m, v):
    offs = _small_offsets()
    names = [e[0] for e in SMALL_LAYOUT]
    n = len(names)

    def body(c_ref, w_ref, go_ref, gg_ref, m_ref, v_ref, *outs):
        own_low = c_ref[0] == 0
        gv = jnp.concatenate([jnp.where(own_low, go_ref[...], gg_ref[...]), jnp.where(own_low, gg_ref[...], go_ref[...])], axis=0)
        mn = ADAM_B1 * m_ref[...] + (1.0 - ADAM_B1) * gv
        vn = ADAM_B2 * v_ref[...] + (1.0 - ADAM_B2) * (gv * gv)
        m_hat = mn / (1.0 - ADAM_B1 ** ADAM_STEP)
        v_hat = vn / (1.0 - ADAM_B2 ** ADAM_STEP)
        dl = -ADAM_LR * (m_hat / (jnp.sqrt(v_hat) + ADAM_EPS) + ADAM_WD * w_ref[...])
        for kind, arr in enumerate((gv, dl, mn, vn)):
            for i, (name, rows, lanes, _) in enumerate(SMALL_LAYOUT):
                outs[kind * n + i][...] = arr[offs[name]:offs[name] + rows, :lanes]

    whole = lambda shape: pl.BlockSpec(shape, lambda i, c_ref: (0,) * len(shape))
    out_shapes = [_sds((rows, lanes)) for _, rows, lanes, _ in SMALL_LAYOUT] * 4
    res = _call(
        body, name="adamw_small",
        grid_spec=pltpu.PrefetchScalarGridSpec(
            num_scalar_prefetch=1, grid=(1,),
            in_specs=[whole(w.shape), whole(g_own.shape), whole(g_got.shape), whole(m.shape), whole(v.shape)],
            out_specs=[whole(o.shape) for o in out_shapes]),
        out_shape=out_shapes,
        compiler_params=_params("arbitrary"),
    )(cidx, w, g_own, g_got, m, v)
    return [{name: res[kind * n + i].reshape(shape) for i, (name, _, _, shape) in enumerate(SMALL_LAYOUT)}
            for kind in range(4)]


def kernel(x, norm_w, w_in, pool_w, pool_scale, conv_w, a_log, dt_bias, dn_norm_w, w_out, final_norm_w, loss_target, m_norm_w, m_w_in, m_pool_w, m_pool_scale, m_conv_w, m_a_log, m_dt_bias, m_dn_norm_w, m_w_out, m_final_norm_w, v_norm_w, v_w_in, v_pool_w, v_pool_scale, v_conv_w, v_a_log, v_dt_bias, v_dn_norm_w, v_w_out, v_final_norm_w):
    cidx = lax.axis_index("c").astype(I32).reshape(1)
    jidx = (2 * lax.axis_index("x") + lax.axis_index("y")).astype(I32)

    wb = jnp.pad(w_in[0].astype(BF16), ((0, 0), (0, BLK_IN_PAD - BLK_IN)))
    ob = w_out[0].astype(BF16)
    gw, cw_full = _gather_weights(wb, conv_w[0])
    w_pad = _assemble_w_in(gw, wb, jidx.reshape(1))

    lands_o, copies_o = _gather_blocks(ob)
    sems_o, ob_thru, zones_o, token_o = _exchange_start("gather_w_out_start", [ob], lands_o, copies_o)

    def w_out_full(after):
        _, (got,) = _exchange_wait("gather_w_out_wait", sems_o, ob_thru, zones_o, copies_o, after)
        return got.reshape(D_MODEL, D_MODEL)

    loss, g_win, g_wout, small, dh, pieces = _local_step(
        x[0], loss_target[0], w_pad, w_out_full, cw_full, norm_w, pool_w[0], pool_scale, a_log, dt_bias,
        dn_norm_w, final_norm_w.reshape(1, D_MODEL), token_o)
    small["loss"] = loss

    blocks_out = g_wout.reshape(4, BLK_OUT, D_MODEL)
    full = [g_win, blocks_out, _pack_small(small)]
    chip_sum = _reduce_sibling("reduce_sibling", full, cidx)
    blocked = [True, True, False]
    lands, copies = _to_other_chips(chip_sum, blocked)
    sems, chip_sum, zones, token = _exchange_start("reduce_chips_start", chip_sum, lands, copies)
    gx, g_nw = _in_bwd(x[0], dh, norm_w, w_pad, pieces, token)
    g_nw = _allreduce_tile("reduce_norm_w", g_nw.reshape(8, HEAD)).reshape(1, D_MODEL)
    chip_sum, from_chips = _exchange_wait("reduce_chips_wait", sems, chip_sum, zones, copies, gx)
    halves, other_halves = _sum_chips_swap("sum_chips_swap", chip_sum, from_chips, jidx.reshape(1), blocked)

    weights = dict(norm_w=norm_w, w_in=w_in, pool_w=pool_w, pool_scale=pool_scale, conv_w=conv_w, a_log=a_log,
                   dt_bias=dt_bias, dn_norm_w=dn_norm_w, w_out=w_out, final_norm_w=final_norm_w)
    ms = dict(norm_w=m_norm_w, w_in=m_w_in, pool_w=m_pool_w, pool_scale=m_pool_scale, conv_w=m_conv_w, a_log=m_a_log,
              dt_bias=m_dt_bias, dn_norm_w=m_dn_norm_w, w_out=m_w_out, final_norm_w=m_final_norm_w)
    vs = dict(norm_w=v_norm_w, w_in=v_w_in, pool_w=v_pool_w, pool_scale=v_pool_scale, conv_w=v_conv_w, a_log=v_a_log,
              dt_bias=v_dt_bias, dn_norm_w=v_dn_norm_w, w_out=v_w_out, final_norm_w=v_final_norm_w)
    names = ["norm_w", "w_in", "pool_w", "pool_scale", "conv_w", "a_log", "dt_bias", "dn_norm_w", "w_out", "final_norm_w"]
    small_names = [n for n in names if n not in ("w_in", "w_out")]

    def pack(t):
        conv = lax.dynamic_update_slice_in_dim(jnp.zeros((CONV_K, 3 * D_HALF), F32), t["conv_w"][0], jidx * BLK_CONV, axis=1)
        return _pack_small({**{n: t[n] for n in small_names if n != "conv_w"}, "conv_w": conv})

    results = [{}, {}, {}, {}]
    to_tiles = lambda a: jnp.transpose(a, (2, 0, 1)).reshape(BLK_IN, 8, HEAD)
    from_tiles = lambda a: jnp.transpose(a, (1, 2, 0)).reshape(1, D_MODEL, BLK_IN)
    lo = jnp.where(cidx[0] == 0, halves[0], other_halves[0])
    hi = jnp.where(cidx[0] == 0, other_halves[0], halves[0])
    g_tiles = jnp.concatenate([lo[:, :BLK_IN].T, hi[:, :BLK_IN].T], axis=1).reshape(BLK_IN, 8, HEAD)
    outs = _adamw_tiles("adamw_w_in", to_tiles(w_in), g_tiles, to_tiles(m_w_in), to_tiles(v_w_in))
    for res, o in zip(results, (g_tiles,) + tuple(outs)):
        res["w_in"] = from_tiles(o)
    outs = _adamw_shard("adamw_w_out", w_out, halves[1], other_halves[1], cidx, m_w_out, v_w_out)
    for res, o in zip(results, outs):
        res["w_out"] = o
    outs = _adamw_small(pack(weights), halves[2], other_halves[2], cidx, pack(ms), pack(vs))
    for res, got in zip(results, outs):
        got["conv_w"] = lax.dynamic_slice_in_dim(got["conv_w"], jidx * BLK_CONV, BLK_CONV, axis=2)
        res.update(got)
    one_tile = lambda a: a.reshape(1, 8, HEAD)
    outs = _adamw_tiles("adamw_norm_w", one_tile(norm_w), one_tile(g_nw), one_tile(m_norm_w), one_tile(v_norm_w))
    for res, o in zip(results, (g_nw,) + tuple(outs)):
        res["norm_w"] = o.reshape(1, D_MODEL)
    grads, delta, new_m, new_v = results

    return (grads["loss"], gx[None], *[grads[n] for n in names], *[delta[n] for n in names],
            *[new_m[n] for n in names], *[new_v[n] for n in names])
```

```python
import functools

import jax
import jax.numpy as jnp
import numpy as np
from jax import lax
from jax.experimental import pallas as pl
from jax.experimental.pallas import tpu as pltpu

F32 = jnp.float32
BF16 = jnp.bfloat16
I32 = jnp.int32

D_MODEL = 1024
D_HALF = 512
N_HEADS = 4
HEAD = 128
CHUNK = 64
PAIR = 2 * CHUNK
WINDOWS = (2, 4, 8, 16)
CONV_K = 4
EPS = 1e-6
N_IN = 3080
N_IN_PAD = 3200
BLK_IN = 770
BLK_IN_PAD = 896
BLK_OUT = 256
BLK_CONV = 384
COL_BA = 3072
QK_SCALE = HEAD ** -0.5
SMALL_ROWS = 608
VMEM_LIMIT = 56 * 1024 * 1024

ADAM_LR = 0.001
ADAM_B1 = 0.9
ADAM_B2 = 0.999
ADAM_EPS = 1e-08
ADAM_WD = 0.01
ADAM_STEP = 10

CHIP_MASKS = (2, 1, 3)
HEADS = range(N_HEADS)
HEAD_COLS = [slice(h * HEAD, (h + 1) * HEAD) for h in HEADS]


def _call(body, **kw):
    return pl.pallas_call(body, **kw)


def _params(*sem):
    return pltpu.CompilerParams(dimension_semantics=sem, vmem_limit_bytes=VMEM_LIMIT)


def _sds(shape, dtype=F32):
    return jax.ShapeDtypeStruct(shape, dtype)


def _bdot(a, b):
    return jnp.dot(a.astype(BF16), b.astype(BF16), preferred_element_type=F32)


def _bdot_nt(a, b):
    return lax.dot_general(a.astype(BF16), b.astype(BF16), (((1,), (1,)), ((), ())), preferred_element_type=F32)


def _bdot_tn(a, b):
    return lax.dot_general(a.astype(BF16), b.astype(BF16), (((0,), (0,)), ((), ())), preferred_element_type=F32)


def _side(a, b):
    return jnp.concatenate([a.astype(BF16), b.astype(BF16)], axis=1)


def _stack(a, b):
    return jnp.concatenate([a.astype(BF16), b.astype(BF16)], axis=0)


def _split(a):
    hi = a.astype(BF16)
    lo = (a - hi.astype(F32)).astype(BF16)
    return hi, lo


def _mask_dot(m, b):
    n = b.shape[1]
    both = jnp.dot(m, jnp.concatenate(_split(b), axis=1), preferred_element_type=F32)
    return both[:, :n] + both[:, n:]


def _sigmoid(x):
    return 0.5 * jnp.tanh(0.5 * x) + 0.5


def _softplus(x):
    return jnp.maximum(x, 0.0) + jnp.log(1.0 + jnp.exp(-jnp.abs(x)))


def _rowsum(x):
    return jnp.sum(x, axis=-1, keepdims=True)


def _colsum(x):
    return jnp.sum(x, axis=0, keepdims=True)


def _shift_down(xv, prev8, k):
    r = pltpu.roll(xv, k, 0)
    q = pltpu.roll(prev8, k, 0)
    row = lax.broadcasted_iota(I32, prev8.shape, 0)
    top = jnp.where(row < k, q, r[0:8])
    return jnp.concatenate([top, r[8:]], axis=0)


def _shift_up(xv, next8, k):
    t = xv.shape[0]
    r = pltpu.roll(xv, t - k, 0)
    q = pltpu.roll(next8, 8 - k, 0)
    row = lax.broadcasted_iota(I32, next8.shape, 0)
    bot = jnp.where(row >= 8 - k, q, r[t - 8:])
    return jnp.concatenate([r[:t - 8], bot], axis=0)


INTRA_PAIRS = 2
UNITS = [(pp, h) for pp in range(INTRA_PAIRS) for h in HEADS]


def _heads_of(ref, rows=PAIR, units=UNITS):
    return [ref[pp * rows:(pp + 1) * rows, HEAD_COLS[h]] for pp, h in units]


def _put_heads_of(ref, vals, rows=PAIR, units=UNITS):
    for (pp, h), v in zip(units, vals):
        ref[pp * rows:(pp + 1) * rows, HEAD_COLS[h]] = v.astype(ref.dtype)


_heads = _heads_of
_put_heads = _put_heads_of


def _each(fn, *lists):
    return [fn(*args) for args in zip(*lists)]


def _pool_bands(t, anti=False):
    r = np.arange(t)[:, None]
    c = np.arange(t + HEAD)[None, :]
    d = (c - r) if anti else (r - c + HEAD)
    return jnp.asarray(np.stack([(d >= 0) & (d < w) for w in WINDOWS]), BF16)


def _pool_mix(u, halo, z, pw, bands, row0):
    t = u[0].shape[0]
    rows = row0 + lax.broadcasted_iota(I32, (t, 1), 0) + 1
    cnt = [jnp.minimum(rows, w).astype(F32) for w in WINDOWS]
    win = _each(lambda b, h, v: _mask_dot(b, jnp.concatenate([h, v], axis=0)), bands, halo, u)
    mix = _each(lambda a, c, v: a / c - v, win, cnt, u)
    mixed = _each(_bdot, mix, pw)
    return mix, mixed, _each(_sigmoid, z), cnt


POOL_T = 256


def _conv_taps(xv, prev8):
    return [_shift_down(xv, prev8, CONV_K - 1 - j) for j in range(CONV_K - 1)] + [xv]


def _conv_pre(taps, cw):
    y = taps[CONV_K - 1] * cw[CONV_K - 1:CONV_K]
    for j in range(CONV_K - 2, -1, -1):
        y = y + taps[j] * cw[j:j + 1]
    return y


CONV_T = 256


def _conv_specs(t, tile_of=lambda i: i):
    tiles = [pl.BlockSpec((t, D_HALF), functools.partial(lambda i, p: (tile_of(i), 2 + p), p=p)) for p in range(3)]
    halos = [pl.BlockSpec((8, D_HALF),
                          functools.partial(lambda i, p: (jnp.maximum(tile_of(i) * (t // 8) - 1, 0), 2 + p), p=p))
             for p in range(3)]
    return tiles + halos


def _pair_masks():
    r = lax.broadcasted_iota(I32, (PAIR, PAIR), 0)
    c = lax.broadcasted_iota(I32, (PAIR, PAIR), 1)
    same = jnp.right_shift(r, 6) == jnp.right_shift(c, 6)
    return same, same & (r >= c), same & (r > c), r == c


def _interleave(*stage_lists):
    live = list(stage_lists)
    while live:
        for gen in list(live):
            try:
                next(gen)
            except StopIteration:
                live.remove(gen)


def _pipelined_step(t, n, leading, trailing):
    @pl.when(t == 0)
    def _():
        _interleave(*leading())

    @pl.when(jnp.logical_and(t > 0, t < n))
    def _():
        _interleave(*leading(), *trailing())

    @pl.when(t == n)
    def _():
        _interleave(*trailing())


def _pair_common_stages(cm, qn, kn, vs, beta, g):
    same, incl, strict, eye = _pair_masks()
    incl_b = incl.astype(BF16)
    first = lax.broadcasted_iota(I32, (PAIR, HEAD), 0) < CHUNK
    cm.update(same=same, incl=incl, strict=strict, eye=eye)
    gc = _each(lambda gv: _mask_dot(incl_b, gv), g)
    q = _each(lambda v: v * QK_SCALE, qn)
    kb = _each(lambda k, b: k * b, kn, beta)
    cm.update(gc=gc, q=q, kb=kb, vb=_each(lambda v, b: v * b, vs, beta))
    yield
    both = _each(lambda a, b, c: _bdot_nt(_stack(a, b), c), kb, q, kn)
    cm.update(kk=[v[:PAIR] for v in both], qk=[v[PAIR:] for v in both])
    gc_row = _each(lambda v: _colsum(jnp.where(eye, v, 0.0)), gc)
    gl = _each(lambda v: jnp.where(first, v[CHUNK - 1:CHUNK], v[PAIR - 1:PAIR]), gc)
    egc = _each(jnp.exp, gc)
    cm.update(gl=gl, egc=egc,
              decay=_each(lambda v, r: jnp.where(incl, jnp.exp(jnp.where(incl, v - r, 0.0)), 0.0), gc, gc_row))
    yield
    cm.update(ekd=_each(lambda a, b: jnp.exp(a - b), gl, gc), cd=_each(jnp.exp, gl),
              kbg=_each(lambda k, e: k * e, kb, egc))
    yield


def _tri_inv_stages(out, a, eye_f):
    p = _each(lambda v: eye_f - v, a)
    x = _each(_bdot, a, a)
    yield
    for it in range(4):
        both = _each(lambda xv, pv: _bdot(xv, _side(pv, xv)), x, p)
        p = _each(lambda pv, b: pv + b[:, :PAIR], p, both)
        x = [b[:, PAIR:] for b in both]
        yield
    out["t"] = _each(lambda pv, xv: pv + _bdot(pv, xv), p, x)
    yield


def _chunk_scalar_spec(pairs=1, index=lambda i: (i, 0)):
    return pl.BlockSpec((16 * pairs, D_HALF), index)


SCAN_PAIRS = 2
SCAN_ROWS = SCAN_PAIRS * PAIR


def _delta_fwd(qn, kn, vs, beta, g):
    s = qn.shape[0]
    n_steps = s // SCAN_ROWS
    n_chunks = s // CHUNK
    assert INTRA_PAIRS == SCAN_PAIRS

    def body(qn_ref, kn_ref, vs_ref, beta_ref, g_ref, w_ref, att_ref, qd_ref, kd_ref, t_ref, cd_ref, o_ref, vn_ref, st_ref,
             state, u_s, w_s, att_s, qd_s, kd_s, cd_s):
        t = pl.program_id(0)

        @pl.when(t == 1)
        def _():
            state[...] = jnp.zeros_like(state)

        cur = lax.rem(t, 2)
        prev = 1 - cur
        cols = list(enumerate(HEAD_COLS))

        def recurrence():
            sm = [state[h] for h in HEADS]
            for ci in range(2 * SCAN_PAIRS):
                rs = slice(ci * CHUNK, (ci + 1) * CHUNK)
                for h in HEADS:
                    st_ref[ci, h] = sm[h]
                both = [_bdot(jnp.concatenate([w_s[prev, rs, sl], qd_s[prev, rs, sl]], axis=0), sm[h]) for h, sl in cols]
                vn = [u_s[prev, rs, sl] - both[h][:CHUNK] for h, sl in cols]
                for h, sl in cols:
                    vn_ref[rs, sl] = vn[h].astype(BF16)
                    o_ref[rs, sl] = both[h][CHUNK:]
                yield
                sm = [sm[h] * cd_s[prev, ci * 8:ci * 8 + 1, sl] + _bdot_tn(kd_s[prev, rs, sl], vn[h]) for h, sl in cols]
                yield
            for h in HEADS:
                state[h] = sm[h]
            for pp in range(SCAN_PAIRS):
                rp = slice(pp * PAIR, (pp + 1) * PAIR)
                intra = [_bdot(att_s[prev, rp, sl], vn_ref[rp, sl]) for sl in HEAD_COLS]
                for h, sl in cols:
                    o_ref[rp, sl] += intra[h]
                yield

        def factors():
            kn = _heads(kn_ref)
            cm = {}
            yield from _pair_common_stages(cm, _heads(qn_ref), kn, _heads(vs_ref), _heads(beta_ref), _heads(g_ref))
            a = _each(lambda kk, d: jnp.where(cm["strict"], kk * d, 0.0), cm["kk"], cm["decay"])
            inv = {}
            yield from _tri_inv_stages(inv, a, cm["eye"].astype(F32))
            tm = inv["t"]
            uw = _each(lambda tv, a, b: _bdot(tv, _side(a, b)), tm, cm["vb"], cm["kbg"])
            res = dict(u=[v[:, :HEAD] for v in uw], w=[v[:, HEAD:] for v in uw],
                       att=_each(lambda a, b: a * b, cm["qk"], cm["decay"]),
                       qd=_each(lambda a, b: a * b, cm["q"], cm["egc"]), kd=_each(lambda a, b: a * b, kn, cm["ekd"]))
            yield
            _put_heads(t_ref, tm)
            for key, out, keep in (("w", w_ref, w_s), ("att", att_ref, att_s), ("qd", qd_ref, qd_s), ("kd", kd_ref, kd_s)):
                _put_heads(out, res[key])
                for (pp, h), v in zip(UNITS, res[key]):
                    keep[cur, pp * PAIR:(pp + 1) * PAIR, HEAD_COLS[h]] = v.astype(BF16)
            for (pp, h), v in zip(UNITS, res["u"]):
                u_s[cur, pp * PAIR:(pp + 1) * PAIR, HEAD_COLS[h]] = v
            for ci in range(2):
                for (pp, h), v in zip(UNITS, cm["cd"]):
                    rows8 = slice(pp * 16 + ci * 8, pp * 16 + (ci + 1) * 8)
                    cd_ref[rows8, HEAD_COLS[h]] = v[ci * CHUNK:ci * CHUNK + 8]
                    cd_s[cur, rows8, HEAD_COLS[h]] = v[ci * CHUNK:ci * CHUNK + 8]
            yield

        _pipelined_step(t, n_steps, lambda: [factors()], lambda: [recurrence()])

    last = n_steps - 1
    now = lambda i: (jnp.minimum(i, last), 0)
    before = lambda i: (jnp.maximum(i - 1, 0), 0)
    rows = lambda index: pl.BlockSpec((SCAN_ROWS, D_HALF), index)
    slot = lambda r, dtype: pltpu.VMEM((2, r, D_HALF), dtype)
    return _call(
        body, name="delta_fwd", grid=(n_steps + 1,),
        in_specs=[rows(now)] * 5,
        out_specs=[rows(now)] * 5 + [_chunk_scalar_spec(SCAN_PAIRS, now), rows(before), rows(before),
                                     pl.BlockSpec((2 * SCAN_PAIRS, N_HEADS, HEAD, HEAD), lambda i: (jnp.maximum(i - 1, 0), 0, 0, 0))],
        out_shape=[_sds((s, D_HALF), BF16)] * 5 + [_sds((s // 8, D_HALF)), _sds((s, D_HALF)), _sds((s, D_HALF), BF16),
                                                  _sds((n_chunks, N_HEADS, HEAD, HEAD))],
        scratch_shapes=[pltpu.VMEM((N_HEADS, HEAD, HEAD), F32), slot(SCAN_ROWS, F32), slot(SCAN_ROWS, BF16),
                        slot(SCAN_ROWS, BF16), slot(SCAN_ROWS, BF16), slot(SCAN_ROWS, BF16), slot(16 * SCAN_PAIRS, F32)],
        compiler_params=_params("arbitrary"),
    )(qn, kn, vs, beta, g)


OUT_T = 512
OUT_ROWS = 256


def _out_fwd_bwd(x, y_pool, o, proj, target, w_out, dn_norm_w, final_norm_w):
    s = x.shape[0]
    t = OUT_T

    def body(x_ref, yp_ref, o_ref, z_ref, tg_ref, wo_ref, dnw_ref, fnw_ref,
             gwo_ref, dh_ref, dyp_ref, do_ref, dz_ref, loss_ref, gfn_ref, gdn_ref, y_ref, yt_ref, gwo_acc):
        @pl.when(pl.program_id(0) == 0)
        def _():
            loss_ref[...] = jnp.zeros_like(loss_ref)
            gfn_ref[...] = jnp.zeros_like(gfn_ref)
            gdn_ref[...] = jnp.zeros_like(gdn_ref)
            gwo_acc[...] = jnp.zeros_like(gwo_acc)

        dnw = dnw_ref[...]
        fnw = fnw_ref[...]

        def stages(rows, lead):
            for _ in range(lead):
                yield
            ypv = yp_ref[rows]
            y_ref[rows, :D_HALF] = ypv.astype(BF16)
            yt_ref[:D_HALF, rows] = ypv.T.astype(BF16)
            keep = []
            for h in HEADS:
                ov = o_ref[rows, HEAD_COLS[h]]
                zv = z_ref[rows, HEAD_COLS[h]]
                ro = lax.rsqrt(jnp.mean(ov * ov, axis=-1, keepdims=True) + EPS)
                ohat = ov * ro
                sg = _sigmoid(zv)
                keep.append((ro, ohat, zv, sg))
                ydn = ohat * dnw * (zv * sg)
                y_ref[rows, D_HALF + h * HEAD:D_HALF + (h + 1) * HEAD] = ydn.astype(BF16)
                yt_ref[D_HALF + h * HEAD:D_HALF + (h + 1) * HEAD, rows] = ydn.T.astype(BF16)
            yield
            hv = x_ref[rows] + jnp.dot(y_ref[rows], wo_ref[...], preferred_element_type=F32)
            yield
            r2 = lax.rsqrt(jnp.mean(hv * hv, axis=-1, keepdims=True) + EPS)
            hhat = hv * r2
            err = hhat * fnw - tg_ref[rows]
            loss_ref[...] += 0.5 * jnp.sum(_rowsum(err * err) * (1.0 / D_MODEL), axis=0, keepdims=True)
            dout = err * (1.0 / D_MODEL)
            gfn_ref[...] += _colsum(dout * hhat)
            dhh = dout * fnw
            dh = r2 * (dhh - hhat * jnp.mean(dhh * hhat, axis=-1, keepdims=True))
            dh_ref[rows] = dh
            yield
            if lead == t // OUT_ROWS - 1:
                gwo_acc[...] += _bdot(yt_ref[...], dh_ref[...])
            dy = _bdot_nt(dh, wo_ref[...])
            yield
            dyp_ref[rows] = dy[:, :D_HALF]
            gdn = jnp.zeros((1, HEAD), F32)
            for h in HEADS:
                ro, ohat, zv, sg = keep[h]
                dyd = dy[:, D_HALF + h * HEAD:D_HALF + (h + 1) * HEAD]
                sz = zv * sg
                dz_ref[rows, HEAD_COLS[h]] = (dyd * ohat * dnw * (sg * (1.0 + zv * (1.0 - sg)))).astype(BF16)
                gdn = gdn + _colsum(dyd * ohat * sz)
                doh = dyd * dnw * sz
                do_ref[rows, HEAD_COLS[h]] = ro * (doh - ohat * jnp.mean(doh * ohat, axis=-1, keepdims=True))
            gdn_ref[...] += gdn
            yield

        _interleave(*[stages(slice(k * OUT_ROWS, (k + 1) * OUT_ROWS), k) for k in range(t // OUT_ROWS)])

        @pl.when(pl.program_id(0) == pl.num_programs(0) - 1)
        def _():
            gwo_ref[...] = gwo_acc[...].astype(BF16)

    wide = pl.BlockSpec((t, D_MODEL), lambda i: (i, 0))
    half = pl.BlockSpec((t, D_HALF), lambda i: (i, 0))
    const = lambda shape: pl.BlockSpec(shape, lambda i: (0,) * len(shape))
    return _call(
        body, name="out_fwd_bwd", grid=(s // t,),
        in_specs=[wide, half, half, pl.BlockSpec((t, D_HALF), lambda i: (i, 5)), wide,
                  const((D_MODEL, D_MODEL)), const((1, HEAD)), const((1, D_MODEL))],
        out_specs=[const((D_MODEL, D_MODEL)), wide, half, half, half,
                   const((1, HEAD)), const((1, D_MODEL)), const((1, HEAD))],
        out_shape=[_sds((D_MODEL, D_MODEL), BF16), _sds((s, D_MODEL)), _sds((s, D_HALF)), _sds((s, D_HALF)),
                   _sds((s, D_HALF), BF16), _sds((1, HEAD)), _sds((1, D_MODEL)), _sds((1, HEAD))],
        scratch_shapes=[pltpu.VMEM((t, D_MODEL), BF16), pltpu.VMEM((D_MODEL, t), BF16), pltpu.VMEM((D_MODEL, D_MODEL), F32)],
        compiler_params=_params("arbitrary"),
    )(x, y_pool, o, proj, target, w_out, dn_norm_w, final_norm_w)


def _grad_w_in(at, pieces):
    m, s = at.shape
    n = len(pieces)
    tn, tk = D_HALF, min(s, 1024)

    def body(a_ref, *refs):
        p_refs, o_ref, acc = refs[:n], refs[n], refs[n + 1]

        @pl.when(pl.program_id(0) == 0)
        def _():
            acc[...] = jnp.zeros_like(acc)

        av = a_ref[...]
        for p in range(n):
            acc[:, p * tn:(p + 1) * tn] += _bdot(av, p_refs[p][...])

        @pl.when(pl.program_id(0) == pl.num_programs(0) - 1)
        def _():
            for j in range(4):
                base = j * BLK_IN // HEAD * HEAD
                win = acc[:, base:base + BLK_IN_PAD]
                if j * BLK_IN > base:
                    win = pltpu.roll(win, BLK_IN_PAD - (j * BLK_IN - base), 1)
                o_ref[j] = win.astype(BF16)

    return _call(
        body, name="grad_w_in", grid=(s // tk,),
        in_specs=[pl.BlockSpec((m, tk), lambda k: (0, k))] + [pl.BlockSpec((tk, tn), lambda k: (k, 0))] * n,
        out_specs=pl.BlockSpec((4, m, BLK_IN_PAD), lambda k: (0, 0, 0)),
        out_shape=_sds((4, m, BLK_IN_PAD), BF16),
        scratch_shapes=[pltpu.VMEM((m, n * tn), F32)],
        compiler_params=_params("arbitrary"),
    )(at, *pieces)


def _delta_bwd(do, vn, qd, kd, w, att, cd, st, qn, kn, vs, beta, g, tm):
    s = do.shape[0]
    n_steps = s // SCAN_ROWS
    assert INTRA_PAIRS == SCAN_PAIRS

    def body(do_ref, vn_ref, qd_ref, kd_ref, w_ref, att_ref, cd_ref, st_ref, qn_ref, kn_ref, vs_ref, beta_ref, g_ref, t_ref,
             dqn_ref, dkn_ref, dvs_ref, dbeta_ref, dg_ref, dstate, du_s, dw_s, datt_s, dqd_s, dkd_s, dcd_s):
        t = pl.program_id(0)

        @pl.when(t == 0)
        def _():
            dstate[...] = jnp.zeros_like(dstate)

        cur = lax.rem(t, 2)
        prev = 1 - cur
        cols = list(enumerate(HEAD_COLS))
        _, incl, _, _ = _pair_masks()

        def recurrence():
            dv_intra = []
            for pp in range(SCAN_PAIRS):
                rp = slice(pp * PAIR, (pp + 1) * PAIR)
                dv_intra.append([_bdot_tn(att_ref[rp, sl], do_ref[rp, sl]) for _, sl in cols])
                for _, sl in cols:
                    datt_s[cur, rp, sl] = jnp.where(incl, _bdot_nt(do_ref[rp, sl], vn_ref[rp, sl]), 0.0)
                yield
            ds = [dstate[h] for h in HEADS]
            for ci in range(2 * SCAN_PAIRS - 1, -1, -1):
                rs = slice(ci * CHUNK, (ci + 1) * CHUNK)
                in_pair = slice((ci % 2) * CHUNK, (ci % 2 + 1) * CHUNK)
                sm = [st_ref[ci, h] for h in HEADS]
                dvn = [dv_intra[ci // 2][h][in_pair] + _bdot(kd_ref[rs, sl], ds[h]) for h, sl in cols]
                dkd = [_bdot_nt(vn_ref[rs, sl], ds[h]) for h, sl in cols]
                dcd = [jnp.broadcast_to(_rowsum(_colsum(ds[h] * sm[h])), (8, HEAD)) for h in HEADS]
                yield
                both = [_bdot_nt(_stack(do_ref[rs, sl], dvn[h]), sm[h]) for h, sl in cols]
                for h, sl in cols:
                    du_s[cur, rs, sl] = dvn[h].astype(BF16)
                    dqd_s[cur, rs, sl] = both[h][:CHUNK]
                    dw_s[cur, rs, sl] = (-both[h][CHUNK:]).astype(BF16)
                    dkd_s[cur, rs, sl] = dkd[h]
                    dcd_s[cur, ci * 8:(ci + 1) * 8, sl] = dcd[h]
                ds = [ds[h] * cd_ref[ci * 8:ci * 8 + 1, sl]
                      + _bdot_tn(_stack(qd_ref[rs, sl], w_ref[rs, sl]), _stack(do_ref[rs, sl], -dvn[h])) for h, sl in cols]
                yield
            for h in HEADS:
                dstate[h] = ds[h]

        def factors(units):
            _heads = functools.partial(_heads_of, units=units)
            _put_heads = functools.partial(_put_heads_of, units=units)
            ones = jnp.ones((2 * PAIR, HEAD), BF16)
            tn = (((0,), (0,)), ((), ()))
            kept = lambda ref, rows=PAIR: [ref[prev, pp * rows:(pp + 1) * rows, HEAD_COLS[h]] for pp, h in units]
            kn, vs, beta = _heads(kn_ref), _heads(vs_ref), _heads(beta_ref)
            cm = {}
            yield from _pair_common_stages(cm, _heads(qn_ref), kn, vs, beta, _heads(g_ref))
            tmv = _heads(t_ref)
            duv, dwv, dattv, dqdv, dkdv = kept(du_s), kept(dw_s), kept(datt_s), kept(dqd_s), kept(dkd_s)
            duw = _each(_side, duv, dwv)
            both = _each(_bdot_tn, tmv, duw)
            dvb, dkbg = [v[:, :HEAD] for v in both], [v[:, HEAD:] for v in both]
            dt = _each(lambda a, b, c: _bdot_nt(a, _side(b, c)), duw, cm["vb"], cm["kbg"])
            yield
            m1 = _each(_bdot_tn, tmv, dt)
            yield
            da = _each(lambda a, b: -jnp.where(cm["strict"], _bdot_nt(a, b), 0.0), m1, tmv)
            yield
            dkk = _each(lambda a, b: a * b, da, cm["decay"])
            dqk = _each(lambda a, b: a * b, dattv, cm["decay"])
            dd = _each(lambda a, b, c, d: a * b + c * d, dkk, cm["kk"], dqk, cm["qk"])
            dkq = _each(_stack, dkk, dqk)
            both = _each(_bdot, dkq, kn)
            dkb = _each(lambda a, c, d: a[:PAIR] + c * d, both, dkbg, cm["egc"])
            dq = _each(lambda a, c, d: a[PAIR:] + c * d, both, dqdv, cm["egc"])
            yield
            dkn = _each(lambda a, b, c: _bdot_tn(a, _stack(b, c)), dkq, cm["kb"], cm["q"])
            dkn = _each(lambda a, b, c, d, e: a + b * c + d * e, dkn, dkdv, cm["ekd"], dkb, beta)
            t_kd = _each(lambda a, b, c: _rowsum(a * b * c), dkdv, kn, cm["ekd"])
            yield
            split = _each(_split, dd)
            rows_dd = [jnp.dot(_side(hi, lo), ones, preferred_element_type=F32) for hi, lo in split]
            cols_dd = [lax.dot_general(_stack(hi, lo), ones, tn, preferred_element_type=F32) for hi, lo in split]
            yield
            dgc = _each(lambda r, c, a, b, e, f, k, tk: r - c + _rowsum(a * b * e) + _rowsum(f * k) - tk,
                        rows_dd, cols_dd, dqdv, cm["q"], cm["egc"], dkbg, cm["kbg"], t_kd)
            same_b = cm["same"].astype(BF16)
            rowi = lax.broadcasted_iota(I32, (PAIR, HEAD), 0)
            dcd = _each(lambda d: jnp.where(rowi < CHUNK, d[0:1], d[8:9]), kept(dcd_s, rows=16))
            dgl = _each(lambda tk, d, c: _mask_dot(same_b, jnp.broadcast_to(tk, (PAIR, HEAD))) + d * c, t_kd, dcd, cm["cd"])
            yield
            is_last = jnp.bitwise_and(rowi, CHUNK - 1) == CHUNK - 1
            dgc = _each(lambda a, b: a + jnp.where(is_last, b, 0.0), dgc, dgl)
            r = lax.broadcasted_iota(I32, (PAIR, PAIR), 0)
            c = lax.broadcasted_iota(I32, (PAIR, PAIR), 1)
            upper_b = (cm["same"] & (r <= c)).astype(BF16)
            _put_heads(dg_ref, _each(lambda v: _mask_dot(upper_b, v), dgc))
            yield
            _put_heads(dbeta_ref, _each(lambda a, b, c, d: jnp.broadcast_to(_rowsum(a * b) + _rowsum(c * d), (PAIR, HEAD)),
                                        dkb, kn, dvb, vs))
            _put_heads(dqn_ref, _each(lambda v: v * QK_SCALE, dq))
            _put_heads(dkn_ref, dkn)
            _put_heads(dvs_ref, _each(lambda a, b: a * b, dvb, beta))
            yield

        _pipelined_step(t, n_steps, lambda: [recurrence()],
                        lambda: [factors(UNITS[pp * N_HEADS:(pp + 1) * N_HEADS]) for pp in range(INTRA_PAIRS)])

    last = n_steps - 1
    now = lambda i: (jnp.maximum(last - i, 0), 0)
    after = lambda i: (jnp.minimum(n_steps - i, last), 0)
    rows = lambda index: pl.BlockSpec((SCAN_ROWS, D_HALF), index)
    slot = lambda r, dtype: pltpu.VMEM((2, r, D_HALF), dtype)
    return _call(
        body, name="delta_bwd", grid=(n_steps + 1,),
        in_specs=[rows(now)] * 6 + [_chunk_scalar_spec(SCAN_PAIRS, now),
                                    pl.BlockSpec((2 * SCAN_PAIRS, N_HEADS, HEAD, HEAD), lambda i: (jnp.maximum(last - i, 0), 0, 0, 0))]
                 + [rows(after)] * 6,
        out_specs=[rows(after)] * 5,
        out_shape=[_sds((s, D_HALF))] * 5,
        scratch_shapes=[pltpu.VMEM((N_HEADS, HEAD, HEAD), F32), slot(SCAN_ROWS, BF16), slot(SCAN_ROWS, BF16),
                        slot(SCAN_ROWS, F32), slot(SCAN_ROWS, F32), slot(SCAN_ROWS, F32), slot(16 * SCAN_PAIRS, F32)],
        compiler_params=_params("arbitrary"),
    )(do, vn, qd, kd, w, att, cd, st, qn, kn, vs, beta, g, tm)


def _fused_call(name, n_steps, parts):
    n_in = [len(p["inputs"]) for p in parts]
    n_out = [len(p["out_shape"]) for p in parts]
    n_scr = [len(p["scratch"]) for p in parts]

    def body(*refs):
        ins, outs, scr = refs[:sum(n_in)], refs[sum(n_in):sum(n_in) + sum(n_out)], refs[sum(n_in) + sum(n_out):]
        gens, a, b, c = [], 0, 0, 0
        for p, ni, no, ns in zip(parts, n_in, n_out, n_scr):
            gens.append(p["stages"](ins[a:a + ni], outs[b:b + no], scr[c:c + ns]))
            a, b, c = a + ni, b + no, c + ns
        _interleave(*gens)

    flat = lambda key: [v for p in parts for v in p[key]]
    res = _call(
        body, name=name, grid=(n_steps,),
        in_specs=flat("in_specs"), out_specs=flat("out_specs"), out_shape=flat("out_shape"),
        scratch_shapes=flat("scratch"),
        compiler_params=_params("arbitrary"),
    )(*flat("inputs"))
    out, b = [], 0
    for no in n_out:
        out.append(res[b:b + no])
        b += no
    return out


def _pool_bwd_part(proj, dyp, pool_w, pool_scale, tile_of, n_tiles):
    s = proj.shape[0]
    t = POOL_T
    hb = t // HEAD
    last = s // HEAD - 1

    def stages(ins, outs, scratch):
        u_ref, z_ref, halo_ref, dy_ref, zn_ref, dyn_ref, pw_ref, ps_ref, band_ref, aband_ref = ins
        du_ref, dz_ref, gpw_ref, gps_ref = outs
        tile = tile_of(pl.program_id(0))

        @pl.when(pl.program_id(0) == 0)
        def _():
            gpw_ref[...] = jnp.zeros_like(gpw_ref)
            gps_ref[...] = jnp.zeros_like(gps_ref)

        live = (tile > 0).astype(F32)
        more = (tile < n_tiles - 1).astype(F32)
        groups = lambda ref: [ref[:, sl] for sl in HEAD_COLS]
        z, ps, dy = groups(z_ref), groups(ps_ref), groups(dy_ref)
        pw = [pw_ref[g] for g in HEADS]
        mix, mixed, sg, cnt = _pool_mix(groups(u_ref), [h * live for h in groups(halo_ref)], z, pw,
                                        [band_ref[g] for g in HEADS], tile * t)
        yield
        sz = _each(lambda a, b: a * b, z, sg)
        for sl, d, m, p, s_, zg in zip(HEAD_COLS, dy, mixed, ps, sg, z):
            dz_ref[:, sl] = (d * m * p * (s_ * (1.0 + zg * (1.0 - s_)))).astype(BF16)
        for sl, d, m, a in zip(HEAD_COLS, dy, mixed, sz):
            gps_ref[:, sl] += _colsum(d * m * a)
        dmixed = _each(lambda d, p, a: d * p * a, dy, ps, sz)
        yield
        for g, gp in enumerate(_each(_bdot_tn, mix, dmixed)):
            gpw_ref[g] += gp
        dmix = _each(_bdot_nt, dmixed, pw)
        yield
        dmix_n = _each(lambda d, p, zn, w_: _bdot_nt(d * more * p * (zn * _sigmoid(zn)), w_),
                       groups(dyn_ref), ps, groups(zn_ref), pw)
        yield
        scaled = [jnp.concatenate([a / c, b * (1.0 / w)], axis=0) for a, c, b, w in zip(dmix, cnt, dmix_n, WINDOWS)]
        du = _each(lambda b, s_, d: _mask_dot(b, s_) - d, [aband_ref[g] for g in HEADS], scaled, dmix)
        for sl, v in zip(HEAD_COLS, du):
            du_ref[:, sl] = v.astype(BF16)
        yield

    tile = lambda col: pl.BlockSpec((t, D_HALF), lambda i: (tile_of(i), col))
    below = lambda col: pl.BlockSpec((HEAD, D_HALF), lambda i: (jnp.minimum((tile_of(i) + 1) * hb, last), col))
    const3 = lambda shape: pl.BlockSpec(shape, lambda i: (0, 0, 0))
    return dict(
        inputs=[proj, proj, proj, dyp, proj, dyp, pool_w, pool_scale, _pool_bands(t), _pool_bands(t, anti=True)],
        in_specs=[tile(0), tile(1), pl.BlockSpec((HEAD, D_HALF), lambda i: (jnp.maximum(tile_of(i) * hb - 1, 0), 0)),
                  tile(0), below(1), below(0), const3((N_HEADS, HEAD, HEAD)), pl.BlockSpec((1, D_HALF), lambda i: (0, 0)),
                  const3((N_HEADS, t, HEAD + t)), const3((N_HEADS, t, HEAD + t))],
        out_specs=[tile(0), tile(0), const3((N_HEADS, HEAD, HEAD)), pl.BlockSpec((1, D_HALF), lambda i: (0, 0))],
        out_shape=[_sds((s, D_HALF), BF16), _sds((s, D_HALF), BF16), _sds((N_HEADS, HEAD, HEAD)), _sds((1, D_HALF))],
        scratch=[], stages=stages)


def _conv_bwd_part(proj, pre, conv_w, a_log, dt_bias, dqn, dkn, dvs, dbeta, dg, tile_of, n_tiles):
    s = proj.shape[0]
    t = CONV_T

    def stages(ins, outs, scratch):
        (q_ref, k_ref, v_ref, yq_ref, yk_ref, yv_ref, ba_ref, cw_ref, al_ref, dtb_ref,
         dqn_ref, dkn_ref, dvs_ref, dbeta_ref, dg_ref) = ins
        oq_ref, ok_ref, ov_ref, dba_ref, gcw_out, gsm_out = outs
        below, gcw_ref, gsm_ref = scratch
        step = pl.program_id(0)

        @pl.when(step == 0)
        def _():
            gcw_ref[...] = jnp.zeros_like(gcw_ref)
            gsm_ref[...] = jnp.zeros_like(gsm_ref)
            below[...] = jnp.zeros_like(below)

        parts = ((q_ref, yq_ref, dqn_ref, oq_ref), (k_ref, yk_ref, dkn_ref, ok_ref), (v_ref, yv_ref, dvs_ref, ov_ref))
        for p, (x_ref, y_ref, d_ref, o_ref) in enumerate(parts):
            for h in HEADS:
                cs = HEAD_COLS[h]
                wide = slice(p * D_HALF + h * HEAD, p * D_HALF + (h + 1) * HEAD)
                cw = cw_ref[:, wide]
                y = y_ref[:, cs]
                sg = _sigmoid(y)
                sv = y * sg
                ds = d_ref[:, cs]
                if p < 2:
                    rn = lax.rsqrt(_rowsum(sv * sv) + EPS)
                    nrm = sv * rn
                    ds = rn * (ds - nrm * _rowsum(ds * nrm))
                dy = ds * (sg * (1.0 + y * (1.0 - sg)))
                nxt = below[:, wide]
                ahead = [dy] + [_shift_up(dy, nxt, sft) for sft in range(1, CONV_K)]
                xv = x_ref[:, cs]
                acc = dy * cw[CONV_K - 1:CONV_K]
                for sft in range(1, CONV_K):
                    acc = acc + ahead[sft] * cw[CONV_K - 1 - sft:CONV_K - sft]
                for j in range(CONV_K):
                    gcw_ref[8 * j:8 * j + 8, wide] += _rows8(xv * ahead[CONV_K - 1 - j])
                o_ref[:, cs] = acc.astype(BF16)
                below[:, wide] = dy[0:8]
                yield

        ba = ba_ref[...]
        lane = lax.broadcasted_iota(I32, (t, HEAD), 1)
        lane8 = lax.broadcasted_iota(I32, (8, HEAD), 1)
        dba = jnp.zeros((t, HEAD), F32)
        gsm = jnp.zeros((8, HEAD), F32)
        for h in HEADS:
            beta = _sigmoid(ba[:, h:h + 1])
            dbeta = dbeta_ref[:, h * HEAD:h * HEAD + 1]
            xg = ba[:, N_HEADS + h:N_HEADS + h + 1] + dtb_ref[0:1, h:h + 1]
            nexp = -jnp.exp(al_ref[0:1, h:h + 1])
            dgv = dg_ref[:, h * HEAD:h * HEAD + 1]
            da = dgv * nexp * _sigmoid(xg)
            dba = dba + jnp.where(lane == h, dbeta * beta * (1.0 - beta), 0.0) + jnp.where(lane == N_HEADS + h, da, 0.0)
            gsm = (gsm + jnp.where(lane8 == h, _rows8(dgv * nexp * _softplus(xg)), 0.0)
                   + jnp.where(lane8 == N_HEADS + h, _rows8(da), 0.0))
        dba_ref[...] = jnp.zeros_like(dba_ref)
        dba_ref[:, :HEAD] = dba.astype(BF16)
        gsm_ref[...] += gsm
        yield

        @pl.when(step == n_tiles - 1)
        def _():
            gcw_out[...] = jnp.zeros_like(gcw_out)
            for j in range(CONV_K):
                gcw_out[j:j + 1, :] = _colsum(gcw_ref[8 * j:8 * j + 8, :])
            gsm_out[...] = jnp.broadcast_to(_colsum(gsm_ref[...]), (8, HEAD))

    row = pl.BlockSpec((t, D_HALF), lambda i: (tile_of(i), 0))
    const = lambda shape: pl.BlockSpec(shape, lambda i: (0, 0))
    return dict(
        inputs=[proj] * 3 + list(pre) + [proj, conv_w, a_log, dt_bias, dqn, dkn, dvs, dbeta, dg],
        in_specs=_conv_specs(t, tile_of)[:3] + [row] * 3
                 + [pl.BlockSpec((t, HEAD), lambda i: (tile_of(i), COL_BA // HEAD)),
                    const((CONV_K, 3 * D_HALF)), const((1, N_HEADS)), const((1, N_HEADS))] + [row] * 5,
        out_specs=[row, row, row, row, const((8, 3 * D_HALF)), const((8, HEAD))],
        out_shape=[_sds((s, D_HALF), BF16)] * 4 + [_sds((8, 3 * D_HALF)), _sds((8, HEAD))],
        scratch=[pltpu.VMEM((8, 3 * D_HALF), F32), pltpu.VMEM((8 * CONV_K, 3 * D_HALF), F32), pltpu.VMEM((8, HEAD), F32)],
        stages=stages)


def _pool_fwd_part(proj, pool_w, pool_scale):
    s = proj.shape[0]
    t = POOL_T
    hb = t // HEAD

    def stages(ins, outs, scratch, tile=None):
        u_ref, z_ref, halo_ref, pw_ref, ps_ref, band_ref = ins
        y_ref, = outs
        i = pl.program_id(0) if tile is None else tile
        live = (i > 0).astype(F32)
        groups = lambda ref: [ref[:, sl] for sl in HEAD_COLS]
        z = groups(z_ref)
        u, halo = groups(u_ref), [h * live for h in groups(halo_ref)]
        yield
        _, mixed, sg, _ = _pool_mix(u, halo, z, [pw_ref[g] for g in HEADS], [band_ref[g] for g in HEADS], i * t)
        yield
        for sl, m, zg, s_ in zip(HEAD_COLS, mixed, z, sg):
            y_ref[:, sl] = m * ps_ref[:, sl] * (zg * s_)
        yield

    const3 = lambda shape: pl.BlockSpec(shape, lambda i: (0, 0, 0))
    return dict(
        inputs=[proj, proj, proj, pool_w, pool_scale, _pool_bands(t)],
        in_specs=[pl.BlockSpec((t, D_HALF), lambda i: (i, 0)), pl.BlockSpec((t, D_HALF), lambda i: (i, 1)),
                  pl.BlockSpec((HEAD, D_HALF), lambda i: (jnp.maximum(i * hb - 1, 0), 0)),
                  const3((N_HEADS, HEAD, HEAD)), pl.BlockSpec((1, D_HALF), lambda i: (0, 0)), const3((N_HEADS, t, HEAD + t))],
        out_specs=[pl.BlockSpec((t, D_HALF), lambda i: (i, 0))], out_shape=[_sds((s, D_HALF))],
        scratch=[], stages=stages)


def _conv_fwd_part(proj, conv_w, a_log, dt_bias):
    s = proj.shape[0]
    t = CONV_T

    def stages(ins, outs, scratch, tile=None):
        q_ref, k_ref, v_ref, hq_ref, hk_ref, hv_ref, ba_ref, cw_ref, al_ref, dtb_ref = ins
        qn_ref, kn_ref, vs_ref, beta_ref, g_ref, yq_ref, yk_ref, yv_ref = outs
        live = ((pl.program_id(0) if tile is None else tile) > 0).astype(F32)
        parts = ((q_ref, hq_ref, qn_ref, yq_ref), (k_ref, hk_ref, kn_ref, yk_ref), (v_ref, hv_ref, vs_ref, yv_ref))
        for p, (x_ref, h_ref, o_ref, y_ref) in enumerate(parts):
            for h in HEADS:
                cs = HEAD_COLS[h]
                taps = _conv_taps(x_ref[:, cs], h_ref[:, cs] * live)
                y = _conv_pre(taps, cw_ref[:, p * D_HALF + h * HEAD:p * D_HALF + (h + 1) * HEAD])
                y_ref[:, cs] = y
                sv = y * _sigmoid(y)
                o_ref[:, cs] = sv if p == 2 else sv * lax.rsqrt(_rowsum(sv * sv) + EPS)
                yield
        ba = ba_ref[...]
        for h in HEADS:
            beta = _sigmoid(ba[:, h:h + 1])
            gl = -jnp.exp(al_ref[0:1, h:h + 1]) * _softplus(ba[:, N_HEADS + h:N_HEADS + h + 1] + dtb_ref[0:1, h:h + 1])
            beta_ref[:, HEAD_COLS[h]] = jnp.broadcast_to(beta, (t, HEAD))
            g_ref[:, HEAD_COLS[h]] = jnp.broadcast_to(gl, (t, HEAD))
        yield

    row = pl.BlockSpec((t, D_HALF), lambda i: (i, 0))
    const = lambda shape: pl.BlockSpec(shape, lambda i: (0, 0))
    return dict(
        inputs=[proj] * 7 + [conv_w, a_log, dt_bias],
        in_specs=_conv_specs(t) + [pl.BlockSpec((t, HEAD), lambda i: (i, COL_BA // HEAD)),
                                   const((CONV_K, 3 * D_HALF)), const((1, N_HEADS)), const((1, N_HEADS))],
        out_specs=[row] * 8, out_shape=[_sds((s, D_HALF))] * 8, scratch=[], stages=stages)


def _front_fwd(x, norm_w, w_pad, conv_w, a_log, dt_bias, pool_w, pool_scale, after):
    s = x.shape[0]
    t = CONV_T
    n_tiles = s // t
    assert POOL_T == CONV_T
    like_proj = _sds((s, N_IN_PAD))
    conv = _conv_fwd_part(like_proj, conv_w, a_log, dt_bias)
    pool = _pool_fwd_part(like_proj, pool_w, pool_scale)
    bands = pool["inputs"][-1]
    mxu_n = 256
    col_bounds = list(range(0, N_IN_PAD, 3 * mxu_n)) + [N_IN_PAD]

    def body(x_ref, nw_ref, w_ref, cw_ref, al_ref, dtb_ref, pw_ref, ps_ref, band_ref, after_ref,
             proj_ref, nt_ref, qn_ref, kn_ref, vs_ref, beta_ref, g_ref, yq_ref, yk_ref, yv_ref, y_ref, prev):
        del after_ref
        i = pl.program_id(0)

        @pl.when(i == 0)
        def _():
            prev[...] = jnp.zeros_like(prev)

        tile = jnp.maximum(i - 1, 0)
        main, above8, above = pl.ds(HEAD, t), pl.ds(HEAD - 8, 8), pl.ds(0, HEAD)
        cols = lambda rows, c0, width=D_HALF: prev.at[rows, pl.ds(c0, width)]
        conv_ins = (cols(main, 2 * D_HALF), cols(main, 3 * D_HALF), cols(main, 4 * D_HALF),
                    cols(above8, 2 * D_HALF), cols(above8, 3 * D_HALF), cols(above8, 4 * D_HALF),
                    cols(main, COL_BA, HEAD), cw_ref, al_ref, dtb_ref)
        pool_ins = (cols(main, 0), cols(main, D_HALF), cols(above, 0), pw_ref, ps_ref, band_ref)

        def projection():
            xv = x_ref[...]
            r = lax.rsqrt(jnp.mean(xv * xv, axis=-1, keepdims=True) + EPS)
            nv = xv * r * nw_ref[...]
            nt_ref[...] = nv.T.astype(BF16)
            nb = nv.astype(BF16)
            yield
            for lo, hi in zip(col_bounds[:-1], col_bounds[1:]):
                proj_ref[:, lo:hi] = jnp.dot(nb, w_ref[:, lo:hi], preferred_element_type=F32)
                yield

        _interleave(projection(),
                    conv["stages"](conv_ins, (qn_ref, kn_ref, vs_ref, beta_ref, g_ref, yq_ref, yk_ref, yv_ref), (), tile),
                    pool["stages"](pool_ins, (y_ref,), (), tile))
        prev[0:HEAD] = prev[t:t + HEAD]
        prev[HEAD:HEAD + t] = proj_ref[...]

    last = n_tiles - 1
    now = lambda i: (jnp.minimum(i, last), 0)
    before = lambda i: (jnp.maximum(i - 1, 0), 0)
    const = lambda a: pl.BlockSpec(a.shape, lambda i: (0,) * a.ndim)
    half = pl.BlockSpec((t, D_HALF), before)
    return _call(
        body, name="front_fwd", grid=(n_tiles + 1,),
        in_specs=[pl.BlockSpec((t, D_MODEL), now), const(norm_w), const(w_pad), const(conv_w), const(a_log), const(dt_bias),
                  const(pool_w), const(pool_scale), const(bands), pl.BlockSpec(memory_space=pl.ANY)],
        out_specs=[pl.BlockSpec((t, N_IN_PAD), now), pl.BlockSpec((D_MODEL, t), lambda i: (0, jnp.minimum(i, last)))]
                  + [half] * 9,
        out_shape=[_sds((s, N_IN_PAD)), _sds((D_MODEL, s), BF16)] + [_sds((s, D_HALF))] * 9,
        scratch_shapes=[pltpu.VMEM((HEAD + t, N_IN_PAD), F32)],
        compiler_params=_params("arbitrary"),
    )(x, norm_w, w_pad, conv_w, a_log, dt_bias, pool_w, pool_scale, bands, after)


def _conv_pool_bwd(proj, pre, conv_w, a_log, dt_bias, dqn, dkn, dvs, dbeta, dg, dyp, pool_w, pool_scale):
    n_tiles = proj.shape[0] // CONV_T
    assert POOL_T == CONV_T
    tile_of = lambda i: n_tiles - 1 - i
    return _fused_call("conv_pool_bwd", n_tiles, [
        _conv_bwd_part(proj, pre, conv_w, a_log, dt_bias, dqn, dkn, dvs, dbeta, dg, tile_of, n_tiles),
        _pool_bwd_part(proj, dyp, pool_w, pool_scale, tile_of, n_tiles)])


def _rows8(x):
    acc = x[0:8]
    for r in range(8, x.shape[0], 8):
        acc = acc + x[r:r + 8]
    return acc


IN_T = 512


def _in_bwd(x, dh, norm_w, w_pad, pieces, after):
    s = x.shape[0]
    t = IN_T
    widths = [D_HALF] * 6 + [N_IN_PAD - COL_BA]

    def body(*refs):
        x_ref, dh_ref, nw_ref, w_ref = refs[:4]
        p_refs = refs[4:4 + len(pieces)]
        gx_ref, gnw_ref = refs[5 + len(pieces):]

        @pl.when(pl.program_id(0) == 0)
        def _():
            gnw_ref[...] = jnp.zeros_like(gnw_ref)

        dn = jnp.zeros((t, D_MODEL), F32)
        col = 0
        for p_ref, wd in zip(p_refs, widths):
            dn = dn + _bdot_nt(p_ref[...], w_ref[:, col:col + wd])
            col += wd
        xv = x_ref[...]
        r = lax.rsqrt(jnp.mean(xv * xv, axis=-1, keepdims=True) + EPS)
        xhat = xv * r
        gnw_ref[...] += _colsum(dn * xhat)
        dxh = dn * nw_ref[...]
        gx_ref[...] = dh_ref[...] + r * (dxh - xhat * jnp.mean(dxh * xhat, axis=-1, keepdims=True))

    wide = pl.BlockSpec((t, D_MODEL), lambda i: (i, 0))
    return _call(
        body, name="in_bwd", grid=(s // t,),
        in_specs=[wide, wide, pl.BlockSpec((1, D_MODEL), lambda i: (0, 0)),
                  pl.BlockSpec((D_MODEL, N_IN_PAD), lambda i: (0, 0))]
                 + [pl.BlockSpec((t, wd), lambda i: (i, 0)) for wd in widths] + [pl.BlockSpec(memory_space=pl.ANY)],
        out_specs=[wide, pl.BlockSpec((1, D_MODEL), lambda i: (0, 0))],
        out_shape=[_sds((s, D_MODEL)), _sds((1, D_MODEL))],
        compiler_params=_params("arbitrary"),
    )(x, dh, norm_w, w_pad, *pieces, after)


def _adamw_shard(name, w, g_own, g_got, cidx, m, v):
    _, r, c = w.shape
    half = r // 2
    rows = 256 if half % 256 == 0 else half
    per_half = half // rows

    def body(c_ref, w_ref, go_ref, gg_ref, m_ref, v_ref, gout_ref, d_ref, nm_ref, nv_ref):
        mine = (pl.program_id(0) // per_half) == c_ref[0]
        gv = jnp.where(mine, go_ref[:, :c], gg_ref[:, :c])
        gout_ref[0] = gv
        mn = ADAM_B1 * m_ref[0] + (1.0 - ADAM_B1) * gv
        vn = ADAM_B2 * v_ref[0] + (1.0 - ADAM_B2) * (gv * gv)
        m_hat = mn / (1.0 - ADAM_B1 ** ADAM_STEP)
        v_hat = vn / (1.0 - ADAM_B2 ** ADAM_STEP)
        d_ref[0] = -ADAM_LR * (m_hat / (jnp.sqrt(v_hat) + ADAM_EPS) + ADAM_WD * w_ref[0])
        nm_ref[0] = mn
        nv_ref[0] = vn

    blk = pl.BlockSpec((1, rows, c), lambda i, c_ref: (0, i, 0))
    gblk = pl.BlockSpec((rows, g_own.shape[1]), lambda i, c_ref: (i % per_half, 0))
    return _call(
        body, name=name,
        grid_spec=pltpu.PrefetchScalarGridSpec(
            num_scalar_prefetch=1, grid=(2 * per_half,),
            in_specs=[blk, gblk, gblk, blk, blk], out_specs=[blk] * 4),
        out_shape=[_sds((1, r, c))] * 4,
        compiler_params=_params("arbitrary"),
    )(cidx, w, g_own, g_got, m, v)


def _adamw_tiles(name, w, g, m, v):
    n = w.shape[0]
    nb = 77 if n % 77 == 0 else n

    def body(w_ref, g_ref, m_ref, v_ref, d_ref, nm_ref, nv_ref):
        gv = g_ref[...]
        mn = ADAM_B1 * m_ref[...] + (1.0 - ADAM_B1) * gv
        vn = ADAM_B2 * v_ref[...] + (1.0 - ADAM_B2) * (gv * gv)
        m_hat = mn / (1.0 - ADAM_B1 ** ADAM_STEP)
        v_hat = vn / (1.0 - ADAM_B2 ** ADAM_STEP)
        d_ref[...] = -ADAM_LR * (m_hat / (jnp.sqrt(v_hat) + ADAM_EPS) + ADAM_WD * w_ref[...])
        nm_ref[...] = mn
        nv_ref[...] = vn

    blk = pl.BlockSpec((nb, 8, HEAD), lambda i: (i, 0, 0))
    return _call(
        body, name=name, grid=(n // nb,),
        in_specs=[blk] * 4, out_specs=[blk] * 3, out_shape=[_sds(w.shape)] * 3,
        compiler_params=_params("arbitrary"),
    )(w, g, m, v)


def _make_copy(src, dst, send, recv, target):
    if target is None:
        return pltpu.make_async_copy(src, dst, recv)
    return pltpu.make_async_remote_copy(src_ref=src, dst_ref=dst, send_sem=send, recv_sem=recv,
                                        device_id=target, device_id_type=pl.DeviceIdType.MESH)


def _exchange(name, inputs, out_shapes, phases):
    n_in = len(inputs)
    n_out = len(out_shapes)
    n_cp = sum(len(p) for p in phases)

    def body(*refs):
        ins, outs = refs[:n_in], refs[n_in:n_in + n_out]
        send, recv = refs[n_in + n_out:]
        pos = (lax.axis_index("x"), lax.axis_index("y"), lax.axis_index("c"))
        k = 0
        for phase in phases:
            cps = []
            for src, dst, target in phase:
                cps.append(_make_copy(src(ins, outs, pos), dst(ins, outs, pos), send.at[k], recv.at[k],
                                      target and target(pos)))
                k += 1
            for cp in cps:
                cp.start()
            for cp in cps:
                cp.wait()

    anyspec = pl.BlockSpec(memory_space=pl.ANY)
    return _call(
        body, name=name,
        in_specs=[anyspec] * n_in, out_specs=[anyspec] * n_out, out_shape=list(out_shapes),
        scratch_shapes=[pltpu.SemaphoreType.DMA((n_cp,)), pltpu.SemaphoreType.DMA((n_cp,))],
    )(*inputs)


def _exchange_start(name, inputs, out_shapes, copies):
    n_in, n_out, n_cp = len(inputs), len(out_shapes), len(copies)

    def body(*refs):
        ins, lands = refs[:n_in], refs[n_in:n_in + n_out]
        sems = refs[n_in + n_out:n_in + n_out + 2 * n_cp]
        token = refs[-1]
        pos = (lax.axis_index("x"), lax.axis_index("y"), lax.axis_index("c"))
        for k, (src, dst, target) in enumerate(copies):
            _make_copy(src(ins, lands, pos), dst(ins, lands, pos), sems[2 * k], sems[2 * k + 1],
                       target and target(pos)).start()
        token[...] = jnp.zeros_like(token)

    hbm = pl.BlockSpec(memory_space=pltpu.HBM)
    sem = pl.BlockSpec(memory_space=pltpu.SEMAPHORE)
    bufs = list(inputs) + [lax.empty(o.shape, o.dtype) for o in out_shapes]
    outs = _call(
        body, name=name,
        out_shape=tuple([pltpu.SemaphoreType.DMA(())] * (2 * n_cp) + [pltpu.HBM(b.shape, b.dtype) for b in bufs]
                        + [_sds((8, HEAD))]),
        in_specs=[hbm] * len(bufs),
        out_specs=tuple([sem] * (2 * n_cp) + [hbm] * len(bufs) + [pl.BlockSpec(memory_space=pltpu.VMEM)]),
        input_output_aliases={i: 2 * n_cp + i for i in range(len(bufs))},
        compiler_params=pltpu.CompilerParams(has_side_effects=pltpu.SideEffectType.DATAFLOW_SIDE_EFFECTING),
    )(*[pltpu.with_memory_space_constraint(b, pltpu.HBM) for b in bufs])
    return outs[:2 * n_cp], outs[2 * n_cp:2 * n_cp + n_in], outs[2 * n_cp + n_in:-1], outs[-1]


def _exchange_wait(name, sems, sources, lands, copies, after):
    n_in, n_out, n_cp = len(sources), len(lands), len(copies)

    def body(*refs):
        ins, zones = refs[:n_in], refs[n_in:n_in + n_out]
        sem_refs = refs[n_in + n_out:n_in + n_out + 2 * n_cp]
        pos = (lax.axis_index("x"), lax.axis_index("y"), lax.axis_index("c"))
        for k, (src, dst, target) in enumerate(copies):
            cp = _make_copy(src(ins, zones, pos), dst(ins, zones, pos), sem_refs[2 * k], sem_refs[2 * k + 1],
                            target and target(pos))
            if target is None:
                cp.wait()
            else:
                cp.wait_send()
                cp.wait_recv()

    hbm = pl.BlockSpec(memory_space=pltpu.HBM)
    sem = pl.BlockSpec(memory_space=pltpu.SEMAPHORE)
    bufs = list(sources) + list(lands)
    outs = _call(
        body, name=name,
        out_shape=tuple(pltpu.HBM(b.shape, b.dtype) for b in bufs),
        in_specs=[hbm] * len(bufs) + [sem] * (2 * n_cp) + [pl.BlockSpec(memory_space=pl.ANY)],
        out_specs=tuple([hbm] * len(bufs)),
        input_output_aliases={i: i for i in range(len(bufs))},
        compiler_params=pltpu.CompilerParams(has_side_effects=pltpu.SideEffectType.DATAFLOW_SIDE_EFFECTING),
    )(*bufs, *sems, after)
    return outs[:n_in], outs[n_in:]


def _allreduce_tile(name, v):
    def body(v_ref, out_ref, slots, send, recv):
        x, y, c = lax.axis_index("x"), lax.axis_index("y"), lax.axis_index("c")
        me = 4 * x + 2 * y + c
        slots[me] = v_ref[...]
        cps = []
        for k in range(1, 8):
            peer = (x ^ (k >> 2), y ^ ((k >> 1) & 1), c ^ (k & 1))
            cps.append(pltpu.make_async_remote_copy(
                src_ref=v_ref, dst_ref=slots.at[me], send_sem=send.at[k - 1], recv_sem=recv.at[k - 1],
                device_id=peer, device_id_type=pl.DeviceIdType.MESH))
        for cp in cps:
            cp.start()
        for cp in cps:
            cp.wait()
        acc = slots[0]
        for i in range(1, 8):
            acc = acc + slots[i]
        out_ref[...] = acc

    vm = pl.BlockSpec(memory_space=pltpu.VMEM)
    return _call(
        body, name=name, in_specs=[vm], out_specs=vm, out_shape=_sds(v.shape),
        scratch_shapes=[pltpu.VMEM((8,) + v.shape, F32), pltpu.SemaphoreType.DMA((7,)), pltpu.SemaphoreType.DMA((7,))],
    )(v)


def _chip(pos):
    return 2 * pos[0] + pos[1]


def _other_chip(pos, mask):
    x, y, c = pos
    return (x ^ (mask >> 1), y ^ (mask & 1), c)


def _sibling(pos):
    return (pos[0], pos[1], 1 - pos[2])


def _gather_weights(wb, cb):
    rows = wb.shape[0] // 2
    x_nb, y_nb, diag = CHIP_MASKS

    def part(pos, mask, quarter=None):
        start = pos[2] * rows if quarter is None else pos[2] * rows + quarter * (rows // 2)
        return lambda outs: outs[0].at[_chip(pos) ^ mask, pl.ds(start, rows if quarter is None else rows // 2)]

    def passed_on(mask, to, quarter=None):
        return (lambda ins, outs, pos: part(pos, mask, quarter)(outs), lambda ins, outs, pos: part(pos, mask, quarter)(outs), to)

    first = [(lambda ins, outs, pos: ins[0].at[pl.ds(pos[2] * rows, rows)], lambda ins, outs, pos: part(pos, 0)(outs),
              functools.partial(_other_chip, mask=mask)) for mask in (x_nb, y_nb)]
    conv_cols = lambda ins, outs, pos: outs[1].at[:, pl.ds(pl.multiple_of(_chip(pos) * cb.shape[1], HEAD), cb.shape[1])]
    first += [(lambda ins, outs, pos: ins[1], conv_cols, functools.partial(_other_chip, mask=mask)) for mask in CHIP_MASKS]
    first += [(lambda ins, outs, pos: ins[1], conv_cols, None)]
    second = [passed_on(x_nb, functools.partial(_other_chip, mask=y_nb), quarter=0),
              passed_on(y_nb, functools.partial(_other_chip, mask=x_nb), quarter=1),
              passed_on(x_nb, _sibling), passed_on(y_nb, _sibling)]
    third = [passed_on(diag, _sibling)]
    return _exchange("gather_weights", [wb, cb],
                     [_sds((4,) + wb.shape, wb.dtype), _sds((cb.shape[0], 4 * cb.shape[1]), cb.dtype)], [first, second, third])


def _assemble_w_in(gw, wb, jidx):
    m = gw.shape[1]

    def body(j_ref, g_ref, wb_ref, o_ref):
        step = pl.program_id(0)

        @pl.when(step == 0)
        def _():
            o_ref[...] = jnp.zeros_like(o_ref)

        blk = jnp.where(step == j_ref[0], wb_ref[...], g_ref[0]).astype(F32)
        lane = lax.broadcasted_iota(I32, (m, BLK_IN_PAD), 1)
        for j in range(4):
            @pl.when(step == j)
            def _(j=j):
                base = j * BLK_IN // HEAD * HEAD
                shift = j * BLK_IN - base
                moved = pltpu.roll(blk, shift, 1) if shift else blk
                window = o_ref[:, base:base + BLK_IN_PAD].astype(F32)
                mine = (lane >= shift) & (lane < shift + BLK_IN)
                o_ref[:, base:base + BLK_IN_PAD] = jnp.where(mine, moved, window).astype(BF16)

    return _call(
        body, name="assemble_w_in",
        grid_spec=pltpu.PrefetchScalarGridSpec(
            num_scalar_prefetch=1, grid=(4,),
            in_specs=[pl.BlockSpec((1, m, BLK_IN_PAD), lambda j, j_ref: (j, 0, 0)),
                      pl.BlockSpec((m, BLK_IN_PAD), lambda j, j_ref: (0, 0))],
            out_specs=pl.BlockSpec((m, N_IN_PAD), lambda j, j_ref: (0, 0))),
        out_shape=_sds((m, N_IN_PAD), BF16),
        compiler_params=_params("arbitrary"),
    )(jidx, gw, wb)


def _gather_blocks(ob):
    copies = [(lambda ins, outs, pos: ins[0], lambda ins, outs, pos: outs[0].at[_chip(pos)],
               functools.partial(_other_chip, mask=mask)) for mask in CHIP_MASKS]
    copies.append((lambda ins, outs, pos: ins[0], lambda ins, outs, pos: outs[0].at[_chip(pos)], None))
    return [_sds((4,) + ob.shape, ob.dtype)], copies


def _reduce_sibling(name, arrays, cidx):
    n = len(arrays)
    halves = [a.shape[:-2] + (a.shape[-2] // 2, a.shape[-1]) for a in arrays]
    pieces = [(a, j) for a in range(n) for j in (range(arrays[a].shape[0]) if arrays[a].ndim == 3 else [None])]

    def body(c_ref, *refs):
        del c_ref
        whole, own, outs, land = refs[:n], refs[n:2 * n], refs[2 * n:3 * n], refs[3 * n:4 * n]
        send, recv = refs[4 * n:]
        pos = (lax.axis_index("x"), lax.axis_index("y"), lax.axis_index("c"))
        cps = []
        for k, (a, j) in enumerate(pieces):
            rows = pl.ds((1 - pos[2]) * halves[a][-2], halves[a][-2])
            cps.append(pltpu.make_async_remote_copy(
                src_ref=whole[a].at[rows] if j is None else whole[a].at[j, rows],
                dst_ref=land[a] if j is None else land[a].at[j],
                send_sem=send.at[k], recv_sem=recv.at[k],
                device_id=_sibling(pos), device_id_type=pl.DeviceIdType.MESH))
        for cp in cps:
            cp.start()
        for cp, (a, j) in zip(cps, pieces):
            cp.wait()
            at = Ellipsis if j is None else j
            outs[a][at] = (own[a][at].astype(F32) + land[a][at].astype(F32)).astype(outs[a].dtype)

    def mine(shape):
        if len(shape) == 3:
            return pl.BlockSpec(shape, lambda i, c_ref: (0, c_ref[0], 0))
        return pl.BlockSpec(shape, lambda i, c_ref: (c_ref[0], 0))

    return _call(
        body, name=name,
        grid_spec=pltpu.PrefetchScalarGridSpec(
            num_scalar_prefetch=1, grid=(1,),
            in_specs=[pl.BlockSpec(memory_space=pl.ANY)] * n + [mine(h) for h in halves],
            out_specs=[pl.BlockSpec(h, lambda i, c_ref, nd=len(h): (0,) * nd) for h in halves],
            scratch_shapes=[pltpu.VMEM(h, a.dtype) for h, a in zip(halves, arrays)]
                           + [pltpu.SemaphoreType.DMA((len(pieces),)), pltpu.SemaphoreType.DMA((len(pieces),))]),
        out_shape=[_sds(h, a.dtype) for h, a in zip(halves, arrays)],
        compiler_params=_params("arbitrary"),
    )(cidx, *arrays, *arrays)


def _to_other_chips(arrays, blocked):
    def src(ins, outs, pos, a, mask):
        return ins[a].at[_chip(pos) ^ mask] if blocked[a] else ins[a]

    outs = [_sds((3,) + (a.shape[1:] if b else a.shape), a.dtype) for a, b in zip(arrays, blocked)]
    copies = []
    for mi, mask in enumerate(CHIP_MASKS):
        for a in range(len(arrays)):
            copies.append((functools.partial(src, a=a, mask=mask), lambda ins, outs, pos, a=a, mi=mi: outs[a].at[mi],
                           functools.partial(_other_chip, mask=mask)))
    return outs, copies


def _sum_chips_swap(name, owns, gots, jidx, blocked):
    n = len(owns)
    shapes = [g.shape[-2:] for g in gots]
    own3 = [o if b else o.reshape((1,) + o.shape) for o, b in zip(owns, blocked)]

    def body(j_ref, *refs):
        del j_ref
        own, got, mine, theirs = refs[:n], refs[n:2 * n], refs[2 * n:3 * n], refs[3 * n:4 * n]
        send, recv = refs[4 * n:]
        pos = (lax.axis_index("x"), lax.axis_index("y"), lax.axis_index("c"))
        cps = []
        for a in range(n):
            mine[a][...] = ((own[a][0].astype(F32) + got[a][0].astype(F32))
                            + (got[a][1].astype(F32) + got[a][2].astype(F32)))
            cps.append(pltpu.make_async_remote_copy(
                src_ref=mine[a], dst_ref=theirs[a], send_sem=send.at[a], recv_sem=recv.at[a],
                device_id=_sibling(pos), device_id_type=pl.DeviceIdType.MESH))
            cps[-1].start()
        for cp in cps:
            cp.wait()

    own_spec = lambda s, b: pl.BlockSpec((1,) + s, (lambda i, j_ref: (j_ref[0], 0, 0)) if b else (lambda i, j_ref: (0, 0, 0)))
    whole = lambda s: pl.BlockSpec(s, lambda i, j_ref: (0, 0))
    outs = _call(
        body, name=name,
        grid_spec=pltpu.PrefetchScalarGridSpec(
            num_scalar_prefetch=1, grid=(1,),
            in_specs=[own_spec(s, b) for s, b in zip(shapes, blocked)]
                     + [pl.BlockSpec((3,) + s, lambda i, j_ref: (0, 0, 0)) for s in shapes],
            out_specs=[whole(s) for s in shapes] * 2,
            scratch_shapes=[pltpu.SemaphoreType.DMA((n,)), pltpu.SemaphoreType.DMA((n,))]),
        out_shape=[_sds(s) for s in shapes] * 2,
        compiler_params=_params("arbitrary"),
    )(jidx, *own3, *gots)
    return outs[:n], outs[n:]


def _local_step(x, target, w_pad, w_out, conv_w, norm_w, pool_w, pool_scale, a_log, dt_bias, dn_norm_w, final_norm_w,
                after):
    proj, n_t, qn, kn, vs, beta, g, yq, yk, yv, y_pool = _front_fwd(
        x, norm_w, w_pad, conv_w, a_log, dt_bias, pool_w, pool_scale, after)
    w, att, qd, kd, tm, cd, o, vn, st = _delta_fwd(qn, kn, vs, beta, g)
    w_out = w_out(o) if callable(w_out) else w_out
    g_wout, dh, dyp, do, ddz, loss, g_fnw, g_dnw = _out_fwd_bwd(x, y_pool, o, proj, target, w_out, dn_norm_w, final_norm_w)
    dqn, dkn, dvs, dbeta, dg = _delta_bwd(do, vn, qd, kd, w, att, cd, st, qn, kn, vs, beta, g, tm)
    (dcq, dck, dcv, dba, g_cw, g_sm), (dpu, dpz, g_pw, g_ps) = _conv_pool_bwd(
        proj, (yq, yk, yv), conv_w, a_log, dt_bias, dqn, dkn, dvs, dbeta, dg, dyp, pool_w, pool_scale)
    pieces = [dpu, dpz, dcq, dck, dcv, ddz, dba]
    g_win = _grad_w_in(n_t, pieces)
    small = dict(norm_w=jnp.zeros_like(norm_w), pool_w=g_pw, pool_scale=g_ps, conv_w=g_cw[:CONV_K],
                 a_log=g_sm[0:1, 0:N_HEADS], dt_bias=g_sm[0:1, N_HEADS:2 * N_HEADS], dn_norm_w=g_dnw, final_norm_w=g_fnw)
    return loss[0, 0], g_win, g_wout, small, dh, pieces


SMALL_LAYOUT = (("pool_w", 512, HEAD, (1, N_HEADS, HEAD, HEAD)), ("final_norm_w", 8, HEAD, (D_MODEL,)),
                ("pool_scale", 4, HEAD, (1, D_HALF)), ("conv_w", 48, HEAD, (1, CONV_K, 3 * D_HALF)),
                ("dn_norm_w", 1, HEAD, (1, HEAD)), ("a_log", 1, N_HEADS, (1, N_HEADS)), ("dt_bias", 1, N_HEADS, (1, N_HEADS)),
                ("loss", 1, 1, ()))


def _small_offsets():
    offs, r = {}, 0
    for name, rows, _, _ in SMALL_LAYOUT:
        offs[name] = r
        r += -(-rows // 8) * 8
    assert r <= SMALL_ROWS
    return offs


def _pack_small(t):
    parts = []
    for name, rows, lanes, _ in SMALL_LAYOUT:
        a = t.get(name, jnp.zeros((1,), F32)).reshape(rows, lanes)
        parts.append(jnp.pad(a, ((0, -(-rows // 8) * 8 - rows), (0, HEAD - lanes))))
    buf = jnp.concatenate(parts, axis=0)
    return jnp.pad(buf, ((0, SMALL_ROWS - buf.shape[0]), (0, 0)))


def _adamw_small(w, g_own, g_got, cidx, m, v):
    offs = _small_offsets()
    names = [e[0] for e in SMALL_LAYOUT]
    n = len(names)

    def body(c_ref, w_ref, go_ref, gg_ref, m_ref, v_ref, *outs):
        own_low = c_ref[0] == 0
        gv = jnp.concatenate([jnp.where(own_low, go_ref[...], gg_ref[...]), jnp.where(own_low, gg_ref[...], go_ref[...])], axis=0)
        mn = ADAM_B1 * m_ref[...] + (1.0 - ADAM_B1) * gv
        vn = ADAM_B2 * v_ref[...] + (1.0 - ADAM_B2) * (gv * gv)
        m_hat = mn / (1.0 - ADAM_B1 ** ADAM_STEP)
        v_hat = vn / (1.0 - ADAM_B2 ** ADAM_STEP)
        dl = -ADAM_LR * (m_hat / (jnp.sqrt(v_hat) + ADAM_EPS) + ADAM_WD * w_ref[...])
        for kind, arr in enumerate((gv, dl, mn, vn)):
            for i, (name, rows, lanes, _) in enumerate(SMALL_LAYOUT):
                outs[kind * n + i][...] = arr[offs[name]:offs[name] + rows, :lanes]

    whole = lambda shape: pl.BlockSpec(shape, lambda i, c_ref: (0,) * len(shape))
    out_shapes = [_sds((rows, lanes)) for _, rows, lanes, _ in SMALL_LAYOUT] * 4
    res = _call(
        body, name="adamw_small",
        grid_spec=pltpu.PrefetchScalarGridSpec(
            num_scalar_prefetch=1, grid=(1,),
            in_specs=[whole(w.shape), whole(g_own.shape), whole(g_got.shape), whole(m.shape), whole(v.shape)],
            out_specs=[whole(o.shape) for o in out_shapes]),
        out_shape=out_shapes,
        compiler_params=_params("arbitrary"),
    )(cidx, w, g_own, g_got, m, v)
    return [{name: res[kind * n + i].reshape(shape) for i, (name, _, _, shape) in enumerate(SMALL_LAYOUT)}
            for kind in range(4)]


def kernel(x, norm_w, w_in, pool_w, pool_scale, conv_w, a_log, dt_bias, dn_norm_w, w_out, final_norm_w, loss_target, m_norm_w, m_w_in, m_pool_w, m_pool_scale, m_conv_w, m_a_log, m_dt_bias, m_dn_norm_w, m_w_out, m_final_norm_w, v_norm_w, v_w_in, v_pool_w, v_pool_scale, v_conv_w, v_a_log, v_dt_bias, v_dn_norm_w, v_w_out, v_final_norm_w):
    cidx = lax.axis_index("c").astype(I32).reshape(1)
    jidx = (2 * lax.axis_index("x") + lax.axis_index("y")).astype(I32)

    wb = jnp.pad(w_in[0].astype(BF16), ((0, 0), (0, BLK_IN_PAD - BLK_IN)))
    ob = w_out[0].astype(BF16)
    gw, cw_full = _gather_weights(wb, conv_w[0])
    w_pad = _assemble_w_in(gw, wb, jidx.reshape(1))

    lands_o, copies_o = _gather_blocks(ob)
    sems_o, ob_thru, zones_o, token_o = _exchange_start("gather_w_out_start", [ob], lands_o, copies_o)

    def w_out_full(after):
        _, (got,) = _exchange_wait("gather_w_out_wait", sems_o, ob_thru, zones_o, copies_o, after)
        return got.reshape(D_MODEL, D_MODEL)

    loss, g_win, g_wout, small, dh, pieces = _local_step(
        x[0], loss_target[0], w_pad, w_out_full, cw_full, norm_w, pool_w[0], pool_scale, a_log, dt_bias,
        dn_norm_w, final_norm_w.reshape(1, D_MODEL), token_o)
    small["loss"] = loss

    blocks_out = g_wout.reshape(4, BLK_OUT, D_MODEL)
    full = [g_win, blocks_out, _pack_small(small)]
    chip_sum = _reduce_sibling("reduce_sibling", full, cidx)
    blocked = [True, True, False]
    lands, copies = _to_other_chips(chip_sum, blocked)
    sems, chip_sum, zones, token = _exchange_start("reduce_chips_start", chip_sum, lands, copies)
    gx, g_nw = _in_bwd(x[0], dh, norm_w, w_pad, pieces, token)
    g_nw = _allreduce_tile("reduce_norm_w", g_nw.reshape(8, HEAD)).reshape(1, D_MODEL)
    chip_sum, from_chips = _exchange_wait("reduce_chips_wait", sems, chip_sum, zones, copies, gx)
    halves, other_halves = _sum_chips_swap("sum_chips_swap", chip_sum, from_chips, jidx.reshape(1), blocked)

    weights = dict(norm_w=norm_w, w_in=w_in, pool_w=pool_w, pool_scale=pool_scale, conv_w=conv_w, a_log=a_log,
                   dt_bias=dt_bias, dn_norm_w=dn_norm_w, w_out=w_out, final_norm_w=final_norm_w)
    ms = dict(norm_w=m_norm_w, w_in=m_w_in, pool_w=m_pool_w, pool_scale=m_pool_scale, conv_w=m_conv_w, a_log=m_a_log,
              dt_bias=m_dt_bias, dn_norm_w=m_dn_norm_w, w_out=m_w_out, final_norm_w=m_final_norm_w)
    vs = dict(norm_w=v_norm_w, w_in=v_w_in, pool_w=v_pool_w, pool_scale=v_pool_scale, conv_w=v_conv_w, a_log=v_a_log,
              dt_bias=v_dt_bias, dn_norm_w=v_dn_norm_w, w_out=v_w_out, final_norm_w=v_final_norm_w)
    names = ["norm_w", "w_in", "pool_w", "pool_scale", "conv_w", "a_log", "dt_bias", "dn_norm_w", "w_out", "final_norm_w"]
    small_names = [n for n in names if n not in ("w_in", "w_out")]

    def pack(t):
        conv = lax.dynamic_update_slice_in_dim(jnp.zeros((CONV_K, 3 * D_HALF), F32), t["conv_w"][0], jidx * BLK_CONV, axis=1)
        return _pack_small({**{n: t[n] for n in small_names if n != "conv_w"}, "conv_w": conv})

    results = [{}, {}, {}, {}]
    to_tiles = lambda a: jnp.transpose(a, (2, 0, 1)).reshape(BLK_IN, 8, HEAD)
    from_tiles = lambda a: jnp.transpose(a, (1, 2, 0)).reshape(1, D_MODEL, BLK_IN)
    lo = jnp.where(cidx[0] == 0, halves[0], other_halves[0])
    hi = jnp.where(cidx[0] == 0, other_halves[0], halves[0])
    g_tiles = jnp.concatenate([lo[:, :BLK_IN].T, hi[:, :BLK_IN].T], axis=1).reshape(BLK_IN, 8, HEAD)
    outs = _adamw_tiles("adamw_w_in", to_tiles(w_in), g_tiles, to_tiles(m_w_in), to_tiles(v_w_in))
    for res, o in zip(results, (g_tiles,) + tuple(outs)):
        res["w_in"] = from_tiles(o)
    outs = _adamw_shard("adamw_w_out", w_out, halves[1], other_halves[1], cidx, m_w_out, v_w_out)
    for res, o in zip(results, outs):
        res["w_out"] = o
    outs = _adamw_small(pack(weights), halves[2], other_halves[2], cidx, pack(ms), pack(vs))
    for res, got in zip(results, outs):
        got["conv_w"] = lax.dynamic_slice_in_dim(got["conv_w"], jidx * BLK_CONV, BLK_CONV, axis=2)
        res.update(got)
    one_tile = lambda a: a.reshape(1, 8, HEAD)
    outs = _adamw_tiles("adamw_norm_w", one_tile(norm_w), one_tile(g_nw), one_tile(m_norm_w), one_tile(v_norm_w))
    for res, o in zip(results, (g_nw,) + tuple(outs)):
        res["norm_w"] = o.reshape(1, D_MODEL)
    grads, delta, new_m, new_v = results

    return (grads["loss"], gx[None], *[grads[n] for n in names], *[delta[n] for n in names],
            *[new_m[n] for n in names], *[new_v[n] for n in names])
```

```python
import functools

import jax
import jax.numpy as jnp
import numpy as np
from jax import lax
from jax.experimental import pallas as pl
from jax.experimental.pallas import tpu as pltpu

F32 = jnp.float32
BF16 = jnp.bfloat16
I32 = jnp.int32

D_MODEL = 1024
D_HALF = 512
N_HEADS = 4
HEAD = 128
CHUNK = 64
PAIR = 2 * CHUNK
WINDOWS = (2, 4, 8, 16)
CONV_K = 4
EPS = 1e-6
N_IN = 3080
N_IN_PAD = 3200
BLK_IN = 770
BLK_IN_PAD = 896
BLK_OUT = 256
BLK_CONV = 384
COL_BA = 3072
QK_SCALE = HEAD ** -0.5
SMALL_ROWS = 608
VMEM_LIMIT = 56 * 1024 * 1024

ADAM_LR = 0.001
ADAM_B1 = 0.9
ADAM_B2 = 0.999
ADAM_EPS = 1e-08
ADAM_WD = 0.01
ADAM_STEP = 10

CHIP_MASKS = (2, 1, 3)
HEADS = range(N_HEADS)
HEAD_COLS = [slice(h * HEAD, (h + 1) * HEAD) for h in HEADS]


def _call(body, **kw):
    return pl.pallas_call(body, **kw)


def _params(*sem):
    return pltpu.CompilerParams(dimension_semantics=sem, vmem_limit_bytes=VMEM_LIMIT)


def _sds(shape, dtype=F32):
    return jax.ShapeDtypeStruct(shape, dtype)


def _bdot(a, b):
    return jnp.dot(a.astype(BF16), b.astype(BF16), preferred_element_type=F32)


def _bdot_nt(a, b):
    return lax.dot_general(a.astype(BF16), b.astype(BF16), (((1,), (1,)), ((), ())), preferred_element_type=F32)


def _bdot_tn(a, b):
    return lax.dot_general(a.astype(BF16), b.astype(BF16), (((0,), (0,)), ((), ())), preferred_element_type=F32)


def _side(a, b):
    return jnp.concatenate([a.astype(BF16), b.astype(BF16)], axis=1)


def _stack(a, b):
    return jnp.concatenate([a.astype(BF16), b.astype(BF16)], axis=0)


def _split(a):
    hi = a.astype(BF16)
    lo = (a - hi.astype(F32)).astype(BF16)
    return hi, lo


def _mask_dot(m, b):
    n = b.shape[1]
    both = jnp.dot(m, jnp.concatenate(_split(b), axis=1), preferred_element_type=F32)
    return both[:, :n] + both[:, n:]


def _sigmoid(x):
    return 0.5 * jnp.tanh(0.5 * x) + 0.5


def _softplus(x):
    return jnp.maximum(x, 0.0) + jnp.log(1.0 + jnp.exp(-jnp.abs(x)))


def _rowsum(x):
    return jnp.sum(x, axis=-1, keepdims=True)


def _colsum(x):
    return jnp.sum(x, axis=0, keepdims=True)


def _shift_down(xv, prev8, k):
    r = pltpu.roll(xv, k, 0)
    q = pltpu.roll(prev8, k, 0)
    row = lax.broadcasted_iota(I32, prev8.shape, 0)
    top = jnp.where(row < k, q, r[0:8])
    return jnp.concatenate([top, r[8:]], axis=0)


def _shift_up(xv, next8, k):
    t = xv.shape[0]
    r = pltpu.roll(xv, t - k, 0)
    q = pltpu.roll(next8, 8 - k, 0)
    row = lax.broadcasted_iota(I32, next8.shape, 0)
    bot = jnp.where(row >= 8 - k, q, r[t - 8:])
    return jnp.concatenate([r[:t - 8], bot], axis=0)


INTRA_PAIRS = 2
UNITS = [(pp, h) for pp in range(INTRA_PAIRS) for h in HEADS]


def _heads_of(ref, rows=PAIR, units=UNITS):
    return [ref[pp * rows:(pp + 1) * rows, HEAD_COLS[h]] for pp, h in units]


def _put_heads_of(ref, vals, rows=PAIR, units=UNITS):
    for (pp, h), v in zip(units, vals):
        ref[pp * rows:(pp + 1) * rows, HEAD_COLS[h]] = v.astype(ref.dtype)


_heads = _heads_of
_put_heads = _put_heads_of


def _each(fn, *lists):
    return [fn(*args) for args in zip(*lists)]


def _pool_bands(t, anti=False):
    r = np.arange(t)[:, None]
    c = np.arange(t + HEAD)[None, :]
    d = (c - r) if anti else (r - c + HEAD)
    return jnp.asarray(np.stack([(d >= 0) & (d < w) for w in WINDOWS]), BF16)


def _pool_mix(u, halo, z, pw, bands, row0):
    t = u[0].shape[0]
    rows = row0 + lax.broadcasted_iota(I32, (t, 1), 0) + 1
    cnt = [jnp.minimum(rows, w).astype(F32) for w in WINDOWS]
    win = _each(lambda b, h, v: _mask_dot(b, jnp.concatenate([h, v], axis=0)), bands, halo, u)
    mix = _each(lambda a, c, v: a / c - v, win, cnt, u)
    mixed = _each(_bdot, mix, pw)
    return mix, mixed, _each(_sigmoid, z), cnt


POOL_T = 256


def _conv_taps(xv, prev8):
    return [_shift_down(xv, prev8, CONV_K - 1 - j) for j in range(CONV_K - 1)] + [xv]


def _conv_pre(taps, cw):
    y = taps[CONV_K - 1] * cw[CONV_K - 1:CONV_K]
    for j in range(CONV_K - 2, -1, -1):
        y = y + taps[j] * cw[j:j + 1]
    return y


CONV_T = 256


def _conv_specs(t, tile_of=lambda i: i):
    tiles = [pl.BlockSpec((t, D_HALF), functools.partial(lambda i, p: (tile_of(i), 2 + p), p=p)) for p in range(3)]
    halos = [pl.BlockSpec((8, D_HALF),
                          functools.partial(lambda i, p: (jnp.maximum(tile_of(i) * (t // 8) - 1, 0), 2 + p), p=p))
             for p in range(3)]
    return tiles + halos


def _pair_masks():
    r = lax.broadcasted_iota(I32, (PAIR, PAIR), 0)
    c = lax.broadcasted_iota(I32, (PAIR, PAIR), 1)
    same = jnp.right_shift(r, 6) == jnp.right_shift(c, 6)
    return same, same & (r >= c), same & (r > c), r == c


def _interleave(*stage_lists):
    live = list(stage_lists)
    while live:
        for gen in list(live):
            try:
                next(gen)
            except StopIteration:
                live.remove(gen)


def _pipelined_step(t, n, leading, trailing):
    @pl.when(t == 0)
    def _():
        _interleave(*leading())

    @pl.when(jnp.logical_and(t > 0, t < n))
    def _():
        _interleave(*leading(), *trailing())

    @pl.when(t == n)
    def _():
        _interleave(*trailing())


def _pair_common_stages(cm, qn, kn, vs, beta, g):
    same, incl, strict, eye = _pair_masks()
    incl_b = incl.astype(BF16)
    first = lax.broadcasted_iota(I32, (PAIR, HEAD), 0) < CHUNK
    cm.update(same=same, incl=incl, strict=strict, eye=eye)
    gc = _each(lambda gv: _mask_dot(incl_b, gv), g)
    q = _each(lambda v: v * QK_SCALE, qn)
    kb = _each(lambda k, b: k * b, kn, beta)
    cm.update(gc=gc, q=q, kb=kb, vb=_each(lambda v, b: v * b, vs, beta))
    yield
    both = _each(lambda a, b, c: _bdot_nt(_stack(a, b), c), kb, q, kn)
    cm.update(kk=[v[:PAIR] for v in both], qk=[v[PAIR:] for v in both])
    gc_row = _each(lambda v: _colsum(jnp.where(eye, v, 0.0)), gc)
    gl = _each(lambda v: jnp.where(first, v[CHUNK - 1:CHUNK], v[PAIR - 1:PAIR]), gc)
    egc = _each(jnp.exp, gc)
    cm.update(gl=gl, egc=egc,
              decay=_each(lambda v, r: jnp.where(incl, jnp.exp(jnp.where(incl, v - r, 0.0)), 0.0), gc, gc_row))
    yield
    cm.update(ekd=_each(lambda a, b: jnp.exp(a - b), gl, gc), cd=_each(jnp.exp, gl),
              kbg=_each(lambda k, e: k * e, kb, egc))
    yield


def _tri_inv_stages(out, a, eye_f):
    p = _each(lambda v: eye_f - v, a)
    x = _each(_bdot, a, a)
    yield
    for it in range(4):
        both = _each(lambda xv, pv: _bdot(xv, _side(pv, xv)), x, p)
        p = _each(lambda pv, b: pv + b[:, :PAIR], p, both)
        x = [b[:, PAIR:] for b in both]
        yield
    out["t"] = _each(lambda pv, xv: pv + _bdot(pv, xv), p, x)
    yield


def _chunk_scalar_spec(pairs=1, index=lambda i: (i, 0)):
    return pl.BlockSpec((16 * pairs, D_HALF), index)


SCAN_PAIRS = 2
SCAN_ROWS = SCAN_PAIRS * PAIR


def _delta_fwd(qn, kn, vs, beta, g):
    s = qn.shape[0]
    n_steps = s // SCAN_ROWS
    n_chunks = s // CHUNK
    assert INTRA_PAIRS == SCAN_PAIRS

    def body(qn_ref, kn_ref, vs_ref, beta_ref, g_ref, w_ref, att_ref, qd_ref, kd_ref, t_ref, cd_ref, o_ref, vn_ref, st_ref,
             state, u_s, w_s, att_s, qd_s, kd_s, cd_s):
        t = pl.program_id(0)

        @pl.when(t == 1)
        def _():
            state[...] = jnp.zeros_like(state)

        cur = lax.rem(t, 2)
        prev = 1 - cur
        cols = list(enumerate(HEAD_COLS))

        def recurrence():
            sm = [state[h] for h in HEADS]
            for ci in range(2 * SCAN_PAIRS):
                rs = slice(ci * CHUNK, (ci + 1) * CHUNK)
                for h in HEADS:
                    st_ref[ci, h] = sm[h]
                both = [_bdot(jnp.concatenate([w_s[prev, rs, sl], qd_s[prev, rs, sl]], axis=0), sm[h]) for h, sl in cols]
                vn = [u_s[prev, rs, sl] - both[h][:CHUNK] for h, sl in cols]
                for h, sl in cols:
                    vn_ref[rs, sl] = vn[h].astype(BF16)
                    o_ref[rs, sl] = both[h][CHUNK:]
                yield
                sm = [sm[h] * cd_s[prev, ci * 8:ci * 8 + 1, sl] + _bdot_tn(kd_s[prev, rs, sl], vn[h]) for h, sl in cols]
                yield
            for h in HEADS:
                state[h] = sm[h]
            for pp in range(SCAN_PAIRS):
                rp = slice(pp * PAIR, (pp + 1) * PAIR)
                intra = [_bdot(att_s[prev, rp, sl], vn_ref[rp, sl]) for sl in HEAD_COLS]
                for h, sl in cols:
                    o_ref[rp, sl] += intra[h]
                yield

        def factors():
            kn = _heads(kn_ref)
            cm = {}
            yield from _pair_common_stages(cm, _heads(qn_ref), kn, _heads(vs_ref), _heads(beta_ref), _heads(g_ref))
            a = _each(lambda kk, d: jnp.where(cm["strict"], kk * d, 0.0), cm["kk"], cm["decay"])
            inv = {}
            yield from _tri_inv_stages(inv, a, cm["eye"].astype(F32))
            tm = inv["t"]
            uw = _each(lambda tv, a, b: _bdot(tv, _side(a, b)), tm, cm["vb"], cm["kbg"])
            res = dict(u=[v[:, :HEAD] for v in uw], w=[v[:, HEAD:] for v in uw],
                       att=_each(lambda a, b: a * b, cm["qk"], cm["decay"]),
                       qd=_each(lambda a, b: a * b, cm["q"], cm["egc"]), kd=_each(lambda a, b: a * b, kn, cm["ekd"]))
            yield
            _put_heads(t_ref, tm)
            for key, out, keep in (("w", w_ref, w_s), ("att", att_ref, att_s), ("qd", qd_ref, qd_s), ("kd", kd_ref, kd_s)):
                _put_heads(out, res[key])
                for (pp, h), v in zip(UNITS, res[key]):
                    keep[cur, pp * PAIR:(pp + 1) * PAIR, HEAD_COLS[h]] = v.astype(BF16)
            for (pp, h), v in zip(UNITS, res["u"]):
                u_s[cur, pp * PAIR:(pp + 1) * PAIR, HEAD_COLS[h]] = v
            for ci in range(2):
                for (pp, h), v in zip(UNITS, cm["cd"]):
                    rows8 = slice(pp * 16 + ci * 8, pp * 16 + (ci + 1) * 8)
                    cd_ref[rows8, HEAD_COLS[h]] = v[ci * CHUNK:ci * CHUNK + 8]
                    cd_s[cur, rows8, HEAD_COLS[h]] = v[ci * CHUNK:ci * CHUNK + 8]
            yield

        _pipelined_step(t, n_steps, lambda: [factors()], lambda: [recurrence()])

    last = n_steps - 1
    now = lambda i: (jnp.minimum(i, last), 0)
    before = lambda i: (jnp.maximum(i - 1, 0), 0)
    rows = lambda index: pl.BlockSpec((SCAN_ROWS, D_HALF), index)
    slot = lambda r, dtype: pltpu.VMEM((2, r, D_HALF), dtype)
    return _call(
        body, name="delta_fwd", grid=(n_steps + 1,),
        in_specs=[rows(now)] * 5,
        out_specs=[rows(now)] * 5 + [_chunk_scalar_spec(SCAN_PAIRS, now), rows(before), rows(before),
                                     pl.BlockSpec((2 * SCAN_PAIRS, N_HEADS, HEAD, HEAD), lambda i: (jnp.maximum(i - 1, 0), 0, 0, 0))],
        out_shape=[_sds((s, D_HALF), BF16)] * 5 + [_sds((s // 8, D_HALF)), _sds((s, D_HALF)), _sds((s, D_HALF), BF16),
                                                  _sds((n_chunks, N_HEADS, HEAD, HEAD))],
        scratch_shapes=[pltpu.VMEM((N_HEADS, HEAD, HEAD), F32), slot(SCAN_ROWS, F32), slot(SCAN_ROWS, BF16),
                        slot(SCAN_ROWS, BF16), slot(SCAN_ROWS, BF16), slot(SCAN_ROWS, BF16), slot(16 * SCAN_PAIRS, F32)],
        compiler_params=_params("arbitrary"),
    )(qn, kn, vs, beta, g)


OUT_T = 512
OUT_ROWS = 256


def _out_fwd_bwd(x, y_pool, o, proj, target, w_out, dn_norm_w, final_norm_w):
    s = x.shape[0]
    t = OUT_T

    def body(x_ref, yp_ref, o_ref, z_ref, tg_ref, wo_ref, dnw_ref, fnw_ref,
             gwo_ref, dh_ref, dyp_ref, do_ref, dz_ref, loss_ref, gfn_ref, gdn_ref, y_ref, yt_ref, gwo_acc):
        @pl.when(pl.program_id(0) == 0)
        def _():
            loss_ref[...] = jnp.zeros_like(loss_ref)
            gfn_ref[...] = jnp.zeros_like(gfn_ref)
            gdn_ref[...] = jnp.zeros_like(gdn_ref)
            gwo_acc[...] = jnp.zeros_like(gwo_acc)

        dnw = dnw_ref[...]
        fnw = fnw_ref[...]

        def stages(rows, lead):
            for _ in range(lead):
                yield
            ypv = yp_ref[rows]
            y_ref[rows, :D_HALF] = ypv.astype(BF16)
            yt_ref[:D_HALF, rows] = ypv.T.astype(BF16)
            keep = []
            for h in HEADS:
                ov = o_ref[rows, HEAD_COLS[h]]
                zv = z_ref[rows, HEAD_COLS[h]]
                ro = lax.rsqrt(jnp.mean(ov * ov, axis=-1, keepdims=True) + EPS)
                ohat = ov * ro
                sg = _sigmoid(zv)
                keep.append((ro, ohat, zv, sg))
                ydn = ohat * dnw * (zv * sg)
                y_ref[rows, D_HALF + h * HEAD:D_HALF + (h + 1) * HEAD] = ydn.astype(BF16)
                yt_ref[D_HALF + h * HEAD:D_HALF + (h + 1) * HEAD, rows] = ydn.T.astype(BF16)
            yield
            hv = x_ref[rows] + jnp.dot(y_ref[rows], wo_ref[...], preferred_element_type=F32)
            yield
            r2 = lax.rsqrt(jnp.mean(hv * hv, axis=-1, keepdims=True) + EPS)
            hhat = hv * r2
            err = hhat * fnw - tg_ref[rows]
            loss_ref[...] += 0.5 * jnp.sum(_rowsum(err * err) * (1.0 / D_MODEL), axis=0, keepdims=True)
            dout = err * (1.0 / D_MODEL)
            gfn_ref[...] += _colsum(dout * hhat)
            dhh = dout * fnw
            dh = r2 * (dhh - hhat * jnp.mean(dhh * hhat, axis=-1, keepdims=True))
            dh_ref[rows] = dh
            yield
            if lead == t // OUT_ROWS - 1:
                gwo_acc[...] += _bdot(yt_ref[...], dh_ref[...])
            dy = _bdot_nt(dh, wo_ref[...])
            yield
            dyp_ref[rows] = dy[:, :D_HALF]
            gdn = jnp.zeros((1, HEAD), F32)
            for h in HEADS:
                ro, ohat, zv, sg = keep[h]
                dyd = dy[:, D_HALF + h * HEAD:D_HALF + (h + 1) * HEAD]
                sz = zv * sg
                dz_ref[rows, HEAD_COLS[h]] = (dyd * ohat * dnw * (sg * (1.0 + zv * (1.0 - sg)))).astype(BF16)
                gdn = gdn + _colsum(dyd * ohat * sz)
                doh = dyd * dnw * sz
                do_ref[rows, HEAD_COLS[h]] = ro * (doh - ohat * jnp.mean(doh * ohat, axis=-1, keepdims=True))
            gdn_ref[...] += gdn
            yield

        _interleave(*[stages(slice(k * OUT_ROWS, (k + 1) * OUT_ROWS), k) for k in range(t // OUT_ROWS)])

        @pl.when(pl.program_id(0) == pl.num_programs(0) - 1)
        def _():
            gwo_ref[...] = gwo_acc[...].astype(BF16)

    wide = pl.BlockSpec((t, D_MODEL), lambda i: (i, 0))
    half = pl.BlockSpec((t, D_HALF), lambda i: (i, 0))
    const = lambda shape: pl.BlockSpec(shape, lambda i: (0,) * len(shape))
    return _call(
        body, name="out_fwd_bwd", grid=(s // t,),
        in_specs=[wide, half, half, pl.BlockSpec((t, D_HALF), lambda i: (i, 5)), wide,
                  const((D_MODEL, D_MODEL)), const((1, HEAD)), const((1, D_MODEL))],
        out_specs=[const((D_MODEL, D_MODEL)), wide, half, half, half,
                   const((1, HEAD)), const((1, D_MODEL)), const((1, HEAD))],
        out_shape=[_sds((D_MODEL, D_MODEL), BF16), _sds((s, D_MODEL)), _sds((s, D_HALF)), _sds((s, D_HALF)),
                   _sds((s, D_HALF), BF16), _sds((1, HEAD)), _sds((1, D_MODEL)), _sds((1, HEAD))],
        scratch_shapes=[pltpu.VMEM((t, D_MODEL), BF16), pltpu.VMEM((D_MODEL, t), BF16), pltpu.VMEM((D_MODEL, D_MODEL), F32)],
        compiler_params=_params("arbitrary"),
    )(x, y_pool, o, proj, target, w_out, dn_norm_w, final_norm_w)


def _grad_w_in(at, pieces):
    m, s = at.shape
    n = len(pieces)
    tn, tk = D_HALF, min(s, 1024)

    def body(a_ref, *refs):
        p_refs, o_ref, acc = refs[:n], refs[n], refs[n + 1]

        @pl.when(pl.program_id(0) == 0)
        def _():
            acc[...] = jnp.zeros_like(acc)

        av = a_ref[...]
        for p in range(n):
            acc[:, p * tn:(p + 1) * tn] += _bdot(av, p_refs[p][...])

        @pl.when(pl.program_id(0) == pl.num_programs(0) - 1)
        def _():
            for j in range(4):
                base = j * BLK_IN // HEAD * HEAD
                win = acc[:, base:base + BLK_IN_PAD]
                if j * BLK_IN > base:
                    win = pltpu.roll(win, BLK_IN_PAD - (j * BLK_IN - base), 1)
                o_ref[j] = win.astype(BF16)

    return _call(
        body, name="grad_w_in", grid=(s // tk,),
        in_specs=[pl.BlockSpec((m, tk), lambda k: (0, k))] + [pl.BlockSpec((tk, tn), lambda k: (k, 0))] * n,
        out_specs=pl.BlockSpec((4, m, BLK_IN_PAD), lambda k: (0, 0, 0)),
        out_shape=_sds((4, m, BLK_IN_PAD), BF16),
        scratch_shapes=[pltpu.VMEM((m, n * tn), F32)],
        compiler_params=_params("arbitrary"),
    )(at, *pieces)


def _delta_bwd(do, vn, qd, kd, w, att, cd, st, qn, kn, vs, beta, g, tm):
    s = do.shape[0]
    n_steps = s // SCAN_ROWS
    assert INTRA_PAIRS == SCAN_PAIRS

    def body(do_ref, vn_ref, qd_ref, kd_ref, w_ref, att_ref, cd_ref, st_ref, qn_ref, kn_ref, vs_ref, beta_ref, g_ref, t_ref,
             dqn_ref, dkn_ref, dvs_ref, dbeta_ref, dg_ref, dstate, du_s, dw_s, datt_s, dqd_s, dkd_s, dcd_s):
        t = pl.program_id(0)

        @pl.when(t == 0)
        def _():
            dstate[...] = jnp.zeros_like(dstate)

        cur = lax.rem(t, 2)
        prev = 1 - cur
        cols = list(enumerate(HEAD_COLS))
        _, incl, _, _ = _pair_masks()

        def recurrence():
            dv_intra = []
            for pp in range(SCAN_PAIRS):
                rp = slice(pp * PAIR, (pp + 1) * PAIR)
                dv_intra.append([_bdot_tn(att_ref[rp, sl], do_ref[rp, sl]) for _, sl in cols])
                for _, sl in cols:
                    datt_s[cur, rp, sl] = jnp.where(incl, _bdot_nt(do_ref[rp, sl], vn_ref[rp, sl]), 0.0)
                yield
            ds = [dstate[h] for h in HEADS]
            for ci in range(2 * SCAN_PAIRS - 1, -1, -1):
                rs = slice(ci * CHUNK, (ci + 1) * CHUNK)
                in_pair = slice((ci % 2) * CHUNK, (ci % 2 + 1) * CHUNK)
                sm = [st_ref[ci, h] for h in HEADS]
                dvn = [dv_intra[ci // 2][h][in_pair] + _bdot(kd_ref[rs, sl], ds[h]) for h, sl in cols]
                dkd = [_bdot_nt(vn_ref[rs, sl], ds[h]) for h, sl in cols]
                dcd = [jnp.broadcast_to(_rowsum(_colsum(ds[h] * sm[h])), (8, HEAD)) for h in HEADS]
                yield
                both = [_bdot_nt(_stack(do_ref[rs, sl], dvn[h]), sm[h]) for h, sl in cols]
                for h, sl in cols:
                    du_s[cur, rs, sl] = dvn[h].astype(BF16)
                    dqd_s[cur, rs, sl] = both[h][:CHUNK]
                    dw_s[cur, rs, sl] = (-both[h][CHUNK:]).astype(BF16)
                    dkd_s[cur, rs, sl] = dkd[h]
                    dcd_s[cur, ci * 8:(ci + 1) * 8, sl] = dcd[h]
                ds = [ds[h] * cd_ref[ci * 8:ci * 8 + 1, sl]
                      + _bdot_tn(_stack(qd_ref[rs, sl], w_ref[rs, sl]), _stack(do_ref[rs, sl], -dvn[h])) for h, sl in cols]
                yield
            for h in HEADS:
                dstate[h] = ds[h]

        def factors(units):
            _heads = functools.partial(_heads_of, units=units)
            _put_heads = functools.partial(_put_heads_of, units=units)
            ones = jnp.ones((2 * PAIR, HEAD), BF16)
            tn = (((0,), (0,)), ((), ()))
            kept = lambda ref, rows=PAIR: [ref[prev, pp * rows:(pp + 1) * rows, HEAD_COLS[h]] for pp, h in units]
            kn, vs, beta = _heads(kn_ref), _heads(vs_ref), _heads(beta_ref)
            cm = {}
            yield from _pair_common_stages(cm, _heads(qn_ref), kn, vs, beta, _heads(g_ref))
            tmv = _heads(t_ref)
            duv, dwv, dattv, dqdv, dkdv = kept(du_s), kept(dw_s), kept(datt_s), kept(dqd_s), kept(dkd_s)
            duw = _each(_side, duv, dwv)
            both = _each(_bdot_tn, tmv, duw)
            dvb, dkbg = [v[:, :HEAD] for v in both], [v[:, HEAD:] for v in both]
            dt = _each(lambda a, b, c: _bdot_nt(a, _side(b, c)), duw, cm["vb"], cm["kbg"])
            yield
            m1 = _each(_bdot_tn, tmv, dt)
            yield
            da = _each(lambda a, b: -jnp.where(cm["strict"], _bdot_nt(a, b), 0.0), m1, tmv)
            yield
            dkk = _each(lambda a, b: a * b, da, cm["decay"])
            dqk = _each(lambda a, b: a * b, dattv, cm["decay"])
            dd = _each(lambda a, b, c, d: a * b + c * d, dkk, cm["kk"], dqk, cm["qk"])
            dkq = _each(_stack, dkk, dqk)
            both = _each(_bdot, dkq, kn)
            dkb = _each(lambda a, c, d: a[:PAIR] + c * d, both, dkbg, cm["egc"])
            dq = _each(lambda a, c, d: a[PAIR:] + c * d, both, dqdv, cm["egc"])
            yield
            dkn = _each(lambda a, b, c: _bdot_tn(a, _stack(b, c)), dkq, cm["kb"], cm["q"])
            dkn = _each(lambda a, b, c, d, e: a + b * c + d * e, dkn, dkdv, cm["ekd"], dkb, beta)
            t_kd = _each(lambda a, b, c: _rowsum(a * b * c), dkdv, kn, cm["ekd"])
            yield
            split = _each(_split, dd)
            rows_dd = [jnp.dot(_side(hi, lo), ones, preferred_element_type=F32) for hi, lo in split]
            cols_dd = [lax.dot_general(_stack(hi, lo), ones, tn, preferred_element_type=F32) for hi, lo in split]
            yield
            dgc = _each(lambda r, c, a, b, e, f, k, tk: r - c + _rowsum(a * b * e) + _rowsum(f * k) - tk,
                        rows_dd, cols_dd, dqdv, cm["q"], cm["egc"], dkbg, cm["kbg"], t_kd)
            same_b = cm["same"].astype(BF16)
            rowi = lax.broadcasted_iota(I32, (PAIR, HEAD), 0)
            dcd = _each(lambda d: jnp.where(rowi < CHUNK, d[0:1], d[8:9]), kept(dcd_s, rows=16))
            dgl = _each(lambda tk, d, c: _mask_dot(same_b, jnp.broadcast_to(tk, (PAIR, HEAD))) + d * c, t_kd, dcd, cm["cd"])
            yield
            is_last = jnp.bitwise_and(rowi, CHUNK - 1) == CHUNK - 1
            dgc = _each(lambda a, b: a + jnp.where(is_last, b, 0.0), dgc, dgl)
            r = lax.broadcasted_iota(I32, (PAIR, PAIR), 0)
            c = lax.broadcasted_iota(I32, (PAIR, PAIR), 1)
            upper_b = (cm["same"] & (r <= c)).astype(BF16)
            _put_heads(dg_ref, _each(lambda v: _mask_dot(upper_b, v), dgc))
            yield
            _put_heads(dbeta_ref, _each(lambda a, b, c, d: jnp.broadcast_to(_rowsum(a * b) + _rowsum(c * d), (PAIR, HEAD)),
                                        dkb, kn, dvb, vs))
            _put_heads(dqn_ref, _each(lambda v: v * QK_SCALE, dq))
            _put_heads(dkn_ref, dkn)
            _put_heads(dvs_ref, _each(lambda a, b: a * b, dvb, beta))
            yield

        _pipelined_step(t, n_steps, lambda: [recurrence()],
                        lambda: [factors(UNITS[pp * N_HEADS:(pp + 1) * N_HEADS]) for pp in range(INTRA_PAIRS)])

    last = n_steps - 1
    now = lambda i: (jnp.maximum(last - i, 0), 0)
    after = lambda i: (jnp.minimum(n_steps - i, last), 0)
    rows = lambda index: pl.BlockSpec((SCAN_ROWS, D_HALF), index)
    slot = lambda r, dtype: pltpu.VMEM((2, r, D_HALF), dtype)
    return _call(
        body, name="delta_bwd", grid=(n_steps + 1,),
        in_specs=[rows(now)] * 6 + [_chunk_scalar_spec(SCAN_PAIRS, now),
                                    pl.BlockSpec((2 * SCAN_PAIRS, N_HEADS, HEAD, HEAD), lambda i: (jnp.maximum(last - i, 0), 0, 0, 0))]
                 + [rows(after)] * 6,
        out_specs=[rows(after)] * 5,
        out_shape=[_sds((s, D_HALF))] * 5,
        scratch_shapes=[pltpu.VMEM((N_HEADS, HEAD, HEAD), F32), slot(SCAN_ROWS, BF16), slot(SCAN_ROWS, BF16),
                        slot(SCAN_ROWS, F32), slot(SCAN_ROWS, F32), slot(SCAN_ROWS, F32), slot(16 * SCAN_PAIRS, F32)],
        compiler_params=_params("arbitrary"),
    )(do, vn, qd, kd, w, att, cd, st, qn, kn, vs, beta, g, tm)


def _fused_call(name, n_steps, parts):
    n_in = [len(p["inputs"]) for p in parts]
    n_out = [len(p["out_shape"]) for p in parts]
    n_scr = [len(p["scratch"]) for p in parts]

    def body(*refs):
        ins, outs, scr = refs[:sum(n_in)], refs[sum(n_in):sum(n_in) + sum(n_out)], refs[sum(n_in) + sum(n_out):]
        gens, a, b, c = [], 0, 0, 0
        for p, ni, no, ns in zip(parts, n_in, n_out, n_scr):
            gens.append(p["stages"](ins[a:a + ni], outs[b:b + no], scr[c:c + ns]))
            a, b, c = a + ni, b + no, c + ns
        _interleave(*gens)

    flat = lambda key: [v for p in parts for v in p[key]]
    res = _call(
        body, name=name, grid=(n_steps,),
        in_specs=flat("in_specs"), out_specs=flat("out_specs"), out_shape=flat("out_shape"),
        scratch_shapes=flat("scratch"),
        compiler_params=_params("arbitrary"),
    )(*flat("inputs"))
    out, b = [], 0
    for no in n_out:
        out.append(res[b:b + no])
        b += no
    return out


def _pool_bwd_part(proj, dyp, pool_w, pool_scale, tile_of, n_tiles):
    s = proj.shape[0]
    t = POOL_T
    hb = t // HEAD
    last = s // HEAD - 1

    def stages(ins, outs, scratch):
        u_ref, z_ref, halo_ref, dy_ref, zn_ref, dyn_ref, pw_ref, ps_ref, band_ref, aband_ref = ins
        du_ref, dz_ref, gpw_ref, gps_ref = outs
        tile = tile_of(pl.program_id(0))

        @pl.when(pl.program_id(0) == 0)
        def _():
            gpw_ref[...] = jnp.zeros_like(gpw_ref)
            gps_ref[...] = jnp.zeros_like(gps_ref)

        live = (tile > 0).astype(F32)
        more = (tile < n_tiles - 1).astype(F32)
        groups = lambda ref: [ref[:, sl] for sl in HEAD_COLS]
        z, ps, dy = groups(z_ref), groups(ps_ref), groups(dy_ref)
        pw = [pw_ref[g] for g in HEADS]
        mix, mixed, sg, cnt = _pool_mix(groups(u_ref), [h * live for h in groups(halo_ref)], z, pw,
                                        [band_ref[g] for g in HEADS], tile * t)
        yield
        sz = _each(lambda a, b: a * b, z, sg)
        for sl, d, m, p, s_, zg in zip(HEAD_COLS, dy, mixed, ps, sg, z):
            dz_ref[:, sl] = (d * m * p * (s_ * (1.0 + zg * (1.0 - s_)))).astype(BF16)
        for sl, d, m, a in zip(HEAD_COLS, dy, mixed, sz):
            gps_ref[:, sl] += _colsum(d * m * a)
        dmixed = _each(lambda d, p, a: d * p * a, dy, ps, sz)
        yield
        for g, gp in enumerate(_each(_bdot_tn, mix, dmixed)):
            gpw_ref[g] += gp
        dmix = _each(_bdot_nt, dmixed, pw)
        yield
        dmix_n = _each(lambda d, p, zn, w_: _bdot_nt(d * more * p * (zn * _sigmoid(zn)), w_),
                       groups(dyn_ref), ps, groups(zn_ref), pw)
        yield
        scaled = [jnp.concatenate([a / c, b * (1.0 / w)], axis=0) for a, c, b, w in zip(dmix, cnt, dmix_n, WINDOWS)]
        du = _each(lambda b, s_, d: _mask_dot(b, s_) - d, [aband_ref[g] for g in HEADS], scaled, dmix)
        for sl, v in zip(HEAD_COLS, du):
            du_ref[:, sl] = v.astype(BF16)
        yield

    tile = lambda col: pl.BlockSpec((t, D_HALF), lambda i: (tile_of(i), col))
    below = lambda col: pl.BlockSpec((HEAD, D_HALF), lambda i: (jnp.minimum((tile_of(i) + 1) * hb, last), col))
    const3 = lambda shape: pl.BlockSpec(shape, lambda i: (0, 0, 0))
    return dict(
        inputs=[proj, proj, proj, dyp, proj, dyp, pool_w, pool_scale, _pool_bands(t), _pool_bands(t, anti=True)],
        in_specs=[tile(0), tile(1), pl.BlockSpec((HEAD, D_HALF), lambda i: (jnp.maximum(tile_of(i) * hb - 1, 0), 0)),
                  tile(0), below(1), below(0), const3((N_HEADS, HEAD, HEAD)), pl.BlockSpec((1, D_HALF), lambda i: (0, 0)),
                  const3((N_HEADS, t, HEAD + t)), const3((N_HEADS, t, HEAD + t))],
        out_specs=[tile(0), tile(0), const3((N_HEADS, HEAD, HEAD)), pl.BlockSpec((1, D_HALF), lambda i: (0, 0))],
        out_shape=[_sds((s, D_HALF), BF16), _sds((s, D_HALF), BF16), _sds((N_HEADS, HEAD, HEAD)), _sds((1, D_HALF))],
        scratch=[], stages=stages)


def _conv_bwd_part(proj, pre, conv_w, a_log, dt_bias, dqn, dkn, dvs, dbeta, dg, tile_of, n_tiles):
    s = proj.shape[0]
    t = CONV_T

    def stages(ins, outs, scratch):
        (q_ref, k_ref, v_ref, yq_ref, yk_ref, yv_ref, ba_ref, cw_ref, al_ref, dtb_ref,
         dqn_ref, dkn_ref, dvs_ref, dbeta_ref, dg_ref) = ins
        oq_ref, ok_ref, ov_ref, dba_ref, gcw_out, gsm_out = outs
        below, gcw_ref, gsm_ref = scratch
        step = pl.program_id(0)

        @pl.when(step == 0)
        def _():
            gcw_ref[...] = jnp.zeros_like(gcw_ref)
            gsm_ref[...] = jnp.zeros_like(gsm_ref)
            below[...] = jnp.zeros_like(below)

        parts = ((q_ref, yq_ref, dqn_ref, oq_ref), (k_ref, yk_ref, dkn_ref, ok_ref), (v_ref, yv_ref, dvs_ref, ov_ref))
        for p, (x_ref, y_ref, d_ref, o_ref) in enumerate(parts):
            for h in HEADS:
                cs = HEAD_COLS[h]
                wide = slice(p * D_HALF + h * HEAD, p * D_HALF + (h + 1) * HEAD)
                cw = cw_ref[:, wide]
                y = y_ref[:, cs]
                sg = _sigmoid(y)
                sv = y * sg
                ds = d_ref[:, cs]
                if p < 2:
                    rn = lax.rsqrt(_rowsum(sv * sv) + EPS)
                    nrm = sv * rn
                    ds = rn * (ds - nrm * _rowsum(ds * nrm))
                dy = ds * (sg * (1.0 + y * (1.0 - sg)))
                nxt = below[:, wide]
                ahead = [dy] + [_shift_up(dy, nxt, sft) for sft in range(1, CONV_K)]
                xv = x_ref[:, cs]
                acc = dy * cw[CONV_K - 1:CONV_K]
                for sft in range(1, CONV_K):
                    acc = acc + ahead[sft] * cw[CONV_K - 1 - sft:CONV_K - sft]
                for j in range(CONV_K):
                    gcw_ref[8 * j:8 * j + 8, wide] += _rows8(xv * ahead[CONV_K - 1 - j])
                o_ref[:, cs] = acc.astype(BF16)
                below[:, wide] = dy[0:8]
                yield

        ba = ba_ref[...]
        lane = lax.broadcasted_iota(I32, (t, HEAD), 1)
        lane8 = lax.broadcasted_iota(I32, (8, HEAD), 1)
        dba = jnp.zeros((t, HEAD), F32)
        gsm = jnp.zeros((8, HEAD), F32)
        for h in HEADS:
            beta = _sigmoid(ba[:, h:h + 1])
            dbeta = dbeta_ref[:, h * HEAD:h * HEAD + 1]
            xg = ba[:, N_HEADS + h:N_HEADS + h + 1] + dtb_ref[0:1, h:h + 1]
            nexp = -jnp.exp(al_ref[0:1, h:h + 1])
            dgv = dg_ref[:, h * HEAD:h * HEAD + 1]
            da = dgv * nexp * _sigmoid(xg)
            dba = dba + jnp.where(lane == h, dbeta * beta * (1.0 - beta), 0.0) + jnp.where(lane == N_HEADS + h, da, 0.0)
            gsm = (gsm + jnp.where(lane8 == h, _rows8(dgv * nexp * _softplus(xg)), 0.0)
                   + jnp.where(lane8 == N_HEADS + h, _rows8(da), 0.0))
        dba_ref[...] = jnp.zeros_like(dba_ref)
        dba_ref[:, :HEAD] = dba.astype(BF16)
        gsm_ref[...] += gsm
        yield

        @pl.when(step == n_tiles - 1)
        def _():
            gcw_out[...] = jnp.zeros_like(gcw_out)
            for j in range(CONV_K):
                gcw_out[j:j + 1, :] = _colsum(gcw_ref[8 * j:8 * j + 8, :])
            gsm_out[...] = jnp.broadcast_to(_colsum(gsm_ref[...]), (8, HEAD))

    row = pl.BlockSpec((t, D_HALF), lambda i: (tile_of(i), 0))
    const = lambda shape: pl.BlockSpec(shape, lambda i: (0, 0))
    return dict(
        inputs=[proj] * 3 + list(pre) + [proj, conv_w, a_log, dt_bias, dqn, dkn, dvs, dbeta, dg],
        in_specs=_conv_specs(t, tile_of)[:3] + [row] * 3
                 + [pl.BlockSpec((t, HEAD), lambda i: (tile_of(i), COL_BA // HEAD)),
                    const((CONV_K, 3 * D_HALF)), const((1, N_HEADS)), const((1, N_HEADS))] + [row] * 5,
        out_specs=[row, row, row, row, const((8, 3 * D_HALF)), const((8, HEAD))],
        out_shape=[_sds((s, D_HALF), BF16)] * 4 + [_sds((8, 3 * D_HALF)), _sds((8, HEAD))],
        scratch=[pltpu.VMEM((8, 3 * D_HALF), F32), pltpu.VMEM((8 * CONV_K, 3 * D_HALF), F32), pltpu.VMEM((8, HEAD), F32)],
        stages=stages)


def _pool_fwd_part(proj, pool_w, pool_scale):
    s = proj.shape[0]
    t = POOL_T
    hb = t // HEAD

    def stages(ins, outs, scratch, tile=None):
        u_ref, z_ref, halo_ref, pw_ref, ps_ref, band_ref = ins
        y_ref, = outs
        i = pl.program_id(0) if tile is None else tile
        live = (i > 0).astype(F32)
        groups = lambda ref: [ref[:, sl] for sl in HEAD_COLS]
        z = groups(z_ref)
        u, halo = groups(u_ref), [h * live for h in groups(halo_ref)]
        yield
        _, mixed, sg, _ = _pool_mix(u, halo, z, [pw_ref[g] for g in HEADS], [band_ref[g] for g in HEADS], i * t)
        yield
        for sl, m, zg, s_ in zip(HEAD_COLS, mixed, z, sg):
            y_ref[:, sl] = m * ps_ref[:, sl] * (zg * s_)
        yield

    const3 = lambda shape: pl.BlockSpec(shape, lambda i: (0, 0, 0))
    return dict(
        inputs=[proj, proj, proj, pool_w, pool_scale, _pool_bands(t)],
        in_specs=[pl.BlockSpec((t, D_HALF), lambda i: (i, 0)), pl.BlockSpec((t, D_HALF), lambda i: (i, 1)),
                  pl.BlockSpec((HEAD, D_HALF), lambda i: (jnp.maximum(i * hb - 1, 0), 0)),
                  const3((N_HEADS, HEAD, HEAD)), pl.BlockSpec((1, D_HALF), lambda i: (0, 0)), const3((N_HEADS, t, HEAD + t))],
        out_specs=[pl.BlockSpec((t, D_HALF), lambda i: (i, 0))], out_shape=[_sds((s, D_HALF))],
        scratch=[], stages=stages)


def _conv_fwd_part(proj, conv_w, a_log, dt_bias):
    s = proj.shape[0]
    t = CONV_T

    def stages(ins, outs, scratch, tile=None):
        q_ref, k_ref, v_ref, hq_ref, hk_ref, hv_ref, ba_ref, cw_ref, al_ref, dtb_ref = ins
        qn_ref, kn_ref, vs_ref, beta_ref, g_ref, yq_ref, yk_ref, yv_ref = outs
        live = ((pl.program_id(0) if tile is None else tile) > 0).astype(F32)
        parts = ((q_ref, hq_ref, qn_ref, yq_ref), (k_ref, hk_ref, kn_ref, yk_ref), (v_ref, hv_ref, vs_ref, yv_ref))
        for p, (x_ref, h_ref, o_ref, y_ref) in enumerate(parts):
            for h in HEADS:
                cs = HEAD_COLS[h]
                taps = _conv_taps(x_ref[:, cs], h_ref[:, cs] * live)
                y = _conv_pre(taps, cw_ref[:, p * D_HALF + h * HEAD:p * D_HALF + (h + 1) * HEAD])
                y_ref[:, cs] = y
                sv = y * _sigmoid(y)
                o_ref[:, cs] = sv if p == 2 else sv * lax.rsqrt(_rowsum(sv * sv) + EPS)
                yield
        ba = ba_ref[...]
        for h in HEADS:
            beta = _sigmoid(ba[:, h:h + 1])
            gl = -jnp.exp(al_ref[0:1, h:h + 1]) * _softplus(ba[:, N_HEADS + h:N_HEADS + h + 1] + dtb_ref[0:1, h:h + 1])
            beta_ref[:, HEAD_COLS[h]] = jnp.broadcast_to(beta, (t, HEAD))
            g_ref[:, HEAD_COLS[h]] = jnp.broadcast_to(gl, (t, HEAD))
        yield

    row = pl.BlockSpec((t, D_HALF), lambda i: (i, 0))
    const = lambda shape: pl.BlockSpec(shape, lambda i: (0, 0))
    return dict(
        inputs=[proj] * 7 + [conv_w, a_log, dt_bias],
        in_specs=_conv_specs(t) + [pl.BlockSpec((t, HEAD), lambda i: (i, COL_BA // HEAD)),
                                   const((CONV_K, 3 * D_HALF)), const((1, N_HEADS)), const((1, N_HEADS))],
        out_specs=[row] * 8, out_shape=[_sds((s, D_HALF))] * 8, scratch=[], stages=stages)


def _front_fwd(x, norm_w, w_pad, conv_w, a_log, dt_bias, pool_w, pool_scale, after):
    s = x.shape[0]
    t = CONV_T
    n_tiles = s // t
    assert POOL_T == CONV_T
    like_proj = _sds((s, N_IN_PAD))
    conv = _conv_fwd_part(like_proj, conv_w, a_log, dt_bias)
    pool = _pool_fwd_part(like_proj, pool_w, pool_scale)
    bands = pool["inputs"][-1]
    mxu_n = 256
    col_bounds = list(range(0, N_IN_PAD, 3 * mxu_n)) + [N_IN_PAD]

    def body(x_ref, nw_ref, w_ref, cw_ref, al_ref, dtb_ref, pw_ref, ps_ref, band_ref, after_ref,
             proj_ref, nt_ref, qn_ref, kn_ref, vs_ref, beta_ref, g_ref, yq_ref, yk_ref, yv_ref, y_ref, prev):
        del after_ref
        i = pl.program_id(0)

        @pl.when(i == 0)
        def _():
            prev[...] = jnp.zeros_like(prev)

        tile = jnp.maximum(i - 1, 0)
        main, above8, above = pl.ds(HEAD, t), pl.ds(HEAD - 8, 8), pl.ds(0, HEAD)
        cols = lambda rows, c0, width=D_HALF: prev.at[rows, pl.ds(c0, width)]
        conv_ins = (cols(main, 2 * D_HALF), cols(main, 3 * D_HALF), cols(main, 4 * D_HALF),
                    cols(above8, 2 * D_HALF), cols(above8, 3 * D_HALF), cols(above8, 4 * D_HALF),
                    cols(main, COL_BA, HEAD), cw_ref, al_ref, dtb_ref)
        pool_ins = (cols(main, 0), cols(main, D_HALF), cols(above, 0), pw_ref, ps_ref, band_ref)

        def projection():
            xv = x_ref[...]
            r = lax.rsqrt(jnp.mean(xv * xv, axis=-1, keepdims=True) + EPS)
            nv = xv * r * nw_ref[...]
            nt_ref[...] = nv.T.astype(BF16)
            nb = nv.astype(BF16)
            yield
            for lo, hi in zip(col_bounds[:-1], col_bounds[1:]):
                proj_ref[:, lo:hi] = jnp.dot(nb, w_ref[:, lo:hi], preferred_element_type=F32)
                yield

        _interleave(projection(),
                    conv["stages"](conv_ins, (qn_ref, kn_ref, vs_ref, beta_ref, g_ref, yq_ref, yk_ref, yv_ref), (), tile),
                    pool["stages"](pool_ins, (y_ref,), (), tile))
        prev[0:HEAD] = prev[t:t + HEAD]
        prev[HEAD:HEAD + t] = proj_ref[...]

    last = n_tiles - 1
    now = lambda i: (jnp.minimum(i, last), 0)
    before = lambda i: (jnp.maximum(i - 1, 0), 0)
    const = lambda a: pl.BlockSpec(a.shape, lambda i: (0,) * a.ndim)
    half = pl.BlockSpec((t, D_HALF), before)
    return _call(
        body, name="front_fwd", grid=(n_tiles + 1,),
        in_specs=[pl.BlockSpec((t, D_MODEL), now), const(norm_w), const(w_pad), const(conv_w), const(a_log), const(dt_bias),
                  const(pool_w), const(pool_scale), const(bands), pl.BlockSpec(memory_space=pl.ANY)],
        out_specs=[pl.BlockSpec((t, N_IN_PAD), now), pl.BlockSpec((D_MODEL, t), lambda i: (0, jnp.minimum(i, last)))]
                  + [half] * 9,
        out_shape=[_sds((s, N_IN_PAD)), _sds((D_MODEL, s), BF16)] + [_sds((s, D_HALF))] * 9,
        scratch_shapes=[pltpu.VMEM((HEAD + t, N_IN_PAD), F32)],
        compiler_params=_params("arbitrary"),
    )(x, norm_w, w_pad, conv_w, a_log, dt_bias, pool_w, pool_scale, bands, after)


def _conv_pool_bwd(proj, pre, conv_w, a_log, dt_bias, dqn, dkn, dvs, dbeta, dg, dyp, pool_w, pool_scale):
    n_tiles = proj.shape[0] // CONV_T
    assert POOL_T == CONV_T
    tile_of = lambda i: n_tiles - 1 - i
    return _fused_call("conv_pool_bwd", n_tiles, [
        _conv_bwd_part(proj, pre, conv_w, a_log, dt_bias, dqn, dkn, dvs, dbeta, dg, tile_of, n_tiles),
        _pool_bwd_part(proj, dyp, pool_w, pool_scale, tile_of, n_tiles)])


def _rows8(x):
    acc = x[0:8]
    for r in range(8, x.shape[0], 8):
        acc = acc + x[r:r + 8]
    return acc


IN_T = 512


def _in_bwd(x, dh, norm_w, w_pad, pieces, after):
    s = x.shape[0]
    t = IN_T
    widths = [D_HALF] * 6 + [N_IN_PAD - COL_BA]

    def body(*refs):
        x_ref, dh_ref, nw_ref, w_ref = refs[:4]
        p_refs = refs[4:4 + len(pieces)]
        gx_ref, gnw_ref = refs[5 + len(pieces):]

        @pl.when(pl.program_id(0) == 0)
        def _():
            gnw_ref[...] = jnp.zeros_like(gnw_ref)

        dn = jnp.zeros((t, D_MODEL), F32)
        col = 0
        for p_ref, wd in zip(p_refs, widths):
            dn = dn + _bdot_nt(p_ref[...], w_ref[:, col:col + wd])
            col += wd
        xv = x_ref[...]
        r = lax.rsqrt(jnp.mean(xv * xv, axis=-1, keepdims=True) + EPS)
        xhat = xv * r
        gnw_ref[...] += _colsum(dn * xhat)
        dxh = dn * nw_ref[...]
        gx_ref[...] = dh_ref[...] + r * (dxh - xhat * jnp.mean(dxh * xhat, axis=-1, keepdims=True))

    wide = pl.BlockSpec((t, D_MODEL), lambda i: (i, 0))
    return _call(
        body, name="in_bwd", grid=(s // t,),
        in_specs=[wide, wide, pl.BlockSpec((1, D_MODEL), lambda i: (0, 0)),
                  pl.BlockSpec((D_MODEL, N_IN_PAD), lambda i: (0, 0))]
                 + [pl.BlockSpec((t, wd), lambda i: (i, 0)) for wd in widths] + [pl.BlockSpec(memory_space=pl.ANY)],
        out_specs=[wide, pl.BlockSpec((1, D_MODEL), lambda i: (0, 0))],
        out_shape=[_sds((s, D_MODEL)), _sds((1, D_MODEL))],
        compiler_params=_params("arbitrary"),
    )(x, dh, norm_w, w_pad, *pieces, after)


def _adamw_shard(name, w, g_own, g_got, cidx, m, v):
    _, r, c = w.shape
    half = r // 2
    rows = 256 if half % 256 == 0 else half
    per_half = half // rows

    def body(c_ref, w_ref, go_ref, gg_ref, m_ref, v_ref, gout_ref, d_ref, nm_ref, nv_ref):
        mine = (pl.program_id(0) // per_half) == c_ref[0]
        gv = jnp.where(mine, go_ref[:, :c], gg_ref[:, :c])
        gout_ref[0] = gv
        mn = ADAM_B1 * m_ref[0] + (1.0 - ADAM_B1) * gv
        vn = ADAM_B2 * v_ref[0] + (1.0 - ADAM_B2) * (gv * gv)
        m_hat = mn / (1.0 - ADAM_B1 ** ADAM_STEP)
        v_hat = vn / (1.0 - ADAM_B2 ** ADAM_STEP)
        d_ref[0] = -ADAM_LR * (m_hat / (jnp.sqrt(v_hat) + ADAM_EPS) + ADAM_WD * w_ref[0])
        nm_ref[0] = mn
        nv_ref[0] = vn

    blk = pl.BlockSpec((1, rows, c), lambda i, c_ref: (0, i, 0))
    gblk = pl.BlockSpec((rows, g_own.shape[1]), lambda i, c_ref: (i % per_half, 0))
    return _call(
        body, name=name,
        grid_spec=pltpu.PrefetchScalarGridSpec(
            num_scalar_prefetch=1, grid=(2 * per_half,),
            in_specs=[blk, gblk, gblk, blk, blk], out_specs=[blk] * 4),
        out_shape=[_sds((1, r, c))] * 4,
        compiler_params=_params("arbitrary"),
    )(cidx, w, g_own, g_got, m, v)


def _adamw_tiles(name, w, g, m, v):
    n = w.shape[0]
    nb = 77 if n % 77 == 0 else n

    def body(w_ref, g_ref, m_ref, v_ref, d_ref, nm_ref, nv_ref):
        gv = g_ref[...]
        mn = ADAM_B1 * m_ref[...] + (1.0 - ADAM_B1) * gv
        vn = ADAM_B2 * v_ref[...] + (1.0 - ADAM_B2) * (gv * gv)
        m_hat = mn / (1.0 - ADAM_B1 ** ADAM_STEP)
        v_hat = vn / (1.0 - ADAM_B2 ** ADAM_STEP)
        d_ref[...] = -ADAM_LR * (m_hat / (jnp.sqrt(v_hat) + ADAM_EPS) + ADAM_WD * w_ref[...])
        nm_ref[...] = mn
        nv_ref[...] = vn

    blk = pl.BlockSpec((nb, 8, HEAD), lambda i: (i, 0, 0))
    return _call(
        body, name=name, grid=(n // nb,),
        in_specs=[blk] * 4, out_specs=[blk] * 3, out_shape=[_sds(w.shape)] * 3,
        compiler_params=_params("arbitrary"),
    )(w, g, m, v)


def _make_copy(src, dst, send, recv, target):
    if target is None:
        return pltpu.make_async_copy(src, dst, recv)
    return pltpu.make_async_remote_copy(src_ref=src, dst_ref=dst, send_sem=send, recv_sem=recv,
                                        device_id=target, device_id_type=pl.DeviceIdType.MESH)


def _exchange(name, inputs, out_shapes, phases):
    n_in = len(inputs)
    n_out = len(out_shapes)
    n_cp = sum(len(p) for p in phases)

    def body(*refs):
        ins, outs = refs[:n_in], refs[n_in:n_in + n_out]
        send, recv = refs[n_in + n_out:]
        pos = (lax.axis_index("x"), lax.axis_index("y"), lax.axis_index("c"))
        k = 0
        for phase in phases:
            cps = []
            for src, dst, target in phase:
                cps.append(_make_copy(src(ins, outs, pos), dst(ins, outs, pos), send.at[k], recv.at[k],
                                      target and target(pos)))
                k += 1
            for cp in cps:
                cp.start()
            for cp in cps:
                cp.wait()

    anyspec = pl.BlockSpec(memory_space=pl.ANY)
    return _call(
        body, name=name,
        in_specs=[anyspec] * n_in, out_specs=[anyspec] * n_out, out_shape=list(out_shapes),
        scratch_shapes=[pltpu.SemaphoreType.DMA((n_cp,)), pltpu.SemaphoreType.DMA((n_cp,))],
    )(*inputs)


def _exchange_start(name, inputs, out_shapes, copies):
    n_in, n_out, n_cp = len(inputs), len(out_shapes), len(copies)

    def body(*refs):
        ins, lands = refs[:n_in], refs[n_in:n_in + n_out]
        sems = refs[n_in + n_out:n_in + n_out + 2 * n_cp]
        token = refs[-1]
        pos = (lax.axis_index("x"), lax.axis_index("y"), lax.axis_index("c"))
        for k, (src, dst, target) in enumerate(copies):
            _make_copy(src(ins, lands, pos), dst(ins, lands, pos), sems[2 * k], sems[2 * k + 1],
                       target and target(pos)).start()
        token[...] = jnp.zeros_like(token)

    hbm = pl.BlockSpec(memory_space=pltpu.HBM)
    sem = pl.BlockSpec(memory_space=pltpu.SEMAPHORE)
    bufs = list(inputs) + [lax.empty(o.shape, o.dtype) for o in out_shapes]
    outs = _call(
        body, name=name,
        out_shape=tuple([pltpu.SemaphoreType.DMA(())] * (2 * n_cp) + [pltpu.HBM(b.shape, b.dtype) for b in bufs]
                        + [_sds((8, HEAD))]),
        in_specs=[hbm] * len(bufs),
        out_specs=tuple([sem] * (2 * n_cp) + [hbm] * len(bufs) + [pl.BlockSpec(memory_space=pltpu.VMEM)]),
        input_output_aliases={i: 2 * n_cp + i for i in range(len(bufs))},
        compiler_params=pltpu.CompilerParams(has_side_effects=pltpu.SideEffectType.DATAFLOW_SIDE_EFFECTING),
    )(*[pltpu.with_memory_space_constraint(b, pltpu.HBM) for b in bufs])
    return outs[:2 * n_cp], outs[2 * n_cp:2 * n_cp + n_in], outs[2 * n_cp + n_in:-1], outs[-1]


def _exchange_wait(name, sems, sources, lands, copies, after):
    n_in, n_out, n_cp = len(sources), len(lands), len(copies)

    def body(*refs):
        ins, zones = refs[:n_in], refs[n_in:n_in + n_out]
        sem_refs = refs[n_in + n_out:n_in + n_out + 2 * n_cp]
        pos = (lax.axis_index("x"), lax.axis_index("y"), lax.axis_index("c"))
        for k, (src, dst, target) in enumerate(copies):
            cp = _make_copy(src(ins, zones, pos), dst(ins, zones, pos), sem_refs[2 * k], sem_refs[2 * k + 1],
                            target and target(pos))
            if target is None:
                cp.wait()
            else:
                cp.wait_send()
                cp.wait_recv()

    hbm = pl.BlockSpec(memory_space=pltpu.HBM)
    sem = pl.BlockSpec(memory_space=pltpu.SEMAPHORE)
    bufs = list(sources) + list(lands)
    outs = _call(
        body, name=name,
        out_shape=tuple(pltpu.HBM(b.shape, b.dtype) for b in bufs),
        in_specs=[hbm] * len(bufs) + [sem] * (2 * n_cp) + [pl.BlockSpec(memory_space=pl.ANY)],
        out_specs=tuple([hbm] * len(bufs)),
        input_output_aliases={i: i for i in range(len(bufs))},
        compiler_params=pltpu.CompilerParams(has_side_effects=pltpu.SideEffectType.DATAFLOW_SIDE_EFFECTING),
    )(*bufs, *sems, after)
    return outs[:n_in], outs[n_in:]


def _allreduce_tile(name, v):
    def body(v_ref, out_ref, slots, send, recv):
        x, y, c = lax.axis_index("x"), lax.axis_index("y"), lax.axis_index("c")
        me = 4 * x + 2 * y + c
        slots[me] = v_ref[...]
        cps = []
        for k in range(1, 8):
            peer = (x ^ (k >> 2), y ^ ((k >> 1) & 1), c ^ (k & 1))
            cps.append(pltpu.make_async_remote_copy(
                src_ref=v_ref, dst_ref=slots.at[me], send_sem=send.at[k - 1], recv_sem=recv.at[k - 1],
                device_id=peer, device_id_type=pl.DeviceIdType.MESH))
        for cp in cps:
            cp.start()
        for cp in cps:
            cp.wait()
        acc = slots[0]
        for i in range(1, 8):
            acc = acc + slots[i]
        out_ref[...] = acc

    vm = pl.BlockSpec(memory_space=pltpu.VMEM)
    return _call(
        body, name=name, in_specs=[vm], out_specs=vm, out_shape=_sds(v.shape),
        scratch_shapes=[pltpu.VMEM((8,) + v.shape, F32), pltpu.SemaphoreType.DMA((7,)), pltpu.SemaphoreType.DMA((7,))],
    )(v)


def _chip(pos):
    return 2 * pos[0] + pos[1]


def _other_chip(pos, mask):
    x, y, c = pos
    return (x ^ (mask >> 1), y ^ (mask & 1), c)


def _sibling(pos):
    return (pos[0], pos[1], 1 - pos[2])


def _gather_weights(wb, cb):
    rows = wb.shape[0] // 2
    x_nb, y_nb, diag = CHIP_MASKS

    def part(pos, mask, quarter=None):
        start = pos[2] * rows if quarter is None else pos[2] * rows + quarter * (rows // 2)
        return lambda outs: outs[0].at[_chip(pos) ^ mask, pl.ds(start, rows if quarter is None else rows // 2)]

    def passed_on(mask, to, quarter=None):
        return (lambda ins, outs, pos: part(pos, mask, quarter)(outs), lambda ins, outs, pos: part(pos, mask, quarter)(outs), to)

    first = [(lambda ins, outs, pos: ins[0].at[pl.ds(pos[2] * rows, rows)], lambda ins, outs, pos: part(pos, 0)(outs),
              functools.partial(_other_chip, mask=mask)) for mask in (x_nb, y_nb)]
    conv_cols = lambda ins, outs, pos: outs[1].at[:, pl.ds(pl.multiple_of(_chip(pos) * cb.shape[1], HEAD), cb.shape[1])]
    first += [(lambda ins, outs, pos: ins[1], conv_cols, functools.partial(_other_chip, mask=mask)) for mask in CHIP_MASKS]
    first += [(lambda ins, outs, pos: ins[1], conv_cols, None)]
    second = [passed_on(x_nb, functools.partial(_other_chip, mask=y_nb), quarter=0),
              passed_on(y_nb, functools.partial(_other_chip, mask=x_nb), quarter=1),
              passed_on(x_nb, _sibling), passed_on(y_nb, _sibling)]
    third = [passed_on(diag, _sibling)]
    return _exchange("gather_weights", [wb, cb],
                     [_sds((4,) + wb.shape, wb.dtype), _sds((cb.shape[0], 4 * cb.shape[1]), cb.dtype)], [first, second, third])


def _assemble_w_in(gw, wb, jidx):
    m = gw.shape[1]

    def body(j_ref, g_ref, wb_ref, o_ref):
        step = pl.program_id(0)

        @pl.when(step == 0)
        def _():
            o_ref[...] = jnp.zeros_like(o_ref)

        blk = jnp.where(step == j_ref[0], wb_ref[...], g_ref[0]).astype(F32)
        lane = lax.broadcasted_iota(I32, (m, BLK_IN_PAD), 1)
        for j in range(4):
            @pl.when(step == j)
            def _(j=j):
                base = j * BLK_IN // HEAD * HEAD
                shift = j * BLK_IN - base
                moved = pltpu.roll(blk, shift, 1) if shift else blk
                window = o_ref[:, base:base + BLK_IN_PAD].astype(F32)
                mine = (lane >= shift) & (lane < shift + BLK_IN)
                o_ref[:, base:base + BLK_IN_PAD] = jnp.where(mine, moved, window).astype(BF16)

    return _call(
        body, name="assemble_w_in",
        grid_spec=pltpu.PrefetchScalarGridSpec(
            num_scalar_prefetch=1, grid=(4,),
            in_specs=[pl.BlockSpec((1, m, BLK_IN_PAD), lambda j, j_ref: (j, 0, 0)),
                      pl.BlockSpec((m, BLK_IN_PAD), lambda j, j_ref: (0, 0))],
            out_specs=pl.BlockSpec((m, N_IN_PAD), lambda j, j_ref: (0, 0))),
        out_shape=_sds((m, N_IN_PAD), BF16),
        compiler_params=_params("arbitrary"),
    )(jidx, gw, wb)


def _gather_blocks(ob):
    copies = [(lambda ins, outs, pos: ins[0], lambda ins, outs, pos: outs[0].at[_chip(pos)],
               functools.partial(_other_chip, mask=mask)) for mask in CHIP_MASKS]
    copies.append((lambda ins, outs, pos: ins[0], lambda ins, outs, pos: outs[0].at[_chip(pos)], None))
    return [_sds((4,) + ob.shape, ob.dtype)], copies


def _reduce_sibling(name, arrays):
    n = len(arrays)
    halves = [a.shape[:-2] + (a.shape[-2] // 2, a.shape[-1]) for a in arrays]
    pieces = [(a, j) for a in range(n) for j in (range(arrays[a].shape[0]) if arrays[a].ndim == 3 else [None])]

    def body(*refs):
        whole, outs = refs[:n], refs[n:2 * n]
        own, land, summed = refs[2 * n:3 * n], refs[3 * n:4 * n], refs[4 * n:5 * n]
        send, recv, loaded, stored = refs[5 * n:]
        pos = (lax.axis_index("x"), lax.axis_index("y"), lax.axis_index("c"))
        block = lambda ref, j, rows=None: ref.at[(() if j is None else (j,)) + (() if rows is None else (rows,))]
        arrive, load = [], []
        for k, (a, j) in enumerate(pieces):
            h = halves[a][-2]
            arrive.append(pltpu.make_async_remote_copy(
                src_ref=block(whole[a], j, pl.ds((1 - pos[2]) * h, h)), dst_ref=block(land[a], j),
                send_sem=send.at[k], recv_sem=recv.at[k],
                device_id=_sibling(pos), device_id_type=pl.DeviceIdType.MESH))
            load.append(pltpu.make_async_copy(block(whole[a], j, pl.ds(pos[2] * h, h)), block(own[a], j), loaded.at[k]))
        for cp in arrive + load:
            cp.start()
        store = []
        for k, (a, j) in enumerate(pieces):
            load[k].wait()
            arrive[k].wait()
            at = Ellipsis if j is None else j
            summed[a][at] = (own[a][at].astype(F32) + land[a][at].astype(F32)).astype(summed[a].dtype)
            store.append(pltpu.make_async_copy(block(summed[a], j), block(outs[a], j), stored.at[k]))
            store[-1].start()
        for cp in store:
            cp.wait()

    vmem = [pltpu.VMEM(h, a.dtype) for h, a in zip(halves, arrays)]
    sems = [pltpu.SemaphoreType.DMA((len(pieces),))] * 4
    return _call(
        body, name=name,
        in_specs=[pl.BlockSpec(memory_space=pl.ANY)] * n, out_specs=[pl.BlockSpec(memory_space=pl.ANY)] * n,
        out_shape=[_sds(h, a.dtype) for h, a in zip(halves, arrays)],
        scratch_shapes=vmem * 3 + sems,
        compiler_params=_params(),
    )(*arrays)


def _to_other_chips(arrays, blocked):
    def src(ins, outs, pos, a, mask):
        return ins[a].at[_chip(pos) ^ mask] if blocked[a] else ins[a]

    outs = [_sds((3,) + (a.shape[1:] if b else a.shape), a.dtype) for a, b in zip(arrays, blocked)]
    copies = []
    for mi, mask in enumerate(CHIP_MASKS):
        for a in range(len(arrays)):
            copies.append((functools.partial(src, a=a, mask=mask), lambda ins, outs, pos, a=a, mi=mi: outs[a].at[mi],
                           functools.partial(_other_chip, mask=mask)))
    return outs, copies


def _sum_chips_swap(name, owns, gots, blocked):
    n = len(owns)
    shapes = [g.shape[-2:] for g in gots]

    def body(*refs):
        own, got, mine, theirs = refs[:n], refs[n:2 * n], refs[2 * n:3 * n], refs[3 * n:4 * n]
        own_v, got_v, sum_v = refs[4 * n:5 * n], refs[5 * n:6 * n], refs[6 * n:7 * n]
        send, recv, loaded, stored = refs[7 * n:]
        pos = (lax.axis_index("x"), lax.axis_index("y"), lax.axis_index("c"))
        load = []
        for a in range(n):
            load.append((pltpu.make_async_copy(own[a].at[_chip(pos)] if blocked[a] else own[a], own_v[a], loaded.at[2 * a]),
                         pltpu.make_async_copy(got[a], got_v[a], loaded.at[2 * a + 1])))
        for pair in load:
            for cp in pair:
                cp.start()
        out = []
        for a in range(n):
            for cp in load[a]:
                cp.wait()
            sum_v[a][...] = ((own_v[a][...].astype(F32) + got_v[a][0].astype(F32))
                             + (got_v[a][1].astype(F32) + got_v[a][2].astype(F32)))
            out.append(pltpu.make_async_remote_copy(
                src_ref=sum_v[a], dst_ref=theirs[a], send_sem=send.at[a], recv_sem=recv.at[a],
                device_id=_sibling(pos), device_id_type=pl.DeviceIdType.MESH))
            out.append(pltpu.make_async_copy(sum_v[a], mine[a], stored.at[a]))
            out[-2].start()
            out[-1].start()
        for cp in out:
            cp.wait()

    outs = _call(
        body, name=name,
        in_specs=[pl.BlockSpec(memory_space=pl.ANY)] * (2 * n), out_specs=[pl.BlockSpec(memory_space=pl.ANY)] * (2 * n),
        out_shape=[_sds(s) for s in shapes] * 2,
        scratch_shapes=[pltpu.VMEM(s, o.dtype) for s, o in zip(shapes, owns)]
                       + [pltpu.VMEM((3,) + s, g.dtype) for s, g in zip(shapes, gots)] + [pltpu.VMEM(s, F32) for s in shapes]
                       + [pltpu.SemaphoreType.DMA((n,)), pltpu.SemaphoreType.DMA((n,)),
                          pltpu.SemaphoreType.DMA((2 * n,)), pltpu.SemaphoreType.DMA((n,))],
        compiler_params=_params(),
    )(*owns, *gots)
    return outs[:n], outs[n:]


def _local_step(x, target, w_pad, w_out, conv_w, norm_w, pool_w, pool_scale, a_log, dt_bias, dn_norm_w, final_norm_w,
                after):
    proj, n_t, qn, kn, vs, beta, g, yq, yk, yv, y_pool = _front_fwd(
        x, norm_w, w_pad, conv_w, a_log, dt_bias, pool_w, pool_scale, after)
    w, att, qd, kd, tm, cd, o, vn, st = _delta_fwd(qn, kn, vs, beta, g)
    w_out = w_out(o) if callable(w_out) else w_out
    g_wout, dh, dyp, do, ddz, loss, g_fnw, g_dnw = _out_fwd_bwd(x, y_pool, o, proj, target, w_out, dn_norm_w, final_norm_w)
    dqn, dkn, dvs, dbeta, dg = _delta_bwd(do, vn, qd, kd, w, att, cd, st, qn, kn, vs, beta, g, tm)
    (dcq, dck, dcv, dba, g_cw, g_sm), (dpu, dpz, g_pw, g_ps) = _conv_pool_bwd(
        proj, (yq, yk, yv), conv_w, a_log, dt_bias, dqn, dkn, dvs, dbeta, dg, dyp, pool_w, pool_scale)
    pieces = [dpu, dpz, dcq, dck, dcv, ddz, dba]
    g_win = _grad_w_in(n_t, pieces)
    small = dict(norm_w=jnp.zeros_like(norm_w), pool_w=g_pw, pool_scale=g_ps, conv_w=g_cw[:CONV_K],
                 a_log=g_sm[0:1, 0:N_HEADS], dt_bias=g_sm[0:1, N_HEADS:2 * N_HEADS], dn_norm_w=g_dnw, final_norm_w=g_fnw)
    return loss[0, 0], g_win, g_wout, small, dh, pieces


SMALL_LAYOUT = (("pool_w", 512, HEAD, (1, N_HEADS, HEAD, HEAD)), ("final_norm_w", 8, HEAD, (D_MODEL,)),
                ("pool_scale", 4, HEAD, (1, D_HALF)), ("conv_w", 48, HEAD, (1, CONV_K, 3 * D_HALF)),
                ("dn_norm_w", 1, HEAD, (1, HEAD)), ("a_log", 1, N_HEADS, (1, N_HEADS)), ("dt_bias", 1, N_HEADS, (1, N_HEADS)),
                ("loss", 1, 1, ()))


def _small_offsets():
    offs, r = {}, 0
    for name, rows, _, _ in SMALL_LAYOUT:
        offs[name] = r
        r += -(-rows // 8) * 8
    assert r <= SMALL_ROWS
    return offs


def _pack_small(t):
    parts = []
    for name, rows, lanes, _ in SMALL_LAYOUT:
        a = t.get(name, jnp.zeros((1,), F32)).reshape(rows, lanes)
        parts.append(jnp.pad(a, ((0, -(-rows // 8) * 8 - rows), (0, HEAD - lanes))))
    buf = jnp.concatenate(parts, axis=0)
    return jnp.pad(buf, ((0, SMALL_ROWS - buf.shape[0]), (0, 0)))


def _adamw_small(w, g_own, g_got, cidx, m, v):
    offs = _small_offsets()
    names = [e[0] for e in SMALL_LAYOUT]
    n = len(names)

    def body(c_ref, w_ref, go_ref, gg_ref, m_ref, v_ref, *outs):
        own_low = c_ref[0] == 0
        gv = jnp.concatenate([jnp.where(own_low, go_ref[...], gg_ref[...]), jnp.where(own_low, gg_ref[...], go_ref[...])], axis=0)
        mn = ADAM_B1 * m_ref[...] + (1.0 - ADAM_B1) * gv
        vn = ADAM_B2 * v_ref[...] + (1.0 - ADAM_B2) * (gv * gv)
        m_hat = mn / (1.0 - ADAM_B1 ** ADAM_STEP)
        v_hat = vn / (1.0 - ADAM_B2 ** ADAM_STEP)
        dl = -ADAM_LR * (m_hat / (jnp.sqrt(v_hat) + ADAM_EPS) + ADAM_WD * w_ref[...])
        for kind, arr in enumerate((gv, dl, mn, vn)):
            for i, (name, rows, lanes, _) in enumerate(SMALL_LAYOUT):
                outs[kind * n + i][...] = arr[offs[name]:offs[name] + rows, :lanes]

    whole = lambda shape: pl.BlockSpec(shape, lambda i, c_ref: (0,) * len(shape))
    out_shapes = [_sds((rows, lanes)) for _, rows, lanes, _ in SMALL_LAYOUT] * 4
    res = _call(
        body, name="adamw_small",
        grid_spec=pltpu.PrefetchScalarGridSpec(
            num_scalar_prefetch=1, grid=(1,),
            in_specs=[whole(w.shape), whole(g_own.shape), whole(g_got.shape), whole(m.shape), whole(v.shape)],
            out_specs=[whole(o.shape) for o in out_shapes]),
        out_shape=out_shapes,
        compiler_params=_params("arbitrary"),
    )(cidx, w, g_own, g_got, m, v)
    return [{name: res[kind * n + i].reshape(shape) for i, (name, _, _, shape) in enumerate(SMALL_LAYOUT)}
            for kind in range(4)]


def kernel(x, norm_w, w_in, pool_w, pool_scale, conv_w, a_log, dt_bias, dn_norm_w, w_out, final_norm_w, loss_target, m_norm_w, m_w_in, m_pool_w, m_pool_scale, m_conv_w, m_a_log, m_dt_bias, m_dn_norm_w, m_w_out, m_final_norm_w, v_norm_w, v_w_in, v_pool_w, v_pool_scale, v_conv_w, v_a_log, v_dt_bias, v_dn_norm_w, v_w_out, v_final_norm_w):
    cidx = lax.axis_index("c").astype(I32).reshape(1)
    jidx = (2 * lax.axis_index("x") + lax.axis_index("y")).astype(I32)

    wb = jnp.pad(w_in[0].astype(BF16), ((0, 0), (0, BLK_IN_PAD - BLK_IN)))
    ob = w_out[0].astype(BF16)
    gw, cw_full = _gather_weights(wb, conv_w[0])
    w_pad = _assemble_w_in(gw, wb, jidx.reshape(1))

    lands_o, copies_o = _gather_blocks(ob)
    sems_o, ob_thru, zones_o, token_o = _exchange_start("gather_w_out_start", [ob], lands_o, copies_o)

    def w_out_full(after):
        _, (got,) = _exchange_wait("gather_w_out_wait", sems_o, ob_thru, zones_o, copies_o, after)
        return got.reshape(D_MODEL, D_MODEL)

    loss, g_win, g_wout, small, dh, pieces = _local_step(
        x[0], loss_target[0], w_pad, w_out_full, cw_full, norm_w, pool_w[0], pool_scale, a_log, dt_bias,
        dn_norm_w, final_norm_w.reshape(1, D_MODEL), token_o)
    small["loss"] = loss

    blocks_out = g_wout.reshape(4, BLK_OUT, D_MODEL)
    full = [g_win, blocks_out, _pack_small(small)]
    chip_sum = _reduce_sibling("reduce_sibling", full)
    blocked = [True, True, False]
    lands, copies = _to_other_chips(chip_sum, blocked)
    sems, chip_sum, zones, token = _exchange_start("reduce_chips_start", chip_sum, lands, copies)
    gx, g_nw = _in_bwd(x[0], dh, norm_w, w_pad, pieces, token)
    g_nw = _allreduce_tile("reduce_norm_w", g_nw.reshape(8, HEAD)).reshape(1, D_MODEL)
    chip_sum, from_chips = _exchange_wait("reduce_chips_wait", sems, chip_sum, zones, copies, gx)
    halves, other_halves = _sum_chips_swap("sum_chips_swap", chip_sum, from_chips, blocked)

    weights = dict(norm_w=norm_w, w_in=w_in, pool_w=pool_w, pool_scale=pool_scale, conv_w=conv_w, a_log=a_log,
                   dt_bias=dt_bias, dn_norm_w=dn_norm_w, w_out=w_out, final_norm_w=final_norm_w)
    ms = dict(norm_w=m_norm_w, w_in=m_w_in, pool_w=m_pool_w, pool_scale=m_pool_scale, conv_w=m_conv_w, a_log=m_a_log,
              dt_bias=m_dt_bias, dn_norm_w=m_dn_norm_w, w_out=m_w_out, final_norm_w=m_final_norm_w)
    vs = dict(norm_w=v_norm_w, w_in=v_w_in, pool_w=v_pool_w, pool_scale=v_pool_scale, conv_w=v_conv_w, a_log=v_a_log,
              dt_bias=v_dt_bias, dn_norm_w=v_dn_norm_w, w_out=v_w_out, final_norm_w=v_final_norm_w)
    names = ["norm_w", "w_in", "pool_w", "pool_scale", "conv_w", "a_log", "dt_bias", "dn_norm_w", "w_out", "final_norm_w"]
    small_names = [n for n in names if n not in ("w_in", "w_out")]

    def pack(t):
        conv = lax.dynamic_update_slice_in_dim(jnp.zeros((CONV_K, 3 * D_HALF), F32), t["conv_w"][0], jidx * BLK_CONV, axis=1)
        return _pack_small({**{n: t[n] for n in small_names if n != "conv_w"}, "conv_w": conv})

    results = [{}, {}, {}, {}]
    to_tiles = lambda a: jnp.transpose(a, (2, 0, 1)).reshape(BLK_IN, 8, HEAD)
    from_tiles = lambda a: jnp.transpose(a, (1, 2, 0)).reshape(1, D_MODEL, BLK_IN)
    lo = jnp.where(cidx[0] == 0, halves[0], other_halves[0])
    hi = jnp.where(cidx[0] == 0, other_halves[0], halves[0])
    g_tiles = jnp.concatenate([lo[:, :BLK_IN].T, hi[:, :BLK_IN].T], axis=1).reshape(BLK_IN, 8, HEAD)
    outs = _adamw_tiles("adamw_w_in", to_tiles(w_in), g_tiles, to_tiles(m_w_in), to_tiles(v_w_in))
    for res, o in zip(results, (g_tiles,) + tuple(outs)):
        res["w_in"] = from_tiles(o)
    outs = _adamw_shard("adamw_w_out", w_out, halves[1], other_halves[1], cidx, m_w_out, v_w_out)
    for res, o in zip(results, outs):
        res["w_out"] = o
    outs = _adamw_small(pack(weights), halves[2], other_halves[2], cidx, pack(ms), pack(vs))
    for res, got in zip(results, outs):
        got["conv_w"] = lax.dynamic_slice_in_dim(got["conv_w"], jidx * BLK_CONV, BLK_CONV, axis=2)
        res.update(got)
    one_tile = lambda a: a.reshape(1, 8, HEAD)
    outs = _adamw_tiles("adamw_norm_w", one_tile(norm_w), one_tile(g_nw), one_tile(m_norm_w), one_tile(v_norm_w))
    for res, o in zip(results, (g_nw,) + tuple(outs)):
        res["norm_w"] = o.reshape(1, D_MODEL)
    grads, delta, new_m, new_v = results

    return (grads["loss"], gx[None], *[grads[n] for n in names], *[delta[n] for n in names],
            *[new_m[n] for n in names], *[new_v[n] for n in names])
```

```python
import functools

import jax
import jax.numpy as jnp
import numpy as np
from jax import lax
from jax.experimental import pallas as pl
from jax.experimental.pallas import tpu as pltpu

F32 = jnp.float32
BF16 = jnp.bfloat16
I32 = jnp.int32

D_MODEL = 1024
D_HALF = 512
N_HEADS = 4
HEAD = 128
CHUNK = 64
PAIR = 2 * CHUNK
WINDOWS = (2, 4, 8, 16)
CONV_K = 4
EPS = 1e-6
N_IN = 3080
N_IN_PAD = 3200
BLK_IN = 770
BLK_IN_PAD = 896
BLK_OUT = 256
BLK_CONV = 384
COL_BA = 3072
QK_SCALE = HEAD ** -0.5
SMALL_ROWS = 608
VMEM_LIMIT = 56 * 1024 * 1024

ADAM_LR = 0.001
ADAM_B1 = 0.9
ADAM_B2 = 0.999
ADAM_EPS = 1e-08
ADAM_WD = 0.01
ADAM_STEP = 10

CHIP_MASKS = (2, 1, 3)
HEADS = range(N_HEADS)
HEAD_COLS = [slice(h * HEAD, (h + 1) * HEAD) for h in HEADS]


def _call(body, **kw):
    return pl.pallas_call(body, **kw)


def _params(*sem):
    return pltpu.CompilerParams(dimension_semantics=sem, vmem_limit_bytes=VMEM_LIMIT)


def _sds(shape, dtype=F32):
    return jax.ShapeDtypeStruct(shape, dtype)


def _bdot(a, b):
    return jnp.dot(a.astype(BF16), b.astype(BF16), preferred_element_type=F32)


def _bdot_nt(a, b):
    return lax.dot_general(a.astype(BF16), b.astype(BF16), (((1,), (1,)), ((), ())), preferred_element_type=F32)


def _bdot_tn(a, b):
    return lax.dot_general(a.astype(BF16), b.astype(BF16), (((0,), (0,)), ((), ())), preferred_element_type=F32)


def _side(a, b):
    return jnp.concatenate([a.astype(BF16), b.astype(BF16)], axis=1)


def _stack(a, b):
    return jnp.concatenate([a.astype(BF16), b.astype(BF16)], axis=0)


def _split(a):
    hi = a.astype(BF16)
    lo = (a - hi.astype(F32)).astype(BF16)
    return hi, lo


def _mask_dot(m, b):
    n = b.shape[1]
    both = jnp.dot(m, jnp.concatenate(_split(b), axis=1), preferred_element_type=F32)
    return both[:, :n] + both[:, n:]


def _sigmoid(x):
    return 0.5 * jnp.tanh(0.5 * x) + 0.5


def _softplus(x):
    return jnp.maximum(x, 0.0) + jnp.log(1.0 + jnp.exp(-jnp.abs(x)))


def _rowsum(x):
    return jnp.sum(x, axis=-1, keepdims=True)


def _colsum(x):
    return jnp.sum(x, axis=0, keepdims=True)


def _shift_down(xv, prev8, k):
    r = pltpu.roll(xv, k, 0)
    q = pltpu.roll(prev8, k, 0)
    row = lax.broadcasted_iota(I32, prev8.shape, 0)
    top = jnp.where(row < k, q, r[0:8])
    return jnp.concatenate([top, r[8:]], axis=0)


def _shift_up(xv, next8, k):
    t = xv.shape[0]
    r = pltpu.roll(xv, t - k, 0)
    q = pltpu.roll(next8, 8 - k, 0)
    row = lax.broadcasted_iota(I32, next8.shape, 0)
    bot = jnp.where(row >= 8 - k, q, r[t - 8:])
    return jnp.concatenate([r[:t - 8], bot], axis=0)


INTRA_PAIRS = 2
UNITS = [(pp, h) for pp in range(INTRA_PAIRS) for h in HEADS]


def _heads_of(ref, rows=PAIR, units=UNITS):
    return [ref[pp * rows:(pp + 1) * rows, HEAD_COLS[h]] for pp, h in units]


def _put_heads_of(ref, vals, rows=PAIR, units=UNITS):
    for (pp, h), v in zip(units, vals):
        ref[pp * rows:(pp + 1) * rows, HEAD_COLS[h]] = v.astype(ref.dtype)


_heads = _heads_of
_put_heads = _put_heads_of


def _each(fn, *lists):
    return [fn(*args) for args in zip(*lists)]


def _pool_bands(t, anti=False):
    r = np.arange(t)[:, None]
    c = np.arange(t + HEAD)[None, :]
    d = (c - r) if anti else (r - c + HEAD)
    return jnp.asarray(np.stack([(d >= 0) & (d < w) for w in WINDOWS]), BF16)


def _pool_mix(u, halo, z, pw, bands, row0):
    t = u[0].shape[0]
    rows = row0 + lax.broadcasted_iota(I32, (t, 1), 0) + 1
    cnt = [jnp.minimum(rows, w).astype(F32) for w in WINDOWS]
    win = _each(lambda b, h, v: _mask_dot(b, jnp.concatenate([h, v], axis=0)), bands, halo, u)
    mix = _each(lambda a, c, v: a / c - v, win, cnt, u)
    mixed = _each(_bdot, mix, pw)
    return mix, mixed, _each(_sigmoid, z), cnt


POOL_T = 256


def _conv_taps(xv, prev8):
    return [_shift_down(xv, prev8, CONV_K - 1 - j) for j in range(CONV_K - 1)] + [xv]


def _conv_pre(taps, cw):
    y = taps[CONV_K - 1] * cw[CONV_K - 1:CONV_K]
    for j in range(CONV_K - 2, -1, -1):
        y = y + taps[j] * cw[j:j + 1]
    return y


CONV_T = 256


def _conv_specs(t, tile_of=lambda i: i):
    tiles = [pl.BlockSpec((t, D_HALF), functools.partial(lambda i, p: (tile_of(i), 2 + p), p=p)) for p in range(3)]
    halos = [pl.BlockSpec((8, D_HALF),
                          functools.partial(lambda i, p: (jnp.maximum(tile_of(i) * (t // 8) - 1, 0), 2 + p), p=p))
             for p in range(3)]
    return tiles + halos


def _pair_masks():
    r = lax.broadcasted_iota(I32, (PAIR, PAIR), 0)
    c = lax.broadcasted_iota(I32, (PAIR, PAIR), 1)
    same = jnp.right_shift(r, 6) == jnp.right_shift(c, 6)
    return same, same & (r >= c), same & (r > c), r == c


def _interleave(*stage_lists):
    live = list(stage_lists)
    while live:
        for gen in list(live):
            try:
                next(gen)
            except StopIteration:
                live.remove(gen)


def _pipelined_step(t, n, leading, trailing):
    @pl.when(t == 0)
    def _():
        _interleave(*leading())

    @pl.when(jnp.logical_and(t > 0, t < n))
    def _():
        _interleave(*leading(), *trailing())

    @pl.when(t == n)
    def _():
        _interleave(*trailing())


def _pair_common_stages(cm, qn, kn, vs, beta, g):
    same, incl, strict, eye = _pair_masks()
    incl_b = incl.astype(BF16)
    first = lax.broadcasted_iota(I32, (PAIR, HEAD), 0) < CHUNK
    cm.update(same=same, incl=incl, strict=strict, eye=eye)
    gc = _each(lambda gv: _mask_dot(incl_b, gv), g)
    q = _each(lambda v: v * QK_SCALE, qn)
    kb = _each(lambda k, b: k * b, kn, beta)
    cm.update(gc=gc, q=q, kb=kb, vb=_each(lambda v, b: v * b, vs, beta))
    yield
    both = _each(lambda a, b, c: _bdot_nt(_stack(a, b), c), kb, q, kn)
    cm.update(kk=[v[:PAIR] for v in both], qk=[v[PAIR:] for v in both])
    gc_row = _each(lambda v: _colsum(jnp.where(eye, v, 0.0)), gc)
    gl = _each(lambda v: jnp.where(first, v[CHUNK - 1:CHUNK], v[PAIR - 1:PAIR]), gc)
    egc = _each(jnp.exp, gc)
    cm.update(gl=gl, egc=egc,
              decay=_each(lambda v, r: jnp.where(incl, jnp.exp(jnp.where(incl, v - r, 0.0)), 0.0), gc, gc_row))
    yield
    cm.update(ekd=_each(lambda a, b: jnp.exp(a - b), gl, gc), cd=_each(jnp.exp, gl),
              kbg=_each(lambda k, e: k * e, kb, egc))
    yield


def _tri_inv_stages(out, a, eye_f):
    p = _each(lambda v: eye_f - v, a)
    x = _each(_bdot, a, a)
    yield
    for it in range(4):
        both = _each(lambda xv, pv: _bdot(xv, _side(pv, xv)), x, p)
        p = _each(lambda pv, b: pv + b[:, :PAIR], p, both)
        x = [b[:, PAIR:] for b in both]
        yield
    out["t"] = _each(lambda pv, xv: pv + _bdot(pv, xv), p, x)
    yield


def _chunk_scalar_spec(pairs=1, index=lambda i: (i, 0)):
    return pl.BlockSpec((16 * pairs, D_HALF), index)


SCAN_PAIRS = 2
SCAN_ROWS = SCAN_PAIRS * PAIR


def _delta_fwd(qn, kn, vs, beta, g):
    s = qn.shape[0]
    n_steps = s // SCAN_ROWS
    n_chunks = s // CHUNK
    assert INTRA_PAIRS == SCAN_PAIRS

    def body(qn_ref, kn_ref, vs_ref, beta_ref, g_ref, w_ref, att_ref, qd_ref, kd_ref, t_ref, cd_ref, o_ref, vn_ref, st_ref,
             state, u_s, w_s, att_s, qd_s, kd_s, cd_s):
        t = pl.program_id(0)

        @pl.when(t == 1)
        def _():
            state[...] = jnp.zeros_like(state)

        cur = lax.rem(t, 2)
        prev = 1 - cur
        cols = list(enumerate(HEAD_COLS))

        def recurrence():
            sm = [state[h] for h in HEADS]
            for ci in range(2 * SCAN_PAIRS):
                rs = slice(ci * CHUNK, (ci + 1) * CHUNK)
                for h in HEADS:
                    st_ref[ci, h] = sm[h]
                both = [_bdot(jnp.concatenate([w_s[prev, rs, sl], qd_s[prev, rs, sl]], axis=0), sm[h]) for h, sl in cols]
                vn = [u_s[prev, rs, sl] - both[h][:CHUNK] for h, sl in cols]
                for h, sl in cols:
                    vn_ref[rs, sl] = vn[h].astype(BF16)
                    o_ref[rs, sl] = both[h][CHUNK:]
                yield
                sm = [sm[h] * cd_s[prev, ci * 8:ci * 8 + 1, sl] + _bdot_tn(kd_s[prev, rs, sl], vn[h]) for h, sl in cols]
                yield
            for h in HEADS:
                state[h] = sm[h]
            for pp in range(SCAN_PAIRS):
                rp = slice(pp * PAIR, (pp + 1) * PAIR)
                intra = [_bdot(att_s[prev, rp, sl], vn_ref[rp, sl]) for sl in HEAD_COLS]
                for h, sl in cols:
                    o_ref[rp, sl] += intra[h]
                yield

        def factors():
            kn = _heads(kn_ref)
            cm = {}
            yield from _pair_common_stages(cm, _heads(qn_ref), kn, _heads(vs_ref), _heads(beta_ref), _heads(g_ref))
            a = _each(lambda kk, d: jnp.where(cm["strict"], kk * d, 0.0), cm["kk"], cm["decay"])
            inv = {}
            yield from _tri_inv_stages(inv, a, cm["eye"].astype(F32))
            tm = inv["t"]
            uw = _each(lambda tv, a, b: _bdot(tv, _side(a, b)), tm, cm["vb"], cm["kbg"])
            res = dict(u=[v[:, :HEAD] for v in uw], w=[v[:, HEAD:] for v in uw],
                       att=_each(lambda a, b: a * b, cm["qk"], cm["decay"]),
                       qd=_each(lambda a, b: a * b, cm["q"], cm["egc"]), kd=_each(lambda a, b: a * b, kn, cm["ekd"]))
            yield
            _put_heads(t_ref, tm)
            for key, out, keep in (("w", w_ref, w_s), ("att", att_ref, att_s), ("qd", qd_ref, qd_s), ("kd", kd_ref, kd_s)):
                _put_heads(out, res[key])
                for (pp, h), v in zip(UNITS, res[key]):
                    keep[cur, pp * PAIR:(pp + 1) * PAIR, HEAD_COLS[h]] = v.astype(BF16)
            for (pp, h), v in zip(UNITS, res["u"]):
                u_s[cur, pp * PAIR:(pp + 1) * PAIR, HEAD_COLS[h]] = v
            for ci in range(2):
                for (pp, h), v in zip(UNITS, cm["cd"]):
                    rows8 = slice(pp * 16 + ci * 8, pp * 16 + (ci + 1) * 8)
                    cd_ref[rows8, HEAD_COLS[h]] = v[ci * CHUNK:ci * CHUNK + 8]
                    cd_s[cur, rows8, HEAD_COLS[h]] = v[ci * CHUNK:ci * CHUNK + 8]
            yield

        _pipelined_step(t, n_steps, lambda: [factors()], lambda: [recurrence()])

    last = n_steps - 1
    now = lambda i: (jnp.minimum(i, last), 0)
    before = lambda i: (jnp.maximum(i - 1, 0), 0)
    rows = lambda index: pl.BlockSpec((SCAN_ROWS, D_HALF), index)
    slot = lambda r, dtype: pltpu.VMEM((2, r, D_HALF), dtype)
    return _call(
        body, name="delta_fwd", grid=(n_steps + 1,),
        in_specs=[rows(now)] * 5,
        out_specs=[rows(now)] * 5 + [_chunk_scalar_spec(SCAN_PAIRS, now), rows(before), rows(before),
                                     pl.BlockSpec((2 * SCAN_PAIRS, N_HEADS, HEAD, HEAD), lambda i: (jnp.maximum(i - 1, 0), 0, 0, 0))],
        out_shape=[_sds((s, D_HALF), BF16)] * 5 + [_sds((s // 8, D_HALF)), _sds((s, D_HALF)), _sds((s, D_HALF), BF16),
                                                  _sds((n_chunks, N_HEADS, HEAD, HEAD))],
        scratch_shapes=[pltpu.VMEM((N_HEADS, HEAD, HEAD), F32), slot(SCAN_ROWS, F32), slot(SCAN_ROWS, BF16),
                        slot(SCAN_ROWS, BF16), slot(SCAN_ROWS, BF16), slot(SCAN_ROWS, BF16), slot(16 * SCAN_PAIRS, F32)],
        compiler_params=_params("arbitrary"),
    )(qn, kn, vs, beta, g)


OUT_T = 512
OUT_ROWS = 256


def _out_fwd_bwd(x, y_pool, o, proj, target, w_out, dn_norm_w, final_norm_w):
    s = x.shape[0]
    t = OUT_T

    def body(x_ref, yp_ref, o_ref, z_ref, tg_ref, wo_ref, dnw_ref, fnw_ref,
             gwo_ref, dh_ref, dyp_ref, do_ref, dz_ref, loss_ref, gfn_ref, gdn_ref, y_ref, yt_ref, gwo_acc):
        @pl.when(pl.program_id(0) == 0)
        def _():
            loss_ref[...] = jnp.zeros_like(loss_ref)
            gfn_ref[...] = jnp.zeros_like(gfn_ref)
            gdn_ref[...] = jnp.zeros_like(gdn_ref)
            gwo_acc[...] = jnp.zeros_like(gwo_acc)

        dnw = dnw_ref[...]
        fnw = fnw_ref[...]

        def stages(rows, lead):
            for _ in range(lead):
                yield
            ypv = yp_ref[rows]
            y_ref[rows, :D_HALF] = ypv.astype(BF16)
            yt_ref[:D_HALF, rows] = ypv.T.astype(BF16)
            keep = []
            for h in HEADS:
                ov = o_ref[rows, HEAD_COLS[h]]
                zv = z_ref[rows, HEAD_COLS[h]]
                ro = lax.rsqrt(jnp.mean(ov * ov, axis=-1, keepdims=True) + EPS)
                ohat = ov * ro
                sg = _sigmoid(zv)
                keep.append((ro, ohat, zv, sg))
                ydn = ohat * dnw * (zv * sg)
                y_ref[rows, D_HALF + h * HEAD:D_HALF + (h + 1) * HEAD] = ydn.astype(BF16)
                yt_ref[D_HALF + h * HEAD:D_HALF + (h + 1) * HEAD, rows] = ydn.T.astype(BF16)
            yield
            hv = x_ref[rows] + jnp.dot(y_ref[rows], wo_ref[...], preferred_element_type=F32)
            yield
            r2 = lax.rsqrt(jnp.mean(hv * hv, axis=-1, keepdims=True) + EPS)
            hhat = hv * r2
            err = hhat * fnw - tg_ref[rows]
            loss_ref[...] += 0.5 * jnp.sum(_rowsum(err * err) * (1.0 / D_MODEL), axis=0, keepdims=True)
            dout = err * (1.0 / D_MODEL)
            gfn_ref[...] += _colsum(dout * hhat)
            dhh = dout * fnw
            dh = r2 * (dhh - hhat * jnp.mean(dhh * hhat, axis=-1, keepdims=True))
            dh_ref[rows] = dh
            yield
            if lead == t // OUT_ROWS - 1:
                gwo_acc[...] += _bdot(yt_ref[...], dh_ref[...])
            dy = _bdot_nt(dh, wo_ref[...])
            yield
            dyp_ref[rows] = dy[:, :D_HALF]
            gdn = jnp.zeros((1, HEAD), F32)
            for h in HEADS:
                ro, ohat, zv, sg = keep[h]
                dyd = dy[:, D_HALF + h * HEAD:D_HALF + (h + 1) * HEAD]
                sz = zv * sg
                dz_ref[rows, HEAD_COLS[h]] = (dyd * ohat * dnw * (sg * (1.0 + zv * (1.0 - sg)))).astype(BF16)
                gdn = gdn + _colsum(dyd * ohat * sz)
                doh = dyd * dnw * sz
                do_ref[rows, HEAD_COLS[h]] = ro * (doh - ohat * jnp.mean(doh * ohat, axis=-1, keepdims=True))
            gdn_ref[...] += gdn
            yield

        _interleave(*[stages(slice(k * OUT_ROWS, (k + 1) * OUT_ROWS), k) for k in range(t // OUT_ROWS)])

        @pl.when(pl.program_id(0) == pl.num_programs(0) - 1)
        def _():
            gwo_ref[...] = gwo_acc[...].astype(BF16)

    wide = pl.BlockSpec((t, D_MODEL), lambda i: (i, 0))
    half = pl.BlockSpec((t, D_HALF), lambda i: (i, 0))
    const = lambda shape: pl.BlockSpec(shape, lambda i: (0,) * len(shape))
    return _call(
        body, name="out_fwd_bwd", grid=(s // t,),
        in_specs=[wide, half, half, pl.BlockSpec((t, D_HALF), lambda i: (i, 5)), wide,
                  const((D_MODEL, D_MODEL)), const((1, HEAD)), const((1, D_MODEL))],
        out_specs=[const((D_MODEL, D_MODEL)), wide, half, half, half,
                   const((1, HEAD)), const((1, D_MODEL)), const((1, HEAD))],
        out_shape=[_sds((D_MODEL, D_MODEL), BF16), _sds((s, D_MODEL)), _sds((s, D_HALF)), _sds((s, D_HALF)),
                   _sds((s, D_HALF), BF16), _sds((1, HEAD)), _sds((1, D_MODEL)), _sds((1, HEAD))],
        scratch_shapes=[pltpu.VMEM((t, D_MODEL), BF16), pltpu.VMEM((D_MODEL, t), BF16), pltpu.VMEM((D_MODEL, D_MODEL), F32)],
        compiler_params=_params("arbitrary"),
    )(x, y_pool, o, proj, target, w_out, dn_norm_w, final_norm_w)


def _grad_w_in(at, pieces):
    m, s = at.shape
    n = len(pieces)
    tn, tk = D_HALF, min(s, 1024)

    def body(a_ref, *refs):
        p_refs, o_ref, acc = refs[:n], refs[n], refs[n + 1]

        @pl.when(pl.program_id(0) == 0)
        def _():
            acc[...] = jnp.zeros_like(acc)

        av = a_ref[...]
        for p in range(n):
            acc[:, p * tn:(p + 1) * tn] += _bdot(av, p_refs[p][...])

        @pl.when(pl.program_id(0) == pl.num_programs(0) - 1)
        def _():
            for j in range(4):
                base = j * BLK_IN // HEAD * HEAD
                win = acc[:, base:base + BLK_IN_PAD]
                if j * BLK_IN > base:
                    win = pltpu.roll(win, BLK_IN_PAD - (j * BLK_IN - base), 1)
                o_ref[j] = win.astype(BF16)

    return _call(
        body, name="grad_w_in", grid=(s // tk,),
        in_specs=[pl.BlockSpec((m, tk), lambda k: (0, k))] + [pl.BlockSpec((tk, tn), lambda k: (k, 0))] * n,
        out_specs=pl.BlockSpec((4, m, BLK_IN_PAD), lambda k: (0, 0, 0)),
        out_shape=_sds((4, m, BLK_IN_PAD), BF16),
        scratch_shapes=[pltpu.VMEM((m, n * tn), F32)],
        compiler_params=_params("arbitrary"),
    )(at, *pieces)


def _delta_bwd(do, vn, qd, kd, w, att, cd, st, qn, kn, vs, beta, g, tm):
    s = do.shape[0]
    n_steps = s // SCAN_ROWS
    assert INTRA_PAIRS == SCAN_PAIRS

    def body(do_ref, vn_ref, qd_ref, kd_ref, w_ref, att_ref, cd_ref, st_ref, qn_ref, kn_ref, vs_ref, beta_ref, g_ref, t_ref,
             dqn_ref, dkn_ref, dvs_ref, dbeta_ref, dg_ref, dstate, du_s, dw_s, datt_s, dqd_s, dkd_s, dcd_s):
        t = pl.program_id(0)

        @pl.when(t == 0)
        def _():
            dstate[...] = jnp.zeros_like(dstate)

        cur = lax.rem(t, 2)
        prev = 1 - cur
        cols = list(enumerate(HEAD_COLS))
        _, incl, _, _ = _pair_masks()

        def recurrence():
            dv_intra = []
            for pp in range(SCAN_PAIRS):
                rp = slice(pp * PAIR, (pp + 1) * PAIR)
                dv_intra.append([_bdot_tn(att_ref[rp, sl], do_ref[rp, sl]) for _, sl in cols])
                for _, sl in cols:
                    datt_s[cur, rp, sl] = jnp.where(incl, _bdot_nt(do_ref[rp, sl], vn_ref[rp, sl]), 0.0)
                yield
            ds = [dstate[h] for h in HEADS]
            for ci in range(2 * SCAN_PAIRS - 1, -1, -1):
                rs = slice(ci * CHUNK, (ci + 1) * CHUNK)
                in_pair = slice((ci % 2) * CHUNK, (ci % 2 + 1) * CHUNK)
                sm = [st_ref[ci, h] for h in HEADS]
                dvn = [dv_intra[ci // 2][h][in_pair] + _bdot(kd_ref[rs, sl], ds[h]) for h, sl in cols]
                dkd = [_bdot_nt(vn_ref[rs, sl], ds[h]) for h, sl in cols]
                dcd = [jnp.broadcast_to(_rowsum(_colsum(ds[h] * sm[h])), (8, HEAD)) for h in HEADS]
                yield
                both = [_bdot_nt(_stack(do_ref[rs, sl], dvn[h]), sm[h]) for h, sl in cols]
                for h, sl in cols:
                    du_s[cur, rs, sl] = dvn[h].astype(BF16)
                    dqd_s[cur, rs, sl] = both[h][:CHUNK]
                    dw_s[cur, rs, sl] = (-both[h][CHUNK:]).astype(BF16)
                    dkd_s[cur, rs, sl] = dkd[h]
                    dcd_s[cur, ci * 8:(ci + 1) * 8, sl] = dcd[h]
                ds = [ds[h] * cd_ref[ci * 8:ci * 8 + 1, sl]
                      + _bdot_tn(_stack(qd_ref[rs, sl], w_ref[rs, sl]), _stack(do_ref[rs, sl], -dvn[h])) for h, sl in cols]
                yield
            for h in HEADS:
                dstate[h] = ds[h]

        def factors(units):
            _heads = functools.partial(_heads_of, units=units)
            _put_heads = functools.partial(_put_heads_of, units=units)
            ones = jnp.ones((2 * PAIR, HEAD), BF16)
            tn = (((0,), (0,)), ((), ()))
            kept = lambda ref, rows=PAIR: [ref[prev, pp * rows:(pp + 1) * rows, HEAD_COLS[h]] for pp, h in units]
            kn, vs, beta = _heads(kn_ref), _heads(vs_ref), _heads(beta_ref)
            cm = {}
            yield from _pair_common_stages(cm, _heads(qn_ref), kn, vs, beta, _heads(g_ref))
            tmv = _heads(t_ref)
            duv, dwv, dattv, dqdv, dkdv = kept(du_s), kept(dw_s), kept(datt_s), kept(dqd_s), kept(dkd_s)
            duw = _each(_side, duv, dwv)
            both = _each(_bdot_tn, tmv, duw)
            dvb, dkbg = [v[:, :HEAD] for v in both], [v[:, HEAD:] for v in both]
            dt = _each(lambda a, b, c: _bdot_nt(a, _side(b, c)), duw, cm["vb"], cm["kbg"])
            yield
            m1 = _each(_bdot_tn, tmv, dt)
            yield
            da = _each(lambda a, b: -jnp.where(cm["strict"], _bdot_nt(a, b), 0.0), m1, tmv)
            yield
            dkk = _each(lambda a, b: a * b, da, cm["decay"])
            dqk = _each(lambda a, b: a * b, dattv, cm["decay"])
            dd = _each(lambda a, b, c, d: a * b + c * d, dkk, cm["kk"], dqk, cm["qk"])
            dkq = _each(_stack, dkk, dqk)
            both = _each(_bdot, dkq, kn)
            dkb = _each(lambda a, c, d: a[:PAIR] + c * d, both, dkbg, cm["egc"])
            dq = _each(lambda a, c, d: a[PAIR:] + c * d, both, dqdv, cm["egc"])
            yield
            dkn = _each(lambda a, b, c: _bdot_tn(a, _stack(b, c)), dkq, cm["kb"], cm["q"])
            dkn = _each(lambda a, b, c, d, e: a + b * c + d * e, dkn, dkdv, cm["ekd"], dkb, beta)
            t_kd = _each(lambda a, b, c: _rowsum(a * b * c), dkdv, kn, cm["ekd"])
            yield
            split = _each(_split, dd)
            rows_dd = [jnp.dot(_side(hi, lo), ones, preferred_element_type=F32) for hi, lo in split]
            cols_dd = [lax.dot_general(_stack(hi, lo), ones, tn, preferred_element_type=F32) for hi, lo in split]
            yield
            dgc = _each(lambda r, c, a, b, e, f, k, tk: r - c + _rowsum(a * b * e) + _rowsum(f * k) - tk,
                        rows_dd, cols_dd, dqdv, cm["q"], cm["egc"], dkbg, cm["kbg"], t_kd)
            same_b = cm["same"].astype(BF16)
            rowi = lax.broadcasted_iota(I32, (PAIR, HEAD), 0)
            dcd = _each(lambda d: jnp.where(rowi < CHUNK, d[0:1], d[8:9]), kept(dcd_s, rows=16))
            dgl = _each(lambda tk, d, c: _mask_dot(same_b, jnp.broadcast_to(tk, (PAIR, HEAD))) + d * c, t_kd, dcd, cm["cd"])
            yield
            is_last = jnp.bitwise_and(rowi, CHUNK - 1) == CHUNK - 1
            dgc = _each(lambda a, b: a + jnp.where(is_last, b, 0.0), dgc, dgl)
            r = lax.broadcasted_iota(I32, (PAIR, PAIR), 0)
            c = lax.broadcasted_iota(I32, (PAIR, PAIR), 1)
            upper_b = (cm["same"] & (r <= c)).astype(BF16)
            _put_heads(dg_ref, _each(lambda v: _mask_dot(upper_b, v), dgc))
            yield
            _put_heads(dbeta_ref, _each(lambda a, b, c, d: jnp.broadcast_to(_rowsum(a * b) + _rowsum(c * d), (PAIR, HEAD)),
                                        dkb, kn, dvb, vs))
            _put_heads(dqn_ref, _each(lambda v: v * QK_SCALE, dq))
            _put_heads(dkn_ref, dkn)
            _put_heads(dvs_ref, _each(lambda a, b: a * b, dvb, beta))
            yield

        _pipelined_step(t, n_steps, lambda: [recurrence()],
                        lambda: [factors(UNITS[pp * N_HEADS:(pp + 1) * N_HEADS]) for pp in range(INTRA_PAIRS)])

    last = n_steps - 1
    now = lambda i: (jnp.maximum(last - i, 0), 0)
    after = lambda i: (jnp.minimum(n_steps - i, last), 0)
    rows = lambda index: pl.BlockSpec((SCAN_ROWS, D_HALF), index)
    slot = lambda r, dtype: pltpu.VMEM((2, r, D_HALF), dtype)
    return _call(
        body, name="delta_bwd", grid=(n_steps + 1,),
        in_specs=[rows(now)] * 6 + [_chunk_scalar_spec(SCAN_PAIRS, now),
                                    pl.BlockSpec((2 * SCAN_PAIRS, N_HEADS, HEAD, HEAD), lambda i: (jnp.maximum(last - i, 0), 0, 0, 0))]
                 + [rows(after)] * 6,
        out_specs=[rows(after)] * 5,
        out_shape=[_sds((s, D_HALF))] * 5,
        scratch_shapes=[pltpu.VMEM((N_HEADS, HEAD, HEAD), F32), slot(SCAN_ROWS, BF16), slot(SCAN_ROWS, BF16),
                        slot(SCAN_ROWS, F32), slot(SCAN_ROWS, F32), slot(SCAN_ROWS, F32), slot(16 * SCAN_PAIRS, F32)],
        compiler_params=_params("arbitrary"),
    )(do, vn, qd, kd, w, att, cd, st, qn, kn, vs, beta, g, tm)


def _fused_call(name, n_steps, parts):
    n_in = [len(p["inputs"]) for p in parts]
    n_out = [len(p["out_shape"]) for p in parts]
    n_scr = [len(p["scratch"]) for p in parts]

    def body(*refs):
        ins, outs, scr = refs[:sum(n_in)], refs[sum(n_in):sum(n_in) + sum(n_out)], refs[sum(n_in) + sum(n_out):]
        gens, a, b, c = [], 0, 0, 0
        for p, ni, no, ns in zip(parts, n_in, n_out, n_scr):
            gens.append(p["stages"](ins[a:a + ni], outs[b:b + no], scr[c:c + ns]))
            a, b, c = a + ni, b + no, c + ns
        _interleave(*gens)

    flat = lambda key: [v for p in parts for v in p[key]]
    res = _call(
        body, name=name, grid=(n_steps,),
        in_specs=flat("in_specs"), out_specs=flat("out_specs"), out_shape=flat("out_shape"),
        scratch_shapes=flat("scratch"),
        compiler_params=_params("arbitrary"),
    )(*flat("inputs"))
    out, b = [], 0
    for no in n_out:
        out.append(res[b:b + no])
        b += no
    return out


def _pool_bwd_part(proj, dyp, pool_w, pool_scale, tile_of, n_tiles):
    s = proj.shape[0]
    t = POOL_T
    hb = t // HEAD
    last = s // HEAD - 1

    def stages(ins, outs, scratch):
        u_ref, z_ref, halo_ref, dy_ref, zn_ref, dyn_ref, pw_ref, ps_ref, band_ref, aband_ref = ins
        du_ref, dz_ref, gpw_ref, gps_ref = outs
        tile = tile_of(pl.program_id(0))

        @pl.when(pl.program_id(0) == 0)
        def _():
            gpw_ref[...] = jnp.zeros_like(gpw_ref)
            gps_ref[...] = jnp.zeros_like(gps_ref)

        live = (tile > 0).astype(F32)
        more = (tile < n_tiles - 1).astype(F32)
        groups = lambda ref: [ref[:, sl] for sl in HEAD_COLS]
        z, ps, dy = groups(z_ref), groups(ps_ref), groups(dy_ref)
        pw = [pw_ref[g] for g in HEADS]
        mix, mixed, sg, cnt = _pool_mix(groups(u_ref), [h * live for h in groups(halo_ref)], z, pw,
                                        [band_ref[g] for g in HEADS], tile * t)
        yield
        sz = _each(lambda a, b: a * b, z, sg)
        for sl, d, m, p, s_, zg in zip(HEAD_COLS, dy, mixed, ps, sg, z):
            dz_ref[:, sl] = (d * m * p * (s_ * (1.0 + zg * (1.0 - s_)))).astype(BF16)
        for sl, d, m, a in zip(HEAD_COLS, dy, mixed, sz):
            gps_ref[:, sl] += _colsum(d * m * a)
        dmixed = _each(lambda d, p, a: d * p * a, dy, ps, sz)
        yield
        for g, gp in enumerate(_each(_bdot_tn, mix, dmixed)):
            gpw_ref[g] += gp
        dmix = _each(_bdot_nt, dmixed, pw)
        yield
        dmix_n = _each(lambda d, p, zn, w_: _bdot_nt(d * more * p * (zn * _sigmoid(zn)), w_),
                       groups(dyn_ref), ps, groups(zn_ref), pw)
        yield
        scaled = [jnp.concatenate([a / c, b * (1.0 / w)], axis=0) for a, c, b, w in zip(dmix, cnt, dmix_n, WINDOWS)]
        du = _each(lambda b, s_, d: _mask_dot(b, s_) - d, [aband_ref[g] for g in HEADS], scaled, dmix)
        for sl, v in zip(HEAD_COLS, du):
            du_ref[:, sl] = v.astype(BF16)
        yield

    tile = lambda col: pl.BlockSpec((t, D_HALF), lambda i: (tile_of(i), col))
    below = lambda col: pl.BlockSpec((HEAD, D_HALF), lambda i: (jnp.minimum((tile_of(i) + 1) * hb, last), col))
    const3 = lambda shape: pl.BlockSpec(shape, lambda i: (0, 0, 0))
    return dict(
        inputs=[proj, proj, proj, dyp, proj, dyp, pool_w, pool_scale, _pool_bands(t), _pool_bands(t, anti=True)],
        in_specs=[tile(0), tile(1), pl.BlockSpec((HEAD, D_HALF), lambda i: (jnp.maximum(tile_of(i) * hb - 1, 0), 0)),
                  tile(0), below(1), below(0), const3((N_HEADS, HEAD, HEAD)), pl.BlockSpec((1, D_HALF), lambda i: (0, 0)),
                  const3((N_HEADS, t, HEAD + t)), const3((N_HEADS, t, HEAD + t))],
        out_specs=[tile(0), tile(0), const3((N_HEADS, HEAD, HEAD)), pl.BlockSpec((1, D_HALF), lambda i: (0, 0))],
        out_shape=[_sds((s, D_HALF), BF16), _sds((s, D_HALF), BF16), _sds((N_HEADS, HEAD, HEAD)), _sds((1, D_HALF))],
        scratch=[], stages=stages)


def _conv_bwd_part(proj, pre, conv_w, a_log, dt_bias, dqn, dkn, dvs, dbeta, dg, tile_of, n_tiles):
    s = proj.shape[0]
    t = CONV_T

    def stages(ins, outs, scratch):
        (q_ref, k_ref, v_ref, yq_ref, yk_ref, yv_ref, ba_ref, cw_ref, al_ref, dtb_ref,
         dqn_ref, dkn_ref, dvs_ref, dbeta_ref, dg_ref) = ins
        oq_ref, ok_ref, ov_ref, dba_ref, gcw_out, gsm_out = outs
        below, gcw_ref, gsm_ref = scratch
        step = pl.program_id(0)

        @pl.when(step == 0)
        def _():
            gcw_ref[...] = jnp.zeros_like(gcw_ref)
            gsm_ref[...] = jnp.zeros_like(gsm_ref)
            below[...] = jnp.zeros_like(below)

        parts = ((q_ref, yq_ref, dqn_ref, oq_ref), (k_ref, yk_ref, dkn_ref, ok_ref), (v_ref, yv_ref, dvs_ref, ov_ref))
        for p, (x_ref, y_ref, d_ref, o_ref) in enumerate(parts):
            for h in HEADS:
                cs = HEAD_COLS[h]
                wide = slice(p * D_HALF + h * HEAD, p * D_HALF + (h + 1) * HEAD)
                cw = cw_ref[:, wide]
                y = y_ref[:, cs]
                sg = _sigmoid(y)
                sv = y * sg
                ds = d_ref[:, cs]
                if p < 2:
                    rn = lax.rsqrt(_rowsum(sv * sv) + EPS)
                    nrm = sv * rn
                    ds = rn * (ds - nrm * _rowsum(ds * nrm))
                dy = ds * (sg * (1.0 + y * (1.0 - sg)))
                nxt = below[:, wide]
                ahead = [dy] + [_shift_up(dy, nxt, sft) for sft in range(1, CONV_K)]
                xv = x_ref[:, cs]
                acc = dy * cw[CONV_K - 1:CONV_K]
                for sft in range(1, CONV_K):
                    acc = acc + ahead[sft] * cw[CONV_K - 1 - sft:CONV_K - sft]
                for j in range(CONV_K):
                    gcw_ref[8 * j:8 * j + 8, wide] += _rows8(xv * ahead[CONV_K - 1 - j])
                o_ref[:, cs] = acc.astype(BF16)
                below[:, wide] = dy[0:8]
                yield

        ba = ba_ref[...]
        lane = lax.broadcasted_iota(I32, (t, HEAD), 1)
        lane8 = lax.broadcasted_iota(I32, (8, HEAD), 1)
        dba = jnp.zeros((t, HEAD), F32)
        gsm = jnp.zeros((8, HEAD), F32)
        for h in HEADS:
            beta = _sigmoid(ba[:, h:h + 1])
            dbeta = dbeta_ref[:, h * HEAD:h * HEAD + 1]
            xg = ba[:, N_HEADS + h:N_HEADS + h + 1] + dtb_ref[0:1, h:h + 1]
            nexp = -jnp.exp(al_ref[0:1, h:h + 1])
            dgv = dg_ref[:, h * HEAD:h * HEAD + 1]
            da = dgv * nexp * _sigmoid(xg)
            dba = dba + jnp.where(lane == h, dbeta * beta * (1.0 - beta), 0.0) + jnp.where(lane == N_HEADS + h, da, 0.0)
            gsm = (gsm + jnp.where(lane8 == h, _rows8(dgv * nexp * _softplus(xg)), 0.0)
                   + jnp.where(lane8 == N_HEADS + h, _rows8(da), 0.0))
        dba_ref[...] = jnp.zeros_like(dba_ref)
        dba_ref[:, :HEAD] = dba.astype(BF16)
        gsm_ref[...] += gsm
        yield

        @pl.when(step == n_tiles - 1)
        def _():
            gcw_out[...] = jnp.zeros_like(gcw_out)
            for j in range(CONV_K):
                gcw_out[j:j + 1, :] = _colsum(gcw_ref[8 * j:8 * j + 8, :])
            gsm_out[...] = jnp.broadcast_to(_colsum(gsm_ref[...]), (8, HEAD))

    row = pl.BlockSpec((t, D_HALF), lambda i: (tile_of(i), 0))
    const = lambda shape: pl.BlockSpec(shape, lambda i: (0, 0))
    return dict(
        inputs=[proj] * 3 + list(pre) + [proj, conv_w, a_log, dt_bias, dqn, dkn, dvs, dbeta, dg],
        in_specs=_conv_specs(t, tile_of)[:3] + [row] * 3
                 + [pl.BlockSpec((t, HEAD), lambda i: (tile_of(i), COL_BA // HEAD)),
                    const((CONV_K, 3 * D_HALF)), const((1, N_HEADS)), const((1, N_HEADS))] + [row] * 5,
        out_specs=[row, row, row, row, const((8, 3 * D_HALF)), const((8, HEAD))],
        out_shape=[_sds((s, D_HALF), BF16)] * 4 + [_sds((8, 3 * D_HALF)), _sds((8, HEAD))],
        scratch=[pltpu.VMEM((8, 3 * D_HALF), F32), pltpu.VMEM((8 * CONV_K, 3 * D_HALF), F32), pltpu.VMEM((8, HEAD), F32)],
        stages=stages)


def _pool_fwd_part(proj, pool_w, pool_scale):
    s = proj.shape[0]
    t = POOL_T
    hb = t // HEAD

    def stages(ins, outs, scratch, tile=None):
        u_ref, z_ref, halo_ref, pw_ref, ps_ref, band_ref = ins
        y_ref, = outs
        i = pl.program_id(0) if tile is None else tile
        live = (i > 0).astype(F32)
        groups = lambda ref: [ref[:, sl] for sl in HEAD_COLS]
        z = groups(z_ref)
        u, halo = groups(u_ref), [h * live for h in groups(halo_ref)]
        yield
        _, mixed, sg, _ = _pool_mix(u, halo, z, [pw_ref[g] for g in HEADS], [band_ref[g] for g in HEADS], i * t)
        yield
        for sl, m, zg, s_ in zip(HEAD_COLS, mixed, z, sg):
            y_ref[:, sl] = m * ps_ref[:, sl] * (zg * s_)
        yield

    const3 = lambda shape: pl.BlockSpec(shape, lambda i: (0, 0, 0))
    return dict(
        inputs=[proj, proj, proj, pool_w, pool_scale, _pool_bands(t)],
        in_specs=[pl.BlockSpec((t, D_HALF), lambda i: (i, 0)), pl.BlockSpec((t, D_HALF), lambda i: (i, 1)),
                  pl.BlockSpec((HEAD, D_HALF), lambda i: (jnp.maximum(i * hb - 1, 0), 0)),
                  const3((N_HEADS, HEAD, HEAD)), pl.BlockSpec((1, D_HALF), lambda i: (0, 0)), const3((N_HEADS, t, HEAD + t))],
        out_specs=[pl.BlockSpec((t, D_HALF), lambda i: (i, 0))], out_shape=[_sds((s, D_HALF))],
        scratch=[], stages=stages)


def _conv_fwd_part(proj, conv_w, a_log, dt_bias):
    s = proj.shape[0]
    t = CONV_T

    def stages(ins, outs, scratch, tile=None):
        q_ref, k_ref, v_ref, hq_ref, hk_ref, hv_ref, ba_ref, cw_ref, al_ref, dtb_ref = ins
        qn_ref, kn_ref, vs_ref, beta_ref, g_ref, yq_ref, yk_ref, yv_ref = outs
        live = ((pl.program_id(0) if tile is None else tile) > 0).astype(F32)
        parts = ((q_ref, hq_ref, qn_ref, yq_ref), (k_ref, hk_ref, kn_ref, yk_ref), (v_ref, hv_ref, vs_ref, yv_ref))
        for p, (x_ref, h_ref, o_ref, y_ref) in enumerate(parts):
            for h in HEADS:
                cs = HEAD_COLS[h]
                taps = _conv_taps(x_ref[:, cs], h_ref[:, cs] * live)
                y = _conv_pre(taps, cw_ref[:, p * D_HALF + h * HEAD:p * D_HALF + (h + 1) * HEAD])
                y_ref[:, cs] = y
                sv = y * _sigmoid(y)
                o_ref[:, cs] = sv if p == 2 else sv * lax.rsqrt(_rowsum(sv * sv) + EPS)
                yield
        ba = ba_ref[...]
        for h in HEADS:
            beta = _sigmoid(ba[:, h:h + 1])
            gl = -jnp.exp(al_ref[0:1, h:h + 1]) * _softplus(ba[:, N_HEADS + h:N_HEADS + h + 1] + dtb_ref[0:1, h:h + 1])
            beta_ref[:, HEAD_COLS[h]] = jnp.broadcast_to(beta, (t, HEAD))
            g_ref[:, HEAD_COLS[h]] = jnp.broadcast_to(gl, (t, HEAD))
        yield

    row = pl.BlockSpec((t, D_HALF), lambda i: (i, 0))
    const = lambda shape: pl.BlockSpec(shape, lambda i: (0, 0))
    return dict(
        inputs=[proj] * 7 + [conv_w, a_log, dt_bias],
        in_specs=_conv_specs(t) + [pl.BlockSpec((t, HEAD), lambda i: (i, COL_BA // HEAD)),
                                   const((CONV_K, 3 * D_HALF)), const((1, N_HEADS)), const((1, N_HEADS))],
        out_specs=[row] * 8, out_shape=[_sds((s, D_HALF))] * 8, scratch=[], stages=stages)


def _front_fwd(x, norm_w, w_pad, conv_w, a_log, dt_bias, pool_w, pool_scale, after):
    s = x.shape[0]
    t = CONV_T
    n_tiles = s // t
    assert POOL_T == CONV_T
    like_proj = _sds((s, N_IN_PAD))
    conv = _conv_fwd_part(like_proj, conv_w, a_log, dt_bias)
    pool = _pool_fwd_part(like_proj, pool_w, pool_scale)
    bands = pool["inputs"][-1]
    mxu_n = 256
    col_bounds = list(range(0, N_IN_PAD, 3 * mxu_n)) + [N_IN_PAD]

    def body(x_ref, nw_ref, w_ref, cw_ref, al_ref, dtb_ref, pw_ref, ps_ref, band_ref, after_ref,
             proj_ref, nt_ref, qn_ref, kn_ref, vs_ref, beta_ref, g_ref, yq_ref, yk_ref, yv_ref, y_ref, prev):
        del after_ref
        i = pl.program_id(0)

        @pl.when(i == 0)
        def _():
            prev[...] = jnp.zeros_like(prev)

        tile = jnp.maximum(i - 1, 0)
        main, above8, above = pl.ds(HEAD, t), pl.ds(HEAD - 8, 8), pl.ds(0, HEAD)
        cols = lambda rows, c0, width=D_HALF: prev.at[rows, pl.ds(c0, width)]
        conv_ins = (cols(main, 2 * D_HALF), cols(main, 3 * D_HALF), cols(main, 4 * D_HALF),
                    cols(above8, 2 * D_HALF), cols(above8, 3 * D_HALF), cols(above8, 4 * D_HALF),
                    cols(main, COL_BA, HEAD), cw_ref, al_ref, dtb_ref)
        pool_ins = (cols(main, 0), cols(main, D_HALF), cols(above, 0), pw_ref, ps_ref, band_ref)

        def projection():
            xv = x_ref[...]
            r = lax.rsqrt(jnp.mean(xv * xv, axis=-1, keepdims=True) + EPS)
            nv = xv * r * nw_ref[...]
            nt_ref[...] = nv.T.astype(BF16)
            nb = nv.astype(BF16)
            yield
            for lo, hi in zip(col_bounds[:-1], col_bounds[1:]):
                proj_ref[:, lo:hi] = jnp.dot(nb, w_ref[:, lo:hi], preferred_element_type=F32)
                yield

        _interleave(projection(),
                    conv["stages"](conv_ins, (qn_ref, kn_ref, vs_ref, beta_ref, g_ref, yq_ref, yk_ref, yv_ref), (), tile),
                    pool["stages"](pool_ins, (y_ref,), (), tile))
        prev[0:HEAD] = prev[t:t + HEAD]
        prev[HEAD:HEAD + t] = proj_ref[...]

    last = n_tiles - 1
    now = lambda i: (jnp.minimum(i, last), 0)
    before = lambda i: (jnp.maximum(i - 1, 0), 0)
    const = lambda a: pl.BlockSpec(a.shape, lambda i: (0,) * a.ndim)
    half = pl.BlockSpec((t, D_HALF), before)
    return _call(
        body, name="front_fwd", grid=(n_tiles + 1,),
        in_specs=[pl.BlockSpec((t, D_MODEL), now), const(norm_w), const(w_pad), const(conv_w), const(a_log), const(dt_bias),
                  const(pool_w), const(pool_scale), const(bands), pl.BlockSpec(memory_space=pl.ANY)],
        out_specs=[pl.BlockSpec((t, N_IN_PAD), now), pl.BlockSpec((D_MODEL, t), lambda i: (0, jnp.minimum(i, last)))]
                  + [half] * 9,
        out_shape=[_sds((s, N_IN_PAD)), _sds((D_MODEL, s), BF16)] + [_sds((s, D_HALF))] * 9,
        scratch_shapes=[pltpu.VMEM((HEAD + t, N_IN_PAD), F32)],
        compiler_params=_params("arbitrary"),
    )(x, norm_w, w_pad, conv_w, a_log, dt_bias, pool_w, pool_scale, bands, after)


def _conv_pool_bwd(proj, pre, conv_w, a_log, dt_bias, dqn, dkn, dvs, dbeta, dg, dyp, pool_w, pool_scale):
    n_tiles = proj.shape[0] // CONV_T
    assert POOL_T == CONV_T
    tile_of = lambda i: n_tiles - 1 - i
    return _fused_call("conv_pool_bwd", n_tiles, [
        _conv_bwd_part(proj, pre, conv_w, a_log, dt_bias, dqn, dkn, dvs, dbeta, dg, tile_of, n_tiles),
        _pool_bwd_part(proj, dyp, pool_w, pool_scale, tile_of, n_tiles)])


def _rows8(x):
    acc = x[0:8]
    for r in range(8, x.shape[0], 8):
        acc = acc + x[r:r + 8]
    return acc


IN_T = 512


def _in_bwd(x, dh, norm_w, w_pad, pieces, after):
    s = x.shape[0]
    t = IN_T
    widths = [D_HALF] * 6 + [N_IN_PAD - COL_BA]

    def body(*refs):
        x_ref, dh_ref, nw_ref, w_ref = refs[:4]
        p_refs = refs[4:4 + len(pieces)]
        gx_ref, gnw_ref = refs[5 + len(pieces):]

        @pl.when(pl.program_id(0) == 0)
        def _():
            gnw_ref[...] = jnp.zeros_like(gnw_ref)

        dn = jnp.zeros((t, D_MODEL), F32)
        col = 0
        for p_ref, wd in zip(p_refs, widths):
            dn = dn + _bdot_nt(p_ref[...], w_ref[:, col:col + wd])
            col += wd
        xv = x_ref[...]
        r = lax.rsqrt(jnp.mean(xv * xv, axis=-1, keepdims=True) + EPS)
        xhat = xv * r
        gnw_ref[...] += _colsum(dn * xhat)
        dxh = dn * nw_ref[...]
        gx_ref[...] = dh_ref[...] + r * (dxh - xhat * jnp.mean(dxh * xhat, axis=-1, keepdims=True))

    wide = pl.BlockSpec((t, D_MODEL), lambda i: (i, 0))
    return _call(
        body, name="in_bwd", grid=(s // t,),
        in_specs=[wide, wide, pl.BlockSpec((1, D_MODEL), lambda i: (0, 0)),
                  pl.BlockSpec((D_MODEL, N_IN_PAD), lambda i: (0, 0))]
                 + [pl.BlockSpec((t, wd), lambda i: (i, 0)) for wd in widths] + [pl.BlockSpec(memory_space=pl.ANY)],
        out_specs=[wide, pl.BlockSpec((1, D_MODEL), lambda i: (0, 0))],
        out_shape=[_sds((s, D_MODEL)), _sds((1, D_MODEL))],
        compiler_params=_params("arbitrary"),
    )(x, dh, norm_w, w_pad, *pieces, after)


def _adamw_shard(name, w, g_own, g_got, cidx, m, v):
    _, r, c = w.shape
    half = r // 2
    rows = 256 if half % 256 == 0 else half
    per_half = half // rows

    def body(c_ref, w_ref, go_ref, gg_ref, m_ref, v_ref, gout_ref, d_ref, nm_ref, nv_ref):
        mine = (pl.program_id(0) // per_half) == c_ref[0]
        gv = jnp.where(mine, go_ref[:, :c], gg_ref[:, :c])
        gout_ref[0] = gv
        mn = ADAM_B1 * m_ref[0] + (1.0 - ADAM_B1) * gv
        vn = ADAM_B2 * v_ref[0] + (1.0 - ADAM_B2) * (gv * gv)
        m_hat = mn / (1.0 - ADAM_B1 ** ADAM_STEP)
        v_hat = vn / (1.0 - ADAM_B2 ** ADAM_STEP)
        d_ref[0] = -ADAM_LR * (m_hat / (jnp.sqrt(v_hat) + ADAM_EPS) + ADAM_WD * w_ref[0])
        nm_ref[0] = mn
        nv_ref[0] = vn

    blk = pl.BlockSpec((1, rows, c), lambda i, c_ref: (0, i, 0))
    gblk = pl.BlockSpec((rows, g_own.shape[1]), lambda i, c_ref: (i % per_half, 0))
    return _call(
        body, name=name,
        grid_spec=pltpu.PrefetchScalarGridSpec(
            num_scalar_prefetch=1, grid=(2 * per_half,),
            in_specs=[blk, gblk, gblk, blk, blk], out_specs=[blk] * 4),
        out_shape=[_sds((1, r, c))] * 4,
        compiler_params=_params("arbitrary"),
    )(cidx, w, g_own, g_got, m, v)


def _adamw_tiles(name, w, g, m, v):
    n = w.shape[0]
    nb = 77 if n % 77 == 0 else n

    def body(w_ref, g_ref, m_ref, v_ref, d_ref, nm_ref, nv_ref):
        gv = g_ref[...]
        mn = ADAM_B1 * m_ref[...] + (1.0 - ADAM_B1) * gv
        vn = ADAM_B2 * v_ref[...] + (1.0 - ADAM_B2) * (gv * gv)
        m_hat = mn / (1.0 - ADAM_B1 ** ADAM_STEP)
        v_hat = vn / (1.0 - ADAM_B2 ** ADAM_STEP)
        d_ref[...] = -ADAM_LR * (m_hat / (jnp.sqrt(v_hat) + ADAM_EPS) + ADAM_WD * w_ref[...])
        nm_ref[...] = mn
        nv_ref[...] = vn

    blk = pl.BlockSpec((nb, 8, HEAD), lambda i: (i, 0, 0))
    return _call(
        body, name=name, grid=(n // nb,),
        in_specs=[blk] * 4, out_specs=[blk] * 3, out_shape=[_sds(w.shape)] * 3,
        compiler_params=_params("arbitrary"),
    )(w, g, m, v)


def _make_copy(src, dst, send, recv, target):
    if target is None:
        return pltpu.make_async_copy(src, dst, recv)
    return pltpu.make_async_remote_copy(src_ref=src, dst_ref=dst, send_sem=send, recv_sem=recv,
                                        device_id=target, device_id_type=pl.DeviceIdType.MESH)


def _exchange(name, inputs, out_shapes, phases):
    n_in = len(inputs)
    n_out = len(out_shapes)
    n_cp = sum(len(p) for p in phases)

    def body(*refs):
        ins, outs = refs[:n_in], refs[n_in:n_in + n_out]
        send, recv = refs[n_in + n_out:]
        pos = (lax.axis_index("x"), lax.axis_index("y"), lax.axis_index("c"))
        k = 0
        for phase in phases:
            cps = []
            for src, dst, target in phase:
                cps.append(_make_copy(src(ins, outs, pos), dst(ins, outs, pos), send.at[k], recv.at[k],
                                      target and target(pos)))
                k += 1
            for cp in cps:
                cp.start()
            for cp in cps:
                cp.wait()

    anyspec = pl.BlockSpec(memory_space=pl.ANY)
    return _call(
        body, name=name,
        in_specs=[anyspec] * n_in, out_specs=[anyspec] * n_out, out_shape=list(out_shapes),
        scratch_shapes=[pltpu.SemaphoreType.DMA((n_cp,)), pltpu.SemaphoreType.DMA((n_cp,))],
    )(*inputs)


def _exchange_start(name, inputs, out_shapes, copies):
    n_in, n_out, n_cp = len(inputs), len(out_shapes), len(copies)

    def body(*refs):
        ins, lands = refs[:n_in], refs[n_in:n_in + n_out]
        sems = refs[n_in + n_out:n_in + n_out + 2 * n_cp]
        token = refs[-1]
        pos = (lax.axis_index("x"), lax.axis_index("y"), lax.axis_index("c"))
        for k, (src, dst, target) in enumerate(copies):
            _make_copy(src(ins, lands, pos), dst(ins, lands, pos), sems[2 * k], sems[2 * k + 1],
                       target and target(pos)).start()
        token[...] = jnp.zeros_like(token)

    hbm = pl.BlockSpec(memory_space=pltpu.HBM)
    sem = pl.BlockSpec(memory_space=pltpu.SEMAPHORE)
    bufs = list(inputs) + [lax.empty(o.shape, o.dtype) for o in out_shapes]
    outs = _call(
        body, name=name,
        out_shape=tuple([pltpu.SemaphoreType.DMA(())] * (2 * n_cp) + [pltpu.HBM(b.shape, b.dtype) for b in bufs]
                        + [_sds((8, HEAD))]),
        in_specs=[hbm] * len(bufs),
        out_specs=tuple([sem] * (2 * n_cp) + [hbm] * len(bufs) + [pl.BlockSpec(memory_space=pltpu.VMEM)]),
        input_output_aliases={i: 2 * n_cp + i for i in range(len(bufs))},
        compiler_params=pltpu.CompilerParams(has_side_effects=pltpu.SideEffectType.DATAFLOW_SIDE_EFFECTING),
    )(*[pltpu.with_memory_space_constraint(b, pltpu.HBM) for b in bufs])
    return outs[:2 * n_cp], outs[2 * n_cp:2 * n_cp + n_in], outs[2 * n_cp + n_in:-1], outs[-1]


def _exchange_wait(name, sems, sources, lands, copies, after):
    n_in, n_out, n_cp = len(sources), len(lands), len(copies)

    def body(*refs):
        ins, zones = refs[:n_in], refs[n_in:n_in + n_out]
        sem_refs = refs[n_in + n_out:n_in + n_out + 2 * n_cp]
        pos = (lax.axis_index("x"), lax.axis_index("y"), lax.axis_index("c"))
        for k, (src, dst, target) in enumerate(copies):
            cp = _make_copy(src(ins, zones, pos), dst(ins, zones, pos), sem_refs[2 * k], sem_refs[2 * k + 1],
                            target and target(pos))
            if target is None:
                cp.wait()
            else:
                cp.wait_send()
                cp.wait_recv()

    hbm = pl.BlockSpec(memory_space=pltpu.HBM)
    sem = pl.BlockSpec(memory_space=pltpu.SEMAPHORE)
    bufs = list(sources) + list(lands)
    outs = _call(
        body, name=name,
        out_shape=tuple(pltpu.HBM(b.shape, b.dtype) for b in bufs),
        in_specs=[hbm] * len(bufs) + [sem] * (2 * n_cp) + [pl.BlockSpec(memory_space=pl.ANY)],
        out_specs=tuple([hbm] * len(bufs)),
        input_output_aliases={i: i for i in range(len(bufs))},
        compiler_params=pltpu.CompilerParams(has_side_effects=pltpu.SideEffectType.DATAFLOW_SIDE_EFFECTING),
    )(*bufs, *sems, after)
    return outs[:n_in], outs[n_in:]


def _allreduce_tile(name, v):
    def body(v_ref, out_ref, slots, send, recv):
        x, y, c = lax.axis_index("x"), lax.axis_index("y"), lax.axis_index("c")
        me = 4 * x + 2 * y + c
        slots[me] = v_ref[...]
        cps = []
        for k in range(1, 8):
            peer = (x ^ (k >> 2), y ^ ((k >> 1) & 1), c ^ (k & 1))
            cps.append(pltpu.make_async_remote_copy(
                src_ref=v_ref, dst_ref=slots.at[me], send_sem=send.at[k - 1], recv_sem=recv.at[k - 1],
                device_id=peer, device_id_type=pl.DeviceIdType.MESH))
        for cp in cps:
            cp.start()
        for cp in cps:
            cp.wait()
        acc = slots[0]
        for i in range(1, 8):
            acc = acc + slots[i]
        out_ref[...] = acc

    vm = pl.BlockSpec(memory_space=pltpu.VMEM)
    return _call(
        body, name=name, in_specs=[vm], out_specs=vm, out_shape=_sds(v.shape),
        scratch_shapes=[pltpu.VMEM((8,) + v.shape, F32), pltpu.SemaphoreType.DMA((7,)), pltpu.SemaphoreType.DMA((7,))],
    )(v)


def _chip(pos):
    return 2 * pos[0] + pos[1]


def _other_chip(pos, mask):
    x, y, c = pos
    return (x ^ (mask >> 1), y ^ (mask & 1), c)


def _sibling(pos):
    return (pos[0], pos[1], 1 - pos[2])


def _gather_weights(wb, cb):
    rows = wb.shape[0] // 2
    x_nb, y_nb, diag = CHIP_MASKS

    def part(pos, mask, quarter=None):
        start = pos[2] * rows if quarter is None else pos[2] * rows + quarter * (rows // 2)
        return lambda outs: outs[0].at[_chip(pos) ^ mask, pl.ds(start, rows if quarter is None else rows // 2)]

    def passed_on(mask, to, quarter=None):
        return (lambda ins, outs, pos: part(pos, mask, quarter)(outs), lambda ins, outs, pos: part(pos, mask, quarter)(outs), to)

    first = [(lambda ins, outs, pos: ins[0].at[pl.ds(pos[2] * rows, rows)], lambda ins, outs, pos: part(pos, 0)(outs),
              functools.partial(_other_chip, mask=mask)) for mask in (x_nb, y_nb)]
    conv_cols = lambda ins, outs, pos: outs[1].at[:, pl.ds(pl.multiple_of(_chip(pos) * cb.shape[1], HEAD), cb.shape[1])]
    first += [(lambda ins, outs, pos: ins[1], conv_cols, functools.partial(_other_chip, mask=mask)) for mask in CHIP_MASKS]
    first += [(lambda ins, outs, pos: ins[1], conv_cols, None)]
    second = [passed_on(x_nb, functools.partial(_other_chip, mask=y_nb), quarter=0),
              passed_on(y_nb, functools.partial(_other_chip, mask=x_nb), quarter=1),
              passed_on(x_nb, _sibling), passed_on(y_nb, _sibling)]
    third = [passed_on(diag, _sibling)]
    return _exchange("gather_weights", [wb, cb],
                     [_sds((4,) + wb.shape, wb.dtype), _sds((cb.shape[0], 4 * cb.shape[1]), cb.dtype)], [first, second, third])


def _assemble_w_in(gw, wb):
    m = gw.shape[1]
    bases = [j * BLK_IN // HEAD * HEAD for j in range(4)] + [N_IN_PAD]

    def body(g_ref, wb_ref, o_ref, blocks, own, whole, loaded, stored):
        me = _chip((lax.axis_index("x"), lax.axis_index("y"), lax.axis_index("c")))
        loads = [pltpu.make_async_copy(g_ref.at[j], blocks.at[j], loaded.at[j]) for j in range(4)]
        loads.append(pltpu.make_async_copy(wb_ref, own, loaded.at[4]))
        for cp in loads:
            cp.start()
        whole[...] = jnp.zeros_like(whole)
        loads[4].wait()
        lane = lax.broadcasted_iota(I32, (m, BLK_IN_PAD), 1)
        stores = []
        for j in range(4):
            loads[j].wait()
            blk = jnp.where(me == j, own[...], blocks[j]).astype(F32)
            base, shift = bases[j], j * BLK_IN - bases[j]
            moved = pltpu.roll(blk, shift, 1) if shift else blk
            window = whole[:, base:base + BLK_IN_PAD].astype(F32)
            mine = (lane >= shift) & (lane < shift + BLK_IN)
            whole[:, base:base + BLK_IN_PAD] = jnp.where(mine, moved, window).astype(BF16)
            done = pl.ds(base, bases[j + 1] - base)
            stores.append(pltpu.make_async_copy(whole.at[:, done], o_ref.at[:, done], stored.at[j]))
            stores[-1].start()
        for cp in stores:
            cp.wait()

    return _call(
        body, name="assemble_w_in",
        in_specs=[pl.BlockSpec(memory_space=pl.ANY)] * 2, out_specs=pl.BlockSpec(memory_space=pl.ANY),
        out_shape=_sds((m, N_IN_PAD), BF16),
        scratch_shapes=[pltpu.VMEM(gw.shape, BF16), pltpu.VMEM(wb.shape, BF16), pltpu.VMEM((m, N_IN_PAD), BF16),
                        pltpu.SemaphoreType.DMA((5,)), pltpu.SemaphoreType.DMA((4,))],
        compiler_params=_params(),
    )(gw, wb)


def _gather_blocks(ob):
    copies = [(lambda ins, outs, pos: ins[0], lambda ins, outs, pos: outs[0].at[_chip(pos)],
               functools.partial(_other_chip, mask=mask)) for mask in CHIP_MASKS]
    copies.append((lambda ins, outs, pos: ins[0], lambda ins, outs, pos: outs[0].at[_chip(pos)], None))
    return [_sds((4,) + ob.shape, ob.dtype)], copies


def _reduce_sibling(name, arrays):
    n = len(arrays)
    halves = [a.shape[:-2] + (a.shape[-2] // 2, a.shape[-1]) for a in arrays]
    pieces = [(a, j) for a in range(n) for j in (range(arrays[a].shape[0]) if arrays[a].ndim == 3 else [None])]

    def body(*refs):
        whole, outs = refs[:n], refs[n:2 * n]
        own, land, summed = refs[2 * n:3 * n], refs[3 * n:4 * n], refs[4 * n:5 * n]
        send, recv, loaded, stored = refs[5 * n:]
        pos = (lax.axis_index("x"), lax.axis_index("y"), lax.axis_index("c"))
        block = lambda ref, j, rows=None: ref.at[(() if j is None else (j,)) + (() if rows is None else (rows,))]
        arrive, load = [], []
        for k, (a, j) in enumerate(pieces):
            h = halves[a][-2]
            arrive.append(pltpu.make_async_remote_copy(
                src_ref=block(whole[a], j, pl.ds((1 - pos[2]) * h, h)), dst_ref=block(land[a], j),
                send_sem=send.at[k], recv_sem=recv.at[k],
                device_id=_sibling(pos), device_id_type=pl.DeviceIdType.MESH))
            load.append(pltpu.make_async_copy(block(whole[a], j, pl.ds(pos[2] * h, h)), block(own[a], j), loaded.at[k]))
        for cp in arrive + load:
            cp.start()
        store = []
        for k, (a, j) in enumerate(pieces):
            load[k].wait()
            arrive[k].wait()
            at = Ellipsis if j is None else j
            summed[a][at] = (own[a][at].astype(F32) + land[a][at].astype(F32)).astype(summed[a].dtype)
            store.append(pltpu.make_async_copy(block(summed[a], j), block(outs[a], j), stored.at[k]))
            store[-1].start()
        for cp in store:
            cp.wait()

    vmem = [pltpu.VMEM(h, a.dtype) for h, a in zip(halves, arrays)]
    sems = [pltpu.SemaphoreType.DMA((len(pieces),))] * 4
    return _call(
        body, name=name,
        in_specs=[pl.BlockSpec(memory_space=pl.ANY)] * n, out_specs=[pl.BlockSpec(memory_space=pl.ANY)] * n,
        out_shape=[_sds(h, a.dtype) for h, a in zip(halves, arrays)],
        scratch_shapes=vmem * 3 + sems,
        compiler_params=_params(),
    )(*arrays)


def _to_other_chips(arrays, blocked):
    def src(ins, outs, pos, a, mask):
        return ins[a].at[_chip(pos) ^ mask] if blocked[a] else ins[a]

    outs = [_sds((3,) + (a.shape[1:] if b else a.shape), a.dtype) for a, b in zip(arrays, blocked)]
    copies = []
    for mi, mask in enumerate(CHIP_MASKS):
        for a in range(len(arrays)):
            copies.append((functools.partial(src, a=a, mask=mask), lambda ins, outs, pos, a=a, mi=mi: outs[a].at[mi],
                           functools.partial(_other_chip, mask=mask)))
    return outs, copies


def _sum_chips_swap(name, owns, gots, blocked):
    n = len(owns)
    shapes = [g.shape[-2:] for g in gots]

    def body(*refs):
        own, got, mine, theirs = refs[:n], refs[n:2 * n], refs[2 * n:3 * n], refs[3 * n:4 * n]
        own_v, got_v, sum_v = refs[4 * n:5 * n], refs[5 * n:6 * n], refs[6 * n:7 * n]
        send, recv, loaded, stored = refs[7 * n:]
        pos = (lax.axis_index("x"), lax.axis_index("y"), lax.axis_index("c"))
        load = []
        for a in range(n):
            load.append((pltpu.make_async_copy(own[a].at[_chip(pos)] if blocked[a] else own[a], own_v[a], loaded.at[2 * a]),
                         pltpu.make_async_copy(got[a], got_v[a], loaded.at[2 * a + 1])))
        for pair in load:
            for cp in pair:
                cp.start()
        out = []
        for a in range(n):
            for cp in load[a]:
                cp.wait()
            sum_v[a][...] = ((own_v[a][...].astype(F32) + got_v[a][0].astype(F32))
                             + (got_v[a][1].astype(F32) + got_v[a][2].astype(F32)))
            out.append(pltpu.make_async_remote_copy(
                src_ref=sum_v[a], dst_ref=theirs[a], send_sem=send.at[a], recv_sem=recv.at[a],
                device_id=_sibling(pos), device_id_type=pl.DeviceIdType.MESH))
            out.append(pltpu.make_async_copy(sum_v[a], mine[a], stored.at[a]))
            out[-2].start()
            out[-1].start()
        for cp in out:
            cp.wait()

    outs = _call(
        body, name=name,
        in_specs=[pl.BlockSpec(memory_space=pl.ANY)] * (2 * n), out_specs=[pl.BlockSpec(memory_space=pl.ANY)] * (2 * n),
        out_shape=[_sds(s) for s in shapes] * 2,
        scratch_shapes=[pltpu.VMEM(s, o.dtype) for s, o in zip(shapes, owns)]
                       + [pltpu.VMEM((3,) + s, g.dtype) for s, g in zip(shapes, gots)] + [pltpu.VMEM(s, F32) for s in shapes]
                       + [pltpu.SemaphoreType.DMA((n,)), pltpu.SemaphoreType.DMA((n,)),
                          pltpu.SemaphoreType.DMA((2 * n,)), pltpu.SemaphoreType.DMA((n,))],
        compiler_params=_params(),
    )(*owns, *gots)
    return outs[:n], outs[n:]


def _local_step(x, target, w_pad, w_out, conv_w, norm_w, pool_w, pool_scale, a_log, dt_bias, dn_norm_w, final_norm_w,
                after):
    proj, n_t, qn, kn, vs, beta, g, yq, yk, yv, y_pool = _front_fwd(
        x, norm_w, w_pad, conv_w, a_log, dt_bias, pool_w, pool_scale, after)
    w, att, qd, kd, tm, cd, o, vn, st = _delta_fwd(qn, kn, vs, beta, g)
    w_out = w_out(o) if callable(w_out) else w_out
    g_wout, dh, dyp, do, ddz, loss, g_fnw, g_dnw = _out_fwd_bwd(x, y_pool, o, proj, target, w_out, dn_norm_w, final_norm_w)
    dqn, dkn, dvs, dbeta, dg = _delta_bwd(do, vn, qd, kd, w, att, cd, st, qn, kn, vs, beta, g, tm)
    (dcq, dck, dcv, dba, g_cw, g_sm), (dpu, dpz, g_pw, g_ps) = _conv_pool_bwd(
        proj, (yq, yk, yv), conv_w, a_log, dt_bias, dqn, dkn, dvs, dbeta, dg, dyp, pool_w, pool_scale)
    pieces = [dpu, dpz, dcq, dck, dcv, ddz, dba]
    g_win = _grad_w_in(n_t, pieces)
    small = dict(norm_w=jnp.zeros_like(norm_w), pool_w=g_pw, pool_scale=g_ps, conv_w=g_cw[:CONV_K],
                 a_log=g_sm[0:1, 0:N_HEADS], dt_bias=g_sm[0:1, N_HEADS:2 * N_HEADS], dn_norm_w=g_dnw, final_norm_w=g_fnw)
    return loss[0, 0], g_win, g_wout, small, dh, pieces


SMALL_LAYOUT = (("pool_w", 512, HEAD, (1, N_HEADS, HEAD, HEAD)), ("final_norm_w", 8, HEAD, (D_MODEL,)),
                ("pool_scale", 4, HEAD, (1, D_HALF)), ("conv_w", 48, HEAD, (1, CONV_K, 3 * D_HALF)),
                ("dn_norm_w", 1, HEAD, (1, HEAD)), ("a_log", 1, N_HEADS, (1, N_HEADS)), ("dt_bias", 1, N_HEADS, (1, N_HEADS)),
                ("loss", 1, 1, ()))


def _small_offsets():
    offs, r = {}, 0
    for name, rows, _, _ in SMALL_LAYOUT:
        offs[name] = r
        r += -(-rows // 8) * 8
    assert r <= SMALL_ROWS
    return offs


def _pack_small(t):
    parts = []
    for name, rows, lanes, _ in SMALL_LAYOUT:
        a = t.get(name, jnp.zeros((1,), F32)).reshape(rows, lanes)
        parts.append(jnp.pad(a, ((0, -(-rows // 8) * 8 - rows), (0, HEAD - lanes))))
    buf = jnp.concatenate(parts, axis=0)
    return jnp.pad(buf, ((0, SMALL_ROWS - buf.shape[0]), (0, 0)))


def _adamw_small(w, g_own, g_got, cidx, m, v):
    offs = _small_offsets()
    names = [e[0] for e in SMALL_LAYOUT]
    n = len(names)

    def body(c_ref, w_ref, go_ref, gg_ref, m_ref, v_ref, *outs):
        own_low = c_ref[0] == 0
        gv = jnp.concatenate([jnp.where(own_low, go_ref[...], gg_ref[...]), jnp.where(own_low, gg_ref[...], go_ref[...])], axis=0)
        mn = ADAM_B1 * m_ref[...] + (1.0 - ADAM_B1) * gv
        vn = ADAM_B2 * v_ref[...] + (1.0 - ADAM_B2) * (gv * gv)
        m_hat = mn / (1.0 - ADAM_B1 ** ADAM_STEP)
        v_hat = vn / (1.0 - ADAM_B2 ** ADAM_STEP)
        dl = -ADAM_LR * (m_hat / (jnp.sqrt(v_hat) + ADAM_EPS) + ADAM_WD * w_ref[...])
        for kind, arr in enumerate((gv, dl, mn, vn)):
            for i, (name, rows, lanes, _) in enumerate(SMALL_LAYOUT):
                outs[kind * n + i][...] = arr[offs[name]:offs[name] + rows, :lanes]

    whole = lambda shape: pl.BlockSpec(shape, lambda i, c_ref: (0,) * len(shape))
    out_shapes = [_sds((rows, lanes)) for _, rows, lanes, _ in SMALL_LAYOUT] * 4
    res = _call(
        body, name="adamw_small",
        grid_spec=pltpu.PrefetchScalarGridSpec(
            num_scalar_prefetch=1, grid=(1,),
            in_specs=[whole(w.shape), whole(g_own.shape), whole(g_got.shape), whole(m.shape), whole(v.shape)],
            out_specs=[whole(o.shape) for o in out_shapes]),
        out_shape=out_shapes,
        compiler_params=_params("arbitrary"),
    )(cidx, w, g_own, g_got, m, v)
    return [{name: res[kind * n + i].reshape(shape) for i, (name, _, _, shape) in enumerate(SMALL_LAYOUT)}
            for kind in range(4)]


def kernel(x, norm_w, w_in, pool_w, pool_scale, conv_w, a_log, dt_bias, dn_norm_w, w_out, final_norm_w, loss_target, m_norm_w, m_w_in, m_pool_w, m_pool_scale, m_conv_w, m_a_log, m_dt_bias, m_dn_norm_w, m_w_out, m_final_norm_w, v_norm_w, v_w_in, v_pool_w, v_pool_scale, v_conv_w, v_a_log, v_dt_bias, v_dn_norm_w, v_w_out, v_final_norm_w):
    cidx = lax.axis_index("c").astype(I32).reshape(1)
    jidx = (2 * lax.axis_index("x") + lax.axis_index("y")).astype(I32)

    wb = jnp.pad(w_in[0].astype(BF16), ((0, 0), (0, BLK_IN_PAD - BLK_IN)))
    ob = w_out[0].astype(BF16)
    gw, cw_full = _gather_weights(wb, conv_w[0])
    w_pad = _assemble_w_in(gw, wb)

    lands_o, copies_o = _gather_blocks(ob)
    sems_o, ob_thru, zones_o, token_o = _exchange_start("gather_w_out_start", [ob], lands_o, copies_o)

    def w_out_full(after):
        _, (got,) = _exchange_wait("gather_w_out_wait", sems_o, ob_thru, zones_o, copies_o, after)
        return got.reshape(D_MODEL, D_MODEL)

    loss, g_win, g_wout, small, dh, pieces = _local_step(
        x[0], loss_target[0], w_pad, w_out_full, cw_full, norm_w, pool_w[0], pool_scale, a_log, dt_bias,
        dn_norm_w, final_norm_w.reshape(1, D_MODEL), token_o)
    small["loss"] = loss

    blocks_out = g_wout.reshape(4, BLK_OUT, D_MODEL)
    full = [g_win, blocks_out, _pack_small(small)]
    chip_sum = _reduce_sibling("reduce_sibling", full)
    blocked = [True, True, False]
    lands, copies = _to_other_chips(chip_sum, blocked)
    sems, chip_sum, zones, token = _exchange_start("reduce_chips_start", chip_sum, lands, copies)
    gx, g_nw = _in_bwd(x[0], dh, norm_w, w_pad, pieces, token)
    g_nw = _allreduce_tile("reduce_norm_w", g_nw.reshape(8, HEAD)).reshape(1, D_MODEL)
    chip_sum, from_chips = _exchange_wait("reduce_chips_wait", sems, chip_sum, zones, copies, gx)
    halves, other_halves = _sum_chips_swap("sum_chips_swap", chip_sum, from_chips, blocked)

    weights = dict(norm_w=norm_w, w_in=w_in, pool_w=pool_w, pool_scale=pool_scale, conv_w=conv_w, a_log=a_log,
                   dt_bias=dt_bias, dn_norm_w=dn_norm_w, w_out=w_out, final_norm_w=final_norm_w)
    ms = dict(norm_w=m_norm_w, w_in=m_w_in, pool_w=m_pool_w, pool_scale=m_pool_scale, conv_w=m_conv_w, a_log=m_a_log,
              dt_bias=m_dt_bias, dn_norm_w=m_dn_norm_w, w_out=m_w_out, final_norm_w=m_final_norm_w)
    vs = dict(norm_w=v_norm_w, w_in=v_w_in, pool_w=v_pool_w, pool_scale=v_pool_scale, conv_w=v_conv_w, a_log=v_a_log,
              dt_bias=v_dt_bias, dn_norm_w=v_dn_norm_w, w_out=v_w_out, final_norm_w=v_final_norm_w)
    names = ["norm_w", "w_in", "pool_w", "pool_scale", "conv_w", "a_log", "dt_bias", "dn_norm_w", "w_out", "final_norm_w"]
    small_names = [n for n in names if n not in ("w_in", "w_out")]

    def pack(t):
        conv = lax.dynamic_update_slice_in_dim(jnp.zeros((CONV_K, 3 * D_HALF), F32), t["conv_w"][0], jidx * BLK_CONV, axis=1)
        return _pack_small({**{n: t[n] for n in small_names if n != "conv_w"}, "conv_w": conv})

    results = [{}, {}, {}, {}]
    to_tiles = lambda a: jnp.transpose(a, (2, 0, 1)).reshape(BLK_IN, 8, HEAD)
    from_tiles = lambda a: jnp.transpose(a, (1, 2, 0)).reshape(1, D_MODEL, BLK_IN)
    lo = jnp.where(cidx[0] == 0, halves[0], other_halves[0])
    hi = jnp.where(cidx[0] == 0, other_halves[0], halves[0])
    g_tiles = jnp.concatenate([lo[:, :BLK_IN].T, hi[:, :BLK_IN].T], axis=1).reshape(BLK_IN, 8, HEAD)
    outs = _adamw_tiles("adamw_w_in", to_tiles(w_in), g_tiles, to_tiles(m_w_in), to_tiles(v_w_in))
    for res, o in zip(results, (g_tiles,) + tuple(outs)):
        res["w_in"] = from_tiles(o)
    outs = _adamw_shard("adamw_w_out", w_out, halves[1], other_halves[1], cidx, m_w_out, v_w_out)
    for res, o in zip(results, outs):
        res["w_out"] = o
    outs = _adamw_small(pack(weights), halves[2], other_halves[2], cidx, pack(ms), pack(vs))
    for res, got in zip(results, outs):
        got["conv_w"] = lax.dynamic_slice_in_dim(got["conv_w"], jidx * BLK_CONV, BLK_CONV, axis=2)
        res.update(got)
    one_tile = lambda a: a.reshape(1, 8, HEAD)
    outs = _adamw_tiles("adamw_norm_w", one_tile(norm_w), one_tile(g_nw), one_tile(m_norm_w), one_tile(v_norm_w))
    for res, o in zip(results, (g_nw,) + tuple(outs)):
        res["norm_w"] = o.reshape(1, D_MODEL)
    grads, delta, new_m, new_v = results

    return (grads["loss"], gx[None], *[grads[n] for n in names], *[delta[n] for n in names],
            *[new_m[n] for n in names], *[new_v[n] for n in names])
```

```python
import functools

import jax
import jax.numpy as jnp
import numpy as np
from jax import lax
from jax.experimental import pallas as pl
from jax.experimental.pallas import tpu as pltpu

F32 = jnp.float32
BF16 = jnp.bfloat16
I32 = jnp.int32

D_MODEL = 1024
D_HALF = 512
N_HEADS = 4
HEAD = 128
CHUNK = 64
PAIR = 2 * CHUNK
WINDOWS = (2, 4, 8, 16)
CONV_K = 4
EPS = 1e-6
N_IN = 3080
N_IN_PAD = 3200
BLK_IN = 770
BLK_IN_PAD = 896
BLK_OUT = 256
BLK_CONV = 384
COL_BA = 3072
QK_SCALE = HEAD ** -0.5
SMALL_ROWS = 608
VMEM_LIMIT = 56 * 1024 * 1024

ADAM_LR = 0.001
ADAM_B1 = 0.9
ADAM_B2 = 0.999
ADAM_EPS = 1e-08
ADAM_WD = 0.01
ADAM_STEP = 10

CHIP_MASKS = (2, 1, 3)
HEADS = range(N_HEADS)
HEAD_COLS = [slice(h * HEAD, (h + 1) * HEAD) for h in HEADS]


def _call(body, **kw):
    return pl.pallas_call(body, **kw)


def _params(*sem):
    return pltpu.CompilerParams(dimension_semantics=sem, vmem_limit_bytes=VMEM_LIMIT)


def _sds(shape, dtype=F32):
    return jax.ShapeDtypeStruct(shape, dtype)


def _bdot(a, b):
    return jnp.dot(a.astype(BF16), b.astype(BF16), preferred_element_type=F32)


def _bdot_nt(a, b):
    return lax.dot_general(a.astype(BF16), b.astype(BF16), (((1,), (1,)), ((), ())), preferred_element_type=F32)


def _bdot_tn(a, b):
    return lax.dot_general(a.astype(BF16), b.astype(BF16), (((0,), (0,)), ((), ())), preferred_element_type=F32)


def _side(a, b):
    return jnp.concatenate([a.astype(BF16), b.astype(BF16)], axis=1)


def _stack(a, b):
    return jnp.concatenate([a.astype(BF16), b.astype(BF16)], axis=0)


def _split(a):
    hi = a.astype(BF16)
    lo = (a - hi.astype(F32)).astype(BF16)
    return hi, lo


def _mask_dot(m, b):
    n = b.shape[1]
    both = jnp.dot(m, jnp.concatenate(_split(b), axis=1), preferred_element_type=F32)
    return both[:, :n] + both[:, n:]


def _sigmoid(x):
    return 0.5 * jnp.tanh(0.5 * x) + 0.5


def _softplus(x):
    return jnp.maximum(x, 0.0) + jnp.log(1.0 + jnp.exp(-jnp.abs(x)))


def _rowsum(x):
    return jnp.sum(x, axis=-1, keepdims=True)


def _colsum(x):
    return jnp.sum(x, axis=0, keepdims=True)


def _shift_down(xv, prev8, k):
    r = pltpu.roll(xv, k, 0)
    q = pltpu.roll(prev8, k, 0)
    row = lax.broadcasted_iota(I32, prev8.shape, 0)
    top = jnp.where(row < k, q, r[0:8])
    return jnp.concatenate([top, r[8:]], axis=0)


def _shift_up(xv, next8, k):
    t = xv.shape[0]
    r = pltpu.roll(xv, t - k, 0)
    q = pltpu.roll(next8, 8 - k, 0)
    row = lax.broadcasted_iota(I32, next8.shape, 0)
    bot = jnp.where(row >= 8 - k, q, r[t - 8:])
    return jnp.concatenate([r[:t - 8], bot], axis=0)


INTRA_PAIRS = 2
UNITS = [(pp, h) for pp in range(INTRA_PAIRS) for h in HEADS]


def _heads_of(ref, rows=PAIR, units=UNITS):
    return [ref[pp * rows:(pp + 1) * rows, HEAD_COLS[h]] for pp, h in units]


def _put_heads_of(ref, vals, rows=PAIR, units=UNITS):
    for (pp, h), v in zip(units, vals):
        ref[pp * rows:(pp + 1) * rows, HEAD_COLS[h]] = v.astype(ref.dtype)


_heads = _heads_of
_put_heads = _put_heads_of


def _each(fn, *lists):
    return [fn(*args) for args in zip(*lists)]


def _pool_bands(t, anti=False):
    r = np.arange(t)[:, None]
    c = np.arange(t + HEAD)[None, :]
    d = (c - r) if anti else (r - c + HEAD)
    return jnp.asarray(np.stack([(d >= 0) & (d < w) for w in WINDOWS]), BF16)


def _pool_mix(u, halo, z, pw, bands, row0):
    t = u[0].shape[0]
    rows = row0 + lax.broadcasted_iota(I32, (t, 1), 0) + 1
    cnt = [jnp.minimum(rows, w).astype(F32) for w in WINDOWS]
    win = _each(lambda b, h, v: _mask_dot(b, jnp.concatenate([h, v], axis=0)), bands, halo, u)
    mix = _each(lambda a, c, v: a / c - v, win, cnt, u)
    mixed = _each(_bdot, mix, pw)
    return mix, mixed, _each(_sigmoid, z), cnt


POOL_T = 256


def _conv_taps(xv, prev8):
    return [_shift_down(xv, prev8, CONV_K - 1 - j) for j in range(CONV_K - 1)] + [xv]


def _conv_pre(taps, cw):
    y = taps[CONV_K - 1] * cw[CONV_K - 1:CONV_K]
    for j in range(CONV_K - 2, -1, -1):
        y = y + taps[j] * cw[j:j + 1]
    return y


CONV_T = 256


def _conv_specs(t, tile_of=lambda i: i):
    tiles = [pl.BlockSpec((t, D_HALF), functools.partial(lambda i, p: (tile_of(i), 2 + p), p=p)) for p in range(3)]
    halos = [pl.BlockSpec((8, D_HALF),
                          functools.partial(lambda i, p: (jnp.maximum(tile_of(i) * (t // 8) - 1, 0), 2 + p), p=p))
             for p in range(3)]
    return tiles + halos


def _pair_masks():
    r = lax.broadcasted_iota(I32, (PAIR, PAIR), 0)
    c = lax.broadcasted_iota(I32, (PAIR, PAIR), 1)
    same = jnp.right_shift(r, 6) == jnp.right_shift(c, 6)
    return same, same & (r >= c), same & (r > c), r == c


def _interleave(*stage_lists):
    live = list(stage_lists)
    while live:
        for gen in list(live):
            try:
                next(gen)
            except StopIteration:
                live.remove(gen)


def _pipelined_step(t, n, leading, trailing):
    @pl.when(t == 0)
    def _():
        _interleave(*leading())

    @pl.when(jnp.logical_and(t > 0, t < n))
    def _():
        _interleave(*leading(), *trailing())

    @pl.when(t == n)
    def _():
        _interleave(*trailing())


def _pair_common_stages(cm, qn, kn, vs, beta, g):
    same, incl, strict, eye = _pair_masks()
    incl_b = incl.astype(BF16)
    first = lax.broadcasted_iota(I32, (PAIR, HEAD), 0) < CHUNK
    cm.update(same=same, incl=incl, strict=strict, eye=eye)
    gc = _each(lambda gv: _mask_dot(incl_b, gv), g)
    q = _each(lambda v: v * QK_SCALE, qn)
    kb = _each(lambda k, b: k * b, kn, beta)
    cm.update(gc=gc, q=q, kb=kb, vb=_each(lambda v, b: v * b, vs, beta))
    yield
    both = _each(lambda a, b, c: _bdot_nt(_stack(a, b), c), kb, q, kn)
    cm.update(kk=[v[:PAIR] for v in both], qk=[v[PAIR:] for v in both])
    gc_row = _each(lambda v: _colsum(jnp.where(eye, v, 0.0)), gc)
    gl = _each(lambda v: jnp.where(first, v[CHUNK - 1:CHUNK], v[PAIR - 1:PAIR]), gc)
    egc = _each(jnp.exp, gc)
    cm.update(gl=gl, egc=egc,
              decay=_each(lambda v, r: jnp.where(incl, jnp.exp(jnp.where(incl, v - r, 0.0)), 0.0), gc, gc_row))
    yield
    cm.update(ekd=_each(lambda a, b: jnp.exp(a - b), gl, gc), cd=_each(jnp.exp, gl),
              kbg=_each(lambda k, e: k * e, kb, egc))
    yield


def _tri_inv_stages(out, a, eye_f):
    p = _each(lambda v: eye_f - v, a)
    x = _each(_bdot, a, a)
    yield
    for it in range(4):
        both = _each(lambda xv, pv: _bdot(xv, _side(pv, xv)), x, p)
        p = _each(lambda pv, b: pv + b[:, :PAIR], p, both)
        x = [b[:, PAIR:] for b in both]
        yield
    out["t"] = _each(lambda pv, xv: pv + _bdot(pv, xv), p, x)
    yield


def _chunk_scalar_spec(pairs=1, index=lambda i: (i, 0)):
    return pl.BlockSpec((16 * pairs, D_HALF), index)


SCAN_PAIRS = 2
SCAN_ROWS = SCAN_PAIRS * PAIR


def _delta_fwd(qn, kn, vs, beta, g):
    s = qn.shape[0]
    n_steps = s // SCAN_ROWS
    n_chunks = s // CHUNK
    assert INTRA_PAIRS == SCAN_PAIRS

    def body(qn_ref, kn_ref, vs_ref, beta_ref, g_ref, w_ref, att_ref, qd_ref, kd_ref, t_ref, cd_ref, o_ref, vn_ref, st_ref,
             state, u_s, w_s, att_s, qd_s, kd_s, cd_s):
        t = pl.program_id(0)

        @pl.when(t == 1)
        def _():
            state[...] = jnp.zeros_like(state)

        cur = lax.rem(t, 2)
        prev = 1 - cur
        cols = list(enumerate(HEAD_COLS))

        def recurrence():
            sm = [state[h] for h in HEADS]
            for ci in range(2 * SCAN_PAIRS):
                rs = slice(ci * CHUNK, (ci + 1) * CHUNK)
                for h in HEADS:
                    st_ref[ci, h] = sm[h]
                both = [_bdot(jnp.concatenate([w_s[prev, rs, sl], qd_s[prev, rs, sl]], axis=0), sm[h]) for h, sl in cols]
                vn = [u_s[prev, rs, sl] - both[h][:CHUNK] for h, sl in cols]
                for h, sl in cols:
                    vn_ref[rs, sl] = vn[h].astype(BF16)
                    o_ref[rs, sl] = both[h][CHUNK:]
                yield
                sm = [sm[h] * cd_s[prev, ci * 8:ci * 8 + 1, sl] + _bdot_tn(kd_s[prev, rs, sl], vn[h]) for h, sl in cols]
                yield
            for h in HEADS:
                state[h] = sm[h]
            for pp in range(SCAN_PAIRS):
                rp = slice(pp * PAIR, (pp + 1) * PAIR)
                intra = [_bdot(att_s[prev, rp, sl], vn_ref[rp, sl]) for sl in HEAD_COLS]
                for h, sl in cols:
                    o_ref[rp, sl] += intra[h]
                yield

        def factors():
            kn = _heads(kn_ref)
            cm = {}
            yield from _pair_common_stages(cm, _heads(qn_ref), kn, _heads(vs_ref), _heads(beta_ref), _heads(g_ref))
            a = _each(lambda kk, d: jnp.where(cm["strict"], kk * d, 0.0), cm["kk"], cm["decay"])
            inv = {}
            yield from _tri_inv_stages(inv, a, cm["eye"].astype(F32))
            tm = inv["t"]
            uw = _each(lambda tv, a, b: _bdot(tv, _side(a, b)), tm, cm["vb"], cm["kbg"])
            res = dict(u=[v[:, :HEAD] for v in uw], w=[v[:, HEAD:] for v in uw],
                       att=_each(lambda a, b: a * b, cm["qk"], cm["decay"]),
                       qd=_each(lambda a, b: a * b, cm["q"], cm["egc"]), kd=_each(lambda a, b: a * b, kn, cm["ekd"]))
            yield
            _put_heads(t_ref, tm)
            for key, out, keep in (("w", w_ref, w_s), ("att", att_ref, att_s), ("qd", qd_ref, qd_s), ("kd", kd_ref, kd_s)):
                _put_heads(out, res[key])
                for (pp, h), v in zip(UNITS, res[key]):
                    keep[cur, pp * PAIR:(pp + 1) * PAIR, HEAD_COLS[h]] = v.astype(BF16)
            for (pp, h), v in zip(UNITS, res["u"]):
                u_s[cur, pp * PAIR:(pp + 1) * PAIR, HEAD_COLS[h]] = v
            for ci in range(2):
                for (pp, h), v in zip(UNITS, cm["cd"]):
                    rows8 = slice(pp * 16 + ci * 8, pp * 16 + (ci + 1) * 8)
                    cd_ref[rows8, HEAD_COLS[h]] = v[ci * CHUNK:ci * CHUNK + 8]
                    cd_s[cur, rows8, HEAD_COLS[h]] = v[ci * CHUNK:ci * CHUNK + 8]
            yield

        _pipelined_step(t, n_steps, lambda: [factors()], lambda: [recurrence()])

    last = n_steps - 1
    now = lambda i: (jnp.minimum(i, last), 0)
    before = lambda i: (jnp.maximum(i - 1, 0), 0)
    rows = lambda index: pl.BlockSpec((SCAN_ROWS, D_HALF), index)
    slot = lambda r, dtype: pltpu.VMEM((2, r, D_HALF), dtype)
    return _call(
        body, name="delta_fwd", grid=(n_steps + 1,),
        in_specs=[rows(now)] * 5,
        out_specs=[rows(now)] * 5 + [_chunk_scalar_spec(SCAN_PAIRS, now), rows(before), rows(before),
                                     pl.BlockSpec((2 * SCAN_PAIRS, N_HEADS, HEAD, HEAD), lambda i: (jnp.maximum(i - 1, 0), 0, 0, 0))],
        out_shape=[_sds((s, D_HALF), BF16)] * 5 + [_sds((s // 8, D_HALF)), _sds((s, D_HALF)), _sds((s, D_HALF), BF16),
                                                  _sds((n_chunks, N_HEADS, HEAD, HEAD))],
        scratch_shapes=[pltpu.VMEM((N_HEADS, HEAD, HEAD), F32), slot(SCAN_ROWS, F32), slot(SCAN_ROWS, BF16),
                        slot(SCAN_ROWS, BF16), slot(SCAN_ROWS, BF16), slot(SCAN_ROWS, BF16), slot(16 * SCAN_PAIRS, F32)],
        compiler_params=_params("arbitrary"),
    )(qn, kn, vs, beta, g)


OUT_T = 512
OUT_ROWS = 256


def _out_fwd_bwd(x, y_pool, o, proj, target, w_out, dn_norm_w, final_norm_w):
    s = x.shape[0]
    t = OUT_T

    def body(x_ref, yp_ref, o_ref, z_ref, tg_ref, wo_ref, dnw_ref, fnw_ref,
             gwo_ref, dh_ref, dyp_ref, do_ref, dz_ref, loss_ref, gfn_ref, gdn_ref, y_ref, yt_ref, gwo_acc):
        @pl.when(pl.program_id(0) == 0)
        def _():
            loss_ref[...] = jnp.zeros_like(loss_ref)
            gfn_ref[...] = jnp.zeros_like(gfn_ref)
            gdn_ref[...] = jnp.zeros_like(gdn_ref)
            gwo_acc[...] = jnp.zeros_like(gwo_acc)

        dnw = dnw_ref[...]
        fnw = fnw_ref[...]

        def stages(rows, lead):
            for _ in range(lead):
                yield
            ypv = yp_ref[rows]
            y_ref[rows, :D_HALF] = ypv.astype(BF16)
            yt_ref[:D_HALF, rows] = ypv.T.astype(BF16)
            keep = []
            for h in HEADS:
                ov = o_ref[rows, HEAD_COLS[h]]
                zv = z_ref[rows, HEAD_COLS[h]]
                ro = lax.rsqrt(jnp.mean(ov * ov, axis=-1, keepdims=True) + EPS)
                ohat = ov * ro
                sg = _sigmoid(zv)
                keep.append((ro, ohat, zv, sg))
                ydn = ohat * dnw * (zv * sg)
                y_ref[rows, D_HALF + h * HEAD:D_HALF + (h + 1) * HEAD] = ydn.astype(BF16)
                yt_ref[D_HALF + h * HEAD:D_HALF + (h + 1) * HEAD, rows] = ydn.T.astype(BF16)
            yield
            hv = x_ref[rows] + jnp.dot(y_ref[rows], wo_ref[...], preferred_element_type=F32)
            yield
            r2 = lax.rsqrt(jnp.mean(hv * hv, axis=-1, keepdims=True) + EPS)
            hhat = hv * r2
            err = hhat * fnw - tg_ref[rows]
            loss_ref[...] += 0.5 * jnp.sum(_rowsum(err * err) * (1.0 / D_MODEL), axis=0, keepdims=True)
            dout = err * (1.0 / D_MODEL)
            gfn_ref[...] += _colsum(dout * hhat)
            dhh = dout * fnw
            dh = r2 * (dhh - hhat * jnp.mean(dhh * hhat, axis=-1, keepdims=True))
            dh_ref[rows] = dh
            yield
            if lead == t // OUT_ROWS - 1:
                gwo_acc[...] += _bdot(yt_ref[...], dh_ref[...])
            dy = _bdot_nt(dh, wo_ref[...])
            yield
            dyp_ref[rows] = dy[:, :D_HALF]
            gdn = jnp.zeros((1, HEAD), F32)
            for h in HEADS:
                ro, ohat, zv, sg = keep[h]
                dyd = dy[:, D_HALF + h * HEAD:D_HALF + (h + 1) * HEAD]
                sz = zv * sg
                dz_ref[rows, HEAD_COLS[h]] = (dyd * ohat * dnw * (sg * (1.0 + zv * (1.0 - sg)))).astype(BF16)
                gdn = gdn + _colsum(dyd * ohat * sz)
                doh = dyd * dnw * sz
                do_ref[rows, HEAD_COLS[h]] = ro * (doh - ohat * jnp.mean(doh * ohat, axis=-1, keepdims=True))
            gdn_ref[...] += gdn
            yield

        _interleave(*[stages(slice(k * OUT_ROWS, (k + 1) * OUT_ROWS), k) for k in range(t // OUT_ROWS)])

        @pl.when(pl.program_id(0) == pl.num_programs(0) - 1)
        def _():
            gwo_ref[...] = gwo_acc[...].astype(BF16)

    wide = pl.BlockSpec((t, D_MODEL), lambda i: (i, 0))
    half = pl.BlockSpec((t, D_HALF), lambda i: (i, 0))
    const = lambda shape: pl.BlockSpec(shape, lambda i: (0,) * len(shape))
    return _call(
        body, name="out_fwd_bwd", grid=(s // t,),
        in_specs=[wide, half, half, pl.BlockSpec((t, D_HALF), lambda i: (i, 5)), wide,
                  const((D_MODEL, D_MODEL)), const((1, HEAD)), const((1, D_MODEL))],
        out_specs=[const((D_MODEL, D_MODEL)), wide, half, half, half,
                   const((1, HEAD)), const((1, D_MODEL)), const((1, HEAD))],
        out_shape=[_sds((D_MODEL, D_MODEL), BF16), _sds((s, D_MODEL)), _sds((s, D_HALF)), _sds((s, D_HALF)),
                   _sds((s, D_HALF), BF16), _sds((1, HEAD)), _sds((1, D_MODEL)), _sds((1, HEAD))],
        scratch_shapes=[pltpu.VMEM((t, D_MODEL), BF16), pltpu.VMEM((D_MODEL, t), BF16), pltpu.VMEM((D_MODEL, D_MODEL), F32)],
        compiler_params=_params("arbitrary"),
    )(x, y_pool, o, proj, target, w_out, dn_norm_w, final_norm_w)


def _grad_w_in(at, pieces):
    m, s = at.shape
    n = len(pieces)
    tn, tk = D_HALF, min(s, 1024)

    def body(a_ref, *refs):
        p_refs, o_ref, acc = refs[:n], refs[n], refs[n + 1]

        @pl.when(pl.program_id(0) == 0)
        def _():
            acc[...] = jnp.zeros_like(acc)

        av = a_ref[...]
        for p in range(n):
            acc[:, p * tn:(p + 1) * tn] += _bdot(av, p_refs[p][...])

        @pl.when(pl.program_id(0) == pl.num_programs(0) - 1)
        def _():
            for j in range(4):
                base = j * BLK_IN // HEAD * HEAD
                win = acc[:, base:base + BLK_IN_PAD]
                if j * BLK_IN > base:
                    win = pltpu.roll(win, BLK_IN_PAD - (j * BLK_IN - base), 1)
                o_ref[j] = win.astype(BF16)

    return _call(
        body, name="grad_w_in", grid=(s // tk,),
        in_specs=[pl.BlockSpec((m, tk), lambda k: (0, k))] + [pl.BlockSpec((tk, tn), lambda k: (k, 0))] * n,
        out_specs=pl.BlockSpec((4, m, BLK_IN_PAD), lambda k: (0, 0, 0)),
        out_shape=_sds((4, m, BLK_IN_PAD), BF16),
        scratch_shapes=[pltpu.VMEM((m, n * tn), F32)],
        compiler_params=_params("arbitrary"),
    )(at, *pieces)


def _delta_bwd(do, vn, qd, kd, w, att, cd, st, qn, kn, vs, beta, g, tm):
    s = do.shape[0]
    n_steps = s // SCAN_ROWS
    assert INTRA_PAIRS == SCAN_PAIRS

    def body(do_ref, vn_ref, qd_ref, kd_ref, w_ref, att_ref, cd_ref, st_ref, qn_ref, kn_ref, vs_ref, beta_ref, g_ref, t_ref,
             dqn_ref, dkn_ref, dvs_ref, dbeta_ref, dg_ref, dstate, du_s, dw_s, datt_s, dqd_s, dkd_s, dcd_s):
        t = pl.program_id(0)

        @pl.when(t == 0)
        def _():
            dstate[...] = jnp.zeros_like(dstate)

        cur = lax.rem(t, 2)
        prev = 1 - cur
        cols = list(enumerate(HEAD_COLS))
        _, incl, _, _ = _pair_masks()

        def recurrence():
            dv_intra = []
            for pp in range(SCAN_PAIRS):
                rp = slice(pp * PAIR, (pp + 1) * PAIR)
                dv_intra.append([_bdot_tn(att_ref[rp, sl], do_ref[rp, sl]) for _, sl in cols])
                for _, sl in cols:
                    datt_s[cur, rp, sl] = jnp.where(incl, _bdot_nt(do_ref[rp, sl], vn_ref[rp, sl]), 0.0)
                yield
            ds = [dstate[h] for h in HEADS]
            for ci in range(2 * SCAN_PAIRS - 1, -1, -1):
                rs = slice(ci * CHUNK, (ci + 1) * CHUNK)
                in_pair = slice((ci % 2) * CHUNK, (ci % 2 + 1) * CHUNK)
                sm = [st_ref[ci, h] for h in HEADS]
                dvn = [dv_intra[ci // 2][h][in_pair] + _bdot(kd_ref[rs, sl], ds[h]) for h, sl in cols]
                dkd = [_bdot_nt(vn_ref[rs, sl], ds[h]) for h, sl in cols]
                dcd = [jnp.broadcast_to(_rowsum(_colsum(ds[h] * sm[h])), (8, HEAD)) for h in HEADS]
                yield
                both = [_bdot_nt(_stack(do_ref[rs, sl], dvn[h]), sm[h]) for h, sl in cols]
                for h, sl in cols:
                    du_s[cur, rs, sl] = dvn[h].astype(BF16)
                    dqd_s[cur, rs, sl] = both[h][:CHUNK]
                    dw_s[cur, rs, sl] = (-both[h][CHUNK:]).astype(BF16)
                    dkd_s[cur, rs, sl] = dkd[h]
                    dcd_s[cur, ci * 8:(ci + 1) * 8, sl] = dcd[h]
                ds = [ds[h] * cd_ref[ci * 8:ci * 8 + 1, sl]
                      + _bdot_tn(_stack(qd_ref[rs, sl], w_ref[rs, sl]), _stack(do_ref[rs, sl], -dvn[h])) for h, sl in cols]
                yield
            for h in HEADS:
                dstate[h] = ds[h]

        def factors(units):
            _heads = functools.partial(_heads_of, units=units)
            _put_heads = functools.partial(_put_heads_of, units=units)
            ones = jnp.ones((2 * PAIR, HEAD), BF16)
            tn = (((0,), (0,)), ((), ()))
            kept = lambda ref, rows=PAIR: [ref[prev, pp * rows:(pp + 1) * rows, HEAD_COLS[h]] for pp, h in units]
            kn, vs, beta = _heads(kn_ref), _heads(vs_ref), _heads(beta_ref)
            cm = {}
            yield from _pair_common_stages(cm, _heads(qn_ref), kn, vs, beta, _heads(g_ref))
            tmv = _heads(t_ref)
            duv, dwv, dattv, dqdv, dkdv = kept(du_s), kept(dw_s), kept(datt_s), kept(dqd_s), kept(dkd_s)
            duw = _each(_side, duv, dwv)
            both = _each(_bdot_tn, tmv, duw)
            dvb, dkbg = [v[:, :HEAD] for v in both], [v[:, HEAD:] for v in both]
            dt = _each(lambda a, b, c: _bdot_nt(a, _side(b, c)), duw, cm["vb"], cm["kbg"])
            yield
            m1 = _each(_bdot_tn, tmv, dt)
            yield
            da = _each(lambda a, b: -jnp.where(cm["strict"], _bdot_nt(a, b), 0.0), m1, tmv)
            yield
            dkk = _each(lambda a, b: a * b, da, cm["decay"])
            dqk = _each(lambda a, b: a * b, dattv, cm["decay"])
            dd = _each(lambda a, b, c, d: a * b + c * d, dkk, cm["kk"], dqk, cm["qk"])
            dkq = _each(_stack, dkk, dqk)
            both = _each(_bdot, dkq, kn)
            dkb = _each(lambda a, c, d: a[:PAIR] + c * d, both, dkbg, cm["egc"])
            dq = _each(lambda a, c, d: a[PAIR:] + c * d, both, dqdv, cm["egc"])
            yield
            dkn = _each(lambda a, b, c: _bdot_tn(a, _stack(b, c)), dkq, cm["kb"], cm["q"])
            dkn = _each(lambda a, b, c, d, e: a + b * c + d * e, dkn, dkdv, cm["ekd"], dkb, beta)
            t_kd = _each(lambda a, b, c: _rowsum(a * b * c), dkdv, kn, cm["ekd"])
            yield
            split = _each(_split, dd)
            rows_dd = [jnp.dot(_side(hi, lo), ones, preferred_element_type=F32) for hi, lo in split]
            cols_dd = [lax.dot_general(_stack(hi, lo), ones, tn, preferred_element_type=F32) for hi, lo in split]
            yield
            dgc = _each(lambda r, c, a, b, e, f, k, tk: r - c + _rowsum(a * b * e) + _rowsum(f * k) - tk,
                        rows_dd, cols_dd, dqdv, cm["q"], cm["egc"], dkbg, cm["kbg"], t_kd)
            same_b = cm["same"].astype(BF16)
            rowi = lax.broadcasted_iota(I32, (PAIR, HEAD), 0)
            dcd = _each(lambda d: jnp.where(rowi < CHUNK, d[0:1], d[8:9]), kept(dcd_s, rows=16))
            dgl = _each(lambda tk, d, c: _mask_dot(same_b, jnp.broadcast_to(tk, (PAIR, HEAD))) + d * c, t_kd, dcd, cm["cd"])
            yield
            is_last = jnp.bitwise_and(rowi, CHUNK - 1) == CHUNK - 1
            dgc = _each(lambda a, b: a + jnp.where(is_last, b, 0.0), dgc, dgl)
            r = lax.broadcasted_iota(I32, (PAIR, PAIR), 0)
            c = lax.broadcasted_iota(I32, (PAIR, PAIR), 1)
            upper_b = (cm["same"] & (r <= c)).astype(BF16)
            _put_heads(dg_ref, _each(lambda v: _mask_dot(upper_b, v), dgc))
            yield
            _put_heads(dbeta_ref, _each(lambda a, b, c, d: jnp.broadcast_to(_rowsum(a * b) + _rowsum(c * d), (PAIR, HEAD)),
                                        dkb, kn, dvb, vs))
            _put_heads(dqn_ref, _each(lambda v: v * QK_SCALE, dq))
            _put_heads(dkn_ref, dkn)
            _put_heads(dvs_ref, _each(lambda a, b: a * b, dvb, beta))
            yield

        _pipelined_step(t, n_steps, lambda: [recurrence()],
                        lambda: [factors(UNITS[pp * N_HEADS:(pp + 1) * N_HEADS]) for pp in range(INTRA_PAIRS)])

    last = n_steps - 1
    now = lambda i: (jnp.maximum(last - i, 0), 0)
    after = lambda i: (jnp.minimum(n_steps - i, last), 0)
    rows = lambda index: pl.BlockSpec((SCAN_ROWS, D_HALF), index)
    slot = lambda r, dtype: pltpu.VMEM((2, r, D_HALF), dtype)
    return _call(
        body, name="delta_bwd", grid=(n_steps + 1,),
        in_specs=[rows(now)] * 6 + [_chunk_scalar_spec(SCAN_PAIRS, now),
                                    pl.BlockSpec((2 * SCAN_PAIRS, N_HEADS, HEAD, HEAD), lambda i: (jnp.maximum(last - i, 0), 0, 0, 0))]
                 + [rows(after)] * 6,
        out_specs=[rows(after)] * 5,
        out_shape=[_sds((s, D_HALF))] * 5,
        scratch_shapes=[pltpu.VMEM((N_HEADS, HEAD, HEAD), F32), slot(SCAN_ROWS, BF16), slot(SCAN_ROWS, BF16),
                        slot(SCAN_ROWS, F32), slot(SCAN_ROWS, F32), slot(SCAN_ROWS, F32), slot(16 * SCAN_PAIRS, F32)],
        compiler_params=_params("arbitrary"),
    )(do, vn, qd, kd, w, att, cd, st, qn, kn, vs, beta, g, tm)


def _fused_call(name, n_steps, parts):
    n_in = [len(p["inputs"]) for p in parts]
    n_out = [len(p["out_shape"]) for p in parts]
    n_scr = [len(p["scratch"]) for p in parts]

    def body(*refs):
        ins, outs, scr = refs[:sum(n_in)], refs[sum(n_in):sum(n_in) + sum(n_out)], refs[sum(n_in) + sum(n_out):]
        gens, a, b, c = [], 0, 0, 0
        for p, ni, no, ns in zip(parts, n_in, n_out, n_scr):
            gens.append(p["stages"](ins[a:a + ni], outs[b:b + no], scr[c:c + ns]))
            a, b, c = a + ni, b + no, c + ns
        _interleave(*gens)

    flat = lambda key: [v for p in parts for v in p[key]]
    res = _call(
        body, name=name, grid=(n_steps,),
        in_specs=flat("in_specs"), out_specs=flat("out_specs"), out_shape=flat("out_shape"),
        scratch_shapes=flat("scratch"),
        compiler_params=_params("arbitrary"),
    )(*flat("inputs"))
    out, b = [], 0
    for no in n_out:
        out.append(res[b:b + no])
        b += no
    return out


def _pool_bwd_part(proj, dyp, pool_w, pool_scale, tile_of, n_tiles):
    s = proj.shape[0]
    t = POOL_T
    hb = t // HEAD
    last = s // HEAD - 1

    def stages(ins, outs, scratch):
        u_ref, z_ref, halo_ref, dy_ref, zn_ref, dyn_ref, pw_ref, ps_ref, band_ref, aband_ref = ins
        du_ref, dz_ref, gpw_ref, gps_ref = outs
        tile = tile_of(pl.program_id(0))

        @pl.when(pl.program_id(0) == 0)
        def _():
            gpw_ref[...] = jnp.zeros_like(gpw_ref)
            gps_ref[...] = jnp.zeros_like(gps_ref)

        live = (tile > 0).astype(F32)
        more = (tile < n_tiles - 1).astype(F32)
        groups = lambda ref: [ref[:, sl] for sl in HEAD_COLS]
        z, ps, dy = groups(z_ref), groups(ps_ref), groups(dy_ref)
        pw = [pw_ref[g] for g in HEADS]
        mix, mixed, sg, cnt = _pool_mix(groups(u_ref), [h * live for h in groups(halo_ref)], z, pw,
                                        [band_ref[g] for g in HEADS], tile * t)
        yield
        sz = _each(lambda a, b: a * b, z, sg)
        for sl, d, m, p, s_, zg in zip(HEAD_COLS, dy, mixed, ps, sg, z):
            dz_ref[:, sl] = (d * m * p * (s_ * (1.0 + zg * (1.0 - s_)))).astype(BF16)
        for sl, d, m, a in zip(HEAD_COLS, dy, mixed, sz):
            gps_ref[:, sl] += _colsum(d * m * a)
        dmixed = _each(lambda d, p, a: d * p * a, dy, ps, sz)
        yield
        for g, gp in enumerate(_each(_bdot_tn, mix, dmixed)):
            gpw_ref[g] += gp
        dmix = _each(_bdot_nt, dmixed, pw)
        yield
        dmix_n = _each(lambda d, p, zn, w_: _bdot_nt(d * more * p * (zn * _sigmoid(zn)), w_),
                       groups(dyn_ref), ps, groups(zn_ref), pw)
        yield
        scaled = [jnp.concatenate([a / c, b * (1.0 / w)], axis=0) for a, c, b, w in zip(dmix, cnt, dmix_n, WINDOWS)]
        du = _each(lambda b, s_, d: _mask_dot(b, s_) - d, [aband_ref[g] for g in HEADS], scaled, dmix)
        for sl, v in zip(HEAD_COLS, du):
            du_ref[:, sl] = v.astype(BF16)
        yield

    tile = lambda col: pl.BlockSpec((t, D_HALF), lambda i: (tile_of(i), col))
    below = lambda col: pl.BlockSpec((HEAD, D_HALF), lambda i: (jnp.minimum((tile_of(i) + 1) * hb, last), col))
    const3 = lambda shape: pl.BlockSpec(shape, lambda i: (0, 0, 0))
    return dict(
        inputs=[proj, proj, proj, dyp, proj, dyp, pool_w, pool_scale, _pool_bands(t), _pool_bands(t, anti=True)],
        in_specs=[tile(0), tile(1), pl.BlockSpec((HEAD, D_HALF), lambda i: (jnp.maximum(tile_of(i) * hb - 1, 0), 0)),
                  tile(0), below(1), below(0), const3((N_HEADS, HEAD, HEAD)), pl.BlockSpec((1, D_HALF), lambda i: (0, 0)),
                  const3((N_HEADS, t, HEAD + t)), const3((N_HEADS, t, HEAD + t))],
        out_specs=[tile(0), tile(0), const3((N_HEADS, HEAD, HEAD)), pl.BlockSpec((1, D_HALF), lambda i: (0, 0))],
        out_shape=[_sds((s, D_HALF), BF16), _sds((s, D_HALF), BF16), _sds((N_HEADS, HEAD, HEAD)), _sds((1, D_HALF))],
        scratch=[], stages=stages)


def _conv_bwd_part(proj, pre, conv_w, a_log, dt_bias, dqn, dkn, dvs, dbeta, dg, tile_of, n_tiles):
    s = proj.shape[0]
    t = CONV_T

    def stages(ins, outs, scratch):
        (q_ref, k_ref, v_ref, yq_ref, yk_ref, yv_ref, ba_ref, cw_ref, al_ref, dtb_ref,
         dqn_ref, dkn_ref, dvs_ref, dbeta_ref, dg_ref) = ins
        oq_ref, ok_ref, ov_ref, dba_ref, gcw_out, gsm_out = outs
        below, gcw_ref, gsm_ref = scratch
        step = pl.program_id(0)

        @pl.when(step == 0)
        def _():
            gcw_ref[...] = jnp.zeros_like(gcw_ref)
            gsm_ref[...] = jnp.zeros_like(gsm_ref)
            below[...] = jnp.zeros_like(below)

        parts = ((q_ref, yq_ref, dqn_ref, oq_ref), (k_ref, yk_ref, dkn_ref, ok_ref), (v_ref, yv_ref, dvs_ref, ov_ref))
        for p, (x_ref, y_ref, d_ref, o_ref) in enumerate(parts):
            for h in HEADS:
                cs = HEAD_COLS[h]
                wide = slice(p * D_HALF + h * HEAD, p * D_HALF + (h + 1) * HEAD)
                cw = cw_ref[:, wide]
                y = y_ref[:, cs]
                sg = _sigmoid(y)
                sv = y * sg
                ds = d_ref[:, cs]
                if p < 2:
                    rn = lax.rsqrt(_rowsum(sv * sv) + EPS)
                    nrm = sv * rn
                    ds = rn * (ds - nrm * _rowsum(ds * nrm))
                dy = ds * (sg * (1.0 + y * (1.0 - sg)))
                nxt = below[:, wide]
                ahead = [dy] + [_shift_up(dy, nxt, sft) for sft in range(1, CONV_K)]
                xv = x_ref[:, cs]
                acc = dy * cw[CONV_K - 1:CONV_K]
                for sft in range(1, CONV_K):
                    acc = acc + ahead[sft] * cw[CONV_K - 1 - sft:CONV_K - sft]
                for j in range(CONV_K):
                    gcw_ref[8 * j:8 * j + 8, wide] += _rows8(xv * ahead[CONV_K - 1 - j])
                o_ref[:, cs] = acc.astype(BF16)
                below[:, wide] = dy[0:8]
                yield

        ba = ba_ref[...]
        lane = lax.broadcasted_iota(I32, (t, HEAD), 1)
        lane8 = lax.broadcasted_iota(I32, (8, HEAD), 1)
        dba = jnp.zeros((t, HEAD), F32)
        gsm = jnp.zeros((8, HEAD), F32)
        for h in HEADS:
            beta = _sigmoid(ba[:, h:h + 1])
            dbeta = dbeta_ref[:, h * HEAD:h * HEAD + 1]
            xg = ba[:, N_HEADS + h:N_HEADS + h + 1] + dtb_ref[0:1, h:h + 1]
            nexp = -jnp.exp(al_ref[0:1, h:h + 1])
            dgv = dg_ref[:, h * HEAD:h * HEAD + 1]
            da = dgv * nexp * _sigmoid(xg)
            dba = dba + jnp.where(lane == h, dbeta * beta * (1.0 - beta), 0.0) + jnp.where(lane == N_HEADS + h, da, 0.0)
            gsm = (gsm + jnp.where(lane8 == h, _rows8(dgv * nexp * _softplus(xg)), 0.0)
                   + jnp.where(lane8 == N_HEADS + h, _rows8(da), 0.0))
        dba_ref[...] = jnp.zeros_like(dba_ref)
        dba_ref[:, :HEAD] = dba.astype(BF16)
        gsm_ref[...] += gsm
        yield

        @pl.when(step == n_tiles - 1)
        def _():
            gcw_out[...] = jnp.zeros_like(gcw_out)
            for j in range(CONV_K):
                gcw_out[j:j + 1, :] = _colsum(gcw_ref[8 * j:8 * j + 8, :])
            gsm_out[...] = jnp.broadcast_to(_colsum(gsm_ref[...]), (8, HEAD))

    row = pl.BlockSpec((t, D_HALF), lambda i: (tile_of(i), 0))
    const = lambda shape: pl.BlockSpec(shape, lambda i: (0, 0))
    return dict(
        inputs=[proj] * 3 + list(pre) + [proj, conv_w, a_log, dt_bias, dqn, dkn, dvs, dbeta, dg],
        in_specs=_conv_specs(t, tile_of)[:3] + [row] * 3
                 + [pl.BlockSpec((t, HEAD), lambda i: (tile_of(i), COL_BA // HEAD)),
                    const((CONV_K, 3 * D_HALF)), const((1, N_HEADS)), const((1, N_HEADS))] + [row] * 5,
        out_specs=[row, row, row, row, const((8, 3 * D_HALF)), const((8, HEAD))],
        out_shape=[_sds((s, D_HALF), BF16)] * 4 + [_sds((8, 3 * D_HALF)), _sds((8, HEAD))],
        scratch=[pltpu.VMEM((8, 3 * D_HALF), F32), pltpu.VMEM((8 * CONV_K, 3 * D_HALF), F32), pltpu.VMEM((8, HEAD), F32)],
        stages=stages)


def _pool_fwd_part(proj, pool_w, pool_scale):
    s = proj.shape[0]
    t = POOL_T
    hb = t // HEAD

    def stages(ins, outs, scratch, tile=None):
        u_ref, z_ref, halo_ref, pw_ref, ps_ref, band_ref = ins
        y_ref, = outs
        i = pl.program_id(0) if tile is None else tile
        live = (i > 0).astype(F32)
        groups = lambda ref: [ref[:, sl] for sl in HEAD_COLS]
        z = groups(z_ref)
        u, halo = groups(u_ref), [h * live for h in groups(halo_ref)]
        yield
        _, mixed, sg, _ = _pool_mix(u, halo, z, [pw_ref[g] for g in HEADS], [band_ref[g] for g in HEADS], i * t)
        yield
        for sl, m, zg, s_ in zip(HEAD_COLS, mixed, z, sg):
            y_ref[:, sl] = m * ps_ref[:, sl] * (zg * s_)
        yield

    const3 = lambda shape: pl.BlockSpec(shape, lambda i: (0, 0, 0))
    return dict(
        inputs=[proj, proj, proj, pool_w, pool_scale, _pool_bands(t)],
        in_specs=[pl.BlockSpec((t, D_HALF), lambda i: (i, 0)), pl.BlockSpec((t, D_HALF), lambda i: (i, 1)),
                  pl.BlockSpec((HEAD, D_HALF), lambda i: (jnp.maximum(i * hb - 1, 0), 0)),
                  const3((N_HEADS, HEAD, HEAD)), pl.BlockSpec((1, D_HALF), lambda i: (0, 0)), const3((N_HEADS, t, HEAD + t))],
        out_specs=[pl.BlockSpec((t, D_HALF), lambda i: (i, 0))], out_shape=[_sds((s, D_HALF))],
        scratch=[], stages=stages)


def _conv_fwd_part(proj, conv_w, a_log, dt_bias):
    s = proj.shape[0]
    t = CONV_T

    def stages(ins, outs, scratch, tile=None):
        q_ref, k_ref, v_ref, hq_ref, hk_ref, hv_ref, ba_ref, cw_ref, al_ref, dtb_ref = ins
        qn_ref, kn_ref, vs_ref, beta_ref, g_ref, yq_ref, yk_ref, yv_ref = outs
        live = ((pl.program_id(0) if tile is None else tile) > 0).astype(F32)
        parts = ((q_ref, hq_ref, qn_ref, yq_ref), (k_ref, hk_ref, kn_ref, yk_ref), (v_ref, hv_ref, vs_ref, yv_ref))
        for p, (x_ref, h_ref, o_ref, y_ref) in enumerate(parts):
            for h in HEADS:
                cs = HEAD_COLS[h]
                taps = _conv_taps(x_ref[:, cs], h_ref[:, cs] * live)
                y = _conv_pre(taps, cw_ref[:, p * D_HALF + h * HEAD:p * D_HALF + (h + 1) * HEAD])
                y_ref[:, cs] = y
                sv = y * _sigmoid(y)
                o_ref[:, cs] = sv if p == 2 else sv * lax.rsqrt(_rowsum(sv * sv) + EPS)
                yield
        ba = ba_ref[...]
        for h in HEADS:
            beta = _sigmoid(ba[:, h:h + 1])
            gl = -jnp.exp(al_ref[0:1, h:h + 1]) * _softplus(ba[:, N_HEADS + h:N_HEADS + h + 1] + dtb_ref[0:1, h:h + 1])
            beta_ref[:, HEAD_COLS[h]] = jnp.broadcast_to(beta, (t, HEAD))
            g_ref[:, HEAD_COLS[h]] = jnp.broadcast_to(gl, (t, HEAD))
        yield

    row = pl.BlockSpec((t, D_HALF), lambda i: (i, 0))
    const = lambda shape: pl.BlockSpec(shape, lambda i: (0, 0))
    return dict(
        inputs=[proj] * 7 + [conv_w, a_log, dt_bias],
        in_specs=_conv_specs(t) + [pl.BlockSpec((t, HEAD), lambda i: (i, COL_BA // HEAD)),
                                   const((CONV_K, 3 * D_HALF)), const((1, N_HEADS)), const((1, N_HEADS))],
        out_specs=[row] * 8, out_shape=[_sds((s, D_HALF))] * 8, scratch=[], stages=stages)


def _front_fwd(x, norm_w, w_pad, conv_w, a_log, dt_bias, pool_w, pool_scale, after):
    s = x.shape[0]
    t = CONV_T
    n_tiles = s // t
    assert POOL_T == CONV_T
    like_proj = _sds((s, N_IN_PAD))
    conv = _conv_fwd_part(like_proj, conv_w, a_log, dt_bias)
    pool = _pool_fwd_part(like_proj, pool_w, pool_scale)
    bands = pool["inputs"][-1]
    mxu_n = 256
    col_bounds = list(range(0, N_IN_PAD, 3 * mxu_n)) + [N_IN_PAD]

    def body(x_ref, nw_ref, w_ref, cw_ref, al_ref, dtb_ref, pw_ref, ps_ref, band_ref, after_ref,
             proj_ref, nt_ref, qn_ref, kn_ref, vs_ref, beta_ref, g_ref, yq_ref, yk_ref, yv_ref, y_ref, prev):
        del after_ref
        i = pl.program_id(0)

        @pl.when(i == 0)
        def _():
            prev[...] = jnp.zeros_like(prev)

        tile = jnp.maximum(i - 1, 0)
        main, above8, above = pl.ds(HEAD, t), pl.ds(HEAD - 8, 8), pl.ds(0, HEAD)
        cols = lambda rows, c0, width=D_HALF: prev.at[rows, pl.ds(c0, width)]
        conv_ins = (cols(main, 2 * D_HALF), cols(main, 3 * D_HALF), cols(main, 4 * D_HALF),
                    cols(above8, 2 * D_HALF), cols(above8, 3 * D_HALF), cols(above8, 4 * D_HALF),
                    cols(main, COL_BA, HEAD), cw_ref, al_ref, dtb_ref)
        pool_ins = (cols(main, 0), cols(main, D_HALF), cols(above, 0), pw_ref, ps_ref, band_ref)

        def projection():
            xv = x_ref[...]
            r = lax.rsqrt(jnp.mean(xv * xv, axis=-1, keepdims=True) + EPS)
            nv = xv * r * nw_ref[...]
            nt_ref[...] = nv.T.astype(BF16)
            nb = nv.astype(BF16)
            yield
            for lo, hi in zip(col_bounds[:-1], col_bounds[1:]):
                proj_ref[:, lo:hi] = jnp.dot(nb, w_ref[:, lo:hi], preferred_element_type=F32)
                yield

        _interleave(projection(),
                    conv["stages"](conv_ins, (qn_ref, kn_ref, vs_ref, beta_ref, g_ref, yq_ref, yk_ref, yv_ref), (), tile),
                    pool["stages"](pool_ins, (y_ref,), (), tile))
        prev[0:HEAD] = prev[t:t + HEAD]
        prev[HEAD:HEAD + t] = proj_ref[...]

    last = n_tiles - 1
    now = lambda i: (jnp.minimum(i, last), 0)
    before = lambda i: (jnp.maximum(i - 1, 0), 0)
    const = lambda a: pl.BlockSpec(a.shape, lambda i: (0,) * a.ndim)
    half = pl.BlockSpec((t, D_HALF), before)
    return _call(
        body, name="front_fwd", grid=(n_tiles + 1,),
        in_specs=[pl.BlockSpec((t, D_MODEL), now), const(norm_w), const(w_pad), const(conv_w), const(a_log), const(dt_bias),
                  const(pool_w), const(pool_scale), const(bands), pl.BlockSpec(memory_space=pl.ANY)],
        out_specs=[pl.BlockSpec((t, N_IN_PAD), now), pl.BlockSpec((D_MODEL, t), lambda i: (0, jnp.minimum(i, last)))]
                  + [half] * 9,
        out_shape=[_sds((s, N_IN_PAD)), _sds((D_MODEL, s), BF16)] + [_sds((s, D_HALF))] * 9,
        scratch_shapes=[pltpu.VMEM((HEAD + t, N_IN_PAD), F32)],
        compiler_params=_params("arbitrary"),
    )(x, norm_w, w_pad, conv_w, a_log, dt_bias, pool_w, pool_scale, bands, after)


def _conv_pool_bwd(proj, pre, conv_w, a_log, dt_bias, dqn, dkn, dvs, dbeta, dg, dyp, pool_w, pool_scale):
    n_tiles = proj.shape[0] // CONV_T
    assert POOL_T == CONV_T
    tile_of = lambda i: n_tiles - 1 - i
    return _fused_call("conv_pool_bwd", n_tiles, [
        _conv_bwd_part(proj, pre, conv_w, a_log, dt_bias, dqn, dkn, dvs, dbeta, dg, tile_of, n_tiles),
        _pool_bwd_part(proj, dyp, pool_w, pool_scale, tile_of, n_tiles)])


def _rows8(x):
    acc = x[0:8]
    for r in range(8, x.shape[0], 8):
        acc = acc + x[r:r + 8]
    return acc


IN_T = 512


def _in_bwd(x, dh, norm_w, w_pad, pieces, after):
    s = x.shape[0]
    t = IN_T
    widths = [D_HALF] * 6 + [N_IN_PAD - COL_BA]
    starts = [sum(widths[:k]) for k in range(len(widths))]

    def body(*refs):
        x_ref, dh_ref, nw_ref, w_hbm = refs[:4]
        p_refs = refs[4:4 + len(pieces)]
        gx_ref, gnw_ref, w_ref, arrived = refs[5 + len(pieces):]
        first = pl.program_id(0) == 0
        cols = [pl.ds(c, wd) for c, wd in zip(starts, widths)]
        copies = [pltpu.make_async_copy(w_hbm.at[:, cs], w_ref.at[:, cs], arrived.at[k]) for k, cs in enumerate(cols)]

        @pl.when(first)
        def _():
            for cp in copies:
                cp.start()
            gnw_ref[...] = jnp.zeros_like(gnw_ref)

        def step(wait):
            dn = jnp.zeros((t, D_MODEL), F32)
            for k, (p_ref, c, wd) in enumerate(zip(p_refs, starts, widths)):
                if wait:
                    copies[k].wait()
                dn = dn + _bdot_nt(p_ref[...], w_ref[:, c:c + wd])
            xv = x_ref[...]
            r = lax.rsqrt(jnp.mean(xv * xv, axis=-1, keepdims=True) + EPS)
            xhat = xv * r
            gnw_ref[...] += _colsum(dn * xhat)
            dxh = dn * nw_ref[...]
            gx_ref[...] = dh_ref[...] + r * (dxh - xhat * jnp.mean(dxh * xhat, axis=-1, keepdims=True))

        pl.when(first)(functools.partial(step, True))
        pl.when(jnp.logical_not(first))(functools.partial(step, False))

    wide = pl.BlockSpec((t, D_MODEL), lambda i: (i, 0))
    hbm = pl.BlockSpec(memory_space=pl.ANY)
    return _call(
        body, name="in_bwd", grid=(s // t,),
        in_specs=[wide, wide, pl.BlockSpec((1, D_MODEL), lambda i: (0, 0)), hbm]
                 + [pl.BlockSpec((t, wd), lambda i: (i, 0)) for wd in widths] + [hbm],
        out_specs=[wide, pl.BlockSpec((1, D_MODEL), lambda i: (0, 0))],
        out_shape=[_sds((s, D_MODEL)), _sds((1, D_MODEL))],
        scratch_shapes=[pltpu.VMEM((D_MODEL, N_IN_PAD), BF16), pltpu.SemaphoreType.DMA((len(widths),))],
        compiler_params=_params("arbitrary"),
    )(x, dh, norm_w, w_pad, *pieces, after)


def _adamw_shard(name, w, g_own, g_got, cidx, m, v):
    _, r, c = w.shape
    half = r // 2
    rows = 256 if half % 256 == 0 else half
    per_half = half // rows

    def body(c_ref, w_ref, go_ref, gg_ref, m_ref, v_ref, gout_ref, d_ref, nm_ref, nv_ref):
        mine = (pl.program_id(0) // per_half) == c_ref[0]
        gv = jnp.where(mine, go_ref[:, :c], gg_ref[:, :c])
        gout_ref[0] = gv
        mn = ADAM_B1 * m_ref[0] + (1.0 - ADAM_B1) * gv
        vn = ADAM_B2 * v_ref[0] + (1.0 - ADAM_B2) * (gv * gv)
        m_hat = mn / (1.0 - ADAM_B1 ** ADAM_STEP)
        v_hat = vn / (1.0 - ADAM_B2 ** ADAM_STEP)
        d_ref[0] = -ADAM_LR * (m_hat / (jnp.sqrt(v_hat) + ADAM_EPS) + ADAM_WD * w_ref[0])
        nm_ref[0] = mn
        nv_ref[0] = vn

    blk = pl.BlockSpec((1, rows, c), lambda i, c_ref: (0, i, 0))
    gblk = pl.BlockSpec((rows, g_own.shape[1]), lambda i, c_ref: (i % per_half, 0))
    return _call(
        body, name=name,
        grid_spec=pltpu.PrefetchScalarGridSpec(
            num_scalar_prefetch=1, grid=(2 * per_half,),
            in_specs=[blk, gblk, gblk, blk, blk], out_specs=[blk] * 4),
        out_shape=[_sds((1, r, c))] * 4,
        compiler_params=_params("arbitrary"),
    )(cidx, w, g_own, g_got, m, v)


def _adamw_tiles(name, w, g, m, v):
    n = w.shape[0]
    nb = 77 if n % 77 == 0 else n

    def body(w_ref, g_ref, m_ref, v_ref, d_ref, nm_ref, nv_ref):
        gv = g_ref[...]
        mn = ADAM_B1 * m_ref[...] + (1.0 - ADAM_B1) * gv
        vn = ADAM_B2 * v_ref[...] + (1.0 - ADAM_B2) * (gv * gv)
        m_hat = mn / (1.0 - ADAM_B1 ** ADAM_STEP)
        v_hat = vn / (1.0 - ADAM_B2 ** ADAM_STEP)
        d_ref[...] = -ADAM_LR * (m_hat / (jnp.sqrt(v_hat) + ADAM_EPS) + ADAM_WD * w_ref[...])
        nm_ref[...] = mn
        nv_ref[...] = vn

    blk = pl.BlockSpec((nb, 8, HEAD), lambda i: (i, 0, 0))
    return _call(
        body, name=name, grid=(n // nb,),
        in_specs=[blk] * 4, out_specs=[blk] * 3, out_shape=[_sds(w.shape)] * 3,
        compiler_params=_params("arbitrary"),
    )(w, g, m, v)


def _make_copy(src, dst, send, recv, target):
    if target is None:
        return pltpu.make_async_copy(src, dst, recv)
    return pltpu.make_async_remote_copy(src_ref=src, dst_ref=dst, send_sem=send, recv_sem=recv,
                                        device_id=target, device_id_type=pl.DeviceIdType.MESH)


def _exchange(name, inputs, out_shapes, phases):
    n_in = len(inputs)
    n_out = len(out_shapes)
    n_cp = sum(len(p) for p in phases)

    def body(*refs):
        ins, outs = refs[:n_in], refs[n_in:n_in + n_out]
        send, recv = refs[n_in + n_out:]
        pos = (lax.axis_index("x"), lax.axis_index("y"), lax.axis_index("c"))
        k = 0
        for phase in phases:
            cps = []
            for src, dst, target in phase:
                cps.append(_make_copy(src(ins, outs, pos), dst(ins, outs, pos), send.at[k], recv.at[k],
                                      target and target(pos)))
                k += 1
            for cp in cps:
                cp.start()
            for cp in cps:
                cp.wait()

    anyspec = pl.BlockSpec(memory_space=pl.ANY)
    return _call(
        body, name=name,
        in_specs=[anyspec] * n_in, out_specs=[anyspec] * n_out, out_shape=list(out_shapes),
        scratch_shapes=[pltpu.SemaphoreType.DMA((n_cp,)), pltpu.SemaphoreType.DMA((n_cp,))],
    )(*inputs)


def _exchange_start(name, inputs, out_shapes, copies):
    n_in, n_out, n_cp = len(inputs), len(out_shapes), len(copies)

    def body(*refs):
        ins, lands = refs[:n_in], refs[n_in:n_in + n_out]
        sems = refs[n_in + n_out:n_in + n_out + 2 * n_cp]
        token = refs[-1]
        pos = (lax.axis_index("x"), lax.axis_index("y"), lax.axis_index("c"))
        for k, (src, dst, target) in enumerate(copies):
            _make_copy(src(ins, lands, pos), dst(ins, lands, pos), sems[2 * k], sems[2 * k + 1],
                       target and target(pos)).start()
        token[...] = jnp.zeros_like(token)

    hbm = pl.BlockSpec(memory_space=pltpu.HBM)
    sem = pl.BlockSpec(memory_space=pltpu.SEMAPHORE)
    bufs = list(inputs) + [lax.empty(o.shape, o.dtype) for o in out_shapes]
    outs = _call(
        body, name=name,
        out_shape=tuple([pltpu.SemaphoreType.DMA(())] * (2 * n_cp) + [pltpu.HBM(b.shape, b.dtype) for b in bufs]
                        + [_sds((8, HEAD))]),
        in_specs=[hbm] * len(bufs),
        out_specs=tuple([sem] * (2 * n_cp) + [hbm] * len(bufs) + [pl.BlockSpec(memory_space=pltpu.VMEM)]),
        input_output_aliases={i: 2 * n_cp + i for i in range(len(bufs))},
        compiler_params=pltpu.CompilerParams(has_side_effects=pltpu.SideEffectType.DATAFLOW_SIDE_EFFECTING),
    )(*[pltpu.with_memory_space_constraint(b, pltpu.HBM) for b in bufs])
    return outs[:2 * n_cp], outs[2 * n_cp:2 * n_cp + n_in], outs[2 * n_cp + n_in:-1], outs[-1]


def _exchange_wait(name, sems, sources, lands, copies, after):
    n_in, n_out, n_cp = len(sources), len(lands), len(copies)

    def body(*refs):
        ins, zones = refs[:n_in], refs[n_in:n_in + n_out]
        sem_refs = refs[n_in + n_out:n_in + n_out + 2 * n_cp]
        pos = (lax.axis_index("x"), lax.axis_index("y"), lax.axis_index("c"))
        for k, (src, dst, target) in enumerate(copies):
            cp = _make_copy(src(ins, zones, pos), dst(ins, zones, pos), sem_refs[2 * k], sem_refs[2 * k + 1],
                            target and target(pos))
            if target is None:
                cp.wait()
            else:
                cp.wait_send()
                cp.wait_recv()

    hbm = pl.BlockSpec(memory_space=pltpu.HBM)
    sem = pl.BlockSpec(memory_space=pltpu.SEMAPHORE)
    bufs = list(sources) + list(lands)
    outs = _call(
        body, name=name,
        out_shape=tuple(pltpu.HBM(b.shape, b.dtype) for b in bufs),
        in_specs=[hbm] * len(bufs) + [sem] * (2 * n_cp) + [pl.BlockSpec(memory_space=pl.ANY)],
        out_specs=tuple([hbm] * len(bufs)),
        input_output_aliases={i: i for i in range(len(bufs))},
        compiler_params=pltpu.CompilerParams(has_side_effects=pltpu.SideEffectType.DATAFLOW_SIDE_EFFECTING),
    )(*bufs, *sems, after)
    return outs[:n_in], outs[n_in:]


def _allreduce_tile(name, v):
    def body(v_ref, out_ref, slots, send, recv):
        x, y, c = lax.axis_index("x"), lax.axis_index("y"), lax.axis_index("c")
        me = 4 * x + 2 * y + c
        slots[me] = v_ref[...]
        cps = []
        for k in range(1, 8):
            peer = (x ^ (k >> 2), y ^ ((k >> 1) & 1), c ^ (k & 1))
            cps.append(pltpu.make_async_remote_copy(
                src_ref=v_ref, dst_ref=slots.at[me], send_sem=send.at[k - 1], recv_sem=recv.at[k - 1],
                device_id=peer, device_id_type=pl.DeviceIdType.MESH))
        for cp in cps:
            cp.start()
        for cp in cps:
            cp.wait()
        acc = slots[0]
        for i in range(1, 8):
            acc = acc + slots[i]
        out_ref[...] = acc

    vm = pl.BlockSpec(memory_space=pltpu.VMEM)
    return _call(
        body, name=name, in_specs=[vm], out_specs=vm, out_shape=_sds(v.shape),
        scratch_shapes=[pltpu.VMEM((8,) + v.shape, F32), pltpu.SemaphoreType.DMA((7,)), pltpu.SemaphoreType.DMA((7,))],
    )(v)


def _chip(pos):
    return 2 * pos[0] + pos[1]


def _other_chip(pos, mask):
    x, y, c = pos
    return (x ^ (mask >> 1), y ^ (mask & 1), c)


def _sibling(pos):
    return (pos[0], pos[1], 1 - pos[2])


def _gather_weights(wb, cb):
    rows = wb.shape[0] // 2
    x_nb, y_nb, diag = CHIP_MASKS

    def part(pos, mask, quarter=None):
        start = pos[2] * rows if quarter is None else pos[2] * rows + quarter * (rows // 2)
        return lambda outs: outs[0].at[_chip(pos) ^ mask, pl.ds(start, rows if quarter is None else rows // 2)]

    def passed_on(mask, to, quarter=None):
        return (lambda ins, outs, pos: part(pos, mask, quarter)(outs), lambda ins, outs, pos: part(pos, mask, quarter)(outs), to)

    first = [(lambda ins, outs, pos: ins[0].at[pl.ds(pos[2] * rows, rows)], lambda ins, outs, pos: part(pos, 0)(outs),
              functools.partial(_other_chip, mask=mask)) for mask in (x_nb, y_nb)]
    conv_cols = lambda ins, outs, pos: outs[1].at[:, pl.ds(pl.multiple_of(_chip(pos) * cb.shape[1], HEAD), cb.shape[1])]
    first += [(lambda ins, outs, pos: ins[1], conv_cols, functools.partial(_other_chip, mask=mask)) for mask in CHIP_MASKS]
    first += [(lambda ins, outs, pos: ins[1], conv_cols, None)]
    second = [passed_on(x_nb, functools.partial(_other_chip, mask=y_nb), quarter=0),
              passed_on(y_nb, functools.partial(_other_chip, mask=x_nb), quarter=1),
              passed_on(x_nb, _sibling), passed_on(y_nb, _sibling)]
    third = [passed_on(diag, _sibling)]
    return _exchange("gather_weights", [wb, cb],
                     [_sds((4,) + wb.shape, wb.dtype), _sds((cb.shape[0], 4 * cb.shape[1]), cb.dtype)], [first, second, third])


def _assemble_w_in(gw, wb, jidx):
    m = gw.shape[1]

    def body(j_ref, g_ref, wb_ref, o_ref):
        step = pl.program_id(0)

        @pl.when(step == 0)
        def _():
            o_ref[...] = jnp.zeros_like(o_ref)

        blk = jnp.where(step == j_ref[0], wb_ref[...], g_ref[0]).astype(F32)
        lane = lax.broadcasted_iota(I32, (m, BLK_IN_PAD), 1)
        for j in range(4):
            @pl.when(step == j)
            def _(j=j):
                base = j * BLK_IN // HEAD * HEAD
                shift = j * BLK_IN - base
                moved = pltpu.roll(blk, shift, 1) if shift else blk
                window = o_ref[:, base:base + BLK_IN_PAD].astype(F32)
                mine = (lane >= shift) & (lane < shift + BLK_IN)
                o_ref[:, base:base + BLK_IN_PAD] = jnp.where(mine, moved, window).astype(BF16)

    return _call(
        body, name="assemble_w_in",
        grid_spec=pltpu.PrefetchScalarGridSpec(
            num_scalar_prefetch=1, grid=(4,),
            in_specs=[pl.BlockSpec((1, m, BLK_IN_PAD), lambda j, j_ref: (j, 0, 0)),
                      pl.BlockSpec((m, BLK_IN_PAD), lambda j, j_ref: (0, 0))],
            out_specs=pl.BlockSpec((m, N_IN_PAD), lambda j, j_ref: (0, 0))),
        out_shape=_sds((m, N_IN_PAD), BF16),
        compiler_params=_params("arbitrary"),
    )(jidx, gw, wb)


def _gather_blocks(ob):
    copies = [(lambda ins, outs, pos: ins[0], lambda ins, outs, pos: outs[0].at[_chip(pos)],
               functools.partial(_other_chip, mask=mask)) for mask in CHIP_MASKS]
    copies.append((lambda ins, outs, pos: ins[0], lambda ins, outs, pos: outs[0].at[_chip(pos)], None))
    return [_sds((4,) + ob.shape, ob.dtype)], copies


def _reduce_sibling(name, arrays):
    n = len(arrays)
    halves = [a.shape[:-2] + (a.shape[-2] // 2, a.shape[-1]) for a in arrays]
    pieces = [(a, j) for a in range(n) for j in (range(arrays[a].shape[0]) if arrays[a].ndim == 3 else [None])]

    def body(*refs):
        whole, outs = refs[:n], refs[n:2 * n]
        own, land, summed = refs[2 * n:3 * n], refs[3 * n:4 * n], refs[4 * n:5 * n]
        send, recv, loaded, stored = refs[5 * n:]
        pos = (lax.axis_index("x"), lax.axis_index("y"), lax.axis_index("c"))
        block = lambda ref, j, rows=None: ref.at[(() if j is None else (j,)) + (() if rows is None else (rows,))]
        arrive, load = [], []
        for k, (a, j) in enumerate(pieces):
            h = halves[a][-2]
            arrive.append(pltpu.make_async_remote_copy(
                src_ref=block(whole[a], j, pl.ds((1 - pos[2]) * h, h)), dst_ref=block(land[a], j),
                send_sem=send.at[k], recv_sem=recv.at[k],
                device_id=_sibling(pos), device_id_type=pl.DeviceIdType.MESH))
            load.append(pltpu.make_async_copy(block(whole[a], j, pl.ds(pos[2] * h, h)), block(own[a], j), loaded.at[k]))
        for cp in arrive + load:
            cp.start()
        store = []
        for k, (a, j) in enumerate(pieces):
            load[k].wait()
            arrive[k].wait()
            at = Ellipsis if j is None else j
            summed[a][at] = (own[a][at].astype(F32) + land[a][at].astype(F32)).astype(summed[a].dtype)
            store.append(pltpu.make_async_copy(block(summed[a], j), block(outs[a], j), stored.at[k]))
            store[-1].start()
        for cp in store:
            cp.wait()

    vmem = [pltpu.VMEM(h, a.dtype) for h, a in zip(halves, arrays)]
    sems = [pltpu.SemaphoreType.DMA((len(pieces),))] * 4
    return _call(
        body, name=name,
        in_specs=[pl.BlockSpec(memory_space=pl.ANY)] * n, out_specs=[pl.BlockSpec(memory_space=pl.ANY)] * n,
        out_shape=[_sds(h, a.dtype) for h, a in zip(halves, arrays)],
        scratch_shapes=vmem * 3 + sems,
        compiler_params=_params(),
    )(*arrays)


def _to_other_chips(arrays, blocked):
    def src(ins, outs, pos, a, mask):
        return ins[a].at[_chip(pos) ^ mask] if blocked[a] else ins[a]

    outs = [_sds((3,) + (a.shape[1:] if b else a.shape), a.dtype) for a, b in zip(arrays, blocked)]
    copies = []
    for mi, mask in enumerate(CHIP_MASKS):
        for a in range(len(arrays)):
            copies.append((functools.partial(src, a=a, mask=mask), lambda ins, outs, pos, a=a, mi=mi: outs[a].at[mi],
                           functools.partial(_other_chip, mask=mask)))
    return outs, copies


def _sum_chips_swap(name, owns, gots, blocked):
    n = len(owns)
    shapes = [g.shape[-2:] for g in gots]

    def body(*refs):
        own, got, mine, theirs = refs[:n], refs[n:2 * n], refs[2 * n:3 * n], refs[3 * n:4 * n]
        own_v, got_v, sum_v = refs[4 * n:5 * n], refs[5 * n:6 * n], refs[6 * n:7 * n]
        send, recv, loaded, stored = refs[7 * n:]
        pos = (lax.axis_index("x"), lax.axis_index("y"), lax.axis_index("c"))
        load = []
        for a in range(n):
            load.append((pltpu.make_async_copy(own[a].at[_chip(pos)] if blocked[a] else own[a], own_v[a], loaded.at[2 * a]),
                         pltpu.make_async_copy(got[a], got_v[a], loaded.at[2 * a + 1])))
        for pair in load:
            for cp in pair:
                cp.start()
        out = []
        for a in range(n):
            for cp in load[a]:
                cp.wait()
            sum_v[a][...] = ((own_v[a][...].astype(F32) + got_v[a][0].astype(F32))
                             + (got_v[a][1].astype(F32) + got_v[a][2].astype(F32)))
            out.append(pltpu.make_async_remote_copy(
                src_ref=sum_v[a], dst_ref=theirs[a], send_sem=send.at[a], recv_sem=recv.at[a],
                device_id=_sibling(pos), device_id_type=pl.DeviceIdType.MESH))
            out.append(pltpu.make_async_copy(sum_v[a], mine[a], stored.at[a]))
            out[-2].start()
            out[-1].start()
        for cp in out:
            cp.wait()

    outs = _call(
        body, name=name,
        in_specs=[pl.BlockSpec(memory_space=pl.ANY)] * (2 * n), out_specs=[pl.BlockSpec(memory_space=pl.ANY)] * (2 * n),
        out_shape=[_sds(s) for s in shapes] * 2,
        scratch_shapes=[pltpu.VMEM(s, o.dtype) for s, o in zip(shapes, owns)]
                       + [pltpu.VMEM((3,) + s, g.dtype) for s, g in zip(shapes, gots)] + [pltpu.VMEM(s, F32) for s in shapes]
                       + [pltpu.SemaphoreType.DMA((n,)), pltpu.SemaphoreType.DMA((n,)),
                          pltpu.SemaphoreType.DMA((2 * n,)), pltpu.SemaphoreType.DMA((n,))],
        compiler_params=_params(),
    )(*owns, *gots)
    return outs[:n], outs[n:]


def _local_step(x, target, w_pad, w_out, conv_w, norm_w, pool_w, pool_scale, a_log, dt_bias, dn_norm_w, final_norm_w,
                after):
    proj, n_t, qn, kn, vs, beta, g, yq, yk, yv, y_pool = _front_fwd(
        x, norm_w, w_pad, conv_w, a_log, dt_bias, pool_w, pool_scale, after)
    w, att, qd, kd, tm, cd, o, vn, st = _delta_fwd(qn, kn, vs, beta, g)
    w_out = w_out(o) if callable(w_out) else w_out
    g_wout, dh, dyp, do, ddz, loss, g_fnw, g_dnw = _out_fwd_bwd(x, y_pool, o, proj, target, w_out, dn_norm_w, final_norm_w)
    dqn, dkn, dvs, dbeta, dg = _delta_bwd(do, vn, qd, kd, w, att, cd, st, qn, kn, vs, beta, g, tm)
    (dcq, dck, dcv, dba, g_cw, g_sm), (dpu, dpz, g_pw, g_ps) = _conv_pool_bwd(
        proj, (yq, yk, yv), conv_w, a_log, dt_bias, dqn, dkn, dvs, dbeta, dg, dyp, pool_w, pool_scale)
    pieces = [dpu, dpz, dcq, dck, dcv, ddz, dba]
    g_win = _grad_w_in(n_t, pieces)
    small = dict(norm_w=jnp.zeros_like(norm_w), pool_w=g_pw, pool_scale=g_ps, conv_w=g_cw[:CONV_K],
                 a_log=g_sm[0:1, 0:N_HEADS], dt_bias=g_sm[0:1, N_HEADS:2 * N_HEADS], dn_norm_w=g_dnw, final_norm_w=g_fnw)
    return loss[0, 0], g_win, g_wout, small, dh, pieces


SMALL_LAYOUT = (("pool_w", 512, HEAD, (1, N_HEADS, HEAD, HEAD)), ("final_norm_w", 8, HEAD, (D_MODEL,)),
                ("pool_scale", 4, HEAD, (1, D_HALF)), ("conv_w", 48, HEAD, (1, CONV_K, 3 * D_HALF)),
                ("dn_norm_w", 1, HEAD, (1, HEAD)), ("a_log", 1, N_HEADS, (1, N_HEADS)), ("dt_bias", 1, N_HEADS, (1, N_HEADS)),
                ("loss", 1, 1, ()))


def _small_offsets():
    offs, r = {}, 0
    for name, rows, _, _ in SMALL_LAYOUT:
        offs[name] = r
        r += -(-rows // 8) * 8
    assert r <= SMALL_ROWS
    return offs


def _pack_small(t):
    parts = []
    for name, rows, lanes, _ in SMALL_LAYOUT:
        a = t.get(name, jnp.zeros((1,), F32)).reshape(rows, lanes)
        parts.append(jnp.pad(a, ((0, -(-rows // 8) * 8 - rows), (0, HEAD - lanes))))
    buf = jnp.concatenate(parts, axis=0)
    return jnp.pad(buf, ((0, SMALL_ROWS - buf.shape[0]), (0, 0)))


def _adamw_small(w, g_own, g_got, cidx, m, v):
    offs = _small_offsets()
    names = [e[0] for e in SMALL_LAYOUT]
    n = len(names)

    def body(c_ref, w_ref, go_ref, gg_ref, m_ref, v_ref, *outs):
        own_low = c_ref[0] == 0
        gv = jnp.concatenate([jnp.where(own_low, go_ref[...], gg_ref[...]), jnp.where(own_low, gg_ref[...], go_ref[...])], axis=0)
        mn = ADAM_B1 * m_ref[...] + (1.0 - ADAM_B1) * gv
        vn = ADAM_B2 * v_ref[...] + (1.0 - ADAM_B2) * (gv * gv)
        m_hat = mn / (1.0 - ADAM_B1 ** ADAM_STEP)
        v_hat = vn / (1.0 - ADAM_B2 ** ADAM_STEP)
        dl = -ADAM_LR * (m_hat / (jnp.sqrt(v_hat) + ADAM_EPS) + ADAM_WD * w_ref[...])
        for kind, arr in enumerate((gv, dl, mn, vn)):
            for i, (name, rows, lanes, _) in enumerate(SMALL_LAYOUT):
                outs[kind * n + i][...] = arr[offs[name]:offs[name] + rows, :lanes]

    whole = lambda shape: pl.BlockSpec(shape, lambda i, c_ref: (0,) * len(shape))
    out_shapes = [_sds((rows, lanes)) for _, rows, lanes, _ in SMALL_LAYOUT] * 4
    res = _call(
        body, name="adamw_small",
        grid_spec=pltpu.PrefetchScalarGridSpec(
            num_scalar_prefetch=1, grid=(1,),
            in_specs=[whole(w.shape), whole(g_own.shape), whole(g_got.shape), whole(m.shape), whole(v.shape)],
            out_specs=[whole(o.shape) for o in out_shapes]),
        out_shape=out_shapes,
        compiler_params=_params("arbitrary"),
    )(cidx, w, g_own, g_got, m, v)
    return [{name: res[kind * n + i].reshape(shape) for i, (name, _, _, shape) in enumerate(SMALL_LAYOUT)}
            for kind in range(4)]


def kernel(x, norm_w, w_in, pool_w, pool_scale, conv_w, a_log, dt_bias, dn_norm_w, w_out, final_norm_w, loss_target, m_norm_w, m_w_in, m_pool_w, m_pool_scale, m_conv_w, m_a_log, m_dt_bias, m_dn_norm_w, m_w_out, m_final_norm_w, v_norm_w, v_w_in, v_pool_w, v_pool_scale, v_conv_w, v_a_log, v_dt_bias, v_dn_norm_w, v_w_out, v_final_norm_w):
    cidx = lax.axis_index("c").astype(I32).reshape(1)
    jidx = (2 * lax.axis_index("x") + lax.axis_index("y")).astype(I32)

    wb = jnp.pad(w_in[0].astype(BF16), ((0, 0), (0, BLK_IN_PAD - BLK_IN)))
    ob = w_out[0].astype(BF16)
    gw, cw_full = _gather_weights(wb, conv_w[0])
    w_pad = _assemble_w_in(gw, wb, jidx.reshape(1))

    lands_o, copies_o = _gather_blocks(ob)
    sems_o, ob_thru, zones_o, token_o = _exchange_start("gather_w_out_start", [ob], lands_o, copies_o)

    def w_out_full(after):
        _, (got,) = _exchange_wait("gather_w_out_wait", sems_o, ob_thru, zones_o, copies_o, after)
        return got.reshape(D_MODEL, D_MODEL)

    loss, g_win, g_wout, small, dh, pieces = _local_step(
        x[0], loss_target[0], w_pad, w_out_full, cw_full, norm_w, pool_w[0], pool_scale, a_log, dt_bias,
        dn_norm_w, final_norm_w.reshape(1, D_MODEL), token_o)
    small["loss"] = loss

    blocks_out = g_wout.reshape(4, BLK_OUT, D_MODEL)
    full = [g_win, blocks_out, _pack_small(small)]
    chip_sum = _reduce_sibling("reduce_sibling", full)
    blocked = [True, True, False]
    lands, copies = _to_other_chips(chip_sum, blocked)
    sems, chip_sum, zones, token = _exchange_start("reduce_chips_start", chip_sum, lands, copies)
    gx, g_nw = _in_bwd(x[0], dh, norm_w, w_pad, pieces, token)
    g_nw = _allreduce_tile("reduce_norm_w", g_nw.reshape(8, HEAD)).reshape(1, D_MODEL)
    chip_sum, from_chips = _exchange_wait("reduce_chips_wait", sems, chip_sum, zones, copies, gx)
    halves, other_halves = _sum_chips_swap("sum_chips_swap", chip_sum, from_chips, blocked)

    weights = dict(norm_w=norm_w, w_in=w_in, pool_w=pool_w, pool_scale=pool_scale, conv_w=conv_w, a_log=a_log,
                   dt_bias=dt_bias, dn_norm_w=dn_norm_w, w_out=w_out, final_norm_w=final_norm_w)
    ms = dict(norm_w=m_norm_w, w_in=m_w_in, pool_w=m_pool_w, pool_scale=m_pool_scale, conv_w=m_conv_w, a_log=m_a_log,
              dt_bias=m_dt_bias, dn_norm_w=m_dn_norm_w, w_out=m_w_out, final_norm_w=m_final_norm_w)
    vs = dict(norm_w=v_norm_w, w_in=v_w_in, pool_w=v_pool_w, pool_scale=v_pool_scale, conv_w=v_conv_w, a_log=v_a_log,
              dt_bias=v_dt_bias, dn_norm_w=v_dn_norm_w, w_out=v_w_out, final_norm_w=v_final_norm_w)
    names = ["norm_w", "w_in", "pool_w", "pool_scale", "conv_w", "a_log", "dt_bias", "dn_norm_w", "w_out", "final_norm_w"]
    small_names = [n for n in names if n not in ("w_in", "w_out")]

    def pack(t):
        conv = lax.dynamic_update_slice_in_dim(jnp.zeros((CONV_K, 3 * D_HALF), F32), t["conv_w"][0], jidx * BLK_CONV, axis=1)
        return _pack_small({**{n: t[n] for n in small_names if n != "conv_w"}, "conv_w": conv})

    results = [{}, {}, {}, {}]
    to_tiles = lambda a: jnp.transpose(a, (2, 0, 1)).reshape(BLK_IN, 8, HEAD)
    from_tiles = lambda a: jnp.transpose(a, (1, 2, 0)).reshape(1, D_MODEL, BLK_IN)
    lo = jnp.where(cidx[0] == 0, halves[0], other_halves[0])
    hi = jnp.where(cidx[0] == 0, other_halves[0], halves[0])
    g_tiles = jnp.concatenate([lo[:, :BLK_IN].T, hi[:, :BLK_IN].T], axis=1).reshape(BLK_IN, 8, HEAD)
    outs = _adamw_tiles("adamw_w_in", to_tiles(w_in), g_tiles, to_tiles(m_w_in), to_tiles(v_w_in))
    for res, o in zip(results, (g_tiles,) + tuple(outs)):
        res["w_in"] = from_tiles(o)
    outs = _adamw_shard("adamw_w_out", w_out, halves[1], other_halves[1], cidx, m_w_out, v_w_out)
    for res, o in zip(results, outs):
        res["w_out"] = o
    outs = _adamw_small(pack(weights), halves[2], other_halves[2], cidx, pack(ms), pack(vs))
    for res, got in zip(results, outs):
        got["conv_w"] = lax.dynamic_slice_in_dim(got["conv_w"], jidx * BLK_CONV, BLK_CONV, axis=2)
        res.update(got)
    one_tile = lambda a: a.reshape(1, 8, HEAD)
    outs = _adamw_tiles("adamw_norm_w", one_tile(norm_w), one_tile(g_nw), one_tile(m_norm_w), one_tile(v_norm_w))
    for res, o in zip(results, (g_nw,) + tuple(outs)):
        res["norm_w"] = o.reshape(1, D_MODEL)
    grads, delta, new_m, new_v = results

    return (grads["loss"], gx[None], *[grads[n] for n in names], *[delta[n] for n in names],
            *[new_m[n] for n in names], *[new_v[n] for n in names])
```

```python
import functools

import jax
import jax.numpy as jnp
import numpy as np
from jax import lax
from jax.experimental import pallas as pl
from jax.experimental.pallas import tpu as pltpu

F32 = jnp.float32
BF16 = jnp.bfloat16
I32 = jnp.int32

D_MODEL = 1024
D_HALF = 512
N_HEADS = 4
HEAD = 128
CHUNK = 64
PAIR = 2 * CHUNK
WINDOWS = (2, 4, 8, 16)
CONV_K = 4
EPS = 1e-6
N_IN = 3080
N_IN_PAD = 3200
BLK_IN = 770
BLK_IN_PAD = 896
BLK_OUT = 256
BLK_CONV = 384
COL_BA = 3072
QK_SCALE = HEAD ** -0.5
SMALL_ROWS = 608
VMEM_LIMIT = 56 * 1024 * 1024

ADAM_LR = 0.001
ADAM_B1 = 0.9
ADAM_B2 = 0.999
ADAM_EPS = 1e-08
ADAM_WD = 0.01
ADAM_STEP = 10

CHIP_MASKS = (2, 1, 3)
HEADS = range(N_HEADS)
HEAD_COLS = [slice(h * HEAD, (h + 1) * HEAD) for h in HEADS]


def _call(body, **kw):
    return pl.pallas_call(body, **kw)


def _params(*sem):
    return pltpu.CompilerParams(dimension_semantics=sem, vmem_limit_bytes=VMEM_LIMIT)


def _sds(shape, dtype=F32):
    return jax.ShapeDtypeStruct(shape, dtype)


def _bdot(a, b):
    return jnp.dot(a.astype(BF16), b.astype(BF16), preferred_element_type=F32)


def _bdot_nt(a, b):
    return lax.dot_general(a.astype(BF16), b.astype(BF16), (((1,), (1,)), ((), ())), preferred_element_type=F32)


def _bdot_tn(a, b):
    return lax.dot_general(a.astype(BF16), b.astype(BF16), (((0,), (0,)), ((), ())), preferred_element_type=F32)


def _side(a, b):
    return jnp.concatenate([a.astype(BF16), b.astype(BF16)], axis=1)


def _stack(a, b):
    return jnp.concatenate([a.astype(BF16), b.astype(BF16)], axis=0)


def _split(a):
    hi = a.astype(BF16)
    lo = (a - hi.astype(F32)).astype(BF16)
    return hi, lo


def _mask_dot(m, b):
    n = b.shape[1]
    both = jnp.dot(m, jnp.concatenate(_split(b), axis=1), preferred_element_type=F32)
    return both[:, :n] + both[:, n:]


def _sigmoid(x):
    return 0.5 * jnp.tanh(0.5 * x) + 0.5


def _softplus(x):
    return jnp.maximum(x, 0.0) + jnp.log(1.0 + jnp.exp(-jnp.abs(x)))


def _rowsum(x):
    return jnp.sum(x, axis=-1, keepdims=True)


def _colsum(x):
    return jnp.sum(x, axis=0, keepdims=True)


def _shift_down(xv, prev8, k):
    r = pltpu.roll(xv, k, 0)
    q = pltpu.roll(prev8, k, 0)
    row = lax.broadcasted_iota(I32, prev8.shape, 0)
    top = jnp.where(row < k, q, r[0:8])
    return jnp.concatenate([top, r[8:]], axis=0)


def _shift_up(xv, next8, k):
    t = xv.shape[0]
    r = pltpu.roll(xv, t - k, 0)
    q = pltpu.roll(next8, 8 - k, 0)
    row = lax.broadcasted_iota(I32, next8.shape, 0)
    bot = jnp.where(row >= 8 - k, q, r[t - 8:])
    return jnp.concatenate([r[:t - 8], bot], axis=0)


INTRA_PAIRS = 2
UNITS = [(pp, h) for pp in range(INTRA_PAIRS) for h in HEADS]


def _heads_of(ref, rows=PAIR, units=UNITS):
    return [ref[pp * rows:(pp + 1) * rows, HEAD_COLS[h]] for pp, h in units]


def _put_heads_of(ref, vals, rows=PAIR, units=UNITS):
    for (pp, h), v in zip(units, vals):
        ref[pp * rows:(pp + 1) * rows, HEAD_COLS[h]] = v.astype(ref.dtype)


_heads = _heads_of
_put_heads = _put_heads_of


def _each(fn, *lists):
    return [fn(*args) for args in zip(*lists)]


def _pool_bands(t, anti=False):
    r = np.arange(t)[:, None]
    c = np.arange(t + HEAD)[None, :]
    d = (c - r) if anti else (r - c + HEAD)
    return jnp.asarray(np.stack([(d >= 0) & (d < w) for w in WINDOWS]), BF16)


def _pool_mix(u, halo, z, pw, bands, row0):
    t = u[0].shape[0]
    rows = row0 + lax.broadcasted_iota(I32, (t, 1), 0) + 1
    cnt = [jnp.minimum(rows, w).astype(F32) for w in WINDOWS]
    win = _each(lambda b, h, v: _mask_dot(b, jnp.concatenate([h, v], axis=0)), bands, halo, u)
    mix = _each(lambda a, c, v: a / c - v, win, cnt, u)
    mixed = _each(_bdot, mix, pw)
    return mix, mixed, _each(_sigmoid, z), cnt


POOL_T = 256


def _conv_taps(xv, prev8):
    return [_shift_down(xv, prev8, CONV_K - 1 - j) for j in range(CONV_K - 1)] + [xv]


def _conv_pre(taps, cw):
    y = taps[CONV_K - 1] * cw[CONV_K - 1:CONV_K]
    for j in range(CONV_K - 2, -1, -1):
        y = y + taps[j] * cw[j:j + 1]
    return y


CONV_T = 256


def _conv_specs(t, tile_of=lambda i: i):
    tiles = [pl.BlockSpec((t, D_HALF), functools.partial(lambda i, p: (tile_of(i), 2 + p), p=p)) for p in range(3)]
    halos = [pl.BlockSpec((8, D_HALF),
                          functools.partial(lambda i, p: (jnp.maximum(tile_of(i) * (t // 8) - 1, 0), 2 + p), p=p))
             for p in range(3)]
    return tiles + halos


def _pair_masks():
    r = lax.broadcasted_iota(I32, (PAIR, PAIR), 0)
    c = lax.broadcasted_iota(I32, (PAIR, PAIR), 1)
    same = jnp.right_shift(r, 6) == jnp.right_shift(c, 6)
    return same, same & (r >= c), same & (r > c), r == c


def _interleave(*stage_lists):
    live = list(stage_lists)
    while live:
        for gen in list(live):
            try:
                next(gen)
            except StopIteration:
                live.remove(gen)


def _pipelined_step(t, n, leading, trailing):
    @pl.when(t == 0)
    def _():
        _interleave(*leading())

    @pl.when(jnp.logical_and(t > 0, t < n))
    def _():
        _interleave(*leading(), *trailing())

    @pl.when(t == n)
    def _():
        _interleave(*trailing())


def _pair_common_stages(cm, qn, kn, vs, beta, g):
    same, incl, strict, eye = _pair_masks()
    incl_b = incl.astype(BF16)
    first = lax.broadcasted_iota(I32, (PAIR, HEAD), 0) < CHUNK
    cm.update(same=same, incl=incl, strict=strict, eye=eye)
    gc = _each(lambda gv: _mask_dot(incl_b, gv), g)
    q = _each(lambda v: v * QK_SCALE, qn)
    kb = _each(lambda k, b: k * b, kn, beta)
    cm.update(gc=gc, q=q, kb=kb, vb=_each(lambda v, b: v * b, vs, beta))
    yield
    both = _each(lambda a, b, c: _bdot_nt(_stack(a, b), c), kb, q, kn)
    cm.update(kk=[v[:PAIR] for v in both], qk=[v[PAIR:] for v in both])
    gc_row = _each(lambda v: _colsum(jnp.where(eye, v, 0.0)), gc)
    gl = _each(lambda v: jnp.where(first, v[CHUNK - 1:CHUNK], v[PAIR - 1:PAIR]), gc)
    egc = _each(jnp.exp, gc)
    cm.update(gl=gl, egc=egc,
              decay=_each(lambda v, r: jnp.where(incl, jnp.exp(jnp.where(incl, v - r, 0.0)), 0.0), gc, gc_row))
    yield
    cm.update(ekd=_each(lambda a, b: jnp.exp(a - b), gl, gc), cd=_each(jnp.exp, gl),
              kbg=_each(lambda k, e: k * e, kb, egc))
    yield


def _tri_inv_stages(out, a, eye_f):
    p = _each(lambda v: eye_f - v, a)
    x = _each(_bdot, a, a)
    yield
    for it in range(4):
        both = _each(lambda xv, pv: _bdot(xv, _side(pv, xv)), x, p)
        p = _each(lambda pv, b: pv + b[:, :PAIR], p, both)
        x = [b[:, PAIR:] for b in both]
        yield
    out["t"] = _each(lambda pv, xv: pv + _bdot(pv, xv), p, x)
    yield


def _chunk_scalar_spec(pairs=1, index=lambda i: (i, 0)):
    return pl.BlockSpec((16 * pairs, D_HALF), index)


SCAN_PAIRS = 2
SCAN_ROWS = SCAN_PAIRS * PAIR


def _delta_fwd(qn, kn, vs, beta, g):
    s = qn.shape[0]
    n_steps = s // SCAN_ROWS
    n_chunks = s // CHUNK
    assert INTRA_PAIRS == SCAN_PAIRS

    def body(qn_ref, kn_ref, vs_ref, beta_ref, g_ref, w_ref, att_ref, qd_ref, kd_ref, t_ref, cd_ref, o_ref, vn_ref, st_ref,
             state, u_s, w_s, att_s, qd_s, kd_s, cd_s):
        t = pl.program_id(0)

        @pl.when(t == 1)
        def _():
            state[...] = jnp.zeros_like(state)

        cur = lax.rem(t, 2)
        prev = 1 - cur
        cols = list(enumerate(HEAD_COLS))

        def recurrence():
            sm = [state[h] for h in HEADS]
            for ci in range(2 * SCAN_PAIRS):
                rs = slice(ci * CHUNK, (ci + 1) * CHUNK)
                for h in HEADS:
                    st_ref[ci, h] = sm[h]
                both = [_bdot(jnp.concatenate([w_s[prev, rs, sl], qd_s[prev, rs, sl]], axis=0), sm[h]) for h, sl in cols]
                vn = [u_s[prev, rs, sl] - both[h][:CHUNK] for h, sl in cols]
                for h, sl in cols:
                    vn_ref[rs, sl] = vn[h].astype(BF16)
                    o_ref[rs, sl] = both[h][CHUNK:]
                yield
                sm = [sm[h] * cd_s[prev, ci * 8:ci * 8 + 1, sl] + _bdot_tn(kd_s[prev, rs, sl], vn[h]) for h, sl in cols]
                yield
            for h in HEADS:
                state[h] = sm[h]
            for pp in range(SCAN_PAIRS):
                rp = slice(pp * PAIR, (pp + 1) * PAIR)
                intra = [_bdot(att_s[prev, rp, sl], vn_ref[rp, sl]) for sl in HEAD_COLS]
                for h, sl in cols:
                    o_ref[rp, sl] += intra[h]
                yield

        def factors():
            kn = _heads(kn_ref)
            cm = {}
            yield from _pair_common_stages(cm, _heads(qn_ref), kn, _heads(vs_ref), _heads(beta_ref), _heads(g_ref))
            a = _each(lambda kk, d: jnp.where(cm["strict"], kk * d, 0.0), cm["kk"], cm["decay"])
            inv = {}
            yield from _tri_inv_stages(inv, a, cm["eye"].astype(F32))
            tm = inv["t"]
            uw = _each(lambda tv, a, b: _bdot(tv, _side(a, b)), tm, cm["vb"], cm["kbg"])
            res = dict(u=[v[:, :HEAD] for v in uw], w=[v[:, HEAD:] for v in uw],
                       att=_each(lambda a, b: a * b, cm["qk"], cm["decay"]),
                       qd=_each(lambda a, b: a * b, cm["q"], cm["egc"]), kd=_each(lambda a, b: a * b, kn, cm["ekd"]))
            yield
            _put_heads(t_ref, tm)
            for key, out, keep in (("w", w_ref, w_s), ("att", att_ref, att_s), ("qd", qd_ref, qd_s), ("kd", kd_ref, kd_s)):
                _put_heads(out, res[key])
                for (pp, h), v in zip(UNITS, res[key]):
                    keep[cur, pp * PAIR:(pp + 1) * PAIR, HEAD_COLS[h]] = v.astype(BF16)
            for (pp, h), v in zip(UNITS, res["u"]):
                u_s[cur, pp * PAIR:(pp + 1) * PAIR, HEAD_COLS[h]] = v
            for ci in range(2):
                for (pp, h), v in zip(UNITS, cm["cd"]):
                    rows8 = slice(pp * 16 + ci * 8, pp * 16 + (ci + 1) * 8)
                    cd_ref[rows8, HEAD_COLS[h]] = v[ci * CHUNK:ci * CHUNK + 8]
                    cd_s[cur, rows8, HEAD_COLS[h]] = v[ci * CHUNK:ci * CHUNK + 8]
            yield

        _pipelined_step(t, n_steps, lambda: [factors()], lambda: [recurrence()])

    last = n_steps - 1
    now = lambda i: (jnp.minimum(i, last), 0)
    before = lambda i: (jnp.maximum(i - 1, 0), 0)
    rows = lambda index: pl.BlockSpec((SCAN_ROWS, D_HALF), index)
    slot = lambda r, dtype: pltpu.VMEM((2, r, D_HALF), dtype)
    return _call(
        body, name="delta_fwd", grid=(n_steps + 1,),
        in_specs=[rows(now)] * 5,
        out_specs=[rows(now)] * 5 + [_chunk_scalar_spec(SCAN_PAIRS, now), rows(before), rows(before),
                                     pl.BlockSpec((2 * SCAN_PAIRS, N_HEADS, HEAD, HEAD), lambda i: (jnp.maximum(i - 1, 0), 0, 0, 0))],
        out_shape=[_sds((s, D_HALF), BF16)] * 5 + [_sds((s // 8, D_HALF)), _sds((s, D_HALF)), _sds((s, D_HALF), BF16),
                                                  _sds((n_chunks, N_HEADS, HEAD, HEAD))],
        scratch_shapes=[pltpu.VMEM((N_HEADS, HEAD, HEAD), F32), slot(SCAN_ROWS, F32), slot(SCAN_ROWS, BF16),
                        slot(SCAN_ROWS, BF16), slot(SCAN_ROWS, BF16), slot(SCAN_ROWS, BF16), slot(16 * SCAN_PAIRS, F32)],
        compiler_params=_params("arbitrary"),
    )(qn, kn, vs, beta, g)


OUT_T = 512
OUT_ROWS = 256


def _out_fwd_bwd(x, y_pool, o, proj, target, w_out, dn_norm_w, final_norm_w):
    s = x.shape[0]
    t = OUT_T

    def body(x_ref, yp_ref, o_ref, z_ref, tg_ref, wo_ref, dnw_ref, fnw_ref,
             gwo_ref, dh_ref, dyp_ref, do_ref, dz_ref, loss_ref, gfn_ref, gdn_ref, y_ref, yt_ref, gwo_acc):
        @pl.when(pl.program_id(0) == 0)
        def _():
            loss_ref[...] = jnp.zeros_like(loss_ref)
            gfn_ref[...] = jnp.zeros_like(gfn_ref)
            gdn_ref[...] = jnp.zeros_like(gdn_ref)
            gwo_acc[...] = jnp.zeros_like(gwo_acc)

        dnw = dnw_ref[...]
        fnw = fnw_ref[...]

        def stages(rows, lead):
            for _ in range(lead):
                yield
            ypv = yp_ref[rows]
            y_ref[rows, :D_HALF] = ypv.astype(BF16)
            yt_ref[:D_HALF, rows] = ypv.T.astype(BF16)
            keep = []
            for h in HEADS:
                ov = o_ref[rows, HEAD_COLS[h]]
                zv = z_ref[rows, HEAD_COLS[h]]
                ro = lax.rsqrt(jnp.mean(ov * ov, axis=-1, keepdims=True) + EPS)
                ohat = ov * ro
                sg = _sigmoid(zv)
                keep.append((ro, ohat, zv, sg))
                ydn = ohat * dnw * (zv * sg)
                y_ref[rows, D_HALF + h * HEAD:D_HALF + (h + 1) * HEAD] = ydn.astype(BF16)
                yt_ref[D_HALF + h * HEAD:D_HALF + (h + 1) * HEAD, rows] = ydn.T.astype(BF16)
            yield
            hv = x_ref[rows] + jnp.dot(y_ref[rows], wo_ref[...], preferred_element_type=F32)
            yield
            r2 = lax.rsqrt(jnp.mean(hv * hv, axis=-1, keepdims=True) + EPS)
            hhat = hv * r2
            err = hhat * fnw - tg_ref[rows]
            loss_ref[...] += 0.5 * jnp.sum(_rowsum(err * err) * (1.0 / D_MODEL), axis=0, keepdims=True)
            dout = err * (1.0 / D_MODEL)
            gfn_ref[...] += _colsum(dout * hhat)
            dhh = dout * fnw
            dh = r2 * (dhh - hhat * jnp.mean(dhh * hhat, axis=-1, keepdims=True))
            dh_ref[rows] = dh
            yield
            if lead == t // OUT_ROWS - 1:
                gwo_acc[...] += _bdot(yt_ref[...], dh_ref[...])
            dy = _bdot_nt(dh, wo_ref[...])
            yield
            dyp_ref[rows] = dy[:, :D_HALF]
            gdn = jnp.zeros((1, HEAD), F32)
            for h in HEADS:
                ro, ohat, zv, sg = keep[h]
                dyd = dy[:, D_HALF + h * HEAD:D_HALF + (h + 1) * HEAD]
                sz = zv * sg
                dz_ref[rows, HEAD_COLS[h]] = (dyd * ohat * dnw * (sg * (1.0 + zv * (1.0 - sg)))).astype(BF16)
                gdn = gdn + _colsum(dyd * ohat * sz)
                doh = dyd * dnw * sz
                do_ref[rows, HEAD_COLS[h]] = ro * (doh - ohat * jnp.mean(doh * ohat, axis=-1, keepdims=True))
            gdn_ref[...] += gdn
            yield

        _interleave(*[stages(slice(k * OUT_ROWS, (k + 1) * OUT_ROWS), k) for k in range(t // OUT_ROWS)])

        @pl.when(pl.program_id(0) == pl.num_programs(0) - 1)
        def _():
            gwo_ref[...] = gwo_acc[...].astype(BF16)

    wide = pl.BlockSpec((t, D_MODEL), lambda i: (i, 0))
    half = pl.BlockSpec((t, D_HALF), lambda i: (i, 0))
    const = lambda shape: pl.BlockSpec(shape, lambda i: (0,) * len(shape))
    return _call(
        body, name="out_fwd_bwd", grid=(s // t,),
        in_specs=[wide, half, half, pl.BlockSpec((t, D_HALF), lambda i: (i, 5)), wide,
                  const((D_MODEL, D_MODEL)), const((1, HEAD)), const((1, D_MODEL))],
        out_specs=[const((D_MODEL, D_MODEL)), wide, half, half, half,
                   const((1, HEAD)), const((1, D_MODEL)), const((1, HEAD))],
        out_shape=[_sds((D_MODEL, D_MODEL), BF16), _sds((s, D_MODEL)), _sds((s, D_HALF)), _sds((s, D_HALF)),
                   _sds((s, D_HALF), BF16), _sds((1, HEAD)), _sds((1, D_MODEL)), _sds((1, HEAD))],
        scratch_shapes=[pltpu.VMEM((t, D_MODEL), BF16), pltpu.VMEM((D_MODEL, t), BF16), pltpu.VMEM((D_MODEL, D_MODEL), F32)],
        compiler_params=_params("arbitrary"),
    )(x, y_pool, o, proj, target, w_out, dn_norm_w, final_norm_w)


def _grad_w_in(at, pieces, after):
    m, s = at.shape
    n = len(pieces)
    tn, tk = D_HALF, min(s, 1024)

    def body(a_ref, *refs):
        p_refs, o_ref, acc = refs[:n], refs[n + 1], refs[n + 2]

        @pl.when(pl.program_id(0) == 0)
        def _():
            acc[...] = jnp.zeros_like(acc)

        av = a_ref[...]
        for p in range(n):
            acc[:, p * tn:(p + 1) * tn] += _bdot(av, p_refs[p][...])

        @pl.when(pl.program_id(0) == pl.num_programs(0) - 1)
        def _():
            for j in range(4):
                base = j * BLK_IN // HEAD * HEAD
                win = acc[:, base:base + BLK_IN_PAD]
                if j * BLK_IN > base:
                    win = pltpu.roll(win, BLK_IN_PAD - (j * BLK_IN - base), 1)
                o_ref[j] = win.astype(BF16)

    return _call(
        body, name="grad_w_in", grid=(s // tk,),
        in_specs=[pl.BlockSpec((m, tk), lambda k: (0, k))] + [pl.BlockSpec((tk, tn), lambda k: (k, 0))] * n
                 + [pl.BlockSpec(memory_space=pl.ANY)],
        out_specs=pl.BlockSpec((4, m, BLK_IN_PAD), lambda k: (0, 0, 0)),
        out_shape=_sds((4, m, BLK_IN_PAD), BF16),
        scratch_shapes=[pltpu.VMEM((m, n * tn), F32)],
        compiler_params=_params("arbitrary"),
    )(at, *pieces, after)


def _delta_bwd(do, vn, qd, kd, w, att, cd, st, qn, kn, vs, beta, g, tm):
    s = do.shape[0]
    n_steps = s // SCAN_ROWS
    assert INTRA_PAIRS == SCAN_PAIRS

    def body(do_ref, vn_ref, qd_ref, kd_ref, w_ref, att_ref, cd_ref, st_ref, qn_ref, kn_ref, vs_ref, beta_ref, g_ref, t_ref,
             dqn_ref, dkn_ref, dvs_ref, dbeta_ref, dg_ref, dstate, du_s, dw_s, datt_s, dqd_s, dkd_s, dcd_s):
        t = pl.program_id(0)

        @pl.when(t == 0)
        def _():
            dstate[...] = jnp.zeros_like(dstate)

        cur = lax.rem(t, 2)
        prev = 1 - cur
        cols = list(enumerate(HEAD_COLS))
        _, incl, _, _ = _pair_masks()

        def recurrence():
            dv_intra = []
            for pp in range(SCAN_PAIRS):
                rp = slice(pp * PAIR, (pp + 1) * PAIR)
                dv_intra.append([_bdot_tn(att_ref[rp, sl], do_ref[rp, sl]) for _, sl in cols])
                for _, sl in cols:
                    datt_s[cur, rp, sl] = jnp.where(incl, _bdot_nt(do_ref[rp, sl], vn_ref[rp, sl]), 0.0)
                yield
            ds = [dstate[h] for h in HEADS]
            for ci in range(2 * SCAN_PAIRS - 1, -1, -1):
                rs = slice(ci * CHUNK, (ci + 1) * CHUNK)
                in_pair = slice((ci % 2) * CHUNK, (ci % 2 + 1) * CHUNK)
                sm = [st_ref[ci, h] for h in HEADS]
                dvn = [dv_intra[ci // 2][h][in_pair] + _bdot(kd_ref[rs, sl], ds[h]) for h, sl in cols]
                dkd = [_bdot_nt(vn_ref[rs, sl], ds[h]) for h, sl in cols]
                dcd = [jnp.broadcast_to(_rowsum(_colsum(ds[h] * sm[h])), (8, HEAD)) for h in HEADS]
                yield
                both = [_bdot_nt(_stack(do_ref[rs, sl], dvn[h]), sm[h]) for h, sl in cols]
                for h, sl in cols:
                    du_s[cur, rs, sl] = dvn[h].astype(BF16)
                    dqd_s[cur, rs, sl] = both[h][:CHUNK]
                    dw_s[cur, rs, sl] = (-both[h][CHUNK:]).astype(BF16)
                    dkd_s[cur, rs, sl] = dkd[h]
                    dcd_s[cur, ci * 8:(ci + 1) * 8, sl] = dcd[h]
                ds = [ds[h] * cd_ref[ci * 8:ci * 8 + 1, sl]
                      + _bdot_tn(_stack(qd_ref[rs, sl], w_ref[rs, sl]), _stack(do_ref[rs, sl], -dvn[h])) for h, sl in cols]
                yield
            for h in HEADS:
                dstate[h] = ds[h]

        def factors(units):
            _heads = functools.partial(_heads_of, units=units)
            _put_heads = functools.partial(_put_heads_of, units=units)
            ones = jnp.ones((2 * PAIR, HEAD), BF16)
            tn = (((0,), (0,)), ((), ()))
            kept = lambda ref, rows=PAIR: [ref[prev, pp * rows:(pp + 1) * rows, HEAD_COLS[h]] for pp, h in units]
            kn, vs, beta = _heads(kn_ref), _heads(vs_ref), _heads(beta_ref)
            cm = {}
            yield from _pair_common_stages(cm, _heads(qn_ref), kn, vs, beta, _heads(g_ref))
            tmv = _heads(t_ref)
            duv, dwv, dattv, dqdv, dkdv = kept(du_s), kept(dw_s), kept(datt_s), kept(dqd_s), kept(dkd_s)
            duw = _each(_side, duv, dwv)
            both = _each(_bdot_tn, tmv, duw)
            dvb, dkbg = [v[:, :HEAD] for v in both], [v[:, HEAD:] for v in both]
            dt = _each(lambda a, b, c: _bdot_nt(a, _side(b, c)), duw, cm["vb"], cm["kbg"])
            yield
            m1 = _each(_bdot_tn, tmv, dt)
            yield
            da = _each(lambda a, b: -jnp.where(cm["strict"], _bdot_nt(a, b), 0.0), m1, tmv)
            yield
            dkk = _each(lambda a, b: a * b, da, cm["decay"])
            dqk = _each(lambda a, b: a * b, dattv, cm["decay"])
            dd = _each(lambda a, b, c, d: a * b + c * d, dkk, cm["kk"], dqk, cm["qk"])
            dkq = _each(_stack, dkk, dqk)
            both = _each(_bdot, dkq, kn)
            dkb = _each(lambda a, c, d: a[:PAIR] + c * d, both, dkbg, cm["egc"])
            dq = _each(lambda a, c, d: a[PAIR:] + c * d, both, dqdv, cm["egc"])
            yield
            dkn = _each(lambda a, b, c: _bdot_tn(a, _stack(b, c)), dkq, cm["kb"], cm["q"])
            dkn = _each(lambda a, b, c, d, e: a + b * c + d * e, dkn, dkdv, cm["ekd"], dkb, beta)
            t_kd = _each(lambda a, b, c: _rowsum(a * b * c), dkdv, kn, cm["ekd"])
            yield
            split = _each(_split, dd)
            rows_dd = [jnp.dot(_side(hi, lo), ones, preferred_element_type=F32) for hi, lo in split]
            cols_dd = [lax.dot_general(_stack(hi, lo), ones, tn, preferred_element_type=F32) for hi, lo in split]
            yield
            dgc = _each(lambda r, c, a, b, e, f, k, tk: r - c + _rowsum(a * b * e) + _rowsum(f * k) - tk,
                        rows_dd, cols_dd, dqdv, cm["q"], cm["egc"], dkbg, cm["kbg"], t_kd)
            same_b = cm["same"].astype(BF16)
            rowi = lax.broadcasted_iota(I32, (PAIR, HEAD), 0)
            dcd = _each(lambda d: jnp.where(rowi < CHUNK, d[0:1], d[8:9]), kept(dcd_s, rows=16))
            dgl = _each(lambda tk, d, c: _mask_dot(same_b, jnp.broadcast_to(tk, (PAIR, HEAD))) + d * c, t_kd, dcd, cm["cd"])
            yield
            is_last = jnp.bitwise_and(rowi, CHUNK - 1) == CHUNK - 1
            dgc = _each(lambda a, b: a + jnp.where(is_last, b, 0.0), dgc, dgl)
            r = lax.broadcasted_iota(I32, (PAIR, PAIR), 0)
            c = lax.broadcasted_iota(I32, (PAIR, PAIR), 1)
            upper_b = (cm["same"] & (r <= c)).astype(BF16)
            _put_heads(dg_ref, _each(lambda v: _mask_dot(upper_b, v), dgc))
            yield
            _put_heads(dbeta_ref, _each(lambda a, b, c, d: jnp.broadcast_to(_rowsum(a * b) + _rowsum(c * d), (PAIR, HEAD)),
                                        dkb, kn, dvb, vs))
            _put_heads(dqn_ref, _each(lambda v: v * QK_SCALE, dq))
            _put_heads(dkn_ref, dkn)
            _put_heads(dvs_ref, _each(lambda a, b: a * b, dvb, beta))
            yield

        _pipelined_step(t, n_steps, lambda: [recurrence()],
                        lambda: [factors(UNITS[pp * N_HEADS:(pp + 1) * N_HEADS]) for pp in range(INTRA_PAIRS)])

    last = n_steps - 1
    now = lambda i: (jnp.maximum(last - i, 0), 0)
    after = lambda i: (jnp.minimum(n_steps - i, last), 0)
    rows = lambda index: pl.BlockSpec((SCAN_ROWS, D_HALF), index)
    slot = lambda r, dtype: pltpu.VMEM((2, r, D_HALF), dtype)
    return _call(
        body, name="delta_bwd", grid=(n_steps + 1,),
        in_specs=[rows(now)] * 6 + [_chunk_scalar_spec(SCAN_PAIRS, now),
                                    pl.BlockSpec((2 * SCAN_PAIRS, N_HEADS, HEAD, HEAD), lambda i: (jnp.maximum(last - i, 0), 0, 0, 0))]
                 + [rows(after)] * 6,
        out_specs=[rows(after)] * 5,
        out_shape=[_sds((s, D_HALF))] * 5,
        scratch_shapes=[pltpu.VMEM((N_HEADS, HEAD, HEAD), F32), slot(SCAN_ROWS, BF16), slot(SCAN_ROWS, BF16),
                        slot(SCAN_ROWS, F32), slot(SCAN_ROWS, F32), slot(SCAN_ROWS, F32), slot(16 * SCAN_PAIRS, F32)],
        compiler_params=_params("arbitrary"),
    )(do, vn, qd, kd, w, att, cd, st, qn, kn, vs, beta, g, tm)


def _fused_call(name, n_steps, parts):
    n_in = [len(p["inputs"]) for p in parts]
    n_out = [len(p["out_shape"]) for p in parts]
    n_scr = [len(p["scratch"]) for p in parts]

    def body(*refs):
        ins, outs, scr = refs[:sum(n_in)], refs[sum(n_in):sum(n_in) + sum(n_out)], refs[sum(n_in) + sum(n_out):]
        gens, a, b, c = [], 0, 0, 0
        for p, ni, no, ns in zip(parts, n_in, n_out, n_scr):
            gens.append(p["stages"](ins[a:a + ni], outs[b:b + no], scr[c:c + ns]))
            a, b, c = a + ni, b + no, c + ns
        _interleave(*gens)

    flat = lambda key: [v for p in parts for v in p[key]]
    res = _call(
        body, name=name, grid=(n_steps,),
        in_specs=flat("in_specs"), out_specs=flat("out_specs"), out_shape=flat("out_shape"),
        scratch_shapes=flat("scratch"),
        compiler_params=_params("arbitrary"),
    )(*flat("inputs"))
    out, b = [], 0
    for no in n_out:
        out.append(res[b:b + no])
        b += no
    return out


def _pool_bwd_part(proj, dyp, pool_w, pool_scale, tile_of, n_tiles):
    s = proj.shape[0]
    t = POOL_T
    hb = t // HEAD
    last = s // HEAD - 1

    def stages(ins, outs, scratch):
        u_ref, z_ref, halo_ref, dy_ref, zn_ref, dyn_ref, pw_ref, ps_ref, band_ref, aband_ref = ins
        du_ref, dz_ref, gpw_ref, gps_ref = outs
        tile = tile_of(pl.program_id(0))

        @pl.when(pl.program_id(0) == 0)
        def _():
            gpw_ref[...] = jnp.zeros_like(gpw_ref)
            gps_ref[...] = jnp.zeros_like(gps_ref)

        live = (tile > 0).astype(F32)
        more = (tile < n_tiles - 1).astype(F32)
        groups = lambda ref: [ref[:, sl] for sl in HEAD_COLS]
        z, ps, dy = groups(z_ref), groups(ps_ref), groups(dy_ref)
        pw = [pw_ref[g] for g in HEADS]
        mix, mixed, sg, cnt = _pool_mix(groups(u_ref), [h * live for h in groups(halo_ref)], z, pw,
                                        [band_ref[g] for g in HEADS], tile * t)
        yield
        sz = _each(lambda a, b: a * b, z, sg)
        for sl, d, m, p, s_, zg in zip(HEAD_COLS, dy, mixed, ps, sg, z):
            dz_ref[:, sl] = (d * m * p * (s_ * (1.0 + zg * (1.0 - s_)))).astype(BF16)
        for sl, d, m, a in zip(HEAD_COLS, dy, mixed, sz):
            gps_ref[:, sl] += _colsum(d * m * a)
        dmixed = _each(lambda d, p, a: d * p * a, dy, ps, sz)
        yield
        for g, gp in enumerate(_each(_bdot_tn, mix, dmixed)):
            gpw_ref[g] += gp
        dmix = _each(_bdot_nt, dmixed, pw)
        yield
        dmix_n = _each(lambda d, p, zn, w_: _bdot_nt(d * more * p * (zn * _sigmoid(zn)), w_),
                       groups(dyn_ref), ps, groups(zn_ref), pw)
        yield
        scaled = [jnp.concatenate([a / c, b * (1.0 / w)], axis=0) for a, c, b, w in zip(dmix, cnt, dmix_n, WINDOWS)]
        du = _each(lambda b, s_, d: _mask_dot(b, s_) - d, [aband_ref[g] for g in HEADS], scaled, dmix)
        for sl, v in zip(HEAD_COLS, du):
            du_ref[:, sl] = v.astype(BF16)
        yield

    tile = lambda col: pl.BlockSpec((t, D_HALF), lambda i: (tile_of(i), col))
    below = lambda col: pl.BlockSpec((HEAD, D_HALF), lambda i: (jnp.minimum((tile_of(i) + 1) * hb, last), col))
    const3 = lambda shape: pl.BlockSpec(shape, lambda i: (0, 0, 0))
    return dict(
        inputs=[proj, proj, proj, dyp, proj, dyp, pool_w, pool_scale, _pool_bands(t), _pool_bands(t, anti=True)],
        in_specs=[tile(0), tile(1), pl.BlockSpec((HEAD, D_HALF), lambda i: (jnp.maximum(tile_of(i) * hb - 1, 0), 0)),
                  tile(0), below(1), below(0), const3((N_HEADS, HEAD, HEAD)), pl.BlockSpec((1, D_HALF), lambda i: (0, 0)),
                  const3((N_HEADS, t, HEAD + t)), const3((N_HEADS, t, HEAD + t))],
        out_specs=[tile(0), tile(0), const3((N_HEADS, HEAD, HEAD)), pl.BlockSpec((1, D_HALF), lambda i: (0, 0))],
        out_shape=[_sds((s, D_HALF), BF16), _sds((s, D_HALF), BF16), _sds((N_HEADS, HEAD, HEAD)), _sds((1, D_HALF))],
        scratch=[], stages=stages)


def _conv_bwd_part(proj, pre, conv_w, a_log, dt_bias, dqn, dkn, dvs, dbeta, dg, tile_of, n_tiles):
    s = proj.shape[0]
    t = CONV_T

    def stages(ins, outs, scratch):
        (q_ref, k_ref, v_ref, yq_ref, yk_ref, yv_ref, ba_ref, cw_ref, al_ref, dtb_ref,
         dqn_ref, dkn_ref, dvs_ref, dbeta_ref, dg_ref) = ins
        oq_ref, ok_ref, ov_ref, dba_ref, gcw_out, gsm_out = outs
        below, gcw_ref, gsm_ref = scratch
        step = pl.program_id(0)

        @pl.when(step == 0)
        def _():
            gcw_ref[...] = jnp.zeros_like(gcw_ref)
            gsm_ref[...] = jnp.zeros_like(gsm_ref)
            below[...] = jnp.zeros_like(below)

        parts = ((q_ref, yq_ref, dqn_ref, oq_ref), (k_ref, yk_ref, dkn_ref, ok_ref), (v_ref, yv_ref, dvs_ref, ov_ref))
        for p, (x_ref, y_ref, d_ref, o_ref) in enumerate(parts):
            for h in HEADS:
                cs = HEAD_COLS[h]
                wide = slice(p * D_HALF + h * HEAD, p * D_HALF + (h + 1) * HEAD)
                cw = cw_ref[:, wide]
                y = y_ref[:, cs]
                sg = _sigmoid(y)
                sv = y * sg
                ds = d_ref[:, cs]
                if p < 2:
                    rn = lax.rsqrt(_rowsum(sv * sv) + EPS)
                    nrm = sv * rn
                    ds = rn * (ds - nrm * _rowsum(ds * nrm))
                dy = ds * (sg * (1.0 + y * (1.0 - sg)))
                nxt = below[:, wide]
                ahead = [dy] + [_shift_up(dy, nxt, sft) for sft in range(1, CONV_K)]
                xv = x_ref[:, cs]
                acc = dy * cw[CONV_K - 1:CONV_K]
                for sft in range(1, CONV_K):
                    acc = acc + ahead[sft] * cw[CONV_K - 1 - sft:CONV_K - sft]
                for j in range(CONV_K):
                    gcw_ref[8 * j:8 * j + 8, wide] += _rows8(xv * ahead[CONV_K - 1 - j])
                o_ref[:, cs] = acc.astype(BF16)
                below[:, wide] = dy[0:8]
                yield

        ba = ba_ref[...]
        lane = lax.broadcasted_iota(I32, (t, HEAD), 1)
        lane8 = lax.broadcasted_iota(I32, (8, HEAD), 1)
        dba = jnp.zeros((t, HEAD), F32)
        gsm = jnp.zeros((8, HEAD), F32)
        for h in HEADS:
            beta = _sigmoid(ba[:, h:h + 1])
            dbeta = dbeta_ref[:, h * HEAD:h * HEAD + 1]
            xg = ba[:, N_HEADS + h:N_HEADS + h + 1] + dtb_ref[0:1, h:h + 1]
            nexp = -jnp.exp(al_ref[0:1, h:h + 1])
            dgv = dg_ref[:, h * HEAD:h * HEAD + 1]
            da = dgv * nexp * _sigmoid(xg)
            dba = dba + jnp.where(lane == h, dbeta * beta * (1.0 - beta), 0.0) + jnp.where(lane == N_HEADS + h, da, 0.0)
            gsm = (gsm + jnp.where(lane8 == h, _rows8(dgv * nexp * _softplus(xg)), 0.0)
                   + jnp.where(lane8 == N_HEADS + h, _rows8(da), 0.0))
        dba_ref[...] = jnp.zeros_like(dba_ref)
        dba_ref[:, :HEAD] = dba.astype(BF16)
        gsm_ref[...] += gsm
        yield

        @pl.when(step == n_tiles - 1)
        def _():
            gcw_out[...] = jnp.zeros_like(gcw_out)
            for j in range(CONV_K):
                gcw_out[j:j + 1, :] = _colsum(gcw_ref[8 * j:8 * j + 8, :])
            gsm_out[...] = jnp.broadcast_to(_colsum(gsm_ref[...]), (8, HEAD))

    row = pl.BlockSpec((t, D_HALF), lambda i: (tile_of(i), 0))
    const = lambda shape: pl.BlockSpec(shape, lambda i: (0, 0))
    return dict(
        inputs=[proj] * 3 + list(pre) + [proj, conv_w, a_log, dt_bias, dqn, dkn, dvs, dbeta, dg],
        in_specs=_conv_specs(t, tile_of)[:3] + [row] * 3
                 + [pl.BlockSpec((t, HEAD), lambda i: (tile_of(i), COL_BA // HEAD)),
                    const((CONV_K, 3 * D_HALF)), const((1, N_HEADS)), const((1, N_HEADS))] + [row] * 5,
        out_specs=[row, row, row, row, const((8, 3 * D_HALF)), const((8, HEAD))],
        out_shape=[_sds((s, D_HALF), BF16)] * 4 + [_sds((8, 3 * D_HALF)), _sds((8, HEAD))],
        scratch=[pltpu.VMEM((8, 3 * D_HALF), F32), pltpu.VMEM((8 * CONV_K, 3 * D_HALF), F32), pltpu.VMEM((8, HEAD), F32)],
        stages=stages)


def _pool_fwd_part(proj, pool_w, pool_scale):
    s = proj.shape[0]
    t = POOL_T
    hb = t // HEAD

    def stages(ins, outs, scratch, tile=None):
        u_ref, z_ref, halo_ref, pw_ref, ps_ref, band_ref = ins
        y_ref, = outs
        i = pl.program_id(0) if tile is None else tile
        live = (i > 0).astype(F32)
        groups = lambda ref: [ref[:, sl] for sl in HEAD_COLS]
        z = groups(z_ref)
        u, halo = groups(u_ref), [h * live for h in groups(halo_ref)]
        yield
        _, mixed, sg, _ = _pool_mix(u, halo, z, [pw_ref[g] for g in HEADS], [band_ref[g] for g in HEADS], i * t)
        yield
        for sl, m, zg, s_ in zip(HEAD_COLS, mixed, z, sg):
            y_ref[:, sl] = m * ps_ref[:, sl] * (zg * s_)
        yield

    const3 = lambda shape: pl.BlockSpec(shape, lambda i: (0, 0, 0))
    return dict(
        inputs=[proj, proj, proj, pool_w, pool_scale, _pool_bands(t)],
        in_specs=[pl.BlockSpec((t, D_HALF), lambda i: (i, 0)), pl.BlockSpec((t, D_HALF), lambda i: (i, 1)),
                  pl.BlockSpec((HEAD, D_HALF), lambda i: (jnp.maximum(i * hb - 1, 0), 0)),
                  const3((N_HEADS, HEAD, HEAD)), pl.BlockSpec((1, D_HALF), lambda i: (0, 0)), const3((N_HEADS, t, HEAD + t))],
        out_specs=[pl.BlockSpec((t, D_HALF), lambda i: (i, 0))], out_shape=[_sds((s, D_HALF))],
        scratch=[], stages=stages)


def _conv_fwd_part(proj, conv_w, a_log, dt_bias):
    s = proj.shape[0]
    t = CONV_T

    def stages(ins, outs, scratch, tile=None):
        q_ref, k_ref, v_ref, hq_ref, hk_ref, hv_ref, ba_ref, cw_ref, al_ref, dtb_ref = ins
        qn_ref, kn_ref, vs_ref, beta_ref, g_ref, yq_ref, yk_ref, yv_ref = outs
        live = ((pl.program_id(0) if tile is None else tile) > 0).astype(F32)
        parts = ((q_ref, hq_ref, qn_ref, yq_ref), (k_ref, hk_ref, kn_ref, yk_ref), (v_ref, hv_ref, vs_ref, yv_ref))
        for p, (x_ref, h_ref, o_ref, y_ref) in enumerate(parts):
            for h in HEADS:
                cs = HEAD_COLS[h]
                taps = _conv_taps(x_ref[:, cs], h_ref[:, cs] * live)
                y = _conv_pre(taps, cw_ref[:, p * D_HALF + h * HEAD:p * D_HALF + (h + 1) * HEAD])
                y_ref[:, cs] = y
                sv = y * _sigmoid(y)
                o_ref[:, cs] = sv if p == 2 else sv * lax.rsqrt(_rowsum(sv * sv) + EPS)
                yield
        ba = ba_ref[...]
        for h in HEADS:
            beta = _sigmoid(ba[:, h:h + 1])
            gl = -jnp.exp(al_ref[0:1, h:h + 1]) * _softplus(ba[:, N_HEADS + h:N_HEADS + h + 1] + dtb_ref[0:1, h:h + 1])
            beta_ref[:, HEAD_COLS[h]] = jnp.broadcast_to(beta, (t, HEAD))
            g_ref[:, HEAD_COLS[h]] = jnp.broadcast_to(gl, (t, HEAD))
        yield

    row = pl.BlockSpec((t, D_HALF), lambda i: (i, 0))
    const = lambda shape: pl.BlockSpec(shape, lambda i: (0, 0))
    return dict(
        inputs=[proj] * 7 + [conv_w, a_log, dt_bias],
        in_specs=_conv_specs(t) + [pl.BlockSpec((t, HEAD), lambda i: (i, COL_BA // HEAD)),
                                   const((CONV_K, 3 * D_HALF)), const((1, N_HEADS)), const((1, N_HEADS))],
        out_specs=[row] * 8, out_shape=[_sds((s, D_HALF))] * 8, scratch=[], stages=stages)


def _front_fwd(x, norm_w, w_pad, conv_w, a_log, dt_bias, pool_w, pool_scale, after):
    s = x.shape[0]
    t = CONV_T
    n_tiles = s // t
    assert POOL_T == CONV_T
    like_proj = _sds((s, N_IN_PAD))
    conv = _conv_fwd_part(like_proj, conv_w, a_log, dt_bias)
    pool = _pool_fwd_part(like_proj, pool_w, pool_scale)
    bands = pool["inputs"][-1]
    mxu_n = 256
    col_bounds = list(range(0, N_IN_PAD, 3 * mxu_n)) + [N_IN_PAD]

    def body(x_ref, nw_ref, w_ref, cw_ref, al_ref, dtb_ref, pw_ref, ps_ref, band_ref, after_ref,
             proj_ref, nt_ref, qn_ref, kn_ref, vs_ref, beta_ref, g_ref, yq_ref, yk_ref, yv_ref, y_ref, prev):
        del after_ref
        i = pl.program_id(0)

        @pl.when(i == 0)
        def _():
            prev[...] = jnp.zeros_like(prev)

        tile = jnp.maximum(i - 1, 0)
        main, above8, above = pl.ds(HEAD, t), pl.ds(HEAD - 8, 8), pl.ds(0, HEAD)
        cols = lambda rows, c0, width=D_HALF: prev.at[rows, pl.ds(c0, width)]
        conv_ins = (cols(main, 2 * D_HALF), cols(main, 3 * D_HALF), cols(main, 4 * D_HALF),
                    cols(above8, 2 * D_HALF), cols(above8, 3 * D_HALF), cols(above8, 4 * D_HALF),
                    cols(main, COL_BA, HEAD), cw_ref, al_ref, dtb_ref)
        pool_ins = (cols(main, 0), cols(main, D_HALF), cols(above, 0), pw_ref, ps_ref, band_ref)

        def projection():
            xv = x_ref[...]
            r = lax.rsqrt(jnp.mean(xv * xv, axis=-1, keepdims=True) + EPS)
            nv = xv * r * nw_ref[...]
            nt_ref[...] = nv.T.astype(BF16)
            nb = nv.astype(BF16)
            yield
            for lo, hi in zip(col_bounds[:-1], col_bounds[1:]):
                proj_ref[:, lo:hi] = jnp.dot(nb, w_ref[:, lo:hi], preferred_element_type=F32)
                yield

        _interleave(projection(),
                    conv["stages"](conv_ins, (qn_ref, kn_ref, vs_ref, beta_ref, g_ref, yq_ref, yk_ref, yv_ref), (), tile),
                    pool["stages"](pool_ins, (y_ref,), (), tile))
        prev[0:HEAD] = prev[t:t + HEAD]
        prev[HEAD:HEAD + t] = proj_ref[...]

    last = n_tiles - 1
    now = lambda i: (jnp.minimum(i, last), 0)
    before = lambda i: (jnp.maximum(i - 1, 0), 0)
    const = lambda a: pl.BlockSpec(a.shape, lambda i: (0,) * a.ndim)
    half = pl.BlockSpec((t, D_HALF), before)
    return _call(
        body, name="front_fwd", grid=(n_tiles + 1,),
        in_specs=[pl.BlockSpec((t, D_MODEL), now), const(norm_w), const(w_pad), const(conv_w), const(a_log), const(dt_bias),
                  const(pool_w), const(pool_scale), const(bands), pl.BlockSpec(memory_space=pl.ANY)],
        out_specs=[pl.BlockSpec((t, N_IN_PAD), now), pl.BlockSpec((D_MODEL, t), lambda i: (0, jnp.minimum(i, last)))]
                  + [half] * 9,
        out_shape=[_sds((s, N_IN_PAD)), _sds((D_MODEL, s), BF16)] + [_sds((s, D_HALF))] * 9,
        scratch_shapes=[pltpu.VMEM((HEAD + t, N_IN_PAD), F32)],
        compiler_params=_params("arbitrary"),
    )(x, norm_w, w_pad, conv_w, a_log, dt_bias, pool_w, pool_scale, bands, after)


def _conv_pool_bwd(proj, pre, conv_w, a_log, dt_bias, dqn, dkn, dvs, dbeta, dg, dyp, pool_w, pool_scale):
    n_tiles = proj.shape[0] // CONV_T
    assert POOL_T == CONV_T
    tile_of = lambda i: n_tiles - 1 - i
    return _fused_call("conv_pool_bwd", n_tiles, [
        _conv_bwd_part(proj, pre, conv_w, a_log, dt_bias, dqn, dkn, dvs, dbeta, dg, tile_of, n_tiles),
        _pool_bwd_part(proj, dyp, pool_w, pool_scale, tile_of, n_tiles)])


def _rows8(x):
    acc = x[0:8]
    for r in range(8, x.shape[0], 8):
        acc = acc + x[r:r + 8]
    return acc


IN_T = 512


def _in_bwd(x, dh, norm_w, w_pad, pieces, after):
    s = x.shape[0]
    t = IN_T
    widths = [D_HALF] * 6 + [N_IN_PAD - COL_BA]
    starts = [sum(widths[:k]) for k in range(len(widths))]

    def body(*refs):
        x_ref, dh_ref, nw_ref, w_hbm = refs[:4]
        p_refs = refs[4:4 + len(pieces)]
        gx_ref, gnw_ref, w_ref, arrived = refs[5 + len(pieces):]
        first = pl.program_id(0) == 0
        cols = [pl.ds(c, wd) for c, wd in zip(starts, widths)]
        copies = [pltpu.make_async_copy(w_hbm.at[:, cs], w_ref.at[:, cs], arrived.at[k]) for k, cs in enumerate(cols)]

        @pl.when(first)
        def _():
            for cp in copies:
                cp.start()
            gnw_ref[...] = jnp.zeros_like(gnw_ref)

        def step(wait):
            dn = jnp.zeros((t, D_MODEL), F32)
            for k, (p_ref, c, wd) in enumerate(zip(p_refs, starts, widths)):
                if wait:
                    copies[k].wait()
                dn = dn + _bdot_nt(p_ref[...], w_ref[:, c:c + wd])
            xv = x_ref[...]
            r = lax.rsqrt(jnp.mean(xv * xv, axis=-1, keepdims=True) + EPS)
            xhat = xv * r
            gnw_ref[...] += _colsum(dn * xhat)
            dxh = dn * nw_ref[...]
            gx_ref[...] = dh_ref[...] + r * (dxh - xhat * jnp.mean(dxh * xhat, axis=-1, keepdims=True))

        pl.when(first)(functools.partial(step, True))
        pl.when(jnp.logical_not(first))(functools.partial(step, False))

    wide = pl.BlockSpec((t, D_MODEL), lambda i: (i, 0))
    hbm = pl.BlockSpec(memory_space=pl.ANY)
    return _call(
        body, name="in_bwd", grid=(s // t,),
        in_specs=[wide, wide, pl.BlockSpec((1, D_MODEL), lambda i: (0, 0)), hbm]
                 + [pl.BlockSpec((t, wd), lambda i: (i, 0)) for wd in widths] + [hbm],
        out_specs=[wide, pl.BlockSpec((1, D_MODEL), lambda i: (0, 0))],
        out_shape=[_sds((s, D_MODEL)), _sds((1, D_MODEL))],
        scratch_shapes=[pltpu.VMEM((D_MODEL, N_IN_PAD), BF16), pltpu.SemaphoreType.DMA((len(widths),))],
        compiler_params=_params("arbitrary"),
    )(x, dh, norm_w, w_pad, *pieces, after)


def _adamw_shard(name, w, g_own, g_got, cidx, m, v):
    _, r, c = w.shape
    half = r // 2
    rows = 256 if half % 256 == 0 else half
    per_half = half // rows

    def body(c_ref, w_ref, go_ref, gg_ref, m_ref, v_ref, gout_ref, d_ref, nm_ref, nv_ref):
        mine = (pl.program_id(0) // per_half) == c_ref[0]
        gv = jnp.where(mine, go_ref[:, :c], gg_ref[:, :c])
        gout_ref[0] = gv
        mn = ADAM_B1 * m_ref[0] + (1.0 - ADAM_B1) * gv
        vn = ADAM_B2 * v_ref[0] + (1.0 - ADAM_B2) * (gv * gv)
        m_hat = mn / (1.0 - ADAM_B1 ** ADAM_STEP)
        v_hat = vn / (1.0 - ADAM_B2 ** ADAM_STEP)
        d_ref[0] = -ADAM_LR * (m_hat / (jnp.sqrt(v_hat) + ADAM_EPS) + ADAM_WD * w_ref[0])
        nm_ref[0] = mn
        nv_ref[0] = vn

    blk = pl.BlockSpec((1, rows, c), lambda i, c_ref: (0, i, 0))
    gblk = pl.BlockSpec((rows, g_own.shape[1]), lambda i, c_ref: (i % per_half, 0))
    return _call(
        body, name=name,
        grid_spec=pltpu.PrefetchScalarGridSpec(
            num_scalar_prefetch=1, grid=(2 * per_half,),
            in_specs=[blk, gblk, gblk, blk, blk], out_specs=[blk] * 4),
        out_shape=[_sds((1, r, c))] * 4,
        compiler_params=_params("arbitrary"),
    )(cidx, w, g_own, g_got, m, v)


def _adamw_tiles(name, w, g, m, v):
    n = w.shape[0]
    nb = 77 if n % 77 == 0 else n

    def body(w_ref, g_ref, m_ref, v_ref, d_ref, nm_ref, nv_ref):
        gv = g_ref[...]
        mn = ADAM_B1 * m_ref[...] + (1.0 - ADAM_B1) * gv
        vn = ADAM_B2 * v_ref[...] + (1.0 - ADAM_B2) * (gv * gv)
        m_hat = mn / (1.0 - ADAM_B1 ** ADAM_STEP)
        v_hat = vn / (1.0 - ADAM_B2 ** ADAM_STEP)
        d_ref[...] = -ADAM_LR * (m_hat / (jnp.sqrt(v_hat) + ADAM_EPS) + ADAM_WD * w_ref[...])
        nm_ref[...] = mn
        nv_ref[...] = vn

    blk = pl.BlockSpec((nb, 8, HEAD), lambda i: (i, 0, 0))
    return _call(
        body, name=name, grid=(n // nb,),
        in_specs=[blk] * 4, out_specs=[blk] * 3, out_shape=[_sds(w.shape)] * 3,
        compiler_params=_params("arbitrary"),
    )(w, g, m, v)


def _make_copy(src, dst, send, recv, target):
    if target is None:
        return pltpu.make_async_copy(src, dst, recv)
    return pltpu.make_async_remote_copy(src_ref=src, dst_ref=dst, send_sem=send, recv_sem=recv,
                                        device_id=target, device_id_type=pl.DeviceIdType.MESH)


def _exchange(name, inputs, out_shapes, phases):
    n_in = len(inputs)
    n_out = len(out_shapes)
    n_cp = sum(len(p) for p in phases)

    def body(*refs):
        ins, outs = refs[:n_in], refs[n_in:n_in + n_out]
        send, recv = refs[n_in + n_out:]
        pos = (lax.axis_index("x"), lax.axis_index("y"), lax.axis_index("c"))
        k = 0
        for phase in phases:
            cps = []
            for src, dst, target in phase:
                cps.append(_make_copy(src(ins, outs, pos), dst(ins, outs, pos), send.at[k], recv.at[k],
                                      target and target(pos)))
                k += 1
            for cp in cps:
                cp.start()
            for cp in cps:
                cp.wait()

    anyspec = pl.BlockSpec(memory_space=pl.ANY)
    return _call(
        body, name=name,
        in_specs=[anyspec] * n_in, out_specs=[anyspec] * n_out, out_shape=list(out_shapes),
        scratch_shapes=[pltpu.SemaphoreType.DMA((n_cp,)), pltpu.SemaphoreType.DMA((n_cp,))],
    )(*inputs)


def _exchange_start(name, inputs, out_shapes, copies):
    n_in, n_out, n_cp = len(inputs), len(out_shapes), len(copies)

    def body(*refs):
        ins, lands = refs[:n_in], refs[n_in:n_in + n_out]
        sems = refs[n_in + n_out:n_in + n_out + 2 * n_cp]
        token = refs[-1]
        pos = (lax.axis_index("x"), lax.axis_index("y"), lax.axis_index("c"))
        for k, (src, dst, target) in enumerate(copies):
            _make_copy(src(ins, lands, pos), dst(ins, lands, pos), sems[2 * k], sems[2 * k + 1],
                       target and target(pos)).start()
        token[...] = jnp.zeros_like(token)

    hbm = pl.BlockSpec(memory_space=pltpu.HBM)
    sem = pl.BlockSpec(memory_space=pltpu.SEMAPHORE)
    bufs = list(inputs) + [lax.empty(o.shape, o.dtype) for o in out_shapes]
    outs = _call(
        body, name=name,
        out_shape=tuple([pltpu.SemaphoreType.DMA(())] * (2 * n_cp) + [pltpu.HBM(b.shape, b.dtype) for b in bufs]
                        + [_sds((8, HEAD))]),
        in_specs=[hbm] * len(bufs),
        out_specs=tuple([sem] * (2 * n_cp) + [hbm] * len(bufs) + [pl.BlockSpec(memory_space=pltpu.VMEM)]),
        input_output_aliases={i: 2 * n_cp + i for i in range(len(bufs))},
        compiler_params=pltpu.CompilerParams(has_side_effects=pltpu.SideEffectType.DATAFLOW_SIDE_EFFECTING),
    )(*[pltpu.with_memory_space_constraint(b, pltpu.HBM) for b in bufs])
    return outs[:2 * n_cp], outs[2 * n_cp:2 * n_cp + n_in], outs[2 * n_cp + n_in:-1], outs[-1]


def _exchange_wait(name, sems, sources, lands, copies, after):
    n_in, n_out, n_cp = len(sources), len(lands), len(copies)

    def body(*refs):
        ins, zones = refs[:n_in], refs[n_in:n_in + n_out]
        sem_refs = refs[n_in + n_out:n_in + n_out + 2 * n_cp]
        pos = (lax.axis_index("x"), lax.axis_index("y"), lax.axis_index("c"))
        for k, (src, dst, target) in enumerate(copies):
            cp = _make_copy(src(ins, zones, pos), dst(ins, zones, pos), sem_refs[2 * k], sem_refs[2 * k + 1],
                            target and target(pos))
            if target is None:
                cp.wait()
            else:
                cp.wait_send()
                cp.wait_recv()

    hbm = pl.BlockSpec(memory_space=pltpu.HBM)
    sem = pl.BlockSpec(memory_space=pltpu.SEMAPHORE)
    bufs = list(sources) + list(lands)
    outs = _call(
        body, name=name,
        out_shape=tuple(pltpu.HBM(b.shape, b.dtype) for b in bufs),
        in_specs=[hbm] * len(bufs) + [sem] * (2 * n_cp) + [pl.BlockSpec(memory_space=pl.ANY)],
        out_specs=tuple([hbm] * len(bufs)),
        input_output_aliases={i: i for i in range(len(bufs))},
        compiler_params=pltpu.CompilerParams(has_side_effects=pltpu.SideEffectType.DATAFLOW_SIDE_EFFECTING),
    )(*bufs, *sems, after)
    return outs[:n_in], outs[n_in:]


def _allreduce_tile(name, v):
    def body(v_ref, out_ref, slots, send, recv):
        x, y, c = lax.axis_index("x"), lax.axis_index("y"), lax.axis_index("c")
        me = 4 * x + 2 * y + c
        slots[me] = v_ref[...]
        cps = []
        for k in range(1, 8):
            peer = (x ^ (k >> 2), y ^ ((k >> 1) & 1), c ^ (k & 1))
            cps.append(pltpu.make_async_remote_copy(
                src_ref=v_ref, dst_ref=slots.at[me], send_sem=send.at[k - 1], recv_sem=recv.at[k - 1],
                device_id=peer, device_id_type=pl.DeviceIdType.MESH))
        for cp in cps:
            cp.start()
        for cp in cps:
            cp.wait()
        acc = slots[0]
        for i in range(1, 8):
            acc = acc + slots[i]
        out_ref[...] = acc

    vm = pl.BlockSpec(memory_space=pltpu.VMEM)
    return _call(
        body, name=name, in_specs=[vm], out_specs=vm, out_shape=_sds(v.shape),
        scratch_shapes=[pltpu.VMEM((8,) + v.shape, F32), pltpu.SemaphoreType.DMA((7,)), pltpu.SemaphoreType.DMA((7,))],
    )(v)


def _chip(pos):
    return 2 * pos[0] + pos[1]


def _other_chip(pos, mask):
    x, y, c = pos
    return (x ^ (mask >> 1), y ^ (mask & 1), c)


def _sibling(pos):
    return (pos[0], pos[1], 1 - pos[2])


def _gather_weights(wb, cb):
    rows = wb.shape[0] // 2
    x_nb, y_nb, diag = CHIP_MASKS

    def part(pos, mask, quarter=None):
        start = pos[2] * rows if quarter is None else pos[2] * rows + quarter * (rows // 2)
        return lambda outs: outs[0].at[_chip(pos) ^ mask, pl.ds(start, rows if quarter is None else rows // 2)]

    def passed_on(mask, to, quarter=None):
        return (lambda ins, outs, pos: part(pos, mask, quarter)(outs), lambda ins, outs, pos: part(pos, mask, quarter)(outs), to)

    first = [(lambda ins, outs, pos: ins[0].at[pl.ds(pos[2] * rows, rows)], lambda ins, outs, pos: part(pos, 0)(outs),
              functools.partial(_other_chip, mask=mask)) for mask in (x_nb, y_nb)]
    conv_cols = lambda ins, outs, pos: outs[1].at[:, pl.ds(pl.multiple_of(_chip(pos) * cb.shape[1], HEAD), cb.shape[1])]
    first += [(lambda ins, outs, pos: ins[1], conv_cols, functools.partial(_other_chip, mask=mask)) for mask in CHIP_MASKS]
    first += [(lambda ins, outs, pos: ins[1], conv_cols, None)]
    second = [passed_on(x_nb, functools.partial(_other_chip, mask=y_nb), quarter=0),
              passed_on(y_nb, functools.partial(_other_chip, mask=x_nb), quarter=1),
              passed_on(x_nb, _sibling), passed_on(y_nb, _sibling)]
    third = [passed_on(diag, _sibling)]
    return _exchange("gather_weights", [wb, cb],
                     [_sds((4,) + wb.shape, wb.dtype), _sds((cb.shape[0], 4 * cb.shape[1]), cb.dtype)], [first, second, third])


def _assemble_w_in(gw, wb, jidx):
    m = gw.shape[1]

    def body(j_ref, g_ref, wb_ref, o_ref):
        step = pl.program_id(0)

        @pl.when(step == 0)
        def _():
            o_ref[...] = jnp.zeros_like(o_ref)

        blk = jnp.where(step == j_ref[0], wb_ref[...], g_ref[0]).astype(F32)
        lane = lax.broadcasted_iota(I32, (m, BLK_IN_PAD), 1)
        for j in range(4):
            @pl.when(step == j)
            def _(j=j):
                base = j * BLK_IN // HEAD * HEAD
                shift = j * BLK_IN - base
                moved = pltpu.roll(blk, shift, 1) if shift else blk
                window = o_ref[:, base:base + BLK_IN_PAD].astype(F32)
                mine = (lane >= shift) & (lane < shift + BLK_IN)
                o_ref[:, base:base + BLK_IN_PAD] = jnp.where(mine, moved, window).astype(BF16)

    return _call(
        body, name="assemble_w_in",
        grid_spec=pltpu.PrefetchScalarGridSpec(
            num_scalar_prefetch=1, grid=(4,),
            in_specs=[pl.BlockSpec((1, m, BLK_IN_PAD), lambda j, j_ref: (j, 0, 0)),
                      pl.BlockSpec((m, BLK_IN_PAD), lambda j, j_ref: (0, 0))],
            out_specs=pl.BlockSpec((m, N_IN_PAD), lambda j, j_ref: (0, 0))),
        out_shape=_sds((m, N_IN_PAD), BF16),
        compiler_params=_params("arbitrary"),
    )(jidx, gw, wb)


def _gather_blocks(ob):
    copies = [(lambda ins, outs, pos: ins[0], lambda ins, outs, pos: outs[0].at[_chip(pos)],
               functools.partial(_other_chip, mask=mask)) for mask in CHIP_MASKS]
    copies.append((lambda ins, outs, pos: ins[0], lambda ins, outs, pos: outs[0].at[_chip(pos)], None))
    return [_sds((4,) + ob.shape, ob.dtype)], copies


def _reduce_sibling(name, arrays):
    n = len(arrays)
    halves = [a.shape[:-2] + (a.shape[-2] // 2, a.shape[-1]) for a in arrays]
    pieces = [(a, j) for a in range(n) for j in (range(arrays[a].shape[0]) if arrays[a].ndim == 3 else [None])]

    def body(*refs):
        whole, outs = refs[:n], refs[n:2 * n]
        own, land, summed = refs[2 * n:3 * n], refs[3 * n:4 * n], refs[4 * n:5 * n]
        send, recv, loaded, stored = refs[5 * n:]
        pos = (lax.axis_index("x"), lax.axis_index("y"), lax.axis_index("c"))
        block = lambda ref, j, rows=None: ref.at[(() if j is None else (j,)) + (() if rows is None else (rows,))]
        arrive, load = [], []
        for k, (a, j) in enumerate(pieces):
            h = halves[a][-2]
            arrive.append(pltpu.make_async_remote_copy(
                src_ref=block(whole[a], j, pl.ds((1 - pos[2]) * h, h)), dst_ref=block(land[a], j),
                send_sem=send.at[k], recv_sem=recv.at[k],
                device_id=_sibling(pos), device_id_type=pl.DeviceIdType.MESH))
            load.append(pltpu.make_async_copy(block(whole[a], j, pl.ds(pos[2] * h, h)), block(own[a], j), loaded.at[k]))
        for cp in arrive + load:
            cp.start()
        store = []
        for k, (a, j) in enumerate(pieces):
            load[k].wait()
            arrive[k].wait()
            at = Ellipsis if j is None else j
            summed[a][at] = (own[a][at].astype(F32) + land[a][at].astype(F32)).astype(summed[a].dtype)
            store.append(pltpu.make_async_copy(block(summed[a], j), block(outs[a], j), stored.at[k]))
            store[-1].start()
        for cp in store:
            cp.wait()

    vmem = [pltpu.VMEM(h, a.dtype) for h, a in zip(halves, arrays)]
    sems = [pltpu.SemaphoreType.DMA((len(pieces),))] * 4
    return _call(
        body, name=name,
        in_specs=[pl.BlockSpec(memory_space=pl.ANY)] * n, out_specs=[pl.BlockSpec(memory_space=pl.ANY)] * n,
        out_shape=[_sds(h, a.dtype) for h, a in zip(halves, arrays)],
        scratch_shapes=vmem * 3 + sems,
        compiler_params=_params(),
    )(*arrays)


def _to_other_chips(arrays, blocked):
    def src(ins, outs, pos, a, mask):
        return ins[a].at[_chip(pos) ^ mask] if blocked[a] else ins[a]

    outs = [_sds((3,) + (a.shape[1:] if b else a.shape), a.dtype) for a, b in zip(arrays, blocked)]
    copies = []
    for mi, mask in enumerate(CHIP_MASKS):
        for a in range(len(arrays)):
            copies.append((functools.partial(src, a=a, mask=mask), lambda ins, outs, pos, a=a, mi=mi: outs[a].at[mi],
                           functools.partial(_other_chip, mask=mask)))
    return outs, copies


def _sum_chips_swap(name, owns, gots, blocked):
    n = len(owns)
    shapes = [g.shape[-2:] for g in gots]

    def body(*refs):
        own, got, mine, theirs = refs[:n], refs[n:2 * n], refs[2 * n:3 * n], refs[3 * n:4 * n]
        own_v, got_v, sum_v = refs[4 * n:5 * n], refs[5 * n:6 * n], refs[6 * n:7 * n]
        send, recv, loaded, stored = refs[7 * n:]
        pos = (lax.axis_index("x"), lax.axis_index("y"), lax.axis_index("c"))
        load = []
        for a in range(n):
            load.append((pltpu.make_async_copy(own[a].at[_chip(pos)] if blocked[a] else own[a], own_v[a], loaded.at[2 * a]),
                         pltpu.make_async_copy(got[a], got_v[a], loaded.at[2 * a + 1])))
        for pair in load:
            for cp in pair:
                cp.start()
        out = []
        for a in range(n):
            for cp in load[a]:
                cp.wait()
            sum_v[a][...] = ((own_v[a][...].astype(F32) + got_v[a][0].astype(F32))
                             + (got_v[a][1].astype(F32) + got_v[a][2].astype(F32)))
            out.append(pltpu.make_async_remote_copy(
                src_ref=sum_v[a], dst_ref=theirs[a], send_sem=send.at[a], recv_sem=recv.at[a],
                device_id=_sibling(pos), device_id_type=pl.DeviceIdType.MESH))
            out.append(pltpu.make_async_copy(sum_v[a], mine[a], stored.at[a]))
            out[-2].start()
            out[-1].start()
        for cp in out:
            cp.wait()

    outs = _call(
        body, name=name,
        in_specs=[pl.BlockSpec(memory_space=pl.ANY)] * (2 * n), out_specs=[pl.BlockSpec(memory_space=pl.ANY)] * (2 * n),
        out_shape=[_sds(s) for s in shapes] * 2,
        scratch_shapes=[pltpu.VMEM(s, o.dtype) for s, o in zip(shapes, owns)]
                       + [pltpu.VMEM((3,) + s, g.dtype) for s, g in zip(shapes, gots)] + [pltpu.VMEM(s, F32) for s in shapes]
                       + [pltpu.SemaphoreType.DMA((n,)), pltpu.SemaphoreType.DMA((n,)),
                          pltpu.SemaphoreType.DMA((2 * n,)), pltpu.SemaphoreType.DMA((n,))],
        compiler_params=_params(),
    )(*owns, *gots)
    return outs[:n], outs[n:]


def _local_step(x, target, w_pad, w_out, conv_w, norm_w, pool_w, pool_scale, a_log, dt_bias, dn_norm_w, final_norm_w,
                after):
    proj, n_t, qn, kn, vs, beta, g, yq, yk, yv, y_pool = _front_fwd(
        x, norm_w, w_pad, conv_w, a_log, dt_bias, pool_w, pool_scale, after)
    w, att, qd, kd, tm, cd, o, vn, st = _delta_fwd(qn, kn, vs, beta, g)
    w_out = w_out(o) if callable(w_out) else w_out
    g_wout, dh, dyp, do, ddz, loss, g_fnw, g_dnw = _out_fwd_bwd(x, y_pool, o, proj, target, w_out, dn_norm_w, final_norm_w)
    dqn, dkn, dvs, dbeta, dg = _delta_bwd(do, vn, qd, kd, w, att, cd, st, qn, kn, vs, beta, g, tm)
    (dcq, dck, dcv, dba, g_cw, g_sm), (dpu, dpz, g_pw, g_ps) = _conv_pool_bwd(
        proj, (yq, yk, yv), conv_w, a_log, dt_bias, dqn, dkn, dvs, dbeta, dg, dyp, pool_w, pool_scale)
    pieces = [dpu, dpz, dcq, dck, dcv, ddz, dba]
    small = dict(norm_w=jnp.zeros_like(norm_w), pool_w=g_pw, pool_scale=g_ps, conv_w=g_cw[:CONV_K],
                 a_log=g_sm[0:1, 0:N_HEADS], dt_bias=g_sm[0:1, N_HEADS:2 * N_HEADS], dn_norm_w=g_dnw, final_norm_w=g_fnw)
    return loss[0, 0], n_t, g_wout, small, dh, pieces


SMALL_LAYOUT = (("pool_w", 512, HEAD, (1, N_HEADS, HEAD, HEAD)), ("final_norm_w", 8, HEAD, (D_MODEL,)),
                ("pool_scale", 4, HEAD, (1, D_HALF)), ("conv_w", 48, HEAD, (1, CONV_K, 3 * D_HALF)),
                ("dn_norm_w", 1, HEAD, (1, HEAD)), ("a_log", 1, N_HEADS, (1, N_HEADS)), ("dt_bias", 1, N_HEADS, (1, N_HEADS)),
                ("loss", 1, 1, ()))


def _small_offsets():
    offs, r = {}, 0
    for name, rows, _, _ in SMALL_LAYOUT:
        offs[name] = r
        r += -(-rows // 8) * 8
    assert r <= SMALL_ROWS
    return offs


def _pack_small(t):
    parts = []
    for name, rows, lanes, _ in SMALL_LAYOUT:
        a = t.get(name, jnp.zeros((1,), F32)).reshape(rows, lanes)
        parts.append(jnp.pad(a, ((0, -(-rows // 8) * 8 - rows), (0, HEAD - lanes))))
    buf = jnp.concatenate(parts, axis=0)
    return jnp.pad(buf, ((0, SMALL_ROWS - buf.shape[0]), (0, 0)))


def _adamw_small(w, g_own, g_got, cidx, m, v):
    offs = _small_offsets()
    names = [e[0] for e in SMALL_LAYOUT]
    n = len(names)

    def body(c_ref, w_ref, go_ref, gg_ref, m_ref, v_ref, *outs):
        own_low = c_ref[0] == 0
        gv = jnp.concatenate([jnp.where(own_low, go_ref[...], gg_ref[...]), jnp.where(own_low, gg_ref[...], go_ref[...])], axis=0)
        mn = ADAM_B1 * m_ref[...] + (1.0 - ADAM_B1) * gv
        vn = ADAM_B2 * v_ref[...] + (1.0 - ADAM_B2) * (gv * gv)
        m_hat = mn / (1.0 - ADAM_B1 ** ADAM_STEP)
        v_hat = vn / (1.0 - ADAM_B2 ** ADAM_STEP)
        dl = -ADAM_LR * (m_hat / (jnp.sqrt(v_hat) + ADAM_EPS) + ADAM_WD * w_ref[...])
        for kind, arr in enumerate((gv, dl, mn, vn)):
            for i, (name, rows, lanes, _) in enumerate(SMALL_LAYOUT):
                outs[kind * n + i][...] = arr[offs[name]:offs[name] + rows, :lanes]

    whole = lambda shape: pl.BlockSpec(shape, lambda i, c_ref: (0,) * len(shape))
    out_shapes = [_sds((rows, lanes)) for _, rows, lanes, _ in SMALL_LAYOUT] * 4
    res = _call(
        body, name="adamw_small",
        grid_spec=pltpu.PrefetchScalarGridSpec(
            num_scalar_prefetch=1, grid=(1,),
            in_specs=[whole(w.shape), whole(g_own.shape), whole(g_got.shape), whole(m.shape), whole(v.shape)],
            out_specs=[whole(o.shape) for o in out_shapes]),
        out_shape=out_shapes,
        compiler_params=_params("arbitrary"),
    )(cidx, w, g_own, g_got, m, v)
    return [{name: res[kind * n + i].reshape(shape) for i, (name, _, _, shape) in enumerate(SMALL_LAYOUT)}
            for kind in range(4)]


def kernel(x, norm_w, w_in, pool_w, pool_scale, conv_w, a_log, dt_bias, dn_norm_w, w_out, final_norm_w, loss_target, m_norm_w, m_w_in, m_pool_w, m_pool_scale, m_conv_w, m_a_log, m_dt_bias, m_dn_norm_w, m_w_out, m_final_norm_w, v_norm_w, v_w_in, v_pool_w, v_pool_scale, v_conv_w, v_a_log, v_dt_bias, v_dn_norm_w, v_w_out, v_final_norm_w):
    cidx = lax.axis_index("c").astype(I32).reshape(1)
    jidx = (2 * lax.axis_index("x") + lax.axis_index("y")).astype(I32)

    wb = jnp.pad(w_in[0].astype(BF16), ((0, 0), (0, BLK_IN_PAD - BLK_IN)))
    ob = w_out[0].astype(BF16)
    gw, cw_full = _gather_weights(wb, conv_w[0])
    w_pad = _assemble_w_in(gw, wb, jidx.reshape(1))

    lands_o, copies_o = _gather_blocks(ob)
    sems_o, ob_thru, zones_o, token_o = _exchange_start("gather_w_out_start", [ob], lands_o, copies_o)

    def w_out_full(after):
        _, (got,) = _exchange_wait("gather_w_out_wait", sems_o, ob_thru, zones_o, copies_o, after)
        return got.reshape(D_MODEL, D_MODEL)

    loss, n_t, g_wout, small, dh, pieces = _local_step(
        x[0], loss_target[0], w_pad, w_out_full, cw_full, norm_w, pool_w[0], pool_scale, a_log, dt_bias,
        dn_norm_w, final_norm_w.reshape(1, D_MODEL), token_o)
    small["loss"] = loss

    blocks_out = g_wout.reshape(4, BLK_OUT, D_MODEL)
    early = _reduce_sibling("reduce_sibling_out", [blocks_out, _pack_small(small)])
    lands_e, copies_e = _to_other_chips(early, [True, False])
    sems_e, early, zones_e, token_e = _exchange_start("reduce_chips_out_start", early, lands_e, copies_e)
    g_win = _grad_w_in(n_t, pieces, token_e)
    late = _reduce_sibling("reduce_sibling", [g_win])
    lands, copies = _to_other_chips(late, [True])
    sems, late, zones, token = _exchange_start("reduce_chips_start", late, lands, copies)
    gx, g_nw = _in_bwd(x[0], dh, norm_w, w_pad, pieces, token)
    g_nw = _allreduce_tile("reduce_norm_w", g_nw.reshape(8, HEAD)).reshape(1, D_MODEL)
    early, from_chips_e = _exchange_wait("reduce_chips_out_wait", sems_e, early, zones_e, copies_e, gx)
    late, from_chips = _exchange_wait("reduce_chips_wait", sems, late, zones, copies, gx)
    halves, other_halves = _sum_chips_swap("sum_chips_swap", list(late) + list(early), list(from_chips) + list(from_chips_e),
                                           [True, True, False])

    weights = dict(norm_w=norm_w, w_in=w_in, pool_w=pool_w, pool_scale=pool_scale, conv_w=conv_w, a_log=a_log,
                   dt_bias=dt_bias, dn_norm_w=dn_norm_w, w_out=w_out, final_norm_w=final_norm_w)
    ms = dict(norm_w=m_norm_w, w_in=m_w_in, pool_w=m_pool_w, pool_scale=m_pool_scale, conv_w=m_conv_w, a_log=m_a_log,
              dt_bias=m_dt_bias, dn_norm_w=m_dn_norm_w, w_out=m_w_out, final_norm_w=m_final_norm_w)
    vs = dict(norm_w=v_norm_w, w_in=v_w_in, pool_w=v_pool_w, pool_scale=v_pool_scale, conv_w=v_conv_w, a_log=v_a_log,
              dt_bias=v_dt_bias, dn_norm_w=v_dn_norm_w, w_out=v_w_out, final_norm_w=v_final_norm_w)
    names = ["norm_w", "w_in", "pool_w", "pool_scale", "conv_w", "a_log", "dt_bias", "dn_norm_w", "w_out", "final_norm_w"]
    small_names = [n for n in names if n not in ("w_in", "w_out")]

    def pack(t):
        conv = lax.dynamic_update_slice_in_dim(jnp.zeros((CONV_K, 3 * D_HALF), F32), t["conv_w"][0], jidx * BLK_CONV, axis=1)
        return _pack_small({**{n: t[n] for n in small_names if n != "conv_w"}, "conv_w": conv})

    results = [{}, {}, {}, {}]
    to_tiles = lambda a: jnp.transpose(a, (2, 0, 1)).reshape(BLK_IN, 8, HEAD)
    from_tiles = lambda a: jnp.transpose(a, (1, 2, 0)).reshape(1, D_MODEL, BLK_IN)
    lo = jnp.where(cidx[0] == 0, halves[0], other_halves[0])
    hi = jnp.where(cidx[0] == 0, other_halves[0], halves[0])
    g_tiles = jnp.concatenate([lo[:, :BLK_IN].T, hi[:, :BLK_IN].T], axis=1).reshape(BLK_IN, 8, HEAD)
    outs = _adamw_tiles("adamw_w_in", to_tiles(w_in), g_tiles, to_tiles(m_w_in), to_tiles(v_w_in))
    for res, o in zip(results, (g_tiles,) + tuple(outs)):
        res["w_in"] = from_tiles(o)
    outs = _adamw_shard("adamw_w_out", w_out, halves[1], other_halves[1], cidx, m_w_out, v_w_out)
    for res, o in zip(results, outs):
        res["w_out"] = o
    outs = _adamw_small(pack(weights), halves[2], other_halves[2], cidx, pack(ms), pack(vs))
    for res, got in zip(results, outs):
        got["conv_w"] = lax.dynamic_slice_in_dim(got["conv_w"], jidx * BLK_CONV, BLK_CONV, axis=2)
        res.update(got)
    one_tile = lambda a: a.reshape(1, 8, HEAD)
    outs = _adamw_tiles("adamw_norm_w", one_tile(norm_w), one_tile(g_nw), one_tile(m_norm_w), one_tile(v_norm_w))
    for res, o in zip(results, (g_nw,) + tuple(outs)):
        res["norm_w"] = o.reshape(1, D_MODEL)
    grads, delta, new_m, new_v = results

    return (grads["loss"], gx[None], *[grads[n] for n in names], *[delta[n] for n in names],
            *[new_m[n] for n in names], *[new_v[n] for n in names])
```

```python
import functools

import jax
import jax.numpy as jnp
import numpy as np
from jax import lax
from jax.experimental import pallas as pl
from jax.experimental.pallas import tpu as pltpu

F32 = jnp.float32
BF16 = jnp.bfloat16
I32 = jnp.int32

D_MODEL = 1024
D_HALF = 512
N_HEADS = 4
HEAD = 128
CHUNK = 64
PAIR = 2 * CHUNK
WINDOWS = (2, 4, 8, 16)
CONV_K = 4
EPS = 1e-6
N_IN = 3080
N_IN_PAD = 3200
BLK_IN = 770
BLK_IN_PAD = 896
BLK_OUT = 256
BLK_CONV = 384
COL_BA = 3072
QK_SCALE = HEAD ** -0.5
SMALL_ROWS = 608
VMEM_LIMIT = 56 * 1024 * 1024

ADAM_LR = 0.001
ADAM_B1 = 0.9
ADAM_B2 = 0.999
ADAM_EPS = 1e-08
ADAM_WD = 0.01
ADAM_STEP = 10

CHIP_MASKS = (2, 1, 3)
HEADS = range(N_HEADS)
HEAD_COLS = [slice(h * HEAD, (h + 1) * HEAD) for h in HEADS]


def _call(body, **kw):
    return pl.pallas_call(body, **kw)


def _params(*sem):
    return pltpu.CompilerParams(dimension_semantics=sem, vmem_limit_bytes=VMEM_LIMIT)


def _sds(shape, dtype=F32):
    return jax.ShapeDtypeStruct(shape, dtype)


def _bdot(a, b):
    return jnp.dot(a.astype(BF16), b.astype(BF16), preferred_element_type=F32)


def _bdot_nt(a, b):
    return lax.dot_general(a.astype(BF16), b.astype(BF16), (((1,), (1,)), ((), ())), preferred_element_type=F32)


def _bdot_tn(a, b):
    return lax.dot_general(a.astype(BF16), b.astype(BF16), (((0,), (0,)), ((), ())), preferred_element_type=F32)


def _side(a, b):
    return jnp.concatenate([a.astype(BF16), b.astype(BF16)], axis=1)


def _stack(a, b):
    return jnp.concatenate([a.astype(BF16), b.astype(BF16)], axis=0)


def _split(a):
    hi = a.astype(BF16)
    lo = (a - hi.astype(F32)).astype(BF16)
    return hi, lo


def _mask_dot(m, b):
    n = b.shape[1]
    both = jnp.dot(m, jnp.concatenate(_split(b), axis=1), preferred_element_type=F32)
    return both[:, :n] + both[:, n:]


def _sigmoid(x):
    return 0.5 * jnp.tanh(0.5 * x) + 0.5


def _softplus(x):
    return jnp.maximum(x, 0.0) + jnp.log(1.0 + jnp.exp(-jnp.abs(x)))


def _rowsum(x):
    return jnp.sum(x, axis=-1, keepdims=True)


def _colsum(x):
    return jnp.sum(x, axis=0, keepdims=True)


def _shift_down(xv, prev8, k):
    r = pltpu.roll(xv, k, 0)
    q = pltpu.roll(prev8, k, 0)
    row = lax.broadcasted_iota(I32, prev8.shape, 0)
    top = jnp.where(row < k, q, r[0:8])
    return jnp.concatenate([top, r[8:]], axis=0)


def _shift_up(xv, next8, k):
    t = xv.shape[0]
    r = pltpu.roll(xv, t - k, 0)
    q = pltpu.roll(next8, 8 - k, 0)
    row = lax.broadcasted_iota(I32, next8.shape, 0)
    bot = jnp.where(row >= 8 - k, q, r[t - 8:])
    return jnp.concatenate([r[:t - 8], bot], axis=0)


INTRA_PAIRS = 2
UNITS = [(pp, h) for pp in range(INTRA_PAIRS) for h in HEADS]


def _heads_of(ref, rows=PAIR, units=UNITS):
    return [ref[pp * rows:(pp + 1) * rows, HEAD_COLS[h]] for pp, h in units]


def _put_heads_of(ref, vals, rows=PAIR, units=UNITS):
    for (pp, h), v in zip(units, vals):
        ref[pp * rows:(pp + 1) * rows, HEAD_COLS[h]] = v.astype(ref.dtype)


_heads = _heads_of
_put_heads = _put_heads_of


def _each(fn, *lists):
    return [fn(*args) for args in zip(*lists)]


def _pool_bands(t, anti=False):
    r = np.arange(t)[:, None]
    c = np.arange(t + HEAD)[None, :]
    d = (c - r) if anti else (r - c + HEAD)
    return jnp.asarray(np.stack([(d >= 0) & (d < w) for w in WINDOWS]), BF16)


def _pool_mix(u, halo, z, pw, bands, row0):
    t = u[0].shape[0]
    rows = row0 + lax.broadcasted_iota(I32, (t, 1), 0) + 1
    cnt = [jnp.minimum(rows, w).astype(F32) for w in WINDOWS]
    win = _each(lambda b, h, v: _mask_dot(b, jnp.concatenate([h, v], axis=0)), bands, halo, u)
    mix = _each(lambda a, c, v: a / c - v, win, cnt, u)
    mixed = _each(_bdot, mix, pw)
    return mix, mixed, _each(_sigmoid, z), cnt


POOL_T = 256


def _conv_taps(xv, prev8):
    return [_shift_down(xv, prev8, CONV_K - 1 - j) for j in range(CONV_K - 1)] + [xv]


def _conv_pre(taps, cw):
    y = taps[CONV_K - 1] * cw[CONV_K - 1:CONV_K]
    for j in range(CONV_K - 2, -1, -1):
        y = y + taps[j] * cw[j:j + 1]
    return y


CONV_T = 256


def _conv_specs(t, tile_of=lambda i: i):
    tiles = [pl.BlockSpec((t, D_HALF), functools.partial(lambda i, p: (tile_of(i), 2 + p), p=p)) for p in range(3)]
    halos = [pl.BlockSpec((8, D_HALF),
                          functools.partial(lambda i, p: (jnp.maximum(tile_of(i) * (t // 8) - 1, 0), 2 + p), p=p))
             for p in range(3)]
    return tiles + halos


def _pair_masks():
    r = lax.broadcasted_iota(I32, (PAIR, PAIR), 0)
    c = lax.broadcasted_iota(I32, (PAIR, PAIR), 1)
    same = jnp.right_shift(r, 6) == jnp.right_shift(c, 6)
    return same, same & (r >= c), same & (r > c), r == c


def _interleave(*stage_lists):
    live = list(stage_lists)
    while live:
        for gen in list(live):
            try:
                next(gen)
            except StopIteration:
                live.remove(gen)


def _pipelined_step(t, n, leading, trailing):
    @pl.when(t == 0)
    def _():
        _interleave(*leading())

    @pl.when(jnp.logical_and(t > 0, t < n))
    def _():
        _interleave(*leading(), *trailing())

    @pl.when(t == n)
    def _():
        _interleave(*trailing())


def _pair_common_stages(cm, qn, kn, vs, beta, g):
    same, incl, strict, eye = _pair_masks()
    incl_b = incl.astype(BF16)
    first = lax.broadcasted_iota(I32, (PAIR, HEAD), 0) < CHUNK
    cm.update(same=same, incl=incl, strict=strict, eye=eye)
    gc = _each(lambda gv: _mask_dot(incl_b, gv), g)
    q = _each(lambda v: v * QK_SCALE, qn)
    kb = _each(lambda k, b: k * b, kn, beta)
    cm.update(gc=gc, q=q, kb=kb, vb=_each(lambda v, b: v * b, vs, beta))
    yield
    both = _each(lambda a, b, c: _bdot_nt(_stack(a, b), c), kb, q, kn)
    cm.update(kk=[v[:PAIR] for v in both], qk=[v[PAIR:] for v in both])
    gc_row = _each(lambda v: _colsum(jnp.where(eye, v, 0.0)), gc)
    gl = _each(lambda v: jnp.where(first, v[CHUNK - 1:CHUNK], v[PAIR - 1:PAIR]), gc)
    egc = _each(jnp.exp, gc)
    cm.update(gl=gl, egc=egc,
              decay=_each(lambda v, r: jnp.where(incl, jnp.exp(jnp.where(incl, v - r, 0.0)), 0.0), gc, gc_row))
    yield
    cm.update(ekd=_each(lambda a, b: jnp.exp(a - b), gl, gc), cd=_each(jnp.exp, gl),
              kbg=_each(lambda k, e: k * e, kb, egc))
    yield


def _tri_inv_stages(out, a, eye_f):
    p = _each(lambda v: eye_f - v, a)
    x = _each(_bdot, a, a)
    yield
    for it in range(4):
        both = _each(lambda xv, pv: _bdot(xv, _side(pv, xv)), x, p)
        p = _each(lambda pv, b: pv + b[:, :PAIR], p, both)
        x = [b[:, PAIR:] for b in both]
        yield
    out["t"] = _each(lambda pv, xv: pv + _bdot(pv, xv), p, x)
    yield


def _chunk_scalar_spec(pairs=1, index=lambda i: (i, 0)):
    return pl.BlockSpec((16 * pairs, D_HALF), index)


SCAN_PAIRS = 2
SCAN_ROWS = SCAN_PAIRS * PAIR


def _delta_fwd(qn, kn, vs, beta, g):
    s = qn.shape[0]
    n_steps = s // SCAN_ROWS
    n_chunks = s // CHUNK
    assert INTRA_PAIRS == SCAN_PAIRS

    def body(qn_ref, kn_ref, vs_ref, beta_ref, g_ref, w_ref, att_ref, qd_ref, kd_ref, t_ref, cd_ref, o_ref, vn_ref, st_ref,
             state, u_s, w_s, att_s, qd_s, kd_s, cd_s):
        t = pl.program_id(0)

        @pl.when(t == 1)
        def _():
            state[...] = jnp.zeros_like(state)

        cur = lax.rem(t, 2)
        prev = 1 - cur
        cols = list(enumerate(HEAD_COLS))

        def recurrence():
            sm = [state[h] for h in HEADS]
            for ci in range(2 * SCAN_PAIRS):
                rs = slice(ci * CHUNK, (ci + 1) * CHUNK)
                for h in HEADS:
                    st_ref[ci, h] = sm[h]
                both = [_bdot(jnp.concatenate([w_s[prev, rs, sl], qd_s[prev, rs, sl]], axis=0), sm[h]) for h, sl in cols]
                vn = [u_s[prev, rs, sl] - both[h][:CHUNK] for h, sl in cols]
                for h, sl in cols:
                    vn_ref[rs, sl] = vn[h].astype(BF16)
                    o_ref[rs, sl] = both[h][CHUNK:]
                yield
                sm = [sm[h] * cd_s[prev, ci * 8:ci * 8 + 1, sl] + _bdot_tn(kd_s[prev, rs, sl], vn[h]) for h, sl in cols]
                yield
            for h in HEADS:
                state[h] = sm[h]
            for pp in range(SCAN_PAIRS):
                rp = slice(pp * PAIR, (pp + 1) * PAIR)
                intra = [_bdot(att_s[prev, rp, sl], vn_ref[rp, sl]) for sl in HEAD_COLS]
                for h, sl in cols:
                    o_ref[rp, sl] += intra[h]
                yield

        def factors():
            kn = _heads(kn_ref)
            cm = {}
            yield from _pair_common_stages(cm, _heads(qn_ref), kn, _heads(vs_ref), _heads(beta_ref), _heads(g_ref))
            a = _each(lambda kk, d: jnp.where(cm["strict"], kk * d, 0.0), cm["kk"], cm["decay"])
            inv = {}
            yield from _tri_inv_stages(inv, a, cm["eye"].astype(F32))
            tm = inv["t"]
            uw = _each(lambda tv, a, b: _bdot(tv, _side(a, b)), tm, cm["vb"], cm["kbg"])
            res = dict(u=[v[:, :HEAD] for v in uw], w=[v[:, HEAD:] for v in uw],
                       att=_each(lambda a, b: a * b, cm["qk"], cm["decay"]),
                       qd=_each(lambda a, b: a * b, cm["q"], cm["egc"]), kd=_each(lambda a, b: a * b, kn, cm["ekd"]))
            yield
            _put_heads(t_ref, tm)
            for key, out, keep in (("w", w_ref, w_s), ("att", att_ref, att_s), ("qd", qd_ref, qd_s), ("kd", kd_ref, kd_s)):
                _put_heads(out, res[key])
                for (pp, h), v in zip(UNITS, res[key]):
                    keep[cur, pp * PAIR:(pp + 1) * PAIR, HEAD_COLS[h]] = v.astype(BF16)
            for (pp, h), v in zip(UNITS, res["u"]):
                u_s[cur, pp * PAIR:(pp + 1) * PAIR, HEAD_COLS[h]] = v
            for ci in range(2):
                for (pp, h), v in zip(UNITS, cm["cd"]):
                    rows8 = slice(pp * 16 + ci * 8, pp * 16 + (ci + 1) * 8)
                    cd_ref[rows8, HEAD_COLS[h]] = v[ci * CHUNK:ci * CHUNK + 8]
                    cd_s[cur, rows8, HEAD_COLS[h]] = v[ci * CHUNK:ci * CHUNK + 8]
            yield

        _pipelined_step(t, n_steps, lambda: [factors()], lambda: [recurrence()])

    last = n_steps - 1
    now = lambda i: (jnp.minimum(i, last), 0)
    before = lambda i: (jnp.maximum(i - 1, 0), 0)
    rows = lambda index: pl.BlockSpec((SCAN_ROWS, D_HALF), index)
    slot = lambda r, dtype: pltpu.VMEM((2, r, D_HALF), dtype)
    return _call(
        body, name="delta_fwd", grid=(n_steps + 1,),
        in_specs=[rows(now)] * 5,
        out_specs=[rows(now)] * 5 + [_chunk_scalar_spec(SCAN_PAIRS, now), rows(before), rows(before),
                                     pl.BlockSpec((2 * SCAN_PAIRS, N_HEADS, HEAD, HEAD), lambda i: (jnp.maximum(i - 1, 0), 0, 0, 0))],
        out_shape=[_sds((s, D_HALF), BF16)] * 5 + [_sds((s // 8, D_HALF)), _sds((s, D_HALF)), _sds((s, D_HALF), BF16),
                                                  _sds((n_chunks, N_HEADS, HEAD, HEAD))],
        scratch_shapes=[pltpu.VMEM((N_HEADS, HEAD, HEAD), F32), slot(SCAN_ROWS, F32), slot(SCAN_ROWS, BF16),
                        slot(SCAN_ROWS, BF16), slot(SCAN_ROWS, BF16), slot(SCAN_ROWS, BF16), slot(16 * SCAN_PAIRS, F32)],
        compiler_params=_params("arbitrary"),
    )(qn, kn, vs, beta, g)


OUT_T = 512
OUT_ROWS = 256


def _out_fwd_bwd(x, y_pool, o, proj, target, w_out, dn_norm_w, final_norm_w):
    s = x.shape[0]
    t = OUT_T

    def body(x_ref, yp_ref, o_ref, z_ref, tg_ref, wo_ref, dnw_ref, fnw_ref,
             gwo_ref, dh_ref, dyp_ref, do_ref, dz_ref, loss_ref, gfn_ref, gdn_ref, y_ref, yt_ref, gwo_acc):
        @pl.when(pl.program_id(0) == 0)
        def _():
            loss_ref[...] = jnp.zeros_like(loss_ref)
            gfn_ref[...] = jnp.zeros_like(gfn_ref)
            gdn_ref[...] = jnp.zeros_like(gdn_ref)
            gwo_acc[...] = jnp.zeros_like(gwo_acc)

        dnw = dnw_ref[...]
        fnw = fnw_ref[...]

        def stages(rows, lead):
            for _ in range(lead):
                yield
            ypv = yp_ref[rows]
            y_ref[rows, :D_HALF] = ypv.astype(BF16)
            yt_ref[:D_HALF, rows] = ypv.T.astype(BF16)
            keep = []
            for h in HEADS:
                ov = o_ref[rows, HEAD_COLS[h]]
                zv = z_ref[rows, HEAD_COLS[h]]
                ro = lax.rsqrt(jnp.mean(ov * ov, axis=-1, keepdims=True) + EPS)
                ohat = ov * ro
                sg = _sigmoid(zv)
                keep.append((ro, ohat, zv, sg))
                ydn = ohat * dnw * (zv * sg)
                y_ref[rows, D_HALF + h * HEAD:D_HALF + (h + 1) * HEAD] = ydn.astype(BF16)
                yt_ref[D_HALF + h * HEAD:D_HALF + (h + 1) * HEAD, rows] = ydn.T.astype(BF16)
            yield
            hv = x_ref[rows] + jnp.dot(y_ref[rows], wo_ref[...], preferred_element_type=F32)
            yield
            r2 = lax.rsqrt(jnp.mean(hv * hv, axis=-1, keepdims=True) + EPS)
            hhat = hv * r2
            err = hhat * fnw - tg_ref[rows]
            loss_ref[...] += 0.5 * jnp.sum(_rowsum(err * err) * (1.0 / D_MODEL), axis=0, keepdims=True)
            dout = err * (1.0 / D_MODEL)
            gfn_ref[...] += _colsum(dout * hhat)
            dhh = dout * fnw
            dh = r2 * (dhh - hhat * jnp.mean(dhh * hhat, axis=-1, keepdims=True))
            dh_ref[rows] = dh
            yield
            if lead == t // OUT_ROWS - 1:
                gwo_acc[...] += _bdot(yt_ref[...], dh_ref[...])
            dy = _bdot_nt(dh, wo_ref[...])
            yield
            dyp_ref[rows] = dy[:, :D_HALF]
            gdn = jnp.zeros((1, HEAD), F32)
            for h in HEADS:
                ro, ohat, zv, sg = keep[h]
                dyd = dy[:, D_HALF + h * HEAD:D_HALF + (h + 1) * HEAD]
                sz = zv * sg
                dz_ref[rows, HEAD_COLS[h]] = (dyd * ohat * dnw * (sg * (1.0 + zv * (1.0 - sg)))).astype(BF16)
                gdn = gdn + _colsum(dyd * ohat * sz)
                doh = dyd * dnw * sz
                do_ref[rows, HEAD_COLS[h]] = ro * (doh - ohat * jnp.mean(doh * ohat, axis=-1, keepdims=True))
            gdn_ref[...] += gdn
            yield

        _interleave(*[stages(slice(k * OUT_ROWS, (k + 1) * OUT_ROWS), k) for k in range(t // OUT_ROWS)])

        @pl.when(pl.program_id(0) == pl.num_programs(0) - 1)
        def _():
            gwo_ref[...] = gwo_acc[...].astype(BF16)

    wide = pl.BlockSpec((t, D_MODEL), lambda i: (i, 0))
    half = pl.BlockSpec((t, D_HALF), lambda i: (i, 0))
    const = lambda shape: pl.BlockSpec(shape, lambda i: (0,) * len(shape))
    return _call(
        body, name="out_fwd_bwd", grid=(s // t,),
        in_specs=[wide, half, half, pl.BlockSpec((t, D_HALF), lambda i: (i, 5)), wide,
                  const((D_MODEL, D_MODEL)), const((1, HEAD)), const((1, D_MODEL))],
        out_specs=[const((D_MODEL, D_MODEL)), wide, half, half, half,
                   const((1, HEAD)), const((1, D_MODEL)), const((1, HEAD))],
        out_shape=[_sds((D_MODEL, D_MODEL), BF16), _sds((s, D_MODEL)), _sds((s, D_HALF)), _sds((s, D_HALF)),
                   _sds((s, D_HALF), BF16), _sds((1, HEAD)), _sds((1, D_MODEL)), _sds((1, HEAD))],
        scratch_shapes=[pltpu.VMEM((t, D_MODEL), BF16), pltpu.VMEM((D_MODEL, t), BF16), pltpu.VMEM((D_MODEL, D_MODEL), F32)],
        compiler_params=_params("arbitrary"),
    )(x, y_pool, o, proj, target, w_out, dn_norm_w, final_norm_w)


def _grad_w_in(at, pieces, after):
    m, s = at.shape
    n = len(pieces)
    tn, tk = D_HALF, min(s, 1024)

    def body(a_ref, *refs):
        p_refs, o_ref, acc = refs[:n], refs[n + 1], refs[n + 2]

        @pl.when(pl.program_id(0) == 0)
        def _():
            acc[...] = jnp.zeros_like(acc)

        av = a_ref[...]
        for p in range(n):
            acc[:, p * tn:(p + 1) * tn] += _bdot(av, p_refs[p][...])

        @pl.when(pl.program_id(0) == pl.num_programs(0) - 1)
        def _():
            for j in range(4):
                base = j * BLK_IN // HEAD * HEAD
                win = acc[:, base:base + BLK_IN_PAD]
                if j * BLK_IN > base:
                    win = pltpu.roll(win, BLK_IN_PAD - (j * BLK_IN - base), 1)
                o_ref[j] = win.astype(BF16)

    return _call(
        body, name="grad_w_in", grid=(s // tk,),
        in_specs=[pl.BlockSpec((m, tk), lambda k: (0, k))] + [pl.BlockSpec((tk, tn), lambda k: (k, 0))] * n
                 + [pl.BlockSpec(memory_space=pl.ANY)],
        out_specs=pl.BlockSpec((4, m, BLK_IN_PAD), lambda k: (0, 0, 0)),
        out_shape=_sds((4, m, BLK_IN_PAD), BF16),
        scratch_shapes=[pltpu.VMEM((m, n * tn), F32)],
        compiler_params=_params("arbitrary"),
    )(at, *pieces, after)


def _delta_bwd(do, vn, qd, kd, w, att, cd, st, qn, kn, vs, beta, g, tm):
    s = do.shape[0]
    n_steps = s // SCAN_ROWS
    assert INTRA_PAIRS == SCAN_PAIRS

    def body(do_ref, vn_ref, qd_ref, kd_ref, w_ref, att_ref, cd_ref, st_ref, qn_ref, kn_ref, vs_ref, beta_ref, g_ref, t_ref,
             dqn_ref, dkn_ref, dvs_ref, dbeta_ref, dg_ref, dstate, du_s, dw_s, datt_s, dqd_s, dkd_s, dcd_s):
        t = pl.program_id(0)

        @pl.when(t == 0)
        def _():
            dstate[...] = jnp.zeros_like(dstate)

        cur = lax.rem(t, 2)
        prev = 1 - cur
        cols = list(enumerate(HEAD_COLS))
        _, incl, _, _ = _pair_masks()

        def recurrence():
            dv_intra = []
            for pp in range(SCAN_PAIRS):
                rp = slice(pp * PAIR, (pp + 1) * PAIR)
                dv_intra.append([_bdot_tn(att_ref[rp, sl], do_ref[rp, sl]) for _, sl in cols])
                for _, sl in cols:
                    datt_s[cur, rp, sl] = jnp.where(incl, _bdot_nt(do_ref[rp, sl], vn_ref[rp, sl]), 0.0)
                yield
            ds = [dstate[h] for h in HEADS]
            for ci in range(2 * SCAN_PAIRS - 1, -1, -1):
                rs = slice(ci * CHUNK, (ci + 1) * CHUNK)
                in_pair = slice((ci % 2) * CHUNK, (ci % 2 + 1) * CHUNK)
                sm = [st_ref[ci, h] for h in HEADS]
                dvn = [dv_intra[ci // 2][h][in_pair] + _bdot(kd_ref[rs, sl], ds[h]) for h, sl in cols]
                dkd = [_bdot_nt(vn_ref[rs, sl], ds[h]) for h, sl in cols]
                dcd = [jnp.broadcast_to(_rowsum(_colsum(ds[h] * sm[h])), (8, HEAD)) for h in HEADS]
                yield
                both = [_bdot_nt(_stack(do_ref[rs, sl], dvn[h]), sm[h]) for h, sl in cols]
                for h, sl in cols:
                    du_s[cur, rs, sl] = dvn[h].astype(BF16)
                    dqd_s[cur, rs, sl] = both[h][:CHUNK]
                    dw_s[cur, rs, sl] = (-both[h][CHUNK:]).astype(BF16)
                    dkd_s[cur, rs, sl] = dkd[h]
                    dcd_s[cur, ci * 8:(ci + 1) * 8, sl] = dcd[h]
                ds = [ds[h] * cd_ref[ci * 8:ci * 8 + 1, sl]
                      + _bdot_tn(_stack(qd_ref[rs, sl], w_ref[rs, sl]), _stack(do_ref[rs, sl], -dvn[h])) for h, sl in cols]
                yield
            for h in HEADS:
                dstate[h] = ds[h]

        def factors(units):
            _heads = functools.partial(_heads_of, units=units)
            _put_heads = functools.partial(_put_heads_of, units=units)
            ones = jnp.ones((2 * PAIR, HEAD), BF16)
            tn = (((0,), (0,)), ((), ()))
            kept = lambda ref, rows=PAIR: [ref[prev, pp * rows:(pp + 1) * rows, HEAD_COLS[h]] for pp, h in units]
            kn, vs, beta = _heads(kn_ref), _heads(vs_ref), _heads(beta_ref)
            cm = {}
            yield from _pair_common_stages(cm, _heads(qn_ref), kn, vs, beta, _heads(g_ref))
            tmv = _heads(t_ref)
            duv, dwv, dattv, dqdv, dkdv = kept(du_s), kept(dw_s), kept(datt_s), kept(dqd_s), kept(dkd_s)
            duw = _each(_side, duv, dwv)
            both = _each(_bdot_tn, tmv, duw)
            dvb, dkbg = [v[:, :HEAD] for v in both], [v[:, HEAD:] for v in both]
            dt = _each(lambda a, b, c: _bdot_nt(a, _side(b, c)), duw, cm["vb"], cm["kbg"])
            yield
            m1 = _each(_bdot_tn, tmv, dt)
            yield
            da = _each(lambda a, b: -jnp.where(cm["strict"], _bdot_nt(a, b), 0.0), m1, tmv)
            yield
            dkk = _each(lambda a, b: a * b, da, cm["decay"])
            dqk = _each(lambda a, b: a * b, dattv, cm["decay"])
            dd = _each(lambda a, b, c, d: a * b + c * d, dkk, cm["kk"], dqk, cm["qk"])
            dkq = _each(_stack, dkk, dqk)
            both = _each(_bdot, dkq, kn)
            dkb = _each(lambda a, c, d: a[:PAIR] + c * d, both, dkbg, cm["egc"])
            dq = _each(lambda a, c, d: a[PAIR:] + c * d, both, dqdv, cm["egc"])
            yield
            dkn = _each(lambda a, b, c: _bdot_tn(a, _stack(b, c)), dkq, cm["kb"], cm["q"])
            dkn = _each(lambda a, b, c, d, e: a + b * c + d * e, dkn, dkdv, cm["ekd"], dkb, beta)
            t_kd = _each(lambda a, b, c: _rowsum(a * b * c), dkdv, kn, cm["ekd"])
            yield
            split = _each(_split, dd)
            rows_dd = [jnp.dot(_side(hi, lo), ones, preferred_element_type=F32) for hi, lo in split]
            cols_dd = [lax.dot_general(_stack(hi, lo), ones, tn, preferred_element_type=F32) for hi, lo in split]
            yield
            dgc = _each(lambda r, c, a, b, e, f, k, tk: r - c + _rowsum(a * b * e) + _rowsum(f * k) - tk,
                        rows_dd, cols_dd, dqdv, cm["q"], cm["egc"], dkbg, cm["kbg"], t_kd)
            same_b = cm["same"].astype(BF16)
            rowi = lax.broadcasted_iota(I32, (PAIR, HEAD), 0)
            dcd = _each(lambda d: jnp.where(rowi < CHUNK, d[0:1], d[8:9]), kept(dcd_s, rows=16))
            dgl = _each(lambda tk, d, c: _mask_dot(same_b, jnp.broadcast_to(tk, (PAIR, HEAD))) + d * c, t_kd, dcd, cm["cd"])
            yield
            is_last = jnp.bitwise_and(rowi, CHUNK - 1) == CHUNK - 1
            dgc = _each(lambda a, b: a + jnp.where(is_last, b, 0.0), dgc, dgl)
            r = lax.broadcasted_iota(I32, (PAIR, PAIR), 0)
            c = lax.broadcasted_iota(I32, (PAIR, PAIR), 1)
            upper_b = (cm["same"] & (r <= c)).astype(BF16)
            _put_heads(dg_ref, _each(lambda v: _mask_dot(upper_b, v), dgc))
            yield
            _put_heads(dbeta_ref, _each(lambda a, b, c, d: jnp.broadcast_to(_rowsum(a * b) + _rowsum(c * d), (PAIR, HEAD)),
                                        dkb, kn, dvb, vs))
            _put_heads(dqn_ref, _each(lambda v: v * QK_SCALE, dq))
            _put_heads(dkn_ref, dkn)
            _put_heads(dvs_ref, _each(lambda a, b: a * b, dvb, beta))
            yield

        _pipelined_step(t, n_steps, lambda: [recurrence()],
                        lambda: [factors(UNITS[pp * N_HEADS:(pp + 1) * N_HEADS]) for pp in range(INTRA_PAIRS)])

    last = n_steps - 1
    now = lambda i: (jnp.maximum(last - i, 0), 0)
    after = lambda i: (jnp.minimum(n_steps - i, last), 0)
    rows = lambda index: pl.BlockSpec((SCAN_ROWS, D_HALF), index)
    slot = lambda r, dtype: pltpu.VMEM((2, r, D_HALF), dtype)
    return _call(
        body, name="delta_bwd", grid=(n_steps + 1,),
        in_specs=[rows(now)] * 6 + [_chunk_scalar_spec(SCAN_PAIRS, now),
                                    pl.BlockSpec((2 * SCAN_PAIRS, N_HEADS, HEAD, HEAD), lambda i: (jnp.maximum(last - i, 0), 0, 0, 0))]
                 + [rows(after)] * 6,
        out_specs=[rows(after)] * 5,
        out_shape=[_sds((s, D_HALF))] * 5,
        scratch_shapes=[pltpu.VMEM((N_HEADS, HEAD, HEAD), F32), slot(SCAN_ROWS, BF16), slot(SCAN_ROWS, BF16),
                        slot(SCAN_ROWS, F32), slot(SCAN_ROWS, F32), slot(SCAN_ROWS, F32), slot(16 * SCAN_PAIRS, F32)],
        compiler_params=_params("arbitrary"),
    )(do, vn, qd, kd, w, att, cd, st, qn, kn, vs, beta, g, tm)


def _fused_call(name, n_steps, parts):
    n_in = [len(p["inputs"]) for p in parts]
    n_out = [len(p["out_shape"]) for p in parts]
    n_scr = [len(p["scratch"]) for p in parts]

    def body(*refs):
        ins, outs, scr = refs[:sum(n_in)], refs[sum(n_in):sum(n_in) + sum(n_out)], refs[sum(n_in) + sum(n_out):]
        gens, a, b, c = [], 0, 0, 0
        for p, ni, no, ns in zip(parts, n_in, n_out, n_scr):
            gens.append(p["stages"](ins[a:a + ni], outs[b:b + no], scr[c:c + ns]))
            a, b, c = a + ni, b + no, c + ns
        _interleave(*gens)

    flat = lambda key: [v for p in parts for v in p[key]]
    res = _call(
        body, name=name, grid=(n_steps,),
        in_specs=flat("in_specs"), out_specs=flat("out_specs"), out_shape=flat("out_shape"),
        scratch_shapes=flat("scratch"),
        compiler_params=_params("arbitrary"),
    )(*flat("inputs"))
    out, b = [], 0
    for no in n_out:
        out.append(res[b:b + no])
        b += no
    return out


def _pool_bwd_part(proj, dyp, pool_w, pool_scale, tile_of, n_tiles):
    s = proj.shape[0]
    t = POOL_T
    hb = t // HEAD
    last = s // HEAD - 1

    def stages(ins, outs, scratch):
        u_ref, z_ref, halo_ref, dy_ref, zn_ref, dyn_ref, pw_ref, ps_ref, band_ref, aband_ref = ins
        du_ref, dz_ref, gpw_ref, gps_ref = outs
        tile = tile_of(pl.program_id(0))

        @pl.when(pl.program_id(0) == 0)
        def _():
            gpw_ref[...] = jnp.zeros_like(gpw_ref)
            gps_ref[...] = jnp.zeros_like(gps_ref)

        live = (tile > 0).astype(F32)
        more = (tile < n_tiles - 1).astype(F32)
        groups = lambda ref: [ref[:, sl] for sl in HEAD_COLS]
        z, ps, dy = groups(z_ref), groups(ps_ref), groups(dy_ref)
        pw = [pw_ref[g] for g in HEADS]
        mix, mixed, sg, cnt = _pool_mix(groups(u_ref), [h * live for h in groups(halo_ref)], z, pw,
                                        [band_ref[g] for g in HEADS], tile * t)
        yield
        sz = _each(lambda a, b: a * b, z, sg)
        for sl, d, m, p, s_, zg in zip(HEAD_COLS, dy, mixed, ps, sg, z):
            dz_ref[:, sl] = (d * m * p * (s_ * (1.0 + zg * (1.0 - s_)))).astype(BF16)
        for sl, d, m, a in zip(HEAD_COLS, dy, mixed, sz):
            gps_ref[:, sl] += _colsum(d * m * a)
        dmixed = _each(lambda d, p, a: d * p * a, dy, ps, sz)
        yield
        for g, gp in enumerate(_each(_bdot_tn, mix, dmixed)):
            gpw_ref[g] += gp
        dmix = _each(_bdot_nt, dmixed, pw)
        yield
        dmix_n = _each(lambda d, p, zn, w_: _bdot_nt(d * more * p * (zn * _sigmoid(zn)), w_),
                       groups(dyn_ref), ps, groups(zn_ref), pw)
        yield
        scaled = [jnp.concatenate([a / c, b * (1.0 / w)], axis=0) for a, c, b, w in zip(dmix, cnt, dmix_n, WINDOWS)]
        du = _each(lambda b, s_, d: _mask_dot(b, s_) - d, [aband_ref[g] for g in HEADS], scaled, dmix)
        for sl, v in zip(HEAD_COLS, du):
            du_ref[:, sl] = v.astype(BF16)
        yield

    tile = lambda col: pl.BlockSpec((t, D_HALF), lambda i: (tile_of(i), col))
    below = lambda col: pl.BlockSpec((HEAD, D_HALF), lambda i: (jnp.minimum((tile_of(i) + 1) * hb, last), col))
    const3 = lambda shape: pl.BlockSpec(shape, lambda i: (0, 0, 0))
    return dict(
        inputs=[proj, proj, proj, dyp, proj, dyp, pool_w, pool_scale, _pool_bands(t), _pool_bands(t, anti=True)],
        in_specs=[tile(0), tile(1), pl.BlockSpec((HEAD, D_HALF), lambda i: (jnp.maximum(tile_of(i) * hb - 1, 0), 0)),
                  tile(0), below(1), below(0), const3((N_HEADS, HEAD, HEAD)), pl.BlockSpec((1, D_HALF), lambda i: (0, 0)),
                  const3((N_HEADS, t, HEAD + t)), const3((N_HEADS, t, HEAD + t))],
        out_specs=[tile(0), tile(0), const3((N_HEADS, HEAD, HEAD)), pl.BlockSpec((1, D_HALF), lambda i: (0, 0))],
        out_shape=[_sds((s, D_HALF), BF16), _sds((s, D_HALF), BF16), _sds((N_HEADS, HEAD, HEAD)), _sds((1, D_HALF))],
        scratch=[], stages=stages)


def _conv_bwd_part(proj, pre, conv_w, a_log, dt_bias, dqn, dkn, dvs, dbeta, dg, tile_of, n_tiles):
    s = proj.shape[0]
    t = CONV_T

    def stages(ins, outs, scratch):
        (q_ref, k_ref, v_ref, yq_ref, yk_ref, yv_ref, ba_ref, cw_ref, al_ref, dtb_ref,
         dqn_ref, dkn_ref, dvs_ref, dbeta_ref, dg_ref) = ins
        oq_ref, ok_ref, ov_ref, dba_ref, gcw_out, gsm_out = outs
        below, gcw_ref, gsm_ref = scratch
        step = pl.program_id(0)

        @pl.when(step == 0)
        def _():
            gcw_ref[...] = jnp.zeros_like(gcw_ref)
            gsm_ref[...] = jnp.zeros_like(gsm_ref)
            below[...] = jnp.zeros_like(below)

        parts = ((q_ref, yq_ref, dqn_ref, oq_ref), (k_ref, yk_ref, dkn_ref, ok_ref), (v_ref, yv_ref, dvs_ref, ov_ref))
        for p, (x_ref, y_ref, d_ref, o_ref) in enumerate(parts):
            for h in HEADS:
                cs = HEAD_COLS[h]
                wide = slice(p * D_HALF + h * HEAD, p * D_HALF + (h + 1) * HEAD)
                cw = cw_ref[:, wide]
                y = y_ref[:, cs]
                sg = _sigmoid(y)
                sv = y * sg
                ds = d_ref[:, cs]
                if p < 2:
                    rn = lax.rsqrt(_rowsum(sv * sv) + EPS)
                    nrm = sv * rn
                    ds = rn * (ds - nrm * _rowsum(ds * nrm))
                dy = ds * (sg * (1.0 + y * (1.0 - sg)))
                nxt = below[:, wide]
                ahead = [dy] + [_shift_up(dy, nxt, sft) for sft in range(1, CONV_K)]
                xv = x_ref[:, cs]
                acc = dy * cw[CONV_K - 1:CONV_K]
                for sft in range(1, CONV_K):
                    acc = acc + ahead[sft] * cw[CONV_K - 1 - sft:CONV_K - sft]
                for j in range(CONV_K):
                    gcw_ref[8 * j:8 * j + 8, wide] += _rows8(xv * ahead[CONV_K - 1 - j])
                o_ref[:, cs] = acc.astype(BF16)
                below[:, wide] = dy[0:8]
                yield

        ba = ba_ref[...]
        lane = lax.broadcasted_iota(I32, (t, HEAD), 1)
        lane8 = lax.broadcasted_iota(I32, (8, HEAD), 1)
        dba = jnp.zeros((t, HEAD), F32)
        gsm = jnp.zeros((8, HEAD), F32)
        for h in HEADS:
            beta = _sigmoid(ba[:, h:h + 1])
            dbeta = dbeta_ref[:, h * HEAD:h * HEAD + 1]
            xg = ba[:, N_HEADS + h:N_HEADS + h + 1] + dtb_ref[0:1, h:h + 1]
            nexp = -jnp.exp(al_ref[0:1, h:h + 1])
            dgv = dg_ref[:, h * HEAD:h * HEAD + 1]
            da = dgv * nexp * _sigmoid(xg)
            dba = dba + jnp.where(lane == h, dbeta * beta * (1.0 - beta), 0.0) + jnp.where(lane == N_HEADS + h, da, 0.0)
            gsm = (gsm + jnp.where(lane8 == h, _rows8(dgv * nexp * _softplus(xg)), 0.0)
                   + jnp.where(lane8 == N_HEADS + h, _rows8(da), 0.0))
        dba_ref[...] = jnp.zeros_like(dba_ref)
        dba_ref[:, :HEAD] = dba.astype(BF16)
        gsm_ref[...] += gsm
        yield

        @pl.when(step == n_tiles - 1)
        def _():
            gcw_out[...] = jnp.zeros_like(gcw_out)
            for j in range(CONV_K):
                gcw_out[j:j + 1, :] = _colsum(gcw_ref[8 * j:8 * j + 8, :])
            gsm_out[...] = jnp.broadcast_to(_colsum(gsm_ref[...]), (8, HEAD))

    row = pl.BlockSpec((t, D_HALF), lambda i: (tile_of(i), 0))
    const = lambda shape: pl.BlockSpec(shape, lambda i: (0, 0))
    return dict(
        inputs=[proj] * 3 + list(pre) + [proj, conv_w, a_log, dt_bias, dqn, dkn, dvs, dbeta, dg],
        in_specs=_conv_specs(t, tile_of)[:3] + [row] * 3
                 + [pl.BlockSpec((t, HEAD), lambda i: (tile_of(i), COL_BA // HEAD)),
                    const((CONV_K, 3 * D_HALF)), const((1, N_HEADS)), const((1, N_HEADS))] + [row] * 5,
        out_specs=[row, row, row, row, const((8, 3 * D_HALF)), const((8, HEAD))],
        out_shape=[_sds((s, D_HALF), BF16)] * 4 + [_sds((8, 3 * D_HALF)), _sds((8, HEAD))],
        scratch=[pltpu.VMEM((8, 3 * D_HALF), F32), pltpu.VMEM((8 * CONV_K, 3 * D_HALF), F32), pltpu.VMEM((8, HEAD), F32)],
        stages=stages)


def _pool_fwd_part(proj, pool_w, pool_scale):
    s = proj.shape[0]
    t = POOL_T
    hb = t // HEAD

    def stages(ins, outs, scratch, tile=None):
        u_ref, z_ref, halo_ref, pw_ref, ps_ref, band_ref = ins
        y_ref, = outs
        i = pl.program_id(0) if tile is None else tile
        live = (i > 0).astype(F32)
        groups = lambda ref: [ref[:, sl] for sl in HEAD_COLS]
        z = groups(z_ref)
        u, halo = groups(u_ref), [h * live for h in groups(halo_ref)]
        yield
        _, mixed, sg, _ = _pool_mix(u, halo, z, [pw_ref[g] for g in HEADS], [band_ref[g] for g in HEADS], i * t)
        yield
        for sl, m, zg, s_ in zip(HEAD_COLS, mixed, z, sg):
            y_ref[:, sl] = m * ps_ref[:, sl] * (zg * s_)
        yield

    const3 = lambda shape: pl.BlockSpec(shape, lambda i: (0, 0, 0))
    return dict(
        inputs=[proj, proj, proj, pool_w, pool_scale, _pool_bands(t)],
        in_specs=[pl.BlockSpec((t, D_HALF), lambda i: (i, 0)), pl.BlockSpec((t, D_HALF), lambda i: (i, 1)),
                  pl.BlockSpec((HEAD, D_HALF), lambda i: (jnp.maximum(i * hb - 1, 0), 0)),
                  const3((N_HEADS, HEAD, HEAD)), pl.BlockSpec((1, D_HALF), lambda i: (0, 0)), const3((N_HEADS, t, HEAD + t))],
        out_specs=[pl.BlockSpec((t, D_HALF), lambda i: (i, 0))], out_shape=[_sds((s, D_HALF))],
        scratch=[], stages=stages)


def _conv_fwd_part(proj, conv_w, a_log, dt_bias):
    s = proj.shape[0]
    t = CONV_T

    def stages(ins, outs, scratch, tile=None):
        q_ref, k_ref, v_ref, hq_ref, hk_ref, hv_ref, ba_ref, cw_ref, al_ref, dtb_ref = ins
        qn_ref, kn_ref, vs_ref, beta_ref, g_ref, yq_ref, yk_ref, yv_ref = outs
        live = ((pl.program_id(0) if tile is None else tile) > 0).astype(F32)
        parts = ((q_ref, hq_ref, qn_ref, yq_ref), (k_ref, hk_ref, kn_ref, yk_ref), (v_ref, hv_ref, vs_ref, yv_ref))
        for p, (x_ref, h_ref, o_ref, y_ref) in enumerate(parts):
            for h in HEADS:
                cs = HEAD_COLS[h]
                taps = _conv_taps(x_ref[:, cs], h_ref[:, cs] * live)
                y = _conv_pre(taps, cw_ref[:, p * D_HALF + h * HEAD:p * D_HALF + (h + 1) * HEAD])
                y_ref[:, cs] = y
                sv = y * _sigmoid(y)
                o_ref[:, cs] = sv if p == 2 else sv * lax.rsqrt(_rowsum(sv * sv) + EPS)
                yield
        ba = ba_ref[...]
        for h in HEADS:
            beta = _sigmoid(ba[:, h:h + 1])
            gl = -jnp.exp(al_ref[0:1, h:h + 1]) * _softplus(ba[:, N_HEADS + h:N_HEADS + h + 1] + dtb_ref[0:1, h:h + 1])
            beta_ref[:, HEAD_COLS[h]] = jnp.broadcast_to(beta, (t, HEAD))
            g_ref[:, HEAD_COLS[h]] = jnp.broadcast_to(gl, (t, HEAD))
        yield

    row = pl.BlockSpec((t, D_HALF), lambda i: (i, 0))
    const = lambda shape: pl.BlockSpec(shape, lambda i: (0, 0))
    return dict(
        inputs=[proj] * 7 + [conv_w, a_log, dt_bias],
        in_specs=_conv_specs(t) + [pl.BlockSpec((t, HEAD), lambda i: (i, COL_BA // HEAD)),
                                   const((CONV_K, 3 * D_HALF)), const((1, N_HEADS)), const((1, N_HEADS))],
        out_specs=[row] * 8, out_shape=[_sds((s, D_HALF))] * 8, scratch=[], stages=stages)


def _front_fwd(x, norm_w, w_pad, conv_w, a_log, dt_bias, pool_w, pool_scale, after):
    s = x.shape[0]
    t = CONV_T
    n_tiles = s // t
    assert POOL_T == CONV_T
    like_proj = _sds((s, N_IN_PAD))
    conv = _conv_fwd_part(like_proj, conv_w, a_log, dt_bias)
    pool = _pool_fwd_part(like_proj, pool_w, pool_scale)
    bands = pool["inputs"][-1]
    mxu_n = 256
    col_bounds = list(range(0, N_IN_PAD, 3 * mxu_n)) + [N_IN_PAD]

    def body(x_ref, nw_ref, w_ref, cw_ref, al_ref, dtb_ref, pw_ref, ps_ref, band_ref, after_ref,
             proj_ref, nt_ref, qn_ref, kn_ref, vs_ref, beta_ref, g_ref, yq_ref, yk_ref, yv_ref, y_ref, prev):
        del after_ref
        i = pl.program_id(0)

        @pl.when(i == 0)
        def _():
            prev[...] = jnp.zeros_like(prev)

        tile = jnp.maximum(i - 1, 0)
        main, above8, above = pl.ds(HEAD, t), pl.ds(HEAD - 8, 8), pl.ds(0, HEAD)
        cols = lambda rows, c0, width=D_HALF: prev.at[rows, pl.ds(c0, width)]
        conv_ins = (cols(main, 2 * D_HALF), cols(main, 3 * D_HALF), cols(main, 4 * D_HALF),
                    cols(above8, 2 * D_HALF), cols(above8, 3 * D_HALF), cols(above8, 4 * D_HALF),
                    cols(main, COL_BA, HEAD), cw_ref, al_ref, dtb_ref)
        pool_ins = (cols(main, 0), cols(main, D_HALF), cols(above, 0), pw_ref, ps_ref, band_ref)

        def projection():
            xv = x_ref[...]
            r = lax.rsqrt(jnp.mean(xv * xv, axis=-1, keepdims=True) + EPS)
            nv = xv * r * nw_ref[...]
            nt_ref[...] = nv.T.astype(BF16)
            nb = nv.astype(BF16)
            yield
            for lo, hi in zip(col_bounds[:-1], col_bounds[1:]):
                proj_ref[:, lo:hi] = jnp.dot(nb, w_ref[:, lo:hi], preferred_element_type=F32)
                yield

        _interleave(projection(),
                    conv["stages"](conv_ins, (qn_ref, kn_ref, vs_ref, beta_ref, g_ref, yq_ref, yk_ref, yv_ref), (), tile),
                    pool["stages"](pool_ins, (y_ref,), (), tile))
        prev[0:HEAD] = prev[t:t + HEAD]
        prev[HEAD:HEAD + t] = proj_ref[...]

    last = n_tiles - 1
    now = lambda i: (jnp.minimum(i, last), 0)
    before = lambda i: (jnp.maximum(i - 1, 0), 0)
    const = lambda a: pl.BlockSpec(a.shape, lambda i: (0,) * a.ndim)
    half = pl.BlockSpec((t, D_HALF), before)
    return _call(
        body, name="front_fwd", grid=(n_tiles + 1,),
        in_specs=[pl.BlockSpec((t, D_MODEL), now), const(norm_w), const(w_pad), const(conv_w), const(a_log), const(dt_bias),
                  const(pool_w), const(pool_scale), const(bands), pl.BlockSpec(memory_space=pl.ANY)],
        out_specs=[pl.BlockSpec((t, N_IN_PAD), now), pl.BlockSpec((D_MODEL, t), lambda i: (0, jnp.minimum(i, last)))]
                  + [half] * 9,
        out_shape=[_sds((s, N_IN_PAD)), _sds((D_MODEL, s), BF16)] + [_sds((s, D_HALF))] * 9,
        scratch_shapes=[pltpu.VMEM((HEAD + t, N_IN_PAD), F32)],
        compiler_params=_params("arbitrary"),
    )(x, norm_w, w_pad, conv_w, a_log, dt_bias, pool_w, pool_scale, bands, after)


def _conv_pool_bwd(proj, pre, conv_w, a_log, dt_bias, dqn, dkn, dvs, dbeta, dg, dyp, pool_w, pool_scale):
    n_tiles = proj.shape[0] // CONV_T
    assert POOL_T == CONV_T
    tile_of = lambda i: n_tiles - 1 - i
    return _fused_call("conv_pool_bwd", n_tiles, [
        _conv_bwd_part(proj, pre, conv_w, a_log, dt_bias, dqn, dkn, dvs, dbeta, dg, tile_of, n_tiles),
        _pool_bwd_part(proj, dyp, pool_w, pool_scale, tile_of, n_tiles)])


def _rows8(x):
    acc = x[0:8]
    for r in range(8, x.shape[0], 8):
        acc = acc + x[r:r + 8]
    return acc


IN_T = 512


def _in_bwd(x, dh, norm_w, w_pad, pieces, after):
    s = x.shape[0]
    t = IN_T
    n_steps = s // t
    widths = [D_HALF] * 6 + [N_IN_PAD - COL_BA]
    starts = [sum(widths[:k]) for k in range(len(widths))]

    def body(*refs):
        x_hbm, dh_hbm, nw_ref, w_hbm = refs[:4]
        p_refs = refs[4:4 + len(pieces)]
        gx_ref, gnw_ref, w_ref, arrived, ring, ring_sems = refs[5 + len(pieces):]
        i = pl.program_id(0)
        first = i == 0

        def tiles(step):
            rows, slot = pl.ds(pl.multiple_of(step * t, t), t), lax.rem(step, 3)
            return [pltpu.make_async_copy(src.at[rows], ring.at[a, slot], ring_sems.at[a, slot])
                    for a, src in enumerate((x_hbm, dh_hbm))]
        cols = [pl.ds(c, wd) for c, wd in zip(starts, widths)]
        copies = [pltpu.make_async_copy(w_hbm.at[:, cs], w_ref.at[:, cs], arrived.at[k]) for k, cs in enumerate(cols)]

        @pl.when(first)
        def _():
            for cp in tiles(0) + (tiles(1) if n_steps > 1 else []) + copies:
                cp.start()
            gnw_ref[...] = jnp.zeros_like(gnw_ref)

        @pl.when(i + 2 < n_steps)
        def _():
            for cp in tiles(i + 2):
                cp.start()

        def step(wait):
            dn = jnp.zeros((t, D_MODEL), F32)
            for k, (p_ref, c, wd) in enumerate(zip(p_refs, starts, widths)):
                if wait:
                    copies[k].wait()
                dn = dn + _bdot_nt(p_ref[...], w_ref[:, c:c + wd])
            for cp in tiles(i):
                cp.wait()
            slot = lax.rem(i, 3)
            xv = ring[0, slot]
            r = lax.rsqrt(jnp.mean(xv * xv, axis=-1, keepdims=True) + EPS)
            xhat = xv * r
            gnw_ref[...] += _colsum(dn * xhat)
            dxh = dn * nw_ref[...]
            gx_ref[...] = ring[1, slot] + r * (dxh - xhat * jnp.mean(dxh * xhat, axis=-1, keepdims=True))

        pl.when(first)(functools.partial(step, True))
        pl.when(jnp.logical_not(first))(functools.partial(step, False))

    wide = pl.BlockSpec((t, D_MODEL), lambda i: (i, 0))
    hbm = pl.BlockSpec(memory_space=pl.ANY)
    return _call(
        body, name="in_bwd", grid=(n_steps,),
        in_specs=[hbm, hbm, pl.BlockSpec((1, D_MODEL), lambda i: (0, 0)), hbm]
                 + [pl.BlockSpec((t, wd), lambda i: (i, 0)) for wd in widths] + [hbm],
        out_specs=[wide, pl.BlockSpec((1, D_MODEL), lambda i: (0, 0))],
        out_shape=[_sds((s, D_MODEL)), _sds((1, D_MODEL))],
        scratch_shapes=[pltpu.VMEM((D_MODEL, N_IN_PAD), BF16), pltpu.SemaphoreType.DMA((len(widths),)),
                        pltpu.VMEM((2, 3, t, D_MODEL), F32), pltpu.SemaphoreType.DMA((2, 3))],
        compiler_params=_params("arbitrary"),
    )(x, dh, norm_w, w_pad, *pieces, after)


def _adamw_shard(name, w, g_own, g_got, cidx, m, v):
    _, r, c = w.shape
    half = r // 2
    rows = 256 if half % 256 == 0 else half
    per_half = half // rows

    def body(c_ref, w_ref, go_ref, gg_ref, m_ref, v_ref, gout_ref, d_ref, nm_ref, nv_ref):
        mine = (pl.program_id(0) // per_half) == c_ref[0]
        gv = jnp.where(mine, go_ref[:, :c], gg_ref[:, :c])
        gout_ref[0] = gv
        mn = ADAM_B1 * m_ref[0] + (1.0 - ADAM_B1) * gv
        vn = ADAM_B2 * v_ref[0] + (1.0 - ADAM_B2) * (gv * gv)
        m_hat = mn / (1.0 - ADAM_B1 ** ADAM_STEP)
        v_hat = vn / (1.0 - ADAM_B2 ** ADAM_STEP)
        d_ref[0] = -ADAM_LR * (m_hat / (jnp.sqrt(v_hat) + ADAM_EPS) + ADAM_WD * w_ref[0])
        nm_ref[0] = mn
        nv_ref[0] = vn

    blk = pl.BlockSpec((1, rows, c), lambda i, c_ref: (0, i, 0))
    gblk = pl.BlockSpec((rows, g_own.shape[1]), lambda i, c_ref: (i % per_half, 0))
    return _call(
        body, name=name,
        grid_spec=pltpu.PrefetchScalarGridSpec(
            num_scalar_prefetch=1, grid=(2 * per_half,),
            in_specs=[blk, gblk, gblk, blk, blk], out_specs=[blk] * 4),
        out_shape=[_sds((1, r, c))] * 4,
        compiler_params=_params("arbitrary"),
    )(cidx, w, g_own, g_got, m, v)


def _adamw_tiles(name, w, g, m, v):
    n = w.shape[0]
    nb = 77 if n % 77 == 0 else n

    def body(w_ref, g_ref, m_ref, v_ref, d_ref, nm_ref, nv_ref):
        gv = g_ref[...]
        mn = ADAM_B1 * m_ref[...] + (1.0 - ADAM_B1) * gv
        vn = ADAM_B2 * v_ref[...] + (1.0 - ADAM_B2) * (gv * gv)
        m_hat = mn / (1.0 - ADAM_B1 ** ADAM_STEP)
        v_hat = vn / (1.0 - ADAM_B2 ** ADAM_STEP)
        d_ref[...] = -ADAM_LR * (m_hat / (jnp.sqrt(v_hat) + ADAM_EPS) + ADAM_WD * w_ref[...])
        nm_ref[...] = mn
        nv_ref[...] = vn

    blk = pl.BlockSpec((nb, 8, HEAD), lambda i: (i, 0, 0))
    return _call(
        body, name=name, grid=(n // nb,),
        in_specs=[blk] * 4, out_specs=[blk] * 3, out_shape=[_sds(w.shape)] * 3,
        compiler_params=_params("arbitrary"),
    )(w, g, m, v)


def _make_copy(src, dst, send, recv, target):
    if target is None:
        return pltpu.make_async_copy(src, dst, recv)
    return pltpu.make_async_remote_copy(src_ref=src, dst_ref=dst, send_sem=send, recv_sem=recv,
                                        device_id=target, device_id_type=pl.DeviceIdType.MESH)


def _exchange(name, inputs, out_shapes, phases):
    n_in = len(inputs)
    n_out = len(out_shapes)
    n_cp = sum(len(p) for p in phases)

    def body(*refs):
        ins, outs = refs[:n_in], refs[n_in:n_in + n_out]
        send, recv = refs[n_in + n_out:]
        pos = (lax.axis_index("x"), lax.axis_index("y"), lax.axis_index("c"))
        k = 0
        for phase in phases:
            cps = []
            for src, dst, target in phase:
                cps.append(_make_copy(src(ins, outs, pos), dst(ins, outs, pos), send.at[k], recv.at[k],
                                      target and target(pos)))
                k += 1
            for cp in cps:
                cp.start()
            for cp in cps:
                cp.wait()

    anyspec = pl.BlockSpec(memory_space=pl.ANY)
    return _call(
        body, name=name,
        in_specs=[anyspec] * n_in, out_specs=[anyspec] * n_out, out_shape=list(out_shapes),
        scratch_shapes=[pltpu.SemaphoreType.DMA((n_cp,)), pltpu.SemaphoreType.DMA((n_cp,))],
    )(*inputs)


def _exchange_start(name, inputs, out_shapes, copies):
    n_in, n_out, n_cp = len(inputs), len(out_shapes), len(copies)

    def body(*refs):
        ins, lands = refs[:n_in], refs[n_in:n_in + n_out]
        sems = refs[n_in + n_out:n_in + n_out + 2 * n_cp]
        token = refs[-1]
        pos = (lax.axis_index("x"), lax.axis_index("y"), lax.axis_index("c"))
        for k, (src, dst, target) in enumerate(copies):
            _make_copy(src(ins, lands, pos), dst(ins, lands, pos), sems[2 * k], sems[2 * k + 1],
                       target and target(pos)).start()
        token[...] = jnp.zeros_like(token)

    hbm = pl.BlockSpec(memory_space=pltpu.HBM)
    sem = pl.BlockSpec(memory_space=pltpu.SEMAPHORE)
    bufs = list(inputs) + [lax.empty(o.shape, o.dtype) for o in out_shapes]
    outs = _call(
        body, name=name,
        out_shape=tuple([pltpu.SemaphoreType.DMA(())] * (2 * n_cp) + [pltpu.HBM(b.shape, b.dtype) for b in bufs]
                        + [_sds((8, HEAD))]),
        in_specs=[hbm] * len(bufs),
        out_specs=tuple([sem] * (2 * n_cp) + [hbm] * len(bufs) + [pl.BlockSpec(memory_space=pltpu.VMEM)]),
        input_output_aliases={i: 2 * n_cp + i for i in range(len(bufs))},
        compiler_params=pltpu.CompilerParams(has_side_effects=pltpu.SideEffectType.DATAFLOW_SIDE_EFFECTING),
    )(*[pltpu.with_memory_space_constraint(b, pltpu.HBM) for b in bufs])
    return outs[:2 * n_cp], outs[2 * n_cp:2 * n_cp + n_in], outs[2 * n_cp + n_in:-1], outs[-1]


def _exchange_wait(name, sems, sources, lands, copies, after):
    n_in, n_out, n_cp = len(sources), len(lands), len(copies)

    def body(*refs):
        ins, zones = refs[:n_in], refs[n_in:n_in + n_out]
        sem_refs = refs[n_in + n_out:n_in + n_out + 2 * n_cp]
        pos = (lax.axis_index("x"), lax.axis_index("y"), lax.axis_index("c"))
        for k, (src, dst, target) in enumerate(copies):
            cp = _make_copy(src(ins, zones, pos), dst(ins, zones, pos), sem_refs[2 * k], sem_refs[2 * k + 1],
                            target and target(pos))
            if target is None:
                cp.wait()
            else:
                cp.wait_send()
                cp.wait_recv()

    hbm = pl.BlockSpec(memory_space=pltpu.HBM)
    sem = pl.BlockSpec(memory_space=pltpu.SEMAPHORE)
    bufs = list(sources) + list(lands)
    outs = _call(
        body, name=name,
        out_shape=tuple(pltpu.HBM(b.shape, b.dtype) for b in bufs),
        in_specs=[hbm] * len(bufs) + [sem] * (2 * n_cp) + [pl.BlockSpec(memory_space=pl.ANY)],
        out_specs=tuple([hbm] * len(bufs)),
        input_output_aliases={i: i for i in range(len(bufs))},
        compiler_params=pltpu.CompilerParams(has_side_effects=pltpu.SideEffectType.DATAFLOW_SIDE_EFFECTING),
    )(*bufs, *sems, after)
    return outs[:n_in], outs[n_in:]


def _allreduce_tile(name, v):
    def body(v_ref, out_ref, slots, send, recv):
        x, y, c = lax.axis_index("x"), lax.axis_index("y"), lax.axis_index("c")
        me = 4 * x + 2 * y + c
        slots[me] = v_ref[...]
        cps = []
        for k in range(1, 8):
            peer = (x ^ (k >> 2), y ^ ((k >> 1) & 1), c ^ (k & 1))
            cps.append(pltpu.make_async_remote_copy(
                src_ref=v_ref, dst_ref=slots.at[me], send_sem=send.at[k - 1], recv_sem=recv.at[k - 1],
                device_id=peer, device_id_type=pl.DeviceIdType.MESH))
        for cp in cps:
            cp.start()
        for cp in cps:
            cp.wait()
        acc = slots[0]
        for i in range(1, 8):
            acc = acc + slots[i]
        out_ref[...] = acc

    vm = pl.BlockSpec(memory_space=pltpu.VMEM)
    return _call(
        body, name=name, in_specs=[vm], out_specs=vm, out_shape=_sds(v.shape),
        scratch_shapes=[pltpu.VMEM((8,) + v.shape, F32), pltpu.SemaphoreType.DMA((7,)), pltpu.SemaphoreType.DMA((7,))],
    )(v)


def _chip(pos):
    return 2 * pos[0] + pos[1]


def _other_chip(pos, mask):
    x, y, c = pos
    return (x ^ (mask >> 1), y ^ (mask & 1), c)


def _sibling(pos):
    return (pos[0], pos[1], 1 - pos[2])


def _gather_weights(wb, cb):
    rows = wb.shape[0] // 2
    x_nb, y_nb, diag = CHIP_MASKS

    def part(pos, mask, quarter=None):
        start = pos[2] * rows if quarter is None else pos[2] * rows + quarter * (rows // 2)
        return lambda outs: outs[0].at[_chip(pos) ^ mask, pl.ds(start, rows if quarter is None else rows // 2)]

    def passed_on(mask, to, quarter=None):
        return (lambda ins, outs, pos: part(pos, mask, quarter)(outs), lambda ins, outs, pos: part(pos, mask, quarter)(outs), to)

    first = [(lambda ins, outs, pos: ins[0].at[pl.ds(pos[2] * rows, rows)], lambda ins, outs, pos: part(pos, 0)(outs),
              functools.partial(_other_chip, mask=mask)) for mask in (x_nb, y_nb)]
    conv_cols = lambda ins, outs, pos: outs[1].at[:, pl.ds(pl.multiple_of(_chip(pos) * cb.shape[1], HEAD), cb.shape[1])]
    first += [(lambda ins, outs, pos: ins[1], conv_cols, functools.partial(_other_chip, mask=mask)) for mask in CHIP_MASKS]
    first += [(lambda ins, outs, pos: ins[1], conv_cols, None)]
    second = [passed_on(x_nb, functools.partial(_other_chip, mask=y_nb), quarter=0),
              passed_on(y_nb, functools.partial(_other_chip, mask=x_nb), quarter=1),
              passed_on(x_nb, _sibling), passed_on(y_nb, _sibling)]
    third = [passed_on(diag, _sibling)]
    return _exchange("gather_weights", [wb, cb],
                     [_sds((4,) + wb.shape, wb.dtype), _sds((cb.shape[0], 4 * cb.shape[1]), cb.dtype)], [first, second, third])


def _assemble_w_in(gw, wb, jidx):
    m = gw.shape[1]

    def body(j_ref, g_ref, wb_ref, o_ref):
        step = pl.program_id(0)

        @pl.when(step == 0)
        def _():
            o_ref[...] = jnp.zeros_like(o_ref)

        blk = jnp.where(step == j_ref[0], wb_ref[...], g_ref[0]).astype(F32)
        lane = lax.broadcasted_iota(I32, (m, BLK_IN_PAD), 1)
        for j in range(4):
            @pl.when(step == j)
            def _(j=j):
                base = j * BLK_IN // HEAD * HEAD
                shift = j * BLK_IN - base
                moved = pltpu.roll(blk, shift, 1) if shift else blk
                window = o_ref[:, base:base + BLK_IN_PAD].astype(F32)
                mine = (lane >= shift) & (lane < shift + BLK_IN)
                o_ref[:, base:base + BLK_IN_PAD] = jnp.where(mine, moved, window).astype(BF16)

    return _call(
        body, name="assemble_w_in",
        grid_spec=pltpu.PrefetchScalarGridSpec(
            num_scalar_prefetch=1, grid=(4,),
            in_specs=[pl.BlockSpec((1, m, BLK_IN_PAD), lambda j, j_ref: (j, 0, 0)),
                      pl.BlockSpec((m, BLK_IN_PAD), lambda j, j_ref: (0, 0))],
            out_specs=pl.BlockSpec((m, N_IN_PAD), lambda j, j_ref: (0, 0))),
        out_shape=_sds((m, N_IN_PAD), BF16),
        compiler_params=_params("arbitrary"),
    )(jidx, gw, wb)


def _gather_blocks(ob):
    copies = [(lambda ins, outs, pos: ins[0], lambda ins, outs, pos: outs[0].at[_chip(pos)],
               functools.partial(_other_chip, mask=mask)) for mask in CHIP_MASKS]
    copies.append((lambda ins, outs, pos: ins[0], lambda ins, outs, pos: outs[0].at[_chip(pos)], None))
    return [_sds((4,) + ob.shape, ob.dtype)], copies


def _reduce_sibling(name, arrays):
    n = len(arrays)
    halves = [a.shape[:-2] + (a.shape[-2] // 2, a.shape[-1]) for a in arrays]
    pieces = [(a, j) for a in range(n) for j in (range(arrays[a].shape[0]) if arrays[a].ndim == 3 else [None])]

    def body(*refs):
        whole, outs = refs[:n], refs[n:2 * n]
        own, land, summed = refs[2 * n:3 * n], refs[3 * n:4 * n], refs[4 * n:5 * n]
        send, recv, loaded, stored = refs[5 * n:]
        pos = (lax.axis_index("x"), lax.axis_index("y"), lax.axis_index("c"))
        block = lambda ref, j, rows=None: ref.at[(() if j is None else (j,)) + (() if rows is None else (rows,))]
        arrive, load = [], []
        for k, (a, j) in enumerate(pieces):
            h = halves[a][-2]
            arrive.append(pltpu.make_async_remote_copy(
                src_ref=block(whole[a], j, pl.ds((1 - pos[2]) * h, h)), dst_ref=block(land[a], j),
                send_sem=send.at[k], recv_sem=recv.at[k],
                device_id=_sibling(pos), device_id_type=pl.DeviceIdType.MESH))
            load.append(pltpu.make_async_copy(block(whole[a], j, pl.ds(pos[2] * h, h)), block(own[a], j), loaded.at[k]))
        for cp in arrive + load:
            cp.start()
        store = []
        for k, (a, j) in enumerate(pieces):
            load[k].wait()
            arrive[k].wait()
            at = Ellipsis if j is None else j
            summed[a][at] = (own[a][at].astype(F32) + land[a][at].astype(F32)).astype(summed[a].dtype)
            store.append(pltpu.make_async_copy(block(summed[a], j), block(outs[a], j), stored.at[k]))
            store[-1].start()
        for cp in store:
            cp.wait()

    vmem = [pltpu.VMEM(h, a.dtype) for h, a in zip(halves, arrays)]
    sems = [pltpu.SemaphoreType.DMA((len(pieces),))] * 4
    return _call(
        body, name=name,
        in_specs=[pl.BlockSpec(memory_space=pl.ANY)] * n, out_specs=[pl.BlockSpec(memory_space=pl.ANY)] * n,
        out_shape=[_sds(h, a.dtype) for h, a in zip(halves, arrays)],
        scratch_shapes=vmem * 3 + sems,
        compiler_params=_params(),
    )(*arrays)


def _to_other_chips(arrays, blocked):
    def src(ins, outs, pos, a, mask):
        return ins[a].at[_chip(pos) ^ mask] if blocked[a] else ins[a]

    outs = [_sds((3,) + (a.shape[1:] if b else a.shape), a.dtype) for a, b in zip(arrays, blocked)]
    copies = []
    for mi, mask in enumerate(CHIP_MASKS):
        for a in range(len(arrays)):
            copies.append((functools.partial(src, a=a, mask=mask), lambda ins, outs, pos, a=a, mi=mi: outs[a].at[mi],
                           functools.partial(_other_chip, mask=mask)))
    return outs, copies


def _sum_chips_swap(name, owns, gots, blocked):
    n = len(owns)
    shapes = [g.shape[-2:] for g in gots]

    def body(*refs):
        own, got, mine, theirs = refs[:n], refs[n:2 * n], refs[2 * n:3 * n], refs[3 * n:4 * n]
        own_v, got_v, sum_v = refs[4 * n:5 * n], refs[5 * n:6 * n], refs[6 * n:7 * n]
        send, recv, loaded, stored = refs[7 * n:]
        pos = (lax.axis_index("x"), lax.axis_index("y"), lax.axis_index("c"))
        load = []
        for a in range(n):
            load.append((pltpu.make_async_copy(own[a].at[_chip(pos)] if blocked[a] else own[a], own_v[a], loaded.at[2 * a]),
                         pltpu.make_async_copy(got[a], got_v[a], loaded.at[2 * a + 1])))
        for pair in load:
            for cp in pair:
                cp.start()
        out = []
        for a in range(n):
            for cp in load[a]:
                cp.wait()
            sum_v[a][...] = ((own_v[a][...].astype(F32) + got_v[a][0].astype(F32))
                             + (got_v[a][1].astype(F32) + got_v[a][2].astype(F32)))
            out.append(pltpu.make_async_remote_copy(
                src_ref=sum_v[a], dst_ref=theirs[a], send_sem=send.at[a], recv_sem=recv.at[a],
                device_id=_sibling(pos), device_id_type=pl.DeviceIdType.MESH))
            out.append(pltpu.make_async_copy(sum_v[a], mine[a], stored.at[a]))
            out[-2].start()
            out[-1].start()
        for cp in out:
            cp.wait()

    outs = _call(
        body, name=name,
        in_specs=[pl.BlockSpec(memory_space=pl.ANY)] * (2 * n), out_specs=[pl.BlockSpec(memory_space=pl.ANY)] * (2 * n),
        out_shape=[_sds(s) for s in shapes] * 2,
        scratch_shapes=[pltpu.VMEM(s, o.dtype) for s, o in zip(shapes, owns)]
                       + [pltpu.VMEM((3,) + s, g.dtype) for s, g in zip(shapes, gots)] + [pltpu.VMEM(s, F32) for s in shapes]
                       + [pltpu.SemaphoreType.DMA((n,)), pltpu.SemaphoreType.DMA((n,)),
                          pltpu.SemaphoreType.DMA((2 * n,)), pltpu.SemaphoreType.DMA((n,))],
        compiler_params=_params(),
    )(*owns, *gots)
    return outs[:n], outs[n:]


def _local_step(x, target, w_pad, w_out, conv_w, norm_w, pool_w, pool_scale, a_log, dt_bias, dn_norm_w, final_norm_w,
                after):
    proj, n_t, qn, kn, vs, beta, g, yq, yk, yv, y_pool = _front_fwd(
        x, norm_w, w_pad, conv_w, a_log, dt_bias, pool_w, pool_scale, after)
    w, att, qd, kd, tm, cd, o, vn, st = _delta_fwd(qn, kn, vs, beta, g)
    w_out = w_out(o) if callable(w_out) else w_out
    g_wout, dh, dyp, do, ddz, loss, g_fnw, g_dnw = _out_fwd_bwd(x, y_pool, o, proj, target, w_out, dn_norm_w, final_norm_w)
    dqn, dkn, dvs, dbeta, dg = _delta_bwd(do, vn, qd, kd, w, att, cd, st, qn, kn, vs, beta, g, tm)
    (dcq, dck, dcv, dba, g_cw, g_sm), (dpu, dpz, g_pw, g_ps) = _conv_pool_bwd(
        proj, (yq, yk, yv), conv_w, a_log, dt_bias, dqn, dkn, dvs, dbeta, dg, dyp, pool_w, pool_scale)
    pieces = [dpu, dpz, dcq, dck, dcv, ddz, dba]
    small = dict(norm_w=jnp.zeros_like(norm_w), pool_w=g_pw, pool_scale=g_ps, conv_w=g_cw[:CONV_K],
                 a_log=g_sm[0:1, 0:N_HEADS], dt_bias=g_sm[0:1, N_HEADS:2 * N_HEADS], dn_norm_w=g_dnw, final_norm_w=g_fnw)
    return loss[0, 0], n_t, g_wout, small, dh, pieces


SMALL_LAYOUT = (("pool_w", 512, HEAD, (1, N_HEADS, HEAD, HEAD)), ("final_norm_w", 8, HEAD, (D_MODEL,)),
                ("pool_scale", 4, HEAD, (1, D_HALF)), ("conv_w", 48, HEAD, (1, CONV_K, 3 * D_HALF)),
                ("dn_norm_w", 1, HEAD, (1, HEAD)), ("a_log", 1, N_HEADS, (1, N_HEADS)), ("dt_bias", 1, N_HEADS, (1, N_HEADS)),
                ("loss", 1, 1, ()))


def _small_offsets():
    offs, r = {}, 0
    for name, rows, _, _ in SMALL_LAYOUT:
        offs[name] = r
        r += -(-rows // 8) * 8
    assert r <= SMALL_ROWS
    return offs


def _pack_small(t):
    parts = []
    for name, rows, lanes, _ in SMALL_LAYOUT:
        a = t.get(name, jnp.zeros((1,), F32)).reshape(rows, lanes)
        parts.append(jnp.pad(a, ((0, -(-rows // 8) * 8 - rows), (0, HEAD - lanes))))
    buf = jnp.concatenate(parts, axis=0)
    return jnp.pad(buf, ((0, SMALL_ROWS - buf.shape[0]), (0, 0)))


def _adamw_small(w, g_own, g_got, cidx, m, v):
    offs = _small_offsets()
    names = [e[0] for e in SMALL_LAYOUT]
    n = len(names)

    def body(c_ref, w_ref, go_ref, gg_ref, m_ref, v_ref, *outs):
        own_low = c_ref[0] == 0
        gv = jnp.concatenate([jnp.where(own_low, go_ref[...], gg_ref[...]), jnp.where(own_low, gg_ref[...], go_ref[...])], axis=0)
        mn = ADAM_B1 * m_ref[...] + (1.0 - ADAM_B1) * gv
        vn = ADAM_B2 * v_ref[...] + (1.0 - ADAM_B2) * (gv * gv)
        m_hat = mn / (1.0 - ADAM_B1 ** ADAM_STEP)
        v_hat = vn / (1.0 - ADAM_B2 ** ADAM_STEP)
        dl = -ADAM_LR * (m_hat / (jnp.sqrt(v_hat) + ADAM_EPS) + ADAM_WD * w_ref[...])
        for kind, arr in enumerate((gv, dl, mn, vn)):
            for i, (name, rows, lanes, _) in enumerate(SMALL_LAYOUT):
                outs[kind * n + i][...] = arr[offs[name]:offs[name] + rows, :lanes]

    whole = lambda shape: pl.BlockSpec(shape, lambda i, c_ref: (0,) * len(shape))
    out_shapes = [_sds((rows, lanes)) for _, rows, lanes, _ in SMALL_LAYOUT] * 4
    res = _call(
        body, name="adamw_small",
        grid_spec=pltpu.PrefetchScalarGridSpec(
            num_scalar_prefetch=1, grid=(1,),
            in_specs=[whole(w.shape), whole(g_own.shape), whole(g_got.shape), whole(m.shape), whole(v.shape)],
            out_specs=[whole(o.shape) for o in out_shapes]),
        out_shape=out_shapes,
        compiler_params=_params("arbitrary"),
    )(cidx, w, g_own, g_got, m, v)
    return [{name: res[kind * n + i].reshape(shape) for i, (name, _, _, shape) in enumerate(SMALL_LAYOUT)}
            for kind in range(4)]


def kernel(x, norm_w, w_in, pool_w, pool_scale, conv_w, a_log, dt_bias, dn_norm_w, w_out, final_norm_w, loss_target, m_norm_w, m_w_in, m_pool_w, m_pool_scale, m_conv_w, m_a_log, m_dt_bias, m_dn_norm_w, m_w_out, m_final_norm_w, v_norm_w, v_w_in, v_pool_w, v_pool_scale, v_conv_w, v_a_log, v_dt_bias, v_dn_norm_w, v_w_out, v_final_norm_w):
    cidx = lax.axis_index("c").astype(I32).reshape(1)
    jidx = (2 * lax.axis_index("x") + lax.axis_index("y")).astype(I32)

    wb = jnp.pad(w_in[0].astype(BF16), ((0, 0), (0, BLK_IN_PAD - BLK_IN)))
    ob = w_out[0].astype(BF16)
    gw, cw_full = _gather_weights(wb, conv_w[0])
    w_pad = _assemble_w_in(gw, wb, jidx.reshape(1))

    lands_o, copies_o = _gather_blocks(ob)
    sems_o, ob_thru, zones_o, token_o = _exchange_start("gather_w_out_start", [ob], lands_o, copies_o)

    def w_out_full(after):
        _, (got,) = _exchange_wait("gather_w_out_wait", sems_o, ob_thru, zones_o, copies_o, after)
        return got.reshape(D_MODEL, D_MODEL)

    loss, n_t, g_wout, small, dh, pieces = _local_step(
        x[0], loss_target[0], w_pad, w_out_full, cw_full, norm_w, pool_w[0], pool_scale, a_log, dt_bias,
        dn_norm_w, final_norm_w.reshape(1, D_MODEL), token_o)
    small["loss"] = loss

    blocks_out = g_wout.reshape(4, BLK_OUT, D_MODEL)
    early = _reduce_sibling("reduce_sibling_out", [blocks_out, _pack_small(small)])
    lands_e, copies_e = _to_other_chips(early, [True, False])
    sems_e, early, zones_e, token_e = _exchange_start("reduce_chips_out_start", early, lands_e, copies_e)
    g_win = _grad_w_in(n_t, pieces, token_e)
    late = _reduce_sibling("reduce_sibling", [g_win])
    lands, copies = _to_other_chips(late, [True])
    sems, late, zones, token = _exchange_start("reduce_chips_start", late, lands, copies)
    gx, g_nw = _in_bwd(x[0], dh, norm_w, w_pad, pieces, token)
    g_nw = _allreduce_tile("reduce_norm_w", g_nw.reshape(8, HEAD)).reshape(1, D_MODEL)
    early, from_chips_e = _exchange_wait("reduce_chips_out_wait", sems_e, early, zones_e, copies_e, gx)
    late, from_chips = _exchange_wait("reduce_chips_wait", sems, late, zones, copies, gx)
    halves, other_halves = _sum_chips_swap("sum_chips_swap", list(late) + list(early), list(from_chips) + list(from_chips_e),
                                           [True, True, False])

    weights = dict(norm_w=norm_w, w_in=w_in, pool_w=pool_w, pool_scale=pool_scale, conv_w=conv_w, a_log=a_log,
                   dt_bias=dt_bias, dn_norm_w=dn_norm_w, w_out=w_out, final_norm_w=final_norm_w)
    ms = dict(norm_w=m_norm_w, w_in=m_w_in, pool_w=m_pool_w, pool_scale=m_pool_scale, conv_w=m_conv_w, a_log=m_a_log,
              dt_bias=m_dt_bias, dn_norm_w=m_dn_norm_w, w_out=m_w_out, final_norm_w=m_final_norm_w)
    vs = dict(norm_w=v_norm_w, w_in=v_w_in, pool_w=v_pool_w, pool_scale=v_pool_scale, conv_w=v_conv_w, a_log=v_a_log,
              dt_bias=v_dt_bias, dn_norm_w=v_dn_norm_w, w_out=v_w_out, final_norm_w=v_final_norm_w)
    names = ["norm_w", "w_in", "pool_w", "pool_scale", "conv_w", "a_log", "dt_bias", "dn_norm_w", "w_out", "final_norm_w"]
    small_names = [n for n in names if n not in ("w_in", "w_out")]

    def pack(t):
        conv = lax.dynamic_update_slice_in_dim(jnp.zeros((CONV_K, 3 * D_HALF), F32), t["conv_w"][0], jidx * BLK_CONV, axis=1)
        return _pack_small({**{n: t[n] for n in small_names if n != "conv_w"}, "conv_w": conv})

    results = [{}, {}, {}, {}]
    to_tiles = lambda a: jnp.transpose(a, (2, 0, 1)).reshape(BLK_IN, 8, HEAD)
    from_tiles = lambda a: jnp.transpose(a, (1, 2, 0)).reshape(1, D_MODEL, BLK_IN)
    lo = jnp.where(cidx[0] == 0, halves[0], other_halves[0])
    hi = jnp.where(cidx[0] == 0, other_halves[0], halves[0])
    g_tiles = jnp.concatenate([lo[:, :BLK_IN].T, hi[:, :BLK_IN].T], axis=1).reshape(BLK_IN, 8, HEAD)
    outs = _adamw_tiles("adamw_w_in", to_tiles(w_in), g_tiles, to_tiles(m_w_in), to_tiles(v_w_in))
    for res, o in zip(results, (g_tiles,) + tuple(outs)):
        res["w_in"] = from_tiles(o)
    outs = _adamw_shard("adamw_w_out", w_out, halves[1], other_halves[1], cidx, m_w_out, v_w_out)
    for res, o in zip(results, outs):
        res["w_out"] = o
    outs = _adamw_small(pack(weights), halves[2], other_halves[2], cidx, pack(ms), pack(vs))
    for res, got in zip(results, outs):
        got["conv_w"] = lax.dynamic_slice_in_dim(got["conv_w"], jidx * BLK_CONV, BLK_CONV, axis=2)
        res.update(got)
    one_tile = lambda a: a.reshape(1, 8, HEAD)
    outs = _adamw_tiles("adamw_norm_w", one_tile(norm_w), one_tile(g_nw), one_tile(m_norm_w), one_tile(v_norm_w))
    for res, o in zip(results, (g_nw,) + tuple(outs)):
        res["norm_w"] = o.reshape(1, D_MODEL)
    grads, delta, new_m, new_v = results

    return (grads["loss"], gx[None], *[grads[n] for n in names], *[delta[n] for n in names],
            *[new_m[n] for n in names], *[new_v[n] for n in names])
```

```python
import functools

import jax
import jax.numpy as jnp
import numpy as np
from jax import lax
from jax.experimental import pallas as pl
from jax.experimental.pallas import tpu as pltpu

F32 = jnp.float32
BF16 = jnp.bfloat16
I32 = jnp.int32

D_MODEL = 1024
D_HALF = 512
N_HEADS = 4
HEAD = 128
CHUNK = 64
PAIR = 2 * CHUNK
WINDOWS = (2, 4, 8, 16)
CONV_K = 4
EPS = 1e-6
N_IN = 3080
N_IN_PAD = 3200
BLK_IN = 770
BLK_IN_PAD = 896
BLK_OUT = 256
BLK_CONV = 384
COL_BA = 3072
QK_SCALE = HEAD ** -0.5
SMALL_ROWS = 608
VMEM_LIMIT = 56 * 1024 * 1024

ADAM_LR = 0.001
ADAM_B1 = 0.9
ADAM_B2 = 0.999
ADAM_EPS = 1e-08
ADAM_WD = 0.01
ADAM_STEP = 10

CHIP_MASKS = (2, 1, 3)
HEADS = range(N_HEADS)
HEAD_COLS = [slice(h * HEAD, (h + 1) * HEAD) for h in HEADS]


def _call(body, **kw):
    return pl.pallas_call(body, **kw)


def _params(*sem):
    return pltpu.CompilerParams(dimension_semantics=sem, vmem_limit_bytes=VMEM_LIMIT)


def _sds(shape, dtype=F32):
    return jax.ShapeDtypeStruct(shape, dtype)


def _bdot(a, b):
    return jnp.dot(a.astype(BF16), b.astype(BF16), preferred_element_type=F32)


def _bdot_nt(a, b):
    return lax.dot_general(a.astype(BF16), b.astype(BF16), (((1,), (1,)), ((), ())), preferred_element_type=F32)


def _bdot_tn(a, b):
    return lax.dot_general(a.astype(BF16), b.astype(BF16), (((0,), (0,)), ((), ())), preferred_element_type=F32)


def _side(a, b):
    return jnp.concatenate([a.astype(BF16), b.astype(BF16)], axis=1)


def _stack(a, b):
    return jnp.concatenate([a.astype(BF16), b.astype(BF16)], axis=0)


def _split(a):
    hi = a.astype(BF16)
    lo = (a - hi.astype(F32)).astype(BF16)
    return hi, lo


def _mask_dot(m, b):
    n = b.shape[1]
    both = jnp.dot(m, jnp.concatenate(_split(b), axis=1), preferred_element_type=F32)
    return both[:, :n] + both[:, n:]


def _sigmoid(x):
    return 0.5 * jnp.tanh(0.5 * x) + 0.5


def _softplus(x):
    return jnp.maximum(x, 0.0) + jnp.log(1.0 + jnp.exp(-jnp.abs(x)))


def _rowsum(x):
    return jnp.sum(x, axis=-1, keepdims=True)


def _colsum(x):
    return jnp.sum(x, axis=0, keepdims=True)


def _shift_down(xv, prev8, k):
    r = pltpu.roll(xv, k, 0)
    q = pltpu.roll(prev8, k, 0)
    row = lax.broadcasted_iota(I32, prev8.shape, 0)
    top = jnp.where(row < k, q, r[0:8])
    return jnp.concatenate([top, r[8:]], axis=0)


def _shift_up(xv, next8, k):
    t = xv.shape[0]
    r = pltpu.roll(xv, t - k, 0)
    q = pltpu.roll(next8, 8 - k, 0)
    row = lax.broadcasted_iota(I32, next8.shape, 0)
    bot = jnp.where(row >= 8 - k, q, r[t - 8:])
    return jnp.concatenate([r[:t - 8], bot], axis=0)


INTRA_PAIRS = 2
UNITS = [(pp, h) for pp in range(INTRA_PAIRS) for h in HEADS]


def _heads_of(ref, rows=PAIR, units=UNITS):
    return [ref[pp * rows:(pp + 1) * rows, HEAD_COLS[h]] for pp, h in units]


def _put_heads_of(ref, vals, rows=PAIR, units=UNITS):
    for (pp, h), v in zip(units, vals):
        ref[pp * rows:(pp + 1) * rows, HEAD_COLS[h]] = v.astype(ref.dtype)


_heads = _heads_of
_put_heads = _put_heads_of


def _each(fn, *lists):
    return [fn(*args) for args in zip(*lists)]


def _pool_bands(t, anti=False):
    r = np.arange(t)[:, None]
    c = np.arange(t + HEAD)[None, :]
    d = (c - r) if anti else (r - c + HEAD)
    return jnp.asarray(np.stack([(d >= 0) & (d < w) for w in WINDOWS]), BF16)


def _pool_mix(u, halo, z, pw, bands, row0):
    t = u[0].shape[0]
    rows = row0 + lax.broadcasted_iota(I32, (t, 1), 0) + 1
    cnt = [jnp.minimum(rows, w).astype(F32) for w in WINDOWS]
    win = _each(lambda b, h, v: _mask_dot(b, jnp.concatenate([h, v], axis=0)), bands, halo, u)
    mix = _each(lambda a, c, v: a / c - v, win, cnt, u)
    mixed = _each(_bdot, mix, pw)
    return mix, mixed, _each(_sigmoid, z), cnt


POOL_T = 256


def _conv_taps(xv, prev8):
    return [_shift_down(xv, prev8, CONV_K - 1 - j) for j in range(CONV_K - 1)] + [xv]


def _conv_pre(taps, cw):
    y = taps[CONV_K - 1] * cw[CONV_K - 1:CONV_K]
    for j in range(CONV_K - 2, -1, -1):
        y = y + taps[j] * cw[j:j + 1]
    return y


CONV_T = 256


def _conv_specs(t, tile_of=lambda i: i):
    tiles = [pl.BlockSpec((t, D_HALF), functools.partial(lambda i, p: (tile_of(i), 2 + p), p=p)) for p in range(3)]
    halos = [pl.BlockSpec((8, D_HALF),
                          functools.partial(lambda i, p: (jnp.maximum(tile_of(i) * (t // 8) - 1, 0), 2 + p), p=p))
             for p in range(3)]
    return tiles + halos


def _pair_masks():
    r = lax.broadcasted_iota(I32, (PAIR, PAIR), 0)
    c = lax.broadcasted_iota(I32, (PAIR, PAIR), 1)
    same = jnp.right_shift(r, 6) == jnp.right_shift(c, 6)
    return same, same & (r >= c), same & (r > c), r == c


def _interleave(*stage_lists):
    live = list(stage_lists)
    while live:
        for gen in list(live):
            try:
                next(gen)
            except StopIteration:
                live.remove(gen)


def _pipelined_step(t, n, leading, trailing):
    @pl.when(t == 0)
    def _():
        _interleave(*leading())

    @pl.when(jnp.logical_and(t > 0, t < n))
    def _():
        _interleave(*leading(), *trailing())

    @pl.when(t == n)
    def _():
        _interleave(*trailing())


def _pair_common_stages(cm, qn, kn, vs, beta, g):
    same, incl, strict, eye = _pair_masks()
    incl_b = incl.astype(BF16)
    first = lax.broadcasted_iota(I32, (PAIR, HEAD), 0) < CHUNK
    cm.update(same=same, incl=incl, strict=strict, eye=eye)
    gc = _each(lambda gv: _mask_dot(incl_b, gv), g)
    q = _each(lambda v: v * QK_SCALE, qn)
    kb = _each(lambda k, b: k * b, kn, beta)
    cm.update(gc=gc, q=q, kb=kb, vb=_each(lambda v, b: v * b, vs, beta))
    yield
    both = _each(lambda a, b, c: _bdot_nt(_stack(a, b), c), kb, q, kn)
    cm.update(kk=[v[:PAIR] for v in both], qk=[v[PAIR:] for v in both])
    gc_row = _each(lambda v: _colsum(jnp.where(eye, v, 0.0)), gc)
    gl = _each(lambda v: jnp.where(first, v[CHUNK - 1:CHUNK], v[PAIR - 1:PAIR]), gc)
    egc = _each(jnp.exp, gc)
    cm.update(gl=gl, egc=egc,
              decay=_each(lambda v, r: jnp.where(incl, jnp.exp(jnp.where(incl, v - r, 0.0)), 0.0), gc, gc_row))
    yield
    cm.update(ekd=_each(lambda a, b: jnp.exp(a - b), gl, gc), cd=_each(jnp.exp, gl),
              kbg=_each(lambda k, e: k * e, kb, egc))
    yield


def _tri_inv_stages(out, a, eye_f):
    p = _each(lambda v: eye_f - v, a)
    x = _each(_bdot, a, a)
    yield
    for it in range(4):
        both = _each(lambda xv, pv: _bdot(xv, _side(pv, xv)), x, p)
        p = _each(lambda pv, b: pv + b[:, :PAIR], p, both)
        x = [b[:, PAIR:] for b in both]
        yield
    out["t"] = _each(lambda pv, xv: pv + _bdot(pv, xv), p, x)
    yield


def _chunk_scalar_spec(pairs=1, index=lambda i: (i, 0)):
    return pl.BlockSpec((16 * pairs, D_HALF), index)


SCAN_PAIRS = 2
SCAN_ROWS = SCAN_PAIRS * PAIR


def _delta_fwd(qn, kn, vs, beta, g):
    s = qn.shape[0]
    n_steps = s // SCAN_ROWS
    n_chunks = s // CHUNK
    assert INTRA_PAIRS == SCAN_PAIRS

    def body(qn_ref, kn_ref, vs_ref, beta_ref, g_ref, w_ref, att_ref, qd_ref, kd_ref, t_ref, cd_ref, o_ref, vn_ref, st_ref,
             state, u_s, w_s, att_s, qd_s, kd_s, cd_s):
        t = pl.program_id(0)

        @pl.when(t == 1)
        def _():
            state[...] = jnp.zeros_like(state)

        cur = lax.rem(t, 2)
        prev = 1 - cur
        cols = list(enumerate(HEAD_COLS))

        def recurrence():
            sm = [state[h] for h in HEADS]
            for ci in range(2 * SCAN_PAIRS):
                rs = slice(ci * CHUNK, (ci + 1) * CHUNK)
                for h in HEADS:
                    st_ref[ci, h] = sm[h]
                both = [_bdot(jnp.concatenate([w_s[prev, rs, sl], qd_s[prev, rs, sl]], axis=0), sm[h]) for h, sl in cols]
                vn = [u_s[prev, rs, sl] - both[h][:CHUNK] for h, sl in cols]
                for h, sl in cols:
                    vn_ref[rs, sl] = vn[h].astype(BF16)
                    o_ref[rs, sl] = both[h][CHUNK:]
                yield
                sm = [sm[h] * cd_s[prev, ci * 8:ci * 8 + 1, sl] + _bdot_tn(kd_s[prev, rs, sl], vn[h]) for h, sl in cols]
                yield
            for h in HEADS:
                state[h] = sm[h]
            for pp in range(SCAN_PAIRS):
                rp = slice(pp * PAIR, (pp + 1) * PAIR)
                intra = [_bdot(att_s[prev, rp, sl], vn_ref[rp, sl]) for sl in HEAD_COLS]
                for h, sl in cols:
                    o_ref[rp, sl] += intra[h]
                yield

        def factors():
            kn = _heads(kn_ref)
            cm = {}
            yield from _pair_common_stages(cm, _heads(qn_ref), kn, _heads(vs_ref), _heads(beta_ref), _heads(g_ref))
            a = _each(lambda kk, d: jnp.where(cm["strict"], kk * d, 0.0), cm["kk"], cm["decay"])
            inv = {}
            yield from _tri_inv_stages(inv, a, cm["eye"].astype(F32))
            tm = inv["t"]
            uw = _each(lambda tv, a, b: _bdot(tv, _side(a, b)), tm, cm["vb"], cm["kbg"])
            res = dict(u=[v[:, :HEAD] for v in uw], w=[v[:, HEAD:] for v in uw],
                       att=_each(lambda a, b: a * b, cm["qk"], cm["decay"]),
                       qd=_each(lambda a, b: a * b, cm["q"], cm["egc"]), kd=_each(lambda a, b: a * b, kn, cm["ekd"]))
            yield
            _put_heads(t_ref, tm)
            for key, out, keep in (("w", w_ref, w_s), ("att", att_ref, att_s), ("qd", qd_ref, qd_s), ("kd", kd_ref, kd_s)):
                _put_heads(out, res[key])
                for (pp, h), v in zip(UNITS, res[key]):
                    keep[cur, pp * PAIR:(pp + 1) * PAIR, HEAD_COLS[h]] = v.astype(BF16)
            for (pp, h), v in zip(UNITS, res["u"]):
                u_s[cur, pp * PAIR:(pp + 1) * PAIR, HEAD_COLS[h]] = v
            for ci in range(2):
                for (pp, h), v in zip(UNITS, cm["cd"]):
                    rows8 = slice(pp * 16 + ci * 8, pp * 16 + (ci + 1) * 8)
                    cd_ref[rows8, HEAD_COLS[h]] = v[ci * CHUNK:ci * CHUNK + 8]
                    cd_s[cur, rows8, HEAD_COLS[h]] = v[ci * CHUNK:ci * CHUNK + 8]
            yield

        _pipelined_step(t, n_steps, lambda: [factors()], lambda: [recurrence()])

    last = n_steps - 1
    now = lambda i: (jnp.minimum(i, last), 0)
    before = lambda i: (jnp.maximum(i - 1, 0), 0)
    rows = lambda index: pl.BlockSpec((SCAN_ROWS, D_HALF), index)
    slot = lambda r, dtype: pltpu.VMEM((2, r, D_HALF), dtype)
    return _call(
        body, name="delta_fwd", grid=(n_steps + 1,),
        in_specs=[rows(now)] * 5,
        out_specs=[rows(now)] * 5 + [_chunk_scalar_spec(SCAN_PAIRS, now), rows(before), rows(before),
                                     pl.BlockSpec((2 * SCAN_PAIRS, N_HEADS, HEAD, HEAD), lambda i: (jnp.maximum(i - 1, 0), 0, 0, 0))],
        out_shape=[_sds((s, D_HALF), BF16)] * 5 + [_sds((s // 8, D_HALF)), _sds((s, D_HALF)), _sds((s, D_HALF), BF16),
                                                  _sds((n_chunks, N_HEADS, HEAD, HEAD))],
        scratch_shapes=[pltpu.VMEM((N_HEADS, HEAD, HEAD), F32), slot(SCAN_ROWS, F32), slot(SCAN_ROWS, BF16),
                        slot(SCAN_ROWS, BF16), slot(SCAN_ROWS, BF16), slot(SCAN_ROWS, BF16), slot(16 * SCAN_PAIRS, F32)],
        compiler_params=_params("arbitrary"),
    )(qn, kn, vs, beta, g)


OUT_T = 512
OUT_ROWS = 256


def _out_fwd_bwd(x, y_pool, o, proj, target, w_out, dn_norm_w, final_norm_w):
    s = x.shape[0]
    t = OUT_T

    def body(x_ref, yp_ref, o_ref, z_ref, tg_ref, wo_ref, dnw_ref, fnw_ref,
             gwo_ref, dh_ref, dyp_ref, do_ref, dz_ref, loss_ref, gfn_ref, gdn_ref, y_ref, yt_ref, gwo_acc):
        @pl.when(pl.program_id(0) == 0)
        def _():
            loss_ref[...] = jnp.zeros_like(loss_ref)
            gfn_ref[...] = jnp.zeros_like(gfn_ref)
            gdn_ref[...] = jnp.zeros_like(gdn_ref)
            gwo_acc[...] = jnp.zeros_like(gwo_acc)

        dnw = dnw_ref[...]
        fnw = fnw_ref[...]

        def stages(rows, lead):
            for _ in range(lead):
                yield
            ypv = yp_ref[rows]
            y_ref[rows, :D_HALF] = ypv.astype(BF16)
            yt_ref[:D_HALF, rows] = ypv.T.astype(BF16)
            keep = []
            for h in HEADS:
                ov = o_ref[rows, HEAD_COLS[h]]
                zv = z_ref[rows, HEAD_COLS[h]]
                ro = lax.rsqrt(jnp.mean(ov * ov, axis=-1, keepdims=True) + EPS)
                ohat = ov * ro
                sg = _sigmoid(zv)
                keep.append((ro, ohat, zv, sg))
                ydn = ohat * dnw * (zv * sg)
                y_ref[rows, D_HALF + h * HEAD:D_HALF + (h + 1) * HEAD] = ydn.astype(BF16)
                yt_ref[D_HALF + h * HEAD:D_HALF + (h + 1) * HEAD, rows] = ydn.T.astype(BF16)
            yield
            hv = x_ref[rows] + jnp.dot(y_ref[rows], wo_ref[...], preferred_element_type=F32)
            yield
            r2 = lax.rsqrt(jnp.mean(hv * hv, axis=-1, keepdims=True) + EPS)
            hhat = hv * r2
            err = hhat * fnw - tg_ref[rows]
            loss_ref[...] += 0.5 * jnp.sum(_rowsum(err * err) * (1.0 / D_MODEL), axis=0, keepdims=True)
            dout = err * (1.0 / D_MODEL)
            gfn_ref[...] += _colsum(dout * hhat)
            dhh = dout * fnw
            dh = r2 * (dhh - hhat * jnp.mean(dhh * hhat, axis=-1, keepdims=True))
            dh_ref[rows] = dh
            yield
            if lead == t // OUT_ROWS - 1:
                gwo_acc[...] += _bdot(yt_ref[...], dh_ref[...])
            dy = _bdot_nt(dh, wo_ref[...])
            yield
            dyp_ref[rows] = dy[:, :D_HALF]
            gdn = jnp.zeros((1, HEAD), F32)
            for h in HEADS:
                ro, ohat, zv, sg = keep[h]
                dyd = dy[:, D_HALF + h * HEAD:D_HALF + (h + 1) * HEAD]
                sz = zv * sg
                dz_ref[rows, HEAD_COLS[h]] = (dyd * ohat * dnw * (sg * (1.0 + zv * (1.0 - sg)))).astype(BF16)
                gdn = gdn + _colsum(dyd * ohat * sz)
                doh = dyd * dnw * sz
                do_ref[rows, HEAD_COLS[h]] = ro * (doh - ohat * jnp.mean(doh * ohat, axis=-1, keepdims=True))
            gdn_ref[...] += gdn
            yield

        _interleave(*[stages(slice(k * OUT_ROWS, (k + 1) * OUT_ROWS), k) for k in range(t // OUT_ROWS)])

        @pl.when(pl.program_id(0) == pl.num_programs(0) - 1)
        def _():
            gwo_ref[...] = gwo_acc[...].astype(BF16)

    wide = pl.BlockSpec((t, D_MODEL), lambda i: (i, 0))
    half = pl.BlockSpec((t, D_HALF), lambda i: (i, 0))
    const = lambda shape: pl.BlockSpec(shape, lambda i: (0,) * len(shape))
    return _call(
        body, name="out_fwd_bwd", grid=(s // t,),
        in_specs=[wide, half, half, pl.BlockSpec((t, D_HALF), lambda i: (i, 5)), wide,
                  const((D_MODEL, D_MODEL)), const((1, HEAD)), const((1, D_MODEL))],
        out_specs=[const((D_MODEL, D_MODEL)), wide, half, half, half,
                   const((1, HEAD)), const((1, D_MODEL)), const((1, HEAD))],
        out_shape=[_sds((D_MODEL, D_MODEL), BF16), _sds((s, D_MODEL)), _sds((s, D_HALF)), _sds((s, D_HALF)),
                   _sds((s, D_HALF), BF16), _sds((1, HEAD)), _sds((1, D_MODEL)), _sds((1, HEAD))],
        scratch_shapes=[pltpu.VMEM((t, D_MODEL), BF16), pltpu.VMEM((D_MODEL, t), BF16), pltpu.VMEM((D_MODEL, D_MODEL), F32)],
        compiler_params=_params("arbitrary"),
    )(x, y_pool, o, proj, target, w_out, dn_norm_w, final_norm_w)


def _grad_w_in(at, pieces, after):
    m, s = at.shape
    n = len(pieces)
    tn, tk = D_HALF, min(s, 1024)

    def body(a_ref, *refs):
        p_refs, o_ref, acc = refs[:n], refs[n + 1], refs[n + 2]

        @pl.when(pl.program_id(0) == 0)
        def _():
            acc[...] = jnp.zeros_like(acc)

        av = a_ref[...]
        for p in range(n):
            acc[:, p * tn:(p + 1) * tn] += _bdot(av, p_refs[p][...])

        @pl.when(pl.program_id(0) == pl.num_programs(0) - 1)
        def _():
            for j in range(4):
                base = j * BLK_IN // HEAD * HEAD
                win = acc[:, base:base + BLK_IN_PAD]
                if j * BLK_IN > base:
                    win = pltpu.roll(win, BLK_IN_PAD - (j * BLK_IN - base), 1)
                o_ref[j] = win.astype(BF16)

    return _call(
        body, name="grad_w_in", grid=(s // tk,),
        in_specs=[pl.BlockSpec((m, tk), lambda k: (0, k))] + [pl.BlockSpec((tk, tn), lambda k: (k, 0))] * n
                 + [pl.BlockSpec(memory_space=pl.ANY)],
        out_specs=pl.BlockSpec((4, m, BLK_IN_PAD), lambda k: (0, 0, 0)),
        out_shape=_sds((4, m, BLK_IN_PAD), BF16),
        scratch_shapes=[pltpu.VMEM((m, n * tn), F32)],
        compiler_params=_params("arbitrary"),
    )(at, *pieces, after)


def _delta_bwd(do, vn, qd, kd, w, att, cd, st, qn, kn, vs, beta, g, tm):
    s = do.shape[0]
    n_steps = s // SCAN_ROWS
    assert INTRA_PAIRS == SCAN_PAIRS

    def body(do_ref, vn_ref, qd_ref, kd_ref, w_ref, att_ref, cd_ref, st_ref, qn_ref, kn_ref, vs_ref, beta_ref, g_ref, t_ref,
             dqn_ref, dkn_ref, dvs_ref, dbeta_ref, dg_ref, dstate, du_s, dw_s, datt_s, dqd_s, dkd_s, dcd_s):
        t = pl.program_id(0)

        @pl.when(t == 0)
        def _():
            dstate[...] = jnp.zeros_like(dstate)

        cur = lax.rem(t, 2)
        prev = 1 - cur
        cols = list(enumerate(HEAD_COLS))
        _, incl, _, _ = _pair_masks()

        def recurrence():
            dv_intra = []
            for pp in range(SCAN_PAIRS):
                rp = slice(pp * PAIR, (pp + 1) * PAIR)
                dv_intra.append([_bdot_tn(att_ref[rp, sl], do_ref[rp, sl]) for _, sl in cols])
                for _, sl in cols:
                    datt_s[cur, rp, sl] = jnp.where(incl, _bdot_nt(do_ref[rp, sl], vn_ref[rp, sl]), 0.0)
                yield
            ds = [dstate[h] for h in HEADS]
            for ci in range(2 * SCAN_PAIRS - 1, -1, -1):
                rs = slice(ci * CHUNK, (ci + 1) * CHUNK)
                in_pair = slice((ci % 2) * CHUNK, (ci % 2 + 1) * CHUNK)
                sm = [st_ref[ci, h] for h in HEADS]
                dvn = [dv_intra[ci // 2][h][in_pair] + _bdot(kd_ref[rs, sl], ds[h]) for h, sl in cols]
                dkd = [_bdot_nt(vn_ref[rs, sl], ds[h]) for h, sl in cols]
                dcd = [jnp.broadcast_to(_rowsum(_colsum(ds[h] * sm[h])), (8, HEAD)) for h in HEADS]
                yield
                both = [_bdot_nt(_stack(do_ref[rs, sl], dvn[h]), sm[h]) for h, sl in cols]
                for h, sl in cols:
                    du_s[cur, rs, sl] = dvn[h].astype(BF16)
                    dqd_s[cur, rs, sl] = both[h][:CHUNK]
                    dw_s[cur, rs, sl] = (-both[h][CHUNK:]).astype(BF16)
                    dkd_s[cur, rs, sl] = dkd[h]
                    dcd_s[cur, ci * 8:(ci + 1) * 8, sl] = dcd[h]
                ds = [ds[h] * cd_ref[ci * 8:ci * 8 + 1, sl]
                      + _bdot_tn(_stack(qd_ref[rs, sl], w_ref[rs, sl]), _stack(do_ref[rs, sl], -dvn[h])) for h, sl in cols]
                yield
            for h in HEADS:
                dstate[h] = ds[h]

        def factors(units):
            _heads = functools.partial(_heads_of, units=units)
            _put_heads = functools.partial(_put_heads_of, units=units)
            ones = jnp.ones((2 * PAIR, HEAD), BF16)
            tn = (((0,), (0,)), ((), ()))
            kept = lambda ref, rows=PAIR: [ref[prev, pp * rows:(pp + 1) * rows, HEAD_COLS[h]] for pp, h in units]
            kn, vs, beta = _heads(kn_ref), _heads(vs_ref), _heads(beta_ref)
            cm = {}
            yield from _pair_common_stages(cm, _heads(qn_ref), kn, vs, beta, _heads(g_ref))
            tmv = _heads(t_ref)
            duv, dwv, dattv, dqdv, dkdv = kept(du_s), kept(dw_s), kept(datt_s), kept(dqd_s), kept(dkd_s)
            duw = _each(_side, duv, dwv)
            both = _each(_bdot_tn, tmv, duw)
            dvb, dkbg = [v[:, :HEAD] for v in both], [v[:, HEAD:] for v in both]
            dt = _each(lambda a, b, c: _bdot_nt(a, _side(b, c)), duw, cm["vb"], cm["kbg"])
            yield
            m1 = _each(_bdot_tn, tmv, dt)
            yield
            da = _each(lambda a, b: -jnp.where(cm["strict"], _bdot_nt(a, b), 0.0), m1, tmv)
            yield
            dkk = _each(lambda a, b: a * b, da, cm["decay"])
            dqk = _each(lambda a, b: a * b, dattv, cm["decay"])
            dd = _each(lambda a, b, c, d: a * b + c * d, dkk, cm["kk"], dqk, cm["qk"])
            dkq = _each(_stack, dkk, dqk)
            both = _each(_bdot, dkq, kn)
            dkb = _each(lambda a, c, d: a[:PAIR] + c * d, both, dkbg, cm["egc"])
            dq = _each(lambda a, c, d: a[PAIR:] + c * d, both, dqdv, cm["egc"])
            yield
            dkn = _each(lambda a, b, c: _bdot_tn(a, _stack(b, c)), dkq, cm["kb"], cm["q"])
            dkn = _each(lambda a, b, c, d, e: a + b * c + d * e, dkn, dkdv, cm["ekd"], dkb, beta)
            t_kd = _each(lambda a, b, c: _rowsum(a * b * c), dkdv, kn, cm["ekd"])
            yield
            split = _each(_split, dd)
            rows_dd = [jnp.dot(_side(hi, lo), ones, preferred_element_type=F32) for hi, lo in split]
            cols_dd = [lax.dot_general(_stack(hi, lo), ones, tn, preferred_element_type=F32) for hi, lo in split]
            yield
            dgc = _each(lambda r, c, a, b, e, f, k, tk: r - c + _rowsum(a * b * e) + _rowsum(f * k) - tk,
                        rows_dd, cols_dd, dqdv, cm["q"], cm["egc"], dkbg, cm["kbg"], t_kd)
            same_b = cm["same"].astype(BF16)
            rowi = lax.broadcasted_iota(I32, (PAIR, HEAD), 0)
            dcd = _each(lambda d: jnp.where(rowi < CHUNK, d[0:1], d[8:9]), kept(dcd_s, rows=16))
            dgl = _each(lambda tk, d, c: _mask_dot(same_b, jnp.broadcast_to(tk, (PAIR, HEAD))) + d * c, t_kd, dcd, cm["cd"])
            yield
            is_last = jnp.bitwise_and(rowi, CHUNK - 1) == CHUNK - 1
            dgc = _each(lambda a, b: a + jnp.where(is_last, b, 0.0), dgc, dgl)
            r = lax.broadcasted_iota(I32, (PAIR, PAIR), 0)
            c = lax.broadcasted_iota(I32, (PAIR, PAIR), 1)
            upper_b = (cm["same"] & (r <= c)).astype(BF16)
            _put_heads(dg_ref, _each(lambda v: _mask_dot(upper_b, v), dgc))
            yield
            _put_heads(dbeta_ref, _each(lambda a, b, c, d: jnp.broadcast_to(_rowsum(a * b) + _rowsum(c * d), (PAIR, HEAD)),
                                        dkb, kn, dvb, vs))
            _put_heads(dqn_ref, _each(lambda v: v * QK_SCALE, dq))
            _put_heads(dkn_ref, dkn)
            _put_heads(dvs_ref, _each(lambda a, b: a * b, dvb, beta))
            yield

        _pipelined_step(t, n_steps, lambda: [recurrence()],
                        lambda: [factors(UNITS[k * 2:(k + 1) * 2]) for k in range(len(UNITS) // 2)])

    last = n_steps - 1
    now = lambda i: (jnp.maximum(last - i, 0), 0)
    after = lambda i: (jnp.minimum(n_steps - i, last), 0)
    rows = lambda index: pl.BlockSpec((SCAN_ROWS, D_HALF), index)
    slot = lambda r, dtype: pltpu.VMEM((2, r, D_HALF), dtype)
    return _call(
        body, name="delta_bwd", grid=(n_steps + 1,),
        in_specs=[rows(now)] * 6 + [_chunk_scalar_spec(SCAN_PAIRS, now),
                                    pl.BlockSpec((2 * SCAN_PAIRS, N_HEADS, HEAD, HEAD), lambda i: (jnp.maximum(last - i, 0), 0, 0, 0))]
                 + [rows(after)] * 6,
        out_specs=[rows(after)] * 5,
        out_shape=[_sds((s, D_HALF))] * 5,
        scratch_shapes=[pltpu.VMEM((N_HEADS, HEAD, HEAD), F32), slot(SCAN_ROWS, BF16), slot(SCAN_ROWS, BF16),
                        slot(SCAN_ROWS, F32), slot(SCAN_ROWS, F32), slot(SCAN_ROWS, F32), slot(16 * SCAN_PAIRS, F32)],
        compiler_params=_params("arbitrary"),
    )(do, vn, qd, kd, w, att, cd, st, qn, kn, vs, beta, g, tm)


def _fused_call(name, n_steps, parts):
    n_in = [len(p["inputs"]) for p in parts]
    n_out = [len(p["out_shape"]) for p in parts]
    n_scr = [len(p["scratch"]) for p in parts]

    def body(*refs):
        ins, outs, scr = refs[:sum(n_in)], refs[sum(n_in):sum(n_in) + sum(n_out)], refs[sum(n_in) + sum(n_out):]
        gens, a, b, c = [], 0, 0, 0
        for p, ni, no, ns in zip(parts, n_in, n_out, n_scr):
            gens.append(p["stages"](ins[a:a + ni], outs[b:b + no], scr[c:c + ns]))
            a, b, c = a + ni, b + no, c + ns
        _interleave(*gens)

    flat = lambda key: [v for p in parts for v in p[key]]
    res = _call(
        body, name=name, grid=(n_steps,),
        in_specs=flat("in_specs"), out_specs=flat("out_specs"), out_shape=flat("out_shape"),
        scratch_shapes=flat("scratch"),
        compiler_params=_params("arbitrary"),
    )(*flat("inputs"))
    out, b = [], 0
    for no in n_out:
        out.append(res[b:b + no])
        b += no
    return out


def _pool_bwd_part(proj, dyp, pool_w, pool_scale, tile_of, n_tiles):
    s = proj.shape[0]
    t = POOL_T
    hb = t // HEAD
    last = s // HEAD - 1

    def stages(ins, outs, scratch):
        u_ref, z_ref, halo_ref, dy_ref, zn_ref, dyn_ref, pw_ref, ps_ref, band_ref, aband_ref = ins
        du_ref, dz_ref, gpw_ref, gps_ref = outs
        tile = tile_of(pl.program_id(0))

        @pl.when(pl.program_id(0) == 0)
        def _():
            gpw_ref[...] = jnp.zeros_like(gpw_ref)
            gps_ref[...] = jnp.zeros_like(gps_ref)

        live = (tile > 0).astype(F32)
        more = (tile < n_tiles - 1).astype(F32)
        groups = lambda ref: [ref[:, sl] for sl in HEAD_COLS]
        z, ps, dy = groups(z_ref), groups(ps_ref), groups(dy_ref)
        pw = [pw_ref[g] for g in HEADS]
        mix, mixed, sg, cnt = _pool_mix(groups(u_ref), [h * live for h in groups(halo_ref)], z, pw,
                                        [band_ref[g] for g in HEADS], tile * t)
        yield
        sz = _each(lambda a, b: a * b, z, sg)
        for sl, d, m, p, s_, zg in zip(HEAD_COLS, dy, mixed, ps, sg, z):
            dz_ref[:, sl] = (d * m * p * (s_ * (1.0 + zg * (1.0 - s_)))).astype(BF16)
        for sl, d, m, a in zip(HEAD_COLS, dy, mixed, sz):
            gps_ref[:, sl] += _colsum(d * m * a)
        dmixed = _each(lambda d, p, a: d * p * a, dy, ps, sz)
        yield
        for g, gp in enumerate(_each(_bdot_tn, mix, dmixed)):
            gpw_ref[g] += gp
        dmix = _each(_bdot_nt, dmixed, pw)
        yield
        dmix_n = _each(lambda d, p, zn, w_: _bdot_nt(d * more * p * (zn * _sigmoid(zn)), w_),
                       groups(dyn_ref), ps, groups(zn_ref), pw)
        yield
        scaled = [jnp.concatenate([a / c, b * (1.0 / w)], axis=0) for a, c, b, w in zip(dmix, cnt, dmix_n, WINDOWS)]
        du = _each(lambda b, s_, d: _mask_dot(b, s_) - d, [aband_ref[g] for g in HEADS], scaled, dmix)
        for sl, v in zip(HEAD_COLS, du):
            du_ref[:, sl] = v.astype(BF16)
        yield

    tile = lambda col: pl.BlockSpec((t, D_HALF), lambda i: (tile_of(i), col))
    below = lambda col: pl.BlockSpec((HEAD, D_HALF), lambda i: (jnp.minimum((tile_of(i) + 1) * hb, last), col))
    const3 = lambda shape: pl.BlockSpec(shape, lambda i: (0, 0, 0))
    return dict(
        inputs=[proj, proj, proj, dyp, proj, dyp, pool_w, pool_scale, _pool_bands(t), _pool_bands(t, anti=True)],
        in_specs=[tile(0), tile(1), pl.BlockSpec((HEAD, D_HALF), lambda i: (jnp.maximum(tile_of(i) * hb - 1, 0), 0)),
                  tile(0), below(1), below(0), const3((N_HEADS, HEAD, HEAD)), pl.BlockSpec((1, D_HALF), lambda i: (0, 0)),
                  const3((N_HEADS, t, HEAD + t)), const3((N_HEADS, t, HEAD + t))],
        out_specs=[tile(0), tile(0), const3((N_HEADS, HEAD, HEAD)), pl.BlockSpec((1, D_HALF), lambda i: (0, 0))],
        out_shape=[_sds((s, D_HALF), BF16), _sds((s, D_HALF), BF16), _sds((N_HEADS, HEAD, HEAD)), _sds((1, D_HALF))],
        scratch=[], stages=stages)


def _conv_bwd_part(proj, pre, conv_w, a_log, dt_bias, dqn, dkn, dvs, dbeta, dg, tile_of, n_tiles):
    s = proj.shape[0]
    t = CONV_T

    def stages(ins, outs, scratch):
        (q_ref, k_ref, v_ref, yq_ref, yk_ref, yv_ref, ba_ref, cw_ref, al_ref, dtb_ref,
         dqn_ref, dkn_ref, dvs_ref, dbeta_ref, dg_ref) = ins
        oq_ref, ok_ref, ov_ref, dba_ref, gcw_out, gsm_out = outs
        below, gcw_ref, gsm_ref = scratch
        step = pl.program_id(0)

        @pl.when(step == 0)
        def _():
            gcw_ref[...] = jnp.zeros_like(gcw_ref)
            gsm_ref[...] = jnp.zeros_like(gsm_ref)
            below[...] = jnp.zeros_like(below)

        parts = ((q_ref, yq_ref, dqn_ref, oq_ref), (k_ref, yk_ref, dkn_ref, ok_ref), (v_ref, yv_ref, dvs_ref, ov_ref))
        for p, (x_ref, y_ref, d_ref, o_ref) in enumerate(parts):
            for h in HEADS:
                cs = HEAD_COLS[h]
                wide = slice(p * D_HALF + h * HEAD, p * D_HALF + (h + 1) * HEAD)
                cw = cw_ref[:, wide]
                y = y_ref[:, cs]
                sg = _sigmoid(y)
                sv = y * sg
                ds = d_ref[:, cs]
                if p < 2:
                    rn = lax.rsqrt(_rowsum(sv * sv) + EPS)
                    nrm = sv * rn
                    ds = rn * (ds - nrm * _rowsum(ds * nrm))
                dy = ds * (sg * (1.0 + y * (1.0 - sg)))
                nxt = below[:, wide]
                ahead = [dy] + [_shift_up(dy, nxt, sft) for sft in range(1, CONV_K)]
                xv = x_ref[:, cs]
                acc = dy * cw[CONV_K - 1:CONV_K]
                for sft in range(1, CONV_K):
                    acc = acc + ahead[sft] * cw[CONV_K - 1 - sft:CONV_K - sft]
                for j in range(CONV_K):
                    gcw_ref[8 * j:8 * j + 8, wide] += _rows8(xv * ahead[CONV_K - 1 - j])
                o_ref[:, cs] = acc.astype(BF16)
                below[:, wide] = dy[0:8]
                yield

        ba = ba_ref[...]
        lane = lax.broadcasted_iota(I32, (t, HEAD), 1)
        lane8 = lax.broadcasted_iota(I32, (8, HEAD), 1)
        dba = jnp.zeros((t, HEAD), F32)
        gsm = jnp.zeros((8, HEAD), F32)
        for h in HEADS:
            beta = _sigmoid(ba[:, h:h + 1])
            dbeta = dbeta_ref[:, h * HEAD:h * HEAD + 1]
            xg = ba[:, N_HEADS + h:N_HEADS + h + 1] + dtb_ref[0:1, h:h + 1]
            nexp = -jnp.exp(al_ref[0:1, h:h + 1])
            dgv = dg_ref[:, h * HEAD:h * HEAD + 1]
            da = dgv * nexp * _sigmoid(xg)
            dba = dba + jnp.where(lane == h, dbeta * beta * (1.0 - beta), 0.0) + jnp.where(lane == N_HEADS + h, da, 0.0)
            gsm = (gsm + jnp.where(lane8 == h, _rows8(dgv * nexp * _softplus(xg)), 0.0)
                   + jnp.where(lane8 == N_HEADS + h, _rows8(da), 0.0))
        dba_ref[...] = jnp.zeros_like(dba_ref)
        dba_ref[:, :HEAD] = dba.astype(BF16)
        gsm_ref[...] += gsm
        yield

        @pl.when(step == n_tiles - 1)
        def _():
            gcw_out[...] = jnp.zeros_like(gcw_out)
            for j in range(CONV_K):
                gcw_out[j:j + 1, :] = _colsum(gcw_ref[8 * j:8 * j + 8, :])
            gsm_out[...] = jnp.broadcast_to(_colsum(gsm_ref[...]), (8, HEAD))

    row = pl.BlockSpec((t, D_HALF), lambda i: (tile_of(i), 0))
    const = lambda shape: pl.BlockSpec(shape, lambda i: (0, 0))
    return dict(
        inputs=[proj] * 3 + list(pre) + [proj, conv_w, a_log, dt_bias, dqn, dkn, dvs, dbeta, dg],
        in_specs=_conv_specs(t, tile_of)[:3] + [row] * 3
                 + [pl.BlockSpec((t, HEAD), lambda i: (tile_of(i), COL_BA // HEAD)),
                    const((CONV_K, 3 * D_HALF)), const((1, N_HEADS)), const((1, N_HEADS))] + [row] * 5,
        out_specs=[row, row, row, row, const((8, 3 * D_HALF)), const((8, HEAD))],
        out_shape=[_sds((s, D_HALF), BF16)] * 4 + [_sds((8, 3 * D_HALF)), _sds((8, HEAD))],
        scratch=[pltpu.VMEM((8, 3 * D_HALF), F32), pltpu.VMEM((8 * CONV_K, 3 * D_HALF), F32), pltpu.VMEM((8, HEAD), F32)],
        stages=stages)


def _pool_fwd_part(proj, pool_w, pool_scale):
    s = proj.shape[0]
    t = POOL_T
    hb = t // HEAD

    def stages(ins, outs, scratch, tile=None):
        u_ref, z_ref, halo_ref, pw_ref, ps_ref, band_ref = ins
        y_ref, = outs
        i = pl.program_id(0) if tile is None else tile
        live = (i > 0).astype(F32)
        groups = lambda ref: [ref[:, sl] for sl in HEAD_COLS]
        z = groups(z_ref)
        u, halo = groups(u_ref), [h * live for h in groups(halo_ref)]
        yield
        _, mixed, sg, _ = _pool_mix(u, halo, z, [pw_ref[g] for g in HEADS], [band_ref[g] for g in HEADS], i * t)
        yield
        for sl, m, zg, s_ in zip(HEAD_COLS, mixed, z, sg):
            y_ref[:, sl] = m * ps_ref[:, sl] * (zg * s_)
        yield

    const3 = lambda shape: pl.BlockSpec(shape, lambda i: (0, 0, 0))
    return dict(
        inputs=[proj, proj, proj, pool_w, pool_scale, _pool_bands(t)],
        in_specs=[pl.BlockSpec((t, D_HALF), lambda i: (i, 0)), pl.BlockSpec((t, D_HALF), lambda i: (i, 1)),
                  pl.BlockSpec((HEAD, D_HALF), lambda i: (jnp.maximum(i * hb - 1, 0), 0)),
                  const3((N_HEADS, HEAD, HEAD)), pl.BlockSpec((1, D_HALF), lambda i: (0, 0)), const3((N_HEADS, t, HEAD + t))],
        out_specs=[pl.BlockSpec((t, D_HALF), lambda i: (i, 0))], out_shape=[_sds((s, D_HALF))],
        scratch=[], stages=stages)


def _conv_fwd_part(proj, conv_w, a_log, dt_bias):
    s = proj.shape[0]
    t = CONV_T

    def stages(ins, outs, scratch, tile=None):
        q_ref, k_ref, v_ref, hq_ref, hk_ref, hv_ref, ba_ref, cw_ref, al_ref, dtb_ref = ins
        qn_ref, kn_ref, vs_ref, beta_ref, g_ref, yq_ref, yk_ref, yv_ref = outs
        live = ((pl.program_id(0) if tile is None else tile) > 0).astype(F32)
        parts = ((q_ref, hq_ref, qn_ref, yq_ref), (k_ref, hk_ref, kn_ref, yk_ref), (v_ref, hv_ref, vs_ref, yv_ref))
        for p, (x_ref, h_ref, o_ref, y_ref) in enumerate(parts):
            for h in HEADS:
                cs = HEAD_COLS[h]
                taps = _conv_taps(x_ref[:, cs], h_ref[:, cs] * live)
                y = _conv_pre(taps, cw_ref[:, p * D_HALF + h * HEAD:p * D_HALF + (h + 1) * HEAD])
                y_ref[:, cs] = y
                sv = y * _sigmoid(y)
                o_ref[:, cs] = sv if p == 2 else sv * lax.rsqrt(_rowsum(sv * sv) + EPS)
                yield
        ba = ba_ref[...]
        for h in HEADS:
            beta = _sigmoid(ba[:, h:h + 1])
            gl = -jnp.exp(al_ref[0:1, h:h + 1]) * _softplus(ba[:, N_HEADS + h:N_HEADS + h + 1] + dtb_ref[0:1, h:h + 1])
            beta_ref[:, HEAD_COLS[h]] = jnp.broadcast_to(beta, (t, HEAD))
            g_ref[:, HEAD_COLS[h]] = jnp.broadcast_to(gl, (t, HEAD))
        yield

    row = pl.BlockSpec((t, D_HALF), lambda i: (i, 0))
    const = lambda shape: pl.BlockSpec(shape, lambda i: (0, 0))
    return dict(
        inputs=[proj] * 7 + [conv_w, a_log, dt_bias],
        in_specs=_conv_specs(t) + [pl.BlockSpec((t, HEAD), lambda i: (i, COL_BA // HEAD)),
                                   const((CONV_K, 3 * D_HALF)), const((1, N_HEADS)), const((1, N_HEADS))],
        out_specs=[row] * 8, out_shape=[_sds((s, D_HALF))] * 8, scratch=[], stages=stages)


def _front_fwd(x, norm_w, w_pad, conv_w, a_log, dt_bias, pool_w, pool_scale, after):
    s = x.shape[0]
    t = CONV_T
    n_tiles = s // t
    assert POOL_T == CONV_T
    like_proj = _sds((s, N_IN_PAD))
    conv = _conv_fwd_part(like_proj, conv_w, a_log, dt_bias)
    pool = _pool_fwd_part(like_proj, pool_w, pool_scale)
    bands = pool["inputs"][-1]
    mxu_n = 256
    col_bounds = list(range(0, N_IN_PAD, 3 * mxu_n)) + [N_IN_PAD]

    def body(x_ref, nw_ref, w_ref, cw_ref, al_ref, dtb_ref, pw_ref, ps_ref, band_ref, after_ref,
             proj_ref, nt_ref, qn_ref, kn_ref, vs_ref, beta_ref, g_ref, yq_ref, yk_ref, yv_ref, y_ref, prev):
        del after_ref
        i = pl.program_id(0)

        @pl.when(i == 0)
        def _():
            prev[...] = jnp.zeros_like(prev)

        tile = jnp.maximum(i - 1, 0)
        main, above8, above = pl.ds(HEAD, t), pl.ds(HEAD - 8, 8), pl.ds(0, HEAD)
        cols = lambda rows, c0, width=D_HALF: prev.at[rows, pl.ds(c0, width)]
        conv_ins = (cols(main, 2 * D_HALF), cols(main, 3 * D_HALF), cols(main, 4 * D_HALF),
                    cols(above8, 2 * D_HALF), cols(above8, 3 * D_HALF), cols(above8, 4 * D_HALF),
                    cols(main, COL_BA, HEAD), cw_ref, al_ref, dtb_ref)
        pool_ins = (cols(main, 0), cols(main, D_HALF), cols(above, 0), pw_ref, ps_ref, band_ref)

        def projection():
            xv = x_ref[...]
            r = lax.rsqrt(jnp.mean(xv * xv, axis=-1, keepdims=True) + EPS)
            nv = xv * r * nw_ref[...]
            nt_ref[...] = nv.T.astype(BF16)
            nb = nv.astype(BF16)
            yield
            for lo, hi in zip(col_bounds[:-1], col_bounds[1:]):
                proj_ref[:, lo:hi] = jnp.dot(nb, w_ref[:, lo:hi], preferred_element_type=F32)
                yield

        _interleave(projection(),
                    conv["stages"](conv_ins, (qn_ref, kn_ref, vs_ref, beta_ref, g_ref, yq_ref, yk_ref, yv_ref), (), tile),
                    pool["stages"](pool_ins, (y_ref,), (), tile))
        prev[0:HEAD] = prev[t:t + HEAD]
        prev[HEAD:HEAD + t] = proj_ref[...]

    last = n_tiles - 1
    now = lambda i: (jnp.minimum(i, last), 0)
    before = lambda i: (jnp.maximum(i - 1, 0), 0)
    const = lambda a: pl.BlockSpec(a.shape, lambda i: (0,) * a.ndim)
    half = pl.BlockSpec((t, D_HALF), before)
    return _call(
        body, name="front_fwd", grid=(n_tiles + 1,),
        in_specs=[pl.BlockSpec((t, D_MODEL), now), const(norm_w), const(w_pad), const(conv_w), const(a_log), const(dt_bias),
                  const(pool_w), const(pool_scale), const(bands), pl.BlockSpec(memory_space=pl.ANY)],
        out_specs=[pl.BlockSpec((t, N_IN_PAD), now), pl.BlockSpec((D_MODEL, t), lambda i: (0, jnp.minimum(i, last)))]
                  + [half] * 9,
        out_shape=[_sds((s, N_IN_PAD)), _sds((D_MODEL, s), BF16)] + [_sds((s, D_HALF))] * 9,
        scratch_shapes=[pltpu.VMEM((HEAD + t, N_IN_PAD), F32)],
        compiler_params=_params("arbitrary"),
    )(x, norm_w, w_pad, conv_w, a_log, dt_bias, pool_w, pool_scale, bands, after)


def _conv_pool_bwd(proj, pre, conv_w, a_log, dt_bias, dqn, dkn, dvs, dbeta, dg, dyp, pool_w, pool_scale):
    n_tiles = proj.shape[0] // CONV_T
    assert POOL_T == CONV_T
    tile_of = lambda i: n_tiles - 1 - i
    return _fused_call("conv_pool_bwd", n_tiles, [
        _conv_bwd_part(proj, pre, conv_w, a_log, dt_bias, dqn, dkn, dvs, dbeta, dg, tile_of, n_tiles),
        _pool_bwd_part(proj, dyp, pool_w, pool_scale, tile_of, n_tiles)])


def _rows8(x):
    acc = x[0:8]
    for r in range(8, x.shape[0], 8):
        acc = acc + x[r:r + 8]
    return acc


IN_T = 512


def _in_bwd(x, dh, norm_w, w_pad, pieces, after):
    s = x.shape[0]
    t = IN_T
    widths = [D_HALF] * 6 + [N_IN_PAD - COL_BA]
    starts = [sum(widths[:k]) for k in range(len(widths))]

    def body(*refs):
        x_ref, dh_ref, nw_ref, w_hbm = refs[:4]
        p_refs = refs[4:4 + len(pieces)]
        gx_ref, gnw_ref, w_ref, arrived = refs[5 + len(pieces):]
        first = pl.program_id(0) == 0
        cols = [pl.ds(c, wd) for c, wd in zip(starts, widths)]
        copies = [pltpu.make_async_copy(w_hbm.at[:, cs], w_ref.at[:, cs], arrived.at[k]) for k, cs in enumerate(cols)]

        @pl.when(first)
        def _():
            for cp in copies:
                cp.start()
            gnw_ref[...] = jnp.zeros_like(gnw_ref)

        def step(wait):
            dn = jnp.zeros((t, D_MODEL), F32)
            for k, (p_ref, c, wd) in enumerate(zip(p_refs, starts, widths)):
                if wait:
                    copies[k].wait()
                dn = dn + _bdot_nt(p_ref[...], w_ref[:, c:c + wd])
            xv = x_ref[...]
            r = lax.rsqrt(jnp.mean(xv * xv, axis=-1, keepdims=True) + EPS)
            xhat = xv * r
            gnw_ref[...] += _colsum(dn * xhat)
            dxh = dn * nw_ref[...]
            gx_ref[...] = dh_ref[...] + r * (dxh - xhat * jnp.mean(dxh * xhat, axis=-1, keepdims=True))

        pl.when(first)(functools.partial(step, True))
        pl.when(jnp.logical_not(first))(functools.partial(step, False))

    wide = pl.BlockSpec((t, D_MODEL), lambda i: (i, 0))
    hbm = pl.BlockSpec(memory_space=pl.ANY)
    return _call(
        body, name="in_bwd", grid=(s // t,),
        in_specs=[wide, wide, pl.BlockSpec((1, D_MODEL), lambda i: (0, 0)), hbm]
                 + [pl.BlockSpec((t, wd), lambda i: (i, 0)) for wd in widths] + [hbm],
        out_specs=[wide, pl.BlockSpec((1, D_MODEL), lambda i: (0, 0))],
        out_shape=[_sds((s, D_MODEL)), _sds((1, D_MODEL))],
        scratch_shapes=[pltpu.VMEM((D_MODEL, N_IN_PAD), BF16), pltpu.SemaphoreType.DMA((len(widths),))],
        compiler_params=_params("arbitrary"),
    )(x, dh, norm_w, w_pad, *pieces, after)


def _adamw_shard(name, w, g_own, g_got, cidx, m, v):
    _, r, c = w.shape
    half = r // 2
    rows = 256 if half % 256 == 0 else half
    per_half = half // rows

    def body(c_ref, w_ref, go_ref, gg_ref, m_ref, v_ref, gout_ref, d_ref, nm_ref, nv_ref):
        mine = (pl.program_id(0) // per_half) == c_ref[0]
        gv = jnp.where(mine, go_ref[:, :c], gg_ref[:, :c])
        gout_ref[0] = gv
        mn = ADAM_B1 * m_ref[0] + (1.0 - ADAM_B1) * gv
        vn = ADAM_B2 * v_ref[0] + (1.0 - ADAM_B2) * (gv * gv)
        m_hat = mn / (1.0 - ADAM_B1 ** ADAM_STEP)
        v_hat = vn / (1.0 - ADAM_B2 ** ADAM_STEP)
        d_ref[0] = -ADAM_LR * (m_hat / (jnp.sqrt(v_hat) + ADAM_EPS) + ADAM_WD * w_ref[0])
        nm_ref[0] = mn
        nv_ref[0] = vn

    blk = pl.BlockSpec((1, rows, c), lambda i, c_ref: (0, i, 0))
    gblk = pl.BlockSpec((rows, g_own.shape[1]), lambda i, c_ref: (i % per_half, 0))
    return _call(
        body, name=name,
        grid_spec=pltpu.PrefetchScalarGridSpec(
            num_scalar_prefetch=1, grid=(2 * per_half,),
            in_specs=[blk, gblk, gblk, blk, blk], out_specs=[blk] * 4),
        out_shape=[_sds((1, r, c))] * 4,
        compiler_params=_params("arbitrary"),
    )(cidx, w, g_own, g_got, m, v)


def _adamw_tiles(name, w, g, m, v):
    n = w.shape[0]
    nb = 77 if n % 77 == 0 else n

    def body(w_ref, g_ref, m_ref, v_ref, d_ref, nm_ref, nv_ref):
        gv = g_ref[...]
        mn = ADAM_B1 * m_ref[...] + (1.0 - ADAM_B1) * gv
        vn = ADAM_B2 * v_ref[...] + (1.0 - ADAM_B2) * (gv * gv)
        m_hat = mn / (1.0 - ADAM_B1 ** ADAM_STEP)
        v_hat = vn / (1.0 - ADAM_B2 ** ADAM_STEP)
        d_ref[...] = -ADAM_LR * (m_hat / (jnp.sqrt(v_hat) + ADAM_EPS) + ADAM_WD * w_ref[...])
        nm_ref[...] = mn
        nv_ref[...] = vn

    blk = pl.BlockSpec((nb, 8, HEAD), lambda i: (i, 0, 0))
    return _call(
        body, name=name, grid=(n // nb,),
        in_specs=[blk] * 4, out_specs=[blk] * 3, out_shape=[_sds(w.shape)] * 3,
        compiler_params=_params("arbitrary"),
    )(w, g, m, v)


def _make_copy(src, dst, send, recv, target):
    if target is None:
        return pltpu.make_async_copy(src, dst, recv)
    return pltpu.make_async_remote_copy(src_ref=src, dst_ref=dst, send_sem=send, recv_sem=recv,
                                        device_id=target, device_id_type=pl.DeviceIdType.MESH)


def _exchange(name, inputs, out_shapes, phases):
    n_in = len(inputs)
    n_out = len(out_shapes)
    n_cp = sum(len(p) for p in phases)

    def body(*refs):
        ins, outs = refs[:n_in], refs[n_in:n_in + n_out]
        send, recv = refs[n_in + n_out:]
        pos = (lax.axis_index("x"), lax.axis_index("y"), lax.axis_index("c"))
        k = 0
        for phase in phases:
            cps = []
            for src, dst, target in phase:
                cps.append(_make_copy(src(ins, outs, pos), dst(ins, outs, pos), send.at[k], recv.at[k],
                                      target and target(pos)))
                k += 1
            for cp in cps:
                cp.start()
            for cp in cps:
                cp.wait()

    anyspec = pl.BlockSpec(memory_space=pl.ANY)
    return _call(
        body, name=name,
        in_specs=[anyspec] * n_in, out_specs=[anyspec] * n_out, out_shape=list(out_shapes),
        scratch_shapes=[pltpu.SemaphoreType.DMA((n_cp,)), pltpu.SemaphoreType.DMA((n_cp,))],
    )(*inputs)


def _exchange_start(name, inputs, out_shapes, copies):
    n_in, n_out, n_cp = len(inputs), len(out_shapes), len(copies)

    def body(*refs):
        ins, lands = refs[:n_in], refs[n_in:n_in + n_out]
        sems = refs[n_in + n_out:n_in + n_out + 2 * n_cp]
        token = refs[-1]
        pos = (lax.axis_index("x"), lax.axis_index("y"), lax.axis_index("c"))
        for k, (src, dst, target) in enumerate(copies):
            _make_copy(src(ins, lands, pos), dst(ins, lands, pos), sems[2 * k], sems[2 * k + 1],
                       target and target(pos)).start()
        token[...] = jnp.zeros_like(token)

    hbm = pl.BlockSpec(memory_space=pltpu.HBM)
    sem = pl.BlockSpec(memory_space=pltpu.SEMAPHORE)
    bufs = list(inputs) + [lax.empty(o.shape, o.dtype) for o in out_shapes]
    outs = _call(
        body, name=name,
        out_shape=tuple([pltpu.SemaphoreType.DMA(())] * (2 * n_cp) + [pltpu.HBM(b.shape, b.dtype) for b in bufs]
                        + [_sds((8, HEAD))]),
        in_specs=[hbm] * len(bufs),
        out_specs=tuple([sem] * (2 * n_cp) + [hbm] * len(bufs) + [pl.BlockSpec(memory_space=pltpu.VMEM)]),
        input_output_aliases={i: 2 * n_cp + i for i in range(len(bufs))},
        compiler_params=pltpu.CompilerParams(has_side_effects=pltpu.SideEffectType.DATAFLOW_SIDE_EFFECTING),
    )(*[pltpu.with_memory_space_constraint(b, pltpu.HBM) for b in bufs])
    return outs[:2 * n_cp], outs[2 * n_cp:2 * n_cp + n_in], outs[2 * n_cp + n_in:-1], outs[-1]


def _exchange_wait(name, sems, sources, lands, copies, after):
    n_in, n_out, n_cp = len(sources), len(lands), len(copies)

    def body(*refs):
        ins, zones = refs[:n_in], refs[n_in:n_in + n_out]
        sem_refs = refs[n_in + n_out:n_in + n_out + 2 * n_cp]
        pos = (lax.axis_index("x"), lax.axis_index("y"), lax.axis_index("c"))
        for k, (src, dst, target) in enumerate(copies):
            cp = _make_copy(src(ins, zones, pos), dst(ins, zones, pos), sem_refs[2 * k], sem_refs[2 * k + 1],
                            target and target(pos))
            if target is None:
                cp.wait()
            else:
                cp.wait_send()
                cp.wait_recv()

    hbm = pl.BlockSpec(memory_space=pltpu.HBM)
    sem = pl.BlockSpec(memory_space=pltpu.SEMAPHORE)
    bufs = list(sources) + list(lands)
    outs = _call(
        body, name=name,
        out_shape=tuple(pltpu.HBM(b.shape, b.dtype) for b in bufs),
        in_specs=[hbm] * len(bufs) + [sem] * (2 * n_cp) + [pl.BlockSpec(memory_space=pl.ANY)],
        out_specs=tuple([hbm] * len(bufs)),
        input_output_aliases={i: i for i in range(len(bufs))},
        compiler_params=pltpu.CompilerParams(has_side_effects=pltpu.SideEffectType.DATAFLOW_SIDE_EFFECTING),
    )(*bufs, *sems, after)
    return outs[:n_in], outs[n_in:]


def _allreduce_tile(name, v):
    def body(v_ref, out_ref, slots, send, recv):
        x, y, c = lax.axis_index("x"), lax.axis_index("y"), lax.axis_index("c")
        me = 4 * x + 2 * y + c
        slots[me] = v_ref[...]
        cps = []
        for k in range(1, 8):
            peer = (x ^ (k >> 2), y ^ ((k >> 1) & 1), c ^ (k & 1))
            cps.append(pltpu.make_async_remote_copy(
                src_ref=v_ref, dst_ref=slots.at[me], send_sem=send.at[k - 1], recv_sem=recv.at[k - 1],
                device_id=peer, device_id_type=pl.DeviceIdType.MESH))
        for cp in cps:
            cp.start()
        for cp in cps:
            cp.wait()
        acc = slots[0]
        for i in range(1, 8):
            acc = acc + slots[i]
        out_ref[...] = acc

    vm = pl.BlockSpec(memory_space=pltpu.VMEM)
    return _call(
        body, name=name, in_specs=[vm], out_specs=vm, out_shape=_sds(v.shape),
        scratch_shapes=[pltpu.VMEM((8,) + v.shape, F32), pltpu.SemaphoreType.DMA((7,)), pltpu.SemaphoreType.DMA((7,))],
    )(v)


def _chip(pos):
    return 2 * pos[0] + pos[1]


def _other_chip(pos, mask):
    x, y, c = pos
    return (x ^ (mask >> 1), y ^ (mask & 1), c)


def _sibling(pos):
    return (pos[0], pos[1], 1 - pos[2])


def _gather_weights(wb, cb):
    rows = wb.shape[0] // 2
    x_nb, y_nb, diag = CHIP_MASKS

    def part(pos, mask, quarter=None):
        start = pos[2] * rows if quarter is None else pos[2] * rows + quarter * (rows // 2)
        return lambda outs: outs[0].at[_chip(pos) ^ mask, pl.ds(start, rows if quarter is None else rows // 2)]

    def passed_on(mask, to, quarter=None):
        return (lambda ins, outs, pos: part(pos, mask, quarter)(outs), lambda ins, outs, pos: part(pos, mask, quarter)(outs), to)

    first = [(lambda ins, outs, pos: ins[0].at[pl.ds(pos[2] * rows, rows)], lambda ins, outs, pos: part(pos, 0)(outs),
              functools.partial(_other_chip, mask=mask)) for mask in (x_nb, y_nb)]
    conv_cols = lambda ins, outs, pos: outs[1].at[:, pl.ds(pl.multiple_of(_chip(pos) * cb.shape[1], HEAD), cb.shape[1])]
    first += [(lambda ins, outs, pos: ins[1], conv_cols, functools.partial(_other_chip, mask=mask)) for mask in CHIP_MASKS]
    first += [(lambda ins, outs, pos: ins[1], conv_cols, None)]
    second = [passed_on(x_nb, functools.partial(_other_chip, mask=y_nb), quarter=0),
              passed_on(y_nb, functools.partial(_other_chip, mask=x_nb), quarter=1),
              passed_on(x_nb, _sibling), passed_on(y_nb, _sibling)]
    third = [passed_on(diag, _sibling)]
    return _exchange("gather_weights", [wb, cb],
                     [_sds((4,) + wb.shape, wb.dtype), _sds((cb.shape[0], 4 * cb.shape[1]), cb.dtype)], [first, second, third])


def _assemble_w_in(gw, wb, jidx):
    m = gw.shape[1]

    def body(j_ref, g_ref, wb_ref, o_ref):
        step = pl.program_id(0)

        @pl.when(step == 0)
        def _():
            o_ref[...] = jnp.zeros_like(o_ref)

        blk = jnp.where(step == j_ref[0], wb_ref[...], g_ref[0]).astype(F32)
        lane = lax.broadcasted_iota(I32, (m, BLK_IN_PAD), 1)
        for j in range(4):
            @pl.when(step == j)
            def _(j=j):
                base = j * BLK_IN // HEAD * HEAD
                shift = j * BLK_IN - base
                moved = pltpu.roll(blk, shift, 1) if shift else blk
                window = o_ref[:, base:base + BLK_IN_PAD].astype(F32)
                mine = (lane >= shift) & (lane < shift + BLK_IN)
                o_ref[:, base:base + BLK_IN_PAD] = jnp.where(mine, moved, window).astype(BF16)

    return _call(
        body, name="assemble_w_in",
        grid_spec=pltpu.PrefetchScalarGridSpec(
            num_scalar_prefetch=1, grid=(4,),
            in_specs=[pl.BlockSpec((1, m, BLK_IN_PAD), lambda j, j_ref: (j, 0, 0)),
                      pl.BlockSpec((m, BLK_IN_PAD), lambda j, j_ref: (0, 0))],
            out_specs=pl.BlockSpec((m, N_IN_PAD), lambda j, j_ref: (0, 0))),
        out_shape=_sds((m, N_IN_PAD), BF16),
        compiler_params=_params("arbitrary"),
    )(jidx, gw, wb)


def _gather_blocks(ob):
    copies = [(lambda ins, outs, pos: ins[0], lambda ins, outs, pos: outs[0].at[_chip(pos)],
               functools.partial(_other_chip, mask=mask)) for mask in CHIP_MASKS]
    copies.append((lambda ins, outs, pos: ins[0], lambda ins, outs, pos: outs[0].at[_chip(pos)], None))
    return [_sds((4,) + ob.shape, ob.dtype)], copies


def _reduce_sibling(name, arrays):
    n = len(arrays)
    halves = [a.shape[:-2] + (a.shape[-2] // 2, a.shape[-1]) for a in arrays]
    pieces = [(a, j) for a in range(n) for j in (range(arrays[a].shape[0]) if arrays[a].ndim == 3 else [None])]

    def body(*refs):
        whole, outs = refs[:n], refs[n:2 * n]
        own, land, summed = refs[2 * n:3 * n], refs[3 * n:4 * n], refs[4 * n:5 * n]
        send, recv, loaded, stored = refs[5 * n:]
        pos = (lax.axis_index("x"), lax.axis_index("y"), lax.axis_index("c"))
        block = lambda ref, j, rows=None: ref.at[(() if j is None else (j,)) + (() if rows is None else (rows,))]
        arrive, load = [], []
        for k, (a, j) in enumerate(pieces):
            h = halves[a][-2]
            arrive.append(pltpu.make_async_remote_copy(
                src_ref=block(whole[a], j, pl.ds((1 - pos[2]) * h, h)), dst_ref=block(land[a], j),
                send_sem=send.at[k], recv_sem=recv.at[k],
                device_id=_sibling(pos), device_id_type=pl.DeviceIdType.MESH))
            load.append(pltpu.make_async_copy(block(whole[a], j, pl.ds(pos[2] * h, h)), block(own[a], j), loaded.at[k]))
        for cp in arrive + load:
            cp.start()
        store = []
        for k, (a, j) in enumerate(pieces):
            load[k].wait()
            arrive[k].wait()
            at = Ellipsis if j is None else j
            summed[a][at] = (own[a][at].astype(F32) + land[a][at].astype(F32)).astype(summed[a].dtype)
            store.append(pltpu.make_async_copy(block(summed[a], j), block(outs[a], j), stored.at[k]))
            store[-1].start()
        for cp in store:
            cp.wait()

    vmem = [pltpu.VMEM(h, a.dtype) for h, a in zip(halves, arrays)]
    sems = [pltpu.SemaphoreType.DMA((len(pieces),))] * 4
    return _call(
        body, name=name,
        in_specs=[pl.BlockSpec(memory_space=pl.ANY)] * n, out_specs=[pl.BlockSpec(memory_space=pl.ANY)] * n,
        out_shape=[_sds(h, a.dtype) for h, a in zip(halves, arrays)],
        scratch_shapes=vmem * 3 + sems,
        compiler_params=_params(),
    )(*arrays)


def _to_other_chips(arrays, blocked):
    def src(ins, outs, pos, a, mask):
        return ins[a].at[_chip(pos) ^ mask] if blocked[a] else ins[a]

    outs = [_sds((3,) + (a.shape[1:] if b else a.shape), a.dtype) for a, b in zip(arrays, blocked)]
    copies = []
    for mi, mask in enumerate(CHIP_MASKS):
        for a in range(len(arrays)):
            copies.append((functools.partial(src, a=a, mask=mask), lambda ins, outs, pos, a=a, mi=mi: outs[a].at[mi],
                           functools.partial(_other_chip, mask=mask)))
    return outs, copies


def _sum_chips_swap(name, owns, gots, blocked):
    n = len(owns)
    shapes = [g.shape[-2:] for g in gots]

    def body(*refs):
        own, got, mine, theirs = refs[:n], refs[n:2 * n], refs[2 * n:3 * n], refs[3 * n:4 * n]
        own_v, got_v, sum_v = refs[4 * n:5 * n], refs[5 * n:6 * n], refs[6 * n:7 * n]
        send, recv, loaded, stored = refs[7 * n:]
        pos = (lax.axis_index("x"), lax.axis_index("y"), lax.axis_index("c"))
        load = []
        for a in range(n):
            load.append((pltpu.make_async_copy(own[a].at[_chip(pos)] if blocked[a] else own[a], own_v[a], loaded.at[2 * a]),
                         pltpu.make_async_copy(got[a], got_v[a], loaded.at[2 * a + 1])))
        for pair in load:
            for cp in pair:
                cp.start()
        out = []
        for a in range(n):
            for cp in load[a]:
                cp.wait()
            sum_v[a][...] = ((own_v[a][...].astype(F32) + got_v[a][0].astype(F32))
                             + (got_v[a][1].astype(F32) + got_v[a][2].astype(F32)))
            out.append(pltpu.make_async_remote_copy(
                src_ref=sum_v[a], dst_ref=theirs[a], send_sem=send.at[a], recv_sem=recv.at[a],
                device_id=_sibling(pos), device_id_type=pl.DeviceIdType.MESH))
            out.append(pltpu.make_async_copy(sum_v[a], mine[a], stored.at[a]))
            out[-2].start()
            out[-1].start()
        for cp in out:
            cp.wait()

    outs = _call(
        body, name=name,
        in_specs=[pl.BlockSpec(memory_space=pl.ANY)] * (2 * n), out_specs=[pl.BlockSpec(memory_space=pl.ANY)] * (2 * n),
        out_shape=[_sds(s) for s in shapes] * 2,
        scratch_shapes=[pltpu.VMEM(s, o.dtype) for s, o in zip(shapes, owns)]
                       + [pltpu.VMEM((3,) + s, g.dtype) for s, g in zip(shapes, gots)] + [pltpu.VMEM(s, F32) for s in shapes]
                       + [pltpu.SemaphoreType.DMA((n,)), pltpu.SemaphoreType.DMA((n,)),
                          pltpu.SemaphoreType.DMA((2 * n,)), pltpu.SemaphoreType.DMA((n,))],
        compiler_params=_params(),
    )(*owns, *gots)
    return outs[:n], outs[n:]


def _local_step(x, target, w_pad, w_out, conv_w, norm_w, pool_w, pool_scale, a_log, dt_bias, dn_norm_w, final_norm_w,
                after):
    proj, n_t, qn, kn, vs, beta, g, yq, yk, yv, y_pool = _front_fwd(
        x, norm_w, w_pad, conv_w, a_log, dt_bias, pool_w, pool_scale, after)
    w, att, qd, kd, tm, cd, o, vn, st = _delta_fwd(qn, kn, vs, beta, g)
    w_out = w_out(o) if callable(w_out) else w_out
    g_wout, dh, dyp, do, ddz, loss, g_fnw, g_dnw = _out_fwd_bwd(x, y_pool, o, proj, target, w_out, dn_norm_w, final_norm_w)
    dqn, dkn, dvs, dbeta, dg = _delta_bwd(do, vn, qd, kd, w, att, cd, st, qn, kn, vs, beta, g, tm)
    (dcq, dck, dcv, dba, g_cw, g_sm), (dpu, dpz, g_pw, g_ps) = _conv_pool_bwd(
        proj, (yq, yk, yv), conv_w, a_log, dt_bias, dqn, dkn, dvs, dbeta, dg, dyp, pool_w, pool_scale)
    pieces = [dpu, dpz, dcq, dck, dcv, ddz, dba]
    small = dict(norm_w=jnp.zeros_like(norm_w), pool_w=g_pw, pool_scale=g_ps, conv_w=g_cw[:CONV_K],
                 a_log=g_sm[0:1, 0:N_HEADS], dt_bias=g_sm[0:1, N_HEADS:2 * N_HEADS], dn_norm_w=g_dnw, final_norm_w=g_fnw)
    return loss[0, 0], n_t, g_wout, small, dh, pieces


SMALL_LAYOUT = (("pool_w", 512, HEAD, (1, N_HEADS, HEAD, HEAD)), ("final_norm_w", 8, HEAD, (D_MODEL,)),
                ("pool_scale", 4, HEAD, (1, D_HALF)), ("conv_w", 48, HEAD, (1, CONV_K, 3 * D_HALF)),
                ("dn_norm_w", 1, HEAD, (1, HEAD)), ("a_log", 1, N_HEADS, (1, N_HEADS)), ("dt_bias", 1, N_HEADS, (1, N_HEADS)),
                ("loss", 1, 1, ()))


def _small_offsets():
    offs, r = {}, 0
    for name, rows, _, _ in SMALL_LAYOUT:
        offs[name] = r
        r += -(-rows // 8) * 8
    assert r <= SMALL_ROWS
    return offs


def _pack_small(t):
    parts = []
    for name, rows, lanes, _ in SMALL_LAYOUT:
        a = t.get(name, jnp.zeros((1,), F32)).reshape(rows, lanes)
        parts.append(jnp.pad(a, ((0, -(-rows // 8) * 8 - rows), (0, HEAD - lanes))))
    buf = jnp.concatenate(parts, axis=0)
    return jnp.pad(buf, ((0, SMALL_ROWS - buf.shape[0]), (0, 0)))


def _adamw_small(w, g_own, g_got, cidx, m, v):
    offs = _small_offsets()
    names = [e[0] for e in SMALL_LAYOUT]
    n = len(names)

    def body(c_ref, w_ref, go_ref, gg_ref, m_ref, v_ref, *outs):
        own_low = c_ref[0] == 0
        gv = jnp.concatenate([jnp.where(own_low, go_ref[...], gg_ref[...]), jnp.where(own_low, gg_ref[...], go_ref[...])], axis=0)
        mn = ADAM_B1 * m_ref[...] + (1.0 - ADAM_B1) * gv
        vn = ADAM_B2 * v_ref[...] + (1.0 - ADAM_B2) * (gv * gv)
        m_hat = mn / (1.0 - ADAM_B1 ** ADAM_STEP)
        v_hat = vn / (1.0 - ADAM_B2 ** ADAM_STEP)
        dl = -ADAM_LR * (m_hat / (jnp.sqrt(v_hat) + ADAM_EPS) + ADAM_WD * w_ref[...])
        for kind, arr in enumerate((gv, dl, mn, vn)):
            for i, (name, rows, lanes, _) in enumerate(SMALL_LAYOUT):
                outs[kind * n + i][...] = arr[offs[name]:offs[name] + rows, :lanes]

    whole = lambda shape: pl.BlockSpec(shape, lambda i, c_ref: (0,) * len(shape))
    out_shapes = [_sds((rows, lanes)) for _, rows, lanes, _ in SMALL_LAYOUT] * 4
    res = _call(
        body, name="adamw_small",
        grid_spec=pltpu.PrefetchScalarGridSpec(
            num_scalar_prefetch=1, grid=(1,),
            in_specs=[whole(w.shape), whole(g_own.shape), whole(g_got.shape), whole(m.shape), whole(v.shape)],
            out_specs=[whole(o.shape) for o in out_shapes]),
        out_shape=out_shapes,
        compiler_params=_params("arbitrary"),
    )(cidx, w, g_own, g_got, m, v)
    return [{name: res[kind * n + i].reshape(shape) for i, (name, _, _, shape) in enumerate(SMALL_LAYOUT)}
            for kind in range(4)]


def kernel(x, norm_w, w_in, pool_w, pool_scale, conv_w, a_log, dt_bias, dn_norm_w, w_out, final_norm_w, loss_target, m_norm_w, m_w_in, m_pool_w, m_pool_scale, m_conv_w, m_a_log, m_dt_bias, m_dn_norm_w, m_w_out, m_final_norm_w, v_norm_w, v_w_in, v_pool_w, v_pool_scale, v_conv_w, v_a_log, v_dt_bias, v_dn_norm_w, v_w_out, v_final_norm_w):
    cidx = lax.axis_index("c").astype(I32).reshape(1)
    jidx = (2 * lax.axis_index("x") + lax.axis_index("y")).astype(I32)

    wb = jnp.pad(w_in[0].astype(BF16), ((0, 0), (0, BLK_IN_PAD - BLK_IN)))
    ob = w_out[0].astype(BF16)
    gw, cw_full = _gather_weights(wb, conv_w[0])
    w_pad = _assemble_w_in(gw, wb, jidx.reshape(1))

    lands_o, copies_o = _gather_blocks(ob)
    sems_o, ob_thru, zones_o, token_o = _exchange_start("gather_w_out_start", [ob], lands_o, copies_o)

    def w_out_full(after):
        _, (got,) = _exchange_wait("gather_w_out_wait", sems_o, ob_thru, zones_o, copies_o, after)
        return got.reshape(D_MODEL, D_MODEL)

    loss, n_t, g_wout, small, dh, pieces = _local_step(
        x[0], loss_target[0], w_pad, w_out_full, cw_full, norm_w, pool_w[0], pool_scale, a_log, dt_bias,
        dn_norm_w, final_norm_w.reshape(1, D_MODEL), token_o)
    small["loss"] = loss

    blocks_out = g_wout.reshape(4, BLK_OUT, D_MODEL)
    early = _reduce_sibling("reduce_sibling_out", [blocks_out, _pack_small(small)])
    lands_e, copies_e = _to_other_chips(early, [True, False])
    sems_e, early, zones_e, token_e = _exchange_start("reduce_chips_out_start", early, lands_e, copies_e)
    g_win = _grad_w_in(n_t, pieces, token_e)
    late = _reduce_sibling("reduce_sibling", [g_win])
    lands, copies = _to_other_chips(late, [True])
    sems, late, zones, token = _exchange_start("reduce_chips_start", late, lands, copies)
    gx, g_nw = _in_bwd(x[0], dh, norm_w, w_pad, pieces, token)
    g_nw = _allreduce_tile("reduce_norm_w", g_nw.reshape(8, HEAD)).reshape(1, D_MODEL)
    early, from_chips_e = _exchange_wait("reduce_chips_out_wait", sems_e, early, zones_e, copies_e, gx)
    late, from_chips = _exchange_wait("reduce_chips_wait", sems, late, zones, copies, gx)
    halves, other_halves = _sum_chips_swap("sum_chips_swap", list(late) + list(early), list(from_chips) + list(from_chips_e),
                                           [True, True, False])

    weights = dict(norm_w=norm_w, w_in=w_in, pool_w=pool_w, pool_scale=pool_scale, conv_w=conv_w, a_log=a_log,
                   dt_bias=dt_bias, dn_norm_w=dn_norm_w, w_out=w_out, final_norm_w=final_norm_w)
    ms = dict(norm_w=m_norm_w, w_in=m_w_in, pool_w=m_pool_w, pool_scale=m_pool_scale, conv_w=m_conv_w, a_log=m_a_log,
              dt_bias=m_dt_bias, dn_norm_w=m_dn_norm_w, w_out=m_w_out, final_norm_w=m_final_norm_w)
    vs = dict(norm_w=v_norm_w, w_in=v_w_in, pool_w=v_pool_w, pool_scale=v_pool_scale, conv_w=v_conv_w, a_log=v_a_log,
              dt_bias=v_dt_bias, dn_norm_w=v_dn_norm_w, w_out=v_w_out, final_norm_w=v_final_norm_w)
    names = ["norm_w", "w_in", "pool_w", "pool_scale", "conv_w", "a_log", "dt_bias", "dn_norm_w", "w_out", "final_norm_w"]
    small_names = [n for n in names if n not in ("w_in", "w_out")]

    def pack(t):
        conv = lax.dynamic_update_slice_in_dim(jnp.zeros((CONV_K, 3 * D_HALF), F32), t["conv_w"][0], jidx * BLK_CONV, axis=1)
        return _pack_small({**{n: t[n] for n in small_names if n != "conv_w"}, "conv_w": conv})

    results = [{}, {}, {}, {}]
    to_tiles = lambda a: jnp.transpose(a, (2, 0, 1)).reshape(BLK_IN, 8, HEAD)
    from_tiles = lambda a: jnp.transpose(a, (1, 2, 0)).reshape(1, D_MODEL, BLK_IN)
    lo = jnp.where(cidx[0] == 0, halves[0], other_halves[0])
    hi = jnp.where(cidx[0] == 0, other_halves[0], halves[0])
    g_tiles = jnp.concatenate([lo[:, :BLK_IN].T, hi[:, :BLK_IN].T], axis=1).reshape(BLK_IN, 8, HEAD)
    outs = _adamw_tiles("adamw_w_in", to_tiles(w_in), g_tiles, to_tiles(m_w_in), to_tiles(v_w_in))
    for res, o in zip(results, (g_tiles,) + tuple(outs)):
        res["w_in"] = from_tiles(o)
    outs = _adamw_shard("adamw_w_out", w_out, halves[1], other_halves[1], cidx, m_w_out, v_w_out)
    for res, o in zip(results, outs):
        res["w_out"] = o
    outs = _adamw_small(pack(weights), halves[2], other_halves[2], cidx, pack(ms), pack(vs))
    for res, got in zip(results, outs):
        got["conv_w"] = lax.dynamic_slice_in_dim(got["conv_w"], jidx * BLK_CONV, BLK_CONV, axis=2)
        res.update(got)
    one_tile = lambda a: a.reshape(1, 8, HEAD)
    outs = _adamw_tiles("adamw_norm_w", one_tile(norm_w), one_tile(g_nw), one_tile(m_norm_w), one_tile(v_norm_w))
    for res, o in zip(results, (g_nw,) + tuple(outs)):
        res["norm_w"] = o.reshape(1, D_MODEL)
    grads, delta, new_m, new_v = results

    return (grads["loss"], gx[None], *[grads[n] for n in names], *[delta[n] for n in names],
            *[new_m[n] for n in names], *[new_v[n] for n in names])
```
